```python
import jax, jax.numpy as jnp
from jax import lax
import numpy as np


D_MODEL = 1024
BATCH = 8
SEQ = 4096
DEPTH = 2

D_FF = 2816
FFN_RES_SCALE = 0.5
RMS_EPS = 1e-6
PLE_DIM = 256
QBLK = 128
A_HEADS = 8
A_KV_HEADS = 2
A_GROUP = A_HEADS // A_KV_HEADS
A_HEAD_DIM = 64
WINDOW = 128
B_HEADS = 8
B_Q_LORA = 256
B_KV_LORA = 128
B_NOPE_DIM = 64
B_ROPE_DIM = 32
B_V_DIM = 64
ROPE_THETA = 10000.0
C_HEADS = 16
C_HEAD_DIM = 64
FORGET_BIAS_CENTER = 3.0
N_EVEN = (DEPTH + 1) // 2
N_ODD = DEPTH // 2
EVEN_IN_SPLITS = (A_HEADS * A_HEAD_DIM, A_KV_HEADS * A_HEAD_DIM, A_KV_HEADS * A_HEAD_DIM, B_Q_LORA, B_KV_LORA, B_ROPE_DIM)
EVEN_IN_DIM = A_HEADS * A_HEAD_DIM + 2 * A_KV_HEADS * A_HEAD_DIM + B_Q_LORA + B_KV_LORA + B_ROPE_DIM
EVEN_MIX_DIM = A_HEADS * A_HEAD_DIM + B_HEADS * B_V_DIM
ODD_MIX_DIM = C_HEADS * C_HEAD_DIM
ODD_IN_DIM = 3 * ODD_MIX_DIM + C_HEADS

kernel_name = 'hybrid_swa_mla_fox_macaron'


def rms_norm(x, g):
    xf = x.astype(jnp.float32)
    y = xf * lax.rsqrt(jnp.mean(xf * xf, axis=-1, keepdims=True) + RMS_EPS)
    return (y * g.astype(jnp.float32)).astype(x.dtype)


def swiglu(x, w_gate_up, w_down):
    g, u = jnp.split(x @ w_gate_up, 2, axis=-1)
    return (jax.nn.silu(g) * u) @ w_down


def alibi_slopes(n):
    return 2.0 ** (-8.0 * jnp.arange(1, n + 1, dtype=jnp.float32) / n)


def rope_tables(seq, dim):
    inv = ROPE_THETA ** (-jnp.arange(0, dim, 2, dtype=jnp.float32) / dim)
    ang = jnp.arange(seq, dtype=jnp.float32)[:, None] * inv[None, :]
    return jnp.cos(ang), jnp.sin(ang)


def apply_rope(x, cos, sin):
    half = x.shape[-1] // 2
    x1 = x[..., :half].astype(jnp.float32)
    x2 = x[..., half:].astype(jnp.float32)
    return jnp.concatenate([x1 * cos - x2 * sin, x1 * sin + x2 * cos], axis=-1).astype(x.dtype)


def swa_sink_attention(q, k, v, sinks):
    B, S = q.shape[0], q.shape[1]
    nb = S // WINDOW
    qb = q.reshape(B, nb, WINDOW, A_KV_HEADS, A_GROUP, A_HEAD_DIM)
    pad = jnp.zeros((B, WINDOW, A_KV_HEADS, A_HEAD_DIM), k.dtype)
    kp = jnp.concatenate([pad, k], axis=1).reshape(B, nb + 1, WINDOW, A_KV_HEADS, A_HEAD_DIM)
    vp = jnp.concatenate([pad, v], axis=1).reshape(B, nb + 1, WINDOW, A_KV_HEADS, A_HEAD_DIM)
    kb = jnp.concatenate([kp[:, :-1], kp[:, 1:]], axis=2)
    vb = jnp.concatenate([vp[:, :-1], vp[:, 1:]], axis=2)
    s = jnp.einsum('bnqkgd,bnskd->bnkgqs', qb, kb).astype(jnp.float32) * (A_HEAD_DIM ** -0.5)
    qi = jnp.arange(WINDOW)[:, None]
    kj = jnp.arange(2 * WINDOW)[None, :]
    dist = qi + WINDOW - kj
    band = (dist >= 0) & (dist < WINDOW)
    start_ok = (jnp.arange(nb)[:, None, None] * WINDOW + kj[None] - WINDOW) >= 0
    mask = band[None] & start_ok
    slopes = alibi_slopes(A_HEADS).reshape(A_KV_HEADS, A_GROUP)
    s = s - slopes[None, None, :, :, None, None] * dist.astype(jnp.float32)[None, None, None, None]
    s = jnp.where(mask[None, :, None, None], s, -jnp.inf)
    sink = sinks.astype(jnp.float32).reshape(A_KV_HEADS, A_GROUP)[None, None, :, :, None, None]
    m = jnp.maximum(jnp.max(s, axis=-1, keepdims=True), sink)
    e = jnp.exp(s - m)
    pr = e / (jnp.sum(e, axis=-1, keepdims=True) + jnp.exp(sink - m))
    out = jnp.einsum('bnkgqs,bnskd->bnqkgd', pr.astype(v.dtype), vb)
    return out.reshape(B, S, A_HEADS * A_HEAD_DIM)


def mla_attention(q_nope, q_rope, k_nope, k_rope, v):
    B, S = q_nope.shape[0], q_nope.shape[1]
    nb = S // QBLK
    qn = q_nope.reshape(B, nb, QBLK, B_HEADS, B_NOPE_DIM).transpose(1, 0, 2, 3, 4)
    qr = q_rope.reshape(B, nb, QBLK, B_HEADS, B_ROPE_DIM).transpose(1, 0, 2, 3, 4)
    kpos = jnp.arange(S)
    scale = (B_NOPE_DIM + B_ROPE_DIM) ** -0.5

    def one_block(args):
        qn_b, qr_b, n = args
        s = jnp.einsum('bqhd,bkhd->bhqk', qn_b, k_nope) + jnp.einsum('bqhd,bkd->bhqk', qr_b, k_rope)
        s = s.astype(jnp.float32) * scale
        qpos = n * QBLK + jnp.arange(QBLK)
        s = jnp.where(kpos[None, :] <= qpos[:, None], s, -jnp.inf)
        pr = jax.nn.softmax(s, axis=-1)
        return jnp.einsum('bhqk,bkhd->bqhd', pr.astype(v.dtype), v)

    out = lax.map(one_block, (qn, qr, jnp.arange(nb)))
    return out.transpose(1, 0, 2, 3, 4).reshape(B, S, B_HEADS * B_V_DIM)


def fox_attention(q, k, v, logc):
    B, S = q.shape[0], q.shape[1]
    nb = S // QBLK
    qb = q.reshape(B, nb, QBLK, C_HEADS, C_HEAD_DIM).transpose(1, 0, 2, 3, 4)
    cb = logc.reshape(B, nb, QBLK, C_HEADS).transpose(1, 0, 2, 3)
    c_keys = logc.transpose(0, 2, 1)
    kpos = jnp.arange(S)

    def one_block(args):
        q_b, c_b, n = args
        s = jnp.einsum('bqhd,bkhd->bhqk', q_b, k).astype(jnp.float32) * (C_HEAD_DIM ** -0.5)
        s = s + c_b.transpose(0, 2, 1)[..., None] - c_keys[:, :, None, :]
        qpos = n * QBLK + jnp.arange(QBLK)
        s = jnp.where(kpos[None, :] <= qpos[:, None], s, -jnp.inf)
        pr = jax.nn.softmax(s, axis=-1)
        return jnp.einsum('bhqk,bkhd->bqhd', pr.astype(v.dtype), v)

    out = lax.map(one_block, (qb, cb, jnp.arange(nb)))
    return out.transpose(1, 0, 2, 3, 4).reshape(B, S, C_HEADS * C_HEAD_DIM)


def even_mixer(h, w_in, sinks, cq_norm, w_uq, ckv_norm, w_ukv, w_out):
    B, S = h.shape[0], h.shape[1]
    idx = [int(i) for i in np.cumsum(EVEN_IN_SPLITS)[:-1]]
    a_q, a_k, a_v, c_q, c_kv, k_rope = jnp.split(h @ w_in, idx, axis=-1)
    out_a = swa_sink_attention(a_q.reshape(B, S, A_HEADS, A_HEAD_DIM),
                               a_k.reshape(B, S, A_KV_HEADS, A_HEAD_DIM),
                               a_v.reshape(B, S, A_KV_HEADS, A_HEAD_DIM), sinks)
    q = (rms_norm(c_q, cq_norm) @ w_uq).reshape(B, S, B_HEADS, B_NOPE_DIM + B_ROPE_DIM)
    kv = (rms_norm(c_kv, ckv_norm) @ w_ukv).reshape(B, S, B_HEADS, B_NOPE_DIM + B_V_DIM)
    cos, sin = rope_tables(S, B_ROPE_DIM)
    q_nope = q[..., :B_NOPE_DIM]
    q_rope = apply_rope(q[..., B_NOPE_DIM:], cos[None, :, None], sin[None, :, None])
    k_nope = kv[..., :B_NOPE_DIM]
    v = kv[..., B_NOPE_DIM:]
    k_rope = apply_rope(k_rope, cos[None], sin[None])
    out_b = mla_attention(q_nope, q_rope, k_nope, k_rope, v)
    return jnp.concatenate([out_a, out_b], axis=-1) @ w_out


def odd_mixer(h, w_in, b_f, w_out):
    B, S = h.shape[0], h.shape[1]
    w = ODD_MIX_DIM
    q, k, v, f_logit = jnp.split(h @ w_in, [w, 2 * w, 3 * w], axis=-1)
    logf = jax.nn.log_sigmoid(f_logit.astype(jnp.float32) + b_f.astype(jnp.float32))
    logc = jnp.cumsum(logf, axis=1)
    shp = (B, S, C_HEADS, C_HEAD_DIM)
    out = fox_attention(q.reshape(shp), k.reshape(shp), v.reshape(shp), logc)
    return out @ w_out


def _normal(key, shape, scale):
    return jax.random.normal(key, shape, jnp.float32) * scale


def _fwd_setup_inputs(seed: int = 0) -> dict:
    key = jax.random.key(seed)
    ks = jax.random.split(key, 23)
    D = D_MODEL
    return {
        'x': _normal(ks[0], (BATCH, SEQ, D), 1.0),
        'p': _normal(ks[1], (DEPTH, BATCH, SEQ, PLE_DIM), 1.0),
        'ffa_norm': 1.0 + _normal(ks[2], (DEPTH, D), 0.05),
        'ffa_w_gate_up': _normal(ks[3], (DEPTH, D, 2 * D_FF), D ** -0.5),
        'ffa_w_down': _normal(ks[4], (DEPTH, D_FF, D), D_FF ** -0.5),
        'mix_norm': 1.0 + _normal(ks[5], (DEPTH, D), 0.05),
        'ffb_norm': 1.0 + _normal(ks[6], (DEPTH, D), 0.05),
        'ffb_w_gate_up': _normal(ks[7], (DEPTH, D, 2 * D_FF), D ** -0.5),
        'ffb_w_down': _normal(ks[8], (DEPTH, D_FF, D), D_FF ** -0.5),
        'ple_norm': 1.0 + _normal(ks[9], (DEPTH, D), 0.05),
        'ple_w_gate': _normal(ks[10], (DEPTH, D, D), D ** -0.5),
        'ple_w_proj': _normal(ks[11], (DEPTH, PLE_DIM, D), PLE_DIM ** -0.5),
        'ev_w_in': _normal(ks[12], (N_EVEN, D, EVEN_IN_DIM), D ** -0.5),
        'ev_sinks': _normal(ks[13], (N_EVEN, A_HEADS), 0.5),
        'ev_cq_norm': 1.0 + _normal(ks[14], (N_EVEN, B_Q_LORA), 0.05),
        'ev_w_uq': _normal(ks[15], (N_EVEN, B_Q_LORA, B_HEADS * (B_NOPE_DIM + B_ROPE_DIM)), B_Q_LORA ** -0.5),
        'ev_ckv_norm': 1.0 + _normal(ks[16], (N_EVEN, B_KV_LORA), 0.05),
        'ev_w_ukv': _normal(ks[17], (N_EVEN, B_KV_LORA, B_HEADS * (B_NOPE_DIM + B_V_DIM)), B_KV_LORA ** -0.5),
        'ev_w_out': _normal(ks[18], (N_EVEN, EVEN_MIX_DIM, D), EVEN_MIX_DIM ** -0.5),
        'od_w_in': _normal(ks[19], (N_ODD, D, ODD_IN_DIM), D ** -0.5),
        'od_b_f': FORGET_BIAS_CENTER + _normal(ks[20], (N_ODD, C_HEADS), 0.5),
        'od_w_out': _normal(ks[21], (N_ODD, ODD_MIX_DIM, D), ODD_MIX_DIM ** -0.5),
        'final_norm': 1.0 + _normal(ks[22], (D,), 0.05),
    }


def _fwd_reference(x, p, ffa_norm, ffa_w_gate_up, ffa_w_down, mix_norm, ffb_norm, ffb_w_gate_up, ffb_w_down,
              ple_norm, ple_w_gate, ple_w_proj, ev_w_in, ev_sinks, ev_cq_norm, ev_w_uq, ev_ckv_norm,
              ev_w_ukv, ev_w_out, od_w_in, od_b_f, od_w_out, final_norm):
    h = x
    for i in range(DEPTH):
        j = i // 2
        h = h + FFN_RES_SCALE * swiglu(rms_norm(h, ffa_norm[i]), ffa_w_gate_up[i], ffa_w_down[i])
        hn = rms_norm(h, mix_norm[i])
        if i % 2 == 0:
            h = h + even_mixer(hn, ev_w_in[j], ev_sinks[j], ev_cq_norm[j], ev_w_uq[j],
                               ev_ckv_norm[j], ev_w_ukv[j], ev_w_out[j])
        else:
            h = h + odd_mixer(hn, od_w_in[j], od_b_f[j], od_w_out[j])
        h = h + FFN_RES_SCALE * swiglu(rms_norm(h, ffb_norm[i]), ffb_w_gate_up[i], ffb_w_down[i])
        gate = jax.nn.sigmoid(rms_norm(h, ple_norm[i]) @ ple_w_gate[i])
        h = h + gate * (p[i] @ ple_w_proj[i])
    return rms_norm(h, final_norm)


import jax as _jax
import jax.numpy as _jnp

TWIN_FORMAT = 'train_step'
FWD_PARAMS = ['x', 'p', 'ffa_norm', 'ffa_w_gate_up', 'ffa_w_down', 'mix_norm', 'ffb_norm', 'ffb_w_gate_up', 'ffb_w_down', 'ple_norm', 'ple_w_gate', 'ple_w_proj', 'ev_w_in', 'ev_sinks', 'ev_cq_norm', 'ev_w_uq', 'ev_ckv_norm', 'ev_w_ukv', 'ev_w_out', 'od_w_in', 'od_b_f', 'od_w_out', 'final_norm']
TWIN_WEIGHTS = ['ffa_norm', 'ffa_w_gate_up', 'ffa_w_down', 'mix_norm', 'ffb_norm', 'ffb_w_gate_up', 'ffb_w_down', 'ple_norm', 'ple_w_gate', 'ple_w_proj', 'ev_w_in', 'ev_sinks', 'ev_cq_norm', 'ev_w_uq', 'ev_ckv_norm', 'ev_w_ukv', 'ev_w_out', 'od_w_in', 'od_b_f', 'od_w_out', 'final_norm']
TWIN_DIFF_INPUT = 'x'
TWIN_INPUTS = ['x', 'p', 'ffa_norm', 'ffa_w_gate_up', 'ffa_w_down', 'mix_norm', 'ffb_norm', 'ffb_w_gate_up', 'ffb_w_down', 'ple_norm', 'ple_w_gate', 'ple_w_proj', 'ev_w_in', 'ev_sinks', 'ev_cq_norm', 'ev_w_uq', 'ev_ckv_norm', 'ev_w_ukv', 'ev_w_out', 'od_w_in', 'od_b_f', 'od_w_out', 'final_norm', 'loss_target', 'm_ffa_norm', 'm_ffa_w_gate_up', 'm_ffa_w_down', 'm_mix_norm', 'm_ffb_norm', 'm_ffb_w_gate_up', 'm_ffb_w_down', 'm_ple_norm', 'm_ple_w_gate', 'm_ple_w_proj', 'm_ev_w_in', 'm_ev_sinks', 'm_ev_cq_norm', 'm_ev_w_uq', 'm_ev_ckv_norm', 'm_ev_w_ukv', 'm_ev_w_out', 'm_od_w_in', 'm_od_b_f', 'm_od_w_out', 'm_final_norm', 'v_ffa_norm', 'v_ffa_w_gate_up', 'v_ffa_w_down', 'v_mix_norm', 'v_ffb_norm', 'v_ffb_w_gate_up', 'v_ffb_w_down', 'v_ple_norm', 'v_ple_w_gate', 'v_ple_w_proj', 'v_ev_w_in', 'v_ev_sinks', 'v_ev_cq_norm', 'v_ev_w_uq', 'v_ev_ckv_norm', 'v_ev_w_ukv', 'v_ev_w_out', 'v_od_w_in', 'v_od_b_f', 'v_od_w_out', 'v_final_norm']
TWIN_OUTPUTS = ['loss', 'grad_x', 'grad_ffa_norm', 'grad_ffa_w_gate_up', 'grad_ffa_w_down', 'grad_mix_norm', 'grad_ffb_norm', 'grad_ffb_w_gate_up', 'grad_ffb_w_down', 'grad_ple_norm', 'grad_ple_w_gate', 'grad_ple_w_proj', 'grad_ev_w_in', 'grad_ev_sinks', 'grad_ev_cq_norm', 'grad_ev_w_uq', 'grad_ev_ckv_norm', 'grad_ev_w_ukv', 'grad_ev_w_out', 'grad_od_w_in', 'grad_od_b_f', 'grad_od_w_out', 'grad_final_norm', 'delta_ffa_norm', 'delta_ffa_w_gate_up', 'delta_ffa_w_down', 'delta_mix_norm', 'delta_ffb_norm', 'delta_ffb_w_gate_up', 'delta_ffb_w_down', 'delta_ple_norm', 'delta_ple_w_gate', 'delta_ple_w_proj', 'delta_ev_w_in', 'delta_ev_sinks', 'delta_ev_cq_norm', 'delta_ev_w_uq', 'delta_ev_ckv_norm', 'delta_ev_w_ukv', 'delta_ev_w_out', 'delta_od_w_in', 'delta_od_b_f', 'delta_od_w_out', 'delta_final_norm', 'new_m_ffa_norm', 'new_m_ffa_w_gate_up', 'new_m_ffa_w_down', 'new_m_mix_norm', 'new_m_ffb_norm', 'new_m_ffb_w_gate_up', 'new_m_ffb_w_down', 'new_m_ple_norm', 'new_m_ple_w_gate', 'new_m_ple_w_proj', 'new_m_ev_w_in', 'new_m_ev_sinks', 'new_m_ev_cq_norm', 'new_m_ev_w_uq', 'new_m_ev_ckv_norm', 'new_m_ev_w_ukv', 'new_m_ev_w_out', 'new_m_od_w_in', 'new_m_od_b_f', 'new_m_od_w_out', 'new_m_final_norm', 'new_v_ffa_norm', 'new_v_ffa_w_gate_up', 'new_v_ffa_w_down', 'new_v_mix_norm', 'new_v_ffb_norm', 'new_v_ffb_w_gate_up', 'new_v_ffb_w_down', 'new_v_ple_norm', 'new_v_ple_w_gate', 'new_v_ple_w_proj', 'new_v_ev_w_in', 'new_v_ev_sinks', 'new_v_ev_cq_norm', 'new_v_ev_w_uq', 'new_v_ev_ckv_norm', 'new_v_ev_w_ukv', 'new_v_ev_w_out', 'new_v_od_w_in', 'new_v_od_b_f', 'new_v_od_w_out', 'new_v_final_norm']
TWIN_LEAF_KINDS = {'loss': 'loss', 'grad_x': 'grad_x', 'grad_ffa_norm': 'grad_w', 'grad_ffa_w_gate_up': 'grad_w', 'grad_ffa_w_down': 'grad_w', 'grad_mix_norm': 'grad_w', 'grad_ffb_norm': 'grad_w', 'grad_ffb_w_gate_up': 'grad_w', 'grad_ffb_w_down': 'grad_w', 'grad_ple_norm': 'grad_w', 'grad_ple_w_gate': 'grad_w', 'grad_ple_w_proj': 'grad_w', 'grad_ev_w_in': 'grad_w', 'grad_ev_sinks': 'grad_w', 'grad_ev_cq_norm': 'grad_w', 'grad_ev_w_uq': 'grad_w', 'grad_ev_ckv_norm': 'grad_w', 'grad_ev_w_ukv': 'grad_w', 'grad_ev_w_out': 'grad_w', 'grad_od_w_in': 'grad_w', 'grad_od_b_f': 'grad_w', 'grad_od_w_out': 'grad_w', 'grad_final_norm': 'grad_w', 'delta_ffa_norm': 'delta_w', 'delta_ffa_w_gate_up': 'delta_w', 'delta_ffa_w_down': 'delta_w', 'delta_mix_norm': 'delta_w', 'delta_ffb_norm': 'delta_w', 'delta_ffb_w_gate_up': 'delta_w', 'delta_ffb_w_down': 'delta_w', 'delta_ple_norm': 'delta_w', 'delta_ple_w_gate': 'delta_w', 'delta_ple_w_proj': 'delta_w', 'delta_ev_w_in': 'delta_w', 'delta_ev_sinks': 'delta_w', 'delta_ev_cq_norm': 'delta_w', 'delta_ev_w_uq': 'delta_w', 'delta_ev_ckv_norm': 'delta_w', 'delta_ev_w_ukv': 'delta_w', 'delta_ev_w_out': 'delta_w', 'delta_od_w_in': 'delta_w', 'delta_od_b_f': 'delta_w', 'delta_od_w_out': 'delta_w', 'delta_final_norm': 'delta_w', 'new_m_ffa_norm': 'new_m', 'new_m_ffa_w_gate_up': 'new_m', 'new_m_ffa_w_down': 'new_m', 'new_m_mix_norm': 'new_m', 'new_m_ffb_norm': 'new_m', 'new_m_ffb_w_gate_up': 'new_m', 'new_m_ffb_w_down': 'new_m', 'new_m_ple_norm': 'new_m', 'new_m_ple_w_gate': 'new_m', 'new_m_ple_w_proj': 'new_m', 'new_m_ev_w_in': 'new_m', 'new_m_ev_sinks': 'new_m', 'new_m_ev_cq_norm': 'new_m', 'new_m_ev_w_uq': 'new_m', 'new_m_ev_ckv_norm': 'new_m', 'new_m_ev_w_ukv': 'new_m', 'new_m_ev_w_out': 'new_m', 'new_m_od_w_in': 'new_m', 'new_m_od_b_f': 'new_m', 'new_m_od_w_out': 'new_m', 'new_m_final_norm': 'new_m', 'new_v_ffa_norm': 'new_v', 'new_v_ffa_w_gate_up': 'new_v', 'new_v_ffa_w_down': 'new_v', 'new_v_mix_norm': 'new_v', 'new_v_ffb_norm': 'new_v', 'new_v_ffb_w_gate_up': 'new_v', 'new_v_ffb_w_down': 'new_v', 'new_v_ple_norm': 'new_v', 'new_v_ple_w_gate': 'new_v', 'new_v_ple_w_proj': 'new_v', 'new_v_ev_w_in': 'new_v', 'new_v_ev_sinks': 'new_v', 'new_v_ev_cq_norm': 'new_v', 'new_v_ev_w_uq': 'new_v', 'new_v_ev_ckv_norm': 'new_v', 'new_v_ev_w_ukv': 'new_v', 'new_v_ev_w_out': 'new_v', 'new_v_od_w_in': 'new_v', 'new_v_od_b_f': 'new_v', 'new_v_od_w_out': 'new_v', 'new_v_final_norm': 'new_v'}


def _forward(args):
    return _fwd_reference(*[args[k] for k in FWD_PARAMS])


def _output_shape():
    out = _jax.eval_shape(lambda: _forward(_fwd_setup_inputs(0)))
    return out.shape, out.dtype

N_MICROBATCH = 1
ADAM_LR = 0.001
ADAM_B1 = 0.9
ADAM_B2 = 0.999
ADAM_EPS = 1e-08
ADAM_WD = 0.01
ADAM_STEP = 10
PER_EXAMPLE_BATCH_AXIS = {'x': 0, 'p': 1, 'loss_target': 0}
SHARED_INPUTS = []
_WEIGHT_DTYPES = {'ffa_norm': _jnp.float32, 'ffa_w_gate_up': _jnp.float32, 'ffa_w_down': _jnp.float32, 'mix_norm': _jnp.float32, 'ffb_norm': _jnp.float32, 'ffb_w_gate_up': _jnp.float32, 'ffb_w_down': _jnp.float32, 'ple_norm': _jnp.float32, 'ple_w_gate': _jnp.float32, 'ple_w_proj': _jnp.float32, 'ev_w_in': _jnp.float32, 'ev_sinks': _jnp.float32, 'ev_cq_norm': _jnp.float32, 'ev_w_uq': _jnp.float32, 'ev_ckv_norm': _jnp.float32, 'ev_w_ukv': _jnp.float32, 'ev_w_out': _jnp.float32, 'od_w_in': _jnp.float32, 'od_b_f': _jnp.float32, 'od_w_out': _jnp.float32, 'final_norm': _jnp.float32}
MOMENT_SCALE = {'ffa_norm': 6.940814e-02, 'ffa_w_gate_up': 2.871204e-02, 'ffa_w_down': 4.686240e-02, 'mix_norm': 6.464819e-02, 'ffb_norm': 5.983734e-02, 'ffb_w_gate_up': 2.560957e-02, 'ffb_w_down': 4.194750e-02, 'ple_norm': 2.968966e-02, 'ple_w_gate': 2.918254e-02, 'ple_w_proj': 7.587155e-02, 'ev_w_in': 6.065036e-02, 'ev_sinks': 3.091446e-02, 'ev_cq_norm': 4.614480e-02, 'ev_w_uq': 2.468347e-02, 'ev_ckv_norm': 9.672518e-02, 'ev_w_ukv': 3.288672e-02, 'ev_w_out': 4.476269e-02, 'od_w_in': 3.551108e-02, 'od_b_f': 2.738606e-01, 'od_w_out': 4.148182e-02, 'final_norm': 3.194571e+01}


def _to_microbatches(a, axis):
    t = _jnp.moveaxis(a, axis, 0)
    t = t.reshape((N_MICROBATCH, t.shape[0] // N_MICROBATCH) + t.shape[1:])
    return _jnp.moveaxis(t, 1, axis + 1)


def setup_inputs(seed: int = 0) -> dict:
    inp = _fwd_setup_inputs(seed)
    key = _jax.random.fold_in(_jax.random.key(seed), 7919)
    shape, _ = _output_shape()
    out = dict(inp)
    out["loss_target"] = _jax.random.normal(_jax.random.fold_in(key, 0), shape, _jnp.float32)
    for i, name in enumerate(TWIN_WEIGHTS):
        w = inp[name].astype(_jnp.float32)
        if MOMENT_SCALE is None:
            s = _jnp.sqrt(_jnp.mean(_jnp.square(w)) + 1e-30)
        else:
            s = MOMENT_SCALE[name]
        km, kv = _jax.random.split(_jax.random.fold_in(key, i + 1))
        out[name] = w
        out["m_" + name] = s * _jax.random.normal(km, w.shape, _jnp.float32)
        out["v_" + name] = (s * s) * _jax.random.uniform(kv, w.shape, _jnp.float32, 0.5, 1.5)
    if N_MICROBATCH > 1:
        for name, axis in PER_EXAMPLE_BATCH_AXIS.items():
            out[name] = _to_microbatches(out[name], axis)
    return {'x': out['x'], 'p': out['p'], 'ffa_norm': out['ffa_norm'], 'ffa_w_gate_up': out['ffa_w_gate_up'], 'ffa_w_down': out['ffa_w_down'], 'mix_norm': out['mix_norm'], 'ffb_norm': out['ffb_norm'], 'ffb_w_gate_up': out['ffb_w_gate_up'], 'ffb_w_down': out['ffb_w_down'], 'ple_norm': out['ple_norm'], 'ple_w_gate': out['ple_w_gate'], 'ple_w_proj': out['ple_w_proj'], 'ev_w_in': out['ev_w_in'], 'ev_sinks': out['ev_sinks'], 'ev_cq_norm': out['ev_cq_norm'], 'ev_w_uq': out['ev_w_uq'], 'ev_ckv_norm': out['ev_ckv_norm'], 'ev_w_ukv': out['ev_w_ukv'], 'ev_w_out': out['ev_w_out'], 'od_w_in': out['od_w_in'], 'od_b_f': out['od_b_f'], 'od_w_out': out['od_w_out'], 'final_norm': out['final_norm'], 'loss_target': out['loss_target'], 'm_ffa_norm': out['m_ffa_norm'], 'm_ffa_w_gate_up': out['m_ffa_w_gate_up'], 'm_ffa_w_down': out['m_ffa_w_down'], 'm_mix_norm': out['m_mix_norm'], 'm_ffb_norm': out['m_ffb_norm'], 'm_ffb_w_gate_up': out['m_ffb_w_gate_up'], 'm_ffb_w_down': out['m_ffb_w_down'], 'm_ple_norm': out['m_ple_norm'], 'm_ple_w_gate': out['m_ple_w_gate'], 'm_ple_w_proj': out['m_ple_w_proj'], 'm_ev_w_in': out['m_ev_w_in'], 'm_ev_sinks': out['m_ev_sinks'], 'm_ev_cq_norm': out['m_ev_cq_norm'], 'm_ev_w_uq': out['m_ev_w_uq'], 'm_ev_ckv_norm': out['m_ev_ckv_norm'], 'm_ev_w_ukv': out['m_ev_w_ukv'], 'm_ev_w_out': out['m_ev_w_out'], 'm_od_w_in': out['m_od_w_in'], 'm_od_b_f': out['m_od_b_f'], 'm_od_w_out': out['m_od_w_out'], 'm_final_norm': out['m_final_norm'], 'v_ffa_norm': out['v_ffa_norm'], 'v_ffa_w_gate_up': out['v_ffa_w_gate_up'], 'v_ffa_w_down': out['v_ffa_w_down'], 'v_mix_norm': out['v_mix_norm'], 'v_ffb_norm': out['v_ffb_norm'], 'v_ffb_w_gate_up': out['v_ffb_w_gate_up'], 'v_ffb_w_down': out['v_ffb_w_down'], 'v_ple_norm': out['v_ple_norm'], 'v_ple_w_gate': out['v_ple_w_gate'], 'v_ple_w_proj': out['v_ple_w_proj'], 'v_ev_w_in': out['v_ev_w_in'], 'v_ev_sinks': out['v_ev_sinks'], 'v_ev_cq_norm': out['v_ev_cq_norm'], 'v_ev_w_uq': out['v_ev_w_uq'], 'v_ev_ckv_norm': out['v_ev_ckv_norm'], 'v_ev_w_ukv': out['v_ev_w_ukv'], 'v_ev_w_out': out['v_ev_w_out'], 'v_od_w_in': out['v_od_w_in'], 'v_od_b_f': out['v_od_b_f'], 'v_od_w_out': out['v_od_w_out'], 'v_final_norm': out['v_final_norm']}


def _loss(weights, diff, rest, loss_target):
    with _jax.named_scope("forward"):
        args = {**rest, TWIN_DIFF_INPUT: diff, **{k: w.astype(_WEIGHT_DTYPES[k]) for k, w in weights.items()}}
        y = _forward(args)
    with _jax.named_scope("loss_head"):
        err = _jnp.square(y.astype(_jnp.float32) - loss_target)
        return 0.5 * _jnp.sum(_jnp.mean(err, axis=-1)) if err.ndim else 0.5 * err


def _adamw(w, g, m, v):
    m = ADAM_B1 * m + (1.0 - ADAM_B1) * g
    v = ADAM_B2 * v + (1.0 - ADAM_B2) * _jnp.square(g)
    m_hat = m / (1.0 - ADAM_B1 ** ADAM_STEP)
    v_hat = v / (1.0 - ADAM_B2 ** ADAM_STEP)
    delta = -ADAM_LR * (m_hat / (_jnp.sqrt(v_hat) + ADAM_EPS) + ADAM_WD * w)
    return delta, m, v


def reference(x, p, ffa_norm, ffa_w_gate_up, ffa_w_down, mix_norm, ffb_norm, ffb_w_gate_up, ffb_w_down, ple_norm, ple_w_gate, ple_w_proj, ev_w_in, ev_sinks, ev_cq_norm, ev_w_uq, ev_ckv_norm, ev_w_ukv, ev_w_out, od_w_in, od_b_f, od_w_out, final_norm, loss_target, m_ffa_norm, m_ffa_w_gate_up, m_ffa_w_down, m_mix_norm, m_ffb_norm, m_ffb_w_gate_up, m_ffb_w_down, m_ple_norm, m_ple_w_gate, m_ple_w_proj, m_ev_w_in, m_ev_sinks, m_ev_cq_norm, m_ev_w_uq, m_ev_ckv_norm, m_ev_w_ukv, m_ev_w_out, m_od_w_in, m_od_b_f, m_od_w_out, m_final_norm, v_ffa_norm, v_ffa_w_gate_up, v_ffa_w_down, v_mix_norm, v_ffb_norm, v_ffb_w_gate_up, v_ffb_w_down, v_ple_norm, v_ple_w_gate, v_ple_w_proj, v_ev_w_in, v_ev_sinks, v_ev_cq_norm, v_ev_w_uq, v_ev_ckv_norm, v_ev_w_ukv, v_ev_w_out, v_od_w_in, v_od_b_f, v_od_w_out, v_final_norm):
    given = dict(x=x, p=p, ffa_norm=ffa_norm, ffa_w_gate_up=ffa_w_gate_up, ffa_w_down=ffa_w_down, mix_norm=mix_norm, ffb_norm=ffb_norm, ffb_w_gate_up=ffb_w_gate_up, ffb_w_down=ffb_w_down, ple_norm=ple_norm, ple_w_gate=ple_w_gate, ple_w_proj=ple_w_proj, ev_w_in=ev_w_in, ev_sinks=ev_sinks, ev_cq_norm=ev_cq_norm, ev_w_uq=ev_w_uq, ev_ckv_norm=ev_ckv_norm, ev_w_ukv=ev_w_ukv, ev_w_out=ev_w_out, od_w_in=od_w_in, od_b_f=od_b_f, od_w_out=od_w_out, final_norm=final_norm, loss_target=loss_target, m_ffa_norm=m_ffa_norm, m_ffa_w_gate_up=m_ffa_w_gate_up, m_ffa_w_down=m_ffa_w_down, m_mix_norm=m_mix_norm, m_ffb_norm=m_ffb_norm, m_ffb_w_gate_up=m_ffb_w_gate_up, m_ffb_w_down=m_ffb_w_down, m_ple_norm=m_ple_norm, m_ple_w_gate=m_ple_w_gate, m_ple_w_proj=m_ple_w_proj, m_ev_w_in=m_ev_w_in, m_ev_sinks=m_ev_sinks, m_ev_cq_norm=m_ev_cq_norm, m_ev_w_uq=m_ev_w_uq, m_ev_ckv_norm=m_ev_ckv_norm, m_ev_w_ukv=m_ev_w_ukv, m_ev_w_out=m_ev_w_out, m_od_w_in=m_od_w_in, m_od_b_f=m_od_b_f, m_od_w_out=m_od_w_out, m_final_norm=m_final_norm, v_ffa_norm=v_ffa_norm, v_ffa_w_gate_up=v_ffa_w_gate_up, v_ffa_w_down=v_ffa_w_down, v_mix_norm=v_mix_norm, v_ffb_norm=v_ffb_norm, v_ffb_w_gate_up=v_ffb_w_gate_up, v_ffb_w_down=v_ffb_w_down, v_ple_norm=v_ple_norm, v_ple_w_gate=v_ple_w_gate, v_ple_w_proj=v_ple_w_proj, v_ev_w_in=v_ev_w_in, v_ev_sinks=v_ev_sinks, v_ev_cq_norm=v_ev_cq_norm, v_ev_w_uq=v_ev_w_uq, v_ev_ckv_norm=v_ev_ckv_norm, v_ev_w_ukv=v_ev_w_ukv, v_ev_w_out=v_ev_w_out, v_od_w_in=v_od_w_in, v_od_b_f=v_od_b_f, v_od_w_out=v_od_w_out, v_final_norm=v_final_norm)
    weights = {n: given[n] for n in TWIN_WEIGHTS}
    shared = {n: given[n] for n in SHARED_INPUTS}
    per_example = {n: given[n] for n in ['x', 'p']}
    grad_fn = _jax.value_and_grad(_loss, argnums=(0, 1))

    def one_microbatch(ex, loss_target):
        ex = dict(ex)
        diff = ex.pop(TWIN_DIFF_INPUT)
        return grad_fn(weights, diff, {**shared, **ex}, loss_target)

    if N_MICROBATCH == 1:
        loss, (grad_w, grad_x) = one_microbatch(per_example, given["loss_target"])
    else:
        def body(carry, xs):
            loss_sum, grad_sum = carry
            l_k, (gw_k, gx_k) = one_microbatch(xs[0], xs[1])
            with _jax.named_scope("update"):
                return (loss_sum + l_k, _jax.tree.map(_jnp.add, grad_sum, gw_k)), gx_k

        init = (_jnp.zeros((), _jnp.float32), _jax.tree.map(_jnp.zeros_like, weights))
        (loss, grad_w), grad_x = _jax.lax.scan(body, init, (per_example, given["loss_target"]))
    with _jax.named_scope("update"):
        delta_w, new_m, new_v = {}, {}, {}
        for n in TWIN_WEIGHTS:
            delta_w[n], new_m[n], new_v[n] = _adamw(weights[n], grad_w[n], given["m_" + n], given["v_" + n])
    return (loss, grad_x, *[grad_w[n] for n in TWIN_WEIGHTS], *[delta_w[n] for n in TWIN_WEIGHTS],
            *[new_m[n] for n in TWIN_WEIGHTS], *[new_v[n] for n in TWIN_WEIGHTS])
```

```python
import functools
import math

import numpy as np
import jax
import jax.numpy as jnp
from jax import lax
from jax.experimental import pallas as pl
from jax.experimental.pallas import tpu as pltpu

F32 = jnp.float32
BF16 = jnp.bfloat16
NT = (((1,), (1,)), ((), ()))
TN = (((0,), (0,)), ((), ()))
MESH = pl.DeviceIdType.MESH

RMS_EPS = 1e-6
FFN_RES_SCALE = 0.5
A_HEADS, A_KV_HEADS, A_HEAD_DIM, WINDOW = 8, 2, 64, 128
B_HEADS, B_Q_LORA, B_KV_LORA, B_NOPE, B_ROPE, B_V = 8, 256, 128, 64, 32, 64
ROPE_THETA = 10000.0
C_HEADS, C_HEAD_DIM = 16, 64
ADAM_LR, ADAM_B1, ADAM_B2, ADAM_EPS, ADAM_WD, ADAM_STEP = 0.001, 0.9, 0.999, 1e-08, 0.01, 10

N_CHIPS = 4
LANES = 128
FLAT_COLS = 1024
GRAD_ROW_ALIGN = 512
MASK_VALUE = -1e30
VMEM_LIMIT = 48 * 2**20

SHARDED = (
    ("ffa_w_gate_up", 2), ("ffa_w_down", 1), ("ffb_w_gate_up", 2), ("ffb_w_down", 1),
    ("ple_w_gate", 1), ("ple_w_proj", 2), ("ev_w_in", 2), ("ev_w_uq", 2), ("ev_w_ukv", 2),
    ("ev_w_out", 1), ("od_w_in", 2), ("od_w_out", 1))
REPLICATED = ("ffa_norm", "mix_norm", "ffb_norm", "ple_norm", "final_norm",
              "ev_sinks", "ev_cq_norm", "ev_ckv_norm", "od_b_f")
WEIGHT_ORDER = ("ffa_norm", "ffa_w_gate_up", "ffa_w_down", "mix_norm", "ffb_norm", "ffb_w_gate_up",
                "ffb_w_down", "ple_norm", "ple_w_gate", "ple_w_proj", "ev_w_in", "ev_sinks",
                "ev_cq_norm", "ev_w_uq", "ev_ckv_norm", "ev_w_ukv", "ev_w_out", "od_w_in", "od_b_f",
                "od_w_out", "final_norm")


def _cp(*sem):
    return pltpu.CompilerParams(dimension_semantics=sem, vmem_limit_bytes=VMEM_LIMIT)


def _sigmoid(z):
    return 1.0 / (1.0 + jnp.exp(-z))


def _rms_stats(xv):
    r = lax.rsqrt(jnp.mean(xv * xv, axis=-1, keepdims=True) + RMS_EPS)
    return r, xv * r


def _rms_bwd(dxn, xv, g):
    r, xhat = _rms_stats(xv)
    u = dxn * g
    dx = r * (u - xhat * jnp.mean(u * xhat, axis=-1, keepdims=True))
    return dx, dxn * xhat


def _col_tile(k_rows, n, budget_bytes=6 * 2**20):
    if k_rows * n * 4 <= budget_bytes or n % LANES:
        return n
    units = n // LANES
    best = LANES
    for d in range(1, units + 1):
        if units % d == 0 and k_rows * d * LANES * 4 <= budget_bytes:
            best = d * LANES
    return best


def _row_tile(rows, cols, target_elems=2**18):
    if rows * cols <= target_elems or rows % 8:
        return rows
    best = 8
    for d in range(8, rows + 1, 8):
        if rows % d == 0 and d * cols <= target_elems:
            best = d
    return best


def _rms_mm_fwd(x, g, w, *, name, tm=512):
    s, k = x.shape
    n = w.shape[1]

    def body(x_ref, g_ref, w_ref, y_ref, xn_ref):
        _, xhat = _rms_stats(x_ref[...])
        xn = (xhat * g_ref[...]).astype(BF16)
        xn_ref[...] = xn
        y_ref[...] = jnp.dot(xn, w_ref[...], preferred_element_type=F32)

    return pl.pallas_call(
        body, name=name, grid=(s // tm,),
        in_specs=[pl.BlockSpec((tm, k), lambda i: (i, 0)), pl.BlockSpec((1, k), lambda i: (0, 0)),
                  pl.BlockSpec((k, n), lambda i: (0, 0))],
        out_specs=[pl.BlockSpec((tm, n), lambda i: (i, 0)), pl.BlockSpec((tm, k), lambda i: (i, 0))],
        out_shape=[jax.ShapeDtypeStruct((s, n), F32), jax.ShapeDtypeStruct((s, k), BF16)],
        compiler_params=_cp("arbitrary"))(x, g, w)


def _ffn_up(x, g, wg, wu, *, name, tm=512):
    s, k = x.shape
    f = wg.shape[1]
    tn = _col_tile(k, f)
    nj = f // tn

    def body(x_ref, g_ref, wg_ref, wu_ref, gate_ref, up_ref, act_ref, xn_ref, xn_sc):
        @pl.when(pl.program_id(1) == 0)
        def _():
            _, xhat = _rms_stats(x_ref[...])
            xn = (xhat * g_ref[...]).astype(BF16)
            xn_sc[...] = xn
            xn_ref[...] = xn

        xn = xn_sc[...]
        gg = jnp.dot(xn, wg_ref[...], preferred_element_type=F32)
        uu = jnp.dot(xn, wu_ref[...], preferred_element_type=F32)
        gate_ref[...] = gg.astype(BF16)
        up_ref[...] = uu.astype(BF16)
        act_ref[...] = ((gg * _sigmoid(gg)) * uu).astype(BF16)

    tile = pl.BlockSpec((tm, tn), lambda i, j: (i, j))
    return pl.pallas_call(
        body, name=name, grid=(s // tm, nj),
        in_specs=[pl.BlockSpec((tm, k), lambda i, j: (i, 0)), pl.BlockSpec((1, k), lambda i, j: (0, 0)),
                  pl.BlockSpec((k, tn), lambda i, j: (0, j)), pl.BlockSpec((k, tn), lambda i, j: (0, j))],
        out_specs=[tile, tile, tile, pl.BlockSpec((tm, k), lambda i, j: (i, 0))],
        out_shape=[jax.ShapeDtypeStruct((s, f), BF16)] * 3 + [jax.ShapeDtypeStruct((s, k), BF16)],
        scratch_shapes=[pltpu.VMEM((tm, k), BF16)],
        compiler_params=_cp("arbitrary", "arbitrary"))(x, g, wg, wu)


def _mm_res_fwd(a, w, res, *, scale, name, tm=512):
    s, k = a.shape
    n = w.shape[1]

    def body(a_ref, w_ref, r_ref, o_ref):
        o_ref[...] = r_ref[...] + scale * jnp.dot(a_ref[...], w_ref[...], preferred_element_type=F32)

    return pl.pallas_call(
        body, name=name, grid=(s // tm,),
        in_specs=[pl.BlockSpec((tm, k), lambda i: (i, 0)), pl.BlockSpec((k, n), lambda i: (0, 0)),
                  pl.BlockSpec((tm, n), lambda i: (i, 0))],
        out_specs=pl.BlockSpec((tm, n), lambda i: (i, 0)),
        out_shape=jax.ShapeDtypeStruct((s, n), F32),
        compiler_params=_cp("arbitrary"))(a, w, res)


def _ffn_down_bwd(dh, wd, gate, up, *, scale, name, tm=512):
    s, d = dh.shape
    f = wd.shape[0]
    tn = _col_tile(d, f)

    def body(dh_ref, wd_ref, gate_ref, up_ref, dg_ref, du_ref):
        dhb = (dh_ref[...] * scale).astype(BF16)
        da = lax.dot_general(dhb, wd_ref[...], NT, preferred_element_type=F32)
        gg = gate_ref[...].astype(F32)
        uu = up_ref[...].astype(F32)
        sg = _sigmoid(gg)
        dg_ref[...] = (da * uu * (sg * (1.0 + gg * (1.0 - sg)))).astype(BF16)
        du_ref[...] = (da * (gg * sg)).astype(BF16)

    tile = pl.BlockSpec((tm, tn), lambda i, j: (i, j))
    return pl.pallas_call(
        body, name=name, grid=(s // tm, f // tn),
        in_specs=[pl.BlockSpec((tm, d), lambda i, j: (i, 0)), pl.BlockSpec((tn, d), lambda i, j: (j, 0)), tile, tile],
        out_specs=[tile, tile],
        out_shape=[jax.ShapeDtypeStruct((s, f), BF16)] * 2,
        compiler_params=_cp("arbitrary", "arbitrary"))(dh, wd, gate, up)


def _mm_tn(a, b, *, name, b_scale=1.0, ts=512):
    s, k = a.shape
    n = b.shape[1]
    tn = _col_tile(k, n)

    def body(a_ref, b_ref, o_ref):
        @pl.when(pl.program_id(1) == 0)
        def _():
            o_ref[...] = jnp.zeros_like(o_ref)

        bv = b_ref[...]
        if b_scale != 1.0:
            bv = bv * b_scale
        o_ref[...] += lax.dot_general(a_ref[...].astype(BF16), bv.astype(BF16), TN, preferred_element_type=F32)

    return pl.pallas_call(
        body, name=name, grid=(n // tn, s // ts),
        in_specs=[pl.BlockSpec((ts, k), lambda j, t: (t, 0)), pl.BlockSpec((ts, tn), lambda j, t: (t, j))],
        out_specs=pl.BlockSpec((k, tn), lambda j, t: (0, j)),
        out_shape=jax.ShapeDtypeStruct((k, n), F32),
        compiler_params=_cp("arbitrary", "arbitrary"))(a, b)


def _mm_nt(dy, w, *, name, tm=512):
    s, n = dy.shape
    k = w.shape[0]

    def body(dy_ref, w_ref, o_ref):
        o_ref[...] = lax.dot_general(dy_ref[...].astype(BF16), w_ref[...], NT, preferred_element_type=F32)

    return pl.pallas_call(
        body, name=name, grid=(s // tm,),
        in_specs=[pl.BlockSpec((tm, n), lambda i: (i, 0)), pl.BlockSpec((k, n), lambda i: (0, 0))],
        out_specs=pl.BlockSpec((tm, k), lambda i: (i, 0)),
        out_shape=jax.ShapeDtypeStruct((s, k), F32),
        compiler_params=_cp("arbitrary"))(dy, w)


def _mm_nt_rmsbwd(pairs, x, g, dres, *, name, tm=256):
    s, k = x.shape
    npairs = len(pairs)

    def body(*refs):
        dy_refs = refs[0:2 * npairs:2]
        w_refs = refs[1:2 * npairs:2]
        rest = refs[2 * npairs:]
        x_ref, g_ref = rest[0], rest[1]
        if dres is None:
            dx_ref, dg_ref = rest[2], rest[3]
        else:
            dres_ref, dx_ref, dg_ref = rest[2], rest[3], rest[4]
        dxn = None
        for dy_ref, w_ref in zip(dy_refs, w_refs):
            t = lax.dot_general(dy_ref[...].astype(BF16), w_ref[...], NT, preferred_element_type=F32)
            dxn = t if dxn is None else dxn + t
        dx, dgrow = _rms_bwd(dxn, x_ref[...], g_ref[...])
        if dres is not None:
            dx = dx + dres_ref[...]
        dx_ref[...] = dx

        @pl.when(pl.program_id(0) == 0)
        def _():
            dg_ref[...] = jnp.zeros_like(dg_ref)

        dg_ref[...] += jnp.sum(dgrow, axis=0, keepdims=True)

    in_specs, args = [], []
    for dy, w in pairs:
        n = dy.shape[1]
        in_specs += [pl.BlockSpec((tm, n), lambda i: (i, 0)), pl.BlockSpec((k, n), lambda i: (0, 0))]
        args += [dy, w]
    row = pl.BlockSpec((tm, k), lambda i: (i, 0))
    vec = pl.BlockSpec((1, k), lambda i: (0, 0))
    in_specs += [row, vec]
    args += [x, g]
    if dres is not None:
        in_specs.append(row)
        args.append(dres)
    return pl.pallas_call(
        body, name=name, grid=(s // tm,), in_specs=in_specs, out_specs=[row, vec],
        out_shape=[jax.ShapeDtypeStruct((s, k), F32), jax.ShapeDtypeStruct((1, k), F32)],
        compiler_params=_cp("arbitrary"))(*args)


def _ple_fwd(h, g, wg, p, wp, *, name, tm=512):
    s, d = h.shape
    pd = p.shape[1]

    def body(h_ref, g_ref, wg_ref, p_ref, wp_ref, o_ref, xn_ref, gate_ref, pp_ref):
        hv = h_ref[...]
        _, xhat = _rms_stats(hv)
        xn = (xhat * g_ref[...]).astype(BF16)
        xn_ref[...] = xn
        gate = _sigmoid(jnp.dot(xn, wg_ref[...], preferred_element_type=F32))
        pp = jnp.dot(p_ref[...].astype(BF16), wp_ref[...], preferred_element_type=F32)
        gate_ref[...] = gate.astype(BF16)
        pp_ref[...] = pp.astype(BF16)
        o_ref[...] = hv + gate * pp

    row = pl.BlockSpec((tm, d), lambda i: (i, 0))
    return pl.pallas_call(
        body, name=name, grid=(s // tm,),
        in_specs=[row, pl.BlockSpec((1, d), lambda i: (0, 0)), pl.BlockSpec((d, d), lambda i: (0, 0)),
                  pl.BlockSpec((tm, pd), lambda i: (i, 0)), pl.BlockSpec((pd, d), lambda i: (0, 0))],
        out_specs=[row, row, row, row],
        out_shape=[jax.ShapeDtypeStruct((s, d), F32)] + [jax.ShapeDtypeStruct((s, d), BF16)] * 3,
        compiler_params=_cp("arbitrary"))(h, g, wg, p, wp)


def _ple_bwd_elem(dh, gate, pp, *, name, tm=512):
    s, d = dh.shape

    def body(dh_ref, gate_ref, pp_ref, dz_ref, dpp_ref):
        dhv = dh_ref[...]
        gt = gate_ref[...].astype(F32)
        dz_ref[...] = (dhv * pp_ref[...].astype(F32) * (gt * (1.0 - gt))).astype(BF16)
        dpp_ref[...] = (dhv * gt).astype(BF16)

    row = pl.BlockSpec((tm, d), lambda i: (i, 0))
    return pl.pallas_call(
        body, name=name, grid=(s // tm,), in_specs=[row, row, row], out_specs=[row, row],
        out_shape=[jax.ShapeDtypeStruct((s, d), BF16)] * 2,
        compiler_params=_cp("arbitrary"))(dh, gate, pp)


def _final_loss(h, g, tgt, *, name, tm=512):
    s, d = h.shape

    def body(h_ref, g_ref, t_ref, loss_ref, dh_ref, dg_ref):
        @pl.when(pl.program_id(0) == 0)
        def _():
            loss_ref[...] = jnp.zeros_like(loss_ref)
            dg_ref[...] = jnp.zeros_like(dg_ref)

        hv = h_ref[...]
        gv = g_ref[...]
        _, xhat = _rms_stats(hv)
        err = xhat * gv - t_ref[...]
        per_row = jnp.mean(err * err, axis=-1, keepdims=True)
        loss_ref[...] += 0.5 * jnp.sum(per_row, axis=0, keepdims=True)
        dx, dgrow = _rms_bwd(err * (1.0 / d), hv, gv)
        dh_ref[...] = dx
        dg_ref[...] += jnp.sum(dgrow, axis=0, keepdims=True)

    row = pl.BlockSpec((tm, d), lambda i: (i, 0))
    vec = pl.BlockSpec((1, d), lambda i: (0, 0))
    return pl.pallas_call(
        body, name=name, grid=(s // tm,), in_specs=[row, vec, row],
        out_specs=[pl.BlockSpec((1, LANES), lambda i: (0, 0)), row, vec],
        out_shape=[jax.ShapeDtypeStruct((1, LANES), F32), jax.ShapeDtypeStruct((s, d), F32),
                   jax.ShapeDtypeStruct((1, d), F32)],
        compiler_params=_cp("arbitrary"))(h, g, tgt)


def _rope_fwd(y1, y2, cos, sin, *, name, tm=512):
    s, r = y1.shape

    def body(a_ref, b_ref, c_ref, s_ref, o_ref):
        o_ref[...] = a_ref[...] * c_ref[...] + b_ref[...] * s_ref[...]

    row = pl.BlockSpec((tm, r), lambda i: (i, 0))
    return pl.pallas_call(
        body, name=name, grid=(s // tm,), in_specs=[row] * 4, out_specs=row,
        out_shape=jax.ShapeDtypeStruct((s, r), F32), compiler_params=_cp("arbitrary"))(y1, y2, cos, sin)


def _rope_bwd(dout, cos, sin, *, name, tm=512):
    nh, s, r = dout.shape

    def body(d_ref, c_ref, s_ref, o1_ref, o2_ref):
        tot = d_ref[0]
        for hh in range(1, nh):
            tot = tot + d_ref[hh]
        o1_ref[...] = tot * c_ref[...]
        o2_ref[...] = tot * s_ref[...]

    row = pl.BlockSpec((tm, r), lambda i: (i, 0))
    return pl.pallas_call(
        body, name=name, grid=(s // tm,),
        in_specs=[pl.BlockSpec((nh, tm, r), lambda i: (0, i, 0)), row, row], out_specs=[row, row],
        out_shape=[jax.ShapeDtypeStruct((s, r), F32)] * 2, compiler_params=_cp("arbitrary"))(dout, cos, sin)


def _split3(v):
    h1 = v.astype(BF16)
    r1 = v - h1.astype(F32)
    h2 = r1.astype(BF16)
    h3 = (r1 - h2.astype(F32)).astype(BF16)
    return h1, h2, h3


def _tri(tb, upper):
    r = lax.broadcasted_iota(jnp.int32, (tb, tb), 0)
    c = lax.broadcasted_iota(jnp.int32, (tb, tb), 1)
    return jnp.where((r <= c) if upper else (r >= c), 1.0, 0.0).astype(BF16)


def _fox_gate_fwd(ft, bf, *, name, tb=512):
    nh, s = ft.shape

    def body(f_ref, b_ref, o_ref, carry):
        @pl.when(pl.program_id(0) == 0)
        def _():
            carry[...] = jnp.zeros_like(carry)

        z = f_ref[...] + b_ref[...]
        lf = jnp.minimum(z, 0.0) - jnp.log(1.0 + jnp.exp(-jnp.abs(z)))
        tri = _tri(tb, True)
        cs = sum(jnp.dot(t, tri, preferred_element_type=F32) for t in _split3(lf))
        o_ref[...] = cs + carry[...]
        carry[...] += jnp.sum(lf, axis=-1, keepdims=True)

    return pl.pallas_call(
        body, name=name, grid=(s // tb,),
        in_specs=[pl.BlockSpec((nh, tb), lambda t: (0, t)), pl.BlockSpec((nh, 1), lambda t: (0, 0))],
        out_specs=pl.BlockSpec((nh, tb), lambda t: (0, t)),
        out_shape=jax.ShapeDtypeStruct((nh, s), F32),
        scratch_shapes=[pltpu.VMEM((nh, 1), F32)], compiler_params=_cp("arbitrary"))(ft, bf)


def _fox_gate_bwd(drow, dcol, ft, bf, *, inv_scale, name, tb=512):
    nh, s = ft.shape
    nb = s // tb

    def body(dr_ref, dc_ref, f_ref, b_ref, df_ref, db_ref, carry):
        @pl.when(pl.program_id(0) == 0)
        def _():
            carry[...] = jnp.zeros_like(carry)
            db_ref[...] = jnp.zeros_like(db_ref)

        dc = (dr_ref[...] - dc_ref[...]) * inv_scale
        tri = _tri(tb, False)
        suf = sum(jnp.dot(t, tri, preferred_element_type=F32) for t in _split3(dc)) + carry[...]
        z = f_ref[...] + b_ref[...]
        dz = suf * (1.0 / (1.0 + jnp.exp(z)))
        df_ref[...] = dz
        db_ref[...] += jnp.sum(dz, axis=-1, keepdims=True)
        carry[...] += jnp.sum(dc, axis=-1, keepdims=True)

    rev = pl.BlockSpec((nh, tb), lambda t: (0, nb - 1 - t))
    one = pl.BlockSpec((nh, 1), lambda t: (0, 0))
    return pl.pallas_call(
        body, name=name, grid=(nb,), in_specs=[rev, rev, rev, one], out_specs=[rev, one],
        out_shape=[jax.ShapeDtypeStruct((nh, s), F32), jax.ShapeDtypeStruct((nh, 1), F32)],
        scratch_shapes=[pltpu.VMEM((nh, 1), F32)], compiler_params=_cp("arbitrary"))(drow, dcol, ft, bf)


def _scores(q, k, i, jblk, *, scale, tq, tk, window, slope, kb):
    s = lax.dot_general(q, k, NT, preferred_element_type=F32) * scale
    qpos = i * tq + lax.broadcasted_iota(jnp.int32, (tq, tk), 0)
    kpos = jblk * tk + lax.broadcasted_iota(jnp.int32, (tq, tk), 1)
    dist = qpos - kpos
    if slope is not None:
        s = s - slope * dist.astype(F32)
    if kb is not None:
        s = s - kb
    ok = dist >= 0
    if window is not None:
        ok = jnp.logical_and(ok, dist < window)
    return jnp.where(ok, s, MASK_VALUE)


def _flash_fwd(q, k, v, *, scale, name, tq, window=None, slopes_sinks=None, kbias=None):
    nh, s, dq = q.shape
    nkv, _, dv = v.shape
    grp = nh // nkv
    tk = tq
    nq = s // tq
    nj = nq if window is None else 2
    assert window is None or window <= tk
    has_ss, has_kb = slopes_sinks is not None, kbias is not None

    def blk(i, jj):
        return jj if window is None else i - jj

    def body(*refs):
        q_ref, k_ref, v_ref = refs[:3]
        pos = 3
        kb_ref = ss_ref = None
        if has_kb:
            kb_ref = refs[pos]
            pos += 1
        if has_ss:
            ss_ref = refs[pos]
            pos += 1
        o_ref, lse_ref, m_sc, l_sc, acc_sc = refs[pos:pos + 5]
        h, i, jj = pl.program_id(0), pl.program_id(1), pl.program_id(2)
        jblk = blk(i, jj)
        valid = (jj <= i) if window is None else (jblk >= 0)

        @pl.when(jj == 0)
        def _():
            m_sc[...] = jnp.full_like(m_sc, MASK_VALUE)
            l_sc[...] = jnp.zeros_like(l_sc)
            acc_sc[...] = jnp.zeros_like(acc_sc)

        @pl.when(valid)
        def _():
            sc = _scores(q_ref[...], k_ref[...], i, jblk, scale=scale, tq=tq, tk=tk, window=window,
                         slope=ss_ref[0, h] if has_ss else None, kb=kb_ref[...] if has_kb else None)
            m_prev = m_sc[...]
            m_new = jnp.maximum(m_prev, jnp.max(sc, axis=-1, keepdims=True))
            alpha = jnp.exp(m_prev - m_new)
            pr = jnp.exp(sc - m_new)
            l_sc[...] = alpha * l_sc[...] + jnp.sum(pr, axis=-1, keepdims=True)
            acc_sc[...] = alpha * acc_sc[...] + jnp.dot(pr.astype(BF16), v_ref[...], preferred_element_type=F32)
            m_sc[...] = m_new

        @pl.when(jj == nj - 1)
        def _():
            m = m_sc[...]
            l = l_sc[...]
            acc = acc_sc[...]
            if has_ss:
                sink = ss_ref[1, h]
                m_f = jnp.maximum(m, sink)
                corr = jnp.exp(m - m_f)
                l = l * corr + jnp.exp(sink - m_f)
                acc = acc * corr
                m = m_f
            o_ref[...] = (acc / l).astype(BF16)
            lse_ref[...] = jnp.broadcast_to(m + jnp.log(l), (tq, LANES))

    def kv_idx(h, i, jj):
        j = jnp.minimum(jj, i) if window is None else jnp.maximum(i - jj, 0)
        return (h // grp, j, 0)

    in_specs = [pl.BlockSpec((None, tq, dq), lambda h, i, jj: (h, i, 0)),
                pl.BlockSpec((None, tk, dq), kv_idx), pl.BlockSpec((None, tk, dv), kv_idx)]
    args = [q, k, v]
    if has_kb:
        in_specs.append(pl.BlockSpec((None, 1, tk), lambda h, i, jj: (h, 0, kv_idx(h, i, jj)[1])))
        args.append(kbias)
    if has_ss:
        in_specs.append(pl.BlockSpec(memory_space=pltpu.SMEM))
        args.append(slopes_sinks)
    return pl.pallas_call(
        body, name=name, grid=(nh, nq, nj), in_specs=in_specs,
        out_specs=[pl.BlockSpec((None, tq, dv), lambda h, i, jj: (h, i, 0)),
                   pl.BlockSpec((None, tq, LANES), lambda h, i, jj: (h, i, 0))],
        out_shape=[jax.ShapeDtypeStruct((nh, s, dv), BF16), jax.ShapeDtypeStruct((nh, s, LANES), F32)],
        scratch_shapes=[pltpu.VMEM((tq, 1), F32), pltpu.VMEM((tq, 1), F32), pltpu.VMEM((tq, dv), F32)],
        compiler_params=_cp("arbitrary", "arbitrary", "arbitrary"))(*args)


def _flash_bwd(q, k, v, o, do, lse, *, scale, name, tq, window=None, slopes_sinks=None, kbias=None):
    nh, s, dq = q.shape
    nkv, _, dv = v.shape
    grp = nh // nkv
    tk = tq
    nq = s // tq
    nsteps = nq if window is None else 2
    has_ss, has_kb = slopes_sinks is not None, kbias is not None
    assert not (has_ss and window is None)

    def body(*refs):
        q_ref, k_ref, v_ref, o_ref, do_ref, lse_ref = refs[:6]
        pos = 6
        kb_ref = ss_ref = dsink_ref = None
        if has_kb:
            kb_ref = refs[pos]
            pos += 1
        if has_ss:
            ss_ref = refs[pos]
            pos += 1
        dq_ref, dk_ref, dv_ref = refs[pos:pos + 3]
        pos += 3
        if has_ss:
            dsink_ref = refs[pos]
            pos += 1
        hk, j, g, ii = (pl.program_id(a) for a in range(4))
        i = ii if window is None else j + ii
        valid = (i >= j) if window is None else (i < nq)

        @pl.when(jnp.logical_and(j == 0, jnp.logical_and(g == 0, ii == 0)))
        def _():
            dq_ref[...] = jnp.zeros_like(dq_ref)
            if has_ss:
                dsink_ref[...] = jnp.zeros_like(dsink_ref)

        @pl.when(jnp.logical_and(g == 0, ii == 0))
        def _():
            dk_ref[...] = jnp.zeros_like(dk_ref)
            dv_ref[...] = jnp.zeros_like(dv_ref)

        @pl.when(valid)
        def _():
            h = hk * grp + g
            qv, kv, dov = q_ref[...], k_ref[...], do_ref[...]
            sc = _scores(qv, kv, i, j, scale=scale, tq=tq, tk=tk, window=window,
                         slope=ss_ref[0, h] if has_ss else None, kb=kb_ref[...] if has_kb else None)
            lse_col = lse_ref[...][:, :1]
            pr = jnp.exp(sc - lse_col)
            dp = lax.dot_general(dov, v_ref[...], NT, preferred_element_type=F32)
            delta = jnp.sum(dov.astype(F32) * o_ref[...].astype(F32), axis=-1, keepdims=True)
            ds = pr * (dp - delta)
            dv_ref[...] += lax.dot_general(pr.astype(BF16), dov, TN, preferred_element_type=F32)
            dsb = (ds * scale).astype(BF16)
            dk_ref[...] += lax.dot_general(dsb, qv, TN, preferred_element_type=F32)
            rows = pl.ds(pl.multiple_of(i * tq, tq), tq)
            dq_ref[g, rows, :] += jnp.dot(dsb, kv, preferred_element_type=F32)
            if has_ss:
                @pl.when(ii == 0)
                def _():
                    psink = jnp.exp(ss_ref[1, h] - lse_col)
                    tot = jnp.sum(psink * delta, axis=0, keepdims=True)
                    dsink_ref[g] -= jnp.broadcast_to(tot, (1, LANES))

    def q_idx(hk, j, g, ii):
        i = jnp.maximum(ii, j) if window is None else jnp.minimum(j + ii, nq - 1)
        return (hk * grp + g, i, 0)

    def kv_idx(hk, j, g, ii):
        return (hk, j, 0)

    in_specs = [pl.BlockSpec((None, tq, dq), q_idx), pl.BlockSpec((None, tk, dq), kv_idx),
                pl.BlockSpec((None, tk, dv), kv_idx), pl.BlockSpec((None, tq, dv), q_idx),
                pl.BlockSpec((None, tq, dv), q_idx), pl.BlockSpec((None, tq, LANES), q_idx)]
    args = [q, k, v, o, do, lse]
    out_specs = [pl.BlockSpec((None, grp, s, dq), lambda hk, j, g, ii: (hk, 0, 0, 0)),
                 pl.BlockSpec((None, tk, dq), kv_idx), pl.BlockSpec((None, tk, dv), kv_idx)]
    out_shape = [jax.ShapeDtypeStruct((nkv, grp, s, dq), F32), jax.ShapeDtypeStruct((nkv, s, dq), F32),
                 jax.ShapeDtypeStruct((nkv, s, dv), F32)]
    if has_kb:
        in_specs.append(pl.BlockSpec((None, 1, tk), lambda hk, j, g, ii: (hk, 0, j)))
        args.append(kbias)
    if has_ss:
        in_specs.append(pl.BlockSpec(memory_space=pltpu.SMEM))
        args.append(slopes_sinks)
        out_specs.append(pl.BlockSpec((None, grp, 1, LANES), lambda hk, j, g, ii: (hk, 0, 0, 0)))
        out_shape.append(jax.ShapeDtypeStruct((nkv, grp, 1, LANES), F32))
    return pl.pallas_call(
        body, name=name, grid=(nkv, nq, grp, nsteps), in_specs=in_specs, out_specs=out_specs,
        out_shape=out_shape, compiler_params=_cp(*["arbitrary"] * 4))(*args)


def _adamw(w, g, m, v, *, name):
    shape = w.shape
    cols = shape[-1]
    rows = int(np.prod(shape[:-1])) if len(shape) > 1 else 1
    tr = _row_tile(rows, cols)
    c1 = 1.0 - ADAM_B1 ** ADAM_STEP
    c2 = 1.0 - ADAM_B2 ** ADAM_STEP

    def body(w_ref, g_ref, m_ref, v_ref, d_ref, mo_ref, vo_ref):
        gv = g_ref[...]
        mn = ADAM_B1 * m_ref[...] + (1.0 - ADAM_B1) * gv
        vn = ADAM_B2 * v_ref[...] + (1.0 - ADAM_B2) * (gv * gv)
        mo_ref[...] = mn
        vo_ref[...] = vn
        d_ref[...] = -ADAM_LR * ((mn / c1) / (jnp.sqrt(vn / c2) + ADAM_EPS) + ADAM_WD * w_ref[...])

    blk = pl.BlockSpec((tr, cols), lambda i: (i, 0))
    outs = pl.pallas_call(
        body, name=name, grid=(rows // tr,), in_specs=[blk] * 4, out_specs=[blk] * 3,
        out_shape=[jax.ShapeDtypeStruct((rows, cols), F32)] * 3,
        compiler_params=_cp("arbitrary"))(*[a.reshape(rows, cols) for a in (w, g, m, v)])
    return tuple(a.reshape(shape) for a in outs)


def _hbm_spec():
    return pl.BlockSpec(memory_space=pl.ANY)


def _allgather_halves(flat, *, name):
    r, c = flat.shape
    rh = r // 2

    def body(x_ref, out_ref, send_sems, recv_sems, local_sem):
        x, y, cc = lax.axis_index("x"), lax.axis_index("y"), lax.axis_index("c")
        me, sibling = (x, y, cc), (x, y, 1 - cc)
        chips = [(1 - x, y), (x, 1 - y), (1 - x, 1 - y)]
        my_half = x_ref.at[pl.ds(pl.multiple_of(cc * rh, 16), rh)]

        def rows(px, py, pc):
            return out_ref.at[4 * px + 2 * py + pc]

        def copy(kk, block, to, src=None):
            return pltpu.make_async_remote_copy(
                src_ref=rows(*block) if src is None else src, dst_ref=rows(*block),
                send_sem=send_sems.at[kk], recv_sem=recv_sems.at[kk], device_id=to, device_id_type=MESH)

        mine = pltpu.make_async_copy(my_half, rows(*me), local_sem)
        mine.start()
        first = [copy(0, me, sibling, src=my_half)]
        first += [copy(1 + j, me, (*chip, cc), src=my_half) for j, chip in enumerate(chips)]
        for cp in first:
            cp.start()
        passed = [copy(4 + j, (*chip, cc), sibling) for j, chip in enumerate(chips)]
        for j, chip in enumerate(chips):
            copy(1 + j, (*chip, cc), me).wait_recv()
            passed[j].start()
        copy(0, sibling, me).wait_recv()
        for j, chip in enumerate(chips):
            copy(4 + j, (*chip, 1 - cc), me).wait_recv()
        for cp in first + passed:
            cp.wait_send()
        mine.wait()

    return pl.pallas_call(
        body, name=name, in_specs=[_hbm_spec()], out_specs=_hbm_spec(),
        out_shape=jax.ShapeDtypeStruct((8, rh, c), flat.dtype),
        scratch_shapes=[pltpu.SemaphoreType.DMA((7,)), pltpu.SemaphoreType.DMA((7,)), pltpu.SemaphoreType.DMA],
        )(flat)


def _pair_swap_halves(g, *, name):
    nb, r, c = g.shape
    rh = r // 2

    def body(g_ref, out_ref, send_sem, recv_sem):
        x, y, cc = lax.axis_index("x"), lax.axis_index("y"), lax.axis_index("c")
        src = g_ref.at[:, pl.ds(pl.multiple_of((1 - cc) * rh, 8), rh), :]
        cp = pltpu.make_async_remote_copy(src_ref=src, dst_ref=out_ref, send_sem=send_sem, recv_sem=recv_sem,
                                          device_id=(x, y, 1 - cc), device_id_type=MESH)
        cp.start()
        cp.wait()

    return pl.pallas_call(
        body, name=name, in_specs=[_hbm_spec()], out_specs=_hbm_spec(),
        out_shape=jax.ShapeDtypeStruct((nb, rh, c), g.dtype),
        scratch_shapes=[pltpu.SemaphoreType.DMA, pltpu.SemaphoreType.DMA])(g)


def _add_own_half(g, recv, cidx, *, name, tr=256):
    nb, r, c = g.shape
    rh = r // 2
    tr = _row_tile(rh, c, tr * c)
    nt = rh // tr

    def body(c_ref, g_ref, r_ref, o_ref):
        o_ref[...] = g_ref[...] + r_ref[...]

    grid_spec = pltpu.PrefetchScalarGridSpec(
        num_scalar_prefetch=1, grid=(nb, nt),
        in_specs=[pl.BlockSpec((None, tr, c), lambda b, t, c_ref: (b, c_ref[0] * nt + t, 0)),
                  pl.BlockSpec((None, tr, c), lambda b, t, c_ref: (b, t, 0))],
        out_specs=pl.BlockSpec((None, tr, c), lambda b, t, c_ref: (b, t, 0)))
    return pl.pallas_call(
        body, name=name, grid_spec=grid_spec, out_shape=jax.ShapeDtypeStruct((nb, rh, c), g.dtype),
        compiler_params=_cp("arbitrary", "arbitrary"))(cidx, g, recv)


def _chip_exchange(part, *, name):
    nb, r, c = part.shape

    def body(p_ref, out_ref, send_sems, recv_sems, local_sem):
        x, y, cc = lax.axis_index("x"), lax.axis_index("y"), lax.axis_index("c")
        mine_idx = 2 * x + y
        chips = [(1 - x, y), (x, 1 - y), (1 - x, 1 - y)]
        own = pltpu.make_async_copy(p_ref.at[mine_idx], out_ref.at[mine_idx], local_sem)
        own.start()
        sends = []
        for j, (px, py) in enumerate(chips):
            sends.append(pltpu.make_async_remote_copy(
                src_ref=p_ref.at[2 * px + py], dst_ref=out_ref.at[mine_idx], send_sem=send_sems.at[j],
                recv_sem=recv_sems.at[j], device_id=(px, py, cc), device_id_type=MESH))
        for cp in sends:
            cp.start()
        for j, (px, py) in enumerate(chips):
            pltpu.make_async_remote_copy(
                src_ref=p_ref.at[mine_idx], dst_ref=out_ref.at[2 * px + py], send_sem=send_sems.at[j],
                recv_sem=recv_sems.at[j], device_id=(px, py, cc), device_id_type=MESH).wait_recv()
        for cp in sends:
            cp.wait_send()
        own.wait()

    return pl.pallas_call(
        body, name=name, in_specs=[_hbm_spec()], out_specs=_hbm_spec(),
        out_shape=jax.ShapeDtypeStruct((nb, r, c), part.dtype),
        scratch_shapes=[pltpu.SemaphoreType.DMA((3,)), pltpu.SemaphoreType.DMA((3,)), pltpu.SemaphoreType.DMA])(part)


def _sum4(parts, *, name, tr=256):
    nb, r, c = parts.shape
    tr = _row_tile(r, c, tr * c)

    def body(p_ref, o_ref):
        o_ref[...] = ((p_ref[0] + p_ref[1]) + p_ref[2]) + p_ref[3]

    return pl.pallas_call(
        body, name=name, grid=(r // tr,), in_specs=[pl.BlockSpec((nb, tr, c), lambda t: (0, t, 0))],
        out_specs=pl.BlockSpec((tr, c), lambda t: (t, 0)), out_shape=jax.ShapeDtypeStruct((r, c), parts.dtype),
        compiler_params=_cp("arbitrary"))(parts)


def _pair_join(half, *, name):
    r, c = half.shape

    def body(h_ref, out_ref, send_sem, recv_sem, local_sem):
        x, y, cc = lax.axis_index("x"), lax.axis_index("y"), lax.axis_index("c")
        own = pltpu.make_async_copy(h_ref, out_ref.at[cc], local_sem)
        own.start()
        cp = pltpu.make_async_remote_copy(src_ref=h_ref, dst_ref=out_ref.at[cc], send_sem=send_sem,
                                          recv_sem=recv_sem, device_id=(x, y, 1 - cc), device_id_type=MESH)
        cp.start()
        pltpu.make_async_remote_copy(src_ref=h_ref, dst_ref=out_ref.at[1 - cc], send_sem=send_sem,
                                     recv_sem=recv_sem, device_id=(x, y, 1 - cc), device_id_type=MESH).wait_recv()
        cp.wait_send()
        own.wait()

    return pl.pallas_call(
        body, name=name, in_specs=[_hbm_spec()], out_specs=_hbm_spec(),
        out_shape=jax.ShapeDtypeStruct((2, r, c), half.dtype),
        scratch_shapes=[pltpu.SemaphoreType.DMA, pltpu.SemaphoreType.DMA, pltpu.SemaphoreType.DMA])(half)


def _reduce_scatter(g, cidx):
    recv = _pair_swap_halves(g, name="rs_pair_swap")
    part = _add_own_half(g, recv, cidx, name="rs_pair_add")
    landed = _chip_exchange(part, name="rs_chip_exchange")
    half = _sum4(landed, name="rs_chip_sum")
    both = _pair_join(half, name="rs_pair_join")
    return both.reshape(g.shape[1], g.shape[2])


def _rows(a):
    return a.reshape(-1, FLAT_COLS)


def _pad_rows(a, rows):
    return jnp.pad(a, ((0, rows - a.shape[0]), (0, 0)))


def _to_heads(a, nh, dh, dtype=BF16):
    return a.reshape(a.shape[0], nh, dh).transpose(1, 0, 2).astype(dtype)


def _from_heads(a):
    return a.transpose(1, 0, 2).reshape(a.shape[1], -1)


def _rope_tables(s, reps):
    half = B_ROPE // 2
    inv = ROPE_THETA ** (-jnp.arange(0, B_ROPE, 2, dtype=F32) / B_ROPE)
    ang = jnp.arange(s, dtype=F32)[:, None] * inv[None, :]
    return jnp.tile(jnp.cos(ang), (1, reps)), jnp.tile(jnp.sin(ang), (1, reps))


def _alibi_slopes():
    return 2.0 ** (-8.0 * jnp.arange(1, A_HEADS + 1, dtype=F32) / A_HEADS)


def _ffn_fwd(h, norm, wts, tag):
    gate, up, act, xn = _ffn_up(h, norm, wts["wg"], wts["wu"], name=f"{tag}_up")
    out = _mm_res_fwd(act, wts["wd"], h, scale=FFN_RES_SCALE, name=f"{tag}_down")
    return out, dict(h_in=h, gate=gate, up=up, act=act, xn=xn)


def _ffn_bwd(dh, norm, wts, sv, tag):
    dgate, dup = _ffn_down_bwd(dh, wts["wd"], sv["gate"], sv["up"], scale=FFN_RES_SCALE, name=f"{tag}_down_bwd")
    d_wd = _mm_tn(sv["act"], dh, b_scale=FFN_RES_SCALE, name=f"{tag}_dwd")
    d_wg = _mm_tn(sv["xn"], dgate, name=f"{tag}_dwg")
    d_wu = _mm_tn(sv["xn"], dup, name=f"{tag}_dwu")
    dh_in, dnorm = _mm_nt_rmsbwd([(dgate, wts["wg"]), (dup, wts["wu"])], sv["h_in"], norm, dh,
                                 name=f"{tag}_dx")
    return dh_in, dnorm, d_wg, d_wu, d_wd


def _even_weights(w_in, w_uq, w_ukv):
    half = B_ROPE // 2
    base = w_in.shape[1]
    kr1, kr2 = w_in[:, base - B_ROPE:base - half], w_in[:, base - half:]
    w_in_cat = jnp.concatenate([w_in, -kr2, kr1, jnp.zeros((w_in.shape[0], 64), w_in.dtype)], axis=1)
    u3 = w_uq.reshape(w_uq.shape[0], B_HEADS, B_NOPE + B_ROPE)
    nope = u3[:, :, :B_NOPE].reshape(w_uq.shape[0], -1)
    r1 = u3[:, :, B_NOPE:B_NOPE + half].reshape(w_uq.shape[0], -1)
    r2 = u3[:, :, B_NOPE + half:].reshape(w_uq.shape[0], -1)
    w_q_cat = jnp.concatenate([nope, r1, r2, -r2, r1], axis=1)
    return w_in_cat, w_q_cat, w_ukv


def _even_fwd(h, w, i):
    s = h.shape[0]
    half = B_ROPE // 2
    ycat, xn = _rms_mm_fwd(h, w["mix_norm"][i:i + 1], w["ev_in_cat"], name="ev_in")
    a_q, a_k, a_v = ycat[:, :512], ycat[:, 512:640], ycat[:, 640:768]
    c_q, c_kv = ycat[:, 768:1024], ycat[:, 1024:1152]
    cos32, sin32 = _rope_tables(s, 2)
    kro = _rope_fwd(ycat[:, 1152:1184], ycat[:, 1184:1216], cos32, sin32, name="ev_k_rope")
    qa, ka, va = _to_heads(a_q, A_HEADS, A_HEAD_DIM), _to_heads(a_k, A_KV_HEADS, A_HEAD_DIM), _to_heads(a_v, A_KV_HEADS, A_HEAD_DIM)
    ss = jnp.stack([_alibi_slopes(), w["ev_sinks"].reshape(-1)])
    oa, lse_a = _flash_fwd(qa, ka, va, scale=A_HEAD_DIM ** -0.5, name="swa_fwd", tq=256, window=WINDOW, slopes_sinks=ss)
    yq, xn_q = _rms_mm_fwd(c_q, w["ev_cq_norm"], w["ev_q_cat"], name="ev_q_up")
    cos256, sin256 = _rope_tables(s, 2 * B_HEADS)
    qro = _rope_fwd(yq[:, 512:768], yq[:, 768:1024], cos256, sin256, name="ev_q_rope")
    ykv, xn_kv = _rms_mm_fwd(c_kv, w["ev_ckv_norm"], w["ev_ukv"], name="ev_kv_up")
    zq = jnp.zeros((s, B_HEADS, LANES - B_NOPE - B_ROPE), F32)
    qb = jnp.concatenate([yq[:, :512].reshape(s, B_HEADS, B_NOPE), qro[:, :128].reshape(s, B_HEADS, half),
                          qro[:, 128:].reshape(s, B_HEADS, half), zq], axis=-1).transpose(1, 0, 2).astype(BF16)
    kv3 = ykv.reshape(s, B_HEADS, B_NOPE + B_V)
    kb = jnp.concatenate([kv3[:, :, :B_NOPE], jnp.broadcast_to(kro[:, None, :], (s, B_HEADS, B_ROPE)), zq],
                         axis=-1).transpose(1, 0, 2).astype(BF16)
    vb = kv3[:, :, B_NOPE:].transpose(1, 0, 2).astype(BF16)
    ob, lse_b = _flash_fwd(qb, kb, vb, scale=(B_NOPE + B_ROPE) ** -0.5, name="mla_fwd", tq=512)
    attn = jnp.concatenate([_from_heads(oa), _from_heads(ob)], axis=-1)
    out = _mm_res_fwd(attn, w["ev_out"], h, scale=1.0, name="ev_out")
    sv = dict(h_in=h, xn=xn, c_q=c_q, c_kv=c_kv, xn_q=xn_q, xn_kv=xn_kv, qa=qa, ka=ka, va=va, oa=oa, lse_a=lse_a,
              ss=ss, qb=qb, kb=kb, vb=vb, ob=ob, lse_b=lse_b, attn=attn, cos32=cos32, sin32=sin32,
              cos256=cos256, sin256=sin256)
    return out, sv


def _even_bwd(dh, w, sv, i):
    s = dh.shape[0]
    half = B_ROPE // 2
    g = {}
    dattn = _mm_nt(dh, w["ev_out"], name="ev_out_dx")
    g["ev_w_out"] = _mm_tn(sv["attn"], dh, name="ev_out_dw")[None]
    doa, dob = _to_heads(dattn[:, :512], A_HEADS, A_HEAD_DIM), _to_heads(dattn[:, 512:], B_HEADS, B_V)
    dqa, dka, dva, dsink = _flash_bwd(sv["qa"], sv["ka"], sv["va"], sv["oa"], doa, sv["lse_a"], scale=A_HEAD_DIM ** -0.5,
                                      name="swa_bwd", tq=256, window=WINDOW, slopes_sinks=sv["ss"])
    g["ev_sinks"] = dsink[:, :, 0, 0].reshape(1, A_HEADS)
    dqb, dkb, dvb = _flash_bwd(sv["qb"], sv["kb"], sv["vb"], sv["ob"], dob, sv["lse_b"],
                               scale=(B_NOPE + B_ROPE) ** -0.5, name="mla_bwd", tq=512)
    dqb = dqb.reshape(B_HEADS, s, LANES)
    dq_r1 = dqb[:, :, B_NOPE:B_NOPE + half].transpose(1, 0, 2).reshape(s, -1)
    dq_r2 = dqb[:, :, B_NOPE + half:B_NOPE + B_ROPE].transpose(1, 0, 2).reshape(s, -1)
    dq1, dq2 = _rope_bwd(jnp.concatenate([dq_r1, dq_r2], axis=-1)[None], sv["cos256"], sv["sin256"], name="ev_q_rope_bwd")
    dyq = jnp.concatenate([_from_heads(dqb[:, :, :B_NOPE]), dq1, dq2], axis=-1)
    dwq = _mm_tn(sv["xn_q"], dyq, name="ev_q_up_dw")
    dcq, g["ev_cq_norm"] = _mm_nt_rmsbwd([(dyq, w["ev_q_cat"])], sv["c_q"], w["ev_cq_norm"], None, name="ev_q_up_dx")
    kq = sv["c_q"].shape[1]
    d_nope = dwq[:, :512].reshape(kq, B_HEADS, B_NOPE)
    d_r1 = (dwq[:, 512:640] + dwq[:, 896:1024]).reshape(kq, B_HEADS, half)
    d_r2 = (dwq[:, 640:768] - dwq[:, 768:896]).reshape(kq, B_HEADS, half)
    g["ev_w_uq"] = jnp.concatenate([d_nope, d_r1, d_r2], axis=-1).reshape(1, kq, -1)
    dykv = jnp.concatenate([dkb[:, :, :B_NOPE].transpose(1, 0, 2), dvb.transpose(1, 0, 2)], axis=-1).reshape(s, -1)
    g["ev_w_ukv"] = _mm_tn(sv["xn_kv"], dykv, name="ev_kv_up_dw")[None]
    dckv, g["ev_ckv_norm"] = _mm_nt_rmsbwd([(dykv, w["ev_ukv"])], sv["c_kv"], w["ev_ckv_norm"], None, name="ev_kv_up_dx")
    dk1, dk2 = _rope_bwd(dkb[:, :, B_NOPE:B_NOPE + B_ROPE], sv["cos32"], sv["sin32"], name="ev_k_rope_bwd")
    dycat = jnp.concatenate([_from_heads(dqa.reshape(A_HEADS, s, A_HEAD_DIM)), _from_heads(dka), _from_heads(dva),
                             dcq, dckv, dk1, dk2, jnp.zeros((s, 64), F32)], axis=-1)
    dwin = _mm_tn(sv["xn"], dycat, name="ev_in_dw")
    base = 1184
    g["ev_w_in"] = jnp.concatenate([dwin[:, :base - B_ROPE],
                                    dwin[:, base - B_ROPE:base - half] + dwin[:, base + half:base + B_ROPE],
                                    dwin[:, base - half:base] - dwin[:, base:base + half]], axis=-1)[None]
    dh_in, dnorm = _mm_nt_rmsbwd([(dycat, w["ev_in_cat"])], sv["h_in"], w["mix_norm"][i:i + 1], dh, name="ev_in_dx")
    return dh_in, dnorm, g


def _odd_fwd(h, w, i):
    s = h.shape[0]
    wd = C_HEADS * C_HEAD_DIM
    y, xn = _rms_mm_fwd(h, w["mix_norm"][i:i + 1], w["od_in_pad"], name="od_in")
    def widen(a, one_at):
        pad = jnp.zeros((s, C_HEADS, LANES - C_HEAD_DIM), F32).at[:, :, one_at].set(1.0)
        return jnp.concatenate([a.reshape(s, C_HEADS, C_HEAD_DIM), pad], axis=-1).transpose(1, 0, 2).astype(BF16)

    q, k = widen(y[:, :wd], 0), widen(y[:, wd:2 * wd], 1)
    v = _to_heads(y[:, 2 * wd:3 * wd], C_HEADS, C_HEAD_DIM)
    ft = y[:, 3 * wd:3 * wd + C_HEADS].T
    bf = w["od_b_f"].reshape(C_HEADS, 1)
    logc = _fox_gate_fwd(ft, bf, name="fox_gate_fwd").reshape(C_HEADS, 1, s)
    o, lse = _flash_fwd(q, k, v, scale=C_HEAD_DIM ** -0.5, name="fox_fwd", tq=512, kbias=logc)
    attn = _from_heads(o)
    out = _mm_res_fwd(attn, w["od_out"], h, scale=1.0, name="od_out")
    return out, dict(h_in=h, xn=xn, q=q, k=k, v=v, o=o, lse=lse, logc=logc, ft=ft, bf=bf, attn=attn)


def _odd_bwd(dh, w, sv, i):
    s = dh.shape[0]
    g = {}
    dattn = _mm_nt(dh, w["od_out"], name="od_out_dx")
    g["od_w_out"] = _mm_tn(sv["attn"], dh, name="od_out_dw")[None]
    do = _to_heads(dattn, C_HEADS, C_HEAD_DIM)
    scale = C_HEAD_DIM ** -0.5
    dq, dk, dv = _flash_bwd(sv["q"], sv["k"], sv["v"], sv["o"], do, sv["lse"], scale=scale,
                            name="fox_bwd", tq=512, kbias=sv["logc"])
    dq = dq.reshape(C_HEADS, s, LANES)
    dft, dbf = _fox_gate_bwd(dq[:, :, C_HEAD_DIM + 1], dk[:, :, C_HEAD_DIM], sv["ft"], sv["bf"],
                             inv_scale=1.0 / scale, name="fox_gate_bwd")
    dq, dk = dq[:, :, :C_HEAD_DIM], dk[:, :, :C_HEAD_DIM]
    g["od_b_f"] = dbf.reshape(1, C_HEADS)
    n_pad = w["od_in_pad"].shape[1]
    n_real = 3 * C_HEADS * C_HEAD_DIM + C_HEADS
    dy = jnp.concatenate([_from_heads(dq), _from_heads(dk), _from_heads(dv), dft.T,
                          jnp.zeros((s, n_pad - n_real), F32)], axis=-1)
    g["od_w_in"] = _mm_tn(sv["xn"], dy, name="od_in_dw")[:, :n_real][None]
    dh_in, dnorm = _mm_nt_rmsbwd([(dy, w["od_in_pad"])], sv["h_in"], w["mix_norm"][i:i + 1], dh, name="od_in_dx")
    return dh_in, dnorm, g


def _kernel_weights(full, replicated):
    depth, f = full["ffa_w_down"].shape[:2]
    w = dict(replicated)
    for tag in ("ffa", "ffb"):
        w[tag] = [dict(wg=full[tag + "_w_gate_up"][i][:, :f], wu=full[tag + "_w_gate_up"][i][:, f:],
                       wd=full[tag + "_w_down"][i]) for i in range(depth)]
    w["ple_gate"], w["ple_proj"] = full["ple_w_gate"], full["ple_w_proj"]
    w["ev_in_cat"], w["ev_q_cat"], w["ev_ukv"] = _even_weights(full["ev_w_in"][0], full["ev_w_uq"][0], full["ev_w_ukv"][0])
    w["ev_out"], w["od_out"] = full["ev_w_out"][0], full["od_w_out"][0]
    od_in = full["od_w_in"][0]
    w["od_in_pad"] = jnp.pad(od_in, ((0, 0), (0, (-od_in.shape[1]) % LANES)))
    return w


def _local_step(x, p, tgt, w):
    depth = p.shape[0]
    h = x
    saved = []
    for i in range(depth):
        sv = {}
        h, sv["ffa"] = _ffn_fwd(h, w["ffa_norm"][i:i + 1], w["ffa"][i], f"ffa{i}")
        if i % 2 == 0:
            h, sv["mix"] = _even_fwd(h, w, i)
        else:
            h, sv["mix"] = _odd_fwd(h, w, i)
        h, sv["ffb"] = _ffn_fwd(h, w["ffb_norm"][i:i + 1], w["ffb"][i], f"ffb{i}")
        h_in = h
        h, xn, gate, pp = _ple_fwd(h, w["ple_norm"][i:i + 1], w["ple_gate"][i], p[i], w["ple_proj"][i], name=f"ple{i}")
        sv["ple"] = dict(h_in=h_in, xn=xn, gate=gate, pp=pp)
        saved.append(sv)
    loss_vec, dh, d_final = _final_loss(h, w["final_norm"].reshape(1, -1), tgt, name="final_loss")

    per_layer = [dict() for _ in range(depth)]
    grads = {}
    for i in reversed(range(depth)):
        sv, gl = saved[i], per_layer[i]
        dz, dpp = _ple_bwd_elem(dh, sv["ple"]["gate"], sv["ple"]["pp"], name=f"ple{i}_bwd")
        gl["ple_w_gate"] = _mm_tn(sv["ple"]["xn"], dz, name=f"ple{i}_dwg")
        gl["ple_w_proj"] = _mm_tn(p[i], dpp, name=f"ple{i}_dwp")
        dh, gl["ple_norm"] = _mm_nt_rmsbwd([(dz, w["ple_gate"][i])], sv["ple"]["h_in"], w["ple_norm"][i:i + 1], dh,
                                           name=f"ple{i}_dx")
        dh, gl["ffb_norm"], d_wg, d_wu, gl["ffb_w_down"] = _ffn_bwd(dh, w["ffb_norm"][i:i + 1], w["ffb"][i], sv["ffb"], f"ffb{i}")
        gl["ffb_w_gate_up"] = (d_wg, d_wu)
        if i % 2 == 0:
            dh, gl["mix_norm"], gm = _even_bwd(dh, w, sv["mix"], i)
        else:
            dh, gl["mix_norm"], gm = _odd_bwd(dh, w, sv["mix"], i)
        grads.update(gm)
        dh, gl["ffa_norm"], d_wg, d_wu, gl["ffa_w_down"] = _ffn_bwd(dh, w["ffa_norm"][i:i + 1], w["ffa"][i], sv["ffa"], f"ffa{i}")
        gl["ffa_w_gate_up"] = (d_wg, d_wu)
    grads["final_norm"] = d_final.reshape(-1)
    for n in ("ffa_norm", "mix_norm", "ffb_norm", "ple_norm"):
        grads[n] = jnp.concatenate([per_layer[i][n] for i in range(depth)], axis=0)
    for n in ("ffa_w_down", "ffb_w_down", "ple_w_gate", "ple_w_proj"):
        grads[n] = jnp.stack([per_layer[i][n] for i in range(depth)])
    for n in ("ffa_w_gate_up", "ffb_w_gate_up"):
        grads[n] = [per_layer[i][n] for i in range(depth)]
    return loss_vec[0, 0], dh, grads


def _grad_shard(name, axis, full, sidx):
    if name.endswith("w_gate_up"):
        pieces = []
        for d_wg, d_wu in full:
            f = d_wg.shape[1]
            src = d_wg if sidx < 2 else d_wu
            lo = (sidx % 2) * (f // 2)
            pieces.append(src[:, lo:lo + f // 2])
        return jnp.stack(pieces)
    n = full.shape[axis] // N_CHIPS
    return lax.slice_in_dim(full, sidx * n, (sidx + 1) * n, axis=axis)


def _small_rows(vals):
    rows = []
    for n in REPLICATED:
        v = vals[n].reshape(-1)
        rows.append(jnp.pad(v, (0, (-v.shape[0]) % FLAT_COLS)).reshape(-1, FLAT_COLS))
    return jnp.concatenate(rows, axis=0)


def kernel(x, p, ffa_norm, ffa_w_gate_up, ffa_w_down, mix_norm, ffb_norm, ffb_w_gate_up, ffb_w_down, ple_norm, ple_w_gate, ple_w_proj, ev_w_in, ev_sinks, ev_cq_norm, ev_w_uq, ev_ckv_norm, ev_w_ukv, ev_w_out, od_w_in, od_b_f, od_w_out, final_norm, loss_target, m_ffa_norm, m_ffa_w_gate_up, m_ffa_w_down, m_mix_norm, m_ffb_norm, m_ffb_w_gate_up, m_ffb_w_down, m_ple_norm, m_ple_w_gate, m_ple_w_proj, m_ev_w_in, m_ev_sinks, m_ev_cq_norm, m_ev_w_uq, m_ev_ckv_norm, m_ev_w_ukv, m_ev_w_out, m_od_w_in, m_od_b_f, m_od_w_out, m_final_norm, v_ffa_norm, v_ffa_w_gate_up, v_ffa_w_down, v_mix_norm, v_ffb_norm, v_ffb_w_gate_up, v_ffb_w_down, v_ple_norm, v_ple_w_gate, v_ple_w_proj, v_ev_w_in, v_ev_sinks, v_ev_cq_norm, v_ev_w_uq, v_ev_ckv_norm, v_ev_w_ukv, v_ev_w_out, v_od_w_in, v_od_b_f, v_od_w_out, v_final_norm):
    env = dict(locals())
    wts = {n: env[n] for n in WEIGHT_ORDER}
    mom1 = {n: env["m_" + n] for n in WEIGHT_ORDER}
    mom2 = {n: env["v_" + n] for n in WEIGHT_ORDER}
    cidx = lax.axis_index("c").astype(jnp.int32).reshape(1)

    seg_rows = [int(np.prod(wts[n].shape)) // FLAT_COLS for n, _ in SHARDED]
    n_rows = sum(seg_rows)
    n_rows_pad = -(-n_rows // 32) * 32
    flat_w = _pad_rows(jnp.concatenate([_rows(wts[n]) for n, _ in SHARDED], axis=0), n_rows_pad).astype(BF16)
    gathered = _allgather_halves(flat_w, name="weight_allgather").reshape(N_CHIPS, n_rows_pad, FLAT_COLS)
    full = {}
    r0 = 0
    for (n, axis), nr in zip(SHARDED, seg_rows):
        parts = gathered[:, r0:r0 + nr].reshape((N_CHIPS,) + wts[n].shape)
        full[n] = jnp.concatenate([parts[k] for k in range(N_CHIPS)], axis=axis)
        r0 += nr

    w = _kernel_weights(full, {n: wts[n] for n in REPLICATED})
    loss_part, grad_x, grads = _local_step(x[0], p[:, 0], loss_target[0], w)
    loss = lax.psum(loss_part, ("x", "y", "c"))

    small = _small_rows(grads)
    g_rows = -(-(n_rows + small.shape[0]) // GRAD_ROW_ALIGN) * GRAD_ROW_ALIGN
    blocks = []
    for k in range(N_CHIPS):
        segs = [_rows(_grad_shard(n, axis, grads[n], k)) for n, axis in SHARDED]
        blocks.append(_pad_rows(jnp.concatenate(segs + [small], axis=0), g_rows))
    reduced = _reduce_scatter(jnp.stack(blocks), cidx)

    gout = {}
    r0 = 0
    for (n, _), nr in zip(SHARDED, seg_rows):
        gout[n] = reduced[r0:r0 + nr].reshape(wts[n].shape)
        r0 += nr
    r0 = n_rows
    for n in REPLICATED:
        size = int(np.prod(wts[n].shape))
        nr = -(-size // FLAT_COLS)
        gout[n] = reduced[r0:r0 + nr].reshape(-1)[:size].reshape(wts[n].shape)
        r0 += nr
    delta, new_m, new_v = {}, {}, {}
    for n in WEIGHT_ORDER:
        delta[n], new_m[n], new_v[n] = _adamw(wts[n], gout[n], mom1[n], mom2[n], name="adamw_" + n)
    return (loss, grad_x[None], *[gout[n] for n in WEIGHT_ORDER], *[delta[n] for n in WEIGHT_ORDER],
            *[new_m[n] for n in WEIGHT_ORDER], *[new_v[n] for n in WEIGHT_ORDER])
```

```python
import functools
import math

import numpy as np
import jax
import jax.numpy as jnp
from jax import lax
from jax.experimental import pallas as pl
from jax.experimental.pallas import tpu as pltpu

F32 = jnp.float32
BF16 = jnp.bfloat16
NT = (((1,), (1,)), ((), ()))
TN = (((0,), (0,)), ((), ()))
MESH = pl.DeviceIdType.MESH

RMS_EPS = 1e-6
FFN_RES_SCALE = 0.5
A_HEADS, A_KV_HEADS, A_HEAD_DIM, WINDOW = 8, 2, 64, 128
B_HEADS, B_Q_LORA, B_KV_LORA, B_NOPE, B_ROPE, B_V = 8, 256, 128, 64, 32, 64
ROPE_THETA = 10000.0
C_HEADS, C_HEAD_DIM = 16, 64
ADAM_LR, ADAM_B1, ADAM_B2, ADAM_EPS, ADAM_WD, ADAM_STEP = 0.001, 0.9, 0.999, 1e-08, 0.01, 10

N_CHIPS = 4
LANES = 128
FLAT_COLS = 1024
GRAD_ROW_ALIGN = 512
MASK_VALUE = -1e30
VMEM_LIMIT = 48 * 2**20

SHARDED = (
    ("ffa_w_gate_up", 2), ("ffa_w_down", 1), ("ffb_w_gate_up", 2), ("ffb_w_down", 1),
    ("ple_w_gate", 1), ("ple_w_proj", 2), ("ev_w_in", 2), ("ev_w_uq", 2), ("ev_w_ukv", 2),
    ("ev_w_out", 1), ("od_w_in", 2), ("od_w_out", 1))
REPLICATED = ("ffa_norm", "mix_norm", "ffb_norm", "ple_norm", "final_norm",
              "ev_sinks", "ev_cq_norm", "ev_ckv_norm", "od_b_f")
WEIGHT_ORDER = ("ffa_norm", "ffa_w_gate_up", "ffa_w_down", "mix_norm", "ffb_norm", "ffb_w_gate_up",
                "ffb_w_down", "ple_norm", "ple_w_gate", "ple_w_proj", "ev_w_in", "ev_sinks",
                "ev_cq_norm", "ev_w_uq", "ev_ckv_norm", "ev_w_ukv", "ev_w_out", "od_w_in", "od_b_f",
                "od_w_out", "final_norm")


def _cp(*sem):
    return pltpu.CompilerParams(dimension_semantics=sem, vmem_limit_bytes=VMEM_LIMIT)


def _sigmoid(z):
    return 1.0 / (1.0 + jnp.exp(-z))


def _rms_stats(xv):
    r = lax.rsqrt(jnp.mean(xv * xv, axis=-1, keepdims=True) + RMS_EPS)
    return r, xv * r


def _rms_bwd(dxn, xv, g):
    r, xhat = _rms_stats(xv)
    u = dxn * g
    dx = r * (u - xhat * jnp.mean(u * xhat, axis=-1, keepdims=True))
    return dx, dxn * xhat


def _col_tile(k_rows, n, budget_bytes=6 * 2**20):
    if k_rows * n * 4 <= budget_bytes or n % LANES:
        return n
    units = n // LANES
    best = LANES
    for d in range(1, units + 1):
        if units % d == 0 and k_rows * d * LANES * 4 <= budget_bytes:
            best = d * LANES
    return best


def _row_tile(rows, cols, target_elems=2**18):
    if rows * cols <= target_elems or rows % 8:
        return rows
    best = 8
    for d in range(8, rows + 1, 8):
        if rows % d == 0 and d * cols <= target_elems:
            best = d
    return best


def _rms_mm_fwd(x, g, w, *, name, tm=512):
    s, k = x.shape
    n = w.shape[1]

    def body(x_ref, g_ref, w_ref, y_ref, xn_ref):
        _, xhat = _rms_stats(x_ref[...])
        xn = (xhat * g_ref[...]).astype(BF16)
        xn_ref[...] = xn
        y_ref[...] = jnp.dot(xn, w_ref[...], preferred_element_type=F32)

    return pl.pallas_call(
        body, name=name, grid=(s // tm,),
        in_specs=[pl.BlockSpec((tm, k), lambda i: (i, 0)), pl.BlockSpec((1, k), lambda i: (0, 0)),
                  pl.BlockSpec((k, n), lambda i: (0, 0))],
        out_specs=[pl.BlockSpec((tm, n), lambda i: (i, 0)), pl.BlockSpec((tm, k), lambda i: (i, 0))],
        out_shape=[jax.ShapeDtypeStruct((s, n), F32), jax.ShapeDtypeStruct((s, k), BF16)],
        compiler_params=_cp("arbitrary"))(x, g, w)


def _ffn_up(x, g, wg, wu, *, name, tm=512):
    s, k = x.shape
    f = wg.shape[1]
    tn = _col_tile(k, f)
    nj = f // tn

    def body(x_ref, g_ref, wg_ref, wu_ref, gate_ref, up_ref, act_ref, xn_ref, xn_sc):
        @pl.when(pl.program_id(1) == 0)
        def _():
            _, xhat = _rms_stats(x_ref[...])
            xn = (xhat * g_ref[...]).astype(BF16)
            xn_sc[...] = xn
            xn_ref[...] = xn

        xn = xn_sc[...]
        gg = jnp.dot(xn, wg_ref[...], preferred_element_type=F32)
        uu = jnp.dot(xn, wu_ref[...], preferred_element_type=F32)
        gate_ref[...] = gg.astype(BF16)
        up_ref[...] = uu.astype(BF16)
        act_ref[...] = ((gg * _sigmoid(gg)) * uu).astype(BF16)

    tile = pl.BlockSpec((tm, tn), lambda i, j: (i, j))
    return pl.pallas_call(
        body, name=name, grid=(s // tm, nj),
        in_specs=[pl.BlockSpec((tm, k), lambda i, j: (i, 0)), pl.BlockSpec((1, k), lambda i, j: (0, 0)),
                  pl.BlockSpec((k, tn), lambda i, j: (0, j)), pl.BlockSpec((k, tn), lambda i, j: (0, j))],
        out_specs=[tile, tile, tile, pl.BlockSpec((tm, k), lambda i, j: (i, 0))],
        out_shape=[jax.ShapeDtypeStruct((s, f), BF16)] * 3 + [jax.ShapeDtypeStruct((s, k), BF16)],
        scratch_shapes=[pltpu.VMEM((tm, k), BF16)],
        compiler_params=_cp("arbitrary", "arbitrary"))(x, g, wg, wu)


def _mm_res_fwd(a, w, res, *, scale, name, tm=512):
    s, k = a.shape
    n = w.shape[1]

    def body(a_ref, w_ref, r_ref, o_ref):
        o_ref[...] = r_ref[...] + scale * jnp.dot(a_ref[...], w_ref[...], preferred_element_type=F32)

    return pl.pallas_call(
        body, name=name, grid=(s // tm,),
        in_specs=[pl.BlockSpec((tm, k), lambda i: (i, 0)), pl.BlockSpec((k, n), lambda i: (0, 0)),
                  pl.BlockSpec((tm, n), lambda i: (i, 0))],
        out_specs=pl.BlockSpec((tm, n), lambda i: (i, 0)),
        out_shape=jax.ShapeDtypeStruct((s, n), F32),
        compiler_params=_cp("arbitrary"))(a, w, res)


def _ffn_down_bwd(dh, wd, gate, up, *, scale, name, tm=512):
    s, d = dh.shape
    f = wd.shape[0]
    tn = _col_tile(d, f)

    def body(dh_ref, wd_ref, gate_ref, up_ref, dg_ref, du_ref):
        dhb = (dh_ref[...] * scale).astype(BF16)
        da = lax.dot_general(dhb, wd_ref[...], NT, preferred_element_type=F32)
        gg = gate_ref[...].astype(F32)
        uu = up_ref[...].astype(F32)
        sg = _sigmoid(gg)
        dg_ref[...] = (da * uu * (sg * (1.0 + gg * (1.0 - sg)))).astype(BF16)
        du_ref[...] = (da * (gg * sg)).astype(BF16)

    tile = pl.BlockSpec((tm, tn), lambda i, j: (i, j))
    return pl.pallas_call(
        body, name=name, grid=(s // tm, f // tn),
        in_specs=[pl.BlockSpec((tm, d), lambda i, j: (i, 0)), pl.BlockSpec((tn, d), lambda i, j: (j, 0)), tile, tile],
        out_specs=[tile, tile],
        out_shape=[jax.ShapeDtypeStruct((s, f), BF16)] * 2,
        compiler_params=_cp("arbitrary", "arbitrary"))(dh, wd, gate, up)


def _mm_tn(a, b, *, name, b_scale=1.0, ts=512):
    s, k = a.shape
    n = b.shape[1]
    tn = _col_tile(k, n)

    def body(a_ref, b_ref, o_ref):
        @pl.when(pl.program_id(1) == 0)
        def _():
            o_ref[...] = jnp.zeros_like(o_ref)

        bv = b_ref[...]
        if b_scale != 1.0:
            bv = bv * b_scale
        o_ref[...] += lax.dot_general(a_ref[...].astype(BF16), bv.astype(BF16), TN, preferred_element_type=F32)

    return pl.pallas_call(
        body, name=name, grid=(n // tn, s // ts),
        in_specs=[pl.BlockSpec((ts, k), lambda j, t: (t, 0)), pl.BlockSpec((ts, tn), lambda j, t: (t, j))],
        out_specs=pl.BlockSpec((k, tn), lambda j, t: (0, j)),
        out_shape=jax.ShapeDtypeStruct((k, n), F32),
        compiler_params=_cp("arbitrary", "arbitrary"))(a, b)


def _mm_nt(dy, w, *, name, tm=512):
    s, n = dy.shape
    k = w.shape[0]

    def body(dy_ref, w_ref, o_ref):
        o_ref[...] = lax.dot_general(dy_ref[...].astype(BF16), w_ref[...], NT, preferred_element_type=F32)

    return pl.pallas_call(
        body, name=name, grid=(s // tm,),
        in_specs=[pl.BlockSpec((tm, n), lambda i: (i, 0)), pl.BlockSpec((k, n), lambda i: (0, 0))],
        out_specs=pl.BlockSpec((tm, k), lambda i: (i, 0)),
        out_shape=jax.ShapeDtypeStruct((s, k), F32),
        compiler_params=_cp("arbitrary"))(dy, w)


def _mm_nt_rmsbwd(pairs, x, g, dres, *, name, tm=256):
    s, k = x.shape
    npairs = len(pairs)

    def body(*refs):
        dy_refs = refs[0:2 * npairs:2]
        w_refs = refs[1:2 * npairs:2]
        rest = refs[2 * npairs:]
        x_ref, g_ref = rest[0], rest[1]
        if dres is None:
            dx_ref, dg_ref = rest[2], rest[3]
        else:
            dres_ref, dx_ref, dg_ref = rest[2], rest[3], rest[4]
        dxn = None
        for dy_ref, w_ref in zip(dy_refs, w_refs):
            t = lax.dot_general(dy_ref[...].astype(BF16), w_ref[...], NT, preferred_element_type=F32)
            dxn = t if dxn is None else dxn + t
        dx, dgrow = _rms_bwd(dxn, x_ref[...], g_ref[...])
        if dres is not None:
            dx = dx + dres_ref[...]
        dx_ref[...] = dx

        @pl.when(pl.program_id(0) == 0)
        def _():
            dg_ref[...] = jnp.zeros_like(dg_ref)

        dg_ref[...] += jnp.sum(dgrow, axis=0, keepdims=True)

    in_specs, args = [], []
    for dy, w in pairs:
        n = dy.shape[1]
        in_specs += [pl.BlockSpec((tm, n), lambda i: (i, 0)), pl.BlockSpec((k, n), lambda i: (0, 0))]
        args += [dy, w]
    row = pl.BlockSpec((tm, k), lambda i: (i, 0))
    vec = pl.BlockSpec((1, k), lambda i: (0, 0))
    in_specs += [row, vec]
    args += [x, g]
    if dres is not None:
        in_specs.append(row)
        args.append(dres)
    return pl.pallas_call(
        body, name=name, grid=(s // tm,), in_specs=in_specs, out_specs=[row, vec],
        out_shape=[jax.ShapeDtypeStruct((s, k), F32), jax.ShapeDtypeStruct((1, k), F32)],
        compiler_params=_cp("arbitrary"))(*args)


def _ple_fwd(h, g, wg, p, wp, *, name, tm=512):
    s, d = h.shape
    pd = p.shape[1]

    def body(h_ref, g_ref, wg_ref, p_ref, wp_ref, o_ref, xn_ref, gate_ref, pp_ref):
        hv = h_ref[...]
        _, xhat = _rms_stats(hv)
        xn = (xhat * g_ref[...]).astype(BF16)
        xn_ref[...] = xn
        gate = _sigmoid(jnp.dot(xn, wg_ref[...], preferred_element_type=F32))
        pp = jnp.dot(p_ref[...].astype(BF16), wp_ref[...], preferred_element_type=F32)
        gate_ref[...] = gate.astype(BF16)
        pp_ref[...] = pp.astype(BF16)
        o_ref[...] = hv + gate * pp

    row = pl.BlockSpec((tm, d), lambda i: (i, 0))
    return pl.pallas_call(
        body, name=name, grid=(s // tm,),
        in_specs=[row, pl.BlockSpec((1, d), lambda i: (0, 0)), pl.BlockSpec((d, d), lambda i: (0, 0)),
                  pl.BlockSpec((tm, pd), lambda i: (i, 0)), pl.BlockSpec((pd, d), lambda i: (0, 0))],
        out_specs=[row, row, row, row],
        out_shape=[jax.ShapeDtypeStruct((s, d), F32)] + [jax.ShapeDtypeStruct((s, d), BF16)] * 3,
        compiler_params=_cp("arbitrary"))(h, g, wg, p, wp)


def _ple_bwd_elem(dh, gate, pp, *, name, tm=512):
    s, d = dh.shape

    def body(dh_ref, gate_ref, pp_ref, dz_ref, dpp_ref):
        dhv = dh_ref[...]
        gt = gate_ref[...].astype(F32)
        dz_ref[...] = (dhv * pp_ref[...].astype(F32) * (gt * (1.0 - gt))).astype(BF16)
        dpp_ref[...] = (dhv * gt).astype(BF16)

    row = pl.BlockSpec((tm, d), lambda i: (i, 0))
    return pl.pallas_call(
        body, name=name, grid=(s // tm,), in_specs=[row, row, row], out_specs=[row, row],
        out_shape=[jax.ShapeDtypeStruct((s, d), BF16)] * 2,
        compiler_params=_cp("arbitrary"))(dh, gate, pp)


def _final_loss(h, g, tgt, *, name, tm=512):
    s, d = h.shape

    def body(h_ref, g_ref, t_ref, loss_ref, dh_ref, dg_ref):
        @pl.when(pl.program_id(0) == 0)
        def _():
            loss_ref[...] = jnp.zeros_like(loss_ref)
            dg_ref[...] = jnp.zeros_like(dg_ref)

        hv = h_ref[...]
        gv = g_ref[...]
        _, xhat = _rms_stats(hv)
        err = xhat * gv - t_ref[...]
        per_row = jnp.mean(err * err, axis=-1, keepdims=True)
        loss_ref[...] += 0.5 * jnp.sum(per_row, axis=0, keepdims=True)
        dx, dgrow = _rms_bwd(err * (1.0 / d), hv, gv)
        dh_ref[...] = dx
        dg_ref[...] += jnp.sum(dgrow, axis=0, keepdims=True)

    row = pl.BlockSpec((tm, d), lambda i: (i, 0))
    vec = pl.BlockSpec((1, d), lambda i: (0, 0))
    return pl.pallas_call(
        body, name=name, grid=(s // tm,), in_specs=[row, vec, row],
        out_specs=[pl.BlockSpec((1, LANES), lambda i: (0, 0)), row, vec],
        out_shape=[jax.ShapeDtypeStruct((1, LANES), F32), jax.ShapeDtypeStruct((s, d), F32),
                   jax.ShapeDtypeStruct((1, d), F32)],
        compiler_params=_cp("arbitrary"))(h, g, tgt)


def _rope_fwd(y1, y2, cos, sin, *, name, tm=512):
    s, r = y1.shape

    def body(a_ref, b_ref, c_ref, s_ref, o_ref):
        o_ref[...] = a_ref[...] * c_ref[...] + b_ref[...] * s_ref[...]

    row = pl.BlockSpec((tm, r), lambda i: (i, 0))
    return pl.pallas_call(
        body, name=name, grid=(s // tm,), in_specs=[row] * 4, out_specs=row,
        out_shape=jax.ShapeDtypeStruct((s, r), F32), compiler_params=_cp("arbitrary"))(y1, y2, cos, sin)


def _rope_bwd(dout, cos, sin, *, name, tm=512):
    nh, s, r = dout.shape

    def body(d_ref, c_ref, s_ref, o1_ref, o2_ref):
        tot = d_ref[0]
        for hh in range(1, nh):
            tot = tot + d_ref[hh]
        o1_ref[...] = tot * c_ref[...]
        o2_ref[...] = tot * s_ref[...]

    row = pl.BlockSpec((tm, r), lambda i: (i, 0))
    return pl.pallas_call(
        body, name=name, grid=(s // tm,),
        in_specs=[pl.BlockSpec((nh, tm, r), lambda i: (0, i, 0)), row, row], out_specs=[row, row],
        out_shape=[jax.ShapeDtypeStruct((s, r), F32)] * 2, compiler_params=_cp("arbitrary"))(dout, cos, sin)


def _split3(v):
    h1 = v.astype(BF16)
    r1 = v - h1.astype(F32)
    h2 = r1.astype(BF16)
    h3 = (r1 - h2.astype(F32)).astype(BF16)
    return h1, h2, h3


def _tri(tb, upper):
    r = lax.broadcasted_iota(jnp.int32, (tb, tb), 0)
    c = lax.broadcasted_iota(jnp.int32, (tb, tb), 1)
    return jnp.where((r <= c) if upper else (r >= c), 1.0, 0.0).astype(BF16)


def _fox_gate_fwd(ft, bf, *, name, tb=512):
    nh, s = ft.shape

    def body(f_ref, b_ref, o_ref, carry):
        @pl.when(pl.program_id(0) == 0)
        def _():
            carry[...] = jnp.zeros_like(carry)

        z = f_ref[...] + b_ref[...]
        lf = jnp.minimum(z, 0.0) - jnp.log(1.0 + jnp.exp(-jnp.abs(z)))
        tri = _tri(tb, True)
        cs = sum(jnp.dot(t, tri, preferred_element_type=F32) for t in _split3(lf))
        o_ref[...] = cs + carry[...]
        carry[...] += jnp.sum(lf, axis=-1, keepdims=True)

    return pl.pallas_call(
        body, name=name, grid=(s // tb,),
        in_specs=[pl.BlockSpec((nh, tb), lambda t: (0, t)), pl.BlockSpec((nh, 1), lambda t: (0, 0))],
        out_specs=pl.BlockSpec((nh, tb), lambda t: (0, t)),
        out_shape=jax.ShapeDtypeStruct((nh, s), F32),
        scratch_shapes=[pltpu.VMEM((nh, 1), F32)], compiler_params=_cp("arbitrary"))(ft, bf)


def _fox_gate_bwd(drow, dcol, ft, bf, *, inv_scale, name, tb=512):
    nh, s = ft.shape
    nb = s // tb

    def body(dr_ref, dc_ref, f_ref, b_ref, df_ref, db_ref, carry):
        @pl.when(pl.program_id(0) == 0)
        def _():
            carry[...] = jnp.zeros_like(carry)
            db_ref[...] = jnp.zeros_like(db_ref)

        dc = (dr_ref[...] - dc_ref[...]) * inv_scale
        tri = _tri(tb, False)
        suf = sum(jnp.dot(t, tri, preferred_element_type=F32) for t in _split3(dc)) + carry[...]
        z = f_ref[...] + b_ref[...]
        dz = suf * (1.0 / (1.0 + jnp.exp(z)))
        df_ref[...] = dz
        db_ref[...] += jnp.sum(dz, axis=-1, keepdims=True)
        carry[...] += jnp.sum(dc, axis=-1, keepdims=True)

    rev = pl.BlockSpec((nh, tb), lambda t: (0, nb - 1 - t))
    one = pl.BlockSpec((nh, 1), lambda t: (0, 0))
    return pl.pallas_call(
        body, name=name, grid=(nb,), in_specs=[rev, rev, rev, one], out_specs=[rev, one],
        out_shape=[jax.ShapeDtypeStruct((nh, s), F32), jax.ShapeDtypeStruct((nh, 1), F32)],
        scratch_shapes=[pltpu.VMEM((nh, 1), F32)], compiler_params=_cp("arbitrary"))(drow, dcol, ft, bf)


def _scores(q, k, i, jblk, *, scale, tq, tk, window, slope, kb):
    s = lax.dot_general(q, k, NT, preferred_element_type=F32) * scale
    qpos = i * tq + lax.broadcasted_iota(jnp.int32, (tq, tk), 0)
    kpos = jblk * tk + lax.broadcasted_iota(jnp.int32, (tq, tk), 1)
    dist = qpos - kpos
    if slope is not None:
        s = s - slope * dist.astype(F32)
    if kb is not None:
        s = s - kb
    ok = dist >= 0
    if window is not None:
        ok = jnp.logical_and(ok, dist < window)
    return jnp.where(ok, s, MASK_VALUE)


def _flash_fwd(q, k, v, *, scale, name, tq, window=None, slopes_sinks=None, kbias=None):
    nh, s, dq = q.shape
    nkv, _, dv = v.shape
    grp = nh // nkv
    tk = tq
    nq = s // tq
    nj = nq if window is None else 2
    assert window is None or window <= tk
    has_ss, has_kb = slopes_sinks is not None, kbias is not None

    def blk(i, jj):
        return jj if window is None else i - jj

    def body(*refs):
        q_ref, k_ref, v_ref = refs[:3]
        pos = 3
        kb_ref = ss_ref = None
        if has_kb:
            kb_ref = refs[pos]
            pos += 1
        if has_ss:
            ss_ref = refs[pos]
            pos += 1
        o_ref, lse_ref, m_sc, l_sc, acc_sc = refs[pos:pos + 5]
        h, i, jj = pl.program_id(0), pl.program_id(1), pl.program_id(2)
        jblk = blk(i, jj)
        valid = (jj <= i) if window is None else (jblk >= 0)

        @pl.when(jj == 0)
        def _():
            m_sc[...] = jnp.full_like(m_sc, MASK_VALUE)
            l_sc[...] = jnp.zeros_like(l_sc)
            acc_sc[...] = jnp.zeros_like(acc_sc)

        @pl.when(valid)
        def _():
            sc = _scores(q_ref[...], k_ref[...], i, jblk, scale=scale, tq=tq, tk=tk, window=window,
                         slope=ss_ref[0, h] if has_ss else None, kb=kb_ref[...] if has_kb else None)
            m_prev = m_sc[...]
            m_new = jnp.maximum(m_prev, jnp.max(sc, axis=-1, keepdims=True))
            alpha = jnp.exp(m_prev - m_new)
            pr = jnp.exp(sc - m_new)
            l_sc[...] = alpha * l_sc[...] + jnp.sum(pr, axis=-1, keepdims=True)
            acc_sc[...] = alpha * acc_sc[...] + jnp.dot(pr.astype(BF16), v_ref[...], preferred_element_type=F32)
            m_sc[...] = m_new

        @pl.when(jj == nj - 1)
        def _():
            m = m_sc[...]
            l = l_sc[...]
            acc = acc_sc[...]
            if has_ss:
                sink = ss_ref[1, h]
                m_f = jnp.maximum(m, sink)
                corr = jnp.exp(m - m_f)
                l = l * corr + jnp.exp(sink - m_f)
                acc = acc * corr
                m = m_f
            o_ref[...] = (acc / l).astype(BF16)
            lse_ref[...] = jnp.broadcast_to(m + jnp.log(l), (tq, LANES))

    def kv_idx(h, i, jj):
        j = jnp.minimum(jj, i) if window is None else jnp.maximum(i - jj, 0)
        return (h // grp, j, 0)

    in_specs = [pl.BlockSpec((None, tq, dq), lambda h, i, jj: (h, i, 0)),
                pl.BlockSpec((None, tk, dq), kv_idx), pl.BlockSpec((None, tk, dv), kv_idx)]
    args = [q, k, v]
    if has_kb:
        in_specs.append(pl.BlockSpec((None, 1, tk), lambda h, i, jj: (h, 0, kv_idx(h, i, jj)[1])))
        args.append(kbias)
    if has_ss:
        in_specs.append(pl.BlockSpec(memory_space=pltpu.SMEM))
        args.append(slopes_sinks)
    return pl.pallas_call(
        body, name=name, grid=(nh, nq, nj), in_specs=in_specs,
        out_specs=[pl.BlockSpec((None, tq, dv), lambda h, i, jj: (h, i, 0)),
                   pl.BlockSpec((None, tq, LANES), lambda h, i, jj: (h, i, 0))],
        out_shape=[jax.ShapeDtypeStruct((nh, s, dv), BF16), jax.ShapeDtypeStruct((nh, s, LANES), F32)],
        scratch_shapes=[pltpu.VMEM((tq, 1), F32), pltpu.VMEM((tq, 1), F32), pltpu.VMEM((tq, dv), F32)],
        compiler_params=_cp("arbitrary", "arbitrary", "arbitrary"))(*args)


def _flash_bwd(q, k, v, o, do, lse, *, scale, name, tq, window=None, slopes_sinks=None, kbias=None):
    nh, s, dq = q.shape
    nkv, _, dv = v.shape
    grp = nh // nkv
    tk = tq
    nq = s // tq
    nsteps = nq if window is None else 2
    has_ss, has_kb = slopes_sinks is not None, kbias is not None
    assert not (has_ss and window is None)

    def body(*refs):
        q_ref, k_ref, v_ref, o_ref, do_ref, lse_ref = refs[:6]
        pos = 6
        kb_ref = ss_ref = dsink_ref = None
        if has_kb:
            kb_ref = refs[pos]
            pos += 1
        if has_ss:
            ss_ref = refs[pos]
            pos += 1
        dq_ref, dk_ref, dv_ref = refs[pos:pos + 3]
        pos += 3
        if has_ss:
            dsink_ref = refs[pos]
            pos += 1
        hk, j, g, ii = (pl.program_id(a) for a in range(4))
        i = ii if window is None else j + ii
        valid = (i >= j) if window is None else (i < nq)

        @pl.when(jnp.logical_and(j == 0, jnp.logical_and(g == 0, ii == 0)))
        def _():
            dq_ref[...] = jnp.zeros_like(dq_ref)
            if has_ss:
                dsink_ref[...] = jnp.zeros_like(dsink_ref)

        @pl.when(jnp.logical_and(g == 0, ii == 0))
        def _():
            dk_ref[...] = jnp.zeros_like(dk_ref)
            dv_ref[...] = jnp.zeros_like(dv_ref)

        @pl.when(valid)
        def _():
            h = hk * grp + g
            qv, kv, dov = q_ref[...], k_ref[...], do_ref[...]
            sc = _scores(qv, kv, i, j, scale=scale, tq=tq, tk=tk, window=window,
                         slope=ss_ref[0, h] if has_ss else None, kb=kb_ref[...] if has_kb else None)
            lse_col = lse_ref[...][:, :1]
            pr = jnp.exp(sc - lse_col)
            dp = lax.dot_general(dov, v_ref[...], NT, preferred_element_type=F32)
            delta = jnp.sum(dov.astype(F32) * o_ref[...].astype(F32), axis=-1, keepdims=True)
            ds = pr * (dp - delta)
            dv_ref[...] += lax.dot_general(pr.astype(BF16), dov, TN, preferred_element_type=F32)
            dsb = (ds * scale).astype(BF16)
            dk_ref[...] += lax.dot_general(dsb, qv, TN, preferred_element_type=F32)
            rows = pl.ds(pl.multiple_of(i * tq, tq), tq)
            dq_ref[g, rows, :] += jnp.dot(dsb, kv, preferred_element_type=F32)
            if has_ss:
                @pl.when(ii == 0)
                def _():
                    psink = jnp.exp(ss_ref[1, h] - lse_col)
                    tot = jnp.sum(psink * delta, axis=0, keepdims=True)
                    dsink_ref[g] -= jnp.broadcast_to(tot, (1, LANES))

    def q_idx(hk, j, g, ii):
        i = jnp.maximum(ii, j) if window is None else jnp.minimum(j + ii, nq - 1)
        return (hk * grp + g, i, 0)

    def kv_idx(hk, j, g, ii):
        return (hk, j, 0)

    in_specs = [pl.BlockSpec((None, tq, dq), q_idx), pl.BlockSpec((None, tk, dq), kv_idx),
                pl.BlockSpec((None, tk, dv), kv_idx), pl.BlockSpec((None, tq, dv), q_idx),
                pl.BlockSpec((None, tq, dv), q_idx), pl.BlockSpec((None, tq, LANES), q_idx)]
    args = [q, k, v, o, do, lse]
    out_specs = [pl.BlockSpec((None, grp, s, dq), lambda hk, j, g, ii: (hk, 0, 0, 0)),
                 pl.BlockSpec((None, tk, dq), kv_idx), pl.BlockSpec((None, tk, dv), kv_idx)]
    out_shape = [jax.ShapeDtypeStruct((nkv, grp, s, dq), F32), jax.ShapeDtypeStruct((nkv, s, dq), F32),
                 jax.ShapeDtypeStruct((nkv, s, dv), F32)]
    if has_kb:
        in_specs.append(pl.BlockSpec((None, 1, tk), lambda hk, j, g, ii: (hk, 0, j)))
        args.append(kbias)
    if has_ss:
        in_specs.append(pl.BlockSpec(memory_space=pltpu.SMEM))
        args.append(slopes_sinks)
        out_specs.append(pl.BlockSpec((None, grp, 1, LANES), lambda hk, j, g, ii: (hk, 0, 0, 0)))
        out_shape.append(jax.ShapeDtypeStruct((nkv, grp, 1, LANES), F32))
    return pl.pallas_call(
        body, name=name, grid=(nkv, nq, grp, nsteps), in_specs=in_specs, out_specs=out_specs,
        out_shape=out_shape, compiler_params=_cp(*["arbitrary"] * 4))(*args)


def _tri_fwd(t, nq):
    i = sum((t >= (r * (r + 1)) // 2).astype(jnp.int32) for r in range(1, nq))
    return i, t - (i * (i + 1)) // 2


def _tri_bwd(t, nq):
    j = sum((t >= r * nq - (r * (r - 1)) // 2).astype(jnp.int32) for r in range(1, nq))
    return j, j + t - (j * nq - (j * (j - 1)) // 2)


def _causal_scores(q, k, kb, *, scale, diag):
    s = lax.dot_general(q, k, NT, preferred_element_type=F32) * scale
    if kb is not None:
        s = s - kb
    if diag:
        r = lax.broadcasted_iota(jnp.int32, s.shape, 0)
        c = lax.broadcasted_iota(jnp.int32, s.shape, 1)
        s = jnp.where(r >= c, s, MASK_VALUE)
    return s


def _causal_fwd(q, k, v, *, scale, name, tq, hb=2, kbias=None):
    nh, s, dq = q.shape
    dv = v.shape[-1]
    nq = s // tq
    nsteps = (nq * (nq + 1)) // 2
    has_kb = kbias is not None

    def body(*refs):
        q_ref, k_ref, v_ref = refs[:3]
        kb_ref = refs[3] if has_kb else None
        o_ref, lse_ref, m_sc, l_sc, acc_sc = refs[3 + has_kb:]
        i, j = _tri_fwd(pl.program_id(1), nq)

        @pl.when(j == 0)
        def _():
            m_sc[...] = jnp.full_like(m_sc, MASK_VALUE)
            l_sc[...] = jnp.zeros_like(l_sc)
            acc_sc[...] = jnp.zeros_like(acc_sc)

        def step(diag):
            for u in range(hb):
                sc = _causal_scores(q_ref[u], k_ref[u], kb_ref[u] if has_kb else None, scale=scale, diag=diag)
                m_prev = m_sc[u]
                m_new = jnp.maximum(m_prev, jnp.max(sc, axis=-1, keepdims=True))
                alpha = jnp.exp(m_prev - m_new)
                pr = jnp.exp(sc - m_new)
                l_new = alpha * l_sc[u] + jnp.sum(pr, axis=-1, keepdims=True)
                acc = alpha * acc_sc[u] + jnp.dot(pr.astype(BF16), v_ref[u], preferred_element_type=F32)
                if diag:
                    o_ref[u] = (acc / l_new).astype(BF16)
                    lse_ref[u] = jnp.broadcast_to(m_new + jnp.log(l_new), (tq, LANES))
                else:
                    m_sc[u], l_sc[u], acc_sc[u] = m_new, l_new, acc

        pl.when(j < i)(functools.partial(step, False))
        pl.when(j == i)(functools.partial(step, True))

    def q_idx(hp, t):
        return (hp, _tri_fwd(t, nq)[0], 0)

    def kv_idx(hp, t):
        return (hp, _tri_fwd(t, nq)[1], 0)

    in_specs = [pl.BlockSpec((hb, tq, dq), q_idx), pl.BlockSpec((hb, tq, dq), kv_idx), pl.BlockSpec((hb, tq, dv), kv_idx)]
    args = [q, k, v]
    if has_kb:
        in_specs.append(pl.BlockSpec((hb, 1, tq), lambda hp, t: (hp, 0, _tri_fwd(t, nq)[1])))
        args.append(kbias)
    return pl.pallas_call(
        body, name=name, grid=(nh // hb, nsteps), in_specs=in_specs,
        out_specs=[pl.BlockSpec((hb, tq, dv), q_idx), pl.BlockSpec((hb, tq, LANES), q_idx)],
        out_shape=[jax.ShapeDtypeStruct((nh, s, dv), BF16), jax.ShapeDtypeStruct((nh, s, LANES), F32)],
        scratch_shapes=[pltpu.VMEM((hb, tq, 1), F32), pltpu.VMEM((hb, tq, 1), F32), pltpu.VMEM((hb, tq, dv), F32)],
        compiler_params=_cp("arbitrary", "arbitrary"))(*args)


def _causal_bwd(q, k, v, o, do, lse, *, scale, name, tq, hb=2, kbias=None):
    nh, s, dq = q.shape
    dv = v.shape[-1]
    nq = s // tq
    nsteps = (nq * (nq + 1)) // 2
    has_kb = kbias is not None

    def body(*refs):
        q_ref, k_ref, v_ref, o_ref, do_ref, lse_ref = refs[:6]
        kb_ref = refs[6] if has_kb else None
        dq_ref, dk_ref, dv_ref = refs[6 + has_kb:]
        t = pl.program_id(1)
        j, i = _tri_bwd(t, nq)

        @pl.when(t == 0)
        def _():
            dq_ref[...] = jnp.zeros_like(dq_ref)

        def step(diag):
            rows = pl.ds(pl.multiple_of(i * tq, tq), tq)
            for u in range(hb):
                qv, kv, dov = q_ref[u], k_ref[u], do_ref[u]
                sc = _causal_scores(qv, kv, kb_ref[u] if has_kb else None, scale=scale, diag=diag)
                pr = jnp.exp(sc - lse_ref[u][:, :1])
                dp = lax.dot_general(dov, v_ref[u], NT, preferred_element_type=F32)
                delta = jnp.sum(dov.astype(F32) * o_ref[u].astype(F32), axis=-1, keepdims=True)
                dsb = ((pr * (dp - delta)) * scale).astype(BF16)
                d_v = lax.dot_general(pr.astype(BF16), dov, TN, preferred_element_type=F32)
                d_k = lax.dot_general(dsb, qv, TN, preferred_element_type=F32)
                if diag:
                    dv_ref[u], dk_ref[u] = d_v, d_k
                else:
                    dv_ref[u] += d_v
                    dk_ref[u] += d_k
                dq_ref[u, rows, :] += jnp.dot(dsb, kv, preferred_element_type=F32)

        pl.when(i > j)(functools.partial(step, False))
        pl.when(i == j)(functools.partial(step, True))

    def q_idx(hp, t):
        return (hp, _tri_bwd(t, nq)[1], 0)

    def kv_idx(hp, t):
        return (hp, _tri_bwd(t, nq)[0], 0)

    in_specs = [pl.BlockSpec((hb, tq, dq), q_idx), pl.BlockSpec((hb, tq, dq), kv_idx), pl.BlockSpec((hb, tq, dv), kv_idx),
                pl.BlockSpec((hb, tq, dv), q_idx), pl.BlockSpec((hb, tq, dv), q_idx), pl.BlockSpec((hb, tq, LANES), q_idx)]
    args = [q, k, v, o, do, lse]
    if has_kb:
        in_specs.append(pl.BlockSpec((hb, 1, tq), lambda hp, t: (hp, 0, _tri_bwd(t, nq)[0])))
        args.append(kbias)
    return pl.pallas_call(
        body, name=name, grid=(nh // hb, nsteps), in_specs=in_specs,
        out_specs=[pl.BlockSpec((hb, s, dq), lambda hp, t: (hp, 0, 0)), pl.BlockSpec((hb, tq, dq), kv_idx),
                   pl.BlockSpec((hb, tq, dv), kv_idx)],
        out_shape=[jax.ShapeDtypeStruct((nh, s, dq), F32), jax.ShapeDtypeStruct((nh, s, dq), F32),
                   jax.ShapeDtypeStruct((nh, s, dv), F32)],
        compiler_params=_cp("arbitrary", "arbitrary"))(*args)


def _adamw(w, g, m, v, *, name):
    shape = w.shape
    cols = shape[-1]
    rows = int(np.prod(shape[:-1])) if len(shape) > 1 else 1
    tr = _row_tile(rows, cols)
    c1 = 1.0 - ADAM_B1 ** ADAM_STEP
    c2 = 1.0 - ADAM_B2 ** ADAM_STEP

    def body(w_ref, g_ref, m_ref, v_ref, d_ref, mo_ref, vo_ref):
        gv = g_ref[...]
        mn = ADAM_B1 * m_ref[...] + (1.0 - ADAM_B1) * gv
        vn = ADAM_B2 * v_ref[...] + (1.0 - ADAM_B2) * (gv * gv)
        mo_ref[...] = mn
        vo_ref[...] = vn
        d_ref[...] = -ADAM_LR * ((mn / c1) / (jnp.sqrt(vn / c2) + ADAM_EPS) + ADAM_WD * w_ref[...])

    blk = pl.BlockSpec((tr, cols), lambda i: (i, 0))
    outs = pl.pallas_call(
        body, name=name, grid=(rows // tr,), in_specs=[blk] * 4, out_specs=[blk] * 3,
        out_shape=[jax.ShapeDtypeStruct((rows, cols), F32)] * 3,
        compiler_params=_cp("arbitrary"))(*[a.reshape(rows, cols) for a in (w, g, m, v)])
    return tuple(a.reshape(shape) for a in outs)


def _hbm_spec():
    return pl.BlockSpec(memory_space=pl.ANY)


def _allgather_halves(flat, *, name):
    r, c = flat.shape
    rh = r // 2

    def body(x_ref, out_ref, send_sems, recv_sems, local_sem):
        x, y, cc = lax.axis_index("x"), lax.axis_index("y"), lax.axis_index("c")
        me, sibling = (x, y, cc), (x, y, 1 - cc)
        chips = [(1 - x, y), (x, 1 - y), (1 - x, 1 - y)]
        my_half = x_ref.at[pl.ds(pl.multiple_of(cc * rh, 16), rh)]

        def rows(px, py, pc):
            return out_ref.at[4 * px + 2 * py + pc]

        def copy(kk, block, to, src=None):
            return pltpu.make_async_remote_copy(
                src_ref=rows(*block) if src is None else src, dst_ref=rows(*block),
                send_sem=send_sems.at[kk], recv_sem=recv_sems.at[kk], device_id=to, device_id_type=MESH)

        mine = pltpu.make_async_copy(my_half, rows(*me), local_sem)
        mine.start()
        first = [copy(0, me, sibling, src=my_half)]
        first += [copy(1 + j, me, (*chip, cc), src=my_half) for j, chip in enumerate(chips)]
        for cp in first:
            cp.start()
        passed = [copy(4 + j, (*chip, cc), sibling) for j, chip in enumerate(chips)]
        for j, chip in enumerate(chips):
            copy(1 + j, (*chip, cc), me).wait_recv()
            passed[j].start()
        copy(0, sibling, me).wait_recv()
        for j, chip in enumerate(chips):
            copy(4 + j, (*chip, 1 - cc), me).wait_recv()
        for cp in first + passed:
            cp.wait_send()
        mine.wait()

    return pl.pallas_call(
        body, name=name, in_specs=[_hbm_spec()], out_specs=_hbm_spec(),
        out_shape=jax.ShapeDtypeStruct((8, rh, c), flat.dtype),
        scratch_shapes=[pltpu.SemaphoreType.DMA((7,)), pltpu.SemaphoreType.DMA((7,)), pltpu.SemaphoreType.DMA],
        )(flat)


def _pair_swap_halves(g, *, name):
    nb, r, c = g.shape
    rh = r // 2

    def body(g_ref, out_ref, send_sem, recv_sem):
        x, y, cc = lax.axis_index("x"), lax.axis_index("y"), lax.axis_index("c")
        src = g_ref.at[:, pl.ds(pl.multiple_of((1 - cc) * rh, 8), rh), :]
        cp = pltpu.make_async_remote_copy(src_ref=src, dst_ref=out_ref, send_sem=send_sem, recv_sem=recv_sem,
                                          device_id=(x, y, 1 - cc), device_id_type=MESH)
        cp.start()
        cp.wait()

    return pl.pallas_call(
        body, name=name, in_specs=[_hbm_spec()], out_specs=_hbm_spec(),
        out_shape=jax.ShapeDtypeStruct((nb, rh, c), g.dtype),
        scratch_shapes=[pltpu.SemaphoreType.DMA, pltpu.SemaphoreType.DMA])(g)


def _add_own_half(g, recv, cidx, *, name, tr=256):
    nb, r, c = g.shape
    rh = r // 2
    tr = _row_tile(rh, c, tr * c)
    nt = rh // tr

    def body(c_ref, g_ref, r_ref, o_ref):
        o_ref[...] = g_ref[...] + r_ref[...]

    grid_spec = pltpu.PrefetchScalarGridSpec(
        num_scalar_prefetch=1, grid=(nb, nt),
        in_specs=[pl.BlockSpec((None, tr, c), lambda b, t, c_ref: (b, c_ref[0] * nt + t, 0)),
                  pl.BlockSpec((None, tr, c), lambda b, t, c_ref: (b, t, 0))],
        out_specs=pl.BlockSpec((None, tr, c), lambda b, t, c_ref: (b, t, 0)))
    return pl.pallas_call(
        body, name=name, grid_spec=grid_spec, out_shape=jax.ShapeDtypeStruct((nb, rh, c), g.dtype),
        compiler_params=_cp("arbitrary", "arbitrary"))(cidx, g, recv)


def _chip_exchange(part, *, name):
    nb, r, c = part.shape

    def body(p_ref, out_ref, send_sems, recv_sems, local_sem):
        x, y, cc = lax.axis_index("x"), lax.axis_index("y"), lax.axis_index("c")
        mine_idx = 2 * x + y
        chips = [(1 - x, y), (x, 1 - y), (1 - x, 1 - y)]
        own = pltpu.make_async_copy(p_ref.at[mine_idx], out_ref.at[mine_idx], local_sem)
        own.start()
        sends = []
        for j, (px, py) in enumerate(chips):
            sends.append(pltpu.make_async_remote_copy(
                src_ref=p_ref.at[2 * px + py], dst_ref=out_ref.at[mine_idx], send_sem=send_sems.at[j],
                recv_sem=recv_sems.at[j], device_id=(px, py, cc), device_id_type=MESH))
        for cp in sends:
            cp.start()
        for j, (px, py) in enumerate(chips):
            pltpu.make_async_remote_copy(
                src_ref=p_ref.at[mine_idx], dst_ref=out_ref.at[2 * px + py], send_sem=send_sems.at[j],
                recv_sem=recv_sems.at[j], device_id=(px, py, cc), device_id_type=MESH).wait_recv()
        for cp in sends:
            cp.wait_send()
        own.wait()

    return pl.pallas_call(
        body, name=name, in_specs=[_hbm_spec()], out_specs=_hbm_spec(),
        out_shape=jax.ShapeDtypeStruct((nb, r, c), part.dtype),
        scratch_shapes=[pltpu.SemaphoreType.DMA((3,)), pltpu.SemaphoreType.DMA((3,)), pltpu.SemaphoreType.DMA])(part)


def _sum4(parts, *, name, tr=256):
    nb, r, c = parts.shape
    tr = _row_tile(r, c, tr * c)

    def body(p_ref, o_ref):
        o_ref[...] = ((p_ref[0] + p_ref[1]) + p_ref[2]) + p_ref[3]

    return pl.pallas_call(
        body, name=name, grid=(r // tr,), in_specs=[pl.BlockSpec((nb, tr, c), lambda t: (0, t, 0))],
        out_specs=pl.BlockSpec((tr, c), lambda t: (t, 0)), out_shape=jax.ShapeDtypeStruct((r, c), parts.dtype),
        compiler_params=_cp("arbitrary"))(parts)


def _pair_join(half, *, name):
    r, c = half.shape

    def body(h_ref, out_ref, send_sem, recv_sem, local_sem):
        x, y, cc = lax.axis_index("x"), lax.axis_index("y"), lax.axis_index("c")
        own = pltpu.make_async_copy(h_ref, out_ref.at[cc], local_sem)
        own.start()
        cp = pltpu.make_async_remote_copy(src_ref=h_ref, dst_ref=out_ref.at[cc], send_sem=send_sem,
                                          recv_sem=recv_sem, device_id=(x, y, 1 - cc), device_id_type=MESH)
        cp.start()
        pltpu.make_async_remote_copy(src_ref=h_ref, dst_ref=out_ref.at[1 - cc], send_sem=send_sem,
                                     recv_sem=recv_sem, device_id=(x, y, 1 - cc), device_id_type=MESH).wait_recv()
        cp.wait_send()
        own.wait()

    return pl.pallas_call(
        body, name=name, in_specs=[_hbm_spec()], out_specs=_hbm_spec(),
        out_shape=jax.ShapeDtypeStruct((2, r, c), half.dtype),
        scratch_shapes=[pltpu.SemaphoreType.DMA, pltpu.SemaphoreType.DMA, pltpu.SemaphoreType.DMA])(half)


def _reduce_scatter(g, cidx):
    recv = _pair_swap_halves(g, name="rs_pair_swap")
    part = _add_own_half(g, recv, cidx, name="rs_pair_add")
    landed = _chip_exchange(part, name="rs_chip_exchange")
    half = _sum4(landed, name="rs_chip_sum")
    both = _pair_join(half, name="rs_pair_join")
    return both.reshape(g.shape[1], g.shape[2])


def _rows(a):
    return a.reshape(-1, FLAT_COLS)


def _pad_rows(a, rows):
    return jnp.pad(a, ((0, rows - a.shape[0]), (0, 0)))


def _to_heads(a, nh, dh, dtype=BF16):
    return a.reshape(a.shape[0], nh, dh).transpose(1, 0, 2).astype(dtype)


def _from_heads(a):
    return a.transpose(1, 0, 2).reshape(a.shape[1], -1)


def _rope_tables(s, reps):
    half = B_ROPE // 2
    inv = ROPE_THETA ** (-jnp.arange(0, B_ROPE, 2, dtype=F32) / B_ROPE)
    ang = jnp.arange(s, dtype=F32)[:, None] * inv[None, :]
    return jnp.tile(jnp.cos(ang), (1, reps)), jnp.tile(jnp.sin(ang), (1, reps))


def _alibi_slopes():
    return 2.0 ** (-8.0 * jnp.arange(1, A_HEADS + 1, dtype=F32) / A_HEADS)


def _ffn_fwd(h, norm, wts, tag):
    gate, up, act, xn = _ffn_up(h, norm, wts["wg"], wts["wu"], name=f"{tag}_up")
    out = _mm_res_fwd(act, wts["wd"], h, scale=FFN_RES_SCALE, name=f"{tag}_down")
    return out, dict(h_in=h, gate=gate, up=up, act=act, xn=xn)


def _ffn_bwd(dh, norm, wts, sv, tag):
    dgate, dup = _ffn_down_bwd(dh, wts["wd"], sv["gate"], sv["up"], scale=FFN_RES_SCALE, name=f"{tag}_down_bwd")
    d_wd = _mm_tn(sv["act"], dh, b_scale=FFN_RES_SCALE, name=f"{tag}_dwd")
    d_wg = _mm_tn(sv["xn"], dgate, name=f"{tag}_dwg")
    d_wu = _mm_tn(sv["xn"], dup, name=f"{tag}_dwu")
    dh_in, dnorm = _mm_nt_rmsbwd([(dgate, wts["wg"]), (dup, wts["wu"])], sv["h_in"], norm, dh,
                                 name=f"{tag}_dx")
    return dh_in, dnorm, d_wg, d_wu, d_wd


def _even_weights(w_in, w_uq, w_ukv):
    half = B_ROPE // 2
    base = w_in.shape[1]
    kr1, kr2 = w_in[:, base - B_ROPE:base - half], w_in[:, base - half:]
    w_in_cat = jnp.concatenate([w_in, -kr2, kr1, jnp.zeros((w_in.shape[0], 64), w_in.dtype)], axis=1)
    u3 = w_uq.reshape(w_uq.shape[0], B_HEADS, B_NOPE + B_ROPE)
    nope = u3[:, :, :B_NOPE].reshape(w_uq.shape[0], -1)
    r1 = u3[:, :, B_NOPE:B_NOPE + half].reshape(w_uq.shape[0], -1)
    r2 = u3[:, :, B_NOPE + half:].reshape(w_uq.shape[0], -1)
    w_q_cat = jnp.concatenate([nope, r1, r2, -r2, r1], axis=1)
    return w_in_cat, w_q_cat, w_ukv


def _even_fwd(h, w, i):
    s = h.shape[0]
    half = B_ROPE // 2
    ycat, xn = _rms_mm_fwd(h, w["mix_norm"][i:i + 1], w["ev_in_cat"], name="ev_in")
    a_q, a_k, a_v = ycat[:, :512], ycat[:, 512:640], ycat[:, 640:768]
    c_q, c_kv = ycat[:, 768:1024], ycat[:, 1024:1152]
    cos32, sin32 = _rope_tables(s, 2)
    kro = _rope_fwd(ycat[:, 1152:1184], ycat[:, 1184:1216], cos32, sin32, name="ev_k_rope")
    qa, ka, va = _to_heads(a_q, A_HEADS, A_HEAD_DIM), _to_heads(a_k, A_KV_HEADS, A_HEAD_DIM), _to_heads(a_v, A_KV_HEADS, A_HEAD_DIM)
    ss = jnp.stack([_alibi_slopes(), w["ev_sinks"].reshape(-1)])
    oa, lse_a = _flash_fwd(qa, ka, va, scale=A_HEAD_DIM ** -0.5, name="swa_fwd", tq=256, window=WINDOW, slopes_sinks=ss)
    yq, xn_q = _rms_mm_fwd(c_q, w["ev_cq_norm"], w["ev_q_cat"], name="ev_q_up")
    cos256, sin256 = _rope_tables(s, 2 * B_HEADS)
    qro = _rope_fwd(yq[:, 512:768], yq[:, 768:1024], cos256, sin256, name="ev_q_rope")
    ykv, xn_kv = _rms_mm_fwd(c_kv, w["ev_ckv_norm"], w["ev_ukv"], name="ev_kv_up")
    zq = jnp.zeros((s, B_HEADS, LANES - B_NOPE - B_ROPE), F32)
    qb = jnp.concatenate([yq[:, :512].reshape(s, B_HEADS, B_NOPE), qro[:, :128].reshape(s, B_HEADS, half),
                          qro[:, 128:].reshape(s, B_HEADS, half), zq], axis=-1).transpose(1, 0, 2).astype(BF16)
    kv3 = ykv.reshape(s, B_HEADS, B_NOPE + B_V)
    kb = jnp.concatenate([kv3[:, :, :B_NOPE], jnp.broadcast_to(kro[:, None, :], (s, B_HEADS, B_ROPE)), zq],
                         axis=-1).transpose(1, 0, 2).astype(BF16)
    vb = kv3[:, :, B_NOPE:].transpose(1, 0, 2).astype(BF16)
    ob, lse_b = _causal_fwd(qb, kb, vb, scale=(B_NOPE + B_ROPE) ** -0.5, name="mla_fwd", tq=512)
    attn = jnp.concatenate([_from_heads(oa), _from_heads(ob)], axis=-1)
    out = _mm_res_fwd(attn, w["ev_out"], h, scale=1.0, name="ev_out")
    sv = dict(h_in=h, xn=xn, c_q=c_q, c_kv=c_kv, xn_q=xn_q, xn_kv=xn_kv, qa=qa, ka=ka, va=va, oa=oa, lse_a=lse_a,
              ss=ss, qb=qb, kb=kb, vb=vb, ob=ob, lse_b=lse_b, attn=attn, cos32=cos32, sin32=sin32,
              cos256=cos256, sin256=sin256)
    return out, sv


def _even_bwd(dh, w, sv, i):
    s = dh.shape[0]
    half = B_ROPE // 2
    g = {}
    dattn = _mm_nt(dh, w["ev_out"], name="ev_out_dx")
    g["ev_w_out"] = _mm_tn(sv["attn"], dh, name="ev_out_dw")[None]
    doa, dob = _to_heads(dattn[:, :512], A_HEADS, A_HEAD_DIM), _to_heads(dattn[:, 512:], B_HEADS, B_V)
    dqa, dka, dva, dsink = _flash_bwd(sv["qa"], sv["ka"], sv["va"], sv["oa"], doa, sv["lse_a"], scale=A_HEAD_DIM ** -0.5,
                                      name="swa_bwd", tq=256, window=WINDOW, slopes_sinks=sv["ss"])
    g["ev_sinks"] = dsink[:, :, 0, 0].reshape(1, A_HEADS)
    dqb, dkb, dvb = _causal_bwd(sv["qb"], sv["kb"], sv["vb"], sv["ob"], dob, sv["lse_b"],
                                scale=(B_NOPE + B_ROPE) ** -0.5, name="mla_bwd", tq=512)
    dq_r1 = dqb[:, :, B_NOPE:B_NOPE + half].transpose(1, 0, 2).reshape(s, -1)
    dq_r2 = dqb[:, :, B_NOPE + half:B_NOPE + B_ROPE].transpose(1, 0, 2).reshape(s, -1)
    dq1, dq2 = _rope_bwd(jnp.concatenate([dq_r1, dq_r2], axis=-1)[None], sv["cos256"], sv["sin256"], name="ev_q_rope_bwd")
    dyq = jnp.concatenate([_from_heads(dqb[:, :, :B_NOPE]), dq1, dq2], axis=-1)
    dwq = _mm_tn(sv["xn_q"], dyq, name="ev_q_up_dw")
    dcq, g["ev_cq_norm"] = _mm_nt_rmsbwd([(dyq, w["ev_q_cat"])], sv["c_q"], w["ev_cq_norm"], None, name="ev_q_up_dx")
    kq = sv["c_q"].shape[1]
    d_nope = dwq[:, :512].reshape(kq, B_HEADS, B_NOPE)
    d_r1 = (dwq[:, 512:640] + dwq[:, 896:1024]).reshape(kq, B_HEADS, half)
    d_r2 = (dwq[:, 640:768] - dwq[:, 768:896]).reshape(kq, B_HEADS, half)
    g["ev_w_uq"] = jnp.concatenate([d_nope, d_r1, d_r2], axis=-1).reshape(1, kq, -1)
    dykv = jnp.concatenate([dkb[:, :, :B_NOPE].transpose(1, 0, 2), dvb.transpose(1, 0, 2)], axis=-1).reshape(s, -1)
    g["ev_w_ukv"] = _mm_tn(sv["xn_kv"], dykv, name="ev_kv_up_dw")[None]
    dckv, g["ev_ckv_norm"] = _mm_nt_rmsbwd([(dykv, w["ev_ukv"])], sv["c_kv"], w["ev_ckv_norm"], None, name="ev_kv_up_dx")
    dk1, dk2 = _rope_bwd(dkb[:, :, B_NOPE:B_NOPE + B_ROPE], sv["cos32"], sv["sin32"], name="ev_k_rope_bwd")
    dycat = jnp.concatenate([_from_heads(dqa.reshape(A_HEADS, s, A_HEAD_DIM)), _from_heads(dka), _from_heads(dva),
                             dcq, dckv, dk1, dk2, jnp.zeros((s, 64), F32)], axis=-1)
    dwin = _mm_tn(sv["xn"], dycat, name="ev_in_dw")
    base = 1184
    g["ev_w_in"] = jnp.concatenate([dwin[:, :base - B_ROPE],
                                    dwin[:, base - B_ROPE:base - half] + dwin[:, base + half:base + B_ROPE],
                                    dwin[:, base - half:base] - dwin[:, base:base + half]], axis=-1)[None]
    dh_in, dnorm = _mm_nt_rmsbwd([(dycat, w["ev_in_cat"])], sv["h_in"], w["mix_norm"][i:i + 1], dh, name="ev_in_dx")
    return dh_in, dnorm, g


def _odd_fwd(h, w, i):
    s = h.shape[0]
    wd = C_HEADS * C_HEAD_DIM
    y, xn = _rms_mm_fwd(h, w["mix_norm"][i:i + 1], w["od_in_pad"], name="od_in")
    def widen(a, one_at):
        pad = jnp.zeros((s, C_HEADS, LANES - C_HEAD_DIM), F32).at[:, :, one_at].set(1.0)
        return jnp.concatenate([a.reshape(s, C_HEADS, C_HEAD_DIM), pad], axis=-1).transpose(1, 0, 2).astype(BF16)

    q, k = widen(y[:, :wd], 0), widen(y[:, wd:2 * wd], 1)
    v = _to_heads(y[:, 2 * wd:3 * wd], C_HEADS, C_HEAD_DIM)
    ft = y[:, 3 * wd:3 * wd + C_HEADS].T
    bf = w["od_b_f"].reshape(C_HEADS, 1)
    logc = _fox_gate_fwd(ft, bf, name="fox_gate_fwd").reshape(C_HEADS, 1, s)
    o, lse = _causal_fwd(q, k, v, scale=C_HEAD_DIM ** -0.5, name="fox_fwd", tq=512, kbias=logc)
    attn = _from_heads(o)
    out = _mm_res_fwd(attn, w["od_out"], h, scale=1.0, name="od_out")
    return out, dict(h_in=h, xn=xn, q=q, k=k, v=v, o=o, lse=lse, logc=logc, ft=ft, bf=bf, attn=attn)


def _odd_bwd(dh, w, sv, i):
    s = dh.shape[0]
    g = {}
    dattn = _mm_nt(dh, w["od_out"], name="od_out_dx")
    g["od_w_out"] = _mm_tn(sv["attn"], dh, name="od_out_dw")[None]
    do = _to_heads(dattn, C_HEADS, C_HEAD_DIM)
    scale = C_HEAD_DIM ** -0.5
    dq, dk, dv = _causal_bwd(sv["q"], sv["k"], sv["v"], sv["o"], do, sv["lse"], scale=scale,
                             name="fox_bwd", tq=512, kbias=sv["logc"])
    dft, dbf = _fox_gate_bwd(dq[:, :, C_HEAD_DIM + 1], dk[:, :, C_HEAD_DIM], sv["ft"], sv["bf"],
                             inv_scale=1.0 / scale, name="fox_gate_bwd")
    dq, dk = dq[:, :, :C_HEAD_DIM], dk[:, :, :C_HEAD_DIM]
    g["od_b_f"] = dbf.reshape(1, C_HEADS)
    n_pad = w["od_in_pad"].shape[1]
    n_real = 3 * C_HEADS * C_HEAD_DIM + C_HEADS
    dy = jnp.concatenate([_from_heads(dq), _from_heads(dk), _from_heads(dv), dft.T,
                          jnp.zeros((s, n_pad - n_real), F32)], axis=-1)
    g["od_w_in"] = _mm_tn(sv["xn"], dy, name="od_in_dw")[:, :n_real][None]
    dh_in, dnorm = _mm_nt_rmsbwd([(dy, w["od_in_pad"])], sv["h_in"], w["mix_norm"][i:i + 1], dh, name="od_in_dx")
    return dh_in, dnorm, g


def _kernel_weights(full, replicated):
    depth, f = full["ffa_w_down"].shape[:2]
    w = dict(replicated)
    for tag in ("ffa", "ffb"):
        w[tag] = [dict(wg=full[tag + "_w_gate_up"][i][:, :f], wu=full[tag + "_w_gate_up"][i][:, f:],
                       wd=full[tag + "_w_down"][i]) for i in range(depth)]
    w["ple_gate"], w["ple_proj"] = full["ple_w_gate"], full["ple_w_proj"]
    w["ev_in_cat"], w["ev_q_cat"], w["ev_ukv"] = _even_weights(full["ev_w_in"][0], full["ev_w_uq"][0], full["ev_w_ukv"][0])
    w["ev_out"], w["od_out"] = full["ev_w_out"][0], full["od_w_out"][0]
    od_in = full["od_w_in"][0]
    w["od_in_pad"] = jnp.pad(od_in, ((0, 0), (0, (-od_in.shape[1]) % LANES)))
    return w


def _local_step(x, p, tgt, w):
    depth = p.shape[0]
    h = x
    saved = []
    for i in range(depth):
        sv = {}
        h, sv["ffa"] = _ffn_fwd(h, w["ffa_norm"][i:i + 1], w["ffa"][i], f"ffa{i}")
        if i % 2 == 0:
            h, sv["mix"] = _even_fwd(h, w, i)
        else:
            h, sv["mix"] = _odd_fwd(h, w, i)
        h, sv["ffb"] = _ffn_fwd(h, w["ffb_norm"][i:i + 1], w["ffb"][i], f"ffb{i}")
        h_in = h
        h, xn, gate, pp = _ple_fwd(h, w["ple_norm"][i:i + 1], w["ple_gate"][i], p[i], w["ple_proj"][i], name=f"ple{i}")
        sv["ple"] = dict(h_in=h_in, xn=xn, gate=gate, pp=pp)
        saved.append(sv)
    loss_vec, dh, d_final = _final_loss(h, w["final_norm"].reshape(1, -1), tgt, name="final_loss")

    per_layer = [dict() for _ in range(depth)]
    grads = {}
    for i in reversed(range(depth)):
        sv, gl = saved[i], per_layer[i]
        dz, dpp = _ple_bwd_elem(dh, sv["ple"]["gate"], sv["ple"]["pp"], name=f"ple{i}_bwd")
        gl["ple_w_gate"] = _mm_tn(sv["ple"]["xn"], dz, name=f"ple{i}_dwg")
        gl["ple_w_proj"] = _mm_tn(p[i], dpp, name=f"ple{i}_dwp")
        dh, gl["ple_norm"] = _mm_nt_rmsbwd([(dz, w["ple_gate"][i])], sv["ple"]["h_in"], w["ple_norm"][i:i + 1], dh,
                                           name=f"ple{i}_dx")
        dh, gl["ffb_norm"], d_wg, d_wu, gl["ffb_w_down"] = _ffn_bwd(dh, w["ffb_norm"][i:i + 1], w["ffb"][i], sv["ffb"], f"ffb{i}")
        gl["ffb_w_gate_up"] = (d_wg, d_wu)
        if i % 2 == 0:
            dh, gl["mix_norm"], gm = _even_bwd(dh, w, sv["mix"], i)
        else:
            dh, gl["mix_norm"], gm = _odd_bwd(dh, w, sv["mix"], i)
        grads.update(gm)
        dh, gl["ffa_norm"], d_wg, d_wu, gl["ffa_w_down"] = _ffn_bwd(dh, w["ffa_norm"][i:i + 1], w["ffa"][i], sv["ffa"], f"ffa{i}")
        gl["ffa_w_gate_up"] = (d_wg, d_wu)
    grads["final_norm"] = d_final.reshape(-1)
    for n in ("ffa_norm", "mix_norm", "ffb_norm", "ple_norm"):
        grads[n] = jnp.concatenate([per_layer[i][n] for i in range(depth)], axis=0)
    for n in ("ffa_w_down", "ffb_w_down", "ple_w_gate", "ple_w_proj"):
        grads[n] = jnp.stack([per_layer[i][n] for i in range(depth)])
    for n in ("ffa_w_gate_up", "ffb_w_gate_up"):
        grads[n] = [per_layer[i][n] for i in range(depth)]
    return loss_vec[0, 0], dh, grads


def _grad_shard(name, axis, full, sidx):
    if name.endswith("w_gate_up"):
        pieces = []
        for d_wg, d_wu in full:
            f = d_wg.shape[1]
            src = d_wg if sidx < 2 else d_wu
            lo = (sidx % 2) * (f // 2)
            pieces.append(src[:, lo:lo + f // 2])
        return jnp.stack(pieces)
    n = full.shape[axis] // N_CHIPS
    return lax.slice_in_dim(full, sidx * n, (sidx + 1) * n, axis=axis)


def _small_rows(vals):
    rows = []
    for n in REPLICATED:
        v = vals[n].reshape(-1)
        rows.append(jnp.pad(v, (0, (-v.shape[0]) % FLAT_COLS)).reshape(-1, FLAT_COLS))
    return jnp.concatenate(rows, axis=0)


def kernel(x, p, ffa_norm, ffa_w_gate_up, ffa_w_down, mix_norm, ffb_norm, ffb_w_gate_up, ffb_w_down, ple_norm, ple_w_gate, ple_w_proj, ev_w_in, ev_sinks, ev_cq_norm, ev_w_uq, ev_ckv_norm, ev_w_ukv, ev_w_out, od_w_in, od_b_f, od_w_out, final_norm, loss_target, m_ffa_norm, m_ffa_w_gate_up, m_ffa_w_down, m_mix_norm, m_ffb_norm, m_ffb_w_gate_up, m_ffb_w_down, m_ple_norm, m_ple_w_gate, m_ple_w_proj, m_ev_w_in, m_ev_sinks, m_ev_cq_norm, m_ev_w_uq, m_ev_ckv_norm, m_ev_w_ukv, m_ev_w_out, m_od_w_in, m_od_b_f, m_od_w_out, m_final_norm, v_ffa_norm, v_ffa_w_gate_up, v_ffa_w_down, v_mix_norm, v_ffb_norm, v_ffb_w_gate_up, v_ffb_w_down, v_ple_norm, v_ple_w_gate, v_ple_w_proj, v_ev_w_in, v_ev_sinks, v_ev_cq_norm, v_ev_w_uq, v_ev_ckv_norm, v_ev_w_ukv, v_ev_w_out, v_od_w_in, v_od_b_f, v_od_w_out, v_final_norm):
    env = dict(locals())
    wts = {n: env[n] for n in WEIGHT_ORDER}
    mom1 = {n: env["m_" + n] for n in WEIGHT_ORDER}
    mom2 = {n: env["v_" + n] for n in WEIGHT_ORDER}
    cidx = lax.axis_index("c").astype(jnp.int32).reshape(1)

    seg_rows = [int(np.prod(wts[n].shape)) // FLAT_COLS for n, _ in SHARDED]
    n_rows = sum(seg_rows)
    n_rows_pad = -(-n_rows // 32) * 32
    flat_w = _pad_rows(jnp.concatenate([_rows(wts[n]) for n, _ in SHARDED], axis=0), n_rows_pad).astype(BF16)
    gathered = _allgather_halves(flat_w, name="weight_allgather").reshape(N_CHIPS, n_rows_pad, FLAT_COLS)
    full = {}
    r0 = 0
    for (n, axis), nr in zip(SHARDED, seg_rows):
        parts = gathered[:, r0:r0 + nr].reshape((N_CHIPS,) + wts[n].shape)
        full[n] = jnp.concatenate([parts[k] for k in range(N_CHIPS)], axis=axis)
        r0 += nr

    w = _kernel_weights(full, {n: wts[n] for n in REPLICATED})
    loss_part, grad_x, grads = _local_step(x[0], p[:, 0], loss_target[0], w)
    loss = lax.psum(loss_part, ("x", "y", "c"))

    small = _small_rows(grads)
    g_rows = -(-(n_rows + small.shape[0]) // GRAD_ROW_ALIGN) * GRAD_ROW_ALIGN
    blocks = []
    for k in range(N_CHIPS):
        segs = [_rows(_grad_shard(n, axis, grads[n], k)) for n, axis in SHARDED]
        blocks.append(_pad_rows(jnp.concatenate(segs + [small], axis=0), g_rows))
    reduced = _reduce_scatter(jnp.stack(blocks), cidx)

    gout = {}
    r0 = 0
    for (n, _), nr in zip(SHARDED, seg_rows):
        gout[n] = reduced[r0:r0 + nr].reshape(wts[n].shape)
        r0 += nr
    r0 = n_rows
    for n in REPLICATED:
        size = int(np.prod(wts[n].shape))
        nr = -(-size // FLAT_COLS)
        gout[n] = reduced[r0:r0 + nr].reshape(-1)[:size].reshape(wts[n].shape)
        r0 += nr
    delta, new_m, new_v = {}, {}, {}
    for n in WEIGHT_ORDER:
        delta[n], new_m[n], new_v[n] = _adamw(wts[n], gout[n], mom1[n], mom2[n], name="adamw_" + n)
    return (loss, grad_x[None], *[gout[n] for n in WEIGHT_ORDER], *[delta[n] for n in WEIGHT_ORDER],
            *[new_m[n] for n in WEIGHT_ORDER], *[new_v[n] for n in WEIGHT_ORDER])
```

```python
import functools
import math

import numpy as np
import jax
import jax.numpy as jnp
from jax import lax
from jax.experimental import pallas as pl
from jax.experimental.pallas import tpu as pltpu

F32 = jnp.float32
BF16 = jnp.bfloat16
NT = (((1,), (1,)), ((), ()))
TN = (((0,), (0,)), ((), ()))
MESH = pl.DeviceIdType.MESH

RMS_EPS = 1e-6
FFN_RES_SCALE = 0.5
A_HEADS, A_KV_HEADS, A_HEAD_DIM, WINDOW = 8, 2, 64, 128
B_HEADS, B_Q_LORA, B_KV_LORA, B_NOPE, B_ROPE, B_V = 8, 256, 128, 64, 32, 64
ROPE_THETA = 10000.0
C_HEADS, C_HEAD_DIM = 16, 64
ADAM_LR, ADAM_B1, ADAM_B2, ADAM_EPS, ADAM_WD, ADAM_STEP = 0.001, 0.9, 0.999, 1e-08, 0.01, 10

N_CHIPS = 4
LANES = 128
FLAT_COLS = 1024
MASK_VALUE = -1e30
VMEM_LIMIT = 48 * 2**20

SHARDED = (
    ("ffa_w_gate_up", 2), ("ffa_w_down", 1), ("ffb_w_gate_up", 2), ("ffb_w_down", 1),
    ("ple_w_gate", 1), ("ple_w_proj", 2), ("ev_w_in", 2), ("ev_w_uq", 2), ("ev_w_ukv", 2),
    ("ev_w_out", 1), ("od_w_in", 2), ("od_w_out", 1))
REPLICATED = ("ffa_norm", "mix_norm", "ffb_norm", "ple_norm", "final_norm",
              "ev_sinks", "ev_cq_norm", "ev_ckv_norm", "od_b_f")
WEIGHT_ORDER = ("ffa_norm", "ffa_w_gate_up", "ffa_w_down", "mix_norm", "ffb_norm", "ffb_w_gate_up",
                "ffb_w_down", "ple_norm", "ple_w_gate", "ple_w_proj", "ev_w_in", "ev_sinks",
                "ev_cq_norm", "ev_w_uq", "ev_ckv_norm", "ev_w_ukv", "ev_w_out", "od_w_in", "od_b_f",
                "od_w_out", "final_norm")


def _cp(*sem):
    return pltpu.CompilerParams(dimension_semantics=sem, vmem_limit_bytes=VMEM_LIMIT)


def _sigmoid(z):
    return 1.0 / (1.0 + jnp.exp(-z))


def _rms_stats(xv):
    r = lax.rsqrt(jnp.mean(xv * xv, axis=-1, keepdims=True) + RMS_EPS)
    return r, xv * r


def _rms_bwd(dxn, xv, g):
    r, xhat = _rms_stats(xv)
    u = dxn * g
    dx = r * (u - xhat * jnp.mean(u * xhat, axis=-1, keepdims=True))
    return dx, dxn * xhat


def _col_tile(k_rows, n, budget_bytes=6 * 2**20):
    if k_rows * n * 4 <= budget_bytes or n % LANES:
        return n
    units = n // LANES
    best = LANES
    for d in range(1, units + 1):
        if units % d == 0 and k_rows * d * LANES * 4 <= budget_bytes:
            best = d * LANES
    return best


def _row_tile(rows, cols, target_elems=2**18):
    if rows * cols <= target_elems or rows % 8:
        return rows
    best = 8
    for d in range(8, rows + 1, 8):
        if rows % d == 0 and d * cols <= target_elems:
            best = d
    return best


def _rms_mm_fwd(x, g, w, *, name, tm=512):
    s, k = x.shape
    n = w.shape[1]

    def body(x_ref, g_ref, w_ref, y_ref, xn_ref):
        _, xhat = _rms_stats(x_ref[...])
        xn = (xhat * g_ref[...]).astype(BF16)
        xn_ref[...] = xn
        y_ref[...] = jnp.dot(xn, w_ref[...], preferred_element_type=F32)

    return pl.pallas_call(
        body, name=name, grid=(s // tm,),
        in_specs=[pl.BlockSpec((tm, k), lambda i: (i, 0)), pl.BlockSpec((1, k), lambda i: (0, 0)),
                  pl.BlockSpec((k, n), lambda i: (0, 0))],
        out_specs=[pl.BlockSpec((tm, n), lambda i: (i, 0)), pl.BlockSpec((tm, k), lambda i: (i, 0))],
        out_shape=[jax.ShapeDtypeStruct((s, n), F32), jax.ShapeDtypeStruct((s, k), BF16)],
        compiler_params=_cp("arbitrary"))(x, g, w)


def _ffn_up(x, g, wgu, *, name, tm=512):
    s, k = x.shape
    f = wgu.shape[1] // 2
    tn = _col_tile(k, f)
    nj = f // tn

    def body(x_ref, g_ref, wg_ref, wu_ref, gate_ref, up_ref, act_ref, xn_ref, xn_sc):
        @pl.when(pl.program_id(1) == 0)
        def _():
            _, xhat = _rms_stats(x_ref[...])
            xn = (xhat * g_ref[...]).astype(BF16)
            xn_sc[...] = xn
            xn_ref[...] = xn

        xn = xn_sc[...]
        gg = jnp.dot(xn, wg_ref[...], preferred_element_type=F32)
        uu = jnp.dot(xn, wu_ref[...], preferred_element_type=F32)
        gate_ref[...] = gg.astype(BF16)
        up_ref[...] = uu.astype(BF16)
        act_ref[...] = ((gg * _sigmoid(gg)) * uu).astype(BF16)

    tile = pl.BlockSpec((tm, tn), lambda i, j: (i, j))
    return pl.pallas_call(
        body, name=name, grid=(s // tm, nj),
        in_specs=[pl.BlockSpec((tm, k), lambda i, j: (i, 0)), pl.BlockSpec((1, k), lambda i, j: (0, 0)),
                  pl.BlockSpec((k, tn), lambda i, j: (0, j)), pl.BlockSpec((k, tn), lambda i, j: (0, j + nj))],
        out_specs=[tile, tile, tile, pl.BlockSpec((tm, k), lambda i, j: (i, 0))],
        out_shape=[jax.ShapeDtypeStruct((s, f), BF16)] * 3 + [jax.ShapeDtypeStruct((s, k), BF16)],
        scratch_shapes=[pltpu.VMEM((tm, k), BF16)],
        compiler_params=_cp("arbitrary", "arbitrary"))(x, g, wgu, wgu)


def _mm_res_fwd(a, w, res, *, scale, name, tm=512):
    s, k = a.shape
    n = w.shape[1]

    def body(a_ref, w_ref, r_ref, o_ref):
        o_ref[...] = r_ref[...] + scale * jnp.dot(a_ref[...], w_ref[...], preferred_element_type=F32)

    return pl.pallas_call(
        body, name=name, grid=(s // tm,),
        in_specs=[pl.BlockSpec((tm, k), lambda i: (i, 0)), pl.BlockSpec((k, n), lambda i: (0, 0)),
                  pl.BlockSpec((tm, n), lambda i: (i, 0))],
        out_specs=pl.BlockSpec((tm, n), lambda i: (i, 0)),
        out_shape=jax.ShapeDtypeStruct((s, n), F32),
        compiler_params=_cp("arbitrary"))(a, w, res)


def _ffn_down_bwd(dh, wd, gate, up, *, scale, name, tm=512):
    s, d = dh.shape
    f = wd.shape[0]
    tn = _col_tile(d, f)

    def body(dh_ref, wd_ref, gate_ref, up_ref, dg_ref, du_ref):
        dhb = (dh_ref[...] * scale).astype(BF16)
        da = lax.dot_general(dhb, wd_ref[...], NT, preferred_element_type=F32)
        gg = gate_ref[...].astype(F32)
        uu = up_ref[...].astype(F32)
        sg = _sigmoid(gg)
        dg_ref[...] = (da * uu * (sg * (1.0 + gg * (1.0 - sg)))).astype(BF16)
        du_ref[...] = (da * (gg * sg)).astype(BF16)

    tile = pl.BlockSpec((tm, tn), lambda i, j: (i, j))
    return pl.pallas_call(
        body, name=name, grid=(s // tm, f // tn),
        in_specs=[pl.BlockSpec((tm, d), lambda i, j: (i, 0)), pl.BlockSpec((tn, d), lambda i, j: (j, 0)), tile, tile],
        out_specs=[tile, tile],
        out_shape=[jax.ShapeDtypeStruct((s, f), BF16)] * 2,
        compiler_params=_cp("arbitrary", "arbitrary"))(dh, wd, gate, up)


def _mm_tn(a, bs, *, name, b_scale=1.0, ts=512):
    bs = list(bs) if isinstance(bs, (list, tuple)) else [bs]
    s, k = a.shape
    n = bs[0].shape[1]
    tn = _col_tile(k, n)
    per = n // tn

    def body(a_ref, *refs):
        b_refs, o_ref = refs[:-1], refs[-1]
        j = pl.program_id(0)

        @pl.when(pl.program_id(1) == 0)
        def _():
            o_ref[...] = jnp.zeros_like(o_ref)

        for m, b_ref in enumerate(b_refs):
            def acc(b_ref=b_ref):
                bv = b_ref[...]
                if b_scale != 1.0:
                    bv = bv * b_scale
                o_ref[...] += lax.dot_general(a_ref[...].astype(BF16), bv.astype(BF16), TN, preferred_element_type=F32)

            if len(b_refs) == 1:
                acc()
            else:
                pl.when(jnp.logical_and(j >= m * per, j < (m + 1) * per))(acc)

    def b_spec(m):
        def idx(j, t):
            mine = jnp.logical_and(j >= m * per, j < (m + 1) * per)
            return (jnp.where(mine, t, 0), jnp.clip(j - m * per, 0, per - 1))
        return pl.BlockSpec((ts, tn), idx)

    return pl.pallas_call(
        body, name=name, grid=(per * len(bs), s // ts),
        in_specs=[pl.BlockSpec((ts, k), lambda j, t: (t, 0))] + [b_spec(m) for m in range(len(bs))],
        out_specs=pl.BlockSpec((k, tn), lambda j, t: (0, j)),
        out_shape=jax.ShapeDtypeStruct((k, n * len(bs)), F32),
        compiler_params=_cp("arbitrary", "arbitrary"))(a, *bs)


def _mm_nt(dy, w, *, name, tm=512):
    s, n = dy.shape
    k = w.shape[0]

    def body(dy_ref, w_ref, o_ref):
        o_ref[...] = lax.dot_general(dy_ref[...].astype(BF16), w_ref[...], NT, preferred_element_type=F32)

    return pl.pallas_call(
        body, name=name, grid=(s // tm,),
        in_specs=[pl.BlockSpec((tm, n), lambda i: (i, 0)), pl.BlockSpec((k, n), lambda i: (0, 0))],
        out_specs=pl.BlockSpec((tm, k), lambda i: (i, 0)),
        out_shape=jax.ShapeDtypeStruct((s, k), F32),
        compiler_params=_cp("arbitrary"))(dy, w)


def _mm_nt_rmsbwd(pairs, x, g, dres, *, name, tm=256):
    s, k = x.shape
    npairs = len(pairs)
    pairs = [pr if len(pr) == 3 else (pr[0], pr[1], 0) for pr in pairs]

    def body(*refs):
        dy_refs = refs[0:2 * npairs:2]
        w_refs = refs[1:2 * npairs:2]
        rest = refs[2 * npairs:]
        x_ref, g_ref = rest[0], rest[1]
        if dres is None:
            dx_ref, dg_ref = rest[2], rest[3]
        else:
            dres_ref, dx_ref, dg_ref = rest[2], rest[3], rest[4]
        dxn = None
        for dy_ref, w_ref in zip(dy_refs, w_refs):
            t = lax.dot_general(dy_ref[...].astype(BF16), w_ref[...], NT, preferred_element_type=F32)
            dxn = t if dxn is None else dxn + t
        dx, dgrow = _rms_bwd(dxn, x_ref[...], g_ref[...])
        if dres is not None:
            dx = dx + dres_ref[...]
        dx_ref[...] = dx

        @pl.when(pl.program_id(0) == 0)
        def _():
            dg_ref[...] = jnp.zeros_like(dg_ref)

        dg_ref[...] += jnp.sum(dgrow, axis=0, keepdims=True)

    in_specs, args = [], []
    for dy, w, cb in pairs:
        n = dy.shape[1]
        in_specs += [pl.BlockSpec((tm, n), lambda i: (i, 0)), pl.BlockSpec((k, n), lambda i, cb=cb: (0, cb))]
        args += [dy, w]
    row = pl.BlockSpec((tm, k), lambda i: (i, 0))
    vec = pl.BlockSpec((1, k), lambda i: (0, 0))
    in_specs += [row, vec]
    args += [x, g]
    if dres is not None:
        in_specs.append(row)
        args.append(dres)
    return pl.pallas_call(
        body, name=name, grid=(s // tm,), in_specs=in_specs, out_specs=[row, vec],
        out_shape=[jax.ShapeDtypeStruct((s, k), F32), jax.ShapeDtypeStruct((1, k), F32)],
        compiler_params=_cp("arbitrary"))(*args)


def _ple_fwd(h, g, wg, p, wp, *, name, tm=512):
    s, d = h.shape
    pd = p.shape[1]

    def body(h_ref, g_ref, wg_ref, p_ref, wp_ref, o_ref, xn_ref, gate_ref, pp_ref):
        hv = h_ref[...]
        _, xhat = _rms_stats(hv)
        xn = (xhat * g_ref[...]).astype(BF16)
        xn_ref[...] = xn
        gate = _sigmoid(jnp.dot(xn, wg_ref[...], preferred_element_type=F32))
        pp = jnp.dot(p_ref[...].astype(BF16), wp_ref[...], preferred_element_type=F32)
        gate_ref[...] = gate.astype(BF16)
        pp_ref[...] = pp.astype(BF16)
        o_ref[...] = hv + gate * pp

    row = pl.BlockSpec((tm, d), lambda i: (i, 0))
    return pl.pallas_call(
        body, name=name, grid=(s // tm,),
        in_specs=[row, pl.BlockSpec((1, d), lambda i: (0, 0)), pl.BlockSpec((d, d), lambda i: (0, 0)),
                  pl.BlockSpec((tm, pd), lambda i: (i, 0)), pl.BlockSpec((pd, d), lambda i: (0, 0))],
        out_specs=[row, row, row, row],
        out_shape=[jax.ShapeDtypeStruct((s, d), F32)] + [jax.ShapeDtypeStruct((s, d), BF16)] * 3,
        compiler_params=_cp("arbitrary"))(h, g, wg, p, wp)


def _ple_bwd_elem(dh, gate, pp, *, name, tm=512):
    s, d = dh.shape

    def body(dh_ref, gate_ref, pp_ref, dz_ref, dpp_ref):
        dhv = dh_ref[...]
        gt = gate_ref[...].astype(F32)
        dz_ref[...] = (dhv * pp_ref[...].astype(F32) * (gt * (1.0 - gt))).astype(BF16)
        dpp_ref[...] = (dhv * gt).astype(BF16)

    row = pl.BlockSpec((tm, d), lambda i: (i, 0))
    return pl.pallas_call(
        body, name=name, grid=(s // tm,), in_specs=[row, row, row], out_specs=[row, row],
        out_shape=[jax.ShapeDtypeStruct((s, d), BF16)] * 2,
        compiler_params=_cp("arbitrary"))(dh, gate, pp)


def _final_loss(h, g, tgt, *, name, tm=512):
    s, d = h.shape

    def body(h_ref, g_ref, t_ref, loss_ref, dh_ref, dg_ref):
        @pl.when(pl.program_id(0) == 0)
        def _():
            loss_ref[...] = jnp.zeros_like(loss_ref)
            dg_ref[...] = jnp.zeros_like(dg_ref)

        hv = h_ref[...]
        gv = g_ref[...]
        _, xhat = _rms_stats(hv)
        err = xhat * gv - t_ref[...]
        per_row = jnp.mean(err * err, axis=-1, keepdims=True)
        loss_ref[...] += 0.5 * jnp.sum(per_row, axis=0, keepdims=True)
        dx, dgrow = _rms_bwd(err * (1.0 / d), hv, gv)
        dh_ref[...] = dx
        dg_ref[...] += jnp.sum(dgrow, axis=0, keepdims=True)

    row = pl.BlockSpec((tm, d), lambda i: (i, 0))
    vec = pl.BlockSpec((1, d), lambda i: (0, 0))
    return pl.pallas_call(
        body, name=name, grid=(s // tm,), in_specs=[row, vec, row],
        out_specs=[pl.BlockSpec((1, LANES), lambda i: (0, 0)), row, vec],
        out_shape=[jax.ShapeDtypeStruct((1, LANES), F32), jax.ShapeDtypeStruct((s, d), F32),
                   jax.ShapeDtypeStruct((1, d), F32)],
        compiler_params=_cp("arbitrary"))(h, g, tgt)


def _rope_fwd(y1, y2, cos, sin, *, name, tm=512):
    s, r = y1.shape

    def body(a_ref, b_ref, c_ref, s_ref, o_ref):
        o_ref[...] = a_ref[...] * c_ref[...] + b_ref[...] * s_ref[...]

    row = pl.BlockSpec((tm, r), lambda i: (i, 0))
    return pl.pallas_call(
        body, name=name, grid=(s // tm,), in_specs=[row] * 4, out_specs=row,
        out_shape=jax.ShapeDtypeStruct((s, r), F32), compiler_params=_cp("arbitrary"))(y1, y2, cos, sin)


def _rope_bwd(dout, cos, sin, *, name, tm=512):
    nh, s, r = dout.shape

    def body(d_ref, c_ref, s_ref, o1_ref, o2_ref):
        tot = d_ref[0]
        for hh in range(1, nh):
            tot = tot + d_ref[hh]
        o1_ref[...] = tot * c_ref[...]
        o2_ref[...] = tot * s_ref[...]

    row = pl.BlockSpec((tm, r), lambda i: (i, 0))
    return pl.pallas_call(
        body, name=name, grid=(s // tm,),
        in_specs=[pl.BlockSpec((nh, tm, r), lambda i: (0, i, 0)), row, row], out_specs=[row, row],
        out_shape=[jax.ShapeDtypeStruct((s, r), F32)] * 2, compiler_params=_cp("arbitrary"))(dout, cos, sin)


def _split3(v):
    h1 = v.astype(BF16)
    r1 = v - h1.astype(F32)
    h2 = r1.astype(BF16)
    h3 = (r1 - h2.astype(F32)).astype(BF16)
    return h1, h2, h3


def _tri(tb, upper):
    r = lax.broadcasted_iota(jnp.int32, (tb, tb), 0)
    c = lax.broadcasted_iota(jnp.int32, (tb, tb), 1)
    return jnp.where((r <= c) if upper else (r >= c), 1.0, 0.0).astype(BF16)


def _fox_gate_fwd(ft, bf, *, name, tb=512):
    nh, s = ft.shape

    def body(f_ref, b_ref, o_ref, carry):
        @pl.when(pl.program_id(0) == 0)
        def _():
            carry[...] = jnp.zeros_like(carry)

        z = f_ref[...] + b_ref[...]
        lf = jnp.minimum(z, 0.0) - jnp.log(1.0 + jnp.exp(-jnp.abs(z)))
        tri = _tri(tb, True)
        cs = sum(jnp.dot(t, tri, preferred_element_type=F32) for t in _split3(lf))
        o_ref[...] = cs + carry[...]
        carry[...] += jnp.sum(lf, axis=-1, keepdims=True)

    return pl.pallas_call(
        body, name=name, grid=(s // tb,),
        in_specs=[pl.BlockSpec((nh, tb), lambda t: (0, t)), pl.BlockSpec((nh, 1), lambda t: (0, 0))],
        out_specs=pl.BlockSpec((nh, tb), lambda t: (0, t)),
        out_shape=jax.ShapeDtypeStruct((nh, s), F32),
        scratch_shapes=[pltpu.VMEM((nh, 1), F32)], compiler_params=_cp("arbitrary"))(ft, bf)


def _fox_gate_bwd(drow, dcol, ft, bf, *, inv_scale, name, tb=512):
    nh, s = ft.shape
    nb = s // tb

    def body(dr_ref, dc_ref, f_ref, b_ref, df_ref, db_ref, carry):
        @pl.when(pl.program_id(0) == 0)
        def _():
            carry[...] = jnp.zeros_like(carry)
            db_ref[...] = jnp.zeros_like(db_ref)

        dc = (dr_ref[...] - dc_ref[...]) * inv_scale
        tri = _tri(tb, False)
        suf = sum(jnp.dot(t, tri, preferred_element_type=F32) for t in _split3(dc)) + carry[...]
        z = f_ref[...] + b_ref[...]
        dz = suf * (1.0 / (1.0 + jnp.exp(z)))
        df_ref[...] = dz
        db_ref[...] += jnp.sum(dz, axis=-1, keepdims=True)
        carry[...] += jnp.sum(dc, axis=-1, keepdims=True)

    rev = pl.BlockSpec((nh, tb), lambda t: (0, nb - 1 - t))
    one = pl.BlockSpec((nh, 1), lambda t: (0, 0))
    return pl.pallas_call(
        body, name=name, grid=(nb,), in_specs=[rev, rev, rev, one], out_specs=[rev, one],
        out_shape=[jax.ShapeDtypeStruct((nh, s), F32), jax.ShapeDtypeStruct((nh, 1), F32)],
        scratch_shapes=[pltpu.VMEM((nh, 1), F32)], compiler_params=_cp("arbitrary"))(drow, dcol, ft, bf)


def _scores(q, k, i, jblk, *, scale, tq, tk, window, slope, kb):
    s = lax.dot_general(q, k, NT, preferred_element_type=F32) * scale
    qpos = i * tq + lax.broadcasted_iota(jnp.int32, (tq, tk), 0)
    kpos = jblk * tk + lax.broadcasted_iota(jnp.int32, (tq, tk), 1)
    dist = qpos - kpos
    if slope is not None:
        s = s - slope * dist.astype(F32)
    if kb is not None:
        s = s - kb
    ok = dist >= 0
    if window is not None:
        ok = jnp.logical_and(ok, dist < window)
    return jnp.where(ok, s, MASK_VALUE)


def _flash_fwd(q, k, v, *, scale, name, tq, window=None, slopes_sinks=None, kbias=None):
    nh, s, dq = q.shape
    nkv, _, dv = v.shape
    grp = nh // nkv
    tk = tq
    nq = s // tq
    nj = nq if window is None else 2
    assert window is None or window <= tk
    has_ss, has_kb = slopes_sinks is not None, kbias is not None

    def blk(i, jj):
        return jj if window is None else i - jj

    def body(*refs):
        q_ref, k_ref, v_ref = refs[:3]
        pos = 3
        kb_ref = ss_ref = None
        if has_kb:
            kb_ref = refs[pos]
            pos += 1
        if has_ss:
            ss_ref = refs[pos]
            pos += 1
        o_ref, lse_ref, m_sc, l_sc, acc_sc = refs[pos:pos + 5]
        h, i, jj = pl.program_id(0), pl.program_id(1), pl.program_id(2)
        jblk = blk(i, jj)
        valid = (jj <= i) if window is None else (jblk >= 0)

        @pl.when(jj == 0)
        def _():
            m_sc[...] = jnp.full_like(m_sc, MASK_VALUE)
            l_sc[...] = jnp.zeros_like(l_sc)
            acc_sc[...] = jnp.zeros_like(acc_sc)

        @pl.when(valid)
        def _():
            sc = _scores(q_ref[...], k_ref[...], i, jblk, scale=scale, tq=tq, tk=tk, window=window,
                         slope=ss_ref[0, h] if has_ss else None, kb=kb_ref[...] if has_kb else None)
            m_prev = m_sc[...]
            m_new = jnp.maximum(m_prev, jnp.max(sc, axis=-1, keepdims=True))
            alpha = jnp.exp(m_prev - m_new)
            pr = jnp.exp(sc - m_new)
            l_sc[...] = alpha * l_sc[...] + jnp.sum(pr, axis=-1, keepdims=True)
            acc_sc[...] = alpha * acc_sc[...] + jnp.dot(pr.astype(BF16), v_ref[...], preferred_element_type=F32)
            m_sc[...] = m_new

        @pl.when(jj == nj - 1)
        def _():
            m = m_sc[...]
            l = l_sc[...]
            acc = acc_sc[...]
            if has_ss:
                sink = ss_ref[1, h]
                m_f = jnp.maximum(m, sink)
                corr = jnp.exp(m - m_f)
                l = l * corr + jnp.exp(sink - m_f)
                acc = acc * corr
                m = m_f
            o_ref[...] = (acc / l).astype(BF16)
            lse_ref[...] = jnp.broadcast_to(m + jnp.log(l), (tq, LANES))

    def kv_idx(h, i, jj):
        j = jnp.minimum(jj, i) if window is None else jnp.maximum(i - jj, 0)
        return (h // grp, j, 0)

    in_specs = [pl.BlockSpec((None, tq, dq), lambda h, i, jj: (h, i, 0)),
                pl.BlockSpec((None, tk, dq), kv_idx), pl.BlockSpec((None, tk, dv), kv_idx)]
    args = [q, k, v]
    if has_kb:
        in_specs.append(pl.BlockSpec((None, 1, tk), lambda h, i, jj: (h, 0, kv_idx(h, i, jj)[1])))
        args.append(kbias)
    if has_ss:
        in_specs.append(pl.BlockSpec(memory_space=pltpu.SMEM))
        args.append(slopes_sinks)
    return pl.pallas_call(
        body, name=name, grid=(nh, nq, nj), in_specs=in_specs,
        out_specs=[pl.BlockSpec((None, tq, dv), lambda h, i, jj: (h, i, 0)),
                   pl.BlockSpec((None, tq, LANES), lambda h, i, jj: (h, i, 0))],
        out_shape=[jax.ShapeDtypeStruct((nh, s, dv), BF16), jax.ShapeDtypeStruct((nh, s, LANES), F32)],
        scratch_shapes=[pltpu.VMEM((tq, 1), F32), pltpu.VMEM((tq, 1), F32), pltpu.VMEM((tq, dv), F32)],
        compiler_params=_cp("arbitrary", "arbitrary", "arbitrary"))(*args)


def _flash_bwd(q, k, v, o, do, lse, *, scale, name, tq, window=None, slopes_sinks=None, kbias=None):
    nh, s, dq = q.shape
    nkv, _, dv = v.shape
    grp = nh // nkv
    tk = tq
    nq = s // tq
    nsteps = nq if window is None else 2
    has_ss, has_kb = slopes_sinks is not None, kbias is not None
    assert not (has_ss and window is None)

    def body(*refs):
        q_ref, k_ref, v_ref, o_ref, do_ref, lse_ref = refs[:6]
        pos = 6
        kb_ref = ss_ref = dsink_ref = None
        if has_kb:
            kb_ref = refs[pos]
            pos += 1
        if has_ss:
            ss_ref = refs[pos]
            pos += 1
        dq_ref, dk_ref, dv_ref = refs[pos:pos + 3]
        pos += 3
        if has_ss:
            dsink_ref = refs[pos]
            pos += 1
        hk, j, g, ii = (pl.program_id(a) for a in range(4))
        i = ii if window is None else j + ii
        valid = (i >= j) if window is None else (i < nq)

        @pl.when(jnp.logical_and(j == 0, jnp.logical_and(g == 0, ii == 0)))
        def _():
            dq_ref[...] = jnp.zeros_like(dq_ref)
            if has_ss:
                dsink_ref[...] = jnp.zeros_like(dsink_ref)

        @pl.when(jnp.logical_and(g == 0, ii == 0))
        def _():
            dk_ref[...] = jnp.zeros_like(dk_ref)
            dv_ref[...] = jnp.zeros_like(dv_ref)

        @pl.when(valid)
        def _():
            h = hk * grp + g
            qv, kv, dov = q_ref[...], k_ref[...], do_ref[...]
            sc = _scores(qv, kv, i, j, scale=scale, tq=tq, tk=tk, window=window,
                         slope=ss_ref[0, h] if has_ss else None, kb=kb_ref[...] if has_kb else None)
            lse_col = lse_ref[...][:, :1]
            pr = jnp.exp(sc - lse_col)
            dp = lax.dot_general(dov, v_ref[...], NT, preferred_element_type=F32)
            delta = jnp.sum(dov.astype(F32) * o_ref[...].astype(F32), axis=-1, keepdims=True)
            ds = pr * (dp - delta)
            dv_ref[...] += lax.dot_general(pr.astype(BF16), dov, TN, preferred_element_type=F32)
            dsb = (ds * scale).astype(BF16)
            dk_ref[...] += lax.dot_general(dsb, qv, TN, preferred_element_type=F32)
            rows = pl.ds(pl.multiple_of(i * tq, tq), tq)
            dq_ref[g, rows, :] += jnp.dot(dsb, kv, preferred_element_type=F32)
            if has_ss:
                @pl.when(ii == 0)
                def _():
                    psink = jnp.exp(ss_ref[1, h] - lse_col)
                    tot = jnp.sum(psink * delta, axis=0, keepdims=True)
                    dsink_ref[g] -= jnp.broadcast_to(tot, (1, LANES))

    def q_idx(hk, j, g, ii):
        i = jnp.maximum(ii, j) if window is None else jnp.minimum(j + ii, nq - 1)
        return (hk * grp + g, i, 0)

    def kv_idx(hk, j, g, ii):
        return (hk, j, 0)

    in_specs = [pl.BlockSpec((None, tq, dq), q_idx), pl.BlockSpec((None, tk, dq), kv_idx),
                pl.BlockSpec((None, tk, dv), kv_idx), pl.BlockSpec((None, tq, dv), q_idx),
                pl.BlockSpec((None, tq, dv), q_idx), pl.BlockSpec((None, tq, LANES), q_idx)]
    args = [q, k, v, o, do, lse]
    out_specs = [pl.BlockSpec((None, grp, s, dq), lambda hk, j, g, ii: (hk, 0, 0, 0)),
                 pl.BlockSpec((None, tk, dq), kv_idx), pl.BlockSpec((None, tk, dv), kv_idx)]
    out_shape = [jax.ShapeDtypeStruct((nkv, grp, s, dq), F32), jax.ShapeDtypeStruct((nkv, s, dq), F32),
                 jax.ShapeDtypeStruct((nkv, s, dv), F32)]
    if has_kb:
        in_specs.append(pl.BlockSpec((None, 1, tk), lambda hk, j, g, ii: (hk, 0, j)))
        args.append(kbias)
    if has_ss:
        in_specs.append(pl.BlockSpec(memory_space=pltpu.SMEM))
        args.append(slopes_sinks)
        out_specs.append(pl.BlockSpec((None, grp, 1, LANES), lambda hk, j, g, ii: (hk, 0, 0, 0)))
        out_shape.append(jax.ShapeDtypeStruct((nkv, grp, 1, LANES), F32))
    return pl.pallas_call(
        body, name=name, grid=(nkv, nq, grp, nsteps), in_specs=in_specs, out_specs=out_specs,
        out_shape=out_shape, compiler_params=_cp(*["arbitrary"] * 4))(*args)


def _tri_fwd(t, nq):
    i = sum((t >= (r * (r + 1)) // 2).astype(jnp.int32) for r in range(1, nq))
    return i, t - (i * (i + 1)) // 2


def _tri_bwd(t, nq):
    j = sum((t >= r * nq - (r * (r - 1)) // 2).astype(jnp.int32) for r in range(1, nq))
    return j, j + t - (j * nq - (j * (j - 1)) // 2)


def _causal_scores(q, k, kb, *, scale, diag):
    s = lax.dot_general(q, k, NT, preferred_element_type=F32) * scale
    if kb is not None:
        s = s - kb
    if diag:
        r = lax.broadcasted_iota(jnp.int32, s.shape, 0)
        c = lax.broadcasted_iota(jnp.int32, s.shape, 1)
        s = jnp.where(r >= c, s, MASK_VALUE)
    return s


def _causal_fwd(q, k, v, *, scale, name, tq, hb=2, kbias=None):
    nh, s, dq = q.shape
    dv = v.shape[-1]
    nq = s // tq
    nsteps = (nq * (nq + 1)) // 2
    has_kb = kbias is not None

    def body(*refs):
        q_ref, k_ref, v_ref = refs[:3]
        kb_ref = refs[3] if has_kb else None
        o_ref, lse_ref, m_sc, l_sc, acc_sc = refs[3 + has_kb:]
        i, j = _tri_fwd(pl.program_id(1), nq)

        @pl.when(j == 0)
        def _():
            m_sc[...] = jnp.full_like(m_sc, MASK_VALUE)
            l_sc[...] = jnp.zeros_like(l_sc)
            acc_sc[...] = jnp.zeros_like(acc_sc)

        def step(diag):
            for u in range(hb):
                sc = _causal_scores(q_ref[u], k_ref[u], kb_ref[u] if has_kb else None, scale=scale, diag=diag)
                m_prev = m_sc[u]
                m_new = jnp.maximum(m_prev, jnp.max(sc, axis=-1, keepdims=True))
                alpha = jnp.exp(m_prev - m_new)
                pr = jnp.exp(sc - m_new)
                l_new = alpha * l_sc[u] + jnp.sum(pr, axis=-1, keepdims=True)
                acc = alpha * acc_sc[u] + jnp.dot(pr.astype(BF16), v_ref[u], preferred_element_type=F32)
                if diag:
                    o_ref[u] = (acc / l_new).astype(BF16)
                    lse_ref[u] = jnp.broadcast_to(m_new + jnp.log(l_new), (tq, LANES))
                else:
                    m_sc[u], l_sc[u], acc_sc[u] = m_new, l_new, acc

        pl.when(j < i)(functools.partial(step, False))
        pl.when(j == i)(functools.partial(step, True))

    def q_idx(hp, t):
        return (hp, _tri_fwd(t, nq)[0], 0)

    def kv_idx(hp, t):
        return (hp, _tri_fwd(t, nq)[1], 0)

    in_specs = [pl.BlockSpec((hb, tq, dq), q_idx), pl.BlockSpec((hb, tq, dq), kv_idx), pl.BlockSpec((hb, tq, dv), kv_idx)]
    args = [q, k, v]
    if has_kb:
        in_specs.append(pl.BlockSpec((hb, 1, tq), lambda hp, t: (hp, 0, _tri_fwd(t, nq)[1])))
        args.append(kbias)
    return pl.pallas_call(
        body, name=name, grid=(nh // hb, nsteps), in_specs=in_specs,
        out_specs=[pl.BlockSpec((hb, tq, dv), q_idx), pl.BlockSpec((hb, tq, LANES), q_idx)],
        out_shape=[jax.ShapeDtypeStruct((nh, s, dv), BF16), jax.ShapeDtypeStruct((nh, s, LANES), F32)],
        scratch_shapes=[pltpu.VMEM((hb, tq, 1), F32), pltpu.VMEM((hb, tq, 1), F32), pltpu.VMEM((hb, tq, dv), F32)],
        compiler_params=_cp("arbitrary", "arbitrary"))(*args)


def _causal_bwd(q, k, v, o, do, lse, *, scale, name, tq, hb=2, kbias=None):
    nh, s, dq = q.shape
    dv = v.shape[-1]
    nq = s // tq
    nsteps = (nq * (nq + 1)) // 2
    has_kb = kbias is not None

    def body(*refs):
        q_ref, k_ref, v_ref, o_ref, do_ref, lse_ref = refs[:6]
        kb_ref = refs[6] if has_kb else None
        dq_ref, dk_ref, dv_ref = refs[6 + has_kb:]
        t = pl.program_id(1)
        j, i = _tri_bwd(t, nq)

        @pl.when(t == 0)
        def _():
            dq_ref[...] = jnp.zeros_like(dq_ref)

        def step(diag):
            rows = pl.ds(pl.multiple_of(i * tq, tq), tq)
            for u in range(hb):
                qv, kv, dov = q_ref[u], k_ref[u], do_ref[u]
                sc = _causal_scores(qv, kv, kb_ref[u] if has_kb else None, scale=scale, diag=diag)
                pr = jnp.exp(sc - lse_ref[u][:, :1])
                dp = lax.dot_general(dov, v_ref[u], NT, preferred_element_type=F32)
                delta = jnp.sum(dov.astype(F32) * o_ref[u].astype(F32), axis=-1, keepdims=True)
                dsb = ((pr * (dp - delta)) * scale).astype(BF16)
                d_v = lax.dot_general(pr.astype(BF16), dov, TN, preferred_element_type=F32)
                d_k = lax.dot_general(dsb, qv, TN, preferred_element_type=F32)
                if diag:
                    dv_ref[u], dk_ref[u] = d_v, d_k
                else:
                    dv_ref[u] += d_v
                    dk_ref[u] += d_k
                dq_ref[u, rows, :] += jnp.dot(dsb, kv, preferred_element_type=F32)

        pl.when(i > j)(functools.partial(step, False))
        pl.when(i == j)(functools.partial(step, True))

    def q_idx(hp, t):
        return (hp, _tri_bwd(t, nq)[1], 0)

    def kv_idx(hp, t):
        return (hp, _tri_bwd(t, nq)[0], 0)

    in_specs = [pl.BlockSpec((hb, tq, dq), q_idx), pl.BlockSpec((hb, tq, dq), kv_idx), pl.BlockSpec((hb, tq, dv), kv_idx),
                pl.BlockSpec((hb, tq, dv), q_idx), pl.BlockSpec((hb, tq, dv), q_idx), pl.BlockSpec((hb, tq, LANES), q_idx)]
    args = [q, k, v, o, do, lse]
    if has_kb:
        in_specs.append(pl.BlockSpec((hb, 1, tq), lambda hp, t: (hp, 0, _tri_bwd(t, nq)[0])))
        args.append(kbias)
    return pl.pallas_call(
        body, name=name, grid=(nh // hb, nsteps), in_specs=in_specs,
        out_specs=[pl.BlockSpec((hb, s, dq), lambda hp, t: (hp, 0, 0)), pl.BlockSpec((hb, tq, dq), kv_idx),
                   pl.BlockSpec((hb, tq, dv), kv_idx)],
        out_shape=[jax.ShapeDtypeStruct((nh, s, dq), F32), jax.ShapeDtypeStruct((nh, s, dq), F32),
                   jax.ShapeDtypeStruct((nh, s, dv), F32)],
        compiler_params=_cp("arbitrary", "arbitrary"))(*args)


def _adamw(w, g, m, v, *, name):
    shape = w.shape
    cols = shape[-1]
    rows = int(np.prod(shape[:-1])) if len(shape) > 1 else 1
    tr = _row_tile(rows, cols)
    c1 = 1.0 - ADAM_B1 ** ADAM_STEP
    c2 = 1.0 - ADAM_B2 ** ADAM_STEP

    def body(w_ref, g_ref, m_ref, v_ref, d_ref, mo_ref, vo_ref):
        gv = g_ref[...]
        mn = ADAM_B1 * m_ref[...] + (1.0 - ADAM_B1) * gv
        vn = ADAM_B2 * v_ref[...] + (1.0 - ADAM_B2) * (gv * gv)
        mo_ref[...] = mn
        vo_ref[...] = vn
        d_ref[...] = -ADAM_LR * ((mn / c1) / (jnp.sqrt(vn / c2) + ADAM_EPS) + ADAM_WD * w_ref[...])

    blk = pl.BlockSpec((tr, cols), lambda i: (i, 0))
    outs = pl.pallas_call(
        body, name=name, grid=(rows // tr,), in_specs=[blk] * 4, out_specs=[blk] * 3,
        out_shape=[jax.ShapeDtypeStruct((rows, cols), F32)] * 3,
        compiler_params=_cp("arbitrary"))(*[a.reshape(rows, cols) for a in (w, g, m, v)])
    return tuple(a.reshape(shape) for a in outs)


def _hbm_spec():
    return pl.BlockSpec(memory_space=pl.ANY)


def _mesh_place():
    x, y, c = lax.axis_index("x"), lax.axis_index("y"), lax.axis_index("c")
    return x, y, c, [(1 - x, y), (x, 1 - y), (1 - x, 1 - y)]


def _half_rows(c, rows, align):
    return pl.ds(pl.multiple_of(c * (rows // 2), align), rows // 2)


def _part(ref, mode, k, n, rows=None):
    if mode == "cols":
        cols = pl.ds(pl.multiple_of(k * n, LANES), n)
        return ref.at[:, cols] if rows is None else ref.at[rows, cols]
    return ref.at[k] if rows is None else ref.at[k, rows, :]


def _gather_weights(shards, modes, *, name):
    n_arr = len(shards)
    out_shape = [jax.ShapeDtypeStruct((s.shape[0], N_CHIPS * s.shape[1]) if m == "cols" else (N_CHIPS,) + s.shape, s.dtype)
                 for s, m in zip(shards, modes)]
    per = 7

    def body(*refs):
        srcs, dsts = refs[:n_arr], refs[n_arr:2 * n_arr]
        send_sems, recv_sems = refs[2 * n_arr:]
        x, y, c, chips = _mesh_place()
        me = 2 * x + y
        sends = []

        def copy(i, slot, src, dst, to):
            return pltpu.make_async_remote_copy(src_ref=src, dst_ref=dst, send_sem=send_sems.at[i * per + slot],
                                                recv_sem=recv_sems.at[i * per + slot], device_id=to, device_id_type=MESH)

        def half(i, k, hc, ref=None):
            r, n = shards[i].shape
            return _part(dsts[i] if ref is None else ref, modes[i], k, n, _half_rows(hc, r, 16))

        for i in range(n_arr):
            r, n = shards[i].shape
            mine = srcs[i].at[_half_rows(c, r, 16)]
            cps = [copy(i, 0, srcs[i], _part(dsts[i], modes[i], me, n), (x, y, 1 - c))]
            cps += [copy(i, 1 + j, mine, half(i, me, c), (px, py, c)) for j, (px, py) in enumerate(chips)]
            for cp in cps:
                cp.start()
            sends += cps
        for i in range(n_arr):
            for j, (px, py) in enumerate(chips):
                k = 2 * px + py
                copy(i, 1 + j, half(i, k, c), half(i, k, c), (px, py, c)).wait_recv()
                fwd = copy(i, 4 + j, half(i, k, c), half(i, k, c), (x, y, 1 - c))
                fwd.start()
                sends.append(fwd)
        for i in range(n_arr):
            n = shards[i].shape[1]
            own = _part(dsts[i], modes[i], me, n)
            copy(i, 0, own, own, (x, y, 1 - c)).wait_recv()
            for j, (px, py) in enumerate(chips):
                k = 2 * px + py
                copy(i, 4 + j, half(i, k, 1 - c), half(i, k, 1 - c), (x, y, 1 - c)).wait_recv()
        for cp in sends:
            cp.wait_send()

    return pl.pallas_call(
        body, name=name, in_specs=[_hbm_spec()] * n_arr, out_specs=[_hbm_spec()] * n_arr, out_shape=out_shape,
        scratch_shapes=[pltpu.SemaphoreType.DMA((n_arr * per,)), pltpu.SemaphoreType.DMA((n_arr * per,))])(*shards)


def _blk_view(a, mode):
    return a[None] if mode == "cols" else a


def _rs_pair_swap(arrs, modes, *, name):
    n_arr = len(arrs)
    out_shape = [jax.ShapeDtypeStruct((a.shape[0] // 2, a.shape[1]) if m == "cols" else (a.shape[0], a.shape[1] // 2, a.shape[2]), a.dtype)
                 for a, m in zip(arrs, modes)]

    def body(*refs):
        srcs, dsts = refs[:n_arr], refs[n_arr:2 * n_arr]
        send_sems, recv_sems = refs[2 * n_arr:]
        x, y, c, _ = _mesh_place()
        cps = []
        for i in range(n_arr):
            if modes[i] == "cols":
                src = srcs[i].at[_half_rows(1 - c, arrs[i].shape[0], 8)]
            else:
                src = srcs[i].at[:, _half_rows(1 - c, arrs[i].shape[1], 8), :]
            cps.append(pltpu.make_async_remote_copy(src_ref=src, dst_ref=dsts[i], send_sem=send_sems.at[i],
                                                    recv_sem=recv_sems.at[i], device_id=(x, y, 1 - c), device_id_type=MESH))
        for cp in cps:
            cp.start()
        for cp in cps:
            cp.wait()

    return pl.pallas_call(
        body, name=name, in_specs=[_hbm_spec()] * n_arr, out_specs=[_hbm_spec()] * n_arr, out_shape=out_shape,
        scratch_shapes=[pltpu.SemaphoreType.DMA((n_arr,)), pltpu.SemaphoreType.DMA((n_arr,))])(*arrs)


def _rs_pair_add(arr, landed, place, *, name):
    nb, r, c = arr.shape
    rh = r // 2
    tr = _row_tile(rh, c)
    nt = rh // tr

    def body(p_ref, a_ref, l_ref, o_ref):
        o_ref[...] = (a_ref[...] + l_ref[...]).astype(BF16)

    grid_spec = pltpu.PrefetchScalarGridSpec(
        num_scalar_prefetch=1, grid=(nb, nt),
        in_specs=[pl.BlockSpec((None, tr, c), lambda b, t, p_ref: (b, p_ref[1] * nt + t, 0)),
                  pl.BlockSpec((None, tr, c), lambda b, t, p_ref: (b, t, 0))],
        out_specs=pl.BlockSpec((None, tr, c), lambda b, t, p_ref: (b, t, 0)))
    return pl.pallas_call(
        body, name=name, grid_spec=grid_spec, out_shape=jax.ShapeDtypeStruct((nb, rh, c), BF16),
        compiler_params=_cp("arbitrary", "arbitrary"))(place, arr, landed)


def _rs_chip_exchange(parts, modes, *, name):
    n_arr = len(parts)
    out_shape = []
    for a, m in zip(parts, modes):
        shp = (a.shape[0], a.shape[1] // N_CHIPS) if m == "cols" else a.shape[1:]
        out_shape.append(jax.ShapeDtypeStruct((3,) + shp, a.dtype))

    def body(*refs):
        srcs, dsts = refs[:n_arr], refs[n_arr:2 * n_arr]
        send_sems, recv_sems = refs[2 * n_arr:]
        x, y, c, chips = _mesh_place()
        cps = []
        for i in range(n_arr):
            n = out_shape[i].shape[-1]
            for j, (px, py) in enumerate(chips):
                cps.append(pltpu.make_async_remote_copy(
                    src_ref=_part(srcs[i], modes[i], 2 * px + py, n), dst_ref=dsts[i].at[j],
                    send_sem=send_sems.at[3 * i + j], recv_sem=recv_sems.at[3 * i + j],
                    device_id=(px, py, c), device_id_type=MESH))
        for cp in cps:
            cp.start()
        for cp in cps:
            cp.wait()

    return pl.pallas_call(
        body, name=name, in_specs=[_hbm_spec()] * n_arr, out_specs=[_hbm_spec()] * n_arr, out_shape=out_shape,
        scratch_shapes=[pltpu.SemaphoreType.DMA((3 * n_arr,)), pltpu.SemaphoreType.DMA((3 * n_arr,))])(*parts)


def _rs_chip_sum(part, landed, mode, place, *, name):
    _, rh, n = landed.shape
    tr = _row_tile(rh, n)
    nt = rh // tr

    def body(p_ref, a_ref, l_ref, o_ref):
        o_ref[...] = ((a_ref[...].astype(F32) + l_ref[0].astype(F32)) + l_ref[1].astype(F32)) + l_ref[2].astype(F32)

    if mode == "cols":
        own = pl.BlockSpec((tr, n), lambda t, p_ref: (t, p_ref[0]))
    else:
        own = pl.BlockSpec((None, tr, n), lambda t, p_ref: (p_ref[0], t, 0))
    grid_spec = pltpu.PrefetchScalarGridSpec(
        num_scalar_prefetch=1, grid=(nt,),
        in_specs=[own, pl.BlockSpec((3, tr, n), lambda t, p_ref: (0, t, 0))],
        out_specs=pl.BlockSpec((tr, n), lambda t, p_ref: (p_ref[1] * nt + t, 0)))
    return pl.pallas_call(
        body, name=name, grid_spec=grid_spec, out_shape=jax.ShapeDtypeStruct((2 * rh, n), F32),
        compiler_params=_cp("arbitrary"))(place, part, landed)


def _rs_pair_join(halves, *, name):
    n_arr = len(halves)

    def body(*refs):
        outs = refs[n_arr:2 * n_arr]
        send_sems, recv_sems = refs[2 * n_arr:]
        x, y, c, _ = _mesh_place()
        cps = []
        for i in range(n_arr):
            rows = _half_rows(c, halves[i].shape[0], 8)
            cps.append(pltpu.make_async_remote_copy(src_ref=outs[i].at[rows], dst_ref=outs[i].at[rows], send_sem=send_sems.at[i],
                                                    recv_sem=recv_sems.at[i], device_id=(x, y, 1 - c), device_id_type=MESH))
        for cp in cps:
            cp.start()
        for i, cp in enumerate(cps):
            cp.wait_send()
            theirs = outs[i].at[_half_rows(1 - c, halves[i].shape[0], 8)]
            pltpu.make_async_remote_copy(src_ref=theirs, dst_ref=theirs, send_sem=send_sems.at[i], recv_sem=recv_sems.at[i],
                                         device_id=(x, y, 1 - c), device_id_type=MESH).wait_recv()

    return pl.pallas_call(
        body, name=name, in_specs=[_hbm_spec()] * n_arr, out_specs=[_hbm_spec()] * n_arr,
        out_shape=[jax.ShapeDtypeStruct(h.shape, h.dtype) for h in halves],
        input_output_aliases={i: i for i in range(n_arr)},
        scratch_shapes=[pltpu.SemaphoreType.DMA((n_arr,)), pltpu.SemaphoreType.DMA((n_arr,))])(*halves)


def _allreduce_small(v, *, name):
    r, c = v.shape

    def body(v_ref, o_ref, gath, send_sems, recv_sems):
        x, y, cc, _ = _mesh_place()
        me = 4 * x + 2 * y + cc
        gath[me] = v_ref[...]
        cps = []
        for rel in range(1, 8):
            px = 1 - x if rel & 4 else x
            py = 1 - y if rel & 2 else y
            pc = 1 - cc if rel & 1 else cc

            def copy(slot, px=px, py=py, pc=pc, rel=rel):
                return pltpu.make_async_remote_copy(
                    src_ref=v_ref, dst_ref=gath.at[slot], send_sem=send_sems.at[rel - 1],
                    recv_sem=recv_sems.at[rel - 1], device_id=(px, py, pc), device_id_type=MESH)

            cps.append((copy(me), copy(4 * px + 2 * py + pc)))
        for send, _ in cps:
            send.start()
        for send, theirs in cps:
            theirs.wait_recv()
            send.wait_send()
        tot = gath[0]
        for d in range(1, 8):
            tot = tot + gath[d]
        o_ref[...] = tot

    vm = pl.BlockSpec(memory_space=pltpu.VMEM)
    return pl.pallas_call(
        body, name=name, in_specs=[vm], out_specs=vm, out_shape=jax.ShapeDtypeStruct((r, c), F32),
        scratch_shapes=[pltpu.VMEM((8, r, c), F32), pltpu.SemaphoreType.DMA((7,)), pltpu.SemaphoreType.DMA((7,))])(v)


def _to_heads(a, nh, dh, dtype=BF16):
    return a.reshape(a.shape[0], nh, dh).transpose(1, 0, 2).astype(dtype)


def _from_heads(a):
    return a.transpose(1, 0, 2).reshape(a.shape[1], -1)


def _rope_tables(s, reps):
    half = B_ROPE // 2
    inv = ROPE_THETA ** (-jnp.arange(0, B_ROPE, 2, dtype=F32) / B_ROPE)
    ang = jnp.arange(s, dtype=F32)[:, None] * inv[None, :]
    return jnp.tile(jnp.cos(ang), (1, reps)), jnp.tile(jnp.sin(ang), (1, reps))


def _alibi_slopes():
    return 2.0 ** (-8.0 * jnp.arange(1, A_HEADS + 1, dtype=F32) / A_HEADS)


def _ffn_fwd(h, norm, wts, tag):
    gate, up, act, xn = _ffn_up(h, norm, wts["wgu"], name=f"{tag}_up")
    out = _mm_res_fwd(act, wts["wd"], h, scale=FFN_RES_SCALE, name=f"{tag}_down")
    return out, dict(h_in=h, gate=gate, up=up, act=act, xn=xn)


def _ffn_bwd(dh, norm, wts, sv, tag):
    dgate, dup = _ffn_down_bwd(dh, wts["wd"], sv["gate"], sv["up"], scale=FFN_RES_SCALE, name=f"{tag}_down_bwd")
    d_wd = _mm_tn(sv["act"], dh, b_scale=FFN_RES_SCALE, name=f"{tag}_dwd")
    d_wgu = _mm_tn(sv["xn"], [dgate, dup], name=f"{tag}_dwgu")
    dh_in, dnorm = _mm_nt_rmsbwd([(dgate, wts["wgu"], 0), (dup, wts["wgu"], 1)], sv["h_in"], norm, dh,
                                 name=f"{tag}_dx")
    return dh_in, dnorm, d_wgu, d_wd


def _even_weights(w_in, w_uq, w_ukv):
    half = B_ROPE // 2
    base = w_in.shape[1]
    kr1, kr2 = w_in[:, base - B_ROPE:base - half], w_in[:, base - half:]
    w_in_cat = jnp.concatenate([w_in, -kr2, kr1, jnp.zeros((w_in.shape[0], 64), w_in.dtype)], axis=1)
    u3 = w_uq.reshape(w_uq.shape[0], B_HEADS, B_NOPE + B_ROPE)
    nope = u3[:, :, :B_NOPE].reshape(w_uq.shape[0], -1)
    r1 = u3[:, :, B_NOPE:B_NOPE + half].reshape(w_uq.shape[0], -1)
    r2 = u3[:, :, B_NOPE + half:].reshape(w_uq.shape[0], -1)
    w_q_cat = jnp.concatenate([nope, r1, r2, -r2, r1], axis=1)
    return w_in_cat, w_q_cat, w_ukv


def _even_fwd(h, w, i):
    s = h.shape[0]
    half = B_ROPE // 2
    ycat, xn = _rms_mm_fwd(h, w["mix_norm"][i:i + 1], w["ev_in_cat"], name="ev_in")
    a_q, a_k, a_v = ycat[:, :512], ycat[:, 512:640], ycat[:, 640:768]
    c_q, c_kv = ycat[:, 768:1024], ycat[:, 1024:1152]
    cos32, sin32 = _rope_tables(s, 2)
    kro = _rope_fwd(ycat[:, 1152:1184], ycat[:, 1184:1216], cos32, sin32, name="ev_k_rope")
    qa, ka, va = _to_heads(a_q, A_HEADS, A_HEAD_DIM), _to_heads(a_k, A_KV_HEADS, A_HEAD_DIM), _to_heads(a_v, A_KV_HEADS, A_HEAD_DIM)
    ss = jnp.stack([_alibi_slopes(), w["ev_sinks"].reshape(-1)])
    oa, lse_a = _flash_fwd(qa, ka, va, scale=A_HEAD_DIM ** -0.5, name="swa_fwd", tq=256, window=WINDOW, slopes_sinks=ss)
    yq, xn_q = _rms_mm_fwd(c_q, w["ev_cq_norm"], w["ev_q_cat"], name="ev_q_up")
    cos256, sin256 = _rope_tables(s, 2 * B_HEADS)
    qro = _rope_fwd(yq[:, 512:768], yq[:, 768:1024], cos256, sin256, name="ev_q_rope")
    ykv, xn_kv = _rms_mm_fwd(c_kv, w["ev_ckv_norm"], w["ev_ukv"], name="ev_kv_up")
    zq = jnp.zeros((s, B_HEADS, LANES - B_NOPE - B_ROPE), F32)
    qb = jnp.concatenate([yq[:, :512].reshape(s, B_HEADS, B_NOPE), qro[:, :128].reshape(s, B_HEADS, half),
                          qro[:, 128:].reshape(s, B_HEADS, half), zq], axis=-1).transpose(1, 0, 2).astype(BF16)
    kv3 = ykv.reshape(s, B_HEADS, B_NOPE + B_V)
    kb = jnp.concatenate([kv3[:, :, :B_NOPE], jnp.broadcast_to(kro[:, None, :], (s, B_HEADS, B_ROPE)), zq],
                         axis=-1).transpose(1, 0, 2).astype(BF16)
    vb = kv3[:, :, B_NOPE:].transpose(1, 0, 2).astype(BF16)
    ob, lse_b = _causal_fwd(qb, kb, vb, scale=(B_NOPE + B_ROPE) ** -0.5, name="mla_fwd", tq=512)
    attn = jnp.concatenate([_from_heads(oa), _from_heads(ob)], axis=-1)
    out = _mm_res_fwd(attn, w["ev_out"], h, scale=1.0, name="ev_out")
    sv = dict(h_in=h, xn=xn, c_q=c_q, c_kv=c_kv, xn_q=xn_q, xn_kv=xn_kv, qa=qa, ka=ka, va=va, oa=oa, lse_a=lse_a,
              ss=ss, qb=qb, kb=kb, vb=vb, ob=ob, lse_b=lse_b, attn=attn, cos32=cos32, sin32=sin32,
              cos256=cos256, sin256=sin256)
    return out, sv


def _even_bwd(dh, w, sv, i):
    s = dh.shape[0]
    half = B_ROPE // 2
    g = {}
    dattn = _mm_nt(dh, w["ev_out"], name="ev_out_dx")
    g["ev_w_out"] = _mm_tn(sv["attn"], dh, name="ev_out_dw")
    doa, dob = _to_heads(dattn[:, :512], A_HEADS, A_HEAD_DIM), _to_heads(dattn[:, 512:], B_HEADS, B_V)
    dqa, dka, dva, dsink = _flash_bwd(sv["qa"], sv["ka"], sv["va"], sv["oa"], doa, sv["lse_a"], scale=A_HEAD_DIM ** -0.5,
                                      name="swa_bwd", tq=256, window=WINDOW, slopes_sinks=sv["ss"])
    g["ev_sinks"] = dsink[:, :, 0, 0].reshape(1, A_HEADS)
    dqb, dkb, dvb = _causal_bwd(sv["qb"], sv["kb"], sv["vb"], sv["ob"], dob, sv["lse_b"],
                                scale=(B_NOPE + B_ROPE) ** -0.5, name="mla_bwd", tq=512)
    dq_r1 = dqb[:, :, B_NOPE:B_NOPE + half].transpose(1, 0, 2).reshape(s, -1)
    dq_r2 = dqb[:, :, B_NOPE + half:B_NOPE + B_ROPE].transpose(1, 0, 2).reshape(s, -1)
    dq1, dq2 = _rope_bwd(jnp.concatenate([dq_r1, dq_r2], axis=-1)[None], sv["cos256"], sv["sin256"], name="ev_q_rope_bwd")
    dyq = jnp.concatenate([_from_heads(dqb[:, :, :B_NOPE]), dq1, dq2], axis=-1)
    dwq = _mm_tn(sv["xn_q"], dyq, name="ev_q_up_dw")
    dcq, g["ev_cq_norm"] = _mm_nt_rmsbwd([(dyq, w["ev_q_cat"])], sv["c_q"], w["ev_cq_norm"], None, name="ev_q_up_dx")
    kq = sv["c_q"].shape[1]
    d_nope = dwq[:, :512].reshape(kq, B_HEADS, B_NOPE)
    d_r1 = (dwq[:, 512:640] + dwq[:, 896:1024]).reshape(kq, B_HEADS, half)
    d_r2 = (dwq[:, 640:768] - dwq[:, 768:896]).reshape(kq, B_HEADS, half)
    g["ev_w_uq"] = jnp.concatenate([d_nope, d_r1, d_r2], axis=-1).reshape(kq, -1)
    dykv = jnp.concatenate([dkb[:, :, :B_NOPE].transpose(1, 0, 2), dvb.transpose(1, 0, 2)], axis=-1).reshape(s, -1)
    g["ev_w_ukv"] = _mm_tn(sv["xn_kv"], dykv, name="ev_kv_up_dw")
    dckv, g["ev_ckv_norm"] = _mm_nt_rmsbwd([(dykv, w["ev_ukv"])], sv["c_kv"], w["ev_ckv_norm"], None, name="ev_kv_up_dx")
    dk1, dk2 = _rope_bwd(dkb[:, :, B_NOPE:B_NOPE + B_ROPE], sv["cos32"], sv["sin32"], name="ev_k_rope_bwd")
    dycat = jnp.concatenate([_from_heads(dqa.reshape(A_HEADS, s, A_HEAD_DIM)), _from_heads(dka), _from_heads(dva),
                             dcq, dckv, dk1, dk2, jnp.zeros((s, 64), F32)], axis=-1)
    dwin = _mm_tn(sv["xn"], dycat, name="ev_in_dw")
    base = 1184
    g["ev_w_in"] = jnp.concatenate([dwin[:, :base - B_ROPE],
                                    dwin[:, base - B_ROPE:base - half] + dwin[:, base + half:base + B_ROPE],
                                    dwin[:, base - half:base] - dwin[:, base:base + half]], axis=-1)
    dh_in, dnorm = _mm_nt_rmsbwd([(dycat, w["ev_in_cat"])], sv["h_in"], w["mix_norm"][i:i + 1], dh, name="ev_in_dx")
    return dh_in, dnorm, g


def _odd_fwd(h, w, i):
    s = h.shape[0]
    wd = C_HEADS * C_HEAD_DIM
    y, xn = _rms_mm_fwd(h, w["mix_norm"][i:i + 1], w["od_in_pad"], name="od_in")
    def widen(a, one_at):
        pad = jnp.zeros((s, C_HEADS, LANES - C_HEAD_DIM), F32).at[:, :, one_at].set(1.0)
        return jnp.concatenate([a.reshape(s, C_HEADS, C_HEAD_DIM), pad], axis=-1).transpose(1, 0, 2).astype(BF16)

    q, k = widen(y[:, :wd], 0), widen(y[:, wd:2 * wd], 1)
    v = _to_heads(y[:, 2 * wd:3 * wd], C_HEADS, C_HEAD_DIM)
    ft = y[:, 3 * wd:3 * wd + C_HEADS].T
    bf = w["od_b_f"].reshape(C_HEADS, 1)
    logc = _fox_gate_fwd(ft, bf, name="fox_gate_fwd").reshape(C_HEADS, 1, s)
    o, lse = _causal_fwd(q, k, v, scale=C_HEAD_DIM ** -0.5, name="fox_fwd", tq=512, kbias=logc)
    attn = _from_heads(o)
    out = _mm_res_fwd(attn, w["od_out"], h, scale=1.0, name="od_out")
    return out, dict(h_in=h, xn=xn, q=q, k=k, v=v, o=o, lse=lse, logc=logc, ft=ft, bf=bf, attn=attn)


def _odd_bwd(dh, w, sv, i):
    s = dh.shape[0]
    g = {}
    dattn = _mm_nt(dh, w["od_out"], name="od_out_dx")
    g["od_w_out"] = _mm_tn(sv["attn"], dh, name="od_out_dw")
    do = _to_heads(dattn, C_HEADS, C_HEAD_DIM)
    scale = C_HEAD_DIM ** -0.5
    dq, dk, dv = _causal_bwd(sv["q"], sv["k"], sv["v"], sv["o"], do, sv["lse"], scale=scale,
                             name="fox_bwd", tq=512, kbias=sv["logc"])
    dft, dbf = _fox_gate_bwd(dq[:, :, C_HEAD_DIM + 1], dk[:, :, C_HEAD_DIM], sv["ft"], sv["bf"],
                             inv_scale=1.0 / scale, name="fox_gate_bwd")
    dq, dk = dq[:, :, :C_HEAD_DIM], dk[:, :, :C_HEAD_DIM]
    g["od_b_f"] = dbf.reshape(1, C_HEADS)
    n_pad = w["od_in_pad"].shape[1]
    n_real = 3 * C_HEADS * C_HEAD_DIM + C_HEADS
    dy = jnp.concatenate([_from_heads(dq), _from_heads(dk), _from_heads(dv), dft.T,
                          jnp.zeros((s, n_pad - n_real), F32)], axis=-1)
    g["od_w_in"] = _mm_tn(sv["xn"], dy, name="od_in_dw")[:, :n_real]
    dh_in, dnorm = _mm_nt_rmsbwd([(dy, w["od_in_pad"])], sv["h_in"], w["mix_norm"][i:i + 1], dh, name="od_in_dx")
    return dh_in, dnorm, g


def _kernel_weights(full, replicated):
    depth = len(full["ffa_w_down"])
    w = dict(replicated)
    for tag in ("ffa", "ffb"):
        w[tag] = [dict(wgu=full[tag + "_w_gate_up"][i], wd=full[tag + "_w_down"][i]) for i in range(depth)]
    w["ple_gate"], w["ple_proj"] = full["ple_w_gate"], full["ple_w_proj"]
    w["ev_in_cat"], w["ev_q_cat"], w["ev_ukv"] = _even_weights(full["ev_w_in"][0], full["ev_w_uq"][0], full["ev_w_ukv"][0])
    w["ev_out"], w["od_out"] = full["ev_w_out"][0], full["od_w_out"][0]
    od_in = full["od_w_in"][0]
    w["od_in_pad"] = jnp.pad(od_in, ((0, 0), (0, (-od_in.shape[1]) % LANES)))
    return w


def _local_step(x, p, tgt, w):
    depth = p.shape[0]
    h = x
    saved = []
    for i in range(depth):
        sv = {}
        h, sv["ffa"] = _ffn_fwd(h, w["ffa_norm"][i:i + 1], w["ffa"][i], f"ffa{i}")
        if i % 2 == 0:
            h, sv["mix"] = _even_fwd(h, w, i)
        else:
            h, sv["mix"] = _odd_fwd(h, w, i)
        h, sv["ffb"] = _ffn_fwd(h, w["ffb_norm"][i:i + 1], w["ffb"][i], f"ffb{i}")
        h_in = h
        h, xn, gate, pp = _ple_fwd(h, w["ple_norm"][i:i + 1], w["ple_gate"][i], p[i], w["ple_proj"][i], name=f"ple{i}")
        sv["ple"] = dict(h_in=h_in, xn=xn, gate=gate, pp=pp)
        saved.append(sv)
    loss_vec, dh, d_final = _final_loss(h, w["final_norm"].reshape(1, -1), tgt, name="final_loss")

    per_layer = [dict() for _ in range(depth)]
    grads = {}
    for i in reversed(range(depth)):
        sv, gl = saved[i], per_layer[i]
        dz, dpp = _ple_bwd_elem(dh, sv["ple"]["gate"], sv["ple"]["pp"], name=f"ple{i}_bwd")
        gl["ple_w_gate"] = _mm_tn(sv["ple"]["xn"], dz, name=f"ple{i}_dwg")
        gl["ple_w_proj"] = _mm_tn(p[i], dpp, name=f"ple{i}_dwp")
        dh, gl["ple_norm"] = _mm_nt_rmsbwd([(dz, w["ple_gate"][i])], sv["ple"]["h_in"], w["ple_norm"][i:i + 1], dh,
                                           name=f"ple{i}_dx")
        dh, gl["ffb_norm"], gl["ffb_w_gate_up"], gl["ffb_w_down"] = _ffn_bwd(dh, w["ffb_norm"][i:i + 1], w["ffb"][i], sv["ffb"], f"ffb{i}")
        if i % 2 == 0:
            dh, gl["mix_norm"], gm = _even_bwd(dh, w, sv["mix"], i)
        else:
            dh, gl["mix_norm"], gm = _odd_bwd(dh, w, sv["mix"], i)
        grads.update({n: (g if n in REPLICATED else [g]) for n, g in gm.items()})
        dh, gl["ffa_norm"], gl["ffa_w_gate_up"], gl["ffa_w_down"] = _ffn_bwd(dh, w["ffa_norm"][i:i + 1], w["ffa"][i], sv["ffa"], f"ffa{i}")
    grads["final_norm"] = d_final.reshape(-1)
    for n in ("ffa_norm", "mix_norm", "ffb_norm", "ple_norm"):
        grads[n] = jnp.concatenate([per_layer[i][n] for i in range(depth)], axis=0)
    for n in ("ffa_w_gate_up", "ffa_w_down", "ffb_w_gate_up", "ffb_w_down", "ple_w_gate", "ple_w_proj"):
        grads[n] = [per_layer[i][n] for i in range(depth)]
    return loss_vec[0, 0], dh, grads


def _cut_mode(local_shape, axis, ncols):
    return "cols" if axis == 2 and ncols % LANES == 0 else "blk"


def _small_rows(vals):
    rows = []
    for n in REPLICATED:
        v = vals[n].reshape(-1)
        rows.append(jnp.pad(v, (0, (-v.shape[0]) % FLAT_COLS)).reshape(-1, FLAT_COLS))
    out = jnp.concatenate(rows, axis=0)
    return jnp.pad(out, ((0, (-out.shape[0]) % 8), (0, 0)))


def kernel(x, p, ffa_norm, ffa_w_gate_up, ffa_w_down, mix_norm, ffb_norm, ffb_w_gate_up, ffb_w_down, ple_norm, ple_w_gate, ple_w_proj, ev_w_in, ev_sinks, ev_cq_norm, ev_w_uq, ev_ckv_norm, ev_w_ukv, ev_w_out, od_w_in, od_b_f, od_w_out, final_norm, loss_target, m_ffa_norm, m_ffa_w_gate_up, m_ffa_w_down, m_mix_norm, m_ffb_norm, m_ffb_w_gate_up, m_ffb_w_down, m_ple_norm, m_ple_w_gate, m_ple_w_proj, m_ev_w_in, m_ev_sinks, m_ev_cq_norm, m_ev_w_uq, m_ev_ckv_norm, m_ev_w_ukv, m_ev_w_out, m_od_w_in, m_od_b_f, m_od_w_out, m_final_norm, v_ffa_norm, v_ffa_w_gate_up, v_ffa_w_down, v_mix_norm, v_ffb_norm, v_ffb_w_gate_up, v_ffb_w_down, v_ple_norm, v_ple_w_gate, v_ple_w_proj, v_ev_w_in, v_ev_sinks, v_ev_cq_norm, v_ev_w_uq, v_ev_ckv_norm, v_ev_w_ukv, v_ev_w_out, v_od_w_in, v_od_b_f, v_od_w_out, v_final_norm):
    env = dict(locals())
    wts = {n: env[n] for n in WEIGHT_ORDER}
    mom1 = {n: env["m_" + n] for n in WEIGHT_ORDER}
    mom2 = {n: env["v_" + n] for n in WEIGHT_ORDER}
    place = jnp.stack([2 * lax.axis_index("x") + lax.axis_index("y"), lax.axis_index("c")]).astype(jnp.int32)

    plan, shards = [], []
    for n, axis in SHARDED:
        wb = wts[n].astype(BF16)
        mode = _cut_mode(wb.shape, axis, wb.shape[2])
        for i in range(wb.shape[0]):
            plan.append((n, i, mode, axis))
            shards.append(wb[i])
    modes = [m for _, _, m, _ in plan]
    gathered = _gather_weights(shards, modes, name="weight_allgather")
    full = {n: [] for n, _ in SHARDED}
    for (n, i, mode, axis), dst in zip(plan, gathered):
        if mode == "blk":
            dst = dst.reshape(-1, dst.shape[2]) if axis == 1 else jnp.moveaxis(dst, 0, 1).reshape(dst.shape[1], -1)
        full[n].append(dst)

    w = _kernel_weights(full, {n: wts[n] for n in REPLICATED})
    loss_part, grad_x, grads = _local_step(x[0], p[:, 0], loss_target[0], w)
    loss = lax.psum(loss_part, ("x", "y", "c"))

    arrs = []
    for n, i, mode, axis in plan:
        g2 = grads[n][i]
        if mode == "blk":
            rr, cc = wts[n].shape[1:]
            g2 = g2.reshape(N_CHIPS, rr, cc) if axis == 1 else g2.reshape(rr, N_CHIPS, cc).transpose(1, 0, 2)
        arrs.append(g2)
    landed = _rs_pair_swap(arrs, modes, name="rs_pair_swap")
    parts = []
    for (n, i, mode, _), a, l in zip(plan, arrs, landed):
        pt = _rs_pair_add(_blk_view(a, mode), _blk_view(l, mode), place, name=f"rs_pair_add_{n}{i}")
        parts.append(pt[0] if mode == "cols" else pt)
    landed = _rs_chip_exchange(parts, modes, name="rs_chip_exchange")
    halves = [_rs_chip_sum(pt, l, mode, place, name=f"rs_chip_sum_{n}{i}")
              for (n, i, mode, _), pt, l in zip(plan, parts, landed)]
    reduced = _rs_pair_join(halves, name="rs_pair_join")
    gout = {n: [] for n, _ in SHARDED}
    for (n, _, _, _), r2 in zip(plan, reduced):
        gout[n].append(r2)
    gout = {n: jnp.stack(v).reshape(wts[n].shape) for n, v in gout.items()}
    small = _allreduce_small(_small_rows(grads), name="small_allreduce")
    r0 = 0
    for n in REPLICATED:
        size = int(np.prod(wts[n].shape))
        nr = -(-size // FLAT_COLS)
        gout[n] = small[r0:r0 + nr].reshape(-1)[:size].reshape(wts[n].shape)
        r0 += nr

    delta, new_m, new_v = {}, {}, {}
    for n in WEIGHT_ORDER:
        delta[n], new_m[n], new_v[n] = _adamw(wts[n], gout[n], mom1[n], mom2[n], name="adamw_" + n)
    return (loss, grad_x[None], *[gout[n] for n in WEIGHT_ORDER], *[delta[n] for n in WEIGHT_ORDER],
            *[new_m[n] for n in WEIGHT_ORDER], *[new_v[n] for n in WEIGHT_ORDER])
```

```python
import functools
import math

import numpy as np
import jax
import jax.numpy as jnp
from jax import lax
from jax.experimental import pallas as pl
from jax.experimental.pallas import tpu as pltpu

F32 = jnp.float32
BF16 = jnp.bfloat16
NT = (((1,), (1,)), ((), ()))
TN = (((0,), (0,)), ((), ()))
MESH = pl.DeviceIdType.MESH

RMS_EPS = 1e-6
FFN_RES_SCALE = 0.5
A_HEADS, A_KV_HEADS, A_HEAD_DIM, WINDOW = 8, 2, 64, 128
B_HEADS, B_Q_LORA, B_KV_LORA, B_NOPE, B_ROPE, B_V = 8, 256, 128, 64, 32, 64
ROPE_THETA = 10000.0
C_HEADS, C_HEAD_DIM = 16, 64
ADAM_LR, ADAM_B1, ADAM_B2, ADAM_EPS, ADAM_WD, ADAM_STEP = 0.001, 0.9, 0.999, 1e-08, 0.01, 10

N_CHIPS = 4
LANES = 128
FLAT_COLS = 1024
MASK_VALUE = -1e30
VMEM_LIMIT = 48 * 2**20

SHARDED = (
    ("ffa_w_gate_up", 2), ("ffa_w_down", 1), ("ffb_w_gate_up", 2), ("ffb_w_down", 1),
    ("ple_w_gate", 1), ("ple_w_proj", 2), ("ev_w_in", 2), ("ev_w_uq", 2), ("ev_w_ukv", 2),
    ("ev_w_out", 1), ("od_w_in", 2), ("od_w_out", 1))
REPLICATED = ("ffa_norm", "mix_norm", "ffb_norm", "ple_norm", "final_norm",
              "ev_sinks", "ev_cq_norm", "ev_ckv_norm", "od_b_f")
WEIGHT_ORDER = ("ffa_norm", "ffa_w_gate_up", "ffa_w_down", "mix_norm", "ffb_norm", "ffb_w_gate_up",
                "ffb_w_down", "ple_norm", "ple_w_gate", "ple_w_proj", "ev_w_in", "ev_sinks",
                "ev_cq_norm", "ev_w_uq", "ev_ckv_norm", "ev_w_ukv", "ev_w_out", "od_w_in", "od_b_f",
                "od_w_out", "final_norm")


def _cp(*sem):
    return pltpu.CompilerParams(dimension_semantics=sem, vmem_limit_bytes=VMEM_LIMIT)


def _sigmoid(z):
    return 1.0 / (1.0 + jnp.exp(-z))


def _rms_stats(xv):
    r = lax.rsqrt(jnp.mean(xv * xv, axis=-1, keepdims=True) + RMS_EPS)
    return r, xv * r


def _rms_bwd(dxn, xv, g):
    r, xhat = _rms_stats(xv)
    u = dxn * g
    dx = r * (u - xhat * jnp.mean(u * xhat, axis=-1, keepdims=True))
    return dx, dxn * xhat


def _col_tile(k_rows, n, budget_bytes=6 * 2**20):
    if k_rows * n * 4 <= budget_bytes or n % LANES:
        return n
    units = n // LANES
    best = LANES
    for d in range(1, units + 1):
        if units % d == 0 and k_rows * d * LANES * 4 <= budget_bytes:
            best = d * LANES
    return best


def _row_tile(rows, cols, target_elems=2**18):
    if rows * cols <= target_elems or rows % 8:
        return rows
    best = 8
    for d in range(8, rows + 1, 8):
        if rows % d == 0 and d * cols <= target_elems:
            best = d
    return best


def _rms_mm_fwd(x, g, w, *, name, tm=512):
    s, k = x.shape
    n = w.shape[1]

    def body(x_ref, g_ref, w_ref, y_ref, xn_ref):
        _, xhat = _rms_stats(x_ref[...])
        xn = (xhat * g_ref[...]).astype(BF16)
        xn_ref[...] = xn
        y_ref[...] = jnp.dot(xn, w_ref[...], preferred_element_type=F32)

    return pl.pallas_call(
        body, name=name, grid=(s // tm,),
        in_specs=[pl.BlockSpec((tm, k), lambda i: (i, 0)), pl.BlockSpec((1, k), lambda i: (0, 0)),
                  pl.BlockSpec((k, n), lambda i: (0, 0))],
        out_specs=[pl.BlockSpec((tm, n), lambda i: (i, 0)), pl.BlockSpec((tm, k), lambda i: (i, 0))],
        out_shape=[jax.ShapeDtypeStruct((s, n), F32), jax.ShapeDtypeStruct((s, k), BF16)],
        compiler_params=_cp("arbitrary"))(x, g, w)


def _ffn_up(x, g, wgu, *, name, tm=512):
    s, k = x.shape
    f = wgu.shape[1] // 2
    tn = _col_tile(k, f)
    nj = f // tn

    def body(x_ref, g_ref, wg_ref, wu_ref, gate_ref, up_ref, act_ref, xn_ref, xn_sc):
        @pl.when(pl.program_id(1) == 0)
        def _():
            _, xhat = _rms_stats(x_ref[...])
            xn = (xhat * g_ref[...]).astype(BF16)
            xn_sc[...] = xn
            xn_ref[...] = xn

        xn = xn_sc[...]
        gg = jnp.dot(xn, wg_ref[...], preferred_element_type=F32)
        uu = jnp.dot(xn, wu_ref[...], preferred_element_type=F32)
        gate_ref[...] = gg.astype(BF16)
        up_ref[...] = uu.astype(BF16)
        act_ref[...] = ((gg * _sigmoid(gg)) * uu).astype(BF16)

    tile = pl.BlockSpec((tm, tn), lambda i, j: (i, j))
    return pl.pallas_call(
        body, name=name, grid=(s // tm, nj),
        in_specs=[pl.BlockSpec((tm, k), lambda i, j: (i, 0)), pl.BlockSpec((1, k), lambda i, j: (0, 0)),
                  pl.BlockSpec((k, tn), lambda i, j: (0, j)), pl.BlockSpec((k, tn), lambda i, j: (0, j + nj))],
        out_specs=[tile, tile, tile, pl.BlockSpec((tm, k), lambda i, j: (i, 0))],
        out_shape=[jax.ShapeDtypeStruct((s, f), BF16)] * 3 + [jax.ShapeDtypeStruct((s, k), BF16)],
        scratch_shapes=[pltpu.VMEM((tm, k), BF16)],
        compiler_params=_cp("arbitrary", "arbitrary"))(x, g, wgu, wgu)


def _mm_res_fwd(a, w, res, *, scale, name, tm=512):
    s, k = a.shape
    n = w.shape[1]

    def body(a_ref, w_ref, r_ref, o_ref):
        o_ref[...] = r_ref[...] + scale * jnp.dot(a_ref[...], w_ref[...], preferred_element_type=F32)

    return pl.pallas_call(
        body, name=name, grid=(s // tm,),
        in_specs=[pl.BlockSpec((tm, k), lambda i: (i, 0)), pl.BlockSpec((k, n), lambda i: (0, 0)),
                  pl.BlockSpec((tm, n), lambda i: (i, 0))],
        out_specs=pl.BlockSpec((tm, n), lambda i: (i, 0)),
        out_shape=jax.ShapeDtypeStruct((s, n), F32),
        compiler_params=_cp("arbitrary"))(a, w, res)


def _ffn_down_bwd(dh, wd, gate, up, *, scale, name, tm=512):
    s, d = dh.shape
    f = wd.shape[0]
    tn = _col_tile(d, f)

    def body(dh_ref, wd_ref, gate_ref, up_ref, dg_ref, du_ref):
        dhb = (dh_ref[...] * scale).astype(BF16)
        da = lax.dot_general(dhb, wd_ref[...], NT, preferred_element_type=F32)
        gg = gate_ref[...].astype(F32)
        uu = up_ref[...].astype(F32)
        sg = _sigmoid(gg)
        dg_ref[...] = (da * uu * (sg * (1.0 + gg * (1.0 - sg)))).astype(BF16)
        du_ref[...] = (da * (gg * sg)).astype(BF16)

    tile = pl.BlockSpec((tm, tn), lambda i, j: (i, j))
    return pl.pallas_call(
        body, name=name, grid=(s // tm, f // tn),
        in_specs=[pl.BlockSpec((tm, d), lambda i, j: (i, 0)), pl.BlockSpec((tn, d), lambda i, j: (j, 0)), tile, tile],
        out_specs=[tile, tile],
        out_shape=[jax.ShapeDtypeStruct((s, f), BF16)] * 2,
        compiler_params=_cp("arbitrary", "arbitrary"))(dh, wd, gate, up)


def _mm_tn(a, bs, *, name, b_scale=1.0, ts=512):
    bs = list(bs) if isinstance(bs, (list, tuple)) else [bs]
    s, k = a.shape
    n = bs[0].shape[1]
    tn = _col_tile(k, n)
    per = n // tn

    def body(a_ref, *refs):
        b_refs, o_ref = refs[:-1], refs[-1]
        j = pl.program_id(0)

        @pl.when(pl.program_id(1) == 0)
        def _():
            o_ref[...] = jnp.zeros_like(o_ref)

        for m, b_ref in enumerate(b_refs):
            def acc(b_ref=b_ref):
                bv = b_ref[...]
                if b_scale != 1.0:
                    bv = bv * b_scale
                o_ref[...] += lax.dot_general(a_ref[...].astype(BF16), bv.astype(BF16), TN, preferred_element_type=F32)

            if len(b_refs) == 1:
                acc()
            else:
                pl.when(jnp.logical_and(j >= m * per, j < (m + 1) * per))(acc)

    def b_spec(m):
        def idx(j, t):
            mine = jnp.logical_and(j >= m * per, j < (m + 1) * per)
            return (jnp.where(mine, t, 0), jnp.clip(j - m * per, 0, per - 1))
        return pl.BlockSpec((ts, tn), idx)

    return pl.pallas_call(
        body, name=name, grid=(per * len(bs), s // ts),
        in_specs=[pl.BlockSpec((ts, k), lambda j, t: (t, 0))] + [b_spec(m) for m in range(len(bs))],
        out_specs=pl.BlockSpec((k, tn), lambda j, t: (0, j)),
        out_shape=jax.ShapeDtypeStruct((k, n * len(bs)), F32),
        compiler_params=_cp("arbitrary", "arbitrary"))(a, *bs)


def _mm_nt(dy, w, *, name, tm=512):
    s, n = dy.shape
    k = w.shape[0]

    def body(dy_ref, w_ref, o_ref):
        o_ref[...] = lax.dot_general(dy_ref[...].astype(BF16), w_ref[...], NT, preferred_element_type=F32)

    return pl.pallas_call(
        body, name=name, grid=(s // tm,),
        in_specs=[pl.BlockSpec((tm, n), lambda i: (i, 0)), pl.BlockSpec((k, n), lambda i: (0, 0))],
        out_specs=pl.BlockSpec((tm, k), lambda i: (i, 0)),
        out_shape=jax.ShapeDtypeStruct((s, k), F32),
        compiler_params=_cp("arbitrary"))(dy, w)


def _mm_nt_rmsbwd(pairs, x, g, dres, *, name, tm=256):
    s, k = x.shape
    npairs = len(pairs)
    pairs = [pr if len(pr) == 3 else (pr[0], pr[1], 0) for pr in pairs]

    def body(*refs):
        dy_refs = refs[0:2 * npairs:2]
        w_refs = refs[1:2 * npairs:2]
        rest = refs[2 * npairs:]
        x_ref, g_ref = rest[0], rest[1]
        if dres is None:
            dx_ref, dg_ref = rest[2], rest[3]
        else:
            dres_ref, dx_ref, dg_ref = rest[2], rest[3], rest[4]
        dxn = None
        for dy_ref, w_ref in zip(dy_refs, w_refs):
            t = lax.dot_general(dy_ref[...].astype(BF16), w_ref[...], NT, preferred_element_type=F32)
            dxn = t if dxn is None else dxn + t
        dx, dgrow = _rms_bwd(dxn, x_ref[...], g_ref[...])
        if dres is not None:
            dx = dx + dres_ref[...]
        dx_ref[...] = dx

        @pl.when(pl.program_id(0) == 0)
        def _():
            dg_ref[...] = jnp.zeros_like(dg_ref)

        dg_ref[...] += jnp.sum(dgrow, axis=0, keepdims=True)

    in_specs, args = [], []
    for dy, w, cb in pairs:
        n = dy.shape[1]
        in_specs += [pl.BlockSpec((tm, n), lambda i: (i, 0)), pl.BlockSpec((k, n), lambda i, cb=cb: (0, cb))]
        args += [dy, w]
    row = pl.BlockSpec((tm, k), lambda i: (i, 0))
    vec = pl.BlockSpec((1, k), lambda i: (0, 0))
    in_specs += [row, vec]
    args += [x, g]
    if dres is not None:
        in_specs.append(row)
        args.append(dres)
    return pl.pallas_call(
        body, name=name, grid=(s // tm,), in_specs=in_specs, out_specs=[row, vec],
        out_shape=[jax.ShapeDtypeStruct((s, k), F32), jax.ShapeDtypeStruct((1, k), F32)],
        compiler_params=_cp("arbitrary"))(*args)


def _ple_fwd(h, g, wg, p, wp, *, name, tm=512):
    s, d = h.shape
    pd = p.shape[1]

    def body(h_ref, g_ref, wg_ref, p_ref, wp_ref, o_ref, xn_ref, gate_ref, pp_ref):
        hv = h_ref[...]
        _, xhat = _rms_stats(hv)
        xn = (xhat * g_ref[...]).astype(BF16)
        xn_ref[...] = xn
        gate = _sigmoid(jnp.dot(xn, wg_ref[...], preferred_element_type=F32))
        pp = jnp.dot(p_ref[...].astype(BF16), wp_ref[...], preferred_element_type=F32)
        gate_ref[...] = gate.astype(BF16)
        pp_ref[...] = pp.astype(BF16)
        o_ref[...] = hv + gate * pp

    row = pl.BlockSpec((tm, d), lambda i: (i, 0))
    return pl.pallas_call(
        body, name=name, grid=(s // tm,),
        in_specs=[row, pl.BlockSpec((1, d), lambda i: (0, 0)), pl.BlockSpec((d, d), lambda i: (0, 0)),
                  pl.BlockSpec((tm, pd), lambda i: (i, 0)), pl.BlockSpec((pd, d), lambda i: (0, 0))],
        out_specs=[row, row, row, row],
        out_shape=[jax.ShapeDtypeStruct((s, d), F32)] + [jax.ShapeDtypeStruct((s, d), BF16)] * 3,
        compiler_params=_cp("arbitrary"))(h, g, wg, p, wp)


def _ple_bwd_elem(dh, gate, pp, *, name, tm=512):
    s, d = dh.shape

    def body(dh_ref, gate_ref, pp_ref, dz_ref, dpp_ref):
        dhv = dh_ref[...]
        gt = gate_ref[...].astype(F32)
        dz_ref[...] = (dhv * pp_ref[...].astype(F32) * (gt * (1.0 - gt))).astype(BF16)
        dpp_ref[...] = (dhv * gt).astype(BF16)

    row = pl.BlockSpec((tm, d), lambda i: (i, 0))
    return pl.pallas_call(
        body, name=name, grid=(s // tm,), in_specs=[row, row, row], out_specs=[row, row],
        out_shape=[jax.ShapeDtypeStruct((s, d), BF16)] * 2,
        compiler_params=_cp("arbitrary"))(dh, gate, pp)


def _final_loss(h, g, tgt, *, name, tm=512):
    s, d = h.shape

    def body(h_ref, g_ref, t_ref, loss_ref, dh_ref, dg_ref):
        @pl.when(pl.program_id(0) == 0)
        def _():
            loss_ref[...] = jnp.zeros_like(loss_ref)
            dg_ref[...] = jnp.zeros_like(dg_ref)

        hv = h_ref[...]
        gv = g_ref[...]
        _, xhat = _rms_stats(hv)
        err = xhat * gv - t_ref[...]
        per_row = jnp.mean(err * err, axis=-1, keepdims=True)
        loss_ref[...] += 0.5 * jnp.sum(per_row, axis=0, keepdims=True)
        dx, dgrow = _rms_bwd(err * (1.0 / d), hv, gv)
        dh_ref[...] = dx
        dg_ref[...] += jnp.sum(dgrow, axis=0, keepdims=True)

    row = pl.BlockSpec((tm, d), lambda i: (i, 0))
    vec = pl.BlockSpec((1, d), lambda i: (0, 0))
    return pl.pallas_call(
        body, name=name, grid=(s // tm,), in_specs=[row, vec, row],
        out_specs=[pl.BlockSpec((1, LANES), lambda i: (0, 0)), row, vec],
        out_shape=[jax.ShapeDtypeStruct((1, LANES), F32), jax.ShapeDtypeStruct((s, d), F32),
                   jax.ShapeDtypeStruct((1, d), F32)],
        compiler_params=_cp("arbitrary"))(h, g, tgt)


def _rope_fwd(y1, y2, cos, sin, *, name, tm=512):
    s, r = y1.shape

    def body(a_ref, b_ref, c_ref, s_ref, o_ref):
        o_ref[...] = a_ref[...] * c_ref[...] + b_ref[...] * s_ref[...]

    row = pl.BlockSpec((tm, r), lambda i: (i, 0))
    return pl.pallas_call(
        body, name=name, grid=(s // tm,), in_specs=[row] * 4, out_specs=row,
        out_shape=jax.ShapeDtypeStruct((s, r), F32), compiler_params=_cp("arbitrary"))(y1, y2, cos, sin)


def _rope_bwd(dout, cos, sin, *, name, tm=512):
    nh, s, r = dout.shape

    def body(d_ref, c_ref, s_ref, o1_ref, o2_ref):
        tot = d_ref[0]
        for hh in range(1, nh):
            tot = tot + d_ref[hh]
        o1_ref[...] = tot * c_ref[...]
        o2_ref[...] = tot * s_ref[...]

    row = pl.BlockSpec((tm, r), lambda i: (i, 0))
    return pl.pallas_call(
        body, name=name, grid=(s // tm,),
        in_specs=[pl.BlockSpec((nh, tm, r), lambda i: (0, i, 0)), row, row], out_specs=[row, row],
        out_shape=[jax.ShapeDtypeStruct((s, r), F32)] * 2, compiler_params=_cp("arbitrary"))(dout, cos, sin)


def _split3(v):
    h1 = v.astype(BF16)
    r1 = v - h1.astype(F32)
    h2 = r1.astype(BF16)
    h3 = (r1 - h2.astype(F32)).astype(BF16)
    return h1, h2, h3


def _tri(tb, upper):
    r = lax.broadcasted_iota(jnp.int32, (tb, tb), 0)
    c = lax.broadcasted_iota(jnp.int32, (tb, tb), 1)
    return jnp.where((r <= c) if upper else (r >= c), 1.0, 0.0).astype(BF16)


def _fox_gate_fwd(ft, bf, *, out_scale, name, tb=512):
    nh, s = ft.shape

    def body(f_ref, b_ref, o_ref, carry):
        @pl.when(pl.program_id(0) == 0)
        def _():
            carry[...] = jnp.zeros_like(carry)

        z = f_ref[...] + b_ref[...]
        lf = jnp.minimum(z, 0.0) - jnp.log(1.0 + jnp.exp(-jnp.abs(z)))
        tri = _tri(tb, True)
        cs = sum(jnp.dot(t, tri, preferred_element_type=F32) for t in _split3(lf)) + carry[...]
        for n, term in enumerate(_split3(cs * out_scale)):
            o_ref[n] = term
        carry[...] += jnp.sum(lf, axis=-1, keepdims=True)

    return pl.pallas_call(
        body, name=name, grid=(s // tb,),
        in_specs=[pl.BlockSpec((nh, tb), lambda t: (0, t)), pl.BlockSpec((nh, 1), lambda t: (0, 0))],
        out_specs=pl.BlockSpec((3, nh, tb), lambda t: (0, 0, t)),
        out_shape=jax.ShapeDtypeStruct((3, nh, s), BF16),
        scratch_shapes=[pltpu.VMEM((nh, 1), F32)], compiler_params=_cp("arbitrary"))(ft, bf)


def _fox_gate_bwd(drow, dcol, ft, bf, *, inv_scale, name, tb=512):
    nh, s = ft.shape
    nb = s // tb

    def body(dr_ref, dc_ref, f_ref, b_ref, df_ref, db_ref, carry):
        @pl.when(pl.program_id(0) == 0)
        def _():
            carry[...] = jnp.zeros_like(carry)
            db_ref[...] = jnp.zeros_like(db_ref)

        dc = (dr_ref[...] - dc_ref[...]) * inv_scale
        tri = _tri(tb, False)
        suf = sum(jnp.dot(t, tri, preferred_element_type=F32) for t in _split3(dc)) + carry[...]
        z = f_ref[...] + b_ref[...]
        dz = suf * (1.0 / (1.0 + jnp.exp(z)))
        df_ref[...] = dz
        db_ref[...] += jnp.sum(dz, axis=-1, keepdims=True)
        carry[...] += jnp.sum(dc, axis=-1, keepdims=True)

    rev = pl.BlockSpec((nh, tb), lambda t: (0, nb - 1 - t))
    one = pl.BlockSpec((nh, 1), lambda t: (0, 0))
    return pl.pallas_call(
        body, name=name, grid=(nb,), in_specs=[rev, rev, rev, one], out_specs=[rev, one],
        out_shape=[jax.ShapeDtypeStruct((nh, s), F32), jax.ShapeDtypeStruct((nh, 1), F32)],
        scratch_shapes=[pltpu.VMEM((nh, 1), F32)], compiler_params=_cp("arbitrary"))(drow, dcol, ft, bf)


def _scores(q, k, i, jblk, *, scale, tq, tk, window, slope, kb):
    s = lax.dot_general(q, k, NT, preferred_element_type=F32) * scale
    qpos = i * tq + lax.broadcasted_iota(jnp.int32, (tq, tk), 0)
    kpos = jblk * tk + lax.broadcasted_iota(jnp.int32, (tq, tk), 1)
    dist = qpos - kpos
    if slope is not None:
        s = s - slope * dist.astype(F32)
    if kb is not None:
        s = s - kb
    ok = dist >= 0
    if window is not None:
        ok = jnp.logical_and(ok, dist < window)
    return jnp.where(ok, s, MASK_VALUE)


def _flash_fwd(q, k, v, *, scale, name, tq, window=None, slopes_sinks=None, kbias=None):
    nh, s, dq = q.shape
    nkv, _, dv = v.shape
    grp = nh // nkv
    tk = tq
    nq = s // tq
    nj = nq if window is None else 2
    assert window is None or window <= tk
    has_ss, has_kb = slopes_sinks is not None, kbias is not None

    def blk(i, jj):
        return jj if window is None else i - jj

    def body(*refs):
        q_ref, k_ref, v_ref = refs[:3]
        pos = 3
        kb_ref = ss_ref = None
        if has_kb:
            kb_ref = refs[pos]
            pos += 1
        if has_ss:
            ss_ref = refs[pos]
            pos += 1
        o_ref, lse_ref, m_sc, l_sc, acc_sc = refs[pos:pos + 5]
        h, i, jj = pl.program_id(0), pl.program_id(1), pl.program_id(2)
        jblk = blk(i, jj)
        valid = (jj <= i) if window is None else (jblk >= 0)

        @pl.when(jj == 0)
        def _():
            m_sc[...] = jnp.full_like(m_sc, MASK_VALUE)
            l_sc[...] = jnp.zeros_like(l_sc)
            acc_sc[...] = jnp.zeros_like(acc_sc)

        @pl.when(valid)
        def _():
            sc = _scores(q_ref[...], k_ref[...], i, jblk, scale=scale, tq=tq, tk=tk, window=window,
                         slope=ss_ref[0, h] if has_ss else None, kb=kb_ref[...] if has_kb else None)
            m_prev = m_sc[...]
            m_new = jnp.maximum(m_prev, jnp.max(sc, axis=-1, keepdims=True))
            alpha = jnp.exp(m_prev - m_new)
            pr = jnp.exp(sc - m_new)
            l_sc[...] = alpha * l_sc[...] + jnp.sum(pr, axis=-1, keepdims=True)
            acc_sc[...] = alpha * acc_sc[...] + jnp.dot(pr.astype(BF16), v_ref[...], preferred_element_type=F32)
            m_sc[...] = m_new

        @pl.when(jj == nj - 1)
        def _():
            m = m_sc[...]
            l = l_sc[...]
            acc = acc_sc[...]
            if has_ss:
                sink = ss_ref[1, h]
                m_f = jnp.maximum(m, sink)
                corr = jnp.exp(m - m_f)
                l = l * corr + jnp.exp(sink - m_f)
                acc = acc * corr
                m = m_f
            o_ref[...] = (acc / l).astype(BF16)
            lse_ref[...] = jnp.broadcast_to(m + jnp.log(l), (tq, LANES))

    def kv_idx(h, i, jj):
        j = jnp.minimum(jj, i) if window is None else jnp.maximum(i - jj, 0)
        return (h // grp, j, 0)

    in_specs = [pl.BlockSpec((None, tq, dq), lambda h, i, jj: (h, i, 0)),
                pl.BlockSpec((None, tk, dq), kv_idx), pl.BlockSpec((None, tk, dv), kv_idx)]
    args = [q, k, v]
    if has_kb:
        in_specs.append(pl.BlockSpec((None, 1, tk), lambda h, i, jj: (h, 0, kv_idx(h, i, jj)[1])))
        args.append(kbias)
    if has_ss:
        in_specs.append(pl.BlockSpec(memory_space=pltpu.SMEM))
        args.append(slopes_sinks)
    return pl.pallas_call(
        body, name=name, grid=(nh, nq, nj), in_specs=in_specs,
        out_specs=[pl.BlockSpec((None, tq, dv), lambda h, i, jj: (h, i, 0)),
                   pl.BlockSpec((None, tq, LANES), lambda h, i, jj: (h, i, 0))],
        out_shape=[jax.ShapeDtypeStruct((nh, s, dv), BF16), jax.ShapeDtypeStruct((nh, s, LANES), F32)],
        scratch_shapes=[pltpu.VMEM((tq, 1), F32), pltpu.VMEM((tq, 1), F32), pltpu.VMEM((tq, dv), F32)],
        compiler_params=_cp("arbitrary", "arbitrary", "arbitrary"))(*args)


def _flash_bwd(q, k, v, o, do, lse, *, scale, name, tq, window=None, slopes_sinks=None, kbias=None):
    nh, s, dq = q.shape
    nkv, _, dv = v.shape
    grp = nh // nkv
    tk = tq
    nq = s // tq
    nsteps = nq if window is None else 2
    has_ss, has_kb = slopes_sinks is not None, kbias is not None
    assert not (has_ss and window is None)

    def body(*refs):
        q_ref, k_ref, v_ref, o_ref, do_ref, lse_ref = refs[:6]
        pos = 6
        kb_ref = ss_ref = dsink_ref = None
        if has_kb:
            kb_ref = refs[pos]
            pos += 1
        if has_ss:
            ss_ref = refs[pos]
            pos += 1
        dq_ref, dk_ref, dv_ref = refs[pos:pos + 3]
        pos += 3
        if has_ss:
            dsink_ref = refs[pos]
            pos += 1
        hk, j, g, ii = (pl.program_id(a) for a in range(4))
        i = ii if window is None else j + ii
        valid = (i >= j) if window is None else (i < nq)

        @pl.when(jnp.logical_and(j == 0, jnp.logical_and(g == 0, ii == 0)))
        def _():
            dq_ref[...] = jnp.zeros_like(dq_ref)
            if has_ss:
                dsink_ref[...] = jnp.zeros_like(dsink_ref)

        @pl.when(jnp.logical_and(g == 0, ii == 0))
        def _():
            dk_ref[...] = jnp.zeros_like(dk_ref)
            dv_ref[...] = jnp.zeros_like(dv_ref)

        @pl.when(valid)
        def _():
            h = hk * grp + g
            qv, kv, dov = q_ref[...], k_ref[...], do_ref[...]
            sc = _scores(qv, kv, i, j, scale=scale, tq=tq, tk=tk, window=window,
                         slope=ss_ref[0, h] if has_ss else None, kb=kb_ref[...] if has_kb else None)
            lse_col = lse_ref[...][:, :1]
            pr = jnp.exp(sc - lse_col)
            dp = lax.dot_general(dov, v_ref[...], NT, preferred_element_type=F32)
            delta = jnp.sum(dov.astype(F32) * o_ref[...].astype(F32), axis=-1, keepdims=True)
            ds = pr * (dp - delta)
            dv_ref[...] += lax.dot_general(pr.astype(BF16), dov, TN, preferred_element_type=F32)
            dsb = (ds * scale).astype(BF16)
            dk_ref[...] += lax.dot_general(dsb, qv, TN, preferred_element_type=F32)
            rows = pl.ds(pl.multiple_of(i * tq, tq), tq)
            dq_ref[g, rows, :] += jnp.dot(dsb, kv, preferred_element_type=F32)
            if has_ss:
                @pl.when(ii == 0)
                def _():
                    psink = jnp.exp(ss_ref[1, h] - lse_col)
                    tot = jnp.sum(psink * delta, axis=0, keepdims=True)
                    dsink_ref[g] -= jnp.broadcast_to(tot, (1, LANES))

    def q_idx(hk, j, g, ii):
        i = jnp.maximum(ii, j) if window is None else jnp.minimum(j + ii, nq - 1)
        return (hk * grp + g, i, 0)

    def kv_idx(hk, j, g, ii):
        return (hk, j, 0)

    in_specs = [pl.BlockSpec((None, tq, dq), q_idx), pl.BlockSpec((None, tk, dq), kv_idx),
                pl.BlockSpec((None, tk, dv), kv_idx), pl.BlockSpec((None, tq, dv), q_idx),
                pl.BlockSpec((None, tq, dv), q_idx), pl.BlockSpec((None, tq, LANES), q_idx)]
    args = [q, k, v, o, do, lse]
    out_specs = [pl.BlockSpec((None, grp, s, dq), lambda hk, j, g, ii: (hk, 0, 0, 0)),
                 pl.BlockSpec((None, tk, dq), kv_idx), pl.BlockSpec((None, tk, dv), kv_idx)]
    out_shape = [jax.ShapeDtypeStruct((nkv, grp, s, dq), F32), jax.ShapeDtypeStruct((nkv, s, dq), F32),
                 jax.ShapeDtypeStruct((nkv, s, dv), F32)]
    if has_kb:
        in_specs.append(pl.BlockSpec((None, 1, tk), lambda hk, j, g, ii: (hk, 0, j)))
        args.append(kbias)
    if has_ss:
        in_specs.append(pl.BlockSpec(memory_space=pltpu.SMEM))
        args.append(slopes_sinks)
        out_specs.append(pl.BlockSpec((None, grp, 1, LANES), lambda hk, j, g, ii: (hk, 0, 0, 0)))
        out_shape.append(jax.ShapeDtypeStruct((nkv, grp, 1, LANES), F32))
    return pl.pallas_call(
        body, name=name, grid=(nkv, nq, grp, nsteps), in_specs=in_specs, out_specs=out_specs,
        out_shape=out_shape, compiler_params=_cp(*["arbitrary"] * 4))(*args)


def _tri_fwd(t, nq):
    i = sum((t >= (r * (r + 1)) // 2).astype(jnp.int32) for r in range(1, nq))
    return i, t - (i * (i + 1)) // 2


def _tri_bwd(t, nq):
    j = sum((t >= r * nq - (r * (r - 1)) // 2).astype(jnp.int32) for r in range(1, nq))
    return j, j + t - (j * nq - (j * (j - 1)) // 2)


def _scores_t(k, q, *, scale, diag):
    s = lax.dot_general(k, q, NT, preferred_element_type=F32) * scale
    if diag:
        r = lax.broadcasted_iota(jnp.int32, s.shape, 0)
        c = lax.broadcasted_iota(jnp.int32, s.shape, 1)
        s = jnp.where(r <= c, s, MASK_VALUE)
    return s


def _causal_fwd_t(q, k, vt, *, scale, name, tq, hb=2):
    nh, s, dq = q.shape
    dv = vt.shape[1]
    nq = s // tq
    nsteps = (nq * (nq + 1)) // 2

    def body(q_ref, k_ref, vt_ref, o_ref, lse_ref, m_sc, l_sc, acc_sc):
        i, j = _tri_fwd(pl.program_id(1), nq)

        @pl.when(j == 0)
        def _():
            m_sc[...] = jnp.full_like(m_sc, MASK_VALUE)
            l_sc[...] = jnp.zeros_like(l_sc)
            acc_sc[...] = jnp.zeros_like(acc_sc)

        def step(diag):
            for u in range(hb):
                sc = _scores_t(k_ref[u], q_ref[u], scale=scale, diag=diag)
                m_prev = m_sc[u]
                m_new = jnp.maximum(m_prev, jnp.max(sc, axis=0, keepdims=True))
                alpha = jnp.exp(m_prev - m_new)
                pr = jnp.exp(sc - m_new)
                l_new = alpha * l_sc[u] + jnp.sum(pr, axis=0, keepdims=True)
                acc = alpha * acc_sc[u] + jnp.dot(vt_ref[u], pr.astype(BF16), preferred_element_type=F32)
                if diag:
                    o_ref[u] = (acc / l_new).astype(BF16)
                    lse_ref[u] = m_new + jnp.log(l_new)
                else:
                    m_sc[u], l_sc[u], acc_sc[u] = m_new, l_new, acc

        pl.when(j < i)(functools.partial(step, False))
        pl.when(j == i)(functools.partial(step, True))

    def qi(t):
        return _tri_fwd(t, nq)[0]

    def kj(t):
        return _tri_fwd(t, nq)[1]

    return pl.pallas_call(
        body, name=name, grid=(nh // hb, nsteps),
        in_specs=[pl.BlockSpec((hb, tq, dq), lambda hp, t: (hp, qi(t), 0)),
                  pl.BlockSpec((hb, tq, dq), lambda hp, t: (hp, kj(t), 0)),
                  pl.BlockSpec((hb, dv, tq), lambda hp, t: (hp, 0, kj(t)))],
        out_specs=[pl.BlockSpec((hb, dv, tq), lambda hp, t: (hp, 0, qi(t))),
                   pl.BlockSpec((hb, 1, tq), lambda hp, t: (hp, 0, qi(t)))],
        out_shape=[jax.ShapeDtypeStruct((nh, dv, s), BF16), jax.ShapeDtypeStruct((nh, 1, s), F32)],
        scratch_shapes=[pltpu.VMEM((hb, 1, tq), F32), pltpu.VMEM((hb, 1, tq), F32), pltpu.VMEM((hb, dv, tq), F32)],
        compiler_params=_cp("arbitrary", "arbitrary"))(q, k, vt)


def _causal_bwd_t(q, k, v, ot, dot_, lse, *, scale, name, tq, hb=2):
    nh, s, dq = q.shape
    dv = v.shape[-1]
    nq = s // tq
    nsteps = (nq * (nq + 1)) // 2

    def body(q_ref, k_ref, v_ref, ot_ref, dot_ref, lse_ref, dq_ref, dk_ref, dvt_ref):
        t = pl.program_id(1)
        j, i = _tri_bwd(t, nq)

        @pl.when(t == 0)
        def _():
            dq_ref[...] = jnp.zeros_like(dq_ref)

        def step(diag):
            rows = pl.ds(pl.multiple_of(i * tq, tq), tq)
            for u in range(hb):
                qv, kv, dov = q_ref[u], k_ref[u], dot_ref[u]
                pr = jnp.exp(_scores_t(kv, qv, scale=scale, diag=diag) - lse_ref[u])
                dp = jnp.dot(v_ref[u], dov, preferred_element_type=F32)
                delta = jnp.sum(dov.astype(F32) * ot_ref[u].astype(F32), axis=0, keepdims=True)
                dsb = ((pr * (dp - delta)) * scale).astype(BF16)
                d_v = lax.dot_general(dov, pr.astype(BF16), NT, preferred_element_type=F32)
                d_k = jnp.dot(dsb, qv, preferred_element_type=F32)
                if diag:
                    dvt_ref[u], dk_ref[u] = d_v, d_k
                else:
                    dvt_ref[u] += d_v
                    dk_ref[u] += d_k
                dq_ref[u, rows, :] += lax.dot_general(dsb, kv, TN, preferred_element_type=F32)

        pl.when(i > j)(functools.partial(step, False))
        pl.when(i == j)(functools.partial(step, True))

    def qi(t):
        return _tri_bwd(t, nq)[1]

    def kj(t):
        return _tri_bwd(t, nq)[0]

    rows_q = pl.BlockSpec((hb, tq, dq), lambda hp, t: (hp, qi(t), 0))
    rows_k = pl.BlockSpec((hb, tq, dq), lambda hp, t: (hp, kj(t), 0))
    lanes_q = pl.BlockSpec((hb, dv, tq), lambda hp, t: (hp, 0, qi(t)))
    return pl.pallas_call(
        body, name=name, grid=(nh // hb, nsteps),
        in_specs=[rows_q, rows_k, pl.BlockSpec((hb, tq, dv), lambda hp, t: (hp, kj(t), 0)), lanes_q, lanes_q,
                  pl.BlockSpec((hb, 1, tq), lambda hp, t: (hp, 0, qi(t)))],
        out_specs=[pl.BlockSpec((hb, s, dq), lambda hp, t: (hp, 0, 0)), rows_k,
                   pl.BlockSpec((hb, dv, tq), lambda hp, t: (hp, 0, kj(t)))],
        out_shape=[jax.ShapeDtypeStruct((nh, s, dq), F32), jax.ShapeDtypeStruct((nh, s, dq), F32),
                   jax.ShapeDtypeStruct((nh, dv, s), F32)],
        compiler_params=_cp("arbitrary", "arbitrary"))(q, k, v, ot, dot_, lse)


def _adamw(w, g, m, v, *, name):
    shape = w.shape
    cols = shape[-1]
    rows = int(np.prod(shape[:-1])) if len(shape) > 1 else 1
    tr = _row_tile(rows, cols)
    c1 = 1.0 - ADAM_B1 ** ADAM_STEP
    c2 = 1.0 - ADAM_B2 ** ADAM_STEP

    def body(w_ref, g_ref, m_ref, v_ref, d_ref, mo_ref, vo_ref):
        gv = g_ref[...]
        mn = ADAM_B1 * m_ref[...] + (1.0 - ADAM_B1) * gv
        vn = ADAM_B2 * v_ref[...] + (1.0 - ADAM_B2) * (gv * gv)
        mo_ref[...] = mn
        vo_ref[...] = vn
        d_ref[...] = -ADAM_LR * ((mn / c1) / (jnp.sqrt(vn / c2) + ADAM_EPS) + ADAM_WD * w_ref[...])

    blk = pl.BlockSpec((tr, cols), lambda i: (i, 0))
    outs = pl.pallas_call(
        body, name=name, grid=(rows // tr,), in_specs=[blk] * 4, out_specs=[blk] * 3,
        out_shape=[jax.ShapeDtypeStruct((rows, cols), F32)] * 3,
        compiler_params=_cp("arbitrary"))(*[a.reshape(rows, cols) for a in (w, g, m, v)])
    return tuple(a.reshape(shape) for a in outs)


def _hbm_spec():
    return pl.BlockSpec(memory_space=pl.ANY)


def _mesh_place():
    x, y, c = lax.axis_index("x"), lax.axis_index("y"), lax.axis_index("c")
    return x, y, c, [(1 - x, y), (x, 1 - y), (1 - x, 1 - y)]


def _half_rows(c, rows, align):
    return pl.ds(pl.multiple_of(c * (rows // 2), align), rows // 2)


def _part(ref, mode, k, n, rows=None):
    if mode == "cols":
        cols = pl.ds(pl.multiple_of(k * n, LANES), n)
        return ref.at[:, cols] if rows is None else ref.at[rows, cols]
    return ref.at[k] if rows is None else ref.at[k, rows, :]


def _gather_weights(shards, modes, *, name):
    n_arr = len(shards)
    out_shape = [jax.ShapeDtypeStruct((s.shape[0], N_CHIPS * s.shape[1]) if m == "cols" else (N_CHIPS,) + s.shape, s.dtype)
                 for s, m in zip(shards, modes)]
    per = 7

    def body(*refs):
        srcs, dsts = refs[:n_arr], refs[n_arr:2 * n_arr]
        send_sems, recv_sems = refs[2 * n_arr:]
        x, y, c, chips = _mesh_place()
        me = 2 * x + y
        sends = []

        def copy(i, slot, src, dst, to):
            return pltpu.make_async_remote_copy(src_ref=src, dst_ref=dst, send_sem=send_sems.at[i * per + slot],
                                                recv_sem=recv_sems.at[i * per + slot], device_id=to, device_id_type=MESH)

        def half(i, k, hc, ref=None):
            r, n = shards[i].shape
            return _part(dsts[i] if ref is None else ref, modes[i], k, n, _half_rows(hc, r, 16))

        for i in range(n_arr):
            r, n = shards[i].shape
            mine = srcs[i].at[_half_rows(c, r, 16)]
            cps = [copy(i, 0, srcs[i], _part(dsts[i], modes[i], me, n), (x, y, 1 - c))]
            cps += [copy(i, 1 + j, mine, half(i, me, c), (px, py, c)) for j, (px, py) in enumerate(chips)]
            for cp in cps:
                cp.start()
            sends += cps
        for i in range(n_arr):
            for j, (px, py) in enumerate(chips):
                k = 2 * px + py
                copy(i, 1 + j, half(i, k, c), half(i, k, c), (px, py, c)).wait_recv()
                fwd = copy(i, 4 + j, half(i, k, c), half(i, k, c), (x, y, 1 - c))
                fwd.start()
                sends.append(fwd)
        for i in range(n_arr):
            n = shards[i].shape[1]
            own = _part(dsts[i], modes[i], me, n)
            copy(i, 0, own, own, (x, y, 1 - c)).wait_recv()
            for j, (px, py) in enumerate(chips):
                k = 2 * px + py
                copy(i, 4 + j, half(i, k, 1 - c), half(i, k, 1 - c), (x, y, 1 - c)).wait_recv()
        for cp in sends:
            cp.wait_send()

    return pl.pallas_call(
        body, name=name, in_specs=[_hbm_spec()] * n_arr, out_specs=[_hbm_spec()] * n_arr, out_shape=out_shape,
        scratch_shapes=[pltpu.SemaphoreType.DMA((n_arr * per,)), pltpu.SemaphoreType.DMA((n_arr * per,))])(*shards)


def _blk_view(a, mode):
    return a[None] if mode == "cols" else a


def _rs_pair_swap(arrs, modes, *, name):
    n_arr = len(arrs)
    out_shape = [jax.ShapeDtypeStruct((a.shape[0] // 2, a.shape[1]) if m == "cols" else (a.shape[0], a.shape[1] // 2, a.shape[2]), a.dtype)
                 for a, m in zip(arrs, modes)]

    def body(*refs):
        srcs, dsts = refs[:n_arr], refs[n_arr:2 * n_arr]
        send_sems, recv_sems = refs[2 * n_arr:]
        x, y, c, _ = _mesh_place()
        cps = []
        for i in range(n_arr):
            if modes[i] == "cols":
                src = srcs[i].at[_half_rows(1 - c, arrs[i].shape[0], 8)]
            else:
                src = srcs[i].at[:, _half_rows(1 - c, arrs[i].shape[1], 8), :]
            cps.append(pltpu.make_async_remote_copy(src_ref=src, dst_ref=dsts[i], send_sem=send_sems.at[i],
                                                    recv_sem=recv_sems.at[i], device_id=(x, y, 1 - c), device_id_type=MESH))
        for cp in cps:
            cp.start()
        for cp in cps:
            cp.wait()

    return pl.pallas_call(
        body, name=name, in_specs=[_hbm_spec()] * n_arr, out_specs=[_hbm_spec()] * n_arr, out_shape=out_shape,
        scratch_shapes=[pltpu.SemaphoreType.DMA((n_arr,)), pltpu.SemaphoreType.DMA((n_arr,))])(*arrs)


def _rs_pair_add(arr, landed, place, *, name):
    nb, r, c = arr.shape
    rh = r // 2
    tr = _row_tile(rh, c)
    nt = rh // tr

    def body(p_ref, a_ref, l_ref, o_ref):
        o_ref[...] = (a_ref[...] + l_ref[...]).astype(BF16)

    grid_spec = pltpu.PrefetchScalarGridSpec(
        num_scalar_prefetch=1, grid=(nb, nt),
        in_specs=[pl.BlockSpec((None, tr, c), lambda b, t, p_ref: (b, p_ref[1] * nt + t, 0)),
                  pl.BlockSpec((None, tr, c), lambda b, t, p_ref: (b, t, 0))],
        out_specs=pl.BlockSpec((None, tr, c), lambda b, t, p_ref: (b, t, 0)))
    return pl.pallas_call(
        body, name=name, grid_spec=grid_spec, out_shape=jax.ShapeDtypeStruct((nb, rh, c), BF16),
        compiler_params=_cp("arbitrary", "arbitrary"))(place, arr, landed)


def _rs_chip_exchange(parts, modes, *, name):
    n_arr = len(parts)
    out_shape = []
    for a, m in zip(parts, modes):
        shp = (a.shape[0], a.shape[1] // N_CHIPS) if m == "cols" else a.shape[1:]
        out_shape.append(jax.ShapeDtypeStruct((3,) + shp, a.dtype))

    def body(*refs):
        srcs, dsts = refs[:n_arr], refs[n_arr:2 * n_arr]
        send_sems, recv_sems = refs[2 * n_arr:]
        x, y, c, chips = _mesh_place()
        cps = []
        for i in range(n_arr):
            n = out_shape[i].shape[-1]
            for j, (px, py) in enumerate(chips):
                cps.append(pltpu.make_async_remote_copy(
                    src_ref=_part(srcs[i], modes[i], 2 * px + py, n), dst_ref=dsts[i].at[j],
                    send_sem=send_sems.at[3 * i + j], recv_sem=recv_sems.at[3 * i + j],
                    device_id=(px, py, c), device_id_type=MESH))
        for cp in cps:
            cp.start()
        for cp in cps:
            cp.wait()

    return pl.pallas_call(
        body, name=name, in_specs=[_hbm_spec()] * n_arr, out_specs=[_hbm_spec()] * n_arr, out_shape=out_shape,
        scratch_shapes=[pltpu.SemaphoreType.DMA((3 * n_arr,)), pltpu.SemaphoreType.DMA((3 * n_arr,))])(*parts)


def _rs_chip_sum(part, landed, mode, place, *, name):
    _, rh, n = landed.shape
    tr = _row_tile(rh, n)
    nt = rh // tr

    def body(p_ref, a_ref, l_ref, o_ref):
        o_ref[...] = ((a_ref[...].astype(F32) + l_ref[0].astype(F32)) + l_ref[1].astype(F32)) + l_ref[2].astype(F32)

    if mode == "cols":
        own = pl.BlockSpec((tr, n), lambda t, p_ref: (t, p_ref[0]))
    else:
        own = pl.BlockSpec((None, tr, n), lambda t, p_ref: (p_ref[0], t, 0))
    grid_spec = pltpu.PrefetchScalarGridSpec(
        num_scalar_prefetch=1, grid=(nt,),
        in_specs=[own, pl.BlockSpec((3, tr, n), lambda t, p_ref: (0, t, 0))],
        out_specs=pl.BlockSpec((tr, n), lambda t, p_ref: (p_ref[1] * nt + t, 0)))
    return pl.pallas_call(
        body, name=name, grid_spec=grid_spec, out_shape=jax.ShapeDtypeStruct((2 * rh, n), F32),
        compiler_params=_cp("arbitrary"))(place, part, landed)


def _rs_pair_join(halves, *, name):
    n_arr = len(halves)

    def body(*refs):
        outs = refs[n_arr:2 * n_arr]
        send_sems, recv_sems = refs[2 * n_arr:]
        x, y, c, _ = _mesh_place()
        cps = []
        for i in range(n_arr):
            rows = _half_rows(c, halves[i].shape[0], 8)
            cps.append(pltpu.make_async_remote_copy(src_ref=outs[i].at[rows], dst_ref=outs[i].at[rows], send_sem=send_sems.at[i],
                                                    recv_sem=recv_sems.at[i], device_id=(x, y, 1 - c), device_id_type=MESH))
        for cp in cps:
            cp.start()
        for i, cp in enumerate(cps):
            cp.wait_send()
            theirs = outs[i].at[_half_rows(1 - c, halves[i].shape[0], 8)]
            pltpu.make_async_remote_copy(src_ref=theirs, dst_ref=theirs, send_sem=send_sems.at[i], recv_sem=recv_sems.at[i],
                                         device_id=(x, y, 1 - c), device_id_type=MESH).wait_recv()

    return pl.pallas_call(
        body, name=name, in_specs=[_hbm_spec()] * n_arr, out_specs=[_hbm_spec()] * n_arr,
        out_shape=[jax.ShapeDtypeStruct(h.shape, h.dtype) for h in halves],
        input_output_aliases={i: i for i in range(n_arr)},
        scratch_shapes=[pltpu.SemaphoreType.DMA((n_arr,)), pltpu.SemaphoreType.DMA((n_arr,))])(*halves)


def _allreduce_small(v, *, name):
    r, c = v.shape

    def body(v_ref, o_ref, gath, send_sems, recv_sems):
        x, y, cc, _ = _mesh_place()
        me = 4 * x + 2 * y + cc
        gath[me] = v_ref[...]
        cps = []
        for rel in range(1, 8):
            px = 1 - x if rel & 4 else x
            py = 1 - y if rel & 2 else y
            pc = 1 - cc if rel & 1 else cc

            def copy(slot, px=px, py=py, pc=pc, rel=rel):
                return pltpu.make_async_remote_copy(
                    src_ref=v_ref, dst_ref=gath.at[slot], send_sem=send_sems.at[rel - 1],
                    recv_sem=recv_sems.at[rel - 1], device_id=(px, py, pc), device_id_type=MESH)

            cps.append((copy(me), copy(4 * px + 2 * py + pc)))
        for send, _ in cps:
            send.start()
        for send, theirs in cps:
            theirs.wait_recv()
            send.wait_send()
        tot = gath[0]
        for d in range(1, 8):
            tot = tot + gath[d]
        o_ref[...] = tot

    vm = pl.BlockSpec(memory_space=pltpu.VMEM)
    return pl.pallas_call(
        body, name=name, in_specs=[vm], out_specs=vm, out_shape=jax.ShapeDtypeStruct((r, c), F32),
        scratch_shapes=[pltpu.VMEM((8, r, c), F32), pltpu.SemaphoreType.DMA((7,)), pltpu.SemaphoreType.DMA((7,))])(v)


def _to_heads(a, nh, dh, dtype=BF16):
    return a.reshape(a.shape[0], nh, dh).transpose(1, 0, 2).astype(dtype)


def _from_heads(a):
    return a.transpose(1, 0, 2).reshape(a.shape[1], -1)


def _rope_tables(s, reps):
    half = B_ROPE // 2
    inv = ROPE_THETA ** (-jnp.arange(0, B_ROPE, 2, dtype=F32) / B_ROPE)
    ang = jnp.arange(s, dtype=F32)[:, None] * inv[None, :]
    return jnp.tile(jnp.cos(ang), (1, reps)), jnp.tile(jnp.sin(ang), (1, reps))


def _alibi_slopes():
    return 2.0 ** (-8.0 * jnp.arange(1, A_HEADS + 1, dtype=F32) / A_HEADS)


def _ffn_fwd(h, norm, wts, tag):
    gate, up, act, xn = _ffn_up(h, norm, wts["wgu"], name=f"{tag}_up")
    out = _mm_res_fwd(act, wts["wd"], h, scale=FFN_RES_SCALE, name=f"{tag}_down")
    return out, dict(h_in=h, gate=gate, up=up, act=act, xn=xn)


def _ffn_bwd(dh, norm, wts, sv, tag):
    dgate, dup = _ffn_down_bwd(dh, wts["wd"], sv["gate"], sv["up"], scale=FFN_RES_SCALE, name=f"{tag}_down_bwd")
    d_wd = _mm_tn(sv["act"], dh, b_scale=FFN_RES_SCALE, name=f"{tag}_dwd")
    d_wgu = _mm_tn(sv["xn"], [dgate, dup], name=f"{tag}_dwgu")
    dh_in, dnorm = _mm_nt_rmsbwd([(dgate, wts["wgu"], 0), (dup, wts["wgu"], 1)], sv["h_in"], norm, dh,
                                 name=f"{tag}_dx")
    return dh_in, dnorm, d_wgu, d_wd


def _even_weights(w_in, w_uq, w_ukv):
    half = B_ROPE // 2
    base = w_in.shape[1]
    kr1, kr2 = w_in[:, base - B_ROPE:base - half], w_in[:, base - half:]
    w_in_cat = jnp.concatenate([w_in, -kr2, kr1, jnp.zeros((w_in.shape[0], 64), w_in.dtype)], axis=1)
    u3 = w_uq.reshape(w_uq.shape[0], B_HEADS, B_NOPE + B_ROPE)
    nope = u3[:, :, :B_NOPE].reshape(w_uq.shape[0], -1)
    r1 = u3[:, :, B_NOPE:B_NOPE + half].reshape(w_uq.shape[0], -1)
    r2 = u3[:, :, B_NOPE + half:].reshape(w_uq.shape[0], -1)
    w_q_cat = jnp.concatenate([nope, r1, r2, -r2, r1], axis=1)
    return w_in_cat, w_q_cat, w_ukv


def _even_fwd(h, w, i):
    s = h.shape[0]
    half = B_ROPE // 2
    ycat, xn = _rms_mm_fwd(h, w["mix_norm"][i:i + 1], w["ev_in_cat"], name="ev_in")
    a_q, a_k, a_v = ycat[:, :512], ycat[:, 512:640], ycat[:, 640:768]
    c_q, c_kv = ycat[:, 768:1024], ycat[:, 1024:1152]
    cos32, sin32 = _rope_tables(s, 2)
    kro = _rope_fwd(ycat[:, 1152:1184], ycat[:, 1184:1216], cos32, sin32, name="ev_k_rope")
    qa, ka, va = _to_heads(a_q, A_HEADS, A_HEAD_DIM), _to_heads(a_k, A_KV_HEADS, A_HEAD_DIM), _to_heads(a_v, A_KV_HEADS, A_HEAD_DIM)
    ss = jnp.stack([_alibi_slopes(), w["ev_sinks"].reshape(-1)])
    oa, lse_a = _flash_fwd(qa, ka, va, scale=A_HEAD_DIM ** -0.5, name="swa_fwd", tq=256, window=WINDOW, slopes_sinks=ss)
    yq, xn_q = _rms_mm_fwd(c_q, w["ev_cq_norm"], w["ev_q_cat"], name="ev_q_up")
    cos256, sin256 = _rope_tables(s, 2 * B_HEADS)
    qro = _rope_fwd(yq[:, 512:768], yq[:, 768:1024], cos256, sin256, name="ev_q_rope")
    ykv, xn_kv = _rms_mm_fwd(c_kv, w["ev_ckv_norm"], w["ev_ukv"], name="ev_kv_up")
    zq = jnp.zeros((s, B_HEADS, LANES - B_NOPE - B_ROPE), F32)
    qb = jnp.concatenate([yq[:, :512].reshape(s, B_HEADS, B_NOPE), qro[:, :128].reshape(s, B_HEADS, half),
                          qro[:, 128:].reshape(s, B_HEADS, half), zq], axis=-1).transpose(1, 0, 2).astype(BF16)
    kv3 = ykv.reshape(s, B_HEADS, B_NOPE + B_V)
    kb = jnp.concatenate([kv3[:, :, :B_NOPE], jnp.broadcast_to(kro[:, None, :], (s, B_HEADS, B_ROPE)), zq],
                         axis=-1).transpose(1, 0, 2).astype(BF16)
    vb = kv3[:, :, B_NOPE:].transpose(1, 0, 2).astype(BF16)
    ob, lse_b = _causal_fwd_t(qb, kb, vb.transpose(0, 2, 1), scale=(B_NOPE + B_ROPE) ** -0.5, name="mla_fwd", tq=512)
    attn = jnp.concatenate([_from_heads(oa), ob.transpose(2, 0, 1).reshape(s, -1)], axis=-1)
    out = _mm_res_fwd(attn, w["ev_out"], h, scale=1.0, name="ev_out")
    sv = dict(h_in=h, xn=xn, c_q=c_q, c_kv=c_kv, xn_q=xn_q, xn_kv=xn_kv, qa=qa, ka=ka, va=va, oa=oa, lse_a=lse_a,
              ss=ss, qb=qb, kb=kb, vb=vb, ob=ob, lse_b=lse_b, attn=attn, cos32=cos32, sin32=sin32,
              cos256=cos256, sin256=sin256)
    return out, sv


def _even_bwd(dh, w, sv, i):
    s = dh.shape[0]
    half = B_ROPE // 2
    g = {}
    dattn = _mm_nt(dh, w["ev_out"], name="ev_out_dx")
    g["ev_w_out"] = _mm_tn(sv["attn"], dh, name="ev_out_dw")
    doa = _to_heads(dattn[:, :512], A_HEADS, A_HEAD_DIM)
    dob = dattn[:, 512:].reshape(s, B_HEADS, B_V).transpose(1, 2, 0).astype(BF16)
    dqa, dka, dva, dsink = _flash_bwd(sv["qa"], sv["ka"], sv["va"], sv["oa"], doa, sv["lse_a"], scale=A_HEAD_DIM ** -0.5,
                                      name="swa_bwd", tq=256, window=WINDOW, slopes_sinks=sv["ss"])
    g["ev_sinks"] = dsink[:, :, 0, 0].reshape(1, A_HEADS)
    dqb, dkb, dvb = _causal_bwd_t(sv["qb"], sv["kb"], sv["vb"], sv["ob"], dob, sv["lse_b"],
                                  scale=(B_NOPE + B_ROPE) ** -0.5, name="mla_bwd", tq=512)
    dq_r1 = dqb[:, :, B_NOPE:B_NOPE + half].transpose(1, 0, 2).reshape(s, -1)
    dq_r2 = dqb[:, :, B_NOPE + half:B_NOPE + B_ROPE].transpose(1, 0, 2).reshape(s, -1)
    dq1, dq2 = _rope_bwd(jnp.concatenate([dq_r1, dq_r2], axis=-1)[None], sv["cos256"], sv["sin256"], name="ev_q_rope_bwd")
    dyq = jnp.concatenate([_from_heads(dqb[:, :, :B_NOPE]), dq1, dq2], axis=-1)
    dwq = _mm_tn(sv["xn_q"], dyq, name="ev_q_up_dw")
    dcq, g["ev_cq_norm"] = _mm_nt_rmsbwd([(dyq, w["ev_q_cat"])], sv["c_q"], w["ev_cq_norm"], None, name="ev_q_up_dx")
    kq = sv["c_q"].shape[1]
    d_nope = dwq[:, :512].reshape(kq, B_HEADS, B_NOPE)
    d_r1 = (dwq[:, 512:640] + dwq[:, 896:1024]).reshape(kq, B_HEADS, half)
    d_r2 = (dwq[:, 640:768] - dwq[:, 768:896]).reshape(kq, B_HEADS, half)
    g["ev_w_uq"] = jnp.concatenate([d_nope, d_r1, d_r2], axis=-1).reshape(kq, -1)
    dykv = jnp.concatenate([dkb[:, :, :B_NOPE].transpose(1, 0, 2), dvb.transpose(2, 0, 1)], axis=-1).reshape(s, -1)
    g["ev_w_ukv"] = _mm_tn(sv["xn_kv"], dykv, name="ev_kv_up_dw")
    dckv, g["ev_ckv_norm"] = _mm_nt_rmsbwd([(dykv, w["ev_ukv"])], sv["c_kv"], w["ev_ckv_norm"], None, name="ev_kv_up_dx")
    dk1, dk2 = _rope_bwd(dkb[:, :, B_NOPE:B_NOPE + B_ROPE], sv["cos32"], sv["sin32"], name="ev_k_rope_bwd")
    dycat = jnp.concatenate([_from_heads(dqa.reshape(A_HEADS, s, A_HEAD_DIM)), _from_heads(dka), _from_heads(dva),
                             dcq, dckv, dk1, dk2, jnp.zeros((s, 64), F32)], axis=-1)
    dwin = _mm_tn(sv["xn"], dycat, name="ev_in_dw")
    base = 1184
    g["ev_w_in"] = jnp.concatenate([dwin[:, :base - B_ROPE],
                                    dwin[:, base - B_ROPE:base - half] + dwin[:, base + half:base + B_ROPE],
                                    dwin[:, base - half:base] - dwin[:, base:base + half]], axis=-1)
    dh_in, dnorm = _mm_nt_rmsbwd([(dycat, w["ev_in_cat"])], sv["h_in"], w["mix_norm"][i:i + 1], dh, name="ev_in_dx")
    return dh_in, dnorm, g


def _odd_fwd(h, w, i):
    s = h.shape[0]
    wd = C_HEADS * C_HEAD_DIM
    y, xn = _rms_mm_fwd(h, w["mix_norm"][i:i + 1], w["od_in_pad"], name="od_in")
    scale = C_HEAD_DIM ** -0.5
    ft = y[:, 3 * wd:3 * wd + C_HEADS].T
    bf = w["od_b_f"].reshape(C_HEADS, 1)
    cb3 = _fox_gate_fwd(ft, bf, out_scale=-1.0 / scale, name="fox_gate_fwd")
    ones, zeros = jnp.ones((s, C_HEADS, 1), BF16), jnp.zeros((s, C_HEADS, 1), BF16)
    tail = jnp.zeros((s, C_HEADS, LANES - C_HEAD_DIM - 5), BF16)
    q3 = y[:, :wd].reshape(s, C_HEADS, C_HEAD_DIM).astype(BF16)
    k3 = y[:, wd:2 * wd].reshape(s, C_HEADS, C_HEAD_DIM).astype(BF16)
    q = jnp.concatenate([q3, ones, zeros, ones, ones, ones, tail], axis=-1).transpose(1, 0, 2)
    k = jnp.concatenate([k3, zeros, ones, cb3.transpose(2, 1, 0), tail], axis=-1).transpose(1, 0, 2)
    v = _to_heads(y[:, 2 * wd:3 * wd], C_HEADS, C_HEAD_DIM)
    o, lse = _causal_fwd_t(q, k, v.transpose(0, 2, 1), scale=scale, name="fox_fwd", tq=512)
    attn = o.transpose(2, 0, 1).reshape(s, -1)
    out = _mm_res_fwd(attn, w["od_out"], h, scale=1.0, name="od_out")
    return out, dict(h_in=h, xn=xn, q=q, k=k, v=v, o=o, lse=lse, ft=ft, bf=bf, attn=attn)


def _odd_bwd(dh, w, sv, i):
    s = dh.shape[0]
    g = {}
    dattn = _mm_nt(dh, w["od_out"], name="od_out_dx")
    g["od_w_out"] = _mm_tn(sv["attn"], dh, name="od_out_dw")
    do = dattn.reshape(s, C_HEADS, C_HEAD_DIM).transpose(1, 2, 0).astype(BF16)
    scale = C_HEAD_DIM ** -0.5
    dq, dk, dv = _causal_bwd_t(sv["q"], sv["k"], sv["v"], sv["o"], do, sv["lse"], scale=scale, name="fox_bwd", tq=512)
    dft, dbf = _fox_gate_bwd(dq[:, :, C_HEAD_DIM + 1], dk[:, :, C_HEAD_DIM], sv["ft"], sv["bf"],
                             inv_scale=1.0 / scale, name="fox_gate_bwd")
    dq, dk = dq[:, :, :C_HEAD_DIM], dk[:, :, :C_HEAD_DIM]
    g["od_b_f"] = dbf.reshape(1, C_HEADS)
    n_pad = w["od_in_pad"].shape[1]
    n_real = 3 * C_HEADS * C_HEAD_DIM + C_HEADS
    dy = jnp.concatenate([_from_heads(dq), _from_heads(dk), dv.transpose(2, 0, 1).reshape(s, -1), dft.T,
                          jnp.zeros((s, n_pad - n_real), F32)], axis=-1)
    g["od_w_in"] = _mm_tn(sv["xn"], dy, name="od_in_dw")[:, :n_real]
    dh_in, dnorm = _mm_nt_rmsbwd([(dy, w["od_in_pad"])], sv["h_in"], w["mix_norm"][i:i + 1], dh, name="od_in_dx")
    return dh_in, dnorm, g


def _kernel_weights(full, replicated):
    depth = len(full["ffa_w_down"])
    w = dict(replicated)
    for tag in ("ffa", "ffb"):
        w[tag] = [dict(wgu=full[tag + "_w_gate_up"][i], wd=full[tag + "_w_down"][i]) for i in range(depth)]
    w["ple_gate"], w["ple_proj"] = full["ple_w_gate"], full["ple_w_proj"]
    w["ev_in_cat"], w["ev_q_cat"], w["ev_ukv"] = _even_weights(full["ev_w_in"][0], full["ev_w_uq"][0], full["ev_w_ukv"][0])
    w["ev_out"], w["od_out"] = full["ev_w_out"][0], full["od_w_out"][0]
    od_in = full["od_w_in"][0]
    w["od_in_pad"] = jnp.pad(od_in, ((0, 0), (0, (-od_in.shape[1]) % LANES)))
    return w


def _local_step(x, p, tgt, w):
    depth = p.shape[0]
    h = x
    saved = []
    for i in range(depth):
        sv = {}
        h, sv["ffa"] = _ffn_fwd(h, w["ffa_norm"][i:i + 1], w["ffa"][i], f"ffa{i}")
        if i % 2 == 0:
            h, sv["mix"] = _even_fwd(h, w, i)
        else:
            h, sv["mix"] = _odd_fwd(h, w, i)
        h, sv["ffb"] = _ffn_fwd(h, w["ffb_norm"][i:i + 1], w["ffb"][i], f"ffb{i}")
        h_in = h
        h, xn, gate, pp = _ple_fwd(h, w["ple_norm"][i:i + 1], w["ple_gate"][i], p[i], w["ple_proj"][i], name=f"ple{i}")
        sv["ple"] = dict(h_in=h_in, xn=xn, gate=gate, pp=pp)
        saved.append(sv)
    loss_vec, dh, d_final = _final_loss(h, w["final_norm"].reshape(1, -1), tgt, name="final_loss")

    per_layer = [dict() for _ in range(depth)]
    grads = {}
    for i in reversed(range(depth)):
        sv, gl = saved[i], per_layer[i]
        dz, dpp = _ple_bwd_elem(dh, sv["ple"]["gate"], sv["ple"]["pp"], name=f"ple{i}_bwd")
        gl["ple_w_gate"] = _mm_tn(sv["ple"]["xn"], dz, name=f"ple{i}_dwg")
        gl["ple_w_proj"] = _mm_tn(p[i], dpp, name=f"ple{i}_dwp")
        dh, gl["ple_norm"] = _mm_nt_rmsbwd([(dz, w["ple_gate"][i])], sv["ple"]["h_in"], w["ple_norm"][i:i + 1], dh,
                                           name=f"ple{i}_dx")
        dh, gl["ffb_norm"], gl["ffb_w_gate_up"], gl["ffb_w_down"] = _ffn_bwd(dh, w["ffb_norm"][i:i + 1], w["ffb"][i], sv["ffb"], f"ffb{i}")
        if i % 2 == 0:
            dh, gl["mix_norm"], gm = _even_bwd(dh, w, sv["mix"], i)
        else:
            dh, gl["mix_norm"], gm = _odd_bwd(dh, w, sv["mix"], i)
        grads.update({n: (g if n in REPLICATED else [g]) for n, g in gm.items()})
        dh, gl["ffa_norm"], gl["ffa_w_gate_up"], gl["ffa_w_down"] = _ffn_bwd(dh, w["ffa_norm"][i:i + 1], w["ffa"][i], sv["ffa"], f"ffa{i}")
    grads["final_norm"] = d_final.reshape(-1)
    for n in ("ffa_norm", "mix_norm", "ffb_norm", "ple_norm"):
        grads[n] = jnp.concatenate([per_layer[i][n] for i in range(depth)], axis=0)
    for n in ("ffa_w_gate_up", "ffa_w_down", "ffb_w_gate_up", "ffb_w_down", "ple_w_gate", "ple_w_proj"):
        grads[n] = [per_layer[i][n] for i in range(depth)]
    return loss_vec[0, 0], dh, grads


def _cut_mode(local_shape, axis, ncols):
    return "cols" if axis == 2 and ncols % LANES == 0 else "blk"


def _small_rows(vals):
    rows = []
    for n in REPLICATED:
        v = vals[n].reshape(-1)
        rows.append(jnp.pad(v, (0, (-v.shape[0]) % FLAT_COLS)).reshape(-1, FLAT_COLS))
    out = jnp.concatenate(rows, axis=0)
    return jnp.pad(out, ((0, (-out.shape[0]) % 8), (0, 0)))


def kernel(x, p, ffa_norm, ffa_w_gate_up, ffa_w_down, mix_norm, ffb_norm, ffb_w_gate_up, ffb_w_down, ple_norm, ple_w_gate, ple_w_proj, ev_w_in, ev_sinks, ev_cq_norm, ev_w_uq, ev_ckv_norm, ev_w_ukv, ev_w_out, od_w_in, od_b_f, od_w_out, final_norm, loss_target, m_ffa_norm, m_ffa_w_gate_up, m_ffa_w_down, m_mix_norm, m_ffb_norm, m_ffb_w_gate_up, m_ffb_w_down, m_ple_norm, m_ple_w_gate, m_ple_w_proj, m_ev_w_in, m_ev_sinks, m_ev_cq_norm, m_ev_w_uq, m_ev_ckv_norm, m_ev_w_ukv, m_ev_w_out, m_od_w_in, m_od_b_f, m_od_w_out, m_final_norm, v_ffa_norm, v_ffa_w_gate_up, v_ffa_w_down, v_mix_norm, v_ffb_norm, v_ffb_w_gate_up, v_ffb_w_down, v_ple_norm, v_ple_w_gate, v_ple_w_proj, v_ev_w_in, v_ev_sinks, v_ev_cq_norm, v_ev_w_uq, v_ev_ckv_norm, v_ev_w_ukv, v_ev_w_out, v_od_w_in, v_od_b_f, v_od_w_out, v_final_norm):
    env = dict(locals())
    wts = {n: env[n] for n in WEIGHT_ORDER}
    mom1 = {n: env["m_" + n] for n in WEIGHT_ORDER}
    mom2 = {n: env["v_" + n] for n in WEIGHT_ORDER}
    place = jnp.stack([2 * lax.axis_index("x") + lax.axis_index("y"), lax.axis_index("c")]).astype(jnp.int32)

    plan, shards = [], []
    for n, axis in SHARDED:
        wb = wts[n].astype(BF16)
        mode = _cut_mode(wb.shape, axis, wb.shape[2])
        for i in range(wb.shape[0]):
            plan.append((n, i, mode, axis))
            shards.append(wb[i])
    modes = [m for _, _, m, _ in plan]
    gathered = _gather_weights(shards, modes, name="weight_allgather")
    full = {n: [] for n, _ in SHARDED}
    for (n, i, mode, axis), dst in zip(plan, gathered):
        if mode == "blk":
            dst = dst.reshape(-1, dst.shape[2]) if axis == 1 else jnp.moveaxis(dst, 0, 1).reshape(dst.shape[1], -1)
        full[n].append(dst)

    w = _kernel_weights(full, {n: wts[n] for n in REPLICATED})
    loss_part, grad_x, grads = _local_step(x[0], p[:, 0], loss_target[0], w)
    loss = lax.psum(loss_part, ("x", "y", "c"))

    arrs = []
    for n, i, mode, axis in plan:
        g2 = grads[n][i]
        if mode == "blk":
            rr, cc = wts[n].shape[1:]
            g2 = g2.reshape(N_CHIPS, rr, cc) if axis == 1 else g2.reshape(rr, N_CHIPS, cc).transpose(1, 0, 2)
        arrs.append(g2)
    landed = _rs_pair_swap(arrs, modes, name="rs_pair_swap")
    parts = []
    for (n, i, mode, _), a, l in zip(plan, arrs, landed):
        pt = _rs_pair_add(_blk_view(a, mode), _blk_view(l, mode), place, name=f"rs_pair_add_{n}{i}")
        parts.append(pt[0] if mode == "cols" else pt)
    landed = _rs_chip_exchange(parts, modes, name="rs_chip_exchange")
    halves = [_rs_chip_sum(pt, l, mode, place, name=f"rs_chip_sum_{n}{i}")
              for (n, i, mode, _), pt, l in zip(plan, parts, landed)]
    reduced = _rs_pair_join(halves, name="rs_pair_join")
    gout = {n: [] for n, _ in SHARDED}
    for (n, _, _, _), r2 in zip(plan, reduced):
        gout[n].append(r2)
    gout = {n: jnp.stack(v).reshape(wts[n].shape) for n, v in gout.items()}
    small = _allreduce_small(_small_rows(grads), name="small_allreduce")
    r0 = 0
    for n in REPLICATED:
        size = int(np.prod(wts[n].shape))
        nr = -(-size // FLAT_COLS)
        gout[n] = small[r0:r0 + nr].reshape(-1)[:size].reshape(wts[n].shape)
        r0 += nr

    delta, new_m, new_v = {}, {}, {}
    for n in WEIGHT_ORDER:
        delta[n], new_m[n], new_v[n] = _adamw(wts[n], gout[n], mom1[n], mom2[n], name="adamw_" + n)
    return (loss, grad_x[None], *[gout[n] for n in WEIGHT_ORDER], *[delta[n] for n in WEIGHT_ORDER],
            *[new_m[n] for n in WEIGHT_ORDER], *[new_v[n] for n in WEIGHT_ORDER])
```

```python
import functools
import math

import numpy as np
import jax
import jax.numpy as jnp
from jax import lax
from jax.experimental import pallas as pl
from jax.experimental.pallas import tpu as pltpu

F32 = jnp.float32
BF16 = jnp.bfloat16
NT = (((1,), (1,)), ((), ()))
TN = (((0,), (0,)), ((), ()))
MESH = pl.DeviceIdType.MESH

RMS_EPS = 1e-6
FFN_RES_SCALE = 0.5
A_HEADS, A_KV_HEADS, A_HEAD_DIM, WINDOW = 8, 2, 64, 128
B_HEADS, B_Q_LORA, B_KV_LORA, B_NOPE, B_ROPE, B_V = 8, 256, 128, 64, 32, 64
ROPE_THETA = 10000.0
C_HEADS, C_HEAD_DIM = 16, 64
ADAM_LR, ADAM_B1, ADAM_B2, ADAM_EPS, ADAM_WD, ADAM_STEP = 0.001, 0.9, 0.999, 1e-08, 0.01, 10

N_CHIPS = 4
LANES = 128
FLAT_COLS = 1024
MASK_VALUE = -1e30
VMEM_LIMIT = 48 * 2**20

SHARDED = (
    ("ffa_w_gate_up", 2), ("ffa_w_down", 1), ("ffb_w_gate_up", 2), ("ffb_w_down", 1),
    ("ple_w_gate", 1), ("ple_w_proj", 2), ("ev_w_in", 2), ("ev_w_uq", 2), ("ev_w_ukv", 2),
    ("ev_w_out", 1), ("od_w_in", 2), ("od_w_out", 1))
REPLICATED = ("ffa_norm", "mix_norm", "ffb_norm", "ple_norm", "final_norm",
              "ev_sinks", "ev_cq_norm", "ev_ckv_norm", "od_b_f")
WEIGHT_ORDER = ("ffa_norm", "ffa_w_gate_up", "ffa_w_down", "mix_norm", "ffb_norm", "ffb_w_gate_up",
                "ffb_w_down", "ple_norm", "ple_w_gate", "ple_w_proj", "ev_w_in", "ev_sinks",
                "ev_cq_norm", "ev_w_uq", "ev_ckv_norm", "ev_w_ukv", "ev_w_out", "od_w_in", "od_b_f",
                "od_w_out", "final_norm")


def _cp(*sem):
    return pltpu.CompilerParams(dimension_semantics=sem, vmem_limit_bytes=VMEM_LIMIT)


def _sigmoid(z):
    return 1.0 / (1.0 + jnp.exp(-z))


def _rms_stats(xv):
    r = lax.rsqrt(jnp.mean(xv * xv, axis=-1, keepdims=True) + RMS_EPS)
    return r, xv * r


def _rms_bwd(dxn, xv, g):
    r, xhat = _rms_stats(xv)
    u = dxn * g
    dx = r * (u - xhat * jnp.mean(u * xhat, axis=-1, keepdims=True))
    return dx, dxn * xhat


def _col_tile(k_rows, n, budget_bytes=6 * 2**20):
    if k_rows * n * 4 <= budget_bytes or n % LANES:
        return n
    units = n // LANES
    best = LANES
    for d in range(1, units + 1):
        if units % d == 0 and k_rows * d * LANES * 4 <= budget_bytes:
            best = d * LANES
    return best


def _row_tile(rows, cols, target_elems=2**18):
    if rows * cols <= target_elems or rows % 8:
        return rows
    best = 8
    for d in range(8, rows + 1, 8):
        if rows % d == 0 and d * cols <= target_elems:
            best = d
    return best


def _rms_mm_fwd(x, g, w, *, name, tm=512):
    s, k = x.shape
    n = w.shape[1]

    def body(x_ref, g_ref, w_ref, y_ref, xn_ref):
        _, xhat = _rms_stats(x_ref[...])
        xn = (xhat * g_ref[...]).astype(BF16)
        xn_ref[...] = xn
        y_ref[...] = jnp.dot(xn, w_ref[...], preferred_element_type=F32)

    return pl.pallas_call(
        body, name=name, grid=(s // tm,),
        in_specs=[pl.BlockSpec((tm, k), lambda i: (i, 0)), pl.BlockSpec((1, k), lambda i: (0, 0)),
                  pl.BlockSpec((k, n), lambda i: (0, 0))],
        out_specs=[pl.BlockSpec((tm, n), lambda i: (i, 0)), pl.BlockSpec((tm, k), lambda i: (i, 0))],
        out_shape=[jax.ShapeDtypeStruct((s, n), F32), jax.ShapeDtypeStruct((s, k), BF16)],
        compiler_params=_cp("arbitrary"))(x, g, w)


def _ffn_up(x, g, wgu, *, name, tm=512):
    s, k = x.shape
    f = wgu.shape[1] // 2
    tn = _col_tile(k, f)
    nj = f // tn

    def body(x_ref, g_ref, wg_ref, wu_ref, gate_ref, up_ref, act_ref, xn_ref, xn_sc):
        @pl.when(pl.program_id(1) == 0)
        def _():
            _, xhat = _rms_stats(x_ref[...])
            xn = (xhat * g_ref[...]).astype(BF16)
            xn_sc[...] = xn
            xn_ref[...] = xn

        xn = xn_sc[...]
        gg = jnp.dot(xn, wg_ref[...], preferred_element_type=F32)
        uu = jnp.dot(xn, wu_ref[...], preferred_element_type=F32)
        gate_ref[...] = gg.astype(BF16)
        up_ref[...] = uu.astype(BF16)
        act_ref[...] = ((gg * _sigmoid(gg)) * uu).astype(BF16)

    tile = pl.BlockSpec((tm, tn), lambda i, j: (i, j))
    return pl.pallas_call(
        body, name=name, grid=(s // tm, nj),
        in_specs=[pl.BlockSpec((tm, k), lambda i, j: (i, 0)), pl.BlockSpec((1, k), lambda i, j: (0, 0)),
                  pl.BlockSpec((k, tn), lambda i, j: (0, j)), pl.BlockSpec((k, tn), lambda i, j: (0, j + nj))],
        out_specs=[tile, tile, tile, pl.BlockSpec((tm, k), lambda i, j: (i, 0))],
        out_shape=[jax.ShapeDtypeStruct((s, f), BF16)] * 3 + [jax.ShapeDtypeStruct((s, k), BF16)],
        scratch_shapes=[pltpu.VMEM((tm, k), BF16)],
        compiler_params=_cp("arbitrary", "arbitrary"))(x, g, wgu, wgu)


def _mm_res_fwd(a, w, res, *, scale, name, tm=512):
    s, k = a.shape
    n = w.shape[1]

    def body(a_ref, w_ref, r_ref, o_ref):
        o_ref[...] = r_ref[...] + scale * jnp.dot(a_ref[...], w_ref[...], preferred_element_type=F32)

    return pl.pallas_call(
        body, name=name, grid=(s // tm,),
        in_specs=[pl.BlockSpec((tm, k), lambda i: (i, 0)), pl.BlockSpec((k, n), lambda i: (0, 0)),
                  pl.BlockSpec((tm, n), lambda i: (i, 0))],
        out_specs=pl.BlockSpec((tm, n), lambda i: (i, 0)),
        out_shape=jax.ShapeDtypeStruct((s, n), F32),
        compiler_params=_cp("arbitrary"))(a, w, res)


def _ffn_down_bwd(dh, wd, gate, up, *, scale, name, tm=512):
    s, d = dh.shape
    f = wd.shape[0]
    tn = _col_tile(d, f)

    def body(dh_ref, wd_ref, gate_ref, up_ref, dg_ref, du_ref):
        dhb = (dh_ref[...] * scale).astype(BF16)
        da = lax.dot_general(dhb, wd_ref[...], NT, preferred_element_type=F32)
        gg = gate_ref[...].astype(F32)
        uu = up_ref[...].astype(F32)
        sg = _sigmoid(gg)
        dg_ref[...] = (da * uu * (sg * (1.0 + gg * (1.0 - sg)))).astype(BF16)
        du_ref[...] = (da * (gg * sg)).astype(BF16)

    tile = pl.BlockSpec((tm, tn), lambda i, j: (i, j))
    return pl.pallas_call(
        body, name=name, grid=(s // tm, f // tn),
        in_specs=[pl.BlockSpec((tm, d), lambda i, j: (i, 0)), pl.BlockSpec((tn, d), lambda i, j: (j, 0)), tile, tile],
        out_specs=[tile, tile],
        out_shape=[jax.ShapeDtypeStruct((s, f), BF16)] * 2,
        compiler_params=_cp("arbitrary", "arbitrary"))(dh, wd, gate, up)


def _mm_tn(a, bs, *, name, b_scale=1.0, ts=512):
    bs = list(bs) if isinstance(bs, (list, tuple)) else [bs]
    s, k = a.shape
    n = bs[0].shape[1]
    tn = _col_tile(k, n)
    per = n // tn

    def body(a_ref, *refs):
        b_refs, o_ref = refs[:-1], refs[-1]
        j = pl.program_id(0)

        @pl.when(pl.program_id(1) == 0)
        def _():
            o_ref[...] = jnp.zeros_like(o_ref)

        for m, b_ref in enumerate(b_refs):
            def acc(b_ref=b_ref):
                bv = b_ref[...]
                if b_scale != 1.0:
                    bv = bv * b_scale
                o_ref[...] += lax.dot_general(a_ref[...].astype(BF16), bv.astype(BF16), TN, preferred_element_type=F32)

            if len(b_refs) == 1:
                acc()
            else:
                pl.when(jnp.logical_and(j >= m * per, j < (m + 1) * per))(acc)

    def b_spec(m):
        def idx(j, t):
            mine = jnp.logical_and(j >= m * per, j < (m + 1) * per)
            return (jnp.where(mine, t, 0), jnp.clip(j - m * per, 0, per - 1))
        return pl.BlockSpec((ts, tn), idx)

    return pl.pallas_call(
        body, name=name, grid=(per * len(bs), s // ts),
        in_specs=[pl.BlockSpec((ts, k), lambda j, t: (t, 0))] + [b_spec(m) for m in range(len(bs))],
        out_specs=pl.BlockSpec((k, tn), lambda j, t: (0, j)),
        out_shape=jax.ShapeDtypeStruct((k, n * len(bs)), F32),
        compiler_params=_cp("arbitrary", "arbitrary"))(a, *bs)


def _mm_nt(dy, w, *, name, tm=512):
    s, n = dy.shape
    k = w.shape[0]

    def body(dy_ref, w_ref, o_ref):
        o_ref[...] = lax.dot_general(dy_ref[...].astype(BF16), w_ref[...], NT, preferred_element_type=F32)

    return pl.pallas_call(
        body, name=name, grid=(s // tm,),
        in_specs=[pl.BlockSpec((tm, n), lambda i: (i, 0)), pl.BlockSpec((k, n), lambda i: (0, 0))],
        out_specs=pl.BlockSpec((tm, k), lambda i: (i, 0)),
        out_shape=jax.ShapeDtypeStruct((s, k), F32),
        compiler_params=_cp("arbitrary"))(dy, w)


def _mm_nt_rmsbwd(pairs, x, g, dres, *, name, tm=256):
    s, k = x.shape
    npairs = len(pairs)
    pairs = [pr if len(pr) == 3 else (pr[0], pr[1], 0) for pr in pairs]

    def body(*refs):
        dy_refs = refs[0:2 * npairs:2]
        w_refs = refs[1:2 * npairs:2]
        rest = refs[2 * npairs:]
        x_ref, g_ref = rest[0], rest[1]
        if dres is None:
            dx_ref, dg_ref = rest[2], rest[3]
        else:
            dres_ref, dx_ref, dg_ref = rest[2], rest[3], rest[4]
        dxn = None
        for dy_ref, w_ref in zip(dy_refs, w_refs):
            t = lax.dot_general(dy_ref[...].astype(BF16), w_ref[...], NT, preferred_element_type=F32)
            dxn = t if dxn is None else dxn + t
        dx, dgrow = _rms_bwd(dxn, x_ref[...], g_ref[...])
        if dres is not None:
            dx = dx + dres_ref[...]
        dx_ref[...] = dx

        @pl.when(pl.program_id(0) == 0)
        def _():
            dg_ref[...] = jnp.zeros_like(dg_ref)

        dg_ref[...] += jnp.sum(dgrow, axis=0, keepdims=True)

    in_specs, args = [], []
    for dy, w, cb in pairs:
        n = dy.shape[1]
        in_specs += [pl.BlockSpec((tm, n), lambda i: (i, 0)), pl.BlockSpec((k, n), lambda i, cb=cb: (0, cb))]
        args += [dy, w]
    row = pl.BlockSpec((tm, k), lambda i: (i, 0))
    vec = pl.BlockSpec((1, k), lambda i: (0, 0))
    in_specs += [row, vec]
    args += [x, g]
    if dres is not None:
        in_specs.append(row)
        args.append(dres)
    return pl.pallas_call(
        body, name=name, grid=(s // tm,), in_specs=in_specs, out_specs=[row, vec],
        out_shape=[jax.ShapeDtypeStruct((s, k), F32), jax.ShapeDtypeStruct((1, k), F32)],
        compiler_params=_cp("arbitrary"))(*args)


def _ple_fwd(h, g, wg, p, wp, *, name, tm=512):
    s, d = h.shape
    pd = p.shape[1]

    def body(h_ref, g_ref, wg_ref, p_ref, wp_ref, o_ref, xn_ref, gate_ref, pp_ref):
        hv = h_ref[...]
        _, xhat = _rms_stats(hv)
        xn = (xhat * g_ref[...]).astype(BF16)
        xn_ref[...] = xn
        gate = _sigmoid(jnp.dot(xn, wg_ref[...], preferred_element_type=F32))
        pp = jnp.dot(p_ref[...].astype(BF16), wp_ref[...], preferred_element_type=F32)
        gate_ref[...] = gate.astype(BF16)
        pp_ref[...] = pp.astype(BF16)
        o_ref[...] = hv + gate * pp

    row = pl.BlockSpec((tm, d), lambda i: (i, 0))
    return pl.pallas_call(
        body, name=name, grid=(s // tm,),
        in_specs=[row, pl.BlockSpec((1, d), lambda i: (0, 0)), pl.BlockSpec((d, d), lambda i: (0, 0)),
                  pl.BlockSpec((tm, pd), lambda i: (i, 0)), pl.BlockSpec((pd, d), lambda i: (0, 0))],
        out_specs=[row, row, row, row],
        out_shape=[jax.ShapeDtypeStruct((s, d), F32)] + [jax.ShapeDtypeStruct((s, d), BF16)] * 3,
        compiler_params=_cp("arbitrary"))(h, g, wg, p, wp)


def _ple_bwd_elem(dh, gate, pp, *, name, tm=512):
    s, d = dh.shape

    def body(dh_ref, gate_ref, pp_ref, dz_ref, dpp_ref):
        dhv = dh_ref[...]
        gt = gate_ref[...].astype(F32)
        dz_ref[...] = (dhv * pp_ref[...].astype(F32) * (gt * (1.0 - gt))).astype(BF16)
        dpp_ref[...] = (dhv * gt).astype(BF16)

    row = pl.BlockSpec((tm, d), lambda i: (i, 0))
    return pl.pallas_call(
        body, name=name, grid=(s // tm,), in_specs=[row, row, row], out_specs=[row, row],
        out_shape=[jax.ShapeDtypeStruct((s, d), BF16)] * 2,
        compiler_params=_cp("arbitrary"))(dh, gate, pp)


def _final_loss(h, g, tgt, *, name, tm=512):
    s, d = h.shape

    def body(h_ref, g_ref, t_ref, loss_ref, dh_ref, dg_ref):
        @pl.when(pl.program_id(0) == 0)
        def _():
            loss_ref[...] = jnp.zeros_like(loss_ref)
            dg_ref[...] = jnp.zeros_like(dg_ref)

        hv = h_ref[...]
        gv = g_ref[...]
        _, xhat = _rms_stats(hv)
        err = xhat * gv - t_ref[...]
        per_row = jnp.mean(err * err, axis=-1, keepdims=True)
        loss_ref[...] += 0.5 * jnp.sum(per_row, axis=0, keepdims=True)
        dx, dgrow = _rms_bwd(err * (1.0 / d), hv, gv)
        dh_ref[...] = dx
        dg_ref[...] += jnp.sum(dgrow, axis=0, keepdims=True)

    row = pl.BlockSpec((tm, d), lambda i: (i, 0))
    vec = pl.BlockSpec((1, d), lambda i: (0, 0))
    return pl.pallas_call(
        body, name=name, grid=(s // tm,), in_specs=[row, vec, row],
        out_specs=[pl.BlockSpec((1, LANES), lambda i: (0, 0)), row, vec],
        out_shape=[jax.ShapeDtypeStruct((1, LANES), F32), jax.ShapeDtypeStruct((s, d), F32),
                   jax.ShapeDtypeStruct((1, d), F32)],
        compiler_params=_cp("arbitrary"))(h, g, tgt)


def _rope_fwd(y1, y2, cos, sin, *, name, tm=512):
    s, r = y1.shape

    def body(a_ref, b_ref, c_ref, s_ref, o_ref):
        o_ref[...] = a_ref[...] * c_ref[...] + b_ref[...] * s_ref[...]

    row = pl.BlockSpec((tm, r), lambda i: (i, 0))
    return pl.pallas_call(
        body, name=name, grid=(s // tm,), in_specs=[row] * 4, out_specs=row,
        out_shape=jax.ShapeDtypeStruct((s, r), F32), compiler_params=_cp("arbitrary"))(y1, y2, cos, sin)


def _rope_bwd(dout, cos, sin, *, name, tm=512):
    nh, s, r = dout.shape

    def body(d_ref, c_ref, s_ref, o1_ref, o2_ref):
        tot = d_ref[0]
        for hh in range(1, nh):
            tot = tot + d_ref[hh]
        o1_ref[...] = tot * c_ref[...]
        o2_ref[...] = tot * s_ref[...]

    row = pl.BlockSpec((tm, r), lambda i: (i, 0))
    return pl.pallas_call(
        body, name=name, grid=(s // tm,),
        in_specs=[pl.BlockSpec((nh, tm, r), lambda i: (0, i, 0)), row, row], out_specs=[row, row],
        out_shape=[jax.ShapeDtypeStruct((s, r), F32)] * 2, compiler_params=_cp("arbitrary"))(dout, cos, sin)


def _split3(v):
    h1 = v.astype(BF16)
    r1 = v - h1.astype(F32)
    h2 = r1.astype(BF16)
    h3 = (r1 - h2.astype(F32)).astype(BF16)
    return h1, h2, h3


def _tri(tb, upper):
    r = lax.broadcasted_iota(jnp.int32, (tb, tb), 0)
    c = lax.broadcasted_iota(jnp.int32, (tb, tb), 1)
    return jnp.where((r <= c) if upper else (r >= c), 1.0, 0.0).astype(BF16)


def _fox_gate_fwd(ft, bf, *, out_scale, name, tb=512):
    nh, s = ft.shape

    def body(f_ref, b_ref, o_ref, carry):
        @pl.when(pl.program_id(0) == 0)
        def _():
            carry[...] = jnp.zeros_like(carry)

        z = f_ref[...] + b_ref[...]
        lf = jnp.minimum(z, 0.0) - jnp.log(1.0 + jnp.exp(-jnp.abs(z)))
        tri = _tri(tb, True)
        cs = sum(jnp.dot(t, tri, preferred_element_type=F32) for t in _split3(lf)) + carry[...]
        for n, term in enumerate(_split3(cs * out_scale)):
            o_ref[n] = term
        carry[...] += jnp.sum(lf, axis=-1, keepdims=True)

    return pl.pallas_call(
        body, name=name, grid=(s // tb,),
        in_specs=[pl.BlockSpec((nh, tb), lambda t: (0, t)), pl.BlockSpec((nh, 1), lambda t: (0, 0))],
        out_specs=pl.BlockSpec((3, nh, tb), lambda t: (0, 0, t)),
        out_shape=jax.ShapeDtypeStruct((3, nh, s), BF16),
        scratch_shapes=[pltpu.VMEM((nh, 1), F32)], compiler_params=_cp("arbitrary"))(ft, bf)


def _fox_gate_bwd(drow, dcol, ft, bf, *, inv_scale, name, tb=512):
    nh, s = ft.shape
    nb = s // tb

    def body(dr_ref, dc_ref, f_ref, b_ref, df_ref, db_ref, carry):
        @pl.when(pl.program_id(0) == 0)
        def _():
            carry[...] = jnp.zeros_like(carry)
            db_ref[...] = jnp.zeros_like(db_ref)

        dc = (dr_ref[...] - dc_ref[...]) * inv_scale
        tri = _tri(tb, False)
        suf = sum(jnp.dot(t, tri, preferred_element_type=F32) for t in _split3(dc)) + carry[...]
        z = f_ref[...] + b_ref[...]
        dz = suf * (1.0 / (1.0 + jnp.exp(z)))
        df_ref[...] = dz
        db_ref[...] += jnp.sum(dz, axis=-1, keepdims=True)
        carry[...] += jnp.sum(dc, axis=-1, keepdims=True)

    rev = pl.BlockSpec((nh, tb), lambda t: (0, nb - 1 - t))
    one = pl.BlockSpec((nh, 1), lambda t: (0, 0))
    return pl.pallas_call(
        body, name=name, grid=(nb,), in_specs=[rev, rev, rev, one], out_specs=[rev, one],
        out_shape=[jax.ShapeDtypeStruct((nh, s), F32), jax.ShapeDtypeStruct((nh, 1), F32)],
        scratch_shapes=[pltpu.VMEM((nh, 1), F32)], compiler_params=_cp("arbitrary"))(drow, dcol, ft, bf)


def _tri_fwd(t, nq):
    i = sum((t >= (r * (r + 1)) // 2).astype(jnp.int32) for r in range(1, nq))
    return i, t - (i * (i + 1)) // 2


def _tri_bwd(t, nq):
    j = sum((t >= r * nq - (r * (r - 1)) // 2).astype(jnp.int32) for r in range(1, nq))
    return j, j + t - (j * nq - (j * (j - 1)) // 2)


def _scores_t(k, q, *, scale, diag):
    s = lax.dot_general(k, q, NT, preferred_element_type=F32) * scale
    if diag:
        r = lax.broadcasted_iota(jnp.int32, s.shape, 0)
        c = lax.broadcasted_iota(jnp.int32, s.shape, 1)
        s = jnp.where(r <= c, s, MASK_VALUE)
    return s


def _causal_fwd_t(q, k, vt, *, scale, name, tq, hb=2):
    nh, s, dq = q.shape
    dv = vt.shape[1]
    nq = s // tq
    nsteps = (nq * (nq + 1)) // 2

    def body(q_ref, k_ref, vt_ref, o_ref, lse_ref, m_sc, l_sc, acc_sc):
        i, j = _tri_fwd(pl.program_id(1), nq)

        @pl.when(j == 0)
        def _():
            m_sc[...] = jnp.full_like(m_sc, MASK_VALUE)
            l_sc[...] = jnp.zeros_like(l_sc)
            acc_sc[...] = jnp.zeros_like(acc_sc)

        def step(diag):
            for u in range(hb):
                sc = _scores_t(k_ref[u], q_ref[u], scale=scale, diag=diag)
                m_prev = m_sc[u]
                m_new = jnp.maximum(m_prev, jnp.max(sc, axis=0, keepdims=True))
                alpha = jnp.exp(m_prev - m_new)
                pr = jnp.exp(sc - m_new)
                l_new = alpha * l_sc[u] + jnp.sum(pr, axis=0, keepdims=True)
                acc = alpha * acc_sc[u] + jnp.dot(vt_ref[u], pr.astype(BF16), preferred_element_type=F32)
                if diag:
                    o_ref[u] = (acc / l_new).astype(BF16)
                    lse_ref[u] = m_new + jnp.log(l_new)
                else:
                    m_sc[u], l_sc[u], acc_sc[u] = m_new, l_new, acc

        pl.when(j < i)(functools.partial(step, False))
        pl.when(j == i)(functools.partial(step, True))

    def qi(t):
        return _tri_fwd(t, nq)[0]

    def kj(t):
        return _tri_fwd(t, nq)[1]

    return pl.pallas_call(
        body, name=name, grid=(nh // hb, nsteps),
        in_specs=[pl.BlockSpec((hb, tq, dq), lambda hp, t: (hp, qi(t), 0)),
                  pl.BlockSpec((hb, tq, dq), lambda hp, t: (hp, kj(t), 0)),
                  pl.BlockSpec((hb, dv, tq), lambda hp, t: (hp, 0, kj(t)))],
        out_specs=[pl.BlockSpec((hb, dv, tq), lambda hp, t: (hp, 0, qi(t))),
                   pl.BlockSpec((hb, 1, tq), lambda hp, t: (hp, 0, qi(t)))],
        out_shape=[jax.ShapeDtypeStruct((nh, dv, s), BF16), jax.ShapeDtypeStruct((nh, 1, s), F32)],
        scratch_shapes=[pltpu.VMEM((hb, 1, tq), F32), pltpu.VMEM((hb, 1, tq), F32), pltpu.VMEM((hb, dv, tq), F32)],
        compiler_params=_cp("arbitrary", "arbitrary"))(q, k, vt)


def _causal_bwd_t(q, k, v, ot, dot_, lse, *, scale, name, tq, hb=2):
    nh, s, dq = q.shape
    dv = v.shape[-1]
    nq = s // tq
    nsteps = (nq * (nq + 1)) // 2

    def body(q_ref, k_ref, v_ref, ot_ref, dot_ref, lse_ref, dq_ref, dk_ref, dvt_ref):
        t = pl.program_id(1)
        j, i = _tri_bwd(t, nq)

        @pl.when(t == 0)
        def _():
            dq_ref[...] = jnp.zeros_like(dq_ref)

        def step(diag):
            rows = pl.ds(pl.multiple_of(i * tq, tq), tq)
            for u in range(hb):
                qv, kv, dov = q_ref[u], k_ref[u], dot_ref[u]
                pr = jnp.exp(_scores_t(kv, qv, scale=scale, diag=diag) - lse_ref[u])
                dp = jnp.dot(v_ref[u], dov, preferred_element_type=F32)
                delta = jnp.sum(dov.astype(F32) * ot_ref[u].astype(F32), axis=0, keepdims=True)
                dsb = ((pr * (dp - delta)) * scale).astype(BF16)
                d_v = lax.dot_general(dov, pr.astype(BF16), NT, preferred_element_type=F32)
                d_k = jnp.dot(dsb, qv, preferred_element_type=F32)
                if diag:
                    dvt_ref[u], dk_ref[u] = d_v, d_k
                else:
                    dvt_ref[u] += d_v
                    dk_ref[u] += d_k
                dq_ref[u, rows, :] += lax.dot_general(dsb, kv, TN, preferred_element_type=F32)

        pl.when(i > j)(functools.partial(step, False))
        pl.when(i == j)(functools.partial(step, True))

    def qi(t):
        return _tri_bwd(t, nq)[1]

    def kj(t):
        return _tri_bwd(t, nq)[0]

    rows_q = pl.BlockSpec((hb, tq, dq), lambda hp, t: (hp, qi(t), 0))
    rows_k = pl.BlockSpec((hb, tq, dq), lambda hp, t: (hp, kj(t), 0))
    lanes_q = pl.BlockSpec((hb, dv, tq), lambda hp, t: (hp, 0, qi(t)))
    return pl.pallas_call(
        body, name=name, grid=(nh // hb, nsteps),
        in_specs=[rows_q, rows_k, pl.BlockSpec((hb, tq, dv), lambda hp, t: (hp, kj(t), 0)), lanes_q, lanes_q,
                  pl.BlockSpec((hb, 1, tq), lambda hp, t: (hp, 0, qi(t)))],
        out_specs=[pl.BlockSpec((hb, s, dq), lambda hp, t: (hp, 0, 0)), rows_k,
                   pl.BlockSpec((hb, dv, tq), lambda hp, t: (hp, 0, kj(t)))],
        out_shape=[jax.ShapeDtypeStruct((nh, s, dq), F32), jax.ShapeDtypeStruct((nh, s, dq), F32),
                   jax.ShapeDtypeStruct((nh, dv, s), F32)],
        compiler_params=_cp("arbitrary", "arbitrary"))(q, k, v, ot, dot_, lse)


def _swa_scores_t(k, q, dist, ok, *, scale, slope):
    s = lax.dot_general(k, q, NT, preferred_element_type=F32) * scale - slope * dist.astype(F32)
    return jnp.where(ok, s, MASK_VALUE)


def _swa_geometry(tb, w, has_other):
    r = lax.broadcasted_iota(jnp.int32, (tb, tb), 0)
    c = lax.broadcasted_iota(jnp.int32, (tb, tb), 1)
    d_same = c - r
    ok_same = jnp.logical_and(d_same >= 0, d_same < w)

    def other(ncols):
        rr = lax.broadcasted_iota(jnp.int32, (w, ncols), 0)
        cc = lax.broadcasted_iota(jnp.int32, (w, ncols), 1)
        dd = cc + w - rr
        return dd, jnp.logical_and(dd < w, has_other)

    return (d_same, ok_same), other


def _swa_fwd_t(q, k, vt, slopes_sinks, *, scale, window, name, tb=256):
    nh, s, d = q.shape
    nkv = k.shape[0]
    grp = nh // nkv
    w = window
    per = tb // w
    assert tb % w == 0

    def body(q_ref, kc_ref, kp_ref, vc_ref, vp_ref, ss_ref, o_ref, lse_ref):
        kvh, i = pl.program_id(0), pl.program_id(1)
        (d_c, ok_c), other = _swa_geometry(tb, w, i > 0)
        d_p, ok_p = other(tb)
        for g in range(grp):
            h = kvh * grp + g
            slope, sink = ss_ref[0, h], ss_ref[1, h]
            qg = q_ref[g]
            s_c = _swa_scores_t(kc_ref[...], qg, d_c, ok_c, scale=scale, slope=slope)
            s_p = _swa_scores_t(kp_ref[...], qg, d_p, ok_p, scale=scale, slope=slope)
            m = jnp.maximum(jnp.maximum(jnp.max(s_c, axis=0, keepdims=True), jnp.max(s_p, axis=0, keepdims=True)), sink)
            p_c, p_p = jnp.exp(s_c - m), jnp.exp(s_p - m)
            l = jnp.sum(p_c, axis=0, keepdims=True) + jnp.sum(p_p, axis=0, keepdims=True) + jnp.exp(sink - m)
            acc = (jnp.dot(vc_ref[...], p_c.astype(BF16), preferred_element_type=F32)
                   + jnp.dot(vp_ref[...], p_p.astype(BF16), preferred_element_type=F32))
            o_ref[g] = (acc / l).astype(BF16)
            lse_ref[g] = m + jnp.log(l)

    def prev(i):
        return jnp.maximum(i * per - 1, 0)

    return pl.pallas_call(
        body, name=name, grid=(nkv, s // tb),
        in_specs=[pl.BlockSpec((grp, tb, d), lambda kh, i: (kh, i, 0)),
                  pl.BlockSpec((None, tb, d), lambda kh, i: (kh, i, 0)),
                  pl.BlockSpec((None, w, d), lambda kh, i: (kh, prev(i), 0)),
                  pl.BlockSpec((None, d, tb), lambda kh, i: (kh, 0, i)),
                  pl.BlockSpec((None, d, w), lambda kh, i: (kh, 0, prev(i))),
                  pl.BlockSpec(memory_space=pltpu.SMEM)],
        out_specs=[pl.BlockSpec((grp, d, tb), lambda kh, i: (kh, 0, i)), pl.BlockSpec((grp, 1, tb), lambda kh, i: (kh, 0, i))],
        out_shape=[jax.ShapeDtypeStruct((nh, d, s), BF16), jax.ShapeDtypeStruct((nh, 1, s), F32)],
        compiler_params=_cp("arbitrary", "arbitrary"))(q, k, k, vt, vt, slopes_sinks)


def _swa_bwd_t(q, k, v, ot, dot_, lse, slopes_sinks, *, scale, window, name, tb=256):
    nh, s, d = q.shape
    nkv = k.shape[0]
    grp = nh // nkv
    w = window
    per = tb // w
    nb = s // tb

    def body(qc_ref, qn_ref, kc_ref, kp_ref, vc_ref, vp_ref, oc_ref, on_ref, doc_ref, don_ref, lc_ref, ln_ref, ss_ref,
             dq_ref, dk_ref, dvt_ref, dsink_ref):
        kvh, i = pl.program_id(0), pl.program_id(1)

        @pl.when(i == 0)
        def _():
            dsink_ref[...] = jnp.zeros_like(dsink_ref)

        (d_c, ok_c), other = _swa_geometry(tb, w, i > 0)
        d_p, ok_p = other(tb)
        d_n, ok_n = _swa_geometry(tb, w, i < nb - 1)[1](w)
        kc, kp, vc, vp = kc_ref[...], kp_ref[...], vc_ref[...], vp_ref[...]
        k_last, v_last = kc[tb - w:, :], vc[tb - w:, :]
        dk_acc = jnp.zeros((tb, d), F32)
        dv_acc = jnp.zeros((d, tb), F32)
        dk_tail = jnp.zeros((w, d), F32)
        dv_tail = jnp.zeros((d, w), F32)
        for g in range(grp):
            h = kvh * grp + g
            slope, sink = ss_ref[0, h], ss_ref[1, h]
            qg, dog, lse_c = qc_ref[g], doc_ref[g], lc_ref[g]
            delta = jnp.sum(dog.astype(F32) * oc_ref[g].astype(F32), axis=0, keepdims=True)
            p_c = jnp.exp(_swa_scores_t(kc, qg, d_c, ok_c, scale=scale, slope=slope) - lse_c)
            p_p = jnp.exp(_swa_scores_t(kp, qg, d_p, ok_p, scale=scale, slope=slope) - lse_c)
            ds_c = ((p_c * (jnp.dot(vc, dog, preferred_element_type=F32) - delta)) * scale).astype(BF16)
            ds_p = ((p_p * (jnp.dot(vp, dog, preferred_element_type=F32) - delta)) * scale).astype(BF16)
            dq_ref[g] = (lax.dot_general(ds_c, kc, TN, preferred_element_type=F32)
                         + lax.dot_general(ds_p, kp, TN, preferred_element_type=F32))
            dk_acc += jnp.dot(ds_c, qg, preferred_element_type=F32)
            dv_acc += lax.dot_general(dog, p_c.astype(BF16), NT, preferred_element_type=F32)
            dsink_ref[g] -= jnp.broadcast_to(jnp.sum(jnp.exp(sink - lse_c) * delta, axis=1, keepdims=True), (1, LANES))
            qn, don = qn_ref[g], don_ref[g]
            delta_n = jnp.sum(don.astype(F32) * on_ref[g].astype(F32), axis=0, keepdims=True)
            p_n = jnp.exp(_swa_scores_t(k_last, qn, d_n, ok_n, scale=scale, slope=slope) - ln_ref[g])
            ds_n = ((p_n * (jnp.dot(v_last, don, preferred_element_type=F32) - delta_n)) * scale).astype(BF16)
            dk_tail += jnp.dot(ds_n, qn, preferred_element_type=F32)
            dv_tail += lax.dot_general(don, p_n.astype(BF16), NT, preferred_element_type=F32)
        dk_ref[...] = dk_acc
        dvt_ref[...] = dv_acc
        dk_ref[tb - w:, :] += dk_tail
        dvt_ref[:, tb - w:] += dv_tail

    def prev(i):
        return jnp.maximum(i * per - 1, 0)

    def nxt(i):
        return jnp.minimum((i + 1) * per, s // w - 1)

    return pl.pallas_call(
        body, name=name, grid=(nkv, nb),
        in_specs=[pl.BlockSpec((grp, tb, d), lambda kh, i: (kh, i, 0)),
                  pl.BlockSpec((grp, w, d), lambda kh, i: (kh, nxt(i), 0)),
                  pl.BlockSpec((None, tb, d), lambda kh, i: (kh, i, 0)),
                  pl.BlockSpec((None, w, d), lambda kh, i: (kh, prev(i), 0)),
                  pl.BlockSpec((None, tb, d), lambda kh, i: (kh, i, 0)),
                  pl.BlockSpec((None, w, d), lambda kh, i: (kh, prev(i), 0)),
                  pl.BlockSpec((grp, d, tb), lambda kh, i: (kh, 0, i)),
                  pl.BlockSpec((grp, d, w), lambda kh, i: (kh, 0, nxt(i))),
                  pl.BlockSpec((grp, d, tb), lambda kh, i: (kh, 0, i)),
                  pl.BlockSpec((grp, d, w), lambda kh, i: (kh, 0, nxt(i))),
                  pl.BlockSpec((grp, 1, tb), lambda kh, i: (kh, 0, i)),
                  pl.BlockSpec((grp, 1, w), lambda kh, i: (kh, 0, nxt(i))),
                  pl.BlockSpec(memory_space=pltpu.SMEM)],
        out_specs=[pl.BlockSpec((grp, tb, d), lambda kh, i: (kh, i, 0)),
                   pl.BlockSpec((None, tb, d), lambda kh, i: (kh, i, 0)),
                   pl.BlockSpec((None, d, tb), lambda kh, i: (kh, 0, i)),
                   pl.BlockSpec((None, grp, 1, LANES), lambda kh, i: (kh, 0, 0, 0))],
        out_shape=[jax.ShapeDtypeStruct((nh, s, d), F32), jax.ShapeDtypeStruct((nkv, s, d), F32),
                   jax.ShapeDtypeStruct((nkv, d, s), F32), jax.ShapeDtypeStruct((nkv, grp, 1, LANES), F32)],
        compiler_params=_cp("arbitrary", "arbitrary"))(q, q, k, k, v, v, ot, ot, dot_, dot_, lse, lse, slopes_sinks)


def _adamw(w, g, m, v, *, name):
    shape = w.shape
    cols = shape[-1]
    rows = int(np.prod(shape[:-1])) if len(shape) > 1 else 1
    tr = _row_tile(rows, cols)
    c1 = 1.0 - ADAM_B1 ** ADAM_STEP
    c2 = 1.0 - ADAM_B2 ** ADAM_STEP

    def body(w_ref, g_ref, m_ref, v_ref, d_ref, mo_ref, vo_ref):
        gv = g_ref[...]
        mn = ADAM_B1 * m_ref[...] + (1.0 - ADAM_B1) * gv
        vn = ADAM_B2 * v_ref[...] + (1.0 - ADAM_B2) * (gv * gv)
        mo_ref[...] = mn
        vo_ref[...] = vn
        d_ref[...] = -ADAM_LR * ((mn / c1) / (jnp.sqrt(vn / c2) + ADAM_EPS) + ADAM_WD * w_ref[...])

    blk = pl.BlockSpec((tr, cols), lambda i: (i, 0))
    outs = pl.pallas_call(
        body, name=name, grid=(rows // tr,), in_specs=[blk] * 4, out_specs=[blk] * 3,
        out_shape=[jax.ShapeDtypeStruct((rows, cols), F32)] * 3,
        compiler_params=_cp("arbitrary"))(*[a.reshape(rows, cols) for a in (w, g, m, v)])
    return tuple(a.reshape(shape) for a in outs)


def _hbm_spec():
    return pl.BlockSpec(memory_space=pl.ANY)


def _mesh_place():
    x, y, c = lax.axis_index("x"), lax.axis_index("y"), lax.axis_index("c")
    return x, y, c, [(1 - x, y), (x, 1 - y), (1 - x, 1 - y)]


def _half_rows(c, rows, align):
    return pl.ds(pl.multiple_of(c * (rows // 2), align), rows // 2)


def _part(ref, mode, k, n, rows=None):
    if mode == "cols":
        cols = pl.ds(pl.multiple_of(k * n, LANES), n)
        return ref.at[:, cols] if rows is None else ref.at[rows, cols]
    return ref.at[k] if rows is None else ref.at[k, rows, :]


def _gather_weights(shards, modes, *, name):
    n_arr = len(shards)
    out_shape = [jax.ShapeDtypeStruct((s.shape[0], N_CHIPS * s.shape[1]) if m == "cols" else (N_CHIPS,) + s.shape, s.dtype)
                 for s, m in zip(shards, modes)]
    per = 7

    def body(*refs):
        srcs, dsts = refs[:n_arr], refs[n_arr:2 * n_arr]
        send_sems, recv_sems = refs[2 * n_arr:]
        x, y, c, chips = _mesh_place()
        me = 2 * x + y
        sends = []

        def copy(i, slot, src, dst, to):
            return pltpu.make_async_remote_copy(src_ref=src, dst_ref=dst, send_sem=send_sems.at[i * per + slot],
                                                recv_sem=recv_sems.at[i * per + slot], device_id=to, device_id_type=MESH)

        def half(i, k, hc, ref=None):
            r, n = shards[i].shape
            return _part(dsts[i] if ref is None else ref, modes[i], k, n, _half_rows(hc, r, 16))

        for i in range(n_arr):
            r, n = shards[i].shape
            mine = srcs[i].at[_half_rows(c, r, 16)]
            cps = [copy(i, 0, srcs[i], _part(dsts[i], modes[i], me, n), (x, y, 1 - c))]
            cps += [copy(i, 1 + j, mine, half(i, me, c), (px, py, c)) for j, (px, py) in enumerate(chips)]
            for cp in cps:
                cp.start()
            sends += cps
        for i in range(n_arr):
            for j, (px, py) in enumerate(chips):
                k = 2 * px + py
                copy(i, 1 + j, half(i, k, c), half(i, k, c), (px, py, c)).wait_recv()
                fwd = copy(i, 4 + j, half(i, k, c), half(i, k, c), (x, y, 1 - c))
                fwd.start()
                sends.append(fwd)
        for i in range(n_arr):
            n = shards[i].shape[1]
            own = _part(dsts[i], modes[i], me, n)
            copy(i, 0, own, own, (x, y, 1 - c)).wait_recv()
            for j, (px, py) in enumerate(chips):
                k = 2 * px + py
                copy(i, 4 + j, half(i, k, 1 - c), half(i, k, 1 - c), (x, y, 1 - c)).wait_recv()
        for cp in sends:
            cp.wait_send()

    return pl.pallas_call(
        body, name=name, in_specs=[_hbm_spec()] * n_arr, out_specs=[_hbm_spec()] * n_arr, out_shape=out_shape,
        scratch_shapes=[pltpu.SemaphoreType.DMA((n_arr * per,)), pltpu.SemaphoreType.DMA((n_arr * per,))])(*shards)


def _blk_view(a, mode):
    return a[None] if mode == "cols" else a


def _rs_pair_swap(arrs, modes, *, name):
    n_arr = len(arrs)
    out_shape = [jax.ShapeDtypeStruct((a.shape[0] // 2, a.shape[1]) if m == "cols" else (a.shape[0], a.shape[1] // 2, a.shape[2]), a.dtype)
                 for a, m in zip(arrs, modes)]

    def body(*refs):
        srcs, dsts = refs[:n_arr], refs[n_arr:2 * n_arr]
        send_sems, recv_sems = refs[2 * n_arr:]
        x, y, c, _ = _mesh_place()
        cps = []
        for i in range(n_arr):
            if modes[i] == "cols":
                src = srcs[i].at[_half_rows(1 - c, arrs[i].shape[0], 8)]
            else:
                src = srcs[i].at[:, _half_rows(1 - c, arrs[i].shape[1], 8), :]
            cps.append(pltpu.make_async_remote_copy(src_ref=src, dst_ref=dsts[i], send_sem=send_sems.at[i],
                                                    recv_sem=recv_sems.at[i], device_id=(x, y, 1 - c), device_id_type=MESH))
        for cp in cps:
            cp.start()
        for cp in cps:
            cp.wait()

    return pl.pallas_call(
        body, name=name, in_specs=[_hbm_spec()] * n_arr, out_specs=[_hbm_spec()] * n_arr, out_shape=out_shape,
        scratch_shapes=[pltpu.SemaphoreType.DMA((n_arr,)), pltpu.SemaphoreType.DMA((n_arr,))])(*arrs)


def _rs_pair_add(arr, landed, place, *, name):
    nb, r, c = arr.shape
    rh = r // 2
    tr = _row_tile(rh, c)
    nt = rh // tr

    def body(p_ref, a_ref, l_ref, o_ref):
        o_ref[...] = (a_ref[...] + l_ref[...]).astype(BF16)

    grid_spec = pltpu.PrefetchScalarGridSpec(
        num_scalar_prefetch=1, grid=(nb, nt),
        in_specs=[pl.BlockSpec((None, tr, c), lambda b, t, p_ref: (b, p_ref[1] * nt + t, 0)),
                  pl.BlockSpec((None, tr, c), lambda b, t, p_ref: (b, t, 0))],
        out_specs=pl.BlockSpec((None, tr, c), lambda b, t, p_ref: (b, t, 0)))
    return pl.pallas_call(
        body, name=name, grid_spec=grid_spec, out_shape=jax.ShapeDtypeStruct((nb, rh, c), BF16),
        compiler_params=_cp("arbitrary", "arbitrary"))(place, arr, landed)


def _rs_chip_exchange(parts, modes, *, name):
    n_arr = len(parts)
    out_shape = []
    for a, m in zip(parts, modes):
        shp = (a.shape[0], a.shape[1] // N_CHIPS) if m == "cols" else a.shape[1:]
        out_shape.append(jax.ShapeDtypeStruct((3,) + shp, a.dtype))

    def body(*refs):
        srcs, dsts = refs[:n_arr], refs[n_arr:2 * n_arr]
        send_sems, recv_sems = refs[2 * n_arr:]
        x, y, c, chips = _mesh_place()
        cps = []
        for i in range(n_arr):
            n = out_shape[i].shape[-1]
            for j, (px, py) in enumerate(chips):
                cps.append(pltpu.make_async_remote_copy(
                    src_ref=_part(srcs[i], modes[i], 2 * px + py, n), dst_ref=dsts[i].at[j],
                    send_sem=send_sems.at[3 * i + j], recv_sem=recv_sems.at[3 * i + j],
                    device_id=(px, py, c), device_id_type=MESH))
        for cp in cps:
            cp.start()
        for cp in cps:
            cp.wait()

    return pl.pallas_call(
        body, name=name, in_specs=[_hbm_spec()] * n_arr, out_specs=[_hbm_spec()] * n_arr, out_shape=out_shape,
        scratch_shapes=[pltpu.SemaphoreType.DMA((3 * n_arr,)), pltpu.SemaphoreType.DMA((3 * n_arr,))])(*parts)


def _rs_chip_sum(part, landed, mode, place, *, name):
    _, rh, n = landed.shape
    tr = _row_tile(rh, n)
    nt = rh // tr

    def body(p_ref, a_ref, l_ref, o_ref):
        o_ref[...] = ((a_ref[...].astype(F32) + l_ref[0].astype(F32)) + l_ref[1].astype(F32)) + l_ref[2].astype(F32)

    if mode == "cols":
        own = pl.BlockSpec((tr, n), lambda t, p_ref: (t, p_ref[0]))
    else:
        own = pl.BlockSpec((None, tr, n), lambda t, p_ref: (p_ref[0], t, 0))
    grid_spec = pltpu.PrefetchScalarGridSpec(
        num_scalar_prefetch=1, grid=(nt,),
        in_specs=[own, pl.BlockSpec((3, tr, n), lambda t, p_ref: (0, t, 0))],
        out_specs=pl.BlockSpec((tr, n), lambda t, p_ref: (p_ref[1] * nt + t, 0)))
    return pl.pallas_call(
        body, name=name, grid_spec=grid_spec, out_shape=jax.ShapeDtypeStruct((2 * rh, n), F32),
        compiler_params=_cp("arbitrary"))(place, part, landed)


def _rs_pair_join(halves, *, name):
    n_arr = len(halves)

    def body(*refs):
        outs = refs[n_arr:2 * n_arr]
        send_sems, recv_sems = refs[2 * n_arr:]
        x, y, c, _ = _mesh_place()
        cps = []
        for i in range(n_arr):
            rows = _half_rows(c, halves[i].shape[0], 8)
            cps.append(pltpu.make_async_remote_copy(src_ref=outs[i].at[rows], dst_ref=outs[i].at[rows], send_sem=send_sems.at[i],
                                                    recv_sem=recv_sems.at[i], device_id=(x, y, 1 - c), device_id_type=MESH))
        for cp in cps:
            cp.start()
        for i, cp in enumerate(cps):
            cp.wait_send()
            theirs = outs[i].at[_half_rows(1 - c, halves[i].shape[0], 8)]
            pltpu.make_async_remote_copy(src_ref=theirs, dst_ref=theirs, send_sem=send_sems.at[i], recv_sem=recv_sems.at[i],
                                         device_id=(x, y, 1 - c), device_id_type=MESH).wait_recv()

    return pl.pallas_call(
        body, name=name, in_specs=[_hbm_spec()] * n_arr, out_specs=[_hbm_spec()] * n_arr,
        out_shape=[jax.ShapeDtypeStruct(h.shape, h.dtype) for h in halves],
        input_output_aliases={i: i for i in range(n_arr)},
        scratch_shapes=[pltpu.SemaphoreType.DMA((n_arr,)), pltpu.SemaphoreType.DMA((n_arr,))])(*halves)


def _allreduce_small(v, *, name):
    r, c = v.shape

    def body(v_ref, o_ref, gath, send_sems, recv_sems):
        x, y, cc, _ = _mesh_place()
        me = 4 * x + 2 * y + cc
        gath[me] = v_ref[...]
        cps = []
        for rel in range(1, 8):
            px = 1 - x if rel & 4 else x
            py = 1 - y if rel & 2 else y
            pc = 1 - cc if rel & 1 else cc

            def copy(slot, px=px, py=py, pc=pc, rel=rel):
                return pltpu.make_async_remote_copy(
                    src_ref=v_ref, dst_ref=gath.at[slot], send_sem=send_sems.at[rel - 1],
                    recv_sem=recv_sems.at[rel - 1], device_id=(px, py, pc), device_id_type=MESH)

            cps.append((copy(me), copy(4 * px + 2 * py + pc)))
        for send, _ in cps:
            send.start()
        for send, theirs in cps:
            theirs.wait_recv()
            send.wait_send()
        tot = gath[0]
        for d in range(1, 8):
            tot = tot + gath[d]
        o_ref[...] = tot

    vm = pl.BlockSpec(memory_space=pltpu.VMEM)
    return pl.pallas_call(
        body, name=name, in_specs=[vm], out_specs=vm, out_shape=jax.ShapeDtypeStruct((r, c), F32),
        scratch_shapes=[pltpu.VMEM((8, r, c), F32), pltpu.SemaphoreType.DMA((7,)), pltpu.SemaphoreType.DMA((7,))])(v)


def _to_heads(a, nh, dh, dtype=BF16):
    return a.reshape(a.shape[0], nh, dh).transpose(1, 0, 2).astype(dtype)


def _from_heads(a):
    return a.transpose(1, 0, 2).reshape(a.shape[1], -1)


def _rope_tables(s, reps):
    half = B_ROPE // 2
    inv = ROPE_THETA ** (-jnp.arange(0, B_ROPE, 2, dtype=F32) / B_ROPE)
    ang = jnp.arange(s, dtype=F32)[:, None] * inv[None, :]
    return jnp.tile(jnp.cos(ang), (1, reps)), jnp.tile(jnp.sin(ang), (1, reps))


def _alibi_slopes():
    return 2.0 ** (-8.0 * jnp.arange(1, A_HEADS + 1, dtype=F32) / A_HEADS)


def _ffn_fwd(h, norm, wts, tag):
    gate, up, act, xn = _ffn_up(h, norm, wts["wgu"], name=f"{tag}_up")
    out = _mm_res_fwd(act, wts["wd"], h, scale=FFN_RES_SCALE, name=f"{tag}_down")
    return out, dict(h_in=h, gate=gate, up=up, act=act, xn=xn)


def _ffn_bwd(dh, norm, wts, sv, tag):
    dgate, dup = _ffn_down_bwd(dh, wts["wd"], sv["gate"], sv["up"], scale=FFN_RES_SCALE, name=f"{tag}_down_bwd")
    d_wd = _mm_tn(sv["act"], dh, b_scale=FFN_RES_SCALE, name=f"{tag}_dwd")
    d_wgu = _mm_tn(sv["xn"], [dgate, dup], name=f"{tag}_dwgu")
    dh_in, dnorm = _mm_nt_rmsbwd([(dgate, wts["wgu"], 0), (dup, wts["wgu"], 1)], sv["h_in"], norm, dh,
                                 name=f"{tag}_dx")
    return dh_in, dnorm, d_wgu, d_wd


def _even_weights(w_in, w_uq, w_ukv):
    half = B_ROPE // 2
    base = w_in.shape[1]
    kr1, kr2 = w_in[:, base - B_ROPE:base - half], w_in[:, base - half:]
    w_in_cat = jnp.concatenate([w_in, -kr2, kr1, jnp.zeros((w_in.shape[0], 64), w_in.dtype)], axis=1)
    u3 = w_uq.reshape(w_uq.shape[0], B_HEADS, B_NOPE + B_ROPE)
    nope = u3[:, :, :B_NOPE].reshape(w_uq.shape[0], -1)
    r1 = u3[:, :, B_NOPE:B_NOPE + half].reshape(w_uq.shape[0], -1)
    r2 = u3[:, :, B_NOPE + half:].reshape(w_uq.shape[0], -1)
    w_q_cat = jnp.concatenate([nope, r1, r2, -r2, r1], axis=1)
    return w_in_cat, w_q_cat, w_ukv


def _even_fwd(h, w, i):
    s = h.shape[0]
    half = B_ROPE // 2
    ycat, xn = _rms_mm_fwd(h, w["mix_norm"][i:i + 1], w["ev_in_cat"], name="ev_in")
    a_q, a_k, a_v = ycat[:, :512], ycat[:, 512:640], ycat[:, 640:768]
    c_q, c_kv = ycat[:, 768:1024], ycat[:, 1024:1152]
    cos32, sin32 = _rope_tables(s, 2)
    kro = _rope_fwd(ycat[:, 1152:1184], ycat[:, 1184:1216], cos32, sin32, name="ev_k_rope")
    qa, ka, va = _to_heads(a_q, A_HEADS, A_HEAD_DIM), _to_heads(a_k, A_KV_HEADS, A_HEAD_DIM), _to_heads(a_v, A_KV_HEADS, A_HEAD_DIM)
    ss = jnp.stack([_alibi_slopes(), w["ev_sinks"].reshape(-1)])
    oa, lse_a = _swa_fwd_t(qa, ka, va.transpose(0, 2, 1), ss, scale=A_HEAD_DIM ** -0.5, window=WINDOW, name="swa_fwd")
    yq, xn_q = _rms_mm_fwd(c_q, w["ev_cq_norm"], w["ev_q_cat"], name="ev_q_up")
    cos256, sin256 = _rope_tables(s, 2 * B_HEADS)
    qro = _rope_fwd(yq[:, 512:768], yq[:, 768:1024], cos256, sin256, name="ev_q_rope")
    ykv, xn_kv = _rms_mm_fwd(c_kv, w["ev_ckv_norm"], w["ev_ukv"], name="ev_kv_up")
    zq = jnp.zeros((s, B_HEADS, LANES - B_NOPE - B_ROPE), F32)
    qb = jnp.concatenate([yq[:, :512].reshape(s, B_HEADS, B_NOPE), qro[:, :128].reshape(s, B_HEADS, half),
                          qro[:, 128:].reshape(s, B_HEADS, half), zq], axis=-1).transpose(1, 0, 2).astype(BF16)
    kv3 = ykv.reshape(s, B_HEADS, B_NOPE + B_V)
    kb = jnp.concatenate([kv3[:, :, :B_NOPE], jnp.broadcast_to(kro[:, None, :], (s, B_HEADS, B_ROPE)), zq],
                         axis=-1).transpose(1, 0, 2).astype(BF16)
    vb = kv3[:, :, B_NOPE:].transpose(1, 0, 2).astype(BF16)
    ob, lse_b = _causal_fwd_t(qb, kb, vb.transpose(0, 2, 1), scale=(B_NOPE + B_ROPE) ** -0.5, name="mla_fwd", tq=512)
    attn = jnp.concatenate([oa.transpose(2, 0, 1).reshape(s, -1), ob.transpose(2, 0, 1).reshape(s, -1)], axis=-1)
    out = _mm_res_fwd(attn, w["ev_out"], h, scale=1.0, name="ev_out")
    sv = dict(h_in=h, xn=xn, c_q=c_q, c_kv=c_kv, xn_q=xn_q, xn_kv=xn_kv, qa=qa, ka=ka, va=va, oa=oa, lse_a=lse_a,
              ss=ss, qb=qb, kb=kb, vb=vb, ob=ob, lse_b=lse_b, attn=attn, cos32=cos32, sin32=sin32,
              cos256=cos256, sin256=sin256)
    return out, sv


def _even_bwd(dh, w, sv, i):
    s = dh.shape[0]
    half = B_ROPE // 2
    g = {}
    dattn = _mm_nt(dh, w["ev_out"], name="ev_out_dx")
    g["ev_w_out"] = _mm_tn(sv["attn"], dh, name="ev_out_dw")
    doa = dattn[:, :512].reshape(s, A_HEADS, A_HEAD_DIM).transpose(1, 2, 0).astype(BF16)
    dob = dattn[:, 512:].reshape(s, B_HEADS, B_V).transpose(1, 2, 0).astype(BF16)
    dqa, dka, dva, dsink = _swa_bwd_t(sv["qa"], sv["ka"], sv["va"], sv["oa"], doa, sv["lse_a"], sv["ss"],
                                      scale=A_HEAD_DIM ** -0.5, window=WINDOW, name="swa_bwd")
    g["ev_sinks"] = dsink[:, :, 0, 0].reshape(1, A_HEADS)
    dqb, dkb, dvb = _causal_bwd_t(sv["qb"], sv["kb"], sv["vb"], sv["ob"], dob, sv["lse_b"],
                                  scale=(B_NOPE + B_ROPE) ** -0.5, name="mla_bwd", tq=512)
    dq_r1 = dqb[:, :, B_NOPE:B_NOPE + half].transpose(1, 0, 2).reshape(s, -1)
    dq_r2 = dqb[:, :, B_NOPE + half:B_NOPE + B_ROPE].transpose(1, 0, 2).reshape(s, -1)
    dq1, dq2 = _rope_bwd(jnp.concatenate([dq_r1, dq_r2], axis=-1)[None], sv["cos256"], sv["sin256"], name="ev_q_rope_bwd")
    dyq = jnp.concatenate([_from_heads(dqb[:, :, :B_NOPE]), dq1, dq2], axis=-1)
    dwq = _mm_tn(sv["xn_q"], dyq, name="ev_q_up_dw")
    dcq, g["ev_cq_norm"] = _mm_nt_rmsbwd([(dyq, w["ev_q_cat"])], sv["c_q"], w["ev_cq_norm"], None, name="ev_q_up_dx")
    kq = sv["c_q"].shape[1]
    d_nope = dwq[:, :512].reshape(kq, B_HEADS, B_NOPE)
    d_r1 = (dwq[:, 512:640] + dwq[:, 896:1024]).reshape(kq, B_HEADS, half)
    d_r2 = (dwq[:, 640:768] - dwq[:, 768:896]).reshape(kq, B_HEADS, half)
    g["ev_w_uq"] = jnp.concatenate([d_nope, d_r1, d_r2], axis=-1).reshape(kq, -1)
    dykv = jnp.concatenate([dkb[:, :, :B_NOPE].transpose(1, 0, 2), dvb.transpose(2, 0, 1)], axis=-1).reshape(s, -1)
    g["ev_w_ukv"] = _mm_tn(sv["xn_kv"], dykv, name="ev_kv_up_dw")
    dckv, g["ev_ckv_norm"] = _mm_nt_rmsbwd([(dykv, w["ev_ukv"])], sv["c_kv"], w["ev_ckv_norm"], None, name="ev_kv_up_dx")
    dk1, dk2 = _rope_bwd(dkb[:, :, B_NOPE:B_NOPE + B_ROPE], sv["cos32"], sv["sin32"], name="ev_k_rope_bwd")
    dycat = jnp.concatenate([_from_heads(dqa), _from_heads(dka), dva.transpose(2, 0, 1).reshape(s, -1),
                             dcq, dckv, dk1, dk2, jnp.zeros((s, 64), F32)], axis=-1)
    dwin = _mm_tn(sv["xn"], dycat, name="ev_in_dw")
    base = 1184
    g["ev_w_in"] = jnp.concatenate([dwin[:, :base - B_ROPE],
                                    dwin[:, base - B_ROPE:base - half] + dwin[:, base + half:base + B_ROPE],
                                    dwin[:, base - half:base] - dwin[:, base:base + half]], axis=-1)
    dh_in, dnorm = _mm_nt_rmsbwd([(dycat, w["ev_in_cat"])], sv["h_in"], w["mix_norm"][i:i + 1], dh, name="ev_in_dx")
    return dh_in, dnorm, g


def _odd_fwd(h, w, i):
    s = h.shape[0]
    wd = C_HEADS * C_HEAD_DIM
    y, xn = _rms_mm_fwd(h, w["mix_norm"][i:i + 1], w["od_in_pad"], name="od_in")
    scale = C_HEAD_DIM ** -0.5
    ft = y[:, 3 * wd:3 * wd + C_HEADS].T
    bf = w["od_b_f"].reshape(C_HEADS, 1)
    cb3 = _fox_gate_fwd(ft, bf, out_scale=-1.0 / scale, name="fox_gate_fwd")
    ones, zeros = jnp.ones((s, C_HEADS, 1), BF16), jnp.zeros((s, C_HEADS, 1), BF16)
    tail = jnp.zeros((s, C_HEADS, LANES - C_HEAD_DIM - 5), BF16)
    q3 = y[:, :wd].reshape(s, C_HEADS, C_HEAD_DIM).astype(BF16)
    k3 = y[:, wd:2 * wd].reshape(s, C_HEADS, C_HEAD_DIM).astype(BF16)
    q = jnp.concatenate([q3, ones, zeros, ones, ones, ones, tail], axis=-1).transpose(1, 0, 2)
    k = jnp.concatenate([k3, zeros, ones, cb3.transpose(2, 1, 0), tail], axis=-1).transpose(1, 0, 2)
    v = _to_heads(y[:, 2 * wd:3 * wd], C_HEADS, C_HEAD_DIM)
    o, lse = _causal_fwd_t(q, k, v.transpose(0, 2, 1), scale=scale, name="fox_fwd", tq=512)
    attn = o.transpose(2, 0, 1).reshape(s, -1)
    out = _mm_res_fwd(attn, w["od_out"], h, scale=1.0, name="od_out")
    return out, dict(h_in=h, xn=xn, q=q, k=k, v=v, o=o, lse=lse, ft=ft, bf=bf, attn=attn)


def _odd_bwd(dh, w, sv, i):
    s = dh.shape[0]
    g = {}
    dattn = _mm_nt(dh, w["od_out"], name="od_out_dx")
    g["od_w_out"] = _mm_tn(sv["attn"], dh, name="od_out_dw")
    do = dattn.reshape(s, C_HEADS, C_HEAD_DIM).transpose(1, 2, 0).astype(BF16)
    scale = C_HEAD_DIM ** -0.5
    dq, dk, dv = _causal_bwd_t(sv["q"], sv["k"], sv["v"], sv["o"], do, sv["lse"], scale=scale, name="fox_bwd", tq=512)
    dft, dbf = _fox_gate_bwd(dq[:, :, C_HEAD_DIM + 1], dk[:, :, C_HEAD_DIM], sv["ft"], sv["bf"],
                             inv_scale=1.0 / scale, name="fox_gate_bwd")
    dq, dk = dq[:, :, :C_HEAD_DIM], dk[:, :, :C_HEAD_DIM]
    g["od_b_f"] = dbf.reshape(1, C_HEADS)
    n_pad = w["od_in_pad"].shape[1]
    n_real = 3 * C_HEADS * C_HEAD_DIM + C_HEADS
    dy = jnp.concatenate([_from_heads(dq), _from_heads(dk), dv.transpose(2, 0, 1).reshape(s, -1), dft.T,
                          jnp.zeros((s, n_pad - n_real), F32)], axis=-1)
    g["od_w_in"] = _mm_tn(sv["xn"], dy, name="od_in_dw")[:, :n_real]
    dh_in, dnorm = _mm_nt_rmsbwd([(dy, w["od_in_pad"])], sv["h_in"], w["mix_norm"][i:i + 1], dh, name="od_in_dx")
    return dh_in, dnorm, g


def _kernel_weights(full, replicated):
    depth = len(full["ffa_w_down"])
    w = dict(replicated)
    for tag in ("ffa", "ffb"):
        w[tag] = [dict(wgu=full[tag + "_w_gate_up"][i], wd=full[tag + "_w_down"][i]) for i in range(depth)]
    w["ple_gate"], w["ple_proj"] = full["ple_w_gate"], full["ple_w_proj"]
    w["ev_in_cat"], w["ev_q_cat"], w["ev_ukv"] = _even_weights(full["ev_w_in"][0], full["ev_w_uq"][0], full["ev_w_ukv"][0])
    w["ev_out"], w["od_out"] = full["ev_w_out"][0], full["od_w_out"][0]
    od_in = full["od_w_in"][0]
    w["od_in_pad"] = jnp.pad(od_in, ((0, 0), (0, (-od_in.shape[1]) % LANES)))
    return w


def _local_step(x, p, tgt, w):
    depth = p.shape[0]
    h = x
    saved = []
    for i in range(depth):
        sv = {}
        h, sv["ffa"] = _ffn_fwd(h, w["ffa_norm"][i:i + 1], w["ffa"][i], f"ffa{i}")
        if i % 2 == 0:
            h, sv["mix"] = _even_fwd(h, w, i)
        else:
            h, sv["mix"] = _odd_fwd(h, w, i)
        h, sv["ffb"] = _ffn_fwd(h, w["ffb_norm"][i:i + 1], w["ffb"][i], f"ffb{i}")
        h_in = h
        h, xn, gate, pp = _ple_fwd(h, w["ple_norm"][i:i + 1], w["ple_gate"][i], p[i], w["ple_proj"][i], name=f"ple{i}")
        sv["ple"] = dict(h_in=h_in, xn=xn, gate=gate, pp=pp)
        saved.append(sv)
    loss_vec, dh, d_final = _final_loss(h, w["final_norm"].reshape(1, -1), tgt, name="final_loss")

    per_layer = [dict() for _ in range(depth)]
    grads = {}
    for i in reversed(range(depth)):
        sv, gl = saved[i], per_layer[i]
        dz, dpp = _ple_bwd_elem(dh, sv["ple"]["gate"], sv["ple"]["pp"], name=f"ple{i}_bwd")
        gl["ple_w_gate"] = _mm_tn(sv["ple"]["xn"], dz, name=f"ple{i}_dwg")
        gl["ple_w_proj"] = _mm_tn(p[i], dpp, name=f"ple{i}_dwp")
        dh, gl["ple_norm"] = _mm_nt_rmsbwd([(dz, w["ple_gate"][i])], sv["ple"]["h_in"], w["ple_norm"][i:i + 1], dh,
                                           name=f"ple{i}_dx")
        dh, gl["ffb_norm"], gl["ffb_w_gate_up"], gl["ffb_w_down"] = _ffn_bwd(dh, w["ffb_norm"][i:i + 1], w["ffb"][i], sv["ffb"], f"ffb{i}")
        if i % 2 == 0:
            dh, gl["mix_norm"], gm = _even_bwd(dh, w, sv["mix"], i)
        else:
            dh, gl["mix_norm"], gm = _odd_bwd(dh, w, sv["mix"], i)
        grads.update({n: (g if n in REPLICATED else [g]) for n, g in gm.items()})
        dh, gl["ffa_norm"], gl["ffa_w_gate_up"], gl["ffa_w_down"] = _ffn_bwd(dh, w["ffa_norm"][i:i + 1], w["ffa"][i], sv["ffa"], f"ffa{i}")
    grads["final_norm"] = d_final.reshape(-1)
    for n in ("ffa_norm", "mix_norm", "ffb_norm", "ple_norm"):
        grads[n] = jnp.concatenate([per_layer[i][n] for i in range(depth)], axis=0)
    for n in ("ffa_w_gate_up", "ffa_w_down", "ffb_w_gate_up", "ffb_w_down", "ple_w_gate", "ple_w_proj"):
        grads[n] = [per_layer[i][n] for i in range(depth)]
    return loss_vec[0, 0], dh, grads


def _cut_mode(local_shape, axis, ncols):
    return "cols" if axis == 2 and ncols % LANES == 0 else "blk"


def _small_rows(vals):
    rows = []
    for n in REPLICATED:
        v = vals[n].reshape(-1)
        rows.append(jnp.pad(v, (0, (-v.shape[0]) % FLAT_COLS)).reshape(-1, FLAT_COLS))
    out = jnp.concatenate(rows, axis=0)
    return jnp.pad(out, ((0, (-out.shape[0]) % 8), (0, 0)))


def kernel(x, p, ffa_norm, ffa_w_gate_up, ffa_w_down, mix_norm, ffb_norm, ffb_w_gate_up, ffb_w_down, ple_norm, ple_w_gate, ple_w_proj, ev_w_in, ev_sinks, ev_cq_norm, ev_w_uq, ev_ckv_norm, ev_w_ukv, ev_w_out, od_w_in, od_b_f, od_w_out, final_norm, loss_target, m_ffa_norm, m_ffa_w_gate_up, m_ffa_w_down, m_mix_norm, m_ffb_norm, m_ffb_w_gate_up, m_ffb_w_down, m_ple_norm, m_ple_w_gate, m_ple_w_proj, m_ev_w_in, m_ev_sinks, m_ev_cq_norm, m_ev_w_uq, m_ev_ckv_norm, m_ev_w_ukv, m_ev_w_out, m_od_w_in, m_od_b_f, m_od_w_out, m_final_norm, v_ffa_norm, v_ffa_w_gate_up, v_ffa_w_down, v_mix_norm, v_ffb_norm, v_ffb_w_gate_up, v_ffb_w_down, v_ple_norm, v_ple_w_gate, v_ple_w_proj, v_ev_w_in, v_ev_sinks, v_ev_cq_norm, v_ev_w_uq, v_ev_ckv_norm, v_ev_w_ukv, v_ev_w_out, v_od_w_in, v_od_b_f, v_od_w_out, v_final_norm):
    env = dict(locals())
    wts = {n: env[n] for n in WEIGHT_ORDER}
    mom1 = {n: env["m_" + n] for n in WEIGHT_ORDER}
    mom2 = {n: env["v_" + n] for n in WEIGHT_ORDER}
    place = jnp.stack([2 * lax.axis_index("x") + lax.axis_index("y"), lax.axis_index("c")]).astype(jnp.int32)

    plan, shards = [], []
    for n, axis in SHARDED:
        wb = wts[n].astype(BF16)
        mode = _cut_mode(wb.shape, axis, wb.shape[2])
        for i in range(wb.shape[0]):
            plan.append((n, i, mode, axis))
            shards.append(wb[i])
    modes = [m for _, _, m, _ in plan]
    gathered = _gather_weights(shards, modes, name="weight_allgather")
    full = {n: [] for n, _ in SHARDED}
    for (n, i, mode, axis), dst in zip(plan, gathered):
        if mode == "blk":
            dst = dst.reshape(-1, dst.shape[2]) if axis == 1 else jnp.moveaxis(dst, 0, 1).reshape(dst.shape[1], -1)
        full[n].append(dst)

    w = _kernel_weights(full, {n: wts[n] for n in REPLICATED})
    loss_part, grad_x, grads = _local_step(x[0], p[:, 0], loss_target[0], w)
    loss = lax.psum(loss_part, ("x", "y", "c"))

    arrs = []
    for n, i, mode, axis in plan:
        g2 = grads[n][i]
        if mode == "blk":
            rr, cc = wts[n].shape[1:]
            g2 = g2.reshape(N_CHIPS, rr, cc) if axis == 1 else g2.reshape(rr, N_CHIPS, cc).transpose(1, 0, 2)
        arrs.append(g2)
    landed = _rs_pair_swap(arrs, modes, name="rs_pair_swap")
    parts = []
    for (n, i, mode, _), a, l in zip(plan, arrs, landed):
        pt = _rs_pair_add(_blk_view(a, mode), _blk_view(l, mode), place, name=f"rs_pair_add_{n}{i}")
        parts.append(pt[0] if mode == "cols" else pt)
    landed = _rs_chip_exchange(parts, modes, name="rs_chip_exchange")
    halves = [_rs_chip_sum(pt, l, mode, place, name=f"rs_chip_sum_{n}{i}")
              for (n, i, mode, _), pt, l in zip(plan, parts, landed)]
    reduced = _rs_pair_join(halves, name="rs_pair_join")
    gout = {n: [] for n, _ in SHARDED}
    for (n, _, _, _), r2 in zip(plan, reduced):
        gout[n].append(r2)
    gout = {n: jnp.stack(v).reshape(wts[n].shape) for n, v in gout.items()}
    small = _allreduce_small(_small_rows(grads), name="small_allreduce")
    r0 = 0
    for n in REPLICATED:
        size = int(np.prod(wts[n].shape))
        nr = -(-size // FLAT_COLS)
        gout[n] = small[r0:r0 + nr].reshape(-1)[:size].reshape(wts[n].shape)
        r0 += nr

    delta, new_m, new_v = {}, {}, {}
    for n in WEIGHT_ORDER:
        delta[n], new_m[n], new_v[n] = _adamw(wts[n], gout[n], mom1[n], mom2[n], name="adamw_" + n)
    return (loss, grad_x[None], *[gout[n] for n in WEIGHT_ORDER], *[delta[n] for n in WEIGHT_ORDER],
            *[new_m[n] for n in WEIGHT_ORDER], *[new_v[n] for n in WEIGHT_ORDER])
```

```python
import functools
import math

import numpy as np
import jax
import jax.numpy as jnp
from jax import lax
from jax.experimental import pallas as pl
from jax.experimental.pallas import tpu as pltpu

F32 = jnp.float32
BF16 = jnp.bfloat16
NT = (((1,), (1,)), ((), ()))
TN = (((0,), (0,)), ((), ()))
MESH = pl.DeviceIdType.MESH

RMS_EPS = 1e-6
FFN_RES_SCALE = 0.5
A_HEADS, A_KV_HEADS, A_HEAD_DIM, WINDOW = 8, 2, 64, 128
B_HEADS, B_Q_LORA, B_KV_LORA, B_NOPE, B_ROPE, B_V = 8, 256, 128, 64, 32, 64
ROPE_THETA = 10000.0
C_HEADS, C_HEAD_DIM = 16, 64
ADAM_LR, ADAM_B1, ADAM_B2, ADAM_EPS, ADAM_WD, ADAM_STEP = 0.001, 0.9, 0.999, 1e-08, 0.01, 10

N_CHIPS = 4
LANES = 128
FLAT_COLS = 1024
MASK_VALUE = -1e30
VMEM_LIMIT = 48 * 2**20

SHARDED = (
    ("ffa_w_gate_up", 2), ("ffa_w_down", 1), ("ffb_w_gate_up", 2), ("ffb_w_down", 1),
    ("ple_w_gate", 1), ("ple_w_proj", 2), ("ev_w_in", 2), ("ev_w_uq", 2), ("ev_w_ukv", 2),
    ("ev_w_out", 1), ("od_w_in", 2), ("od_w_out", 1))
REPLICATED = ("ffa_norm", "mix_norm", "ffb_norm", "ple_norm", "final_norm",
              "ev_sinks", "ev_cq_norm", "ev_ckv_norm", "od_b_f")
WEIGHT_ORDER = ("ffa_norm", "ffa_w_gate_up", "ffa_w_down", "mix_norm", "ffb_norm", "ffb_w_gate_up",
                "ffb_w_down", "ple_norm", "ple_w_gate", "ple_w_proj", "ev_w_in", "ev_sinks",
                "ev_cq_norm", "ev_w_uq", "ev_ckv_norm", "ev_w_ukv", "ev_w_out", "od_w_in", "od_b_f",
                "od_w_out", "final_norm")


def _cp(*sem):
    return pltpu.CompilerParams(dimension_semantics=sem, vmem_limit_bytes=VMEM_LIMIT)


def _sigmoid(z):
    return 1.0 / (1.0 + jnp.exp(-z))


def _rms_stats(xv):
    r = lax.rsqrt(jnp.mean(xv * xv, axis=-1, keepdims=True) + RMS_EPS)
    return r, xv * r


def _rms_bwd(dxn, xv, g):
    r, xhat = _rms_stats(xv)
    u = dxn * g
    dx = r * (u - xhat * jnp.mean(u * xhat, axis=-1, keepdims=True))
    return dx, dxn * xhat


def _col_tile(k_rows, n, budget_bytes=6 * 2**20):
    if k_rows * n * 4 <= budget_bytes or n % LANES:
        return n
    units = n // LANES
    best = LANES
    for d in range(1, units + 1):
        if units % d == 0 and k_rows * d * LANES * 4 <= budget_bytes:
            best = d * LANES
    return best


def _row_tile(rows, cols, target_elems=2**18):
    if rows * cols <= target_elems or rows % 8:
        return rows
    best = 8
    for d in range(8, rows + 1, 8):
        if rows % d == 0 and d * cols <= target_elems:
            best = d
    return best


def _rms_mm_fwd(x, g, w, *, name, tm=512):
    s, k = x.shape
    n = w.shape[1]

    def body(x_ref, g_ref, w_ref, y_ref, xn_ref):
        _, xhat = _rms_stats(x_ref[...])
        xn = (xhat * g_ref[...]).astype(BF16)
        xn_ref[...] = xn
        y_ref[...] = jnp.dot(xn, w_ref[...], preferred_element_type=F32)

    return pl.pallas_call(
        body, name=name, grid=(s // tm,),
        in_specs=[pl.BlockSpec((tm, k), lambda i: (i, 0)), pl.BlockSpec((1, k), lambda i: (0, 0)),
                  pl.BlockSpec((k, n), lambda i: (0, 0))],
        out_specs=[pl.BlockSpec((tm, n), lambda i: (i, 0)), pl.BlockSpec((tm, k), lambda i: (i, 0))],
        out_shape=[jax.ShapeDtypeStruct((s, n), F32), jax.ShapeDtypeStruct((s, k), BF16)],
        compiler_params=_cp("arbitrary"))(x, g, w)


def _ffn_up(x, g, wgu, *, name, tm=512):
    s, k = x.shape
    f = wgu.shape[1] // 2
    tn = _col_tile(k, f)
    nj = f // tn

    def body(x_ref, g_ref, wg_ref, wu_ref, gate_ref, up_ref, act_ref, xn_ref, xn_sc):
        @pl.when(pl.program_id(1) == 0)
        def _():
            _, xhat = _rms_stats(x_ref[...])
            xn = (xhat * g_ref[...]).astype(BF16)
            xn_sc[...] = xn
            xn_ref[...] = xn

        xn = xn_sc[...]
        gg = jnp.dot(xn, wg_ref[...], preferred_element_type=F32)
        uu = jnp.dot(xn, wu_ref[...], preferred_element_type=F32)
        gate_ref[...] = gg.astype(BF16)
        up_ref[...] = uu.astype(BF16)
        act_ref[...] = ((gg * _sigmoid(gg)) * uu).astype(BF16)

    tile = pl.BlockSpec((tm, tn), lambda i, j: (i, j))
    return pl.pallas_call(
        body, name=name, grid=(s // tm, nj),
        in_specs=[pl.BlockSpec((tm, k), lambda i, j: (i, 0)), pl.BlockSpec((1, k), lambda i, j: (0, 0)),
                  pl.BlockSpec((k, tn), lambda i, j: (0, j)), pl.BlockSpec((k, tn), lambda i, j: (0, j + nj))],
        out_specs=[tile, tile, tile, pl.BlockSpec((tm, k), lambda i, j: (i, 0))],
        out_shape=[jax.ShapeDtypeStruct((s, f), BF16)] * 3 + [jax.ShapeDtypeStruct((s, k), BF16)],
        scratch_shapes=[pltpu.VMEM((tm, k), BF16)],
        compiler_params=_cp("arbitrary", "arbitrary"))(x, g, wgu, wgu)


def _mm_res_fwd(a, w, res, *, scale, name, tm=512):
    s, k = a.shape
    n = w.shape[1]

    def body(a_ref, w_ref, r_ref, o_ref):
        o_ref[...] = r_ref[...] + scale * jnp.dot(a_ref[...], w_ref[...], preferred_element_type=F32)

    return pl.pallas_call(
        body, name=name, grid=(s // tm,),
        in_specs=[pl.BlockSpec((tm, k), lambda i: (i, 0)), pl.BlockSpec((k, n), lambda i: (0, 0)),
                  pl.BlockSpec((tm, n), lambda i: (i, 0))],
        out_specs=pl.BlockSpec((tm, n), lambda i: (i, 0)),
        out_shape=jax.ShapeDtypeStruct((s, n), F32),
        compiler_params=_cp("arbitrary"))(a, w, res)


def _ffn_down_bwd(dh, wd, gate, up, *, scale, name, tm=512):
    s, d = dh.shape
    f = wd.shape[0]
    tn = _col_tile(d, f)

    def body(dh_ref, wd_ref, gate_ref, up_ref, dg_ref, du_ref):
        dhb = (dh_ref[...] * scale).astype(BF16)
        da = lax.dot_general(dhb, wd_ref[...], NT, preferred_element_type=F32)
        gg = gate_ref[...].astype(F32)
        uu = up_ref[...].astype(F32)
        sg = _sigmoid(gg)
        dg_ref[...] = (da * uu * (sg * (1.0 + gg * (1.0 - sg)))).astype(BF16)
        du_ref[...] = (da * (gg * sg)).astype(BF16)

    tile = pl.BlockSpec((tm, tn), lambda i, j: (i, j))
    return pl.pallas_call(
        body, name=name, grid=(s // tm, f // tn),
        in_specs=[pl.BlockSpec((tm, d), lambda i, j: (i, 0)), pl.BlockSpec((tn, d), lambda i, j: (j, 0)), tile, tile],
        out_specs=[tile, tile],
        out_shape=[jax.ShapeDtypeStruct((s, f), BF16)] * 2,
        compiler_params=_cp("arbitrary", "arbitrary"))(dh, wd, gate, up)


def _mm_tn(a, bs, *, name, b_scale=1.0, ts=512):
    bs = list(bs) if isinstance(bs, (list, tuple)) else [bs]
    s, k = a.shape
    n = bs[0].shape[1]
    tn = _col_tile(k, n)
    per = n // tn

    def body(a_ref, *refs):
        b_refs, o_ref = refs[:-1], refs[-1]
        j = pl.program_id(0)

        @pl.when(pl.program_id(1) == 0)
        def _():
            o_ref[...] = jnp.zeros_like(o_ref)

        for m, b_ref in enumerate(b_refs):
            def acc(b_ref=b_ref):
                bv = b_ref[...]
                if b_scale != 1.0:
                    bv = bv * b_scale
                o_ref[...] += lax.dot_general(a_ref[...].astype(BF16), bv.astype(BF16), TN, preferred_element_type=F32)

            if len(b_refs) == 1:
                acc()
            else:
                pl.when(jnp.logical_and(j >= m * per, j < (m + 1) * per))(acc)

    def b_spec(m):
        def idx(j, t):
            mine = jnp.logical_and(j >= m * per, j < (m + 1) * per)
            return (jnp.where(mine, t, 0), jnp.clip(j - m * per, 0, per - 1))
        return pl.BlockSpec((ts, tn), idx)

    return pl.pallas_call(
        body, name=name, grid=(per * len(bs), s // ts),
        in_specs=[pl.BlockSpec((ts, k), lambda j, t: (t, 0))] + [b_spec(m) for m in range(len(bs))],
        out_specs=pl.BlockSpec((k, tn), lambda j, t: (0, j)),
        out_shape=jax.ShapeDtypeStruct((k, n * len(bs)), F32),
        compiler_params=_cp("arbitrary", "arbitrary"))(a, *bs)


def _mm_nt(dy, w, *, name, tm=512):
    s, n = dy.shape
    k = w.shape[0]

    def body(dy_ref, w_ref, o_ref):
        o_ref[...] = lax.dot_general(dy_ref[...].astype(BF16), w_ref[...], NT, preferred_element_type=F32)

    return pl.pallas_call(
        body, name=name, grid=(s // tm,),
        in_specs=[pl.BlockSpec((tm, n), lambda i: (i, 0)), pl.BlockSpec((k, n), lambda i: (0, 0))],
        out_specs=pl.BlockSpec((tm, k), lambda i: (i, 0)),
        out_shape=jax.ShapeDtypeStruct((s, k), F32),
        compiler_params=_cp("arbitrary"))(dy, w)


def _mm_nt_rmsbwd(pairs, x, g, dres, *, name, tm=256):
    s, k = x.shape
    npairs = len(pairs)
    pairs = [pr if len(pr) == 3 else (pr[0], pr[1], 0) for pr in pairs]

    def body(*refs):
        dy_refs = refs[0:2 * npairs:2]
        w_refs = refs[1:2 * npairs:2]
        rest = refs[2 * npairs:]
        x_ref, g_ref = rest[0], rest[1]
        if dres is None:
            dx_ref, dg_ref = rest[2], rest[3]
        else:
            dres_ref, dx_ref, dg_ref = rest[2], rest[3], rest[4]
        dxn = None
        for dy_ref, w_ref in zip(dy_refs, w_refs):
            t = lax.dot_general(dy_ref[...].astype(BF16), w_ref[...], NT, preferred_element_type=F32)
            dxn = t if dxn is None else dxn + t
        dx, dgrow = _rms_bwd(dxn, x_ref[...], g_ref[...])
        if dres is not None:
            dx = dx + dres_ref[...]
        dx_ref[...] = dx

        @pl.when(pl.program_id(0) == 0)
        def _():
            dg_ref[...] = jnp.zeros_like(dg_ref)

        dg_ref[...] += jnp.sum(dgrow, axis=0, keepdims=True)

    in_specs, args = [], []
    for dy, w, cb in pairs:
        n = dy.shape[1]
        in_specs += [pl.BlockSpec((tm, n), lambda i: (i, 0)), pl.BlockSpec((k, n), lambda i, cb=cb: (0, cb))]
        args += [dy, w]
    row = pl.BlockSpec((tm, k), lambda i: (i, 0))
    vec = pl.BlockSpec((1, k), lambda i: (0, 0))
    in_specs += [row, vec]
    args += [x, g]
    if dres is not None:
        in_specs.append(row)
        args.append(dres)
    return pl.pallas_call(
        body, name=name, grid=(s // tm,), in_specs=in_specs, out_specs=[row, vec],
        out_shape=[jax.ShapeDtypeStruct((s, k), F32), jax.ShapeDtypeStruct((1, k), F32)],
        compiler_params=_cp("arbitrary"))(*args)


def _ple_fwd(h, g, wg, p, wp, *, name, tm=512):
    s, d = h.shape
    pd = p.shape[1]

    def body(h_ref, g_ref, wg_ref, p_ref, wp_ref, o_ref, xn_ref, gate_ref, pp_ref):
        hv = h_ref[...]
        _, xhat = _rms_stats(hv)
        xn = (xhat * g_ref[...]).astype(BF16)
        xn_ref[...] = xn
        gate = _sigmoid(jnp.dot(xn, wg_ref[...], preferred_element_type=F32))
        pp = jnp.dot(p_ref[...].astype(BF16), wp_ref[...], preferred_element_type=F32)
        gate_ref[...] = gate.astype(BF16)
        pp_ref[...] = pp.astype(BF16)
        o_ref[...] = hv + gate * pp

    row = pl.BlockSpec((tm, d), lambda i: (i, 0))
    return pl.pallas_call(
        body, name=name, grid=(s // tm,),
        in_specs=[row, pl.BlockSpec((1, d), lambda i: (0, 0)), pl.BlockSpec((d, d), lambda i: (0, 0)),
                  pl.BlockSpec((tm, pd), lambda i: (i, 0)), pl.BlockSpec((pd, d), lambda i: (0, 0))],
        out_specs=[row, row, row, row],
        out_shape=[jax.ShapeDtypeStruct((s, d), F32)] + [jax.ShapeDtypeStruct((s, d), BF16)] * 3,
        compiler_params=_cp("arbitrary"))(h, g, wg, p, wp)


def _ple_bwd_elem(dh, gate, pp, *, name, tm=512):
    s, d = dh.shape

    def body(dh_ref, gate_ref, pp_ref, dz_ref, dpp_ref):
        dhv = dh_ref[...]
        gt = gate_ref[...].astype(F32)
        dz_ref[...] = (dhv * pp_ref[...].astype(F32) * (gt * (1.0 - gt))).astype(BF16)
        dpp_ref[...] = (dhv * gt).astype(BF16)

    row = pl.BlockSpec((tm, d), lambda i: (i, 0))
    return pl.pallas_call(
        body, name=name, grid=(s // tm,), in_specs=[row, row, row], out_specs=[row, row],
        out_shape=[jax.ShapeDtypeStruct((s, d), BF16)] * 2,
        compiler_params=_cp("arbitrary"))(dh, gate, pp)


def _final_loss(h, g, tgt, *, name, tm=512):
    s, d = h.shape

    def body(h_ref, g_ref, t_ref, loss_ref, dh_ref, dg_ref):
        @pl.when(pl.program_id(0) == 0)
        def _():
            loss_ref[...] = jnp.zeros_like(loss_ref)
            dg_ref[...] = jnp.zeros_like(dg_ref)

        hv = h_ref[...]
        gv = g_ref[...]
        _, xhat = _rms_stats(hv)
        err = xhat * gv - t_ref[...]
        per_row = jnp.mean(err * err, axis=-1, keepdims=True)
        loss_ref[...] += 0.5 * jnp.sum(per_row, axis=0, keepdims=True)
        dx, dgrow = _rms_bwd(err * (1.0 / d), hv, gv)
        dh_ref[...] = dx
        dg_ref[...] += jnp.sum(dgrow, axis=0, keepdims=True)

    row = pl.BlockSpec((tm, d), lambda i: (i, 0))
    vec = pl.BlockSpec((1, d), lambda i: (0, 0))
    return pl.pallas_call(
        body, name=name, grid=(s // tm,), in_specs=[row, vec, row],
        out_specs=[pl.BlockSpec((1, LANES), lambda i: (0, 0)), row, vec],
        out_shape=[jax.ShapeDtypeStruct((1, LANES), F32), jax.ShapeDtypeStruct((s, d), F32),
                   jax.ShapeDtypeStruct((1, d), F32)],
        compiler_params=_cp("arbitrary"))(h, g, tgt)


def _rope_fwd(y1, y2, cos, sin, *, name, tm=512):
    s, r = y1.shape

    def body(a_ref, b_ref, c_ref, s_ref, o_ref):
        o_ref[...] = a_ref[...] * c_ref[...] + b_ref[...] * s_ref[...]

    row = pl.BlockSpec((tm, r), lambda i: (i, 0))
    return pl.pallas_call(
        body, name=name, grid=(s // tm,), in_specs=[row] * 4, out_specs=row,
        out_shape=jax.ShapeDtypeStruct((s, r), F32), compiler_params=_cp("arbitrary"))(y1, y2, cos, sin)


def _rope_bwd(dout, cos, sin, *, name, tm=512):
    nh, s, r = dout.shape

    def body(d_ref, c_ref, s_ref, o1_ref, o2_ref):
        tot = d_ref[0]
        for hh in range(1, nh):
            tot = tot + d_ref[hh]
        o1_ref[...] = tot * c_ref[...]
        o2_ref[...] = tot * s_ref[...]

    row = pl.BlockSpec((tm, r), lambda i: (i, 0))
    return pl.pallas_call(
        body, name=name, grid=(s // tm,),
        in_specs=[pl.BlockSpec((nh, tm, r), lambda i: (0, i, 0)), row, row], out_specs=[row, row],
        out_shape=[jax.ShapeDtypeStruct((s, r), F32)] * 2, compiler_params=_cp("arbitrary"))(dout, cos, sin)


def _split3(v):
    h1 = v.astype(BF16)
    r1 = v - h1.astype(F32)
    h2 = r1.astype(BF16)
    h3 = (r1 - h2.astype(F32)).astype(BF16)
    return h1, h2, h3


def _tri(tb, upper):
    r = lax.broadcasted_iota(jnp.int32, (tb, tb), 0)
    c = lax.broadcasted_iota(jnp.int32, (tb, tb), 1)
    return jnp.where((r <= c) if upper else (r >= c), 1.0, 0.0).astype(BF16)


def _fox_gate_fwd(ft, bf, *, out_scale, name, tb=512):
    nh, s = ft.shape

    def body(f_ref, b_ref, o_ref, carry):
        @pl.when(pl.program_id(0) == 0)
        def _():
            carry[...] = jnp.zeros_like(carry)

        z = f_ref[...] + b_ref[...]
        lf = jnp.minimum(z, 0.0) - jnp.log(1.0 + jnp.exp(-jnp.abs(z)))
        tri = _tri(tb, True)
        cs = sum(jnp.dot(t, tri, preferred_element_type=F32) for t in _split3(lf)) + carry[...]
        for n, term in enumerate(_split3(cs * out_scale)):
            o_ref[n] = term
        carry[...] += jnp.sum(lf, axis=-1, keepdims=True)

    return pl.pallas_call(
        body, name=name, grid=(s // tb,),
        in_specs=[pl.BlockSpec((nh, tb), lambda t: (0, t)), pl.BlockSpec((nh, 1), lambda t: (0, 0))],
        out_specs=pl.BlockSpec((3, nh, tb), lambda t: (0, 0, t)),
        out_shape=jax.ShapeDtypeStruct((3, nh, s), BF16),
        scratch_shapes=[pltpu.VMEM((nh, 1), F32)], compiler_params=_cp("arbitrary"))(ft, bf)


def _fox_gate_bwd(drow, dcol, ft, bf, *, inv_scale, name, tb=512):
    nh, s = ft.shape
    nb = s // tb

    def body(dr_ref, dc_ref, f_ref, b_ref, df_ref, db_ref, carry):
        @pl.when(pl.program_id(0) == 0)
        def _():
            carry[...] = jnp.zeros_like(carry)
            db_ref[...] = jnp.zeros_like(db_ref)

        dc = (dr_ref[...] - dc_ref[...]) * inv_scale
        tri = _tri(tb, False)
        suf = sum(jnp.dot(t, tri, preferred_element_type=F32) for t in _split3(dc)) + carry[...]
        z = f_ref[...] + b_ref[...]
        dz = suf * (1.0 / (1.0 + jnp.exp(z)))
        df_ref[...] = dz
        db_ref[...] += jnp.sum(dz, axis=-1, keepdims=True)
        carry[...] += jnp.sum(dc, axis=-1, keepdims=True)

    rev = pl.BlockSpec((nh, tb), lambda t: (0, nb - 1 - t))
    one = pl.BlockSpec((nh, 1), lambda t: (0, 0))
    return pl.pallas_call(
        body, name=name, grid=(nb,), in_specs=[rev, rev, rev, one], out_specs=[rev, one],
        out_shape=[jax.ShapeDtypeStruct((nh, s), F32), jax.ShapeDtypeStruct((nh, 1), F32)],
        scratch_shapes=[pltpu.VMEM((nh, 1), F32)], compiler_params=_cp("arbitrary"))(drow, dcol, ft, bf)


def _tri_fwd(t, nq):
    i = sum((t >= (r * (r + 1)) // 2).astype(jnp.int32) for r in range(1, nq))
    return i, t - (i * (i + 1)) // 2


def _tri_bwd(t, nq):
    j = sum((t >= r * nq - (r * (r - 1)) // 2).astype(jnp.int32) for r in range(1, nq))
    return j, j + t - (j * nq - (j * (j - 1)) // 2)


def _scores_t(k, q, *, scale, diag):
    s = lax.dot_general(k, q, NT, preferred_element_type=F32) * scale
    if diag:
        r = lax.broadcasted_iota(jnp.int32, s.shape, 0)
        c = lax.broadcasted_iota(jnp.int32, s.shape, 1)
        s = jnp.where(r <= c, s, MASK_VALUE)
    return s


def _causal_fwd_t(q, k, vt, *, scale, name, tq, hb=2, rider=None):
    nh, s, dq = q.shape
    dv = vt.shape[1]
    nq = s // tq
    nsteps = (nq * (nq + 1)) // 2

    def body(q_ref, k_ref, vt_ref, o_ref, lse_ref, m_sc, l_sc, acc_sc):
        i, j = _tri_fwd(pl.program_id(1), nq)

        @pl.when(j == 0)
        def _():
            m_sc[...] = jnp.full_like(m_sc, MASK_VALUE)
            l_sc[...] = jnp.zeros_like(l_sc)
            acc_sc[...] = jnp.zeros_like(acc_sc)

        def step(diag):
            for u in range(hb):
                sc = _scores_t(k_ref[u], q_ref[u], scale=scale, diag=diag)
                m_prev = m_sc[u]
                m_new = jnp.maximum(m_prev, jnp.max(sc, axis=0, keepdims=True))
                alpha = jnp.exp(m_prev - m_new)
                pr = jnp.exp(sc - m_new)
                l_new = alpha * l_sc[u] + jnp.sum(pr, axis=0, keepdims=True)
                acc = alpha * acc_sc[u] + jnp.dot(vt_ref[u], pr.astype(BF16), preferred_element_type=F32)
                if diag:
                    o_ref[u] = (acc / l_new).astype(BF16)
                    lse_ref[u] = m_new + jnp.log(l_new)
                else:
                    m_sc[u], l_sc[u], acc_sc[u] = m_new, l_new, acc

        pl.when(j < i)(functools.partial(step, False))
        pl.when(j == i)(functools.partial(step, True))

    def qi(t):
        return _tri_fwd(t, nq)[0]

    def kj(t):
        return _tri_fwd(t, nq)[1]

    return _call_with_rider(
        body, rider, name=name, grid=(nh // hb, nsteps),
        in_specs=[pl.BlockSpec((hb, tq, dq), lambda hp, t: (hp, qi(t), 0)),
                  pl.BlockSpec((hb, tq, dq), lambda hp, t: (hp, kj(t), 0)),
                  pl.BlockSpec((hb, dv, tq), lambda hp, t: (hp, 0, kj(t)))],
        out_specs=[pl.BlockSpec((hb, dv, tq), lambda hp, t: (hp, 0, qi(t))),
                   pl.BlockSpec((hb, 1, tq), lambda hp, t: (hp, 0, qi(t)))],
        out_shape=[jax.ShapeDtypeStruct((nh, dv, s), BF16), jax.ShapeDtypeStruct((nh, 1, s), F32)],
        scratch_shapes=[pltpu.VMEM((hb, 1, tq), F32), pltpu.VMEM((hb, 1, tq), F32), pltpu.VMEM((hb, dv, tq), F32)],
        compiler_params=_cp("arbitrary", "arbitrary"), args=(q, k, vt))


def _causal_bwd_t(q, k, v, ot, dot_, lse, *, scale, name, tq, hb=2, rider=None):
    nh, s, dq = q.shape
    dv = v.shape[-1]
    nq = s // tq
    nsteps = (nq * (nq + 1)) // 2

    def body(q_ref, k_ref, v_ref, ot_ref, dot_ref, lse_ref, dq_ref, dk_ref, dvt_ref):
        t = pl.program_id(1)
        j, i = _tri_bwd(t, nq)

        @pl.when(t == 0)
        def _():
            dq_ref[...] = jnp.zeros_like(dq_ref)

        def step(diag):
            rows = pl.ds(pl.multiple_of(i * tq, tq), tq)
            for u in range(hb):
                qv, kv, dov = q_ref[u], k_ref[u], dot_ref[u]
                pr = jnp.exp(_scores_t(kv, qv, scale=scale, diag=diag) - lse_ref[u])
                dp = jnp.dot(v_ref[u], dov, preferred_element_type=F32)
                delta = jnp.sum(dov.astype(F32) * ot_ref[u].astype(F32), axis=0, keepdims=True)
                dsb = ((pr * (dp - delta)) * scale).astype(BF16)
                d_v = lax.dot_general(dov, pr.astype(BF16), NT, preferred_element_type=F32)
                d_k = jnp.dot(dsb, qv, preferred_element_type=F32)
                if diag:
                    dvt_ref[u], dk_ref[u] = d_v, d_k
                else:
                    dvt_ref[u] += d_v
                    dk_ref[u] += d_k
                dq_ref[u, rows, :] += lax.dot_general(dsb, kv, TN, preferred_element_type=F32)

        pl.when(i > j)(functools.partial(step, False))
        pl.when(i == j)(functools.partial(step, True))

    def qi(t):
        return _tri_bwd(t, nq)[1]

    def kj(t):
        return _tri_bwd(t, nq)[0]

    rows_q = pl.BlockSpec((hb, tq, dq), lambda hp, t: (hp, qi(t), 0))
    rows_k = pl.BlockSpec((hb, tq, dq), lambda hp, t: (hp, kj(t), 0))
    lanes_q = pl.BlockSpec((hb, dv, tq), lambda hp, t: (hp, 0, qi(t)))
    return _call_with_rider(
        body, rider, name=name, grid=(nh // hb, nsteps),
        in_specs=[rows_q, rows_k, pl.BlockSpec((hb, tq, dv), lambda hp, t: (hp, kj(t), 0)), lanes_q, lanes_q,
                  pl.BlockSpec((hb, 1, tq), lambda hp, t: (hp, 0, qi(t)))],
        out_specs=[pl.BlockSpec((hb, s, dq), lambda hp, t: (hp, 0, 0)), rows_k,
                   pl.BlockSpec((hb, dv, tq), lambda hp, t: (hp, 0, kj(t)))],
        out_shape=[jax.ShapeDtypeStruct((nh, s, dq), F32), jax.ShapeDtypeStruct((nh, s, dq), F32),
                   jax.ShapeDtypeStruct((nh, dv, s), F32)],
        scratch_shapes=[], compiler_params=_cp("arbitrary", "arbitrary"), args=(q, k, v, ot, dot_, lse))


def _swa_scores_t(k, q, dist, ok, *, scale, slope):
    s = lax.dot_general(k, q, NT, preferred_element_type=F32) * scale - slope * dist.astype(F32)
    return jnp.where(ok, s, MASK_VALUE)


def _swa_geometry(tb, w, has_other):
    r = lax.broadcasted_iota(jnp.int32, (tb, tb), 0)
    c = lax.broadcasted_iota(jnp.int32, (tb, tb), 1)
    d_same = c - r
    ok_same = jnp.logical_and(d_same >= 0, d_same < w)

    def other(ncols):
        rr = lax.broadcasted_iota(jnp.int32, (w, ncols), 0)
        cc = lax.broadcasted_iota(jnp.int32, (w, ncols), 1)
        dd = cc + w - rr
        return dd, jnp.logical_and(dd < w, has_other)

    return (d_same, ok_same), other


def _swa_fwd_t(q, k, vt, slopes_sinks, *, scale, window, name, tb=256):
    nh, s, d = q.shape
    nkv = k.shape[0]
    grp = nh // nkv
    w = window
    per = tb // w
    assert tb % w == 0

    def body(q_ref, kc_ref, kp_ref, vc_ref, vp_ref, ss_ref, o_ref, lse_ref):
        kvh, i = pl.program_id(0), pl.program_id(1)
        (d_c, ok_c), other = _swa_geometry(tb, w, i > 0)
        d_p, ok_p = other(tb)
        for g in range(grp):
            h = kvh * grp + g
            slope, sink = ss_ref[0, h], ss_ref[1, h]
            qg = q_ref[g]
            s_c = _swa_scores_t(kc_ref[...], qg, d_c, ok_c, scale=scale, slope=slope)
            s_p = _swa_scores_t(kp_ref[...], qg, d_p, ok_p, scale=scale, slope=slope)
            m = jnp.maximum(jnp.maximum(jnp.max(s_c, axis=0, keepdims=True), jnp.max(s_p, axis=0, keepdims=True)), sink)
            p_c, p_p = jnp.exp(s_c - m), jnp.exp(s_p - m)
            l = jnp.sum(p_c, axis=0, keepdims=True) + jnp.sum(p_p, axis=0, keepdims=True) + jnp.exp(sink - m)
            acc = (jnp.dot(vc_ref[...], p_c.astype(BF16), preferred_element_type=F32)
                   + jnp.dot(vp_ref[...], p_p.astype(BF16), preferred_element_type=F32))
            o_ref[g] = (acc / l).astype(BF16)
            lse_ref[g] = m + jnp.log(l)

    def prev(i):
        return jnp.maximum(i * per - 1, 0)

    return pl.pallas_call(
        body, name=name, grid=(nkv, s // tb),
        in_specs=[pl.BlockSpec((grp, tb, d), lambda kh, i: (kh, i, 0)),
                  pl.BlockSpec((None, tb, d), lambda kh, i: (kh, i, 0)),
                  pl.BlockSpec((None, w, d), lambda kh, i: (kh, prev(i), 0)),
                  pl.BlockSpec((None, d, tb), lambda kh, i: (kh, 0, i)),
                  pl.BlockSpec((None, d, w), lambda kh, i: (kh, 0, prev(i))),
                  pl.BlockSpec(memory_space=pltpu.SMEM)],
        out_specs=[pl.BlockSpec((grp, d, tb), lambda kh, i: (kh, 0, i)), pl.BlockSpec((grp, 1, tb), lambda kh, i: (kh, 0, i))],
        out_shape=[jax.ShapeDtypeStruct((nh, d, s), BF16), jax.ShapeDtypeStruct((nh, 1, s), F32)],
        compiler_params=_cp("arbitrary", "arbitrary"))(q, k, k, vt, vt, slopes_sinks)


def _swa_bwd_t(q, k, v, ot, dot_, lse, slopes_sinks, *, scale, window, name, tb=256):
    nh, s, d = q.shape
    nkv = k.shape[0]
    grp = nh // nkv
    w = window
    per = tb // w
    nb = s // tb

    def body(qc_ref, qn_ref, kc_ref, kp_ref, vc_ref, vp_ref, oc_ref, on_ref, doc_ref, don_ref, lc_ref, ln_ref, ss_ref,
             dq_ref, dk_ref, dvt_ref, dsink_ref):
        kvh, i = pl.program_id(0), pl.program_id(1)

        @pl.when(i == 0)
        def _():
            dsink_ref[...] = jnp.zeros_like(dsink_ref)

        (d_c, ok_c), other = _swa_geometry(tb, w, i > 0)
        d_p, ok_p = other(tb)
        d_n, ok_n = _swa_geometry(tb, w, i < nb - 1)[1](w)
        kc, kp, vc, vp = kc_ref[...], kp_ref[...], vc_ref[...], vp_ref[...]
        k_last, v_last = kc[tb - w:, :], vc[tb - w:, :]
        dk_acc = jnp.zeros((tb, d), F32)
        dv_acc = jnp.zeros((d, tb), F32)
        dk_tail = jnp.zeros((w, d), F32)
        dv_tail = jnp.zeros((d, w), F32)
        for g in range(grp):
            h = kvh * grp + g
            slope, sink = ss_ref[0, h], ss_ref[1, h]
            qg, dog, lse_c = qc_ref[g], doc_ref[g], lc_ref[g]
            delta = jnp.sum(dog.astype(F32) * oc_ref[g].astype(F32), axis=0, keepdims=True)
            p_c = jnp.exp(_swa_scores_t(kc, qg, d_c, ok_c, scale=scale, slope=slope) - lse_c)
            p_p = jnp.exp(_swa_scores_t(kp, qg, d_p, ok_p, scale=scale, slope=slope) - lse_c)
            ds_c = ((p_c * (jnp.dot(vc, dog, preferred_element_type=F32) - delta)) * scale).astype(BF16)
            ds_p = ((p_p * (jnp.dot(vp, dog, preferred_element_type=F32) - delta)) * scale).astype(BF16)
            dq_ref[g] = (lax.dot_general(ds_c, kc, TN, preferred_element_type=F32)
                         + lax.dot_general(ds_p, kp, TN, preferred_element_type=F32))
            dk_acc += jnp.dot(ds_c, qg, preferred_element_type=F32)
            dv_acc += lax.dot_general(dog, p_c.astype(BF16), NT, preferred_element_type=F32)
            dsink_ref[g] -= jnp.broadcast_to(jnp.sum(jnp.exp(sink - lse_c) * delta, axis=1, keepdims=True), (1, LANES))
            qn, don = qn_ref[g], don_ref[g]
            delta_n = jnp.sum(don.astype(F32) * on_ref[g].astype(F32), axis=0, keepdims=True)
            p_n = jnp.exp(_swa_scores_t(k_last, qn, d_n, ok_n, scale=scale, slope=slope) - ln_ref[g])
            ds_n = ((p_n * (jnp.dot(v_last, don, preferred_element_type=F32) - delta_n)) * scale).astype(BF16)
            dk_tail += jnp.dot(ds_n, qn, preferred_element_type=F32)
            dv_tail += lax.dot_general(don, p_n.astype(BF16), NT, preferred_element_type=F32)
        dk_ref[...] = dk_acc
        dvt_ref[...] = dv_acc
        dk_ref[tb - w:, :] += dk_tail
        dvt_ref[:, tb - w:] += dv_tail

    def prev(i):
        return jnp.maximum(i * per - 1, 0)

    def nxt(i):
        return jnp.minimum((i + 1) * per, s // w - 1)

    return pl.pallas_call(
        body, name=name, grid=(nkv, nb),
        in_specs=[pl.BlockSpec((grp, tb, d), lambda kh, i: (kh, i, 0)),
                  pl.BlockSpec((grp, w, d), lambda kh, i: (kh, nxt(i), 0)),
                  pl.BlockSpec((None, tb, d), lambda kh, i: (kh, i, 0)),
                  pl.BlockSpec((None, w, d), lambda kh, i: (kh, prev(i), 0)),
                  pl.BlockSpec((None, tb, d), lambda kh, i: (kh, i, 0)),
                  pl.BlockSpec((None, w, d), lambda kh, i: (kh, prev(i), 0)),
                  pl.BlockSpec((grp, d, tb), lambda kh, i: (kh, 0, i)),
                  pl.BlockSpec((grp, d, w), lambda kh, i: (kh, 0, nxt(i))),
                  pl.BlockSpec((grp, d, tb), lambda kh, i: (kh, 0, i)),
                  pl.BlockSpec((grp, d, w), lambda kh, i: (kh, 0, nxt(i))),
                  pl.BlockSpec((grp, 1, tb), lambda kh, i: (kh, 0, i)),
                  pl.BlockSpec((grp, 1, w), lambda kh, i: (kh, 0, nxt(i))),
                  pl.BlockSpec(memory_space=pltpu.SMEM)],
        out_specs=[pl.BlockSpec((grp, tb, d), lambda kh, i: (kh, i, 0)),
                   pl.BlockSpec((None, tb, d), lambda kh, i: (kh, i, 0)),
                   pl.BlockSpec((None, d, tb), lambda kh, i: (kh, 0, i)),
                   pl.BlockSpec((None, grp, 1, LANES), lambda kh, i: (kh, 0, 0, 0))],
        out_shape=[jax.ShapeDtypeStruct((nh, s, d), F32), jax.ShapeDtypeStruct((nkv, s, d), F32),
                   jax.ShapeDtypeStruct((nkv, d, s), F32), jax.ShapeDtypeStruct((nkv, grp, 1, LANES), F32)],
        compiler_params=_cp("arbitrary", "arbitrary"))(q, q, k, k, v, v, ot, ot, dot_, dot_, lse, lse, slopes_sinks)


def _adamw(w, g, m, v, *, name):
    shape = w.shape
    cols = shape[-1]
    rows = int(np.prod(shape[:-1])) if len(shape) > 1 else 1
    tr = _row_tile(rows, cols)
    c1 = 1.0 - ADAM_B1 ** ADAM_STEP
    c2 = 1.0 - ADAM_B2 ** ADAM_STEP

    def body(w_ref, g_ref, m_ref, v_ref, d_ref, mo_ref, vo_ref):
        gv = g_ref[...]
        mn = ADAM_B1 * m_ref[...] + (1.0 - ADAM_B1) * gv
        vn = ADAM_B2 * v_ref[...] + (1.0 - ADAM_B2) * (gv * gv)
        mo_ref[...] = mn
        vo_ref[...] = vn
        d_ref[...] = -ADAM_LR * ((mn / c1) / (jnp.sqrt(vn / c2) + ADAM_EPS) + ADAM_WD * w_ref[...])

    blk = pl.BlockSpec((tr, cols), lambda i: (i, 0))
    outs = pl.pallas_call(
        body, name=name, grid=(rows // tr,), in_specs=[blk] * 4, out_specs=[blk] * 3,
        out_shape=[jax.ShapeDtypeStruct((rows, cols), F32)] * 3,
        compiler_params=_cp("arbitrary"))(*[a.reshape(rows, cols) for a in (w, g, m, v)])
    return tuple(a.reshape(shape) for a in outs)


def _hbm_spec():
    return pl.BlockSpec(memory_space=pl.ANY)


def _mesh_place():
    x, y, c = lax.axis_index("x"), lax.axis_index("y"), lax.axis_index("c")
    return x, y, c, [(1 - x, y), (x, 1 - y), (1 - x, 1 - y)]


def _half_rows(c, rows, align):
    return pl.ds(pl.multiple_of(c * (rows // 2), align), rows // 2)


def _part(ref, mode, k, n, rows=None):
    if mode == "cols":
        cols = pl.ds(pl.multiple_of(k * n, LANES), n)
        return ref.at[:, cols] if rows is None else ref.at[rows, cols]
    return ref.at[k] if rows is None else ref.at[k, rows, :]


class _Rider:
    def __init__(self, inputs, out_shape, n_sems, start, finish):
        self.inputs, self.out_shape, self.n_sems, self.start, self.finish = inputs, out_shape, n_sems, start, finish


def _call_with_rider(body, rider, *, name, grid, in_specs, out_specs, out_shape, scratch_shapes, compiler_params, args):
    if rider is None:
        outs = pl.pallas_call(body, name=name, grid=grid, in_specs=in_specs, out_specs=out_specs, out_shape=out_shape,
                              scratch_shapes=scratch_shapes, compiler_params=compiler_params)(*args)
        return outs, []
    n_in, n_out, n_sc = len(in_specs), len(out_specs), len(scratch_shapes)
    n_rin, n_rout = len(rider.inputs), len(rider.out_shape)

    def wrapped(*refs):
        pos = 0
        groups = []
        for n in (n_in, n_rin, n_out, n_rout, n_sc, 2):
            groups.append(refs[pos:pos + n])
            pos += n
        ins, rins, outs, routs, scratch, sems = groups
        ids = [pl.program_id(a) for a in range(len(grid))]
        first = functools.reduce(jnp.logical_and, [i == 0 for i in ids])
        last = functools.reduce(jnp.logical_and, [i == g - 1 for i, g in zip(ids, grid)])
        pl.when(first)(lambda: rider.start(rins, routs, *sems))
        body(*ins, *outs, *scratch)
        pl.when(last)(lambda: rider.finish(rins, routs, *sems))

    outs = pl.pallas_call(
        wrapped, name=name, grid=grid, in_specs=list(in_specs) + [_hbm_spec()] * n_rin,
        out_specs=list(out_specs) + [_hbm_spec()] * n_rout, out_shape=list(out_shape) + list(rider.out_shape),
        scratch_shapes=list(scratch_shapes) + [pltpu.SemaphoreType.DMA((rider.n_sems,))] * 2,
        compiler_params=compiler_params)(*args, *rider.inputs)
    return outs[:n_out], outs[n_out:]


def _run_rider(rider, *, name):
    n_rin = len(rider.inputs)

    def body(*refs):
        rins, routs, sems = refs[:n_rin], refs[n_rin:-2], refs[-2:]
        rider.start(rins, routs, *sems)
        rider.finish(rins, routs, *sems)

    return pl.pallas_call(
        body, name=name, in_specs=[_hbm_spec()] * n_rin, out_specs=[_hbm_spec()] * len(rider.out_shape),
        out_shape=rider.out_shape, scratch_shapes=[pltpu.SemaphoreType.DMA((rider.n_sems,))] * 2)(*rider.inputs)


def _gather_rider(shards, modes):
    n_arr = len(shards)
    out_shape = [jax.ShapeDtypeStruct((s.shape[0], N_CHIPS * s.shape[1]) if m == "cols" else (N_CHIPS,) + s.shape, s.dtype)
                 for s, m in zip(shards, modes)]
    per = 4

    def copies(srcs, dsts, send_sems, recv_sems):
        x, y, c, chips = _mesh_place()
        me = 2 * x + y
        sends, waits = [], []
        for i in range(n_arr):
            r, n = shards[i].shape
            rows = _half_rows(c, r, 16)

            def copy(slot, src, dst, to, i=i):
                return pltpu.make_async_remote_copy(src_ref=src, dst_ref=dst, send_sem=send_sems.at[i * per + slot],
                                                    recv_sem=recv_sems.at[i * per + slot], device_id=to, device_id_type=MESH)

            own = _part(dsts[i], modes[i], me, n)
            sends.append(copy(0, srcs[i], own, (x, y, 1 - c)))
            waits.append(copy(0, own, own, (x, y, 1 - c)))
            for j, (px, py) in enumerate(chips):
                sends.append(copy(1 + j, srcs[i].at[rows], _part(dsts[i], modes[i], me, n, rows), (px, py, c)))
                theirs = _part(dsts[i], modes[i], 2 * px + py, n, rows)
                waits.append(copy(1 + j, theirs, theirs, (px, py, c)))
        return sends, waits

    def start(*refs):
        for cp in copies(*refs)[0]:
            cp.start()

    def finish(*refs):
        sends, waits = copies(*refs)
        for cp in waits:
            cp.wait_recv()
        for cp in sends:
            cp.wait_send()

    return _Rider(list(shards), out_shape, per * n_arr, start, finish)


def _gather_forward(dsts, shard_shapes, modes, *, name):
    n_arr = len(dsts)

    def body(*refs):
        outs = refs[n_arr:2 * n_arr]
        send_sems, recv_sems = refs[2 * n_arr:]
        x, y, c, chips = _mesh_place()
        cps = []
        for i in range(n_arr):
            r, n = shard_shapes[i]
            for j, (px, py) in enumerate(chips):
                def view(hc, i=i, px=px, py=py, r=r, n=n):
                    return _part(outs[i], modes[i], 2 * px + py, n, _half_rows(hc, r, 16))

                def copy(ref, i=i, j=j):
                    return pltpu.make_async_remote_copy(src_ref=ref, dst_ref=ref, send_sem=send_sems.at[3 * i + j],
                                                        recv_sem=recv_sems.at[3 * i + j], device_id=(x, y, 1 - c), device_id_type=MESH)

                cps.append((copy(view(c)), copy(view(1 - c))))
        for send, _ in cps:
            send.start()
        for send, theirs in cps:
            theirs.wait_recv()
            send.wait_send()

    return pl.pallas_call(
        body, name=name, in_specs=[_hbm_spec()] * n_arr, out_specs=[_hbm_spec()] * n_arr,
        out_shape=[jax.ShapeDtypeStruct(d.shape, d.dtype) for d in dsts],
        input_output_aliases={i: i for i in range(n_arr)},
        scratch_shapes=[pltpu.SemaphoreType.DMA((3 * n_arr,)), pltpu.SemaphoreType.DMA((3 * n_arr,))])(*dsts)


def _blk_view(a, mode):
    return a[None] if mode == "cols" else a


def _rs_pair_swap(arrs, modes, *, name):
    n_arr = len(arrs)
    out_shape = [jax.ShapeDtypeStruct((a.shape[0] // 2, a.shape[1]) if m == "cols" else (a.shape[0], a.shape[1] // 2, a.shape[2]), a.dtype)
                 for a, m in zip(arrs, modes)]

    def body(*refs):
        srcs, dsts = refs[:n_arr], refs[n_arr:2 * n_arr]
        send_sems, recv_sems = refs[2 * n_arr:]
        x, y, c, _ = _mesh_place()
        cps = []
        for i in range(n_arr):
            if modes[i] == "cols":
                src = srcs[i].at[_half_rows(1 - c, arrs[i].shape[0], 8)]
            else:
                src = srcs[i].at[:, _half_rows(1 - c, arrs[i].shape[1], 8), :]
            cps.append(pltpu.make_async_remote_copy(src_ref=src, dst_ref=dsts[i], send_sem=send_sems.at[i],
                                                    recv_sem=recv_sems.at[i], device_id=(x, y, 1 - c), device_id_type=MESH))
        for cp in cps:
            cp.start()
        for cp in cps:
            cp.wait()

    return pl.pallas_call(
        body, name=name, in_specs=[_hbm_spec()] * n_arr, out_specs=[_hbm_spec()] * n_arr, out_shape=out_shape,
        scratch_shapes=[pltpu.SemaphoreType.DMA((n_arr,)), pltpu.SemaphoreType.DMA((n_arr,))])(*arrs)


def _rs_pair_add(arr, landed, place, *, name):
    nb, r, c = arr.shape
    rh = r // 2
    tr = _row_tile(rh, c)
    nt = rh // tr

    def body(p_ref, a_ref, l_ref, o_ref):
        o_ref[...] = (a_ref[...] + l_ref[...]).astype(BF16)

    grid_spec = pltpu.PrefetchScalarGridSpec(
        num_scalar_prefetch=1, grid=(nb, nt),
        in_specs=[pl.BlockSpec((None, tr, c), lambda b, t, p_ref: (b, p_ref[1] * nt + t, 0)),
                  pl.BlockSpec((None, tr, c), lambda b, t, p_ref: (b, t, 0))],
        out_specs=pl.BlockSpec((None, tr, c), lambda b, t, p_ref: (b, t, 0)))
    return pl.pallas_call(
        body, name=name, grid_spec=grid_spec, out_shape=jax.ShapeDtypeStruct((nb, rh, c), BF16),
        compiler_params=_cp("arbitrary", "arbitrary"))(place, arr, landed)


def _exchange_rider(parts, modes):
    n_arr = len(parts)
    out_shape = []
    for a, m in zip(parts, modes):
        shp = (a.shape[0], a.shape[1] // N_CHIPS) if m == "cols" else a.shape[1:]
        out_shape.append(jax.ShapeDtypeStruct((3,) + shp, a.dtype))

    def copies(srcs, dsts, send_sems, recv_sems):
        x, y, c, chips = _mesh_place()
        cps = []
        for i in range(n_arr):
            n = out_shape[i].shape[-1]
            for j, (px, py) in enumerate(chips):
                cps.append(pltpu.make_async_remote_copy(
                    src_ref=_part(srcs[i], modes[i], 2 * px + py, n), dst_ref=dsts[i].at[j],
                    send_sem=send_sems.at[3 * i + j], recv_sem=recv_sems.at[3 * i + j],
                    device_id=(px, py, c), device_id_type=MESH))
        return cps

    def start(*refs):
        for cp in copies(*refs):
            cp.start()

    def finish(*refs):
        for cp in copies(*refs):
            cp.wait()

    return _Rider(list(parts), out_shape, 3 * n_arr, start, finish)


def _rs_chip_sum(part, landed, mode, place, *, name):
    _, rh, n = landed.shape
    tr = _row_tile(rh, n)
    nt = rh // tr

    def body(p_ref, a_ref, l_ref, o_ref):
        o_ref[...] = ((a_ref[...].astype(F32) + l_ref[0].astype(F32)) + l_ref[1].astype(F32)) + l_ref[2].astype(F32)

    if mode == "cols":
        own = pl.BlockSpec((tr, n), lambda t, p_ref: (t, p_ref[0]))
    else:
        own = pl.BlockSpec((None, tr, n), lambda t, p_ref: (p_ref[0], t, 0))
    grid_spec = pltpu.PrefetchScalarGridSpec(
        num_scalar_prefetch=1, grid=(nt,),
        in_specs=[own, pl.BlockSpec((3, tr, n), lambda t, p_ref: (0, t, 0))],
        out_specs=pl.BlockSpec((tr, n), lambda t, p_ref: (p_ref[1] * nt + t, 0)))
    return pl.pallas_call(
        body, name=name, grid_spec=grid_spec, out_shape=jax.ShapeDtypeStruct((2 * rh, n), F32),
        compiler_params=_cp("arbitrary"))(place, part, landed)


def _rs_pair_join(halves, *, name):
    n_arr = len(halves)

    def body(*refs):
        outs = refs[n_arr:2 * n_arr]
        send_sems, recv_sems = refs[2 * n_arr:]
        x, y, c, _ = _mesh_place()
        cps = []
        for i in range(n_arr):
            rows = _half_rows(c, halves[i].shape[0], 8)
            cps.append(pltpu.make_async_remote_copy(src_ref=outs[i].at[rows], dst_ref=outs[i].at[rows], send_sem=send_sems.at[i],
                                                    recv_sem=recv_sems.at[i], device_id=(x, y, 1 - c), device_id_type=MESH))
        for cp in cps:
            cp.start()
        for i, cp in enumerate(cps):
            cp.wait_send()
            theirs = outs[i].at[_half_rows(1 - c, halves[i].shape[0], 8)]
            pltpu.make_async_remote_copy(src_ref=theirs, dst_ref=theirs, send_sem=send_sems.at[i], recv_sem=recv_sems.at[i],
                                         device_id=(x, y, 1 - c), device_id_type=MESH).wait_recv()

    return pl.pallas_call(
        body, name=name, in_specs=[_hbm_spec()] * n_arr, out_specs=[_hbm_spec()] * n_arr,
        out_shape=[jax.ShapeDtypeStruct(h.shape, h.dtype) for h in halves],
        input_output_aliases={i: i for i in range(n_arr)},
        scratch_shapes=[pltpu.SemaphoreType.DMA((n_arr,)), pltpu.SemaphoreType.DMA((n_arr,))])(*halves)


def _allreduce_small(v, *, name):
    r, c = v.shape

    def body(v_ref, o_ref, gath, send_sems, recv_sems):
        x, y, cc, _ = _mesh_place()
        me = 4 * x + 2 * y + cc
        gath[me] = v_ref[...]
        cps = []
        for rel in range(1, 8):
            px = 1 - x if rel & 4 else x
            py = 1 - y if rel & 2 else y
            pc = 1 - cc if rel & 1 else cc

            def copy(slot, px=px, py=py, pc=pc, rel=rel):
                return pltpu.make_async_remote_copy(
                    src_ref=v_ref, dst_ref=gath.at[slot], send_sem=send_sems.at[rel - 1],
                    recv_sem=recv_sems.at[rel - 1], device_id=(px, py, pc), device_id_type=MESH)

            cps.append((copy(me), copy(4 * px + 2 * py + pc)))
        for send, _ in cps:
            send.start()
        for send, theirs in cps:
            theirs.wait_recv()
            send.wait_send()
        tot = gath[0]
        for d in range(1, 8):
            tot = tot + gath[d]
        o_ref[...] = tot

    vm = pl.BlockSpec(memory_space=pltpu.VMEM)
    return pl.pallas_call(
        body, name=name, in_specs=[vm], out_specs=vm, out_shape=jax.ShapeDtypeStruct((r, c), F32),
        scratch_shapes=[pltpu.VMEM((8, r, c), F32), pltpu.SemaphoreType.DMA((7,)), pltpu.SemaphoreType.DMA((7,))])(v)


def _to_heads(a, nh, dh, dtype=BF16):
    return a.reshape(a.shape[0], nh, dh).transpose(1, 0, 2).astype(dtype)


def _from_heads(a):
    return a.transpose(1, 0, 2).reshape(a.shape[1], -1)


def _rope_tables(s, reps):
    half = B_ROPE // 2
    inv = ROPE_THETA ** (-jnp.arange(0, B_ROPE, 2, dtype=F32) / B_ROPE)
    ang = jnp.arange(s, dtype=F32)[:, None] * inv[None, :]
    return jnp.tile(jnp.cos(ang), (1, reps)), jnp.tile(jnp.sin(ang), (1, reps))


def _alibi_slopes():
    return 2.0 ** (-8.0 * jnp.arange(1, A_HEADS + 1, dtype=F32) / A_HEADS)


def _ffn_fwd(h, norm, wts, tag):
    gate, up, act, xn = _ffn_up(h, norm, wts["wgu"], name=f"{tag}_up")
    out = _mm_res_fwd(act, wts["wd"], h, scale=FFN_RES_SCALE, name=f"{tag}_down")
    return out, dict(h_in=h, gate=gate, up=up, act=act, xn=xn)


def _ffn_bwd(dh, norm, wts, sv, tag):
    dgate, dup = _ffn_down_bwd(dh, wts["wd"], sv["gate"], sv["up"], scale=FFN_RES_SCALE, name=f"{tag}_down_bwd")
    d_wd = _mm_tn(sv["act"], dh, b_scale=FFN_RES_SCALE, name=f"{tag}_dwd")
    d_wgu = _mm_tn(sv["xn"], [dgate, dup], name=f"{tag}_dwgu")
    dh_in, dnorm = _mm_nt_rmsbwd([(dgate, wts["wgu"], 0), (dup, wts["wgu"], 1)], sv["h_in"], norm, dh,
                                 name=f"{tag}_dx")
    return dh_in, dnorm, d_wgu, d_wd


def _even_weights(w_in, w_uq, w_ukv):
    half = B_ROPE // 2
    base = w_in.shape[1]
    kr1, kr2 = w_in[:, base - B_ROPE:base - half], w_in[:, base - half:]
    w_in_cat = jnp.concatenate([w_in, -kr2, kr1, jnp.zeros((w_in.shape[0], 64), w_in.dtype)], axis=1)
    u3 = w_uq.reshape(w_uq.shape[0], B_HEADS, B_NOPE + B_ROPE)
    nope = u3[:, :, :B_NOPE].reshape(w_uq.shape[0], -1)
    r1 = u3[:, :, B_NOPE:B_NOPE + half].reshape(w_uq.shape[0], -1)
    r2 = u3[:, :, B_NOPE + half:].reshape(w_uq.shape[0], -1)
    w_q_cat = jnp.concatenate([nope, r1, r2, -r2, r1], axis=1)
    return w_in_cat, w_q_cat, w_ukv


def _even_fwd(h, w, i, rider=None):
    s = h.shape[0]
    half = B_ROPE // 2
    ycat, xn = _rms_mm_fwd(h, w["mix_norm"][i:i + 1], w["ev_in_cat"], name="ev_in")
    a_q, a_k, a_v = ycat[:, :512], ycat[:, 512:640], ycat[:, 640:768]
    c_q, c_kv = ycat[:, 768:1024], ycat[:, 1024:1152]
    cos32, sin32 = _rope_tables(s, 2)
    kro = _rope_fwd(ycat[:, 1152:1184], ycat[:, 1184:1216], cos32, sin32, name="ev_k_rope")
    qa, ka, va = _to_heads(a_q, A_HEADS, A_HEAD_DIM), _to_heads(a_k, A_KV_HEADS, A_HEAD_DIM), _to_heads(a_v, A_KV_HEADS, A_HEAD_DIM)
    ss = jnp.stack([_alibi_slopes(), w["ev_sinks"].reshape(-1)])
    oa, lse_a = _swa_fwd_t(qa, ka, va.transpose(0, 2, 1), ss, scale=A_HEAD_DIM ** -0.5, window=WINDOW, name="swa_fwd")
    yq, xn_q = _rms_mm_fwd(c_q, w["ev_cq_norm"], w["ev_q_cat"], name="ev_q_up")
    cos256, sin256 = _rope_tables(s, 2 * B_HEADS)
    qro = _rope_fwd(yq[:, 512:768], yq[:, 768:1024], cos256, sin256, name="ev_q_rope")
    ykv, xn_kv = _rms_mm_fwd(c_kv, w["ev_ckv_norm"], w["ev_ukv"], name="ev_kv_up")
    zq = jnp.zeros((s, B_HEADS, LANES - B_NOPE - B_ROPE), F32)
    qb = jnp.concatenate([yq[:, :512].reshape(s, B_HEADS, B_NOPE), qro[:, :128].reshape(s, B_HEADS, half),
                          qro[:, 128:].reshape(s, B_HEADS, half), zq], axis=-1).transpose(1, 0, 2).astype(BF16)
    kv3 = ykv.reshape(s, B_HEADS, B_NOPE + B_V)
    kb = jnp.concatenate([kv3[:, :, :B_NOPE], jnp.broadcast_to(kro[:, None, :], (s, B_HEADS, B_ROPE)), zq],
                         axis=-1).transpose(1, 0, 2).astype(BF16)
    vb = kv3[:, :, B_NOPE:].transpose(1, 0, 2).astype(BF16)
    (ob, lse_b), rode = _causal_fwd_t(qb, kb, vb.transpose(0, 2, 1), scale=(B_NOPE + B_ROPE) ** -0.5, name="mla_fwd",
                                      tq=512, rider=rider)
    attn = jnp.concatenate([oa.transpose(2, 0, 1).reshape(s, -1), ob.transpose(2, 0, 1).reshape(s, -1)], axis=-1)
    out = _mm_res_fwd(attn, w["ev_out"], h, scale=1.0, name="ev_out")
    sv = dict(h_in=h, xn=xn, c_q=c_q, c_kv=c_kv, xn_q=xn_q, xn_kv=xn_kv, qa=qa, ka=ka, va=va, oa=oa, lse_a=lse_a,
              ss=ss, qb=qb, kb=kb, vb=vb, ob=ob, lse_b=lse_b, attn=attn, cos32=cos32, sin32=sin32,
              cos256=cos256, sin256=sin256)
    return out, sv, rode


def _even_bwd(dh, w, sv, i, rider=None):
    s = dh.shape[0]
    half = B_ROPE // 2
    g = {}
    dattn = _mm_nt(dh, w["ev_out"], name="ev_out_dx")
    g["ev_w_out"] = _mm_tn(sv["attn"], dh, name="ev_out_dw")
    doa = dattn[:, :512].reshape(s, A_HEADS, A_HEAD_DIM).transpose(1, 2, 0).astype(BF16)
    dob = dattn[:, 512:].reshape(s, B_HEADS, B_V).transpose(1, 2, 0).astype(BF16)
    dqa, dka, dva, dsink = _swa_bwd_t(sv["qa"], sv["ka"], sv["va"], sv["oa"], doa, sv["lse_a"], sv["ss"],
                                      scale=A_HEAD_DIM ** -0.5, window=WINDOW, name="swa_bwd")
    g["ev_sinks"] = dsink[:, :, 0, 0].reshape(1, A_HEADS)
    (dqb, dkb, dvb), rode = _causal_bwd_t(sv["qb"], sv["kb"], sv["vb"], sv["ob"], dob, sv["lse_b"],
                                          scale=(B_NOPE + B_ROPE) ** -0.5, name="mla_bwd", tq=512, rider=rider)
    dq_r1 = dqb[:, :, B_NOPE:B_NOPE + half].transpose(1, 0, 2).reshape(s, -1)
    dq_r2 = dqb[:, :, B_NOPE + half:B_NOPE + B_ROPE].transpose(1, 0, 2).reshape(s, -1)
    dq1, dq2 = _rope_bwd(jnp.concatenate([dq_r1, dq_r2], axis=-1)[None], sv["cos256"], sv["sin256"], name="ev_q_rope_bwd")
    dyq = jnp.concatenate([_from_heads(dqb[:, :, :B_NOPE]), dq1, dq2], axis=-1)
    dwq = _mm_tn(sv["xn_q"], dyq, name="ev_q_up_dw")
    dcq, g["ev_cq_norm"] = _mm_nt_rmsbwd([(dyq, w["ev_q_cat"])], sv["c_q"], w["ev_cq_norm"], None, name="ev_q_up_dx")
    kq = sv["c_q"].shape[1]
    d_nope = dwq[:, :512].reshape(kq, B_HEADS, B_NOPE)
    d_r1 = (dwq[:, 512:640] + dwq[:, 896:1024]).reshape(kq, B_HEADS, half)
    d_r2 = (dwq[:, 640:768] - dwq[:, 768:896]).reshape(kq, B_HEADS, half)
    g["ev_w_uq"] = jnp.concatenate([d_nope, d_r1, d_r2], axis=-1).reshape(kq, -1)
    dykv = jnp.concatenate([dkb[:, :, :B_NOPE].transpose(1, 0, 2), dvb.transpose(2, 0, 1)], axis=-1).reshape(s, -1)
    g["ev_w_ukv"] = _mm_tn(sv["xn_kv"], dykv, name="ev_kv_up_dw")
    dckv, g["ev_ckv_norm"] = _mm_nt_rmsbwd([(dykv, w["ev_ukv"])], sv["c_kv"], w["ev_ckv_norm"], None, name="ev_kv_up_dx")
    dk1, dk2 = _rope_bwd(dkb[:, :, B_NOPE:B_NOPE + B_ROPE], sv["cos32"], sv["sin32"], name="ev_k_rope_bwd")
    dycat = jnp.concatenate([_from_heads(dqa), _from_heads(dka), dva.transpose(2, 0, 1).reshape(s, -1),
                             dcq, dckv, dk1, dk2, jnp.zeros((s, 64), F32)], axis=-1)
    dwin = _mm_tn(sv["xn"], dycat, name="ev_in_dw")
    base = 1184
    g["ev_w_in"] = jnp.concatenate([dwin[:, :base - B_ROPE],
                                    dwin[:, base - B_ROPE:base - half] + dwin[:, base + half:base + B_ROPE],
                                    dwin[:, base - half:base] - dwin[:, base:base + half]], axis=-1)
    dh_in, dnorm = _mm_nt_rmsbwd([(dycat, w["ev_in_cat"])], sv["h_in"], w["mix_norm"][i:i + 1], dh, name="ev_in_dx")
    return dh_in, dnorm, g, rode


def _odd_fwd(h, w, i, rider=None):
    s = h.shape[0]
    wd = C_HEADS * C_HEAD_DIM
    y, xn = _rms_mm_fwd(h, w["mix_norm"][i:i + 1], w["od_in_pad"], name="od_in")
    scale = C_HEAD_DIM ** -0.5
    ft = y[:, 3 * wd:3 * wd + C_HEADS].T
    bf = w["od_b_f"].reshape(C_HEADS, 1)
    cb3 = _fox_gate_fwd(ft, bf, out_scale=-1.0 / scale, name="fox_gate_fwd")
    ones, zeros = jnp.ones((s, C_HEADS, 1), BF16), jnp.zeros((s, C_HEADS, 1), BF16)
    tail = jnp.zeros((s, C_HEADS, LANES - C_HEAD_DIM - 5), BF16)
    q3 = y[:, :wd].reshape(s, C_HEADS, C_HEAD_DIM).astype(BF16)
    k3 = y[:, wd:2 * wd].reshape(s, C_HEADS, C_HEAD_DIM).astype(BF16)
    q = jnp.concatenate([q3, ones, zeros, ones, ones, ones, tail], axis=-1).transpose(1, 0, 2)
    k = jnp.concatenate([k3, zeros, ones, cb3.transpose(2, 1, 0), tail], axis=-1).transpose(1, 0, 2)
    v = _to_heads(y[:, 2 * wd:3 * wd], C_HEADS, C_HEAD_DIM)
    (o, lse), rode = _causal_fwd_t(q, k, v.transpose(0, 2, 1), scale=scale, name="fox_fwd", tq=512, rider=rider)
    attn = o.transpose(2, 0, 1).reshape(s, -1)
    out = _mm_res_fwd(attn, w["od_out"], h, scale=1.0, name="od_out")
    return out, dict(h_in=h, xn=xn, q=q, k=k, v=v, o=o, lse=lse, ft=ft, bf=bf, attn=attn), rode


def _odd_bwd(dh, w, sv, i, rider=None):
    s = dh.shape[0]
    g = {}
    dattn = _mm_nt(dh, w["od_out"], name="od_out_dx")
    g["od_w_out"] = _mm_tn(sv["attn"], dh, name="od_out_dw")
    do = dattn.reshape(s, C_HEADS, C_HEAD_DIM).transpose(1, 2, 0).astype(BF16)
    scale = C_HEAD_DIM ** -0.5
    (dq, dk, dv), rode = _causal_bwd_t(sv["q"], sv["k"], sv["v"], sv["o"], do, sv["lse"], scale=scale, name="fox_bwd",
                                       tq=512, rider=rider)
    dft, dbf = _fox_gate_bwd(dq[:, :, C_HEAD_DIM + 1], dk[:, :, C_HEAD_DIM], sv["ft"], sv["bf"],
                             inv_scale=1.0 / scale, name="fox_gate_bwd")
    dq, dk = dq[:, :, :C_HEAD_DIM], dk[:, :, :C_HEAD_DIM]
    g["od_b_f"] = dbf.reshape(1, C_HEADS)
    n_pad = w["od_in_pad"].shape[1]
    n_real = 3 * C_HEADS * C_HEAD_DIM + C_HEADS
    dy = jnp.concatenate([_from_heads(dq), _from_heads(dk), dv.transpose(2, 0, 1).reshape(s, -1), dft.T,
                          jnp.zeros((s, n_pad - n_real), F32)], axis=-1)
    g["od_w_in"] = _mm_tn(sv["xn"], dy, name="od_in_dw")[:, :n_real]
    dh_in, dnorm = _mm_nt_rmsbwd([(dy, w["od_in_pad"])], sv["h_in"], w["mix_norm"][i:i + 1], dh, name="od_in_dx")
    return dh_in, dnorm, g, rode


def _kernel_weights(full, replicated):
    w = dict(replicated)
    _install_weights(w, {(n, i): a for n, per_layer in full.items() for i, a in enumerate(per_layer)})
    return w


def _install_weights(w, got):
    raw = w.setdefault("raw", {})
    raw.update(got)
    for (n, i), a in got.items():
        if n in ("ffa_w_gate_up", "ffa_w_down", "ffb_w_gate_up", "ffb_w_down"):
            w.setdefault(n[:3], {}).setdefault(i, {})["wgu" if n.endswith("gate_up") else "wd"] = a
        elif n in ("ple_w_gate", "ple_w_proj"):
            w.setdefault("ple_gate" if n.endswith("gate") else "ple_proj", {})[i] = a
    if "ev_in_cat" not in w and all((n, 0) in raw for n in ("ev_w_in", "ev_w_uq", "ev_w_ukv", "ev_w_out")):
        w["ev_in_cat"], w["ev_q_cat"], w["ev_ukv"] = _even_weights(raw["ev_w_in", 0], raw["ev_w_uq", 0], raw["ev_w_ukv", 0])
        w["ev_out"] = raw["ev_w_out", 0]
    if "od_in_pad" not in w and all((n, 0) in raw for n in ("od_w_in", "od_w_out")):
        od_in = raw["od_w_in", 0]
        w["od_in_pad"] = jnp.pad(od_in, ((0, 0), (0, (-od_in.shape[1]) % LANES)))
        w["od_out"] = raw["od_w_out", 0]


_LAYER0 = tuple((n, 0) for n in ("ffa_w_gate_up", "ffa_w_down", "ev_w_in", "ev_w_uq", "ev_w_ukv", "ev_w_out",
                                "ffb_w_gate_up", "ffb_w_down", "ple_w_gate", "ple_w_proj"))
_LAYER1_HEAD = (("ffa_w_gate_up", 1), ("ffa_w_down", 1), ("od_w_in", 0), ("od_w_out", 0))
_LAYER1_TAIL = (("ffb_w_gate_up", 1), ("ffb_w_down", 1), ("ple_w_gate", 1), ("ple_w_proj", 1))
_LAYER0_TAIL = (("ffb_w_gate_up", 0), ("ffb_w_down", 0), ("ple_w_gate", 0), ("ple_w_proj", 0))
_LAYER0_HEAD = tuple((n, 0) for n in ("ffa_w_gate_up", "ffa_w_down", "ev_w_in", "ev_w_uq", "ev_w_ukv", "ev_w_out"))


def _local_step(x, p, tgt, w, ex=None):
    depth = p.shape[0]
    h = x
    saved = []
    for i in range(depth):
        sv = {}
        h, sv["ffa"] = _ffn_fwd(h, w["ffa_norm"][i:i + 1], w["ffa"][i], f"ffa{i}")
        keys = None if ex is None else (_LAYER1_HEAD, _LAYER1_TAIL)[i]
        rider = None if ex is None else ex.gather_rider(keys)
        h, sv["mix"], rode = (_even_fwd if i % 2 == 0 else _odd_fwd)(h, w, i, rider)
        if ex is not None:
            _install_weights(w, ex.gather_finish(keys, rode, name=f"weight_forward{i + 1}"))
        h, sv["ffb"] = _ffn_fwd(h, w["ffb_norm"][i:i + 1], w["ffb"][i], f"ffb{i}")
        h_in = h
        h, xn, gate, pp = _ple_fwd(h, w["ple_norm"][i:i + 1], w["ple_gate"][i], p[i], w["ple_proj"][i], name=f"ple{i}")
        sv["ple"] = dict(h_in=h_in, xn=xn, gate=gate, pp=pp)
        saved.append(sv)
    loss_vec, dh, d_final = _final_loss(h, w["final_norm"].reshape(1, -1), tgt, name="final_loss")

    per_layer = [dict() for _ in range(depth)]
    mats = {}
    grads = {}
    for i in reversed(range(depth)):
        sv, gl = saved[i], per_layer[i]
        dz, dpp = _ple_bwd_elem(dh, sv["ple"]["gate"], sv["ple"]["pp"], name=f"ple{i}_bwd")
        mats["ple_w_gate", i] = _mm_tn(sv["ple"]["xn"], dz, name=f"ple{i}_dwg")
        mats["ple_w_proj", i] = _mm_tn(p[i], dpp, name=f"ple{i}_dwp")
        dh, gl["ple_norm"] = _mm_nt_rmsbwd([(dz, w["ple_gate"][i])], sv["ple"]["h_in"], w["ple_norm"][i:i + 1], dh,
                                           name=f"ple{i}_dx")
        dh, gl["ffb_norm"], mats["ffb_w_gate_up", i], mats["ffb_w_down", i] = _ffn_bwd(
            dh, w["ffb_norm"][i:i + 1], w["ffb"][i], sv["ffb"], f"ffb{i}")
        keys = None if ex is None else (_LAYER1_HEAD + _LAYER0_TAIL, _LAYER1_TAIL)[i]
        state = None if ex is None else ex.reduce_begin(keys, mats, tag=f"g{i}")
        dh, gl["mix_norm"], gm, rode = (_even_bwd if i % 2 == 0 else _odd_bwd)(dh, w, sv["mix"], i,
                                                                              None if ex is None else state[0])
        if ex is not None:
            ex.reduce_finish(state, rode)
        for n, g in gm.items():
            if n in REPLICATED:
                grads[n] = g
            else:
                mats[n, 0] = g
        dh, gl["ffa_norm"], mats["ffa_w_gate_up", i], mats["ffa_w_down", i] = _ffn_bwd(
            dh, w["ffa_norm"][i:i + 1], w["ffa"][i], sv["ffa"], f"ffa{i}")
    grads["final_norm"] = d_final.reshape(-1)
    for n in ("ffa_norm", "mix_norm", "ffb_norm", "ple_norm"):
        grads[n] = jnp.concatenate([per_layer[i][n] for i in range(depth)], axis=0)
    if ex is not None:
        ex.reduce(_LAYER0_HEAD, mats, tag="g2")
    else:
        for n, _ in SHARDED:
            grads[n] = [mats[n, i] for i in range(depth) if (n, i) in mats]
    return loss_vec[0, 0], dh, grads


def _cut_mode(local_shape, axis, ncols):
    return "cols" if axis == 2 and ncols % LANES == 0 else "blk"


class _Exchange:
    def __init__(self, wts):
        self.place = jnp.stack([2 * lax.axis_index("x") + lax.axis_index("y"), lax.axis_index("c")]).astype(jnp.int32)
        self.info = {}
        for n, axis in SHARDED:
            wb = wts[n].astype(BF16)
            mode = _cut_mode(wb.shape, axis, wb.shape[2])
            for i in range(wb.shape[0]):
                self.info[n, i] = dict(shard=wb[i], mode=mode, axis=axis)
        self.halves = {}

    def _modes(self, keys):
        return [self.info[k]["mode"] for k in keys]

    def gather_rider(self, keys):
        return _gather_rider([self.info[k]["shard"] for k in keys], self._modes(keys))

    def gather_finish(self, keys, landed, *, name):
        outs = _gather_forward(landed, [self.info[k]["shard"].shape for k in keys], self._modes(keys), name=name)
        got = {}
        for k, dst in zip(keys, outs):
            if self.info[k]["mode"] == "blk":
                dst = dst.reshape(-1, dst.shape[2]) if self.info[k]["axis"] == 1 else jnp.moveaxis(dst, 0, 1).reshape(dst.shape[1], -1)
            got[k] = dst
        return got

    def gather(self, keys, *, name):
        return self.gather_finish(keys, _run_rider(self.gather_rider(keys), name=name), name=name + "_forward")

    def reduce_begin(self, keys, mats, *, tag):
        modes = self._modes(keys)
        arrs = []
        for k in keys:
            g2, (rr, cc) = mats[k], self.info[k]["shard"].shape
            if self.info[k]["mode"] == "blk":
                g2 = g2.reshape(N_CHIPS, rr, cc) if self.info[k]["axis"] == 1 else g2.reshape(rr, N_CHIPS, cc).transpose(1, 0, 2)
            arrs.append(g2)
        landed = _rs_pair_swap(arrs, modes, name=f"rs_pair_swap_{tag}")
        parts = []
        for (n, i), m, a, l in zip(keys, modes, arrs, landed):
            pt = _rs_pair_add(_blk_view(a, m), _blk_view(l, m), self.place, name=f"rs_pair_add_{n}{i}")
            parts.append(pt[0] if m == "cols" else pt)
        return _exchange_rider(parts, modes), keys, parts

    def reduce_finish(self, state, landed):
        _, keys, parts = state
        for (n, i), m, pt, l in zip(keys, self._modes(keys), parts, landed):
            self.halves[n, i] = _rs_chip_sum(pt, l, m, self.place, name=f"rs_chip_sum_{n}{i}")

    def reduce(self, keys, mats, *, tag):
        state = self.reduce_begin(keys, mats, tag=tag)
        self.reduce_finish(state, _run_rider(state[0], name=f"rs_chip_exchange_{tag}"))

    def join(self, wts):
        keys = list(self.info)
        joined = dict(zip(keys, _rs_pair_join([self.halves[k] for k in keys], name="rs_pair_join")))
        return {n: jnp.stack([joined[n, i] for i in range(wts[n].shape[0])]).reshape(wts[n].shape) for n, _ in SHARDED}


def _small_rows(vals):
    rows = []
    for n in REPLICATED:
        v = vals[n].reshape(-1)
        rows.append(jnp.pad(v, (0, (-v.shape[0]) % FLAT_COLS)).reshape(-1, FLAT_COLS))
    out = jnp.concatenate(rows, axis=0)
    return jnp.pad(out, ((0, (-out.shape[0]) % 8), (0, 0)))


def kernel(x, p, ffa_norm, ffa_w_gate_up, ffa_w_down, mix_norm, ffb_norm, ffb_w_gate_up, ffb_w_down, ple_norm, ple_w_gate, ple_w_proj, ev_w_in, ev_sinks, ev_cq_norm, ev_w_uq, ev_ckv_norm, ev_w_ukv, ev_w_out, od_w_in, od_b_f, od_w_out, final_norm, loss_target, m_ffa_norm, m_ffa_w_gate_up, m_ffa_w_down, m_mix_norm, m_ffb_norm, m_ffb_w_gate_up, m_ffb_w_down, m_ple_norm, m_ple_w_gate, m_ple_w_proj, m_ev_w_in, m_ev_sinks, m_ev_cq_norm, m_ev_w_uq, m_ev_ckv_norm, m_ev_w_ukv, m_ev_w_out, m_od_w_in, m_od_b_f, m_od_w_out, m_final_norm, v_ffa_norm, v_ffa_w_gate_up, v_ffa_w_down, v_mix_norm, v_ffb_norm, v_ffb_w_gate_up, v_ffb_w_down, v_ple_norm, v_ple_w_gate, v_ple_w_proj, v_ev_w_in, v_ev_sinks, v_ev_cq_norm, v_ev_w_uq, v_ev_ckv_norm, v_ev_w_ukv, v_ev_w_out, v_od_w_in, v_od_b_f, v_od_w_out, v_final_norm):
    env = dict(locals())
    wts = {n: env[n] for n in WEIGHT_ORDER}
    mom1 = {n: env["m_" + n] for n in WEIGHT_ORDER}
    mom2 = {n: env["v_" + n] for n in WEIGHT_ORDER}
    ex = _Exchange(wts)

    w = {n: wts[n] for n in REPLICATED}
    _install_weights(w, ex.gather(_LAYER0, name="weight_gather0"))

    loss_part, grad_x, grads = _local_step(x[0], p[:, 0], loss_target[0], w, ex)
    loss = lax.psum(loss_part, ("x", "y", "c"))
    gout = ex.join(wts)
    small = _allreduce_small(_small_rows(grads), name="small_allreduce")
    r0 = 0
    for n in REPLICATED:
        size = int(np.prod(wts[n].shape))
        nr = -(-size // FLAT_COLS)
        gout[n] = small[r0:r0 + nr].reshape(-1)[:size].reshape(wts[n].shape)
        r0 += nr

    delta, new_m, new_v = {}, {}, {}
    for n in WEIGHT_ORDER:
        delta[n], new_m[n], new_v[n] = _adamw(wts[n], gout[n], mom1[n], mom2[n], name="adamw_" + n)
    return (loss, grad_x[None], *[gout[n] for n in WEIGHT_ORDER], *[delta[n] for n in WEIGHT_ORDER],
            *[new_m[n] for n in WEIGHT_ORDER], *[new_v[n] for n in WEIGHT_ORDER])
```

```python
import functools
import math

import numpy as np
import jax
import jax.numpy as jnp
from jax import lax
from jax.experimental import pallas as pl
from jax.experimental.pallas import tpu as pltpu

F32 = jnp.float32
BF16 = jnp.bfloat16
NT = (((1,), (1,)), ((), ()))
TN = (((0,), (0,)), ((), ()))
MESH = pl.DeviceIdType.MESH

RMS_EPS = 1e-6
FFN_RES_SCALE = 0.5
A_HEADS, A_KV_HEADS, A_HEAD_DIM, WINDOW = 8, 2, 64, 128
B_HEADS, B_Q_LORA, B_KV_LORA, B_NOPE, B_ROPE, B_V = 8, 256, 128, 64, 32, 64
ROPE_THETA = 10000.0
C_HEADS, C_HEAD_DIM = 16, 64
ADAM_LR, ADAM_B1, ADAM_B2, ADAM_EPS, ADAM_WD, ADAM_STEP = 0.001, 0.9, 0.999, 1e-08, 0.01, 10

N_CHIPS = 4
LANES = 128
FLAT_COLS = 1024
MASK_VALUE = -1e30
VMEM_LIMIT = 48 * 2**20

SHARDED = (
    ("ffa_w_gate_up", 2), ("ffa_w_down", 1), ("ffb_w_gate_up", 2), ("ffb_w_down", 1),
    ("ple_w_gate", 1), ("ple_w_proj", 2), ("ev_w_in", 2), ("ev_w_uq", 2), ("ev_w_ukv", 2),
    ("ev_w_out", 1), ("od_w_in", 2), ("od_w_out", 1))
REPLICATED = ("ffa_norm", "mix_norm", "ffb_norm", "ple_norm", "final_norm",
              "ev_sinks", "ev_cq_norm", "ev_ckv_norm", "od_b_f")
WEIGHT_ORDER = ("ffa_norm", "ffa_w_gate_up", "ffa_w_down", "mix_norm", "ffb_norm", "ffb_w_gate_up",
                "ffb_w_down", "ple_norm", "ple_w_gate", "ple_w_proj", "ev_w_in", "ev_sinks",
                "ev_cq_norm", "ev_w_uq", "ev_ckv_norm", "ev_w_ukv", "ev_w_out", "od_w_in", "od_b_f",
                "od_w_out", "final_norm")


def _cp(*sem):
    return pltpu.CompilerParams(dimension_semantics=sem, vmem_limit_bytes=VMEM_LIMIT)


def _sigmoid(z):
    return 1.0 / (1.0 + jnp.exp(-z))


def _rms_stats(xv):
    r = lax.rsqrt(jnp.mean(xv * xv, axis=-1, keepdims=True) + RMS_EPS)
    return r, xv * r


def _rms_bwd(dxn, xv, g):
    r, xhat = _rms_stats(xv)
    u = dxn * g
    dx = r * (u - xhat * jnp.mean(u * xhat, axis=-1, keepdims=True))
    return dx, dxn * xhat


def _col_tile(k_rows, n, budget_bytes=6 * 2**20):
    if k_rows * n * 4 <= budget_bytes or n % LANES:
        return n
    units = n // LANES
    best = LANES
    for d in range(1, units + 1):
        if units % d == 0 and k_rows * d * LANES * 4 <= budget_bytes:
            best = d * LANES
    return best


def _row_tile(rows, cols, target_elems=2**18):
    if rows * cols <= target_elems or rows % 8:
        return rows
    best = 8
    for d in range(8, rows + 1, 8):
        if rows % d == 0 and d * cols <= target_elems:
            best = d
    return best


def _rms_mm_fwd(x, g, w, *, name, tm=512):
    s, k = x.shape
    n = w.shape[1]

    def body(x_ref, g_ref, w_ref, y_ref, xn_ref):
        _, xhat = _rms_stats(x_ref[...])
        xn = (xhat * g_ref[...]).astype(BF16)
        xn_ref[...] = xn
        y_ref[...] = jnp.dot(xn, w_ref[...], preferred_element_type=F32)

    return pl.pallas_call(
        body, name=name, grid=(s // tm,),
        in_specs=[pl.BlockSpec((tm, k), lambda i: (i, 0)), pl.BlockSpec((1, k), lambda i: (0, 0)),
                  pl.BlockSpec((k, n), lambda i: (0, 0))],
        out_specs=[pl.BlockSpec((tm, n), lambda i: (i, 0)), pl.BlockSpec((tm, k), lambda i: (i, 0))],
        out_shape=[jax.ShapeDtypeStruct((s, n), F32), jax.ShapeDtypeStruct((s, k), BF16)],
        compiler_params=_cp("arbitrary"))(x, g, w)


def _ffn_up(x, g, wgu, *, name, tm=512, rider=None):
    s, k = x.shape
    f = wgu.shape[1] // 2
    tn = _col_tile(k, f)
    nj = f // tn

    def body(x_ref, g_ref, wg_ref, wu_ref, gate_ref, up_ref, act_ref, xn_ref, xn_sc):
        @pl.when(pl.program_id(1) == 0)
        def _():
            _, xhat = _rms_stats(x_ref[...])
            xn = (xhat * g_ref[...]).astype(BF16)
            xn_sc[...] = xn
            xn_ref[...] = xn

        xn = xn_sc[...]
        gg = jnp.dot(xn, wg_ref[...], preferred_element_type=F32)
        uu = jnp.dot(xn, wu_ref[...], preferred_element_type=F32)
        gate_ref[...] = gg.astype(BF16)
        up_ref[...] = uu.astype(BF16)
        act_ref[...] = ((gg * _sigmoid(gg)) * uu).astype(BF16)

    tile = pl.BlockSpec((tm, tn), lambda i, j: (i, j))
    return _call_with_rider(
        body, rider, name=name, grid=(s // tm, nj),
        in_specs=[pl.BlockSpec((tm, k), lambda i, j: (i, 0)), pl.BlockSpec((1, k), lambda i, j: (0, 0)),
                  pl.BlockSpec((k, tn), lambda i, j: (0, j)), pl.BlockSpec((k, tn), lambda i, j: (0, j + nj))],
        out_specs=[tile, tile, tile, pl.BlockSpec((tm, k), lambda i, j: (i, 0))],
        out_shape=[jax.ShapeDtypeStruct((s, f), BF16)] * 3 + [jax.ShapeDtypeStruct((s, k), BF16)],
        scratch_shapes=[pltpu.VMEM((tm, k), BF16)],
        compiler_params=_cp("arbitrary", "arbitrary"), args=(x, g, wgu, wgu))


def _mm_res_fwd(a, w, res, *, scale, name, tm=512):
    s, k = a.shape
    n = w.shape[1]

    def body(a_ref, w_ref, r_ref, o_ref):
        o_ref[...] = r_ref[...] + scale * jnp.dot(a_ref[...], w_ref[...], preferred_element_type=F32)

    return pl.pallas_call(
        body, name=name, grid=(s // tm,),
        in_specs=[pl.BlockSpec((tm, k), lambda i: (i, 0)), pl.BlockSpec((k, n), lambda i: (0, 0)),
                  pl.BlockSpec((tm, n), lambda i: (i, 0))],
        out_specs=pl.BlockSpec((tm, n), lambda i: (i, 0)),
        out_shape=jax.ShapeDtypeStruct((s, n), F32),
        compiler_params=_cp("arbitrary"))(a, w, res)


def _ffn_down_bwd(dh, wd, gate, up, *, scale, name, tm=512, rider=None):
    s, d = dh.shape
    f = wd.shape[0]
    tn = _col_tile(d, f)

    def body(dh_ref, wd_ref, gate_ref, up_ref, dg_ref, du_ref):
        dhb = (dh_ref[...] * scale).astype(BF16)
        da = lax.dot_general(dhb, wd_ref[...], NT, preferred_element_type=F32)
        gg = gate_ref[...].astype(F32)
        uu = up_ref[...].astype(F32)
        sg = _sigmoid(gg)
        dg_ref[...] = (da * uu * (sg * (1.0 + gg * (1.0 - sg)))).astype(BF16)
        du_ref[...] = (da * (gg * sg)).astype(BF16)

    tile = pl.BlockSpec((tm, tn), lambda i, j: (i, j))
    return _call_with_rider(
        body, rider, name=name, grid=(s // tm, f // tn),
        in_specs=[pl.BlockSpec((tm, d), lambda i, j: (i, 0)), pl.BlockSpec((tn, d), lambda i, j: (j, 0)), tile, tile],
        out_specs=[tile, tile],
        out_shape=[jax.ShapeDtypeStruct((s, f), BF16)] * 2, scratch_shapes=[],
        compiler_params=_cp("arbitrary", "arbitrary"), args=(dh, wd, gate, up))


def _mm_tn(a, bs, *, name, b_scale=1.0, ts=512):
    bs = list(bs) if isinstance(bs, (list, tuple)) else [bs]
    s, k = a.shape
    n = bs[0].shape[1]
    tn = _col_tile(k, n, 12 * 2**20)
    per = n // tn

    def body(a_ref, *refs):
        b_refs, o_ref = refs[:-1], refs[-1]
        j = pl.program_id(0)

        @pl.when(pl.program_id(1) == 0)
        def _():
            o_ref[...] = jnp.zeros_like(o_ref)

        for m, b_ref in enumerate(b_refs):
            def acc(b_ref=b_ref):
                bv = b_ref[...]
                if b_scale != 1.0:
                    bv = bv * b_scale
                o_ref[...] += lax.dot_general(a_ref[...].astype(BF16), bv.astype(BF16), TN, preferred_element_type=F32)

            if len(b_refs) == 1:
                acc()
            else:
                pl.when(jnp.logical_and(j >= m * per, j < (m + 1) * per))(acc)

    def b_spec(m):
        def idx(j, t):
            mine = jnp.logical_and(j >= m * per, j < (m + 1) * per)
            return (jnp.where(mine, t, 0), jnp.clip(j - m * per, 0, per - 1))
        return pl.BlockSpec((ts, tn), idx)

    return pl.pallas_call(
        body, name=name, grid=(per * len(bs), s // ts),
        in_specs=[pl.BlockSpec((ts, k), lambda j, t: (t, 0))] + [b_spec(m) for m in range(len(bs))],
        out_specs=pl.BlockSpec((k, tn), lambda j, t: (0, j)),
        out_shape=jax.ShapeDtypeStruct((k, n * len(bs)), F32),
        compiler_params=_cp("arbitrary", "arbitrary"))(a, *bs)


def _mm_nt(dy, w, *, name, tm=512):
    s, n = dy.shape
    k = w.shape[0]

    def body(dy_ref, w_ref, o_ref):
        o_ref[...] = lax.dot_general(dy_ref[...].astype(BF16), w_ref[...], NT, preferred_element_type=F32)

    return pl.pallas_call(
        body, name=name, grid=(s // tm,),
        in_specs=[pl.BlockSpec((tm, n), lambda i: (i, 0)), pl.BlockSpec((k, n), lambda i: (0, 0))],
        out_specs=pl.BlockSpec((tm, k), lambda i: (i, 0)),
        out_shape=jax.ShapeDtypeStruct((s, k), F32),
        compiler_params=_cp("arbitrary"))(dy, w)


def _mm_nt_rmsbwd(pairs, x, g, dres, *, name, tm=256):
    s, k = x.shape
    npairs = len(pairs)
    pairs = [pr if len(pr) == 3 else (pr[0], pr[1], 0) for pr in pairs]

    def body(*refs):
        dy_refs = refs[0:2 * npairs:2]
        w_refs = refs[1:2 * npairs:2]
        rest = refs[2 * npairs:]
        x_ref, g_ref = rest[0], rest[1]
        if dres is None:
            dx_ref, dg_ref = rest[2], rest[3]
        else:
            dres_ref, dx_ref, dg_ref = rest[2], rest[3], rest[4]
        dxn = None
        for dy_ref, w_ref in zip(dy_refs, w_refs):
            t = lax.dot_general(dy_ref[...].astype(BF16), w_ref[...], NT, preferred_element_type=F32)
            dxn = t if dxn is None else dxn + t
        dx, dgrow = _rms_bwd(dxn, x_ref[...], g_ref[...])
        if dres is not None:
            dx = dx + dres_ref[...]
        dx_ref[...] = dx

        @pl.when(pl.program_id(0) == 0)
        def _():
            dg_ref[...] = jnp.zeros_like(dg_ref)

        dg_ref[...] += jnp.sum(dgrow, axis=0, keepdims=True)

    in_specs, args = [], []
    for dy, w, cb in pairs:
        n = dy.shape[1]
        in_specs += [pl.BlockSpec((tm, n), lambda i: (i, 0)), pl.BlockSpec((k, n), lambda i, cb=cb: (0, cb))]
        args += [dy, w]
    row = pl.BlockSpec((tm, k), lambda i: (i, 0))
    vec = pl.BlockSpec((1, k), lambda i: (0, 0))
    in_specs += [row, vec]
    args += [x, g]
    if dres is not None:
        in_specs.append(row)
        args.append(dres)
    return pl.pallas_call(
        body, name=name, grid=(s // tm,), in_specs=in_specs, out_specs=[row, vec],
        out_shape=[jax.ShapeDtypeStruct((s, k), F32), jax.ShapeDtypeStruct((1, k), F32)],
        compiler_params=_cp("arbitrary"))(*args)


def _ple_fwd(h, g, wg, p, wp, *, name, tm=512):
    s, d = h.shape
    pd = p.shape[1]

    def body(h_ref, g_ref, wg_ref, p_ref, wp_ref, o_ref, xn_ref, gate_ref, pp_ref):
        hv = h_ref[...]
        _, xhat = _rms_stats(hv)
        xn = (xhat * g_ref[...]).astype(BF16)
        xn_ref[...] = xn
        gate = _sigmoid(jnp.dot(xn, wg_ref[...], preferred_element_type=F32))
        pp = jnp.dot(p_ref[...].astype(BF16), wp_ref[...], preferred_element_type=F32)
        gate_ref[...] = gate.astype(BF16)
        pp_ref[...] = pp.astype(BF16)
        o_ref[...] = hv + gate * pp

    row = pl.BlockSpec((tm, d), lambda i: (i, 0))
    return pl.pallas_call(
        body, name=name, grid=(s // tm,),
        in_specs=[row, pl.BlockSpec((1, d), lambda i: (0, 0)), pl.BlockSpec((d, d), lambda i: (0, 0)),
                  pl.BlockSpec((tm, pd), lambda i: (i, 0)), pl.BlockSpec((pd, d), lambda i: (0, 0))],
        out_specs=[row, row, row, row],
        out_shape=[jax.ShapeDtypeStruct((s, d), F32)] + [jax.ShapeDtypeStruct((s, d), BF16)] * 3,
        compiler_params=_cp("arbitrary"))(h, g, wg, p, wp)


def _ple_bwd_elem(dh, gate, pp, *, name, tm=512):
    s, d = dh.shape

    def body(dh_ref, gate_ref, pp_ref, dz_ref, dpp_ref):
        dhv = dh_ref[...]
        gt = gate_ref[...].astype(F32)
        dz_ref[...] = (dhv * pp_ref[...].astype(F32) * (gt * (1.0 - gt))).astype(BF16)
        dpp_ref[...] = (dhv * gt).astype(BF16)

    row = pl.BlockSpec((tm, d), lambda i: (i, 0))
    return pl.pallas_call(
        body, name=name, grid=(s // tm,), in_specs=[row, row, row], out_specs=[row, row],
        out_shape=[jax.ShapeDtypeStruct((s, d), BF16)] * 2,
        compiler_params=_cp("arbitrary"))(dh, gate, pp)


def _final_loss(h, g, tgt, *, name, tm=512):
    s, d = h.shape

    def body(h_ref, g_ref, t_ref, loss_ref, dh_ref, dg_ref):
        @pl.when(pl.program_id(0) == 0)
        def _():
            loss_ref[...] = jnp.zeros_like(loss_ref)
            dg_ref[...] = jnp.zeros_like(dg_ref)

        hv = h_ref[...]
        gv = g_ref[...]
        _, xhat = _rms_stats(hv)
        err = xhat * gv - t_ref[...]
        per_row = jnp.mean(err * err, axis=-1, keepdims=True)
        loss_ref[...] += 0.5 * jnp.sum(per_row, axis=0, keepdims=True)
        dx, dgrow = _rms_bwd(err * (1.0 / d), hv, gv)
        dh_ref[...] = dx
        dg_ref[...] += jnp.sum(dgrow, axis=0, keepdims=True)

    row = pl.BlockSpec((tm, d), lambda i: (i, 0))
    vec = pl.BlockSpec((1, d), lambda i: (0, 0))
    return pl.pallas_call(
        body, name=name, grid=(s // tm,), in_specs=[row, vec, row],
        out_specs=[pl.BlockSpec((1, LANES), lambda i: (0, 0)), row, vec],
        out_shape=[jax.ShapeDtypeStruct((1, LANES), F32), jax.ShapeDtypeStruct((s, d), F32),
                   jax.ShapeDtypeStruct((1, d), F32)],
        compiler_params=_cp("arbitrary"))(h, g, tgt)


def _rope_fwd(y1, y2, cos, sin, *, name, tm=512):
    s, r = y1.shape

    def body(a_ref, b_ref, c_ref, s_ref, o_ref):
        o_ref[...] = a_ref[...] * c_ref[...] + b_ref[...] * s_ref[...]

    row = pl.BlockSpec((tm, r), lambda i: (i, 0))
    return pl.pallas_call(
        body, name=name, grid=(s // tm,), in_specs=[row] * 4, out_specs=row,
        out_shape=jax.ShapeDtypeStruct((s, r), F32), compiler_params=_cp("arbitrary"))(y1, y2, cos, sin)


def _rope_bwd(dout, cos, sin, *, name, tm=512):
    nh, s, r = dout.shape

    def body(d_ref, c_ref, s_ref, o1_ref, o2_ref):
        tot = d_ref[0]
        for hh in range(1, nh):
            tot = tot + d_ref[hh]
        o1_ref[...] = tot * c_ref[...]
        o2_ref[...] = tot * s_ref[...]

    row = pl.BlockSpec((tm, r), lambda i: (i, 0))
    return pl.pallas_call(
        body, name=name, grid=(s // tm,),
        in_specs=[pl.BlockSpec((nh, tm, r), lambda i: (0, i, 0)), row, row], out_specs=[row, row],
        out_shape=[jax.ShapeDtypeStruct((s, r), F32)] * 2, compiler_params=_cp("arbitrary"))(dout, cos, sin)


def _split3(v):
    h1 = v.astype(BF16)
    r1 = v - h1.astype(F32)
    h2 = r1.astype(BF16)
    h3 = (r1 - h2.astype(F32)).astype(BF16)
    return h1, h2, h3


def _tri(tb, upper):
    r = lax.broadcasted_iota(jnp.int32, (tb, tb), 0)
    c = lax.broadcasted_iota(jnp.int32, (tb, tb), 1)
    return jnp.where((r <= c) if upper else (r >= c), 1.0, 0.0).astype(BF16)


def _fox_gate_fwd(ft, bf, *, out_scale, name, tb=512):
    nh, s = ft.shape

    def body(f_ref, b_ref, o_ref, carry):
        @pl.when(pl.program_id(0) == 0)
        def _():
            carry[...] = jnp.zeros_like(carry)

        z = f_ref[...] + b_ref[...]
        lf = jnp.minimum(z, 0.0) - jnp.log(1.0 + jnp.exp(-jnp.abs(z)))
        tri = _tri(tb, True)
        cs = sum(jnp.dot(t, tri, preferred_element_type=F32) for t in _split3(lf)) + carry[...]
        for n, term in enumerate(_split3(cs * out_scale)):
            o_ref[n] = term
        carry[...] += jnp.sum(lf, axis=-1, keepdims=True)

    return pl.pallas_call(
        body, name=name, grid=(s // tb,),
        in_specs=[pl.BlockSpec((nh, tb), lambda t: (0, t)), pl.BlockSpec((nh, 1), lambda t: (0, 0))],
        out_specs=pl.BlockSpec((3, nh, tb), lambda t: (0, 0, t)),
        out_shape=jax.ShapeDtypeStruct((3, nh, s), BF16),
        scratch_shapes=[pltpu.VMEM((nh, 1), F32)], compiler_params=_cp("arbitrary"))(ft, bf)


def _fox_gate_bwd(drow, dcol, ft, bf, *, inv_scale, name, tb=512):
    nh, s = ft.shape
    nb = s // tb

    def body(dr_ref, dc_ref, f_ref, b_ref, df_ref, db_ref, carry):
        @pl.when(pl.program_id(0) == 0)
        def _():
            carry[...] = jnp.zeros_like(carry)
            db_ref[...] = jnp.zeros_like(db_ref)

        dc = (dr_ref[...] - dc_ref[...]) * inv_scale
        tri = _tri(tb, False)
        suf = sum(jnp.dot(t, tri, preferred_element_type=F32) for t in _split3(dc)) + carry[...]
        z = f_ref[...] + b_ref[...]
        dz = suf * (1.0 / (1.0 + jnp.exp(z)))
        df_ref[...] = dz
        db_ref[...] += jnp.sum(dz, axis=-1, keepdims=True)
        carry[...] += jnp.sum(dc, axis=-1, keepdims=True)

    rev = pl.BlockSpec((nh, tb), lambda t: (0, nb - 1 - t))
    one = pl.BlockSpec((nh, 1), lambda t: (0, 0))
    return pl.pallas_call(
        body, name=name, grid=(nb,), in_specs=[rev, rev, rev, one], out_specs=[rev, one],
        out_shape=[jax.ShapeDtypeStruct((nh, s), F32), jax.ShapeDtypeStruct((nh, 1), F32)],
        scratch_shapes=[pltpu.VMEM((nh, 1), F32)], compiler_params=_cp("arbitrary"))(drow, dcol, ft, bf)


def _tri_fwd(t, nq):
    i = sum((t >= (r * (r + 1)) // 2).astype(jnp.int32) for r in range(1, nq))
    return i, t - (i * (i + 1)) // 2


def _tri_bwd(t, nq):
    j = sum((t >= r * nq - (r * (r - 1)) // 2).astype(jnp.int32) for r in range(1, nq))
    return j, j + t - (j * nq - (j * (j - 1)) // 2)


def _scores_t(k, q, *, scale, diag):
    s = lax.dot_general(k, q, NT, preferred_element_type=F32) * scale
    if diag:
        r = lax.broadcasted_iota(jnp.int32, s.shape, 0)
        c = lax.broadcasted_iota(jnp.int32, s.shape, 1)
        s = jnp.where(r <= c, s, MASK_VALUE)
    return s


def _causal_fwd_t(q, k, vt, *, scale, name, tq, hb=2, rider=None):
    nh, s, dq = q.shape
    dv = vt.shape[1]
    nq = s // tq
    nsteps = (nq * (nq + 1)) // 2

    def body(q_ref, k_ref, vt_ref, o_ref, lse_ref, m_sc, l_sc, acc_sc):
        i, j = _tri_fwd(pl.program_id(1), nq)

        @pl.when(j == 0)
        def _():
            m_sc[...] = jnp.full_like(m_sc, MASK_VALUE)
            l_sc[...] = jnp.zeros_like(l_sc)
            acc_sc[...] = jnp.zeros_like(acc_sc)

        def step(diag):
            for u in range(hb):
                sc = _scores_t(k_ref[u], q_ref[u], scale=scale, diag=diag)
                m_prev = m_sc[u]
                m_new = jnp.maximum(m_prev, jnp.max(sc, axis=0, keepdims=True))
                alpha = jnp.exp(m_prev - m_new)
                pr = jnp.exp(sc - m_new)
                l_new = alpha * l_sc[u] + jnp.sum(pr, axis=0, keepdims=True)
                acc = alpha * acc_sc[u] + jnp.dot(vt_ref[u], pr.astype(BF16), preferred_element_type=F32)
                if diag:
                    o_ref[u] = (acc / l_new).astype(BF16)
                    lse_ref[u] = m_new + jnp.log(l_new)
                else:
                    m_sc[u], l_sc[u], acc_sc[u] = m_new, l_new, acc

        pl.when(j < i)(functools.partial(step, False))
        pl.when(j == i)(functools.partial(step, True))

    def qi(t):
        return _tri_fwd(t, nq)[0]

    def kj(t):
        return _tri_fwd(t, nq)[1]

    return _call_with_rider(
        body, rider, name=name, grid=(nh // hb, nsteps),
        in_specs=[pl.BlockSpec((hb, tq, dq), lambda hp, t: (hp, qi(t), 0)),
                  pl.BlockSpec((hb, tq, dq), lambda hp, t: (hp, kj(t), 0)),
                  pl.BlockSpec((hb, dv, tq), lambda hp, t: (hp, 0, kj(t)))],
        out_specs=[pl.BlockSpec((hb, dv, tq), lambda hp, t: (hp, 0, qi(t))),
                   pl.BlockSpec((hb, 1, tq), lambda hp, t: (hp, 0, qi(t)))],
        out_shape=[jax.ShapeDtypeStruct((nh, dv, s), BF16), jax.ShapeDtypeStruct((nh, 1, s), F32)],
        scratch_shapes=[pltpu.VMEM((hb, 1, tq), F32), pltpu.VMEM((hb, 1, tq), F32), pltpu.VMEM((hb, dv, tq), F32)],
        compiler_params=_cp("arbitrary", "arbitrary"), args=(q, k, vt))


def _causal_bwd_t(q, k, v, ot, dot_, lse, *, scale, name, tq, hb=2, rider=None):
    nh, s, dq = q.shape
    dv = v.shape[-1]
    nq = s // tq
    nsteps = (nq * (nq + 1)) // 2

    def body(q_ref, k_ref, v_ref, ot_ref, dot_ref, lse_ref, dq_ref, dk_ref, dvt_ref):
        t = pl.program_id(1)
        j, i = _tri_bwd(t, nq)

        @pl.when(t == 0)
        def _():
            dq_ref[...] = jnp.zeros_like(dq_ref)

        def step(diag):
            rows = pl.ds(pl.multiple_of(i * tq, tq), tq)
            for u in range(hb):
                qv, kv, dov = q_ref[u], k_ref[u], dot_ref[u]
                pr = jnp.exp(_scores_t(kv, qv, scale=scale, diag=diag) - lse_ref[u])
                dp = jnp.dot(v_ref[u], dov, preferred_element_type=F32)
                delta = jnp.sum(dov.astype(F32) * ot_ref[u].astype(F32), axis=0, keepdims=True)
                dsb = ((pr * (dp - delta)) * scale).astype(BF16)
                d_v = lax.dot_general(dov, pr.astype(BF16), NT, preferred_element_type=F32)
                d_k = jnp.dot(dsb, qv, preferred_element_type=F32)
                if diag:
                    dvt_ref[u], dk_ref[u] = d_v, d_k
                else:
                    dvt_ref[u] += d_v
                    dk_ref[u] += d_k
                dq_ref[u, rows, :] += lax.dot_general(dsb, kv, TN, preferred_element_type=F32)

        pl.when(i > j)(functools.partial(step, False))
        pl.when(i == j)(functools.partial(step, True))

    def qi(t):
        return _tri_bwd(t, nq)[1]

    def kj(t):
        return _tri_bwd(t, nq)[0]

    rows_q = pl.BlockSpec((hb, tq, dq), lambda hp, t: (hp, qi(t), 0))
    rows_k = pl.BlockSpec((hb, tq, dq), lambda hp, t: (hp, kj(t), 0))
    lanes_q = pl.BlockSpec((hb, dv, tq), lambda hp, t: (hp, 0, qi(t)))
    return _call_with_rider(
        body, rider, name=name, grid=(nh // hb, nsteps),
        in_specs=[rows_q, rows_k, pl.BlockSpec((hb, tq, dv), lambda hp, t: (hp, kj(t), 0)), lanes_q, lanes_q,
                  pl.BlockSpec((hb, 1, tq), lambda hp, t: (hp, 0, qi(t)))],
        out_specs=[pl.BlockSpec((hb, s, dq), lambda hp, t: (hp, 0, 0)), rows_k,
                   pl.BlockSpec((hb, dv, tq), lambda hp, t: (hp, 0, kj(t)))],
        out_shape=[jax.ShapeDtypeStruct((nh, s, dq), F32), jax.ShapeDtypeStruct((nh, s, dq), F32),
                   jax.ShapeDtypeStruct((nh, dv, s), F32)],
        scratch_shapes=[], compiler_params=_cp("arbitrary", "arbitrary"), args=(q, k, v, ot, dot_, lse))


def _swa_scores_t(k, q, dist, ok, *, scale, slope):
    s = lax.dot_general(k, q, NT, preferred_element_type=F32) * scale - slope * dist.astype(F32)
    return jnp.where(ok, s, MASK_VALUE)


def _swa_geometry(tb, w, has_other):
    r = lax.broadcasted_iota(jnp.int32, (tb, tb), 0)
    c = lax.broadcasted_iota(jnp.int32, (tb, tb), 1)
    d_same = c - r
    ok_same = jnp.logical_and(d_same >= 0, d_same < w)

    def other(ncols):
        rr = lax.broadcasted_iota(jnp.int32, (w, ncols), 0)
        cc = lax.broadcasted_iota(jnp.int32, (w, ncols), 1)
        dd = cc + w - rr
        return dd, jnp.logical_and(dd < w, has_other)

    return (d_same, ok_same), other


def _swa_fwd_t(q, k, vt, slopes_sinks, *, scale, window, name, tb=256):
    nh, s, d = q.shape
    nkv = k.shape[0]
    grp = nh // nkv
    w = window
    per = tb // w
    assert tb % w == 0

    def body(q_ref, kc_ref, kp_ref, vc_ref, vp_ref, ss_ref, o_ref, lse_ref):
        kvh, i = pl.program_id(0), pl.program_id(1)
        (d_c, ok_c), other = _swa_geometry(tb, w, i > 0)
        d_p, ok_p = other(tb)
        for g in range(grp):
            h = kvh * grp + g
            slope, sink = ss_ref[0, h], ss_ref[1, h]
            qg = q_ref[g]
            s_c = _swa_scores_t(kc_ref[...], qg, d_c, ok_c, scale=scale, slope=slope)
            s_p = _swa_scores_t(kp_ref[...], qg, d_p, ok_p, scale=scale, slope=slope)
            m = jnp.maximum(jnp.maximum(jnp.max(s_c, axis=0, keepdims=True), jnp.max(s_p, axis=0, keepdims=True)), sink)
            p_c, p_p = jnp.exp(s_c - m), jnp.exp(s_p - m)
            l = jnp.sum(p_c, axis=0, keepdims=True) + jnp.sum(p_p, axis=0, keepdims=True) + jnp.exp(sink - m)
            acc = (jnp.dot(vc_ref[...], p_c.astype(BF16), preferred_element_type=F32)
                   + jnp.dot(vp_ref[...], p_p.astype(BF16), preferred_element_type=F32))
            o_ref[g] = (acc / l).astype(BF16)
            lse_ref[g] = m + jnp.log(l)

    def prev(i):
        return jnp.maximum(i * per - 1, 0)

    return pl.pallas_call(
        body, name=name, grid=(nkv, s // tb),
        in_specs=[pl.BlockSpec((grp, tb, d), lambda kh, i: (kh, i, 0)),
                  pl.BlockSpec((None, tb, d), lambda kh, i: (kh, i, 0)),
                  pl.BlockSpec((None, w, d), lambda kh, i: (kh, prev(i), 0)),
                  pl.BlockSpec((None, d, tb), lambda kh, i: (kh, 0, i)),
                  pl.BlockSpec((None, d, w), lambda kh, i: (kh, 0, prev(i))),
                  pl.BlockSpec(memory_space=pltpu.SMEM)],
        out_specs=[pl.BlockSpec((grp, d, tb), lambda kh, i: (kh, 0, i)), pl.BlockSpec((grp, 1, tb), lambda kh, i: (kh, 0, i))],
        out_shape=[jax.ShapeDtypeStruct((nh, d, s), BF16), jax.ShapeDtypeStruct((nh, 1, s), F32)],
        compiler_params=_cp("arbitrary", "arbitrary"))(q, k, k, vt, vt, slopes_sinks)


def _swa_bwd_t(q, k, v, ot, dot_, lse, slopes_sinks, *, scale, window, name, tb=256):
    nh, s, d = q.shape
    nkv = k.shape[0]
    grp = nh // nkv
    w = window
    per = tb // w
    nb = s // tb

    def body(qc_ref, qn_ref, kc_ref, kp_ref, vc_ref, vp_ref, oc_ref, on_ref, doc_ref, don_ref, lc_ref, ln_ref, ss_ref,
             dq_ref, dk_ref, dvt_ref, dsink_ref):
        kvh, i = pl.program_id(0), pl.program_id(1)

        @pl.when(i == 0)
        def _():
            dsink_ref[...] = jnp.zeros_like(dsink_ref)

        (d_c, ok_c), other = _swa_geometry(tb, w, i > 0)
        d_p, ok_p = other(tb)
        d_n, ok_n = _swa_geometry(tb, w, i < nb - 1)[1](w)
        kc, kp, vc, vp = kc_ref[...], kp_ref[...], vc_ref[...], vp_ref[...]
        k_last, v_last = kc[tb - w:, :], vc[tb - w:, :]
        dk_acc = jnp.zeros((tb, d), F32)
        dv_acc = jnp.zeros((d, tb), F32)
        dk_tail = jnp.zeros((w, d), F32)
        dv_tail = jnp.zeros((d, w), F32)
        for g in range(grp):
            h = kvh * grp + g
            slope, sink = ss_ref[0, h], ss_ref[1, h]
            qg, dog, lse_c = qc_ref[g], doc_ref[g], lc_ref[g]
            delta = jnp.sum(dog.astype(F32) * oc_ref[g].astype(F32), axis=0, keepdims=True)
            p_c = jnp.exp(_swa_scores_t(kc, qg, d_c, ok_c, scale=scale, slope=slope) - lse_c)
            p_p = jnp.exp(_swa_scores_t(kp, qg, d_p, ok_p, scale=scale, slope=slope) - lse_c)
            ds_c = ((p_c * (jnp.dot(vc, dog, preferred_element_type=F32) - delta)) * scale).astype(BF16)
            ds_p = ((p_p * (jnp.dot(vp, dog, preferred_element_type=F32) - delta)) * scale).astype(BF16)
            dq_ref[g] = (lax.dot_general(ds_c, kc, TN, preferred_element_type=F32)
                         + lax.dot_general(ds_p, kp, TN, preferred_element_type=F32))
            dk_acc += jnp.dot(ds_c, qg, preferred_element_type=F32)
            dv_acc += lax.dot_general(dog, p_c.astype(BF16), NT, preferred_element_type=F32)
            dsink_ref[g] -= jnp.broadcast_to(jnp.sum(jnp.exp(sink - lse_c) * delta, axis=1, keepdims=True), (1, LANES))
            qn, don = qn_ref[g], don_ref[g]
            delta_n = jnp.sum(don.astype(F32) * on_ref[g].astype(F32), axis=0, keepdims=True)
            p_n = jnp.exp(_swa_scores_t(k_last, qn, d_n, ok_n, scale=scale, slope=slope) - ln_ref[g])
            ds_n = ((p_n * (jnp.dot(v_last, don, preferred_element_type=F32) - delta_n)) * scale).astype(BF16)
            dk_tail += jnp.dot(ds_n, qn, preferred_element_type=F32)
            dv_tail += lax.dot_general(don, p_n.astype(BF16), NT, preferred_element_type=F32)
        dk_ref[...] = dk_acc
        dvt_ref[...] = dv_acc
        dk_ref[tb - w:, :] += dk_tail
        dvt_ref[:, tb - w:] += dv_tail

    def prev(i):
        return jnp.maximum(i * per - 1, 0)

    def nxt(i):
        return jnp.minimum((i + 1) * per, s // w - 1)

    return pl.pallas_call(
        body, name=name, grid=(nkv, nb),
        in_specs=[pl.BlockSpec((grp, tb, d), lambda kh, i: (kh, i, 0)),
                  pl.BlockSpec((grp, w, d), lambda kh, i: (kh, nxt(i), 0)),
                  pl.BlockSpec((None, tb, d), lambda kh, i: (kh, i, 0)),
                  pl.BlockSpec((None, w, d), lambda kh, i: (kh, prev(i), 0)),
                  pl.BlockSpec((None, tb, d), lambda kh, i: (kh, i, 0)),
                  pl.BlockSpec((None, w, d), lambda kh, i: (kh, prev(i), 0)),
                  pl.BlockSpec((grp, d, tb), lambda kh, i: (kh, 0, i)),
                  pl.BlockSpec((grp, d, w), lambda kh, i: (kh, 0, nxt(i))),
                  pl.BlockSpec((grp, d, tb), lambda kh, i: (kh, 0, i)),
                  pl.BlockSpec((grp, d, w), lambda kh, i: (kh, 0, nxt(i))),
                  pl.BlockSpec((grp, 1, tb), lambda kh, i: (kh, 0, i)),
                  pl.BlockSpec((grp, 1, w), lambda kh, i: (kh, 0, nxt(i))),
                  pl.BlockSpec(memory_space=pltpu.SMEM)],
        out_specs=[pl.BlockSpec((grp, tb, d), lambda kh, i: (kh, i, 0)),
                   pl.BlockSpec((None, tb, d), lambda kh, i: (kh, i, 0)),
                   pl.BlockSpec((None, d, tb), lambda kh, i: (kh, 0, i)),
                   pl.BlockSpec((None, grp, 1, LANES), lambda kh, i: (kh, 0, 0, 0))],
        out_shape=[jax.ShapeDtypeStruct((nh, s, d), F32), jax.ShapeDtypeStruct((nkv, s, d), F32),
                   jax.ShapeDtypeStruct((nkv, d, s), F32), jax.ShapeDtypeStruct((nkv, grp, 1, LANES), F32)],
        compiler_params=_cp("arbitrary", "arbitrary"))(q, q, k, k, v, v, ot, ot, dot_, dot_, lse, lse, slopes_sinks)


def _adamw(w, g, m, v, *, name):
    shape = w.shape
    cols = shape[-1]
    rows = int(np.prod(shape[:-1])) if len(shape) > 1 else 1
    tr = _row_tile(rows, cols)
    c1 = 1.0 - ADAM_B1 ** ADAM_STEP
    c2 = 1.0 - ADAM_B2 ** ADAM_STEP

    def body(w_ref, g_ref, m_ref, v_ref, d_ref, mo_ref, vo_ref):
        gv = g_ref[...]
        mn = ADAM_B1 * m_ref[...] + (1.0 - ADAM_B1) * gv
        vn = ADAM_B2 * v_ref[...] + (1.0 - ADAM_B2) * (gv * gv)
        mo_ref[...] = mn
        vo_ref[...] = vn
        d_ref[...] = -ADAM_LR * ((mn / c1) / (jnp.sqrt(vn / c2) + ADAM_EPS) + ADAM_WD * w_ref[...])

    blk = pl.BlockSpec((tr, cols), lambda i: (i, 0))
    outs = pl.pallas_call(
        body, name=name, grid=(rows // tr,), in_specs=[blk] * 4, out_specs=[blk] * 3,
        out_shape=[jax.ShapeDtypeStruct((rows, cols), F32)] * 3,
        compiler_params=_cp("arbitrary"))(*[a.reshape(rows, cols) for a in (w, g, m, v)])
    return tuple(a.reshape(shape) for a in outs)


def _hbm_spec():
    return pl.BlockSpec(memory_space=pl.ANY)


def _mesh_place():
    x, y, c = lax.axis_index("x"), lax.axis_index("y"), lax.axis_index("c")
    return x, y, c, [(1 - x, y), (x, 1 - y), (1 - x, 1 - y)]


def _half_rows(c, rows, align):
    return pl.ds(pl.multiple_of(c * (rows // 2), align), rows // 2)


def _part(ref, mode, k, n, rows=None):
    if mode == "cols":
        cols = pl.ds(pl.multiple_of(k * n, LANES), n)
        return ref.at[:, cols] if rows is None else ref.at[rows, cols]
    return ref.at[k] if rows is None else ref.at[k, rows, :]


class _Rider:
    def __init__(self, inputs, out_shape, n_sems, start, finish):
        self.inputs, self.out_shape, self.n_sems, self.start, self.finish = inputs, out_shape, n_sems, start, finish


def _call_with_rider(body, rider, *, name, grid, in_specs, out_specs, out_shape, scratch_shapes, compiler_params, args):
    if rider is None:
        outs = pl.pallas_call(body, name=name, grid=grid, in_specs=in_specs, out_specs=out_specs, out_shape=out_shape,
                              scratch_shapes=scratch_shapes, compiler_params=compiler_params)(*args)
        return outs, []
    n_in, n_out, n_sc = len(in_specs), len(out_specs), len(scratch_shapes)
    n_rin, n_rout = len(rider.inputs), len(rider.out_shape)

    def wrapped(*refs):
        pos = 0
        groups = []
        for n in (n_in, n_rin, n_out, n_rout, n_sc, 2):
            groups.append(refs[pos:pos + n])
            pos += n
        ins, rins, outs, routs, scratch, sems = groups
        ids = [pl.program_id(a) for a in range(len(grid))]
        first = functools.reduce(jnp.logical_and, [i == 0 for i in ids])
        last = functools.reduce(jnp.logical_and, [i == g - 1 for i, g in zip(ids, grid)])
        pl.when(first)(lambda: rider.start(rins, routs, *sems))
        body(*ins, *outs, *scratch)
        pl.when(last)(lambda: rider.finish(rins, routs, *sems))

    outs = pl.pallas_call(
        wrapped, name=name, grid=grid, in_specs=list(in_specs) + [_hbm_spec()] * n_rin,
        out_specs=list(out_specs) + [_hbm_spec()] * n_rout, out_shape=list(out_shape) + list(rider.out_shape),
        scratch_shapes=list(scratch_shapes) + [pltpu.SemaphoreType.DMA((rider.n_sems,))] * 2,
        compiler_params=compiler_params)(*args, *rider.inputs)
    return outs[:n_out], outs[n_out:]


def _run_rider(rider, *, name):
    n_rin = len(rider.inputs)

    def body(*refs):
        rins, routs, sems = refs[:n_rin], refs[n_rin:-2], refs[-2:]
        rider.start(rins, routs, *sems)
        rider.finish(rins, routs, *sems)

    return pl.pallas_call(
        body, name=name, in_specs=[_hbm_spec()] * n_rin, out_specs=[_hbm_spec()] * len(rider.out_shape),
        out_shape=rider.out_shape, scratch_shapes=[pltpu.SemaphoreType.DMA((rider.n_sems,))] * 2)(*rider.inputs)


def _gather_rider(shards, modes):
    n_arr = len(shards)
    out_shape = [jax.ShapeDtypeStruct((s.shape[0], N_CHIPS * s.shape[1]) if m == "cols" else (N_CHIPS,) + s.shape, s.dtype)
                 for s, m in zip(shards, modes)]
    per = 4

    def copies(srcs, dsts, send_sems, recv_sems):
        x, y, c, chips = _mesh_place()
        me = 2 * x + y
        sends, waits = [], []
        for i in range(n_arr):
            r, n = shards[i].shape
            rows = _half_rows(c, r, 16)

            def copy(slot, src, dst, to, i=i):
                return pltpu.make_async_remote_copy(src_ref=src, dst_ref=dst, send_sem=send_sems.at[i * per + slot],
                                                    recv_sem=recv_sems.at[i * per + slot], device_id=to, device_id_type=MESH)

            own = _part(dsts[i], modes[i], me, n)
            sends.append(copy(0, srcs[i], own, (x, y, 1 - c)))
            waits.append(copy(0, own, own, (x, y, 1 - c)))
            for j, (px, py) in enumerate(chips):
                sends.append(copy(1 + j, srcs[i].at[rows], _part(dsts[i], modes[i], me, n, rows), (px, py, c)))
                theirs = _part(dsts[i], modes[i], 2 * px + py, n, rows)
                waits.append(copy(1 + j, theirs, theirs, (px, py, c)))
        return sends, waits

    def start(*refs):
        for cp in copies(*refs)[0]:
            cp.start()

    def finish(*refs):
        sends, waits = copies(*refs)
        for cp in waits:
            cp.wait_recv()
        for cp in sends:
            cp.wait_send()

    return _Rider(list(shards), out_shape, per * n_arr, start, finish)


def _gather_forward(dsts, shard_shapes, modes, *, name):
    n_arr = len(dsts)

    def body(*refs):
        outs = refs[n_arr:2 * n_arr]
        send_sems, recv_sems = refs[2 * n_arr:]
        x, y, c, chips = _mesh_place()
        cps = []
        for i in range(n_arr):
            r, n = shard_shapes[i]
            for j, (px, py) in enumerate(chips):
                def view(hc, i=i, px=px, py=py, r=r, n=n):
                    return _part(outs[i], modes[i], 2 * px + py, n, _half_rows(hc, r, 16))

                def copy(ref, i=i, j=j):
                    return pltpu.make_async_remote_copy(src_ref=ref, dst_ref=ref, send_sem=send_sems.at[3 * i + j],
                                                        recv_sem=recv_sems.at[3 * i + j], device_id=(x, y, 1 - c), device_id_type=MESH)

                cps.append((copy(view(c)), copy(view(1 - c))))
        for send, _ in cps:
            send.start()
        for send, theirs in cps:
            theirs.wait_recv()
            send.wait_send()

    return pl.pallas_call(
        body, name=name, in_specs=[_hbm_spec()] * n_arr, out_specs=[_hbm_spec()] * n_arr,
        out_shape=[jax.ShapeDtypeStruct(d.shape, d.dtype) for d in dsts],
        input_output_aliases={i: i for i in range(n_arr)},
        scratch_shapes=[pltpu.SemaphoreType.DMA((3 * n_arr,)), pltpu.SemaphoreType.DMA((3 * n_arr,))])(*dsts)


def _blk_view(a, mode):
    return a[None] if mode == "cols" else a


def _rs_pair_swap(arrs, modes, *, name):
    n_arr = len(arrs)
    out_shape = [jax.ShapeDtypeStruct((a.shape[0] // 2, a.shape[1]) if m == "cols" else (a.shape[0], a.shape[1] // 2, a.shape[2]), a.dtype)
                 for a, m in zip(arrs, modes)]

    def body(*refs):
        srcs, dsts = refs[:n_arr], refs[n_arr:2 * n_arr]
        send_sems, recv_sems = refs[2 * n_arr:]
        x, y, c, _ = _mesh_place()
        cps = []
        for i in range(n_arr):
            if modes[i] == "cols":
                src = srcs[i].at[_half_rows(1 - c, arrs[i].shape[0], 8)]
            else:
                src = srcs[i].at[:, _half_rows(1 - c, arrs[i].shape[1], 8), :]
            cps.append(pltpu.make_async_remote_copy(src_ref=src, dst_ref=dsts[i], send_sem=send_sems.at[i],
                                                    recv_sem=recv_sems.at[i], device_id=(x, y, 1 - c), device_id_type=MESH))
        for cp in cps:
            cp.start()
        for cp in cps:
            cp.wait()

    return pl.pallas_call(
        body, name=name, in_specs=[_hbm_spec()] * n_arr, out_specs=[_hbm_spec()] * n_arr, out_shape=out_shape,
        scratch_shapes=[pltpu.SemaphoreType.DMA((n_arr,)), pltpu.SemaphoreType.DMA((n_arr,))])(*arrs)


def _rs_pair_add(arr, landed, place, *, name):
    nb, r, c = arr.shape
    rh = r // 2
    tr = _row_tile(rh, c)
    nt = rh // tr

    def body(p_ref, a_ref, l_ref, o_ref):
        o_ref[...] = (a_ref[...] + l_ref[...]).astype(BF16)

    grid_spec = pltpu.PrefetchScalarGridSpec(
        num_scalar_prefetch=1, grid=(nb, nt),
        in_specs=[pl.BlockSpec((None, tr, c), lambda b, t, p_ref: (b, p_ref[1] * nt + t, 0)),
                  pl.BlockSpec((None, tr, c), lambda b, t, p_ref: (b, t, 0))],
        out_specs=pl.BlockSpec((None, tr, c), lambda b, t, p_ref: (b, t, 0)))
    return pl.pallas_call(
        body, name=name, grid_spec=grid_spec, out_shape=jax.ShapeDtypeStruct((nb, rh, c), BF16),
        compiler_params=_cp("arbitrary", "arbitrary"))(place, arr, landed)


def _exchange_rider(parts, modes):
    n_arr = len(parts)
    out_shape = []
    for a, m in zip(parts, modes):
        shp = (a.shape[0], a.shape[1] // N_CHIPS) if m == "cols" else a.shape[1:]
        out_shape.append(jax.ShapeDtypeStruct((3,) + shp, a.dtype))

    def copies(srcs, dsts, send_sems, recv_sems):
        x, y, c, chips = _mesh_place()
        cps = []
        for i in range(n_arr):
            n = out_shape[i].shape[-1]
            for j, (px, py) in enumerate(chips):
                cps.append(pltpu.make_async_remote_copy(
                    src_ref=_part(srcs[i], modes[i], 2 * px + py, n), dst_ref=dsts[i].at[j],
                    send_sem=send_sems.at[3 * i + j], recv_sem=recv_sems.at[3 * i + j],
                    device_id=(px, py, c), device_id_type=MESH))
        return cps

    def start(*refs):
        for cp in copies(*refs):
            cp.start()

    def finish(*refs):
        for cp in copies(*refs):
            cp.wait()

    return _Rider(list(parts), out_shape, 3 * n_arr, start, finish)


def _rs_chip_sum(part, landed, mode, place, *, name):
    _, rh, n = landed.shape
    tr = _row_tile(rh, n)
    nt = rh // tr

    def body(p_ref, a_ref, l_ref, o_ref):
        o_ref[...] = ((a_ref[...].astype(F32) + l_ref[0].astype(F32)) + l_ref[1].astype(F32)) + l_ref[2].astype(F32)

    if mode == "cols":
        own = pl.BlockSpec((tr, n), lambda t, p_ref: (t, p_ref[0]))
    else:
        own = pl.BlockSpec((None, tr, n), lambda t, p_ref: (p_ref[0], t, 0))
    grid_spec = pltpu.PrefetchScalarGridSpec(
        num_scalar_prefetch=1, grid=(nt,),
        in_specs=[own, pl.BlockSpec((3, tr, n), lambda t, p_ref: (0, t, 0))],
        out_specs=pl.BlockSpec((tr, n), lambda t, p_ref: (p_ref[1] * nt + t, 0)))
    return pl.pallas_call(
        body, name=name, grid_spec=grid_spec, out_shape=jax.ShapeDtypeStruct((2 * rh, n), F32),
        compiler_params=_cp("arbitrary"))(place, part, landed)


def _rs_pair_join(halves, *, name):
    n_arr = len(halves)

    def body(*refs):
        outs = refs[n_arr:2 * n_arr]
        send_sems, recv_sems = refs[2 * n_arr:]
        x, y, c, _ = _mesh_place()
        cps = []
        for i in range(n_arr):
            rows = _half_rows(c, halves[i].shape[0], 8)
            cps.append(pltpu.make_async_remote_copy(src_ref=outs[i].at[rows], dst_ref=outs[i].at[rows], send_sem=send_sems.at[i],
                                                    recv_sem=recv_sems.at[i], device_id=(x, y, 1 - c), device_id_type=MESH))
        for cp in cps:
            cp.start()
        for i, cp in enumerate(cps):
            cp.wait_send()
            theirs = outs[i].at[_half_rows(1 - c, halves[i].shape[0], 8)]
            pltpu.make_async_remote_copy(src_ref=theirs, dst_ref=theirs, send_sem=send_sems.at[i], recv_sem=recv_sems.at[i],
                                         device_id=(x, y, 1 - c), device_id_type=MESH).wait_recv()

    return pl.pallas_call(
        body, name=name, in_specs=[_hbm_spec()] * n_arr, out_specs=[_hbm_spec()] * n_arr,
        out_shape=[jax.ShapeDtypeStruct(h.shape, h.dtype) for h in halves],
        input_output_aliases={i: i for i in range(n_arr)},
        scratch_shapes=[pltpu.SemaphoreType.DMA((n_arr,)), pltpu.SemaphoreType.DMA((n_arr,))])(*halves)


def _allreduce_small(v, *, name):
    r, c = v.shape

    def body(v_ref, o_ref, gath, send_sems, recv_sems):
        x, y, cc, _ = _mesh_place()
        me = 4 * x + 2 * y + cc
        gath[me] = v_ref[...]
        cps = []
        for rel in range(1, 8):
            px = 1 - x if rel & 4 else x
            py = 1 - y if rel & 2 else y
            pc = 1 - cc if rel & 1 else cc

            def copy(slot, px=px, py=py, pc=pc, rel=rel):
                return pltpu.make_async_remote_copy(
                    src_ref=v_ref, dst_ref=gath.at[slot], send_sem=send_sems.at[rel - 1],
                    recv_sem=recv_sems.at[rel - 1], device_id=(px, py, pc), device_id_type=MESH)

            cps.append((copy(me), copy(4 * px + 2 * py + pc)))
        for send, _ in cps:
            send.start()
        for send, theirs in cps:
            theirs.wait_recv()
            send.wait_send()
        tot = gath[0]
        for d in range(1, 8):
            tot = tot + gath[d]
        o_ref[...] = tot

    vm = pl.BlockSpec(memory_space=pltpu.VMEM)
    return pl.pallas_call(
        body, name=name, in_specs=[vm], out_specs=vm, out_shape=jax.ShapeDtypeStruct((r, c), F32),
        scratch_shapes=[pltpu.VMEM((8, r, c), F32), pltpu.SemaphoreType.DMA((7,)), pltpu.SemaphoreType.DMA((7,))])(v)


def _to_heads(a, nh, dh, dtype=BF16):
    return a.reshape(a.shape[0], nh, dh).transpose(1, 0, 2).astype(dtype)


def _from_heads(a):
    return a.transpose(1, 0, 2).reshape(a.shape[1], -1)


def _rope_tables(s, reps):
    half = B_ROPE // 2
    inv = ROPE_THETA ** (-jnp.arange(0, B_ROPE, 2, dtype=F32) / B_ROPE)
    ang = jnp.arange(s, dtype=F32)[:, None] * inv[None, :]
    return jnp.tile(jnp.cos(ang), (1, reps)), jnp.tile(jnp.sin(ang), (1, reps))


def _alibi_slopes():
    return 2.0 ** (-8.0 * jnp.arange(1, A_HEADS + 1, dtype=F32) / A_HEADS)


def _ffn_fwd(h, norm, wts, tag, rider=None):
    (gate, up, act, xn), rode = _ffn_up(h, norm, wts["wgu"], name=f"{tag}_up", rider=rider)
    out = _mm_res_fwd(act, wts["wd"], h, scale=FFN_RES_SCALE, name=f"{tag}_down")
    return out, dict(h_in=h, gate=gate, up=up, act=act, xn=xn), rode


def _ffn_bwd(dh, norm, wts, sv, tag, rider=None):
    (dgate, dup), rode = _ffn_down_bwd(dh, wts["wd"], sv["gate"], sv["up"], scale=FFN_RES_SCALE,
                                      name=f"{tag}_down_bwd", rider=rider)
    d_wd = _mm_tn(sv["act"], dh, b_scale=FFN_RES_SCALE, name=f"{tag}_dwd")
    d_wgu = _mm_tn(sv["xn"], [dgate, dup], name=f"{tag}_dwgu")
    dh_in, dnorm = _mm_nt_rmsbwd([(dgate, wts["wgu"], 0), (dup, wts["wgu"], 1)], sv["h_in"], norm, dh,
                                 name=f"{tag}_dx")
    return dh_in, dnorm, d_wgu, d_wd, rode


def _even_weights(w_in, w_uq, w_ukv):
    half = B_ROPE // 2
    base = w_in.shape[1]
    kr1, kr2 = w_in[:, base - B_ROPE:base - half], w_in[:, base - half:]
    w_in_cat = jnp.concatenate([w_in, -kr2, kr1, jnp.zeros((w_in.shape[0], 64), w_in.dtype)], axis=1)
    u3 = w_uq.reshape(w_uq.shape[0], B_HEADS, B_NOPE + B_ROPE)
    nope = u3[:, :, :B_NOPE].reshape(w_uq.shape[0], -1)
    r1 = u3[:, :, B_NOPE:B_NOPE + half].reshape(w_uq.shape[0], -1)
    r2 = u3[:, :, B_NOPE + half:].reshape(w_uq.shape[0], -1)
    w_q_cat = jnp.concatenate([nope, r1, r2, -r2, r1], axis=1)
    return w_in_cat, w_q_cat, w_ukv


def _even_fwd(h, w, i, rider=None):
    s = h.shape[0]
    half = B_ROPE // 2
    ycat, xn = _rms_mm_fwd(h, w["mix_norm"][i:i + 1], w["ev_in_cat"], name="ev_in")
    a_q, a_k, a_v = ycat[:, :512], ycat[:, 512:640], ycat[:, 640:768]
    c_q, c_kv = ycat[:, 768:1024], ycat[:, 1024:1152]
    cos32, sin32 = _rope_tables(s, 2)
    kro = _rope_fwd(ycat[:, 1152:1184], ycat[:, 1184:1216], cos32, sin32, name="ev_k_rope")
    qa, ka, va = _to_heads(a_q, A_HEADS, A_HEAD_DIM), _to_heads(a_k, A_KV_HEADS, A_HEAD_DIM), _to_heads(a_v, A_KV_HEADS, A_HEAD_DIM)
    ss = jnp.stack([_alibi_slopes(), w["ev_sinks"].reshape(-1)])
    oa, lse_a = _swa_fwd_t(qa, ka, va.transpose(0, 2, 1), ss, scale=A_HEAD_DIM ** -0.5, window=WINDOW, name="swa_fwd")
    yq, xn_q = _rms_mm_fwd(c_q, w["ev_cq_norm"], w["ev_q_cat"], name="ev_q_up")
    cos256, sin256 = _rope_tables(s, 2 * B_HEADS)
    qro = _rope_fwd(yq[:, 512:768], yq[:, 768:1024], cos256, sin256, name="ev_q_rope")
    ykv, xn_kv = _rms_mm_fwd(c_kv, w["ev_ckv_norm"], w["ev_ukv"], name="ev_kv_up")
    zq = jnp.zeros((s, B_HEADS, LANES - B_NOPE - B_ROPE), F32)
    qb = jnp.concatenate([yq[:, :512].reshape(s, B_HEADS, B_NOPE), qro[:, :128].reshape(s, B_HEADS, half),
                          qro[:, 128:].reshape(s, B_HEADS, half), zq], axis=-1).transpose(1, 0, 2).astype(BF16)
    kv3 = ykv.reshape(s, B_HEADS, B_NOPE + B_V)
    kb = jnp.concatenate([kv3[:, :, :B_NOPE], jnp.broadcast_to(kro[:, None, :], (s, B_HEADS, B_ROPE)), zq],
                         axis=-1).transpose(1, 0, 2).astype(BF16)
    vb = kv3[:, :, B_NOPE:].transpose(1, 0, 2).astype(BF16)
    (ob, lse_b), rode = _causal_fwd_t(qb, kb, vb.transpose(0, 2, 1), scale=(B_NOPE + B_ROPE) ** -0.5, name="mla_fwd",
                                      tq=512, rider=rider)
    attn = jnp.concatenate([oa.transpose(2, 0, 1).reshape(s, -1), ob.transpose(2, 0, 1).reshape(s, -1)], axis=-1)
    out = _mm_res_fwd(attn, w["ev_out"], h, scale=1.0, name="ev_out")
    sv = dict(h_in=h, xn=xn, c_q=c_q, c_kv=c_kv, xn_q=xn_q, xn_kv=xn_kv, qa=qa, ka=ka, va=va, oa=oa, lse_a=lse_a,
              ss=ss, qb=qb, kb=kb, vb=vb, ob=ob, lse_b=lse_b, attn=attn, cos32=cos32, sin32=sin32,
              cos256=cos256, sin256=sin256)
    return out, sv, rode


def _even_bwd(dh, w, sv, i, rider=None):
    s = dh.shape[0]
    half = B_ROPE // 2
    g = {}
    dattn = _mm_nt(dh, w["ev_out"], name="ev_out_dx")
    g["ev_w_out"] = _mm_tn(sv["attn"], dh, name="ev_out_dw")
    doa = dattn[:, :512].reshape(s, A_HEADS, A_HEAD_DIM).transpose(1, 2, 0).astype(BF16)
    dob = dattn[:, 512:].reshape(s, B_HEADS, B_V).transpose(1, 2, 0).astype(BF16)
    dqa, dka, dva, dsink = _swa_bwd_t(sv["qa"], sv["ka"], sv["va"], sv["oa"], doa, sv["lse_a"], sv["ss"],
                                      scale=A_HEAD_DIM ** -0.5, window=WINDOW, name="swa_bwd")
    g["ev_sinks"] = dsink[:, :, 0, 0].reshape(1, A_HEADS)
    (dqb, dkb, dvb), rode = _causal_bwd_t(sv["qb"], sv["kb"], sv["vb"], sv["ob"], dob, sv["lse_b"],
                                          scale=(B_NOPE + B_ROPE) ** -0.5, name="mla_bwd", tq=512, rider=rider)
    dq_r1 = dqb[:, :, B_NOPE:B_NOPE + half].transpose(1, 0, 2).reshape(s, -1)
    dq_r2 = dqb[:, :, B_NOPE + half:B_NOPE + B_ROPE].transpose(1, 0, 2).reshape(s, -1)
    dq1, dq2 = _rope_bwd(jnp.concatenate([dq_r1, dq_r2], axis=-1)[None], sv["cos256"], sv["sin256"], name="ev_q_rope_bwd")
    dyq = jnp.concatenate([_from_heads(dqb[:, :, :B_NOPE]), dq1, dq2], axis=-1)
    dwq = _mm_tn(sv["xn_q"], dyq, name="ev_q_up_dw")
    dcq, g["ev_cq_norm"] = _mm_nt_rmsbwd([(dyq, w["ev_q_cat"])], sv["c_q"], w["ev_cq_norm"], None, name="ev_q_up_dx")
    kq = sv["c_q"].shape[1]
    d_nope = dwq[:, :512].reshape(kq, B_HEADS, B_NOPE)
    d_r1 = (dwq[:, 512:640] + dwq[:, 896:1024]).reshape(kq, B_HEADS, half)
    d_r2 = (dwq[:, 640:768] - dwq[:, 768:896]).reshape(kq, B_HEADS, half)
    g["ev_w_uq"] = jnp.concatenate([d_nope, d_r1, d_r2], axis=-1).reshape(kq, -1)
    dykv = jnp.concatenate([dkb[:, :, :B_NOPE].transpose(1, 0, 2), dvb.transpose(2, 0, 1)], axis=-1).reshape(s, -1)
    g["ev_w_ukv"] = _mm_tn(sv["xn_kv"], dykv, name="ev_kv_up_dw")
    dckv, g["ev_ckv_norm"] = _mm_nt_rmsbwd([(dykv, w["ev_ukv"])], sv["c_kv"], w["ev_ckv_norm"], None, name="ev_kv_up_dx")
    dk1, dk2 = _rope_bwd(dkb[:, :, B_NOPE:B_NOPE + B_ROPE], sv["cos32"], sv["sin32"], name="ev_k_rope_bwd")
    dycat = jnp.concatenate([_from_heads(dqa), _from_heads(dka), dva.transpose(2, 0, 1).reshape(s, -1),
                             dcq, dckv, dk1, dk2, jnp.zeros((s, 64), F32)], axis=-1)
    dwin = _mm_tn(sv["xn"], dycat, name="ev_in_dw")
    base = 1184
    g["ev_w_in"] = jnp.concatenate([dwin[:, :base - B_ROPE],
                                    dwin[:, base - B_ROPE:base - half] + dwin[:, base + half:base + B_ROPE],
                                    dwin[:, base - half:base] - dwin[:, base:base + half]], axis=-1)
    dh_in, dnorm = _mm_nt_rmsbwd([(dycat, w["ev_in_cat"])], sv["h_in"], w["mix_norm"][i:i + 1], dh, name="ev_in_dx")
    return dh_in, dnorm, g, rode


def _odd_fwd(h, w, i, rider=None):
    s = h.shape[0]
    wd = C_HEADS * C_HEAD_DIM
    y, xn = _rms_mm_fwd(h, w["mix_norm"][i:i + 1], w["od_in_pad"], name="od_in")
    scale = C_HEAD_DIM ** -0.5
    ft = y[:, 3 * wd:3 * wd + C_HEADS].T
    bf = w["od_b_f"].reshape(C_HEADS, 1)
    cb3 = _fox_gate_fwd(ft, bf, out_scale=-1.0 / scale, name="fox_gate_fwd")
    ones, zeros = jnp.ones((s, C_HEADS, 1), BF16), jnp.zeros((s, C_HEADS, 1), BF16)
    tail = jnp.zeros((s, C_HEADS, LANES - C_HEAD_DIM - 5), BF16)
    q3 = y[:, :wd].reshape(s, C_HEADS, C_HEAD_DIM).astype(BF16)
    k3 = y[:, wd:2 * wd].reshape(s, C_HEADS, C_HEAD_DIM).astype(BF16)
    q = jnp.concatenate([q3, ones, zeros, ones, ones, ones, tail], axis=-1).transpose(1, 0, 2)
    k = jnp.concatenate([k3, zeros, ones, cb3.transpose(2, 1, 0), tail], axis=-1).transpose(1, 0, 2)
    v = _to_heads(y[:, 2 * wd:3 * wd], C_HEADS, C_HEAD_DIM)
    (o, lse), rode = _causal_fwd_t(q, k, v.transpose(0, 2, 1), scale=scale, name="fox_fwd", tq=512, rider=rider)
    attn = o.transpose(2, 0, 1).reshape(s, -1)
    out = _mm_res_fwd(attn, w["od_out"], h, scale=1.0, name="od_out")
    return out, dict(h_in=h, xn=xn, q=q, k=k, v=v, o=o, lse=lse, ft=ft, bf=bf, attn=attn), rode


def _odd_bwd(dh, w, sv, i, rider=None):
    s = dh.shape[0]
    g = {}
    dattn = _mm_nt(dh, w["od_out"], name="od_out_dx")
    g["od_w_out"] = _mm_tn(sv["attn"], dh, name="od_out_dw")
    do = dattn.reshape(s, C_HEADS, C_HEAD_DIM).transpose(1, 2, 0).astype(BF16)
    scale = C_HEAD_DIM ** -0.5
    (dq, dk, dv), rode = _causal_bwd_t(sv["q"], sv["k"], sv["v"], sv["o"], do, sv["lse"], scale=scale, name="fox_bwd",
                                       tq=512, rider=rider)
    dft, dbf = _fox_gate_bwd(dq[:, :, C_HEAD_DIM + 1], dk[:, :, C_HEAD_DIM], sv["ft"], sv["bf"],
                             inv_scale=1.0 / scale, name="fox_gate_bwd")
    dq, dk = dq[:, :, :C_HEAD_DIM], dk[:, :, :C_HEAD_DIM]
    g["od_b_f"] = dbf.reshape(1, C_HEADS)
    n_pad = w["od_in_pad"].shape[1]
    n_real = 3 * C_HEADS * C_HEAD_DIM + C_HEADS
    dy = jnp.concatenate([_from_heads(dq), _from_heads(dk), dv.transpose(2, 0, 1).reshape(s, -1), dft.T,
                          jnp.zeros((s, n_pad - n_real), F32)], axis=-1)
    g["od_w_in"] = _mm_tn(sv["xn"], dy, name="od_in_dw")[:, :n_real]
    dh_in, dnorm = _mm_nt_rmsbwd([(dy, w["od_in_pad"])], sv["h_in"], w["mix_norm"][i:i + 1], dh, name="od_in_dx")
    return dh_in, dnorm, g, rode


def _kernel_weights(full, replicated):
    w = dict(replicated)
    _install_weights(w, {(n, i): a for n, per_layer in full.items() for i, a in enumerate(per_layer)})
    return w


def _install_weights(w, got):
    raw = w.setdefault("raw", {})
    raw.update(got)
    for (n, i), a in got.items():
        if n in ("ffa_w_gate_up", "ffa_w_down", "ffb_w_gate_up", "ffb_w_down"):
            w.setdefault(n[:3], {}).setdefault(i, {})["wgu" if n.endswith("gate_up") else "wd"] = a
        elif n in ("ple_w_gate", "ple_w_proj"):
            w.setdefault("ple_gate" if n.endswith("gate") else "ple_proj", {})[i] = a
    if "ev_in_cat" not in w and all((n, 0) in raw for n in ("ev_w_in", "ev_w_uq", "ev_w_ukv", "ev_w_out")):
        w["ev_in_cat"], w["ev_q_cat"], w["ev_ukv"] = _even_weights(raw["ev_w_in", 0], raw["ev_w_uq", 0], raw["ev_w_ukv", 0])
        w["ev_out"] = raw["ev_w_out", 0]
    if "od_in_pad" not in w and all((n, 0) in raw for n in ("od_w_in", "od_w_out")):
        od_in = raw["od_w_in", 0]
        w["od_in_pad"] = jnp.pad(od_in, ((0, 0), (0, (-od_in.shape[1]) % LANES)))
        w["od_out"] = raw["od_w_out", 0]


def _keys(names, layer):
    return tuple((n, layer) for n in names)


_FFA, _FFB, _PLE = ("ffa_w_gate_up", "ffa_w_down"), ("ffb_w_gate_up", "ffb_w_down"), ("ple_w_gate", "ple_w_proj")
_EV, _OD = ("ev_w_in", "ev_w_uq", "ev_w_ukv", "ev_w_out"), ("od_w_in", "od_w_out")
_GATHER_FIRST = _keys(_FFA, 0)
_GATHER_RIDES = {("ffa", 0): _keys(_EV, 0), ("mix", 0): _keys(_FFB + _PLE, 0) + _keys(_FFA, 1),
                 ("ffb", 0): _keys(_OD, 0), ("mix", 1): _keys(_FFB + _PLE, 1)}
_REDUCE_RIDES = {("mix", 1): _keys(_FFB + _PLE, 1), ("mix", 0): _keys(_FFA, 1) + _keys(_OD, 0) + _keys(_FFB + _PLE, 0),
                 ("ffa", 0): _keys(_EV, 0)}
_REDUCE_LAST = _keys(_FFA, 0)


def _local_step(x, p, tgt, w, ex=None):
    depth = p.shape[0]

    def gather_behind(host, fn, *args):
        keys = None if ex is None else _GATHER_RIDES.get(host)
        res = fn(*args, None if keys is None else ex.gather_rider(keys))
        if keys is not None:
            _install_weights(w, ex.gather_finish(keys, res[-1], name=f"weight_forward_{host[0]}{host[1]}"))
        return res[:-1]

    h = x
    saved = []
    for i in range(depth):
        sv = {}
        h, sv["ffa"] = gather_behind(("ffa", i), _ffn_fwd, h, w["ffa_norm"][i:i + 1], w["ffa"][i], f"ffa{i}")
        h, sv["mix"] = gather_behind(("mix", i), _even_fwd if i % 2 == 0 else _odd_fwd, h, w, i)
        h, sv["ffb"] = gather_behind(("ffb", i), _ffn_fwd, h, w["ffb_norm"][i:i + 1], w["ffb"][i], f"ffb{i}")
        h_in = h
        h, xn, gate, pp = _ple_fwd(h, w["ple_norm"][i:i + 1], w["ple_gate"][i], p[i], w["ple_proj"][i], name=f"ple{i}")
        sv["ple"] = dict(h_in=h_in, xn=xn, gate=gate, pp=pp)
        saved.append(sv)
    loss_vec, dh, d_final = _final_loss(h, w["final_norm"].reshape(1, -1), tgt, name="final_loss")

    per_layer = [dict() for _ in range(depth)]
    mats = {}
    grads = {}

    def reduce_behind(host, fn, *args):
        keys = None if ex is None else _REDUCE_RIDES.get(host)
        state = None if keys is None else ex.reduce_begin(keys, mats, tag=f"{host[0]}{host[1]}")
        res = fn(*args, None if keys is None else state[0])
        if keys is not None:
            ex.reduce_finish(state, res[-1])
        return res[:-1]

    for i in reversed(range(depth)):
        sv, gl = saved[i], per_layer[i]
        dz, dpp = _ple_bwd_elem(dh, sv["ple"]["gate"], sv["ple"]["pp"], name=f"ple{i}_bwd")
        mats["ple_w_gate", i] = _mm_tn(sv["ple"]["xn"], dz, name=f"ple{i}_dwg")
        mats["ple_w_proj", i] = _mm_tn(p[i], dpp, name=f"ple{i}_dwp")
        dh, gl["ple_norm"] = _mm_nt_rmsbwd([(dz, w["ple_gate"][i])], sv["ple"]["h_in"], w["ple_norm"][i:i + 1], dh,
                                           name=f"ple{i}_dx")
        dh, gl["ffb_norm"], mats["ffb_w_gate_up", i], mats["ffb_w_down", i] = reduce_behind(
            ("ffb", i), _ffn_bwd, dh, w["ffb_norm"][i:i + 1], w["ffb"][i], sv["ffb"], f"ffb{i}")
        dh, gl["mix_norm"], gm = reduce_behind(("mix", i), _even_bwd if i % 2 == 0 else _odd_bwd, dh, w, sv["mix"], i)
        for n, g in gm.items():
            if n in REPLICATED:
                grads[n] = g
            else:
                mats[n, 0] = g
        dh, gl["ffa_norm"], mats["ffa_w_gate_up", i], mats["ffa_w_down", i] = reduce_behind(
            ("ffa", i), _ffn_bwd, dh, w["ffa_norm"][i:i + 1], w["ffa"][i], sv["ffa"], f"ffa{i}")
    grads["final_norm"] = d_final.reshape(-1)
    for n in ("ffa_norm", "mix_norm", "ffb_norm", "ple_norm"):
        grads[n] = jnp.concatenate([per_layer[i][n] for i in range(depth)], axis=0)
    if ex is not None:
        ex.reduce(_REDUCE_LAST, mats, tag="last")
    else:
        for n, _ in SHARDED:
            grads[n] = [mats[n, i] for i in range(depth) if (n, i) in mats]
    return loss_vec[0, 0], dh, grads


def _cut_mode(local_shape, axis, ncols):
    return "cols" if axis == 2 and ncols % LANES == 0 else "blk"


class _Exchange:
    def __init__(self, wts):
        self.place = jnp.stack([2 * lax.axis_index("x") + lax.axis_index("y"), lax.axis_index("c")]).astype(jnp.int32)
        self.info = {}
        for n, axis in SHARDED:
            wb = wts[n].astype(BF16)
            mode = _cut_mode(wb.shape, axis, wb.shape[2])
            for i in range(wb.shape[0]):
                self.info[n, i] = dict(shard=wb[i], mode=mode, axis=axis)
        self.halves = {}

    def _modes(self, keys):
        return [self.info[k]["mode"] for k in keys]

    def gather_rider(self, keys):
        return _gather_rider([self.info[k]["shard"] for k in keys], self._modes(keys))

    def gather_finish(self, keys, landed, *, name):
        outs = _gather_forward(landed, [self.info[k]["shard"].shape for k in keys], self._modes(keys), name=name)
        got = {}
        for k, dst in zip(keys, outs):
            if self.info[k]["mode"] == "blk":
                dst = dst.reshape(-1, dst.shape[2]) if self.info[k]["axis"] == 1 else jnp.moveaxis(dst, 0, 1).reshape(dst.shape[1], -1)
            got[k] = dst
        return got

    def gather(self, keys, *, name):
        return self.gather_finish(keys, _run_rider(self.gather_rider(keys), name=name), name=name + "_forward")

    def reduce_begin(self, keys, mats, *, tag):
        modes = self._modes(keys)
        arrs = []
        for k in keys:
            g2, (rr, cc) = mats[k], self.info[k]["shard"].shape
            if self.info[k]["mode"] == "blk":
                g2 = g2.reshape(N_CHIPS, rr, cc) if self.info[k]["axis"] == 1 else g2.reshape(rr, N_CHIPS, cc).transpose(1, 0, 2)
            arrs.append(g2)
        landed = _rs_pair_swap(arrs, modes, name=f"rs_pair_swap_{tag}")
        parts = []
        for (n, i), m, a, l in zip(keys, modes, arrs, landed):
            pt = _rs_pair_add(_blk_view(a, m), _blk_view(l, m), self.place, name=f"rs_pair_add_{n}{i}")
            parts.append(pt[0] if m == "cols" else pt)
        return _exchange_rider(parts, modes), keys, parts

    def reduce_finish(self, state, landed):
        _, keys, parts = state
        for (n, i), m, pt, l in zip(keys, self._modes(keys), parts, landed):
            self.halves[n, i] = _rs_chip_sum(pt, l, m, self.place, name=f"rs_chip_sum_{n}{i}")

    def reduce(self, keys, mats, *, tag):
        state = self.reduce_begin(keys, mats, tag=tag)
        self.reduce_finish(state, _run_rider(state[0], name=f"rs_chip_exchange_{tag}"))

    def join(self, wts):
        keys = list(self.info)
        joined = dict(zip(keys, _rs_pair_join([self.halves[k] for k in keys], name="rs_pair_join")))
        return {n: jnp.stack([joined[n, i] for i in range(wts[n].shape[0])]).reshape(wts[n].shape) for n, _ in SHARDED}


def _small_rows(vals):
    rows = []
    for n in REPLICATED:
        v = vals[n].reshape(-1)
        rows.append(jnp.pad(v, (0, (-v.shape[0]) % FLAT_COLS)).reshape(-1, FLAT_COLS))
    out = jnp.concatenate(rows, axis=0)
    return jnp.pad(out, ((0, (-out.shape[0]) % 8), (0, 0)))


def kernel(x, p, ffa_norm, ffa_w_gate_up, ffa_w_down, mix_norm, ffb_norm, ffb_w_gate_up, ffb_w_down, ple_norm, ple_w_gate, ple_w_proj, ev_w_in, ev_sinks, ev_cq_norm, ev_w_uq, ev_ckv_norm, ev_w_ukv, ev_w_out, od_w_in, od_b_f, od_w_out, final_norm, loss_target, m_ffa_norm, m_ffa_w_gate_up, m_ffa_w_down, m_mix_norm, m_ffb_norm, m_ffb_w_gate_up, m_ffb_w_down, m_ple_norm, m_ple_w_gate, m_ple_w_proj, m_ev_w_in, m_ev_sinks, m_ev_cq_norm, m_ev_w_uq, m_ev_ckv_norm, m_ev_w_ukv, m_ev_w_out, m_od_w_in, m_od_b_f, m_od_w_out, m_final_norm, v_ffa_norm, v_ffa_w_gate_up, v_ffa_w_down, v_mix_norm, v_ffb_norm, v_ffb_w_gate_up, v_ffb_w_down, v_ple_norm, v_ple_w_gate, v_ple_w_proj, v_ev_w_in, v_ev_sinks, v_ev_cq_norm, v_ev_w_uq, v_ev_ckv_norm, v_ev_w_ukv, v_ev_w_out, v_od_w_in, v_od_b_f, v_od_w_out, v_final_norm):
    env = dict(locals())
    wts = {n: env[n] for n in WEIGHT_ORDER}
    mom1 = {n: env["m_" + n] for n in WEIGHT_ORDER}
    mom2 = {n: env["v_" + n] for n in WEIGHT_ORDER}
    ex = _Exchange(wts)

    w = {n: wts[n] for n in REPLICATED}
    _install_weights(w, ex.gather(_GATHER_FIRST, name="weight_gather_first"))

    loss_part, grad_x, grads = _local_step(x[0], p[:, 0], loss_target[0], w, ex)
    loss = lax.psum(loss_part, ("x", "y", "c"))
    gout = ex.join(wts)
    small = _allreduce_small(_small_rows(grads), name="small_allreduce")
    r0 = 0
    for n in REPLICATED:
        size = int(np.prod(wts[n].shape))
        nr = -(-size // FLAT_COLS)
        gout[n] = small[r0:r0 + nr].reshape(-1)[:size].reshape(wts[n].shape)
        r0 += nr

    delta, new_m, new_v = {}, {}, {}
    for n in WEIGHT_ORDER:
        delta[n], new_m[n], new_v[n] = _adamw(wts[n], gout[n], mom1[n], mom2[n], name="adamw_" + n)
    return (loss, grad_x[None], *[gout[n] for n in WEIGHT_ORDER], *[delta[n] for n in WEIGHT_ORDER],
            *[new_m[n] for n in WEIGHT_ORDER], *[new_v[n] for n in WEIGHT_ORDER])
```

```python
import functools
import math

import numpy as np
import jax
import jax.numpy as jnp
from jax import lax
from jax.experimental import pallas as pl
from jax.experimental.pallas import tpu as pltpu

F32 = jnp.float32
BF16 = jnp.bfloat16
NT = (((1,), (1,)), ((), ()))
TN = (((0,), (0,)), ((), ()))
MESH = pl.DeviceIdType.MESH

RMS_EPS = 1e-6
FFN_RES_SCALE = 0.5
A_HEADS, A_KV_HEADS, A_HEAD_DIM, WINDOW = 8, 2, 64, 128
B_HEADS, B_Q_LORA, B_KV_LORA, B_NOPE, B_ROPE, B_V = 8, 256, 128, 64, 32, 64
ROPE_THETA = 10000.0
C_HEADS, C_HEAD_DIM = 16, 64
ADAM_LR, ADAM_B1, ADAM_B2, ADAM_EPS, ADAM_WD, ADAM_STEP = 0.001, 0.9, 0.999, 1e-08, 0.01, 10

N_CHIPS = 4
LANES = 128
FLAT_COLS = 1024
MASK_VALUE = -1e30
VMEM_LIMIT = 48 * 2**20

SHARDED = (
    ("ffa_w_gate_up", 2), ("ffa_w_down", 1), ("ffb_w_gate_up", 2), ("ffb_w_down", 1),
    ("ple_w_gate", 1), ("ple_w_proj", 2), ("ev_w_in", 2), ("ev_w_uq", 2), ("ev_w_ukv", 2),
    ("ev_w_out", 1), ("od_w_in", 2), ("od_w_out", 1))
REPLICATED = ("ffa_norm", "mix_norm", "ffb_norm", "ple_norm", "final_norm",
              "ev_sinks", "ev_cq_norm", "ev_ckv_norm", "od_b_f")
WEIGHT_ORDER = ("ffa_norm", "ffa_w_gate_up", "ffa_w_down", "mix_norm", "ffb_norm", "ffb_w_gate_up",
                "ffb_w_down", "ple_norm", "ple_w_gate", "ple_w_proj", "ev_w_in", "ev_sinks",
                "ev_cq_norm", "ev_w_uq", "ev_ckv_norm", "ev_w_ukv", "ev_w_out", "od_w_in", "od_b_f",
                "od_w_out", "final_norm")


def _cp(*sem):
    return pltpu.CompilerParams(dimension_semantics=sem, vmem_limit_bytes=VMEM_LIMIT)


def _sigmoid(z):
    return 1.0 / (1.0 + jnp.exp(-z))


def _rms_stats(xv):
    r = lax.rsqrt(jnp.mean(xv * xv, axis=-1, keepdims=True) + RMS_EPS)
    return r, xv * r


def _rms_bwd(dxn, xv, g):
    r, xhat = _rms_stats(xv)
    u = dxn * g
    dx = r * (u - xhat * jnp.mean(u * xhat, axis=-1, keepdims=True))
    return dx, dxn * xhat


def _col_tile(k_rows, n, budget_bytes=6 * 2**20):
    if k_rows * n * 4 <= budget_bytes or n % LANES:
        return n
    units = n // LANES
    best = LANES
    for d in range(1, units + 1):
        if units % d == 0 and k_rows * d * LANES * 4 <= budget_bytes:
            best = d * LANES
    return best


def _row_tile(rows, cols, target_elems=2**18):
    if rows * cols <= target_elems or rows % 8:
        return rows
    best = 8
    for d in range(8, rows + 1, 8):
        if rows % d == 0 and d * cols <= target_elems:
            best = d
    return best


def _rms_mm_fwd(x, g, w, *, name, tm=512, tail=0):
    s, k = x.shape
    n = w.shape[1]
    head = n - tail

    def body(x_ref, g_ref, w_ref, *out_refs):
        _, xhat = _rms_stats(x_ref[...])
        xn = (xhat * g_ref[...]).astype(BF16)
        out_refs[-1][...] = xn
        y = jnp.dot(xn, w_ref[...], preferred_element_type=F32)
        if tail:
            out_refs[0][...] = y[:, :head]
            out_refs[1][...] = y[:, head:]
        else:
            out_refs[0][...] = y

    widths = [head, tail] if tail else [n]
    return pl.pallas_call(
        body, name=name, grid=(s // tm,),
        in_specs=[pl.BlockSpec((tm, k), lambda i: (i, 0)), pl.BlockSpec((1, k), lambda i: (0, 0)),
                  pl.BlockSpec((k, n), lambda i: (0, 0))],
        out_specs=[pl.BlockSpec((tm, c), lambda i: (i, 0)) for c in widths] + [pl.BlockSpec((tm, k), lambda i: (i, 0))],
        out_shape=[jax.ShapeDtypeStruct((s, c), F32) for c in widths] + [jax.ShapeDtypeStruct((s, k), BF16)],
        compiler_params=_cp("arbitrary"))(x, g, w)


def _ffn_up(x, g, wgu, *, name, tm=512, rider=None):
    s, k = x.shape
    f = wgu.shape[1] // 2
    tn = _col_tile(k, f)
    nj = f // tn

    def body(x_ref, g_ref, wg_ref, wu_ref, gate_ref, up_ref, act_ref, xn_ref, xn_sc):
        @pl.when(pl.program_id(1) == 0)
        def _():
            _, xhat = _rms_stats(x_ref[...])
            xn = (xhat * g_ref[...]).astype(BF16)
            xn_sc[...] = xn
            xn_ref[...] = xn

        xn = xn_sc[...]
        gg = jnp.dot(xn, wg_ref[...], preferred_element_type=F32)
        uu = jnp.dot(xn, wu_ref[...], preferred_element_type=F32)
        gate_ref[...] = gg.astype(BF16)
        up_ref[...] = uu.astype(BF16)
        act_ref[...] = ((gg * _sigmoid(gg)) * uu).astype(BF16)

    tile = pl.BlockSpec((tm, tn), lambda i, j: (i, j))
    return _call_with_rider(
        body, rider, name=name, grid=(s // tm, nj),
        in_specs=[pl.BlockSpec((tm, k), lambda i, j: (i, 0)), pl.BlockSpec((1, k), lambda i, j: (0, 0)),
                  pl.BlockSpec((k, tn), lambda i, j: (0, j)), pl.BlockSpec((k, tn), lambda i, j: (0, j + nj))],
        out_specs=[tile, tile, tile, pl.BlockSpec((tm, k), lambda i, j: (i, 0))],
        out_shape=[jax.ShapeDtypeStruct((s, f), BF16)] * 3 + [jax.ShapeDtypeStruct((s, k), BF16)],
        scratch_shapes=[pltpu.VMEM((tm, k), BF16)],
        compiler_params=_cp("arbitrary", "arbitrary"), args=(x, g, wgu, wgu))


def _mm_res_fwd(a, w, res, *, scale, name, tm=512):
    s, k = a.shape
    n = w.shape[1]

    def body(a_ref, w_ref, r_ref, o_ref):
        o_ref[...] = r_ref[...] + scale * jnp.dot(a_ref[...], w_ref[...], preferred_element_type=F32)

    return pl.pallas_call(
        body, name=name, grid=(s // tm,),
        in_specs=[pl.BlockSpec((tm, k), lambda i: (i, 0)), pl.BlockSpec((k, n), lambda i: (0, 0)),
                  pl.BlockSpec((tm, n), lambda i: (i, 0))],
        out_specs=pl.BlockSpec((tm, n), lambda i: (i, 0)),
        out_shape=jax.ShapeDtypeStruct((s, n), F32),
        compiler_params=_cp("arbitrary"))(a, w, res)


def _ffn_down_bwd(dh, wd, gate, up, *, scale, name, tm=512, rider=None):
    s, d = dh.shape
    f = wd.shape[0]
    tn = _col_tile(d, f)

    def body(dh_ref, wd_ref, gate_ref, up_ref, dg_ref, du_ref):
        dhb = (dh_ref[...] * scale).astype(BF16)
        da = lax.dot_general(dhb, wd_ref[...], NT, preferred_element_type=F32)
        gg = gate_ref[...].astype(F32)
        uu = up_ref[...].astype(F32)
        sg = _sigmoid(gg)
        dg_ref[...] = (da * uu * (sg * (1.0 + gg * (1.0 - sg)))).astype(BF16)
        du_ref[...] = (da * (gg * sg)).astype(BF16)

    tile = pl.BlockSpec((tm, tn), lambda i, j: (i, j))
    return _call_with_rider(
        body, rider, name=name, grid=(s // tm, f // tn),
        in_specs=[pl.BlockSpec((tm, d), lambda i, j: (i, 0)), pl.BlockSpec((tn, d), lambda i, j: (j, 0)), tile, tile],
        out_specs=[tile, tile],
        out_shape=[jax.ShapeDtypeStruct((s, f), BF16)] * 2, scratch_shapes=[],
        compiler_params=_cp("arbitrary", "arbitrary"), args=(dh, wd, gate, up))


def _mm_tn(a, bs, *, name, b_scale=1.0, ts=512):
    bs = list(bs) if isinstance(bs, (list, tuple)) else [bs]
    s, k = a.shape
    n = bs[0].shape[1]
    tn = _col_tile(k, n, 12 * 2**20)
    per = n // tn

    def body(a_ref, *refs):
        b_refs, o_ref = refs[:-1], refs[-1]
        j = pl.program_id(0)

        @pl.when(pl.program_id(1) == 0)
        def _():
            o_ref[...] = jnp.zeros_like(o_ref)

        for m, b_ref in enumerate(b_refs):
            def acc(b_ref=b_ref):
                bv = b_ref[...]
                if b_scale != 1.0:
                    bv = bv * b_scale
                o_ref[...] += lax.dot_general(a_ref[...].astype(BF16), bv.astype(BF16), TN, preferred_element_type=F32)

            if len(b_refs) == 1:
                acc()
            else:
                pl.when(jnp.logical_and(j >= m * per, j < (m + 1) * per))(acc)

    def b_spec(m):
        def idx(j, t):
            mine = jnp.logical_and(j >= m * per, j < (m + 1) * per)
            return (jnp.where(mine, t, 0), jnp.clip(j - m * per, 0, per - 1))
        return pl.BlockSpec((ts, tn), idx)

    return pl.pallas_call(
        body, name=name, grid=(per * len(bs), s // ts),
        in_specs=[pl.BlockSpec((ts, k), lambda j, t: (t, 0))] + [b_spec(m) for m in range(len(bs))],
        out_specs=pl.BlockSpec((k, tn), lambda j, t: (0, j)),
        out_shape=jax.ShapeDtypeStruct((k, n * len(bs)), F32),
        compiler_params=_cp("arbitrary", "arbitrary"))(a, *bs)


def _mm_nt(dy, w, *, name, tm=512):
    s, n = dy.shape
    k = w.shape[0]

    def body(dy_ref, w_ref, o_ref):
        o_ref[...] = lax.dot_general(dy_ref[...].astype(BF16), w_ref[...], NT, preferred_element_type=F32)

    return pl.pallas_call(
        body, name=name, grid=(s // tm,),
        in_specs=[pl.BlockSpec((tm, n), lambda i: (i, 0)), pl.BlockSpec((k, n), lambda i: (0, 0))],
        out_specs=pl.BlockSpec((tm, k), lambda i: (i, 0)),
        out_shape=jax.ShapeDtypeStruct((s, k), F32),
        compiler_params=_cp("arbitrary"))(dy, w)


def _mm_nt_rmsbwd(pairs, x, g, dres, *, name, tm=256):
    s, k = x.shape
    npairs = len(pairs)
    pairs = [pr if len(pr) == 3 else (pr[0], pr[1], 0) for pr in pairs]

    def body(*refs):
        dy_refs = refs[0:2 * npairs:2]
        w_refs = refs[1:2 * npairs:2]
        rest = refs[2 * npairs:]
        x_ref, g_ref = rest[0], rest[1]
        if dres is None:
            dx_ref, dg_ref = rest[2], rest[3]
        else:
            dres_ref, dx_ref, dg_ref = rest[2], rest[3], rest[4]
        dxn = None
        for dy_ref, w_ref in zip(dy_refs, w_refs):
            t = lax.dot_general(dy_ref[...].astype(BF16), w_ref[...], NT, preferred_element_type=F32)
            dxn = t if dxn is None else dxn + t
        dx, dgrow = _rms_bwd(dxn, x_ref[...], g_ref[...])
        if dres is not None:
            dx = dx + dres_ref[...]
        dx_ref[...] = dx

        @pl.when(pl.program_id(0) == 0)
        def _():
            dg_ref[...] = jnp.zeros_like(dg_ref)

        dg_ref[...] += jnp.sum(dgrow, axis=0, keepdims=True)

    in_specs, args = [], []
    for dy, w, cb in pairs:
        n = dy.shape[1]
        in_specs += [pl.BlockSpec((tm, n), lambda i: (i, 0)), pl.BlockSpec((k, n), lambda i, cb=cb: (0, cb))]
        args += [dy, w]
    row = pl.BlockSpec((tm, k), lambda i: (i, 0))
    vec = pl.BlockSpec((1, k), lambda i: (0, 0))
    in_specs += [row, vec]
    args += [x, g]
    if dres is not None:
        in_specs.append(row)
        args.append(dres)
    return pl.pallas_call(
        body, name=name, grid=(s // tm,), in_specs=in_specs, out_specs=[row, vec],
        out_shape=[jax.ShapeDtypeStruct((s, k), F32), jax.ShapeDtypeStruct((1, k), F32)],
        compiler_params=_cp("arbitrary"))(*args)


def _ple_fwd(h, g, wg, p, wp, *, name, tm=512):
    s, d = h.shape
    pd = p.shape[1]

    def body(h_ref, g_ref, wg_ref, p_ref, wp_ref, o_ref, xn_ref, gate_ref, pp_ref):
        hv = h_ref[...]
        _, xhat = _rms_stats(hv)
        xn = (xhat * g_ref[...]).astype(BF16)
        xn_ref[...] = xn
        gate = _sigmoid(jnp.dot(xn, wg_ref[...], preferred_element_type=F32))
        pp = jnp.dot(p_ref[...].astype(BF16), wp_ref[...], preferred_element_type=F32)
        gate_ref[...] = gate.astype(BF16)
        pp_ref[...] = pp.astype(BF16)
        o_ref[...] = hv + gate * pp

    row = pl.BlockSpec((tm, d), lambda i: (i, 0))
    return pl.pallas_call(
        body, name=name, grid=(s // tm,),
        in_specs=[row, pl.BlockSpec((1, d), lambda i: (0, 0)), pl.BlockSpec((d, d), lambda i: (0, 0)),
                  pl.BlockSpec((tm, pd), lambda i: (i, 0)), pl.BlockSpec((pd, d), lambda i: (0, 0))],
        out_specs=[row, row, row, row],
        out_shape=[jax.ShapeDtypeStruct((s, d), F32)] + [jax.ShapeDtypeStruct((s, d), BF16)] * 3,
        compiler_params=_cp("arbitrary"))(h, g, wg, p, wp)


def _ple_bwd_elem(dh, gate, pp, *, name, tm=512):
    s, d = dh.shape

    def body(dh_ref, gate_ref, pp_ref, dz_ref, dpp_ref):
        dhv = dh_ref[...]
        gt = gate_ref[...].astype(F32)
        dz_ref[...] = (dhv * pp_ref[...].astype(F32) * (gt * (1.0 - gt))).astype(BF16)
        dpp_ref[...] = (dhv * gt).astype(BF16)

    row = pl.BlockSpec((tm, d), lambda i: (i, 0))
    return pl.pallas_call(
        body, name=name, grid=(s // tm,), in_specs=[row, row, row], out_specs=[row, row],
        out_shape=[jax.ShapeDtypeStruct((s, d), BF16)] * 2,
        compiler_params=_cp("arbitrary"))(dh, gate, pp)


def _final_loss(h, g, tgt, *, name, tm=512):
    s, d = h.shape

    def body(h_ref, g_ref, t_ref, loss_ref, dh_ref, dg_ref):
        @pl.when(pl.program_id(0) == 0)
        def _():
            loss_ref[...] = jnp.zeros_like(loss_ref)
            dg_ref[...] = jnp.zeros_like(dg_ref)

        hv = h_ref[...]
        gv = g_ref[...]
        _, xhat = _rms_stats(hv)
        err = xhat * gv - t_ref[...]
        per_row = jnp.mean(err * err, axis=-1, keepdims=True)
        loss_ref[...] += 0.5 * jnp.sum(per_row, axis=0, keepdims=True)
        dx, dgrow = _rms_bwd(err * (1.0 / d), hv, gv)
        dh_ref[...] = dx
        dg_ref[...] += jnp.sum(dgrow, axis=0, keepdims=True)

    row = pl.BlockSpec((tm, d), lambda i: (i, 0))
    vec = pl.BlockSpec((1, d), lambda i: (0, 0))
    return pl.pallas_call(
        body, name=name, grid=(s // tm,), in_specs=[row, vec, row],
        out_specs=[pl.BlockSpec((1, LANES), lambda i: (0, 0)), row, vec],
        out_shape=[jax.ShapeDtypeStruct((1, LANES), F32), jax.ShapeDtypeStruct((s, d), F32),
                   jax.ShapeDtypeStruct((1, d), F32)],
        compiler_params=_cp("arbitrary"))(h, g, tgt)


def _rope_fwd(y1, y2, cos, sin, *, name, tm=512):
    s, r = y1.shape

    def body(a_ref, b_ref, c_ref, s_ref, o_ref):
        o_ref[...] = a_ref[...] * c_ref[...] + b_ref[...] * s_ref[...]

    row = pl.BlockSpec((tm, r), lambda i: (i, 0))
    return pl.pallas_call(
        body, name=name, grid=(s // tm,), in_specs=[row] * 4, out_specs=row,
        out_shape=jax.ShapeDtypeStruct((s, r), F32), compiler_params=_cp("arbitrary"))(y1, y2, cos, sin)


def _rope_bwd(dout, cos, sin, *, name, tm=512):
    nh, s, r = dout.shape

    def body(d_ref, c_ref, s_ref, o1_ref, o2_ref):
        tot = d_ref[0]
        for hh in range(1, nh):
            tot = tot + d_ref[hh]
        o1_ref[...] = tot * c_ref[...]
        o2_ref[...] = tot * s_ref[...]

    row = pl.BlockSpec((tm, r), lambda i: (i, 0))
    return pl.pallas_call(
        body, name=name, grid=(s // tm,),
        in_specs=[pl.BlockSpec((nh, tm, r), lambda i: (0, i, 0)), row, row], out_specs=[row, row],
        out_shape=[jax.ShapeDtypeStruct((s, r), F32)] * 2, compiler_params=_cp("arbitrary"))(dout, cos, sin)


def _split3(v):
    h1 = v.astype(BF16)
    r1 = v - h1.astype(F32)
    h2 = r1.astype(BF16)
    h3 = (r1 - h2.astype(F32)).astype(BF16)
    return h1, h2, h3


def _tri(tb, upper):
    r = lax.broadcasted_iota(jnp.int32, (tb, tb), 0)
    c = lax.broadcasted_iota(jnp.int32, (tb, tb), 1)
    return jnp.where((r <= c) if upper else (r >= c), 1.0, 0.0).astype(BF16)


def _fox_gate_fwd(ft, bf, *, out_scale, name, tb=512):
    nh, s = ft.shape

    def body(f_ref, b_ref, o_ref, carry):
        @pl.when(pl.program_id(0) == 0)
        def _():
            carry[...] = jnp.zeros_like(carry)

        z = f_ref[...] + b_ref[...]
        lf = jnp.minimum(z, 0.0) - jnp.log(1.0 + jnp.exp(-jnp.abs(z)))
        tri = _tri(tb, True)
        cs = sum(jnp.dot(t, tri, preferred_element_type=F32) for t in _split3(lf)) + carry[...]
        for n, term in enumerate(_split3(cs * out_scale)):
            o_ref[n] = term
        carry[...] += jnp.sum(lf, axis=-1, keepdims=True)

    return pl.pallas_call(
        body, name=name, grid=(s // tb,),
        in_specs=[pl.BlockSpec((nh, tb), lambda t: (0, t)), pl.BlockSpec((nh, 1), lambda t: (0, 0))],
        out_specs=pl.BlockSpec((3, nh, tb), lambda t: (0, 0, t)),
        out_shape=jax.ShapeDtypeStruct((3, nh, s), BF16),
        scratch_shapes=[pltpu.VMEM((nh, 1), F32)], compiler_params=_cp("arbitrary"))(ft, bf)


def _fox_gate_bwd(drow, dcol, ft, bf, *, inv_scale, name, tb=512):
    nh, s = ft.shape
    nb = s // tb

    def body(dr_ref, dc_ref, f_ref, b_ref, df_ref, db_ref, carry):
        @pl.when(pl.program_id(0) == 0)
        def _():
            carry[...] = jnp.zeros_like(carry)
            db_ref[...] = jnp.zeros_like(db_ref)

        dc = (dr_ref[...] - dc_ref[...]) * inv_scale
        tri = _tri(tb, False)
        suf = sum(jnp.dot(t, tri, preferred_element_type=F32) for t in _split3(dc)) + carry[...]
        z = f_ref[...] + b_ref[...]
        dz = suf * (1.0 / (1.0 + jnp.exp(z)))
        df_ref[...] = dz
        db_ref[...] += jnp.sum(dz, axis=-1, keepdims=True)
        carry[...] += jnp.sum(dc, axis=-1, keepdims=True)

    rev = pl.BlockSpec((nh, tb), lambda t: (0, nb - 1 - t))
    one = pl.BlockSpec((nh, 1), lambda t: (0, 0))
    return pl.pallas_call(
        body, name=name, grid=(nb,), in_specs=[rev, rev, rev, one], out_specs=[rev, one],
        out_shape=[jax.ShapeDtypeStruct((nh, s), F32), jax.ShapeDtypeStruct((nh, 1), F32)],
        scratch_shapes=[pltpu.VMEM((nh, 1), F32)], compiler_params=_cp("arbitrary"))(drow, dcol, ft, bf)


def _tri_fwd(t, nq):
    i = sum((t >= (r * (r + 1)) // 2).astype(jnp.int32) for r in range(1, nq))
    return i, t - (i * (i + 1)) // 2


def _tri_bwd(t, nq):
    j = sum((t >= r * nq - (r * (r - 1)) // 2).astype(jnp.int32) for r in range(1, nq))
    return j, j + t - (j * nq - (j * (j - 1)) // 2)


def _scores_t(k, q, *, scale, diag):
    s = lax.dot_general(k, q, NT, preferred_element_type=F32) * scale
    if diag:
        r = lax.broadcasted_iota(jnp.int32, s.shape, 0)
        c = lax.broadcasted_iota(jnp.int32, s.shape, 1)
        s = jnp.where(r <= c, s, MASK_VALUE)
    return s


def _causal_fwd_t(q, k, vt, *, scale, name, tq, hb=2, rider=None):
    nh, s, dq = q.shape
    dv = vt.shape[1]
    nq = s // tq
    nsteps = (nq * (nq + 1)) // 2

    def body(q_ref, k_ref, vt_ref, o_ref, lse_ref, m_sc, l_sc, acc_sc):
        i, j = _tri_fwd(pl.program_id(1), nq)

        @pl.when(j == 0)
        def _():
            m_sc[...] = jnp.full_like(m_sc, MASK_VALUE)
            l_sc[...] = jnp.zeros_like(l_sc)
            acc_sc[...] = jnp.zeros_like(acc_sc)

        def step(diag):
            for u in range(hb):
                sc = _scores_t(k_ref[u], q_ref[u], scale=scale, diag=diag)
                m_prev = m_sc[u]
                m_new = jnp.maximum(m_prev, jnp.max(sc, axis=0, keepdims=True))
                alpha = jnp.exp(m_prev - m_new)
                pr = jnp.exp(sc - m_new)
                l_new = alpha * l_sc[u] + jnp.sum(pr, axis=0, keepdims=True)
                acc = alpha * acc_sc[u] + jnp.dot(vt_ref[u], pr.astype(BF16), preferred_element_type=F32)
                if diag:
                    o_ref[u] = (acc / l_new).astype(BF16)
                    lse_ref[u] = m_new + jnp.log(l_new)
                else:
                    m_sc[u], l_sc[u], acc_sc[u] = m_new, l_new, acc

        pl.when(j < i)(functools.partial(step, False))
        pl.when(j == i)(functools.partial(step, True))

    def qi(t):
        return _tri_fwd(t, nq)[0]

    def kj(t):
        return _tri_fwd(t, nq)[1]

    return _call_with_rider(
        body, rider, name=name, grid=(nh // hb, nsteps),
        in_specs=[pl.BlockSpec((hb, tq, dq), lambda hp, t: (hp, qi(t), 0)),
                  pl.BlockSpec((hb, tq, dq), lambda hp, t: (hp, kj(t), 0)),
                  pl.BlockSpec((hb, dv, tq), lambda hp, t: (hp, 0, kj(t)))],
        out_specs=[pl.BlockSpec((hb, dv, tq), lambda hp, t: (hp, 0, qi(t))),
                   pl.BlockSpec((hb, 1, tq), lambda hp, t: (hp, 0, qi(t)))],
        out_shape=[jax.ShapeDtypeStruct((nh, dv, s), BF16), jax.ShapeDtypeStruct((nh, 1, s), F32)],
        scratch_shapes=[pltpu.VMEM((hb, 1, tq), F32), pltpu.VMEM((hb, 1, tq), F32), pltpu.VMEM((hb, dv, tq), F32)],
        compiler_params=_cp("arbitrary", "arbitrary"), args=(q, k, vt))


def _causal_bwd_t(q, k, v, ot, dot_, lse, *, scale, name, tq, hb=2, rider=None):
    nh, s, dq = q.shape
    dv = v.shape[-1]
    nq = s // tq
    nsteps = (nq * (nq + 1)) // 2

    def body(q_ref, k_ref, v_ref, ot_ref, dot_ref, lse_ref, dq_ref, dk_ref, dvt_ref):
        t = pl.program_id(1)
        j, i = _tri_bwd(t, nq)

        @pl.when(t == 0)
        def _():
            dq_ref[...] = jnp.zeros_like(dq_ref)

        def step(diag):
            rows = pl.ds(pl.multiple_of(i * tq, tq), tq)
            for u in range(hb):
                qv, kv, dov = q_ref[u], k_ref[u], dot_ref[u]
                pr = jnp.exp(_scores_t(kv, qv, scale=scale, diag=diag) - lse_ref[u])
                dp = jnp.dot(v_ref[u], dov, preferred_element_type=F32)
                delta = jnp.sum(dov.astype(F32) * ot_ref[u].astype(F32), axis=0, keepdims=True)
                dsb = ((pr * (dp - delta)) * scale).astype(BF16)
                d_v = lax.dot_general(dov, pr.astype(BF16), NT, preferred_element_type=F32)
                d_k = jnp.dot(dsb, qv, preferred_element_type=F32)
                if diag:
                    dvt_ref[u], dk_ref[u] = d_v, d_k
                else:
                    dvt_ref[u] += d_v
                    dk_ref[u] += d_k
                dq_ref[u, rows, :] += lax.dot_general(dsb, kv, TN, preferred_element_type=F32)

        pl.when(i > j)(functools.partial(step, False))
        pl.when(i == j)(functools.partial(step, True))

    def qi(t):
        return _tri_bwd(t, nq)[1]

    def kj(t):
        return _tri_bwd(t, nq)[0]

    rows_q = pl.BlockSpec((hb, tq, dq), lambda hp, t: (hp, qi(t), 0))
    rows_k = pl.BlockSpec((hb, tq, dq), lambda hp, t: (hp, kj(t), 0))
    lanes_q = pl.BlockSpec((hb, dv, tq), lambda hp, t: (hp, 0, qi(t)))
    return _call_with_rider(
        body, rider, name=name, grid=(nh // hb, nsteps),
        in_specs=[rows_q, rows_k, pl.BlockSpec((hb, tq, dv), lambda hp, t: (hp, kj(t), 0)), lanes_q, lanes_q,
                  pl.BlockSpec((hb, 1, tq), lambda hp, t: (hp, 0, qi(t)))],
        out_specs=[pl.BlockSpec((hb, s, dq), lambda hp, t: (hp, 0, 0)), rows_k,
                   pl.BlockSpec((hb, dv, tq), lambda hp, t: (hp, 0, kj(t)))],
        out_shape=[jax.ShapeDtypeStruct((nh, s, dq), F32), jax.ShapeDtypeStruct((nh, s, dq), F32),
                   jax.ShapeDtypeStruct((nh, dv, s), F32)],
        scratch_shapes=[], compiler_params=_cp("arbitrary", "arbitrary"), args=(q, k, v, ot, dot_, lse))


def _swa_scores_t(k, q, dist, ok, *, scale, slope):
    s = lax.dot_general(k, q, NT, preferred_element_type=F32) * scale - slope * dist.astype(F32)
    return jnp.where(ok, s, MASK_VALUE)


def _swa_geometry(tb, w, has_other):
    r = lax.broadcasted_iota(jnp.int32, (tb, tb), 0)
    c = lax.broadcasted_iota(jnp.int32, (tb, tb), 1)
    d_same = c - r
    ok_same = jnp.logical_and(d_same >= 0, d_same < w)

    def other(ncols):
        rr = lax.broadcasted_iota(jnp.int32, (w, ncols), 0)
        cc = lax.broadcasted_iota(jnp.int32, (w, ncols), 1)
        dd = cc + w - rr
        return dd, jnp.logical_and(dd < w, has_other)

    return (d_same, ok_same), other


def _swa_fwd_t(q, k, vt, slopes_sinks, *, scale, window, name, tb=256):
    nh, s, d = q.shape
    nkv = k.shape[0]
    grp = nh // nkv
    w = window
    per = tb // w
    assert tb % w == 0

    def body(q_ref, kc_ref, kp_ref, vc_ref, vp_ref, ss_ref, o_ref, lse_ref):
        kvh, i = pl.program_id(0), pl.program_id(1)
        (d_c, ok_c), other = _swa_geometry(tb, w, i > 0)
        d_p, ok_p = other(tb)
        for g in range(grp):
            h = kvh * grp + g
            slope, sink = ss_ref[0, h], ss_ref[1, h]
            qg = q_ref[g]
            s_c = _swa_scores_t(kc_ref[...], qg, d_c, ok_c, scale=scale, slope=slope)
            s_p = _swa_scores_t(kp_ref[...], qg, d_p, ok_p, scale=scale, slope=slope)
            m = jnp.maximum(jnp.maximum(jnp.max(s_c, axis=0, keepdims=True), jnp.max(s_p, axis=0, keepdims=True)), sink)
            p_c, p_p = jnp.exp(s_c - m), jnp.exp(s_p - m)
            l = jnp.sum(p_c, axis=0, keepdims=True) + jnp.sum(p_p, axis=0, keepdims=True) + jnp.exp(sink - m)
            acc = (jnp.dot(vc_ref[...], p_c.astype(BF16), preferred_element_type=F32)
                   + jnp.dot(vp_ref[...], p_p.astype(BF16), preferred_element_type=F32))
            o_ref[g] = (acc / l).astype(BF16)
            lse_ref[g] = m + jnp.log(l)

    def prev(i):
        return jnp.maximum(i * per - 1, 0)

    return pl.pallas_call(
        body, name=name, grid=(nkv, s // tb),
        in_specs=[pl.BlockSpec((grp, tb, d), lambda kh, i: (kh, i, 0)),
                  pl.BlockSpec((None, tb, d), lambda kh, i: (kh, i, 0)),
                  pl.BlockSpec((None, w, d), lambda kh, i: (kh, prev(i), 0)),
                  pl.BlockSpec((None, d, tb), lambda kh, i: (kh, 0, i)),
                  pl.BlockSpec((None, d, w), lambda kh, i: (kh, 0, prev(i))),
                  pl.BlockSpec(memory_space=pltpu.SMEM)],
        out_specs=[pl.BlockSpec((grp, d, tb), lambda kh, i: (kh, 0, i)), pl.BlockSpec((grp, 1, tb), lambda kh, i: (kh, 0, i))],
        out_shape=[jax.ShapeDtypeStruct((nh, d, s), BF16), jax.ShapeDtypeStruct((nh, 1, s), F32)],
        compiler_params=_cp("arbitrary", "arbitrary"))(q, k, k, vt, vt, slopes_sinks)


def _swa_bwd_t(q, k, v, ot, dot_, lse, slopes_sinks, *, scale, window, name, tb=256):
    nh, s, d = q.shape
    nkv = k.shape[0]
    grp = nh // nkv
    w = window
    per = tb // w
    nb = s // tb

    def body(qc_ref, qn_ref, kc_ref, kp_ref, vc_ref, vp_ref, oc_ref, on_ref, doc_ref, don_ref, lc_ref, ln_ref, ss_ref,
             dq_ref, dk_ref, dvt_ref, dsink_ref):
        kvh, i = pl.program_id(0), pl.program_id(1)

        @pl.when(i == 0)
        def _():
            dsink_ref[...] = jnp.zeros_like(dsink_ref)

        (d_c, ok_c), other = _swa_geometry(tb, w, i > 0)
        d_p, ok_p = other(tb)
        d_n, ok_n = _swa_geometry(tb, w, i < nb - 1)[1](w)
        kc, kp, vc, vp = kc_ref[...], kp_ref[...], vc_ref[...], vp_ref[...]
        k_last, v_last = kc[tb - w:, :], vc[tb - w:, :]
        dk_acc = jnp.zeros((tb, d), F32)
        dv_acc = jnp.zeros((d, tb), F32)
        dk_tail = jnp.zeros((w, d), F32)
        dv_tail = jnp.zeros((d, w), F32)
        for g in range(grp):
            h = kvh * grp + g
            slope, sink = ss_ref[0, h], ss_ref[1, h]
            qg, dog, lse_c = qc_ref[g], doc_ref[g], lc_ref[g]
            delta = jnp.sum(dog.astype(F32) * oc_ref[g].astype(F32), axis=0, keepdims=True)
            p_c = jnp.exp(_swa_scores_t(kc, qg, d_c, ok_c, scale=scale, slope=slope) - lse_c)
            p_p = jnp.exp(_swa_scores_t(kp, qg, d_p, ok_p, scale=scale, slope=slope) - lse_c)
            ds_c = ((p_c * (jnp.dot(vc, dog, preferred_element_type=F32) - delta)) * scale).astype(BF16)
            ds_p = ((p_p * (jnp.dot(vp, dog, preferred_element_type=F32) - delta)) * scale).astype(BF16)
            dq_ref[g] = (lax.dot_general(ds_c, kc, TN, preferred_element_type=F32)
                         + lax.dot_general(ds_p, kp, TN, preferred_element_type=F32))
            dk_acc += jnp.dot(ds_c, qg, preferred_element_type=F32)
            dv_acc += lax.dot_general(dog, p_c.astype(BF16), NT, preferred_element_type=F32)
            dsink_ref[g] -= jnp.broadcast_to(jnp.sum(jnp.exp(sink - lse_c) * delta, axis=1, keepdims=True), (1, LANES))
            qn, don = qn_ref[g], don_ref[g]
            delta_n = jnp.sum(don.astype(F32) * on_ref[g].astype(F32), axis=0, keepdims=True)
            p_n = jnp.exp(_swa_scores_t(k_last, qn, d_n, ok_n, scale=scale, slope=slope) - ln_ref[g])
            ds_n = ((p_n * (jnp.dot(v_last, don, preferred_element_type=F32) - delta_n)) * scale).astype(BF16)
            dk_tail += jnp.dot(ds_n, qn, preferred_element_type=F32)
            dv_tail += lax.dot_general(don, p_n.astype(BF16), NT, preferred_element_type=F32)
        dk_ref[...] = dk_acc
        dvt_ref[...] = dv_acc
        dk_ref[tb - w:, :] += dk_tail
        dvt_ref[:, tb - w:] += dv_tail

    def prev(i):
        return jnp.maximum(i * per - 1, 0)

    def nxt(i):
        return jnp.minimum((i + 1) * per, s // w - 1)

    return pl.pallas_call(
        body, name=name, grid=(nkv, nb),
        in_specs=[pl.BlockSpec((grp, tb, d), lambda kh, i: (kh, i, 0)),
                  pl.BlockSpec((grp, w, d), lambda kh, i: (kh, nxt(i), 0)),
                  pl.BlockSpec((None, tb, d), lambda kh, i: (kh, i, 0)),
                  pl.BlockSpec((None, w, d), lambda kh, i: (kh, prev(i), 0)),
                  pl.BlockSpec((None, tb, d), lambda kh, i: (kh, i, 0)),
                  pl.BlockSpec((None, w, d), lambda kh, i: (kh, prev(i), 0)),
                  pl.BlockSpec((grp, d, tb), lambda kh, i: (kh, 0, i)),
                  pl.BlockSpec((grp, d, w), lambda kh, i: (kh, 0, nxt(i))),
                  pl.BlockSpec((grp, d, tb), lambda kh, i: (kh, 0, i)),
                  pl.BlockSpec((grp, d, w), lambda kh, i: (kh, 0, nxt(i))),
                  pl.BlockSpec((grp, 1, tb), lambda kh, i: (kh, 0, i)),
                  pl.BlockSpec((grp, 1, w), lambda kh, i: (kh, 0, nxt(i))),
                  pl.BlockSpec(memory_space=pltpu.SMEM)],
        out_specs=[pl.BlockSpec((grp, tb, d), lambda kh, i: (kh, i, 0)),
                   pl.BlockSpec((None, tb, d), lambda kh, i: (kh, i, 0)),
                   pl.BlockSpec((None, d, tb), lambda kh, i: (kh, 0, i)),
                   pl.BlockSpec((None, grp, 1, LANES), lambda kh, i: (kh, 0, 0, 0))],
        out_shape=[jax.ShapeDtypeStruct((nh, s, d), F32), jax.ShapeDtypeStruct((nkv, s, d), F32),
                   jax.ShapeDtypeStruct((nkv, d, s), F32), jax.ShapeDtypeStruct((nkv, grp, 1, LANES), F32)],
        compiler_params=_cp("arbitrary", "arbitrary"))(q, q, k, k, v, v, ot, ot, dot_, dot_, lse, lse, slopes_sinks)


def _adamw(w, g, m, v, *, name):
    shape = w.shape
    cols = shape[-1]
    rows = int(np.prod(shape[:-1])) if len(shape) > 1 else 1
    tr = _row_tile(rows, cols)
    c1 = 1.0 - ADAM_B1 ** ADAM_STEP
    c2 = 1.0 - ADAM_B2 ** ADAM_STEP

    def body(w_ref, g_ref, m_ref, v_ref, d_ref, mo_ref, vo_ref):
        gv = g_ref[...]
        mn = ADAM_B1 * m_ref[...] + (1.0 - ADAM_B1) * gv
        vn = ADAM_B2 * v_ref[...] + (1.0 - ADAM_B2) * (gv * gv)
        mo_ref[...] = mn
        vo_ref[...] = vn
        d_ref[...] = -ADAM_LR * ((mn / c1) / (jnp.sqrt(vn / c2) + ADAM_EPS) + ADAM_WD * w_ref[...])

    blk = pl.BlockSpec((tr, cols), lambda i: (i, 0))
    outs = pl.pallas_call(
        body, name=name, grid=(rows // tr,), in_specs=[blk] * 4, out_specs=[blk] * 3,
        out_shape=[jax.ShapeDtypeStruct((rows, cols), F32)] * 3,
        compiler_params=_cp("arbitrary"))(*[a.reshape(rows, cols) for a in (w, g, m, v)])
    return tuple(a.reshape(shape) for a in outs)


def _hbm_spec():
    return pl.BlockSpec(memory_space=pl.ANY)


def _mesh_place():
    x, y, c = lax.axis_index("x"), lax.axis_index("y"), lax.axis_index("c")
    return x, y, c, [(1 - x, y), (x, 1 - y), (1 - x, 1 - y)]


def _half_rows(c, rows, align):
    return pl.ds(pl.multiple_of(c * (rows // 2), align), rows // 2)


def _part(ref, mode, k, n, rows=None):
    if mode == "cols":
        cols = pl.ds(pl.multiple_of(k * n, LANES), n)
        return ref.at[:, cols] if rows is None else ref.at[rows, cols]
    return ref.at[k] if rows is None else ref.at[k, rows, :]


class _Rider:
    def __init__(self, inputs, out_shape, n_sems, start, finish):
        self.inputs, self.out_shape, self.n_sems, self.start, self.finish = inputs, out_shape, n_sems, start, finish


def _call_with_rider(body, rider, *, name, grid, in_specs, out_specs, out_shape, scratch_shapes, compiler_params, args):
    if rider is None:
        outs = pl.pallas_call(body, name=name, grid=grid, in_specs=in_specs, out_specs=out_specs, out_shape=out_shape,
                              scratch_shapes=scratch_shapes, compiler_params=compiler_params)(*args)
        return outs, []
    n_in, n_out, n_sc = len(in_specs), len(out_specs), len(scratch_shapes)
    n_rin, n_rout = len(rider.inputs), len(rider.out_shape)

    def wrapped(*refs):
        pos = 0
        groups = []
        for n in (n_in, n_rin, n_out, n_rout, n_sc, 2):
            groups.append(refs[pos:pos + n])
            pos += n
        ins, rins, outs, routs, scratch, sems = groups
        ids = [pl.program_id(a) for a in range(len(grid))]
        first = functools.reduce(jnp.logical_and, [i == 0 for i in ids])
        last = functools.reduce(jnp.logical_and, [i == g - 1 for i, g in zip(ids, grid)])
        pl.when(first)(lambda: rider.start(rins, routs, *sems))
        body(*ins, *outs, *scratch)
        pl.when(last)(lambda: rider.finish(rins, routs, *sems))

    outs = pl.pallas_call(
        wrapped, name=name, grid=grid, in_specs=list(in_specs) + [_hbm_spec()] * n_rin,
        out_specs=list(out_specs) + [_hbm_spec()] * n_rout, out_shape=list(out_shape) + list(rider.out_shape),
        scratch_shapes=list(scratch_shapes) + [pltpu.SemaphoreType.DMA((rider.n_sems,))] * 2,
        compiler_params=compiler_params)(*args, *rider.inputs)
    return outs[:n_out], outs[n_out:]


def _run_rider(rider, *, name):
    n_rin = len(rider.inputs)

    def body(*refs):
        rins, routs, sems = refs[:n_rin], refs[n_rin:-2], refs[-2:]
        rider.start(rins, routs, *sems)
        rider.finish(rins, routs, *sems)

    return pl.pallas_call(
        body, name=name, in_specs=[_hbm_spec()] * n_rin, out_specs=[_hbm_spec()] * len(rider.out_shape),
        out_shape=rider.out_shape, scratch_shapes=[pltpu.SemaphoreType.DMA((rider.n_sems,))] * 2)(*rider.inputs)


def _gather_rider(shards, modes):
    n_arr = len(shards)
    out_shape = [jax.ShapeDtypeStruct((s.shape[0], N_CHIPS * s.shape[1]) if m == "cols" else (N_CHIPS,) + s.shape, s.dtype)
                 for s, m in zip(shards, modes)]
    per = 4

    def copies(srcs, dsts, send_sems, recv_sems):
        x, y, c, chips = _mesh_place()
        me = 2 * x + y
        sends, waits = [], []
        for i in range(n_arr):
            r, n = shards[i].shape
            rows = _half_rows(c, r, 16)

            def copy(slot, src, dst, to, i=i):
                return pltpu.make_async_remote_copy(src_ref=src, dst_ref=dst, send_sem=send_sems.at[i * per + slot],
                                                    recv_sem=recv_sems.at[i * per + slot], device_id=to, device_id_type=MESH)

            own = _part(dsts[i], modes[i], me, n)
            sends.append(copy(0, srcs[i], own, (x, y, 1 - c)))
            waits.append(copy(0, own, own, (x, y, 1 - c)))
            for j, (px, py) in enumerate(chips):
                sends.append(copy(1 + j, srcs[i].at[rows], _part(dsts[i], modes[i], me, n, rows), (px, py, c)))
                theirs = _part(dsts[i], modes[i], 2 * px + py, n, rows)
                waits.append(copy(1 + j, theirs, theirs, (px, py, c)))
        return sends, waits

    def start(*refs):
        for cp in copies(*refs)[0]:
            cp.start()

    def finish(*refs):
        sends, waits = copies(*refs)
        for cp in waits:
            cp.wait_recv()
        for cp in sends:
            cp.wait_send()

    return _Rider(list(shards), out_shape, per * n_arr, start, finish)


def _gather_forward(dsts, shard_shapes, modes, *, name):
    n_arr = len(dsts)

    def body(*refs):
        outs = refs[n_arr:2 * n_arr]
        send_sems, recv_sems = refs[2 * n_arr:]
        x, y, c, chips = _mesh_place()
        cps = []
        for i in range(n_arr):
            r, n = shard_shapes[i]
            for j, (px, py) in enumerate(chips):
                def view(hc, i=i, px=px, py=py, r=r, n=n):
                    return _part(outs[i], modes[i], 2 * px + py, n, _half_rows(hc, r, 16))

                def copy(ref, i=i, j=j):
                    return pltpu.make_async_remote_copy(src_ref=ref, dst_ref=ref, send_sem=send_sems.at[3 * i + j],
                                                        recv_sem=recv_sems.at[3 * i + j], device_id=(x, y, 1 - c), device_id_type=MESH)

                cps.append((copy(view(c)), copy(view(1 - c))))
        for send, _ in cps:
            send.start()
        for send, theirs in cps:
            theirs.wait_recv()
            send.wait_send()

    return pl.pallas_call(
        body, name=name, in_specs=[_hbm_spec()] * n_arr, out_specs=[_hbm_spec()] * n_arr,
        out_shape=[jax.ShapeDtypeStruct(d.shape, d.dtype) for d in dsts],
        input_output_aliases={i: i for i in range(n_arr)},
        scratch_shapes=[pltpu.SemaphoreType.DMA((3 * n_arr,)), pltpu.SemaphoreType.DMA((3 * n_arr,))])(*dsts)


def _blk_view(a, mode):
    return a[None] if mode == "cols" else a


def _rs_pair_swap(arrs, modes, *, name):
    n_arr = len(arrs)
    out_shape = [jax.ShapeDtypeStruct((a.shape[0] // 2, a.shape[1]) if m == "cols" else (a.shape[0], a.shape[1] // 2, a.shape[2]), a.dtype)
                 for a, m in zip(arrs, modes)]

    def body(*refs):
        srcs, dsts = refs[:n_arr], refs[n_arr:2 * n_arr]
        send_sems, recv_sems = refs[2 * n_arr:]
        x, y, c, _ = _mesh_place()
        cps = []
        for i in range(n_arr):
            if modes[i] == "cols":
                src = srcs[i].at[_half_rows(1 - c, arrs[i].shape[0], 8)]
            else:
                src = srcs[i].at[:, _half_rows(1 - c, arrs[i].shape[1], 8), :]
            cps.append(pltpu.make_async_remote_copy(src_ref=src, dst_ref=dsts[i], send_sem=send_sems.at[i],
                                                    recv_sem=recv_sems.at[i], device_id=(x, y, 1 - c), device_id_type=MESH))
        for cp in cps:
            cp.start()
        for cp in cps:
            cp.wait()

    return pl.pallas_call(
        body, name=name, in_specs=[_hbm_spec()] * n_arr, out_specs=[_hbm_spec()] * n_arr, out_shape=out_shape,
        scratch_shapes=[pltpu.SemaphoreType.DMA((n_arr,)), pltpu.SemaphoreType.DMA((n_arr,))])(*arrs)


def _rs_pair_add(arr, landed, place, *, name):
    nb, r, c = arr.shape
    rh = r // 2
    tr = _row_tile(rh, c)
    nt = rh // tr

    def body(p_ref, a_ref, l_ref, o_ref):
        o_ref[...] = (a_ref[...] + l_ref[...]).astype(BF16)

    grid_spec = pltpu.PrefetchScalarGridSpec(
        num_scalar_prefetch=1, grid=(nb, nt),
        in_specs=[pl.BlockSpec((None, tr, c), lambda b, t, p_ref: (b, p_ref[1] * nt + t, 0)),
                  pl.BlockSpec((None, tr, c), lambda b, t, p_ref: (b, t, 0))],
        out_specs=pl.BlockSpec((None, tr, c), lambda b, t, p_ref: (b, t, 0)))
    return pl.pallas_call(
        body, name=name, grid_spec=grid_spec, out_shape=jax.ShapeDtypeStruct((nb, rh, c), BF16),
        compiler_params=_cp("arbitrary", "arbitrary"))(place, arr, landed)


def _exchange_rider(parts, modes):
    n_arr = len(parts)
    out_shape = []
    for a, m in zip(parts, modes):
        shp = (a.shape[0], a.shape[1] // N_CHIPS) if m == "cols" else a.shape[1:]
        out_shape.append(jax.ShapeDtypeStruct((3,) + shp, a.dtype))

    def copies(srcs, dsts, send_sems, recv_sems):
        x, y, c, chips = _mesh_place()
        cps = []
        for i in range(n_arr):
            n = out_shape[i].shape[-1]
            for j, (px, py) in enumerate(chips):
                cps.append(pltpu.make_async_remote_copy(
                    src_ref=_part(srcs[i], modes[i], 2 * px + py, n), dst_ref=dsts[i].at[j],
                    send_sem=send_sems.at[3 * i + j], recv_sem=recv_sems.at[3 * i + j],
                    device_id=(px, py, c), device_id_type=MESH))
        return cps

    def start(*refs):
        for cp in copies(*refs):
            cp.start()

    def finish(*refs):
        for cp in copies(*refs):
            cp.wait()

    return _Rider(list(parts), out_shape, 3 * n_arr, start, finish)


def _rs_chip_sum(part, landed, mode, place, *, name):
    _, rh, n = landed.shape
    tr = _row_tile(rh, n)
    nt = rh // tr

    def body(p_ref, a_ref, l_ref, o_ref):
        o_ref[...] = ((a_ref[...].astype(F32) + l_ref[0].astype(F32)) + l_ref[1].astype(F32)) + l_ref[2].astype(F32)

    if mode == "cols":
        own = pl.BlockSpec((tr, n), lambda t, p_ref: (t, p_ref[0]))
    else:
        own = pl.BlockSpec((None, tr, n), lambda t, p_ref: (p_ref[0], t, 0))
    grid_spec = pltpu.PrefetchScalarGridSpec(
        num_scalar_prefetch=1, grid=(nt,),
        in_specs=[own, pl.BlockSpec((3, tr, n), lambda t, p_ref: (0, t, 0))],
        out_specs=pl.BlockSpec((tr, n), lambda t, p_ref: (p_ref[1] * nt + t, 0)))
    return pl.pallas_call(
        body, name=name, grid_spec=grid_spec, out_shape=jax.ShapeDtypeStruct((2 * rh, n), F32),
        compiler_params=_cp("arbitrary"))(place, part, landed)


def _rs_pair_join(halves, *, name):
    n_arr = len(halves)

    def body(*refs):
        outs = refs[n_arr:2 * n_arr]
        send_sems, recv_sems = refs[2 * n_arr:]
        x, y, c, _ = _mesh_place()
        cps = []
        for i in range(n_arr):
            rows = _half_rows(c, halves[i].shape[0], 8)
            cps.append(pltpu.make_async_remote_copy(src_ref=outs[i].at[rows], dst_ref=outs[i].at[rows], send_sem=send_sems.at[i],
                                                    recv_sem=recv_sems.at[i], device_id=(x, y, 1 - c), device_id_type=MESH))
        for cp in cps:
            cp.start()
        for i, cp in enumerate(cps):
            cp.wait_send()
            theirs = outs[i].at[_half_rows(1 - c, halves[i].shape[0], 8)]
            pltpu.make_async_remote_copy(src_ref=theirs, dst_ref=theirs, send_sem=send_sems.at[i], recv_sem=recv_sems.at[i],
                                         device_id=(x, y, 1 - c), device_id_type=MESH).wait_recv()

    return pl.pallas_call(
        body, name=name, in_specs=[_hbm_spec()] * n_arr, out_specs=[_hbm_spec()] * n_arr,
        out_shape=[jax.ShapeDtypeStruct(h.shape, h.dtype) for h in halves],
        input_output_aliases={i: i for i in range(n_arr)},
        scratch_shapes=[pltpu.SemaphoreType.DMA((n_arr,)), pltpu.SemaphoreType.DMA((n_arr,))])(*halves)


def _allreduce_small(v, *, name):
    r, c = v.shape

    def body(v_ref, o_ref, gath, send_sems, recv_sems):
        x, y, cc, _ = _mesh_place()
        me = 4 * x + 2 * y + cc
        gath[me] = v_ref[...]
        cps = []
        for rel in range(1, 8):
            px = 1 - x if rel & 4 else x
            py = 1 - y if rel & 2 else y
            pc = 1 - cc if rel & 1 else cc

            def copy(slot, px=px, py=py, pc=pc, rel=rel):
                return pltpu.make_async_remote_copy(
                    src_ref=v_ref, dst_ref=gath.at[slot], send_sem=send_sems.at[rel - 1],
                    recv_sem=recv_sems.at[rel - 1], device_id=(px, py, pc), device_id_type=MESH)

            cps.append((copy(me), copy(4 * px + 2 * py + pc)))
        for send, _ in cps:
            send.start()
        for send, theirs in cps:
            theirs.wait_recv()
            send.wait_send()
        tot = gath[0]
        for d in range(1, 8):
            tot = tot + gath[d]
        o_ref[...] = tot

    vm = pl.BlockSpec(memory_space=pltpu.VMEM)
    return pl.pallas_call(
        body, name=name, in_specs=[vm], out_specs=vm, out_shape=jax.ShapeDtypeStruct((r, c), F32),
        scratch_shapes=[pltpu.VMEM((8, r, c), F32), pltpu.SemaphoreType.DMA((7,)), pltpu.SemaphoreType.DMA((7,))])(v)


def _to_heads(a, nh, dh, dtype=BF16):
    return a.reshape(a.shape[0], nh, dh).transpose(1, 0, 2).astype(dtype)


def _from_heads(a):
    return a.transpose(1, 0, 2).reshape(a.shape[1], -1)


def _rope_tables(s, reps):
    half = B_ROPE // 2
    inv = ROPE_THETA ** (-jnp.arange(0, B_ROPE, 2, dtype=F32) / B_ROPE)
    ang = jnp.arange(s, dtype=F32)[:, None] * inv[None, :]
    return jnp.tile(jnp.cos(ang), (1, reps)), jnp.tile(jnp.sin(ang), (1, reps))


def _alibi_slopes():
    return 2.0 ** (-8.0 * jnp.arange(1, A_HEADS + 1, dtype=F32) / A_HEADS)


def _ffn_fwd(h, norm, wts, tag, rider=None):
    (gate, up, act, xn), rode = _ffn_up(h, norm, wts["wgu"], name=f"{tag}_up", rider=rider)
    out = _mm_res_fwd(act, wts["wd"], h, scale=FFN_RES_SCALE, name=f"{tag}_down")
    return out, dict(h_in=h, gate=gate, up=up, act=act, xn=xn), rode


def _ffn_bwd(dh, norm, wts, sv, tag, rider=None):
    (dgate, dup), rode = _ffn_down_bwd(dh, wts["wd"], sv["gate"], sv["up"], scale=FFN_RES_SCALE,
                                      name=f"{tag}_down_bwd", rider=rider)
    d_wd = _mm_tn(sv["act"], dh, b_scale=FFN_RES_SCALE, name=f"{tag}_dwd")
    d_wgu = _mm_tn(sv["xn"], [dgate, dup], name=f"{tag}_dwgu")
    dh_in, dnorm = _mm_nt_rmsbwd([(dgate, wts["wgu"], 0), (dup, wts["wgu"], 1)], sv["h_in"], norm, dh,
                                 name=f"{tag}_dx")
    return dh_in, dnorm, d_wgu, d_wd, rode


def _even_weights(w_in, w_uq, w_ukv):
    half = B_ROPE // 2
    base = w_in.shape[1]
    kr1, kr2 = w_in[:, base - B_ROPE:base - half], w_in[:, base - half:]
    w_in_cat = jnp.concatenate([w_in, -kr2, kr1, jnp.zeros((w_in.shape[0], 64), w_in.dtype)], axis=1)
    u3 = w_uq.reshape(w_uq.shape[0], B_HEADS, B_NOPE + B_ROPE)
    nope = u3[:, :, :B_NOPE].reshape(w_uq.shape[0], -1)
    r1 = u3[:, :, B_NOPE:B_NOPE + half].reshape(w_uq.shape[0], -1)
    r2 = u3[:, :, B_NOPE + half:].reshape(w_uq.shape[0], -1)
    w_q_cat = jnp.concatenate([nope, r1, r2, -r2, r1], axis=1)
    return w_in_cat, w_q_cat, w_ukv


def _even_fwd(h, w, i, rider=None):
    s = h.shape[0]
    half = B_ROPE // 2
    ycat, xn = _rms_mm_fwd(h, w["mix_norm"][i:i + 1], w["ev_in_cat"], name="ev_in")
    a_q, a_k, a_v = ycat[:, :512], ycat[:, 512:640], ycat[:, 640:768]
    c_q, c_kv = ycat[:, 768:1024], ycat[:, 1024:1152]
    cos32, sin32 = _rope_tables(s, 2)
    kro = _rope_fwd(ycat[:, 1152:1184], ycat[:, 1184:1216], cos32, sin32, name="ev_k_rope")
    qa, ka, va = _to_heads(a_q, A_HEADS, A_HEAD_DIM), _to_heads(a_k, A_KV_HEADS, A_HEAD_DIM), _to_heads(a_v, A_KV_HEADS, A_HEAD_DIM)
    ss = jnp.stack([_alibi_slopes(), w["ev_sinks"].reshape(-1)])
    oa, lse_a = _swa_fwd_t(qa, ka, va.transpose(0, 2, 1), ss, scale=A_HEAD_DIM ** -0.5, window=WINDOW, name="swa_fwd")
    yq, xn_q = _rms_mm_fwd(c_q, w["ev_cq_norm"], w["ev_q_cat"], name="ev_q_up")
    cos256, sin256 = _rope_tables(s, 2 * B_HEADS)
    qro = _rope_fwd(yq[:, 512:768], yq[:, 768:1024], cos256, sin256, name="ev_q_rope")
    ykv, xn_kv = _rms_mm_fwd(c_kv, w["ev_ckv_norm"], w["ev_ukv"], name="ev_kv_up")
    zq = jnp.zeros((s, B_HEADS, LANES - B_NOPE - B_ROPE), F32)
    qb = jnp.concatenate([yq[:, :512].reshape(s, B_HEADS, B_NOPE), qro[:, :128].reshape(s, B_HEADS, half),
                          qro[:, 128:].reshape(s, B_HEADS, half), zq], axis=-1).transpose(1, 0, 2).astype(BF16)
    kv3 = ykv.reshape(s, B_HEADS, B_NOPE + B_V)
    kb = jnp.concatenate([kv3[:, :, :B_NOPE], jnp.broadcast_to(kro[:, None, :], (s, B_HEADS, B_ROPE)), zq],
                         axis=-1).transpose(1, 0, 2).astype(BF16)
    vb = kv3[:, :, B_NOPE:].transpose(1, 0, 2).astype(BF16)
    (ob, lse_b), rode = _causal_fwd_t(qb, kb, vb.transpose(0, 2, 1), scale=(B_NOPE + B_ROPE) ** -0.5, name="mla_fwd",
                                      tq=512, rider=rider)
    attn = jnp.concatenate([oa.transpose(2, 0, 1).reshape(s, -1), ob.transpose(2, 0, 1).reshape(s, -1)], axis=-1)
    out = _mm_res_fwd(attn, w["ev_out"], h, scale=1.0, name="ev_out")
    sv = dict(h_in=h, xn=xn, c_q=c_q, c_kv=c_kv, xn_q=xn_q, xn_kv=xn_kv, qa=qa, ka=ka, va=va, oa=oa, lse_a=lse_a,
              ss=ss, qb=qb, kb=kb, vb=vb, ob=ob, lse_b=lse_b, attn=attn, cos32=cos32, sin32=sin32,
              cos256=cos256, sin256=sin256)
    return out, sv, rode


def _even_bwd(dh, w, sv, i, rider=None):
    s = dh.shape[0]
    half = B_ROPE // 2
    g = {}
    dattn = _mm_nt(dh, w["ev_out"], name="ev_out_dx")
    g["ev_w_out"] = _mm_tn(sv["attn"], dh, name="ev_out_dw")
    doa = dattn[:, :512].reshape(s, A_HEADS, A_HEAD_DIM).transpose(1, 2, 0).astype(BF16)
    dob = dattn[:, 512:].reshape(s, B_HEADS, B_V).transpose(1, 2, 0).astype(BF16)
    dqa, dka, dva, dsink = _swa_bwd_t(sv["qa"], sv["ka"], sv["va"], sv["oa"], doa, sv["lse_a"], sv["ss"],
                                      scale=A_HEAD_DIM ** -0.5, window=WINDOW, name="swa_bwd")
    g["ev_sinks"] = dsink[:, :, 0, 0].reshape(1, A_HEADS)
    (dqb, dkb, dvb), rode = _causal_bwd_t(sv["qb"], sv["kb"], sv["vb"], sv["ob"], dob, sv["lse_b"],
                                          scale=(B_NOPE + B_ROPE) ** -0.5, name="mla_bwd", tq=512, rider=rider)
    dq_r1 = dqb[:, :, B_NOPE:B_NOPE + half].transpose(1, 0, 2).reshape(s, -1)
    dq_r2 = dqb[:, :, B_NOPE + half:B_NOPE + B_ROPE].transpose(1, 0, 2).reshape(s, -1)
    dq1, dq2 = _rope_bwd(jnp.concatenate([dq_r1, dq_r2], axis=-1)[None], sv["cos256"], sv["sin256"], name="ev_q_rope_bwd")
    dyq = jnp.concatenate([_from_heads(dqb[:, :, :B_NOPE]), dq1, dq2], axis=-1)
    dwq = _mm_tn(sv["xn_q"], dyq, name="ev_q_up_dw")
    dcq, g["ev_cq_norm"] = _mm_nt_rmsbwd([(dyq, w["ev_q_cat"])], sv["c_q"], w["ev_cq_norm"], None, name="ev_q_up_dx")
    kq = sv["c_q"].shape[1]
    d_nope = dwq[:, :512].reshape(kq, B_HEADS, B_NOPE)
    d_r1 = (dwq[:, 512:640] + dwq[:, 896:1024]).reshape(kq, B_HEADS, half)
    d_r2 = (dwq[:, 640:768] - dwq[:, 768:896]).reshape(kq, B_HEADS, half)
    g["ev_w_uq"] = jnp.concatenate([d_nope, d_r1, d_r2], axis=-1).reshape(kq, -1)
    dykv = jnp.concatenate([dkb[:, :, :B_NOPE].transpose(1, 0, 2), dvb.transpose(2, 0, 1)], axis=-1).reshape(s, -1)
    g["ev_w_ukv"] = _mm_tn(sv["xn_kv"], dykv, name="ev_kv_up_dw")
    dckv, g["ev_ckv_norm"] = _mm_nt_rmsbwd([(dykv, w["ev_ukv"])], sv["c_kv"], w["ev_ckv_norm"], None, name="ev_kv_up_dx")
    dk1, dk2 = _rope_bwd(dkb[:, :, B_NOPE:B_NOPE + B_ROPE], sv["cos32"], sv["sin32"], name="ev_k_rope_bwd")
    dycat = jnp.concatenate([_from_heads(dqa), _from_heads(dka), dva.transpose(2, 0, 1).reshape(s, -1),
                             dcq, dckv, dk1, dk2, jnp.zeros((s, 64), F32)], axis=-1)
    dwin = _mm_tn(sv["xn"], dycat, name="ev_in_dw")
    base = 1184
    g["ev_w_in"] = jnp.concatenate([dwin[:, :base - B_ROPE],
                                    dwin[:, base - B_ROPE:base - half] + dwin[:, base + half:base + B_ROPE],
                                    dwin[:, base - half:base] - dwin[:, base:base + half]], axis=-1)
    dh_in, dnorm = _mm_nt_rmsbwd([(dycat, w["ev_in_cat"])], sv["h_in"], w["mix_norm"][i:i + 1], dh, name="ev_in_dx")
    return dh_in, dnorm, g, rode


def _odd_fwd(h, w, i, rider=None):
    s = h.shape[0]
    wd = C_HEADS * C_HEAD_DIM
    y, y_f, xn = _rms_mm_fwd(h, w["mix_norm"][i:i + 1], w["od_in_pad"], name="od_in", tail=LANES)
    scale = C_HEAD_DIM ** -0.5
    ft = y_f[:, :C_HEADS].T
    bf = w["od_b_f"].reshape(C_HEADS, 1)
    cb3 = _fox_gate_fwd(ft, bf, out_scale=-1.0 / scale, name="fox_gate_fwd")
    spare = LANES - C_HEAD_DIM
    q_fill = jnp.zeros((spare,), F32).at[jnp.array([0, 2, 3, 4])].set(1.0)
    k_fill = jnp.zeros((spare,), F32).at[1].set(1.0)

    def widen(cols, fill):
        return jnp.pad(cols.reshape(s, C_HEADS, C_HEAD_DIM), ((0, 0), (0, 0), (0, spare))) + jnp.pad(fill, (C_HEAD_DIM, 0))

    q = widen(y[:, :wd], q_fill).transpose(1, 0, 2).astype(BF16)
    bias = jnp.pad(cb3.transpose(2, 1, 0).astype(F32), ((0, 0), (0, 0), (C_HEAD_DIM + 2, spare - 5)))
    k = (widen(y[:, wd:2 * wd], k_fill) + bias).transpose(1, 0, 2).astype(BF16)
    v = _to_heads(y[:, 2 * wd:3 * wd], C_HEADS, C_HEAD_DIM)
    (o, lse), rode = _causal_fwd_t(q, k, v.transpose(0, 2, 1), scale=scale, name="fox_fwd", tq=512, rider=rider)
    attn = o.transpose(2, 0, 1).reshape(s, -1)
    out = _mm_res_fwd(attn, w["od_out"], h, scale=1.0, name="od_out")
    return out, dict(h_in=h, xn=xn, q=q, k=k, v=v, o=o, lse=lse, ft=ft, bf=bf, attn=attn), rode


def _odd_bwd(dh, w, sv, i, rider=None):
    s = dh.shape[0]
    g = {}
    dattn = _mm_nt(dh, w["od_out"], name="od_out_dx")
    g["od_w_out"] = _mm_tn(sv["attn"], dh, name="od_out_dw")
    do = dattn.reshape(s, C_HEADS, C_HEAD_DIM).transpose(1, 2, 0).astype(BF16)
    scale = C_HEAD_DIM ** -0.5
    (dq, dk, dv), rode = _causal_bwd_t(sv["q"], sv["k"], sv["v"], sv["o"], do, sv["lse"], scale=scale, name="fox_bwd",
                                       tq=512, rider=rider)
    dft, dbf = _fox_gate_bwd(dq[:, :, C_HEAD_DIM + 1], dk[:, :, C_HEAD_DIM], sv["ft"], sv["bf"],
                             inv_scale=1.0 / scale, name="fox_gate_bwd")
    dq, dk = dq[:, :, :C_HEAD_DIM], dk[:, :, :C_HEAD_DIM]
    g["od_b_f"] = dbf.reshape(1, C_HEADS)
    wd = C_HEADS * C_HEAD_DIM
    dqkv = [_from_heads(dq), _from_heads(dk), dv.transpose(2, 0, 1).reshape(s, -1)]
    df = jnp.pad(dft.T, ((0, 0), (0, LANES - C_HEADS)))
    g["od_w_in"] = jnp.concatenate([_mm_tn(sv["xn"], dqkv, name="od_in_dw"),
                                    _mm_tn(sv["xn"], df, name="od_in_dwf")[:, :C_HEADS]], axis=-1)
    pairs = [(d, w["od_in_pad"], j) for j, d in enumerate(dqkv)] + [(df, w["od_in_pad"], 3 * wd // LANES)]
    dh_in, dnorm = _mm_nt_rmsbwd(pairs, sv["h_in"], w["mix_norm"][i:i + 1], dh, name="od_in_dx")
    return dh_in, dnorm, g, rode


def _kernel_weights(full, replicated):
    w = dict(replicated)
    _install_weights(w, {(n, i): a for n, per_layer in full.items() for i, a in enumerate(per_layer)})
    return w


def _install_weights(w, got):
    raw = w.setdefault("raw", {})
    raw.update(got)
    for (n, i), a in got.items():
        if n in ("ffa_w_gate_up", "ffa_w_down", "ffb_w_gate_up", "ffb_w_down"):
            w.setdefault(n[:3], {}).setdefault(i, {})["wgu" if n.endswith("gate_up") else "wd"] = a
        elif n in ("ple_w_gate", "ple_w_proj"):
            w.setdefault("ple_gate" if n.endswith("gate") else "ple_proj", {})[i] = a
    if "ev_in_cat" not in w and all((n, 0) in raw for n in ("ev_w_in", "ev_w_uq", "ev_w_ukv", "ev_w_out")):
        w["ev_in_cat"], w["ev_q_cat"], w["ev_ukv"] = _even_weights(raw["ev_w_in", 0], raw["ev_w_uq", 0], raw["ev_w_ukv", 0])
        w["ev_out"] = raw["ev_w_out", 0]
    if "od_in_pad" not in w and all((n, 0) in raw for n in ("od_w_in", "od_w_out")):
        od_in = raw["od_w_in", 0]
        w["od_in_pad"] = jnp.pad(od_in, ((0, 0), (0, (-od_in.shape[1]) % LANES)))
        w["od_out"] = raw["od_w_out", 0]


def _keys(names, layer):
    return tuple((n, layer) for n in names)


_FFA, _FFB, _PLE = ("ffa_w_gate_up", "ffa_w_down"), ("ffb_w_gate_up", "ffb_w_down"), ("ple_w_gate", "ple_w_proj")
_EV, _OD = ("ev_w_in", "ev_w_uq", "ev_w_ukv", "ev_w_out"), ("od_w_in", "od_w_out")
_GATHER_FIRST = _keys(_FFA, 0)
_GATHER_RIDES = {("ffa", 0): _keys(_EV, 0), ("mix", 0): _keys(_FFB + _PLE, 0) + _keys(_FFA, 1),
                 ("ffb", 0): _keys(_OD, 0), ("mix", 1): _keys(_FFB + _PLE, 1)}
_REDUCE_RIDES = {("mix", 1): _keys(_FFB + _PLE, 1), ("mix", 0): _keys(_FFA, 1) + _keys(_OD, 0) + _keys(_FFB + _PLE, 0),
                 ("ffa", 0): _keys(_EV, 0)}
_REDUCE_LAST = _keys(_FFA, 0)


def _local_step(x, p, tgt, w, ex=None):
    depth = p.shape[0]

    def gather_behind(host, fn, *args):
        keys = None if ex is None else _GATHER_RIDES.get(host)
        res = fn(*args, None if keys is None else ex.gather_rider(keys))
        if keys is not None:
            _install_weights(w, ex.gather_finish(keys, res[-1], name=f"weight_forward_{host[0]}{host[1]}"))
        return res[:-1]

    h = x
    saved = []
    for i in range(depth):
        sv = {}
        h, sv["ffa"] = gather_behind(("ffa", i), _ffn_fwd, h, w["ffa_norm"][i:i + 1], w["ffa"][i], f"ffa{i}")
        h, sv["mix"] = gather_behind(("mix", i), _even_fwd if i % 2 == 0 else _odd_fwd, h, w, i)
        h, sv["ffb"] = gather_behind(("ffb", i), _ffn_fwd, h, w["ffb_norm"][i:i + 1], w["ffb"][i], f"ffb{i}")
        h_in = h
        h, xn, gate, pp = _ple_fwd(h, w["ple_norm"][i:i + 1], w["ple_gate"][i], p[i], w["ple_proj"][i], name=f"ple{i}")
        sv["ple"] = dict(h_in=h_in, xn=xn, gate=gate, pp=pp)
        saved.append(sv)
    loss_vec, dh, d_final = _final_loss(h, w["final_norm"].reshape(1, -1), tgt, name="final_loss")

    per_layer = [dict() for _ in range(depth)]
    mats = {}
    grads = {}

    def reduce_behind(host, fn, *args):
        keys = None if ex is None else _REDUCE_RIDES.get(host)
        state = None if keys is None else ex.reduce_begin(keys, mats, tag=f"{host[0]}{host[1]}")
        res = fn(*args, None if keys is None else state[0])
        if keys is not None:
            ex.reduce_finish(state, res[-1])
        return res[:-1]

    for i in reversed(range(depth)):
        sv, gl = saved[i], per_layer[i]
        dz, dpp = _ple_bwd_elem(dh, sv["ple"]["gate"], sv["ple"]["pp"], name=f"ple{i}_bwd")
        mats["ple_w_gate", i] = _mm_tn(sv["ple"]["xn"], dz, name=f"ple{i}_dwg")
        mats["ple_w_proj", i] = _mm_tn(p[i], dpp, name=f"ple{i}_dwp")
        dh, gl["ple_norm"] = _mm_nt_rmsbwd([(dz, w["ple_gate"][i])], sv["ple"]["h_in"], w["ple_norm"][i:i + 1], dh,
                                           name=f"ple{i}_dx")
        dh, gl["ffb_norm"], mats["ffb_w_gate_up", i], mats["ffb_w_down", i] = reduce_behind(
            ("ffb", i), _ffn_bwd, dh, w["ffb_norm"][i:i + 1], w["ffb"][i], sv["ffb"], f"ffb{i}")
        dh, gl["mix_norm"], gm = reduce_behind(("mix", i), _even_bwd if i % 2 == 0 else _odd_bwd, dh, w, sv["mix"], i)
        for n, g in gm.items():
            if n in REPLICATED:
                grads[n] = g
            else:
                mats[n, 0] = g
        dh, gl["ffa_norm"], mats["ffa_w_gate_up", i], mats["ffa_w_down", i] = reduce_behind(
            ("ffa", i), _ffn_bwd, dh, w["ffa_norm"][i:i + 1], w["ffa"][i], sv["ffa"], f"ffa{i}")
    grads["final_norm"] = d_final.reshape(-1)
    for n in ("ffa_norm", "mix_norm", "ffb_norm", "ple_norm"):
        grads[n] = jnp.concatenate([per_layer[i][n] for i in range(depth)], axis=0)
    if ex is not None:
        ex.reduce(_REDUCE_LAST, mats, tag="last")
    else:
        for n, _ in SHARDED:
            grads[n] = [mats[n, i] for i in range(depth) if (n, i) in mats]
    return loss_vec[0, 0], dh, grads


def _cut_mode(local_shape, axis, ncols):
    return "cols" if axis == 2 and ncols % LANES == 0 else "blk"


class _Exchange:
    def __init__(self, wts):
        self.place = jnp.stack([2 * lax.axis_index("x") + lax.axis_index("y"), lax.axis_index("c")]).astype(jnp.int32)
        self.info = {}
        for n, axis in SHARDED:
            wb = wts[n].astype(BF16)
            mode = _cut_mode(wb.shape, axis, wb.shape[2])
            for i in range(wb.shape[0]):
                self.info[n, i] = dict(shard=wb[i], mode=mode, axis=axis)
        self.halves = {}

    def _modes(self, keys):
        return [self.info[k]["mode"] for k in keys]

    def gather_rider(self, keys):
        return _gather_rider([self.info[k]["shard"] for k in keys], self._modes(keys))

    def gather_finish(self, keys, landed, *, name):
        outs = _gather_forward(landed, [self.info[k]["shard"].shape for k in keys], self._modes(keys), name=name)
        got = {}
        for k, dst in zip(keys, outs):
            if self.info[k]["mode"] == "blk":
                dst = dst.reshape(-1, dst.shape[2]) if self.info[k]["axis"] == 1 else jnp.moveaxis(dst, 0, 1).reshape(dst.shape[1], -1)
            got[k] = dst
        return got

    def gather(self, keys, *, name):
        return self.gather_finish(keys, _run_rider(self.gather_rider(keys), name=name), name=name + "_forward")

    def reduce_begin(self, keys, mats, *, tag):
        modes = self._modes(keys)
        arrs = []
        for k in keys:
            g2, (rr, cc) = mats[k], self.info[k]["shard"].shape
            if self.info[k]["mode"] == "blk":
                g2 = g2.reshape(N_CHIPS, rr, cc) if self.info[k]["axis"] == 1 else g2.reshape(rr, N_CHIPS, cc).transpose(1, 0, 2)
            arrs.append(g2)
        landed = _rs_pair_swap(arrs, modes, name=f"rs_pair_swap_{tag}")
        parts = []
        for (n, i), m, a, l in zip(keys, modes, arrs, landed):
            pt = _rs_pair_add(_blk_view(a, m), _blk_view(l, m), self.place, name=f"rs_pair_add_{n}{i}")
            parts.append(pt[0] if m == "cols" else pt)
        return _exchange_rider(parts, modes), keys, parts

    def reduce_finish(self, state, landed):
        _, keys, parts = state
        for (n, i), m, pt, l in zip(keys, self._modes(keys), parts, landed):
            self.halves[n, i] = _rs_chip_sum(pt, l, m, self.place, name=f"rs_chip_sum_{n}{i}")

    def reduce(self, keys, mats, *, tag):
        state = self.reduce_begin(keys, mats, tag=tag)
        self.reduce_finish(state, _run_rider(state[0], name=f"rs_chip_exchange_{tag}"))

    def join(self, wts):
        keys = list(self.info)
        joined = dict(zip(keys, _rs_pair_join([self.halves[k] for k in keys], name="rs_pair_join")))
        return {n: jnp.stack([joined[n, i] for i in range(wts[n].shape[0])]).reshape(wts[n].shape) for n, _ in SHARDED}


def _small_rows(vals):
    rows = []
    for n in REPLICATED:
        v = vals[n].reshape(-1)
        rows.append(jnp.pad(v, (0, (-v.shape[0]) % FLAT_COLS)).reshape(-1, FLAT_COLS))
    out = jnp.concatenate(rows, axis=0)
    return jnp.pad(out, ((0, (-out.shape[0]) % 8), (0, 0)))


def kernel(x, p, ffa_norm, ffa_w_gate_up, ffa_w_down, mix_norm, ffb_norm, ffb_w_gate_up, ffb_w_down, ple_norm, ple_w_gate, ple_w_proj, ev_w_in, ev_sinks, ev_cq_norm, ev_w_uq, ev_ckv_norm, ev_w_ukv, ev_w_out, od_w_in, od_b_f, od_w_out, final_norm, loss_target, m_ffa_norm, m_ffa_w_gate_up, m_ffa_w_down, m_mix_norm, m_ffb_norm, m_ffb_w_gate_up, m_ffb_w_down, m_ple_norm, m_ple_w_gate, m_ple_w_proj, m_ev_w_in, m_ev_sinks, m_ev_cq_norm, m_ev_w_uq, m_ev_ckv_norm, m_ev_w_ukv, m_ev_w_out, m_od_w_in, m_od_b_f, m_od_w_out, m_final_norm, v_ffa_norm, v_ffa_w_gate_up, v_ffa_w_down, v_mix_norm, v_ffb_norm, v_ffb_w_gate_up, v_ffb_w_down, v_ple_norm, v_ple_w_gate, v_ple_w_proj, v_ev_w_in, v_ev_sinks, v_ev_cq_norm, v_ev_w_uq, v_ev_ckv_norm, v_ev_w_ukv, v_ev_w_out, v_od_w_in, v_od_b_f, v_od_w_out, v_final_norm):
    env = dict(locals())
    wts = {n: env[n] for n in WEIGHT_ORDER}
    mom1 = {n: env["m_" + n] for n in WEIGHT_ORDER}
    mom2 = {n: env["v_" + n] for n in WEIGHT_ORDER}
    ex = _Exchange(wts)

    w = {n: wts[n] for n in REPLICATED}
    _install_weights(w, ex.gather(_GATHER_FIRST, name="weight_gather_first"))

    loss_part, grad_x, grads = _local_step(x[0], p[:, 0], loss_target[0], w, ex)
    loss = lax.psum(loss_part, ("x", "y", "c"))
    gout = ex.join(wts)
    small = _allreduce_small(_small_rows(grads), name="small_allreduce")
    r0 = 0
    for n in REPLICATED:
        size = int(np.prod(wts[n].shape))
        nr = -(-size // FLAT_COLS)
        gout[n] = small[r0:r0 + nr].reshape(-1)[:size].reshape(wts[n].shape)
        r0 += nr

    delta, new_m, new_v = {}, {}, {}
    for n in WEIGHT_ORDER:
        delta[n], new_m[n], new_v[n] = _adamw(wts[n], gout[n], mom1[n], mom2[n], name="adamw_" + n)
    return (loss, grad_x[None], *[gout[n] for n in WEIGHT_ORDER], *[delta[n] for n in WEIGHT_ORDER],
            *[new_m[n] for n in WEIGHT_ORDER], *[new_v[n] for n in WEIGHT_ORDER])
```

```python
import functools
import math

import numpy as np
import jax
import jax.numpy as jnp
from jax import lax
from jax.experimental import pallas as pl
from jax.experimental.pallas import tpu as pltpu

F32 = jnp.float32
BF16 = jnp.bfloat16
NT = (((1,), (1,)), ((), ()))
TN = (((0,), (0,)), ((), ()))
MESH = pl.DeviceIdType.MESH

RMS_EPS = 1e-6
FFN_RES_SCALE = 0.5
A_HEADS, A_KV_HEADS, A_HEAD_DIM, WINDOW = 8, 2, 64, 128
B_HEADS, B_Q_LORA, B_KV_LORA, B_NOPE, B_ROPE, B_V = 8, 256, 128, 64, 32, 64
ROPE_THETA = 10000.0
C_HEADS, C_HEAD_DIM = 16, 64
ADAM_LR, ADAM_B1, ADAM_B2, ADAM_EPS, ADAM_WD, ADAM_STEP = 0.001, 0.9, 0.999, 1e-08, 0.01, 10

N_CHIPS = 4
LANES = 128
FLAT_COLS = 1024
MASK_VALUE = -1e30
VMEM_LIMIT = 48 * 2**20

SHARDED = (
    ("ffa_w_gate_up", 2), ("ffa_w_down", 1), ("ffb_w_gate_up", 2), ("ffb_w_down", 1),
    ("ple_w_gate", 1), ("ple_w_proj", 2), ("ev_w_in", 2), ("ev_w_uq", 2), ("ev_w_ukv", 2),
    ("ev_w_out", 1), ("od_w_in", 2), ("od_w_out", 1))
REPLICATED = ("ffa_norm", "mix_norm", "ffb_norm", "ple_norm", "final_norm",
              "ev_sinks", "ev_cq_norm", "ev_ckv_norm", "od_b_f")
WEIGHT_ORDER = ("ffa_norm", "ffa_w_gate_up", "ffa_w_down", "mix_norm", "ffb_norm", "ffb_w_gate_up",
                "ffb_w_down", "ple_norm", "ple_w_gate", "ple_w_proj", "ev_w_in", "ev_sinks",
                "ev_cq_norm", "ev_w_uq", "ev_ckv_norm", "ev_w_ukv", "ev_w_out", "od_w_in", "od_b_f",
                "od_w_out", "final_norm")


def _cp(*sem):
    return pltpu.CompilerParams(dimension_semantics=sem, vmem_limit_bytes=VMEM_LIMIT)


def _sigmoid(z):
    return 1.0 / (1.0 + jnp.exp(-z))


def _rms_stats(xv):
    r = lax.rsqrt(jnp.mean(xv * xv, axis=-1, keepdims=True) + RMS_EPS)
    return r, xv * r


def _rms_bwd(dxn, xv, g):
    r, xhat = _rms_stats(xv)
    u = dxn * g
    dx = r * (u - xhat * jnp.mean(u * xhat, axis=-1, keepdims=True))
    return dx, dxn * xhat


def _col_tile(k_rows, n, budget_bytes=6 * 2**20):
    if k_rows * n * 4 <= budget_bytes or n % LANES:
        return n
    units = n // LANES
    best = LANES
    for d in range(1, units + 1):
        if units % d == 0 and k_rows * d * LANES * 4 <= budget_bytes:
            best = d * LANES
    return best


def _row_tile(rows, cols, target_elems=2**18):
    if rows * cols <= target_elems or rows % 8:
        return rows
    best = 8
    for d in range(8, rows + 1, 8):
        if rows % d == 0 and d * cols <= target_elems:
            best = d
    return best


def _rms_mm_fwd(x, g, w, *, name, tm=512, tail=0):
    s, k = x.shape
    n = w.shape[1]
    head = n - tail

    def body(x_ref, g_ref, w_ref, *out_refs):
        _, xhat = _rms_stats(x_ref[...])
        xn = (xhat * g_ref[...]).astype(BF16)
        out_refs[-1][...] = xn
        y = jnp.dot(xn, w_ref[...], preferred_element_type=F32)
        if tail:
            out_refs[0][...] = y[:, :head]
            out_refs[1][...] = y[:, head:]
        else:
            out_refs[0][...] = y

    widths = [head, tail] if tail else [n]
    return pl.pallas_call(
        body, name=name, grid=(s // tm,),
        in_specs=[pl.BlockSpec((tm, k), lambda i: (i, 0)), pl.BlockSpec((1, k), lambda i: (0, 0)),
                  pl.BlockSpec((k, n), lambda i: (0, 0))],
        out_specs=[pl.BlockSpec((tm, c), lambda i: (i, 0)) for c in widths] + [pl.BlockSpec((tm, k), lambda i: (i, 0))],
        out_shape=[jax.ShapeDtypeStruct((s, c), F32) for c in widths] + [jax.ShapeDtypeStruct((s, k), BF16)],
        compiler_params=_cp("arbitrary"))(x, g, w)


def _fox_in_fwd(x, g, w, *, nheads, dh, q_ones, k_ones, name, tm=512):
    s, k = x.shape
    n = w.shape[1]
    wd = nheads * dh
    spare = LANES - dh

    def body(x_ref, g_ref, w_ref, q_ref, k_ref, v_ref, vt_ref, f_ref, xn_ref):
        _, xhat = _rms_stats(x_ref[...])
        xn = (xhat * g_ref[...]).astype(BF16)
        xn_ref[...] = xn
        y = jnp.dot(xn, w_ref[...], preferred_element_type=F32)
        f_ref[...] = y[:, 3 * wd:]
        lane = lax.broadcasted_iota(jnp.int32, (tm, spare), 1)

        def fill(cols):
            return functools.reduce(jnp.logical_or, [lane == c for c in cols]).astype(F32)

        q_fill, k_fill = fill(q_ones), fill(k_ones)
        for h in range(nheads):
            q_ref[h] = jnp.concatenate([y[:, h * dh:(h + 1) * dh], q_fill], axis=-1).astype(BF16)
            k_ref[h] = jnp.concatenate([y[:, wd + h * dh:wd + (h + 1) * dh], k_fill], axis=-1).astype(BF16)
            vh = y[:, 2 * wd + h * dh:2 * wd + (h + 1) * dh]
            v_ref[h] = vh.astype(BF16)
            vt_ref[h] = vh.T.astype(BF16)

    wide = pl.BlockSpec((nheads, tm, LANES), lambda i: (0, i, 0))
    return pl.pallas_call(
        body, name=name, grid=(s // tm,),
        in_specs=[pl.BlockSpec((tm, k), lambda i: (i, 0)), pl.BlockSpec((1, k), lambda i: (0, 0)),
                  pl.BlockSpec((k, n), lambda i: (0, 0))],
        out_specs=[wide, wide, pl.BlockSpec((nheads, tm, dh), lambda i: (0, i, 0)),
                   pl.BlockSpec((nheads, dh, tm), lambda i: (0, 0, i)), pl.BlockSpec((tm, LANES), lambda i: (i, 0)),
                   pl.BlockSpec((tm, k), lambda i: (i, 0))],
        out_shape=[jax.ShapeDtypeStruct((nheads, s, LANES), BF16)] * 2
        + [jax.ShapeDtypeStruct((nheads, s, dh), BF16), jax.ShapeDtypeStruct((nheads, dh, s), BF16),
           jax.ShapeDtypeStruct((s, LANES), F32), jax.ShapeDtypeStruct((s, k), BF16)],
        compiler_params=_cp("arbitrary"))(x, g, w)


def _merge_heads(dq, dk, dvt, *, dh, name, tm=512):
    nheads, s, _ = dq.shape

    def body(dq_ref, dk_ref, dvt_ref, o_ref):
        pieces = [dq_ref[h][:, :dh] for h in range(nheads)] + [dk_ref[h][:, :dh] for h in range(nheads)]
        pieces += [dvt_ref[h].T for h in range(nheads)]
        o_ref[...] = jnp.concatenate(pieces, axis=-1)

    wide = pl.BlockSpec((nheads, tm, LANES), lambda i: (0, i, 0))
    return pl.pallas_call(
        body, name=name, grid=(s // tm,),
        in_specs=[wide, wide, pl.BlockSpec((nheads, dh, tm), lambda i: (0, 0, i))],
        out_specs=pl.BlockSpec((tm, 3 * nheads * dh), lambda i: (i, 0)),
        out_shape=jax.ShapeDtypeStruct((s, 3 * nheads * dh), F32),
        compiler_params=_cp("arbitrary"))(dq, dk, dvt)


def _ffn_up(x, g, wgu, *, name, tm=512, rider=None):
    s, k = x.shape
    f = wgu.shape[1] // 2
    tn = _col_tile(k, f)
    nj = f // tn

    def body(x_ref, g_ref, wg_ref, wu_ref, gate_ref, up_ref, act_ref, xn_ref, xn_sc):
        @pl.when(pl.program_id(1) == 0)
        def _():
            _, xhat = _rms_stats(x_ref[...])
            xn = (xhat * g_ref[...]).astype(BF16)
            xn_sc[...] = xn
            xn_ref[...] = xn

        xn = xn_sc[...]
        gg = jnp.dot(xn, wg_ref[...], preferred_element_type=F32)
        uu = jnp.dot(xn, wu_ref[...], preferred_element_type=F32)
        gate_ref[...] = gg.astype(BF16)
        up_ref[...] = uu.astype(BF16)
        act_ref[...] = ((gg * _sigmoid(gg)) * uu).astype(BF16)

    tile = pl.BlockSpec((tm, tn), lambda i, j: (i, j))
    return _call_with_rider(
        body, rider, name=name, grid=(s // tm, nj),
        in_specs=[pl.BlockSpec((tm, k), lambda i, j: (i, 0)), pl.BlockSpec((1, k), lambda i, j: (0, 0)),
                  pl.BlockSpec((k, tn), lambda i, j: (0, j)), pl.BlockSpec((k, tn), lambda i, j: (0, j + nj))],
        out_specs=[tile, tile, tile, pl.BlockSpec((tm, k), lambda i, j: (i, 0))],
        out_shape=[jax.ShapeDtypeStruct((s, f), BF16)] * 3 + [jax.ShapeDtypeStruct((s, k), BF16)],
        scratch_shapes=[pltpu.VMEM((tm, k), BF16)],
        compiler_params=_cp("arbitrary", "arbitrary"), args=(x, g, wgu, wgu))


def _mm_res_fwd(a, w, res, *, scale, name, tm=512):
    s, k = a.shape
    n = w.shape[1]

    def body(a_ref, w_ref, r_ref, o_ref):
        o_ref[...] = r_ref[...] + scale * jnp.dot(a_ref[...], w_ref[...], preferred_element_type=F32)

    return pl.pallas_call(
        body, name=name, grid=(s // tm,),
        in_specs=[pl.BlockSpec((tm, k), lambda i: (i, 0)), pl.BlockSpec((k, n), lambda i: (0, 0)),
                  pl.BlockSpec((tm, n), lambda i: (i, 0))],
        out_specs=pl.BlockSpec((tm, n), lambda i: (i, 0)),
        out_shape=jax.ShapeDtypeStruct((s, n), F32),
        compiler_params=_cp("arbitrary"))(a, w, res)


def _ffn_down_bwd(dh, wd, gate, up, *, scale, name, tm=512, rider=None):
    s, d = dh.shape
    f = wd.shape[0]
    tn = _col_tile(d, f)

    def body(dh_ref, wd_ref, gate_ref, up_ref, dg_ref, du_ref):
        dhb = (dh_ref[...] * scale).astype(BF16)
        da = lax.dot_general(dhb, wd_ref[...], NT, preferred_element_type=F32)
        gg = gate_ref[...].astype(F32)
        uu = up_ref[...].astype(F32)
        sg = _sigmoid(gg)
        dg_ref[...] = (da * uu * (sg * (1.0 + gg * (1.0 - sg)))).astype(BF16)
        du_ref[...] = (da * (gg * sg)).astype(BF16)

    tile = pl.BlockSpec((tm, tn), lambda i, j: (i, j))
    return _call_with_rider(
        body, rider, name=name, grid=(s // tm, f // tn),
        in_specs=[pl.BlockSpec((tm, d), lambda i, j: (i, 0)), pl.BlockSpec((tn, d), lambda i, j: (j, 0)), tile, tile],
        out_specs=[tile, tile],
        out_shape=[jax.ShapeDtypeStruct((s, f), BF16)] * 2, scratch_shapes=[],
        compiler_params=_cp("arbitrary", "arbitrary"), args=(dh, wd, gate, up))


def _mm_tn(a, bs, *, name, b_scale=1.0, ts=512):
    bs = list(bs) if isinstance(bs, (list, tuple)) else [bs]
    s, k = a.shape
    n = bs[0].shape[1]
    tn = _col_tile(k, n, 12 * 2**20)
    per = n // tn

    def body(a_ref, *refs):
        b_refs, o_ref = refs[:-1], refs[-1]
        j = pl.program_id(0)

        @pl.when(pl.program_id(1) == 0)
        def _():
            o_ref[...] = jnp.zeros_like(o_ref)

        for m, b_ref in enumerate(b_refs):
            def acc(b_ref=b_ref):
                bv = b_ref[...]
                if b_scale != 1.0:
                    bv = bv * b_scale
                o_ref[...] += lax.dot_general(a_ref[...].astype(BF16), bv.astype(BF16), TN, preferred_element_type=F32)

            if len(b_refs) == 1:
                acc()
            else:
                pl.when(jnp.logical_and(j >= m * per, j < (m + 1) * per))(acc)

    def b_spec(m):
        def idx(j, t):
            mine = jnp.logical_and(j >= m * per, j < (m + 1) * per)
            return (jnp.where(mine, t, 0), jnp.clip(j - m * per, 0, per - 1))
        return pl.BlockSpec((ts, tn), idx)

    return pl.pallas_call(
        body, name=name, grid=(per * len(bs), s // ts),
        in_specs=[pl.BlockSpec((ts, k), lambda j, t: (t, 0))] + [b_spec(m) for m in range(len(bs))],
        out_specs=pl.BlockSpec((k, tn), lambda j, t: (0, j)),
        out_shape=jax.ShapeDtypeStruct((k, n * len(bs)), F32),
        compiler_params=_cp("arbitrary", "arbitrary"))(a, *bs)


def _mm_nt(dy, w, *, name, tm=512):
    s, n = dy.shape
    k = w.shape[0]

    def body(dy_ref, w_ref, o_ref):
        o_ref[...] = lax.dot_general(dy_ref[...].astype(BF16), w_ref[...], NT, preferred_element_type=F32)

    return pl.pallas_call(
        body, name=name, grid=(s // tm,),
        in_specs=[pl.BlockSpec((tm, n), lambda i: (i, 0)), pl.BlockSpec((k, n), lambda i: (0, 0))],
        out_specs=pl.BlockSpec((tm, k), lambda i: (i, 0)),
        out_shape=jax.ShapeDtypeStruct((s, k), F32),
        compiler_params=_cp("arbitrary"))(dy, w)


def _mm_nt_rmsbwd(pairs, x, g, dres, *, name, tm=256):
    s, k = x.shape
    npairs = len(pairs)
    pairs = [pr if len(pr) == 3 else (pr[0], pr[1], 0) for pr in pairs]

    def body(*refs):
        dy_refs = refs[0:2 * npairs:2]
        w_refs = refs[1:2 * npairs:2]
        rest = refs[2 * npairs:]
        x_ref, g_ref = rest[0], rest[1]
        if dres is None:
            dx_ref, dg_ref = rest[2], rest[3]
        else:
            dres_ref, dx_ref, dg_ref = rest[2], rest[3], rest[4]
        dxn = None
        for dy_ref, w_ref in zip(dy_refs, w_refs):
            t = lax.dot_general(dy_ref[...].astype(BF16), w_ref[...], NT, preferred_element_type=F32)
            dxn = t if dxn is None else dxn + t
        dx, dgrow = _rms_bwd(dxn, x_ref[...], g_ref[...])
        if dres is not None:
            dx = dx + dres_ref[...]
        dx_ref[...] = dx

        @pl.when(pl.program_id(0) == 0)
        def _():
            dg_ref[...] = jnp.zeros_like(dg_ref)

        dg_ref[...] += jnp.sum(dgrow, axis=0, keepdims=True)

    in_specs, args = [], []
    for dy, w, cb in pairs:
        n = dy.shape[1]
        in_specs += [pl.BlockSpec((tm, n), lambda i: (i, 0)), pl.BlockSpec((k, n), lambda i, cb=cb: (0, cb))]
        args += [dy, w]
    row = pl.BlockSpec((tm, k), lambda i: (i, 0))
    vec = pl.BlockSpec((1, k), lambda i: (0, 0))
    in_specs += [row, vec]
    args += [x, g]
    if dres is not None:
        in_specs.append(row)
        args.append(dres)
    return pl.pallas_call(
        body, name=name, grid=(s // tm,), in_specs=in_specs, out_specs=[row, vec],
        out_shape=[jax.ShapeDtypeStruct((s, k), F32), jax.ShapeDtypeStruct((1, k), F32)],
        compiler_params=_cp("arbitrary"))(*args)


def _ple_fwd(h, g, wg, p, wp, *, name, tm=512):
    s, d = h.shape
    pd = p.shape[1]

    def body(h_ref, g_ref, wg_ref, p_ref, wp_ref, o_ref, xn_ref, gate_ref, pp_ref):
        hv = h_ref[...]
        _, xhat = _rms_stats(hv)
        xn = (xhat * g_ref[...]).astype(BF16)
        xn_ref[...] = xn
        gate = _sigmoid(jnp.dot(xn, wg_ref[...], preferred_element_type=F32))
        pp = jnp.dot(p_ref[...].astype(BF16), wp_ref[...], preferred_element_type=F32)
        gate_ref[...] = gate.astype(BF16)
        pp_ref[...] = pp.astype(BF16)
        o_ref[...] = hv + gate * pp

    row = pl.BlockSpec((tm, d), lambda i: (i, 0))
    return pl.pallas_call(
        body, name=name, grid=(s // tm,),
        in_specs=[row, pl.BlockSpec((1, d), lambda i: (0, 0)), pl.BlockSpec((d, d), lambda i: (0, 0)),
                  pl.BlockSpec((tm, pd), lambda i: (i, 0)), pl.BlockSpec((pd, d), lambda i: (0, 0))],
        out_specs=[row, row, row, row],
        out_shape=[jax.ShapeDtypeStruct((s, d), F32)] + [jax.ShapeDtypeStruct((s, d), BF16)] * 3,
        compiler_params=_cp("arbitrary"))(h, g, wg, p, wp)


def _ple_bwd_elem(dh, gate, pp, *, name, tm=512):
    s, d = dh.shape

    def body(dh_ref, gate_ref, pp_ref, dz_ref, dpp_ref):
        dhv = dh_ref[...]
        gt = gate_ref[...].astype(F32)
        dz_ref[...] = (dhv * pp_ref[...].astype(F32) * (gt * (1.0 - gt))).astype(BF16)
        dpp_ref[...] = (dhv * gt).astype(BF16)

    row = pl.BlockSpec((tm, d), lambda i: (i, 0))
    return pl.pallas_call(
        body, name=name, grid=(s // tm,), in_specs=[row, row, row], out_specs=[row, row],
        out_shape=[jax.ShapeDtypeStruct((s, d), BF16)] * 2,
        compiler_params=_cp("arbitrary"))(dh, gate, pp)


def _final_loss(h, g, tgt, *, name, tm=512):
    s, d = h.shape

    def body(h_ref, g_ref, t_ref, loss_ref, dh_ref, dg_ref):
        @pl.when(pl.program_id(0) == 0)
        def _():
            loss_ref[...] = jnp.zeros_like(loss_ref)
            dg_ref[...] = jnp.zeros_like(dg_ref)

        hv = h_ref[...]
        gv = g_ref[...]
        _, xhat = _rms_stats(hv)
        err = xhat * gv - t_ref[...]
        per_row = jnp.mean(err * err, axis=-1, keepdims=True)
        loss_ref[...] += 0.5 * jnp.sum(per_row, axis=0, keepdims=True)
        dx, dgrow = _rms_bwd(err * (1.0 / d), hv, gv)
        dh_ref[...] = dx
        dg_ref[...] += jnp.sum(dgrow, axis=0, keepdims=True)

    row = pl.BlockSpec((tm, d), lambda i: (i, 0))
    vec = pl.BlockSpec((1, d), lambda i: (0, 0))
    return pl.pallas_call(
        body, name=name, grid=(s // tm,), in_specs=[row, vec, row],
        out_specs=[pl.BlockSpec((1, LANES), lambda i: (0, 0)), row, vec],
        out_shape=[jax.ShapeDtypeStruct((1, LANES), F32), jax.ShapeDtypeStruct((s, d), F32),
                   jax.ShapeDtypeStruct((1, d), F32)],
        compiler_params=_cp("arbitrary"))(h, g, tgt)


def _rope_fwd(y1, y2, cos, sin, *, name, tm=512):
    s, r = y1.shape

    def body(a_ref, b_ref, c_ref, s_ref, o_ref):
        o_ref[...] = a_ref[...] * c_ref[...] + b_ref[...] * s_ref[...]

    row = pl.BlockSpec((tm, r), lambda i: (i, 0))
    return pl.pallas_call(
        body, name=name, grid=(s // tm,), in_specs=[row] * 4, out_specs=row,
        out_shape=jax.ShapeDtypeStruct((s, r), F32), compiler_params=_cp("arbitrary"))(y1, y2, cos, sin)


def _rope_bwd(dout, cos, sin, *, name, tm=512):
    nh, s, r = dout.shape

    def body(d_ref, c_ref, s_ref, o1_ref, o2_ref):
        tot = d_ref[0]
        for hh in range(1, nh):
            tot = tot + d_ref[hh]
        o1_ref[...] = tot * c_ref[...]
        o2_ref[...] = tot * s_ref[...]

    row = pl.BlockSpec((tm, r), lambda i: (i, 0))
    return pl.pallas_call(
        body, name=name, grid=(s // tm,),
        in_specs=[pl.BlockSpec((nh, tm, r), lambda i: (0, i, 0)), row, row], out_specs=[row, row],
        out_shape=[jax.ShapeDtypeStruct((s, r), F32)] * 2, compiler_params=_cp("arbitrary"))(dout, cos, sin)


def _split3(v):
    h1 = v.astype(BF16)
    r1 = v - h1.astype(F32)
    h2 = r1.astype(BF16)
    h3 = (r1 - h2.astype(F32)).astype(BF16)
    return h1, h2, h3


def _tri(tb, upper):
    r = lax.broadcasted_iota(jnp.int32, (tb, tb), 0)
    c = lax.broadcasted_iota(jnp.int32, (tb, tb), 1)
    return jnp.where((r <= c) if upper else (r >= c), 1.0, 0.0).astype(BF16)


def _fox_gate_fwd(ft, bf, *, out_scale, name, tb=512):
    nh, s = ft.shape

    def body(f_ref, b_ref, o_ref, carry):
        @pl.when(pl.program_id(0) == 0)
        def _():
            carry[...] = jnp.zeros_like(carry)

        z = f_ref[...] + b_ref[...]
        lf = jnp.minimum(z, 0.0) - jnp.log(1.0 + jnp.exp(-jnp.abs(z)))
        tri = _tri(tb, True)
        cs = sum(jnp.dot(t, tri, preferred_element_type=F32) for t in _split3(lf)) + carry[...]
        for n, term in enumerate(_split3(cs * out_scale)):
            o_ref[n] = term
        carry[...] += jnp.sum(lf, axis=-1, keepdims=True)

    return pl.pallas_call(
        body, name=name, grid=(s // tb,),
        in_specs=[pl.BlockSpec((nh, tb), lambda t: (0, t)), pl.BlockSpec((nh, 1), lambda t: (0, 0))],
        out_specs=pl.BlockSpec((3, nh, tb), lambda t: (0, 0, t)),
        out_shape=jax.ShapeDtypeStruct((3, nh, s), BF16),
        scratch_shapes=[pltpu.VMEM((nh, 1), F32)], compiler_params=_cp("arbitrary"))(ft, bf)


def _fox_gate_bwd(drow, dcol, ft, bf, *, inv_scale, name, tb=512):
    nh, s = ft.shape
    nb = s // tb

    def body(dr_ref, dc_ref, f_ref, b_ref, df_ref, db_ref, carry):
        @pl.when(pl.program_id(0) == 0)
        def _():
            carry[...] = jnp.zeros_like(carry)
            db_ref[...] = jnp.zeros_like(db_ref)

        dc = (dr_ref[...] - dc_ref[...]) * inv_scale
        tri = _tri(tb, False)
        suf = sum(jnp.dot(t, tri, preferred_element_type=F32) for t in _split3(dc)) + carry[...]
        z = f_ref[...] + b_ref[...]
        dz = suf * (1.0 / (1.0 + jnp.exp(z)))
        df_ref[...] = dz
        db_ref[...] += jnp.sum(dz, axis=-1, keepdims=True)
        carry[...] += jnp.sum(dc, axis=-1, keepdims=True)

    rev = pl.BlockSpec((nh, tb), lambda t: (0, nb - 1 - t))
    one = pl.BlockSpec((nh, 1), lambda t: (0, 0))
    return pl.pallas_call(
        body, name=name, grid=(nb,), in_specs=[rev, rev, rev, one], out_specs=[rev, one],
        out_shape=[jax.ShapeDtypeStruct((nh, s), F32), jax.ShapeDtypeStruct((nh, 1), F32)],
        scratch_shapes=[pltpu.VMEM((nh, 1), F32)], compiler_params=_cp("arbitrary"))(drow, dcol, ft, bf)


def _tri_fwd(t, nq):
    i = sum((t >= (r * (r + 1)) // 2).astype(jnp.int32) for r in range(1, nq))
    return i, t - (i * (i + 1)) // 2


def _tri_bwd(t, nq):
    j = sum((t >= r * nq - (r * (r - 1)) // 2).astype(jnp.int32) for r in range(1, nq))
    return j, j + t - (j * nq - (j * (j - 1)) // 2)


def _scores_t(k, q, *, scale, diag):
    s = lax.dot_general(k, q, NT, preferred_element_type=F32) * scale
    if diag:
        r = lax.broadcasted_iota(jnp.int32, s.shape, 0)
        c = lax.broadcasted_iota(jnp.int32, s.shape, 1)
        s = jnp.where(r <= c, s, MASK_VALUE)
    return s


def _causal_fwd_t(q, k, vt, *, scale, name, tq, hb=2, rider=None):
    nh, s, dq = q.shape
    dv = vt.shape[1]
    nq = s // tq
    nsteps = (nq * (nq + 1)) // 2

    def body(q_ref, k_ref, vt_ref, o_ref, lse_ref, m_sc, l_sc, acc_sc):
        i, j = _tri_fwd(pl.program_id(1), nq)

        @pl.when(j == 0)
        def _():
            m_sc[...] = jnp.full_like(m_sc, MASK_VALUE)
            l_sc[...] = jnp.zeros_like(l_sc)
            acc_sc[...] = jnp.zeros_like(acc_sc)

        def step(diag):
            for u in range(hb):
                sc = _scores_t(k_ref[u], q_ref[u], scale=scale, diag=diag)
                m_prev = m_sc[u]
                m_new = jnp.maximum(m_prev, jnp.max(sc, axis=0, keepdims=True))
                alpha = jnp.exp(m_prev - m_new)
                pr = jnp.exp(sc - m_new)
                l_new = alpha * l_sc[u] + jnp.sum(pr, axis=0, keepdims=True)
                acc = alpha * acc_sc[u] + jnp.dot(vt_ref[u], pr.astype(BF16), preferred_element_type=F32)
                if diag:
                    o_ref[u] = (acc / l_new).astype(BF16)
                    lse_ref[u] = m_new + jnp.log(l_new)
                else:
                    m_sc[u], l_sc[u], acc_sc[u] = m_new, l_new, acc

        pl.when(j < i)(functools.partial(step, False))
        pl.when(j == i)(functools.partial(step, True))

    def qi(t):
        return _tri_fwd(t, nq)[0]

    def kj(t):
        return _tri_fwd(t, nq)[1]

    return _call_with_rider(
        body, rider, name=name, grid=(nh // hb, nsteps),
        in_specs=[pl.BlockSpec((hb, tq, dq), lambda hp, t: (hp, qi(t), 0)),
                  pl.BlockSpec((hb, tq, dq), lambda hp, t: (hp, kj(t), 0)),
                  pl.BlockSpec((hb, dv, tq), lambda hp, t: (hp, 0, kj(t)))],
        out_specs=[pl.BlockSpec((hb, dv, tq), lambda hp, t: (hp, 0, qi(t))),
                   pl.BlockSpec((hb, 1, tq), lambda hp, t: (hp, 0, qi(t)))],
        out_shape=[jax.ShapeDtypeStruct((nh, dv, s), BF16), jax.ShapeDtypeStruct((nh, 1, s), F32)],
        scratch_shapes=[pltpu.VMEM((hb, 1, tq), F32), pltpu.VMEM((hb, 1, tq), F32), pltpu.VMEM((hb, dv, tq), F32)],
        compiler_params=_cp("arbitrary", "arbitrary"), args=(q, k, vt))


def _causal_bwd_t(q, k, v, ot, dot_, lse, *, scale, name, tq, hb=2, rider=None):
    nh, s, dq = q.shape
    dv = v.shape[-1]
    nq = s // tq
    nsteps = (nq * (nq + 1)) // 2

    def body(q_ref, k_ref, v_ref, ot_ref, dot_ref, lse_ref, dq_ref, dk_ref, dvt_ref):
        t = pl.program_id(1)
        j, i = _tri_bwd(t, nq)

        @pl.when(t == 0)
        def _():
            dq_ref[...] = jnp.zeros_like(dq_ref)

        def step(diag):
            rows = pl.ds(pl.multiple_of(i * tq, tq), tq)
            for u in range(hb):
                qv, kv, dov = q_ref[u], k_ref[u], dot_ref[u]
                pr = jnp.exp(_scores_t(kv, qv, scale=scale, diag=diag) - lse_ref[u])
                dp = jnp.dot(v_ref[u], dov, preferred_element_type=F32)
                delta = jnp.sum(dov.astype(F32) * ot_ref[u].astype(F32), axis=0, keepdims=True)
                dsb = ((pr * (dp - delta)) * scale).astype(BF16)
                d_v = lax.dot_general(dov, pr.astype(BF16), NT, preferred_element_type=F32)
                d_k = jnp.dot(dsb, qv, preferred_element_type=F32)
                if diag:
                    dvt_ref[u], dk_ref[u] = d_v, d_k
                else:
                    dvt_ref[u] += d_v
                    dk_ref[u] += d_k
                dq_ref[u, rows, :] += lax.dot_general(dsb, kv, TN, preferred_element_type=F32)

        pl.when(i > j)(functools.partial(step, False))
        pl.when(i == j)(functools.partial(step, True))

    def qi(t):
        return _tri_bwd(t, nq)[1]

    def kj(t):
        return _tri_bwd(t, nq)[0]

    rows_q = pl.BlockSpec((hb, tq, dq), lambda hp, t: (hp, qi(t), 0))
    rows_k = pl.BlockSpec((hb, tq, dq), lambda hp, t: (hp, kj(t), 0))
    lanes_q = pl.BlockSpec((hb, dv, tq), lambda hp, t: (hp, 0, qi(t)))
    return _call_with_rider(
        body, rider, name=name, grid=(nh // hb, nsteps),
        in_specs=[rows_q, rows_k, pl.BlockSpec((hb, tq, dv), lambda hp, t: (hp, kj(t), 0)), lanes_q, lanes_q,
                  pl.BlockSpec((hb, 1, tq), lambda hp, t: (hp, 0, qi(t)))],
        out_specs=[pl.BlockSpec((hb, s, dq), lambda hp, t: (hp, 0, 0)), rows_k,
                   pl.BlockSpec((hb, dv, tq), lambda hp, t: (hp, 0, kj(t)))],
        out_shape=[jax.ShapeDtypeStruct((nh, s, dq), F32), jax.ShapeDtypeStruct((nh, s, dq), F32),
                   jax.ShapeDtypeStruct((nh, dv, s), F32)],
        scratch_shapes=[], compiler_params=_cp("arbitrary", "arbitrary"), args=(q, k, v, ot, dot_, lse))


def _swa_scores_t(k, q, dist, ok, *, scale, slope):
    s = lax.dot_general(k, q, NT, preferred_element_type=F32) * scale - slope * dist.astype(F32)
    return jnp.where(ok, s, MASK_VALUE)


def _swa_geometry(tb, w, has_other):
    r = lax.broadcasted_iota(jnp.int32, (tb, tb), 0)
    c = lax.broadcasted_iota(jnp.int32, (tb, tb), 1)
    d_same = c - r
    ok_same = jnp.logical_and(d_same >= 0, d_same < w)

    def other(ncols):
        rr = lax.broadcasted_iota(jnp.int32, (w, ncols), 0)
        cc = lax.broadcasted_iota(jnp.int32, (w, ncols), 1)
        dd = cc + w - rr
        return dd, jnp.logical_and(dd < w, has_other)

    return (d_same, ok_same), other


def _swa_fwd_t(q, k, vt, slopes_sinks, *, scale, window, name, tb=256):
    nh, s, d = q.shape
    nkv = k.shape[0]
    grp = nh // nkv
    w = window
    per = tb // w
    assert tb % w == 0

    def body(q_ref, kc_ref, kp_ref, vc_ref, vp_ref, ss_ref, o_ref, lse_ref):
        kvh, i = pl.program_id(0), pl.program_id(1)
        (d_c, ok_c), other = _swa_geometry(tb, w, i > 0)
        d_p, ok_p = other(tb)
        for g in range(grp):
            h = kvh * grp + g
            slope, sink = ss_ref[0, h], ss_ref[1, h]
            qg = q_ref[g]
            s_c = _swa_scores_t(kc_ref[...], qg, d_c, ok_c, scale=scale, slope=slope)
            s_p = _swa_scores_t(kp_ref[...], qg, d_p, ok_p, scale=scale, slope=slope)
            m = jnp.maximum(jnp.maximum(jnp.max(s_c, axis=0, keepdims=True), jnp.max(s_p, axis=0, keepdims=True)), sink)
            p_c, p_p = jnp.exp(s_c - m), jnp.exp(s_p - m)
            l = jnp.sum(p_c, axis=0, keepdims=True) + jnp.sum(p_p, axis=0, keepdims=True) + jnp.exp(sink - m)
            acc = (jnp.dot(vc_ref[...], p_c.astype(BF16), preferred_element_type=F32)
                   + jnp.dot(vp_ref[...], p_p.astype(BF16), preferred_element_type=F32))
            o_ref[g] = (acc / l).astype(BF16)
            lse_ref[g] = m + jnp.log(l)

    def prev(i):
        return jnp.maximum(i * per - 1, 0)

    return pl.pallas_call(
        body, name=name, grid=(nkv, s // tb),
        in_specs=[pl.BlockSpec((grp, tb, d), lambda kh, i: (kh, i, 0)),
                  pl.BlockSpec((None, tb, d), lambda kh, i: (kh, i, 0)),
                  pl.BlockSpec((None, w, d), lambda kh, i: (kh, prev(i), 0)),
                  pl.BlockSpec((None, d, tb), lambda kh, i: (kh, 0, i)),
                  pl.BlockSpec((None, d, w), lambda kh, i: (kh, 0, prev(i))),
                  pl.BlockSpec(memory_space=pltpu.SMEM)],
        out_specs=[pl.BlockSpec((grp, d, tb), lambda kh, i: (kh, 0, i)), pl.BlockSpec((grp, 1, tb), lambda kh, i: (kh, 0, i))],
        out_shape=[jax.ShapeDtypeStruct((nh, d, s), BF16), jax.ShapeDtypeStruct((nh, 1, s), F32)],
        compiler_params=_cp("arbitrary", "arbitrary"))(q, k, k, vt, vt, slopes_sinks)


def _swa_bwd_t(q, k, v, ot, dot_, lse, slopes_sinks, *, scale, window, name, tb=256):
    nh, s, d = q.shape
    nkv = k.shape[0]
    grp = nh // nkv
    w = window
    per = tb // w
    nb = s // tb

    def body(qc_ref, qn_ref, kc_ref, kp_ref, vc_ref, vp_ref, oc_ref, on_ref, doc_ref, don_ref, lc_ref, ln_ref, ss_ref,
             dq_ref, dk_ref, dvt_ref, dsink_ref):
        kvh, i = pl.program_id(0), pl.program_id(1)

        @pl.when(i == 0)
        def _():
            dsink_ref[...] = jnp.zeros_like(dsink_ref)

        (d_c, ok_c), other = _swa_geometry(tb, w, i > 0)
        d_p, ok_p = other(tb)
        d_n, ok_n = _swa_geometry(tb, w, i < nb - 1)[1](w)
        kc, kp, vc, vp = kc_ref[...], kp_ref[...], vc_ref[...], vp_ref[...]
        k_last, v_last = kc[tb - w:, :], vc[tb - w:, :]
        dk_acc = jnp.zeros((tb, d), F32)
        dv_acc = jnp.zeros((d, tb), F32)
        dk_tail = jnp.zeros((w, d), F32)
        dv_tail = jnp.zeros((d, w), F32)
        for g in range(grp):
            h = kvh * grp + g
            slope, sink = ss_ref[0, h], ss_ref[1, h]
            qg, dog, lse_c = qc_ref[g], doc_ref[g], lc_ref[g]
            delta = jnp.sum(dog.astype(F32) * oc_ref[g].astype(F32), axis=0, keepdims=True)
            p_c = jnp.exp(_swa_scores_t(kc, qg, d_c, ok_c, scale=scale, slope=slope) - lse_c)
            p_p = jnp.exp(_swa_scores_t(kp, qg, d_p, ok_p, scale=scale, slope=slope) - lse_c)
            ds_c = ((p_c * (jnp.dot(vc, dog, preferred_element_type=F32) - delta)) * scale).astype(BF16)
            ds_p = ((p_p * (jnp.dot(vp, dog, preferred_element_type=F32) - delta)) * scale).astype(BF16)
            dq_ref[g] = (lax.dot_general(ds_c, kc, TN, preferred_element_type=F32)
                         + lax.dot_general(ds_p, kp, TN, preferred_element_type=F32))
            dk_acc += jnp.dot(ds_c, qg, preferred_element_type=F32)
            dv_acc += lax.dot_general(dog, p_c.astype(BF16), NT, preferred_element_type=F32)
            dsink_ref[g] -= jnp.broadcast_to(jnp.sum(jnp.exp(sink - lse_c) * delta, axis=1, keepdims=True), (1, LANES))
            qn, don = qn_ref[g], don_ref[g]
            delta_n = jnp.sum(don.astype(F32) * on_ref[g].astype(F32), axis=0, keepdims=True)
            p_n = jnp.exp(_swa_scores_t(k_last, qn, d_n, ok_n, scale=scale, slope=slope) - ln_ref[g])
            ds_n = ((p_n * (jnp.dot(v_last, don, preferred_element_type=F32) - delta_n)) * scale).astype(BF16)
            dk_tail += jnp.dot(ds_n, qn, preferred_element_type=F32)
            dv_tail += lax.dot_general(don, p_n.astype(BF16), NT, preferred_element_type=F32)
        dk_ref[...] = dk_acc
        dvt_ref[...] = dv_acc
        dk_ref[tb - w:, :] += dk_tail
        dvt_ref[:, tb - w:] += dv_tail

    def prev(i):
        return jnp.maximum(i * per - 1, 0)

    def nxt(i):
        return jnp.minimum((i + 1) * per, s // w - 1)

    return pl.pallas_call(
        body, name=name, grid=(nkv, nb),
        in_specs=[pl.BlockSpec((grp, tb, d), lambda kh, i: (kh, i, 0)),
                  pl.BlockSpec((grp, w, d), lambda kh, i: (kh, nxt(i), 0)),
                  pl.BlockSpec((None, tb, d), lambda kh, i: (kh, i, 0)),
                  pl.BlockSpec((None, w, d), lambda kh, i: (kh, prev(i), 0)),
                  pl.BlockSpec((None, tb, d), lambda kh, i: (kh, i, 0)),
                  pl.BlockSpec((None, w, d), lambda kh, i: (kh, prev(i), 0)),
                  pl.BlockSpec((grp, d, tb), lambda kh, i: (kh, 0, i)),
                  pl.BlockSpec((grp, d, w), lambda kh, i: (kh, 0, nxt(i))),
                  pl.BlockSpec((grp, d, tb), lambda kh, i: (kh, 0, i)),
                  pl.BlockSpec((grp, d, w), lambda kh, i: (kh, 0, nxt(i))),
                  pl.BlockSpec((grp, 1, tb), lambda kh, i: (kh, 0, i)),
                  pl.BlockSpec((grp, 1, w), lambda kh, i: (kh, 0, nxt(i))),
                  pl.BlockSpec(memory_space=pltpu.SMEM)],
        out_specs=[pl.BlockSpec((grp, tb, d), lambda kh, i: (kh, i, 0)),
                   pl.BlockSpec((None, tb, d), lambda kh, i: (kh, i, 0)),
                   pl.BlockSpec((None, d, tb), lambda kh, i: (kh, 0, i)),
                   pl.BlockSpec((None, grp, 1, LANES), lambda kh, i: (kh, 0, 0, 0))],
        out_shape=[jax.ShapeDtypeStruct((nh, s, d), F32), jax.ShapeDtypeStruct((nkv, s, d), F32),
                   jax.ShapeDtypeStruct((nkv, d, s), F32), jax.ShapeDtypeStruct((nkv, grp, 1, LANES), F32)],
        compiler_params=_cp("arbitrary", "arbitrary"))(q, q, k, k, v, v, ot, ot, dot_, dot_, lse, lse, slopes_sinks)


def _adamw(w, g, m, v, *, name):
    shape = w.shape
    cols = shape[-1]
    rows = int(np.prod(shape[:-1])) if len(shape) > 1 else 1
    tr = _row_tile(rows, cols)
    c1 = 1.0 - ADAM_B1 ** ADAM_STEP
    c2 = 1.0 - ADAM_B2 ** ADAM_STEP

    def body(w_ref, g_ref, m_ref, v_ref, d_ref, mo_ref, vo_ref):
        gv = g_ref[...]
        mn = ADAM_B1 * m_ref[...] + (1.0 - ADAM_B1) * gv
        vn = ADAM_B2 * v_ref[...] + (1.0 - ADAM_B2) * (gv * gv)
        mo_ref[...] = mn
        vo_ref[...] = vn
        d_ref[...] = -ADAM_LR * ((mn / c1) / (jnp.sqrt(vn / c2) + ADAM_EPS) + ADAM_WD * w_ref[...])

    blk = pl.BlockSpec((tr, cols), lambda i: (i, 0))
    outs = pl.pallas_call(
        body, name=name, grid=(rows // tr,), in_specs=[blk] * 4, out_specs=[blk] * 3,
        out_shape=[jax.ShapeDtypeStruct((rows, cols), F32)] * 3,
        compiler_params=_cp("arbitrary"))(*[a.reshape(rows, cols) for a in (w, g, m, v)])
    return tuple(a.reshape(shape) for a in outs)


def _hbm_spec():
    return pl.BlockSpec(memory_space=pl.ANY)


def _mesh_place():
    x, y, c = lax.axis_index("x"), lax.axis_index("y"), lax.axis_index("c")
    return x, y, c, [(1 - x, y), (x, 1 - y), (1 - x, 1 - y)]


def _half_rows(c, rows, align):
    return pl.ds(pl.multiple_of(c * (rows // 2), align), rows // 2)


def _part(ref, mode, k, n, rows=None):
    if mode == "cols":
        cols = pl.ds(pl.multiple_of(k * n, LANES), n)
        return ref.at[:, cols] if rows is None else ref.at[rows, cols]
    return ref.at[k] if rows is None else ref.at[k, rows, :]


class _Rider:
    def __init__(self, inputs, out_shape, n_sems, start, finish):
        self.inputs, self.out_shape, self.n_sems, self.start, self.finish = inputs, out_shape, n_sems, start, finish


def _call_with_rider(body, rider, *, name, grid, in_specs, out_specs, out_shape, scratch_shapes, compiler_params, args):
    if rider is None:
        outs = pl.pallas_call(body, name=name, grid=grid, in_specs=in_specs, out_specs=out_specs, out_shape=out_shape,
                              scratch_shapes=scratch_shapes, compiler_params=compiler_params)(*args)
        return outs, []
    n_in, n_out, n_sc = len(in_specs), len(out_specs), len(scratch_shapes)
    n_rin, n_rout = len(rider.inputs), len(rider.out_shape)

    def wrapped(*refs):
        pos = 0
        groups = []
        for n in (n_in, n_rin, n_out, n_rout, n_sc, 2):
            groups.append(refs[pos:pos + n])
            pos += n
        ins, rins, outs, routs, scratch, sems = groups
        ids = [pl.program_id(a) for a in range(len(grid))]
        first = functools.reduce(jnp.logical_and, [i == 0 for i in ids])
        last = functools.reduce(jnp.logical_and, [i == g - 1 for i, g in zip(ids, grid)])
        pl.when(first)(lambda: rider.start(rins, routs, *sems))
        body(*ins, *outs, *scratch)
        pl.when(last)(lambda: rider.finish(rins, routs, *sems))

    outs = pl.pallas_call(
        wrapped, name=name, grid=grid, in_specs=list(in_specs) + [_hbm_spec()] * n_rin,
        out_specs=list(out_specs) + [_hbm_spec()] * n_rout, out_shape=list(out_shape) + list(rider.out_shape),
        scratch_shapes=list(scratch_shapes) + [pltpu.SemaphoreType.DMA((rider.n_sems,))] * 2,
        compiler_params=compiler_params)(*args, *rider.inputs)
    return outs[:n_out], outs[n_out:]


def _run_rider(rider, *, name):
    n_rin = len(rider.inputs)

    def body(*refs):
        rins, routs, sems = refs[:n_rin], refs[n_rin:-2], refs[-2:]
        rider.start(rins, routs, *sems)
        rider.finish(rins, routs, *sems)

    return pl.pallas_call(
        body, name=name, in_specs=[_hbm_spec()] * n_rin, out_specs=[_hbm_spec()] * len(rider.out_shape),
        out_shape=rider.out_shape, scratch_shapes=[pltpu.SemaphoreType.DMA((rider.n_sems,))] * 2)(*rider.inputs)


def _gather_rider(shards, modes):
    n_arr = len(shards)
    out_shape = [jax.ShapeDtypeStruct((s.shape[0], N_CHIPS * s.shape[1]) if m == "cols" else (N_CHIPS,) + s.shape, s.dtype)
                 for s, m in zip(shards, modes)]
    per = 4

    def copies(srcs, dsts, send_sems, recv_sems):
        x, y, c, chips = _mesh_place()
        me = 2 * x + y
        sends, waits = [], []
        for i in range(n_arr):
            r, n = shards[i].shape
            rows = _half_rows(c, r, 16)

            def copy(slot, src, dst, to, i=i):
                return pltpu.make_async_remote_copy(src_ref=src, dst_ref=dst, send_sem=send_sems.at[i * per + slot],
                                                    recv_sem=recv_sems.at[i * per + slot], device_id=to, device_id_type=MESH)

            own = _part(dsts[i], modes[i], me, n)
            sends.append(copy(0, srcs[i], own, (x, y, 1 - c)))
            waits.append(copy(0, own, own, (x, y, 1 - c)))
            for j, (px, py) in enumerate(chips):
                sends.append(copy(1 + j, srcs[i].at[rows], _part(dsts[i], modes[i], me, n, rows), (px, py, c)))
                theirs = _part(dsts[i], modes[i], 2 * px + py, n, rows)
                waits.append(copy(1 + j, theirs, theirs, (px, py, c)))
        return sends, waits

    def start(*refs):
        for cp in copies(*refs)[0]:
            cp.start()

    def finish(*refs):
        sends, waits = copies(*refs)
        for cp in waits:
            cp.wait_recv()
        for cp in sends:
            cp.wait_send()

    return _Rider(list(shards), out_shape, per * n_arr, start, finish)


def _gather_forward(dsts, shard_shapes, modes, *, name):
    n_arr = len(dsts)

    def body(*refs):
        outs = refs[n_arr:2 * n_arr]
        send_sems, recv_sems = refs[2 * n_arr:]
        x, y, c, chips = _mesh_place()
        cps = []
        for i in range(n_arr):
            r, n = shard_shapes[i]
            for j, (px, py) in enumerate(chips):
                def view(hc, i=i, px=px, py=py, r=r, n=n):
                    return _part(outs[i], modes[i], 2 * px + py, n, _half_rows(hc, r, 16))

                def copy(ref, i=i, j=j):
                    return pltpu.make_async_remote_copy(src_ref=ref, dst_ref=ref, send_sem=send_sems.at[3 * i + j],
                                                        recv_sem=recv_sems.at[3 * i + j], device_id=(x, y, 1 - c), device_id_type=MESH)

                cps.append((copy(view(c)), copy(view(1 - c))))
        for send, _ in cps:
            send.start()
        for send, theirs in cps:
            theirs.wait_recv()
            send.wait_send()

    return pl.pallas_call(
        body, name=name, in_specs=[_hbm_spec()] * n_arr, out_specs=[_hbm_spec()] * n_arr,
        out_shape=[jax.ShapeDtypeStruct(d.shape, d.dtype) for d in dsts],
        input_output_aliases={i: i for i in range(n_arr)},
        scratch_shapes=[pltpu.SemaphoreType.DMA((3 * n_arr,)), pltpu.SemaphoreType.DMA((3 * n_arr,))])(*dsts)


def _blk_view(a, mode):
    return a[None] if mode == "cols" else a


def _rs_pair_swap(arrs, modes, *, name):
    n_arr = len(arrs)
    out_shape = [jax.ShapeDtypeStruct((a.shape[0] // 2, a.shape[1]) if m == "cols" else (a.shape[0], a.shape[1] // 2, a.shape[2]), a.dtype)
                 for a, m in zip(arrs, modes)]

    def body(*refs):
        srcs, dsts = refs[:n_arr], refs[n_arr:2 * n_arr]
        send_sems, recv_sems = refs[2 * n_arr:]
        x, y, c, _ = _mesh_place()
        cps = []
        for i in range(n_arr):
            if modes[i] == "cols":
                src = srcs[i].at[_half_rows(1 - c, arrs[i].shape[0], 8)]
            else:
                src = srcs[i].at[:, _half_rows(1 - c, arrs[i].shape[1], 8), :]
            cps.append(pltpu.make_async_remote_copy(src_ref=src, dst_ref=dsts[i], send_sem=send_sems.at[i],
                                                    recv_sem=recv_sems.at[i], device_id=(x, y, 1 - c), device_id_type=MESH))
        for cp in cps:
            cp.start()
        for cp in cps:
            cp.wait()

    return pl.pallas_call(
        body, name=name, in_specs=[_hbm_spec()] * n_arr, out_specs=[_hbm_spec()] * n_arr, out_shape=out_shape,
        scratch_shapes=[pltpu.SemaphoreType.DMA((n_arr,)), pltpu.SemaphoreType.DMA((n_arr,))])(*arrs)


def _rs_pair_add(arr, landed, place, *, name):
    nb, r, c = arr.shape
    rh = r // 2
    tr = _row_tile(rh, c)
    nt = rh // tr

    def body(p_ref, a_ref, l_ref, o_ref):
        o_ref[...] = (a_ref[...] + l_ref[...]).astype(BF16)

    grid_spec = pltpu.PrefetchScalarGridSpec(
        num_scalar_prefetch=1, grid=(nb, nt),
        in_specs=[pl.BlockSpec((None, tr, c), lambda b, t, p_ref: (b, p_ref[1] * nt + t, 0)),
                  pl.BlockSpec((None, tr, c), lambda b, t, p_ref: (b, t, 0))],
        out_specs=pl.BlockSpec((None, tr, c), lambda b, t, p_ref: (b, t, 0)))
    return pl.pallas_call(
        body, name=name, grid_spec=grid_spec, out_shape=jax.ShapeDtypeStruct((nb, rh, c), BF16),
        compiler_params=_cp("arbitrary", "arbitrary"))(place, arr, landed)


def _exchange_rider(parts, modes):
    n_arr = len(parts)
    out_shape = []
    for a, m in zip(parts, modes):
        shp = (a.shape[0], a.shape[1] // N_CHIPS) if m == "cols" else a.shape[1:]
        out_shape.append(jax.ShapeDtypeStruct((3,) + shp, a.dtype))

    def copies(srcs, dsts, send_sems, recv_sems):
        x, y, c, chips = _mesh_place()
        cps = []
        for i in range(n_arr):
            n = out_shape[i].shape[-1]
            for j, (px, py) in enumerate(chips):
                cps.append(pltpu.make_async_remote_copy(
                    src_ref=_part(srcs[i], modes[i], 2 * px + py, n), dst_ref=dsts[i].at[j],
                    send_sem=send_sems.at[3 * i + j], recv_sem=recv_sems.at[3 * i + j],
                    device_id=(px, py, c), device_id_type=MESH))
        return cps

    def start(*refs):
        for cp in copies(*refs):
            cp.start()

    def finish(*refs):
        for cp in copies(*refs):
            cp.wait()

    return _Rider(list(parts), out_shape, 3 * n_arr, start, finish)


def _rs_chip_sum(part, landed, mode, place, *, name):
    _, rh, n = landed.shape
    tr = _row_tile(rh, n)
    nt = rh // tr

    def body(p_ref, a_ref, l_ref, o_ref):
        o_ref[...] = ((a_ref[...].astype(F32) + l_ref[0].astype(F32)) + l_ref[1].astype(F32)) + l_ref[2].astype(F32)

    if mode == "cols":
        own = pl.BlockSpec((tr, n), lambda t, p_ref: (t, p_ref[0]))
    else:
        own = pl.BlockSpec((None, tr, n), lambda t, p_ref: (p_ref[0], t, 0))
    grid_spec = pltpu.PrefetchScalarGridSpec(
        num_scalar_prefetch=1, grid=(nt,),
        in_specs=[own, pl.BlockSpec((3, tr, n), lambda t, p_ref: (0, t, 0))],
        out_specs=pl.BlockSpec((tr, n), lambda t, p_ref: (p_ref[1] * nt + t, 0)))
    return pl.pallas_call(
        body, name=name, grid_spec=grid_spec, out_shape=jax.ShapeDtypeStruct((2 * rh, n), F32),
        compiler_params=_cp("arbitrary"))(place, part, landed)


def _rs_pair_join(halves, *, name):
    n_arr = len(halves)

    def body(*refs):
        outs = refs[n_arr:2 * n_arr]
        send_sems, recv_sems = refs[2 * n_arr:]
        x, y, c, _ = _mesh_place()
        cps = []
        for i in range(n_arr):
            rows = _half_rows(c, halves[i].shape[0], 8)
            cps.append(pltpu.make_async_remote_copy(src_ref=outs[i].at[rows], dst_ref=outs[i].at[rows], send_sem=send_sems.at[i],
                                                    recv_sem=recv_sems.at[i], device_id=(x, y, 1 - c), device_id_type=MESH))
        for cp in cps:
            cp.start()
        for i, cp in enumerate(cps):
            cp.wait_send()
            theirs = outs[i].at[_half_rows(1 - c, halves[i].shape[0], 8)]
            pltpu.make_async_remote_copy(src_ref=theirs, dst_ref=theirs, send_sem=send_sems.at[i], recv_sem=recv_sems.at[i],
                                         device_id=(x, y, 1 - c), device_id_type=MESH).wait_recv()

    return pl.pallas_call(
        body, name=name, in_specs=[_hbm_spec()] * n_arr, out_specs=[_hbm_spec()] * n_arr,
        out_shape=[jax.ShapeDtypeStruct(h.shape, h.dtype) for h in halves],
        input_output_aliases={i: i for i in range(n_arr)},
        scratch_shapes=[pltpu.SemaphoreType.DMA((n_arr,)), pltpu.SemaphoreType.DMA((n_arr,))])(*halves)


def _allreduce_small(v, *, name):
    r, c = v.shape

    def body(v_ref, o_ref, gath, send_sems, recv_sems):
        x, y, cc, _ = _mesh_place()
        me = 4 * x + 2 * y + cc
        gath[me] = v_ref[...]
        cps = []
        for rel in range(1, 8):
            px = 1 - x if rel & 4 else x
            py = 1 - y if rel & 2 else y
            pc = 1 - cc if rel & 1 else cc

            def copy(slot, px=px, py=py, pc=pc, rel=rel):
                return pltpu.make_async_remote_copy(
                    src_ref=v_ref, dst_ref=gath.at[slot], send_sem=send_sems.at[rel - 1],
                    recv_sem=recv_sems.at[rel - 1], device_id=(px, py, pc), device_id_type=MESH)

            cps.append((copy(me), copy(4 * px + 2 * py + pc)))
        for send, _ in cps:
            send.start()
        for send, theirs in cps:
            theirs.wait_recv()
            send.wait_send()
        tot = gath[0]
        for d in range(1, 8):
            tot = tot + gath[d]
        o_ref[...] = tot

    vm = pl.BlockSpec(memory_space=pltpu.VMEM)
    return pl.pallas_call(
        body, name=name, in_specs=[vm], out_specs=vm, out_shape=jax.ShapeDtypeStruct((r, c), F32),
        scratch_shapes=[pltpu.VMEM((8, r, c), F32), pltpu.SemaphoreType.DMA((7,)), pltpu.SemaphoreType.DMA((7,))])(v)


def _to_heads(a, nh, dh, dtype=BF16):
    return a.reshape(a.shape[0], nh, dh).transpose(1, 0, 2).astype(dtype)


def _from_heads(a):
    return a.transpose(1, 0, 2).reshape(a.shape[1], -1)


def _rope_tables(s, reps):
    half = B_ROPE // 2
    inv = ROPE_THETA ** (-jnp.arange(0, B_ROPE, 2, dtype=F32) / B_ROPE)
    ang = jnp.arange(s, dtype=F32)[:, None] * inv[None, :]
    return jnp.tile(jnp.cos(ang), (1, reps)), jnp.tile(jnp.sin(ang), (1, reps))


def _alibi_slopes():
    return 2.0 ** (-8.0 * jnp.arange(1, A_HEADS + 1, dtype=F32) / A_HEADS)


def _ffn_fwd(h, norm, wts, tag, rider=None, on_rode=None):
    (gate, up, act, xn), rode = _ffn_up(h, norm, wts["wgu"], name=f"{tag}_up", rider=rider)
    if on_rode is not None:
        on_rode(rode)
    out = _mm_res_fwd(act, wts["wd"], h, scale=FFN_RES_SCALE, name=f"{tag}_down")
    return out, dict(h_in=h, gate=gate, up=up, act=act, xn=xn), rode


def _ffn_bwd(dh, norm, wts, sv, tag, rider=None):
    (dgate, dup), rode = _ffn_down_bwd(dh, wts["wd"], sv["gate"], sv["up"], scale=FFN_RES_SCALE,
                                      name=f"{tag}_down_bwd", rider=rider)
    d_wd = _mm_tn(sv["act"], dh, b_scale=FFN_RES_SCALE, name=f"{tag}_dwd")
    d_wgu = _mm_tn(sv["xn"], [dgate, dup], name=f"{tag}_dwgu")
    dh_in, dnorm = _mm_nt_rmsbwd([(dgate, wts["wgu"], 0), (dup, wts["wgu"], 1)], sv["h_in"], norm, dh,
                                 name=f"{tag}_dx")
    return dh_in, dnorm, d_wgu, d_wd, rode


def _even_weights(w_in, w_uq, w_ukv):
    half = B_ROPE // 2
    base = w_in.shape[1]
    kr1, kr2 = w_in[:, base - B_ROPE:base - half], w_in[:, base - half:]
    w_in_cat = jnp.concatenate([w_in, -kr2, kr1, jnp.zeros((w_in.shape[0], 64), w_in.dtype)], axis=1)
    u3 = w_uq.reshape(w_uq.shape[0], B_HEADS, B_NOPE + B_ROPE)
    nope = u3[:, :, :B_NOPE].reshape(w_uq.shape[0], -1)
    r1 = u3[:, :, B_NOPE:B_NOPE + half].reshape(w_uq.shape[0], -1)
    r2 = u3[:, :, B_NOPE + half:].reshape(w_uq.shape[0], -1)
    w_q_cat = jnp.concatenate([nope, r1, r2, -r2, r1], axis=1)
    return w_in_cat, w_q_cat, w_ukv


def _even_fwd(h, w, i, rider=None):
    s = h.shape[0]
    half = B_ROPE // 2
    ycat, xn = _rms_mm_fwd(h, w["mix_norm"][i:i + 1], w["ev_in_cat"], name="ev_in")
    a_q, a_k, a_v = ycat[:, :512], ycat[:, 512:640], ycat[:, 640:768]
    c_q, c_kv = ycat[:, 768:1024], ycat[:, 1024:1152]
    cos32, sin32 = _rope_tables(s, 2)
    kro = _rope_fwd(ycat[:, 1152:1184], ycat[:, 1184:1216], cos32, sin32, name="ev_k_rope")
    qa, ka, va = _to_heads(a_q, A_HEADS, A_HEAD_DIM), _to_heads(a_k, A_KV_HEADS, A_HEAD_DIM), _to_heads(a_v, A_KV_HEADS, A_HEAD_DIM)
    ss = jnp.stack([_alibi_slopes(), w["ev_sinks"].reshape(-1)])
    oa, lse_a = _swa_fwd_t(qa, ka, va.transpose(0, 2, 1), ss, scale=A_HEAD_DIM ** -0.5, window=WINDOW, name="swa_fwd")
    yq, xn_q = _rms_mm_fwd(c_q, w["ev_cq_norm"], w["ev_q_cat"], name="ev_q_up")
    cos256, sin256 = _rope_tables(s, 2 * B_HEADS)
    qro = _rope_fwd(yq[:, 512:768], yq[:, 768:1024], cos256, sin256, name="ev_q_rope")
    ykv, xn_kv = _rms_mm_fwd(c_kv, w["ev_ckv_norm"], w["ev_ukv"], name="ev_kv_up")
    zq = jnp.zeros((s, B_HEADS, LANES - B_NOPE - B_ROPE), F32)
    qb = jnp.concatenate([yq[:, :512].reshape(s, B_HEADS, B_NOPE), qro[:, :128].reshape(s, B_HEADS, half),
                          qro[:, 128:].reshape(s, B_HEADS, half), zq], axis=-1).transpose(1, 0, 2).astype(BF16)
    kv3 = ykv.reshape(s, B_HEADS, B_NOPE + B_V)
    kb = jnp.concatenate([kv3[:, :, :B_NOPE], jnp.broadcast_to(kro[:, None, :], (s, B_HEADS, B_ROPE)), zq],
                         axis=-1).transpose(1, 0, 2).astype(BF16)
    vb = kv3[:, :, B_NOPE:].transpose(1, 0, 2).astype(BF16)
    (ob, lse_b), rode = _causal_fwd_t(qb, kb, vb.transpose(0, 2, 1), scale=(B_NOPE + B_ROPE) ** -0.5, name="mla_fwd",
                                      tq=512, rider=rider)
    attn = jnp.concatenate([oa.transpose(2, 0, 1).reshape(s, -1), ob.transpose(2, 0, 1).reshape(s, -1)], axis=-1)
    out = _mm_res_fwd(attn, w["ev_out"], h, scale=1.0, name="ev_out")
    sv = dict(h_in=h, xn=xn, c_q=c_q, c_kv=c_kv, xn_q=xn_q, xn_kv=xn_kv, qa=qa, ka=ka, va=va, oa=oa, lse_a=lse_a,
              ss=ss, qb=qb, kb=kb, vb=vb, ob=ob, lse_b=lse_b, attn=attn, cos32=cos32, sin32=sin32,
              cos256=cos256, sin256=sin256)
    return out, sv, rode


def _even_bwd(dh, w, sv, i, rider=None):
    s = dh.shape[0]
    half = B_ROPE // 2
    g = {}
    dattn = _mm_nt(dh, w["ev_out"], name="ev_out_dx")
    g["ev_w_out"] = _mm_tn(sv["attn"], dh, name="ev_out_dw")
    doa = dattn[:, :512].reshape(s, A_HEADS, A_HEAD_DIM).transpose(1, 2, 0).astype(BF16)
    dob = dattn[:, 512:].reshape(s, B_HEADS, B_V).transpose(1, 2, 0).astype(BF16)
    dqa, dka, dva, dsink = _swa_bwd_t(sv["qa"], sv["ka"], sv["va"], sv["oa"], doa, sv["lse_a"], sv["ss"],
                                      scale=A_HEAD_DIM ** -0.5, window=WINDOW, name="swa_bwd")
    g["ev_sinks"] = dsink[:, :, 0, 0].reshape(1, A_HEADS)
    (dqb, dkb, dvb), rode = _causal_bwd_t(sv["qb"], sv["kb"], sv["vb"], sv["ob"], dob, sv["lse_b"],
                                          scale=(B_NOPE + B_ROPE) ** -0.5, name="mla_bwd", tq=512, rider=rider)
    dq_r1 = dqb[:, :, B_NOPE:B_NOPE + half].transpose(1, 0, 2).reshape(s, -1)
    dq_r2 = dqb[:, :, B_NOPE + half:B_NOPE + B_ROPE].transpose(1, 0, 2).reshape(s, -1)
    dq1, dq2 = _rope_bwd(jnp.concatenate([dq_r1, dq_r2], axis=-1)[None], sv["cos256"], sv["sin256"], name="ev_q_rope_bwd")
    dyq = jnp.concatenate([_from_heads(dqb[:, :, :B_NOPE]), dq1, dq2], axis=-1)
    dwq = _mm_tn(sv["xn_q"], dyq, name="ev_q_up_dw")
    dcq, g["ev_cq_norm"] = _mm_nt_rmsbwd([(dyq, w["ev_q_cat"])], sv["c_q"], w["ev_cq_norm"], None, name="ev_q_up_dx")
    kq = sv["c_q"].shape[1]
    d_nope = dwq[:, :512].reshape(kq, B_HEADS, B_NOPE)
    d_r1 = (dwq[:, 512:640] + dwq[:, 896:1024]).reshape(kq, B_HEADS, half)
    d_r2 = (dwq[:, 640:768] - dwq[:, 768:896]).reshape(kq, B_HEADS, half)
    g["ev_w_uq"] = jnp.concatenate([d_nope, d_r1, d_r2], axis=-1).reshape(kq, -1)
    dykv = jnp.concatenate([dkb[:, :, :B_NOPE].transpose(1, 0, 2), dvb.transpose(2, 0, 1)], axis=-1).reshape(s, -1)
    g["ev_w_ukv"] = _mm_tn(sv["xn_kv"], dykv, name="ev_kv_up_dw")
    dckv, g["ev_ckv_norm"] = _mm_nt_rmsbwd([(dykv, w["ev_ukv"])], sv["c_kv"], w["ev_ckv_norm"], None, name="ev_kv_up_dx")
    dk1, dk2 = _rope_bwd(dkb[:, :, B_NOPE:B_NOPE + B_ROPE], sv["cos32"], sv["sin32"], name="ev_k_rope_bwd")
    dycat = jnp.concatenate([_from_heads(dqa), _from_heads(dka), dva.transpose(2, 0, 1).reshape(s, -1),
                             dcq, dckv, dk1, dk2, jnp.zeros((s, 64), F32)], axis=-1)
    dwin = _mm_tn(sv["xn"], dycat, name="ev_in_dw")
    base = 1184
    g["ev_w_in"] = jnp.concatenate([dwin[:, :base - B_ROPE],
                                    dwin[:, base - B_ROPE:base - half] + dwin[:, base + half:base + B_ROPE],
                                    dwin[:, base - half:base] - dwin[:, base:base + half]], axis=-1)
    dh_in, dnorm = _mm_nt_rmsbwd([(dycat, w["ev_in_cat"])], sv["h_in"], w["mix_norm"][i:i + 1], dh, name="ev_in_dx")
    return dh_in, dnorm, g, rode


def _odd_fwd(h, w, i, rider=None):
    s = h.shape[0]
    wd = C_HEADS * C_HEAD_DIM
    q, k, v, vt, y_f, xn = _fox_in_fwd(h, w["mix_norm"][i:i + 1], w["od_in_pad"], nheads=C_HEADS, dh=C_HEAD_DIM,
                                       q_ones=(0, 2, 3, 4), k_ones=(1,), name="od_in")
    scale = C_HEAD_DIM ** -0.5
    ft = y_f[:, :C_HEADS].T
    bf = w["od_b_f"].reshape(C_HEADS, 1)
    cb3 = _fox_gate_fwd(ft, bf, out_scale=-1.0 / scale, name="fox_gate_fwd")
    k = k + jnp.pad(cb3.transpose(1, 2, 0), ((0, 0), (0, 0), (C_HEAD_DIM + 2, LANES - C_HEAD_DIM - 5)))
    (o, lse), rode = _causal_fwd_t(q, k, vt, scale=scale, name="fox_fwd", tq=512, rider=rider)
    attn = o.transpose(2, 0, 1).reshape(s, -1)
    out = _mm_res_fwd(attn, w["od_out"], h, scale=1.0, name="od_out")
    return out, dict(h_in=h, xn=xn, q=q, k=k, v=v, o=o, lse=lse, ft=ft, bf=bf, attn=attn), rode


def _odd_bwd(dh, w, sv, i, rider=None):
    s = dh.shape[0]
    g = {}
    dattn = _mm_nt(dh, w["od_out"], name="od_out_dx")
    g["od_w_out"] = _mm_tn(sv["attn"], dh, name="od_out_dw")
    do = dattn.reshape(s, C_HEADS, C_HEAD_DIM).transpose(1, 2, 0).astype(BF16)
    scale = C_HEAD_DIM ** -0.5
    (dq, dk, dv), rode = _causal_bwd_t(sv["q"], sv["k"], sv["v"], sv["o"], do, sv["lse"], scale=scale, name="fox_bwd",
                                       tq=512, rider=rider)
    dft, dbf = _fox_gate_bwd(dq[:, :, C_HEAD_DIM + 1], dk[:, :, C_HEAD_DIM], sv["ft"], sv["bf"],
                             inv_scale=1.0 / scale, name="fox_gate_bwd")
    g["od_b_f"] = dbf.reshape(1, C_HEADS)
    wd = C_HEADS * C_HEAD_DIM
    dqkv = _merge_heads(dq, dk, dv, dh=C_HEAD_DIM, name="fox_merge")
    df = jnp.pad(dft.T, ((0, 0), (0, LANES - C_HEADS)))
    g["od_w_in"] = jnp.concatenate([_mm_tn(sv["xn"], dqkv, name="od_in_dw"),
                                    _mm_tn(sv["xn"], df, name="od_in_dwf")[:, :C_HEADS]], axis=-1)
    dh_in, dnorm = _mm_nt_rmsbwd([(dqkv, w["od_in_pad"], 0), (df, w["od_in_pad"], 3 * wd // LANES)],
                                 sv["h_in"], w["mix_norm"][i:i + 1], dh, name="od_in_dx")
    return dh_in, dnorm, g, rode


def _kernel_weights(full, replicated):
    w = dict(replicated)
    _install_weights(w, {(n, i): a for n, per_layer in full.items() for i, a in enumerate(per_layer)})
    return w


def _install_weights(w, got):
    raw = w.setdefault("raw", {})
    raw.update(got)
    for (n, i), a in got.items():
        if n in ("ffa_w_gate_up", "ffa_w_down", "ffb_w_gate_up", "ffb_w_down"):
            w.setdefault(n[:3], {}).setdefault(i, {})["wgu" if n.endswith("gate_up") else "wd"] = a
        elif n in ("ple_w_gate", "ple_w_proj"):
            w.setdefault("ple_gate" if n.endswith("gate") else "ple_proj", {})[i] = a
    if "ev_in_cat" not in w and all((n, 0) in raw for n in ("ev_w_in", "ev_w_uq", "ev_w_ukv", "ev_w_out")):
        w["ev_in_cat"], w["ev_q_cat"], w["ev_ukv"] = _even_weights(raw["ev_w_in", 0], raw["ev_w_uq", 0], raw["ev_w_ukv", 0])
        w["ev_out"] = raw["ev_w_out", 0]
    if "od_in_pad" not in w and all((n, 0) in raw for n in ("od_w_in", "od_w_out")):
        od_in = raw["od_w_in", 0]
        w["od_in_pad"] = jnp.pad(od_in, ((0, 0), (0, (-od_in.shape[1]) % LANES)))
        w["od_out"] = raw["od_w_out", 0]


def _keys(names, layer):
    return tuple((n, layer) for n in names)


_FFA, _FFB, _PLE = ("ffa_w_gate_up", "ffa_w_down"), ("ffb_w_gate_up", "ffb_w_down"), ("ple_w_gate", "ple_w_proj")
_EV, _OD = ("ev_w_in", "ev_w_uq", "ev_w_ukv", "ev_w_out"), ("od_w_in", "od_w_out")
_GATHER_FIRST = _keys(_FFA[:1], 0)
_GATHER_RIDES = {("ffa", 0): _keys(_FFA[1:] + _EV, 0), ("mix", 0): _keys(_FFB + _PLE, 0) + _keys(_FFA, 1),
                 ("ffb", 0): _keys(_OD, 0), ("mix", 1): _keys(_FFB + _PLE, 1)}
_REDUCE_RIDES = {("mix", 1): _keys(_FFB + _PLE, 1), ("mix", 0): _keys(_FFA, 1) + _keys(_OD, 0) + _keys(_FFB + _PLE, 0),
                 ("ffa", 0): _keys(_EV, 0)}
_REDUCE_LAST = _keys(_FFA, 0)


def _local_step(x, p, tgt, w, ex=None):
    depth = p.shape[0]

    def gather_behind(host, fn, *args):
        keys = None if ex is None else _GATHER_RIDES.get(host)
        if keys is None:
            return fn(*args, None)[:-1]
        done = []

        def install(rode):
            if not done:
                _install_weights(w, ex.gather_finish(keys, rode, name=f"weight_forward_{host[0]}{host[1]}"))
                done.append(True)

        res = fn(*args, ex.gather_rider(keys), install) if fn is _ffn_fwd else fn(*args, ex.gather_rider(keys))
        install(res[-1])
        return res[:-1]

    h = x
    saved = []
    for i in range(depth):
        sv = {}
        h, sv["ffa"] = gather_behind(("ffa", i), _ffn_fwd, h, w["ffa_norm"][i:i + 1], w["ffa"][i], f"ffa{i}")
        h, sv["mix"] = gather_behind(("mix", i), _even_fwd if i % 2 == 0 else _odd_fwd, h, w, i)
        h, sv["ffb"] = gather_behind(("ffb", i), _ffn_fwd, h, w["ffb_norm"][i:i + 1], w["ffb"][i], f"ffb{i}")
        h_in = h
        h, xn, gate, pp = _ple_fwd(h, w["ple_norm"][i:i + 1], w["ple_gate"][i], p[i], w["ple_proj"][i], name=f"ple{i}")
        sv["ple"] = dict(h_in=h_in, xn=xn, gate=gate, pp=pp)
        saved.append(sv)
    loss_vec, dh, d_final = _final_loss(h, w["final_norm"].reshape(1, -1), tgt, name="final_loss")

    per_layer = [dict() for _ in range(depth)]
    mats = {}
    grads = {}

    def reduce_behind(host, fn, *args):
        keys = None if ex is None else _REDUCE_RIDES.get(host)
        state = None if keys is None else ex.reduce_begin(keys, mats, tag=f"{host[0]}{host[1]}")
        res = fn(*args, None if keys is None else state[0])
        if keys is not None:
            ex.reduce_finish(state, res[-1])
        return res[:-1]

    for i in reversed(range(depth)):
        sv, gl = saved[i], per_layer[i]
        dz, dpp = _ple_bwd_elem(dh, sv["ple"]["gate"], sv["ple"]["pp"], name=f"ple{i}_bwd")
        mats["ple_w_gate", i] = _mm_tn(sv["ple"]["xn"], dz, name=f"ple{i}_dwg")
        mats["ple_w_proj", i] = _mm_tn(p[i], dpp, name=f"ple{i}_dwp")
        dh, gl["ple_norm"] = _mm_nt_rmsbwd([(dz, w["ple_gate"][i])], sv["ple"]["h_in"], w["ple_norm"][i:i + 1], dh,
                                           name=f"ple{i}_dx")
        dh, gl["ffb_norm"], mats["ffb_w_gate_up", i], mats["ffb_w_down", i] = reduce_behind(
            ("ffb", i), _ffn_bwd, dh, w["ffb_norm"][i:i + 1], w["ffb"][i], sv["ffb"], f"ffb{i}")
        dh, gl["mix_norm"], gm = reduce_behind(("mix", i), _even_bwd if i % 2 == 0 else _odd_bwd, dh, w, sv["mix"], i)
        for n, g in gm.items():
            if n in REPLICATED:
                grads[n] = g
            else:
                mats[n, 0] = g
        dh, gl["ffa_norm"], mats["ffa_w_gate_up", i], mats["ffa_w_down", i] = reduce_behind(
            ("ffa", i), _ffn_bwd, dh, w["ffa_norm"][i:i + 1], w["ffa"][i], sv["ffa"], f"ffa{i}")
    grads["final_norm"] = d_final.reshape(-1)
    for n in ("ffa_norm", "mix_norm", "ffb_norm", "ple_norm"):
        grads[n] = jnp.concatenate([per_layer[i][n] for i in range(depth)], axis=0)
    if ex is not None:
        ex.reduce(_REDUCE_LAST, mats, tag="last")
    else:
        for n, _ in SHARDED:
            grads[n] = [mats[n, i] for i in range(depth) if (n, i) in mats]
    return loss_vec[0, 0], dh, grads


def _cut_mode(local_shape, axis, ncols):
    return "cols" if axis == 2 and ncols % LANES == 0 else "blk"


class _Exchange:
    def __init__(self, wts):
        self.place = jnp.stack([2 * lax.axis_index("x") + lax.axis_index("y"), lax.axis_index("c")]).astype(jnp.int32)
        self.info = {}
        for n, axis in SHARDED:
            wb = wts[n].astype(BF16)
            mode = _cut_mode(wb.shape, axis, wb.shape[2])
            for i in range(wb.shape[0]):
                self.info[n, i] = dict(shard=wb[i], mode=mode, axis=axis)
        self.halves = {}

    def _modes(self, keys):
        return [self.info[k]["mode"] for k in keys]

    def gather_rider(self, keys):
        return _gather_rider([self.info[k]["shard"] for k in keys], self._modes(keys))

    def gather_finish(self, keys, landed, *, name):
        outs = _gather_forward(landed, [self.info[k]["shard"].shape for k in keys], self._modes(keys), name=name)
        got = {}
        for k, dst in zip(keys, outs):
            if self.info[k]["mode"] == "blk":
                dst = dst.reshape(-1, dst.shape[2]) if self.info[k]["axis"] == 1 else jnp.moveaxis(dst, 0, 1).reshape(dst.shape[1], -1)
            got[k] = dst
        return got

    def gather(self, keys, *, name):
        return self.gather_finish(keys, _run_rider(self.gather_rider(keys), name=name), name=name + "_forward")

    def reduce_begin(self, keys, mats, *, tag):
        modes = self._modes(keys)
        arrs = []
        for k in keys:
            g2, (rr, cc) = mats[k], self.info[k]["shard"].shape
            if self.info[k]["mode"] == "blk":
                g2 = g2.reshape(N_CHIPS, rr, cc) if self.info[k]["axis"] == 1 else g2.reshape(rr, N_CHIPS, cc).transpose(1, 0, 2)
            arrs.append(g2)
        landed = _rs_pair_swap(arrs, modes, name=f"rs_pair_swap_{tag}")
        parts = []
        for (n, i), m, a, l in zip(keys, modes, arrs, landed):
            pt = _rs_pair_add(_blk_view(a, m), _blk_view(l, m), self.place, name=f"rs_pair_add_{n}{i}")
            parts.append(pt[0] if m == "cols" else pt)
        return _exchange_rider(parts, modes), keys, parts

    def reduce_finish(self, state, landed):
        _, keys, parts = state
        for (n, i), m, pt, l in zip(keys, self._modes(keys), parts, landed):
            self.halves[n, i] = _rs_chip_sum(pt, l, m, self.place, name=f"rs_chip_sum_{n}{i}")

    def reduce(self, keys, mats, *, tag):
        state = self.reduce_begin(keys, mats, tag=tag)
        self.reduce_finish(state, _run_rider(state[0], name=f"rs_chip_exchange_{tag}"))

    def join(self, wts):
        keys = list(self.info)
        joined = dict(zip(keys, _rs_pair_join([self.halves[k] for k in keys], name="rs_pair_join")))
        return {n: jnp.stack([joined[n, i] for i in range(wts[n].shape[0])]).reshape(wts[n].shape) for n, _ in SHARDED}


def _small_rows(vals):
    rows = []
    for n in REPLICATED:
        v = vals[n].reshape(-1)
        rows.append(jnp.pad(v, (0, (-v.shape[0]) % FLAT_COLS)).reshape(-1, FLAT_COLS))
    out = jnp.concatenate(rows, axis=0)
    return jnp.pad(out, ((0, (-out.shape[0]) % 8), (0, 0)))


def kernel(x, p, ffa_norm, ffa_w_gate_up, ffa_w_down, mix_norm, ffb_norm, ffb_w_gate_up, ffb_w_down, ple_norm, ple_w_gate, ple_w_proj, ev_w_in, ev_sinks, ev_cq_norm, ev_w_uq, ev_ckv_norm, ev_w_ukv, ev_w_out, od_w_in, od_b_f, od_w_out, final_norm, loss_target, m_ffa_norm, m_ffa_w_gate_up, m_ffa_w_down, m_mix_norm, m_ffb_norm, m_ffb_w_gate_up, m_ffb_w_down, m_ple_norm, m_ple_w_gate, m_ple_w_proj, m_ev_w_in, m_ev_sinks, m_ev_cq_norm, m_ev_w_uq, m_ev_ckv_norm, m_ev_w_ukv, m_ev_w_out, m_od_w_in, m_od_b_f, m_od_w_out, m_final_norm, v_ffa_norm, v_ffa_w_gate_up, v_ffa_w_down, v_mix_norm, v_ffb_norm, v_ffb_w_gate_up, v_ffb_w_down, v_ple_norm, v_ple_w_gate, v_ple_w_proj, v_ev_w_in, v_ev_sinks, v_ev_cq_norm, v_ev_w_uq, v_ev_ckv_norm, v_ev_w_ukv, v_ev_w_out, v_od_w_in, v_od_b_f, v_od_w_out, v_final_norm):
    env = dict(locals())
    wts = {n: env[n] for n in WEIGHT_ORDER}
    mom1 = {n: env["m_" + n] for n in WEIGHT_ORDER}
    mom2 = {n: env["v_" + n] for n in WEIGHT_ORDER}
    ex = _Exchange(wts)

    w = {n: wts[n] for n in REPLICATED}
    _install_weights(w, ex.gather(_GATHER_FIRST, name="weight_gather_first"))

    loss_part, grad_x, grads = _local_step(x[0], p[:, 0], loss_target[0], w, ex)
    loss = lax.psum(loss_part, ("x", "y", "c"))
    gout = ex.join(wts)
    small = _allreduce_small(_small_rows(grads), name="small_allreduce")
    r0 = 0
    for n in REPLICATED:
        size = int(np.prod(wts[n].shape))
        nr = -(-size // FLAT_COLS)
        gout[n] = small[r0:r0 + nr].reshape(-1)[:size].reshape(wts[n].shape)
        r0 += nr

    delta, new_m, new_v = {}, {}, {}
    for n in WEIGHT_ORDER:
        delta[n], new_m[n], new_v[n] = _adamw(wts[n], gout[n], mom1[n], mom2[n], name="adamw_" + n)
    return (loss, grad_x[None], *[gout[n] for n in WEIGHT_ORDER], *[delta[n] for n in WEIGHT_ORDER],
            *[new_m[n] for n in WEIGHT_ORDER], *[new_v[n] for n in WEIGHT_ORDER])
```

```python
import functools
import math

import numpy as np
import jax
import jax.numpy as jnp
from jax import lax
from jax.experimental import pallas as pl
from jax.experimental.pallas import tpu as pltpu

F32 = jnp.float32
BF16 = jnp.bfloat16
NT = (((1,), (1,)), ((), ()))
TN = (((0,), (0,)), ((), ()))
MESH = pl.DeviceIdType.MESH

RMS_EPS = 1e-6
FFN_RES_SCALE = 0.5
A_HEADS, A_KV_HEADS, A_HEAD_DIM, WINDOW = 8, 2, 64, 128
B_HEADS, B_Q_LORA, B_KV_LORA, B_NOPE, B_ROPE, B_V = 8, 256, 128, 64, 32, 64
ROPE_THETA = 10000.0
C_HEADS, C_HEAD_DIM = 16, 64
ADAM_LR, ADAM_B1, ADAM_B2, ADAM_EPS, ADAM_WD, ADAM_STEP = 0.001, 0.9, 0.999, 1e-08, 0.01, 10

N_CHIPS = 4
LANES = 128
FLAT_COLS = 1024
MASK_VALUE = -1e30
VMEM_LIMIT = 48 * 2**20

SHARDED = (
    ("ffa_w_gate_up", 2), ("ffa_w_down", 1), ("ffb_w_gate_up", 2), ("ffb_w_down", 1),
    ("ple_w_gate", 1), ("ple_w_proj", 2), ("ev_w_in", 2), ("ev_w_uq", 2), ("ev_w_ukv", 2),
    ("ev_w_out", 1), ("od_w_in", 2), ("od_w_out", 1))
REPLICATED = ("ffa_norm", "mix_norm", "ffb_norm", "ple_norm", "final_norm",
              "ev_sinks", "ev_cq_norm", "ev_ckv_norm", "od_b_f")
WEIGHT_ORDER = ("ffa_norm", "ffa_w_gate_up", "ffa_w_down", "mix_norm", "ffb_norm", "ffb_w_gate_up",
                "ffb_w_down", "ple_norm", "ple_w_gate", "ple_w_proj", "ev_w_in", "ev_sinks",
                "ev_cq_norm", "ev_w_uq", "ev_ckv_norm", "ev_w_ukv", "ev_w_out", "od_w_in", "od_b_f",
                "od_w_out", "final_norm")


def _cp(*sem):
    return pltpu.CompilerParams(dimension_semantics=sem, vmem_limit_bytes=VMEM_LIMIT)


def _sigmoid(z):
    return 1.0 / (1.0 + jnp.exp(-z))


def _rms_stats(xv):
    r = lax.rsqrt(jnp.mean(xv * xv, axis=-1, keepdims=True) + RMS_EPS)
    return r, xv * r


def _rms_bwd(dxn, xv, g):
    r, xhat = _rms_stats(xv)
    u = dxn * g
    dx = r * (u - xhat * jnp.mean(u * xhat, axis=-1, keepdims=True))
    return dx, dxn * xhat


def _col_tile(k_rows, n, budget_bytes=6 * 2**20):
    if k_rows * n * 4 <= budget_bytes or n % LANES:
        return n
    units = n // LANES
    best = LANES
    for d in range(1, units + 1):
        if units % d == 0 and k_rows * d * LANES * 4 <= budget_bytes:
            best = d * LANES
    return best


def _row_tile(rows, cols, target_elems=2**18):
    if rows * cols <= target_elems or rows % 8:
        return rows
    best = 8
    for d in range(8, rows + 1, 8):
        if rows % d == 0 and d * cols <= target_elems:
            best = d
    return best


def _fox_in_fwd(x, g, w, *, nheads, dh, q_ones, k_ones, name, tm=512):
    s, k = x.shape
    n = w.shape[1]
    wd = nheads * dh
    spare = LANES - dh

    def body(x_ref, g_ref, w_ref, q_ref, k_ref, v_ref, vt_ref, f_ref, xn_ref):
        _, xhat = _rms_stats(x_ref[...])
        xn = (xhat * g_ref[...]).astype(BF16)
        xn_ref[...] = xn
        y = jnp.dot(xn, w_ref[...], preferred_element_type=F32)
        f_ref[...] = y[:, 3 * wd:]
        lane = lax.broadcasted_iota(jnp.int32, (tm, spare), 1)

        def fill(cols):
            return functools.reduce(jnp.logical_or, [lane == c for c in cols]).astype(F32)

        q_fill, k_fill = fill(q_ones), fill(k_ones)
        for h in range(nheads):
            q_ref[h] = jnp.concatenate([y[:, h * dh:(h + 1) * dh], q_fill], axis=-1).astype(BF16)
            k_ref[h] = jnp.concatenate([y[:, wd + h * dh:wd + (h + 1) * dh], k_fill], axis=-1).astype(BF16)
            vh = y[:, 2 * wd + h * dh:2 * wd + (h + 1) * dh]
            v_ref[h] = vh.astype(BF16)
            vt_ref[h] = vh.T.astype(BF16)

    wide = pl.BlockSpec((nheads, tm, LANES), lambda i: (0, i, 0))
    return pl.pallas_call(
        body, name=name, grid=(s // tm,),
        in_specs=[pl.BlockSpec((tm, k), lambda i: (i, 0)), pl.BlockSpec((1, k), lambda i: (0, 0)),
                  pl.BlockSpec((k, n), lambda i: (0, 0))],
        out_specs=[wide, wide, pl.BlockSpec((nheads, tm, dh), lambda i: (0, i, 0)),
                   pl.BlockSpec((nheads, dh, tm), lambda i: (0, 0, i)), pl.BlockSpec((tm, LANES), lambda i: (i, 0)),
                   pl.BlockSpec((tm, k), lambda i: (i, 0))],
        out_shape=[jax.ShapeDtypeStruct((nheads, s, LANES), BF16)] * 2
        + [jax.ShapeDtypeStruct((nheads, s, dh), BF16), jax.ShapeDtypeStruct((nheads, dh, s), BF16),
           jax.ShapeDtypeStruct((s, LANES), F32), jax.ShapeDtypeStruct((s, k), BF16)],
        compiler_params=_cp("arbitrary"))(x, g, w)


def _merge_heads(dq, dk, dvt, *, dh, name, tm=512):
    nheads, s, _ = dq.shape

    def body(dq_ref, dk_ref, dvt_ref, o_ref):
        pieces = [dq_ref[h][:, :dh] for h in range(nheads)] + [dk_ref[h][:, :dh] for h in range(nheads)]
        pieces += [dvt_ref[h].T for h in range(nheads)]
        o_ref[...] = jnp.concatenate(pieces, axis=-1)

    wide = pl.BlockSpec((nheads, tm, LANES), lambda i: (0, i, 0))
    return pl.pallas_call(
        body, name=name, grid=(s // tm,),
        in_specs=[wide, wide, pl.BlockSpec((nheads, dh, tm), lambda i: (0, 0, i))],
        out_specs=pl.BlockSpec((tm, 3 * nheads * dh), lambda i: (i, 0)),
        out_shape=jax.ShapeDtypeStruct((s, 3 * nheads * dh), F32),
        compiler_params=_cp("arbitrary"))(dq, dk, dvt)


def _row_call(body, n_rows, ins, outs, *, name, tm=512):
    def spec(a, axis):
        shape = a.shape
        if axis is None:
            return pl.BlockSpec(shape, lambda i: (0,) * len(shape))
        blk = tuple(tm if d == axis else n for d, n in enumerate(shape))
        return pl.BlockSpec(blk, lambda i: tuple(i if d == axis else 0 for d in range(len(shape))))

    return pl.pallas_call(
        body, name=name, grid=(n_rows // tm,), in_specs=[spec(a, ax) for a, ax in ins],
        out_specs=[spec(a, ax) for a, ax in outs], out_shape=[a for a, _ in outs],
        compiler_params=_cp("arbitrary"))(*[a for a, _ in ins])


def _sds(shape, dtype):
    return jax.ShapeDtypeStruct(shape, dtype)


def _ev_in_fwd(x, g, w, *, name):
    s, k = x.shape
    d = A_HEAD_DIM

    def body(x_ref, g_ref, w_ref, q_ref, k_ref, v_ref, vt_ref, cq_ref, ckv_ref, kr_ref, xn_ref):
        _, xhat = _rms_stats(x_ref[...])
        xn = (xhat * g_ref[...]).astype(BF16)
        xn_ref[...] = xn
        y = jnp.dot(xn, w_ref[...], preferred_element_type=F32)
        for h in range(A_HEADS):
            q_ref[h] = y[:, h * d:(h + 1) * d].astype(BF16)
        for h in range(A_KV_HEADS):
            k_ref[h] = y[:, 512 + h * d:512 + (h + 1) * d].astype(BF16)
            vh = y[:, 640 + h * d:640 + (h + 1) * d]
            v_ref[h] = vh.astype(BF16)
            vt_ref[h] = vh.T.astype(BF16)
        cq_ref[...] = y[:, 768:1024]
        ckv_ref[...] = y[:, 1024:1152]
        kr_ref[...] = y[:, 1152:1280]

    return _row_call(
        body, s, [(x, 0), (g, None), (w, None)],
        [(_sds((A_HEADS, s, d), BF16), 1), (_sds((A_KV_HEADS, s, d), BF16), 1), (_sds((A_KV_HEADS, s, d), BF16), 1),
         (_sds((A_KV_HEADS, d, s), BF16), 2), (_sds((s, B_Q_LORA), F32), 0), (_sds((s, B_KV_LORA), F32), 0),
         (_sds((s, LANES), F32), 0), (_sds((s, k), BF16), 0)], name=name)


def _ev_q_fwd(x, g, w, cos, sin, *, name):
    s, k = x.shape
    rot = B_HEADS * B_ROPE

    def body(x_ref, g_ref, w_ref, c_ref, s_ref, q_ref, xn_ref):
        _, xhat = _rms_stats(x_ref[...])
        xn = (xhat * g_ref[...]).astype(BF16)
        xn_ref[...] = xn
        y = jnp.dot(xn, w_ref[...], preferred_element_type=F32)
        ro = y[:, 512:512 + rot] * c_ref[...] + y[:, 512 + rot:] * s_ref[...]
        zero = jnp.zeros((y.shape[0], LANES - B_NOPE - B_ROPE), F32)
        for h in range(B_HEADS):
            q_ref[h] = jnp.concatenate([y[:, h * B_NOPE:(h + 1) * B_NOPE], ro[:, h * B_ROPE:(h + 1) * B_ROPE], zero],
                                       axis=-1).astype(BF16)

    return _row_call(body, s, [(x, 0), (g, None), (w, None), (cos, 0), (sin, 0)],
                     [(_sds((B_HEADS, s, LANES), BF16), 1), (_sds((s, k), BF16), 0)], name=name)


def _ev_kv_fwd(x, g, w, kro, *, name):
    s, k = x.shape
    per = B_NOPE + B_V

    def body(x_ref, g_ref, w_ref, kr_ref, k_ref, v_ref, vt_ref, xn_ref):
        _, xhat = _rms_stats(x_ref[...])
        xn = (xhat * g_ref[...]).astype(BF16)
        xn_ref[...] = xn
        y = jnp.dot(xn, w_ref[...], preferred_element_type=F32)
        kr = kr_ref[...]
        zero = jnp.zeros((y.shape[0], LANES - B_NOPE - B_ROPE), F32)
        for h in range(B_HEADS):
            k_ref[h] = jnp.concatenate([y[:, h * per:h * per + B_NOPE], kr, zero], axis=-1).astype(BF16)
            vh = y[:, h * per + B_NOPE:(h + 1) * per]
            v_ref[h] = vh.astype(BF16)
            vt_ref[h] = vh.T.astype(BF16)

    return _row_call(body, s, [(x, 0), (g, None), (w, None), (kro, 0)],
                     [(_sds((B_HEADS, s, LANES), BF16), 1), (_sds((B_HEADS, s, B_V), BF16), 1),
                      (_sds((B_HEADS, B_V, s), BF16), 2), (_sds((s, k), BF16), 0)], name=name)


def _ev_q_merge(dq, cos, sin, *, name):
    nh, s, _ = dq.shape

    def body(dq_ref, c_ref, s_ref, o_ref):
        dro = jnp.concatenate([dq_ref[h][:, B_NOPE:B_NOPE + B_ROPE] for h in range(nh)], axis=-1)
        o_ref[...] = jnp.concatenate([dq_ref[h][:, :B_NOPE] for h in range(nh)] + [dro * c_ref[...], dro * s_ref[...]], axis=-1)

    return _row_call(body, s, [(dq, 1), (cos, 0), (sin, 0)], [(_sds((s, 2 * nh * B_NOPE), F32), 0)], name=name)[0]


def _ev_kv_merge(dk, dvt, cos, sin, *, name):
    nh, s, _ = dk.shape

    def body(dk_ref, dvt_ref, c_ref, s_ref, o_ref, kr_ref):
        pieces = []
        tot = None
        for h in range(nh):
            pieces += [dk_ref[h][:, :B_NOPE], dvt_ref[h].T]
            rot = dk_ref[h][:, B_NOPE:B_NOPE + B_ROPE]
            tot = rot if tot is None else tot + rot
        o_ref[...] = jnp.concatenate(pieces, axis=-1)
        kr_ref[...] = jnp.concatenate([tot * c_ref[...], tot * s_ref[...], jnp.zeros((tot.shape[0], LANES - 2 * B_ROPE), F32)],
                                      axis=-1)

    return _row_call(body, s, [(dk, 1), (dvt, 2), (cos, 0), (sin, 0)],
                     [(_sds((s, nh * (B_NOPE + B_V)), F32), 0), (_sds((s, LANES), F32), 0)], name=name)


def _ev_in_merge(dq, dk, dvt, dcq, dckv, dkr, *, name):
    s = dcq.shape[0]

    def body(dq_ref, dk_ref, dvt_ref, cq_ref, ckv_ref, kr_ref, o_ref):
        pieces = [dq_ref[h] for h in range(A_HEADS)] + [dk_ref[h] for h in range(A_KV_HEADS)]
        pieces += [dvt_ref[h].T for h in range(A_KV_HEADS)] + [cq_ref[...], ckv_ref[...], kr_ref[...]]
        o_ref[...] = jnp.concatenate(pieces, axis=-1)

    return _row_call(body, s, [(dq, 1), (dk, 1), (dvt, 2), (dcq, 0), (dckv, 0), (dkr, 0)],
                     [(_sds((s, 1280), F32), 0)], name=name)[0]


def _ffn_up(x, g, wgu, *, name, tm=512, rider=None):
    s, k = x.shape
    f = wgu.shape[1] // 2
    tn = _col_tile(k, f)
    nj = f // tn

    def body(x_ref, g_ref, wg_ref, wu_ref, gate_ref, up_ref, act_ref, xn_ref, xn_sc):
        @pl.when(pl.program_id(1) == 0)
        def _():
            _, xhat = _rms_stats(x_ref[...])
            xn = (xhat * g_ref[...]).astype(BF16)
            xn_sc[...] = xn
            xn_ref[...] = xn

        xn = xn_sc[...]
        gg = jnp.dot(xn, wg_ref[...], preferred_element_type=F32)
        uu = jnp.dot(xn, wu_ref[...], preferred_element_type=F32)
        gate_ref[...] = gg.astype(BF16)
        up_ref[...] = uu.astype(BF16)
        act_ref[...] = ((gg * _sigmoid(gg)) * uu).astype(BF16)

    tile = pl.BlockSpec((tm, tn), lambda i, j: (i, j))
    return _call_with_rider(
        body, rider, name=name, grid=(s // tm, nj),
        in_specs=[pl.BlockSpec((tm, k), lambda i, j: (i, 0)), pl.BlockSpec((1, k), lambda i, j: (0, 0)),
                  pl.BlockSpec((k, tn), lambda i, j: (0, j)), pl.BlockSpec((k, tn), lambda i, j: (0, j + nj))],
        out_specs=[tile, tile, tile, pl.BlockSpec((tm, k), lambda i, j: (i, 0))],
        out_shape=[jax.ShapeDtypeStruct((s, f), BF16)] * 3 + [jax.ShapeDtypeStruct((s, k), BF16)],
        scratch_shapes=[pltpu.VMEM((tm, k), BF16)],
        compiler_params=_cp("arbitrary", "arbitrary"), args=(x, g, wgu, wgu))


def _mm_res_fwd(a, w, res, *, scale, name, tm=512):
    s, k = a.shape
    n = w.shape[1]

    def body(a_ref, w_ref, r_ref, o_ref):
        o_ref[...] = r_ref[...] + scale * jnp.dot(a_ref[...], w_ref[...], preferred_element_type=F32)

    return pl.pallas_call(
        body, name=name, grid=(s // tm,),
        in_specs=[pl.BlockSpec((tm, k), lambda i: (i, 0)), pl.BlockSpec((k, n), lambda i: (0, 0)),
                  pl.BlockSpec((tm, n), lambda i: (i, 0))],
        out_specs=pl.BlockSpec((tm, n), lambda i: (i, 0)),
        out_shape=jax.ShapeDtypeStruct((s, n), F32),
        compiler_params=_cp("arbitrary"))(a, w, res)


def _ffn_down_bwd(dh, wd, gate, up, *, scale, name, tm=512, rider=None):
    s, d = dh.shape
    f = wd.shape[0]
    tn = _col_tile(d, f)

    def body(dh_ref, wd_ref, gate_ref, up_ref, dg_ref, du_ref):
        dhb = (dh_ref[...] * scale).astype(BF16)
        da = lax.dot_general(dhb, wd_ref[...], NT, preferred_element_type=F32)
        gg = gate_ref[...].astype(F32)
        uu = up_ref[...].astype(F32)
        sg = _sigmoid(gg)
        dg_ref[...] = (da * uu * (sg * (1.0 + gg * (1.0 - sg)))).astype(BF16)
        du_ref[...] = (da * (gg * sg)).astype(BF16)

    tile = pl.BlockSpec((tm, tn), lambda i, j: (i, j))
    return _call_with_rider(
        body, rider, name=name, grid=(s // tm, f // tn),
        in_specs=[pl.BlockSpec((tm, d), lambda i, j: (i, 0)), pl.BlockSpec((tn, d), lambda i, j: (j, 0)), tile, tile],
        out_specs=[tile, tile],
        out_shape=[jax.ShapeDtypeStruct((s, f), BF16)] * 2, scratch_shapes=[],
        compiler_params=_cp("arbitrary", "arbitrary"), args=(dh, wd, gate, up))


def _mm_tn(a, bs, *, name, b_scale=1.0, ts=512):
    bs = list(bs) if isinstance(bs, (list, tuple)) else [bs]
    s, k = a.shape
    n = bs[0].shape[1]
    tn = _col_tile(k, n, 12 * 2**20)
    per = n // tn

    def body(a_ref, *refs):
        b_refs, o_ref = refs[:-1], refs[-1]
        j = pl.program_id(0)

        @pl.when(pl.program_id(1) == 0)
        def _():
            o_ref[...] = jnp.zeros_like(o_ref)

        for m, b_ref in enumerate(b_refs):
            def acc(b_ref=b_ref):
                bv = b_ref[...]
                if b_scale != 1.0:
                    bv = bv * b_scale
                o_ref[...] += lax.dot_general(a_ref[...].astype(BF16), bv.astype(BF16), TN, preferred_element_type=F32)

            if len(b_refs) == 1:
                acc()
            else:
                pl.when(jnp.logical_and(j >= m * per, j < (m + 1) * per))(acc)

    def b_spec(m):
        def idx(j, t):
            mine = jnp.logical_and(j >= m * per, j < (m + 1) * per)
            return (jnp.where(mine, t, 0), jnp.clip(j - m * per, 0, per - 1))
        return pl.BlockSpec((ts, tn), idx)

    return pl.pallas_call(
        body, name=name, grid=(per * len(bs), s // ts),
        in_specs=[pl.BlockSpec((ts, k), lambda j, t: (t, 0))] + [b_spec(m) for m in range(len(bs))],
        out_specs=pl.BlockSpec((k, tn), lambda j, t: (0, j)),
        out_shape=jax.ShapeDtypeStruct((k, n * len(bs)), F32),
        compiler_params=_cp("arbitrary", "arbitrary"))(a, *bs)


def _mm_nt(dy, w, *, name, tm=512):
    s, n = dy.shape
    k = w.shape[0]

    def body(dy_ref, w_ref, o_ref):
        o_ref[...] = lax.dot_general(dy_ref[...].astype(BF16), w_ref[...], NT, preferred_element_type=F32)

    return pl.pallas_call(
        body, name=name, grid=(s // tm,),
        in_specs=[pl.BlockSpec((tm, n), lambda i: (i, 0)), pl.BlockSpec((k, n), lambda i: (0, 0))],
        out_specs=pl.BlockSpec((tm, k), lambda i: (i, 0)),
        out_shape=jax.ShapeDtypeStruct((s, k), F32),
        compiler_params=_cp("arbitrary"))(dy, w)


def _mm_nt_rmsbwd(pairs, x, g, dres, *, name, tm=256):
    s, k = x.shape
    npairs = len(pairs)
    pairs = [pr if len(pr) == 3 else (pr[0], pr[1], 0) for pr in pairs]

    def body(*refs):
        dy_refs = refs[0:2 * npairs:2]
        w_refs = refs[1:2 * npairs:2]
        rest = refs[2 * npairs:]
        x_ref, g_ref = rest[0], rest[1]
        if dres is None:
            dx_ref, dg_ref = rest[2], rest[3]
        else:
            dres_ref, dx_ref, dg_ref = rest[2], rest[3], rest[4]
        dxn = None
        for dy_ref, w_ref in zip(dy_refs, w_refs):
            t = lax.dot_general(dy_ref[...].astype(BF16), w_ref[...], NT, preferred_element_type=F32)
            dxn = t if dxn is None else dxn + t
        dx, dgrow = _rms_bwd(dxn, x_ref[...], g_ref[...])
        if dres is not None:
            dx = dx + dres_ref[...]
        dx_ref[...] = dx

        @pl.when(pl.program_id(0) == 0)
        def _():
            dg_ref[...] = jnp.zeros_like(dg_ref)

        dg_ref[...] += jnp.sum(dgrow, axis=0, keepdims=True)

    in_specs, args = [], []
    for dy, w, cb in pairs:
        n = dy.shape[1]
        in_specs += [pl.BlockSpec((tm, n), lambda i: (i, 0)), pl.BlockSpec((k, n), lambda i, cb=cb: (0, cb))]
        args += [dy, w]
    row = pl.BlockSpec((tm, k), lambda i: (i, 0))
    vec = pl.BlockSpec((1, k), lambda i: (0, 0))
    in_specs += [row, vec]
    args += [x, g]
    if dres is not None:
        in_specs.append(row)
        args.append(dres)
    return pl.pallas_call(
        body, name=name, grid=(s // tm,), in_specs=in_specs, out_specs=[row, vec],
        out_shape=[jax.ShapeDtypeStruct((s, k), F32), jax.ShapeDtypeStruct((1, k), F32)],
        compiler_params=_cp("arbitrary"))(*args)


def _ple_fwd(h, g, wg, p, wp, *, name, tm=512):
    s, d = h.shape
    pd = p.shape[1]

    def body(h_ref, g_ref, wg_ref, p_ref, wp_ref, o_ref, xn_ref, gate_ref, pp_ref):
        hv = h_ref[...]
        _, xhat = _rms_stats(hv)
        xn = (xhat * g_ref[...]).astype(BF16)
        xn_ref[...] = xn
        gate = _sigmoid(jnp.dot(xn, wg_ref[...], preferred_element_type=F32))
        pp = jnp.dot(p_ref[...].astype(BF16), wp_ref[...], preferred_element_type=F32)
        gate_ref[...] = gate.astype(BF16)
        pp_ref[...] = pp.astype(BF16)
        o_ref[...] = hv + gate * pp

    row = pl.BlockSpec((tm, d), lambda i: (i, 0))
    return pl.pallas_call(
        body, name=name, grid=(s // tm,),
        in_specs=[row, pl.BlockSpec((1, d), lambda i: (0, 0)), pl.BlockSpec((d, d), lambda i: (0, 0)),
                  pl.BlockSpec((tm, pd), lambda i: (i, 0)), pl.BlockSpec((pd, d), lambda i: (0, 0))],
        out_specs=[row, row, row, row],
        out_shape=[jax.ShapeDtypeStruct((s, d), F32)] + [jax.ShapeDtypeStruct((s, d), BF16)] * 3,
        compiler_params=_cp("arbitrary"))(h, g, wg, p, wp)


def _ple_bwd_elem(dh, gate, pp, *, name, tm=512):
    s, d = dh.shape

    def body(dh_ref, gate_ref, pp_ref, dz_ref, dpp_ref):
        dhv = dh_ref[...]
        gt = gate_ref[...].astype(F32)
        dz_ref[...] = (dhv * pp_ref[...].astype(F32) * (gt * (1.0 - gt))).astype(BF16)
        dpp_ref[...] = (dhv * gt).astype(BF16)

    row = pl.BlockSpec((tm, d), lambda i: (i, 0))
    return pl.pallas_call(
        body, name=name, grid=(s // tm,), in_specs=[row, row, row], out_specs=[row, row],
        out_shape=[jax.ShapeDtypeStruct((s, d), BF16)] * 2,
        compiler_params=_cp("arbitrary"))(dh, gate, pp)


def _final_loss(h, g, tgt, *, name, tm=512):
    s, d = h.shape

    def body(h_ref, g_ref, t_ref, loss_ref, dh_ref, dg_ref):
        @pl.when(pl.program_id(0) == 0)
        def _():
            loss_ref[...] = jnp.zeros_like(loss_ref)
            dg_ref[...] = jnp.zeros_like(dg_ref)

        hv = h_ref[...]
        gv = g_ref[...]
        _, xhat = _rms_stats(hv)
        err = xhat * gv - t_ref[...]
        per_row = jnp.mean(err * err, axis=-1, keepdims=True)
        loss_ref[...] += 0.5 * jnp.sum(per_row, axis=0, keepdims=True)
        dx, dgrow = _rms_bwd(err * (1.0 / d), hv, gv)
        dh_ref[...] = dx
        dg_ref[...] += jnp.sum(dgrow, axis=0, keepdims=True)

    row = pl.BlockSpec((tm, d), lambda i: (i, 0))
    vec = pl.BlockSpec((1, d), lambda i: (0, 0))
    return pl.pallas_call(
        body, name=name, grid=(s // tm,), in_specs=[row, vec, row],
        out_specs=[pl.BlockSpec((1, LANES), lambda i: (0, 0)), row, vec],
        out_shape=[jax.ShapeDtypeStruct((1, LANES), F32), jax.ShapeDtypeStruct((s, d), F32),
                   jax.ShapeDtypeStruct((1, d), F32)],
        compiler_params=_cp("arbitrary"))(h, g, tgt)


def _rope_fwd(y1, y2, cos, sin, *, name, tm=512):
    s, r = y1.shape

    def body(a_ref, b_ref, c_ref, s_ref, o_ref):
        o_ref[...] = a_ref[...] * c_ref[...] + b_ref[...] * s_ref[...]

    row = pl.BlockSpec((tm, r), lambda i: (i, 0))
    return pl.pallas_call(
        body, name=name, grid=(s // tm,), in_specs=[row] * 4, out_specs=row,
        out_shape=jax.ShapeDtypeStruct((s, r), F32), compiler_params=_cp("arbitrary"))(y1, y2, cos, sin)


def _split3(v):
    h1 = v.astype(BF16)
    r1 = v - h1.astype(F32)
    h2 = r1.astype(BF16)
    h3 = (r1 - h2.astype(F32)).astype(BF16)
    return h1, h2, h3


def _tri(tb, upper):
    r = lax.broadcasted_iota(jnp.int32, (tb, tb), 0)
    c = lax.broadcasted_iota(jnp.int32, (tb, tb), 1)
    return jnp.where((r <= c) if upper else (r >= c), 1.0, 0.0).astype(BF16)


def _fox_gate_fwd(ft, bf, *, out_scale, name, tb=512):
    nh, s = ft.shape

    def body(f_ref, b_ref, o_ref, carry):
        @pl.when(pl.program_id(0) == 0)
        def _():
            carry[...] = jnp.zeros_like(carry)

        z = f_ref[...] + b_ref[...]
        lf = jnp.minimum(z, 0.0) - jnp.log(1.0 + jnp.exp(-jnp.abs(z)))
        tri = _tri(tb, True)
        cs = sum(jnp.dot(t, tri, preferred_element_type=F32) for t in _split3(lf)) + carry[...]
        for n, term in enumerate(_split3(cs * out_scale)):
            o_ref[n] = term
        carry[...] += jnp.sum(lf, axis=-1, keepdims=True)

    return pl.pallas_call(
        body, name=name, grid=(s // tb,),
        in_specs=[pl.BlockSpec((nh, tb), lambda t: (0, t)), pl.BlockSpec((nh, 1), lambda t: (0, 0))],
        out_specs=pl.BlockSpec((3, nh, tb), lambda t: (0, 0, t)),
        out_shape=jax.ShapeDtypeStruct((3, nh, s), BF16),
        scratch_shapes=[pltpu.VMEM((nh, 1), F32)], compiler_params=_cp("arbitrary"))(ft, bf)


def _fox_gate_bwd(drow, dcol, ft, bf, *, inv_scale, name, tb=512):
    nh, s = ft.shape
    nb = s // tb

    def body(dr_ref, dc_ref, f_ref, b_ref, df_ref, db_ref, carry):
        @pl.when(pl.program_id(0) == 0)
        def _():
            carry[...] = jnp.zeros_like(carry)
            db_ref[...] = jnp.zeros_like(db_ref)

        dc = (dr_ref[...] - dc_ref[...]) * inv_scale
        tri = _tri(tb, False)
        suf = sum(jnp.dot(t, tri, preferred_element_type=F32) for t in _split3(dc)) + carry[...]
        z = f_ref[...] + b_ref[...]
        dz = suf * (1.0 / (1.0 + jnp.exp(z)))
        df_ref[...] = dz
        db_ref[...] += jnp.sum(dz, axis=-1, keepdims=True)
        carry[...] += jnp.sum(dc, axis=-1, keepdims=True)

    rev = pl.BlockSpec((nh, tb), lambda t: (0, nb - 1 - t))
    one = pl.BlockSpec((nh, 1), lambda t: (0, 0))
    return pl.pallas_call(
        body, name=name, grid=(nb,), in_specs=[rev, rev, rev, one], out_specs=[rev, one],
        out_shape=[jax.ShapeDtypeStruct((nh, s), F32), jax.ShapeDtypeStruct((nh, 1), F32)],
        scratch_shapes=[pltpu.VMEM((nh, 1), F32)], compiler_params=_cp("arbitrary"))(drow, dcol, ft, bf)


def _tri_fwd(t, nq):
    i = sum((t >= (r * (r + 1)) // 2).astype(jnp.int32) for r in range(1, nq))
    return i, t - (i * (i + 1)) // 2


def _tri_bwd(t, nq):
    j = sum((t >= r * nq - (r * (r - 1)) // 2).astype(jnp.int32) for r in range(1, nq))
    return j, j + t - (j * nq - (j * (j - 1)) // 2)


def _scores_t(k, q, *, scale, diag):
    s = lax.dot_general(k, q, NT, preferred_element_type=F32) * scale
    if diag:
        r = lax.broadcasted_iota(jnp.int32, s.shape, 0)
        c = lax.broadcasted_iota(jnp.int32, s.shape, 1)
        s = jnp.where(r <= c, s, MASK_VALUE)
    return s


def _causal_fwd_t(q, k, vt, *, scale, name, tq, hb=2, rider=None):
    nh, s, dq = q.shape
    dv = vt.shape[1]
    nq = s // tq
    nsteps = (nq * (nq + 1)) // 2

    def body(q_ref, k_ref, vt_ref, o_ref, lse_ref, m_sc, l_sc, acc_sc):
        i, j = _tri_fwd(pl.program_id(1), nq)

        @pl.when(j == 0)
        def _():
            m_sc[...] = jnp.full_like(m_sc, MASK_VALUE)
            l_sc[...] = jnp.zeros_like(l_sc)
            acc_sc[...] = jnp.zeros_like(acc_sc)

        def step(diag):
            for u in range(hb):
                sc = _scores_t(k_ref[u], q_ref[u], scale=scale, diag=diag)
                m_prev = m_sc[u]
                m_new = jnp.maximum(m_prev, jnp.max(sc, axis=0, keepdims=True))
                alpha = jnp.exp(m_prev - m_new)
                pr = jnp.exp(sc - m_new)
                l_new = alpha * l_sc[u] + jnp.sum(pr, axis=0, keepdims=True)
                acc = alpha * acc_sc[u] + jnp.dot(vt_ref[u], pr.astype(BF16), preferred_element_type=F32)
                if diag:
                    o_ref[u] = (acc / l_new).astype(BF16)
                    lse_ref[u] = m_new + jnp.log(l_new)
                else:
                    m_sc[u], l_sc[u], acc_sc[u] = m_new, l_new, acc

        pl.when(j < i)(functools.partial(step, False))
        pl.when(j == i)(functools.partial(step, True))

    def qi(t):
        return _tri_fwd(t, nq)[0]

    def kj(t):
        return _tri_fwd(t, nq)[1]

    return _call_with_rider(
        body, rider, name=name, grid=(nh // hb, nsteps),
        in_specs=[pl.BlockSpec((hb, tq, dq), lambda hp, t: (hp, qi(t), 0)),
                  pl.BlockSpec((hb, tq, dq), lambda hp, t: (hp, kj(t), 0)),
                  pl.BlockSpec((hb, dv, tq), lambda hp, t: (hp, 0, kj(t)))],
        out_specs=[pl.BlockSpec((hb, dv, tq), lambda hp, t: (hp, 0, qi(t))),
                   pl.BlockSpec((hb, 1, tq), lambda hp, t: (hp, 0, qi(t)))],
        out_shape=[jax.ShapeDtypeStruct((nh, dv, s), BF16), jax.ShapeDtypeStruct((nh, 1, s), F32)],
        scratch_shapes=[pltpu.VMEM((hb, 1, tq), F32), pltpu.VMEM((hb, 1, tq), F32), pltpu.VMEM((hb, dv, tq), F32)],
        compiler_params=_cp("arbitrary", "arbitrary"), args=(q, k, vt))


def _causal_bwd_t(q, k, v, ot, dot_, lse, *, scale, name, tq, hb=2, rider=None):
    nh, s, dq = q.shape
    dv = v.shape[-1]
    nq = s // tq
    nsteps = (nq * (nq + 1)) // 2

    def body(q_ref, k_ref, v_ref, ot_ref, dot_ref, lse_ref, dq_ref, dk_ref, dvt_ref):
        t = pl.program_id(1)
        j, i = _tri_bwd(t, nq)

        @pl.when(t == 0)
        def _():
            dq_ref[...] = jnp.zeros_like(dq_ref)

        def step(diag):
            rows = pl.ds(pl.multiple_of(i * tq, tq), tq)
            for u in range(hb):
                qv, kv, dov = q_ref[u], k_ref[u], dot_ref[u]
                pr = jnp.exp(_scores_t(kv, qv, scale=scale, diag=diag) - lse_ref[u])
                dp = jnp.dot(v_ref[u], dov, preferred_element_type=F32)
                delta = jnp.sum(dov.astype(F32) * ot_ref[u].astype(F32), axis=0, keepdims=True)
                dsb = ((pr * (dp - delta)) * scale).astype(BF16)
                d_v = lax.dot_general(dov, pr.astype(BF16), NT, preferred_element_type=F32)
                d_k = jnp.dot(dsb, qv, preferred_element_type=F32)
                if diag:
                    dvt_ref[u], dk_ref[u] = d_v, d_k
                else:
                    dvt_ref[u] += d_v
                    dk_ref[u] += d_k
                dq_ref[u, rows, :] += lax.dot_general(dsb, kv, TN, preferred_element_type=F32)

        pl.when(i > j)(functools.partial(step, False))
        pl.when(i == j)(functools.partial(step, True))

    def qi(t):
        return _tri_bwd(t, nq)[1]

    def kj(t):
        return _tri_bwd(t, nq)[0]

    rows_q = pl.BlockSpec((hb, tq, dq), lambda hp, t: (hp, qi(t), 0))
    rows_k = pl.BlockSpec((hb, tq, dq), lambda hp, t: (hp, kj(t), 0))
    lanes_q = pl.BlockSpec((hb, dv, tq), lambda hp, t: (hp, 0, qi(t)))
    return _call_with_rider(
        body, rider, name=name, grid=(nh // hb, nsteps),
        in_specs=[rows_q, rows_k, pl.BlockSpec((hb, tq, dv), lambda hp, t: (hp, kj(t), 0)), lanes_q, lanes_q,
                  pl.BlockSpec((hb, 1, tq), lambda hp, t: (hp, 0, qi(t)))],
        out_specs=[pl.BlockSpec((hb, s, dq), lambda hp, t: (hp, 0, 0)), rows_k,
                   pl.BlockSpec((hb, dv, tq), lambda hp, t: (hp, 0, kj(t)))],
        out_shape=[jax.ShapeDtypeStruct((nh, s, dq), F32), jax.ShapeDtypeStruct((nh, s, dq), F32),
                   jax.ShapeDtypeStruct((nh, dv, s), F32)],
        scratch_shapes=[], compiler_params=_cp("arbitrary", "arbitrary"), args=(q, k, v, ot, dot_, lse))


def _swa_scores_t(k, q, dist, ok, *, scale, slope):
    s = lax.dot_general(k, q, NT, preferred_element_type=F32) * scale - slope * dist.astype(F32)
    return jnp.where(ok, s, MASK_VALUE)


def _swa_geometry(tb, w, has_other):
    r = lax.broadcasted_iota(jnp.int32, (tb, tb), 0)
    c = lax.broadcasted_iota(jnp.int32, (tb, tb), 1)
    d_same = c - r
    ok_same = jnp.logical_and(d_same >= 0, d_same < w)

    def other(ncols):
        rr = lax.broadcasted_iota(jnp.int32, (w, ncols), 0)
        cc = lax.broadcasted_iota(jnp.int32, (w, ncols), 1)
        dd = cc + w - rr
        return dd, jnp.logical_and(dd < w, has_other)

    return (d_same, ok_same), other


def _swa_fwd_t(q, k, vt, slopes_sinks, *, scale, window, name, tb=256):
    nh, s, d = q.shape
    nkv = k.shape[0]
    grp = nh // nkv
    w = window
    per = tb // w
    assert tb % w == 0

    def body(q_ref, kc_ref, kp_ref, vc_ref, vp_ref, ss_ref, o_ref, lse_ref):
        kvh, i = pl.program_id(0), pl.program_id(1)
        (d_c, ok_c), other = _swa_geometry(tb, w, i > 0)
        d_p, ok_p = other(tb)
        for g in range(grp):
            h = kvh * grp + g
            slope, sink = ss_ref[0, h], ss_ref[1, h]
            qg = q_ref[g]
            s_c = _swa_scores_t(kc_ref[...], qg, d_c, ok_c, scale=scale, slope=slope)
            s_p = _swa_scores_t(kp_ref[...], qg, d_p, ok_p, scale=scale, slope=slope)
            m = jnp.maximum(jnp.maximum(jnp.max(s_c, axis=0, keepdims=True), jnp.max(s_p, axis=0, keepdims=True)), sink)
            p_c, p_p = jnp.exp(s_c - m), jnp.exp(s_p - m)
            l = jnp.sum(p_c, axis=0, keepdims=True) + jnp.sum(p_p, axis=0, keepdims=True) + jnp.exp(sink - m)
            acc = (jnp.dot(vc_ref[...], p_c.astype(BF16), preferred_element_type=F32)
                   + jnp.dot(vp_ref[...], p_p.astype(BF16), preferred_element_type=F32))
            o_ref[g] = (acc / l).astype(BF16)
            lse_ref[g] = m + jnp.log(l)

    def prev(i):
        return jnp.maximum(i * per - 1, 0)

    return pl.pallas_call(
        body, name=name, grid=(nkv, s // tb),
        in_specs=[pl.BlockSpec((grp, tb, d), lambda kh, i: (kh, i, 0)),
                  pl.BlockSpec((None, tb, d), lambda kh, i: (kh, i, 0)),
                  pl.BlockSpec((None, w, d), lambda kh, i: (kh, prev(i), 0)),
                  pl.BlockSpec((None, d, tb), lambda kh, i: (kh, 0, i)),
                  pl.BlockSpec((None, d, w), lambda kh, i: (kh, 0, prev(i))),
                  pl.BlockSpec(memory_space=pltpu.SMEM)],
        out_specs=[pl.BlockSpec((grp, d, tb), lambda kh, i: (kh, 0, i)), pl.BlockSpec((grp, 1, tb), lambda kh, i: (kh, 0, i))],
        out_shape=[jax.ShapeDtypeStruct((nh, d, s), BF16), jax.ShapeDtypeStruct((nh, 1, s), F32)],
        compiler_params=_cp("arbitrary", "arbitrary"))(q, k, k, vt, vt, slopes_sinks)


def _swa_bwd_t(q, k, v, ot, dot_, lse, slopes_sinks, *, scale, window, name, tb=256):
    nh, s, d = q.shape
    nkv = k.shape[0]
    grp = nh // nkv
    w = window
    per = tb // w
    nb = s // tb

    def body(qc_ref, qn_ref, kc_ref, kp_ref, vc_ref, vp_ref, oc_ref, on_ref, doc_ref, don_ref, lc_ref, ln_ref, ss_ref,
             dq_ref, dk_ref, dvt_ref, dsink_ref):
        kvh, i = pl.program_id(0), pl.program_id(1)

        @pl.when(i == 0)
        def _():
            dsink_ref[...] = jnp.zeros_like(dsink_ref)

        (d_c, ok_c), other = _swa_geometry(tb, w, i > 0)
        d_p, ok_p = other(tb)
        d_n, ok_n = _swa_geometry(tb, w, i < nb - 1)[1](w)
        kc, kp, vc, vp = kc_ref[...], kp_ref[...], vc_ref[...], vp_ref[...]
        k_last, v_last = kc[tb - w:, :], vc[tb - w:, :]
        dk_acc = jnp.zeros((tb, d), F32)
        dv_acc = jnp.zeros((d, tb), F32)
        dk_tail = jnp.zeros((w, d), F32)
        dv_tail = jnp.zeros((d, w), F32)
        for g in range(grp):
            h = kvh * grp + g
            slope, sink = ss_ref[0, h], ss_ref[1, h]
            qg, dog, lse_c = qc_ref[g], doc_ref[g], lc_ref[g]
            delta = jnp.sum(dog.astype(F32) * oc_ref[g].astype(F32), axis=0, keepdims=True)
            p_c = jnp.exp(_swa_scores_t(kc, qg, d_c, ok_c, scale=scale, slope=slope) - lse_c)
            p_p = jnp.exp(_swa_scores_t(kp, qg, d_p, ok_p, scale=scale, slope=slope) - lse_c)
            ds_c = ((p_c * (jnp.dot(vc, dog, preferred_element_type=F32) - delta)) * scale).astype(BF16)
            ds_p = ((p_p * (jnp.dot(vp, dog, preferred_element_type=F32) - delta)) * scale).astype(BF16)
            dq_ref[g] = (lax.dot_general(ds_c, kc, TN, preferred_element_type=F32)
                         + lax.dot_general(ds_p, kp, TN, preferred_element_type=F32))
            dk_acc += jnp.dot(ds_c, qg, preferred_element_type=F32)
            dv_acc += lax.dot_general(dog, p_c.astype(BF16), NT, preferred_element_type=F32)
            dsink_ref[g] -= jnp.broadcast_to(jnp.sum(jnp.exp(sink - lse_c) * delta, axis=1, keepdims=True), (1, LANES))
            qn, don = qn_ref[g], don_ref[g]
            delta_n = jnp.sum(don.astype(F32) * on_ref[g].astype(F32), axis=0, keepdims=True)
            p_n = jnp.exp(_swa_scores_t(k_last, qn, d_n, ok_n, scale=scale, slope=slope) - ln_ref[g])
            ds_n = ((p_n * (jnp.dot(v_last, don, preferred_element_type=F32) - delta_n)) * scale).astype(BF16)
            dk_tail += jnp.dot(ds_n, qn, preferred_element_type=F32)
            dv_tail += lax.dot_general(don, p_n.astype(BF16), NT, preferred_element_type=F32)
        dk_ref[...] = dk_acc
        dvt_ref[...] = dv_acc
        dk_ref[tb - w:, :] += dk_tail
        dvt_ref[:, tb - w:] += dv_tail

    def prev(i):
        return jnp.maximum(i * per - 1, 0)

    def nxt(i):
        return jnp.minimum((i + 1) * per, s // w - 1)

    return pl.pallas_call(
        body, name=name, grid=(nkv, nb),
        in_specs=[pl.BlockSpec((grp, tb, d), lambda kh, i: (kh, i, 0)),
                  pl.BlockSpec((grp, w, d), lambda kh, i: (kh, nxt(i), 0)),
                  pl.BlockSpec((None, tb, d), lambda kh, i: (kh, i, 0)),
                  pl.BlockSpec((None, w, d), lambda kh, i: (kh, prev(i), 0)),
                  pl.BlockSpec((None, tb, d), lambda kh, i: (kh, i, 0)),
                  pl.BlockSpec((None, w, d), lambda kh, i: (kh, prev(i), 0)),
                  pl.BlockSpec((grp, d, tb), lambda kh, i: (kh, 0, i)),
                  pl.BlockSpec((grp, d, w), lambda kh, i: (kh, 0, nxt(i))),
                  pl.BlockSpec((grp, d, tb), lambda kh, i: (kh, 0, i)),
                  pl.BlockSpec((grp, d, w), lambda kh, i: (kh, 0, nxt(i))),
                  pl.BlockSpec((grp, 1, tb), lambda kh, i: (kh, 0, i)),
                  pl.BlockSpec((grp, 1, w), lambda kh, i: (kh, 0, nxt(i))),
                  pl.BlockSpec(memory_space=pltpu.SMEM)],
        out_specs=[pl.BlockSpec((grp, tb, d), lambda kh, i: (kh, i, 0)),
                   pl.BlockSpec((None, tb, d), lambda kh, i: (kh, i, 0)),
                   pl.BlockSpec((None, d, tb), lambda kh, i: (kh, 0, i)),
                   pl.BlockSpec((None, grp, 1, LANES), lambda kh, i: (kh, 0, 0, 0))],
        out_shape=[jax.ShapeDtypeStruct((nh, s, d), F32), jax.ShapeDtypeStruct((nkv, s, d), F32),
                   jax.ShapeDtypeStruct((nkv, d, s), F32), jax.ShapeDtypeStruct((nkv, grp, 1, LANES), F32)],
        compiler_params=_cp("arbitrary", "arbitrary"))(q, q, k, k, v, v, ot, ot, dot_, dot_, lse, lse, slopes_sinks)


def _adamw(w, g, m, v, *, name):
    shape = w.shape
    cols = shape[-1]
    rows = int(np.prod(shape[:-1])) if len(shape) > 1 else 1
    tr = _row_tile(rows, cols)
    c1 = 1.0 - ADAM_B1 ** ADAM_STEP
    c2 = 1.0 - ADAM_B2 ** ADAM_STEP

    def body(w_ref, g_ref, m_ref, v_ref, d_ref, mo_ref, vo_ref):
        gv = g_ref[...]
        mn = ADAM_B1 * m_ref[...] + (1.0 - ADAM_B1) * gv
        vn = ADAM_B2 * v_ref[...] + (1.0 - ADAM_B2) * (gv * gv)
        mo_ref[...] = mn
        vo_ref[...] = vn
        d_ref[...] = -ADAM_LR * ((mn / c1) / (jnp.sqrt(vn / c2) + ADAM_EPS) + ADAM_WD * w_ref[...])

    blk = pl.BlockSpec((tr, cols), lambda i: (i, 0))
    outs = pl.pallas_call(
        body, name=name, grid=(rows // tr,), in_specs=[blk] * 4, out_specs=[blk] * 3,
        out_shape=[jax.ShapeDtypeStruct((rows, cols), F32)] * 3,
        compiler_params=_cp("arbitrary"))(*[a.reshape(rows, cols) for a in (w, g, m, v)])
    return tuple(a.reshape(shape) for a in outs)


def _hbm_spec():
    return pl.BlockSpec(memory_space=pl.ANY)


def _mesh_place():
    x, y, c = lax.axis_index("x"), lax.axis_index("y"), lax.axis_index("c")
    return x, y, c, [(1 - x, y), (x, 1 - y), (1 - x, 1 - y)]


def _half_rows(c, rows, align):
    return pl.ds(pl.multiple_of(c * (rows // 2), align), rows // 2)


def _part(ref, mode, k, n, rows=None):
    if mode == "cols":
        cols = pl.ds(pl.multiple_of(k * n, LANES), n)
        return ref.at[:, cols] if rows is None else ref.at[rows, cols]
    return ref.at[k] if rows is None else ref.at[k, rows, :]


class _Rider:
    def __init__(self, inputs, out_shape, n_sems, start, finish):
        self.inputs, self.out_shape, self.n_sems, self.start, self.finish = inputs, out_shape, n_sems, start, finish


def _call_with_rider(body, rider, *, name, grid, in_specs, out_specs, out_shape, scratch_shapes, compiler_params, args):
    if rider is None:
        outs = pl.pallas_call(body, name=name, grid=grid, in_specs=in_specs, out_specs=out_specs, out_shape=out_shape,
                              scratch_shapes=scratch_shapes, compiler_params=compiler_params)(*args)
        return outs, []
    n_in, n_out, n_sc = len(in_specs), len(out_specs), len(scratch_shapes)
    n_rin, n_rout = len(rider.inputs), len(rider.out_shape)

    def wrapped(*refs):
        pos = 0
        groups = []
        for n in (n_in, n_rin, n_out, n_rout, n_sc, 2):
            groups.append(refs[pos:pos + n])
            pos += n
        ins, rins, outs, routs, scratch, sems = groups
        ids = [pl.program_id(a) for a in range(len(grid))]
        first = functools.reduce(jnp.logical_and, [i == 0 for i in ids])
        last = functools.reduce(jnp.logical_and, [i == g - 1 for i, g in zip(ids, grid)])
        pl.when(first)(lambda: rider.start(rins, routs, *sems))
        body(*ins, *outs, *scratch)
        pl.when(last)(lambda: rider.finish(rins, routs, *sems))

    outs = pl.pallas_call(
        wrapped, name=name, grid=grid, in_specs=list(in_specs) + [_hbm_spec()] * n_rin,
        out_specs=list(out_specs) + [_hbm_spec()] * n_rout, out_shape=list(out_shape) + list(rider.out_shape),
        scratch_shapes=list(scratch_shapes) + [pltpu.SemaphoreType.DMA((rider.n_sems,))] * 2,
        compiler_params=compiler_params)(*args, *rider.inputs)
    return outs[:n_out], outs[n_out:]


def _run_rider(rider, *, name):
    n_rin = len(rider.inputs)

    def body(*refs):
        rins, routs, sems = refs[:n_rin], refs[n_rin:-2], refs[-2:]
        rider.start(rins, routs, *sems)
        rider.finish(rins, routs, *sems)

    return pl.pallas_call(
        body, name=name, in_specs=[_hbm_spec()] * n_rin, out_specs=[_hbm_spec()] * len(rider.out_shape),
        out_shape=rider.out_shape, scratch_shapes=[pltpu.SemaphoreType.DMA((rider.n_sems,))] * 2)(*rider.inputs)


def _gather_rider(shards, modes):
    n_arr = len(shards)
    out_shape = [jax.ShapeDtypeStruct((s.shape[0], N_CHIPS * s.shape[1]) if m == "cols" else (N_CHIPS,) + s.shape, s.dtype)
                 for s, m in zip(shards, modes)]
    per = 4

    def copies(srcs, dsts, send_sems, recv_sems):
        x, y, c, chips = _mesh_place()
        me = 2 * x + y
        sends, waits = [], []
        for i in range(n_arr):
            r, n = shards[i].shape
            rows = _half_rows(c, r, 16)

            def copy(slot, src, dst, to, i=i):
                return pltpu.make_async_remote_copy(src_ref=src, dst_ref=dst, send_sem=send_sems.at[i * per + slot],
                                                    recv_sem=recv_sems.at[i * per + slot], device_id=to, device_id_type=MESH)

            own = _part(dsts[i], modes[i], me, n)
            sends.append(copy(0, srcs[i], own, (x, y, 1 - c)))
            waits.append(copy(0, own, own, (x, y, 1 - c)))
            for j, (px, py) in enumerate(chips):
                sends.append(copy(1 + j, srcs[i].at[rows], _part(dsts[i], modes[i], me, n, rows), (px, py, c)))
                theirs = _part(dsts[i], modes[i], 2 * px + py, n, rows)
                waits.append(copy(1 + j, theirs, theirs, (px, py, c)))
        return sends, waits

    def start(*refs):
        for cp in copies(*refs)[0]:
            cp.start()

    def finish(*refs):
        sends, waits = copies(*refs)
        for cp in waits:
            cp.wait_recv()
        for cp in sends:
            cp.wait_send()

    return _Rider(list(shards), out_shape, per * n_arr, start, finish)


def _gather_forward(dsts, shard_shapes, modes, *, name):
    n_arr = len(dsts)

    def body(*refs):
        outs = refs[n_arr:2 * n_arr]
        send_sems, recv_sems = refs[2 * n_arr:]
        x, y, c, chips = _mesh_place()
        cps = []
        for i in range(n_arr):
            r, n = shard_shapes[i]
            for j, (px, py) in enumerate(chips):
                def view(hc, i=i, px=px, py=py, r=r, n=n):
                    return _part(outs[i], modes[i], 2 * px + py, n, _half_rows(hc, r, 16))

                def copy(ref, i=i, j=j):
                    return pltpu.make_async_remote_copy(src_ref=ref, dst_ref=ref, send_sem=send_sems.at[3 * i + j],
                                                        recv_sem=recv_sems.at[3 * i + j], device_id=(x, y, 1 - c), device_id_type=MESH)

                cps.append((copy(view(c)), copy(view(1 - c))))
        for send, _ in cps:
            send.start()
        for send, theirs in cps:
            theirs.wait_recv()
            send.wait_send()

    return pl.pallas_call(
        body, name=name, in_specs=[_hbm_spec()] * n_arr, out_specs=[_hbm_spec()] * n_arr,
        out_shape=[jax.ShapeDtypeStruct(d.shape, d.dtype) for d in dsts],
        input_output_aliases={i: i for i in range(n_arr)},
        scratch_shapes=[pltpu.SemaphoreType.DMA((3 * n_arr,)), pltpu.SemaphoreType.DMA((3 * n_arr,))])(*dsts)


def _blk_view(a, mode):
    return a[None] if mode == "cols" else a


def _rs_pair_swap(arrs, modes, *, name):
    n_arr = len(arrs)
    out_shape = [jax.ShapeDtypeStruct((a.shape[0] // 2, a.shape[1]) if m == "cols" else (a.shape[0], a.shape[1] // 2, a.shape[2]), a.dtype)
                 for a, m in zip(arrs, modes)]

    def body(*refs):
        srcs, dsts = refs[:n_arr], refs[n_arr:2 * n_arr]
        send_sems, recv_sems = refs[2 * n_arr:]
        x, y, c, _ = _mesh_place()
        cps = []
        for i in range(n_arr):
            if modes[i] == "cols":
                src = srcs[i].at[_half_rows(1 - c, arrs[i].shape[0], 8)]
            else:
                src = srcs[i].at[:, _half_rows(1 - c, arrs[i].shape[1], 8), :]
            cps.append(pltpu.make_async_remote_copy(src_ref=src, dst_ref=dsts[i], send_sem=send_sems.at[i],
                                                    recv_sem=recv_sems.at[i], device_id=(x, y, 1 - c), device_id_type=MESH))
        for cp in cps:
            cp.start()
        for cp in cps:
            cp.wait()

    return pl.pallas_call(
        body, name=name, in_specs=[_hbm_spec()] * n_arr, out_specs=[_hbm_spec()] * n_arr, out_shape=out_shape,
        scratch_shapes=[pltpu.SemaphoreType.DMA((n_arr,)), pltpu.SemaphoreType.DMA((n_arr,))])(*arrs)


def _rs_pair_add(arr, landed, place, *, name):
    nb, r, c = arr.shape
    rh = r // 2
    tr = _row_tile(rh, c)
    nt = rh // tr

    def body(p_ref, a_ref, l_ref, o_ref):
        o_ref[...] = (a_ref[...] + l_ref[...]).astype(BF16)

    grid_spec = pltpu.PrefetchScalarGridSpec(
        num_scalar_prefetch=1, grid=(nb, nt),
        in_specs=[pl.BlockSpec((None, tr, c), lambda b, t, p_ref: (b, p_ref[1] * nt + t, 0)),
                  pl.BlockSpec((None, tr, c), lambda b, t, p_ref: (b, t, 0))],
        out_specs=pl.BlockSpec((None, tr, c), lambda b, t, p_ref: (b, t, 0)))
    return pl.pallas_call(
        body, name=name, grid_spec=grid_spec, out_shape=jax.ShapeDtypeStruct((nb, rh, c), BF16),
        compiler_params=_cp("arbitrary", "arbitrary"))(place, arr, landed)


def _exchange_rider(parts, modes):
    n_arr = len(parts)
    out_shape = []
    for a, m in zip(parts, modes):
        shp = (a.shape[0], a.shape[1] // N_CHIPS) if m == "cols" else a.shape[1:]
        out_shape.append(jax.ShapeDtypeStruct((3,) + shp, a.dtype))

    def copies(srcs, dsts, send_sems, recv_sems):
        x, y, c, chips = _mesh_place()
        cps = []
        for i in range(n_arr):
            n = out_shape[i].shape[-1]
            for j, (px, py) in enumerate(chips):
                cps.append(pltpu.make_async_remote_copy(
                    src_ref=_part(srcs[i], modes[i], 2 * px + py, n), dst_ref=dsts[i].at[j],
                    send_sem=send_sems.at[3 * i + j], recv_sem=recv_sems.at[3 * i + j],
                    device_id=(px, py, c), device_id_type=MESH))
        return cps

    def start(*refs):
        for cp in copies(*refs):
            cp.start()

    def finish(*refs):
        for cp in copies(*refs):
            cp.wait()

    return _Rider(list(parts), out_shape, 3 * n_arr, start, finish)


def _rs_chip_sum(part, landed, mode, place, *, name):
    _, rh, n = landed.shape
    tr = _row_tile(rh, n)
    nt = rh // tr

    def body(p_ref, a_ref, l_ref, o_ref):
        o_ref[...] = ((a_ref[...].astype(F32) + l_ref[0].astype(F32)) + l_ref[1].astype(F32)) + l_ref[2].astype(F32)

    if mode == "cols":
        own = pl.BlockSpec((tr, n), lambda t, p_ref: (t, p_ref[0]))
    else:
        own = pl.BlockSpec((None, tr, n), lambda t, p_ref: (p_ref[0], t, 0))
    grid_spec = pltpu.PrefetchScalarGridSpec(
        num_scalar_prefetch=1, grid=(nt,),
        in_specs=[own, pl.BlockSpec((3, tr, n), lambda t, p_ref: (0, t, 0))],
        out_specs=pl.BlockSpec((tr, n), lambda t, p_ref: (p_ref[1] * nt + t, 0)))
    return pl.pallas_call(
        body, name=name, grid_spec=grid_spec, out_shape=jax.ShapeDtypeStruct((2 * rh, n), F32),
        compiler_params=_cp("arbitrary"))(place, part, landed)


def _rs_pair_join(halves, *, name):
    n_arr = len(halves)

    def body(*refs):
        outs = refs[n_arr:2 * n_arr]
        send_sems, recv_sems = refs[2 * n_arr:]
        x, y, c, _ = _mesh_place()
        cps = []
        for i in range(n_arr):
            rows = _half_rows(c, halves[i].shape[0], 8)
            cps.append(pltpu.make_async_remote_copy(src_ref=outs[i].at[rows], dst_ref=outs[i].at[rows], send_sem=send_sems.at[i],
                                                    recv_sem=recv_sems.at[i], device_id=(x, y, 1 - c), device_id_type=MESH))
        for cp in cps:
            cp.start()
        for i, cp in enumerate(cps):
            cp.wait_send()
            theirs = outs[i].at[_half_rows(1 - c, halves[i].shape[0], 8)]
            pltpu.make_async_remote_copy(src_ref=theirs, dst_ref=theirs, send_sem=send_sems.at[i], recv_sem=recv_sems.at[i],
                                         device_id=(x, y, 1 - c), device_id_type=MESH).wait_recv()

    return pl.pallas_call(
        body, name=name, in_specs=[_hbm_spec()] * n_arr, out_specs=[_hbm_spec()] * n_arr,
        out_shape=[jax.ShapeDtypeStruct(h.shape, h.dtype) for h in halves],
        input_output_aliases={i: i for i in range(n_arr)},
        scratch_shapes=[pltpu.SemaphoreType.DMA((n_arr,)), pltpu.SemaphoreType.DMA((n_arr,))])(*halves)


def _allreduce_small(v, *, name):
    r, c = v.shape

    def body(v_ref, o_ref, gath, send_sems, recv_sems):
        x, y, cc, _ = _mesh_place()
        me = 4 * x + 2 * y + cc
        gath[me] = v_ref[...]
        cps = []
        for rel in range(1, 8):
            px = 1 - x if rel & 4 else x
            py = 1 - y if rel & 2 else y
            pc = 1 - cc if rel & 1 else cc

            def copy(slot, px=px, py=py, pc=pc, rel=rel):
                return pltpu.make_async_remote_copy(
                    src_ref=v_ref, dst_ref=gath.at[slot], send_sem=send_sems.at[rel - 1],
                    recv_sem=recv_sems.at[rel - 1], device_id=(px, py, pc), device_id_type=MESH)

            cps.append((copy(me), copy(4 * px + 2 * py + pc)))
        for send, _ in cps:
            send.start()
        for send, theirs in cps:
            theirs.wait_recv()
            send.wait_send()
        tot = gath[0]
        for d in range(1, 8):
            tot = tot + gath[d]
        o_ref[...] = tot

    vm = pl.BlockSpec(memory_space=pltpu.VMEM)
    return pl.pallas_call(
        body, name=name, in_specs=[vm], out_specs=vm, out_shape=jax.ShapeDtypeStruct((r, c), F32),
        scratch_shapes=[pltpu.VMEM((8, r, c), F32), pltpu.SemaphoreType.DMA((7,)), pltpu.SemaphoreType.DMA((7,))])(v)


def _rope_tables(s, reps):
    half = B_ROPE // 2
    inv = ROPE_THETA ** (-jnp.arange(0, B_ROPE, 2, dtype=F32) / B_ROPE)
    ang = jnp.arange(s, dtype=F32)[:, None] * inv[None, :]
    return jnp.tile(jnp.cos(ang), (1, reps)), jnp.tile(jnp.sin(ang), (1, reps))


def _alibi_slopes():
    return 2.0 ** (-8.0 * jnp.arange(1, A_HEADS + 1, dtype=F32) / A_HEADS)


def _ffn_fwd(h, norm, wts, tag, rider=None, on_rode=None):
    (gate, up, act, xn), rode = _ffn_up(h, norm, wts["wgu"], name=f"{tag}_up", rider=rider)
    if on_rode is not None:
        on_rode(rode)
    out = _mm_res_fwd(act, wts["wd"], h, scale=FFN_RES_SCALE, name=f"{tag}_down")
    return out, dict(h_in=h, gate=gate, up=up, act=act, xn=xn), rode


def _ffn_bwd(dh, norm, wts, sv, tag, rider=None):
    (dgate, dup), rode = _ffn_down_bwd(dh, wts["wd"], sv["gate"], sv["up"], scale=FFN_RES_SCALE,
                                      name=f"{tag}_down_bwd", rider=rider)
    d_wd = _mm_tn(sv["act"], dh, b_scale=FFN_RES_SCALE, name=f"{tag}_dwd")
    d_wgu = _mm_tn(sv["xn"], [dgate, dup], name=f"{tag}_dwgu")
    dh_in, dnorm = _mm_nt_rmsbwd([(dgate, wts["wgu"], 0), (dup, wts["wgu"], 1)], sv["h_in"], norm, dh,
                                 name=f"{tag}_dx")
    return dh_in, dnorm, d_wgu, d_wd, rode


def _even_weights(w_in, w_uq, w_ukv):
    half = B_ROPE // 2
    base = w_in.shape[1]
    kr1, kr2 = w_in[:, base - B_ROPE:base - half], w_in[:, base - half:]
    w_in_cat = jnp.concatenate([w_in, -kr2, kr1, jnp.zeros((w_in.shape[0], 64), w_in.dtype)], axis=1)
    u3 = w_uq.reshape(w_uq.shape[0], B_HEADS, B_NOPE + B_ROPE)
    nope = u3[:, :, :B_NOPE].reshape(w_uq.shape[0], -1)
    rot = u3[:, :, B_NOPE:].reshape(w_uq.shape[0], -1)
    swapped = jnp.concatenate([-u3[:, :, B_NOPE + half:], u3[:, :, B_NOPE:B_NOPE + half]], axis=-1).reshape(w_uq.shape[0], -1)
    return w_in_cat, jnp.concatenate([nope, rot, swapped], axis=1), w_ukv


def _even_fwd(h, w, i, rider=None):
    s = h.shape[0]
    qa, ka, va, vat, c_q, c_kv, kr_blk, xn = _ev_in_fwd(h, w["mix_norm"][i:i + 1], w["ev_in_cat"], name="ev_in")
    cos32, sin32 = _rope_tables(s, 2)
    kro = _rope_fwd(kr_blk[:, :B_ROPE], kr_blk[:, B_ROPE:2 * B_ROPE], cos32, sin32, name="ev_k_rope")
    ss = jnp.stack([_alibi_slopes(), w["ev_sinks"].reshape(-1)])
    oa, lse_a = _swa_fwd_t(qa, ka, vat, ss, scale=A_HEAD_DIM ** -0.5, window=WINDOW, name="swa_fwd")
    cos256, sin256 = _rope_tables(s, 2 * B_HEADS)
    qb, xn_q = _ev_q_fwd(c_q, w["ev_cq_norm"], w["ev_q_cat"], cos256, sin256, name="ev_q_up")
    kb, vb, vbt, xn_kv = _ev_kv_fwd(c_kv, w["ev_ckv_norm"], w["ev_ukv"], kro, name="ev_kv_up")
    (ob, lse_b), rode = _causal_fwd_t(qb, kb, vbt, scale=(B_NOPE + B_ROPE) ** -0.5, name="mla_fwd", tq=512, rider=rider)
    attn = jnp.concatenate([oa.transpose(2, 0, 1).reshape(s, -1), ob.transpose(2, 0, 1).reshape(s, -1)], axis=-1)
    out = _mm_res_fwd(attn, w["ev_out"], h, scale=1.0, name="ev_out")
    sv = dict(h_in=h, xn=xn, c_q=c_q, c_kv=c_kv, xn_q=xn_q, xn_kv=xn_kv, qa=qa, ka=ka, va=va, oa=oa, lse_a=lse_a,
              ss=ss, qb=qb, kb=kb, vb=vb, ob=ob, lse_b=lse_b, attn=attn, cos32=cos32, sin32=sin32,
              cos256=cos256, sin256=sin256)
    return out, sv, rode


def _even_bwd(dh, w, sv, i, rider=None):
    s = dh.shape[0]
    half = B_ROPE // 2
    g = {}
    dattn = _mm_nt(dh, w["ev_out"], name="ev_out_dx")
    g["ev_w_out"] = _mm_tn(sv["attn"], dh, name="ev_out_dw")
    doa = dattn[:, :512].reshape(s, A_HEADS, A_HEAD_DIM).transpose(1, 2, 0).astype(BF16)
    dob = dattn[:, 512:].reshape(s, B_HEADS, B_V).transpose(1, 2, 0).astype(BF16)
    dqa, dka, dva, dsink = _swa_bwd_t(sv["qa"], sv["ka"], sv["va"], sv["oa"], doa, sv["lse_a"], sv["ss"],
                                      scale=A_HEAD_DIM ** -0.5, window=WINDOW, name="swa_bwd")
    g["ev_sinks"] = dsink[:, :, 0, 0].reshape(1, A_HEADS)
    (dqb, dkb, dvb), rode = _causal_bwd_t(sv["qb"], sv["kb"], sv["vb"], sv["ob"], dob, sv["lse_b"],
                                          scale=(B_NOPE + B_ROPE) ** -0.5, name="mla_bwd", tq=512, rider=rider)
    dyq = _ev_q_merge(dqb, sv["cos256"], sv["sin256"], name="ev_q_merge")
    dwq = _mm_tn(sv["xn_q"], dyq, name="ev_q_up_dw")
    dcq, g["ev_cq_norm"] = _mm_nt_rmsbwd([(dyq, w["ev_q_cat"])], sv["c_q"], w["ev_cq_norm"], None, name="ev_q_up_dx")
    kq = sv["c_q"].shape[1]
    d_nope = dwq[:, :512].reshape(kq, B_HEADS, B_NOPE)
    d_rot = dwq[:, 512:768].reshape(kq, B_HEADS, B_ROPE)
    d_swp = dwq[:, 768:].reshape(kq, B_HEADS, B_ROPE)
    g["ev_w_uq"] = jnp.concatenate([d_nope, d_rot[:, :, :half] + d_swp[:, :, half:], d_rot[:, :, half:] - d_swp[:, :, :half]],
                                   axis=-1).reshape(kq, -1)
    dykv, dkr = _ev_kv_merge(dkb, dvb, sv["cos32"], sv["sin32"], name="ev_kv_merge")
    g["ev_w_ukv"] = _mm_tn(sv["xn_kv"], dykv, name="ev_kv_up_dw")
    dckv, g["ev_ckv_norm"] = _mm_nt_rmsbwd([(dykv, w["ev_ukv"])], sv["c_kv"], w["ev_ckv_norm"], None, name="ev_kv_up_dx")
    dycat = _ev_in_merge(dqa, dka, dva, dcq, dckv, dkr, name="ev_in_merge")
    dwin = _mm_tn(sv["xn"], dycat, name="ev_in_dw")
    base = 1184
    g["ev_w_in"] = jnp.concatenate([dwin[:, :base - B_ROPE],
                                    dwin[:, base - B_ROPE:base - half] + dwin[:, base + half:base + B_ROPE],
                                    dwin[:, base - half:base] - dwin[:, base:base + half]], axis=-1)
    dh_in, dnorm = _mm_nt_rmsbwd([(dycat, w["ev_in_cat"])], sv["h_in"], w["mix_norm"][i:i + 1], dh, name="ev_in_dx")
    return dh_in, dnorm, g, rode


def _odd_fwd(h, w, i, rider=None):
    s = h.shape[0]
    wd = C_HEADS * C_HEAD_DIM
    q, k, v, vt, y_f, xn = _fox_in_fwd(h, w["mix_norm"][i:i + 1], w["od_in_pad"], nheads=C_HEADS, dh=C_HEAD_DIM,
                                       q_ones=(0, 2, 3, 4), k_ones=(1,), name="od_in")
    scale = C_HEAD_DIM ** -0.5
    ft = y_f[:, :C_HEADS].T
    bf = w["od_b_f"].reshape(C_HEADS, 1)
    cb3 = _fox_gate_fwd(ft, bf, out_scale=-1.0 / scale, name="fox_gate_fwd")
    k = k + jnp.pad(cb3.transpose(1, 2, 0), ((0, 0), (0, 0), (C_HEAD_DIM + 2, LANES - C_HEAD_DIM - 5)))
    (o, lse), rode = _causal_fwd_t(q, k, vt, scale=scale, name="fox_fwd", tq=512, rider=rider)
    attn = o.transpose(2, 0, 1).reshape(s, -1)
    out = _mm_res_fwd(attn, w["od_out"], h, scale=1.0, name="od_out")
    return out, dict(h_in=h, xn=xn, q=q, k=k, v=v, o=o, lse=lse, ft=ft, bf=bf, attn=attn), rode


def _odd_bwd(dh, w, sv, i, rider=None):
    s = dh.shape[0]
    g = {}
    dattn = _mm_nt(dh, w["od_out"], name="od_out_dx")
    g["od_w_out"] = _mm_tn(sv["attn"], dh, name="od_out_dw")
    do = dattn.reshape(s, C_HEADS, C_HEAD_DIM).transpose(1, 2, 0).astype(BF16)
    scale = C_HEAD_DIM ** -0.5
    (dq, dk, dv), rode = _causal_bwd_t(sv["q"], sv["k"], sv["v"], sv["o"], do, sv["lse"], scale=scale, name="fox_bwd",
                                       tq=512, rider=rider)
    dft, dbf = _fox_gate_bwd(dq[:, :, C_HEAD_DIM + 1], dk[:, :, C_HEAD_DIM], sv["ft"], sv["bf"],
                             inv_scale=1.0 / scale, name="fox_gate_bwd")
    g["od_b_f"] = dbf.reshape(1, C_HEADS)
    wd = C_HEADS * C_HEAD_DIM
    dqkv = _merge_heads(dq, dk, dv, dh=C_HEAD_DIM, name="fox_merge")
    df = jnp.pad(dft.T, ((0, 0), (0, LANES - C_HEADS)))
    g["od_w_in"] = jnp.concatenate([_mm_tn(sv["xn"], dqkv, name="od_in_dw"),
                                    _mm_tn(sv["xn"], df, name="od_in_dwf")[:, :C_HEADS]], axis=-1)
    dh_in, dnorm = _mm_nt_rmsbwd([(dqkv, w["od_in_pad"], 0), (df, w["od_in_pad"], 3 * wd // LANES)],
                                 sv["h_in"], w["mix_norm"][i:i + 1], dh, name="od_in_dx")
    return dh_in, dnorm, g, rode


def _kernel_weights(full, replicated):
    w = dict(replicated)
    _install_weights(w, {(n, i): a for n, per_layer in full.items() for i, a in enumerate(per_layer)})
    return w


def _install_weights(w, got):
    raw = w.setdefault("raw", {})
    raw.update(got)
    for (n, i), a in got.items():
        if n in ("ffa_w_gate_up", "ffa_w_down", "ffb_w_gate_up", "ffb_w_down"):
            w.setdefault(n[:3], {}).setdefault(i, {})["wgu" if n.endswith("gate_up") else "wd"] = a
        elif n in ("ple_w_gate", "ple_w_proj"):
            w.setdefault("ple_gate" if n.endswith("gate") else "ple_proj", {})[i] = a
    if "ev_in_cat" not in w and all((n, 0) in raw for n in ("ev_w_in", "ev_w_uq", "ev_w_ukv", "ev_w_out")):
        w["ev_in_cat"], w["ev_q_cat"], w["ev_ukv"] = _even_weights(raw["ev_w_in", 0], raw["ev_w_uq", 0], raw["ev_w_ukv", 0])
        w["ev_out"] = raw["ev_w_out", 0]
    if "od_in_pad" not in w and all((n, 0) in raw for n in ("od_w_in", "od_w_out")):
        od_in = raw["od_w_in", 0]
        w["od_in_pad"] = jnp.pad(od_in, ((0, 0), (0, (-od_in.shape[1]) % LANES)))
        w["od_out"] = raw["od_w_out", 0]


def _keys(names, layer):
    return tuple((n, layer) for n in names)


_FFA, _FFB, _PLE = ("ffa_w_gate_up", "ffa_w_down"), ("ffb_w_gate_up", "ffb_w_down"), ("ple_w_gate", "ple_w_proj")
_EV, _OD = ("ev_w_in", "ev_w_uq", "ev_w_ukv", "ev_w_out"), ("od_w_in", "od_w_out")
_GATHER_FIRST = _keys(_FFA[:1], 0)
_GATHER_RIDES = {("ffa", 0): _keys(_FFA[1:] + _EV, 0), ("mix", 0): _keys(_FFB + _PLE, 0) + _keys(_FFA, 1),
                 ("ffb", 0): _keys(_OD, 0), ("mix", 1): _keys(_FFB + _PLE, 1)}
_REDUCE_RIDES = {("mix", 1): _keys(_FFB + _PLE, 1), ("mix", 0): _keys(_FFA, 1) + _keys(_OD, 0) + _keys(_FFB + _PLE, 0),
                 ("ffa", 0): _keys(_EV, 0)}
_REDUCE_LAST = _keys(_FFA, 0)


def _local_step(x, p, tgt, w, ex=None):
    depth = p.shape[0]

    def gather_behind(host, fn, *args):
        keys = None if ex is None else _GATHER_RIDES.get(host)
        if keys is None:
            return fn(*args, None)[:-1]
        done = []

        def install(rode):
            if not done:
                _install_weights(w, ex.gather_finish(keys, rode, name=f"weight_forward_{host[0]}{host[1]}"))
                done.append(True)

        res = fn(*args, ex.gather_rider(keys), install) if fn is _ffn_fwd else fn(*args, ex.gather_rider(keys))
        install(res[-1])
        return res[:-1]

    h = x
    saved = []
    for i in range(depth):
        sv = {}
        h, sv["ffa"] = gather_behind(("ffa", i), _ffn_fwd, h, w["ffa_norm"][i:i + 1], w["ffa"][i], f"ffa{i}")
        h, sv["mix"] = gather_behind(("mix", i), _even_fwd if i % 2 == 0 else _odd_fwd, h, w, i)
        h, sv["ffb"] = gather_behind(("ffb", i), _ffn_fwd, h, w["ffb_norm"][i:i + 1], w["ffb"][i], f"ffb{i}")
        h_in = h
        h, xn, gate, pp = _ple_fwd(h, w["ple_norm"][i:i + 1], w["ple_gate"][i], p[i], w["ple_proj"][i], name=f"ple{i}")
        sv["ple"] = dict(h_in=h_in, xn=xn, gate=gate, pp=pp)
        saved.append(sv)
    loss_vec, dh, d_final = _final_loss(h, w["final_norm"].reshape(1, -1), tgt, name="final_loss")

    per_layer = [dict() for _ in range(depth)]
    mats = {}
    grads = {}

    def reduce_behind(host, fn, *args):
        keys = None if ex is None else _REDUCE_RIDES.get(host)
        state = None if keys is None else ex.reduce_begin(keys, mats, tag=f"{host[0]}{host[1]}")
        res = fn(*args, None if keys is None else state[0])
        if keys is not None:
            ex.reduce_finish(state, res[-1])
        return res[:-1]

    for i in reversed(range(depth)):
        sv, gl = saved[i], per_layer[i]
        dz, dpp = _ple_bwd_elem(dh, sv["ple"]["gate"], sv["ple"]["pp"], name=f"ple{i}_bwd")
        mats["ple_w_gate", i] = _mm_tn(sv["ple"]["xn"], dz, name=f"ple{i}_dwg")
        mats["ple_w_proj", i] = _mm_tn(p[i], dpp, name=f"ple{i}_dwp")
        dh, gl["ple_norm"] = _mm_nt_rmsbwd([(dz, w["ple_gate"][i])], sv["ple"]["h_in"], w["ple_norm"][i:i + 1], dh,
                                           name=f"ple{i}_dx")
        dh, gl["ffb_norm"], mats["ffb_w_gate_up", i], mats["ffb_w_down", i] = reduce_behind(
            ("ffb", i), _ffn_bwd, dh, w["ffb_norm"][i:i + 1], w["ffb"][i], sv["ffb"], f"ffb{i}")
        dh, gl["mix_norm"], gm = reduce_behind(("mix", i), _even_bwd if i % 2 == 0 else _odd_bwd, dh, w, sv["mix"], i)
        for n, g in gm.items():
            if n in REPLICATED:
                grads[n] = g
            else:
                mats[n, 0] = g
        dh, gl["ffa_norm"], mats["ffa_w_gate_up", i], mats["ffa_w_down", i] = reduce_behind(
            ("ffa", i), _ffn_bwd, dh, w["ffa_norm"][i:i + 1], w["ffa"][i], sv["ffa"], f"ffa{i}")
    grads["final_norm"] = d_final.reshape(-1)
    for n in ("ffa_norm", "mix_norm", "ffb_norm", "ple_norm"):
        grads[n] = jnp.concatenate([per_layer[i][n] for i in range(depth)], axis=0)
    if ex is not None:
        ex.reduce(_REDUCE_LAST, mats, tag="last")
    else:
        for n, _ in SHARDED:
            grads[n] = [mats[n, i] for i in range(depth) if (n, i) in mats]
    return loss_vec[0, 0], dh, grads


def _cut_mode(local_shape, axis, ncols):
    return "cols" if axis == 2 and ncols % LANES == 0 else "blk"


class _Exchange:
    def __init__(self, wts):
        self.place = jnp.stack([2 * lax.axis_index("x") + lax.axis_index("y"), lax.axis_index("c")]).astype(jnp.int32)
        self.info = {}
        for n, axis in SHARDED:
            wb = wts[n].astype(BF16)
            mode = _cut_mode(wb.shape, axis, wb.shape[2])
            for i in range(wb.shape[0]):
                self.info[n, i] = dict(shard=wb[i], mode=mode, axis=axis)
        self.halves = {}

    def _modes(self, keys):
        return [self.info[k]["mode"] for k in keys]

    def gather_rider(self, keys):
        return _gather_rider([self.info[k]["shard"] for k in keys], self._modes(keys))

    def gather_finish(self, keys, landed, *, name):
        outs = _gather_forward(landed, [self.info[k]["shard"].shape for k in keys], self._modes(keys), name=name)
        got = {}
        for k, dst in zip(keys, outs):
            if self.info[k]["mode"] == "blk":
                dst = dst.reshape(-1, dst.shape[2]) if self.info[k]["axis"] == 1 else jnp.moveaxis(dst, 0, 1).reshape(dst.shape[1], -1)
            got[k] = dst
        return got

    def gather(self, keys, *, name):
        return self.gather_finish(keys, _run_rider(self.gather_rider(keys), name=name), name=name + "_forward")

    def reduce_begin(self, keys, mats, *, tag):
        modes = self._modes(keys)
        arrs = []
        for k in keys:
            g2, (rr, cc) = mats[k], self.info[k]["shard"].shape
            if self.info[k]["mode"] == "blk":
                g2 = g2.reshape(N_CHIPS, rr, cc) if self.info[k]["axis"] == 1 else g2.reshape(rr, N_CHIPS, cc).transpose(1, 0, 2)
            arrs.append(g2)
        landed = _rs_pair_swap(arrs, modes, name=f"rs_pair_swap_{tag}")
        parts = []
        for (n, i), m, a, l in zip(keys, modes, arrs, landed):
            pt = _rs_pair_add(_blk_view(a, m), _blk_view(l, m), self.place, name=f"rs_pair_add_{n}{i}")
            parts.append(pt[0] if m == "cols" else pt)
        return _exchange_rider(parts, modes), keys, parts

    def reduce_finish(self, state, landed):
        _, keys, parts = state
        for (n, i), m, pt, l in zip(keys, self._modes(keys), parts, landed):
            self.halves[n, i] = _rs_chip_sum(pt, l, m, self.place, name=f"rs_chip_sum_{n}{i}")

    def reduce(self, keys, mats, *, tag):
        state = self.reduce_begin(keys, mats, tag=tag)
        self.reduce_finish(state, _run_rider(state[0], name=f"rs_chip_exchange_{tag}"))

    def join(self, wts):
        keys = list(self.info)
        joined = dict(zip(keys, _rs_pair_join([self.halves[k] for k in keys], name="rs_pair_join")))
        return {n: jnp.stack([joined[n, i] for i in range(wts[n].shape[0])]).reshape(wts[n].shape) for n, _ in SHARDED}


def _small_rows(vals):
    rows = []
    for n in REPLICATED:
        v = vals[n].reshape(-1)
        rows.append(jnp.pad(v, (0, (-v.shape[0]) % FLAT_COLS)).reshape(-1, FLAT_COLS))
    out = jnp.concatenate(rows, axis=0)
    return jnp.pad(out, ((0, (-out.shape[0]) % 8), (0, 0)))


def kernel(x, p, ffa_norm, ffa_w_gate_up, ffa_w_down, mix_norm, ffb_norm, ffb_w_gate_up, ffb_w_down, ple_norm, ple_w_gate, ple_w_proj, ev_w_in, ev_sinks, ev_cq_norm, ev_w_uq, ev_ckv_norm, ev_w_ukv, ev_w_out, od_w_in, od_b_f, od_w_out, final_norm, loss_target, m_ffa_norm, m_ffa_w_gate_up, m_ffa_w_down, m_mix_norm, m_ffb_norm, m_ffb_w_gate_up, m_ffb_w_down, m_ple_norm, m_ple_w_gate, m_ple_w_proj, m_ev_w_in, m_ev_sinks, m_ev_cq_norm, m_ev_w_uq, m_ev_ckv_norm, m_ev_w_ukv, m_ev_w_out, m_od_w_in, m_od_b_f, m_od_w_out, m_final_norm, v_ffa_norm, v_ffa_w_gate_up, v_ffa_w_down, v_mix_norm, v_ffb_norm, v_ffb_w_gate_up, v_ffb_w_down, v_ple_norm, v_ple_w_gate, v_ple_w_proj, v_ev_w_in, v_ev_sinks, v_ev_cq_norm, v_ev_w_uq, v_ev_ckv_norm, v_ev_w_ukv, v_ev_w_out, v_od_w_in, v_od_b_f, v_od_w_out, v_final_norm):
    env = dict(locals())
    wts = {n: env[n] for n in WEIGHT_ORDER}
    mom1 = {n: env["m_" + n] for n in WEIGHT_ORDER}
    mom2 = {n: env["v_" + n] for n in WEIGHT_ORDER}
    ex = _Exchange(wts)

    w = {n: wts[n] for n in REPLICATED}
    _install_weights(w, ex.gather(_GATHER_FIRST, name="weight_gather_first"))

    loss_part, grad_x, grads = _local_step(x[0], p[:, 0], loss_target[0], w, ex)
    loss = lax.psum(loss_part, ("x", "y", "c"))
    gout = ex.join(wts)
    small = _allreduce_small(_small_rows(grads), name="small_allreduce")
    r0 = 0
    for n in REPLICATED:
        size = int(np.prod(wts[n].shape))
        nr = -(-size // FLAT_COLS)
        gout[n] = small[r0:r0 + nr].reshape(-1)[:size].reshape(wts[n].shape)
        r0 += nr

    delta, new_m, new_v = {}, {}, {}
    for n in WEIGHT_ORDER:
        delta[n], new_m[n], new_v[n] = _adamw(wts[n], gout[n], mom1[n], mom2[n], name="adamw_" + n)
    return (loss, grad_x[None], *[gout[n] for n in WEIGHT_ORDER], *[delta[n] for n in WEIGHT_ORDER],
            *[new_m[n] for n in WEIGHT_ORDER], *[new_v[n] for n in WEIGHT_ORDER])
```

```python
import functools
import math

import numpy as np
import jax
import jax.numpy as jnp
from jax import lax
from jax.experimental import pallas as pl
from jax.experimental.pallas import tpu as pltpu

F32 = jnp.float32
BF16 = jnp.bfloat16
NT = (((1,), (1,)), ((), ()))
TN = (((0,), (0,)), ((), ()))
MESH = pl.DeviceIdType.MESH

RMS_EPS = 1e-6
FFN_RES_SCALE = 0.5
A_HEADS, A_KV_HEADS, A_HEAD_DIM, WINDOW = 8, 2, 64, 128
B_HEADS, B_Q_LORA, B_KV_LORA, B_NOPE, B_ROPE, B_V = 8, 256, 128, 64, 32, 64
ROPE_THETA = 10000.0
C_HEADS, C_HEAD_DIM = 16, 64
ADAM_LR, ADAM_B1, ADAM_B2, ADAM_EPS, ADAM_WD, ADAM_STEP = 0.001, 0.9, 0.999, 1e-08, 0.01, 10

N_CHIPS = 4
LANES = 128
FLAT_COLS = 1024
MASK_VALUE = -1e30
VMEM_LIMIT = 48 * 2**20

SHARDED = (
    ("ffa_w_gate_up", 2), ("ffa_w_down", 1), ("ffb_w_gate_up", 2), ("ffb_w_down", 1),
    ("ple_w_gate", 1), ("ple_w_proj", 2), ("ev_w_in", 2), ("ev_w_uq", 2), ("ev_w_ukv", 2),
    ("ev_w_out", 1), ("od_w_in", 2), ("od_w_out", 1))
REPLICATED = ("ffa_norm", "mix_norm", "ffb_norm", "ple_norm", "final_norm",
              "ev_sinks", "ev_cq_norm", "ev_ckv_norm", "od_b_f")
WEIGHT_ORDER = ("ffa_norm", "ffa_w_gate_up", "ffa_w_down", "mix_norm", "ffb_norm", "ffb_w_gate_up",
                "ffb_w_down", "ple_norm", "ple_w_gate", "ple_w_proj", "ev_w_in", "ev_sinks",
                "ev_cq_norm", "ev_w_uq", "ev_ckv_norm", "ev_w_ukv", "ev_w_out", "od_w_in", "od_b_f",
                "od_w_out", "final_norm")


def _cp(*sem):
    return pltpu.CompilerParams(dimension_semantics=sem, vmem_limit_bytes=VMEM_LIMIT)


def _sigmoid(z):
    return 1.0 / (1.0 + jnp.exp(-z))


def _rms_stats(xv):
    r = lax.rsqrt(jnp.mean(xv * xv, axis=-1, keepdims=True) + RMS_EPS)
    return r, xv * r


def _rms_bwd(dxn, xv, g):
    r, xhat = _rms_stats(xv)
    u = dxn * g
    dx = r * (u - xhat * jnp.mean(u * xhat, axis=-1, keepdims=True))
    return dx, dxn * xhat


def _col_tile(k_rows, n, budget_bytes=6 * 2**20):
    if k_rows * n * 4 <= budget_bytes or n % LANES:
        return n
    units = n // LANES
    best = LANES
    for d in range(1, units + 1):
        if units % d == 0 and k_rows * d * LANES * 4 <= budget_bytes:
            best = d * LANES
    return best


def _row_tile(rows, cols, target_elems=2**18):
    if rows * cols <= target_elems or rows % 8:
        return rows
    best = 8
    for d in range(8, rows + 1, 8):
        if rows % d == 0 and d * cols <= target_elems:
            best = d
    return best


def _fox_in_fwd(x, g, w, *, nheads, dh, q_ones, k_ones, name, tm=512):
    s, k = x.shape
    n = w.shape[1]
    wd = nheads * dh
    spare = LANES - dh

    def body(x_ref, g_ref, w_ref, q_ref, k_ref, v_ref, vt_ref, f_ref, xn_ref):
        _, xhat = _rms_stats(x_ref[...])
        xn = (xhat * g_ref[...]).astype(BF16)
        xn_ref[...] = xn
        y = jnp.dot(xn, w_ref[...], preferred_element_type=F32)
        f_ref[...] = y[:, 3 * wd:]
        lane = lax.broadcasted_iota(jnp.int32, (tm, spare), 1)

        def fill(cols):
            return functools.reduce(jnp.logical_or, [lane == c for c in cols]).astype(F32)

        q_fill, k_fill = fill(q_ones), fill(k_ones)
        for h in range(nheads):
            q_ref[h] = jnp.concatenate([y[:, h * dh:(h + 1) * dh], q_fill], axis=-1).astype(BF16)
            k_ref[h] = jnp.concatenate([y[:, wd + h * dh:wd + (h + 1) * dh], k_fill], axis=-1).astype(BF16)
            vh = y[:, 2 * wd + h * dh:2 * wd + (h + 1) * dh]
            v_ref[h] = vh.astype(BF16)
            vt_ref[h] = vh.T.astype(BF16)

    wide = pl.BlockSpec((nheads, tm, LANES), lambda i: (0, i, 0))
    return pl.pallas_call(
        body, name=name, grid=(s // tm,),
        in_specs=[pl.BlockSpec((tm, k), lambda i: (i, 0)), pl.BlockSpec((1, k), lambda i: (0, 0)),
                  pl.BlockSpec((k, n), lambda i: (0, 0))],
        out_specs=[wide, wide, pl.BlockSpec((nheads, tm, dh), lambda i: (0, i, 0)),
                   pl.BlockSpec((nheads, dh, tm), lambda i: (0, 0, i)), pl.BlockSpec((tm, LANES), lambda i: (i, 0)),
                   pl.BlockSpec((tm, k), lambda i: (i, 0))],
        out_shape=[jax.ShapeDtypeStruct((nheads, s, LANES), BF16)] * 2
        + [jax.ShapeDtypeStruct((nheads, s, dh), BF16), jax.ShapeDtypeStruct((nheads, dh, s), BF16),
           jax.ShapeDtypeStruct((s, LANES), F32), jax.ShapeDtypeStruct((s, k), BF16)],
        compiler_params=_cp("arbitrary"))(x, g, w)


def _merge_heads(dq, dk, dvt, *, dh, name, tm=512):
    nheads, s, _ = dq.shape

    def body(dq_ref, dk_ref, dvt_ref, o_ref):
        pieces = [dq_ref[h][:, :dh] for h in range(nheads)] + [dk_ref[h][:, :dh] for h in range(nheads)]
        pieces += [dvt_ref[h].T for h in range(nheads)]
        o_ref[...] = jnp.concatenate(pieces, axis=-1)

    wide = pl.BlockSpec((nheads, tm, LANES), lambda i: (0, i, 0))
    return pl.pallas_call(
        body, name=name, grid=(s // tm,),
        in_specs=[wide, wide, pl.BlockSpec((nheads, dh, tm), lambda i: (0, 0, i))],
        out_specs=pl.BlockSpec((tm, 3 * nheads * dh), lambda i: (i, 0)),
        out_shape=jax.ShapeDtypeStruct((s, 3 * nheads * dh), F32),
        compiler_params=_cp("arbitrary"))(dq, dk, dvt)


def _row_call(body, n_rows, ins, outs, *, name, tm=512):
    def spec(a, axis):
        shape = a.shape
        if axis is None:
            return pl.BlockSpec(shape, lambda i: (0,) * len(shape))
        blk = tuple(tm if d == axis else n for d, n in enumerate(shape))
        return pl.BlockSpec(blk, lambda i: tuple(i if d == axis else 0 for d in range(len(shape))))

    return pl.pallas_call(
        body, name=name, grid=(n_rows // tm,), in_specs=[spec(a, ax) for a, ax in ins],
        out_specs=[spec(a, ax) for a, ax in outs], out_shape=[a for a, _ in outs],
        compiler_params=_cp("arbitrary"))(*[a for a, _ in ins])


def _sds(shape, dtype):
    return jax.ShapeDtypeStruct(shape, dtype)


def _ev_in_fwd(x, g, w, *, name):
    s, k = x.shape
    d = A_HEAD_DIM

    def body(x_ref, g_ref, w_ref, q_ref, k_ref, v_ref, vt_ref, cq_ref, ckv_ref, kr_ref, xn_ref):
        _, xhat = _rms_stats(x_ref[...])
        xn = (xhat * g_ref[...]).astype(BF16)
        xn_ref[...] = xn
        y = jnp.dot(xn, w_ref[...], preferred_element_type=F32)
        for h in range(A_HEADS):
            q_ref[h] = y[:, h * d:(h + 1) * d].astype(BF16)
        for h in range(A_KV_HEADS):
            k_ref[h] = y[:, 512 + h * d:512 + (h + 1) * d].astype(BF16)
            vh = y[:, 640 + h * d:640 + (h + 1) * d]
            v_ref[h] = vh.astype(BF16)
            vt_ref[h] = vh.T.astype(BF16)
        cq_ref[...] = y[:, 768:1024]
        ckv_ref[...] = y[:, 1024:1152]
        kr_ref[...] = y[:, 1152:1280]

    return _row_call(
        body, s, [(x, 0), (g, None), (w, None)],
        [(_sds((A_HEADS, s, d), BF16), 1), (_sds((A_KV_HEADS, s, d), BF16), 1), (_sds((A_KV_HEADS, s, d), BF16), 1),
         (_sds((A_KV_HEADS, d, s), BF16), 2), (_sds((s, B_Q_LORA), F32), 0), (_sds((s, B_KV_LORA), F32), 0),
         (_sds((s, LANES), F32), 0), (_sds((s, k), BF16), 0)], name=name)


def _ev_q_fwd(x, g, w, cos, sin, *, name):
    s, k = x.shape
    rot = B_HEADS * B_ROPE

    def body(x_ref, g_ref, w_ref, c_ref, s_ref, q_ref, xn_ref):
        _, xhat = _rms_stats(x_ref[...])
        xn = (xhat * g_ref[...]).astype(BF16)
        xn_ref[...] = xn
        y = jnp.dot(xn, w_ref[...], preferred_element_type=F32)
        ro = y[:, 512:512 + rot] * c_ref[...] + y[:, 512 + rot:] * s_ref[...]
        zero = jnp.zeros((y.shape[0], LANES - B_NOPE - B_ROPE), F32)
        for h in range(B_HEADS):
            q_ref[h] = jnp.concatenate([y[:, h * B_NOPE:(h + 1) * B_NOPE], ro[:, h * B_ROPE:(h + 1) * B_ROPE], zero],
                                       axis=-1).astype(BF16)

    return _row_call(body, s, [(x, 0), (g, None), (w, None), (cos, 0), (sin, 0)],
                     [(_sds((B_HEADS, s, LANES), BF16), 1), (_sds((s, k), BF16), 0)], name=name)


def _ev_kv_fwd(x, g, w, kro, *, name):
    s, k = x.shape
    per = B_NOPE + B_V

    def body(x_ref, g_ref, w_ref, kr_ref, k_ref, v_ref, vt_ref, xn_ref):
        _, xhat = _rms_stats(x_ref[...])
        xn = (xhat * g_ref[...]).astype(BF16)
        xn_ref[...] = xn
        y = jnp.dot(xn, w_ref[...], preferred_element_type=F32)
        kr = kr_ref[...]
        zero = jnp.zeros((y.shape[0], LANES - B_NOPE - B_ROPE), F32)
        for h in range(B_HEADS):
            k_ref[h] = jnp.concatenate([y[:, h * per:h * per + B_NOPE], kr, zero], axis=-1).astype(BF16)
            vh = y[:, h * per + B_NOPE:(h + 1) * per]
            v_ref[h] = vh.astype(BF16)
            vt_ref[h] = vh.T.astype(BF16)

    return _row_call(body, s, [(x, 0), (g, None), (w, None), (kro, 0)],
                     [(_sds((B_HEADS, s, LANES), BF16), 1), (_sds((B_HEADS, s, B_V), BF16), 1),
                      (_sds((B_HEADS, B_V, s), BF16), 2), (_sds((s, k), BF16), 0)], name=name)


def _ev_q_merge(dq, cos, sin, *, name):
    nh, s, _ = dq.shape

    def body(dq_ref, c_ref, s_ref, o_ref):
        dro = jnp.concatenate([dq_ref[h][:, B_NOPE:B_NOPE + B_ROPE] for h in range(nh)], axis=-1)
        o_ref[...] = jnp.concatenate([dq_ref[h][:, :B_NOPE] for h in range(nh)] + [dro * c_ref[...], dro * s_ref[...]], axis=-1)

    return _row_call(body, s, [(dq, 1), (cos, 0), (sin, 0)], [(_sds((s, 2 * nh * B_NOPE), F32), 0)], name=name)[0]


def _ev_kv_merge(dk, dvt, cos, sin, *, name):
    nh, s, _ = dk.shape

    def body(dk_ref, dvt_ref, c_ref, s_ref, o_ref, kr_ref):
        pieces = []
        tot = None
        for h in range(nh):
            pieces += [dk_ref[h][:, :B_NOPE], dvt_ref[h].T]
            rot = dk_ref[h][:, B_NOPE:B_NOPE + B_ROPE]
            tot = rot if tot is None else tot + rot
        o_ref[...] = jnp.concatenate(pieces, axis=-1)
        kr_ref[...] = jnp.concatenate([tot * c_ref[...], tot * s_ref[...], jnp.zeros((tot.shape[0], LANES - 2 * B_ROPE), F32)],
                                      axis=-1)

    return _row_call(body, s, [(dk, 1), (dvt, 2), (cos, 0), (sin, 0)],
                     [(_sds((s, nh * (B_NOPE + B_V)), F32), 0), (_sds((s, LANES), F32), 0)], name=name)


def _ev_in_merge(dq, dk, dvt, dcq, dckv, dkr, *, name):
    s = dcq.shape[0]

    def body(dq_ref, dk_ref, dvt_ref, cq_ref, ckv_ref, kr_ref, o_ref):
        pieces = [dq_ref[h] for h in range(A_HEADS)] + [dk_ref[h] for h in range(A_KV_HEADS)]
        pieces += [dvt_ref[h].T for h in range(A_KV_HEADS)] + [cq_ref[...], ckv_ref[...], kr_ref[...]]
        o_ref[...] = jnp.concatenate(pieces, axis=-1)

    return _row_call(body, s, [(dq, 1), (dk, 1), (dvt, 2), (dcq, 0), (dckv, 0), (dkr, 0)],
                     [(_sds((s, 1280), F32), 0)], name=name)[0]


def _ffn_up(x, g, wgu, *, name, tm=512, rider=None):
    s, k = x.shape
    f = wgu.shape[1] // 2
    tn = _col_tile(k, f)
    nj = f // tn

    def body(x_ref, g_ref, wg_ref, wu_ref, gate_ref, up_ref, act_ref, xn_ref, xn_sc):
        @pl.when(pl.program_id(1) == 0)
        def _():
            _, xhat = _rms_stats(x_ref[...])
            xn = (xhat * g_ref[...]).astype(BF16)
            xn_sc[...] = xn
            xn_ref[...] = xn

        xn = xn_sc[...]
        gg = jnp.dot(xn, wg_ref[...], preferred_element_type=F32)
        uu = jnp.dot(xn, wu_ref[...], preferred_element_type=F32)
        gate_ref[...] = gg.astype(BF16)
        up_ref[...] = uu.astype(BF16)
        act_ref[...] = ((gg * _sigmoid(gg)) * uu).astype(BF16)

    tile = pl.BlockSpec((tm, tn), lambda i, j: (i, j))
    return _call_with_rider(
        body, rider, name=name, grid=(s // tm, nj),
        in_specs=[pl.BlockSpec((tm, k), lambda i, j: (i, 0)), pl.BlockSpec((1, k), lambda i, j: (0, 0)),
                  pl.BlockSpec((k, tn), lambda i, j: (0, j)), pl.BlockSpec((k, tn), lambda i, j: (0, j + nj))],
        out_specs=[tile, tile, tile, pl.BlockSpec((tm, k), lambda i, j: (i, 0))],
        out_shape=[jax.ShapeDtypeStruct((s, f), BF16)] * 3 + [jax.ShapeDtypeStruct((s, k), BF16)],
        scratch_shapes=[pltpu.VMEM((tm, k), BF16)],
        compiler_params=_cp("arbitrary", "arbitrary"), args=(x, g, wgu, wgu))


def _mm_res_fwd(a, w, res, *, scale, name, tm=512):
    s, k = a.shape
    n = w.shape[1]

    def body(a_ref, w_ref, r_ref, o_ref):
        o_ref[...] = r_ref[...] + scale * jnp.dot(a_ref[...], w_ref[...], preferred_element_type=F32)

    return pl.pallas_call(
        body, name=name, grid=(s // tm,),
        in_specs=[pl.BlockSpec((tm, k), lambda i: (i, 0)), pl.BlockSpec((k, n), lambda i: (0, 0)),
                  pl.BlockSpec((tm, n), lambda i: (i, 0))],
        out_specs=pl.BlockSpec((tm, n), lambda i: (i, 0)),
        out_shape=jax.ShapeDtypeStruct((s, n), F32),
        compiler_params=_cp("arbitrary"))(a, w, res)


def _ffn_down_bwd(dh, wd, gate, up, *, scale, name, tm=512, rider=None):
    s, d = dh.shape
    f = wd.shape[0]
    tn = _col_tile(d, f)

    def body(dh_ref, wd_ref, gate_ref, up_ref, dg_ref, du_ref):
        dhb = (dh_ref[...] * scale).astype(BF16)
        da = lax.dot_general(dhb, wd_ref[...], NT, preferred_element_type=F32)
        gg = gate_ref[...].astype(F32)
        uu = up_ref[...].astype(F32)
        sg = _sigmoid(gg)
        dg_ref[...] = (da * uu * (sg * (1.0 + gg * (1.0 - sg)))).astype(BF16)
        du_ref[...] = (da * (gg * sg)).astype(BF16)

    tile = pl.BlockSpec((tm, tn), lambda i, j: (i, j))
    return _call_with_rider(
        body, rider, name=name, grid=(s // tm, f // tn),
        in_specs=[pl.BlockSpec((tm, d), lambda i, j: (i, 0)), pl.BlockSpec((tn, d), lambda i, j: (j, 0)), tile, tile],
        out_specs=[tile, tile],
        out_shape=[jax.ShapeDtypeStruct((s, f), BF16)] * 2, scratch_shapes=[],
        compiler_params=_cp("arbitrary", "arbitrary"), args=(dh, wd, gate, up))


def _mm_tn(a, bs, *, name, b_scale=1.0, ts=512, rider=None):
    bs = list(bs) if isinstance(bs, (list, tuple)) else [bs]
    s, k = a.shape
    n = bs[0].shape[1]
    tn = _col_tile(k, n, 12 * 2**20)
    per = n // tn

    def body(a_ref, *refs):
        b_refs, o_ref = refs[:-1], refs[-1]
        j = pl.program_id(0)

        @pl.when(pl.program_id(1) == 0)
        def _():
            o_ref[...] = jnp.zeros_like(o_ref)

        for m, b_ref in enumerate(b_refs):
            def acc(b_ref=b_ref):
                bv = b_ref[...]
                if b_scale != 1.0:
                    bv = bv * b_scale
                o_ref[...] += lax.dot_general(a_ref[...].astype(BF16), bv.astype(BF16), TN, preferred_element_type=F32)

            if len(b_refs) == 1:
                acc()
            else:
                pl.when(jnp.logical_and(j >= m * per, j < (m + 1) * per))(acc)

    def b_spec(m):
        def idx(j, t):
            mine = jnp.logical_and(j >= m * per, j < (m + 1) * per)
            return (jnp.where(mine, t, 0), jnp.clip(j - m * per, 0, per - 1))
        return pl.BlockSpec((ts, tn), idx)

    (out,), rode = _call_with_rider(
        body, rider, name=name, grid=(per * len(bs), s // ts),
        in_specs=[pl.BlockSpec((ts, k), lambda j, t: (t, 0))] + [b_spec(m) for m in range(len(bs))],
        out_specs=[pl.BlockSpec((k, tn), lambda j, t: (0, j))],
        out_shape=[jax.ShapeDtypeStruct((k, n * len(bs)), F32)], scratch_shapes=[],
        compiler_params=_cp("arbitrary", "arbitrary"), args=(a, *bs))
    return out if rider is None else (out, rode)


def _mm_nt(dy, w, *, name, tm=512):
    s, n = dy.shape
    k = w.shape[0]

    def body(dy_ref, w_ref, o_ref):
        o_ref[...] = lax.dot_general(dy_ref[...].astype(BF16), w_ref[...], NT, preferred_element_type=F32)

    return pl.pallas_call(
        body, name=name, grid=(s // tm,),
        in_specs=[pl.BlockSpec((tm, n), lambda i: (i, 0)), pl.BlockSpec((k, n), lambda i: (0, 0))],
        out_specs=pl.BlockSpec((tm, k), lambda i: (i, 0)),
        out_shape=jax.ShapeDtypeStruct((s, k), F32),
        compiler_params=_cp("arbitrary"))(dy, w)


def _mm_nt_rmsbwd(pairs, x, g, dres, *, name, tm=256, rider=None):
    s, k = x.shape
    npairs = len(pairs)
    pairs = [pr if len(pr) == 3 else (pr[0], pr[1], 0) for pr in pairs]

    def body(*refs):
        dy_refs = refs[0:2 * npairs:2]
        w_refs = refs[1:2 * npairs:2]
        rest = refs[2 * npairs:]
        x_ref, g_ref = rest[0], rest[1]
        if dres is None:
            dx_ref, dg_ref = rest[2], rest[3]
        else:
            dres_ref, dx_ref, dg_ref = rest[2], rest[3], rest[4]
        dxn = None
        for dy_ref, w_ref in zip(dy_refs, w_refs):
            t = lax.dot_general(dy_ref[...].astype(BF16), w_ref[...], NT, preferred_element_type=F32)
            dxn = t if dxn is None else dxn + t
        dx, dgrow = _rms_bwd(dxn, x_ref[...], g_ref[...])
        if dres is not None:
            dx = dx + dres_ref[...]
        dx_ref[...] = dx

        @pl.when(pl.program_id(0) == 0)
        def _():
            dg_ref[...] = jnp.zeros_like(dg_ref)

        dg_ref[...] += jnp.sum(dgrow, axis=0, keepdims=True)

    in_specs, args = [], []
    for dy, w, cb in pairs:
        n = dy.shape[1]
        in_specs += [pl.BlockSpec((tm, n), lambda i: (i, 0)), pl.BlockSpec((k, n), lambda i, cb=cb: (0, cb))]
        args += [dy, w]
    row = pl.BlockSpec((tm, k), lambda i: (i, 0))
    vec = pl.BlockSpec((1, k), lambda i: (0, 0))
    in_specs += [row, vec]
    args += [x, g]
    if dres is not None:
        in_specs.append(row)
        args.append(dres)
    (dx, dgain), rode = _call_with_rider(
        body, rider, name=name, grid=(s // tm,), in_specs=in_specs, out_specs=[row, vec],
        out_shape=[jax.ShapeDtypeStruct((s, k), F32), jax.ShapeDtypeStruct((1, k), F32)], scratch_shapes=[],
        compiler_params=_cp("arbitrary"), args=args)
    return (dx, dgain) if rider is None else (dx, dgain, rode)


def _ple_fwd(h, g, wg, p, wp, *, name, tm=512):
    s, d = h.shape
    pd = p.shape[1]

    def body(h_ref, g_ref, wg_ref, p_ref, wp_ref, o_ref, xn_ref, gate_ref, pp_ref):
        hv = h_ref[...]
        _, xhat = _rms_stats(hv)
        xn = (xhat * g_ref[...]).astype(BF16)
        xn_ref[...] = xn
        gate = _sigmoid(jnp.dot(xn, wg_ref[...], preferred_element_type=F32))
        pp = jnp.dot(p_ref[...].astype(BF16), wp_ref[...], preferred_element_type=F32)
        gate_ref[...] = gate.astype(BF16)
        pp_ref[...] = pp.astype(BF16)
        o_ref[...] = hv + gate * pp

    row = pl.BlockSpec((tm, d), lambda i: (i, 0))
    return pl.pallas_call(
        body, name=name, grid=(s // tm,),
        in_specs=[row, pl.BlockSpec((1, d), lambda i: (0, 0)), pl.BlockSpec((d, d), lambda i: (0, 0)),
                  pl.BlockSpec((tm, pd), lambda i: (i, 0)), pl.BlockSpec((pd, d), lambda i: (0, 0))],
        out_specs=[row, row, row, row],
        out_shape=[jax.ShapeDtypeStruct((s, d), F32)] + [jax.ShapeDtypeStruct((s, d), BF16)] * 3,
        compiler_params=_cp("arbitrary"))(h, g, wg, p, wp)


def _ple_bwd_elem(dh, gate, pp, *, name, tm=512):
    s, d = dh.shape

    def body(dh_ref, gate_ref, pp_ref, dz_ref, dpp_ref):
        dhv = dh_ref[...]
        gt = gate_ref[...].astype(F32)
        dz_ref[...] = (dhv * pp_ref[...].astype(F32) * (gt * (1.0 - gt))).astype(BF16)
        dpp_ref[...] = (dhv * gt).astype(BF16)

    row = pl.BlockSpec((tm, d), lambda i: (i, 0))
    return pl.pallas_call(
        body, name=name, grid=(s // tm,), in_specs=[row, row, row], out_specs=[row, row],
        out_shape=[jax.ShapeDtypeStruct((s, d), BF16)] * 2,
        compiler_params=_cp("arbitrary"))(dh, gate, pp)


def _final_loss(h, g, tgt, *, name, tm=512):
    s, d = h.shape

    def body(h_ref, g_ref, t_ref, loss_ref, dh_ref, dg_ref):
        @pl.when(pl.program_id(0) == 0)
        def _():
            loss_ref[...] = jnp.zeros_like(loss_ref)
            dg_ref[...] = jnp.zeros_like(dg_ref)

        hv = h_ref[...]
        gv = g_ref[...]
        _, xhat = _rms_stats(hv)
        err = xhat * gv - t_ref[...]
        per_row = jnp.mean(err * err, axis=-1, keepdims=True)
        loss_ref[...] += 0.5 * jnp.sum(per_row, axis=0, keepdims=True)
        dx, dgrow = _rms_bwd(err * (1.0 / d), hv, gv)
        dh_ref[...] = dx
        dg_ref[...] += jnp.sum(dgrow, axis=0, keepdims=True)

    row = pl.BlockSpec((tm, d), lambda i: (i, 0))
    vec = pl.BlockSpec((1, d), lambda i: (0, 0))
    return pl.pallas_call(
        body, name=name, grid=(s // tm,), in_specs=[row, vec, row],
        out_specs=[pl.BlockSpec((1, LANES), lambda i: (0, 0)), row, vec],
        out_shape=[jax.ShapeDtypeStruct((1, LANES), F32), jax.ShapeDtypeStruct((s, d), F32),
                   jax.ShapeDtypeStruct((1, d), F32)],
        compiler_params=_cp("arbitrary"))(h, g, tgt)


def _rope_fwd(y1, y2, cos, sin, *, name, tm=512):
    s, r = y1.shape

    def body(a_ref, b_ref, c_ref, s_ref, o_ref):
        o_ref[...] = a_ref[...] * c_ref[...] + b_ref[...] * s_ref[...]

    row = pl.BlockSpec((tm, r), lambda i: (i, 0))
    return pl.pallas_call(
        body, name=name, grid=(s // tm,), in_specs=[row] * 4, out_specs=row,
        out_shape=jax.ShapeDtypeStruct((s, r), F32), compiler_params=_cp("arbitrary"))(y1, y2, cos, sin)


def _split3(v):
    h1 = v.astype(BF16)
    r1 = v - h1.astype(F32)
    h2 = r1.astype(BF16)
    h3 = (r1 - h2.astype(F32)).astype(BF16)
    return h1, h2, h3


def _tri(tb, upper):
    r = lax.broadcasted_iota(jnp.int32, (tb, tb), 0)
    c = lax.broadcasted_iota(jnp.int32, (tb, tb), 1)
    return jnp.where((r <= c) if upper else (r >= c), 1.0, 0.0).astype(BF16)


def _fox_gate_fwd(ft, bf, *, out_scale, name, tb=512):
    nh, s = ft.shape

    def body(f_ref, b_ref, o_ref, carry):
        @pl.when(pl.program_id(0) == 0)
        def _():
            carry[...] = jnp.zeros_like(carry)

        z = f_ref[...] + b_ref[...]
        lf = jnp.minimum(z, 0.0) - jnp.log(1.0 + jnp.exp(-jnp.abs(z)))
        tri = _tri(tb, True)
        cs = sum(jnp.dot(t, tri, preferred_element_type=F32) for t in _split3(lf)) + carry[...]
        for n, term in enumerate(_split3(cs * out_scale)):
            o_ref[n] = term
        carry[...] += jnp.sum(lf, axis=-1, keepdims=True)

    return pl.pallas_call(
        body, name=name, grid=(s // tb,),
        in_specs=[pl.BlockSpec((nh, tb), lambda t: (0, t)), pl.BlockSpec((nh, 1), lambda t: (0, 0))],
        out_specs=pl.BlockSpec((3, nh, tb), lambda t: (0, 0, t)),
        out_shape=jax.ShapeDtypeStruct((3, nh, s), BF16),
        scratch_shapes=[pltpu.VMEM((nh, 1), F32)], compiler_params=_cp("arbitrary"))(ft, bf)


def _fox_gate_bwd(drow, dcol, ft, bf, *, inv_scale, name, tb=512):
    nh, s = ft.shape
    nb = s // tb

    def body(dr_ref, dc_ref, f_ref, b_ref, df_ref, db_ref, carry):
        @pl.when(pl.program_id(0) == 0)
        def _():
            carry[...] = jnp.zeros_like(carry)
            db_ref[...] = jnp.zeros_like(db_ref)

        dc = (dr_ref[...] - dc_ref[...]) * inv_scale
        tri = _tri(tb, False)
        suf = sum(jnp.dot(t, tri, preferred_element_type=F32) for t in _split3(dc)) + carry[...]
        z = f_ref[...] + b_ref[...]
        dz = suf * (1.0 / (1.0 + jnp.exp(z)))
        df_ref[...] = dz
        db_ref[...] += jnp.sum(dz, axis=-1, keepdims=True)
        carry[...] += jnp.sum(dc, axis=-1, keepdims=True)

    rev = pl.BlockSpec((nh, tb), lambda t: (0, nb - 1 - t))
    one = pl.BlockSpec((nh, 1), lambda t: (0, 0))
    return pl.pallas_call(
        body, name=name, grid=(nb,), in_specs=[rev, rev, rev, one], out_specs=[rev, one],
        out_shape=[jax.ShapeDtypeStruct((nh, s), F32), jax.ShapeDtypeStruct((nh, 1), F32)],
        scratch_shapes=[pltpu.VMEM((nh, 1), F32)], compiler_params=_cp("arbitrary"))(drow, dcol, ft, bf)


def _tri_fwd(t, nq):
    i = sum((t >= (r * (r + 1)) // 2).astype(jnp.int32) for r in range(1, nq))
    return i, t - (i * (i + 1)) // 2


def _tri_bwd(t, nq):
    j = sum((t >= r * nq - (r * (r - 1)) // 2).astype(jnp.int32) for r in range(1, nq))
    return j, j + t - (j * nq - (j * (j - 1)) // 2)


def _scores_t(k, q, *, scale, diag):
    s = lax.dot_general(k, q, NT, preferred_element_type=F32) * scale
    if diag:
        r = lax.broadcasted_iota(jnp.int32, s.shape, 0)
        c = lax.broadcasted_iota(jnp.int32, s.shape, 1)
        s = jnp.where(r <= c, s, MASK_VALUE)
    return s


def _causal_fwd_t(q, k, vt, *, scale, name, tq, hb=2, rider=None):
    nh, s, dq = q.shape
    dv = vt.shape[1]
    nq = s // tq
    nsteps = (nq * (nq + 1)) // 2

    def body(q_ref, k_ref, vt_ref, o_ref, lse_ref, m_sc, l_sc, acc_sc):
        i, j = _tri_fwd(pl.program_id(1), nq)

        @pl.when(j == 0)
        def _():
            m_sc[...] = jnp.full_like(m_sc, MASK_VALUE)
            l_sc[...] = jnp.zeros_like(l_sc)
            acc_sc[...] = jnp.zeros_like(acc_sc)

        def step(diag):
            for u in range(hb):
                sc = _scores_t(k_ref[u], q_ref[u], scale=scale, diag=diag)
                m_prev = m_sc[u]
                m_new = jnp.maximum(m_prev, jnp.max(sc, axis=0, keepdims=True))
                alpha = jnp.exp(m_prev - m_new)
                pr = jnp.exp(sc - m_new)
                l_new = alpha * l_sc[u] + jnp.sum(pr, axis=0, keepdims=True)
                acc = alpha * acc_sc[u] + jnp.dot(vt_ref[u], pr.astype(BF16), preferred_element_type=F32)
                if diag:
                    o_ref[u] = (acc / l_new).astype(BF16)
                    lse_ref[u] = m_new + jnp.log(l_new)
                else:
                    m_sc[u], l_sc[u], acc_sc[u] = m_new, l_new, acc

        pl.when(j < i)(functools.partial(step, False))
        pl.when(j == i)(functools.partial(step, True))

    def qi(t):
        return _tri_fwd(t, nq)[0]

    def kj(t):
        return _tri_fwd(t, nq)[1]

    return _call_with_rider(
        body, rider, name=name, grid=(nh // hb, nsteps),
        in_specs=[pl.BlockSpec((hb, tq, dq), lambda hp, t: (hp, qi(t), 0)),
                  pl.BlockSpec((hb, tq, dq), lambda hp, t: (hp, kj(t), 0)),
                  pl.BlockSpec((hb, dv, tq), lambda hp, t: (hp, 0, kj(t)))],
        out_specs=[pl.BlockSpec((hb, dv, tq), lambda hp, t: (hp, 0, qi(t))),
                   pl.BlockSpec((hb, 1, tq), lambda hp, t: (hp, 0, qi(t)))],
        out_shape=[jax.ShapeDtypeStruct((nh, dv, s), BF16), jax.ShapeDtypeStruct((nh, 1, s), F32)],
        scratch_shapes=[pltpu.VMEM((hb, 1, tq), F32), pltpu.VMEM((hb, 1, tq), F32), pltpu.VMEM((hb, dv, tq), F32)],
        compiler_params=_cp("arbitrary", "arbitrary"), args=(q, k, vt))


def _causal_bwd_t(q, k, v, ot, dot_, lse, *, scale, name, tq, hb=2, rider=None):
    nh, s, dq = q.shape
    dv = v.shape[-1]
    nq = s // tq
    nsteps = (nq * (nq + 1)) // 2

    def body(q_ref, k_ref, v_ref, ot_ref, dot_ref, lse_ref, dq_ref, dk_ref, dvt_ref):
        t = pl.program_id(1)
        j, i = _tri_bwd(t, nq)

        @pl.when(t == 0)
        def _():
            dq_ref[...] = jnp.zeros_like(dq_ref)

        def step(diag):
            rows = pl.ds(pl.multiple_of(i * tq, tq), tq)
            for u in range(hb):
                qv, kv, dov = q_ref[u], k_ref[u], dot_ref[u]
                pr = jnp.exp(_scores_t(kv, qv, scale=scale, diag=diag) - lse_ref[u])
                dp = jnp.dot(v_ref[u], dov, preferred_element_type=F32)
                delta = jnp.sum(dov.astype(F32) * ot_ref[u].astype(F32), axis=0, keepdims=True)
                dsb = ((pr * (dp - delta)) * scale).astype(BF16)
                d_v = lax.dot_general(dov, pr.astype(BF16), NT, preferred_element_type=F32)
                d_k = jnp.dot(dsb, qv, preferred_element_type=F32)
                if diag:
                    dvt_ref[u], dk_ref[u] = d_v, d_k
                else:
                    dvt_ref[u] += d_v
                    dk_ref[u] += d_k
                dq_ref[u, rows, :] += lax.dot_general(dsb, kv, TN, preferred_element_type=F32)

        pl.when(i > j)(functools.partial(step, False))
        pl.when(i == j)(functools.partial(step, True))

    def qi(t):
        return _tri_bwd(t, nq)[1]

    def kj(t):
        return _tri_bwd(t, nq)[0]

    rows_q = pl.BlockSpec((hb, tq, dq), lambda hp, t: (hp, qi(t), 0))
    rows_k = pl.BlockSpec((hb, tq, dq), lambda hp, t: (hp, kj(t), 0))
    lanes_q = pl.BlockSpec((hb, dv, tq), lambda hp, t: (hp, 0, qi(t)))
    return _call_with_rider(
        body, rider, name=name, grid=(nh // hb, nsteps),
        in_specs=[rows_q, rows_k, pl.BlockSpec((hb, tq, dv), lambda hp, t: (hp, kj(t), 0)), lanes_q, lanes_q,
                  pl.BlockSpec((hb, 1, tq), lambda hp, t: (hp, 0, qi(t)))],
        out_specs=[pl.BlockSpec((hb, s, dq), lambda hp, t: (hp, 0, 0)), rows_k,
                   pl.BlockSpec((hb, dv, tq), lambda hp, t: (hp, 0, kj(t)))],
        out_shape=[jax.ShapeDtypeStruct((nh, s, dq), F32), jax.ShapeDtypeStruct((nh, s, dq), F32),
                   jax.ShapeDtypeStruct((nh, dv, s), F32)],
        scratch_shapes=[], compiler_params=_cp("arbitrary", "arbitrary"), args=(q, k, v, ot, dot_, lse))


def _swa_scores_t(k, q, dist, ok, *, scale, slope):
    s = lax.dot_general(k, q, NT, preferred_element_type=F32) * scale - slope * dist.astype(F32)
    return jnp.where(ok, s, MASK_VALUE)


def _swa_geometry(tb, w, has_other):
    r = lax.broadcasted_iota(jnp.int32, (tb, tb), 0)
    c = lax.broadcasted_iota(jnp.int32, (tb, tb), 1)
    d_same = c - r
    ok_same = jnp.logical_and(d_same >= 0, d_same < w)

    def other(ncols):
        rr = lax.broadcasted_iota(jnp.int32, (w, ncols), 0)
        cc = lax.broadcasted_iota(jnp.int32, (w, ncols), 1)
        dd = cc + w - rr
        return dd, jnp.logical_and(dd < w, has_other)

    return (d_same, ok_same), other


def _swa_fwd_t(q, k, vt, slopes_sinks, *, scale, window, name, tb=256):
    nh, s, d = q.shape
    nkv = k.shape[0]
    grp = nh // nkv
    w = window
    per = tb // w
    assert tb % w == 0

    def body(q_ref, kc_ref, kp_ref, vc_ref, vp_ref, ss_ref, o_ref, lse_ref):
        kvh, i = pl.program_id(0), pl.program_id(1)
        (d_c, ok_c), other = _swa_geometry(tb, w, i > 0)
        d_p, ok_p = other(tb)
        for g in range(grp):
            h = kvh * grp + g
            slope, sink = ss_ref[0, h], ss_ref[1, h]
            qg = q_ref[g]
            s_c = _swa_scores_t(kc_ref[...], qg, d_c, ok_c, scale=scale, slope=slope)
            s_p = _swa_scores_t(kp_ref[...], qg, d_p, ok_p, scale=scale, slope=slope)
            m = jnp.maximum(jnp.maximum(jnp.max(s_c, axis=0, keepdims=True), jnp.max(s_p, axis=0, keepdims=True)), sink)
            p_c, p_p = jnp.exp(s_c - m), jnp.exp(s_p - m)
            l = jnp.sum(p_c, axis=0, keepdims=True) + jnp.sum(p_p, axis=0, keepdims=True) + jnp.exp(sink - m)
            acc = (jnp.dot(vc_ref[...], p_c.astype(BF16), preferred_element_type=F32)
                   + jnp.dot(vp_ref[...], p_p.astype(BF16), preferred_element_type=F32))
            o_ref[g] = (acc / l).astype(BF16)
            lse_ref[g] = m + jnp.log(l)

    def prev(i):
        return jnp.maximum(i * per - 1, 0)

    return pl.pallas_call(
        body, name=name, grid=(nkv, s // tb),
        in_specs=[pl.BlockSpec((grp, tb, d), lambda kh, i: (kh, i, 0)),
                  pl.BlockSpec((None, tb, d), lambda kh, i: (kh, i, 0)),
                  pl.BlockSpec((None, w, d), lambda kh, i: (kh, prev(i), 0)),
                  pl.BlockSpec((None, d, tb), lambda kh, i: (kh, 0, i)),
                  pl.BlockSpec((None, d, w), lambda kh, i: (kh, 0, prev(i))),
                  pl.BlockSpec(memory_space=pltpu.SMEM)],
        out_specs=[pl.BlockSpec((grp, d, tb), lambda kh, i: (kh, 0, i)), pl.BlockSpec((grp, 1, tb), lambda kh, i: (kh, 0, i))],
        out_shape=[jax.ShapeDtypeStruct((nh, d, s), BF16), jax.ShapeDtypeStruct((nh, 1, s), F32)],
        compiler_params=_cp("arbitrary", "arbitrary"))(q, k, k, vt, vt, slopes_sinks)


def _swa_bwd_t(q, k, v, ot, dot_, lse, slopes_sinks, *, scale, window, name, tb=256, rider=None):
    nh, s, d = q.shape
    nkv = k.shape[0]
    grp = nh // nkv
    w = window
    per = tb // w
    nb = s // tb

    def body(qc_ref, qn_ref, kc_ref, kp_ref, vc_ref, vp_ref, oc_ref, on_ref, doc_ref, don_ref, lc_ref, ln_ref, ss_ref,
             dq_ref, dk_ref, dvt_ref, dsink_ref):
        kvh, i = pl.program_id(0), pl.program_id(1)

        @pl.when(i == 0)
        def _():
            dsink_ref[...] = jnp.zeros_like(dsink_ref)

        (d_c, ok_c), other = _swa_geometry(tb, w, i > 0)
        d_p, ok_p = other(tb)
        d_n, ok_n = _swa_geometry(tb, w, i < nb - 1)[1](w)
        kc, kp, vc, vp = kc_ref[...], kp_ref[...], vc_ref[...], vp_ref[...]
        k_last, v_last = kc[tb - w:, :], vc[tb - w:, :]
        dk_acc = jnp.zeros((tb, d), F32)
        dv_acc = jnp.zeros((d, tb), F32)
        dk_tail = jnp.zeros((w, d), F32)
        dv_tail = jnp.zeros((d, w), F32)
        for g in range(grp):
            h = kvh * grp + g
            slope, sink = ss_ref[0, h], ss_ref[1, h]
            qg, dog, lse_c = qc_ref[g], doc_ref[g], lc_ref[g]
            delta = jnp.sum(dog.astype(F32) * oc_ref[g].astype(F32), axis=0, keepdims=True)
            p_c = jnp.exp(_swa_scores_t(kc, qg, d_c, ok_c, scale=scale, slope=slope) - lse_c)
            p_p = jnp.exp(_swa_scores_t(kp, qg, d_p, ok_p, scale=scale, slope=slope) - lse_c)
            ds_c = ((p_c * (jnp.dot(vc, dog, preferred_element_type=F32) - delta)) * scale).astype(BF16)
            ds_p = ((p_p * (jnp.dot(vp, dog, preferred_element_type=F32) - delta)) * scale).astype(BF16)
            dq_ref[g] = (lax.dot_general(ds_c, kc, TN, preferred_element_type=F32)
                         + lax.dot_general(ds_p, kp, TN, preferred_element_type=F32))
            dk_acc += jnp.dot(ds_c, qg, preferred_element_type=F32)
            dv_acc += lax.dot_general(dog, p_c.astype(BF16), NT, preferred_element_type=F32)
            dsink_ref[g] -= jnp.broadcast_to(jnp.sum(jnp.exp(sink - lse_c) * delta, axis=1, keepdims=True), (1, LANES))
            qn, don = qn_ref[g], don_ref[g]
            delta_n = jnp.sum(don.astype(F32) * on_ref[g].astype(F32), axis=0, keepdims=True)
            p_n = jnp.exp(_swa_scores_t(k_last, qn, d_n, ok_n, scale=scale, slope=slope) - ln_ref[g])
            ds_n = ((p_n * (jnp.dot(v_last, don, preferred_element_type=F32) - delta_n)) * scale).astype(BF16)
            dk_tail += jnp.dot(ds_n, qn, preferred_element_type=F32)
            dv_tail += lax.dot_general(don, p_n.astype(BF16), NT, preferred_element_type=F32)
        dk_ref[...] = dk_acc
        dvt_ref[...] = dv_acc
        dk_ref[tb - w:, :] += dk_tail
        dvt_ref[:, tb - w:] += dv_tail

    def prev(i):
        return jnp.maximum(i * per - 1, 0)

    def nxt(i):
        return jnp.minimum((i + 1) * per, s // w - 1)

    return _call_with_rider(
        body, rider, name=name, grid=(nkv, nb), scratch_shapes=[],
        args=(q, q, k, k, v, v, ot, ot, dot_, dot_, lse, lse, slopes_sinks),
        in_specs=[pl.BlockSpec((grp, tb, d), lambda kh, i: (kh, i, 0)),
                  pl.BlockSpec((grp, w, d), lambda kh, i: (kh, nxt(i), 0)),
                  pl.BlockSpec((None, tb, d), lambda kh, i: (kh, i, 0)),
                  pl.BlockSpec((None, w, d), lambda kh, i: (kh, prev(i), 0)),
                  pl.BlockSpec((None, tb, d), lambda kh, i: (kh, i, 0)),
                  pl.BlockSpec((None, w, d), lambda kh, i: (kh, prev(i), 0)),
                  pl.BlockSpec((grp, d, tb), lambda kh, i: (kh, 0, i)),
                  pl.BlockSpec((grp, d, w), lambda kh, i: (kh, 0, nxt(i))),
                  pl.BlockSpec((grp, d, tb), lambda kh, i: (kh, 0, i)),
                  pl.BlockSpec((grp, d, w), lambda kh, i: (kh, 0, nxt(i))),
                  pl.BlockSpec((grp, 1, tb), lambda kh, i: (kh, 0, i)),
                  pl.BlockSpec((grp, 1, w), lambda kh, i: (kh, 0, nxt(i))),
                  pl.BlockSpec(memory_space=pltpu.SMEM)],
        out_specs=[pl.BlockSpec((grp, tb, d), lambda kh, i: (kh, i, 0)),
                   pl.BlockSpec((None, tb, d), lambda kh, i: (kh, i, 0)),
                   pl.BlockSpec((None, d, tb), lambda kh, i: (kh, 0, i)),
                   pl.BlockSpec((None, grp, 1, LANES), lambda kh, i: (kh, 0, 0, 0))],
        out_shape=[jax.ShapeDtypeStruct((nh, s, d), F32), jax.ShapeDtypeStruct((nkv, s, d), F32),
                   jax.ShapeDtypeStruct((nkv, d, s), F32), jax.ShapeDtypeStruct((nkv, grp, 1, LANES), F32)],
        compiler_params=_cp("arbitrary", "arbitrary"))


def _adamw(w, g, m, v, *, name):
    shape = w.shape
    cols = shape[-1]
    rows = int(np.prod(shape[:-1])) if len(shape) > 1 else 1
    tr = _row_tile(rows, cols)
    c1 = 1.0 - ADAM_B1 ** ADAM_STEP
    c2 = 1.0 - ADAM_B2 ** ADAM_STEP

    def body(w_ref, g_ref, m_ref, v_ref, d_ref, mo_ref, vo_ref):
        gv = g_ref[...]
        mn = ADAM_B1 * m_ref[...] + (1.0 - ADAM_B1) * gv
        vn = ADAM_B2 * v_ref[...] + (1.0 - ADAM_B2) * (gv * gv)
        mo_ref[...] = mn
        vo_ref[...] = vn
        d_ref[...] = -ADAM_LR * ((mn / c1) / (jnp.sqrt(vn / c2) + ADAM_EPS) + ADAM_WD * w_ref[...])

    blk = pl.BlockSpec((tr, cols), lambda i: (i, 0))
    outs = pl.pallas_call(
        body, name=name, grid=(rows // tr,), in_specs=[blk] * 4, out_specs=[blk] * 3,
        out_shape=[jax.ShapeDtypeStruct((rows, cols), F32)] * 3,
        compiler_params=_cp("arbitrary"))(*[a.reshape(rows, cols) for a in (w, g, m, v)])
    return tuple(a.reshape(shape) for a in outs)


def _hbm_spec():
    return pl.BlockSpec(memory_space=pl.ANY)


def _mesh_place():
    x, y, c = lax.axis_index("x"), lax.axis_index("y"), lax.axis_index("c")
    return x, y, c, [(1 - x, y), (x, 1 - y), (1 - x, 1 - y)]


def _half_rows(c, rows, align):
    return pl.ds(pl.multiple_of(c * (rows // 2), align), rows // 2)


def _part(ref, mode, k, n, rows=None):
    if mode == "cols":
        cols = pl.ds(pl.multiple_of(k * n, LANES), n)
        return ref.at[:, cols] if rows is None else ref.at[rows, cols]
    return ref.at[k] if rows is None else ref.at[k, rows, :]


class _Rider:
    def __init__(self, inputs, out_shape, n_sems, start, finish):
        self.inputs, self.out_shape, self.n_sems, self.start, self.finish = inputs, out_shape, n_sems, start, finish


def _call_with_rider(body, rider, *, name, grid, in_specs, out_specs, out_shape, scratch_shapes, compiler_params, args):
    if rider is None:
        outs = pl.pallas_call(body, name=name, grid=grid, in_specs=in_specs, out_specs=out_specs, out_shape=out_shape,
                              scratch_shapes=scratch_shapes, compiler_params=compiler_params)(*args)
        return outs, []
    n_in, n_out, n_sc = len(in_specs), len(out_specs), len(scratch_shapes)
    n_rin, n_rout = len(rider.inputs), len(rider.out_shape)

    def wrapped(*refs):
        pos = 0
        groups = []
        for n in (n_in, n_rin, n_out, n_rout, n_sc, 2):
            groups.append(refs[pos:pos + n])
            pos += n
        ins, rins, outs, routs, scratch, sems = groups
        ids = [pl.program_id(a) for a in range(len(grid))]
        first = functools.reduce(jnp.logical_and, [i == 0 for i in ids])
        last = functools.reduce(jnp.logical_and, [i == g - 1 for i, g in zip(ids, grid)])
        pl.when(first)(lambda: rider.start(rins, routs, *sems))
        body(*ins, *outs, *scratch)
        pl.when(last)(lambda: rider.finish(rins, routs, *sems))

    outs = pl.pallas_call(
        wrapped, name=name, grid=grid, in_specs=list(in_specs) + [_hbm_spec()] * n_rin,
        out_specs=list(out_specs) + [_hbm_spec()] * n_rout, out_shape=list(out_shape) + list(rider.out_shape),
        scratch_shapes=list(scratch_shapes) + [pltpu.SemaphoreType.DMA((rider.n_sems,))] * 2,
        compiler_params=compiler_params)(*args, *rider.inputs)
    return outs[:n_out], outs[n_out:]


def _run_rider(rider, *, name):
    n_rin = len(rider.inputs)

    def body(*refs):
        rins, routs, sems = refs[:n_rin], refs[n_rin:-2], refs[-2:]
        rider.start(rins, routs, *sems)
        rider.finish(rins, routs, *sems)

    return pl.pallas_call(
        body, name=name, in_specs=[_hbm_spec()] * n_rin, out_specs=[_hbm_spec()] * len(rider.out_shape),
        out_shape=rider.out_shape, scratch_shapes=[pltpu.SemaphoreType.DMA((rider.n_sems,))] * 2)(*rider.inputs)


def _gather_rider(shards, modes):
    n_arr = len(shards)
    out_shape = [jax.ShapeDtypeStruct((s.shape[0], N_CHIPS * s.shape[1]) if m == "cols" else (N_CHIPS,) + s.shape, s.dtype)
                 for s, m in zip(shards, modes)]
    per = 4

    def copies(srcs, dsts, send_sems, recv_sems):
        x, y, c, chips = _mesh_place()
        me = 2 * x + y
        sends, waits = [], []
        for i in range(n_arr):
            r, n = shards[i].shape
            rows = _half_rows(c, r, 16)

            def copy(slot, src, dst, to, i=i):
                return pltpu.make_async_remote_copy(src_ref=src, dst_ref=dst, send_sem=send_sems.at[i * per + slot],
                                                    recv_sem=recv_sems.at[i * per + slot], device_id=to, device_id_type=MESH)

            own = _part(dsts[i], modes[i], me, n)
            sends.append(copy(0, srcs[i], own, (x, y, 1 - c)))
            waits.append(copy(0, own, own, (x, y, 1 - c)))
            for j, (px, py) in enumerate(chips):
                sends.append(copy(1 + j, srcs[i].at[rows], _part(dsts[i], modes[i], me, n, rows), (px, py, c)))
                theirs = _part(dsts[i], modes[i], 2 * px + py, n, rows)
                waits.append(copy(1 + j, theirs, theirs, (px, py, c)))
        return sends, waits

    def start(*refs):
        for cp in copies(*refs)[0]:
            cp.start()

    def finish(*refs):
        sends, waits = copies(*refs)
        for cp in waits:
            cp.wait_recv()
        for cp in sends:
            cp.wait_send()

    return _Rider(list(shards), out_shape, per * n_arr, start, finish)


def _gather_forward(dsts, shard_shapes, modes, *, name):
    n_arr = len(dsts)

    def body(*refs):
        outs = refs[n_arr:2 * n_arr]
        send_sems, recv_sems = refs[2 * n_arr:]
        x, y, c, chips = _mesh_place()
        cps = []
        for i in range(n_arr):
            r, n = shard_shapes[i]
            for j, (px, py) in enumerate(chips):
                def view(hc, i=i, px=px, py=py, r=r, n=n):
                    return _part(outs[i], modes[i], 2 * px + py, n, _half_rows(hc, r, 16))

                def copy(ref, i=i, j=j):
                    return pltpu.make_async_remote_copy(src_ref=ref, dst_ref=ref, send_sem=send_sems.at[3 * i + j],
                                                        recv_sem=recv_sems.at[3 * i + j], device_id=(x, y, 1 - c), device_id_type=MESH)

                cps.append((copy(view(c)), copy(view(1 - c))))
        for send, _ in cps:
            send.start()
        for send, theirs in cps:
            theirs.wait_recv()
            send.wait_send()

    return pl.pallas_call(
        body, name=name, in_specs=[_hbm_spec()] * n_arr, out_specs=[_hbm_spec()] * n_arr,
        out_shape=[jax.ShapeDtypeStruct(d.shape, d.dtype) for d in dsts],
        input_output_aliases={i: i for i in range(n_arr)},
        scratch_shapes=[pltpu.SemaphoreType.DMA((3 * n_arr,)), pltpu.SemaphoreType.DMA((3 * n_arr,))])(*dsts)


def _blk_view(a, mode):
    return a[None] if mode == "cols" else a


def _swap_rider(arrs, modes):
    n_arr = len(arrs)
    out_shape = [jax.ShapeDtypeStruct((a.shape[0] // 2, a.shape[1]) if m == "cols" else (a.shape[0], a.shape[1] // 2, a.shape[2]), a.dtype)
                 for a, m in zip(arrs, modes)]

    def copies(srcs, dsts, send_sems, recv_sems):
        x, y, c, _ = _mesh_place()
        cps = []
        for i in range(n_arr):
            if modes[i] == "cols":
                src = srcs[i].at[_half_rows(1 - c, arrs[i].shape[0], 8)]
            else:
                src = srcs[i].at[:, _half_rows(1 - c, arrs[i].shape[1], 8), :]
            cps.append(pltpu.make_async_remote_copy(src_ref=src, dst_ref=dsts[i], send_sem=send_sems.at[i],
                                                    recv_sem=recv_sems.at[i], device_id=(x, y, 1 - c), device_id_type=MESH))
        return cps

    def start(*refs):
        for cp in copies(*refs):
            cp.start()

    def finish(*refs):
        for cp in copies(*refs):
            cp.wait()

    return _Rider(list(arrs), out_shape, n_arr, start, finish)


def _rs_pair_add(arr, landed, place, *, name):
    nb, r, c = arr.shape
    rh = r // 2
    tr = _row_tile(rh, c)
    nt = rh // tr

    def body(p_ref, a_ref, l_ref, o_ref):
        o_ref[...] = (a_ref[...] + l_ref[...]).astype(BF16)

    grid_spec = pltpu.PrefetchScalarGridSpec(
        num_scalar_prefetch=1, grid=(nb, nt),
        in_specs=[pl.BlockSpec((None, tr, c), lambda b, t, p_ref: (b, p_ref[1] * nt + t, 0)),
                  pl.BlockSpec((None, tr, c), lambda b, t, p_ref: (b, t, 0))],
        out_specs=pl.BlockSpec((None, tr, c), lambda b, t, p_ref: (b, t, 0)))
    return pl.pallas_call(
        body, name=name, grid_spec=grid_spec, out_shape=jax.ShapeDtypeStruct((nb, rh, c), BF16),
        compiler_params=_cp("arbitrary", "arbitrary"))(place, arr, landed)


def _exchange_rider(parts, modes):
    n_arr = len(parts)
    out_shape = []
    for a, m in zip(parts, modes):
        shp = (a.shape[0], a.shape[1] // N_CHIPS) if m == "cols" else a.shape[1:]
        out_shape.append(jax.ShapeDtypeStruct((3,) + shp, a.dtype))

    def copies(srcs, dsts, send_sems, recv_sems):
        x, y, c, chips = _mesh_place()
        cps = []
        for i in range(n_arr):
            n = out_shape[i].shape[-1]
            for j, (px, py) in enumerate(chips):
                cps.append(pltpu.make_async_remote_copy(
                    src_ref=_part(srcs[i], modes[i], 2 * px + py, n), dst_ref=dsts[i].at[j],
                    send_sem=send_sems.at[3 * i + j], recv_sem=recv_sems.at[3 * i + j],
                    device_id=(px, py, c), device_id_type=MESH))
        return cps

    def start(*refs):
        for cp in copies(*refs):
            cp.start()

    def finish(*refs):
        for cp in copies(*refs):
            cp.wait()

    return _Rider(list(parts), out_shape, 3 * n_arr, start, finish)


def _rs_chip_sum(part, landed, mode, place, *, name):
    _, rh, n = landed.shape
    tr = _row_tile(rh, n)
    nt = rh // tr

    def body(p_ref, a_ref, l_ref, o_ref):
        o_ref[...] = ((a_ref[...].astype(F32) + l_ref[0].astype(F32)) + l_ref[1].astype(F32)) + l_ref[2].astype(F32)

    if mode == "cols":
        own = pl.BlockSpec((tr, n), lambda t, p_ref: (t, p_ref[0]))
    else:
        own = pl.BlockSpec((None, tr, n), lambda t, p_ref: (p_ref[0], t, 0))
    grid_spec = pltpu.PrefetchScalarGridSpec(
        num_scalar_prefetch=1, grid=(nt,),
        in_specs=[own, pl.BlockSpec((3, tr, n), lambda t, p_ref: (0, t, 0))],
        out_specs=pl.BlockSpec((tr, n), lambda t, p_ref: (p_ref[1] * nt + t, 0)))
    return pl.pallas_call(
        body, name=name, grid_spec=grid_spec, out_shape=jax.ShapeDtypeStruct((2 * rh, n), F32),
        compiler_params=_cp("arbitrary"))(place, part, landed)


def _rs_pair_join(halves, *, name):
    n_arr = len(halves)

    def body(*refs):
        outs = refs[n_arr:2 * n_arr]
        send_sems, recv_sems = refs[2 * n_arr:]
        x, y, c, _ = _mesh_place()
        cps = []
        for i in range(n_arr):
            rows = _half_rows(c, halves[i].shape[0], 8)
            cps.append(pltpu.make_async_remote_copy(src_ref=outs[i].at[rows], dst_ref=outs[i].at[rows], send_sem=send_sems.at[i],
                                                    recv_sem=recv_sems.at[i], device_id=(x, y, 1 - c), device_id_type=MESH))
        for cp in cps:
            cp.start()
        for i, cp in enumerate(cps):
            cp.wait_send()
            theirs = outs[i].at[_half_rows(1 - c, halves[i].shape[0], 8)]
            pltpu.make_async_remote_copy(src_ref=theirs, dst_ref=theirs, send_sem=send_sems.at[i], recv_sem=recv_sems.at[i],
                                         device_id=(x, y, 1 - c), device_id_type=MESH).wait_recv()

    return pl.pallas_call(
        body, name=name, in_specs=[_hbm_spec()] * n_arr, out_specs=[_hbm_spec()] * n_arr,
        out_shape=[jax.ShapeDtypeStruct(h.shape, h.dtype) for h in halves],
        input_output_aliases={i: i for i in range(n_arr)},
        scratch_shapes=[pltpu.SemaphoreType.DMA((n_arr,)), pltpu.SemaphoreType.DMA((n_arr,))])(*halves)


def _allreduce_small(v, *, name):
    r, c = v.shape

    def body(v_ref, o_ref, gath, send_sems, recv_sems):
        x, y, cc, _ = _mesh_place()
        me = 4 * x + 2 * y + cc
        gath[me] = v_ref[...]
        cps = []
        for rel in range(1, 8):
            px = 1 - x if rel & 4 else x
            py = 1 - y if rel & 2 else y
            pc = 1 - cc if rel & 1 else cc

            def copy(slot, px=px, py=py, pc=pc, rel=rel):
                return pltpu.make_async_remote_copy(
                    src_ref=v_ref, dst_ref=gath.at[slot], send_sem=send_sems.at[rel - 1],
                    recv_sem=recv_sems.at[rel - 1], device_id=(px, py, pc), device_id_type=MESH)

            cps.append((copy(me), copy(4 * px + 2 * py + pc)))
        for send, _ in cps:
            send.start()
        for send, theirs in cps:
            theirs.wait_recv()
            send.wait_send()
        tot = gath[0]
        for d in range(1, 8):
            tot = tot + gath[d]
        o_ref[...] = tot

    vm = pl.BlockSpec(memory_space=pltpu.VMEM)
    return pl.pallas_call(
        body, name=name, in_specs=[vm], out_specs=vm, out_shape=jax.ShapeDtypeStruct((r, c), F32),
        scratch_shapes=[pltpu.VMEM((8, r, c), F32), pltpu.SemaphoreType.DMA((7,)), pltpu.SemaphoreType.DMA((7,))])(v)


def _rope_tables(s, reps):
    half = B_ROPE // 2
    inv = ROPE_THETA ** (-jnp.arange(0, B_ROPE, 2, dtype=F32) / B_ROPE)
    ang = jnp.arange(s, dtype=F32)[:, None] * inv[None, :]
    return jnp.tile(jnp.cos(ang), (1, reps)), jnp.tile(jnp.sin(ang), (1, reps))


def _alibi_slopes():
    return 2.0 ** (-8.0 * jnp.arange(1, A_HEADS + 1, dtype=F32) / A_HEADS)


def _ffn_fwd(h, norm, wts, tag, rider=None, on_rode=None):
    (gate, up, act, xn), rode = _ffn_up(h, norm, wts["wgu"], name=f"{tag}_up", rider=rider)
    if on_rode is not None:
        on_rode(rode)
    out = _mm_res_fwd(act, wts["wd"], h, scale=FFN_RES_SCALE, name=f"{tag}_down")
    return out, dict(h_in=h, gate=gate, up=up, act=act, xn=xn), rode


def _ffn_bwd(dh, norm, wts, sv, tag, rider=None, own=None):
    (dgate, dup), rode = _ffn_down_bwd(dh, wts["wd"], sv["gate"], sv["up"], scale=FFN_RES_SCALE,
                                      name=f"{tag}_down_bwd", rider=rider)
    d_wd = _mm_tn(sv["act"], dh, b_scale=FFN_RES_SCALE, name=f"{tag}_dwd")
    pairs = [(dgate, wts["wgu"], 0), (dup, wts["wgu"], 1)]
    if own is None:
        d_wgu = _mm_tn(sv["xn"], [dgate, dup], name=f"{tag}_dwgu")
        dh_in, dnorm = _mm_nt_rmsbwd(pairs, sv["h_in"], norm, dh, name=f"{tag}_dx")
    else:
        wd_ready, wgu_ready, done = own
        d_wgu, brought = _mm_tn(sv["xn"], [dgate, dup], name=f"{tag}_dwgu", rider=wd_ready(d_wd))
        dh_in, dnorm, brought = _mm_nt_rmsbwd(pairs, sv["h_in"], norm, dh, name=f"{tag}_dx", rider=wgu_ready(brought, d_wgu))
        done(brought)
    return dh_in, dnorm, d_wgu, d_wd, rode


def _even_weights(w_in, w_uq, w_ukv):
    half = B_ROPE // 2
    base = w_in.shape[1]
    kr1, kr2 = w_in[:, base - B_ROPE:base - half], w_in[:, base - half:]
    w_in_cat = jnp.concatenate([w_in, -kr2, kr1, jnp.zeros((w_in.shape[0], 64), w_in.dtype)], axis=1)
    u3 = w_uq.reshape(w_uq.shape[0], B_HEADS, B_NOPE + B_ROPE)
    nope = u3[:, :, :B_NOPE].reshape(w_uq.shape[0], -1)
    rot = u3[:, :, B_NOPE:].reshape(w_uq.shape[0], -1)
    swapped = jnp.concatenate([-u3[:, :, B_NOPE + half:], u3[:, :, B_NOPE:B_NOPE + half]], axis=-1).reshape(w_uq.shape[0], -1)
    return w_in_cat, jnp.concatenate([nope, rot, swapped], axis=1), w_ukv


def _even_fwd(h, w, i, rider=None):
    s = h.shape[0]
    qa, ka, va, vat, c_q, c_kv, kr_blk, xn = _ev_in_fwd(h, w["mix_norm"][i:i + 1], w["ev_in_cat"], name="ev_in")
    cos32, sin32 = _rope_tables(s, 2)
    kro = _rope_fwd(kr_blk[:, :B_ROPE], kr_blk[:, B_ROPE:2 * B_ROPE], cos32, sin32, name="ev_k_rope")
    ss = jnp.stack([_alibi_slopes(), w["ev_sinks"].reshape(-1)])
    oa, lse_a = _swa_fwd_t(qa, ka, vat, ss, scale=A_HEAD_DIM ** -0.5, window=WINDOW, name="swa_fwd")
    cos256, sin256 = _rope_tables(s, 2 * B_HEADS)
    qb, xn_q = _ev_q_fwd(c_q, w["ev_cq_norm"], w["ev_q_cat"], cos256, sin256, name="ev_q_up")
    kb, vb, vbt, xn_kv = _ev_kv_fwd(c_kv, w["ev_ckv_norm"], w["ev_ukv"], kro, name="ev_kv_up")
    (ob, lse_b), rode = _causal_fwd_t(qb, kb, vbt, scale=(B_NOPE + B_ROPE) ** -0.5, name="mla_fwd", tq=512, rider=rider)
    attn = jnp.concatenate([oa.transpose(2, 0, 1).reshape(s, -1), ob.transpose(2, 0, 1).reshape(s, -1)], axis=-1)
    out = _mm_res_fwd(attn, w["ev_out"], h, scale=1.0, name="ev_out")
    sv = dict(h_in=h, xn=xn, c_q=c_q, c_kv=c_kv, xn_q=xn_q, xn_kv=xn_kv, qa=qa, ka=ka, va=va, oa=oa, lse_a=lse_a,
              ss=ss, qb=qb, kb=kb, vb=vb, ob=ob, lse_b=lse_b, attn=attn, cos32=cos32, sin32=sin32,
              cos256=cos256, sin256=sin256)
    return out, sv, rode


def _even_bwd(dh, w, sv, i, rider=None):
    s = dh.shape[0]
    half = B_ROPE // 2
    g = {}
    dattn = _mm_nt(dh, w["ev_out"], name="ev_out_dx")
    g["ev_w_out"] = _mm_tn(sv["attn"], dh, name="ev_out_dw")
    doa = dattn[:, :512].reshape(s, A_HEADS, A_HEAD_DIM).transpose(1, 2, 0).astype(BF16)
    dob = dattn[:, 512:].reshape(s, B_HEADS, B_V).transpose(1, 2, 0).astype(BF16)
    first, then = rider if isinstance(rider, tuple) else (None, None)
    (dqa, dka, dva, dsink), brought = _swa_bwd_t(sv["qa"], sv["ka"], sv["va"], sv["oa"], doa, sv["lse_a"], sv["ss"],
                                                 scale=A_HEAD_DIM ** -0.5, window=WINDOW, name="swa_bwd", rider=first)
    if then is not None:
        rider = then(brought)
    g["ev_sinks"] = dsink[:, :, 0, 0].reshape(1, A_HEADS)
    (dqb, dkb, dvb), rode = _causal_bwd_t(sv["qb"], sv["kb"], sv["vb"], sv["ob"], dob, sv["lse_b"],
                                          scale=(B_NOPE + B_ROPE) ** -0.5, name="mla_bwd", tq=512, rider=rider)
    dyq = _ev_q_merge(dqb, sv["cos256"], sv["sin256"], name="ev_q_merge")
    dwq = _mm_tn(sv["xn_q"], dyq, name="ev_q_up_dw")
    dcq, g["ev_cq_norm"] = _mm_nt_rmsbwd([(dyq, w["ev_q_cat"])], sv["c_q"], w["ev_cq_norm"], None, name="ev_q_up_dx")
    kq = sv["c_q"].shape[1]
    d_nope = dwq[:, :512].reshape(kq, B_HEADS, B_NOPE)
    d_rot = dwq[:, 512:768].reshape(kq, B_HEADS, B_ROPE)
    d_swp = dwq[:, 768:].reshape(kq, B_HEADS, B_ROPE)
    g["ev_w_uq"] = jnp.concatenate([d_nope, d_rot[:, :, :half] + d_swp[:, :, half:], d_rot[:, :, half:] - d_swp[:, :, :half]],
                                   axis=-1).reshape(kq, -1)
    dykv, dkr = _ev_kv_merge(dkb, dvb, sv["cos32"], sv["sin32"], name="ev_kv_merge")
    g["ev_w_ukv"] = _mm_tn(sv["xn_kv"], dykv, name="ev_kv_up_dw")
    dckv, g["ev_ckv_norm"] = _mm_nt_rmsbwd([(dykv, w["ev_ukv"])], sv["c_kv"], w["ev_ckv_norm"], None, name="ev_kv_up_dx")
    dycat = _ev_in_merge(dqa, dka, dva, dcq, dckv, dkr, name="ev_in_merge")
    dwin = _mm_tn(sv["xn"], dycat, name="ev_in_dw")
    base = 1184
    g["ev_w_in"] = jnp.concatenate([dwin[:, :base - B_ROPE],
                                    dwin[:, base - B_ROPE:base - half] + dwin[:, base + half:base + B_ROPE],
                                    dwin[:, base - half:base] - dwin[:, base:base + half]], axis=-1)
    dh_in, dnorm = _mm_nt_rmsbwd([(dycat, w["ev_in_cat"])], sv["h_in"], w["mix_norm"][i:i + 1], dh, name="ev_in_dx")
    return dh_in, dnorm, g, rode


def _odd_fwd(h, w, i, rider=None):
    s = h.shape[0]
    wd = C_HEADS * C_HEAD_DIM
    q, k, v, vt, y_f, xn = _fox_in_fwd(h, w["mix_norm"][i:i + 1], w["od_in_pad"], nheads=C_HEADS, dh=C_HEAD_DIM,
                                       q_ones=(0, 2, 3, 4), k_ones=(1,), name="od_in")
    scale = C_HEAD_DIM ** -0.5
    ft = y_f[:, :C_HEADS].T
    bf = w["od_b_f"].reshape(C_HEADS, 1)
    cb3 = _fox_gate_fwd(ft, bf, out_scale=-1.0 / scale, name="fox_gate_fwd")
    k = k + jnp.pad(cb3.transpose(1, 2, 0), ((0, 0), (0, 0), (C_HEAD_DIM + 2, LANES - C_HEAD_DIM - 5)))
    (o, lse), rode = _causal_fwd_t(q, k, vt, scale=scale, name="fox_fwd", tq=512, hb=4, rider=rider)
    attn = o.transpose(2, 0, 1).reshape(s, -1)
    out = _mm_res_fwd(attn, w["od_out"], h, scale=1.0, name="od_out")
    return out, dict(h_in=h, xn=xn, q=q, k=k, v=v, o=o, lse=lse, ft=ft, bf=bf, attn=attn), rode


def _odd_bwd(dh, w, sv, i, rider=None):
    s = dh.shape[0]
    g = {}
    dattn = _mm_nt(dh, w["od_out"], name="od_out_dx")
    g["od_w_out"] = _mm_tn(sv["attn"], dh, name="od_out_dw")
    do = dattn.reshape(s, C_HEADS, C_HEAD_DIM).transpose(1, 2, 0).astype(BF16)
    scale = C_HEAD_DIM ** -0.5
    (dq, dk, dv), rode = _causal_bwd_t(sv["q"], sv["k"], sv["v"], sv["o"], do, sv["lse"], scale=scale, name="fox_bwd",
                                       tq=512, hb=4, rider=rider)
    dft, dbf = _fox_gate_bwd(dq[:, :, C_HEAD_DIM + 1], dk[:, :, C_HEAD_DIM], sv["ft"], sv["bf"],
                             inv_scale=1.0 / scale, name="fox_gate_bwd")
    g["od_b_f"] = dbf.reshape(1, C_HEADS)
    wd = C_HEADS * C_HEAD_DIM
    dqkv = _merge_heads(dq, dk, dv, dh=C_HEAD_DIM, name="fox_merge")
    df = jnp.pad(dft.T, ((0, 0), (0, LANES - C_HEADS)))
    g["od_w_in"] = jnp.concatenate([_mm_tn(sv["xn"], dqkv, name="od_in_dw"),
                                    _mm_tn(sv["xn"], df, name="od_in_dwf")[:, :C_HEADS]], axis=-1)
    dh_in, dnorm = _mm_nt_rmsbwd([(dqkv, w["od_in_pad"], 0), (df, w["od_in_pad"], 3 * wd // LANES)],
                                 sv["h_in"], w["mix_norm"][i:i + 1], dh, name="od_in_dx")
    return dh_in, dnorm, g, rode


def _kernel_weights(full, replicated):
    w = dict(replicated)
    _install_weights(w, {(n, i): a for n, per_layer in full.items() for i, a in enumerate(per_layer)})
    return w


def _install_weights(w, got):
    raw = w.setdefault("raw", {})
    raw.update(got)
    for (n, i), a in got.items():
        if n in ("ffa_w_gate_up", "ffa_w_down", "ffb_w_gate_up", "ffb_w_down"):
            w.setdefault(n[:3], {}).setdefault(i, {})["wgu" if n.endswith("gate_up") else "wd"] = a
        elif n in ("ple_w_gate", "ple_w_proj"):
            w.setdefault("ple_gate" if n.endswith("gate") else "ple_proj", {})[i] = a
    if "ev_in_cat" not in w and all((n, 0) in raw for n in ("ev_w_in", "ev_w_uq", "ev_w_ukv", "ev_w_out")):
        w["ev_in_cat"], w["ev_q_cat"], w["ev_ukv"] = _even_weights(raw["ev_w_in", 0], raw["ev_w_uq", 0], raw["ev_w_ukv", 0])
        w["ev_out"] = raw["ev_w_out", 0]
    if "od_in_pad" not in w and all((n, 0) in raw for n in ("od_w_in", "od_w_out")):
        od_in = raw["od_w_in", 0]
        w["od_in_pad"] = jnp.pad(od_in, ((0, 0), (0, (-od_in.shape[1]) % LANES)))
        w["od_out"] = raw["od_w_out", 0]


def _keys(names, layer):
    return tuple((n, layer) for n in names)


_FFA, _FFB, _PLE = ("ffa_w_gate_up", "ffa_w_down"), ("ffb_w_gate_up", "ffb_w_down"), ("ple_w_gate", "ple_w_proj")
_EV, _OD = ("ev_w_in", "ev_w_uq", "ev_w_ukv", "ev_w_out"), ("od_w_in", "od_w_out")
_GATHER_FIRST = _keys(_FFA[:1], 0)
_GATHER_RIDES = {("ffa", 0): _keys(_FFA[1:] + _EV, 0), ("mix", 0): _keys(_FFB + _PLE, 0) + _keys(_FFA, 1),
                 ("ffb", 0): _keys(_OD, 0), ("mix", 1): _keys(_FFB + _PLE, 1)}
_REDUCE_RIDES = {("mix", 1): _keys(_FFB + _PLE, 1), ("mix", 0): _keys(_FFA, 1) + _keys(_OD, 0) + _keys(_FFB + _PLE, 0),
                 ("ffa", 0): _keys(_EV, 0)}
_REDUCE_OWN = ("ffa", 0)


def _local_step(x, p, tgt, w, ex=None):
    depth = p.shape[0]

    def gather_behind(host, fn, *args):
        keys = None if ex is None else _GATHER_RIDES.get(host)
        if keys is None:
            return fn(*args, None)[:-1]
        done = []

        def install(rode):
            if not done:
                _install_weights(w, ex.gather_finish(keys, rode, name=f"weight_forward_{host[0]}{host[1]}"))
                done.append(True)

        res = fn(*args, ex.gather_rider(keys), install) if fn is _ffn_fwd else fn(*args, ex.gather_rider(keys))
        install(res[-1])
        return res[:-1]

    h = x
    saved = []
    for i in range(depth):
        sv = {}
        h, sv["ffa"] = gather_behind(("ffa", i), _ffn_fwd, h, w["ffa_norm"][i:i + 1], w["ffa"][i], f"ffa{i}")
        h, sv["mix"] = gather_behind(("mix", i), _even_fwd if i % 2 == 0 else _odd_fwd, h, w, i)
        h, sv["ffb"] = gather_behind(("ffb", i), _ffn_fwd, h, w["ffb_norm"][i:i + 1], w["ffb"][i], f"ffb{i}")
        h_in = h
        h, xn, gate, pp = _ple_fwd(h, w["ple_norm"][i:i + 1], w["ple_gate"][i], p[i], w["ple_proj"][i], name=f"ple{i}")
        sv["ple"] = dict(h_in=h_in, xn=xn, gate=gate, pp=pp)
        saved.append(sv)
    loss_vec, dh, d_final = _final_loss(h, w["final_norm"].reshape(1, -1), tgt, name="final_loss")

    per_layer = [dict() for _ in range(depth)]
    mats = {}
    grads = {}

    def reduce_behind(host, fn, *args):
        keys = None if ex is None else _REDUCE_RIDES.get(host)
        if keys is None:
            return fn(*args, None)[:-1]
        states = []
        if fn is _even_bwd:
            swap, ctx = ex.swap_rider(keys, mats)

            def then(brought):
                states.append(ex.after_swap(ctx, brought))
                return states[0][0]

            res = fn(*args, (swap, then))
        else:
            states.append(ex.reduce_begin(keys, mats, tag=f"{host[0]}{host[1]}"))
            if fn is _ffn_bwd and host == _REDUCE_OWN:
                own = []

                def wd_ready(d_wd):
                    own.append(ex.reduce_begin(_keys(_FFA[1:], 0), {("ffa_w_down", 0): d_wd}, tag="own_wd"))
                    return own[0][0]

                def wgu_ready(brought, d_wgu):
                    ex.reduce_finish(own[0], brought)
                    own.append(ex.reduce_begin(_keys(_FFA[:1], 0), {("ffa_w_gate_up", 0): d_wgu}, tag="own_wgu"))
                    return own[1][0]

                res = fn(*args, states[0][0], (wd_ready, wgu_ready, lambda brought: ex.reduce_finish(own[1], brought)))
            else:
                res = fn(*args, states[0][0])
        ex.reduce_finish(states[0], res[-1])
        return res[:-1]

    for i in reversed(range(depth)):
        sv, gl = saved[i], per_layer[i]
        dz, dpp = _ple_bwd_elem(dh, sv["ple"]["gate"], sv["ple"]["pp"], name=f"ple{i}_bwd")
        mats["ple_w_gate", i] = _mm_tn(sv["ple"]["xn"], dz, name=f"ple{i}_dwg")
        mats["ple_w_proj", i] = _mm_tn(p[i], dpp, name=f"ple{i}_dwp")
        dh, gl["ple_norm"] = _mm_nt_rmsbwd([(dz, w["ple_gate"][i])], sv["ple"]["h_in"], w["ple_norm"][i:i + 1], dh,
                                           name=f"ple{i}_dx")
        dh, gl["ffb_norm"], mats["ffb_w_gate_up", i], mats["ffb_w_down", i] = reduce_behind(
            ("ffb", i), _ffn_bwd, dh, w["ffb_norm"][i:i + 1], w["ffb"][i], sv["ffb"], f"ffb{i}")
        dh, gl["mix_norm"], gm = reduce_behind(("mix", i), _even_bwd if i % 2 == 0 else _odd_bwd, dh, w, sv["mix"], i)
        for n, g in gm.items():
            if n in REPLICATED:
                grads[n] = g
            else:
                mats[n, 0] = g
        dh, gl["ffa_norm"], mats["ffa_w_gate_up", i], mats["ffa_w_down", i] = reduce_behind(
            ("ffa", i), _ffn_bwd, dh, w["ffa_norm"][i:i + 1], w["ffa"][i], sv["ffa"], f"ffa{i}")
    grads["final_norm"] = d_final.reshape(-1)
    for n in ("ffa_norm", "mix_norm", "ffb_norm", "ple_norm"):
        grads[n] = jnp.concatenate([per_layer[i][n] for i in range(depth)], axis=0)
    if ex is None:
        for n, _ in SHARDED:
            grads[n] = [mats[n, i] for i in range(depth) if (n, i) in mats]
    return loss_vec[0, 0], dh, grads


def _cut_mode(local_shape, axis, ncols):
    return "cols" if axis == 2 and ncols % LANES == 0 else "blk"


class _Exchange:
    def __init__(self, wts):
        self.place = jnp.stack([2 * lax.axis_index("x") + lax.axis_index("y"), lax.axis_index("c")]).astype(jnp.int32)
        self.info = {}
        for n, axis in SHARDED:
            wb = wts[n].astype(BF16)
            mode = _cut_mode(wb.shape, axis, wb.shape[2])
            for i in range(wb.shape[0]):
                self.info[n, i] = dict(shard=wb[i], mode=mode, axis=axis)
        self.halves = {}

    def _modes(self, keys):
        return [self.info[k]["mode"] for k in keys]

    def gather_rider(self, keys):
        return _gather_rider([self.info[k]["shard"] for k in keys], self._modes(keys))

    def gather_finish(self, keys, landed, *, name):
        outs = _gather_forward(landed, [self.info[k]["shard"].shape for k in keys], self._modes(keys), name=name)
        got = {}
        for k, dst in zip(keys, outs):
            if self.info[k]["mode"] == "blk":
                dst = dst.reshape(-1, dst.shape[2]) if self.info[k]["axis"] == 1 else jnp.moveaxis(dst, 0, 1).reshape(dst.shape[1], -1)
            got[k] = dst
        return got

    def gather(self, keys, *, name):
        return self.gather_finish(keys, _run_rider(self.gather_rider(keys), name=name), name=name + "_forward")

    def swap_rider(self, keys, mats):
        modes = self._modes(keys)
        arrs = []
        for k in keys:
            g2, (rr, cc) = mats[k], self.info[k]["shard"].shape
            if self.info[k]["mode"] == "blk":
                g2 = g2.reshape(N_CHIPS, rr, cc) if self.info[k]["axis"] == 1 else g2.reshape(rr, N_CHIPS, cc).transpose(1, 0, 2)
            arrs.append(g2)
        return _swap_rider(arrs, modes), (keys, modes, arrs)

    def after_swap(self, ctx, landed):
        keys, modes, arrs = ctx
        parts = []
        for (n, i), m, a, l in zip(keys, modes, arrs, landed):
            pt = _rs_pair_add(_blk_view(a, m), _blk_view(l, m), self.place, name=f"rs_pair_add_{n}{i}")
            parts.append(pt[0] if m == "cols" else pt)
        return _exchange_rider(parts, modes), keys, parts

    def reduce_begin(self, keys, mats, *, tag):
        rider, ctx = self.swap_rider(keys, mats)
        return self.after_swap(ctx, _run_rider(rider, name=f"rs_pair_swap_{tag}"))

    def reduce_finish(self, state, landed):
        _, keys, parts = state
        for (n, i), m, pt, l in zip(keys, self._modes(keys), parts, landed):
            self.halves[n, i] = _rs_chip_sum(pt, l, m, self.place, name=f"rs_chip_sum_{n}{i}")

    def reduce(self, keys, mats, *, tag):
        state = self.reduce_begin(keys, mats, tag=tag)
        self.reduce_finish(state, _run_rider(state[0], name=f"rs_chip_exchange_{tag}"))

    def join(self, wts):
        keys = list(self.info)
        joined = dict(zip(keys, _rs_pair_join([self.halves[k] for k in keys], name="rs_pair_join")))
        return {n: jnp.stack([joined[n, i] for i in range(wts[n].shape[0])]).reshape(wts[n].shape) for n, _ in SHARDED}


def _small_rows(vals):
    rows = []
    for n in REPLICATED:
        v = vals[n].reshape(-1)
        rows.append(jnp.pad(v, (0, (-v.shape[0]) % FLAT_COLS)).reshape(-1, FLAT_COLS))
    out = jnp.concatenate(rows, axis=0)
    return jnp.pad(out, ((0, (-out.shape[0]) % 8), (0, 0)))


def kernel(x, p, ffa_norm, ffa_w_gate_up, ffa_w_down, mix_norm, ffb_norm, ffb_w_gate_up, ffb_w_down, ple_norm, ple_w_gate, ple_w_proj, ev_w_in, ev_sinks, ev_cq_norm, ev_w_uq, ev_ckv_norm, ev_w_ukv, ev_w_out, od_w_in, od_b_f, od_w_out, final_norm, loss_target, m_ffa_norm, m_ffa_w_gate_up, m_ffa_w_down, m_mix_norm, m_ffb_norm, m_ffb_w_gate_up, m_ffb_w_down, m_ple_norm, m_ple_w_gate, m_ple_w_proj, m_ev_w_in, m_ev_sinks, m_ev_cq_norm, m_ev_w_uq, m_ev_ckv_norm, m_ev_w_ukv, m_ev_w_out, m_od_w_in, m_od_b_f, m_od_w_out, m_final_norm, v_ffa_norm, v_ffa_w_gate_up, v_ffa_w_down, v_mix_norm, v_ffb_norm, v_ffb_w_gate_up, v_ffb_w_down, v_ple_norm, v_ple_w_gate, v_ple_w_proj, v_ev_w_in, v_ev_sinks, v_ev_cq_norm, v_ev_w_uq, v_ev_ckv_norm, v_ev_w_ukv, v_ev_w_out, v_od_w_in, v_od_b_f, v_od_w_out, v_final_norm):
    env = dict(locals())
    wts = {n: env[n] for n in WEIGHT_ORDER}
    mom1 = {n: env["m_" + n] for n in WEIGHT_ORDER}
    mom2 = {n: env["v_" + n] for n in WEIGHT_ORDER}
    ex = _Exchange(wts)

    w = {n: wts[n] for n in REPLICATED}
    _install_weights(w, ex.gather(_GATHER_FIRST, name="weight_gather_first"))

    loss_part, grad_x, grads = _local_step(x[0], p[:, 0], loss_target[0], w, ex)
    loss = lax.psum(loss_part, ("x", "y", "c"))
    gout = ex.join(wts)
    small = _allreduce_small(_small_rows(grads), name="small_allreduce")
    r0 = 0
    for n in REPLICATED:
        size = int(np.prod(wts[n].shape))
        nr = -(-size // FLAT_COLS)
        gout[n] = small[r0:r0 + nr].reshape(-1)[:size].reshape(wts[n].shape)
        r0 += nr

    delta, new_m, new_v = {}, {}, {}
    for n in WEIGHT_ORDER:
        delta[n], new_m[n], new_v[n] = _adamw(wts[n], gout[n], mom1[n], mom2[n], name="adamw_" + n)
    return (loss, grad_x[None], *[gout[n] for n in WEIGHT_ORDER], *[delta[n] for n in WEIGHT_ORDER],
            *[new_m[n] for n in WEIGHT_ORDER], *[new_v[n] for n in WEIGHT_ORDER])
```

```python
import functools
import math

import numpy as np
import jax
import jax.numpy as jnp
from jax import lax
from jax.experimental import pallas as pl
from jax.experimental.pallas import tpu as pltpu

F32 = jnp.float32
BF16 = jnp.bfloat16
NT = (((1,), (1,)), ((), ()))
TN = (((0,), (0,)), ((), ()))
MESH = pl.DeviceIdType.MESH

RMS_EPS = 1e-6
FFN_RES_SCALE = 0.5
A_HEADS, A_KV_HEADS, A_HEAD_DIM, WINDOW = 8, 2, 64, 128
B_HEADS, B_Q_LORA, B_KV_LORA, B_NOPE, B_ROPE, B_V = 8, 256, 128, 64, 32, 64
ROPE_THETA = 10000.0
C_HEADS, C_HEAD_DIM = 16, 64
ADAM_LR, ADAM_B1, ADAM_B2, ADAM_EPS, ADAM_WD, ADAM_STEP = 0.001, 0.9, 0.999, 1e-08, 0.01, 10

N_CHIPS = 4
LANES = 128
FLAT_COLS = 1024
MASK_VALUE = -1e30
VMEM_LIMIT = 48 * 2**20

SHARDED = (
    ("ffa_w_gate_up", 2), ("ffa_w_down", 1), ("ffb_w_gate_up", 2), ("ffb_w_down", 1),
    ("ple_w_gate", 1), ("ple_w_proj", 2), ("ev_w_in", 2), ("ev_w_uq", 2), ("ev_w_ukv", 2),
    ("ev_w_out", 1), ("od_w_in", 2), ("od_w_out", 1))
REPLICATED = ("ffa_norm", "mix_norm", "ffb_norm", "ple_norm", "final_norm",
              "ev_sinks", "ev_cq_norm", "ev_ckv_norm", "od_b_f")
WEIGHT_ORDER = ("ffa_norm", "ffa_w_gate_up", "ffa_w_down", "mix_norm", "ffb_norm", "ffb_w_gate_up",
                "ffb_w_down", "ple_norm", "ple_w_gate", "ple_w_proj", "ev_w_in", "ev_sinks",
                "ev_cq_norm", "ev_w_uq", "ev_ckv_norm", "ev_w_ukv", "ev_w_out", "od_w_in", "od_b_f",
                "od_w_out", "final_norm")


def _cp(*sem):
    return pltpu.CompilerParams(dimension_semantics=sem, vmem_limit_bytes=VMEM_LIMIT)


def _sigmoid(z):
    return 1.0 / (1.0 + jnp.exp(-z))


def _rms_stats(xv):
    r = lax.rsqrt(jnp.mean(xv * xv, axis=-1, keepdims=True) + RMS_EPS)
    return r, xv * r


def _rms_bwd(dxn, xv, g):
    r, xhat = _rms_stats(xv)
    u = dxn * g
    dx = r * (u - xhat * jnp.mean(u * xhat, axis=-1, keepdims=True))
    return dx, dxn * xhat


def _col_tile(k_rows, n, budget_bytes=6 * 2**20):
    if k_rows * n * 4 <= budget_bytes or n % LANES:
        return n
    units = n // LANES
    best = LANES
    for d in range(1, units + 1):
        if units % d == 0 and k_rows * d * LANES * 4 <= budget_bytes:
            best = d * LANES
    return best


def _row_tile(rows, cols, target_elems=2**18):
    if rows * cols <= target_elems or rows % 8:
        return rows
    best = 8
    for d in range(8, rows + 1, 8):
        if rows % d == 0 and d * cols <= target_elems:
            best = d
    return best


def _fox_in_fwd(x, g, w, *, nheads, dh, q_ones, k_ones, name, tm=512):
    s, k = x.shape
    n = w.shape[1]
    wd = nheads * dh
    spare = LANES - dh

    def body(x_ref, g_ref, w_ref, q_ref, k_ref, v_ref, vt_ref, f_ref, xn_ref):
        _, xhat = _rms_stats(x_ref[...])
        xn = (xhat * g_ref[...]).astype(BF16)
        xn_ref[...] = xn
        y = jnp.dot(xn, w_ref[...], preferred_element_type=F32)
        f_ref[...] = y[:, 3 * wd:]
        lane = lax.broadcasted_iota(jnp.int32, (tm, spare), 1)

        def fill(cols):
            return functools.reduce(jnp.logical_or, [lane == c for c in cols]).astype(F32)

        q_fill, k_fill = fill(q_ones), fill(k_ones)
        for h in range(nheads):
            q_ref[h] = jnp.concatenate([y[:, h * dh:(h + 1) * dh], q_fill], axis=-1).astype(BF16)
            k_ref[h] = jnp.concatenate([y[:, wd + h * dh:wd + (h + 1) * dh], k_fill], axis=-1).astype(BF16)
            vh = y[:, 2 * wd + h * dh:2 * wd + (h + 1) * dh]
            v_ref[h] = vh.astype(BF16)
            vt_ref[h] = vh.T.astype(BF16)

    wide = pl.BlockSpec((nheads, tm, LANES), lambda i: (0, i, 0))
    return pl.pallas_call(
        body, name=name, grid=(s // tm,),
        in_specs=[pl.BlockSpec((tm, k), lambda i: (i, 0)), pl.BlockSpec((1, k), lambda i: (0, 0)),
                  pl.BlockSpec((k, n), lambda i: (0, 0))],
        out_specs=[wide, wide, pl.BlockSpec((nheads, tm, dh), lambda i: (0, i, 0)),
                   pl.BlockSpec((nheads, dh, tm), lambda i: (0, 0, i)), pl.BlockSpec((tm, LANES), lambda i: (i, 0)),
                   pl.BlockSpec((tm, k), lambda i: (i, 0))],
        out_shape=[jax.ShapeDtypeStruct((nheads, s, LANES), BF16)] * 2
        + [jax.ShapeDtypeStruct((nheads, s, dh), BF16), jax.ShapeDtypeStruct((nheads, dh, s), BF16),
           jax.ShapeDtypeStruct((s, LANES), F32), jax.ShapeDtypeStruct((s, k), BF16)],
        compiler_params=_cp("arbitrary"))(x, g, w)


def _merge_heads(dq, dk, dvt, *, dh, name, tm=512):
    nheads, s, _ = dq.shape

    def body(dq_ref, dk_ref, dvt_ref, o_ref):
        pieces = [dq_ref[h][:, :dh] for h in range(nheads)] + [dk_ref[h][:, :dh] for h in range(nheads)]
        pieces += [dvt_ref[h].T for h in range(nheads)]
        o_ref[...] = jnp.concatenate(pieces, axis=-1)

    wide = pl.BlockSpec((nheads, tm, LANES), lambda i: (0, i, 0))
    return pl.pallas_call(
        body, name=name, grid=(s // tm,),
        in_specs=[wide, wide, pl.BlockSpec((nheads, dh, tm), lambda i: (0, 0, i))],
        out_specs=pl.BlockSpec((tm, 3 * nheads * dh), lambda i: (i, 0)),
        out_shape=jax.ShapeDtypeStruct((s, 3 * nheads * dh), F32),
        compiler_params=_cp("arbitrary"))(dq, dk, dvt)


def _row_call(body, n_rows, ins, outs, *, name, tm=512):
    def spec(a, axis):
        shape = a.shape
        if axis is None:
            return pl.BlockSpec(shape, lambda i: (0,) * len(shape))
        blk = tuple(tm if d == axis else n for d, n in enumerate(shape))
        return pl.BlockSpec(blk, lambda i: tuple(i if d == axis else 0 for d in range(len(shape))))

    return pl.pallas_call(
        body, name=name, grid=(n_rows // tm,), in_specs=[spec(a, ax) for a, ax in ins],
        out_specs=[spec(a, ax) for a, ax in outs], out_shape=[a for a, _ in outs],
        compiler_params=_cp("arbitrary"))(*[a for a, _ in ins])


def _sds(shape, dtype):
    return jax.ShapeDtypeStruct(shape, dtype)


def _ev_in_fwd(x, g, w, *, name):
    s, k = x.shape
    d = A_HEAD_DIM

    def body(x_ref, g_ref, w_ref, q_ref, k_ref, v_ref, vt_ref, cq_ref, ckv_ref, kr_ref, xn_ref):
        _, xhat = _rms_stats(x_ref[...])
        xn = (xhat * g_ref[...]).astype(BF16)
        xn_ref[...] = xn
        y = jnp.dot(xn, w_ref[...], preferred_element_type=F32)
        for h in range(A_HEADS):
            q_ref[h] = y[:, h * d:(h + 1) * d].astype(BF16)
        for h in range(A_KV_HEADS):
            k_ref[h] = y[:, 512 + h * d:512 + (h + 1) * d].astype(BF16)
            vh = y[:, 640 + h * d:640 + (h + 1) * d]
            v_ref[h] = vh.astype(BF16)
            vt_ref[h] = vh.T.astype(BF16)
        cq_ref[...] = y[:, 768:1024]
        ckv_ref[...] = y[:, 1024:1152]
        kr_ref[...] = y[:, 1152:1280]

    return _row_call(
        body, s, [(x, 0), (g, None), (w, None)],
        [(_sds((A_HEADS, s, d), BF16), 1), (_sds((A_KV_HEADS, s, d), BF16), 1), (_sds((A_KV_HEADS, s, d), BF16), 1),
         (_sds((A_KV_HEADS, d, s), BF16), 2), (_sds((s, B_Q_LORA), F32), 0), (_sds((s, B_KV_LORA), F32), 0),
         (_sds((s, LANES), F32), 0), (_sds((s, k), BF16), 0)], name=name)


def _ev_q_fwd(x, g, w, cos, sin, *, name):
    s, k = x.shape
    rot = B_HEADS * B_ROPE

    def body(x_ref, g_ref, w_ref, c_ref, s_ref, q_ref, xn_ref):
        _, xhat = _rms_stats(x_ref[...])
        xn = (xhat * g_ref[...]).astype(BF16)
        xn_ref[...] = xn
        y = jnp.dot(xn, w_ref[...], preferred_element_type=F32)
        ro = y[:, 512:512 + rot] * c_ref[...] + y[:, 512 + rot:] * s_ref[...]
        zero = jnp.zeros((y.shape[0], LANES - B_NOPE - B_ROPE), F32)
        for h in range(B_HEADS):
            q_ref[h] = jnp.concatenate([y[:, h * B_NOPE:(h + 1) * B_NOPE], ro[:, h * B_ROPE:(h + 1) * B_ROPE], zero],
                                       axis=-1).astype(BF16)

    return _row_call(body, s, [(x, 0), (g, None), (w, None), (cos, 0), (sin, 0)],
                     [(_sds((B_HEADS, s, LANES), BF16), 1), (_sds((s, k), BF16), 0)], name=name)


def _ev_kv_fwd(x, g, w, kro, *, name):
    s, k = x.shape
    per = B_NOPE + B_V

    def body(x_ref, g_ref, w_ref, kr_ref, k_ref, v_ref, vt_ref, xn_ref):
        _, xhat = _rms_stats(x_ref[...])
        xn = (xhat * g_ref[...]).astype(BF16)
        xn_ref[...] = xn
        y = jnp.dot(xn, w_ref[...], preferred_element_type=F32)
        kr = kr_ref[...]
        zero = jnp.zeros((y.shape[0], LANES - B_NOPE - B_ROPE), F32)
        for h in range(B_HEADS):
            k_ref[h] = jnp.concatenate([y[:, h * per:h * per + B_NOPE], kr, zero], axis=-1).astype(BF16)
            vh = y[:, h * per + B_NOPE:(h + 1) * per]
            v_ref[h] = vh.astype(BF16)
            vt_ref[h] = vh.T.astype(BF16)

    return _row_call(body, s, [(x, 0), (g, None), (w, None), (kro, 0)],
                     [(_sds((B_HEADS, s, LANES), BF16), 1), (_sds((B_HEADS, s, B_V), BF16), 1),
                      (_sds((B_HEADS, B_V, s), BF16), 2), (_sds((s, k), BF16), 0)], name=name)


def _ev_q_merge(dq, cos, sin, *, name):
    nh, s, _ = dq.shape

    def body(dq_ref, c_ref, s_ref, o_ref):
        dro = jnp.concatenate([dq_ref[h][:, B_NOPE:B_NOPE + B_ROPE] for h in range(nh)], axis=-1)
        o_ref[...] = jnp.concatenate([dq_ref[h][:, :B_NOPE] for h in range(nh)] + [dro * c_ref[...], dro * s_ref[...]], axis=-1)

    return _row_call(body, s, [(dq, 1), (cos, 0), (sin, 0)], [(_sds((s, 2 * nh * B_NOPE), F32), 0)], name=name)[0]


def _ev_kv_merge(dk, dvt, cos, sin, *, name):
    nh, s, _ = dk.shape

    def body(dk_ref, dvt_ref, c_ref, s_ref, o_ref, kr_ref):
        pieces = []
        tot = None
        for h in range(nh):
            pieces += [dk_ref[h][:, :B_NOPE], dvt_ref[h].T]
            rot = dk_ref[h][:, B_NOPE:B_NOPE + B_ROPE]
            tot = rot if tot is None else tot + rot
        o_ref[...] = jnp.concatenate(pieces, axis=-1)
        kr_ref[...] = jnp.concatenate([tot * c_ref[...], tot * s_ref[...], jnp.zeros((tot.shape[0], LANES - 2 * B_ROPE), F32)],
                                      axis=-1)

    return _row_call(body, s, [(dk, 1), (dvt, 2), (cos, 0), (sin, 0)],
                     [(_sds((s, nh * (B_NOPE + B_V)), F32), 0), (_sds((s, LANES), F32), 0)], name=name)


def _ev_in_merge(dq, dk, dvt, dcq, dckv, dkr, *, name):
    s = dcq.shape[0]

    def body(dq_ref, dk_ref, dvt_ref, cq_ref, ckv_ref, kr_ref, o_ref):
        pieces = [dq_ref[h] for h in range(A_HEADS)] + [dk_ref[h] for h in range(A_KV_HEADS)]
        pieces += [dvt_ref[h].T for h in range(A_KV_HEADS)] + [cq_ref[...], ckv_ref[...], kr_ref[...]]
        o_ref[...] = jnp.concatenate(pieces, axis=-1)

    return _row_call(body, s, [(dq, 1), (dk, 1), (dvt, 2), (dcq, 0), (dckv, 0), (dkr, 0)],
                     [(_sds((s, 1280), F32), 0)], name=name)[0]


def _ffn_up(x, g, wgu, *, name, tm=512, rider=None):
    s, k = x.shape
    f = wgu.shape[1] // 2
    tn = _col_tile(k, f)
    nj = f // tn

    def body(x_ref, g_ref, wg_ref, wu_ref, dgate_ref, dup_ref, act_ref, xn_ref, xn_sc):
        @pl.when(pl.program_id(1) == 0)
        def _():
            _, xhat = _rms_stats(x_ref[...])
            xn = (xhat * g_ref[...]).astype(BF16)
            xn_sc[...] = xn
            xn_ref[...] = xn

        xn = xn_sc[...]
        gg = jnp.dot(xn, wg_ref[...], preferred_element_type=F32)
        uu = jnp.dot(xn, wu_ref[...], preferred_element_type=F32)
        sg = _sigmoid(gg)
        silu = gg * sg
        dgate_ref[...] = (uu * (sg * (1.0 + gg * (1.0 - sg)))).astype(BF16)
        dup_ref[...] = silu.astype(BF16)
        act_ref[...] = (silu * uu).astype(BF16)

    tile = pl.BlockSpec((tm, tn), lambda i, j: (i, j))
    return _call_with_rider(
        body, rider, name=name, grid=(s // tm, nj),
        in_specs=[pl.BlockSpec((tm, k), lambda i, j: (i, 0)), pl.BlockSpec((1, k), lambda i, j: (0, 0)),
                  pl.BlockSpec((k, tn), lambda i, j: (0, j)), pl.BlockSpec((k, tn), lambda i, j: (0, j + nj))],
        out_specs=[tile, tile, tile, pl.BlockSpec((tm, k), lambda i, j: (i, 0))],
        out_shape=[jax.ShapeDtypeStruct((s, f), BF16)] * 3 + [jax.ShapeDtypeStruct((s, k), BF16)],
        scratch_shapes=[pltpu.VMEM((tm, k), BF16)],
        compiler_params=_cp("arbitrary", "arbitrary"), args=(x, g, wgu, wgu))


def _mm_res_fwd(a, w, res, *, scale, name, tm=512):
    s, k = a.shape
    n = w.shape[1]

    def body(a_ref, w_ref, r_ref, o_ref):
        o_ref[...] = r_ref[...] + scale * jnp.dot(a_ref[...], w_ref[...], preferred_element_type=F32)

    return pl.pallas_call(
        body, name=name, grid=(s // tm,),
        in_specs=[pl.BlockSpec((tm, k), lambda i: (i, 0)), pl.BlockSpec((k, n), lambda i: (0, 0)),
                  pl.BlockSpec((tm, n), lambda i: (i, 0))],
        out_specs=pl.BlockSpec((tm, n), lambda i: (i, 0)),
        out_shape=jax.ShapeDtypeStruct((s, n), F32),
        compiler_params=_cp("arbitrary"))(a, w, res)


def _ffn_down_bwd(dh, wd, dact_dgate, dact_dup, *, scale, name, tm=512, rider=None):
    s, d = dh.shape
    f = wd.shape[0]
    tn = _col_tile(d, f)

    def body(dh_ref, wd_ref, fg_ref, fu_ref, dg_ref, du_ref):
        dhb = (dh_ref[...] * scale).astype(BF16)
        da = lax.dot_general(dhb, wd_ref[...], NT, preferred_element_type=F32)
        dg_ref[...] = (da * fg_ref[...].astype(F32)).astype(BF16)
        du_ref[...] = (da * fu_ref[...].astype(F32)).astype(BF16)

    tile = pl.BlockSpec((tm, tn), lambda i, j: (i, j))
    return _call_with_rider(
        body, rider, name=name, grid=(s // tm, f // tn),
        in_specs=[pl.BlockSpec((tm, d), lambda i, j: (i, 0)), pl.BlockSpec((tn, d), lambda i, j: (j, 0)), tile, tile],
        out_specs=[tile, tile],
        out_shape=[jax.ShapeDtypeStruct((s, f), BF16)] * 2, scratch_shapes=[],
        compiler_params=_cp("arbitrary", "arbitrary"), args=(dh, wd, dact_dgate, dact_dup))


def _mm_tn(a, bs, *, name, b_scale=1.0, ts=512, rider=None):
    bs = list(bs) if isinstance(bs, (list, tuple)) else [bs]
    s, k = a.shape
    n = bs[0].shape[1]
    tn = _col_tile(k, n, 12 * 2**20)
    per = n // tn

    def body(a_ref, *refs):
        b_refs, o_ref = refs[:-1], refs[-1]
        j = pl.program_id(0)

        @pl.when(pl.program_id(1) == 0)
        def _():
            o_ref[...] = jnp.zeros_like(o_ref)

        for m, b_ref in enumerate(b_refs):
            def acc(b_ref=b_ref):
                bv = b_ref[...]
                if b_scale != 1.0:
                    bv = bv * b_scale
                o_ref[...] += lax.dot_general(a_ref[...].astype(BF16), bv.astype(BF16), TN, preferred_element_type=F32)

            if len(b_refs) == 1:
                acc()
            else:
                pl.when(jnp.logical_and(j >= m * per, j < (m + 1) * per))(acc)

    def b_spec(m):
        def idx(j, t):
            mine = jnp.logical_and(j >= m * per, j < (m + 1) * per)
            return (jnp.where(mine, t, 0), jnp.clip(j - m * per, 0, per - 1))
        return pl.BlockSpec((ts, tn), idx)

    (out,), rode = _call_with_rider(
        body, rider, name=name, grid=(per * len(bs), s // ts),
        in_specs=[pl.BlockSpec((ts, k), lambda j, t: (t, 0))] + [b_spec(m) for m in range(len(bs))],
        out_specs=[pl.BlockSpec((k, tn), lambda j, t: (0, j))],
        out_shape=[jax.ShapeDtypeStruct((k, n * len(bs)), F32)], scratch_shapes=[],
        compiler_params=_cp("arbitrary", "arbitrary"), args=(a, *bs))
    return out if rider is None else (out, rode)


def _mm_nt(dy, w, *, name, tm=512):
    s, n = dy.shape
    k = w.shape[0]

    def body(dy_ref, w_ref, o_ref):
        o_ref[...] = lax.dot_general(dy_ref[...].astype(BF16), w_ref[...], NT, preferred_element_type=F32)

    return pl.pallas_call(
        body, name=name, grid=(s // tm,),
        in_specs=[pl.BlockSpec((tm, n), lambda i: (i, 0)), pl.BlockSpec((k, n), lambda i: (0, 0))],
        out_specs=pl.BlockSpec((tm, k), lambda i: (i, 0)),
        out_shape=jax.ShapeDtypeStruct((s, k), F32),
        compiler_params=_cp("arbitrary"))(dy, w)


def _mm_nt_rmsbwd(pairs, x, g, dres, *, name, tm=256, rider=None):
    s, k = x.shape
    npairs = len(pairs)
    pairs = [pr if len(pr) == 3 else (pr[0], pr[1], 0) for pr in pairs]

    def body(*refs):
        dy_refs = refs[0:2 * npairs:2]
        w_refs = refs[1:2 * npairs:2]
        rest = refs[2 * npairs:]
        x_ref, g_ref = rest[0], rest[1]
        if dres is None:
            dx_ref, dg_ref = rest[2], rest[3]
        else:
            dres_ref, dx_ref, dg_ref = rest[2], rest[3], rest[4]
        dxn = None
        for dy_ref, w_ref in zip(dy_refs, w_refs):
            t = lax.dot_general(dy_ref[...].astype(BF16), w_ref[...], NT, preferred_element_type=F32)
            dxn = t if dxn is None else dxn + t
        dx, dgrow = _rms_bwd(dxn, x_ref[...], g_ref[...])
        if dres is not None:
            dx = dx + dres_ref[...]
        dx_ref[...] = dx

        @pl.when(pl.program_id(0) == 0)
        def _():
            dg_ref[...] = jnp.zeros_like(dg_ref)

        dg_ref[...] += jnp.sum(dgrow, axis=0, keepdims=True)

    in_specs, args = [], []
    for dy, w, cb in pairs:
        n = dy.shape[1]
        in_specs += [pl.BlockSpec((tm, n), lambda i: (i, 0)), pl.BlockSpec((k, n), lambda i, cb=cb: (0, cb))]
        args += [dy, w]
    row = pl.BlockSpec((tm, k), lambda i: (i, 0))
    vec = pl.BlockSpec((1, k), lambda i: (0, 0))
    in_specs += [row, vec]
    args += [x, g]
    if dres is not None:
        in_specs.append(row)
        args.append(dres)
    (dx, dgain), rode = _call_with_rider(
        body, rider, name=name, grid=(s // tm,), in_specs=in_specs, out_specs=[row, vec],
        out_shape=[jax.ShapeDtypeStruct((s, k), F32), jax.ShapeDtypeStruct((1, k), F32)], scratch_shapes=[],
        compiler_params=_cp("arbitrary"), args=args)
    return (dx, dgain) if rider is None else (dx, dgain, rode)


def _ple_fwd(h, g, wg, p, wp, *, name, tm=512):
    s, d = h.shape
    pd = p.shape[1]

    def body(h_ref, g_ref, wg_ref, p_ref, wp_ref, o_ref, xn_ref, gate_ref, pp_ref):
        hv = h_ref[...]
        _, xhat = _rms_stats(hv)
        xn = (xhat * g_ref[...]).astype(BF16)
        xn_ref[...] = xn
        gate = _sigmoid(jnp.dot(xn, wg_ref[...], preferred_element_type=F32))
        pp = jnp.dot(p_ref[...].astype(BF16), wp_ref[...], preferred_element_type=F32)
        gate_ref[...] = gate.astype(BF16)
        pp_ref[...] = pp.astype(BF16)
        o_ref[...] = hv + gate * pp

    row = pl.BlockSpec((tm, d), lambda i: (i, 0))
    return pl.pallas_call(
        body, name=name, grid=(s // tm,),
        in_specs=[row, pl.BlockSpec((1, d), lambda i: (0, 0)), pl.BlockSpec((d, d), lambda i: (0, 0)),
                  pl.BlockSpec((tm, pd), lambda i: (i, 0)), pl.BlockSpec((pd, d), lambda i: (0, 0))],
        out_specs=[row, row, row, row],
        out_shape=[jax.ShapeDtypeStruct((s, d), F32)] + [jax.ShapeDtypeStruct((s, d), BF16)] * 3,
        compiler_params=_cp("arbitrary"))(h, g, wg, p, wp)


def _ple_bwd_elem(dh, gate, pp, *, name, tm=512):
    s, d = dh.shape

    def body(dh_ref, gate_ref, pp_ref, dz_ref, dpp_ref):
        dhv = dh_ref[...]
        gt = gate_ref[...].astype(F32)
        dz_ref[...] = (dhv * pp_ref[...].astype(F32) * (gt * (1.0 - gt))).astype(BF16)
        dpp_ref[...] = (dhv * gt).astype(BF16)

    row = pl.BlockSpec((tm, d), lambda i: (i, 0))
    return pl.pallas_call(
        body, name=name, grid=(s // tm,), in_specs=[row, row, row], out_specs=[row, row],
        out_shape=[jax.ShapeDtypeStruct((s, d), BF16)] * 2,
        compiler_params=_cp("arbitrary"))(dh, gate, pp)


def _final_loss(h, g, tgt, *, name, tm=512):
    s, d = h.shape

    def body(h_ref, g_ref, t_ref, loss_ref, dh_ref, dg_ref):
        @pl.when(pl.program_id(0) == 0)
        def _():
            loss_ref[...] = jnp.zeros_like(loss_ref)
            dg_ref[...] = jnp.zeros_like(dg_ref)

        hv = h_ref[...]
        gv = g_ref[...]
        _, xhat = _rms_stats(hv)
        err = xhat * gv - t_ref[...]
        per_row = jnp.mean(err * err, axis=-1, keepdims=True)
        loss_ref[...] += 0.5 * jnp.sum(per_row, axis=0, keepdims=True)
        dx, dgrow = _rms_bwd(err * (1.0 / d), hv, gv)
        dh_ref[...] = dx
        dg_ref[...] += jnp.sum(dgrow, axis=0, keepdims=True)

    row = pl.BlockSpec((tm, d), lambda i: (i, 0))
    vec = pl.BlockSpec((1, d), lambda i: (0, 0))
    return pl.pallas_call(
        body, name=name, grid=(s // tm,), in_specs=[row, vec, row],
        out_specs=[pl.BlockSpec((1, LANES), lambda i: (0, 0)), row, vec],
        out_shape=[jax.ShapeDtypeStruct((1, LANES), F32), jax.ShapeDtypeStruct((s, d), F32),
                   jax.ShapeDtypeStruct((1, d), F32)],
        compiler_params=_cp("arbitrary"))(h, g, tgt)


def _rope_fwd(y1, y2, cos, sin, *, name, tm=512):
    s, r = y1.shape

    def body(a_ref, b_ref, c_ref, s_ref, o_ref):
        o_ref[...] = a_ref[...] * c_ref[...] + b_ref[...] * s_ref[...]

    row = pl.BlockSpec((tm, r), lambda i: (i, 0))
    return pl.pallas_call(
        body, name=name, grid=(s // tm,), in_specs=[row] * 4, out_specs=row,
        out_shape=jax.ShapeDtypeStruct((s, r), F32), compiler_params=_cp("arbitrary"))(y1, y2, cos, sin)


def _split3(v):
    h1 = v.astype(BF16)
    r1 = v - h1.astype(F32)
    h2 = r1.astype(BF16)
    h3 = (r1 - h2.astype(F32)).astype(BF16)
    return h1, h2, h3


def _tri(tb, upper):
    r = lax.broadcasted_iota(jnp.int32, (tb, tb), 0)
    c = lax.broadcasted_iota(jnp.int32, (tb, tb), 1)
    return jnp.where((r <= c) if upper else (r >= c), 1.0, 0.0).astype(BF16)


def _fox_gate_fwd(ft, bf, *, out_scale, name, tb=512):
    nh, s = ft.shape

    def body(f_ref, b_ref, o_ref, carry):
        @pl.when(pl.program_id(0) == 0)
        def _():
            carry[...] = jnp.zeros_like(carry)

        z = f_ref[...] + b_ref[...]
        lf = jnp.minimum(z, 0.0) - jnp.log(1.0 + jnp.exp(-jnp.abs(z)))
        tri = _tri(tb, True)
        cs = sum(jnp.dot(t, tri, preferred_element_type=F32) for t in _split3(lf)) + carry[...]
        for n, term in enumerate(_split3(cs * out_scale)):
            o_ref[n] = term
        carry[...] += jnp.sum(lf, axis=-1, keepdims=True)

    return pl.pallas_call(
        body, name=name, grid=(s // tb,),
        in_specs=[pl.BlockSpec((nh, tb), lambda t: (0, t)), pl.BlockSpec((nh, 1), lambda t: (0, 0))],
        out_specs=pl.BlockSpec((3, nh, tb), lambda t: (0, 0, t)),
        out_shape=jax.ShapeDtypeStruct((3, nh, s), BF16),
        scratch_shapes=[pltpu.VMEM((nh, 1), F32)], compiler_params=_cp("arbitrary"))(ft, bf)


def _fox_gate_bwd(drow, dcol, ft, bf, *, inv_scale, name, tb=512):
    nh, s = ft.shape
    nb = s // tb

    def body(dr_ref, dc_ref, f_ref, b_ref, df_ref, db_ref, carry):
        @pl.when(pl.program_id(0) == 0)
        def _():
            carry[...] = jnp.zeros_like(carry)
            db_ref[...] = jnp.zeros_like(db_ref)

        dc = (dr_ref[...] - dc_ref[...]) * inv_scale
        tri = _tri(tb, False)
        suf = sum(jnp.dot(t, tri, preferred_element_type=F32) for t in _split3(dc)) + carry[...]
        z = f_ref[...] + b_ref[...]
        dz = suf * (1.0 / (1.0 + jnp.exp(z)))
        df_ref[...] = dz
        db_ref[...] += jnp.sum(dz, axis=-1, keepdims=True)
        carry[...] += jnp.sum(dc, axis=-1, keepdims=True)

    rev = pl.BlockSpec((nh, tb), lambda t: (0, nb - 1 - t))
    one = pl.BlockSpec((nh, 1), lambda t: (0, 0))
    return pl.pallas_call(
        body, name=name, grid=(nb,), in_specs=[rev, rev, rev, one], out_specs=[rev, one],
        out_shape=[jax.ShapeDtypeStruct((nh, s), F32), jax.ShapeDtypeStruct((nh, 1), F32)],
        scratch_shapes=[pltpu.VMEM((nh, 1), F32)], compiler_params=_cp("arbitrary"))(drow, dcol, ft, bf)


def _tri_fwd(t, nq):
    i = sum((t >= (r * (r + 1)) // 2).astype(jnp.int32) for r in range(1, nq))
    return i, t - (i * (i + 1)) // 2


def _tri_bwd(t, nq):
    j = sum((t >= r * nq - (r * (r - 1)) // 2).astype(jnp.int32) for r in range(1, nq))
    return j, j + t - (j * nq - (j * (j - 1)) // 2)


def _scores_t(k, q, *, scale, diag):
    s = lax.dot_general(k, q, NT, preferred_element_type=F32) * scale
    if diag:
        r = lax.broadcasted_iota(jnp.int32, s.shape, 0)
        c = lax.broadcasted_iota(jnp.int32, s.shape, 1)
        s = jnp.where(r <= c, s, MASK_VALUE)
    return s


def _causal_fwd_t(q, k, vt, *, scale, name, tq, hb=2, rider=None):
    nh, s, dq = q.shape
    dv = vt.shape[1]
    nq = s // tq
    nsteps = (nq * (nq + 1)) // 2

    def body(q_ref, k_ref, vt_ref, o_ref, lse_ref, m_sc, l_sc, acc_sc):
        i, j = _tri_fwd(pl.program_id(1), nq)

        @pl.when(j == 0)
        def _():
            m_sc[...] = jnp.full_like(m_sc, MASK_VALUE)
            l_sc[...] = jnp.zeros_like(l_sc)
            acc_sc[...] = jnp.zeros_like(acc_sc)

        def step(diag):
            for u in range(hb):
                sc = _scores_t(k_ref[u], q_ref[u], scale=scale, diag=diag)
                m_prev = m_sc[u]
                m_new = jnp.maximum(m_prev, jnp.max(sc, axis=0, keepdims=True))
                alpha = jnp.exp(m_prev - m_new)
                pr = jnp.exp(sc - m_new)
                l_new = alpha * l_sc[u] + jnp.sum(pr, axis=0, keepdims=True)
                acc = alpha * acc_sc[u] + jnp.dot(vt_ref[u], pr.astype(BF16), preferred_element_type=F32)
                if diag:
                    o_ref[u] = (acc / l_new).astype(BF16)
                    lse_ref[u] = m_new + jnp.log(l_new)
                else:
                    m_sc[u], l_sc[u], acc_sc[u] = m_new, l_new, acc

        pl.when(j < i)(functools.partial(step, False))
        pl.when(j == i)(functools.partial(step, True))

    def qi(t):
        return _tri_fwd(t, nq)[0]

    def kj(t):
        return _tri_fwd(t, nq)[1]

    return _call_with_rider(
        body, rider, name=name, grid=(nh // hb, nsteps),
        in_specs=[pl.BlockSpec((hb, tq, dq), lambda hp, t: (hp, qi(t), 0)),
                  pl.BlockSpec((hb, tq, dq), lambda hp, t: (hp, kj(t), 0)),
                  pl.BlockSpec((hb, dv, tq), lambda hp, t: (hp, 0, kj(t)))],
        out_specs=[pl.BlockSpec((hb, dv, tq), lambda hp, t: (hp, 0, qi(t))),
                   pl.BlockSpec((hb, 1, tq), lambda hp, t: (hp, 0, qi(t)))],
        out_shape=[jax.ShapeDtypeStruct((nh, dv, s), BF16), jax.ShapeDtypeStruct((nh, 1, s), F32)],
        scratch_shapes=[pltpu.VMEM((hb, 1, tq), F32), pltpu.VMEM((hb, 1, tq), F32), pltpu.VMEM((hb, dv, tq), F32)],
        compiler_params=_cp("arbitrary", "arbitrary"), args=(q, k, vt))


def _causal_bwd_t(q, k, v, ot, dot_, lse, *, scale, name, tq, hb=2, rider=None):
    nh, s, dq = q.shape
    dv = v.shape[-1]
    nq = s // tq
    nsteps = (nq * (nq + 1)) // 2

    def body(q_ref, k_ref, v_ref, ot_ref, dot_ref, lse_ref, dq_ref, dk_ref, dvt_ref):
        t = pl.program_id(1)
        j, i = _tri_bwd(t, nq)

        @pl.when(t == 0)
        def _():
            dq_ref[...] = jnp.zeros_like(dq_ref)

        def step(diag):
            rows = pl.ds(pl.multiple_of(i * tq, tq), tq)
            for u in range(hb):
                qv, kv, dov = q_ref[u], k_ref[u], dot_ref[u]
                pr = jnp.exp(_scores_t(kv, qv, scale=scale, diag=diag) - lse_ref[u])
                dp = jnp.dot(v_ref[u], dov, preferred_element_type=F32)
                delta = jnp.sum(dov.astype(F32) * ot_ref[u].astype(F32), axis=0, keepdims=True)
                dsb = ((pr * (dp - delta)) * scale).astype(BF16)
                d_v = lax.dot_general(dov, pr.astype(BF16), NT, preferred_element_type=F32)
                d_k = jnp.dot(dsb, qv, preferred_element_type=F32)
                if diag:
                    dvt_ref[u], dk_ref[u] = d_v, d_k
                else:
                    dvt_ref[u] += d_v
                    dk_ref[u] += d_k
                dq_ref[u, rows, :] += lax.dot_general(dsb, kv, TN, preferred_element_type=F32)

        pl.when(i > j)(functools.partial(step, False))
        pl.when(i == j)(functools.partial(step, True))

    def qi(t):
        return _tri_bwd(t, nq)[1]

    def kj(t):
        return _tri_bwd(t, nq)[0]

    rows_q = pl.BlockSpec((hb, tq, dq), lambda hp, t: (hp, qi(t), 0))
    rows_k = pl.BlockSpec((hb, tq, dq), lambda hp, t: (hp, kj(t), 0))
    lanes_q = pl.BlockSpec((hb, dv, tq), lambda hp, t: (hp, 0, qi(t)))
    return _call_with_rider(
        body, rider, name=name, grid=(nh // hb, nsteps),
        in_specs=[rows_q, rows_k, pl.BlockSpec((hb, tq, dv), lambda hp, t: (hp, kj(t), 0)), lanes_q, lanes_q,
                  pl.BlockSpec((hb, 1, tq), lambda hp, t: (hp, 0, qi(t)))],
        out_specs=[pl.BlockSpec((hb, s, dq), lambda hp, t: (hp, 0, 0)), rows_k,
                   pl.BlockSpec((hb, dv, tq), lambda hp, t: (hp, 0, kj(t)))],
        out_shape=[jax.ShapeDtypeStruct((nh, s, dq), F32), jax.ShapeDtypeStruct((nh, s, dq), F32),
                   jax.ShapeDtypeStruct((nh, dv, s), F32)],
        scratch_shapes=[], compiler_params=_cp("arbitrary", "arbitrary"), args=(q, k, v, ot, dot_, lse))


def _swa_scores_t(k, q, dist, ok, *, scale, slope):
    s = lax.dot_general(k, q, NT, preferred_element_type=F32) * scale - slope * dist.astype(F32)
    return jnp.where(ok, s, MASK_VALUE)


def _swa_geometry(tb, w, has_other):
    r = lax.broadcasted_iota(jnp.int32, (tb, tb), 0)
    c = lax.broadcasted_iota(jnp.int32, (tb, tb), 1)
    d_same = c - r
    ok_same = jnp.logical_and(d_same >= 0, d_same < w)

    def other(ncols):
        rr = lax.broadcasted_iota(jnp.int32, (w, ncols), 0)
        cc = lax.broadcasted_iota(jnp.int32, (w, ncols), 1)
        dd = cc + w - rr
        return dd, jnp.logical_and(dd < w, has_other)

    return (d_same, ok_same), other


def _swa_fwd_t(q, k, vt, slopes_sinks, *, scale, window, name, tb=256):
    nh, s, d = q.shape
    nkv = k.shape[0]
    grp = nh // nkv
    w = window
    per = tb // w
    assert tb % w == 0

    def body(q_ref, kc_ref, kp_ref, vc_ref, vp_ref, ss_ref, o_ref, lse_ref):
        kvh, i = pl.program_id(0), pl.program_id(1)
        (d_c, ok_c), other = _swa_geometry(tb, w, i > 0)
        d_p, ok_p = other(tb)
        for g in range(grp):
            h = kvh * grp + g
            slope, sink = ss_ref[0, h], ss_ref[1, h]
            qg = q_ref[g]
            s_c = _swa_scores_t(kc_ref[...], qg, d_c, ok_c, scale=scale, slope=slope)
            s_p = _swa_scores_t(kp_ref[...], qg, d_p, ok_p, scale=scale, slope=slope)
            m = jnp.maximum(jnp.maximum(jnp.max(s_c, axis=0, keepdims=True), jnp.max(s_p, axis=0, keepdims=True)), sink)
            p_c, p_p = jnp.exp(s_c - m), jnp.exp(s_p - m)
            l = jnp.sum(p_c, axis=0, keepdims=True) + jnp.sum(p_p, axis=0, keepdims=True) + jnp.exp(sink - m)
            acc = (jnp.dot(vc_ref[...], p_c.astype(BF16), preferred_element_type=F32)
                   + jnp.dot(vp_ref[...], p_p.astype(BF16), preferred_element_type=F32))
            o_ref[g] = (acc / l).astype(BF16)
            lse_ref[g] = m + jnp.log(l)

    def prev(i):
        return jnp.maximum(i * per - 1, 0)

    return pl.pallas_call(
        body, name=name, grid=(nkv, s // tb),
        in_specs=[pl.BlockSpec((grp, tb, d), lambda kh, i: (kh, i, 0)),
                  pl.BlockSpec((None, tb, d), lambda kh, i: (kh, i, 0)),
                  pl.BlockSpec((None, w, d), lambda kh, i: (kh, prev(i), 0)),
                  pl.BlockSpec((None, d, tb), lambda kh, i: (kh, 0, i)),
                  pl.BlockSpec((None, d, w), lambda kh, i: (kh, 0, prev(i))),
                  pl.BlockSpec(memory_space=pltpu.SMEM)],
        out_specs=[pl.BlockSpec((grp, d, tb), lambda kh, i: (kh, 0, i)), pl.BlockSpec((grp, 1, tb), lambda kh, i: (kh, 0, i))],
        out_shape=[jax.ShapeDtypeStruct((nh, d, s), BF16), jax.ShapeDtypeStruct((nh, 1, s), F32)],
        compiler_params=_cp("arbitrary", "arbitrary"))(q, k, k, vt, vt, slopes_sinks)


def _swa_bwd_t(q, k, v, ot, dot_, lse, slopes_sinks, *, scale, window, name, tb=256, rider=None):
    nh, s, d = q.shape
    nkv = k.shape[0]
    grp = nh // nkv
    w = window
    per = tb // w
    nb = s // tb

    def body(qc_ref, qn_ref, kc_ref, kp_ref, vc_ref, vp_ref, oc_ref, on_ref, doc_ref, don_ref, lc_ref, ln_ref, ss_ref,
             dq_ref, dk_ref, dvt_ref, dsink_ref):
        kvh, i = pl.program_id(0), pl.program_id(1)

        @pl.when(i == 0)
        def _():
            dsink_ref[...] = jnp.zeros_like(dsink_ref)

        (d_c, ok_c), other = _swa_geometry(tb, w, i > 0)
        d_p, ok_p = other(tb)
        d_n, ok_n = _swa_geometry(tb, w, i < nb - 1)[1](w)
        kc, kp, vc, vp = kc_ref[...], kp_ref[...], vc_ref[...], vp_ref[...]
        k_last, v_last = kc[tb - w:, :], vc[tb - w:, :]
        dk_acc = jnp.zeros((tb, d), F32)
        dv_acc = jnp.zeros((d, tb), F32)
        dk_tail = jnp.zeros((w, d), F32)
        dv_tail = jnp.zeros((d, w), F32)
        for g in range(grp):
            h = kvh * grp + g
            slope, sink = ss_ref[0, h], ss_ref[1, h]
            qg, dog, lse_c = qc_ref[g], doc_ref[g], lc_ref[g]
            delta = jnp.sum(dog.astype(F32) * oc_ref[g].astype(F32), axis=0, keepdims=True)
            p_c = jnp.exp(_swa_scores_t(kc, qg, d_c, ok_c, scale=scale, slope=slope) - lse_c)
            p_p = jnp.exp(_swa_scores_t(kp, qg, d_p, ok_p, scale=scale, slope=slope) - lse_c)
            ds_c = ((p_c * (jnp.dot(vc, dog, preferred_element_type=F32) - delta)) * scale).astype(BF16)
            ds_p = ((p_p * (jnp.dot(vp, dog, preferred_element_type=F32) - delta)) * scale).astype(BF16)
            dq_ref[g] = (lax.dot_general(ds_c, kc, TN, preferred_element_type=F32)
                         + lax.dot_general(ds_p, kp, TN, preferred_element_type=F32))
            dk_acc += jnp.dot(ds_c, qg, preferred_element_type=F32)
            dv_acc += lax.dot_general(dog, p_c.astype(BF16), NT, preferred_element_type=F32)
            dsink_ref[g] -= jnp.broadcast_to(jnp.sum(jnp.exp(sink - lse_c) * delta, axis=1, keepdims=True), (1, LANES))
            qn, don = qn_ref[g], don_ref[g]
            delta_n = jnp.sum(don.astype(F32) * on_ref[g].astype(F32), axis=0, keepdims=True)
            p_n = jnp.exp(_swa_scores_t(k_last, qn, d_n, ok_n, scale=scale, slope=slope) - ln_ref[g])
            ds_n = ((p_n * (jnp.dot(v_last, don, preferred_element_type=F32) - delta_n)) * scale).astype(BF16)
            dk_tail += jnp.dot(ds_n, qn, preferred_element_type=F32)
            dv_tail += lax.dot_general(don, p_n.astype(BF16), NT, preferred_element_type=F32)
        dk_ref[...] = dk_acc
        dvt_ref[...] = dv_acc
        dk_ref[tb - w:, :] += dk_tail
        dvt_ref[:, tb - w:] += dv_tail

    def prev(i):
        return jnp.maximum(i * per - 1, 0)

    def nxt(i):
        return jnp.minimum((i + 1) * per, s // w - 1)

    return _call_with_rider(
        body, rider, name=name, grid=(nkv, nb), scratch_shapes=[],
        args=(q, q, k, k, v, v, ot, ot, dot_, dot_, lse, lse, slopes_sinks),
        in_specs=[pl.BlockSpec((grp, tb, d), lambda kh, i: (kh, i, 0)),
                  pl.BlockSpec((grp, w, d), lambda kh, i: (kh, nxt(i), 0)),
                  pl.BlockSpec((None, tb, d), lambda kh, i: (kh, i, 0)),
                  pl.BlockSpec((None, w, d), lambda kh, i: (kh, prev(i), 0)),
                  pl.BlockSpec((None, tb, d), lambda kh, i: (kh, i, 0)),
                  pl.BlockSpec((None, w, d), lambda kh, i: (kh, prev(i), 0)),
                  pl.BlockSpec((grp, d, tb), lambda kh, i: (kh, 0, i)),
                  pl.BlockSpec((grp, d, w), lambda kh, i: (kh, 0, nxt(i))),
                  pl.BlockSpec((grp, d, tb), lambda kh, i: (kh, 0, i)),
                  pl.BlockSpec((grp, d, w), lambda kh, i: (kh, 0, nxt(i))),
                  pl.BlockSpec((grp, 1, tb), lambda kh, i: (kh, 0, i)),
                  pl.BlockSpec((grp, 1, w), lambda kh, i: (kh, 0, nxt(i))),
                  pl.BlockSpec(memory_space=pltpu.SMEM)],
        out_specs=[pl.BlockSpec((grp, tb, d), lambda kh, i: (kh, i, 0)),
                   pl.BlockSpec((None, tb, d), lambda kh, i: (kh, i, 0)),
                   pl.BlockSpec((None, d, tb), lambda kh, i: (kh, 0, i)),
                   pl.BlockSpec((None, grp, 1, LANES), lambda kh, i: (kh, 0, 0, 0))],
        out_shape=[jax.ShapeDtypeStruct((nh, s, d), F32), jax.ShapeDtypeStruct((nkv, s, d), F32),
                   jax.ShapeDtypeStruct((nkv, d, s), F32), jax.ShapeDtypeStruct((nkv, grp, 1, LANES), F32)],
        compiler_params=_cp("arbitrary", "arbitrary"))


def _adamw(w, g, m, v, *, name):
    shape = w.shape
    cols = shape[-1]
    rows = int(np.prod(shape[:-1])) if len(shape) > 1 else 1
    tr = _row_tile(rows, cols)
    c1 = 1.0 - ADAM_B1 ** ADAM_STEP
    c2 = 1.0 - ADAM_B2 ** ADAM_STEP

    def body(w_ref, g_ref, m_ref, v_ref, d_ref, mo_ref, vo_ref):
        gv = g_ref[...]
        mn = ADAM_B1 * m_ref[...] + (1.0 - ADAM_B1) * gv
        vn = ADAM_B2 * v_ref[...] + (1.0 - ADAM_B2) * (gv * gv)
        mo_ref[...] = mn
        vo_ref[...] = vn
        d_ref[...] = -ADAM_LR * ((mn / c1) / (jnp.sqrt(vn / c2) + ADAM_EPS) + ADAM_WD * w_ref[...])

    blk = pl.BlockSpec((tr, cols), lambda i: (i, 0))
    outs = pl.pallas_call(
        body, name=name, grid=(rows // tr,), in_specs=[blk] * 4, out_specs=[blk] * 3,
        out_shape=[jax.ShapeDtypeStruct((rows, cols), F32)] * 3,
        compiler_params=_cp("arbitrary"))(*[a.reshape(rows, cols) for a in (w, g, m, v)])
    return tuple(a.reshape(shape) for a in outs)


def _hbm_spec():
    return pl.BlockSpec(memory_space=pl.ANY)


def _mesh_place():
    x, y, c = lax.axis_index("x"), lax.axis_index("y"), lax.axis_index("c")
    return x, y, c, [(1 - x, y), (x, 1 - y), (1 - x, 1 - y)]


def _half_rows(c, rows, align):
    return pl.ds(pl.multiple_of(c * (rows // 2), align), rows // 2)


def _part(ref, mode, k, n, rows=None):
    if mode == "cols":
        cols = pl.ds(pl.multiple_of(k * n, LANES), n)
        return ref.at[:, cols] if rows is None else ref.at[rows, cols]
    return ref.at[k] if rows is None else ref.at[k, rows, :]


class _Rider:
    def __init__(self, inputs, out_shape, n_sems, start, finish):
        self.inputs, self.out_shape, self.n_sems, self.start, self.finish = inputs, out_shape, n_sems, start, finish


def _call_with_rider(body, rider, *, name, grid, in_specs, out_specs, out_shape, scratch_shapes, compiler_params, args):
    if rider is None:
        outs = pl.pallas_call(body, name=name, grid=grid, in_specs=in_specs, out_specs=out_specs, out_shape=out_shape,
                              scratch_shapes=scratch_shapes, compiler_params=compiler_params)(*args)
        return outs, []
    n_in, n_out, n_sc = len(in_specs), len(out_specs), len(scratch_shapes)
    n_rin, n_rout = len(rider.inputs), len(rider.out_shape)

    def wrapped(*refs):
        pos = 0
        groups = []
        for n in (n_in, n_rin, n_out, n_rout, n_sc, 2):
            groups.append(refs[pos:pos + n])
            pos += n
        ins, rins, outs, routs, scratch, sems = groups
        ids = [pl.program_id(a) for a in range(len(grid))]
        first = functools.reduce(jnp.logical_and, [i == 0 for i in ids])
        last = functools.reduce(jnp.logical_and, [i == g - 1 for i, g in zip(ids, grid)])
        pl.when(first)(lambda: rider.start(rins, routs, *sems))
        body(*ins, *outs, *scratch)
        pl.when(last)(lambda: rider.finish(rins, routs, *sems))

    outs = pl.pallas_call(
        wrapped, name=name, grid=grid, in_specs=list(in_specs) + [_hbm_spec()] * n_rin,
        out_specs=list(out_specs) + [_hbm_spec()] * n_rout, out_shape=list(out_shape) + list(rider.out_shape),
        scratch_shapes=list(scratch_shapes) + [pltpu.SemaphoreType.DMA((rider.n_sems,))] * 2,
        compiler_params=compiler_params)(*args, *rider.inputs)
    return outs[:n_out], outs[n_out:]


def _run_rider(rider, *, name):
    n_rin = len(rider.inputs)

    def body(*refs):
        rins, routs, sems = refs[:n_rin], refs[n_rin:-2], refs[-2:]
        rider.start(rins, routs, *sems)
        rider.finish(rins, routs, *sems)

    return pl.pallas_call(
        body, name=name, in_specs=[_hbm_spec()] * n_rin, out_specs=[_hbm_spec()] * len(rider.out_shape),
        out_shape=rider.out_shape, scratch_shapes=[pltpu.SemaphoreType.DMA((rider.n_sems,))] * 2)(*rider.inputs)


def _gather_rider(shards, modes):
    n_arr = len(shards)
    out_shape = [jax.ShapeDtypeStruct((s.shape[0], N_CHIPS * s.shape[1]) if m == "cols" else (N_CHIPS,) + s.shape, s.dtype)
                 for s, m in zip(shards, modes)]
    per = 4

    def copies(srcs, dsts, send_sems, recv_sems):
        x, y, c, chips = _mesh_place()
        me = 2 * x + y
        sends, waits = [], []
        for i in range(n_arr):
            r, n = shards[i].shape
            rows = _half_rows(c, r, 16)

            def copy(slot, src, dst, to, i=i):
                return pltpu.make_async_remote_copy(src_ref=src, dst_ref=dst, send_sem=send_sems.at[i * per + slot],
                                                    recv_sem=recv_sems.at[i * per + slot], device_id=to, device_id_type=MESH)

            own = _part(dsts[i], modes[i], me, n)
            sends.append(copy(0, srcs[i], own, (x, y, 1 - c)))
            waits.append(copy(0, own, own, (x, y, 1 - c)))
            for j, (px, py) in enumerate(chips):
                sends.append(copy(1 + j, srcs[i].at[rows], _part(dsts[i], modes[i], me, n, rows), (px, py, c)))
                theirs = _part(dsts[i], modes[i], 2 * px + py, n, rows)
                waits.append(copy(1 + j, theirs, theirs, (px, py, c)))
        return sends, waits

    def start(*refs):
        for cp in copies(*refs)[0]:
            cp.start()

    def finish(*refs):
        sends, waits = copies(*refs)
        for cp in waits:
            cp.wait_recv()
        for cp in sends:
            cp.wait_send()

    return _Rider(list(shards), out_shape, per * n_arr, start, finish)


def _gather_forward(dsts, shard_shapes, modes, *, name):
    n_arr = len(dsts)

    def body(*refs):
        outs = refs[n_arr:2 * n_arr]
        send_sems, recv_sems = refs[2 * n_arr:]
        x, y, c, chips = _mesh_place()
        cps = []
        for i in range(n_arr):
            r, n = shard_shapes[i]
            for j, (px, py) in enumerate(chips):
                def view(hc, i=i, px=px, py=py, r=r, n=n):
                    return _part(outs[i], modes[i], 2 * px + py, n, _half_rows(hc, r, 16))

                def copy(ref, i=i, j=j):
                    return pltpu.make_async_remote_copy(src_ref=ref, dst_ref=ref, send_sem=send_sems.at[3 * i + j],
                                                        recv_sem=recv_sems.at[3 * i + j], device_id=(x, y, 1 - c), device_id_type=MESH)

                cps.append((copy(view(c)), copy(view(1 - c))))
        for send, _ in cps:
            send.start()
        for send, theirs in cps:
            theirs.wait_recv()
            send.wait_send()

    return pl.pallas_call(
        body, name=name, in_specs=[_hbm_spec()] * n_arr, out_specs=[_hbm_spec()] * n_arr,
        out_shape=[jax.ShapeDtypeStruct(d.shape, d.dtype) for d in dsts],
        input_output_aliases={i: i for i in range(n_arr)},
        scratch_shapes=[pltpu.SemaphoreType.DMA((3 * n_arr,)), pltpu.SemaphoreType.DMA((3 * n_arr,))])(*dsts)


def _blk_view(a, mode):
    return a[None] if mode == "cols" else a


def _swap_rider(arrs, modes):
    n_arr = len(arrs)
    out_shape = [jax.ShapeDtypeStruct((a.shape[0] // 2, a.shape[1]) if m == "cols" else (a.shape[0], a.shape[1] // 2, a.shape[2]), a.dtype)
                 for a, m in zip(arrs, modes)]

    def copies(srcs, dsts, send_sems, recv_sems):
        x, y, c, _ = _mesh_place()
        cps = []
        for i in range(n_arr):
            if modes[i] == "cols":
                src = srcs[i].at[_half_rows(1 - c, arrs[i].shape[0], 8)]
            else:
                src = srcs[i].at[:, _half_rows(1 - c, arrs[i].shape[1], 8), :]
            cps.append(pltpu.make_async_remote_copy(src_ref=src, dst_ref=dsts[i], send_sem=send_sems.at[i],
                                                    recv_sem=recv_sems.at[i], device_id=(x, y, 1 - c), device_id_type=MESH))
        return cps

    def start(*refs):
        for cp in copies(*refs):
            cp.start()

    def finish(*refs):
        for cp in copies(*refs):
            cp.wait()

    return _Rider(list(arrs), out_shape, n_arr, start, finish)


def _rs_pair_add(arr, landed, place, *, name):
    nb, r, c = arr.shape
    rh = r // 2
    tr = _row_tile(rh, c)
    nt = rh // tr

    def body(p_ref, a_ref, l_ref, o_ref):
        o_ref[...] = (a_ref[...] + l_ref[...]).astype(BF16)

    grid_spec = pltpu.PrefetchScalarGridSpec(
        num_scalar_prefetch=1, grid=(nb, nt),
        in_specs=[pl.BlockSpec((None, tr, c), lambda b, t, p_ref: (b, p_ref[1] * nt + t, 0)),
                  pl.BlockSpec((None, tr, c), lambda b, t, p_ref: (b, t, 0))],
        out_specs=pl.BlockSpec((None, tr, c), lambda b, t, p_ref: (b, t, 0)))
    return pl.pallas_call(
        body, name=name, grid_spec=grid_spec, out_shape=jax.ShapeDtypeStruct((nb, rh, c), BF16),
        compiler_params=_cp("arbitrary", "arbitrary"))(place, arr, landed)


def _exchange_rider(parts, modes):
    n_arr = len(parts)
    out_shape = []
    for a, m in zip(parts, modes):
        shp = (a.shape[0], a.shape[1] // N_CHIPS) if m == "cols" else a.shape[1:]
        out_shape.append(jax.ShapeDtypeStruct((3,) + shp, a.dtype))

    def copies(srcs, dsts, send_sems, recv_sems):
        x, y, c, chips = _mesh_place()
        cps = []
        for i in range(n_arr):
            n = out_shape[i].shape[-1]
            for j, (px, py) in enumerate(chips):
                cps.append(pltpu.make_async_remote_copy(
                    src_ref=_part(srcs[i], modes[i], 2 * px + py, n), dst_ref=dsts[i].at[j],
                    send_sem=send_sems.at[3 * i + j], recv_sem=recv_sems.at[3 * i + j],
                    device_id=(px, py, c), device_id_type=MESH))
        return cps

    def start(*refs):
        for cp in copies(*refs):
            cp.start()

    def finish(*refs):
        for cp in copies(*refs):
            cp.wait()

    return _Rider(list(parts), out_shape, 3 * n_arr, start, finish)


def _rs_chip_sum(part, landed, mode, place, *, name):
    _, rh, n = landed.shape
    tr = _row_tile(rh, n)
    nt = rh // tr

    def body(p_ref, a_ref, l_ref, o_ref):
        o_ref[...] = ((a_ref[...].astype(F32) + l_ref[0].astype(F32)) + l_ref[1].astype(F32)) + l_ref[2].astype(F32)

    if mode == "cols":
        own = pl.BlockSpec((tr, n), lambda t, p_ref: (t, p_ref[0]))
    else:
        own = pl.BlockSpec((None, tr, n), lambda t, p_ref: (p_ref[0], t, 0))
    grid_spec = pltpu.PrefetchScalarGridSpec(
        num_scalar_prefetch=1, grid=(nt,),
        in_specs=[own, pl.BlockSpec((3, tr, n), lambda t, p_ref: (0, t, 0))],
        out_specs=pl.BlockSpec((tr, n), lambda t, p_ref: (p_ref[1] * nt + t, 0)))
    return pl.pallas_call(
        body, name=name, grid_spec=grid_spec, out_shape=jax.ShapeDtypeStruct((2 * rh, n), F32),
        compiler_params=_cp("arbitrary"))(place, part, landed)


def _rs_pair_join(halves, *, name):
    n_arr = len(halves)

    def body(*refs):
        outs = refs[n_arr:2 * n_arr]
        send_sems, recv_sems = refs[2 * n_arr:]
        x, y, c, _ = _mesh_place()
        cps = []
        for i in range(n_arr):
            rows = _half_rows(c, halves[i].shape[0], 8)
            cps.append(pltpu.make_async_remote_copy(src_ref=outs[i].at[rows], dst_ref=outs[i].at[rows], send_sem=send_sems.at[i],
                                                    recv_sem=recv_sems.at[i], device_id=(x, y, 1 - c), device_id_type=MESH))
        for cp in cps:
            cp.start()
        for i, cp in enumerate(cps):
            cp.wait_send()
            theirs = outs[i].at[_half_rows(1 - c, halves[i].shape[0], 8)]
            pltpu.make_async_remote_copy(src_ref=theirs, dst_ref=theirs, send_sem=send_sems.at[i], recv_sem=recv_sems.at[i],
                                         device_id=(x, y, 1 - c), device_id_type=MESH).wait_recv()

    return pl.pallas_call(
        body, name=name, in_specs=[_hbm_spec()] * n_arr, out_specs=[_hbm_spec()] * n_arr,
        out_shape=[jax.ShapeDtypeStruct(h.shape, h.dtype) for h in halves],
        input_output_aliases={i: i for i in range(n_arr)},
        scratch_shapes=[pltpu.SemaphoreType.DMA((n_arr,)), pltpu.SemaphoreType.DMA((n_arr,))])(*halves)


def _allreduce_small(v, *, name):
    r, c = v.shape

    def body(v_ref, o_ref, gath, send_sems, recv_sems):
        x, y, cc, _ = _mesh_place()
        me = 4 * x + 2 * y + cc
        gath[me] = v_ref[...]
        cps = []
        for rel in range(1, 8):
            px = 1 - x if rel & 4 else x
            py = 1 - y if rel & 2 else y
            pc = 1 - cc if rel & 1 else cc

            def copy(slot, px=px, py=py, pc=pc, rel=rel):
                return pltpu.make_async_remote_copy(
                    src_ref=v_ref, dst_ref=gath.at[slot], send_sem=send_sems.at[rel - 1],
                    recv_sem=recv_sems.at[rel - 1], device_id=(px, py, pc), device_id_type=MESH)

            cps.append((copy(me), copy(4 * px + 2 * py + pc)))
        for send, _ in cps:
            send.start()
        for send, theirs in cps:
            theirs.wait_recv()
            send.wait_send()
        tot = gath[0]
        for d in range(1, 8):
            tot = tot + gath[d]
        o_ref[...] = tot

    vm = pl.BlockSpec(memory_space=pltpu.VMEM)
    return pl.pallas_call(
        body, name=name, in_specs=[vm], out_specs=vm, out_shape=jax.ShapeDtypeStruct((r, c), F32),
        scratch_shapes=[pltpu.VMEM((8, r, c), F32), pltpu.SemaphoreType.DMA((7,)), pltpu.SemaphoreType.DMA((7,))])(v)


def _rope_tables(s, reps):
    half = B_ROPE // 2
    inv = ROPE_THETA ** (-jnp.arange(0, B_ROPE, 2, dtype=F32) / B_ROPE)
    ang = jnp.arange(s, dtype=F32)[:, None] * inv[None, :]
    return jnp.tile(jnp.cos(ang), (1, reps)), jnp.tile(jnp.sin(ang), (1, reps))


def _alibi_slopes():
    return 2.0 ** (-8.0 * jnp.arange(1, A_HEADS + 1, dtype=F32) / A_HEADS)


def _ffn_fwd(h, norm, wts, tag, rider=None, on_rode=None):
    (dact_dgate, dact_dup, act, xn), rode = _ffn_up(h, norm, wts["wgu"], name=f"{tag}_up", rider=rider)
    if on_rode is not None:
        on_rode(rode)
    out = _mm_res_fwd(act, wts["wd"], h, scale=FFN_RES_SCALE, name=f"{tag}_down")
    return out, dict(h_in=h, dact_dgate=dact_dgate, dact_dup=dact_dup, act=act, xn=xn), rode


def _ffn_bwd(dh, norm, wts, sv, tag, rider=None, own=None):
    (dgate, dup), rode = _ffn_down_bwd(dh, wts["wd"], sv["dact_dgate"], sv["dact_dup"], scale=FFN_RES_SCALE,
                                      name=f"{tag}_down_bwd", rider=rider)
    d_wd = _mm_tn(sv["act"], dh, b_scale=FFN_RES_SCALE, name=f"{tag}_dwd")
    pairs = [(dgate, wts["wgu"], 0), (dup, wts["wgu"], 1)]
    if own is None:
        d_wgu = _mm_tn(sv["xn"], [dgate, dup], name=f"{tag}_dwgu")
        dh_in, dnorm = _mm_nt_rmsbwd(pairs, sv["h_in"], norm, dh, name=f"{tag}_dx")
    else:
        wd_ready, wgu_ready, done = own
        d_wgu, brought = _mm_tn(sv["xn"], [dgate, dup], name=f"{tag}_dwgu", rider=wd_ready(d_wd))
        dh_in, dnorm, brought = _mm_nt_rmsbwd(pairs, sv["h_in"], norm, dh, name=f"{tag}_dx", rider=wgu_ready(brought, d_wgu))
        done(brought)
    return dh_in, dnorm, d_wgu, d_wd, rode


def _even_weights(w_in, w_uq, w_ukv):
    half = B_ROPE // 2
    base = w_in.shape[1]
    kr1, kr2 = w_in[:, base - B_ROPE:base - half], w_in[:, base - half:]
    w_in_cat = jnp.concatenate([w_in, -kr2, kr1, jnp.zeros((w_in.shape[0], 64), w_in.dtype)], axis=1)
    u3 = w_uq.reshape(w_uq.shape[0], B_HEADS, B_NOPE + B_ROPE)
    nope = u3[:, :, :B_NOPE].reshape(w_uq.shape[0], -1)
    rot = u3[:, :, B_NOPE:].reshape(w_uq.shape[0], -1)
    swapped = jnp.concatenate([-u3[:, :, B_NOPE + half:], u3[:, :, B_NOPE:B_NOPE + half]], axis=-1).reshape(w_uq.shape[0], -1)
    return w_in_cat, jnp.concatenate([nope, rot, swapped], axis=1), w_ukv


def _even_fwd(h, w, i, rider=None):
    s = h.shape[0]
    qa, ka, va, vat, c_q, c_kv, kr_blk, xn = _ev_in_fwd(h, w["mix_norm"][i:i + 1], w["ev_in_cat"], name="ev_in")
    cos32, sin32 = _rope_tables(s, 2)
    kro = _rope_fwd(kr_blk[:, :B_ROPE], kr_blk[:, B_ROPE:2 * B_ROPE], cos32, sin32, name="ev_k_rope")
    ss = jnp.stack([_alibi_slopes(), w["ev_sinks"].reshape(-1)])
    oa, lse_a = _swa_fwd_t(qa, ka, vat, ss, scale=A_HEAD_DIM ** -0.5, window=WINDOW, name="swa_fwd")
    cos256, sin256 = _rope_tables(s, 2 * B_HEADS)
    qb, xn_q = _ev_q_fwd(c_q, w["ev_cq_norm"], w["ev_q_cat"], cos256, sin256, name="ev_q_up")
    kb, vb, vbt, xn_kv = _ev_kv_fwd(c_kv, w["ev_ckv_norm"], w["ev_ukv"], kro, name="ev_kv_up")
    (ob, lse_b), rode = _causal_fwd_t(qb, kb, vbt, scale=(B_NOPE + B_ROPE) ** -0.5, name="mla_fwd", tq=512, hb=4, rider=rider)
    attn = jnp.concatenate([oa.transpose(2, 0, 1).reshape(s, -1), ob.transpose(2, 0, 1).reshape(s, -1)], axis=-1)
    out = _mm_res_fwd(attn, w["ev_out"], h, scale=1.0, name="ev_out")
    sv = dict(h_in=h, xn=xn, c_q=c_q, c_kv=c_kv, xn_q=xn_q, xn_kv=xn_kv, qa=qa, ka=ka, va=va, oa=oa, lse_a=lse_a,
              ss=ss, qb=qb, kb=kb, vb=vb, ob=ob, lse_b=lse_b, attn=attn, cos32=cos32, sin32=sin32,
              cos256=cos256, sin256=sin256)
    return out, sv, rode


def _even_bwd(dh, w, sv, i, rider=None):
    s = dh.shape[0]
    half = B_ROPE // 2
    g = {}
    dattn = _mm_nt(dh, w["ev_out"], name="ev_out_dx")
    g["ev_w_out"] = _mm_tn(sv["attn"], dh, name="ev_out_dw")
    doa = dattn[:, :512].reshape(s, A_HEADS, A_HEAD_DIM).transpose(1, 2, 0).astype(BF16)
    dob = dattn[:, 512:].reshape(s, B_HEADS, B_V).transpose(1, 2, 0).astype(BF16)
    first, then = rider if isinstance(rider, tuple) else (None, None)
    (dqa, dka, dva, dsink), brought = _swa_bwd_t(sv["qa"], sv["ka"], sv["va"], sv["oa"], doa, sv["lse_a"], sv["ss"],
                                                 scale=A_HEAD_DIM ** -0.5, window=WINDOW, name="swa_bwd", rider=first)
    if then is not None:
        rider = then(brought)
    g["ev_sinks"] = dsink[:, :, 0, 0].reshape(1, A_HEADS)
    (dqb, dkb, dvb), rode = _causal_bwd_t(sv["qb"], sv["kb"], sv["vb"], sv["ob"], dob, sv["lse_b"],
                                          scale=(B_NOPE + B_ROPE) ** -0.5, name="mla_bwd", tq=512, hb=4, rider=rider)
    dyq = _ev_q_merge(dqb, sv["cos256"], sv["sin256"], name="ev_q_merge")
    dwq = _mm_tn(sv["xn_q"], dyq, name="ev_q_up_dw")
    dcq, g["ev_cq_norm"] = _mm_nt_rmsbwd([(dyq, w["ev_q_cat"])], sv["c_q"], w["ev_cq_norm"], None, name="ev_q_up_dx")
    kq = sv["c_q"].shape[1]
    d_nope = dwq[:, :512].reshape(kq, B_HEADS, B_NOPE)
    d_rot = dwq[:, 512:768].reshape(kq, B_HEADS, B_ROPE)
    d_swp = dwq[:, 768:].reshape(kq, B_HEADS, B_ROPE)
    g["ev_w_uq"] = jnp.concatenate([d_nope, d_rot[:, :, :half] + d_swp[:, :, half:], d_rot[:, :, half:] - d_swp[:, :, :half]],
                                   axis=-1).reshape(kq, -1)
    dykv, dkr = _ev_kv_merge(dkb, dvb, sv["cos32"], sv["sin32"], name="ev_kv_merge")
    g["ev_w_ukv"] = _mm_tn(sv["xn_kv"], dykv, name="ev_kv_up_dw")
    dckv, g["ev_ckv_norm"] = _mm_nt_rmsbwd([(dykv, w["ev_ukv"])], sv["c_kv"], w["ev_ckv_norm"], None, name="ev_kv_up_dx")
    dycat = _ev_in_merge(dqa, dka, dva, dcq, dckv, dkr, name="ev_in_merge")
    dwin = _mm_tn(sv["xn"], dycat, name="ev_in_dw")
    base = 1184
    g["ev_w_in"] = jnp.concatenate([dwin[:, :base - B_ROPE],
                                    dwin[:, base - B_ROPE:base - half] + dwin[:, base + half:base + B_ROPE],
                                    dwin[:, base - half:base] - dwin[:, base:base + half]], axis=-1)
    dh_in, dnorm = _mm_nt_rmsbwd([(dycat, w["ev_in_cat"])], sv["h_in"], w["mix_norm"][i:i + 1], dh, name="ev_in_dx")
    return dh_in, dnorm, g, rode


def _odd_fwd(h, w, i, rider=None):
    s = h.shape[0]
    wd = C_HEADS * C_HEAD_DIM
    q, k, v, vt, y_f, xn = _fox_in_fwd(h, w["mix_norm"][i:i + 1], w["od_in_pad"], nheads=C_HEADS, dh=C_HEAD_DIM,
                                       q_ones=(0, 2, 3, 4), k_ones=(1,), name="od_in")
    scale = C_HEAD_DIM ** -0.5
    ft = y_f[:, :C_HEADS].T
    bf = w["od_b_f"].reshape(C_HEADS, 1)
    cb3 = _fox_gate_fwd(ft, bf, out_scale=-1.0 / scale, name="fox_gate_fwd")
    k = k + jnp.pad(cb3.transpose(1, 2, 0), ((0, 0), (0, 0), (C_HEAD_DIM + 2, LANES - C_HEAD_DIM - 5)))
    (o, lse), rode = _causal_fwd_t(q, k, vt, scale=scale, name="fox_fwd", tq=512, hb=4, rider=rider)
    attn = o.transpose(2, 0, 1).reshape(s, -1)
    out = _mm_res_fwd(attn, w["od_out"], h, scale=1.0, name="od_out")
    return out, dict(h_in=h, xn=xn, q=q, k=k, v=v, o=o, lse=lse, ft=ft, bf=bf, attn=attn), rode


def _odd_bwd(dh, w, sv, i, rider=None):
    s = dh.shape[0]
    g = {}
    dattn = _mm_nt(dh, w["od_out"], name="od_out_dx")
    g["od_w_out"] = _mm_tn(sv["attn"], dh, name="od_out_dw")
    do = dattn.reshape(s, C_HEADS, C_HEAD_DIM).transpose(1, 2, 0).astype(BF16)
    scale = C_HEAD_DIM ** -0.5
    (dq, dk, dv), rode = _causal_bwd_t(sv["q"], sv["k"], sv["v"], sv["o"], do, sv["lse"], scale=scale, name="fox_bwd",
                                       tq=512, hb=4, rider=rider)
    dft, dbf = _fox_gate_bwd(dq[:, :, C_HEAD_DIM + 1], dk[:, :, C_HEAD_DIM], sv["ft"], sv["bf"],
                             inv_scale=1.0 / scale, name="fox_gate_bwd")
    g["od_b_f"] = dbf.reshape(1, C_HEADS)
    wd = C_HEADS * C_HEAD_DIM
    dqkv = _merge_heads(dq, dk, dv, dh=C_HEAD_DIM, name="fox_merge")
    df = jnp.pad(dft.T, ((0, 0), (0, LANES - C_HEADS)))
    g["od_w_in"] = jnp.concatenate([_mm_tn(sv["xn"], dqkv, name="od_in_dw"),
                                    _mm_tn(sv["xn"], df, name="od_in_dwf")[:, :C_HEADS]], axis=-1)
    dh_in, dnorm = _mm_nt_rmsbwd([(dqkv, w["od_in_pad"], 0), (df, w["od_in_pad"], 3 * wd // LANES)],
                                 sv["h_in"], w["mix_norm"][i:i + 1], dh, name="od_in_dx")
    return dh_in, dnorm, g, rode


def _kernel_weights(full, replicated):
    w = dict(replicated)
    _install_weights(w, {(n, i): a for n, per_layer in full.items() for i, a in enumerate(per_layer)})
    return w


def _install_weights(w, got):
    raw = w.setdefault("raw", {})
    raw.update(got)
    for (n, i), a in got.items():
        if n in ("ffa_w_gate_up", "ffa_w_down", "ffb_w_gate_up", "ffb_w_down"):
            w.setdefault(n[:3], {}).setdefault(i, {})["wgu" if n.endswith("gate_up") else "wd"] = a
        elif n in ("ple_w_gate", "ple_w_proj"):
            w.setdefault("ple_gate" if n.endswith("gate") else "ple_proj", {})[i] = a
    if "ev_in_cat" not in w and all((n, 0) in raw for n in ("ev_w_in", "ev_w_uq", "ev_w_ukv", "ev_w_out")):
        w["ev_in_cat"], w["ev_q_cat"], w["ev_ukv"] = _even_weights(raw["ev_w_in", 0], raw["ev_w_uq", 0], raw["ev_w_ukv", 0])
        w["ev_out"] = raw["ev_w_out", 0]
    if "od_in_pad" not in w and all((n, 0) in raw for n in ("od_w_in", "od_w_out")):
        od_in = raw["od_w_in", 0]
        w["od_in_pad"] = jnp.pad(od_in, ((0, 0), (0, (-od_in.shape[1]) % LANES)))
        w["od_out"] = raw["od_w_out", 0]


def _keys(names, layer):
    return tuple((n, layer) for n in names)


_FFA, _FFB, _PLE = ("ffa_w_gate_up", "ffa_w_down"), ("ffb_w_gate_up", "ffb_w_down"), ("ple_w_gate", "ple_w_proj")
_EV, _OD = ("ev_w_in", "ev_w_uq", "ev_w_ukv", "ev_w_out"), ("od_w_in", "od_w_out")
_GATHER_FIRST = _keys(_FFA[:1], 0)
_GATHER_RIDES = {("ffa", 0): _keys(_FFA[1:] + _EV, 0), ("mix", 0): _keys(_FFB + _PLE, 0) + _keys(_FFA, 1),
                 ("ffb", 0): _keys(_OD, 0), ("mix", 1): _keys(_FFB + _PLE, 1)}
_REDUCE_RIDES = {("mix", 1): _keys(_FFB + _PLE, 1), ("mix", 0): _keys(_FFA, 1) + _keys(_OD, 0) + _keys(_FFB + _PLE, 0),
                 ("ffa", 0): _keys(_EV, 0)}
_REDUCE_OWN = ("ffa", 0)


def _local_step(x, p, tgt, w, ex=None):
    depth = p.shape[0]

    def gather_behind(host, fn, *args):
        keys = None if ex is None else _GATHER_RIDES.get(host)
        if keys is None:
            return fn(*args, None)[:-1]
        done = []

        def install(rode):
            if not done:
                _install_weights(w, ex.gather_finish(keys, rode, name=f"weight_forward_{host[0]}{host[1]}"))
                done.append(True)

        res = fn(*args, ex.gather_rider(keys), install) if fn is _ffn_fwd else fn(*args, ex.gather_rider(keys))
        install(res[-1])
        return res[:-1]

    h = x
    saved = []
    for i in range(depth):
        sv = {}
        h, sv["ffa"] = gather_behind(("ffa", i), _ffn_fwd, h, w["ffa_norm"][i:i + 1], w["ffa"][i], f"ffa{i}")
        h, sv["mix"] = gather_behind(("mix", i), _even_fwd if i % 2 == 0 else _odd_fwd, h, w, i)
        h, sv["ffb"] = gather_behind(("ffb", i), _ffn_fwd, h, w["ffb_norm"][i:i + 1], w["ffb"][i], f"ffb{i}")
        h_in = h
        h, xn, gate, pp = _ple_fwd(h, w["ple_norm"][i:i + 1], w["ple_gate"][i], p[i], w["ple_proj"][i], name=f"ple{i}")
        sv["ple"] = dict(h_in=h_in, xn=xn, gate=gate, pp=pp)
        saved.append(sv)
    loss_vec, dh, d_final = _final_loss(h, w["final_norm"].reshape(1, -1), tgt, name="final_loss")

    per_layer = [dict() for _ in range(depth)]
    mats = {}
    grads = {}

    def reduce_behind(host, fn, *args):
        keys = None if ex is None else _REDUCE_RIDES.get(host)
        if keys is None:
            return fn(*args, None)[:-1]
        states = []
        if fn is _even_bwd:
            swap, ctx = ex.swap_rider(keys, mats)

            def then(brought):
                states.append(ex.after_swap(ctx, brought))
                return states[0][0]

            res = fn(*args, (swap, then))
        else:
            states.append(ex.reduce_begin(keys, mats, tag=f"{host[0]}{host[1]}"))
            if fn is _ffn_bwd and host == _REDUCE_OWN:
                own = []

                def wd_ready(d_wd):
                    own.append(ex.reduce_begin(_keys(_FFA[1:], 0), {("ffa_w_down", 0): d_wd}, tag="own_wd"))
                    return own[0][0]

                def wgu_ready(brought, d_wgu):
                    ex.reduce_finish(own[0], brought)
                    own.append(ex.reduce_begin(_keys(_FFA[:1], 0), {("ffa_w_gate_up", 0): d_wgu}, tag="own_wgu"))
                    return own[1][0]

                res = fn(*args, states[0][0], (wd_ready, wgu_ready, lambda brought: ex.reduce_finish(own[1], brought)))
            else:
                res = fn(*args, states[0][0])
        ex.reduce_finish(states[0], res[-1])
        return res[:-1]

    for i in reversed(range(depth)):
        sv, gl = saved[i], per_layer[i]
        dz, dpp = _ple_bwd_elem(dh, sv["ple"]["gate"], sv["ple"]["pp"], name=f"ple{i}_bwd")
        mats["ple_w_gate", i] = _mm_tn(sv["ple"]["xn"], dz, name=f"ple{i}_dwg")
        mats["ple_w_proj", i] = _mm_tn(p[i], dpp, name=f"ple{i}_dwp")
        dh, gl["ple_norm"] = _mm_nt_rmsbwd([(dz, w["ple_gate"][i])], sv["ple"]["h_in"], w["ple_norm"][i:i + 1], dh,
                                           name=f"ple{i}_dx")
        dh, gl["ffb_norm"], mats["ffb_w_gate_up", i], mats["ffb_w_down", i] = reduce_behind(
            ("ffb", i), _ffn_bwd, dh, w["ffb_norm"][i:i + 1], w["ffb"][i], sv["ffb"], f"ffb{i}")
        dh, gl["mix_norm"], gm = reduce_behind(("mix", i), _even_bwd if i % 2 == 0 else _odd_bwd, dh, w, sv["mix"], i)
        for n, g in gm.items():
            if n in REPLICATED:
                grads[n] = g
            else:
                mats[n, 0] = g
        dh, gl["ffa_norm"], mats["ffa_w_gate_up", i], mats["ffa_w_down", i] = reduce_behind(
            ("ffa", i), _ffn_bwd, dh, w["ffa_norm"][i:i + 1], w["ffa"][i], sv["ffa"], f"ffa{i}")
    grads["final_norm"] = d_final.reshape(-1)
    for n in ("ffa_norm", "mix_norm", "ffb_norm", "ple_norm"):
        grads[n] = jnp.concatenate([per_layer[i][n] for i in range(depth)], axis=0)
    if ex is None:
        for n, _ in SHARDED:
            grads[n] = [mats[n, i] for i in range(depth) if (n, i) in mats]
    return loss_vec[0, 0], dh, grads


def _cut_mode(local_shape, axis, ncols):
    return "cols" if axis == 2 and ncols % LANES == 0 else "blk"


class _Exchange:
    def __init__(self, wts):
        self.place = jnp.stack([2 * lax.axis_index("x") + lax.axis_index("y"), lax.axis_index("c")]).astype(jnp.int32)
        self.info = {}
        for n, axis in SHARDED:
            wb = wts[n].astype(BF16)
            mode = _cut_mode(wb.shape, axis, wb.shape[2])
            for i in range(wb.shape[0]):
                self.info[n, i] = dict(shard=wb[i], mode=mode, axis=axis)
        self.halves = {}

    def _modes(self, keys):
        return [self.info[k]["mode"] for k in keys]

    def gather_rider(self, keys):
        return _gather_rider([self.info[k]["shard"] for k in keys], self._modes(keys))

    def gather_finish(self, keys, landed, *, name):
        outs = _gather_forward(landed, [self.info[k]["shard"].shape for k in keys], self._modes(keys), name=name)
        got = {}
        for k, dst in zip(keys, outs):
            if self.info[k]["mode"] == "blk":
                dst = dst.reshape(-1, dst.shape[2]) if self.info[k]["axis"] == 1 else jnp.moveaxis(dst, 0, 1).reshape(dst.shape[1], -1)
            got[k] = dst
        return got

    def gather(self, keys, *, name):
        return self.gather_finish(keys, _run_rider(self.gather_rider(keys), name=name), name=name + "_forward")

    def swap_rider(self, keys, mats):
        modes = self._modes(keys)
        arrs = []
        for k in keys:
            g2, (rr, cc) = mats[k], self.info[k]["shard"].shape
            if self.info[k]["mode"] == "blk":
                g2 = g2.reshape(N_CHIPS, rr, cc) if self.info[k]["axis"] == 1 else g2.reshape(rr, N_CHIPS, cc).transpose(1, 0, 2)
            arrs.append(g2)
        return _swap_rider(arrs, modes), (keys, modes, arrs)

    def after_swap(self, ctx, landed):
        keys, modes, arrs = ctx
        parts = []
        for (n, i), m, a, l in zip(keys, modes, arrs, landed):
            pt = _rs_pair_add(_blk_view(a, m), _blk_view(l, m), self.place, name=f"rs_pair_add_{n}{i}")
            parts.append(pt[0] if m == "cols" else pt)
        return _exchange_rider(parts, modes), keys, parts

    def reduce_begin(self, keys, mats, *, tag):
        rider, ctx = self.swap_rider(keys, mats)
        return self.after_swap(ctx, _run_rider(rider, name=f"rs_pair_swap_{tag}"))

    def reduce_finish(self, state, landed):
        _, keys, parts = state
        for (n, i), m, pt, l in zip(keys, self._modes(keys), parts, landed):
            self.halves[n, i] = _rs_chip_sum(pt, l, m, self.place, name=f"rs_chip_sum_{n}{i}")

    def reduce(self, keys, mats, *, tag):
        state = self.reduce_begin(keys, mats, tag=tag)
        self.reduce_finish(state, _run_rider(state[0], name=f"rs_chip_exchange_{tag}"))

    def join(self, wts):
        keys = list(self.info)
        joined = dict(zip(keys, _rs_pair_join([self.halves[k] for k in keys], name="rs_pair_join")))
        return {n: jnp.stack([joined[n, i] for i in range(wts[n].shape[0])]).reshape(wts[n].shape) for n, _ in SHARDED}


def _small_rows(vals):
    rows = []
    for n in REPLICATED:
        v = vals[n].reshape(-1)
        rows.append(jnp.pad(v, (0, (-v.shape[0]) % FLAT_COLS)).reshape(-1, FLAT_COLS))
    out = jnp.concatenate(rows, axis=0)
    return jnp.pad(out, ((0, (-out.shape[0]) % 8), (0, 0)))


def kernel(x, p, ffa_norm, ffa_w_gate_up, ffa_w_down, mix_norm, ffb_norm, ffb_w_gate_up, ffb_w_down, ple_norm, ple_w_gate, ple_w_proj, ev_w_in, ev_sinks, ev_cq_norm, ev_w_uq, ev_ckv_norm, ev_w_ukv, ev_w_out, od_w_in, od_b_f, od_w_out, final_norm, loss_target, m_ffa_norm, m_ffa_w_gate_up, m_ffa_w_down, m_mix_norm, m_ffb_norm, m_ffb_w_gate_up, m_ffb_w_down, m_ple_norm, m_ple_w_gate, m_ple_w_proj, m_ev_w_in, m_ev_sinks, m_ev_cq_norm, m_ev_w_uq, m_ev_ckv_norm, m_ev_w_ukv, m_ev_w_out, m_od_w_in, m_od_b_f, m_od_w_out, m_final_norm, v_ffa_norm, v_ffa_w_gate_up, v_ffa_w_down, v_mix_norm, v_ffb_norm, v_ffb_w_gate_up, v_ffb_w_down, v_ple_norm, v_ple_w_gate, v_ple_w_proj, v_ev_w_in, v_ev_sinks, v_ev_cq_norm, v_ev_w_uq, v_ev_ckv_norm, v_ev_w_ukv, v_ev_w_out, v_od_w_in, v_od_b_f, v_od_w_out, v_final_norm):
    env = dict(locals())
    wts = {n: env[n] for n in WEIGHT_ORDER}
    mom1 = {n: env["m_" + n] for n in WEIGHT_ORDER}
    mom2 = {n: env["v_" + n] for n in WEIGHT_ORDER}
    ex = _Exchange(wts)

    w = {n: wts[n] for n in REPLICATED}
    _install_weights(w, ex.gather(_GATHER_FIRST, name="weight_gather_first"))

    loss_part, grad_x, grads = _local_step(x[0], p[:, 0], loss_target[0], w, ex)
    loss = lax.psum(loss_part, ("x", "y", "c"))
    gout = ex.join(wts)
    small = _allreduce_small(_small_rows(grads), name="small_allreduce")
    r0 = 0
    for n in REPLICATED:
        size = int(np.prod(wts[n].shape))
        nr = -(-size // FLAT_COLS)
        gout[n] = small[r0:r0 + nr].reshape(-1)[:size].reshape(wts[n].shape)
        r0 += nr

    delta, new_m, new_v = {}, {}, {}
    for n in WEIGHT_ORDER:
        delta[n], new_m[n], new_v[n] = _adamw(wts[n], gout[n], mom1[n], mom2[n], name="adamw_" + n)
    return (loss, grad_x[None], *[gout[n] for n in WEIGHT_ORDER], *[delta[n] for n in WEIGHT_ORDER],
            *[new_m[n] for n in WEIGHT_ORDER], *[new_v[n] for n in WEIGHT_ORDER])
```

```python
import functools
import math

import numpy as np
import jax
import jax.numpy as jnp
from jax import lax
from jax.experimental import pallas as pl
from jax.experimental.pallas import tpu as pltpu

F32 = jnp.float32
BF16 = jnp.bfloat16
NT = (((1,), (1,)), ((), ()))
TN = (((0,), (0,)), ((), ()))
MESH = pl.DeviceIdType.MESH

RMS_EPS = 1e-6
FFN_RES_SCALE = 0.5
A_HEADS, A_KV_HEADS, A_HEAD_DIM, WINDOW = 8, 2, 64, 128
B_HEADS, B_Q_LORA, B_KV_LORA, B_NOPE, B_ROPE, B_V = 8, 256, 128, 64, 32, 64
ROPE_THETA = 10000.0
C_HEADS, C_HEAD_DIM = 16, 64
ADAM_LR, ADAM_B1, ADAM_B2, ADAM_EPS, ADAM_WD, ADAM_STEP = 0.001, 0.9, 0.999, 1e-08, 0.01, 10

N_CHIPS = 4
LANES = 128
FLAT_COLS = 1024
MASK_VALUE = -1e30
VMEM_LIMIT = 48 * 2**20

SHARDED = (
    ("ffa_w_gate_up", 2), ("ffa_w_down", 1), ("ffb_w_gate_up", 2), ("ffb_w_down", 1),
    ("ple_w_gate", 1), ("ple_w_proj", 2), ("ev_w_in", 2), ("ev_w_uq", 2), ("ev_w_ukv", 2),
    ("ev_w_out", 1), ("od_w_in", 2), ("od_w_out", 1))
REPLICATED = ("ffa_norm", "mix_norm", "ffb_norm", "ple_norm", "final_norm",
              "ev_sinks", "ev_cq_norm", "ev_ckv_norm", "od_b_f")
WEIGHT_ORDER = ("ffa_norm", "ffa_w_gate_up", "ffa_w_down", "mix_norm", "ffb_norm", "ffb_w_gate_up",
                "ffb_w_down", "ple_norm", "ple_w_gate", "ple_w_proj", "ev_w_in", "ev_sinks",
                "ev_cq_norm", "ev_w_uq", "ev_ckv_norm", "ev_w_ukv", "ev_w_out", "od_w_in", "od_b_f",
                "od_w_out", "final_norm")


def _cp(*sem):
    return pltpu.CompilerParams(dimension_semantics=sem, vmem_limit_bytes=VMEM_LIMIT)


def _sigmoid(z):
    return 1.0 / (1.0 + jnp.exp(-z))


def _rms_stats(xv):
    r = lax.rsqrt(jnp.mean(xv * xv, axis=-1, keepdims=True) + RMS_EPS)
    return r, xv * r


def _rms_bwd(dxn, xv, g):
    r, xhat = _rms_stats(xv)
    u = dxn * g
    dx = r * (u - xhat * jnp.mean(u * xhat, axis=-1, keepdims=True))
    return dx, dxn * xhat


def _col_tile(k_rows, n, budget_bytes=6 * 2**20):
    if k_rows * n * 4 <= budget_bytes or n % LANES:
        return n
    units = n // LANES
    best = LANES
    for d in range(1, units + 1):
        if units % d == 0 and k_rows * d * LANES * 4 <= budget_bytes:
            best = d * LANES
    return best


def _row_tile(rows, cols, target_elems=2**18):
    if rows * cols <= target_elems or rows % 8:
        return rows
    best = 8
    for d in range(8, rows + 1, 8):
        if rows % d == 0 and d * cols <= target_elems:
            best = d
    return best


def _fox_in_fwd(x, g, w, *, nheads, dh, q_ones, k_ones, name, tm=512):
    s, k = x.shape
    n = w.shape[1]
    wd = nheads * dh
    spare = LANES - dh

    def body(x_ref, g_ref, w_ref, q_ref, k_ref, v_ref, vt_ref, f_ref, xn_ref):
        _, xhat = _rms_stats(x_ref[...])
        xn = (xhat * g_ref[...]).astype(BF16)
        xn_ref[...] = xn
        y = jnp.dot(xn, w_ref[...], preferred_element_type=F32)
        f_ref[...] = y[:, 3 * wd:]
        lane = lax.broadcasted_iota(jnp.int32, (tm, spare), 1)

        def fill(cols):
            return functools.reduce(jnp.logical_or, [lane == c for c in cols]).astype(F32)

        q_fill, k_fill = fill(q_ones), fill(k_ones)
        for h in range(nheads):
            q_ref[h] = jnp.concatenate([y[:, h * dh:(h + 1) * dh], q_fill], axis=-1).astype(BF16)
            k_ref[h] = jnp.concatenate([y[:, wd + h * dh:wd + (h + 1) * dh], k_fill], axis=-1).astype(BF16)
            vh = y[:, 2 * wd + h * dh:2 * wd + (h + 1) * dh]
            v_ref[h] = vh.astype(BF16)
            vt_ref[h] = vh.T.astype(BF16)

    wide = pl.BlockSpec((nheads, tm, LANES), lambda i: (0, i, 0))
    return pl.pallas_call(
        body, name=name, grid=(s // tm,),
        in_specs=[pl.BlockSpec((tm, k), lambda i: (i, 0)), pl.BlockSpec((1, k), lambda i: (0, 0)),
                  pl.BlockSpec((k, n), lambda i: (0, 0))],
        out_specs=[wide, wide, pl.BlockSpec((nheads, tm, dh), lambda i: (0, i, 0)),
                   pl.BlockSpec((nheads, dh, tm), lambda i: (0, 0, i)), pl.BlockSpec((tm, LANES), lambda i: (i, 0)),
                   pl.BlockSpec((tm, k), lambda i: (i, 0))],
        out_shape=[jax.ShapeDtypeStruct((nheads, s, LANES), BF16)] * 2
        + [jax.ShapeDtypeStruct((nheads, s, dh), BF16), jax.ShapeDtypeStruct((nheads, dh, s), BF16),
           jax.ShapeDtypeStruct((s, LANES), F32), jax.ShapeDtypeStruct((s, k), BF16)],
        compiler_params=_cp("arbitrary"))(x, g, w)


def _merge_heads(dq, dk, dvt, *, dh, name, tm=512):
    nheads, s, _ = dq.shape

    def body(dq_ref, dk_ref, dvt_ref, o_ref):
        pieces = [dq_ref[h][:, :dh] for h in range(nheads)] + [dk_ref[h][:, :dh] for h in range(nheads)]
        pieces += [dvt_ref[h].T for h in range(nheads)]
        o_ref[...] = jnp.concatenate(pieces, axis=-1)

    wide = pl.BlockSpec((nheads, tm, LANES), lambda i: (0, i, 0))
    return pl.pallas_call(
        body, name=name, grid=(s // tm,),
        in_specs=[wide, wide, pl.BlockSpec((nheads, dh, tm), lambda i: (0, 0, i))],
        out_specs=pl.BlockSpec((tm, 3 * nheads * dh), lambda i: (i, 0)),
        out_shape=jax.ShapeDtypeStruct((s, 3 * nheads * dh), F32),
        compiler_params=_cp("arbitrary"))(dq, dk, dvt)


def _row_call(body, n_rows, ins, outs, *, name, tm=512):
    def spec(a, axis):
        shape = a.shape
        if axis is None:
            return pl.BlockSpec(shape, lambda i: (0,) * len(shape))
        blk = tuple(tm if d == axis else n for d, n in enumerate(shape))
        return pl.BlockSpec(blk, lambda i: tuple(i if d == axis else 0 for d in range(len(shape))))

    return pl.pallas_call(
        body, name=name, grid=(n_rows // tm,), in_specs=[spec(a, ax) for a, ax in ins],
        out_specs=[spec(a, ax) for a, ax in outs], out_shape=[a for a, _ in outs],
        compiler_params=_cp("arbitrary"))(*[a for a, _ in ins])


def _sds(shape, dtype):
    return jax.ShapeDtypeStruct(shape, dtype)


def _ev_in_fwd(x, g, w, *, name):
    s, k = x.shape
    d = A_HEAD_DIM

    def body(x_ref, g_ref, w_ref, q_ref, k_ref, v_ref, vt_ref, cq_ref, ckv_ref, kr_ref, xn_ref):
        _, xhat = _rms_stats(x_ref[...])
        xn = (xhat * g_ref[...]).astype(BF16)
        xn_ref[...] = xn
        y = jnp.dot(xn, w_ref[...], preferred_element_type=F32)
        for h in range(A_HEADS):
            q_ref[h] = y[:, h * d:(h + 1) * d].astype(BF16)
        for h in range(A_KV_HEADS):
            k_ref[h] = y[:, 512 + h * d:512 + (h + 1) * d].astype(BF16)
            vh = y[:, 640 + h * d:640 + (h + 1) * d]
            v_ref[h] = vh.astype(BF16)
            vt_ref[h] = vh.T.astype(BF16)
        cq_ref[...] = y[:, 768:1024]
        ckv_ref[...] = y[:, 1024:1152]
        kr_ref[...] = y[:, 1152:1280]

    return _row_call(
        body, s, [(x, 0), (g, None), (w, None)],
        [(_sds((A_HEADS, s, d), BF16), 1), (_sds((A_KV_HEADS, s, d), BF16), 1), (_sds((A_KV_HEADS, s, d), BF16), 1),
         (_sds((A_KV_HEADS, d, s), BF16), 2), (_sds((s, B_Q_LORA), F32), 0), (_sds((s, B_KV_LORA), F32), 0),
         (_sds((s, LANES), F32), 0), (_sds((s, k), BF16), 0)], name=name)


def _ev_q_fwd(x, g, w, cos, sin, *, name):
    s, k = x.shape
    rot = B_HEADS * B_ROPE

    def body(x_ref, g_ref, w_ref, c_ref, s_ref, q_ref, xn_ref):
        _, xhat = _rms_stats(x_ref[...])
        xn = (xhat * g_ref[...]).astype(BF16)
        xn_ref[...] = xn
        y = jnp.dot(xn, w_ref[...], preferred_element_type=F32)
        ro = y[:, 512:512 + rot] * c_ref[...] + y[:, 512 + rot:] * s_ref[...]
        zero = jnp.zeros((y.shape[0], LANES - B_NOPE - B_ROPE), F32)
        for h in range(B_HEADS):
            q_ref[h] = jnp.concatenate([y[:, h * B_NOPE:(h + 1) * B_NOPE], ro[:, h * B_ROPE:(h + 1) * B_ROPE], zero],
                                       axis=-1).astype(BF16)

    return _row_call(body, s, [(x, 0), (g, None), (w, None), (cos, 0), (sin, 0)],
                     [(_sds((B_HEADS, s, LANES), BF16), 1), (_sds((s, k), BF16), 0)], name=name)


def _ev_kv_fwd(x, g, w, kro, *, name):
    s, k = x.shape
    per = B_NOPE + B_V

    def body(x_ref, g_ref, w_ref, kr_ref, k_ref, v_ref, vt_ref, xn_ref):
        _, xhat = _rms_stats(x_ref[...])
        xn = (xhat * g_ref[...]).astype(BF16)
        xn_ref[...] = xn
        y = jnp.dot(xn, w_ref[...], preferred_element_type=F32)
        kr = kr_ref[...]
        zero = jnp.zeros((y.shape[0], LANES - B_NOPE - B_ROPE), F32)
        for h in range(B_HEADS):
            k_ref[h] = jnp.concatenate([y[:, h * per:h * per + B_NOPE], kr, zero], axis=-1).astype(BF16)
            vh = y[:, h * per + B_NOPE:(h + 1) * per]
            v_ref[h] = vh.astype(BF16)
            vt_ref[h] = vh.T.astype(BF16)

    return _row_call(body, s, [(x, 0), (g, None), (w, None), (kro, 0)],
                     [(_sds((B_HEADS, s, LANES), BF16), 1), (_sds((B_HEADS, s, B_V), BF16), 1),
                      (_sds((B_HEADS, B_V, s), BF16), 2), (_sds((s, k), BF16), 0)], name=name)


def _ev_q_merge(dq, cos, sin, *, name):
    nh, s, _ = dq.shape

    def body(dq_ref, c_ref, s_ref, o_ref):
        dro = jnp.concatenate([dq_ref[h][:, B_NOPE:B_NOPE + B_ROPE] for h in range(nh)], axis=-1)
        o_ref[...] = jnp.concatenate([dq_ref[h][:, :B_NOPE] for h in range(nh)] + [dro * c_ref[...], dro * s_ref[...]], axis=-1)

    return _row_call(body, s, [(dq, 1), (cos, 0), (sin, 0)], [(_sds((s, 2 * nh * B_NOPE), F32), 0)], name=name)[0]


def _ev_kv_merge(dk, dvt, cos, sin, *, name):
    nh, s, _ = dk.shape

    def body(dk_ref, dvt_ref, c_ref, s_ref, o_ref, kr_ref):
        pieces = []
        tot = None
        for h in range(nh):
            pieces += [dk_ref[h][:, :B_NOPE], dvt_ref[h].T]
            rot = dk_ref[h][:, B_NOPE:B_NOPE + B_ROPE]
            tot = rot if tot is None else tot + rot
        o_ref[...] = jnp.concatenate(pieces, axis=-1)
        kr_ref[...] = jnp.concatenate([tot * c_ref[...], tot * s_ref[...], jnp.zeros((tot.shape[0], LANES - 2 * B_ROPE), F32)],
                                      axis=-1)

    return _row_call(body, s, [(dk, 1), (dvt, 2), (cos, 0), (sin, 0)],
                     [(_sds((s, nh * (B_NOPE + B_V)), F32), 0), (_sds((s, LANES), F32), 0)], name=name)


def _ev_in_merge(dq, dk, dvt, dcq, dckv, dkr, *, name):
    s = dcq.shape[0]

    def body(dq_ref, dk_ref, dvt_ref, cq_ref, ckv_ref, kr_ref, o_ref):
        pieces = [dq_ref[h] for h in range(A_HEADS)] + [dk_ref[h] for h in range(A_KV_HEADS)]
        pieces += [dvt_ref[h].T for h in range(A_KV_HEADS)] + [cq_ref[...], ckv_ref[...], kr_ref[...]]
        o_ref[...] = jnp.concatenate(pieces, axis=-1)

    return _row_call(body, s, [(dq, 1), (dk, 1), (dvt, 2), (dcq, 0), (dckv, 0), (dkr, 0)],
                     [(_sds((s, 1280), F32), 0)], name=name)[0]


def _ffn_up(x, g, wgu, *, name, tm=512, rider=None):
    s, k = x.shape
    f = wgu.shape[1] // 2
    tn = _col_tile(k, f)
    nj = f // tn

    def body(x_ref, g_ref, wg_ref, wu_ref, dgate_ref, dup_ref, act_ref, xn_ref, xn_sc):
        @pl.when(pl.program_id(1) == 0)
        def _():
            _, xhat = _rms_stats(x_ref[...])
            xn = (xhat * g_ref[...]).astype(BF16)
            xn_sc[...] = xn
            xn_ref[...] = xn

        xn = xn_sc[...]
        gg = jnp.dot(xn, wg_ref[...], preferred_element_type=F32)
        uu = jnp.dot(xn, wu_ref[...], preferred_element_type=F32)
        sg = _sigmoid(gg)
        silu = gg * sg
        dgate_ref[...] = (uu * (sg * (1.0 + gg * (1.0 - sg)))).astype(BF16)
        dup_ref[...] = silu.astype(BF16)
        act_ref[...] = (silu * uu).astype(BF16)

    tile = pl.BlockSpec((tm, tn), lambda i, j: (i, j))
    return _call_with_rider(
        body, rider, name=name, grid=(s // tm, nj),
        in_specs=[pl.BlockSpec((tm, k), lambda i, j: (i, 0)), pl.BlockSpec((1, k), lambda i, j: (0, 0)),
                  pl.BlockSpec((k, tn), lambda i, j: (0, j)), pl.BlockSpec((k, tn), lambda i, j: (0, j + nj))],
        out_specs=[tile, tile, tile, pl.BlockSpec((tm, k), lambda i, j: (i, 0))],
        out_shape=[jax.ShapeDtypeStruct((s, f), BF16)] * 3 + [jax.ShapeDtypeStruct((s, k), BF16)],
        scratch_shapes=[pltpu.VMEM((tm, k), BF16)],
        compiler_params=_cp("arbitrary", "arbitrary"), args=(x, g, wgu, wgu))


def _mm_res_fwd(a, w, res, *, scale, name, tm=512, a_t=False):
    k, n = w.shape
    s = res.shape[0]

    def body(a_ref, w_ref, r_ref, o_ref):
        prod = (lax.dot_general(a_ref[...], w_ref[...], TN, preferred_element_type=F32) if a_t
                else jnp.dot(a_ref[...], w_ref[...], preferred_element_type=F32))
        o_ref[...] = r_ref[...] + scale * prod

    a_spec = pl.BlockSpec((k, tm), lambda i: (0, i)) if a_t else pl.BlockSpec((tm, k), lambda i: (i, 0))
    return pl.pallas_call(
        body, name=name, grid=(s // tm,),
        in_specs=[a_spec, pl.BlockSpec((k, n), lambda i: (0, 0)),
                  pl.BlockSpec((tm, n), lambda i: (i, 0))],
        out_specs=pl.BlockSpec((tm, n), lambda i: (i, 0)),
        out_shape=jax.ShapeDtypeStruct((s, n), F32),
        compiler_params=_cp("arbitrary"))(a, w, res)


def _ffn_down_bwd(dh, wd, dact_dgate, dact_dup, *, scale, name, tm=512, rider=None):
    s, d = dh.shape
    f = wd.shape[0]
    tn = _col_tile(d, f)

    def body(dh_ref, wd_ref, fg_ref, fu_ref, dg_ref, du_ref):
        dhb = (dh_ref[...] * scale).astype(BF16)
        da = lax.dot_general(dhb, wd_ref[...], NT, preferred_element_type=F32)
        dg_ref[...] = (da * fg_ref[...].astype(F32)).astype(BF16)
        du_ref[...] = (da * fu_ref[...].astype(F32)).astype(BF16)

    tile = pl.BlockSpec((tm, tn), lambda i, j: (i, j))
    return _call_with_rider(
        body, rider, name=name, grid=(s // tm, f // tn),
        in_specs=[pl.BlockSpec((tm, d), lambda i, j: (i, 0)), pl.BlockSpec((tn, d), lambda i, j: (j, 0)), tile, tile],
        out_specs=[tile, tile],
        out_shape=[jax.ShapeDtypeStruct((s, f), BF16)] * 2, scratch_shapes=[],
        compiler_params=_cp("arbitrary", "arbitrary"), args=(dh, wd, dact_dgate, dact_dup))


def _mm_tn(a, bs, *, name, b_scale=1.0, ts=512, rider=None, a_t=False):
    bs = list(bs) if isinstance(bs, (list, tuple)) else [bs]
    k, s = a.shape if a_t else a.shape[::-1]
    n = bs[0].shape[1]
    tn = _col_tile(k, n, 12 * 2**20)
    per = n // tn

    def body(a_ref, *refs):
        b_refs, o_ref = refs[:-1], refs[-1]
        j = pl.program_id(0)

        @pl.when(pl.program_id(1) == 0)
        def _():
            o_ref[...] = jnp.zeros_like(o_ref)

        for m, b_ref in enumerate(b_refs):
            def acc(b_ref=b_ref):
                bv = b_ref[...]
                if b_scale != 1.0:
                    bv = bv * b_scale
                av = a_ref[...].astype(BF16)
                o_ref[...] += (jnp.dot(av, bv.astype(BF16), preferred_element_type=F32) if a_t
                               else lax.dot_general(av, bv.astype(BF16), TN, preferred_element_type=F32))

            if len(b_refs) == 1:
                acc()
            else:
                pl.when(jnp.logical_and(j >= m * per, j < (m + 1) * per))(acc)

    def b_spec(m):
        def idx(j, t):
            mine = jnp.logical_and(j >= m * per, j < (m + 1) * per)
            return (jnp.where(mine, t, 0), jnp.clip(j - m * per, 0, per - 1))
        return pl.BlockSpec((ts, tn), idx)

    (out,), rode = _call_with_rider(
        body, rider, name=name, grid=(per * len(bs), s // ts),
        in_specs=[pl.BlockSpec((k, ts), lambda j, t: (0, t)) if a_t else pl.BlockSpec((ts, k), lambda j, t: (t, 0))]
        + [b_spec(m) for m in range(len(bs))],
        out_specs=[pl.BlockSpec((k, tn), lambda j, t: (0, j))],
        out_shape=[jax.ShapeDtypeStruct((k, n * len(bs)), F32)], scratch_shapes=[],
        compiler_params=_cp("arbitrary", "arbitrary"), args=(a, *bs))
    return out if rider is None else (out, rode)


def _mm_nt_t(dy, w, *, name, tm=512):
    s, n = dy.shape
    k = w.shape[0]

    def body(dy_ref, w_ref, o_ref):
        o_ref[...] = lax.dot_general(w_ref[...], dy_ref[...].astype(BF16), NT, preferred_element_type=F32).astype(BF16)

    return pl.pallas_call(
        body, name=name, grid=(s // tm,),
        in_specs=[pl.BlockSpec((tm, n), lambda i: (i, 0)), pl.BlockSpec((k, n), lambda i: (0, 0))],
        out_specs=pl.BlockSpec((k, tm), lambda i: (0, i)),
        out_shape=jax.ShapeDtypeStruct((k, s), BF16),
        compiler_params=_cp("arbitrary"))(dy, w)


def _mm_nt_rmsbwd(pairs, x, g, dres, *, name, tm=256, rider=None):
    s, k = x.shape
    npairs = len(pairs)
    pairs = [pr if len(pr) == 3 else (pr[0], pr[1], 0) for pr in pairs]

    def body(*refs):
        dy_refs = refs[0:2 * npairs:2]
        w_refs = refs[1:2 * npairs:2]
        rest = refs[2 * npairs:]
        x_ref, g_ref = rest[0], rest[1]
        if dres is None:
            dx_ref, dg_ref = rest[2], rest[3]
        else:
            dres_ref, dx_ref, dg_ref = rest[2], rest[3], rest[4]
        dxn = None
        for dy_ref, w_ref in zip(dy_refs, w_refs):
            t = lax.dot_general(dy_ref[...].astype(BF16), w_ref[...], NT, preferred_element_type=F32)
            dxn = t if dxn is None else dxn + t
        dx, dgrow = _rms_bwd(dxn, x_ref[...], g_ref[...])
        if dres is not None:
            dx = dx + dres_ref[...]
        dx_ref[...] = dx

        @pl.when(pl.program_id(0) == 0)
        def _():
            dg_ref[...] = jnp.zeros_like(dg_ref)

        dg_ref[...] += jnp.sum(dgrow, axis=0, keepdims=True)

    in_specs, args = [], []
    for dy, w, cb in pairs:
        n = dy.shape[1]
        in_specs += [pl.BlockSpec((tm, n), lambda i: (i, 0)), pl.BlockSpec((k, n), lambda i, cb=cb: (0, cb))]
        args += [dy, w]
    row = pl.BlockSpec((tm, k), lambda i: (i, 0))
    vec = pl.BlockSpec((1, k), lambda i: (0, 0))
    in_specs += [row, vec]
    args += [x, g]
    if dres is not None:
        in_specs.append(row)
        args.append(dres)
    (dx, dgain), rode = _call_with_rider(
        body, rider, name=name, grid=(s // tm,), in_specs=in_specs, out_specs=[row, vec],
        out_shape=[jax.ShapeDtypeStruct((s, k), F32), jax.ShapeDtypeStruct((1, k), F32)], scratch_shapes=[],
        compiler_params=_cp("arbitrary"), args=args)
    return (dx, dgain) if rider is None else (dx, dgain, rode)


def _ple_fwd(h, g, wg, p, wp, *, name, tm=512):
    s, d = h.shape
    pd = p.shape[1]

    def body(h_ref, g_ref, wg_ref, p_ref, wp_ref, o_ref, xn_ref, gate_ref, pp_ref):
        hv = h_ref[...]
        _, xhat = _rms_stats(hv)
        xn = (xhat * g_ref[...]).astype(BF16)
        xn_ref[...] = xn
        gate = _sigmoid(jnp.dot(xn, wg_ref[...], preferred_element_type=F32))
        pp = jnp.dot(p_ref[...].astype(BF16), wp_ref[...], preferred_element_type=F32)
        gate_ref[...] = gate.astype(BF16)
        pp_ref[...] = pp.astype(BF16)
        o_ref[...] = hv + gate * pp

    row = pl.BlockSpec((tm, d), lambda i: (i, 0))
    return pl.pallas_call(
        body, name=name, grid=(s // tm,),
        in_specs=[row, pl.BlockSpec((1, d), lambda i: (0, 0)), pl.BlockSpec((d, d), lambda i: (0, 0)),
                  pl.BlockSpec((tm, pd), lambda i: (i, 0)), pl.BlockSpec((pd, d), lambda i: (0, 0))],
        out_specs=[row, row, row, row],
        out_shape=[jax.ShapeDtypeStruct((s, d), F32)] + [jax.ShapeDtypeStruct((s, d), BF16)] * 3,
        compiler_params=_cp("arbitrary"))(h, g, wg, p, wp)


def _ple_bwd_elem(dh, gate, pp, *, name, tm=512):
    s, d = dh.shape

    def body(dh_ref, gate_ref, pp_ref, dz_ref, dpp_ref):
        dhv = dh_ref[...]
        gt = gate_ref[...].astype(F32)
        dz_ref[...] = (dhv * pp_ref[...].astype(F32) * (gt * (1.0 - gt))).astype(BF16)
        dpp_ref[...] = (dhv * gt).astype(BF16)

    row = pl.BlockSpec((tm, d), lambda i: (i, 0))
    return pl.pallas_call(
        body, name=name, grid=(s // tm,), in_specs=[row, row, row], out_specs=[row, row],
        out_shape=[jax.ShapeDtypeStruct((s, d), BF16)] * 2,
        compiler_params=_cp("arbitrary"))(dh, gate, pp)


def _final_loss(h, g, tgt, *, name, tm=512):
    s, d = h.shape

    def body(h_ref, g_ref, t_ref, loss_ref, dh_ref, dg_ref):
        @pl.when(pl.program_id(0) == 0)
        def _():
            loss_ref[...] = jnp.zeros_like(loss_ref)
            dg_ref[...] = jnp.zeros_like(dg_ref)

        hv = h_ref[...]
        gv = g_ref[...]
        _, xhat = _rms_stats(hv)
        err = xhat * gv - t_ref[...]
        per_row = jnp.mean(err * err, axis=-1, keepdims=True)
        loss_ref[...] += 0.5 * jnp.sum(per_row, axis=0, keepdims=True)
        dx, dgrow = _rms_bwd(err * (1.0 / d), hv, gv)
        dh_ref[...] = dx
        dg_ref[...] += jnp.sum(dgrow, axis=0, keepdims=True)

    row = pl.BlockSpec((tm, d), lambda i: (i, 0))
    vec = pl.BlockSpec((1, d), lambda i: (0, 0))
    return pl.pallas_call(
        body, name=name, grid=(s // tm,), in_specs=[row, vec, row],
        out_specs=[pl.BlockSpec((1, LANES), lambda i: (0, 0)), row, vec],
        out_shape=[jax.ShapeDtypeStruct((1, LANES), F32), jax.ShapeDtypeStruct((s, d), F32),
                   jax.ShapeDtypeStruct((1, d), F32)],
        compiler_params=_cp("arbitrary"))(h, g, tgt)


def _rope_fwd(y1, y2, cos, sin, *, name, tm=512):
    s, r = y1.shape

    def body(a_ref, b_ref, c_ref, s_ref, o_ref):
        o_ref[...] = a_ref[...] * c_ref[...] + b_ref[...] * s_ref[...]

    row = pl.BlockSpec((tm, r), lambda i: (i, 0))
    return pl.pallas_call(
        body, name=name, grid=(s // tm,), in_specs=[row] * 4, out_specs=row,
        out_shape=jax.ShapeDtypeStruct((s, r), F32), compiler_params=_cp("arbitrary"))(y1, y2, cos, sin)


def _split3(v):
    h1 = v.astype(BF16)
    r1 = v - h1.astype(F32)
    h2 = r1.astype(BF16)
    h3 = (r1 - h2.astype(F32)).astype(BF16)
    return h1, h2, h3


def _tri(tb, upper):
    r = lax.broadcasted_iota(jnp.int32, (tb, tb), 0)
    c = lax.broadcasted_iota(jnp.int32, (tb, tb), 1)
    return jnp.where((r <= c) if upper else (r >= c), 1.0, 0.0).astype(BF16)


def _fox_gate_fwd(ft, bf, *, out_scale, name, tb=512):
    nh, s = ft.shape

    def body(f_ref, b_ref, o_ref, carry):
        @pl.when(pl.program_id(0) == 0)
        def _():
            carry[...] = jnp.zeros_like(carry)

        z = f_ref[...] + b_ref[...]
        lf = jnp.minimum(z, 0.0) - jnp.log(1.0 + jnp.exp(-jnp.abs(z)))
        tri = _tri(tb, True)
        cs = sum(jnp.dot(t, tri, preferred_element_type=F32) for t in _split3(lf)) + carry[...]
        for n, term in enumerate(_split3(cs * out_scale)):
            o_ref[n] = term
        carry[...] += jnp.sum(lf, axis=-1, keepdims=True)

    return pl.pallas_call(
        body, name=name, grid=(s // tb,),
        in_specs=[pl.BlockSpec((nh, tb), lambda t: (0, t)), pl.BlockSpec((nh, 1), lambda t: (0, 0))],
        out_specs=pl.BlockSpec((3, nh, tb), lambda t: (0, 0, t)),
        out_shape=jax.ShapeDtypeStruct((3, nh, s), BF16),
        scratch_shapes=[pltpu.VMEM((nh, 1), F32)], compiler_params=_cp("arbitrary"))(ft, bf)


def _fox_gate_bwd(drow, dcol, ft, bf, *, inv_scale, name, tb=512):
    nh, s = ft.shape
    nb = s // tb

    def body(dr_ref, dc_ref, f_ref, b_ref, df_ref, db_ref, carry):
        @pl.when(pl.program_id(0) == 0)
        def _():
            carry[...] = jnp.zeros_like(carry)
            db_ref[...] = jnp.zeros_like(db_ref)

        dc = (dr_ref[...] - dc_ref[...]) * inv_scale
        tri = _tri(tb, False)
        suf = sum(jnp.dot(t, tri, preferred_element_type=F32) for t in _split3(dc)) + carry[...]
        z = f_ref[...] + b_ref[...]
        dz = suf * (1.0 / (1.0 + jnp.exp(z)))
        df_ref[...] = dz
        db_ref[...] += jnp.sum(dz, axis=-1, keepdims=True)
        carry[...] += jnp.sum(dc, axis=-1, keepdims=True)

    rev = pl.BlockSpec((nh, tb), lambda t: (0, nb - 1 - t))
    one = pl.BlockSpec((nh, 1), lambda t: (0, 0))
    return pl.pallas_call(
        body, name=name, grid=(nb,), in_specs=[rev, rev, rev, one], out_specs=[rev, one],
        out_shape=[jax.ShapeDtypeStruct((nh, s), F32), jax.ShapeDtypeStruct((nh, 1), F32)],
        scratch_shapes=[pltpu.VMEM((nh, 1), F32)], compiler_params=_cp("arbitrary"))(drow, dcol, ft, bf)


def _tri_fwd(t, nq):
    i = sum((t >= (r * (r + 1)) // 2).astype(jnp.int32) for r in range(1, nq))
    return i, t - (i * (i + 1)) // 2


def _tri_bwd(t, nq):
    j = sum((t >= r * nq - (r * (r - 1)) // 2).astype(jnp.int32) for r in range(1, nq))
    return j, j + t - (j * nq - (j * (j - 1)) // 2)


def _scores_t(k, q, *, scale, diag):
    s = lax.dot_general(k, q, NT, preferred_element_type=F32) * scale
    if diag:
        r = lax.broadcasted_iota(jnp.int32, s.shape, 0)
        c = lax.broadcasted_iota(jnp.int32, s.shape, 1)
        s = jnp.where(r <= c, s, MASK_VALUE)
    return s


def _causal_fwd_t(q, k, vt, *, scale, name, tq, hb=2, rider=None):
    nh, s, dq = q.shape
    dv = vt.shape[1]
    nq = s // tq
    nsteps = (nq * (nq + 1)) // 2

    def body(q_ref, k_ref, vt_ref, o_ref, lse_ref, m_sc, l_sc, acc_sc):
        i, j = _tri_fwd(pl.program_id(1), nq)

        @pl.when(j == 0)
        def _():
            m_sc[...] = jnp.full_like(m_sc, MASK_VALUE)
            l_sc[...] = jnp.zeros_like(l_sc)
            acc_sc[...] = jnp.zeros_like(acc_sc)

        def step(diag):
            for u in range(hb):
                sc = _scores_t(k_ref[u], q_ref[u], scale=scale, diag=diag)
                m_prev = m_sc[u]
                m_new = jnp.maximum(m_prev, jnp.max(sc, axis=0, keepdims=True))
                alpha = jnp.exp(m_prev - m_new)
                pr = jnp.exp(sc - m_new)
                l_new = alpha * l_sc[u] + jnp.sum(pr, axis=0, keepdims=True)
                acc = alpha * acc_sc[u] + jnp.dot(vt_ref[u], pr.astype(BF16), preferred_element_type=F32)
                if diag:
                    o_ref[u] = (acc / l_new).astype(BF16)
                    lse_ref[u] = m_new + jnp.log(l_new)
                else:
                    m_sc[u], l_sc[u], acc_sc[u] = m_new, l_new, acc

        pl.when(j < i)(functools.partial(step, False))
        pl.when(j == i)(functools.partial(step, True))

    def qi(t):
        return _tri_fwd(t, nq)[0]

    def kj(t):
        return _tri_fwd(t, nq)[1]

    return _call_with_rider(
        body, rider, name=name, grid=(nh // hb, nsteps),
        in_specs=[pl.BlockSpec((hb, tq, dq), lambda hp, t: (hp, qi(t), 0)),
                  pl.BlockSpec((hb, tq, dq), lambda hp, t: (hp, kj(t), 0)),
                  pl.BlockSpec((hb, dv, tq), lambda hp, t: (hp, 0, kj(t)))],
        out_specs=[pl.BlockSpec((hb, dv, tq), lambda hp, t: (hp, 0, qi(t))),
                   pl.BlockSpec((hb, 1, tq), lambda hp, t: (hp, 0, qi(t)))],
        out_shape=[jax.ShapeDtypeStruct((nh, dv, s), BF16), jax.ShapeDtypeStruct((nh, 1, s), F32)],
        scratch_shapes=[pltpu.VMEM((hb, 1, tq), F32), pltpu.VMEM((hb, 1, tq), F32), pltpu.VMEM((hb, dv, tq), F32)],
        compiler_params=_cp("arbitrary", "arbitrary"), args=(q, k, vt))


def _causal_bwd_t(q, k, v, ot, dot_, lse, *, scale, name, tq, hb=2, rider=None):
    nh, s, dq = q.shape
    dv = v.shape[-1]
    nq = s // tq
    nsteps = (nq * (nq + 1)) // 2

    def body(q_ref, k_ref, v_ref, ot_ref, dot_ref, lse_ref, dq_ref, dk_ref, dvt_ref):
        t = pl.program_id(1)
        j, i = _tri_bwd(t, nq)

        @pl.when(t == 0)
        def _():
            dq_ref[...] = jnp.zeros_like(dq_ref)

        def step(diag):
            rows = pl.ds(pl.multiple_of(i * tq, tq), tq)
            for u in range(hb):
                qv, kv, dov = q_ref[u], k_ref[u], dot_ref[u]
                pr = jnp.exp(_scores_t(kv, qv, scale=scale, diag=diag) - lse_ref[u])
                dp = jnp.dot(v_ref[u], dov, preferred_element_type=F32)
                delta = jnp.sum(dov.astype(F32) * ot_ref[u].astype(F32), axis=0, keepdims=True)
                dsb = ((pr * (dp - delta)) * scale).astype(BF16)
                d_v = lax.dot_general(dov, pr.astype(BF16), NT, preferred_element_type=F32)
                d_k = jnp.dot(dsb, qv, preferred_element_type=F32)
                if diag:
                    dvt_ref[u], dk_ref[u] = d_v, d_k
                else:
                    dvt_ref[u] += d_v
                    dk_ref[u] += d_k
                dq_ref[u, rows, :] += lax.dot_general(dsb, kv, TN, preferred_element_type=F32)

        pl.when(i > j)(functools.partial(step, False))
        pl.when(i == j)(functools.partial(step, True))

    def qi(t):
        return _tri_bwd(t, nq)[1]

    def kj(t):
        return _tri_bwd(t, nq)[0]

    rows_q = pl.BlockSpec((hb, tq, dq), lambda hp, t: (hp, qi(t), 0))
    rows_k = pl.BlockSpec((hb, tq, dq), lambda hp, t: (hp, kj(t), 0))
    lanes_q = pl.BlockSpec((hb, dv, tq), lambda hp, t: (hp, 0, qi(t)))
    return _call_with_rider(
        body, rider, name=name, grid=(nh // hb, nsteps),
        in_specs=[rows_q, rows_k, pl.BlockSpec((hb, tq, dv), lambda hp, t: (hp, kj(t), 0)), lanes_q, lanes_q,
                  pl.BlockSpec((hb, 1, tq), lambda hp, t: (hp, 0, qi(t)))],
        out_specs=[pl.BlockSpec((hb, s, dq), lambda hp, t: (hp, 0, 0)), rows_k,
                   pl.BlockSpec((hb, dv, tq), lambda hp, t: (hp, 0, kj(t)))],
        out_shape=[jax.ShapeDtypeStruct((nh, s, dq), F32), jax.ShapeDtypeStruct((nh, s, dq), F32),
                   jax.ShapeDtypeStruct((nh, dv, s), F32)],
        scratch_shapes=[], compiler_params=_cp("arbitrary", "arbitrary"), args=(q, k, v, ot, dot_, lse))


def _swa_scores_t(k, q, dist, ok, *, scale, slope):
    s = lax.dot_general(k, q, NT, preferred_element_type=F32) * scale - slope * dist.astype(F32)
    return jnp.where(ok, s, MASK_VALUE)


def _swa_geometry(tb, w, has_other):
    r = lax.broadcasted_iota(jnp.int32, (tb, tb), 0)
    c = lax.broadcasted_iota(jnp.int32, (tb, tb), 1)
    d_same = c - r
    ok_same = jnp.logical_and(d_same >= 0, d_same < w)

    def other(ncols):
        rr = lax.broadcasted_iota(jnp.int32, (w, ncols), 0)
        cc = lax.broadcasted_iota(jnp.int32, (w, ncols), 1)
        dd = cc + w - rr
        return dd, jnp.logical_and(dd < w, has_other)

    return (d_same, ok_same), other


def _swa_fwd_t(q, k, vt, slopes_sinks, *, scale, window, name, tb=256):
    nh, s, d = q.shape
    nkv = k.shape[0]
    grp = nh // nkv
    w = window
    per = tb // w
    assert tb % w == 0

    def body(q_ref, kc_ref, kp_ref, vc_ref, vp_ref, ss_ref, o_ref, lse_ref):
        kvh, i = pl.program_id(0), pl.program_id(1)
        (d_c, ok_c), other = _swa_geometry(tb, w, i > 0)
        d_p, ok_p = other(tb)
        for g in range(grp):
            h = kvh * grp + g
            slope, sink = ss_ref[0, h], ss_ref[1, h]
            qg = q_ref[g]
            s_c = _swa_scores_t(kc_ref[...], qg, d_c, ok_c, scale=scale, slope=slope)
            s_p = _swa_scores_t(kp_ref[...], qg, d_p, ok_p, scale=scale, slope=slope)
            m = jnp.maximum(jnp.maximum(jnp.max(s_c, axis=0, keepdims=True), jnp.max(s_p, axis=0, keepdims=True)), sink)
            p_c, p_p = jnp.exp(s_c - m), jnp.exp(s_p - m)
            l = jnp.sum(p_c, axis=0, keepdims=True) + jnp.sum(p_p, axis=0, keepdims=True) + jnp.exp(sink - m)
            acc = (jnp.dot(vc_ref[...], p_c.astype(BF16), preferred_element_type=F32)
                   + jnp.dot(vp_ref[...], p_p.astype(BF16), preferred_element_type=F32))
            o_ref[g] = (acc / l).astype(BF16)
            lse_ref[g] = m + jnp.log(l)

    def prev(i):
        return jnp.maximum(i * per - 1, 0)

    return pl.pallas_call(
        body, name=name, grid=(nkv, s // tb),
        in_specs=[pl.BlockSpec((grp, tb, d), lambda kh, i: (kh, i, 0)),
                  pl.BlockSpec((None, tb, d), lambda kh, i: (kh, i, 0)),
                  pl.BlockSpec((None, w, d), lambda kh, i: (kh, prev(i), 0)),
                  pl.BlockSpec((None, d, tb), lambda kh, i: (kh, 0, i)),
                  pl.BlockSpec((None, d, w), lambda kh, i: (kh, 0, prev(i))),
                  pl.BlockSpec(memory_space=pltpu.SMEM)],
        out_specs=[pl.BlockSpec((grp, d, tb), lambda kh, i: (kh, 0, i)), pl.BlockSpec((grp, 1, tb), lambda kh, i: (kh, 0, i))],
        out_shape=[jax.ShapeDtypeStruct((nh, d, s), BF16), jax.ShapeDtypeStruct((nh, 1, s), F32)],
        compiler_params=_cp("arbitrary", "arbitrary"))(q, k, k, vt, vt, slopes_sinks)


def _swa_bwd_t(q, k, v, ot, dot_, lse, slopes_sinks, *, scale, window, name, tb=256, rider=None):
    nh, s, d = q.shape
    nkv = k.shape[0]
    grp = nh // nkv
    w = window
    per = tb // w
    nb = s // tb

    def body(qc_ref, qn_ref, kc_ref, kp_ref, vc_ref, vp_ref, oc_ref, on_ref, doc_ref, don_ref, lc_ref, ln_ref, ss_ref,
             dq_ref, dk_ref, dvt_ref, dsink_ref):
        kvh, i = pl.program_id(0), pl.program_id(1)

        @pl.when(i == 0)
        def _():
            dsink_ref[...] = jnp.zeros_like(dsink_ref)

        (d_c, ok_c), other = _swa_geometry(tb, w, i > 0)
        d_p, ok_p = other(tb)
        d_n, ok_n = _swa_geometry(tb, w, i < nb - 1)[1](w)
        kc, kp, vc, vp = kc_ref[...], kp_ref[...], vc_ref[...], vp_ref[...]
        k_last, v_last = kc[tb - w:, :], vc[tb - w:, :]
        dk_acc = jnp.zeros((tb, d), F32)
        dv_acc = jnp.zeros((d, tb), F32)
        dk_tail = jnp.zeros((w, d), F32)
        dv_tail = jnp.zeros((d, w), F32)
        for g in range(grp):
            h = kvh * grp + g
            slope, sink = ss_ref[0, h], ss_ref[1, h]
            qg, dog, lse_c = qc_ref[g], doc_ref[g], lc_ref[g]
            delta = jnp.sum(dog.astype(F32) * oc_ref[g].astype(F32), axis=0, keepdims=True)
            p_c = jnp.exp(_swa_scores_t(kc, qg, d_c, ok_c, scale=scale, slope=slope) - lse_c)
            p_p = jnp.exp(_swa_scores_t(kp, qg, d_p, ok_p, scale=scale, slope=slope) - lse_c)
            ds_c = ((p_c * (jnp.dot(vc, dog, preferred_element_type=F32) - delta)) * scale).astype(BF16)
            ds_p = ((p_p * (jnp.dot(vp, dog, preferred_element_type=F32) - delta)) * scale).astype(BF16)
            dq_ref[g] = (lax.dot_general(ds_c, kc, TN, preferred_element_type=F32)
                         + lax.dot_general(ds_p, kp, TN, preferred_element_type=F32))
            dk_acc += jnp.dot(ds_c, qg, preferred_element_type=F32)
            dv_acc += lax.dot_general(dog, p_c.astype(BF16), NT, preferred_element_type=F32)
            dsink_ref[g] -= jnp.broadcast_to(jnp.sum(jnp.exp(sink - lse_c) * delta, axis=1, keepdims=True), (1, LANES))
            qn, don = qn_ref[g], don_ref[g]
            delta_n = jnp.sum(don.astype(F32) * on_ref[g].astype(F32), axis=0, keepdims=True)
            p_n = jnp.exp(_swa_scores_t(k_last, qn, d_n, ok_n, scale=scale, slope=slope) - ln_ref[g])
            ds_n = ((p_n * (jnp.dot(v_last, don, preferred_element_type=F32) - delta_n)) * scale).astype(BF16)
            dk_tail += jnp.dot(ds_n, qn, preferred_element_type=F32)
            dv_tail += lax.dot_general(don, p_n.astype(BF16), NT, preferred_element_type=F32)
        dk_ref[...] = dk_acc
        dvt_ref[...] = dv_acc
        dk_ref[tb - w:, :] += dk_tail
        dvt_ref[:, tb - w:] += dv_tail

    def prev(i):
        return jnp.maximum(i * per - 1, 0)

    def nxt(i):
        return jnp.minimum((i + 1) * per, s // w - 1)

    return _call_with_rider(
        body, rider, name=name, grid=(nkv, nb), scratch_shapes=[],
        args=(q, q, k, k, v, v, ot, ot, dot_, dot_, lse, lse, slopes_sinks),
        in_specs=[pl.BlockSpec((grp, tb, d), lambda kh, i: (kh, i, 0)),
                  pl.BlockSpec((grp, w, d), lambda kh, i: (kh, nxt(i), 0)),
                  pl.BlockSpec((None, tb, d), lambda kh, i: (kh, i, 0)),
                  pl.BlockSpec((None, w, d), lambda kh, i: (kh, prev(i), 0)),
                  pl.BlockSpec((None, tb, d), lambda kh, i: (kh, i, 0)),
                  pl.BlockSpec((None, w, d), lambda kh, i: (kh, prev(i), 0)),
                  pl.BlockSpec((grp, d, tb), lambda kh, i: (kh, 0, i)),
                  pl.BlockSpec((grp, d, w), lambda kh, i: (kh, 0, nxt(i))),
                  pl.BlockSpec((grp, d, tb), lambda kh, i: (kh, 0, i)),
                  pl.BlockSpec((grp, d, w), lambda kh, i: (kh, 0, nxt(i))),
                  pl.BlockSpec((grp, 1, tb), lambda kh, i: (kh, 0, i)),
                  pl.BlockSpec((grp, 1, w), lambda kh, i: (kh, 0, nxt(i))),
                  pl.BlockSpec(memory_space=pltpu.SMEM)],
        out_specs=[pl.BlockSpec((grp, tb, d), lambda kh, i: (kh, i, 0)),
                   pl.BlockSpec((None, tb, d), lambda kh, i: (kh, i, 0)),
                   pl.BlockSpec((None, d, tb), lambda kh, i: (kh, 0, i)),
                   pl.BlockSpec((None, grp, 1, LANES), lambda kh, i: (kh, 0, 0, 0))],
        out_shape=[jax.ShapeDtypeStruct((nh, s, d), F32), jax.ShapeDtypeStruct((nkv, s, d), F32),
                   jax.ShapeDtypeStruct((nkv, d, s), F32), jax.ShapeDtypeStruct((nkv, grp, 1, LANES), F32)],
        compiler_params=_cp("arbitrary", "arbitrary"))


def _adamw(w, g, m, v, *, name):
    shape = w.shape
    cols = shape[-1]
    rows = int(np.prod(shape[:-1])) if len(shape) > 1 else 1
    tr = _row_tile(rows, cols)
    c1 = 1.0 - ADAM_B1 ** ADAM_STEP
    c2 = 1.0 - ADAM_B2 ** ADAM_STEP

    def body(w_ref, g_ref, m_ref, v_ref, d_ref, mo_ref, vo_ref):
        gv = g_ref[...]
        mn = ADAM_B1 * m_ref[...] + (1.0 - ADAM_B1) * gv
        vn = ADAM_B2 * v_ref[...] + (1.0 - ADAM_B2) * (gv * gv)
        mo_ref[...] = mn
        vo_ref[...] = vn
        d_ref[...] = -ADAM_LR * ((mn / c1) / (jnp.sqrt(vn / c2) + ADAM_EPS) + ADAM_WD * w_ref[...])

    blk = pl.BlockSpec((tr, cols), lambda i: (i, 0))
    outs = pl.pallas_call(
        body, name=name, grid=(rows // tr,), in_specs=[blk] * 4, out_specs=[blk] * 3,
        out_shape=[jax.ShapeDtypeStruct((rows, cols), F32)] * 3,
        compiler_params=_cp("arbitrary"))(*[a.reshape(rows, cols) for a in (w, g, m, v)])
    return tuple(a.reshape(shape) for a in outs)


def _hbm_spec():
    return pl.BlockSpec(memory_space=pl.ANY)


def _mesh_place():
    x, y, c = lax.axis_index("x"), lax.axis_index("y"), lax.axis_index("c")
    return x, y, c, [(1 - x, y), (x, 1 - y), (1 - x, 1 - y)]


def _half_rows(c, rows, align):
    return pl.ds(pl.multiple_of(c * (rows // 2), align), rows // 2)


def _part(ref, mode, k, n, rows=None):
    if mode == "cols":
        cols = pl.ds(pl.multiple_of(k * n, LANES), n)
        return ref.at[:, cols] if rows is None else ref.at[rows, cols]
    return ref.at[k] if rows is None else ref.at[k, rows, :]


class _Rider:
    def __init__(self, inputs, out_shape, n_sems, start, finish):
        self.inputs, self.out_shape, self.n_sems, self.start, self.finish = inputs, out_shape, n_sems, start, finish


def _call_with_rider(body, rider, *, name, grid, in_specs, out_specs, out_shape, scratch_shapes, compiler_params, args):
    if rider is None:
        outs = pl.pallas_call(body, name=name, grid=grid, in_specs=in_specs, out_specs=out_specs, out_shape=out_shape,
                              scratch_shapes=scratch_shapes, compiler_params=compiler_params)(*args)
        return outs, []
    n_in, n_out, n_sc = len(in_specs), len(out_specs), len(scratch_shapes)
    n_rin, n_rout = len(rider.inputs), len(rider.out_shape)

    def wrapped(*refs):
        pos = 0
        groups = []
        for n in (n_in, n_rin, n_out, n_rout, n_sc, 2):
            groups.append(refs[pos:pos + n])
            pos += n
        ins, rins, outs, routs, scratch, sems = groups
        ids = [pl.program_id(a) for a in range(len(grid))]
        first = functools.reduce(jnp.logical_and, [i == 0 for i in ids])
        last = functools.reduce(jnp.logical_and, [i == g - 1 for i, g in zip(ids, grid)])
        pl.when(first)(lambda: rider.start(rins, routs, *sems))
        body(*ins, *outs, *scratch)
        pl.when(last)(lambda: rider.finish(rins, routs, *sems))

    outs = pl.pallas_call(
        wrapped, name=name, grid=grid, in_specs=list(in_specs) + [_hbm_spec()] * n_rin,
        out_specs=list(out_specs) + [_hbm_spec()] * n_rout, out_shape=list(out_shape) + list(rider.out_shape),
        scratch_shapes=list(scratch_shapes) + [pltpu.SemaphoreType.DMA((rider.n_sems,))] * 2,
        compiler_params=compiler_params)(*args, *rider.inputs)
    return outs[:n_out], outs[n_out:]


def _run_rider(rider, *, name):
    n_rin = len(rider.inputs)

    def body(*refs):
        rins, routs, sems = refs[:n_rin], refs[n_rin:-2], refs[-2:]
        rider.start(rins, routs, *sems)
        rider.finish(rins, routs, *sems)

    return pl.pallas_call(
        body, name=name, in_specs=[_hbm_spec()] * n_rin, out_specs=[_hbm_spec()] * len(rider.out_shape),
        out_shape=rider.out_shape, scratch_shapes=[pltpu.SemaphoreType.DMA((rider.n_sems,))] * 2)(*rider.inputs)


def _gather_rider(shards, modes):
    n_arr = len(shards)
    out_shape = [jax.ShapeDtypeStruct((s.shape[0], N_CHIPS * s.shape[1]) if m == "cols" else (N_CHIPS,) + s.shape, s.dtype)
                 for s, m in zip(shards, modes)]
    per = 4

    def copies(srcs, dsts, send_sems, recv_sems):
        x, y, c, chips = _mesh_place()
        me = 2 * x + y
        sends, waits = [], []
        for i in range(n_arr):
            r, n = shards[i].shape
            rows = _half_rows(c, r, 16)

            def copy(slot, src, dst, to, i=i):
                return pltpu.make_async_remote_copy(src_ref=src, dst_ref=dst, send_sem=send_sems.at[i * per + slot],
                                                    recv_sem=recv_sems.at[i * per + slot], device_id=to, device_id_type=MESH)

            own = _part(dsts[i], modes[i], me, n)
            sends.append(copy(0, srcs[i], own, (x, y, 1 - c)))
            waits.append(copy(0, own, own, (x, y, 1 - c)))
            for j, (px, py) in enumerate(chips):
                sends.append(copy(1 + j, srcs[i].at[rows], _part(dsts[i], modes[i], me, n, rows), (px, py, c)))
                theirs = _part(dsts[i], modes[i], 2 * px + py, n, rows)
                waits.append(copy(1 + j, theirs, theirs, (px, py, c)))
        return sends, waits

    def start(*refs):
        for cp in copies(*refs)[0]:
            cp.start()

    def finish(*refs):
        sends, waits = copies(*refs)
        for cp in waits:
            cp.wait_recv()
        for cp in sends:
            cp.wait_send()

    return _Rider(list(shards), out_shape, per * n_arr, start, finish)


def _gather_forward(dsts, shard_shapes, modes, *, name):
    n_arr = len(dsts)

    def body(*refs):
        outs = refs[n_arr:2 * n_arr]
        send_sems, recv_sems = refs[2 * n_arr:]
        x, y, c, chips = _mesh_place()
        cps = []
        for i in range(n_arr):
            r, n = shard_shapes[i]
            for j, (px, py) in enumerate(chips):
                def view(hc, i=i, px=px, py=py, r=r, n=n):
                    return _part(outs[i], modes[i], 2 * px + py, n, _half_rows(hc, r, 16))

                def copy(ref, i=i, j=j):
                    return pltpu.make_async_remote_copy(src_ref=ref, dst_ref=ref, send_sem=send_sems.at[3 * i + j],
                                                        recv_sem=recv_sems.at[3 * i + j], device_id=(x, y, 1 - c), device_id_type=MESH)

                cps.append((copy(view(c)), copy(view(1 - c))))
        for send, _ in cps:
            send.start()
        for send, theirs in cps:
            theirs.wait_recv()
            send.wait_send()

    return pl.pallas_call(
        body, name=name, in_specs=[_hbm_spec()] * n_arr, out_specs=[_hbm_spec()] * n_arr,
        out_shape=[jax.ShapeDtypeStruct(d.shape, d.dtype) for d in dsts],
        input_output_aliases={i: i for i in range(n_arr)},
        scratch_shapes=[pltpu.SemaphoreType.DMA((3 * n_arr,)), pltpu.SemaphoreType.DMA((3 * n_arr,))])(*dsts)


def _blk_view(a, mode):
    return a[None] if mode == "cols" else a


def _swap_rider(arrs, modes):
    n_arr = len(arrs)
    out_shape = [jax.ShapeDtypeStruct((a.shape[0] // 2, a.shape[1]) if m == "cols" else (a.shape[0], a.shape[1] // 2, a.shape[2]), a.dtype)
                 for a, m in zip(arrs, modes)]

    def copies(srcs, dsts, send_sems, recv_sems):
        x, y, c, _ = _mesh_place()
        cps = []
        for i in range(n_arr):
            if modes[i] == "cols":
                src = srcs[i].at[_half_rows(1 - c, arrs[i].shape[0], 8)]
            else:
                src = srcs[i].at[:, _half_rows(1 - c, arrs[i].shape[1], 8), :]
            cps.append(pltpu.make_async_remote_copy(src_ref=src, dst_ref=dsts[i], send_sem=send_sems.at[i],
                                                    recv_sem=recv_sems.at[i], device_id=(x, y, 1 - c), device_id_type=MESH))
        return cps

    def start(*refs):
        for cp in copies(*refs):
            cp.start()

    def finish(*refs):
        for cp in copies(*refs):
            cp.wait()

    return _Rider(list(arrs), out_shape, n_arr, start, finish)


def _rs_pair_add(arr, landed, place, *, name):
    nb, r, c = arr.shape
    rh = r // 2
    tr = _row_tile(rh, c)
    nt = rh // tr

    def body(p_ref, a_ref, l_ref, o_ref):
        o_ref[...] = (a_ref[...] + l_ref[...]).astype(BF16)

    grid_spec = pltpu.PrefetchScalarGridSpec(
        num_scalar_prefetch=1, grid=(nb, nt),
        in_specs=[pl.BlockSpec((None, tr, c), lambda b, t, p_ref: (b, p_ref[1] * nt + t, 0)),
                  pl.BlockSpec((None, tr, c), lambda b, t, p_ref: (b, t, 0))],
        out_specs=pl.BlockSpec((None, tr, c), lambda b, t, p_ref: (b, t, 0)))
    return pl.pallas_call(
        body, name=name, grid_spec=grid_spec, out_shape=jax.ShapeDtypeStruct((nb, rh, c), BF16),
        compiler_params=_cp("arbitrary", "arbitrary"))(place, arr, landed)


def _exchange_rider(parts, modes):
    n_arr = len(parts)
    out_shape = []
    for a, m in zip(parts, modes):
        shp = (a.shape[0], a.shape[1] // N_CHIPS) if m == "cols" else a.shape[1:]
        out_shape.append(jax.ShapeDtypeStruct((3,) + shp, a.dtype))

    def copies(srcs, dsts, send_sems, recv_sems):
        x, y, c, chips = _mesh_place()
        cps = []
        for i in range(n_arr):
            n = out_shape[i].shape[-1]
            for j, (px, py) in enumerate(chips):
                cps.append(pltpu.make_async_remote_copy(
                    src_ref=_part(srcs[i], modes[i], 2 * px + py, n), dst_ref=dsts[i].at[j],
                    send_sem=send_sems.at[3 * i + j], recv_sem=recv_sems.at[3 * i + j],
                    device_id=(px, py, c), device_id_type=MESH))
        return cps

    def start(*refs):
        for cp in copies(*refs):
            cp.start()

    def finish(*refs):
        for cp in copies(*refs):
            cp.wait()

    return _Rider(list(parts), out_shape, 3 * n_arr, start, finish)


def _rs_chip_sum(part, landed, mode, place, *, name):
    _, rh, n = landed.shape
    tr = _row_tile(rh, n)
    nt = rh // tr

    def body(p_ref, a_ref, l_ref, o_ref):
        o_ref[...] = ((a_ref[...].astype(F32) + l_ref[0].astype(F32)) + l_ref[1].astype(F32)) + l_ref[2].astype(F32)

    if mode == "cols":
        own = pl.BlockSpec((tr, n), lambda t, p_ref: (t, p_ref[0]))
    else:
        own = pl.BlockSpec((None, tr, n), lambda t, p_ref: (p_ref[0], t, 0))
    grid_spec = pltpu.PrefetchScalarGridSpec(
        num_scalar_prefetch=1, grid=(nt,),
        in_specs=[own, pl.BlockSpec((3, tr, n), lambda t, p_ref: (0, t, 0))],
        out_specs=pl.BlockSpec((tr, n), lambda t, p_ref: (p_ref[1] * nt + t, 0)))
    return pl.pallas_call(
        body, name=name, grid_spec=grid_spec, out_shape=jax.ShapeDtypeStruct((2 * rh, n), F32),
        compiler_params=_cp("arbitrary"))(place, part, landed)


def _rs_pair_join(halves, *, name):
    n_arr = len(halves)

    def body(*refs):
        outs = refs[n_arr:2 * n_arr]
        send_sems, recv_sems = refs[2 * n_arr:]
        x, y, c, _ = _mesh_place()
        cps = []
        for i in range(n_arr):
            rows = _half_rows(c, halves[i].shape[0], 8)
            cps.append(pltpu.make_async_remote_copy(src_ref=outs[i].at[rows], dst_ref=outs[i].at[rows], send_sem=send_sems.at[i],
                                                    recv_sem=recv_sems.at[i], device_id=(x, y, 1 - c), device_id_type=MESH))
        for cp in cps:
            cp.start()
        for i, cp in enumerate(cps):
            cp.wait_send()
            theirs = outs[i].at[_half_rows(1 - c, halves[i].shape[0], 8)]
            pltpu.make_async_remote_copy(src_ref=theirs, dst_ref=theirs, send_sem=send_sems.at[i], recv_sem=recv_sems.at[i],
                                         device_id=(x, y, 1 - c), device_id_type=MESH).wait_recv()

    return pl.pallas_call(
        body, name=name, in_specs=[_hbm_spec()] * n_arr, out_specs=[_hbm_spec()] * n_arr,
        out_shape=[jax.ShapeDtypeStruct(h.shape, h.dtype) for h in halves],
        input_output_aliases={i: i for i in range(n_arr)},
        scratch_shapes=[pltpu.SemaphoreType.DMA((n_arr,)), pltpu.SemaphoreType.DMA((n_arr,))])(*halves)


def _allreduce_small(v, *, name):
    r, c = v.shape

    def body(v_ref, o_ref, gath, send_sems, recv_sems):
        x, y, cc, _ = _mesh_place()
        me = 4 * x + 2 * y + cc
        gath[me] = v_ref[...]
        cps = []
        for rel in range(1, 8):
            px = 1 - x if rel & 4 else x
            py = 1 - y if rel & 2 else y
            pc = 1 - cc if rel & 1 else cc

            def copy(slot, px=px, py=py, pc=pc, rel=rel):
                return pltpu.make_async_remote_copy(
                    src_ref=v_ref, dst_ref=gath.at[slot], send_sem=send_sems.at[rel - 1],
                    recv_sem=recv_sems.at[rel - 1], device_id=(px, py, pc), device_id_type=MESH)

            cps.append((copy(me), copy(4 * px + 2 * py + pc)))
        for send, _ in cps:
            send.start()
        for send, theirs in cps:
            theirs.wait_recv()
            send.wait_send()
        tot = gath[0]
        for d in range(1, 8):
            tot = tot + gath[d]
        o_ref[...] = tot

    vm = pl.BlockSpec(memory_space=pltpu.VMEM)
    return pl.pallas_call(
        body, name=name, in_specs=[vm], out_specs=vm, out_shape=jax.ShapeDtypeStruct((r, c), F32),
        scratch_shapes=[pltpu.VMEM((8, r, c), F32), pltpu.SemaphoreType.DMA((7,)), pltpu.SemaphoreType.DMA((7,))])(v)


def _rope_tables(s, reps):
    half = B_ROPE // 2
    inv = ROPE_THETA ** (-jnp.arange(0, B_ROPE, 2, dtype=F32) / B_ROPE)
    ang = jnp.arange(s, dtype=F32)[:, None] * inv[None, :]
    return jnp.tile(jnp.cos(ang), (1, reps)), jnp.tile(jnp.sin(ang), (1, reps))


def _alibi_slopes():
    return 2.0 ** (-8.0 * jnp.arange(1, A_HEADS + 1, dtype=F32) / A_HEADS)


def _ffn_fwd(h, norm, wts, tag, rider=None, on_rode=None):
    (dact_dgate, dact_dup, act, xn), rode = _ffn_up(h, norm, wts["wgu"], name=f"{tag}_up", rider=rider)
    if on_rode is not None:
        on_rode(rode)
    out = _mm_res_fwd(act, wts["wd"], h, scale=FFN_RES_SCALE, name=f"{tag}_down")
    return out, dict(h_in=h, dact_dgate=dact_dgate, dact_dup=dact_dup, act=act, xn=xn), rode


def _ffn_bwd(dh, norm, wts, sv, tag, rider=None, own=None):
    (dgate, dup), rode = _ffn_down_bwd(dh, wts["wd"], sv["dact_dgate"], sv["dact_dup"], scale=FFN_RES_SCALE,
                                      name=f"{tag}_down_bwd", rider=rider)
    d_wd = _mm_tn(sv["act"], dh, b_scale=FFN_RES_SCALE, name=f"{tag}_dwd")
    pairs = [(dgate, wts["wgu"], 0), (dup, wts["wgu"], 1)]
    if own is None:
        d_wgu = _mm_tn(sv["xn"], [dgate, dup], name=f"{tag}_dwgu")
        dh_in, dnorm = _mm_nt_rmsbwd(pairs, sv["h_in"], norm, dh, name=f"{tag}_dx")
    else:
        wd_ready, wgu_ready, done = own
        d_wgu, brought = _mm_tn(sv["xn"], [dgate, dup], name=f"{tag}_dwgu", rider=wd_ready(d_wd))
        dh_in, dnorm, brought = _mm_nt_rmsbwd(pairs, sv["h_in"], norm, dh, name=f"{tag}_dx", rider=wgu_ready(brought, d_wgu))
        done(brought)
    return dh_in, dnorm, d_wgu, d_wd, rode


def _even_weights(w_in, w_uq, w_ukv):
    half = B_ROPE // 2
    base = w_in.shape[1]
    kr1, kr2 = w_in[:, base - B_ROPE:base - half], w_in[:, base - half:]
    w_in_cat = jnp.concatenate([w_in, -kr2, kr1, jnp.zeros((w_in.shape[0], 64), w_in.dtype)], axis=1)
    u3 = w_uq.reshape(w_uq.shape[0], B_HEADS, B_NOPE + B_ROPE)
    nope = u3[:, :, :B_NOPE].reshape(w_uq.shape[0], -1)
    rot = u3[:, :, B_NOPE:].reshape(w_uq.shape[0], -1)
    swapped = jnp.concatenate([-u3[:, :, B_NOPE + half:], u3[:, :, B_NOPE:B_NOPE + half]], axis=-1).reshape(w_uq.shape[0], -1)
    return w_in_cat, jnp.concatenate([nope, rot, swapped], axis=1), w_ukv


def _even_fwd(h, w, i, rider=None):
    s = h.shape[0]
    qa, ka, va, vat, c_q, c_kv, kr_blk, xn = _ev_in_fwd(h, w["mix_norm"][i:i + 1], w["ev_in_cat"], name="ev_in")
    cos32, sin32 = _rope_tables(s, 2)
    kro = _rope_fwd(kr_blk[:, :B_ROPE], kr_blk[:, B_ROPE:2 * B_ROPE], cos32, sin32, name="ev_k_rope")
    ss = jnp.stack([_alibi_slopes(), w["ev_sinks"].reshape(-1)])
    oa, lse_a = _swa_fwd_t(qa, ka, vat, ss, scale=A_HEAD_DIM ** -0.5, window=WINDOW, name="swa_fwd")
    cos256, sin256 = _rope_tables(s, 2 * B_HEADS)
    qb, xn_q = _ev_q_fwd(c_q, w["ev_cq_norm"], w["ev_q_cat"], cos256, sin256, name="ev_q_up")
    kb, vb, vbt, xn_kv = _ev_kv_fwd(c_kv, w["ev_ckv_norm"], w["ev_ukv"], kro, name="ev_kv_up")
    (ob, lse_b), rode = _causal_fwd_t(qb, kb, vbt, scale=(B_NOPE + B_ROPE) ** -0.5, name="mla_fwd", tq=512, hb=4, rider=rider)
    attn = jnp.concatenate([oa.reshape(-1, s), ob.reshape(-1, s)], axis=0)
    out = _mm_res_fwd(attn, w["ev_out"], h, scale=1.0, name="ev_out", a_t=True)
    sv = dict(h_in=h, xn=xn, c_q=c_q, c_kv=c_kv, xn_q=xn_q, xn_kv=xn_kv, qa=qa, ka=ka, va=va, oa=oa, lse_a=lse_a,
              ss=ss, qb=qb, kb=kb, vb=vb, ob=ob, lse_b=lse_b, attn=attn, cos32=cos32, sin32=sin32,
              cos256=cos256, sin256=sin256)
    return out, sv, rode


def _even_bwd(dh, w, sv, i, rider=None):
    s = dh.shape[0]
    half = B_ROPE // 2
    g = {}
    dattn = _mm_nt_t(dh, w["ev_out"], name="ev_out_dx")
    g["ev_w_out"] = _mm_tn(sv["attn"], dh, name="ev_out_dw", a_t=True)
    doa = dattn[:A_HEADS * A_HEAD_DIM].reshape(A_HEADS, A_HEAD_DIM, s)
    dob = dattn[A_HEADS * A_HEAD_DIM:].reshape(B_HEADS, B_V, s)
    first, then = rider if isinstance(rider, tuple) else (None, None)
    (dqa, dka, dva, dsink), brought = _swa_bwd_t(sv["qa"], sv["ka"], sv["va"], sv["oa"], doa, sv["lse_a"], sv["ss"],
                                                 scale=A_HEAD_DIM ** -0.5, window=WINDOW, name="swa_bwd", rider=first)
    if then is not None:
        rider = then(brought)
    g["ev_sinks"] = dsink[:, :, 0, 0].reshape(1, A_HEADS)
    (dqb, dkb, dvb), rode = _causal_bwd_t(sv["qb"], sv["kb"], sv["vb"], sv["ob"], dob, sv["lse_b"],
                                          scale=(B_NOPE + B_ROPE) ** -0.5, name="mla_bwd", tq=512, hb=4, rider=rider)
    dyq = _ev_q_merge(dqb, sv["cos256"], sv["sin256"], name="ev_q_merge")
    dwq = _mm_tn(sv["xn_q"], dyq, name="ev_q_up_dw")
    dcq, g["ev_cq_norm"] = _mm_nt_rmsbwd([(dyq, w["ev_q_cat"])], sv["c_q"], w["ev_cq_norm"], None, name="ev_q_up_dx")
    kq = sv["c_q"].shape[1]
    d_nope = dwq[:, :512].reshape(kq, B_HEADS, B_NOPE)
    d_rot = dwq[:, 512:768].reshape(kq, B_HEADS, B_ROPE)
    d_swp = dwq[:, 768:].reshape(kq, B_HEADS, B_ROPE)
    g["ev_w_uq"] = jnp.concatenate([d_nope, d_rot[:, :, :half] + d_swp[:, :, half:], d_rot[:, :, half:] - d_swp[:, :, :half]],
                                   axis=-1).reshape(kq, -1)
    dykv, dkr = _ev_kv_merge(dkb, dvb, sv["cos32"], sv["sin32"], name="ev_kv_merge")
    g["ev_w_ukv"] = _mm_tn(sv["xn_kv"], dykv, name="ev_kv_up_dw")
    dckv, g["ev_ckv_norm"] = _mm_nt_rmsbwd([(dykv, w["ev_ukv"])], sv["c_kv"], w["ev_ckv_norm"], None, name="ev_kv_up_dx")
    dycat = _ev_in_merge(dqa, dka, dva, dcq, dckv, dkr, name="ev_in_merge")
    dwin = _mm_tn(sv["xn"], dycat, name="ev_in_dw")
    base = 1184
    g["ev_w_in"] = jnp.concatenate([dwin[:, :base - B_ROPE],
                                    dwin[:, base - B_ROPE:base - half] + dwin[:, base + half:base + B_ROPE],
                                    dwin[:, base - half:base] - dwin[:, base:base + half]], axis=-1)
    dh_in, dnorm = _mm_nt_rmsbwd([(dycat, w["ev_in_cat"])], sv["h_in"], w["mix_norm"][i:i + 1], dh, name="ev_in_dx")
    return dh_in, dnorm, g, rode


def _odd_fwd(h, w, i, rider=None):
    s = h.shape[0]
    wd = C_HEADS * C_HEAD_DIM
    q, k, v, vt, y_f, xn = _fox_in_fwd(h, w["mix_norm"][i:i + 1], w["od_in_pad"], nheads=C_HEADS, dh=C_HEAD_DIM,
                                       q_ones=(0, 2, 3, 4), k_ones=(1,), name="od_in")
    scale = C_HEAD_DIM ** -0.5
    ft = y_f[:, :C_HEADS].T
    bf = w["od_b_f"].reshape(C_HEADS, 1)
    cb3 = _fox_gate_fwd(ft, bf, out_scale=-1.0 / scale, name="fox_gate_fwd")
    k = k + jnp.pad(cb3.transpose(1, 2, 0), ((0, 0), (0, 0), (C_HEAD_DIM + 2, LANES - C_HEAD_DIM - 5)))
    (o, lse), rode = _causal_fwd_t(q, k, vt, scale=scale, name="fox_fwd", tq=512, hb=4, rider=rider)
    attn = o.reshape(-1, s)
    out = _mm_res_fwd(attn, w["od_out"], h, scale=1.0, name="od_out", a_t=True)
    return out, dict(h_in=h, xn=xn, q=q, k=k, v=v, o=o, lse=lse, ft=ft, bf=bf, attn=attn), rode


def _odd_bwd(dh, w, sv, i, rider=None):
    s = dh.shape[0]
    g = {}
    dattn = _mm_nt_t(dh, w["od_out"], name="od_out_dx")
    g["od_w_out"] = _mm_tn(sv["attn"], dh, name="od_out_dw", a_t=True)
    do = dattn.reshape(C_HEADS, C_HEAD_DIM, s)
    scale = C_HEAD_DIM ** -0.5
    (dq, dk, dv), rode = _causal_bwd_t(sv["q"], sv["k"], sv["v"], sv["o"], do, sv["lse"], scale=scale, name="fox_bwd",
                                       tq=512, hb=4, rider=rider)
    dft, dbf = _fox_gate_bwd(dq[:, :, C_HEAD_DIM + 1], dk[:, :, C_HEAD_DIM], sv["ft"], sv["bf"],
                             inv_scale=1.0 / scale, name="fox_gate_bwd")
    g["od_b_f"] = dbf.reshape(1, C_HEADS)
    wd = C_HEADS * C_HEAD_DIM
    dqkv = _merge_heads(dq, dk, dv, dh=C_HEAD_DIM, name="fox_merge")
    df = jnp.pad(dft.T, ((0, 0), (0, LANES - C_HEADS)))
    g["od_w_in"] = jnp.concatenate([_mm_tn(sv["xn"], dqkv, name="od_in_dw"),
                                    _mm_tn(sv["xn"], df, name="od_in_dwf")[:, :C_HEADS]], axis=-1)
    dh_in, dnorm = _mm_nt_rmsbwd([(dqkv, w["od_in_pad"], 0), (df, w["od_in_pad"], 3 * wd // LANES)],
                                 sv["h_in"], w["mix_norm"][i:i + 1], dh, name="od_in_dx")
    return dh_in, dnorm, g, rode


def _kernel_weights(full, replicated):
    w = dict(replicated)
    _install_weights(w, {(n, i): a for n, per_layer in full.items() for i, a in enumerate(per_layer)})
    return w


def _install_weights(w, got):
    raw = w.setdefault("raw", {})
    raw.update(got)
    for (n, i), a in got.items():
        if n in ("ffa_w_gate_up", "ffa_w_down", "ffb_w_gate_up", "ffb_w_down"):
            w.setdefault(n[:3], {}).setdefault(i, {})["wgu" if n.endswith("gate_up") else "wd"] = a
        elif n in ("ple_w_gate", "ple_w_proj"):
            w.setdefault("ple_gate" if n.endswith("gate") else "ple_proj", {})[i] = a
    if "ev_in_cat" not in w and all((n, 0) in raw for n in ("ev_w_in", "ev_w_uq", "ev_w_ukv", "ev_w_out")):
        w["ev_in_cat"], w["ev_q_cat"], w["ev_ukv"] = _even_weights(raw["ev_w_in", 0], raw["ev_w_uq", 0], raw["ev_w_ukv", 0])
        w["ev_out"] = raw["ev_w_out", 0]
    if "od_in_pad" not in w and all((n, 0) in raw for n in ("od_w_in", "od_w_out")):
        od_in = raw["od_w_in", 0]
        w["od_in_pad"] = jnp.pad(od_in, ((0, 0), (0, (-od_in.shape[1]) % LANES)))
        w["od_out"] = raw["od_w_out", 0]


def _keys(names, layer):
    return tuple((n, layer) for n in names)


_FFA, _FFB, _PLE = ("ffa_w_gate_up", "ffa_w_down"), ("ffb_w_gate_up", "ffb_w_down"), ("ple_w_gate", "ple_w_proj")
_EV, _OD = ("ev_w_in", "ev_w_uq", "ev_w_ukv", "ev_w_out"), ("od_w_in", "od_w_out")
_GATHER_FIRST = _keys(_FFA[:1], 0)
_GATHER_RIDES = {("ffa", 0): _keys(_FFA[1:] + _EV, 0), ("mix", 0): _keys(_FFB + _PLE, 0) + _keys(_FFA, 1),
                 ("ffb", 0): _keys(_OD, 0), ("mix", 1): _keys(_FFB + _PLE, 1)}
_REDUCE_RIDES = {("mix", 1): _keys(_FFB + _PLE, 1), ("mix", 0): _keys(_FFA, 1) + _keys(_OD, 0) + _keys(_FFB + _PLE, 0),
                 ("ffa", 0): _keys(_EV, 0)}
_REDUCE_OWN = ("ffa", 0)


def _local_step(x, p, tgt, w, ex=None):
    depth = p.shape[0]

    def gather_behind(host, fn, *args):
        keys = None if ex is None else _GATHER_RIDES.get(host)
        if keys is None:
            return fn(*args, None)[:-1]
        done = []

        def install(rode):
            if not done:
                _install_weights(w, ex.gather_finish(keys, rode, name=f"weight_forward_{host[0]}{host[1]}"))
                done.append(True)

        res = fn(*args, ex.gather_rider(keys), install) if fn is _ffn_fwd else fn(*args, ex.gather_rider(keys))
        install(res[-1])
        return res[:-1]

    h = x
    saved = []
    for i in range(depth):
        sv = {}
        h, sv["ffa"] = gather_behind(("ffa", i), _ffn_fwd, h, w["ffa_norm"][i:i + 1], w["ffa"][i], f"ffa{i}")
        h, sv["mix"] = gather_behind(("mix", i), _even_fwd if i % 2 == 0 else _odd_fwd, h, w, i)
        h, sv["ffb"] = gather_behind(("ffb", i), _ffn_fwd, h, w["ffb_norm"][i:i + 1], w["ffb"][i], f"ffb{i}")
        h_in = h
        h, xn, gate, pp = _ple_fwd(h, w["ple_norm"][i:i + 1], w["ple_gate"][i], p[i], w["ple_proj"][i], name=f"ple{i}")
        sv["ple"] = dict(h_in=h_in, xn=xn, gate=gate, pp=pp)
        saved.append(sv)
    loss_vec, dh, d_final = _final_loss(h, w["final_norm"].reshape(1, -1), tgt, name="final_loss")

    per_layer = [dict() for _ in range(depth)]
    mats = {}
    grads = {}

    def reduce_behind(host, fn, *args):
        keys = None if ex is None else _REDUCE_RIDES.get(host)
        if keys is None:
            return fn(*args, None)[:-1]
        states = []
        if fn is _even_bwd:
            swap, ctx = ex.swap_rider(keys, mats)

            def then(brought):
                states.append(ex.after_swap(ctx, brought))
                return states[0][0]

            res = fn(*args, (swap, then))
        else:
            states.append(ex.reduce_begin(keys, mats, tag=f"{host[0]}{host[1]}"))
            if fn is _ffn_bwd and host == _REDUCE_OWN:
                own = []

                def wd_ready(d_wd):
                    own.append(ex.reduce_begin(_keys(_FFA[1:], 0), {("ffa_w_down", 0): d_wd}, tag="own_wd"))
                    return own[0][0]

                def wgu_ready(brought, d_wgu):
                    ex.reduce_finish(own[0], brought)
                    own.append(ex.reduce_begin(_keys(_FFA[:1], 0), {("ffa_w_gate_up", 0): d_wgu}, tag="own_wgu"))
                    return own[1][0]

                res = fn(*args, states[0][0], (wd_ready, wgu_ready, lambda brought: ex.reduce_finish(own[1], brought)))
            else:
                res = fn(*args, states[0][0])
        ex.reduce_finish(states[0], res[-1])
        return res[:-1]

    for i in reversed(range(depth)):
        sv, gl = saved[i], per_layer[i]
        dz, dpp = _ple_bwd_elem(dh, sv["ple"]["gate"], sv["ple"]["pp"], name=f"ple{i}_bwd")
        mats["ple_w_gate", i] = _mm_tn(sv["ple"]["xn"], dz, name=f"ple{i}_dwg")
        mats["ple_w_proj", i] = _mm_tn(p[i], dpp, name=f"ple{i}_dwp")
        dh, gl["ple_norm"] = _mm_nt_rmsbwd([(dz, w["ple_gate"][i])], sv["ple"]["h_in"], w["ple_norm"][i:i + 1], dh,
                                           name=f"ple{i}_dx")
        dh, gl["ffb_norm"], mats["ffb_w_gate_up", i], mats["ffb_w_down", i] = reduce_behind(
            ("ffb", i), _ffn_bwd, dh, w["ffb_norm"][i:i + 1], w["ffb"][i], sv["ffb"], f"ffb{i}")
        dh, gl["mix_norm"], gm = reduce_behind(("mix", i), _even_bwd if i % 2 == 0 else _odd_bwd, dh, w, sv["mix"], i)
        for n, g in gm.items():
            if n in REPLICATED:
                grads[n] = g
            else:
                mats[n, 0] = g
        dh, gl["ffa_norm"], mats["ffa_w_gate_up", i], mats["ffa_w_down", i] = reduce_behind(
            ("ffa", i), _ffn_bwd, dh, w["ffa_norm"][i:i + 1], w["ffa"][i], sv["ffa"], f"ffa{i}")
    grads["final_norm"] = d_final.reshape(-1)
    for n in ("ffa_norm", "mix_norm", "ffb_norm", "ple_norm"):
        grads[n] = jnp.concatenate([per_layer[i][n] for i in range(depth)], axis=0)
    if ex is None:
        for n, _ in SHARDED:
            grads[n] = [mats[n, i] for i in range(depth) if (n, i) in mats]
    return loss_vec[0, 0], dh, grads


def _cut_mode(local_shape, axis, ncols):
    return "cols" if axis == 2 and ncols % LANES == 0 else "blk"


class _Exchange:
    def __init__(self, wts):
        self.place = jnp.stack([2 * lax.axis_index("x") + lax.axis_index("y"), lax.axis_index("c")]).astype(jnp.int32)
        self.info = {}
        for n, axis in SHARDED:
            wb = wts[n].astype(BF16)
            mode = _cut_mode(wb.shape, axis, wb.shape[2])
            for i in range(wb.shape[0]):
                self.info[n, i] = dict(shard=wb[i], mode=mode, axis=axis)
        self.halves = {}

    def _modes(self, keys):
        return [self.info[k]["mode"] for k in keys]

    def gather_rider(self, keys):
        return _gather_rider([self.info[k]["shard"] for k in keys], self._modes(keys))

    def gather_finish(self, keys, landed, *, name):
        outs = _gather_forward(landed, [self.info[k]["shard"].shape for k in keys], self._modes(keys), name=name)
        got = {}
        for k, dst in zip(keys, outs):
            if self.info[k]["mode"] == "blk":
                dst = dst.reshape(-1, dst.shape[2]) if self.info[k]["axis"] == 1 else jnp.moveaxis(dst, 0, 1).reshape(dst.shape[1], -1)
            got[k] = dst
        return got

    def gather(self, keys, *, name):
        return self.gather_finish(keys, _run_rider(self.gather_rider(keys), name=name), name=name + "_forward")

    def swap_rider(self, keys, mats):
        modes = self._modes(keys)
        arrs = []
        for k in keys:
            g2, (rr, cc) = mats[k], self.info[k]["shard"].shape
            if self.info[k]["mode"] == "blk":
                g2 = g2.reshape(N_CHIPS, rr, cc) if self.info[k]["axis"] == 1 else g2.reshape(rr, N_CHIPS, cc).transpose(1, 0, 2)
            arrs.append(g2)
        return _swap_rider(arrs, modes), (keys, modes, arrs)

    def after_swap(self, ctx, landed):
        keys, modes, arrs = ctx
        parts = []
        for (n, i), m, a, l in zip(keys, modes, arrs, landed):
            pt = _rs_pair_add(_blk_view(a, m), _blk_view(l, m), self.place, name=f"rs_pair_add_{n}{i}")
            parts.append(pt[0] if m == "cols" else pt)
        return _exchange_rider(parts, modes), keys, parts

    def reduce_begin(self, keys, mats, *, tag):
        rider, ctx = self.swap_rider(keys, mats)
        return self.after_swap(ctx, _run_rider(rider, name=f"rs_pair_swap_{tag}"))

    def reduce_finish(self, state, landed):
        _, keys, parts = state
        for (n, i), m, pt, l in zip(keys, self._modes(keys), parts, landed):
            self.halves[n, i] = _rs_chip_sum(pt, l, m, self.place, name=f"rs_chip_sum_{n}{i}")

    def reduce(self, keys, mats, *, tag):
        state = self.reduce_begin(keys, mats, tag=tag)
        self.reduce_finish(state, _run_rider(state[0], name=f"rs_chip_exchange_{tag}"))

    def join(self, wts):
        keys = list(self.info)
        joined = dict(zip(keys, _rs_pair_join([self.halves[k] for k in keys], name="rs_pair_join")))
        return {n: jnp.stack([joined[n, i] for i in range(wts[n].shape[0])]).reshape(wts[n].shape) for n, _ in SHARDED}


def _small_rows(vals):
    rows = []
    for n in REPLICATED:
        v = vals[n].reshape(-1)
        rows.append(jnp.pad(v, (0, (-v.shape[0]) % FLAT_COLS)).reshape(-1, FLAT_COLS))
    out = jnp.concatenate(rows, axis=0)
    return jnp.pad(out, ((0, (-out.shape[0]) % 8), (0, 0)))


def kernel(x, p, ffa_norm, ffa_w_gate_up, ffa_w_down, mix_norm, ffb_norm, ffb_w_gate_up, ffb_w_down, ple_norm, ple_w_gate, ple_w_proj, ev_w_in, ev_sinks, ev_cq_norm, ev_w_uq, ev_ckv_norm, ev_w_ukv, ev_w_out, od_w_in, od_b_f, od_w_out, final_norm, loss_target, m_ffa_norm, m_ffa_w_gate_up, m_ffa_w_down, m_mix_norm, m_ffb_norm, m_ffb_w_gate_up, m_ffb_w_down, m_ple_norm, m_ple_w_gate, m_ple_w_proj, m_ev_w_in, m_ev_sinks, m_ev_cq_norm, m_ev_w_uq, m_ev_ckv_norm, m_ev_w_ukv, m_ev_w_out, m_od_w_in, m_od_b_f, m_od_w_out, m_final_norm, v_ffa_norm, v_ffa_w_gate_up, v_ffa_w_down, v_mix_norm, v_ffb_norm, v_ffb_w_gate_up, v_ffb_w_down, v_ple_norm, v_ple_w_gate, v_ple_w_proj, v_ev_w_in, v_ev_sinks, v_ev_cq_norm, v_ev_w_uq, v_ev_ckv_norm, v_ev_w_ukv, v_ev_w_out, v_od_w_in, v_od_b_f, v_od_w_out, v_final_norm):
    env = dict(locals())
    wts = {n: env[n] for n in WEIGHT_ORDER}
    mom1 = {n: env["m_" + n] for n in WEIGHT_ORDER}
    mom2 = {n: env["v_" + n] for n in WEIGHT_ORDER}
    ex = _Exchange(wts)

    w = {n: wts[n] for n in REPLICATED}
    _install_weights(w, ex.gather(_GATHER_FIRST, name="weight_gather_first"))

    loss_part, grad_x, grads = _local_step(x[0], p[:, 0], loss_target[0], w, ex)
    loss = lax.psum(loss_part, ("x", "y", "c"))
    gout = ex.join(wts)
    small = _allreduce_small(_small_rows(grads), name="small_allreduce")
    r0 = 0
    for n in REPLICATED:
        size = int(np.prod(wts[n].shape))
        nr = -(-size // FLAT_COLS)
        gout[n] = small[r0:r0 + nr].reshape(-1)[:size].reshape(wts[n].shape)
        r0 += nr

    delta, new_m, new_v = {}, {}, {}
    for n in WEIGHT_ORDER:
        delta[n], new_m[n], new_v[n] = _adamw(wts[n], gout[n], mom1[n], mom2[n], name="adamw_" + n)
    return (loss, grad_x[None], *[gout[n] for n in WEIGHT_ORDER], *[delta[n] for n in WEIGHT_ORDER],
            *[new_m[n] for n in WEIGHT_ORDER], *[new_v[n] for n in WEIGHT_ORDER])
```

```python
import functools
import math

import numpy as np
import jax
import jax.numpy as jnp
from jax import lax
from jax.experimental import pallas as pl
from jax.experimental.pallas import tpu as pltpu

F32 = jnp.float32
BF16 = jnp.bfloat16
NT = (((1,), (1,)), ((), ()))
TN = (((0,), (0,)), ((), ()))
MESH = pl.DeviceIdType.MESH

RMS_EPS = 1e-6
FFN_RES_SCALE = 0.5
A_HEADS, A_KV_HEADS, A_HEAD_DIM, WINDOW = 8, 2, 64, 128
B_HEADS, B_Q_LORA, B_KV_LORA, B_NOPE, B_ROPE, B_V = 8, 256, 128, 64, 32, 64
ROPE_THETA = 10000.0
C_HEADS, C_HEAD_DIM = 16, 64
ADAM_LR, ADAM_B1, ADAM_B2, ADAM_EPS, ADAM_WD, ADAM_STEP = 0.001, 0.9, 0.999, 1e-08, 0.01, 10

N_CHIPS = 4
LANES = 128
FLAT_COLS = 1024
MASK_VALUE = -1e30
VMEM_LIMIT = 48 * 2**20

SHARDED = (
    ("ffa_w_gate_up", 2), ("ffa_w_down", 1), ("ffb_w_gate_up", 2), ("ffb_w_down", 1),
    ("ple_w_gate", 1), ("ple_w_proj", 2), ("ev_w_in", 2), ("ev_w_uq", 2), ("ev_w_ukv", 2),
    ("ev_w_out", 1), ("od_w_in", 2), ("od_w_out", 1))
REPLICATED = ("ffa_norm", "mix_norm", "ffb_norm", "ple_norm", "final_norm",
              "ev_sinks", "ev_cq_norm", "ev_ckv_norm", "od_b_f")
WEIGHT_ORDER = ("ffa_norm", "ffa_w_gate_up", "ffa_w_down", "mix_norm", "ffb_norm", "ffb_w_gate_up",
                "ffb_w_down", "ple_norm", "ple_w_gate", "ple_w_proj", "ev_w_in", "ev_sinks",
                "ev_cq_norm", "ev_w_uq", "ev_ckv_norm", "ev_w_ukv", "ev_w_out", "od_w_in", "od_b_f",
                "od_w_out", "final_norm")


def _cp(*sem):
    return pltpu.CompilerParams(dimension_semantics=sem, vmem_limit_bytes=VMEM_LIMIT)


def _sigmoid(z):
    return 1.0 / (1.0 + jnp.exp(-z))


def _rms_stats(xv):
    r = lax.rsqrt(jnp.mean(xv * xv, axis=-1, keepdims=True) + RMS_EPS)
    return r, xv * r


def _rms_bwd(dxn, xv, g):
    r, xhat = _rms_stats(xv)
    u = dxn * g
    dx = r * (u - xhat * jnp.mean(u * xhat, axis=-1, keepdims=True))
    return dx, dxn * xhat


def _col_tile(k_rows, n, budget_bytes=6 * 2**20):
    if k_rows * n * 4 <= budget_bytes or n % LANES:
        return n
    units = n // LANES
    best = LANES
    for d in range(1, units + 1):
        if units % d == 0 and k_rows * d * LANES * 4 <= budget_bytes:
            best = d * LANES
    return best


def _row_tile(rows, cols, target_elems=2**18):
    if rows * cols <= target_elems or rows % 8:
        return rows
    best = 8
    for d in range(8, rows + 1, 8):
        if rows % d == 0 and d * cols <= target_elems:
            best = d
    return best


def _fox_in_fwd(x, g, w, *, nheads, dh, q_ones, k_ones, name, tm=512):
    s, k = x.shape
    n = w.shape[1]
    wd = nheads * dh
    spare = LANES - dh

    def body(x_ref, g_ref, w_ref, q_ref, k_ref, v_ref, vt_ref, f_ref, xn_ref):
        _, xhat = _rms_stats(x_ref[...])
        xn = (xhat * g_ref[...]).astype(BF16)
        xn_ref[...] = xn
        y = jnp.dot(xn, w_ref[...], preferred_element_type=F32)
        f_ref[...] = y[:, 3 * wd:]
        lane = lax.broadcasted_iota(jnp.int32, (tm, spare), 1)

        def fill(cols):
            return functools.reduce(jnp.logical_or, [lane == c for c in cols]).astype(F32)

        q_fill, k_fill = fill(q_ones), fill(k_ones)
        for h in range(nheads):
            q_ref[h] = jnp.concatenate([y[:, h * dh:(h + 1) * dh], q_fill], axis=-1).astype(BF16)
            k_ref[h] = jnp.concatenate([y[:, wd + h * dh:wd + (h + 1) * dh], k_fill], axis=-1).astype(BF16)
            vh = y[:, 2 * wd + h * dh:2 * wd + (h + 1) * dh]
            v_ref[h] = vh.astype(BF16)
            vt_ref[h] = vh.T.astype(BF16)

    wide = pl.BlockSpec((nheads, tm, LANES), lambda i: (0, i, 0))
    return pl.pallas_call(
        body, name=name, grid=(s // tm,),
        in_specs=[pl.BlockSpec((tm, k), lambda i: (i, 0)), pl.BlockSpec((1, k), lambda i: (0, 0)),
                  pl.BlockSpec((k, n), lambda i: (0, 0))],
        out_specs=[wide, wide, pl.BlockSpec((nheads, tm, dh), lambda i: (0, i, 0)),
                   pl.BlockSpec((nheads, dh, tm), lambda i: (0, 0, i)), pl.BlockSpec((tm, LANES), lambda i: (i, 0)),
                   pl.BlockSpec((tm, k), lambda i: (i, 0))],
        out_shape=[jax.ShapeDtypeStruct((nheads, s, LANES), BF16)] * 2
        + [jax.ShapeDtypeStruct((nheads, s, dh), BF16), jax.ShapeDtypeStruct((nheads, dh, s), BF16),
           jax.ShapeDtypeStruct((s, LANES), F32), jax.ShapeDtypeStruct((s, k), BF16)],
        compiler_params=_cp("arbitrary"))(x, g, w)


def _merge_heads(dq, dk, dvt, *, dh, q_col, k_col, name, tm=512):
    nheads, s, _ = dq.shape

    def body(dq_ref, dk_ref, dvt_ref, o_ref, cols_ref):
        pieces = [dq_ref[h][:, :dh] for h in range(nheads)] + [dk_ref[h][:, :dh] for h in range(nheads)]
        pieces += [dvt_ref[h].T for h in range(nheads)]
        o_ref[...] = jnp.concatenate(pieces, axis=-1)
        lane = lax.broadcasted_iota(jnp.int32, (tm, LANES), 1)
        cols = jnp.zeros((tm, LANES), F32)
        for h in range(nheads):
            cols = jnp.where(lane == h, jnp.broadcast_to(dq_ref[h][:, q_col:q_col + 1], (tm, LANES)), cols)
            cols = jnp.where(lane == nheads + h, jnp.broadcast_to(dk_ref[h][:, k_col:k_col + 1], (tm, LANES)), cols)
        cols_ref[...] = cols

    wide = pl.BlockSpec((nheads, tm, LANES), lambda i: (0, i, 0))
    return pl.pallas_call(
        body, name=name, grid=(s // tm,),
        in_specs=[wide, wide, pl.BlockSpec((nheads, dh, tm), lambda i: (0, 0, i))],
        out_specs=[pl.BlockSpec((tm, 3 * nheads * dh), lambda i: (i, 0)), pl.BlockSpec((tm, LANES), lambda i: (i, 0))],
        out_shape=[jax.ShapeDtypeStruct((s, 3 * nheads * dh), F32), jax.ShapeDtypeStruct((s, LANES), F32)],
        compiler_params=_cp("arbitrary"))(dq, dk, dvt)


def _row_call(body, n_rows, ins, outs, *, name, tm=512):
    def spec(a, axis):
        shape = a.shape
        if axis is None:
            return pl.BlockSpec(shape, lambda i: (0,) * len(shape))
        blk = tuple(tm if d == axis else n for d, n in enumerate(shape))
        return pl.BlockSpec(blk, lambda i: tuple(i if d == axis else 0 for d in range(len(shape))))

    return pl.pallas_call(
        body, name=name, grid=(n_rows // tm,), in_specs=[spec(a, ax) for a, ax in ins],
        out_specs=[spec(a, ax) for a, ax in outs], out_shape=[a for a, _ in outs],
        compiler_params=_cp("arbitrary"))(*[a for a, _ in ins])


def _sds(shape, dtype):
    return jax.ShapeDtypeStruct(shape, dtype)


def _ev_in_fwd(x, g, w, *, name):
    s, k = x.shape
    d = A_HEAD_DIM

    def body(x_ref, g_ref, w_ref, q_ref, k_ref, v_ref, vt_ref, cq_ref, ckv_ref, kr_ref, xn_ref):
        _, xhat = _rms_stats(x_ref[...])
        xn = (xhat * g_ref[...]).astype(BF16)
        xn_ref[...] = xn
        y = jnp.dot(xn, w_ref[...], preferred_element_type=F32)
        for h in range(A_HEADS):
            q_ref[h] = y[:, h * d:(h + 1) * d].astype(BF16)
        for h in range(A_KV_HEADS):
            k_ref[h] = y[:, 512 + h * d:512 + (h + 1) * d].astype(BF16)
            vh = y[:, 640 + h * d:640 + (h + 1) * d]
            v_ref[h] = vh.astype(BF16)
            vt_ref[h] = vh.T.astype(BF16)
        cq_ref[...] = y[:, 768:1024]
        ckv_ref[...] = y[:, 1024:1152]
        kr_ref[...] = y[:, 1152:1280]

    return _row_call(
        body, s, [(x, 0), (g, None), (w, None)],
        [(_sds((A_HEADS, s, d), BF16), 1), (_sds((A_KV_HEADS, s, d), BF16), 1), (_sds((A_KV_HEADS, s, d), BF16), 1),
         (_sds((A_KV_HEADS, d, s), BF16), 2), (_sds((s, B_Q_LORA), F32), 0), (_sds((s, B_KV_LORA), F32), 0),
         (_sds((s, LANES), F32), 0), (_sds((s, k), BF16), 0)], name=name)


def _ev_q_fwd(x, g, w, cos, sin, *, name):
    s, k = x.shape
    rot = B_HEADS * B_ROPE

    def body(x_ref, g_ref, w_ref, c_ref, s_ref, q_ref, xn_ref):
        _, xhat = _rms_stats(x_ref[...])
        xn = (xhat * g_ref[...]).astype(BF16)
        xn_ref[...] = xn
        y = jnp.dot(xn, w_ref[...], preferred_element_type=F32)
        ro = y[:, 512:512 + rot] * c_ref[...] + y[:, 512 + rot:] * s_ref[...]
        zero = jnp.zeros((y.shape[0], LANES - B_NOPE - B_ROPE), F32)
        for h in range(B_HEADS):
            q_ref[h] = jnp.concatenate([y[:, h * B_NOPE:(h + 1) * B_NOPE], ro[:, h * B_ROPE:(h + 1) * B_ROPE], zero],
                                       axis=-1).astype(BF16)

    return _row_call(body, s, [(x, 0), (g, None), (w, None), (cos, 0), (sin, 0)],
                     [(_sds((B_HEADS, s, LANES), BF16), 1), (_sds((s, k), BF16), 0)], name=name)


def _ev_kv_fwd(x, g, w, kro, *, name):
    s, k = x.shape
    per = B_NOPE + B_V

    def body(x_ref, g_ref, w_ref, kr_ref, k_ref, v_ref, vt_ref, xn_ref):
        _, xhat = _rms_stats(x_ref[...])
        xn = (xhat * g_ref[...]).astype(BF16)
        xn_ref[...] = xn
        y = jnp.dot(xn, w_ref[...], preferred_element_type=F32)
        kr = kr_ref[...]
        zero = jnp.zeros((y.shape[0], LANES - B_NOPE - B_ROPE), F32)
        for h in range(B_HEADS):
            k_ref[h] = jnp.concatenate([y[:, h * per:h * per + B_NOPE], kr, zero], axis=-1).astype(BF16)
            vh = y[:, h * per + B_NOPE:(h + 1) * per]
            v_ref[h] = vh.astype(BF16)
            vt_ref[h] = vh.T.astype(BF16)

    return _row_call(body, s, [(x, 0), (g, None), (w, None), (kro, 0)],
                     [(_sds((B_HEADS, s, LANES), BF16), 1), (_sds((B_HEADS, s, B_V), BF16), 1),
                      (_sds((B_HEADS, B_V, s), BF16), 2), (_sds((s, k), BF16), 0)], name=name)


def _ev_q_merge(dq, cos, sin, *, name):
    nh, s, _ = dq.shape

    def body(dq_ref, c_ref, s_ref, o_ref):
        dro = jnp.concatenate([dq_ref[h][:, B_NOPE:B_NOPE + B_ROPE] for h in range(nh)], axis=-1)
        o_ref[...] = jnp.concatenate([dq_ref[h][:, :B_NOPE] for h in range(nh)] + [dro * c_ref[...], dro * s_ref[...]], axis=-1)

    return _row_call(body, s, [(dq, 1), (cos, 0), (sin, 0)], [(_sds((s, 2 * nh * B_NOPE), F32), 0)], name=name)[0]


def _ev_kv_merge(dk, dvt, cos, sin, *, name):
    nh, s, _ = dk.shape

    def body(dk_ref, dvt_ref, c_ref, s_ref, o_ref, kr_ref):
        pieces = []
        tot = None
        for h in range(nh):
            pieces += [dk_ref[h][:, :B_NOPE], dvt_ref[h].T]
            rot = dk_ref[h][:, B_NOPE:B_NOPE + B_ROPE]
            tot = rot if tot is None else tot + rot
        o_ref[...] = jnp.concatenate(pieces, axis=-1)
        kr_ref[...] = jnp.concatenate([tot * c_ref[...], tot * s_ref[...], jnp.zeros((tot.shape[0], LANES - 2 * B_ROPE), F32)],
                                      axis=-1)

    return _row_call(body, s, [(dk, 1), (dvt, 2), (cos, 0), (sin, 0)],
                     [(_sds((s, nh * (B_NOPE + B_V)), F32), 0), (_sds((s, LANES), F32), 0)], name=name)


def _ev_in_merge(dq, dk, dvt, dcq, dckv, dkr, *, name):
    s = dcq.shape[0]

    def body(dq_ref, dk_ref, dvt_ref, cq_ref, ckv_ref, kr_ref, o_ref):
        pieces = [dq_ref[h] for h in range(A_HEADS)] + [dk_ref[h] for h in range(A_KV_HEADS)]
        pieces += [dvt_ref[h].T for h in range(A_KV_HEADS)] + [cq_ref[...], ckv_ref[...], kr_ref[...]]
        o_ref[...] = jnp.concatenate(pieces, axis=-1)

    return _row_call(body, s, [(dq, 1), (dk, 1), (dvt, 2), (dcq, 0), (dckv, 0), (dkr, 0)],
                     [(_sds((s, 1280), F32), 0)], name=name)[0]


def _ffn_up(x, g, wgu, *, name, tm=512, rider=None):
    s, k = x.shape
    f = wgu.shape[1] // 2
    tn = _col_tile(k, f)
    nj = f // tn

    def body(x_ref, g_ref, wg_ref, wu_ref, dgate_ref, dup_ref, act_ref, xn_ref, xn_sc):
        @pl.when(pl.program_id(1) == 0)
        def _():
            _, xhat = _rms_stats(x_ref[...])
            xn = (xhat * g_ref[...]).astype(BF16)
            xn_sc[...] = xn
            xn_ref[...] = xn

        xn = xn_sc[...]
        gg = jnp.dot(xn, wg_ref[...], preferred_element_type=F32)
        uu = jnp.dot(xn, wu_ref[...], preferred_element_type=F32)
        sg = _sigmoid(gg)
        silu = gg * sg
        dgate_ref[...] = (uu * (sg * (1.0 + gg * (1.0 - sg)))).astype(BF16)
        dup_ref[...] = silu.astype(BF16)
        act_ref[...] = (silu * uu).astype(BF16)

    tile = pl.BlockSpec((tm, tn), lambda i, j: (i, j))
    return _call_with_rider(
        body, rider, name=name, grid=(s // tm, nj),
        in_specs=[pl.BlockSpec((tm, k), lambda i, j: (i, 0)), pl.BlockSpec((1, k), lambda i, j: (0, 0)),
                  pl.BlockSpec((k, tn), lambda i, j: (0, j)), pl.BlockSpec((k, tn), lambda i, j: (0, j + nj))],
        out_specs=[tile, tile, tile, pl.BlockSpec((tm, k), lambda i, j: (i, 0))],
        out_shape=[jax.ShapeDtypeStruct((s, f), BF16)] * 3 + [jax.ShapeDtypeStruct((s, k), BF16)],
        scratch_shapes=[pltpu.VMEM((tm, k), BF16)],
        compiler_params=_cp("arbitrary", "arbitrary"), args=(x, g, wgu, wgu))


def _mm_res_fwd(a, w, res, *, scale, name, tm=512, a_t=False):
    k, n = w.shape
    s = res.shape[0]

    def body(a_ref, w_ref, r_ref, o_ref):
        prod = (lax.dot_general(a_ref[...], w_ref[...], TN, preferred_element_type=F32) if a_t
                else jnp.dot(a_ref[...], w_ref[...], preferred_element_type=F32))
        o_ref[...] = r_ref[...] + scale * prod

    a_spec = pl.BlockSpec((k, tm), lambda i: (0, i)) if a_t else pl.BlockSpec((tm, k), lambda i: (i, 0))
    return pl.pallas_call(
        body, name=name, grid=(s // tm,),
        in_specs=[a_spec, pl.BlockSpec((k, n), lambda i: (0, 0)),
                  pl.BlockSpec((tm, n), lambda i: (i, 0))],
        out_specs=pl.BlockSpec((tm, n), lambda i: (i, 0)),
        out_shape=jax.ShapeDtypeStruct((s, n), F32),
        compiler_params=_cp("arbitrary"))(a, w, res)


def _ffn_down_bwd(dh, wd, dact_dgate, dact_dup, *, scale, name, tm=512, rider=None):
    s, d = dh.shape
    f = wd.shape[0]
    tn = _col_tile(d, f)

    def body(dh_ref, wd_ref, fg_ref, fu_ref, dg_ref, du_ref):
        dhb = (dh_ref[...] * scale).astype(BF16)
        da = lax.dot_general(dhb, wd_ref[...], NT, preferred_element_type=F32)
        dg_ref[...] = (da * fg_ref[...].astype(F32)).astype(BF16)
        du_ref[...] = (da * fu_ref[...].astype(F32)).astype(BF16)

    tile = pl.BlockSpec((tm, tn), lambda i, j: (i, j))
    return _call_with_rider(
        body, rider, name=name, grid=(s // tm, f // tn),
        in_specs=[pl.BlockSpec((tm, d), lambda i, j: (i, 0)), pl.BlockSpec((tn, d), lambda i, j: (j, 0)), tile, tile],
        out_specs=[tile, tile],
        out_shape=[jax.ShapeDtypeStruct((s, f), BF16)] * 2, scratch_shapes=[],
        compiler_params=_cp("arbitrary", "arbitrary"), args=(dh, wd, dact_dgate, dact_dup))


def _mm_tn(a, bs, *, name, b_scale=1.0, ts=512, rider=None, a_t=False):
    bs = list(bs) if isinstance(bs, (list, tuple)) else [bs]
    k, s = a.shape if a_t else a.shape[::-1]
    n = bs[0].shape[1]
    tn = _col_tile(k, n, 12 * 2**20)
    per = n // tn

    def body(a_ref, *refs):
        b_refs, o_ref = refs[:-1], refs[-1]
        j = pl.program_id(0)

        @pl.when(pl.program_id(1) == 0)
        def _():
            o_ref[...] = jnp.zeros_like(o_ref)

        for m, b_ref in enumerate(b_refs):
            def acc(b_ref=b_ref):
                bv = b_ref[...]
                if b_scale != 1.0:
                    bv = bv * b_scale
                av = a_ref[...].astype(BF16)
                o_ref[...] += (jnp.dot(av, bv.astype(BF16), preferred_element_type=F32) if a_t
                               else lax.dot_general(av, bv.astype(BF16), TN, preferred_element_type=F32))

            if len(b_refs) == 1:
                acc()
            else:
                pl.when(jnp.logical_and(j >= m * per, j < (m + 1) * per))(acc)

    def b_spec(m):
        def idx(j, t):
            mine = jnp.logical_and(j >= m * per, j < (m + 1) * per)
            return (jnp.where(mine, t, 0), jnp.clip(j - m * per, 0, per - 1))
        return pl.BlockSpec((ts, tn), idx)

    (out,), rode = _call_with_rider(
        body, rider, name=name, grid=(per * len(bs), s // ts),
        in_specs=[pl.BlockSpec((k, ts), lambda j, t: (0, t)) if a_t else pl.BlockSpec((ts, k), lambda j, t: (t, 0))]
        + [b_spec(m) for m in range(len(bs))],
        out_specs=[pl.BlockSpec((k, tn), lambda j, t: (0, j))],
        out_shape=[jax.ShapeDtypeStruct((k, n * len(bs)), F32)], scratch_shapes=[],
        compiler_params=_cp("arbitrary", "arbitrary"), args=(a, *bs))
    return out if rider is None else (out, rode)


def _mm_nt_t(dy, w, *, name, tm=512):
    s, n = dy.shape
    k = w.shape[0]

    def body(dy_ref, w_ref, o_ref):
        o_ref[...] = lax.dot_general(w_ref[...], dy_ref[...].astype(BF16), NT, preferred_element_type=F32).astype(BF16)

    return pl.pallas_call(
        body, name=name, grid=(s // tm,),
        in_specs=[pl.BlockSpec((tm, n), lambda i: (i, 0)), pl.BlockSpec((k, n), lambda i: (0, 0))],
        out_specs=pl.BlockSpec((k, tm), lambda i: (0, i)),
        out_shape=jax.ShapeDtypeStruct((k, s), BF16),
        compiler_params=_cp("arbitrary"))(dy, w)


def _mm_nt_rmsbwd(pairs, x, g, dres, *, name, tm=256, rider=None):
    s, k = x.shape
    npairs = len(pairs)
    pairs = [pr if len(pr) == 3 else (pr[0], pr[1], 0) for pr in pairs]

    def body(*refs):
        dy_refs = refs[0:2 * npairs:2]
        w_refs = refs[1:2 * npairs:2]
        rest = refs[2 * npairs:]
        x_ref, g_ref = rest[0], rest[1]
        if dres is None:
            dx_ref, dg_ref = rest[2], rest[3]
        else:
            dres_ref, dx_ref, dg_ref = rest[2], rest[3], rest[4]
        dxn = None
        for dy_ref, w_ref in zip(dy_refs, w_refs):
            t = lax.dot_general(dy_ref[...].astype(BF16), w_ref[...], NT, preferred_element_type=F32)
            dxn = t if dxn is None else dxn + t
        dx, dgrow = _rms_bwd(dxn, x_ref[...], g_ref[...])
        if dres is not None:
            dx = dx + dres_ref[...]
        dx_ref[...] = dx

        @pl.when(pl.program_id(0) == 0)
        def _():
            dg_ref[...] = jnp.zeros_like(dg_ref)

        dg_ref[...] += jnp.sum(dgrow, axis=0, keepdims=True)

    in_specs, args = [], []
    for dy, w, cb in pairs:
        n = dy.shape[1]
        in_specs += [pl.BlockSpec((tm, n), lambda i: (i, 0)), pl.BlockSpec((k, n), lambda i, cb=cb: (0, cb))]
        args += [dy, w]
    row = pl.BlockSpec((tm, k), lambda i: (i, 0))
    vec = pl.BlockSpec((1, k), lambda i: (0, 0))
    in_specs += [row, vec]
    args += [x, g]
    if dres is not None:
        in_specs.append(row)
        args.append(dres)
    (dx, dgain), rode = _call_with_rider(
        body, rider, name=name, grid=(s // tm,), in_specs=in_specs, out_specs=[row, vec],
        out_shape=[jax.ShapeDtypeStruct((s, k), F32), jax.ShapeDtypeStruct((1, k), F32)], scratch_shapes=[],
        compiler_params=_cp("arbitrary"), args=args)
    return (dx, dgain) if rider is None else (dx, dgain, rode)


def _ple_fwd(h, g, wg, p, wp, *, name, tm=512):
    s, d = h.shape
    pd = p.shape[1]

    def body(h_ref, g_ref, wg_ref, p_ref, wp_ref, o_ref, xn_ref, gate_ref, pp_ref):
        hv = h_ref[...]
        _, xhat = _rms_stats(hv)
        xn = (xhat * g_ref[...]).astype(BF16)
        xn_ref[...] = xn
        gate = _sigmoid(jnp.dot(xn, wg_ref[...], preferred_element_type=F32))
        pp = jnp.dot(p_ref[...].astype(BF16), wp_ref[...], preferred_element_type=F32)
        gate_ref[...] = gate.astype(BF16)
        pp_ref[...] = pp.astype(BF16)
        o_ref[...] = hv + gate * pp

    row = pl.BlockSpec((tm, d), lambda i: (i, 0))
    return pl.pallas_call(
        body, name=name, grid=(s // tm,),
        in_specs=[row, pl.BlockSpec((1, d), lambda i: (0, 0)), pl.BlockSpec((d, d), lambda i: (0, 0)),
                  pl.BlockSpec((tm, pd), lambda i: (i, 0)), pl.BlockSpec((pd, d), lambda i: (0, 0))],
        out_specs=[row, row, row, row],
        out_shape=[jax.ShapeDtypeStruct((s, d), F32)] + [jax.ShapeDtypeStruct((s, d), BF16)] * 3,
        compiler_params=_cp("arbitrary"))(h, g, wg, p, wp)


def _ple_bwd_elem(dh, gate, pp, *, name, tm=512):
    s, d = dh.shape

    def body(dh_ref, gate_ref, pp_ref, dz_ref, dpp_ref):
        dhv = dh_ref[...]
        gt = gate_ref[...].astype(F32)
        dz_ref[...] = (dhv * pp_ref[...].astype(F32) * (gt * (1.0 - gt))).astype(BF16)
        dpp_ref[...] = (dhv * gt).astype(BF16)

    row = pl.BlockSpec((tm, d), lambda i: (i, 0))
    return pl.pallas_call(
        body, name=name, grid=(s // tm,), in_specs=[row, row, row], out_specs=[row, row],
        out_shape=[jax.ShapeDtypeStruct((s, d), BF16)] * 2,
        compiler_params=_cp("arbitrary"))(dh, gate, pp)


def _final_loss(h, g, tgt, *, name, tm=512):
    s, d = h.shape

    def body(h_ref, g_ref, t_ref, loss_ref, dh_ref, dg_ref):
        @pl.when(pl.program_id(0) == 0)
        def _():
            loss_ref[...] = jnp.zeros_like(loss_ref)
            dg_ref[...] = jnp.zeros_like(dg_ref)

        hv = h_ref[...]
        gv = g_ref[...]
        _, xhat = _rms_stats(hv)
        err = xhat * gv - t_ref[...]
        per_row = jnp.mean(err * err, axis=-1, keepdims=True)
        loss_ref[...] += 0.5 * jnp.sum(per_row, axis=0, keepdims=True)
        dx, dgrow = _rms_bwd(err * (1.0 / d), hv, gv)
        dh_ref[...] = dx
        dg_ref[...] += jnp.sum(dgrow, axis=0, keepdims=True)

    row = pl.BlockSpec((tm, d), lambda i: (i, 0))
    vec = pl.BlockSpec((1, d), lambda i: (0, 0))
    return pl.pallas_call(
        body, name=name, grid=(s // tm,), in_specs=[row, vec, row],
        out_specs=[pl.BlockSpec((1, LANES), lambda i: (0, 0)), row, vec],
        out_shape=[jax.ShapeDtypeStruct((1, LANES), F32), jax.ShapeDtypeStruct((s, d), F32),
                   jax.ShapeDtypeStruct((1, d), F32)],
        compiler_params=_cp("arbitrary"))(h, g, tgt)


def _rope_fwd(y1, y2, cos, sin, *, name, tm=512):
    s, r = y1.shape

    def body(a_ref, b_ref, c_ref, s_ref, o_ref):
        o_ref[...] = a_ref[...] * c_ref[...] + b_ref[...] * s_ref[...]

    row = pl.BlockSpec((tm, r), lambda i: (i, 0))
    return pl.pallas_call(
        body, name=name, grid=(s // tm,), in_specs=[row] * 4, out_specs=row,
        out_shape=jax.ShapeDtypeStruct((s, r), F32), compiler_params=_cp("arbitrary"))(y1, y2, cos, sin)


def _split3(v):
    h1 = v.astype(BF16)
    r1 = v - h1.astype(F32)
    h2 = r1.astype(BF16)
    h3 = (r1 - h2.astype(F32)).astype(BF16)
    return h1, h2, h3


def _tri(tb, upper):
    r = lax.broadcasted_iota(jnp.int32, (tb, tb), 0)
    c = lax.broadcasted_iota(jnp.int32, (tb, tb), 1)
    return jnp.where((r <= c) if upper else (r >= c), 1.0, 0.0).astype(BF16)


def _fox_gate_fwd(ft, bf, *, out_scale, name, tb=512):
    nh, s = ft.shape

    def body(f_ref, b_ref, o_ref, carry):
        @pl.when(pl.program_id(0) == 0)
        def _():
            carry[...] = jnp.zeros_like(carry)

        z = f_ref[...] + b_ref[...]
        lf = jnp.minimum(z, 0.0) - jnp.log(1.0 + jnp.exp(-jnp.abs(z)))
        tri = _tri(tb, True)
        cs = sum(jnp.dot(t, tri, preferred_element_type=F32) for t in _split3(lf)) + carry[...]
        for n, term in enumerate(_split3(cs * out_scale)):
            o_ref[n] = term
        carry[...] += jnp.sum(lf, axis=-1, keepdims=True)

    return pl.pallas_call(
        body, name=name, grid=(s // tb,),
        in_specs=[pl.BlockSpec((nh, tb), lambda t: (0, t)), pl.BlockSpec((nh, 1), lambda t: (0, 0))],
        out_specs=pl.BlockSpec((3, nh, tb), lambda t: (0, 0, t)),
        out_shape=jax.ShapeDtypeStruct((3, nh, s), BF16),
        scratch_shapes=[pltpu.VMEM((nh, 1), F32)], compiler_params=_cp("arbitrary"))(ft, bf)


def _fox_gate_bwd(drow, dcol, ft, bf, *, inv_scale, name, tb=512):
    nh, s = ft.shape
    nb = s // tb

    def body(dr_ref, dc_ref, f_ref, b_ref, df_ref, db_ref, carry):
        @pl.when(pl.program_id(0) == 0)
        def _():
            carry[...] = jnp.zeros_like(carry)
            db_ref[...] = jnp.zeros_like(db_ref)

        dc = (dr_ref[...] - dc_ref[...]) * inv_scale
        tri = _tri(tb, False)
        suf = sum(jnp.dot(t, tri, preferred_element_type=F32) for t in _split3(dc)) + carry[...]
        z = f_ref[...] + b_ref[...]
        dz = suf * (1.0 / (1.0 + jnp.exp(z)))
        df_ref[...] = dz
        db_ref[...] += jnp.sum(dz, axis=-1, keepdims=True)
        carry[...] += jnp.sum(dc, axis=-1, keepdims=True)

    rev = pl.BlockSpec((nh, tb), lambda t: (0, nb - 1 - t))
    one = pl.BlockSpec((nh, 1), lambda t: (0, 0))
    return pl.pallas_call(
        body, name=name, grid=(nb,), in_specs=[rev, rev, rev, one], out_specs=[rev, one],
        out_shape=[jax.ShapeDtypeStruct((nh, s), F32), jax.ShapeDtypeStruct((nh, 1), F32)],
        scratch_shapes=[pltpu.VMEM((nh, 1), F32)], compiler_params=_cp("arbitrary"))(drow, dcol, ft, bf)


def _tri_fwd(t, nq):
    i = sum((t >= (r * (r + 1)) // 2).astype(jnp.int32) for r in range(1, nq))
    return i, t - (i * (i + 1)) // 2


def _tri_bwd(t, nq):
    j = sum((t >= r * nq - (r * (r - 1)) // 2).astype(jnp.int32) for r in range(1, nq))
    return j, j + t - (j * nq - (j * (j - 1)) // 2)


def _scores_t(k, q, *, scale, diag):
    s = lax.dot_general(k, q, NT, preferred_element_type=F32) * scale
    if diag:
        r = lax.broadcasted_iota(jnp.int32, s.shape, 0)
        c = lax.broadcasted_iota(jnp.int32, s.shape, 1)
        s = jnp.where(r <= c, s, MASK_VALUE)
    return s


def _causal_fwd_t(q, k, vt, *, scale, name, tq, hb=2, rider=None):
    nh, s, dq = q.shape
    dv = vt.shape[1]
    nq = s // tq
    nsteps = (nq * (nq + 1)) // 2

    def body(q_ref, k_ref, vt_ref, o_ref, lse_ref, m_sc, l_sc, acc_sc):
        i, j = _tri_fwd(pl.program_id(1), nq)

        @pl.when(j == 0)
        def _():
            m_sc[...] = jnp.full_like(m_sc, MASK_VALUE)
            l_sc[...] = jnp.zeros_like(l_sc)
            acc_sc[...] = jnp.zeros_like(acc_sc)

        def step(diag):
            for u in range(hb):
                sc = _scores_t(k_ref[u], q_ref[u], scale=scale, diag=diag)
                m_prev = m_sc[u]
                m_new = jnp.maximum(m_prev, jnp.max(sc, axis=0, keepdims=True))
                alpha = jnp.exp(m_prev - m_new)
                pr = jnp.exp(sc - m_new)
                l_new = alpha * l_sc[u] + jnp.sum(pr, axis=0, keepdims=True)
                acc = alpha * acc_sc[u] + jnp.dot(vt_ref[u], pr.astype(BF16), preferred_element_type=F32)
                if diag:
                    o_ref[u] = (acc / l_new).astype(BF16)
                    lse_ref[u] = m_new + jnp.log(l_new)
                else:
                    m_sc[u], l_sc[u], acc_sc[u] = m_new, l_new, acc

        pl.when(j < i)(functools.partial(step, False))
        pl.when(j == i)(functools.partial(step, True))

    def qi(t):
        return _tri_fwd(t, nq)[0]

    def kj(t):
        return _tri_fwd(t, nq)[1]

    return _call_with_rider(
        body, rider, name=name, grid=(nh // hb, nsteps),
        in_specs=[pl.BlockSpec((hb, tq, dq), lambda hp, t: (hp, qi(t), 0)),
                  pl.BlockSpec((hb, tq, dq), lambda hp, t: (hp, kj(t), 0)),
                  pl.BlockSpec((hb, dv, tq), lambda hp, t: (hp, 0, kj(t)))],
        out_specs=[pl.BlockSpec((hb, dv, tq), lambda hp, t: (hp, 0, qi(t))),
                   pl.BlockSpec((hb, 1, tq), lambda hp, t: (hp, 0, qi(t)))],
        out_shape=[jax.ShapeDtypeStruct((nh, dv, s), BF16), jax.ShapeDtypeStruct((nh, 1, s), F32)],
        scratch_shapes=[pltpu.VMEM((hb, 1, tq), F32), pltpu.VMEM((hb, 1, tq), F32), pltpu.VMEM((hb, dv, tq), F32)],
        compiler_params=_cp("arbitrary", "arbitrary"), args=(q, k, vt))


def _causal_bwd_t(q, k, v, ot, dot_, lse, *, scale, name, tq, hb=2, rider=None):
    nh, s, dq = q.shape
    dv = v.shape[-1]
    nq = s // tq
    nsteps = (nq * (nq + 1)) // 2

    def body(q_ref, k_ref, v_ref, ot_ref, dot_ref, lse_ref, dq_ref, dk_ref, dvt_ref):
        t = pl.program_id(1)
        j, i = _tri_bwd(t, nq)

        @pl.when(t == 0)
        def _():
            dq_ref[...] = jnp.zeros_like(dq_ref)

        def step(diag):
            rows = pl.ds(pl.multiple_of(i * tq, tq), tq)
            for u in range(hb):
                qv, kv, dov = q_ref[u], k_ref[u], dot_ref[u]
                pr = jnp.exp(_scores_t(kv, qv, scale=scale, diag=diag) - lse_ref[u])
                dp = jnp.dot(v_ref[u], dov, preferred_element_type=F32)
                delta = jnp.sum(dov.astype(F32) * ot_ref[u].astype(F32), axis=0, keepdims=True)
                dsb = ((pr * (dp - delta)) * scale).astype(BF16)
                d_v = lax.dot_general(dov, pr.astype(BF16), NT, preferred_element_type=F32)
                d_k = jnp.dot(dsb, qv, preferred_element_type=F32)
                if diag:
                    dvt_ref[u], dk_ref[u] = d_v, d_k
                else:
                    dvt_ref[u] += d_v
                    dk_ref[u] += d_k
                dq_ref[u, rows, :] += lax.dot_general(dsb, kv, TN, preferred_element_type=F32)

        pl.when(i > j)(functools.partial(step, False))
        pl.when(i == j)(functools.partial(step, True))

    def qi(t):
        return _tri_bwd(t, nq)[1]

    def kj(t):
        return _tri_bwd(t, nq)[0]

    rows_q = pl.BlockSpec((hb, tq, dq), lambda hp, t: (hp, qi(t), 0))
    rows_k = pl.BlockSpec((hb, tq, dq), lambda hp, t: (hp, kj(t), 0))
    lanes_q = pl.BlockSpec((hb, dv, tq), lambda hp, t: (hp, 0, qi(t)))
    return _call_with_rider(
        body, rider, name=name, grid=(nh // hb, nsteps),
        in_specs=[rows_q, rows_k, pl.BlockSpec((hb, tq, dv), lambda hp, t: (hp, kj(t), 0)), lanes_q, lanes_q,
                  pl.BlockSpec((hb, 1, tq), lambda hp, t: (hp, 0, qi(t)))],
        out_specs=[pl.BlockSpec((hb, s, dq), lambda hp, t: (hp, 0, 0)), rows_k,
                   pl.BlockSpec((hb, dv, tq), lambda hp, t: (hp, 0, kj(t)))],
        out_shape=[jax.ShapeDtypeStruct((nh, s, dq), F32), jax.ShapeDtypeStruct((nh, s, dq), F32),
                   jax.ShapeDtypeStruct((nh, dv, s), F32)],
        scratch_shapes=[], compiler_params=_cp("arbitrary", "arbitrary"), args=(q, k, v, ot, dot_, lse))


def _swa_scores_t(k, q, dist, ok, *, scale, slope):
    s = lax.dot_general(k, q, NT, preferred_element_type=F32) * scale - slope * dist.astype(F32)
    return jnp.where(ok, s, MASK_VALUE)


def _swa_geometry(tb, w, has_other):
    r = lax.broadcasted_iota(jnp.int32, (tb, tb), 0)
    c = lax.broadcasted_iota(jnp.int32, (tb, tb), 1)
    d_same = c - r
    ok_same = jnp.logical_and(d_same >= 0, d_same < w)

    def other(ncols):
        rr = lax.broadcasted_iota(jnp.int32, (w, ncols), 0)
        cc = lax.broadcasted_iota(jnp.int32, (w, ncols), 1)
        dd = cc + w - rr
        return dd, jnp.logical_and(dd < w, has_other)

    return (d_same, ok_same), other


def _swa_fwd_t(q, k, vt, slopes_sinks, *, scale, window, name, tb=256):
    nh, s, d = q.shape
    nkv = k.shape[0]
    grp = nh // nkv
    w = window
    per = tb // w
    assert tb % w == 0

    def body(q_ref, kc_ref, kp_ref, vc_ref, vp_ref, ss_ref, o_ref, lse_ref):
        kvh, i = pl.program_id(0), pl.program_id(1)
        (d_c, ok_c), other = _swa_geometry(tb, w, i > 0)
        d_p, ok_p = other(tb)
        for g in range(grp):
            h = kvh * grp + g
            slope, sink = ss_ref[0, h], ss_ref[1, h]
            qg = q_ref[g]
            s_c = _swa_scores_t(kc_ref[...], qg, d_c, ok_c, scale=scale, slope=slope)
            s_p = _swa_scores_t(kp_ref[...], qg, d_p, ok_p, scale=scale, slope=slope)
            m = jnp.maximum(jnp.maximum(jnp.max(s_c, axis=0, keepdims=True), jnp.max(s_p, axis=0, keepdims=True)), sink)
            p_c, p_p = jnp.exp(s_c - m), jnp.exp(s_p - m)
            l = jnp.sum(p_c, axis=0, keepdims=True) + jnp.sum(p_p, axis=0, keepdims=True) + jnp.exp(sink - m)
            acc = (jnp.dot(vc_ref[...], p_c.astype(BF16), preferred_element_type=F32)
                   + jnp.dot(vp_ref[...], p_p.astype(BF16), preferred_element_type=F32))
            o_ref[g] = (acc / l).astype(BF16)
            lse_ref[g] = m + jnp.log(l)

    def prev(i):
        return jnp.maximum(i * per - 1, 0)

    return pl.pallas_call(
        body, name=name, grid=(nkv, s // tb),
        in_specs=[pl.BlockSpec((grp, tb, d), lambda kh, i: (kh, i, 0)),
                  pl.BlockSpec((None, tb, d), lambda kh, i: (kh, i, 0)),
                  pl.BlockSpec((None, w, d), lambda kh, i: (kh, prev(i), 0)),
                  pl.BlockSpec((None, d, tb), lambda kh, i: (kh, 0, i)),
                  pl.BlockSpec((None, d, w), lambda kh, i: (kh, 0, prev(i))),
                  pl.BlockSpec(memory_space=pltpu.SMEM)],
        out_specs=[pl.BlockSpec((grp, d, tb), lambda kh, i: (kh, 0, i)), pl.BlockSpec((grp, 1, tb), lambda kh, i: (kh, 0, i))],
        out_shape=[jax.ShapeDtypeStruct((nh, d, s), BF16), jax.ShapeDtypeStruct((nh, 1, s), F32)],
        compiler_params=_cp("arbitrary", "arbitrary"))(q, k, k, vt, vt, slopes_sinks)


def _swa_bwd_t(q, k, v, ot, dot_, lse, slopes_sinks, *, scale, window, name, tb=256, rider=None):
    nh, s, d = q.shape
    nkv = k.shape[0]
    grp = nh // nkv
    w = window
    per = tb // w
    nb = s // tb

    def body(qc_ref, qn_ref, kc_ref, kp_ref, vc_ref, vp_ref, oc_ref, on_ref, doc_ref, don_ref, lc_ref, ln_ref, ss_ref,
             dq_ref, dk_ref, dvt_ref, dsink_ref):
        kvh, i = pl.program_id(0), pl.program_id(1)

        @pl.when(i == 0)
        def _():
            dsink_ref[...] = jnp.zeros_like(dsink_ref)

        (d_c, ok_c), other = _swa_geometry(tb, w, i > 0)
        d_p, ok_p = other(tb)
        d_n, ok_n = _swa_geometry(tb, w, i < nb - 1)[1](w)
        kc, kp, vc, vp = kc_ref[...], kp_ref[...], vc_ref[...], vp_ref[...]
        k_last, v_last = kc[tb - w:, :], vc[tb - w:, :]
        dk_acc = jnp.zeros((tb, d), F32)
        dv_acc = jnp.zeros((d, tb), F32)
        dk_tail = jnp.zeros((w, d), F32)
        dv_tail = jnp.zeros((d, w), F32)
        for g in range(grp):
            h = kvh * grp + g
            slope, sink = ss_ref[0, h], ss_ref[1, h]
            qg, dog, lse_c = qc_ref[g], doc_ref[g], lc_ref[g]
            delta = jnp.sum(dog.astype(F32) * oc_ref[g].astype(F32), axis=0, keepdims=True)
            p_c = jnp.exp(_swa_scores_t(kc, qg, d_c, ok_c, scale=scale, slope=slope) - lse_c)
            p_p = jnp.exp(_swa_scores_t(kp, qg, d_p, ok_p, scale=scale, slope=slope) - lse_c)
            ds_c = ((p_c * (jnp.dot(vc, dog, preferred_element_type=F32) - delta)) * scale).astype(BF16)
            ds_p = ((p_p * (jnp.dot(vp, dog, preferred_element_type=F32) - delta)) * scale).astype(BF16)
            dq_ref[g] = (lax.dot_general(ds_c, kc, TN, preferred_element_type=F32)
                         + lax.dot_general(ds_p, kp, TN, preferred_element_type=F32))
            dk_acc += jnp.dot(ds_c, qg, preferred_element_type=F32)
            dv_acc += lax.dot_general(dog, p_c.astype(BF16), NT, preferred_element_type=F32)
            dsink_ref[g] -= jnp.broadcast_to(jnp.sum(jnp.exp(sink - lse_c) * delta, axis=1, keepdims=True), (1, LANES))
            qn, don = qn_ref[g], don_ref[g]
            delta_n = jnp.sum(don.astype(F32) * on_ref[g].astype(F32), axis=0, keepdims=True)
            p_n = jnp.exp(_swa_scores_t(k_last, qn, d_n, ok_n, scale=scale, slope=slope) - ln_ref[g])
            ds_n = ((p_n * (jnp.dot(v_last, don, preferred_element_type=F32) - delta_n)) * scale).astype(BF16)
            dk_tail += jnp.dot(ds_n, qn, preferred_element_type=F32)
            dv_tail += lax.dot_general(don, p_n.astype(BF16), NT, preferred_element_type=F32)
        dk_ref[...] = dk_acc
        dvt_ref[...] = dv_acc
        dk_ref[tb - w:, :] += dk_tail
        dvt_ref[:, tb - w:] += dv_tail

    def prev(i):
        return jnp.maximum(i * per - 1, 0)

    def nxt(i):
        return jnp.minimum((i + 1) * per, s // w - 1)

    return _call_with_rider(
        body, rider, name=name, grid=(nkv, nb), scratch_shapes=[],
        args=(q, q, k, k, v, v, ot, ot, dot_, dot_, lse, lse, slopes_sinks),
        in_specs=[pl.BlockSpec((grp, tb, d), lambda kh, i: (kh, i, 0)),
                  pl.BlockSpec((grp, w, d), lambda kh, i: (kh, nxt(i), 0)),
                  pl.BlockSpec((None, tb, d), lambda kh, i: (kh, i, 0)),
                  pl.BlockSpec((None, w, d), lambda kh, i: (kh, prev(i), 0)),
                  pl.BlockSpec((None, tb, d), lambda kh, i: (kh, i, 0)),
                  pl.BlockSpec((None, w, d), lambda kh, i: (kh, prev(i), 0)),
                  pl.BlockSpec((grp, d, tb), lambda kh, i: (kh, 0, i)),
                  pl.BlockSpec((grp, d, w), lambda kh, i: (kh, 0, nxt(i))),
                  pl.BlockSpec((grp, d, tb), lambda kh, i: (kh, 0, i)),
                  pl.BlockSpec((grp, d, w), lambda kh, i: (kh, 0, nxt(i))),
                  pl.BlockSpec((grp, 1, tb), lambda kh, i: (kh, 0, i)),
                  pl.BlockSpec((grp, 1, w), lambda kh, i: (kh, 0, nxt(i))),
                  pl.BlockSpec(memory_space=pltpu.SMEM)],
        out_specs=[pl.BlockSpec((grp, tb, d), lambda kh, i: (kh, i, 0)),
                   pl.BlockSpec((None, tb, d), lambda kh, i: (kh, i, 0)),
                   pl.BlockSpec((None, d, tb), lambda kh, i: (kh, 0, i)),
                   pl.BlockSpec((None, grp, 1, LANES), lambda kh, i: (kh, 0, 0, 0))],
        out_shape=[jax.ShapeDtypeStruct((nh, s, d), F32), jax.ShapeDtypeStruct((nkv, s, d), F32),
                   jax.ShapeDtypeStruct((nkv, d, s), F32), jax.ShapeDtypeStruct((nkv, grp, 1, LANES), F32)],
        compiler_params=_cp("arbitrary", "arbitrary"))


def _adamw(w, g, m, v, *, name):
    shape = w.shape
    cols = shape[-1]
    rows = int(np.prod(shape[:-1])) if len(shape) > 1 else 1
    tr = _row_tile(rows, cols)
    c1 = 1.0 - ADAM_B1 ** ADAM_STEP
    c2 = 1.0 - ADAM_B2 ** ADAM_STEP

    def body(w_ref, g_ref, m_ref, v_ref, d_ref, mo_ref, vo_ref):
        gv = g_ref[...]
        mn = ADAM_B1 * m_ref[...] + (1.0 - ADAM_B1) * gv
        vn = ADAM_B2 * v_ref[...] + (1.0 - ADAM_B2) * (gv * gv)
        mo_ref[...] = mn
        vo_ref[...] = vn
        d_ref[...] = -ADAM_LR * ((mn / c1) / (jnp.sqrt(vn / c2) + ADAM_EPS) + ADAM_WD * w_ref[...])

    blk = pl.BlockSpec((tr, cols), lambda i: (i, 0))
    outs = pl.pallas_call(
        body, name=name, grid=(rows // tr,), in_specs=[blk] * 4, out_specs=[blk] * 3,
        out_shape=[jax.ShapeDtypeStruct((rows, cols), F32)] * 3,
        compiler_params=_cp("arbitrary"))(*[a.reshape(rows, cols) for a in (w, g, m, v)])
    return tuple(a.reshape(shape) for a in outs)


def _hbm_spec():
    return pl.BlockSpec(memory_space=pl.ANY)


def _mesh_place():
    x, y, c = lax.axis_index("x"), lax.axis_index("y"), lax.axis_index("c")
    return x, y, c, [(1 - x, y), (x, 1 - y), (1 - x, 1 - y)]


def _half_rows(c, rows, align):
    return pl.ds(pl.multiple_of(c * (rows // 2), align), rows // 2)


def _part(ref, mode, k, n, rows=None):
    if mode == "cols":
        cols = pl.ds(pl.multiple_of(k * n, LANES), n)
        return ref.at[:, cols] if rows is None else ref.at[rows, cols]
    return ref.at[k] if rows is None else ref.at[k, rows, :]


class _Rider:
    def __init__(self, inputs, out_shape, n_sems, start, finish):
        self.inputs, self.out_shape, self.n_sems, self.start, self.finish = inputs, out_shape, n_sems, start, finish


def _call_with_rider(body, rider, *, name, grid, in_specs, out_specs, out_shape, scratch_shapes, compiler_params, args):
    if rider is None:
        outs = pl.pallas_call(body, name=name, grid=grid, in_specs=in_specs, out_specs=out_specs, out_shape=out_shape,
                              scratch_shapes=scratch_shapes, compiler_params=compiler_params)(*args)
        return outs, []
    n_in, n_out, n_sc = len(in_specs), len(out_specs), len(scratch_shapes)
    n_rin, n_rout = len(rider.inputs), len(rider.out_shape)

    def wrapped(*refs):
        pos = 0
        groups = []
        for n in (n_in, n_rin, n_out, n_rout, n_sc, 2):
            groups.append(refs[pos:pos + n])
            pos += n
        ins, rins, outs, routs, scratch, sems = groups
        ids = [pl.program_id(a) for a in range(len(grid))]
        first = functools.reduce(jnp.logical_and, [i == 0 for i in ids])
        last = functools.reduce(jnp.logical_and, [i == g - 1 for i, g in zip(ids, grid)])
        pl.when(first)(lambda: rider.start(rins, routs, *sems))
        body(*ins, *outs, *scratch)
        pl.when(last)(lambda: rider.finish(rins, routs, *sems))

    outs = pl.pallas_call(
        wrapped, name=name, grid=grid, in_specs=list(in_specs) + [_hbm_spec()] * n_rin,
        out_specs=list(out_specs) + [_hbm_spec()] * n_rout, out_shape=list(out_shape) + list(rider.out_shape),
        scratch_shapes=list(scratch_shapes) + [pltpu.SemaphoreType.DMA((rider.n_sems,))] * 2,
        compiler_params=compiler_params)(*args, *rider.inputs)
    return outs[:n_out], outs[n_out:]


def _run_rider(rider, *, name):
    n_rin = len(rider.inputs)

    def body(*refs):
        rins, routs, sems = refs[:n_rin], refs[n_rin:-2], refs[-2:]
        rider.start(rins, routs, *sems)
        rider.finish(rins, routs, *sems)

    return pl.pallas_call(
        body, name=name, in_specs=[_hbm_spec()] * n_rin, out_specs=[_hbm_spec()] * len(rider.out_shape),
        out_shape=rider.out_shape, scratch_shapes=[pltpu.SemaphoreType.DMA((rider.n_sems,))] * 2)(*rider.inputs)


def _gather_rider(shards, modes):
    n_arr = len(shards)
    out_shape = [jax.ShapeDtypeStruct((s.shape[0], N_CHIPS * s.shape[1]) if m == "cols" else (N_CHIPS,) + s.shape, s.dtype)
                 for s, m in zip(shards, modes)]
    per = 4

    def copies(srcs, dsts, send_sems, recv_sems):
        x, y, c, chips = _mesh_place()
        me = 2 * x + y
        sends, waits = [], []
        for i in range(n_arr):
            r, n = shards[i].shape
            rows = _half_rows(c, r, 16)

            def copy(slot, src, dst, to, i=i):
                return pltpu.make_async_remote_copy(src_ref=src, dst_ref=dst, send_sem=send_sems.at[i * per + slot],
                                                    recv_sem=recv_sems.at[i * per + slot], device_id=to, device_id_type=MESH)

            own = _part(dsts[i], modes[i], me, n)
            sends.append(copy(0, srcs[i], own, (x, y, 1 - c)))
            waits.append(copy(0, own, own, (x, y, 1 - c)))
            for j, (px, py) in enumerate(chips):
                sends.append(copy(1 + j, srcs[i].at[rows], _part(dsts[i], modes[i], me, n, rows), (px, py, c)))
                theirs = _part(dsts[i], modes[i], 2 * px + py, n, rows)
                waits.append(copy(1 + j, theirs, theirs, (px, py, c)))
        return sends, waits

    def start(*refs):
        for cp in copies(*refs)[0]:
            cp.start()

    def finish(*refs):
        sends, waits = copies(*refs)
        for cp in waits:
            cp.wait_recv()
        for cp in sends:
            cp.wait_send()

    return _Rider(list(shards), out_shape, per * n_arr, start, finish)


def _gather_forward(dsts, shard_shapes, modes, *, name):
    n_arr = len(dsts)

    def body(*refs):
        outs = refs[n_arr:2 * n_arr]
        send_sems, recv_sems = refs[2 * n_arr:]
        x, y, c, chips = _mesh_place()
        cps = []
        for i in range(n_arr):
            r, n = shard_shapes[i]
            for j, (px, py) in enumerate(chips):
                def view(hc, i=i, px=px, py=py, r=r, n=n):
                    return _part(outs[i], modes[i], 2 * px + py, n, _half_rows(hc, r, 16))

                def copy(ref, i=i, j=j):
                    return pltpu.make_async_remote_copy(src_ref=ref, dst_ref=ref, send_sem=send_sems.at[3 * i + j],
                                                        recv_sem=recv_sems.at[3 * i + j], device_id=(x, y, 1 - c), device_id_type=MESH)

                cps.append((copy(view(c)), copy(view(1 - c))))
        for send, _ in cps:
            send.start()
        for send, theirs in cps:
            theirs.wait_recv()
            send.wait_send()

    return pl.pallas_call(
        body, name=name, in_specs=[_hbm_spec()] * n_arr, out_specs=[_hbm_spec()] * n_arr,
        out_shape=[jax.ShapeDtypeStruct(d.shape, d.dtype) for d in dsts],
        input_output_aliases={i: i for i in range(n_arr)},
        scratch_shapes=[pltpu.SemaphoreType.DMA((3 * n_arr,)), pltpu.SemaphoreType.DMA((3 * n_arr,))])(*dsts)


def _blk_view(a, mode):
    return a[None] if mode == "cols" else a


def _swap_rider(arrs, modes):
    n_arr = len(arrs)
    out_shape = [jax.ShapeDtypeStruct((a.shape[0] // 2, a.shape[1]) if m == "cols" else (a.shape[0], a.shape[1] // 2, a.shape[2]), a.dtype)
                 for a, m in zip(arrs, modes)]

    def copies(srcs, dsts, send_sems, recv_sems):
        x, y, c, _ = _mesh_place()
        cps = []
        for i in range(n_arr):
            if modes[i] == "cols":
                src = srcs[i].at[_half_rows(1 - c, arrs[i].shape[0], 8)]
            else:
                src = srcs[i].at[:, _half_rows(1 - c, arrs[i].shape[1], 8), :]
            cps.append(pltpu.make_async_remote_copy(src_ref=src, dst_ref=dsts[i], send_sem=send_sems.at[i],
                                                    recv_sem=recv_sems.at[i], device_id=(x, y, 1 - c), device_id_type=MESH))
        return cps

    def start(*refs):
        for cp in copies(*refs):
            cp.start()

    def finish(*refs):
        for cp in copies(*refs):
            cp.wait()

    return _Rider(list(arrs), out_shape, n_arr, start, finish)


def _rs_pair_add(arr, landed, place, *, name):
    nb, r, c = arr.shape
    rh = r // 2
    tr = _row_tile(rh, c)
    nt = rh // tr

    def body(p_ref, a_ref, l_ref, o_ref):
        o_ref[...] = (a_ref[...] + l_ref[...]).astype(BF16)

    grid_spec = pltpu.PrefetchScalarGridSpec(
        num_scalar_prefetch=1, grid=(nb, nt),
        in_specs=[pl.BlockSpec((None, tr, c), lambda b, t, p_ref: (b, p_ref[1] * nt + t, 0)),
                  pl.BlockSpec((None, tr, c), lambda b, t, p_ref: (b, t, 0))],
        out_specs=pl.BlockSpec((None, tr, c), lambda b, t, p_ref: (b, t, 0)))
    return pl.pallas_call(
        body, name=name, grid_spec=grid_spec, out_shape=jax.ShapeDtypeStruct((nb, rh, c), BF16),
        compiler_params=_cp("arbitrary", "arbitrary"))(place, arr, landed)


def _exchange_rider(parts, modes):
    n_arr = len(parts)
    out_shape = []
    for a, m in zip(parts, modes):
        shp = (a.shape[0], a.shape[1] // N_CHIPS) if m == "cols" else a.shape[1:]
        out_shape.append(jax.ShapeDtypeStruct((3,) + shp, a.dtype))

    def copies(srcs, dsts, send_sems, recv_sems):
        x, y, c, chips = _mesh_place()
        cps = []
        for i in range(n_arr):
            n = out_shape[i].shape[-1]
            for j, (px, py) in enumerate(chips):
                cps.append(pltpu.make_async_remote_copy(
                    src_ref=_part(srcs[i], modes[i], 2 * px + py, n), dst_ref=dsts[i].at[j],
                    send_sem=send_sems.at[3 * i + j], recv_sem=recv_sems.at[3 * i + j],
                    device_id=(px, py, c), device_id_type=MESH))
        return cps

    def start(*refs):
        for cp in copies(*refs):
            cp.start()

    def finish(*refs):
        for cp in copies(*refs):
            cp.wait()

    return _Rider(list(parts), out_shape, 3 * n_arr, start, finish)


def _rs_chip_sum(part, landed, mode, place, *, name):
    _, rh, n = landed.shape
    tr = _row_tile(rh, n)
    nt = rh // tr

    def body(p_ref, a_ref, l_ref, o_ref):
        o_ref[...] = ((a_ref[...].astype(F32) + l_ref[0].astype(F32)) + l_ref[1].astype(F32)) + l_ref[2].astype(F32)

    if mode == "cols":
        own = pl.BlockSpec((tr, n), lambda t, p_ref: (t, p_ref[0]))
    else:
        own = pl.BlockSpec((None, tr, n), lambda t, p_ref: (p_ref[0], t, 0))
    grid_spec = pltpu.PrefetchScalarGridSpec(
        num_scalar_prefetch=1, grid=(nt,),
        in_specs=[own, pl.BlockSpec((3, tr, n), lambda t, p_ref: (0, t, 0))],
        out_specs=pl.BlockSpec((tr, n), lambda t, p_ref: (p_ref[1] * nt + t, 0)))
    return pl.pallas_call(
        body, name=name, grid_spec=grid_spec, out_shape=jax.ShapeDtypeStruct((2 * rh, n), F32),
        compiler_params=_cp("arbitrary"))(place, part, landed)


def _rs_pair_join(halves, *, name):
    n_arr = len(halves)

    def body(*refs):
        outs = refs[n_arr:2 * n_arr]
        send_sems, recv_sems = refs[2 * n_arr:]
        x, y, c, _ = _mesh_place()
        cps = []
        for i in range(n_arr):
            rows = _half_rows(c, halves[i].shape[0], 8)
            cps.append(pltpu.make_async_remote_copy(src_ref=outs[i].at[rows], dst_ref=outs[i].at[rows], send_sem=send_sems.at[i],
                                                    recv_sem=recv_sems.at[i], device_id=(x, y, 1 - c), device_id_type=MESH))
        for cp in cps:
            cp.start()
        for i, cp in enumerate(cps):
            cp.wait_send()
            theirs = outs[i].at[_half_rows(1 - c, halves[i].shape[0], 8)]
            pltpu.make_async_remote_copy(src_ref=theirs, dst_ref=theirs, send_sem=send_sems.at[i], recv_sem=recv_sems.at[i],
                                         device_id=(x, y, 1 - c), device_id_type=MESH).wait_recv()

    return pl.pallas_call(
        body, name=name, in_specs=[_hbm_spec()] * n_arr, out_specs=[_hbm_spec()] * n_arr,
        out_shape=[jax.ShapeDtypeStruct(h.shape, h.dtype) for h in halves],
        input_output_aliases={i: i for i in range(n_arr)},
        scratch_shapes=[pltpu.SemaphoreType.DMA((n_arr,)), pltpu.SemaphoreType.DMA((n_arr,))])(*halves)


def _allreduce_small(v, *, name):
    r, c = v.shape

    def body(v_ref, o_ref, gath, send_sems, recv_sems):
        x, y, cc, _ = _mesh_place()
        me = 4 * x + 2 * y + cc
        gath[me] = v_ref[...]
        cps = []
        for rel in range(1, 8):
            px = 1 - x if rel & 4 else x
            py = 1 - y if rel & 2 else y
            pc = 1 - cc if rel & 1 else cc

            def copy(slot, px=px, py=py, pc=pc, rel=rel):
                return pltpu.make_async_remote_copy(
                    src_ref=v_ref, dst_ref=gath.at[slot], send_sem=send_sems.at[rel - 1],
                    recv_sem=recv_sems.at[rel - 1], device_id=(px, py, pc), device_id_type=MESH)

            cps.append((copy(me), copy(4 * px + 2 * py + pc)))
        for send, _ in cps:
            send.start()
        for send, theirs in cps:
            theirs.wait_recv()
            send.wait_send()
        tot = gath[0]
        for d in range(1, 8):
            tot = tot + gath[d]
        o_ref[...] = tot

    vm = pl.BlockSpec(memory_space=pltpu.VMEM)
    return pl.pallas_call(
        body, name=name, in_specs=[vm], out_specs=vm, out_shape=jax.ShapeDtypeStruct((r, c), F32),
        scratch_shapes=[pltpu.VMEM((8, r, c), F32), pltpu.SemaphoreType.DMA((7,)), pltpu.SemaphoreType.DMA((7,))])(v)


def _rope_tables(s, reps):
    half = B_ROPE // 2
    inv = ROPE_THETA ** (-jnp.arange(0, B_ROPE, 2, dtype=F32) / B_ROPE)
    ang = jnp.arange(s, dtype=F32)[:, None] * inv[None, :]
    return jnp.tile(jnp.cos(ang), (1, reps)), jnp.tile(jnp.sin(ang), (1, reps))


def _alibi_slopes():
    return 2.0 ** (-8.0 * jnp.arange(1, A_HEADS + 1, dtype=F32) / A_HEADS)


def _ffn_fwd(h, norm, wts, tag, rider=None, on_rode=None):
    (dact_dgate, dact_dup, act, xn), rode = _ffn_up(h, norm, wts["wgu"], name=f"{tag}_up", rider=rider)
    if on_rode is not None:
        on_rode(rode)
    out = _mm_res_fwd(act, wts["wd"], h, scale=FFN_RES_SCALE, name=f"{tag}_down")
    return out, dict(h_in=h, dact_dgate=dact_dgate, dact_dup=dact_dup, act=act, xn=xn), rode


def _ffn_bwd(dh, norm, wts, sv, tag, rider=None, own=None):
    (dgate, dup), rode = _ffn_down_bwd(dh, wts["wd"], sv["dact_dgate"], sv["dact_dup"], scale=FFN_RES_SCALE,
                                      name=f"{tag}_down_bwd", rider=rider)
    d_wd = _mm_tn(sv["act"], dh, b_scale=FFN_RES_SCALE, name=f"{tag}_dwd")
    pairs = [(dgate, wts["wgu"], 0), (dup, wts["wgu"], 1)]
    if own is None:
        d_wgu = _mm_tn(sv["xn"], [dgate, dup], name=f"{tag}_dwgu")
        dh_in, dnorm = _mm_nt_rmsbwd(pairs, sv["h_in"], norm, dh, name=f"{tag}_dx")
    else:
        wd_ready, wgu_ready, done = own
        first = wd_ready(d_wd)
        res = _mm_tn(sv["xn"], [dgate, dup], name=f"{tag}_dwgu", rider=first)
        d_wgu, brought = (res, []) if first is None else res
        second = wgu_ready(brought, d_wgu)
        res = _mm_nt_rmsbwd(pairs, sv["h_in"], norm, dh, name=f"{tag}_dx", rider=second)
        dh_in, dnorm, brought = (*res, []) if second is None else res
        done(brought)
    return dh_in, dnorm, d_wgu, d_wd, rode


def _even_weights(w_in, w_uq, w_ukv):
    half = B_ROPE // 2
    base = w_in.shape[1]
    kr1, kr2 = w_in[:, base - B_ROPE:base - half], w_in[:, base - half:]
    w_in_cat = jnp.concatenate([w_in, -kr2, kr1, jnp.zeros((w_in.shape[0], 64), w_in.dtype)], axis=1)
    u3 = w_uq.reshape(w_uq.shape[0], B_HEADS, B_NOPE + B_ROPE)
    nope = u3[:, :, :B_NOPE].reshape(w_uq.shape[0], -1)
    rot = u3[:, :, B_NOPE:].reshape(w_uq.shape[0], -1)
    swapped = jnp.concatenate([-u3[:, :, B_NOPE + half:], u3[:, :, B_NOPE:B_NOPE + half]], axis=-1).reshape(w_uq.shape[0], -1)
    return w_in_cat, jnp.concatenate([nope, rot, swapped], axis=1), w_ukv


def _even_fwd(h, w, i, rider=None):
    s = h.shape[0]
    qa, ka, va, vat, c_q, c_kv, kr_blk, xn = _ev_in_fwd(h, w["mix_norm"][i:i + 1], w["ev_in_cat"], name="ev_in")
    cos32, sin32 = _rope_tables(s, 2)
    kro = _rope_fwd(kr_blk[:, :B_ROPE], kr_blk[:, B_ROPE:2 * B_ROPE], cos32, sin32, name="ev_k_rope")
    ss = jnp.stack([_alibi_slopes(), w["ev_sinks"].reshape(-1)])
    oa, lse_a = _swa_fwd_t(qa, ka, vat, ss, scale=A_HEAD_DIM ** -0.5, window=WINDOW, name="swa_fwd")
    cos256, sin256 = _rope_tables(s, 2 * B_HEADS)
    qb, xn_q = _ev_q_fwd(c_q, w["ev_cq_norm"], w["ev_q_cat"], cos256, sin256, name="ev_q_up")
    kb, vb, vbt, xn_kv = _ev_kv_fwd(c_kv, w["ev_ckv_norm"], w["ev_ukv"], kro, name="ev_kv_up")
    (ob, lse_b), rode = _causal_fwd_t(qb, kb, vbt, scale=(B_NOPE + B_ROPE) ** -0.5, name="mla_fwd", tq=512, hb=4, rider=rider)
    attn = jnp.concatenate([oa.reshape(-1, s), ob.reshape(-1, s)], axis=0)
    out = _mm_res_fwd(attn, w["ev_out"], h, scale=1.0, name="ev_out", a_t=True)
    sv = dict(h_in=h, xn=xn, c_q=c_q, c_kv=c_kv, xn_q=xn_q, xn_kv=xn_kv, qa=qa, ka=ka, va=va, oa=oa, lse_a=lse_a,
              ss=ss, qb=qb, kb=kb, vb=vb, ob=ob, lse_b=lse_b, attn=attn, cos32=cos32, sin32=sin32,
              cos256=cos256, sin256=sin256)
    return out, sv, rode


def _even_bwd(dh, w, sv, i, rider=None):
    s = dh.shape[0]
    half = B_ROPE // 2
    g = {}
    dattn = _mm_nt_t(dh, w["ev_out"], name="ev_out_dx")
    g["ev_w_out"] = _mm_tn(sv["attn"], dh, name="ev_out_dw", a_t=True)
    doa = dattn[:A_HEADS * A_HEAD_DIM].reshape(A_HEADS, A_HEAD_DIM, s)
    dob = dattn[A_HEADS * A_HEAD_DIM:].reshape(B_HEADS, B_V, s)
    first, then = rider if isinstance(rider, tuple) else (None, None)
    (dqa, dka, dva, dsink), brought = _swa_bwd_t(sv["qa"], sv["ka"], sv["va"], sv["oa"], doa, sv["lse_a"], sv["ss"],
                                                 scale=A_HEAD_DIM ** -0.5, window=WINDOW, name="swa_bwd", rider=first)
    if then is not None:
        rider = then(brought)
    g["ev_sinks"] = dsink[:, :, 0, 0].reshape(1, A_HEADS)
    (dqb, dkb, dvb), rode = _causal_bwd_t(sv["qb"], sv["kb"], sv["vb"], sv["ob"], dob, sv["lse_b"],
                                          scale=(B_NOPE + B_ROPE) ** -0.5, name="mla_bwd", tq=512, hb=4, rider=rider)
    dyq = _ev_q_merge(dqb, sv["cos256"], sv["sin256"], name="ev_q_merge")
    dwq = _mm_tn(sv["xn_q"], dyq, name="ev_q_up_dw")
    dcq, g["ev_cq_norm"] = _mm_nt_rmsbwd([(dyq, w["ev_q_cat"])], sv["c_q"], w["ev_cq_norm"], None, name="ev_q_up_dx")
    kq = sv["c_q"].shape[1]
    d_nope = dwq[:, :512].reshape(kq, B_HEADS, B_NOPE)
    d_rot = dwq[:, 512:768].reshape(kq, B_HEADS, B_ROPE)
    d_swp = dwq[:, 768:].reshape(kq, B_HEADS, B_ROPE)
    g["ev_w_uq"] = jnp.concatenate([d_nope, d_rot[:, :, :half] + d_swp[:, :, half:], d_rot[:, :, half:] - d_swp[:, :, :half]],
                                   axis=-1).reshape(kq, -1)
    dykv, dkr = _ev_kv_merge(dkb, dvb, sv["cos32"], sv["sin32"], name="ev_kv_merge")
    g["ev_w_ukv"] = _mm_tn(sv["xn_kv"], dykv, name="ev_kv_up_dw")
    dckv, g["ev_ckv_norm"] = _mm_nt_rmsbwd([(dykv, w["ev_ukv"])], sv["c_kv"], w["ev_ckv_norm"], None, name="ev_kv_up_dx")
    dycat = _ev_in_merge(dqa, dka, dva, dcq, dckv, dkr, name="ev_in_merge")
    dwin = _mm_tn(sv["xn"], dycat, name="ev_in_dw")
    base = 1184
    g["ev_w_in"] = jnp.concatenate([dwin[:, :base - B_ROPE],
                                    dwin[:, base - B_ROPE:base - half] + dwin[:, base + half:base + B_ROPE],
                                    dwin[:, base - half:base] - dwin[:, base:base + half]], axis=-1)
    dh_in, dnorm = _mm_nt_rmsbwd([(dycat, w["ev_in_cat"])], sv["h_in"], w["mix_norm"][i:i + 1], dh, name="ev_in_dx")
    return dh_in, dnorm, g, rode


def _odd_fwd(h, w, i, rider=None):
    s = h.shape[0]
    wd = C_HEADS * C_HEAD_DIM
    q, k, v, vt, y_f, xn = _fox_in_fwd(h, w["mix_norm"][i:i + 1], w["od_in_pad"], nheads=C_HEADS, dh=C_HEAD_DIM,
                                       q_ones=(0, 2, 3, 4), k_ones=(1,), name="od_in")
    scale = C_HEAD_DIM ** -0.5
    ft = y_f[:, :C_HEADS].T
    bf = w["od_b_f"].reshape(C_HEADS, 1)
    cb3 = _fox_gate_fwd(ft, bf, out_scale=-1.0 / scale, name="fox_gate_fwd")
    k = k + jnp.pad(cb3.transpose(1, 2, 0), ((0, 0), (0, 0), (C_HEAD_DIM + 2, LANES - C_HEAD_DIM - 5)))
    (o, lse), rode = _causal_fwd_t(q, k, vt, scale=scale, name="fox_fwd", tq=512, hb=4, rider=rider)
    attn = o.reshape(-1, s)
    out = _mm_res_fwd(attn, w["od_out"], h, scale=1.0, name="od_out", a_t=True)
    return out, dict(h_in=h, xn=xn, q=q, k=k, v=v, o=o, lse=lse, ft=ft, bf=bf, attn=attn), rode


def _odd_bwd(dh, w, sv, i, rider=None):
    s = dh.shape[0]
    g = {}
    dattn = _mm_nt_t(dh, w["od_out"], name="od_out_dx")
    g["od_w_out"] = _mm_tn(sv["attn"], dh, name="od_out_dw", a_t=True)
    do = dattn.reshape(C_HEADS, C_HEAD_DIM, s)
    scale = C_HEAD_DIM ** -0.5
    (dq, dk, dv), rode = _causal_bwd_t(sv["q"], sv["k"], sv["v"], sv["o"], do, sv["lse"], scale=scale, name="fox_bwd",
                                       tq=512, hb=4, rider=rider)
    dqkv, sums = _merge_heads(dq, dk, dv, dh=C_HEAD_DIM, q_col=C_HEAD_DIM + 1, k_col=C_HEAD_DIM, name="fox_merge")
    dft, dbf = _fox_gate_bwd(sums[:, :C_HEADS].T, sums[:, C_HEADS:2 * C_HEADS].T, sv["ft"], sv["bf"],
                             inv_scale=1.0 / scale, name="fox_gate_bwd")
    g["od_b_f"] = dbf.reshape(1, C_HEADS)
    wd = C_HEADS * C_HEAD_DIM
    df = jnp.pad(dft.T, ((0, 0), (0, LANES - C_HEADS)))
    g["od_w_in"] = jnp.concatenate([_mm_tn(sv["xn"], dqkv, name="od_in_dw"),
                                    _mm_tn(sv["xn"], df, name="od_in_dwf")[:, :C_HEADS]], axis=-1)
    dh_in, dnorm = _mm_nt_rmsbwd([(dqkv, w["od_in_pad"], 0), (df, w["od_in_pad"], 3 * wd // LANES)],
                                 sv["h_in"], w["mix_norm"][i:i + 1], dh, name="od_in_dx")
    return dh_in, dnorm, g, rode


def _kernel_weights(full, replicated):
    w = dict(replicated)
    _install_weights(w, {(n, i): a for n, per_layer in full.items() for i, a in enumerate(per_layer)})
    return w


def _install_weights(w, got):
    raw = w.setdefault("raw", {})
    raw.update(got)
    for (n, i), a in got.items():
        if n in ("ffa_w_gate_up", "ffa_w_down", "ffb_w_gate_up", "ffb_w_down"):
            w.setdefault(n[:3], {}).setdefault(i, {})["wgu" if n.endswith("gate_up") else "wd"] = a
        elif n in ("ple_w_gate", "ple_w_proj"):
            w.setdefault("ple_gate" if n.endswith("gate") else "ple_proj", {})[i] = a
    if "ev_in_cat" not in w and all((n, 0) in raw for n in ("ev_w_in", "ev_w_uq", "ev_w_ukv", "ev_w_out")):
        w["ev_in_cat"], w["ev_q_cat"], w["ev_ukv"] = _even_weights(raw["ev_w_in", 0], raw["ev_w_uq", 0], raw["ev_w_ukv", 0])
        w["ev_out"] = raw["ev_w_out", 0]
    if "od_in_pad" not in w and all((n, 0) in raw for n in ("od_w_in", "od_w_out")):
        od_in = raw["od_w_in", 0]
        w["od_in_pad"] = jnp.pad(od_in, ((0, 0), (0, (-od_in.shape[1]) % LANES)))
        w["od_out"] = raw["od_w_out", 0]


def _keys(names, layer):
    return tuple((n, layer) for n in names)


_FFA, _FFB, _PLE = ("ffa_w_gate_up", "ffa_w_down"), ("ffb_w_gate_up", "ffb_w_down"), ("ple_w_gate", "ple_w_proj")
_EV, _OD = ("ev_w_in", "ev_w_uq", "ev_w_ukv", "ev_w_out"), ("od_w_in", "od_w_out")
_GATHER_FIRST = _keys(_FFA[:1], 0)
_GATHER_RIDES = {("ffa", 0): _keys(_FFA[1:] + _EV, 0), ("mix", 0): _keys(_FFB + _PLE, 0) + _keys(_FFA, 1),
                 ("ffb", 0): _keys(_OD, 0), ("mix", 1): _keys(_FFB + _PLE, 1)}
_REDUCE_RIDES = {("mix", 1): _keys(_FFB + _PLE, 1), ("mix", 0): _keys(_FFA, 1) + _keys(_OD, 0) + _keys(_FFB + _PLE, 0),
                 ("ffa", 0): _keys(_EV, 0)}
_REDUCE_OWN = ("ffa", 0)
_SWAP_AHEAD = {("ffb", 1): ("mix", 1)}


def _local_step(x, p, tgt, w, ex=None):
    depth = p.shape[0]

    def gather_behind(host, fn, *args):
        keys = None if ex is None else _GATHER_RIDES.get(host)
        if keys is None:
            return fn(*args, None)[:-1]
        done = []

        def install(rode):
            if not done:
                _install_weights(w, ex.gather_finish(keys, rode, name=f"weight_forward_{host[0]}{host[1]}"))
                done.append(True)

        res = fn(*args, ex.gather_rider(keys), install) if fn is _ffn_fwd else fn(*args, ex.gather_rider(keys))
        install(res[-1])
        return res[:-1]

    h = x
    saved = []
    for i in range(depth):
        sv = {}
        h, sv["ffa"] = gather_behind(("ffa", i), _ffn_fwd, h, w["ffa_norm"][i:i + 1], w["ffa"][i], f"ffa{i}")
        h, sv["mix"] = gather_behind(("mix", i), _even_fwd if i % 2 == 0 else _odd_fwd, h, w, i)
        h, sv["ffb"] = gather_behind(("ffb", i), _ffn_fwd, h, w["ffb_norm"][i:i + 1], w["ffb"][i], f"ffb{i}")
        h_in = h
        h, xn, gate, pp = _ple_fwd(h, w["ple_norm"][i:i + 1], w["ple_gate"][i], p[i], w["ple_proj"][i], name=f"ple{i}")
        sv["ple"] = dict(h_in=h_in, xn=xn, gate=gate, pp=pp)
        saved.append(sv)
    loss_vec, dh, d_final = _final_loss(h, w["final_norm"].reshape(1, -1), tgt, name="final_loss")

    per_layer = [dict() for _ in range(depth)]
    mats = {}
    grads = {}

    pending = {}

    def reduce_behind(host, fn, *args):
        keys = None if ex is None else _REDUCE_RIDES.get(host)
        ahead = None if ex is None else _SWAP_AHEAD.get(host)
        if keys is None and ahead is None:
            return fn(*args, None)[:-1]
        if ahead is not None:
            got, ctxs = {}, []

            def note_wd(d_wd):
                got[f"{host[0]}_w_down", host[1]] = d_wd

            def swap_now(brought, d_wgu):
                got[f"{host[0]}_w_gate_up", host[1]] = d_wgu
                swap, ctx = ex.swap_rider(_REDUCE_RIDES[ahead], {**mats, **got})
                ctxs.append(ctx)
                return swap

            def stash(brought):
                pending[ahead] = ex.after_swap(ctxs[0], brought)

            return fn(*args, None, (note_wd, swap_now, stash))[:-1]
        states = []
        if fn is _even_bwd:
            swap, ctx = ex.swap_rider(keys, mats)

            def then(brought):
                states.append(ex.after_swap(ctx, brought))
                return states[0][0]

            res = fn(*args, (swap, then))
        else:
            states.append(pending.pop(host, None) or ex.reduce_begin(keys, mats, tag=f"{host[0]}{host[1]}"))
            if fn is _ffn_bwd and host == _REDUCE_OWN:
                own = []

                def wd_ready(d_wd):
                    own.append(ex.reduce_begin(_keys(_FFA[1:], 0), {("ffa_w_down", 0): d_wd}, tag="own_wd"))
                    return own[0][0]

                def wgu_ready(brought, d_wgu):
                    ex.reduce_finish(own[0], brought)
                    own.append(ex.reduce_begin(_keys(_FFA[:1], 0), {("ffa_w_gate_up", 0): d_wgu}, tag="own_wgu"))
                    return own[1][0]

                res = fn(*args, states[0][0], (wd_ready, wgu_ready, lambda brought: ex.reduce_finish(own[1], brought)))
            else:
                res = fn(*args, states[0][0])
        ex.reduce_finish(states[0], res[-1])
        return res[:-1]

    for i in reversed(range(depth)):
        sv, gl = saved[i], per_layer[i]
        dz, dpp = _ple_bwd_elem(dh, sv["ple"]["gate"], sv["ple"]["pp"], name=f"ple{i}_bwd")
        mats["ple_w_gate", i] = _mm_tn(sv["ple"]["xn"], dz, name=f"ple{i}_dwg")
        mats["ple_w_proj", i] = _mm_tn(p[i], dpp, name=f"ple{i}_dwp")
        dh, gl["ple_norm"] = _mm_nt_rmsbwd([(dz, w["ple_gate"][i])], sv["ple"]["h_in"], w["ple_norm"][i:i + 1], dh,
                                           name=f"ple{i}_dx")
        dh, gl["ffb_norm"], mats["ffb_w_gate_up", i], mats["ffb_w_down", i] = reduce_behind(
            ("ffb", i), _ffn_bwd, dh, w["ffb_norm"][i:i + 1], w["ffb"][i], sv["ffb"], f"ffb{i}")
        dh, gl["mix_norm"], gm = reduce_behind(("mix", i), _even_bwd if i % 2 == 0 else _odd_bwd, dh, w, sv["mix"], i)
        for n, g in gm.items():
            if n in REPLICATED:
                grads[n] = g
            else:
                mats[n, 0] = g
        dh, gl["ffa_norm"], mats["ffa_w_gate_up", i], mats["ffa_w_down", i] = reduce_behind(
            ("ffa", i), _ffn_bwd, dh, w["ffa_norm"][i:i + 1], w["ffa"][i], sv["ffa"], f"ffa{i}")
    grads["final_norm"] = d_final.reshape(-1)
    for n in ("ffa_norm", "mix_norm", "ffb_norm", "ple_norm"):
        grads[n] = jnp.concatenate([per_layer[i][n] for i in range(depth)], axis=0)
    if ex is None:
        for n, _ in SHARDED:
            grads[n] = [mats[n, i] for i in range(depth) if (n, i) in mats]
    return loss_vec[0, 0], dh, grads


def _cut_mode(local_shape, axis, ncols):
    return "cols" if axis == 2 and ncols % LANES == 0 else "blk"


class _Exchange:
    def __init__(self, wts):
        self.place = jnp.stack([2 * lax.axis_index("x") + lax.axis_index("y"), lax.axis_index("c")]).astype(jnp.int32)
        self.info = {}
        for n, axis in SHARDED:
            wb = wts[n].astype(BF16)
            mode = _cut_mode(wb.shape, axis, wb.shape[2])
            for i in range(wb.shape[0]):
                self.info[n, i] = dict(shard=wb[i], mode=mode, axis=axis)
        self.halves = {}

    def _modes(self, keys):
        return [self.info[k]["mode"] for k in keys]

    def gather_rider(self, keys):
        return _gather_rider([self.info[k]["shard"] for k in keys], self._modes(keys))

    def gather_finish(self, keys, landed, *, name):
        outs = _gather_forward(landed, [self.info[k]["shard"].shape for k in keys], self._modes(keys), name=name)
        got = {}
        for k, dst in zip(keys, outs):
            if self.info[k]["mode"] == "blk":
                dst = dst.reshape(-1, dst.shape[2]) if self.info[k]["axis"] == 1 else jnp.moveaxis(dst, 0, 1).reshape(dst.shape[1], -1)
            got[k] = dst
        return got

    def gather(self, keys, *, name):
        return self.gather_finish(keys, _run_rider(self.gather_rider(keys), name=name), name=name + "_forward")

    def swap_rider(self, keys, mats):
        modes = self._modes(keys)
        arrs = []
        for k in keys:
            g2, (rr, cc) = mats[k], self.info[k]["shard"].shape
            if self.info[k]["mode"] == "blk":
                g2 = g2.reshape(N_CHIPS, rr, cc) if self.info[k]["axis"] == 1 else g2.reshape(rr, N_CHIPS, cc).transpose(1, 0, 2)
            arrs.append(g2)
        return _swap_rider(arrs, modes), (keys, modes, arrs)

    def after_swap(self, ctx, landed):
        keys, modes, arrs = ctx
        parts = []
        for (n, i), m, a, l in zip(keys, modes, arrs, landed):
            pt = _rs_pair_add(_blk_view(a, m), _blk_view(l, m), self.place, name=f"rs_pair_add_{n}{i}")
            parts.append(pt[0] if m == "cols" else pt)
        return _exchange_rider(parts, modes), keys, parts

    def reduce_begin(self, keys, mats, *, tag):
        rider, ctx = self.swap_rider(keys, mats)
        return self.after_swap(ctx, _run_rider(rider, name=f"rs_pair_swap_{tag}"))

    def reduce_finish(self, state, landed):
        _, keys, parts = state
        for (n, i), m, pt, l in zip(keys, self._modes(keys), parts, landed):
            self.halves[n, i] = _rs_chip_sum(pt, l, m, self.place, name=f"rs_chip_sum_{n}{i}")

    def reduce(self, keys, mats, *, tag):
        state = self.reduce_begin(keys, mats, tag=tag)
        self.reduce_finish(state, _run_rider(state[0], name=f"rs_chip_exchange_{tag}"))

    def join(self, wts):
        keys = list(self.info)
        joined = dict(zip(keys, _rs_pair_join([self.halves[k] for k in keys], name="rs_pair_join")))
        return {n: jnp.stack([joined[n, i] for i in range(wts[n].shape[0])]).reshape(wts[n].shape) for n, _ in SHARDED}


def _small_rows(vals):
    rows = []
    for n in REPLICATED:
        v = vals[n].reshape(-1)
        rows.append(jnp.pad(v, (0, (-v.shape[0]) % FLAT_COLS)).reshape(-1, FLAT_COLS))
    out = jnp.concatenate(rows, axis=0)
    return jnp.pad(out, ((0, (-out.shape[0]) % 8), (0, 0)))


def kernel(x, p, ffa_norm, ffa_w_gate_up, ffa_w_down, mix_norm, ffb_norm, ffb_w_gate_up, ffb_w_down, ple_norm, ple_w_gate, ple_w_proj, ev_w_in, ev_sinks, ev_cq_norm, ev_w_uq, ev_ckv_norm, ev_w_ukv, ev_w_out, od_w_in, od_b_f, od_w_out, final_norm, loss_target, m_ffa_norm, m_ffa_w_gate_up, m_ffa_w_down, m_mix_norm, m_ffb_norm, m_ffb_w_gate_up, m_ffb_w_down, m_ple_norm, m_ple_w_gate, m_ple_w_proj, m_ev_w_in, m_ev_sinks, m_ev_cq_norm, m_ev_w_uq, m_ev_ckv_norm, m_ev_w_ukv, m_ev_w_out, m_od_w_in, m_od_b_f, m_od_w_out, m_final_norm, v_ffa_norm, v_ffa_w_gate_up, v_ffa_w_down, v_mix_norm, v_ffb_norm, v_ffb_w_gate_up, v_ffb_w_down, v_ple_norm, v_ple_w_gate, v_ple_w_proj, v_ev_w_in, v_ev_sinks, v_ev_cq_norm, v_ev_w_uq, v_ev_ckv_norm, v_ev_w_ukv, v_ev_w_out, v_od_w_in, v_od_b_f, v_od_w_out, v_final_norm):
    env = dict(locals())
    wts = {n: env[n] for n in WEIGHT_ORDER}
    mom1 = {n: env["m_" + n] for n in WEIGHT_ORDER}
    mom2 = {n: env["v_" + n] for n in WEIGHT_ORDER}
    ex = _Exchange(wts)

    w = {n: wts[n] for n in REPLICATED}
    _install_weights(w, ex.gather(_GATHER_FIRST, name="weight_gather_first"))

    loss_part, grad_x, grads = _local_step(x[0], p[:, 0], loss_target[0], w, ex)
    loss = lax.psum(loss_part, ("x", "y", "c"))
    gout = ex.join(wts)
    small = _allreduce_small(_small_rows(grads), name="small_allreduce")
    r0 = 0
    for n in REPLICATED:
        size = int(np.prod(wts[n].shape))
        nr = -(-size // FLAT_COLS)
        gout[n] = small[r0:r0 + nr].reshape(-1)[:size].reshape(wts[n].shape)
        r0 += nr

    delta, new_m, new_v = {}, {}, {}
    for n in WEIGHT_ORDER:
        delta[n], new_m[n], new_v[n] = _adamw(wts[n], gout[n], mom1[n], mom2[n], name="adamw_" + n)
    return (loss, grad_x[None], *[gout[n] for n in WEIGHT_ORDER], *[delta[n] for n in WEIGHT_ORDER],
            *[new_m[n] for n in WEIGHT_ORDER], *[new_v[n] for n in WEIGHT_ORDER])
```

```python
import functools
import math

import numpy as np
import jax
import jax.numpy as jnp
from jax import lax
from jax.experimental import pallas as pl
from jax.experimental.pallas import tpu as pltpu

F32 = jnp.float32
BF16 = jnp.bfloat16
NT = (((1,), (1,)), ((), ()))
TN = (((0,), (0,)), ((), ()))
MESH = pl.DeviceIdType.MESH

RMS_EPS = 1e-6
FFN_RES_SCALE = 0.5
A_HEADS, A_KV_HEADS, A_HEAD_DIM, WINDOW = 8, 2, 64, 128
B_HEADS, B_Q_LORA, B_KV_LORA, B_NOPE, B_ROPE, B_V = 8, 256, 128, 64, 32, 64
ROPE_THETA = 10000.0
C_HEADS, C_HEAD_DIM = 16, 64
ADAM_LR, ADAM_B1, ADAM_B2, ADAM_EPS, ADAM_WD, ADAM_STEP = 0.001, 0.9, 0.999, 1e-08, 0.01, 10

N_CHIPS = 4
LANES = 128
FLAT_COLS = 1024
MASK_VALUE = -1e30
VMEM_LIMIT = 48 * 2**20

SHARDED = (
    ("ffa_w_gate_up", 2), ("ffa_w_down", 1), ("ffb_w_gate_up", 2), ("ffb_w_down", 1),
    ("ple_w_gate", 1), ("ple_w_proj", 2), ("ev_w_in", 2), ("ev_w_uq", 2), ("ev_w_ukv", 2),
    ("ev_w_out", 1), ("od_w_in", 2), ("od_w_out", 1))
REPLICATED = ("ffa_norm", "mix_norm", "ffb_norm", "ple_norm", "final_norm",
              "ev_sinks", "ev_cq_norm", "ev_ckv_norm", "od_b_f")
WEIGHT_ORDER = ("ffa_norm", "ffa_w_gate_up", "ffa_w_down", "mix_norm", "ffb_norm", "ffb_w_gate_up",
                "ffb_w_down", "ple_norm", "ple_w_gate", "ple_w_proj", "ev_w_in", "ev_sinks",
                "ev_cq_norm", "ev_w_uq", "ev_ckv_norm", "ev_w_ukv", "ev_w_out", "od_w_in", "od_b_f",
                "od_w_out", "final_norm")


def _cp(*sem):
    return pltpu.CompilerParams(dimension_semantics=sem, vmem_limit_bytes=VMEM_LIMIT)


def _sigmoid(z):
    return 1.0 / (1.0 + jnp.exp(-z))


def _rms_stats(xv):
    r = lax.rsqrt(jnp.mean(xv * xv, axis=-1, keepdims=True) + RMS_EPS)
    return r, xv * r


def _rms_bwd(dxn, xv, g):
    r, xhat = _rms_stats(xv)
    u = dxn * g
    dx = r * (u - xhat * jnp.mean(u * xhat, axis=-1, keepdims=True))
    return dx, dxn * xhat


def _col_tile(k_rows, n, budget_bytes=6 * 2**20):
    if k_rows * n * 4 <= budget_bytes or n % LANES:
        return n
    units = n // LANES
    best = LANES
    for d in range(1, units + 1):
        if units % d == 0 and k_rows * d * LANES * 4 <= budget_bytes:
            best = d * LANES
    return best


def _row_tile(rows, cols, target_elems=2**18):
    if rows * cols <= target_elems or rows % 8:
        return rows
    best = 8
    for d in range(8, rows + 1, 8):
        if rows % d == 0 and d * cols <= target_elems:
            best = d
    return best


def _fox_in_fwd(x, g, w, *, nheads, dh, q_ones, k_ones, name, tm=512):
    s, k = x.shape
    n = w.shape[1]
    wd = nheads * dh
    spare = LANES - dh

    def body(x_ref, g_ref, w_ref, q_ref, k_ref, v_ref, vt_ref, f_ref, xn_ref):
        _, xhat = _rms_stats(x_ref[...])
        xn = (xhat * g_ref[...]).astype(BF16)
        xn_ref[...] = xn
        y = jnp.dot(xn, w_ref[...], preferred_element_type=F32)
        f_ref[...] = y[:, 3 * wd:]
        lane = lax.broadcasted_iota(jnp.int32, (tm, spare), 1)

        def fill(cols):
            return functools.reduce(jnp.logical_or, [lane == c for c in cols]).astype(F32)

        q_fill, k_fill = fill(q_ones), fill(k_ones)
        for h in range(nheads):
            q_ref[h] = jnp.concatenate([y[:, h * dh:(h + 1) * dh], q_fill], axis=-1).astype(BF16)
            k_ref[h] = jnp.concatenate([y[:, wd + h * dh:wd + (h + 1) * dh], k_fill], axis=-1).astype(BF16)
            vh = y[:, 2 * wd + h * dh:2 * wd + (h + 1) * dh]
            v_ref[h] = vh.astype(BF16)
            vt_ref[h] = vh.T.astype(BF16)

    wide = pl.BlockSpec((nheads, tm, LANES), lambda i: (0, i, 0))
    return pl.pallas_call(
        body, name=name, grid=(s // tm,),
        in_specs=[pl.BlockSpec((tm, k), lambda i: (i, 0)), pl.BlockSpec((1, k), lambda i: (0, 0)),
                  pl.BlockSpec((k, n), lambda i: (0, 0))],
        out_specs=[wide, wide, pl.BlockSpec((nheads, tm, dh), lambda i: (0, i, 0)),
                   pl.BlockSpec((nheads, dh, tm), lambda i: (0, 0, i)), pl.BlockSpec((tm, LANES), lambda i: (i, 0)),
                   pl.BlockSpec((tm, k), lambda i: (i, 0))],
        out_shape=[jax.ShapeDtypeStruct((nheads, s, LANES), BF16)] * 2
        + [jax.ShapeDtypeStruct((nheads, s, dh), BF16), jax.ShapeDtypeStruct((nheads, dh, s), BF16),
           jax.ShapeDtypeStruct((s, LANES), F32), jax.ShapeDtypeStruct((s, k), BF16)],
        compiler_params=_cp("arbitrary"))(x, g, w)


def _merge_heads(dq, dk, dvt, *, dh, q_col, k_col, name, tm=512):
    nheads, s, _ = dq.shape

    def body(dq_ref, dk_ref, dvt_ref, o_ref, cols_ref):
        pieces = [dq_ref[h][:, :dh] for h in range(nheads)] + [dk_ref[h][:, :dh] for h in range(nheads)]
        pieces += [dvt_ref[h].T for h in range(nheads)]
        o_ref[...] = jnp.concatenate(pieces, axis=-1)
        lane = lax.broadcasted_iota(jnp.int32, (tm, LANES), 1)
        cols = jnp.zeros((tm, LANES), F32)
        for h in range(nheads):
            cols = jnp.where(lane == h, jnp.broadcast_to(dq_ref[h][:, q_col:q_col + 1], (tm, LANES)), cols)
            cols = jnp.where(lane == nheads + h, jnp.broadcast_to(dk_ref[h][:, k_col:k_col + 1], (tm, LANES)), cols)
        cols_ref[...] = cols

    wide = pl.BlockSpec((nheads, tm, LANES), lambda i: (0, i, 0))
    return pl.pallas_call(
        body, name=name, grid=(s // tm,),
        in_specs=[wide, wide, pl.BlockSpec((nheads, dh, tm), lambda i: (0, 0, i))],
        out_specs=[pl.BlockSpec((tm, 3 * nheads * dh), lambda i: (i, 0)), pl.BlockSpec((tm, LANES), lambda i: (i, 0))],
        out_shape=[jax.ShapeDtypeStruct((s, 3 * nheads * dh), F32), jax.ShapeDtypeStruct((s, LANES), F32)],
        compiler_params=_cp("arbitrary"))(dq, dk, dvt)


def _row_call(body, n_rows, ins, outs, *, name, tm=512):
    def spec(a, axis):
        shape = a.shape
        if axis is None:
            return pl.BlockSpec(shape, lambda i: (0,) * len(shape))
        blk = tuple(tm if d == axis else n for d, n in enumerate(shape))
        return pl.BlockSpec(blk, lambda i: tuple(i if d == axis else 0 for d in range(len(shape))))

    return pl.pallas_call(
        body, name=name, grid=(n_rows // tm,), in_specs=[spec(a, ax) for a, ax in ins],
        out_specs=[spec(a, ax) for a, ax in outs], out_shape=[a for a, _ in outs],
        compiler_params=_cp("arbitrary"))(*[a for a, _ in ins])


def _sds(shape, dtype):
    return jax.ShapeDtypeStruct(shape, dtype)


def _ev_in_fwd(x, g, w, *, name):
    s, k = x.shape
    d = A_HEAD_DIM

    def body(x_ref, g_ref, w_ref, q_ref, k_ref, v_ref, vt_ref, cq_ref, ckv_ref, kr_ref, xn_ref):
        _, xhat = _rms_stats(x_ref[...])
        xn = (xhat * g_ref[...]).astype(BF16)
        xn_ref[...] = xn
        y = jnp.dot(xn, w_ref[...], preferred_element_type=F32)
        for h in range(A_HEADS):
            q_ref[h] = y[:, h * d:(h + 1) * d].astype(BF16)
        for h in range(A_KV_HEADS):
            k_ref[h] = y[:, 512 + h * d:512 + (h + 1) * d].astype(BF16)
            vh = y[:, 640 + h * d:640 + (h + 1) * d]
            v_ref[h] = vh.astype(BF16)
            vt_ref[h] = vh.T.astype(BF16)
        cq_ref[...] = y[:, 768:1024]
        ckv_ref[...] = y[:, 1024:1152]
        kr_ref[...] = y[:, 1152:1280]

    return _row_call(
        body, s, [(x, 0), (g, None), (w, None)],
        [(_sds((A_HEADS, s, d), BF16), 1), (_sds((A_KV_HEADS, s, d), BF16), 1), (_sds((A_KV_HEADS, s, d), BF16), 1),
         (_sds((A_KV_HEADS, d, s), BF16), 2), (_sds((s, B_Q_LORA), F32), 0), (_sds((s, B_KV_LORA), F32), 0),
         (_sds((s, LANES), F32), 0), (_sds((s, k), BF16), 0)], name=name)


def _ev_q_fwd(x, g, w, cos, sin, *, name):
    s, k = x.shape
    rot = B_HEADS * B_ROPE

    def body(x_ref, g_ref, w_ref, c_ref, s_ref, q_ref, xn_ref):
        _, xhat = _rms_stats(x_ref[...])
        xn = (xhat * g_ref[...]).astype(BF16)
        xn_ref[...] = xn
        y = jnp.dot(xn, w_ref[...], preferred_element_type=F32)
        ro = y[:, 512:512 + rot] * c_ref[...] + y[:, 512 + rot:] * s_ref[...]
        zero = jnp.zeros((y.shape[0], LANES - B_NOPE - B_ROPE), F32)
        for h in range(B_HEADS):
            q_ref[h] = jnp.concatenate([y[:, h * B_NOPE:(h + 1) * B_NOPE], ro[:, h * B_ROPE:(h + 1) * B_ROPE], zero],
                                       axis=-1).astype(BF16)

    return _row_call(body, s, [(x, 0), (g, None), (w, None), (cos, 0), (sin, 0)],
                     [(_sds((B_HEADS, s, LANES), BF16), 1), (_sds((s, k), BF16), 0)], name=name)


def _ev_kv_fwd(x, g, w, kro, *, name):
    s, k = x.shape
    per = B_NOPE + B_V

    def body(x_ref, g_ref, w_ref, kr_ref, k_ref, v_ref, vt_ref, xn_ref):
        _, xhat = _rms_stats(x_ref[...])
        xn = (xhat * g_ref[...]).astype(BF16)
        xn_ref[...] = xn
        y = jnp.dot(xn, w_ref[...], preferred_element_type=F32)
        kr = kr_ref[...]
        zero = jnp.zeros((y.shape[0], LANES - B_NOPE - B_ROPE), F32)
        for h in range(B_HEADS):
            k_ref[h] = jnp.concatenate([y[:, h * per:h * per + B_NOPE], kr, zero], axis=-1).astype(BF16)
            vh = y[:, h * per + B_NOPE:(h + 1) * per]
            v_ref[h] = vh.astype(BF16)
            vt_ref[h] = vh.T.astype(BF16)

    return _row_call(body, s, [(x, 0), (g, None), (w, None), (kro, 0)],
                     [(_sds((B_HEADS, s, LANES), BF16), 1), (_sds((B_HEADS, s, B_V), BF16), 1),
                      (_sds((B_HEADS, B_V, s), BF16), 2), (_sds((s, k), BF16), 0)], name=name)


def _ev_q_merge(dq, cos, sin, *, name):
    nh, s, _ = dq.shape

    def body(dq_ref, c_ref, s_ref, o_ref):
        dro = jnp.concatenate([dq_ref[h][:, B_NOPE:B_NOPE + B_ROPE] for h in range(nh)], axis=-1)
        o_ref[...] = jnp.concatenate([dq_ref[h][:, :B_NOPE] for h in range(nh)] + [dro * c_ref[...], dro * s_ref[...]], axis=-1)

    return _row_call(body, s, [(dq, 1), (cos, 0), (sin, 0)], [(_sds((s, 2 * nh * B_NOPE), F32), 0)], name=name)[0]


def _ev_kv_merge(dk, dvt, cos, sin, *, name):
    nh, s, _ = dk.shape

    def body(dk_ref, dvt_ref, c_ref, s_ref, o_ref, kr_ref):
        pieces = []
        tot = None
        for h in range(nh):
            pieces += [dk_ref[h][:, :B_NOPE], dvt_ref[h].T]
            rot = dk_ref[h][:, B_NOPE:B_NOPE + B_ROPE]
            tot = rot if tot is None else tot + rot
        o_ref[...] = jnp.concatenate(pieces, axis=-1)
        kr_ref[...] = jnp.concatenate([tot * c_ref[...], tot * s_ref[...], jnp.zeros((tot.shape[0], LANES - 2 * B_ROPE), F32)],
                                      axis=-1)

    return _row_call(body, s, [(dk, 1), (dvt, 2), (cos, 0), (sin, 0)],
                     [(_sds((s, nh * (B_NOPE + B_V)), F32), 0), (_sds((s, LANES), F32), 0)], name=name)


def _ev_in_merge(dq, dk, dvt, dcq, dckv, dkr, *, name):
    s = dcq.shape[0]

    def body(dq_ref, dk_ref, dvt_ref, cq_ref, ckv_ref, kr_ref, o_ref):
        pieces = [dq_ref[h] for h in range(A_HEADS)] + [dk_ref[h] for h in range(A_KV_HEADS)]
        pieces += [dvt_ref[h].T for h in range(A_KV_HEADS)] + [cq_ref[...], ckv_ref[...], kr_ref[...]]
        o_ref[...] = jnp.concatenate(pieces, axis=-1)

    return _row_call(body, s, [(dq, 1), (dk, 1), (dvt, 2), (dcq, 0), (dckv, 0), (dkr, 0)],
                     [(_sds((s, 1280), F32), 0)], name=name)[0]


def _ffn_up(x, g, wgu, *, name, tm=512, rider=None):
    s, k = x.shape
    f = wgu.shape[1] // 2
    tn = _col_tile(k, f)
    nj = f // tn

    def body(x_ref, g_ref, wg_ref, wu_ref, dgate_ref, dup_ref, act_ref, xn_ref, xn_sc):
        @pl.when(pl.program_id(1) == 0)
        def _():
            _, xhat = _rms_stats(x_ref[...])
            xn = (xhat * g_ref[...]).astype(BF16)
            xn_sc[...] = xn
            xn_ref[...] = xn

        xn = xn_sc[...]
        gg = jnp.dot(xn, wg_ref[...], preferred_element_type=F32)
        uu = jnp.dot(xn, wu_ref[...], preferred_element_type=F32)
        sg = _sigmoid(gg)
        silu = gg * sg
        dgate_ref[...] = (uu * (sg * (1.0 + gg * (1.0 - sg)))).astype(BF16)
        dup_ref[...] = silu.astype(BF16)
        act_ref[...] = (silu * uu).astype(BF16)

    tile = pl.BlockSpec((tm, tn), lambda i, j: (i, j))
    return _call_with_rider(
        body, rider, name=name, grid=(s // tm, nj),
        in_specs=[pl.BlockSpec((tm, k), lambda i, j: (i, 0)), pl.BlockSpec((1, k), lambda i, j: (0, 0)),
                  pl.BlockSpec((k, tn), lambda i, j: (0, j)), pl.BlockSpec((k, tn), lambda i, j: (0, j + nj))],
        out_specs=[tile, tile, tile, pl.BlockSpec((tm, k), lambda i, j: (i, 0))],
        out_shape=[jax.ShapeDtypeStruct((s, f), BF16)] * 3 + [jax.ShapeDtypeStruct((s, k), BF16)],
        scratch_shapes=[pltpu.VMEM((tm, k), BF16)],
        compiler_params=_cp("arbitrary", "arbitrary"), args=(x, g, wgu, wgu))


def _mm_res_fwd(a, w, res, *, scale, name, tm=512, a_t=False):
    k, n = w.shape
    s = res.shape[0]

    def body(a_ref, w_ref, r_ref, o_ref):
        prod = (lax.dot_general(a_ref[...], w_ref[...], TN, preferred_element_type=F32) if a_t
                else jnp.dot(a_ref[...], w_ref[...], preferred_element_type=F32))
        o_ref[...] = r_ref[...] + scale * prod

    a_spec = pl.BlockSpec((k, tm), lambda i: (0, i)) if a_t else pl.BlockSpec((tm, k), lambda i: (i, 0))
    return pl.pallas_call(
        body, name=name, grid=(s // tm,),
        in_specs=[a_spec, pl.BlockSpec((k, n), lambda i: (0, 0)),
                  pl.BlockSpec((tm, n), lambda i: (i, 0))],
        out_specs=pl.BlockSpec((tm, n), lambda i: (i, 0)),
        out_shape=jax.ShapeDtypeStruct((s, n), F32),
        compiler_params=_cp("arbitrary"))(a, w, res)


def _ffn_down_bwd(dh, wdt, dact_dgate, dact_dup, *, scale, name, tm=512, rider=None):
    s, d = dh.shape
    f = wdt.shape[1]
    tn = _col_tile(d, f)

    def body(dh_ref, wdt_ref, fg_ref, fu_ref, dg_ref, du_ref):
        dhb = (dh_ref[...] * scale).astype(BF16)
        da = jnp.dot(dhb, wdt_ref[...], preferred_element_type=F32)
        dg_ref[...] = (da * fg_ref[...].astype(F32)).astype(BF16)
        du_ref[...] = (da * fu_ref[...].astype(F32)).astype(BF16)

    tile = pl.BlockSpec((tm, tn), lambda i, j: (i, j))
    return _call_with_rider(
        body, rider, name=name, grid=(s // tm, f // tn),
        in_specs=[pl.BlockSpec((tm, d), lambda i, j: (i, 0)), pl.BlockSpec((d, tn), lambda i, j: (0, j)), tile, tile],
        out_specs=[tile, tile],
        out_shape=[jax.ShapeDtypeStruct((s, f), BF16)] * 2, scratch_shapes=[],
        compiler_params=_cp("arbitrary", "arbitrary"), args=(dh, wdt, dact_dgate, dact_dup))


def _mm_tn(a, bs, *, name, b_scale=1.0, ts=512, rider=None, a_t=False):
    bs = list(bs) if isinstance(bs, (list, tuple)) else [bs]
    k, s = a.shape if a_t else a.shape[::-1]
    n = bs[0].shape[1]
    tn = _col_tile(k, n, 12 * 2**20)
    per = n // tn

    def body(a_ref, *refs):
        b_refs, o_ref = refs[:-1], refs[-1]
        j = pl.program_id(0)

        @pl.when(pl.program_id(1) == 0)
        def _():
            o_ref[...] = jnp.zeros_like(o_ref)

        for m, b_ref in enumerate(b_refs):
            def acc(b_ref=b_ref):
                bv = b_ref[...]
                if b_scale != 1.0:
                    bv = bv * b_scale
                av = a_ref[...].astype(BF16)
                o_ref[...] += (jnp.dot(av, bv.astype(BF16), preferred_element_type=F32) if a_t
                               else lax.dot_general(av, bv.astype(BF16), TN, preferred_element_type=F32))

            if len(b_refs) == 1:
                acc()
            else:
                pl.when(jnp.logical_and(j >= m * per, j < (m + 1) * per))(acc)

    def b_spec(m):
        def idx(j, t):
            mine = jnp.logical_and(j >= m * per, j < (m + 1) * per)
            return (jnp.where(mine, t, 0), jnp.clip(j - m * per, 0, per - 1))
        return pl.BlockSpec((ts, tn), idx)

    (out,), rode = _call_with_rider(
        body, rider, name=name, grid=(per * len(bs), s // ts),
        in_specs=[pl.BlockSpec((k, ts), lambda j, t: (0, t)) if a_t else pl.BlockSpec((ts, k), lambda j, t: (t, 0))]
        + [b_spec(m) for m in range(len(bs))],
        out_specs=[pl.BlockSpec((k, tn), lambda j, t: (0, j))],
        out_shape=[jax.ShapeDtypeStruct((k, n * len(bs)), F32)], scratch_shapes=[],
        compiler_params=_cp("arbitrary", "arbitrary"), args=(a, *bs))
    return out if rider is None else (out, rode)


def _mm_nt_t(dy, w, *, name, tm=512):
    s, n = dy.shape
    k = w.shape[0]

    def body(dy_ref, w_ref, o_ref):
        o_ref[...] = lax.dot_general(w_ref[...], dy_ref[...].astype(BF16), NT, preferred_element_type=F32).astype(BF16)

    return pl.pallas_call(
        body, name=name, grid=(s // tm,),
        in_specs=[pl.BlockSpec((tm, n), lambda i: (i, 0)), pl.BlockSpec((k, n), lambda i: (0, 0))],
        out_specs=pl.BlockSpec((k, tm), lambda i: (0, i)),
        out_shape=jax.ShapeDtypeStruct((k, s), BF16),
        compiler_params=_cp("arbitrary"))(dy, w)


def _mm_nt_rmsbwd(pairs, x, g, dres, *, name, tm=512, rider=None):
    s, k = x.shape
    npairs = len(pairs)
    pairs = [pr if len(pr) == 3 else (pr[0], pr[1], 0) for pr in pairs]

    def body(*refs):
        dy_refs = refs[0:2 * npairs:2]
        w_refs = refs[1:2 * npairs:2]
        rest = refs[2 * npairs:]
        x_ref, g_ref = rest[0], rest[1]
        if dres is None:
            dx_ref, dg_ref = rest[2], rest[3]
        else:
            dres_ref, dx_ref, dg_ref = rest[2], rest[3], rest[4]
        dxn = None
        for dy_ref, w_ref in zip(dy_refs, w_refs):
            t = jnp.dot(dy_ref[...].astype(BF16), w_ref[...], preferred_element_type=F32)
            dxn = t if dxn is None else dxn + t
        dx, dgrow = _rms_bwd(dxn, x_ref[...], g_ref[...])
        if dres is not None:
            dx = dx + dres_ref[...]
        dx_ref[...] = dx

        @pl.when(pl.program_id(0) == 0)
        def _():
            dg_ref[...] = jnp.zeros_like(dg_ref)

        dg_ref[...] += jnp.sum(dgrow, axis=0, keepdims=True)

    in_specs, args = [], []
    for dy, wt, rb in pairs:
        n = dy.shape[1]
        in_specs += [pl.BlockSpec((tm, n), lambda i: (i, 0)),
                     pl.BlockSpec((n, k), lambda i, rb=rb: (rb, 0), pipeline_mode=pl.Buffered(1))]
        args += [dy, wt]
    row = pl.BlockSpec((tm, k), lambda i: (i, 0))
    vec = pl.BlockSpec((1, k), lambda i: (0, 0))
    in_specs += [row, vec]
    args += [x, g]
    if dres is not None:
        in_specs.append(row)
        args.append(dres)
    (dx, dgain), rode = _call_with_rider(
        body, rider, name=name, grid=(s // tm,), in_specs=in_specs, out_specs=[row, vec],
        out_shape=[jax.ShapeDtypeStruct((s, k), F32), jax.ShapeDtypeStruct((1, k), F32)], scratch_shapes=[],
        compiler_params=_cp("arbitrary"), args=args)
    return (dx, dgain) if rider is None else (dx, dgain, rode)


def _ple_fwd(h, g, wg, p, wp, *, name, tm=512):
    s, d = h.shape
    pd = p.shape[1]

    def body(h_ref, g_ref, wg_ref, p_ref, wp_ref, o_ref, xn_ref, gate_ref, pp_ref):
        hv = h_ref[...]
        _, xhat = _rms_stats(hv)
        xn = (xhat * g_ref[...]).astype(BF16)
        xn_ref[...] = xn
        gate = _sigmoid(jnp.dot(xn, wg_ref[...], preferred_element_type=F32))
        pp = jnp.dot(p_ref[...].astype(BF16), wp_ref[...], preferred_element_type=F32)
        gate_ref[...] = gate.astype(BF16)
        pp_ref[...] = pp.astype(BF16)
        o_ref[...] = hv + gate * pp

    row = pl.BlockSpec((tm, d), lambda i: (i, 0))
    return pl.pallas_call(
        body, name=name, grid=(s // tm,),
        in_specs=[row, pl.BlockSpec((1, d), lambda i: (0, 0)), pl.BlockSpec((d, d), lambda i: (0, 0)),
                  pl.BlockSpec((tm, pd), lambda i: (i, 0)), pl.BlockSpec((pd, d), lambda i: (0, 0))],
        out_specs=[row, row, row, row],
        out_shape=[jax.ShapeDtypeStruct((s, d), F32)] + [jax.ShapeDtypeStruct((s, d), BF16)] * 3,
        compiler_params=_cp("arbitrary"))(h, g, wg, p, wp)


def _ple_bwd_elem(dh, gate, pp, *, name, tm=512):
    s, d = dh.shape

    def body(dh_ref, gate_ref, pp_ref, dz_ref, dpp_ref):
        dhv = dh_ref[...]
        gt = gate_ref[...].astype(F32)
        dz_ref[...] = (dhv * pp_ref[...].astype(F32) * (gt * (1.0 - gt))).astype(BF16)
        dpp_ref[...] = (dhv * gt).astype(BF16)

    row = pl.BlockSpec((tm, d), lambda i: (i, 0))
    return pl.pallas_call(
        body, name=name, grid=(s // tm,), in_specs=[row, row, row], out_specs=[row, row],
        out_shape=[jax.ShapeDtypeStruct((s, d), BF16)] * 2,
        compiler_params=_cp("arbitrary"))(dh, gate, pp)


def _final_loss(h, g, tgt, *, name, tm=512):
    s, d = h.shape

    def body(h_ref, g_ref, t_ref, loss_ref, dh_ref, dg_ref):
        @pl.when(pl.program_id(0) == 0)
        def _():
            loss_ref[...] = jnp.zeros_like(loss_ref)
            dg_ref[...] = jnp.zeros_like(dg_ref)

        hv = h_ref[...]
        gv = g_ref[...]
        _, xhat = _rms_stats(hv)
        err = xhat * gv - t_ref[...]
        per_row = jnp.mean(err * err, axis=-1, keepdims=True)
        loss_ref[...] += 0.5 * jnp.sum(per_row, axis=0, keepdims=True)
        dx, dgrow = _rms_bwd(err * (1.0 / d), hv, gv)
        dh_ref[...] = dx
        dg_ref[...] += jnp.sum(dgrow, axis=0, keepdims=True)

    row = pl.BlockSpec((tm, d), lambda i: (i, 0))
    vec = pl.BlockSpec((1, d), lambda i: (0, 0))
    return pl.pallas_call(
        body, name=name, grid=(s // tm,), in_specs=[row, vec, row],
        out_specs=[pl.BlockSpec((1, LANES), lambda i: (0, 0)), row, vec],
        out_shape=[jax.ShapeDtypeStruct((1, LANES), F32), jax.ShapeDtypeStruct((s, d), F32),
                   jax.ShapeDtypeStruct((1, d), F32)],
        compiler_params=_cp("arbitrary"))(h, g, tgt)


def _rope_fwd(y1, y2, cos, sin, *, name, tm=512):
    s, r = y1.shape

    def body(a_ref, b_ref, c_ref, s_ref, o_ref):
        o_ref[...] = a_ref[...] * c_ref[...] + b_ref[...] * s_ref[...]

    row = pl.BlockSpec((tm, r), lambda i: (i, 0))
    return pl.pallas_call(
        body, name=name, grid=(s // tm,), in_specs=[row] * 4, out_specs=row,
        out_shape=jax.ShapeDtypeStruct((s, r), F32), compiler_params=_cp("arbitrary"))(y1, y2, cos, sin)


def _split3(v):
    h1 = v.astype(BF16)
    r1 = v - h1.astype(F32)
    h2 = r1.astype(BF16)
    h3 = (r1 - h2.astype(F32)).astype(BF16)
    return h1, h2, h3


def _tri(tb, upper):
    r = lax.broadcasted_iota(jnp.int32, (tb, tb), 0)
    c = lax.broadcasted_iota(jnp.int32, (tb, tb), 1)
    return jnp.where((r <= c) if upper else (r >= c), 1.0, 0.0).astype(BF16)


def _fox_gate_fwd(ft, bf, *, out_scale, name, tb=512):
    nh, s = ft.shape

    def body(f_ref, b_ref, o_ref, carry):
        @pl.when(pl.program_id(0) == 0)
        def _():
            carry[...] = jnp.zeros_like(carry)

        z = f_ref[...] + b_ref[...]
        lf = jnp.minimum(z, 0.0) - jnp.log(1.0 + jnp.exp(-jnp.abs(z)))
        tri = _tri(tb, True)
        cs = sum(jnp.dot(t, tri, preferred_element_type=F32) for t in _split3(lf)) + carry[...]
        for n, term in enumerate(_split3(cs * out_scale)):
            o_ref[n] = term
        carry[...] += jnp.sum(lf, axis=-1, keepdims=True)

    return pl.pallas_call(
        body, name=name, grid=(s // tb,),
        in_specs=[pl.BlockSpec((nh, tb), lambda t: (0, t)), pl.BlockSpec((nh, 1), lambda t: (0, 0))],
        out_specs=pl.BlockSpec((3, nh, tb), lambda t: (0, 0, t)),
        out_shape=jax.ShapeDtypeStruct((3, nh, s), BF16),
        scratch_shapes=[pltpu.VMEM((nh, 1), F32)], compiler_params=_cp("arbitrary"))(ft, bf)


def _fox_gate_bwd(drow, dcol, ft, bf, *, inv_scale, name, tb=512):
    nh, s = ft.shape
    nb = s // tb

    def body(dr_ref, dc_ref, f_ref, b_ref, df_ref, db_ref, carry):
        @pl.when(pl.program_id(0) == 0)
        def _():
            carry[...] = jnp.zeros_like(carry)
            db_ref[...] = jnp.zeros_like(db_ref)

        dc = (dr_ref[...] - dc_ref[...]) * inv_scale
        tri = _tri(tb, False)
        suf = sum(jnp.dot(t, tri, preferred_element_type=F32) for t in _split3(dc)) + carry[...]
        z = f_ref[...] + b_ref[...]
        dz = suf * (1.0 / (1.0 + jnp.exp(z)))
        df_ref[...] = dz
        db_ref[...] += jnp.sum(dz, axis=-1, keepdims=True)
        carry[...] += jnp.sum(dc, axis=-1, keepdims=True)

    rev = pl.BlockSpec((nh, tb), lambda t: (0, nb - 1 - t))
    one = pl.BlockSpec((nh, 1), lambda t: (0, 0))
    return pl.pallas_call(
        body, name=name, grid=(nb,), in_specs=[rev, rev, rev, one], out_specs=[rev, one],
        out_shape=[jax.ShapeDtypeStruct((nh, s), F32), jax.ShapeDtypeStruct((nh, 1), F32)],
        scratch_shapes=[pltpu.VMEM((nh, 1), F32)], compiler_params=_cp("arbitrary"))(drow, dcol, ft, bf)


def _tri_fwd(t, nq):
    i = sum((t >= (r * (r + 1)) // 2).astype(jnp.int32) for r in range(1, nq))
    return i, t - (i * (i + 1)) // 2


def _tri_bwd(t, nq):
    j = sum((t >= r * nq - (r * (r - 1)) // 2).astype(jnp.int32) for r in range(1, nq))
    return j, j + t - (j * nq - (j * (j - 1)) // 2)


def _scores_t(k, q, *, scale, diag):
    s = lax.dot_general(k, q, NT, preferred_element_type=F32) * scale
    if diag:
        r = lax.broadcasted_iota(jnp.int32, s.shape, 0)
        c = lax.broadcasted_iota(jnp.int32, s.shape, 1)
        s = jnp.where(r <= c, s, MASK_VALUE)
    return s


def _causal_fwd_t(q, k, vt, *, scale, name, tq, hb=2, rider=None):
    nh, s, dq = q.shape
    dv = vt.shape[1]
    nq = s // tq
    nsteps = (nq * (nq + 1)) // 2

    def body(q_ref, k_ref, vt_ref, o_ref, lse_ref, m_sc, l_sc, acc_sc):
        i, j = _tri_fwd(pl.program_id(1), nq)

        @pl.when(j == 0)
        def _():
            m_sc[...] = jnp.full_like(m_sc, MASK_VALUE)
            l_sc[...] = jnp.zeros_like(l_sc)
            acc_sc[...] = jnp.zeros_like(acc_sc)

        def step(diag):
            for u in range(hb):
                sc = _scores_t(k_ref[u], q_ref[u], scale=scale, diag=diag)
                m_prev = m_sc[u]
                m_new = jnp.maximum(m_prev, jnp.max(sc, axis=0, keepdims=True))
                alpha = jnp.exp(m_prev - m_new)
                pr = jnp.exp(sc - m_new)
                l_new = alpha * l_sc[u] + jnp.sum(pr, axis=0, keepdims=True)
                acc = alpha * acc_sc[u] + jnp.dot(vt_ref[u], pr.astype(BF16), preferred_element_type=F32)
                if diag:
                    o_ref[u] = (acc / l_new).astype(BF16)
                    lse_ref[u] = m_new + jnp.log(l_new)
                else:
                    m_sc[u], l_sc[u], acc_sc[u] = m_new, l_new, acc

        pl.when(j < i)(functools.partial(step, False))
        pl.when(j == i)(functools.partial(step, True))

    def qi(t):
        return _tri_fwd(t, nq)[0]

    def kj(t):
        return _tri_fwd(t, nq)[1]

    return _call_with_rider(
        body, rider, name=name, grid=(nh // hb, nsteps),
        in_specs=[pl.BlockSpec((hb, tq, dq), lambda hp, t: (hp, qi(t), 0)),
                  pl.BlockSpec((hb, tq, dq), lambda hp, t: (hp, kj(t), 0)),
                  pl.BlockSpec((hb, dv, tq), lambda hp, t: (hp, 0, kj(t)))],
        out_specs=[pl.BlockSpec((hb, dv, tq), lambda hp, t: (hp, 0, qi(t))),
                   pl.BlockSpec((hb, 1, tq), lambda hp, t: (hp, 0, qi(t)))],
        out_shape=[jax.ShapeDtypeStruct((nh, dv, s), BF16), jax.ShapeDtypeStruct((nh, 1, s), F32)],
        scratch_shapes=[pltpu.VMEM((hb, 1, tq), F32), pltpu.VMEM((hb, 1, tq), F32), pltpu.VMEM((hb, dv, tq), F32)],
        compiler_params=_cp("arbitrary", "arbitrary"), args=(q, k, vt))


def _causal_bwd_t(q, k, v, ot, dot_, lse, *, scale, name, tq, hb=2, rider=None):
    nh, s, dq = q.shape
    dv = v.shape[-1]
    nq = s // tq
    nsteps = (nq * (nq + 1)) // 2

    def body(q_ref, k_ref, v_ref, ot_ref, dot_ref, lse_ref, dq_ref, dk_ref, dvt_ref):
        t = pl.program_id(1)
        j, i = _tri_bwd(t, nq)

        @pl.when(t == 0)
        def _():
            dq_ref[...] = jnp.zeros_like(dq_ref)

        def step(diag):
            rows = pl.ds(pl.multiple_of(i * tq, tq), tq)
            for u in range(hb):
                qv, kv, dov = q_ref[u], k_ref[u], dot_ref[u]
                pr = jnp.exp(_scores_t(kv, qv, scale=scale, diag=diag) - lse_ref[u])
                dp = jnp.dot(v_ref[u], dov, preferred_element_type=F32)
                delta = jnp.sum(dov.astype(F32) * ot_ref[u].astype(F32), axis=0, keepdims=True)
                dsb = ((pr * (dp - delta)) * scale).astype(BF16)
                d_v = lax.dot_general(dov, pr.astype(BF16), NT, preferred_element_type=F32)
                d_k = jnp.dot(dsb, qv, preferred_element_type=F32)
                if diag:
                    dvt_ref[u], dk_ref[u] = d_v, d_k
                else:
                    dvt_ref[u] += d_v
                    dk_ref[u] += d_k
                dq_ref[u, rows, :] += lax.dot_general(dsb, kv, TN, preferred_element_type=F32)

        pl.when(i > j)(functools.partial(step, False))
        pl.when(i == j)(functools.partial(step, True))

    def qi(t):
        return _tri_bwd(t, nq)[1]

    def kj(t):
        return _tri_bwd(t, nq)[0]

    rows_q = pl.BlockSpec((hb, tq, dq), lambda hp, t: (hp, qi(t), 0))
    rows_k = pl.BlockSpec((hb, tq, dq), lambda hp, t: (hp, kj(t), 0))
    lanes_q = pl.BlockSpec((hb, dv, tq), lambda hp, t: (hp, 0, qi(t)))
    return _call_with_rider(
        body, rider, name=name, grid=(nh // hb, nsteps),
        in_specs=[rows_q, rows_k, pl.BlockSpec((hb, tq, dv), lambda hp, t: (hp, kj(t), 0)), lanes_q, lanes_q,
                  pl.BlockSpec((hb, 1, tq), lambda hp, t: (hp, 0, qi(t)))],
        out_specs=[pl.BlockSpec((hb, s, dq), lambda hp, t: (hp, 0, 0)), rows_k,
                   pl.BlockSpec((hb, dv, tq), lambda hp, t: (hp, 0, kj(t)))],
        out_shape=[jax.ShapeDtypeStruct((nh, s, dq), F32), jax.ShapeDtypeStruct((nh, s, dq), F32),
                   jax.ShapeDtypeStruct((nh, dv, s), F32)],
        scratch_shapes=[], compiler_params=_cp("arbitrary", "arbitrary"), args=(q, k, v, ot, dot_, lse))


def _swa_scores_t(k, q, dist, ok, *, scale, slope):
    s = lax.dot_general(k, q, NT, preferred_element_type=F32) * scale - slope * dist.astype(F32)
    return jnp.where(ok, s, MASK_VALUE)


def _swa_geometry(tb, w, has_other):
    r = lax.broadcasted_iota(jnp.int32, (tb, tb), 0)
    c = lax.broadcasted_iota(jnp.int32, (tb, tb), 1)
    d_same = c - r
    ok_same = jnp.logical_and(d_same >= 0, d_same < w)

    def other(ncols):
        rr = lax.broadcasted_iota(jnp.int32, (w, ncols), 0)
        cc = lax.broadcasted_iota(jnp.int32, (w, ncols), 1)
        dd = cc + w - rr
        return dd, jnp.logical_and(dd < w, has_other)

    return (d_same, ok_same), other


def _swa_fwd_t(q, k, vt, slopes_sinks, *, scale, window, name, tb=256):
    nh, s, d = q.shape
    nkv = k.shape[0]
    grp = nh // nkv
    w = window
    per = tb // w
    assert tb % w == 0

    def body(q_ref, kc_ref, kp_ref, vc_ref, vp_ref, ss_ref, o_ref, lse_ref):
        kvh, i = pl.program_id(0), pl.program_id(1)
        (d_c, ok_c), other = _swa_geometry(tb, w, i > 0)
        d_p, ok_p = other(tb)
        for g in range(grp):
            h = kvh * grp + g
            slope, sink = ss_ref[0, h], ss_ref[1, h]
            qg = q_ref[g]
            s_c = _swa_scores_t(kc_ref[...], qg, d_c, ok_c, scale=scale, slope=slope)
            s_p = _swa_scores_t(kp_ref[...], qg, d_p, ok_p, scale=scale, slope=slope)
            m = jnp.maximum(jnp.maximum(jnp.max(s_c, axis=0, keepdims=True), jnp.max(s_p, axis=0, keepdims=True)), sink)
            p_c, p_p = jnp.exp(s_c - m), jnp.exp(s_p - m)
            l = jnp.sum(p_c, axis=0, keepdims=True) + jnp.sum(p_p, axis=0, keepdims=True) + jnp.exp(sink - m)
            acc = (jnp.dot(vc_ref[...], p_c.astype(BF16), preferred_element_type=F32)
                   + jnp.dot(vp_ref[...], p_p.astype(BF16), preferred_element_type=F32))
            o_ref[g] = (acc / l).astype(BF16)
            lse_ref[g] = m + jnp.log(l)

    def prev(i):
        return jnp.maximum(i * per - 1, 0)

    return pl.pallas_call(
        body, name=name, grid=(nkv, s // tb),
        in_specs=[pl.BlockSpec((grp, tb, d), lambda kh, i: (kh, i, 0)),
                  pl.BlockSpec((None, tb, d), lambda kh, i: (kh, i, 0)),
                  pl.BlockSpec((None, w, d), lambda kh, i: (kh, prev(i), 0)),
                  pl.BlockSpec((None, d, tb), lambda kh, i: (kh, 0, i)),
                  pl.BlockSpec((None, d, w), lambda kh, i: (kh, 0, prev(i))),
                  pl.BlockSpec(memory_space=pltpu.SMEM)],
        out_specs=[pl.BlockSpec((grp, d, tb), lambda kh, i: (kh, 0, i)), pl.BlockSpec((grp, 1, tb), lambda kh, i: (kh, 0, i))],
        out_shape=[jax.ShapeDtypeStruct((nh, d, s), BF16), jax.ShapeDtypeStruct((nh, 1, s), F32)],
        compiler_params=_cp("arbitrary", "arbitrary"))(q, k, k, vt, vt, slopes_sinks)


def _swa_bwd_t(q, k, v, ot, dot_, lse, slopes_sinks, *, scale, window, name, tb=256, rider=None):
    nh, s, d = q.shape
    nkv = k.shape[0]
    grp = nh // nkv
    w = window
    per = tb // w
    nb = s // tb

    def body(qc_ref, qn_ref, kc_ref, kp_ref, vc_ref, vp_ref, oc_ref, on_ref, doc_ref, don_ref, lc_ref, ln_ref, ss_ref,
             dq_ref, dk_ref, dvt_ref, dsink_ref):
        kvh, i = pl.program_id(0), pl.program_id(1)

        @pl.when(i == 0)
        def _():
            dsink_ref[...] = jnp.zeros_like(dsink_ref)

        (d_c, ok_c), other = _swa_geometry(tb, w, i > 0)
        d_p, ok_p = other(tb)
        d_n, ok_n = _swa_geometry(tb, w, i < nb - 1)[1](w)
        kc, kp, vc, vp = kc_ref[...], kp_ref[...], vc_ref[...], vp_ref[...]
        k_last, v_last = kc[tb - w:, :], vc[tb - w:, :]
        dk_acc = jnp.zeros((tb, d), F32)
        dv_acc = jnp.zeros((d, tb), F32)
        dk_tail = jnp.zeros((w, d), F32)
        dv_tail = jnp.zeros((d, w), F32)
        for g in range(grp):
            h = kvh * grp + g
            slope, sink = ss_ref[0, h], ss_ref[1, h]
            qg, dog, lse_c = qc_ref[g], doc_ref[g], lc_ref[g]
            delta = jnp.sum(dog.astype(F32) * oc_ref[g].astype(F32), axis=0, keepdims=True)
            p_c = jnp.exp(_swa_scores_t(kc, qg, d_c, ok_c, scale=scale, slope=slope) - lse_c)
            p_p = jnp.exp(_swa_scores_t(kp, qg, d_p, ok_p, scale=scale, slope=slope) - lse_c)
            ds_c = ((p_c * (jnp.dot(vc, dog, preferred_element_type=F32) - delta)) * scale).astype(BF16)
            ds_p = ((p_p * (jnp.dot(vp, dog, preferred_element_type=F32) - delta)) * scale).astype(BF16)
            dq_ref[g] = (lax.dot_general(ds_c, kc, TN, preferred_element_type=F32)
                         + lax.dot_general(ds_p, kp, TN, preferred_element_type=F32))
            dk_acc += jnp.dot(ds_c, qg, preferred_element_type=F32)
            dv_acc += lax.dot_general(dog, p_c.astype(BF16), NT, preferred_element_type=F32)
            dsink_ref[g] -= jnp.broadcast_to(jnp.sum(jnp.exp(sink - lse_c) * delta, axis=1, keepdims=True), (1, LANES))
            qn, don = qn_ref[g], don_ref[g]
            delta_n = jnp.sum(don.astype(F32) * on_ref[g].astype(F32), axis=0, keepdims=True)
            p_n = jnp.exp(_swa_scores_t(k_last, qn, d_n, ok_n, scale=scale, slope=slope) - ln_ref[g])
            ds_n = ((p_n * (jnp.dot(v_last, don, preferred_element_type=F32) - delta_n)) * scale).astype(BF16)
            dk_tail += jnp.dot(ds_n, qn, preferred_element_type=F32)
            dv_tail += lax.dot_general(don, p_n.astype(BF16), NT, preferred_element_type=F32)
        dk_ref[...] = dk_acc
        dvt_ref[...] = dv_acc
        dk_ref[tb - w:, :] += dk_tail
        dvt_ref[:, tb - w:] += dv_tail

    def prev(i):
        return jnp.maximum(i * per - 1, 0)

    def nxt(i):
        return jnp.minimum((i + 1) * per, s // w - 1)

    return _call_with_rider(
        body, rider, name=name, grid=(nkv, nb), scratch_shapes=[],
        args=(q, q, k, k, v, v, ot, ot, dot_, dot_, lse, lse, slopes_sinks),
        in_specs=[pl.BlockSpec((grp, tb, d), lambda kh, i: (kh, i, 0)),
                  pl.BlockSpec((grp, w, d), lambda kh, i: (kh, nxt(i), 0)),
                  pl.BlockSpec((None, tb, d), lambda kh, i: (kh, i, 0)),
                  pl.BlockSpec((None, w, d), lambda kh, i: (kh, prev(i), 0)),
                  pl.BlockSpec((None, tb, d), lambda kh, i: (kh, i, 0)),
                  pl.BlockSpec((None, w, d), lambda kh, i: (kh, prev(i), 0)),
                  pl.BlockSpec((grp, d, tb), lambda kh, i: (kh, 0, i)),
                  pl.BlockSpec((grp, d, w), lambda kh, i: (kh, 0, nxt(i))),
                  pl.BlockSpec((grp, d, tb), lambda kh, i: (kh, 0, i)),
                  pl.BlockSpec((grp, d, w), lambda kh, i: (kh, 0, nxt(i))),
                  pl.BlockSpec((grp, 1, tb), lambda kh, i: (kh, 0, i)),
                  pl.BlockSpec((grp, 1, w), lambda kh, i: (kh, 0, nxt(i))),
                  pl.BlockSpec(memory_space=pltpu.SMEM)],
        out_specs=[pl.BlockSpec((grp, tb, d), lambda kh, i: (kh, i, 0)),
                   pl.BlockSpec((None, tb, d), lambda kh, i: (kh, i, 0)),
                   pl.BlockSpec((None, d, tb), lambda kh, i: (kh, 0, i)),
                   pl.BlockSpec((None, grp, 1, LANES), lambda kh, i: (kh, 0, 0, 0))],
        out_shape=[jax.ShapeDtypeStruct((nh, s, d), F32), jax.ShapeDtypeStruct((nkv, s, d), F32),
                   jax.ShapeDtypeStruct((nkv, d, s), F32), jax.ShapeDtypeStruct((nkv, grp, 1, LANES), F32)],
        compiler_params=_cp("arbitrary", "arbitrary"))


def _adamw(w, g, m, v, *, name):
    shape = w.shape
    cols = shape[-1]
    rows = int(np.prod(shape[:-1])) if len(shape) > 1 else 1
    tr = _row_tile(rows, cols)
    c1 = 1.0 - ADAM_B1 ** ADAM_STEP
    c2 = 1.0 - ADAM_B2 ** ADAM_STEP

    def body(w_ref, g_ref, m_ref, v_ref, d_ref, mo_ref, vo_ref):
        gv = g_ref[...]
        mn = ADAM_B1 * m_ref[...] + (1.0 - ADAM_B1) * gv
        vn = ADAM_B2 * v_ref[...] + (1.0 - ADAM_B2) * (gv * gv)
        mo_ref[...] = mn
        vo_ref[...] = vn
        d_ref[...] = -ADAM_LR * ((mn / c1) / (jnp.sqrt(vn / c2) + ADAM_EPS) + ADAM_WD * w_ref[...])

    blk = pl.BlockSpec((tr, cols), lambda i: (i, 0))
    outs = pl.pallas_call(
        body, name=name, grid=(rows // tr,), in_specs=[blk] * 4, out_specs=[blk] * 3,
        out_shape=[jax.ShapeDtypeStruct((rows, cols), F32)] * 3,
        compiler_params=_cp("arbitrary"))(*[a.reshape(rows, cols) for a in (w, g, m, v)])
    return tuple(a.reshape(shape) for a in outs)


def _hbm_spec():
    return pl.BlockSpec(memory_space=pl.ANY)


def _mesh_place():
    x, y, c = lax.axis_index("x"), lax.axis_index("y"), lax.axis_index("c")
    return x, y, c, [(1 - x, y), (x, 1 - y), (1 - x, 1 - y)]


def _half_rows(c, rows, align):
    return pl.ds(pl.multiple_of(c * (rows // 2), align), rows // 2)


def _part(ref, mode, k, n, rows=None):
    if mode == "cols":
        cols = pl.ds(pl.multiple_of(k * n, LANES), n)
        return ref.at[:, cols] if rows is None else ref.at[rows, cols]
    return ref.at[k] if rows is None else ref.at[k, rows, :]


class _Rider:
    def __init__(self, inputs, out_shape, n_sems, start, finish):
        self.inputs, self.out_shape, self.n_sems, self.start, self.finish = inputs, out_shape, n_sems, start, finish


def _call_with_rider(body, rider, *, name, grid, in_specs, out_specs, out_shape, scratch_shapes, compiler_params, args):
    if rider is None:
        outs = pl.pallas_call(body, name=name, grid=grid, in_specs=in_specs, out_specs=out_specs, out_shape=out_shape,
                              scratch_shapes=scratch_shapes, compiler_params=compiler_params)(*args)
        return outs, []
    n_in, n_out, n_sc = len(in_specs), len(out_specs), len(scratch_shapes)
    n_rin, n_rout = len(rider.inputs), len(rider.out_shape)

    def wrapped(*refs):
        pos = 0
        groups = []
        for n in (n_in, n_rin, n_out, n_rout, n_sc, 2):
            groups.append(refs[pos:pos + n])
            pos += n
        ins, rins, outs, routs, scratch, sems = groups
        ids = [pl.program_id(a) for a in range(len(grid))]
        first = functools.reduce(jnp.logical_and, [i == 0 for i in ids])
        last = functools.reduce(jnp.logical_and, [i == g - 1 for i, g in zip(ids, grid)])
        pl.when(first)(lambda: rider.start(rins, routs, *sems))
        body(*ins, *outs, *scratch)
        pl.when(last)(lambda: rider.finish(rins, routs, *sems))

    outs = pl.pallas_call(
        wrapped, name=name, grid=grid, in_specs=list(in_specs) + [_hbm_spec()] * n_rin,
        out_specs=list(out_specs) + [_hbm_spec()] * n_rout, out_shape=list(out_shape) + list(rider.out_shape),
        scratch_shapes=list(scratch_shapes) + [pltpu.SemaphoreType.DMA((rider.n_sems,))] * 2,
        compiler_params=compiler_params)(*args, *rider.inputs)
    return outs[:n_out], outs[n_out:]


def _run_rider(rider, *, name):
    n_rin = len(rider.inputs)

    def body(*refs):
        rins, routs, sems = refs[:n_rin], refs[n_rin:-2], refs[-2:]
        rider.start(rins, routs, *sems)
        rider.finish(rins, routs, *sems)

    return pl.pallas_call(
        body, name=name, in_specs=[_hbm_spec()] * n_rin, out_specs=[_hbm_spec()] * len(rider.out_shape),
        out_shape=rider.out_shape, scratch_shapes=[pltpu.SemaphoreType.DMA((rider.n_sems,))] * 2)(*rider.inputs)


def _gather_rider(shards, modes):
    n_arr = len(shards)
    out_shape = [jax.ShapeDtypeStruct((s.shape[0], N_CHIPS * s.shape[1]) if m == "cols" else (N_CHIPS,) + s.shape, s.dtype)
                 for s, m in zip(shards, modes)]
    per = 4

    def copies(srcs, dsts, send_sems, recv_sems):
        x, y, c, chips = _mesh_place()
        me = 2 * x + y
        sends, waits = [], []
        for i in range(n_arr):
            r, n = shards[i].shape
            rows = _half_rows(c, r, 16)

            def copy(slot, src, dst, to, i=i):
                return pltpu.make_async_remote_copy(src_ref=src, dst_ref=dst, send_sem=send_sems.at[i * per + slot],
                                                    recv_sem=recv_sems.at[i * per + slot], device_id=to, device_id_type=MESH)

            own = _part(dsts[i], modes[i], me, n)
            sends.append(copy(0, srcs[i], own, (x, y, 1 - c)))
            waits.append(copy(0, own, own, (x, y, 1 - c)))
            for j, (px, py) in enumerate(chips):
                sends.append(copy(1 + j, srcs[i].at[rows], _part(dsts[i], modes[i], me, n, rows), (px, py, c)))
                theirs = _part(dsts[i], modes[i], 2 * px + py, n, rows)
                waits.append(copy(1 + j, theirs, theirs, (px, py, c)))
        return sends, waits

    def start(*refs):
        for cp in copies(*refs)[0]:
            cp.start()

    def finish(*refs):
        sends, waits = copies(*refs)
        for cp in waits:
            cp.wait_recv()
        for cp in sends:
            cp.wait_send()

    return _Rider(list(shards), out_shape, per * n_arr, start, finish)


def _gather_forward(dsts, shard_shapes, modes, *, name):
    n_arr = len(dsts)

    def body(*refs):
        outs = refs[n_arr:2 * n_arr]
        send_sems, recv_sems = refs[2 * n_arr:]
        x, y, c, chips = _mesh_place()
        cps = []
        for i in range(n_arr):
            r, n = shard_shapes[i]
            for j, (px, py) in enumerate(chips):
                def view(hc, i=i, px=px, py=py, r=r, n=n):
                    return _part(outs[i], modes[i], 2 * px + py, n, _half_rows(hc, r, 16))

                def copy(ref, i=i, j=j):
                    return pltpu.make_async_remote_copy(src_ref=ref, dst_ref=ref, send_sem=send_sems.at[3 * i + j],
                                                        recv_sem=recv_sems.at[3 * i + j], device_id=(x, y, 1 - c), device_id_type=MESH)

                cps.append((copy(view(c)), copy(view(1 - c))))
        for send, _ in cps:
            send.start()
        for send, theirs in cps:
            theirs.wait_recv()
            send.wait_send()

    return pl.pallas_call(
        body, name=name, in_specs=[_hbm_spec()] * n_arr, out_specs=[_hbm_spec()] * n_arr,
        out_shape=[jax.ShapeDtypeStruct(d.shape, d.dtype) for d in dsts],
        input_output_aliases={i: i for i in range(n_arr)},
        scratch_shapes=[pltpu.SemaphoreType.DMA((3 * n_arr,)), pltpu.SemaphoreType.DMA((3 * n_arr,))])(*dsts)


def _blk_view(a, mode):
    return a[None] if mode == "cols" else a


def _swap_rider(arrs, modes):
    n_arr = len(arrs)
    out_shape = [jax.ShapeDtypeStruct((a.shape[0] // 2, a.shape[1]) if m == "cols" else (a.shape[0], a.shape[1] // 2, a.shape[2]), a.dtype)
                 for a, m in zip(arrs, modes)]

    def copies(srcs, dsts, send_sems, recv_sems):
        x, y, c, _ = _mesh_place()
        cps = []
        for i in range(n_arr):
            if modes[i] == "cols":
                src = srcs[i].at[_half_rows(1 - c, arrs[i].shape[0], 8)]
            else:
                src = srcs[i].at[:, _half_rows(1 - c, arrs[i].shape[1], 8), :]
            cps.append(pltpu.make_async_remote_copy(src_ref=src, dst_ref=dsts[i], send_sem=send_sems.at[i],
                                                    recv_sem=recv_sems.at[i], device_id=(x, y, 1 - c), device_id_type=MESH))
        return cps

    def start(*refs):
        for cp in copies(*refs):
            cp.start()

    def finish(*refs):
        for cp in copies(*refs):
            cp.wait()

    return _Rider(list(arrs), out_shape, n_arr, start, finish)


def _rs_pair_add(arr, landed, place, *, name):
    nb, r, c = arr.shape
    rh = r // 2
    tr = _row_tile(rh, c)
    nt = rh // tr

    def body(p_ref, a_ref, l_ref, o_ref):
        o_ref[...] = (a_ref[...] + l_ref[...]).astype(BF16)

    grid_spec = pltpu.PrefetchScalarGridSpec(
        num_scalar_prefetch=1, grid=(nb, nt),
        in_specs=[pl.BlockSpec((None, tr, c), lambda b, t, p_ref: (b, p_ref[1] * nt + t, 0)),
                  pl.BlockSpec((None, tr, c), lambda b, t, p_ref: (b, t, 0))],
        out_specs=pl.BlockSpec((None, tr, c), lambda b, t, p_ref: (b, t, 0)))
    return pl.pallas_call(
        body, name=name, grid_spec=grid_spec, out_shape=jax.ShapeDtypeStruct((nb, rh, c), BF16),
        compiler_params=_cp("arbitrary", "arbitrary"))(place, arr, landed)


def _exchange_rider(parts, modes):
    n_arr = len(parts)
    out_shape = []
    for a, m in zip(parts, modes):
        shp = (a.shape[0], a.shape[1] // N_CHIPS) if m == "cols" else a.shape[1:]
        out_shape.append(jax.ShapeDtypeStruct((3,) + shp, a.dtype))

    def copies(srcs, dsts, send_sems, recv_sems):
        x, y, c, chips = _mesh_place()
        cps = []
        for i in range(n_arr):
            n = out_shape[i].shape[-1]
            for j, (px, py) in enumerate(chips):
                cps.append(pltpu.make_async_remote_copy(
                    src_ref=_part(srcs[i], modes[i], 2 * px + py, n), dst_ref=dsts[i].at[j],
                    send_sem=send_sems.at[3 * i + j], recv_sem=recv_sems.at[3 * i + j],
                    device_id=(px, py, c), device_id_type=MESH))
        return cps

    def start(*refs):
        for cp in copies(*refs):
            cp.start()

    def finish(*refs):
        for cp in copies(*refs):
            cp.wait()

    return _Rider(list(parts), out_shape, 3 * n_arr, start, finish)


def _rs_chip_sum(part, landed, mode, place, *, name):
    _, rh, n = landed.shape
    tr = _row_tile(rh, n)
    nt = rh // tr

    def body(p_ref, a_ref, l_ref, o_ref):
        o_ref[...] = ((a_ref[...].astype(F32) + l_ref[0].astype(F32)) + l_ref[1].astype(F32)) + l_ref[2].astype(F32)

    if mode == "cols":
        own = pl.BlockSpec((tr, n), lambda t, p_ref: (t, p_ref[0]))
    else:
        own = pl.BlockSpec((None, tr, n), lambda t, p_ref: (p_ref[0], t, 0))
    grid_spec = pltpu.PrefetchScalarGridSpec(
        num_scalar_prefetch=1, grid=(nt,),
        in_specs=[own, pl.BlockSpec((3, tr, n), lambda t, p_ref: (0, t, 0))],
        out_specs=pl.BlockSpec((tr, n), lambda t, p_ref: (p_ref[1] * nt + t, 0)))
    return pl.pallas_call(
        body, name=name, grid_spec=grid_spec, out_shape=jax.ShapeDtypeStruct((2 * rh, n), F32),
        compiler_params=_cp("arbitrary"))(place, part, landed)


def _rs_pair_join(halves, *, name):
    n_arr = len(halves)

    def body(*refs):
        outs = refs[n_arr:2 * n_arr]
        send_sems, recv_sems = refs[2 * n_arr:]
        x, y, c, _ = _mesh_place()
        cps = []
        for i in range(n_arr):
            rows = _half_rows(c, halves[i].shape[0], 8)
            cps.append(pltpu.make_async_remote_copy(src_ref=outs[i].at[rows], dst_ref=outs[i].at[rows], send_sem=send_sems.at[i],
                                                    recv_sem=recv_sems.at[i], device_id=(x, y, 1 - c), device_id_type=MESH))
        for cp in cps:
            cp.start()
        for i, cp in enumerate(cps):
            cp.wait_send()
            theirs = outs[i].at[_half_rows(1 - c, halves[i].shape[0], 8)]
            pltpu.make_async_remote_copy(src_ref=theirs, dst_ref=theirs, send_sem=send_sems.at[i], recv_sem=recv_sems.at[i],
                                         device_id=(x, y, 1 - c), device_id_type=MESH).wait_recv()

    return pl.pallas_call(
        body, name=name, in_specs=[_hbm_spec()] * n_arr, out_specs=[_hbm_spec()] * n_arr,
        out_shape=[jax.ShapeDtypeStruct(h.shape, h.dtype) for h in halves],
        input_output_aliases={i: i for i in range(n_arr)},
        scratch_shapes=[pltpu.SemaphoreType.DMA((n_arr,)), pltpu.SemaphoreType.DMA((n_arr,))])(*halves)


def _allreduce_small(v, *, name):
    r, c = v.shape

    def body(v_ref, o_ref, gath, send_sems, recv_sems):
        x, y, cc, _ = _mesh_place()
        me = 4 * x + 2 * y + cc
        gath[me] = v_ref[...]
        cps = []
        for rel in range(1, 8):
            px = 1 - x if rel & 4 else x
            py = 1 - y if rel & 2 else y
            pc = 1 - cc if rel & 1 else cc

            def copy(slot, px=px, py=py, pc=pc, rel=rel):
                return pltpu.make_async_remote_copy(
                    src_ref=v_ref, dst_ref=gath.at[slot], send_sem=send_sems.at[rel - 1],
                    recv_sem=recv_sems.at[rel - 1], device_id=(px, py, pc), device_id_type=MESH)

            cps.append((copy(me), copy(4 * px + 2 * py + pc)))
        for send, _ in cps:
            send.start()
        for send, theirs in cps:
            theirs.wait_recv()
            send.wait_send()
        tot = gath[0]
        for d in range(1, 8):
            tot = tot + gath[d]
        o_ref[...] = tot

    vm = pl.BlockSpec(memory_space=pltpu.VMEM)
    return pl.pallas_call(
        body, name=name, in_specs=[vm], out_specs=vm, out_shape=jax.ShapeDtypeStruct((r, c), F32),
        scratch_shapes=[pltpu.VMEM((8, r, c), F32), pltpu.SemaphoreType.DMA((7,)), pltpu.SemaphoreType.DMA((7,))])(v)


def _rope_tables(s, reps):
    half = B_ROPE // 2
    inv = ROPE_THETA ** (-jnp.arange(0, B_ROPE, 2, dtype=F32) / B_ROPE)
    ang = jnp.arange(s, dtype=F32)[:, None] * inv[None, :]
    return jnp.tile(jnp.cos(ang), (1, reps)), jnp.tile(jnp.sin(ang), (1, reps))


def _alibi_slopes():
    return 2.0 ** (-8.0 * jnp.arange(1, A_HEADS + 1, dtype=F32) / A_HEADS)


def _ffn_fwd(h, norm, wts, tag, rider=None, on_rode=None):
    (dact_dgate, dact_dup, act, xn), rode = _ffn_up(h, norm, wts["wgu"], name=f"{tag}_up", rider=rider)
    if on_rode is not None:
        on_rode(rode)
    out = _mm_res_fwd(act, wts["wd"], h, scale=FFN_RES_SCALE, name=f"{tag}_down")
    return out, dict(h_in=h, dact_dgate=dact_dgate, dact_dup=dact_dup, act=act, xn=xn), rode


def _ffn_bwd(dh, norm, wts, sv, tag, rider=None, own=None):
    (dgate, dup), rode = _ffn_down_bwd(dh, wts["wd"].T, sv["dact_dgate"], sv["dact_dup"], scale=FFN_RES_SCALE,
                                      name=f"{tag}_down_bwd", rider=rider)
    d_wd = _mm_tn(sv["act"], dh, b_scale=FFN_RES_SCALE, name=f"{tag}_dwd")
    wgut = wts["wgu"].T
    pairs = [(dgate, wgut, 0), (dup, wgut, 1)]
    if own is None:
        d_wgu = _mm_tn(sv["xn"], [dgate, dup], name=f"{tag}_dwgu")
        dh_in, dnorm = _mm_nt_rmsbwd(pairs, sv["h_in"], norm, dh, name=f"{tag}_dx")
    else:
        wd_ready, wgu_ready, done = own
        first = wd_ready(d_wd)
        res = _mm_tn(sv["xn"], [dgate, dup], name=f"{tag}_dwgu", rider=first)
        d_wgu, brought = (res, []) if first is None else res
        second = wgu_ready(brought, d_wgu)
        res = _mm_nt_rmsbwd(pairs, sv["h_in"], norm, dh, name=f"{tag}_dx", rider=second)
        dh_in, dnorm, brought = (*res, []) if second is None else res
        done(brought)
    return dh_in, dnorm, d_wgu, d_wd, rode


def _even_weights(w_in, w_uq, w_ukv):
    half = B_ROPE // 2
    base = w_in.shape[1]
    kr1, kr2 = w_in[:, base - B_ROPE:base - half], w_in[:, base - half:]
    w_in_cat = jnp.concatenate([w_in, -kr2, kr1, jnp.zeros((w_in.shape[0], 64), w_in.dtype)], axis=1)
    u3 = w_uq.reshape(w_uq.shape[0], B_HEADS, B_NOPE + B_ROPE)
    nope = u3[:, :, :B_NOPE].reshape(w_uq.shape[0], -1)
    rot = u3[:, :, B_NOPE:].reshape(w_uq.shape[0], -1)
    swapped = jnp.concatenate([-u3[:, :, B_NOPE + half:], u3[:, :, B_NOPE:B_NOPE + half]], axis=-1).reshape(w_uq.shape[0], -1)
    return w_in_cat, jnp.concatenate([nope, rot, swapped], axis=1), w_ukv


def _even_fwd(h, w, i, rider=None):
    s = h.shape[0]
    qa, ka, va, vat, c_q, c_kv, kr_blk, xn = _ev_in_fwd(h, w["mix_norm"][i:i + 1], w["ev_in_cat"], name="ev_in")
    cos32, sin32 = _rope_tables(s, 2)
    kro = _rope_fwd(kr_blk[:, :B_ROPE], kr_blk[:, B_ROPE:2 * B_ROPE], cos32, sin32, name="ev_k_rope")
    ss = jnp.stack([_alibi_slopes(), w["ev_sinks"].reshape(-1)])
    oa, lse_a = _swa_fwd_t(qa, ka, vat, ss, scale=A_HEAD_DIM ** -0.5, window=WINDOW, name="swa_fwd")
    cos256, sin256 = _rope_tables(s, 2 * B_HEADS)
    qb, xn_q = _ev_q_fwd(c_q, w["ev_cq_norm"], w["ev_q_cat"], cos256, sin256, name="ev_q_up")
    kb, vb, vbt, xn_kv = _ev_kv_fwd(c_kv, w["ev_ckv_norm"], w["ev_ukv"], kro, name="ev_kv_up")
    (ob, lse_b), rode = _causal_fwd_t(qb, kb, vbt, scale=(B_NOPE + B_ROPE) ** -0.5, name="mla_fwd", tq=512, hb=4, rider=rider)
    attn = jnp.concatenate([oa.reshape(-1, s), ob.reshape(-1, s)], axis=0)
    out = _mm_res_fwd(attn, w["ev_out"], h, scale=1.0, name="ev_out", a_t=True)
    sv = dict(h_in=h, xn=xn, c_q=c_q, c_kv=c_kv, xn_q=xn_q, xn_kv=xn_kv, qa=qa, ka=ka, va=va, oa=oa, lse_a=lse_a,
              ss=ss, qb=qb, kb=kb, vb=vb, ob=ob, lse_b=lse_b, attn=attn, cos32=cos32, sin32=sin32,
              cos256=cos256, sin256=sin256)
    return out, sv, rode


def _even_bwd(dh, w, sv, i, rider=None):
    s = dh.shape[0]
    half = B_ROPE // 2
    g = {}
    dattn = _mm_nt_t(dh, w["ev_out"], name="ev_out_dx")
    g["ev_w_out"] = _mm_tn(sv["attn"], dh, name="ev_out_dw", a_t=True)
    doa = dattn[:A_HEADS * A_HEAD_DIM].reshape(A_HEADS, A_HEAD_DIM, s)
    dob = dattn[A_HEADS * A_HEAD_DIM:].reshape(B_HEADS, B_V, s)
    first, then = rider if isinstance(rider, tuple) else (None, None)
    (dqa, dka, dva, dsink), brought = _swa_bwd_t(sv["qa"], sv["ka"], sv["va"], sv["oa"], doa, sv["lse_a"], sv["ss"],
                                                 scale=A_HEAD_DIM ** -0.5, window=WINDOW, name="swa_bwd", rider=first)
    if then is not None:
        rider = then(brought)
    g["ev_sinks"] = dsink[:, :, 0, 0].reshape(1, A_HEADS)
    (dqb, dkb, dvb), rode = _causal_bwd_t(sv["qb"], sv["kb"], sv["vb"], sv["ob"], dob, sv["lse_b"],
                                          scale=(B_NOPE + B_ROPE) ** -0.5, name="mla_bwd", tq=512, hb=4, rider=rider)
    dyq = _ev_q_merge(dqb, sv["cos256"], sv["sin256"], name="ev_q_merge")
    dwq = _mm_tn(sv["xn_q"], dyq, name="ev_q_up_dw")
    dcq, g["ev_cq_norm"] = _mm_nt_rmsbwd([(dyq, w["ev_q_cat"].T)], sv["c_q"], w["ev_cq_norm"], None, name="ev_q_up_dx")
    kq = sv["c_q"].shape[1]
    d_nope = dwq[:, :512].reshape(kq, B_HEADS, B_NOPE)
    d_rot = dwq[:, 512:768].reshape(kq, B_HEADS, B_ROPE)
    d_swp = dwq[:, 768:].reshape(kq, B_HEADS, B_ROPE)
    g["ev_w_uq"] = jnp.concatenate([d_nope, d_rot[:, :, :half] + d_swp[:, :, half:], d_rot[:, :, half:] - d_swp[:, :, :half]],
                                   axis=-1).reshape(kq, -1)
    dykv, dkr = _ev_kv_merge(dkb, dvb, sv["cos32"], sv["sin32"], name="ev_kv_merge")
    g["ev_w_ukv"] = _mm_tn(sv["xn_kv"], dykv, name="ev_kv_up_dw")
    dckv, g["ev_ckv_norm"] = _mm_nt_rmsbwd([(dykv, w["ev_ukv"].T)], sv["c_kv"], w["ev_ckv_norm"], None, name="ev_kv_up_dx")
    dycat = _ev_in_merge(dqa, dka, dva, dcq, dckv, dkr, name="ev_in_merge")
    dwin = _mm_tn(sv["xn"], dycat, name="ev_in_dw")
    base = 1184
    g["ev_w_in"] = jnp.concatenate([dwin[:, :base - B_ROPE],
                                    dwin[:, base - B_ROPE:base - half] + dwin[:, base + half:base + B_ROPE],
                                    dwin[:, base - half:base] - dwin[:, base:base + half]], axis=-1)
    dh_in, dnorm = _mm_nt_rmsbwd([(dycat, w["ev_in_cat"].T)], sv["h_in"], w["mix_norm"][i:i + 1], dh, name="ev_in_dx")
    return dh_in, dnorm, g, rode


def _odd_fwd(h, w, i, rider=None):
    s = h.shape[0]
    wd = C_HEADS * C_HEAD_DIM
    q, k, v, vt, y_f, xn = _fox_in_fwd(h, w["mix_norm"][i:i + 1], w["od_in_pad"], nheads=C_HEADS, dh=C_HEAD_DIM,
                                       q_ones=(0, 2, 3, 4), k_ones=(1,), name="od_in")
    scale = C_HEAD_DIM ** -0.5
    ft = y_f[:, :C_HEADS].T
    bf = w["od_b_f"].reshape(C_HEADS, 1)
    cb3 = _fox_gate_fwd(ft, bf, out_scale=-1.0 / scale, name="fox_gate_fwd")
    k = k + jnp.pad(cb3.transpose(1, 2, 0), ((0, 0), (0, 0), (C_HEAD_DIM + 2, LANES - C_HEAD_DIM - 5)))
    (o, lse), rode = _causal_fwd_t(q, k, vt, scale=scale, name="fox_fwd", tq=512, hb=4, rider=rider)
    attn = o.reshape(-1, s)
    out = _mm_res_fwd(attn, w["od_out"], h, scale=1.0, name="od_out", a_t=True)
    return out, dict(h_in=h, xn=xn, q=q, k=k, v=v, o=o, lse=lse, ft=ft, bf=bf, attn=attn), rode


def _odd_bwd(dh, w, sv, i, rider=None):
    s = dh.shape[0]
    g = {}
    dattn = _mm_nt_t(dh, w["od_out"], name="od_out_dx")
    g["od_w_out"] = _mm_tn(sv["attn"], dh, name="od_out_dw", a_t=True)
    do = dattn.reshape(C_HEADS, C_HEAD_DIM, s)
    scale = C_HEAD_DIM ** -0.5
    (dq, dk, dv), rode = _causal_bwd_t(sv["q"], sv["k"], sv["v"], sv["o"], do, sv["lse"], scale=scale, name="fox_bwd",
                                       tq=512, hb=4, rider=rider)
    dqkv, sums = _merge_heads(dq, dk, dv, dh=C_HEAD_DIM, q_col=C_HEAD_DIM + 1, k_col=C_HEAD_DIM, name="fox_merge")
    dft, dbf = _fox_gate_bwd(sums[:, :C_HEADS].T, sums[:, C_HEADS:2 * C_HEADS].T, sv["ft"], sv["bf"],
                             inv_scale=1.0 / scale, name="fox_gate_bwd")
    g["od_b_f"] = dbf.reshape(1, C_HEADS)
    wd = C_HEADS * C_HEAD_DIM
    df = jnp.pad(dft.T, ((0, 0), (0, LANES - C_HEADS)))
    g["od_w_in"] = jnp.concatenate([_mm_tn(sv["xn"], dqkv, name="od_in_dw"),
                                    _mm_tn(sv["xn"], df, name="od_in_dwf")[:, :C_HEADS]], axis=-1)
    od_in_t = w["od_in_pad"].T
    dh_in, dnorm = _mm_nt_rmsbwd([(dqkv, od_in_t, 0), (df, od_in_t, 3 * wd // LANES)],
                                 sv["h_in"], w["mix_norm"][i:i + 1], dh, name="od_in_dx")
    return dh_in, dnorm, g, rode


def _kernel_weights(full, replicated):
    w = dict(replicated)
    _install_weights(w, {(n, i): a for n, per_layer in full.items() for i, a in enumerate(per_layer)})
    return w


def _install_weights(w, got):
    raw = w.setdefault("raw", {})
    raw.update(got)
    for (n, i), a in got.items():
        if n in ("ffa_w_gate_up", "ffa_w_down", "ffb_w_gate_up", "ffb_w_down"):
            w.setdefault(n[:3], {}).setdefault(i, {})["wgu" if n.endswith("gate_up") else "wd"] = a
        elif n in ("ple_w_gate", "ple_w_proj"):
            w.setdefault("ple_gate" if n.endswith("gate") else "ple_proj", {})[i] = a
    if "ev_in_cat" not in w and all((n, 0) in raw for n in ("ev_w_in", "ev_w_uq", "ev_w_ukv", "ev_w_out")):
        w["ev_in_cat"], w["ev_q_cat"], w["ev_ukv"] = _even_weights(raw["ev_w_in", 0], raw["ev_w_uq", 0], raw["ev_w_ukv", 0])
        w["ev_out"] = raw["ev_w_out", 0]
    if "od_in_pad" not in w and all((n, 0) in raw for n in ("od_w_in", "od_w_out")):
        od_in = raw["od_w_in", 0]
        w["od_in_pad"] = jnp.pad(od_in, ((0, 0), (0, (-od_in.shape[1]) % LANES)))
        w["od_out"] = raw["od_w_out", 0]


def _keys(names, layer):
    return tuple((n, layer) for n in names)


_FFA, _FFB, _PLE = ("ffa_w_gate_up", "ffa_w_down"), ("ffb_w_gate_up", "ffb_w_down"), ("ple_w_gate", "ple_w_proj")
_EV, _OD = ("ev_w_in", "ev_w_uq", "ev_w_ukv", "ev_w_out"), ("od_w_in", "od_w_out")
_GATHER_FIRST = _keys(_FFA[:1], 0)
_GATHER_RIDES = {("ffa", 0): _keys(_FFA[1:] + _EV, 0), ("mix", 0): _keys(_FFB + _PLE, 0) + _keys(_FFA, 1),
                 ("ffb", 0): _keys(_OD, 0), ("mix", 1): _keys(_FFB + _PLE, 1)}
_REDUCE_RIDES = {("mix", 1): _keys(_FFB + _PLE, 1), ("mix", 0): _keys(_FFA, 1) + _keys(_OD, 0) + _keys(_FFB + _PLE, 0),
                 ("ffa", 0): _keys(_EV, 0)}
_REDUCE_OWN = ("ffa", 0)
_SWAP_AHEAD = {("ffb", 1): ("mix", 1)}


def _local_step(x, p, tgt, w, ex=None):
    depth = p.shape[0]

    def gather_behind(host, fn, *args):
        keys = None if ex is None else _GATHER_RIDES.get(host)
        if keys is None:
            return fn(*args, None)[:-1]
        done = []

        def install(rode):
            if not done:
                _install_weights(w, ex.gather_finish(keys, rode, name=f"weight_forward_{host[0]}{host[1]}"))
                done.append(True)

        res = fn(*args, ex.gather_rider(keys), install) if fn is _ffn_fwd else fn(*args, ex.gather_rider(keys))
        install(res[-1])
        return res[:-1]

    h = x
    saved = []
    for i in range(depth):
        sv = {}
        h, sv["ffa"] = gather_behind(("ffa", i), _ffn_fwd, h, w["ffa_norm"][i:i + 1], w["ffa"][i], f"ffa{i}")
        h, sv["mix"] = gather_behind(("mix", i), _even_fwd if i % 2 == 0 else _odd_fwd, h, w, i)
        h, sv["ffb"] = gather_behind(("ffb", i), _ffn_fwd, h, w["ffb_norm"][i:i + 1], w["ffb"][i], f"ffb{i}")
        h_in = h
        h, xn, gate, pp = _ple_fwd(h, w["ple_norm"][i:i + 1], w["ple_gate"][i], p[i], w["ple_proj"][i], name=f"ple{i}")
        sv["ple"] = dict(h_in=h_in, xn=xn, gate=gate, pp=pp)
        saved.append(sv)
    loss_vec, dh, d_final = _final_loss(h, w["final_norm"].reshape(1, -1), tgt, name="final_loss")

    per_layer = [dict() for _ in range(depth)]
    mats = {}
    grads = {}

    pending = {}

    def reduce_behind(host, fn, *args):
        keys = None if ex is None else _REDUCE_RIDES.get(host)
        ahead = None if ex is None else _SWAP_AHEAD.get(host)
        if keys is None and ahead is None:
            return fn(*args, None)[:-1]
        if ahead is not None:
            got, ctxs = {}, []

            def note_wd(d_wd):
                got[f"{host[0]}_w_down", host[1]] = d_wd

            def swap_now(brought, d_wgu):
                got[f"{host[0]}_w_gate_up", host[1]] = d_wgu
                swap, ctx = ex.swap_rider(_REDUCE_RIDES[ahead], {**mats, **got})
                ctxs.append(ctx)
                return swap

            def stash(brought):
                pending[ahead] = ex.after_swap(ctxs[0], brought)

            return fn(*args, None, (note_wd, swap_now, stash))[:-1]
        states = []
        if fn is _even_bwd:
            swap, ctx = ex.swap_rider(keys, mats)

            def then(brought):
                states.append(ex.after_swap(ctx, brought))
                return states[0][0]

            res = fn(*args, (swap, then))
        else:
            states.append(pending.pop(host, None) or ex.reduce_begin(keys, mats, tag=f"{host[0]}{host[1]}"))
            if fn is _ffn_bwd and host == _REDUCE_OWN:
                own = []

                def wd_ready(d_wd):
                    own.append(ex.reduce_begin(_keys(_FFA[1:], 0), {("ffa_w_down", 0): d_wd}, tag="own_wd"))
                    return own[0][0]

                def wgu_ready(brought, d_wgu):
                    ex.reduce_finish(own[0], brought)
                    own.append(ex.reduce_begin(_keys(_FFA[:1], 0), {("ffa_w_gate_up", 0): d_wgu}, tag="own_wgu"))
                    return own[1][0]

                res = fn(*args, states[0][0], (wd_ready, wgu_ready, lambda brought: ex.reduce_finish(own[1], brought)))
            else:
                res = fn(*args, states[0][0])
        ex.reduce_finish(states[0], res[-1])
        return res[:-1]

    for i in reversed(range(depth)):
        sv, gl = saved[i], per_layer[i]
        dz, dpp = _ple_bwd_elem(dh, sv["ple"]["gate"], sv["ple"]["pp"], name=f"ple{i}_bwd")
        mats["ple_w_gate", i] = _mm_tn(sv["ple"]["xn"], dz, name=f"ple{i}_dwg")
        mats["ple_w_proj", i] = _mm_tn(p[i], dpp, name=f"ple{i}_dwp")
        dh, gl["ple_norm"] = _mm_nt_rmsbwd([(dz, w["ple_gate"][i].T)], sv["ple"]["h_in"], w["ple_norm"][i:i + 1], dh,
                                           name=f"ple{i}_dx")
        dh, gl["ffb_norm"], mats["ffb_w_gate_up", i], mats["ffb_w_down", i] = reduce_behind(
            ("ffb", i), _ffn_bwd, dh, w["ffb_norm"][i:i + 1], w["ffb"][i], sv["ffb"], f"ffb{i}")
        dh, gl["mix_norm"], gm = reduce_behind(("mix", i), _even_bwd if i % 2 == 0 else _odd_bwd, dh, w, sv["mix"], i)
        for n, g in gm.items():
            if n in REPLICATED:
                grads[n] = g
            else:
                mats[n, 0] = g
        dh, gl["ffa_norm"], mats["ffa_w_gate_up", i], mats["ffa_w_down", i] = reduce_behind(
            ("ffa", i), _ffn_bwd, dh, w["ffa_norm"][i:i + 1], w["ffa"][i], sv["ffa"], f"ffa{i}")
    grads["final_norm"] = d_final.reshape(-1)
    for n in ("ffa_norm", "mix_norm", "ffb_norm", "ple_norm"):
        grads[n] = jnp.concatenate([per_layer[i][n] for i in range(depth)], axis=0)
    if ex is None:
        for n, _ in SHARDED:
            grads[n] = [mats[n, i] for i in range(depth) if (n, i) in mats]
    return loss_vec[0, 0], dh, grads


def _cut_mode(local_shape, axis, ncols):
    return "cols" if axis == 2 and ncols % LANES == 0 else "blk"


class _Exchange:
    def __init__(self, wts):
        self.place = jnp.stack([2 * lax.axis_index("x") + lax.axis_index("y"), lax.axis_index("c")]).astype(jnp.int32)
        self.info = {}
        for n, axis in SHARDED:
            wb = wts[n].astype(BF16)
            mode = _cut_mode(wb.shape, axis, wb.shape[2])
            for i in range(wb.shape[0]):
                self.info[n, i] = dict(shard=wb[i], mode=mode, axis=axis)
        self.halves = {}

    def _modes(self, keys):
        return [self.info[k]["mode"] for k in keys]

    def gather_rider(self, keys):
        return _gather_rider([self.info[k]["shard"] for k in keys], self._modes(keys))

    def gather_finish(self, keys, landed, *, name):
        outs = _gather_forward(landed, [self.info[k]["shard"].shape for k in keys], self._modes(keys), name=name)
        got = {}
        for k, dst in zip(keys, outs):
            if self.info[k]["mode"] == "blk":
                dst = dst.reshape(-1, dst.shape[2]) if self.info[k]["axis"] == 1 else jnp.moveaxis(dst, 0, 1).reshape(dst.shape[1], -1)
            got[k] = dst
        return got

    def gather(self, keys, *, name):
        return self.gather_finish(keys, _run_rider(self.gather_rider(keys), name=name), name=name + "_forward")

    def swap_rider(self, keys, mats):
        modes = self._modes(keys)
        arrs = []
        for k in keys:
            g2, (rr, cc) = mats[k], self.info[k]["shard"].shape
            if self.info[k]["mode"] == "blk":
                g2 = g2.reshape(N_CHIPS, rr, cc) if self.info[k]["axis"] == 1 else g2.reshape(rr, N_CHIPS, cc).transpose(1, 0, 2)
            arrs.append(g2)
        return _swap_rider(arrs, modes), (keys, modes, arrs)

    def after_swap(self, ctx, landed):
        keys, modes, arrs = ctx
        parts = []
        for (n, i), m, a, l in zip(keys, modes, arrs, landed):
            pt = _rs_pair_add(_blk_view(a, m), _blk_view(l, m), self.place, name=f"rs_pair_add_{n}{i}")
            parts.append(pt[0] if m == "cols" else pt)
        return _exchange_rider(parts, modes), keys, parts

    def reduce_begin(self, keys, mats, *, tag):
        rider, ctx = self.swap_rider(keys, mats)
        return self.after_swap(ctx, _run_rider(rider, name=f"rs_pair_swap_{tag}"))

    def reduce_finish(self, state, landed):
        _, keys, parts = state
        for (n, i), m, pt, l in zip(keys, self._modes(keys), parts, landed):
            self.halves[n, i] = _rs_chip_sum(pt, l, m, self.place, name=f"rs_chip_sum_{n}{i}")

    def reduce(self, keys, mats, *, tag):
        state = self.reduce_begin(keys, mats, tag=tag)
        self.reduce_finish(state, _run_rider(state[0], name=f"rs_chip_exchange_{tag}"))

    def join(self, wts):
        keys = list(self.info)
        joined = dict(zip(keys, _rs_pair_join([self.halves[k] for k in keys], name="rs_pair_join")))
        return {n: jnp.stack([joined[n, i] for i in range(wts[n].shape[0])]).reshape(wts[n].shape) for n, _ in SHARDED}


def _small_rows(vals):
    rows = []
    for n in REPLICATED:
        v = vals[n].reshape(-1)
        rows.append(jnp.pad(v, (0, (-v.shape[0]) % FLAT_COLS)).reshape(-1, FLAT_COLS))
    out = jnp.concatenate(rows, axis=0)
    return jnp.pad(out, ((0, (-out.shape[0]) % 8), (0, 0)))


def kernel(x, p, ffa_norm, ffa_w_gate_up, ffa_w_down, mix_norm, ffb_norm, ffb_w_gate_up, ffb_w_down, ple_norm, ple_w_gate, ple_w_proj, ev_w_in, ev_sinks, ev_cq_norm, ev_w_uq, ev_ckv_norm, ev_w_ukv, ev_w_out, od_w_in, od_b_f, od_w_out, final_norm, loss_target, m_ffa_norm, m_ffa_w_gate_up, m_ffa_w_down, m_mix_norm, m_ffb_norm, m_ffb_w_gate_up, m_ffb_w_down, m_ple_norm, m_ple_w_gate, m_ple_w_proj, m_ev_w_in, m_ev_sinks, m_ev_cq_norm, m_ev_w_uq, m_ev_ckv_norm, m_ev_w_ukv, m_ev_w_out, m_od_w_in, m_od_b_f, m_od_w_out, m_final_norm, v_ffa_norm, v_ffa_w_gate_up, v_ffa_w_down, v_mix_norm, v_ffb_norm, v_ffb_w_gate_up, v_ffb_w_down, v_ple_norm, v_ple_w_gate, v_ple_w_proj, v_ev_w_in, v_ev_sinks, v_ev_cq_norm, v_ev_w_uq, v_ev_ckv_norm, v_ev_w_ukv, v_ev_w_out, v_od_w_in, v_od_b_f, v_od_w_out, v_final_norm):
    env = dict(locals())
    wts = {n: env[n] for n in WEIGHT_ORDER}
    mom1 = {n: env["m_" + n] for n in WEIGHT_ORDER}
    mom2 = {n: env["v_" + n] for n in WEIGHT_ORDER}
    ex = _Exchange(wts)

    w = {n: wts[n] for n in REPLICATED}
    _install_weights(w, ex.gather(_GATHER_FIRST, name="weight_gather_first"))

    loss_part, grad_x, grads = _local_step(x[0], p[:, 0], loss_target[0], w, ex)
    loss = lax.psum(loss_part, ("x", "y", "c"))
    gout = ex.join(wts)
    small = _allreduce_small(_small_rows(grads), name="small_allreduce")
    r0 = 0
    for n in REPLICATED:
        size = int(np.prod(wts[n].shape))
        nr = -(-size // FLAT_COLS)
        gout[n] = small[r0:r0 + nr].reshape(-1)[:size].reshape(wts[n].shape)
        r0 += nr

    delta, new_m, new_v = {}, {}, {}
    for n in WEIGHT_ORDER:
        delta[n], new_m[n], new_v[n] = _adamw(wts[n], gout[n], mom1[n], mom2[n], name="adamw_" + n)
    return (loss, grad_x[None], *[gout[n] for n in WEIGHT_ORDER], *[delta[n] for n in WEIGHT_ORDER],
            *[new_m[n] for n in WEIGHT_ORDER], *[new_v[n] for n in WEIGHT_ORDER])
```

```python
import functools
import math

import numpy as np
import jax
import jax.numpy as jnp
from jax import lax
from jax.experimental import pallas as pl
from jax.experimental.pallas import tpu as pltpu

F32 = jnp.float32
BF16 = jnp.bfloat16
NT = (((1,), (1,)), ((), ()))
TN = (((0,), (0,)), ((), ()))
MESH = pl.DeviceIdType.MESH

RMS_EPS = 1e-6
FFN_RES_SCALE = 0.5
A_HEADS, A_KV_HEADS, A_HEAD_DIM, WINDOW = 8, 2, 64, 128
B_HEADS, B_Q_LORA, B_KV_LORA, B_NOPE, B_ROPE, B_V = 8, 256, 128, 64, 32, 64
ROPE_THETA = 10000.0
C_HEADS, C_HEAD_DIM = 16, 64
ADAM_LR, ADAM_B1, ADAM_B2, ADAM_EPS, ADAM_WD, ADAM_STEP = 0.001, 0.9, 0.999, 1e-08, 0.01, 10

N_CHIPS = 4
LANES = 128
FLAT_COLS = 1024
MASK_VALUE = -1e30
VMEM_LIMIT = 48 * 2**20

SHARDED = (
    ("ffa_w_gate_up", 2), ("ffa_w_down", 1), ("ffb_w_gate_up", 2), ("ffb_w_down", 1),
    ("ple_w_gate", 1), ("ple_w_proj", 2), ("ev_w_in", 2), ("ev_w_uq", 2), ("ev_w_ukv", 2),
    ("ev_w_out", 1), ("od_w_in", 2), ("od_w_out", 1))
REPLICATED = ("ffa_norm", "mix_norm", "ffb_norm", "ple_norm", "final_norm",
              "ev_sinks", "ev_cq_norm", "ev_ckv_norm", "od_b_f")
WEIGHT_ORDER = ("ffa_norm", "ffa_w_gate_up", "ffa_w_down", "mix_norm", "ffb_norm", "ffb_w_gate_up",
                "ffb_w_down", "ple_norm", "ple_w_gate", "ple_w_proj", "ev_w_in", "ev_sinks",
                "ev_cq_norm", "ev_w_uq", "ev_ckv_norm", "ev_w_ukv", "ev_w_out", "od_w_in", "od_b_f",
                "od_w_out", "final_norm")


def _cp(*sem):
    return pltpu.CompilerParams(dimension_semantics=sem, vmem_limit_bytes=VMEM_LIMIT)


def _sigmoid(z):
    return 1.0 / (1.0 + jnp.exp(-z))


def _rms_stats(xv):
    r = lax.rsqrt(jnp.mean(xv * xv, axis=-1, keepdims=True) + RMS_EPS)
    return r, xv * r


def _rms_bwd(dxn, xv, g):
    r, xhat = _rms_stats(xv)
    u = dxn * g
    dx = r * (u - xhat * jnp.mean(u * xhat, axis=-1, keepdims=True))
    return dx, dxn * xhat


def _col_tile(k_rows, n, budget_bytes=6 * 2**20):
    if k_rows * n * 4 <= budget_bytes or n % LANES:
        return n
    units = n // LANES
    best = LANES
    for d in range(1, units + 1):
        if units % d == 0 and k_rows * d * LANES * 4 <= budget_bytes:
            best = d * LANES
    return best


def _row_tile(rows, cols, target_elems=2**18):
    if rows * cols <= target_elems or rows % 8:
        return rows
    best = 8
    for d in range(8, rows + 1, 8):
        if rows % d == 0 and d * cols <= target_elems:
            best = d
    return best


def _fox_in_fwd(x, g, w, *, nheads, dh, q_ones, k_ones, name, tm=512):
    s, k = x.shape
    n = w.shape[1]
    wd = nheads * dh
    spare = LANES - dh

    def body(x_ref, g_ref, w_ref, q_ref, k_ref, v_ref, vt_ref, f_ref, xn_ref):
        _, xhat = _rms_stats(x_ref[...])
        xn = (xhat * g_ref[...]).astype(BF16)
        xn_ref[...] = xn
        y = jnp.dot(xn, w_ref[...], preferred_element_type=F32)
        f_ref[...] = y[:, 3 * wd:]
        lane = lax.broadcasted_iota(jnp.int32, (tm, spare), 1)

        def fill(cols):
            return functools.reduce(jnp.logical_or, [lane == c for c in cols]).astype(F32)

        q_fill, k_fill = fill(q_ones), fill(k_ones)
        for h in range(nheads):
            q_ref[h] = jnp.concatenate([y[:, h * dh:(h + 1) * dh], q_fill], axis=-1).astype(BF16)
            k_ref[h] = jnp.concatenate([y[:, wd + h * dh:wd + (h + 1) * dh], k_fill], axis=-1).astype(BF16)
            vh = y[:, 2 * wd + h * dh:2 * wd + (h + 1) * dh]
            v_ref[h] = vh.astype(BF16)
            vt_ref[h] = vh.T.astype(BF16)

    wide = pl.BlockSpec((nheads, tm, LANES), lambda i: (0, i, 0))
    return pl.pallas_call(
        body, name=name, grid=(s // tm,),
        in_specs=[pl.BlockSpec((tm, k), lambda i: (i, 0)), pl.BlockSpec((1, k), lambda i: (0, 0)),
                  pl.BlockSpec((k, n), lambda i: (0, 0))],
        out_specs=[wide, wide, pl.BlockSpec((nheads, tm, dh), lambda i: (0, i, 0)),
                   pl.BlockSpec((nheads, dh, tm), lambda i: (0, 0, i)), pl.BlockSpec((tm, LANES), lambda i: (i, 0)),
                   pl.BlockSpec((tm, k), lambda i: (i, 0))],
        out_shape=[jax.ShapeDtypeStruct((nheads, s, LANES), BF16)] * 2
        + [jax.ShapeDtypeStruct((nheads, s, dh), BF16), jax.ShapeDtypeStruct((nheads, dh, s), BF16),
           jax.ShapeDtypeStruct((s, LANES), F32), jax.ShapeDtypeStruct((s, k), BF16)],
        compiler_params=_cp("arbitrary"))(x, g, w)


def _merge_heads(dq, dk, dvt, *, dh, q_col, k_col, name, tm=512):
    nheads, s, _ = dq.shape

    def body(dq_ref, dk_ref, dvt_ref, o_ref, cols_ref):
        pieces = [dq_ref[h][:, :dh] for h in range(nheads)] + [dk_ref[h][:, :dh] for h in range(nheads)]
        pieces += [dvt_ref[h].T for h in range(nheads)]
        o_ref[...] = jnp.concatenate(pieces, axis=-1)
        lane = lax.broadcasted_iota(jnp.int32, (tm, LANES), 1)
        cols = jnp.zeros((tm, LANES), F32)
        for h in range(nheads):
            cols = jnp.where(lane == h, jnp.broadcast_to(dq_ref[h][:, q_col:q_col + 1], (tm, LANES)), cols)
            cols = jnp.where(lane == nheads + h, jnp.broadcast_to(dk_ref[h][:, k_col:k_col + 1], (tm, LANES)), cols)
        cols_ref[...] = cols

    wide = pl.BlockSpec((nheads, tm, LANES), lambda i: (0, i, 0))
    return pl.pallas_call(
        body, name=name, grid=(s // tm,),
        in_specs=[wide, wide, pl.BlockSpec((nheads, dh, tm), lambda i: (0, 0, i))],
        out_specs=[pl.BlockSpec((tm, 3 * nheads * dh), lambda i: (i, 0)), pl.BlockSpec((tm, LANES), lambda i: (i, 0))],
        out_shape=[jax.ShapeDtypeStruct((s, 3 * nheads * dh), F32), jax.ShapeDtypeStruct((s, LANES), F32)],
        compiler_params=_cp("arbitrary"))(dq, dk, dvt)


def _row_call(body, n_rows, ins, outs, *, name, tm=512):
    def spec(a, axis):
        shape = a.shape
        if axis is None:
            return pl.BlockSpec(shape, lambda i: (0,) * len(shape))
        blk = tuple(tm if d == axis else n for d, n in enumerate(shape))
        return pl.BlockSpec(blk, lambda i: tuple(i if d == axis else 0 for d in range(len(shape))))

    return pl.pallas_call(
        body, name=name, grid=(n_rows // tm,), in_specs=[spec(a, ax) for a, ax in ins],
        out_specs=[spec(a, ax) for a, ax in outs], out_shape=[a for a, _ in outs],
        compiler_params=_cp("arbitrary"))(*[a for a, _ in ins])


def _sds(shape, dtype):
    return jax.ShapeDtypeStruct(shape, dtype)


def _ev_in_fwd(x, g, w, *, name):
    s, k = x.shape
    d = A_HEAD_DIM

    def body(x_ref, g_ref, w_ref, q_ref, k_ref, v_ref, vt_ref, cq_ref, ckv_ref, kr_ref, xn_ref):
        _, xhat = _rms_stats(x_ref[...])
        xn = (xhat * g_ref[...]).astype(BF16)
        xn_ref[...] = xn
        y = jnp.dot(xn, w_ref[...], preferred_element_type=F32)
        for h in range(A_HEADS):
            q_ref[h] = y[:, h * d:(h + 1) * d].astype(BF16)
        for h in range(A_KV_HEADS):
            k_ref[h] = y[:, 512 + h * d:512 + (h + 1) * d].astype(BF16)
            vh = y[:, 640 + h * d:640 + (h + 1) * d]
            v_ref[h] = vh.astype(BF16)
            vt_ref[h] = vh.T.astype(BF16)
        cq_ref[...] = y[:, 768:1024]
        ckv_ref[...] = y[:, 1024:1152]
        kr_ref[...] = y[:, 1152:1280]

    return _row_call(
        body, s, [(x, 0), (g, None), (w, None)],
        [(_sds((A_HEADS, s, d), BF16), 1), (_sds((A_KV_HEADS, s, d), BF16), 1), (_sds((A_KV_HEADS, s, d), BF16), 1),
         (_sds((A_KV_HEADS, d, s), BF16), 2), (_sds((s, B_Q_LORA), F32), 0), (_sds((s, B_KV_LORA), F32), 0),
         (_sds((s, LANES), F32), 0), (_sds((s, k), BF16), 0)], name=name)


def _ev_q_fwd(x, g, w, cos, sin, *, name):
    s, k = x.shape
    rot = B_HEADS * B_ROPE

    def body(x_ref, g_ref, w_ref, c_ref, s_ref, q_ref, xn_ref):
        _, xhat = _rms_stats(x_ref[...])
        xn = (xhat * g_ref[...]).astype(BF16)
        xn_ref[...] = xn
        y = jnp.dot(xn, w_ref[...], preferred_element_type=F32)
        ro = y[:, 512:512 + rot] * c_ref[...] + y[:, 512 + rot:] * s_ref[...]
        zero = jnp.zeros((y.shape[0], LANES - B_NOPE - B_ROPE), F32)
        for h in range(B_HEADS):
            q_ref[h] = jnp.concatenate([y[:, h * B_NOPE:(h + 1) * B_NOPE], ro[:, h * B_ROPE:(h + 1) * B_ROPE], zero],
                                       axis=-1).astype(BF16)

    return _row_call(body, s, [(x, 0), (g, None), (w, None), (cos, 0), (sin, 0)],
                     [(_sds((B_HEADS, s, LANES), BF16), 1), (_sds((s, k), BF16), 0)], name=name)


def _ev_kv_fwd(x, g, w, kro, *, name):
    s, k = x.shape
    per = B_NOPE + B_V

    def body(x_ref, g_ref, w_ref, kr_ref, k_ref, v_ref, vt_ref, xn_ref):
        _, xhat = _rms_stats(x_ref[...])
        xn = (xhat * g_ref[...]).astype(BF16)
        xn_ref[...] = xn
        y = jnp.dot(xn, w_ref[...], preferred_element_type=F32)
        kr = kr_ref[...]
        zero = jnp.zeros((y.shape[0], LANES - B_NOPE - B_ROPE), F32)
        for h in range(B_HEADS):
            k_ref[h] = jnp.concatenate([y[:, h * per:h * per + B_NOPE], kr, zero], axis=-1).astype(BF16)
            vh = y[:, h * per + B_NOPE:(h + 1) * per]
            v_ref[h] = vh.astype(BF16)
            vt_ref[h] = vh.T.astype(BF16)

    return _row_call(body, s, [(x, 0), (g, None), (w, None), (kro, 0)],
                     [(_sds((B_HEADS, s, LANES), BF16), 1), (_sds((B_HEADS, s, B_V), BF16), 1),
                      (_sds((B_HEADS, B_V, s), BF16), 2), (_sds((s, k), BF16), 0)], name=name)


def _ev_q_merge(dq, cos, sin, *, name):
    nh, s, _ = dq.shape

    def body(dq_ref, c_ref, s_ref, o_ref):
        dro = jnp.concatenate([dq_ref[h][:, B_NOPE:B_NOPE + B_ROPE] for h in range(nh)], axis=-1)
        o_ref[...] = jnp.concatenate([dq_ref[h][:, :B_NOPE] for h in range(nh)] + [dro * c_ref[...], dro * s_ref[...]], axis=-1)

    return _row_call(body, s, [(dq, 1), (cos, 0), (sin, 0)], [(_sds((s, 2 * nh * B_NOPE), F32), 0)], name=name)[0]


def _ev_kv_merge(dk, dvt, cos, sin, *, name):
    nh, s, _ = dk.shape

    def body(dk_ref, dvt_ref, c_ref, s_ref, o_ref, kr_ref):
        pieces = []
        tot = None
        for h in range(nh):
            pieces += [dk_ref[h][:, :B_NOPE], dvt_ref[h].T]
            rot = dk_ref[h][:, B_NOPE:B_NOPE + B_ROPE]
            tot = rot if tot is None else tot + rot
        o_ref[...] = jnp.concatenate(pieces, axis=-1)
        kr_ref[...] = jnp.concatenate([tot * c_ref[...], tot * s_ref[...], jnp.zeros((tot.shape[0], LANES - 2 * B_ROPE), F32)],
                                      axis=-1)

    return _row_call(body, s, [(dk, 1), (dvt, 2), (cos, 0), (sin, 0)],
                     [(_sds((s, nh * (B_NOPE + B_V)), F32), 0), (_sds((s, LANES), F32), 0)], name=name)


def _ev_in_merge(dq, dk, dvt, dcq, dckv, dkr, *, name):
    s = dcq.shape[0]

    def body(dq_ref, dk_ref, dvt_ref, cq_ref, ckv_ref, kr_ref, o_ref):
        pieces = [dq_ref[h] for h in range(A_HEADS)] + [dk_ref[h] for h in range(A_KV_HEADS)]
        pieces += [dvt_ref[h].T for h in range(A_KV_HEADS)] + [cq_ref[...], ckv_ref[...], kr_ref[...]]
        o_ref[...] = jnp.concatenate(pieces, axis=-1)

    return _row_call(body, s, [(dq, 1), (dk, 1), (dvt, 2), (dcq, 0), (dckv, 0), (dkr, 0)],
                     [(_sds((s, 1280), F32), 0)], name=name)[0]


def _ffn_up(x, g, wgu, *, name, tm=512, rider=None):
    s, k = x.shape
    f = wgu.shape[1] // 2
    tn = _col_tile(k, f)
    nj = f // tn

    def body(x_ref, g_ref, wg_ref, wu_ref, dgate_ref, dup_ref, act_ref, xn_ref, xn_sc):
        @pl.when(pl.program_id(1) == 0)
        def _():
            _, xhat = _rms_stats(x_ref[...])
            xn = (xhat * g_ref[...]).astype(BF16)
            xn_sc[...] = xn
            xn_ref[...] = xn

        xn = xn_sc[...]
        gg = jnp.dot(xn, wg_ref[...], preferred_element_type=F32)
        uu = jnp.dot(xn, wu_ref[...], preferred_element_type=F32)
        sg = _sigmoid(gg)
        silu = gg * sg
        dgate_ref[...] = (uu * (sg * (1.0 + gg * (1.0 - sg)))).astype(BF16)
        dup_ref[...] = silu.astype(BF16)
        act_ref[...] = (silu * uu).astype(BF16)

    tile = pl.BlockSpec((tm, tn), lambda i, j: (i, j))
    return _call_with_rider(
        body, rider, name=name, grid=(s // tm, nj),
        in_specs=[pl.BlockSpec((tm, k), lambda i, j: (i, 0)), pl.BlockSpec((1, k), lambda i, j: (0, 0)),
                  pl.BlockSpec((k, tn), lambda i, j: (0, j)), pl.BlockSpec((k, tn), lambda i, j: (0, j + nj))],
        out_specs=[tile, tile, tile, pl.BlockSpec((tm, k), lambda i, j: (i, 0))],
        out_shape=[jax.ShapeDtypeStruct((s, f), BF16)] * 3 + [jax.ShapeDtypeStruct((s, k), BF16)],
        scratch_shapes=[pltpu.VMEM((tm, k), BF16)],
        compiler_params=_cp("arbitrary", "arbitrary"), args=(x, g, wgu, wgu))


def _mm_res_fwd(a, w, res, *, scale, name, tm=512, a_t=False):
    k, n = w.shape
    s = res.shape[0]

    def body(a_ref, w_ref, r_ref, o_ref):
        prod = (lax.dot_general(a_ref[...], w_ref[...], TN, preferred_element_type=F32) if a_t
                else jnp.dot(a_ref[...], w_ref[...], preferred_element_type=F32))
        o_ref[...] = r_ref[...] + scale * prod

    a_spec = pl.BlockSpec((k, tm), lambda i: (0, i)) if a_t else pl.BlockSpec((tm, k), lambda i: (i, 0))
    return pl.pallas_call(
        body, name=name, grid=(s // tm,),
        in_specs=[a_spec, pl.BlockSpec((k, n), lambda i: (0, 0)),
                  pl.BlockSpec((tm, n), lambda i: (i, 0))],
        out_specs=pl.BlockSpec((tm, n), lambda i: (i, 0)),
        out_shape=jax.ShapeDtypeStruct((s, n), F32),
        compiler_params=_cp("arbitrary"))(a, w, res)


def _ffn_down_bwd(dh, wd, dact_dgate, dact_dup, *, scale, name, tm=512, rider=None):
    s, d = dh.shape
    f = wd.shape[0]
    tn = _col_tile(d, f)

    def body(dh_ref, wd_ref, fg_ref, fu_ref, dg_ref, du_ref):
        dhb = (dh_ref[...] * scale).astype(BF16)
        da = lax.dot_general(dhb, wd_ref[...], NT, preferred_element_type=F32)
        dg_ref[...] = (da * fg_ref[...].astype(F32)).astype(BF16)
        du_ref[...] = (da * fu_ref[...].astype(F32)).astype(BF16)

    tile = pl.BlockSpec((tm, tn), lambda i, j: (i, j))
    return _call_with_rider(
        body, rider, name=name, grid=(s // tm, f // tn),
        in_specs=[pl.BlockSpec((tm, d), lambda i, j: (i, 0)), pl.BlockSpec((tn, d), lambda i, j: (j, 0)), tile, tile],
        out_specs=[tile, tile],
        out_shape=[jax.ShapeDtypeStruct((s, f), BF16)] * 2, scratch_shapes=[],
        compiler_params=_cp("arbitrary", "arbitrary"), args=(dh, wd, dact_dgate, dact_dup))


def _mm_tn(a, bs, *, name, b_scale=1.0, ts=512, rider=None, a_t=False):
    bs = list(bs) if isinstance(bs, (list, tuple)) else [bs]
    k, s = a.shape if a_t else a.shape[::-1]
    n = bs[0].shape[1]
    tn = _col_tile(k, n, 12 * 2**20)
    per = n // tn

    def body(a_ref, *refs):
        b_refs, o_ref = refs[:-1], refs[-1]
        j = pl.program_id(0)

        @pl.when(pl.program_id(1) == 0)
        def _():
            o_ref[...] = jnp.zeros_like(o_ref)

        for m, b_ref in enumerate(b_refs):
            def acc(b_ref=b_ref):
                bv = b_ref[...]
                if b_scale != 1.0:
                    bv = bv * b_scale
                av = a_ref[...].astype(BF16)
                o_ref[...] += (jnp.dot(av, bv.astype(BF16), preferred_element_type=F32) if a_t
                               else lax.dot_general(av, bv.astype(BF16), TN, preferred_element_type=F32))

            if len(b_refs) == 1:
                acc()
            else:
                pl.when(jnp.logical_and(j >= m * per, j < (m + 1) * per))(acc)

    def b_spec(m):
        def idx(j, t):
            mine = jnp.logical_and(j >= m * per, j < (m + 1) * per)
            return (jnp.where(mine, t, 0), jnp.clip(j - m * per, 0, per - 1))
        return pl.BlockSpec((ts, tn), idx)

    (out,), rode = _call_with_rider(
        body, rider, name=name, grid=(per * len(bs), s // ts),
        in_specs=[pl.BlockSpec((k, ts), lambda j, t: (0, t)) if a_t else pl.BlockSpec((ts, k), lambda j, t: (t, 0))]
        + [b_spec(m) for m in range(len(bs))],
        out_specs=[pl.BlockSpec((k, tn), lambda j, t: (0, j))],
        out_shape=[jax.ShapeDtypeStruct((k, n * len(bs)), F32)], scratch_shapes=[],
        compiler_params=_cp("arbitrary", "arbitrary"), args=(a, *bs))
    return out if rider is None else (out, rode)


def _mm_nt_t(dy, w, *, name, tm=512):
    s, n = dy.shape
    k = w.shape[0]

    def body(dy_ref, w_ref, o_ref):
        o_ref[...] = lax.dot_general(w_ref[...], dy_ref[...].astype(BF16), NT, preferred_element_type=F32).astype(BF16)

    return pl.pallas_call(
        body, name=name, grid=(s // tm,),
        in_specs=[pl.BlockSpec((tm, n), lambda i: (i, 0)), pl.BlockSpec((k, n), lambda i: (0, 0))],
        out_specs=pl.BlockSpec((k, tm), lambda i: (0, i)),
        out_shape=jax.ShapeDtypeStruct((k, s), BF16),
        compiler_params=_cp("arbitrary"))(dy, w)


def _mm_nt_rmsbwd(pairs, x, g, dres, *, name, tm=512, rider=None):
    s, k = x.shape
    npairs = len(pairs)
    pairs = [pr if len(pr) == 3 else (pr[0], pr[1], 0) for pr in pairs]

    def body(*refs):
        dy_refs = refs[0:2 * npairs:2]
        w_refs = refs[1:2 * npairs:2]
        rest = refs[2 * npairs:]
        x_ref, g_ref = rest[0], rest[1]
        if dres is None:
            dx_ref, dg_ref = rest[2], rest[3]
        else:
            dres_ref, dx_ref, dg_ref = rest[2], rest[3], rest[4]
        dxn = None
        for dy_ref, w_ref in zip(dy_refs, w_refs):
            t = lax.dot_general(dy_ref[...].astype(BF16), w_ref[...], NT, preferred_element_type=F32)
            dxn = t if dxn is None else dxn + t
        dx, dgrow = _rms_bwd(dxn, x_ref[...], g_ref[...])
        if dres is not None:
            dx = dx + dres_ref[...]
        dx_ref[...] = dx

        @pl.when(pl.program_id(0) == 0)
        def _():
            dg_ref[...] = jnp.zeros_like(dg_ref)

        dg_ref[...] += jnp.sum(dgrow, axis=0, keepdims=True)

    in_specs, args = [], []
    for dy, w, cb in pairs:
        n = dy.shape[1]
        in_specs += [pl.BlockSpec((tm, n), lambda i: (i, 0)),
                     pl.BlockSpec((k, n), lambda i, cb=cb: (0, cb), pipeline_mode=pl.Buffered(1))]
        args += [dy, w]
    row = pl.BlockSpec((tm, k), lambda i: (i, 0))
    vec = pl.BlockSpec((1, k), lambda i: (0, 0))
    in_specs += [row, vec]
    args += [x, g]
    if dres is not None:
        in_specs.append(row)
        args.append(dres)
    (dx, dgain), rode = _call_with_rider(
        body, rider, name=name, grid=(s // tm,), in_specs=in_specs, out_specs=[row, vec],
        out_shape=[jax.ShapeDtypeStruct((s, k), F32), jax.ShapeDtypeStruct((1, k), F32)], scratch_shapes=[],
        compiler_params=_cp("arbitrary"), args=args)
    return (dx, dgain) if rider is None else (dx, dgain, rode)


def _ple_fwd(h, g, wg, p, wp, *, name, tm=512):
    s, d = h.shape
    pd = p.shape[1]

    def body(h_ref, g_ref, wg_ref, p_ref, wp_ref, o_ref, xn_ref, gate_ref, pp_ref):
        hv = h_ref[...]
        _, xhat = _rms_stats(hv)
        xn = (xhat * g_ref[...]).astype(BF16)
        xn_ref[...] = xn
        gate = _sigmoid(jnp.dot(xn, wg_ref[...], preferred_element_type=F32))
        pp = jnp.dot(p_ref[...].astype(BF16), wp_ref[...], preferred_element_type=F32)
        gate_ref[...] = gate.astype(BF16)
        pp_ref[...] = pp.astype(BF16)
        o_ref[...] = hv + gate * pp

    row = pl.BlockSpec((tm, d), lambda i: (i, 0))
    return pl.pallas_call(
        body, name=name, grid=(s // tm,),
        in_specs=[row, pl.BlockSpec((1, d), lambda i: (0, 0)), pl.BlockSpec((d, d), lambda i: (0, 0)),
                  pl.BlockSpec((tm, pd), lambda i: (i, 0)), pl.BlockSpec((pd, d), lambda i: (0, 0))],
        out_specs=[row, row, row, row],
        out_shape=[jax.ShapeDtypeStruct((s, d), F32)] + [jax.ShapeDtypeStruct((s, d), BF16)] * 3,
        compiler_params=_cp("arbitrary"))(h, g, wg, p, wp)


def _ple_bwd_elem(dh, gate, pp, *, name, tm=512):
    s, d = dh.shape

    def body(dh_ref, gate_ref, pp_ref, dz_ref, dpp_ref):
        dhv = dh_ref[...]
        gt = gate_ref[...].astype(F32)
        dz_ref[...] = (dhv * pp_ref[...].astype(F32) * (gt * (1.0 - gt))).astype(BF16)
        dpp_ref[...] = (dhv * gt).astype(BF16)

    row = pl.BlockSpec((tm, d), lambda i: (i, 0))
    return pl.pallas_call(
        body, name=name, grid=(s // tm,), in_specs=[row, row, row], out_specs=[row, row],
        out_shape=[jax.ShapeDtypeStruct((s, d), BF16)] * 2,
        compiler_params=_cp("arbitrary"))(dh, gate, pp)


def _final_loss(h, g, tgt, *, name, tm=512):
    s, d = h.shape

    def body(h_ref, g_ref, t_ref, loss_ref, dh_ref, dg_ref):
        @pl.when(pl.program_id(0) == 0)
        def _():
            loss_ref[...] = jnp.zeros_like(loss_ref)
            dg_ref[...] = jnp.zeros_like(dg_ref)

        hv = h_ref[...]
        gv = g_ref[...]
        _, xhat = _rms_stats(hv)
        err = xhat * gv - t_ref[...]
        per_row = jnp.mean(err * err, axis=-1, keepdims=True)
        loss_ref[...] += 0.5 * jnp.sum(per_row, axis=0, keepdims=True)
        dx, dgrow = _rms_bwd(err * (1.0 / d), hv, gv)
        dh_ref[...] = dx
        dg_ref[...] += jnp.sum(dgrow, axis=0, keepdims=True)

    row = pl.BlockSpec((tm, d), lambda i: (i, 0))
    vec = pl.BlockSpec((1, d), lambda i: (0, 0))
    return pl.pallas_call(
        body, name=name, grid=(s // tm,), in_specs=[row, vec, row],
        out_specs=[pl.BlockSpec((1, LANES), lambda i: (0, 0)), row, vec],
        out_shape=[jax.ShapeDtypeStruct((1, LANES), F32), jax.ShapeDtypeStruct((s, d), F32),
                   jax.ShapeDtypeStruct((1, d), F32)],
        compiler_params=_cp("arbitrary"))(h, g, tgt)


def _rope_fwd(y1, y2, cos, sin, *, name, tm=512):
    s, r = y1.shape

    def body(a_ref, b_ref, c_ref, s_ref, o_ref):
        o_ref[...] = a_ref[...] * c_ref[...] + b_ref[...] * s_ref[...]

    row = pl.BlockSpec((tm, r), lambda i: (i, 0))
    return pl.pallas_call(
        body, name=name, grid=(s // tm,), in_specs=[row] * 4, out_specs=row,
        out_shape=jax.ShapeDtypeStruct((s, r), F32), compiler_params=_cp("arbitrary"))(y1, y2, cos, sin)


def _split3(v):
    h1 = v.astype(BF16)
    r1 = v - h1.astype(F32)
    h2 = r1.astype(BF16)
    h3 = (r1 - h2.astype(F32)).astype(BF16)
    return h1, h2, h3


def _tri(tb, upper):
    r = lax.broadcasted_iota(jnp.int32, (tb, tb), 0)
    c = lax.broadcasted_iota(jnp.int32, (tb, tb), 1)
    return jnp.where((r <= c) if upper else (r >= c), 1.0, 0.0).astype(BF16)


def _fox_gate_fwd(ft, bf, *, out_scale, name, tb=512):
    nh, s = ft.shape

    def body(f_ref, b_ref, o_ref, carry):
        @pl.when(pl.program_id(0) == 0)
        def _():
            carry[...] = jnp.zeros_like(carry)

        z = f_ref[...] + b_ref[...]
        lf = jnp.minimum(z, 0.0) - jnp.log(1.0 + jnp.exp(-jnp.abs(z)))
        tri = _tri(tb, True)
        cs = sum(jnp.dot(t, tri, preferred_element_type=F32) for t in _split3(lf)) + carry[...]
        for n, term in enumerate(_split3(cs * out_scale)):
            o_ref[n] = term
        carry[...] += jnp.sum(lf, axis=-1, keepdims=True)

    return pl.pallas_call(
        body, name=name, grid=(s // tb,),
        in_specs=[pl.BlockSpec((nh, tb), lambda t: (0, t)), pl.BlockSpec((nh, 1), lambda t: (0, 0))],
        out_specs=pl.BlockSpec((3, nh, tb), lambda t: (0, 0, t)),
        out_shape=jax.ShapeDtypeStruct((3, nh, s), BF16),
        scratch_shapes=[pltpu.VMEM((nh, 1), F32)], compiler_params=_cp("arbitrary"))(ft, bf)


def _fox_gate_bwd(drow, dcol, ft, bf, *, inv_scale, name, tb=512):
    nh, s = ft.shape
    nb = s // tb

    def body(dr_ref, dc_ref, f_ref, b_ref, df_ref, db_ref, carry):
        @pl.when(pl.program_id(0) == 0)
        def _():
            carry[...] = jnp.zeros_like(carry)
            db_ref[...] = jnp.zeros_like(db_ref)

        dc = (dr_ref[...] - dc_ref[...]) * inv_scale
        tri = _tri(tb, False)
        suf = sum(jnp.dot(t, tri, preferred_element_type=F32) for t in _split3(dc)) + carry[...]
        z = f_ref[...] + b_ref[...]
        dz = suf * (1.0 / (1.0 + jnp.exp(z)))
        df_ref[...] = dz
        db_ref[...] += jnp.sum(dz, axis=-1, keepdims=True)
        carry[...] += jnp.sum(dc, axis=-1, keepdims=True)

    rev = pl.BlockSpec((nh, tb), lambda t: (0, nb - 1 - t))
    one = pl.BlockSpec((nh, 1), lambda t: (0, 0))
    return pl.pallas_call(
        body, name=name, grid=(nb,), in_specs=[rev, rev, rev, one], out_specs=[rev, one],
        out_shape=[jax.ShapeDtypeStruct((nh, s), F32), jax.ShapeDtypeStruct((nh, 1), F32)],
        scratch_shapes=[pltpu.VMEM((nh, 1), F32)], compiler_params=_cp("arbitrary"))(drow, dcol, ft, bf)


def _tri_fwd(t, nq):
    i = sum((t >= (r * (r + 1)) // 2).astype(jnp.int32) for r in range(1, nq))
    return i, t - (i * (i + 1)) // 2


def _tri_bwd(t, nq):
    j = sum((t >= r * nq - (r * (r - 1)) // 2).astype(jnp.int32) for r in range(1, nq))
    return j, j + t - (j * nq - (j * (j - 1)) // 2)


def _scores_t(k, q, *, scale, diag):
    s = lax.dot_general(k, q, NT, preferred_element_type=F32) * scale
    if diag:
        r = lax.broadcasted_iota(jnp.int32, s.shape, 0)
        c = lax.broadcasted_iota(jnp.int32, s.shape, 1)
        s = jnp.where(r <= c, s, MASK_VALUE)
    return s


def _causal_fwd_t(q, k, vt, *, scale, name, tq, hb=2, rider=None):
    nh, s, dq = q.shape
    dv = vt.shape[1]
    nq = s // tq
    nsteps = (nq * (nq + 1)) // 2

    def body(q_ref, k_ref, vt_ref, o_ref, lse_ref, m_sc, l_sc, acc_sc):
        i, j = _tri_fwd(pl.program_id(1), nq)

        @pl.when(j == 0)
        def _():
            m_sc[...] = jnp.full_like(m_sc, MASK_VALUE)
            l_sc[...] = jnp.zeros_like(l_sc)
            acc_sc[...] = jnp.zeros_like(acc_sc)

        def step(diag):
            for u in range(hb):
                sc = _scores_t(k_ref[u], q_ref[u], scale=scale, diag=diag)
                m_prev = m_sc[u]
                m_new = jnp.maximum(m_prev, jnp.max(sc, axis=0, keepdims=True))
                alpha = jnp.exp(m_prev - m_new)
                pr = jnp.exp(sc - m_new)
                l_new = alpha * l_sc[u] + jnp.sum(pr, axis=0, keepdims=True)
                acc = alpha * acc_sc[u] + jnp.dot(vt_ref[u], pr.astype(BF16), preferred_element_type=F32)
                if diag:
                    o_ref[u] = (acc / l_new).astype(BF16)
                    lse_ref[u] = m_new + jnp.log(l_new)
                else:
                    m_sc[u], l_sc[u], acc_sc[u] = m_new, l_new, acc

        pl.when(j < i)(functools.partial(step, False))
        pl.when(j == i)(functools.partial(step, True))

    def qi(t):
        return _tri_fwd(t, nq)[0]

    def kj(t):
        return _tri_fwd(t, nq)[1]

    return _call_with_rider(
        body, rider, name=name, grid=(nh // hb, nsteps),
        in_specs=[pl.BlockSpec((hb, tq, dq), lambda hp, t: (hp, qi(t), 0)),
                  pl.BlockSpec((hb, tq, dq), lambda hp, t: (hp, kj(t), 0)),
                  pl.BlockSpec((hb, dv, tq), lambda hp, t: (hp, 0, kj(t)))],
        out_specs=[pl.BlockSpec((hb, dv, tq), lambda hp, t: (hp, 0, qi(t))),
                   pl.BlockSpec((hb, 1, tq), lambda hp, t: (hp, 0, qi(t)))],
        out_shape=[jax.ShapeDtypeStruct((nh, dv, s), BF16), jax.ShapeDtypeStruct((nh, 1, s), F32)],
        scratch_shapes=[pltpu.VMEM((hb, 1, tq), F32), pltpu.VMEM((hb, 1, tq), F32), pltpu.VMEM((hb, dv, tq), F32)],
        compiler_params=_cp("arbitrary", "arbitrary"), args=(q, k, vt))


def _causal_bwd_t(q, k, v, ot, dot_, lse, *, scale, name, tq, hb=2, rider=None):
    nh, s, dq = q.shape
    dv = v.shape[-1]
    nq = s // tq
    nsteps = (nq * (nq + 1)) // 2

    def body(q_ref, k_ref, v_ref, ot_ref, dot_ref, lse_ref, dq_ref, dk_ref, dvt_ref):
        t = pl.program_id(1)
        j, i = _tri_bwd(t, nq)

        @pl.when(t == 0)
        def _():
            dq_ref[...] = jnp.zeros_like(dq_ref)

        def step(diag):
            rows = pl.ds(pl.multiple_of(i * tq, tq), tq)
            for u in range(hb):
                qv, kv, dov = q_ref[u], k_ref[u], dot_ref[u]
                pr = jnp.exp(_scores_t(kv, qv, scale=scale, diag=diag) - lse_ref[u])
                dp = jnp.dot(v_ref[u], dov, preferred_element_type=F32)
                delta = jnp.sum(dov.astype(F32) * ot_ref[u].astype(F32), axis=0, keepdims=True)
                dsb = ((pr * (dp - delta)) * scale).astype(BF16)
                d_v = lax.dot_general(dov, pr.astype(BF16), NT, preferred_element_type=F32)
                d_k = jnp.dot(dsb, qv, preferred_element_type=F32)
                if diag:
                    dvt_ref[u], dk_ref[u] = d_v, d_k
                else:
                    dvt_ref[u] += d_v
                    dk_ref[u] += d_k
                dq_ref[u, rows, :] += lax.dot_general(dsb, kv, TN, preferred_element_type=F32)

        pl.when(i > j)(functools.partial(step, False))
        pl.when(i == j)(functools.partial(step, True))

    def qi(t):
        return _tri_bwd(t, nq)[1]

    def kj(t):
        return _tri_bwd(t, nq)[0]

    rows_q = pl.BlockSpec((hb, tq, dq), lambda hp, t: (hp, qi(t), 0))
    rows_k = pl.BlockSpec((hb, tq, dq), lambda hp, t: (hp, kj(t), 0))
    lanes_q = pl.BlockSpec((hb, dv, tq), lambda hp, t: (hp, 0, qi(t)))
    return _call_with_rider(
        body, rider, name=name, grid=(nh // hb, nsteps),
        in_specs=[rows_q, rows_k, pl.BlockSpec((hb, tq, dv), lambda hp, t: (hp, kj(t), 0)), lanes_q, lanes_q,
                  pl.BlockSpec((hb, 1, tq), lambda hp, t: (hp, 0, qi(t)))],
        out_specs=[pl.BlockSpec((hb, s, dq), lambda hp, t: (hp, 0, 0)), rows_k,
                   pl.BlockSpec((hb, dv, tq), lambda hp, t: (hp, 0, kj(t)))],
        out_shape=[jax.ShapeDtypeStruct((nh, s, dq), F32), jax.ShapeDtypeStruct((nh, s, dq), F32),
                   jax.ShapeDtypeStruct((nh, dv, s), F32)],
        scratch_shapes=[], compiler_params=_cp("arbitrary", "arbitrary"), args=(q, k, v, ot, dot_, lse))


def _swa_scores_t(k, q, dist, ok, *, scale, slope):
    s = lax.dot_general(k, q, NT, preferred_element_type=F32) * scale - slope * dist.astype(F32)
    return jnp.where(ok, s, MASK_VALUE)


def _swa_geometry(tb, w, has_other):
    r = lax.broadcasted_iota(jnp.int32, (tb, tb), 0)
    c = lax.broadcasted_iota(jnp.int32, (tb, tb), 1)
    d_same = c - r
    ok_same = jnp.logical_and(d_same >= 0, d_same < w)

    def other(ncols):
        rr = lax.broadcasted_iota(jnp.int32, (w, ncols), 0)
        cc = lax.broadcasted_iota(jnp.int32, (w, ncols), 1)
        dd = cc + w - rr
        return dd, jnp.logical_and(dd < w, has_other)

    return (d_same, ok_same), other


def _swa_fwd_t(q, k, vt, slopes_sinks, *, scale, window, name, tb=256):
    nh, s, d = q.shape
    nkv = k.shape[0]
    grp = nh // nkv
    w = window
    per = tb // w
    assert tb % w == 0

    def body(q_ref, kc_ref, kp_ref, vc_ref, vp_ref, ss_ref, o_ref, lse_ref):
        kvh, i = pl.program_id(0), pl.program_id(1)
        (d_c, ok_c), other = _swa_geometry(tb, w, i > 0)
        d_p, ok_p = other(tb)
        for g in range(grp):
            h = kvh * grp + g
            slope, sink = ss_ref[0, h], ss_ref[1, h]
            qg = q_ref[g]
            s_c = _swa_scores_t(kc_ref[...], qg, d_c, ok_c, scale=scale, slope=slope)
            s_p = _swa_scores_t(kp_ref[...], qg, d_p, ok_p, scale=scale, slope=slope)
            m = jnp.maximum(jnp.maximum(jnp.max(s_c, axis=0, keepdims=True), jnp.max(s_p, axis=0, keepdims=True)), sink)
            p_c, p_p = jnp.exp(s_c - m), jnp.exp(s_p - m)
            l = jnp.sum(p_c, axis=0, keepdims=True) + jnp.sum(p_p, axis=0, keepdims=True) + jnp.exp(sink - m)
            acc = (jnp.dot(vc_ref[...], p_c.astype(BF16), preferred_element_type=F32)
                   + jnp.dot(vp_ref[...], p_p.astype(BF16), preferred_element_type=F32))
            o_ref[g] = (acc / l).astype(BF16)
            lse_ref[g] = m + jnp.log(l)

    def prev(i):
        return jnp.maximum(i * per - 1, 0)

    return pl.pallas_call(
        body, name=name, grid=(nkv, s // tb),
        in_specs=[pl.BlockSpec((grp, tb, d), lambda kh, i: (kh, i, 0)),
                  pl.BlockSpec((None, tb, d), lambda kh, i: (kh, i, 0)),
                  pl.BlockSpec((None, w, d), lambda kh, i: (kh, prev(i), 0)),
                  pl.BlockSpec((None, d, tb), lambda kh, i: (kh, 0, i)),
                  pl.BlockSpec((None, d, w), lambda kh, i: (kh, 0, prev(i))),
                  pl.BlockSpec(memory_space=pltpu.SMEM)],
        out_specs=[pl.BlockSpec((grp, d, tb), lambda kh, i: (kh, 0, i)), pl.BlockSpec((grp, 1, tb), lambda kh, i: (kh, 0, i))],
        out_shape=[jax.ShapeDtypeStruct((nh, d, s), BF16), jax.ShapeDtypeStruct((nh, 1, s), F32)],
        compiler_params=_cp("arbitrary", "arbitrary"))(q, k, k, vt, vt, slopes_sinks)


def _swa_bwd_t(q, k, v, ot, dot_, lse, slopes_sinks, *, scale, window, name, tb=256, rider=None):
    nh, s, d = q.shape
    nkv = k.shape[0]
    grp = nh // nkv
    w = window
    per = tb // w
    nb = s // tb

    def body(qc_ref, qn_ref, kc_ref, kp_ref, vc_ref, vp_ref, oc_ref, on_ref, doc_ref, don_ref, lc_ref, ln_ref, ss_ref,
             dq_ref, dk_ref, dvt_ref, dsink_ref):
        kvh, i = pl.program_id(0), pl.program_id(1)

        @pl.when(i == 0)
        def _():
            dsink_ref[...] = jnp.zeros_like(dsink_ref)

        (d_c, ok_c), other = _swa_geometry(tb, w, i > 0)
        d_p, ok_p = other(tb)
        d_n, ok_n = _swa_geometry(tb, w, i < nb - 1)[1](w)
        kc, kp, vc, vp = kc_ref[...], kp_ref[...], vc_ref[...], vp_ref[...]
        k_last, v_last = kc[tb - w:, :], vc[tb - w:, :]
        dk_acc = jnp.zeros((tb, d), F32)
        dv_acc = jnp.zeros((d, tb), F32)
        dk_tail = jnp.zeros((w, d), F32)
        dv_tail = jnp.zeros((d, w), F32)
        for g in range(grp):
            h = kvh * grp + g
            slope, sink = ss_ref[0, h], ss_ref[1, h]
            qg, dog, lse_c = qc_ref[g], doc_ref[g], lc_ref[g]
            delta = jnp.sum(dog.astype(F32) * oc_ref[g].astype(F32), axis=0, keepdims=True)
            p_c = jnp.exp(_swa_scores_t(kc, qg, d_c, ok_c, scale=scale, slope=slope) - lse_c)
            p_p = jnp.exp(_swa_scores_t(kp, qg, d_p, ok_p, scale=scale, slope=slope) - lse_c)
            ds_c = ((p_c * (jnp.dot(vc, dog, preferred_element_type=F32) - delta)) * scale).astype(BF16)
            ds_p = ((p_p * (jnp.dot(vp, dog, preferred_element_type=F32) - delta)) * scale).astype(BF16)
            dq_ref[g] = (lax.dot_general(ds_c, kc, TN, preferred_element_type=F32)
                         + lax.dot_general(ds_p, kp, TN, preferred_element_type=F32))
            dk_acc += jnp.dot(ds_c, qg, preferred_element_type=F32)
            dv_acc += lax.dot_general(dog, p_c.astype(BF16), NT, preferred_element_type=F32)
            dsink_ref[g] -= jnp.broadcast_to(jnp.sum(jnp.exp(sink - lse_c) * delta, axis=1, keepdims=True), (1, LANES))
            qn, don = qn_ref[g], don_ref[g]
            delta_n = jnp.sum(don.astype(F32) * on_ref[g].astype(F32), axis=0, keepdims=True)
            p_n = jnp.exp(_swa_scores_t(k_last, qn, d_n, ok_n, scale=scale, slope=slope) - ln_ref[g])
            ds_n = ((p_n * (jnp.dot(v_last, don, preferred_element_type=F32) - delta_n)) * scale).astype(BF16)
            dk_tail += jnp.dot(ds_n, qn, preferred_element_type=F32)
            dv_tail += lax.dot_general(don, p_n.astype(BF16), NT, preferred_element_type=F32)
        dk_ref[...] = dk_acc
        dvt_ref[...] = dv_acc
        dk_ref[tb - w:, :] += dk_tail
        dvt_ref[:, tb - w:] += dv_tail

    def prev(i):
        return jnp.maximum(i * per - 1, 0)

    def nxt(i):
        return jnp.minimum((i + 1) * per, s // w - 1)

    return _call_with_rider(
        body, rider, name=name, grid=(nkv, nb), scratch_shapes=[],
        args=(q, q, k, k, v, v, ot, ot, dot_, dot_, lse, lse, slopes_sinks),
        in_specs=[pl.BlockSpec((grp, tb, d), lambda kh, i: (kh, i, 0)),
                  pl.BlockSpec((grp, w, d), lambda kh, i: (kh, nxt(i), 0)),
                  pl.BlockSpec((None, tb, d), lambda kh, i: (kh, i, 0)),
                  pl.BlockSpec((None, w, d), lambda kh, i: (kh, prev(i), 0)),
                  pl.BlockSpec((None, tb, d), lambda kh, i: (kh, i, 0)),
                  pl.BlockSpec((None, w, d), lambda kh, i: (kh, prev(i), 0)),
                  pl.BlockSpec((grp, d, tb), lambda kh, i: (kh, 0, i)),
                  pl.BlockSpec((grp, d, w), lambda kh, i: (kh, 0, nxt(i))),
                  pl.BlockSpec((grp, d, tb), lambda kh, i: (kh, 0, i)),
                  pl.BlockSpec((grp, d, w), lambda kh, i: (kh, 0, nxt(i))),
                  pl.BlockSpec((grp, 1, tb), lambda kh, i: (kh, 0, i)),
                  pl.BlockSpec((grp, 1, w), lambda kh, i: (kh, 0, nxt(i))),
                  pl.BlockSpec(memory_space=pltpu.SMEM)],
        out_specs=[pl.BlockSpec((grp, tb, d), lambda kh, i: (kh, i, 0)),
                   pl.BlockSpec((None, tb, d), lambda kh, i: (kh, i, 0)),
                   pl.BlockSpec((None, d, tb), lambda kh, i: (kh, 0, i)),
                   pl.BlockSpec((None, grp, 1, LANES), lambda kh, i: (kh, 0, 0, 0))],
        out_shape=[jax.ShapeDtypeStruct((nh, s, d), F32), jax.ShapeDtypeStruct((nkv, s, d), F32),
                   jax.ShapeDtypeStruct((nkv, d, s), F32), jax.ShapeDtypeStruct((nkv, grp, 1, LANES), F32)],
        compiler_params=_cp("arbitrary", "arbitrary"))


def _adamw(w, g, m, v, *, name):
    shape = w.shape
    cols = shape[-1]
    rows = int(np.prod(shape[:-1])) if len(shape) > 1 else 1
    tr = _row_tile(rows, cols)
    c1 = 1.0 - ADAM_B1 ** ADAM_STEP
    c2 = 1.0 - ADAM_B2 ** ADAM_STEP

    def body(w_ref, g_ref, m_ref, v_ref, d_ref, mo_ref, vo_ref):
        gv = g_ref[...]
        mn = ADAM_B1 * m_ref[...] + (1.0 - ADAM_B1) * gv
        vn = ADAM_B2 * v_ref[...] + (1.0 - ADAM_B2) * (gv * gv)
        mo_ref[...] = mn
        vo_ref[...] = vn
        d_ref[...] = -ADAM_LR * ((mn / c1) / (jnp.sqrt(vn / c2) + ADAM_EPS) + ADAM_WD * w_ref[...])

    blk = pl.BlockSpec((tr, cols), lambda i: (i, 0))
    outs = pl.pallas_call(
        body, name=name, grid=(rows // tr,), in_specs=[blk] * 4, out_specs=[blk] * 3,
        out_shape=[jax.ShapeDtypeStruct((rows, cols), F32)] * 3,
        compiler_params=_cp("arbitrary"))(*[a.reshape(rows, cols) for a in (w, g, m, v)])
    return tuple(a.reshape(shape) for a in outs)


def _hbm_spec():
    return pl.BlockSpec(memory_space=pl.ANY)


def _mesh_place():
    x, y, c = lax.axis_index("x"), lax.axis_index("y"), lax.axis_index("c")
    return x, y, c, [(1 - x, y), (x, 1 - y), (1 - x, 1 - y)]


def _half_rows(c, rows, align):
    return pl.ds(pl.multiple_of(c * (rows // 2), align), rows // 2)


def _part(ref, mode, k, n, rows=None):
    if mode == "cols":
        cols = pl.ds(pl.multiple_of(k * n, LANES), n)
        return ref.at[:, cols] if rows is None else ref.at[rows, cols]
    return ref.at[k] if rows is None else ref.at[k, rows, :]


class _Rider:
    def __init__(self, inputs, out_shape, n_sems, start, finish):
        self.inputs, self.out_shape, self.n_sems, self.start, self.finish = inputs, out_shape, n_sems, start, finish


def _call_with_rider(body, rider, *, name, grid, in_specs, out_specs, out_shape, scratch_shapes, compiler_params, args):
    if rider is None:
        outs = pl.pallas_call(body, name=name, grid=grid, in_specs=in_specs, out_specs=out_specs, out_shape=out_shape,
                              scratch_shapes=scratch_shapes, compiler_params=compiler_params)(*args)
        return outs, []
    n_in, n_out, n_sc = len(in_specs), len(out_specs), len(scratch_shapes)
    n_rin, n_rout = len(rider.inputs), len(rider.out_shape)

    def wrapped(*refs):
        pos = 0
        groups = []
        for n in (n_in, n_rin, n_out, n_rout, n_sc, 2):
            groups.append(refs[pos:pos + n])
            pos += n
        ins, rins, outs, routs, scratch, sems = groups
        ids = [pl.program_id(a) for a in range(len(grid))]
        first = functools.reduce(jnp.logical_and, [i == 0 for i in ids])
        last = functools.reduce(jnp.logical_and, [i == g - 1 for i, g in zip(ids, grid)])
        pl.when(first)(lambda: rider.start(rins, routs, *sems))
        body(*ins, *outs, *scratch)
        pl.when(last)(lambda: rider.finish(rins, routs, *sems))

    outs = pl.pallas_call(
        wrapped, name=name, grid=grid, in_specs=list(in_specs) + [_hbm_spec()] * n_rin,
        out_specs=list(out_specs) + [_hbm_spec()] * n_rout, out_shape=list(out_shape) + list(rider.out_shape),
        scratch_shapes=list(scratch_shapes) + [pltpu.SemaphoreType.DMA((rider.n_sems,))] * 2,
        compiler_params=compiler_params)(*args, *rider.inputs)
    return outs[:n_out], outs[n_out:]


def _run_rider(rider, *, name):
    n_rin = len(rider.inputs)

    def body(*refs):
        rins, routs, sems = refs[:n_rin], refs[n_rin:-2], refs[-2:]
        rider.start(rins, routs, *sems)
        rider.finish(rins, routs, *sems)

    return pl.pallas_call(
        body, name=name, in_specs=[_hbm_spec()] * n_rin, out_specs=[_hbm_spec()] * len(rider.out_shape),
        out_shape=rider.out_shape, scratch_shapes=[pltpu.SemaphoreType.DMA((rider.n_sems,))] * 2)(*rider.inputs)


def _gather_rider(shards, modes):
    n_arr = len(shards)
    out_shape = [jax.ShapeDtypeStruct((s.shape[0], N_CHIPS * s.shape[1]) if m == "cols" else (N_CHIPS,) + s.shape, s.dtype)
                 for s, m in zip(shards, modes)]
    per = 4

    def copies(srcs, dsts, send_sems, recv_sems):
        x, y, c, chips = _mesh_place()
        me = 2 * x + y
        sends, waits = [], []
        for i in range(n_arr):
            r, n = shards[i].shape
            rows = _half_rows(c, r, 16)

            def copy(slot, src, dst, to, i=i):
                return pltpu.make_async_remote_copy(src_ref=src, dst_ref=dst, send_sem=send_sems.at[i * per + slot],
                                                    recv_sem=recv_sems.at[i * per + slot], device_id=to, device_id_type=MESH)

            own = _part(dsts[i], modes[i], me, n)
            sends.append(copy(0, srcs[i], own, (x, y, 1 - c)))
            waits.append(copy(0, own, own, (x, y, 1 - c)))
            for j, (px, py) in enumerate(chips):
                sends.append(copy(1 + j, srcs[i].at[rows], _part(dsts[i], modes[i], me, n, rows), (px, py, c)))
                theirs = _part(dsts[i], modes[i], 2 * px + py, n, rows)
                waits.append(copy(1 + j, theirs, theirs, (px, py, c)))
        return sends, waits

    def start(*refs):
        for cp in copies(*refs)[0]:
            cp.start()

    def finish(*refs):
        sends, waits = copies(*refs)
        for cp in waits:
            cp.wait_recv()
        for cp in sends:
            cp.wait_send()

    return _Rider(list(shards), out_shape, per * n_arr, start, finish)


def _gather_forward(dsts, shard_shapes, modes, *, name):
    n_arr = len(dsts)

    def body(*refs):
        outs = refs[n_arr:2 * n_arr]
        send_sems, recv_sems = refs[2 * n_arr:]
        x, y, c, chips = _mesh_place()
        cps = []
        for i in range(n_arr):
            r, n = shard_shapes[i]
            for j, (px, py) in enumerate(chips):
                def view(hc, i=i, px=px, py=py, r=r, n=n):
                    return _part(outs[i], modes[i], 2 * px + py, n, _half_rows(hc, r, 16))

                def copy(ref, i=i, j=j):
                    return pltpu.make_async_remote_copy(src_ref=ref, dst_ref=ref, send_sem=send_sems.at[3 * i + j],
                                                        recv_sem=recv_sems.at[3 * i + j], device_id=(x, y, 1 - c), device_id_type=MESH)

                cps.append((copy(view(c)), copy(view(1 - c))))
        for send, _ in cps:
            send.start()
        for send, theirs in cps:
            theirs.wait_recv()
            send.wait_send()

    return pl.pallas_call(
        body, name=name, in_specs=[_hbm_spec()] * n_arr, out_specs=[_hbm_spec()] * n_arr,
        out_shape=[jax.ShapeDtypeStruct(d.shape, d.dtype) for d in dsts],
        input_output_aliases={i: i for i in range(n_arr)},
        scratch_shapes=[pltpu.SemaphoreType.DMA((3 * n_arr,)), pltpu.SemaphoreType.DMA((3 * n_arr,))])(*dsts)


def _blk_view(a, mode):
    return a[None] if mode == "cols" else a


def _swap_rider(arrs, modes):
    n_arr = len(arrs)
    out_shape = [jax.ShapeDtypeStruct((a.shape[0] // 2, a.shape[1]) if m == "cols" else (a.shape[0], a.shape[1] // 2, a.shape[2]), a.dtype)
                 for a, m in zip(arrs, modes)]

    def copies(srcs, dsts, send_sems, recv_sems):
        x, y, c, _ = _mesh_place()
        cps = []
        for i in range(n_arr):
            if modes[i] == "cols":
                src = srcs[i].at[_half_rows(1 - c, arrs[i].shape[0], 8)]
            else:
                src = srcs[i].at[:, _half_rows(1 - c, arrs[i].shape[1], 8), :]
            cps.append(pltpu.make_async_remote_copy(src_ref=src, dst_ref=dsts[i], send_sem=send_sems.at[i],
                                                    recv_sem=recv_sems.at[i], device_id=(x, y, 1 - c), device_id_type=MESH))
        return cps

    def start(*refs):
        for cp in copies(*refs):
            cp.start()

    def finish(*refs):
        for cp in copies(*refs):
            cp.wait()

    return _Rider(list(arrs), out_shape, n_arr, start, finish)


def _rs_pair_add(arr, landed, place, *, name):
    nb, r, c = arr.shape
    rh = r // 2
    tr = _row_tile(rh, c)
    nt = rh // tr

    def body(p_ref, a_ref, l_ref, o_ref):
        o_ref[...] = (a_ref[...] + l_ref[...]).astype(BF16)

    grid_spec = pltpu.PrefetchScalarGridSpec(
        num_scalar_prefetch=1, grid=(nb, nt),
        in_specs=[pl.BlockSpec((None, tr, c), lambda b, t, p_ref: (b, p_ref[1] * nt + t, 0)),
                  pl.BlockSpec((None, tr, c), lambda b, t, p_ref: (b, t, 0))],
        out_specs=pl.BlockSpec((None, tr, c), lambda b, t, p_ref: (b, t, 0)))
    return pl.pallas_call(
        body, name=name, grid_spec=grid_spec, out_shape=jax.ShapeDtypeStruct((nb, rh, c), BF16),
        compiler_params=_cp("arbitrary", "arbitrary"))(place, arr, landed)


def _exchange_rider(parts, modes):
    n_arr = len(parts)
    out_shape = []
    for a, m in zip(parts, modes):
        shp = (a.shape[0], a.shape[1] // N_CHIPS) if m == "cols" else a.shape[1:]
        out_shape.append(jax.ShapeDtypeStruct((3,) + shp, a.dtype))

    def copies(srcs, dsts, send_sems, recv_sems):
        x, y, c, chips = _mesh_place()
        cps = []
        for i in range(n_arr):
            n = out_shape[i].shape[-1]
            for j, (px, py) in enumerate(chips):
                cps.append(pltpu.make_async_remote_copy(
                    src_ref=_part(srcs[i], modes[i], 2 * px + py, n), dst_ref=dsts[i].at[j],
                    send_sem=send_sems.at[3 * i + j], recv_sem=recv_sems.at[3 * i + j],
                    device_id=(px, py, c), device_id_type=MESH))
        return cps

    def start(*refs):
        for cp in copies(*refs):
            cp.start()

    def finish(*refs):
        for cp in copies(*refs):
            cp.wait()

    return _Rider(list(parts), out_shape, 3 * n_arr, start, finish)


def _rs_chip_sum(part, landed, mode, place, *, name):
    _, rh, n = landed.shape
    tr = _row_tile(rh, n)
    nt = rh // tr

    def body(p_ref, a_ref, l_ref, o_ref):
        o_ref[...] = ((a_ref[...].astype(F32) + l_ref[0].astype(F32)) + l_ref[1].astype(F32)) + l_ref[2].astype(F32)

    if mode == "cols":
        own = pl.BlockSpec((tr, n), lambda t, p_ref: (t, p_ref[0]))
    else:
        own = pl.BlockSpec((None, tr, n), lambda t, p_ref: (p_ref[0], t, 0))
    grid_spec = pltpu.PrefetchScalarGridSpec(
        num_scalar_prefetch=1, grid=(nt,),
        in_specs=[own, pl.BlockSpec((3, tr, n), lambda t, p_ref: (0, t, 0))],
        out_specs=pl.BlockSpec((tr, n), lambda t, p_ref: (p_ref[1] * nt + t, 0)))
    return pl.pallas_call(
        body, name=name, grid_spec=grid_spec, out_shape=jax.ShapeDtypeStruct((2 * rh, n), F32),
        compiler_params=_cp("arbitrary"))(place, part, landed)


def _rs_pair_join(halves, *, name):
    n_arr = len(halves)

    def body(*refs):
        outs = refs[n_arr:2 * n_arr]
        send_sems, recv_sems = refs[2 * n_arr:]
        x, y, c, _ = _mesh_place()
        cps = []
        for i in range(n_arr):
            rows = _half_rows(c, halves[i].shape[0], 8)
            cps.append(pltpu.make_async_remote_copy(src_ref=outs[i].at[rows], dst_ref=outs[i].at[rows], send_sem=send_sems.at[i],
                                                    recv_sem=recv_sems.at[i], device_id=(x, y, 1 - c), device_id_type=MESH))
        for cp in cps:
            cp.start()
        for i, cp in enumerate(cps):
            cp.wait_send()
            theirs = outs[i].at[_half_rows(1 - c, halves[i].shape[0], 8)]
            pltpu.make_async_remote_copy(src_ref=theirs, dst_ref=theirs, send_sem=send_sems.at[i], recv_sem=recv_sems.at[i],
                                         device_id=(x, y, 1 - c), device_id_type=MESH).wait_recv()

    return pl.pallas_call(
        body, name=name, in_specs=[_hbm_spec()] * n_arr, out_specs=[_hbm_spec()] * n_arr,
        out_shape=[jax.ShapeDtypeStruct(h.shape, h.dtype) for h in halves],
        input_output_aliases={i: i for i in range(n_arr)},
        scratch_shapes=[pltpu.SemaphoreType.DMA((n_arr,)), pltpu.SemaphoreType.DMA((n_arr,))])(*halves)


def _allreduce_small(v, *, name):
    r, c = v.shape

    def body(v_ref, o_ref, gath, send_sems, recv_sems):
        x, y, cc, _ = _mesh_place()
        me = 4 * x + 2 * y + cc
        gath[me] = v_ref[...]
        cps = []
        for rel in range(1, 8):
            px = 1 - x if rel & 4 else x
            py = 1 - y if rel & 2 else y
            pc = 1 - cc if rel & 1 else cc

            def copy(slot, px=px, py=py, pc=pc, rel=rel):
                return pltpu.make_async_remote_copy(
                    src_ref=v_ref, dst_ref=gath.at[slot], send_sem=send_sems.at[rel - 1],
                    recv_sem=recv_sems.at[rel - 1], device_id=(px, py, pc), device_id_type=MESH)

            cps.append((copy(me), copy(4 * px + 2 * py + pc)))
        for send, _ in cps:
            send.start()
        for send, theirs in cps:
            theirs.wait_recv()
            send.wait_send()
        tot = gath[0]
        for d in range(1, 8):
            tot = tot + gath[d]
        o_ref[...] = tot

    vm = pl.BlockSpec(memory_space=pltpu.VMEM)
    return pl.pallas_call(
        body, name=name, in_specs=[vm], out_specs=vm, out_shape=jax.ShapeDtypeStruct((r, c), F32),
        scratch_shapes=[pltpu.VMEM((8, r, c), F32), pltpu.SemaphoreType.DMA((7,)), pltpu.SemaphoreType.DMA((7,))])(v)


def _rope_tables(s, reps):
    half = B_ROPE // 2
    inv = ROPE_THETA ** (-jnp.arange(0, B_ROPE, 2, dtype=F32) / B_ROPE)
    ang = jnp.arange(s, dtype=F32)[:, None] * inv[None, :]
    return jnp.tile(jnp.cos(ang), (1, reps)), jnp.tile(jnp.sin(ang), (1, reps))


def _alibi_slopes():
    return 2.0 ** (-8.0 * jnp.arange(1, A_HEADS + 1, dtype=F32) / A_HEADS)


def _ffn_fwd(h, norm, wts, tag, rider=None, on_rode=None):
    (dact_dgate, dact_dup, act, xn), rode = _ffn_up(h, norm, wts["wgu"], name=f"{tag}_up", rider=rider)
    if on_rode is not None:
        on_rode(rode)
    out = _mm_res_fwd(act, wts["wd"], h, scale=FFN_RES_SCALE, name=f"{tag}_down")
    return out, dict(h_in=h, dact_dgate=dact_dgate, dact_dup=dact_dup, act=act, xn=xn), rode


def _ffn_bwd(dh, norm, wts, sv, tag, rider=None, own=None):
    (dgate, dup), rode = _ffn_down_bwd(dh, wts["wd"], sv["dact_dgate"], sv["dact_dup"], scale=FFN_RES_SCALE,
                                      name=f"{tag}_down_bwd", rider=rider)
    d_wd = _mm_tn(sv["act"], dh, b_scale=FFN_RES_SCALE, name=f"{tag}_dwd")
    pairs = [(dgate, wts["wgu"], 0), (dup, wts["wgu"], 1)]
    if own is None:
        d_wgu = _mm_tn(sv["xn"], [dgate, dup], name=f"{tag}_dwgu")
        dh_in, dnorm = _mm_nt_rmsbwd(pairs, sv["h_in"], norm, dh, name=f"{tag}_dx")
    else:
        wd_ready, wgu_ready, done = own
        first = wd_ready(d_wd)
        res = _mm_tn(sv["xn"], [dgate, dup], name=f"{tag}_dwgu", rider=first)
        d_wgu, brought = (res, []) if first is None else res
        second = wgu_ready(brought, d_wgu)
        res = _mm_nt_rmsbwd(pairs, sv["h_in"], norm, dh, name=f"{tag}_dx", rider=second)
        dh_in, dnorm, brought = (*res, []) if second is None else res
        done(brought)
    return dh_in, dnorm, d_wgu, d_wd, rode


def _even_weights(w_in, w_uq, w_ukv):
    half = B_ROPE // 2
    base = w_in.shape[1]
    kr1, kr2 = w_in[:, base - B_ROPE:base - half], w_in[:, base - half:]
    w_in_cat = jnp.concatenate([w_in, -kr2, kr1, jnp.zeros((w_in.shape[0], 64), w_in.dtype)], axis=1)
    u3 = w_uq.reshape(w_uq.shape[0], B_HEADS, B_NOPE + B_ROPE)
    nope = u3[:, :, :B_NOPE].reshape(w_uq.shape[0], -1)
    rot = u3[:, :, B_NOPE:].reshape(w_uq.shape[0], -1)
    swapped = jnp.concatenate([-u3[:, :, B_NOPE + half:], u3[:, :, B_NOPE:B_NOPE + half]], axis=-1).reshape(w_uq.shape[0], -1)
    return w_in_cat, jnp.concatenate([nope, rot, swapped], axis=1), w_ukv


def _even_fwd(h, w, i, rider=None):
    s = h.shape[0]
    qa, ka, va, vat, c_q, c_kv, kr_blk, xn = _ev_in_fwd(h, w["mix_norm"][i:i + 1], w["ev_in_cat"], name="ev_in")
    cos32, sin32 = _rope_tables(s, 2)
    kro = _rope_fwd(kr_blk[:, :B_ROPE], kr_blk[:, B_ROPE:2 * B_ROPE], cos32, sin32, name="ev_k_rope")
    ss = jnp.stack([_alibi_slopes(), w["ev_sinks"].reshape(-1)])
    oa, lse_a = _swa_fwd_t(qa, ka, vat, ss, scale=A_HEAD_DIM ** -0.5, window=WINDOW, name="swa_fwd")
    cos256, sin256 = _rope_tables(s, 2 * B_HEADS)
    qb, xn_q = _ev_q_fwd(c_q, w["ev_cq_norm"], w["ev_q_cat"], cos256, sin256, name="ev_q_up")
    kb, vb, vbt, xn_kv = _ev_kv_fwd(c_kv, w["ev_ckv_norm"], w["ev_ukv"], kro, name="ev_kv_up")
    (ob, lse_b), rode = _causal_fwd_t(qb, kb, vbt, scale=(B_NOPE + B_ROPE) ** -0.5, name="mla_fwd", tq=512, hb=8, rider=rider)
    attn = jnp.concatenate([oa.reshape(-1, s), ob.reshape(-1, s)], axis=0)
    out = _mm_res_fwd(attn, w["ev_out"], h, scale=1.0, name="ev_out", a_t=True)
    sv = dict(h_in=h, xn=xn, c_q=c_q, c_kv=c_kv, xn_q=xn_q, xn_kv=xn_kv, qa=qa, ka=ka, va=va, oa=oa, lse_a=lse_a,
              ss=ss, qb=qb, kb=kb, vb=vb, ob=ob, lse_b=lse_b, attn=attn, cos32=cos32, sin32=sin32,
              cos256=cos256, sin256=sin256)
    return out, sv, rode


def _even_bwd(dh, w, sv, i, rider=None):
    s = dh.shape[0]
    half = B_ROPE // 2
    g = {}
    dattn = _mm_nt_t(dh, w["ev_out"], name="ev_out_dx")
    g["ev_w_out"] = _mm_tn(sv["attn"], dh, name="ev_out_dw", a_t=True)
    doa = dattn[:A_HEADS * A_HEAD_DIM].reshape(A_HEADS, A_HEAD_DIM, s)
    dob = dattn[A_HEADS * A_HEAD_DIM:].reshape(B_HEADS, B_V, s)
    first, then = rider if isinstance(rider, tuple) else (None, None)
    (dqa, dka, dva, dsink), brought = _swa_bwd_t(sv["qa"], sv["ka"], sv["va"], sv["oa"], doa, sv["lse_a"], sv["ss"],
                                                 scale=A_HEAD_DIM ** -0.5, window=WINDOW, name="swa_bwd", rider=first)
    if then is not None:
        rider = then(brought)
    g["ev_sinks"] = dsink[:, :, 0, 0].reshape(1, A_HEADS)
    (dqb, dkb, dvb), rode = _causal_bwd_t(sv["qb"], sv["kb"], sv["vb"], sv["ob"], dob, sv["lse_b"],
                                          scale=(B_NOPE + B_ROPE) ** -0.5, name="mla_bwd", tq=512, hb=4, rider=rider)
    dyq = _ev_q_merge(dqb, sv["cos256"], sv["sin256"], name="ev_q_merge")
    dwq = _mm_tn(sv["xn_q"], dyq, name="ev_q_up_dw")
    dcq, g["ev_cq_norm"] = _mm_nt_rmsbwd([(dyq, w["ev_q_cat"])], sv["c_q"], w["ev_cq_norm"], None, name="ev_q_up_dx")
    kq = sv["c_q"].shape[1]
    d_nope = dwq[:, :512].reshape(kq, B_HEADS, B_NOPE)
    d_rot = dwq[:, 512:768].reshape(kq, B_HEADS, B_ROPE)
    d_swp = dwq[:, 768:].reshape(kq, B_HEADS, B_ROPE)
    g["ev_w_uq"] = jnp.concatenate([d_nope, d_rot[:, :, :half] + d_swp[:, :, half:], d_rot[:, :, half:] - d_swp[:, :, :half]],
                                   axis=-1).reshape(kq, -1)
    dykv, dkr = _ev_kv_merge(dkb, dvb, sv["cos32"], sv["sin32"], name="ev_kv_merge")
    g["ev_w_ukv"] = _mm_tn(sv["xn_kv"], dykv, name="ev_kv_up_dw")
    dckv, g["ev_ckv_norm"] = _mm_nt_rmsbwd([(dykv, w["ev_ukv"])], sv["c_kv"], w["ev_ckv_norm"], None, name="ev_kv_up_dx")
    dycat = _ev_in_merge(dqa, dka, dva, dcq, dckv, dkr, name="ev_in_merge")
    dwin = _mm_tn(sv["xn"], dycat, name="ev_in_dw")
    base = 1184
    g["ev_w_in"] = jnp.concatenate([dwin[:, :base - B_ROPE],
                                    dwin[:, base - B_ROPE:base - half] + dwin[:, base + half:base + B_ROPE],
                                    dwin[:, base - half:base] - dwin[:, base:base + half]], axis=-1)
    dh_in, dnorm = _mm_nt_rmsbwd([(dycat, w["ev_in_cat"])], sv["h_in"], w["mix_norm"][i:i + 1], dh, name="ev_in_dx")
    return dh_in, dnorm, g, rode


def _odd_fwd(h, w, i, rider=None):
    s = h.shape[0]
    wd = C_HEADS * C_HEAD_DIM
    q, k, v, vt, y_f, xn = _fox_in_fwd(h, w["mix_norm"][i:i + 1], w["od_in_pad"], nheads=C_HEADS, dh=C_HEAD_DIM,
                                       q_ones=(0, 2, 3, 4), k_ones=(1,), name="od_in")
    scale = C_HEAD_DIM ** -0.5
    ft = y_f[:, :C_HEADS].T
    bf = w["od_b_f"].reshape(C_HEADS, 1)
    cb3 = _fox_gate_fwd(ft, bf, out_scale=-1.0 / scale, name="fox_gate_fwd")
    k = k + jnp.pad(cb3.transpose(1, 2, 0), ((0, 0), (0, 0), (C_HEAD_DIM + 2, LANES - C_HEAD_DIM - 5)))
    (o, lse), rode = _causal_fwd_t(q, k, vt, scale=scale, name="fox_fwd", tq=512, hb=8, rider=rider)
    attn = o.reshape(-1, s)
    out = _mm_res_fwd(attn, w["od_out"], h, scale=1.0, name="od_out", a_t=True)
    return out, dict(h_in=h, xn=xn, q=q, k=k, v=v, o=o, lse=lse, ft=ft, bf=bf, attn=attn), rode


def _odd_bwd(dh, w, sv, i, rider=None):
    s = dh.shape[0]
    g = {}
    dattn = _mm_nt_t(dh, w["od_out"], name="od_out_dx")
    g["od_w_out"] = _mm_tn(sv["attn"], dh, name="od_out_dw", a_t=True)
    do = dattn.reshape(C_HEADS, C_HEAD_DIM, s)
    scale = C_HEAD_DIM ** -0.5
    (dq, dk, dv), rode = _causal_bwd_t(sv["q"], sv["k"], sv["v"], sv["o"], do, sv["lse"], scale=scale, name="fox_bwd",
                                       tq=512, hb=4, rider=rider)
    dqkv, sums = _merge_heads(dq, dk, dv, dh=C_HEAD_DIM, q_col=C_HEAD_DIM + 1, k_col=C_HEAD_DIM, name="fox_merge")
    dft, dbf = _fox_gate_bwd(sums[:, :C_HEADS].T, sums[:, C_HEADS:2 * C_HEADS].T, sv["ft"], sv["bf"],
                             inv_scale=1.0 / scale, name="fox_gate_bwd")
    g["od_b_f"] = dbf.reshape(1, C_HEADS)
    wd = C_HEADS * C_HEAD_DIM
    df = jnp.pad(dft.T, ((0, 0), (0, LANES - C_HEADS)))
    g["od_w_in"] = jnp.concatenate([_mm_tn(sv["xn"], dqkv, name="od_in_dw"),
                                    _mm_tn(sv["xn"], df, name="od_in_dwf")[:, :C_HEADS]], axis=-1)
    dh_in, dnorm = _mm_nt_rmsbwd([(dqkv, w["od_in_pad"], 0), (df, w["od_in_pad"], 3 * wd // LANES)],
                                 sv["h_in"], w["mix_norm"][i:i + 1], dh, name="od_in_dx")
    return dh_in, dnorm, g, rode


def _kernel_weights(full, replicated):
    w = dict(replicated)
    _install_weights(w, {(n, i): a for n, per_layer in full.items() for i, a in enumerate(per_layer)})
    return w


def _install_weights(w, got):
    raw = w.setdefault("raw", {})
    raw.update(got)
    for (n, i), a in got.items():
        if n in ("ffa_w_gate_up", "ffa_w_down", "ffb_w_gate_up", "ffb_w_down"):
            w.setdefault(n[:3], {}).setdefault(i, {})["wgu" if n.endswith("gate_up") else "wd"] = a
        elif n in ("ple_w_gate", "ple_w_proj"):
            w.setdefault("ple_gate" if n.endswith("gate") else "ple_proj", {})[i] = a
    if "ev_in_cat" not in w and all((n, 0) in raw for n in ("ev_w_in", "ev_w_uq", "ev_w_ukv", "ev_w_out")):
        w["ev_in_cat"], w["ev_q_cat"], w["ev_ukv"] = _even_weights(raw["ev_w_in", 0], raw["ev_w_uq", 0], raw["ev_w_ukv", 0])
        w["ev_out"] = raw["ev_w_out", 0]
    if "od_in_pad" not in w and all((n, 0) in raw for n in ("od_w_in", "od_w_out")):
        od_in = raw["od_w_in", 0]
        w["od_in_pad"] = jnp.pad(od_in, ((0, 0), (0, (-od_in.shape[1]) % LANES)))
        w["od_out"] = raw["od_w_out", 0]


def _keys(names, layer):
    return tuple((n, layer) for n in names)


_FFA, _FFB, _PLE = ("ffa_w_gate_up", "ffa_w_down"), ("ffb_w_gate_up", "ffb_w_down"), ("ple_w_gate", "ple_w_proj")
_EV, _OD = ("ev_w_in", "ev_w_uq", "ev_w_ukv", "ev_w_out"), ("od_w_in", "od_w_out")
_GATHER_FIRST = _keys(_FFA[:1], 0)
_GATHER_RIDES = {("ffa", 0): _keys(_FFA[1:] + _EV, 0), ("mix", 0): _keys(_FFB + _PLE, 0) + _keys(_FFA, 1),
                 ("ffb", 0): _keys(_OD, 0), ("mix", 1): _keys(_FFB + _PLE, 1)}
_REDUCE_RIDES = {("mix", 1): _keys(_FFB + _PLE, 1), ("mix", 0): _keys(_FFA, 1) + _keys(_OD, 0) + _keys(_FFB + _PLE, 0),
                 ("ffa", 0): _keys(_EV, 0)}
_REDUCE_OWN = ("ffa", 0)
_SWAP_AHEAD = {("ffb", 1): ("mix", 1)}


def _local_step(x, p, tgt, w, ex=None):
    depth = p.shape[0]

    def gather_behind(host, fn, *args):
        keys = None if ex is None else _GATHER_RIDES.get(host)
        if keys is None:
            return fn(*args, None)[:-1]
        done = []

        def install(rode):
            if not done:
                _install_weights(w, ex.gather_finish(keys, rode, name=f"weight_forward_{host[0]}{host[1]}"))
                done.append(True)

        res = fn(*args, ex.gather_rider(keys), install) if fn is _ffn_fwd else fn(*args, ex.gather_rider(keys))
        install(res[-1])
        return res[:-1]

    h = x
    saved = []
    for i in range(depth):
        sv = {}
        h, sv["ffa"] = gather_behind(("ffa", i), _ffn_fwd, h, w["ffa_norm"][i:i + 1], w["ffa"][i], f"ffa{i}")
        h, sv["mix"] = gather_behind(("mix", i), _even_fwd if i % 2 == 0 else _odd_fwd, h, w, i)
        h, sv["ffb"] = gather_behind(("ffb", i), _ffn_fwd, h, w["ffb_norm"][i:i + 1], w["ffb"][i], f"ffb{i}")
        h_in = h
        h, xn, gate, pp = _ple_fwd(h, w["ple_norm"][i:i + 1], w["ple_gate"][i], p[i], w["ple_proj"][i], name=f"ple{i}")
        sv["ple"] = dict(h_in=h_in, xn=xn, gate=gate, pp=pp)
        saved.append(sv)
    loss_vec, dh, d_final = _final_loss(h, w["final_norm"].reshape(1, -1), tgt, name="final_loss")

    per_layer = [dict() for _ in range(depth)]
    mats = {}
    grads = {}

    pending = {}

    def reduce_behind(host, fn, *args):
        keys = None if ex is None else _REDUCE_RIDES.get(host)
        ahead = None if ex is None else _SWAP_AHEAD.get(host)
        if keys is None and ahead is None:
            return fn(*args, None)[:-1]
        if ahead is not None:
            got, ctxs = {}, []

            def note_wd(d_wd):
                got[f"{host[0]}_w_down", host[1]] = d_wd

            def swap_now(brought, d_wgu):
                got[f"{host[0]}_w_gate_up", host[1]] = d_wgu
                swap, ctx = ex.swap_rider(_REDUCE_RIDES[ahead], {**mats, **got})
                ctxs.append(ctx)
                return swap

            def stash(brought):
                pending[ahead] = ex.after_swap(ctxs[0], brought)

            return fn(*args, None, (note_wd, swap_now, stash))[:-1]
        states = []
        if fn is _even_bwd:
            swap, ctx = ex.swap_rider(keys, mats)

            def then(brought):
                states.append(ex.after_swap(ctx, brought))
                return states[0][0]

            res = fn(*args, (swap, then))
        else:
            states.append(pending.pop(host, None) or ex.reduce_begin(keys, mats, tag=f"{host[0]}{host[1]}"))
            if fn is _ffn_bwd and host == _REDUCE_OWN:
                own = []

                def wd_ready(d_wd):
                    own.append(ex.reduce_begin(_keys(_FFA[1:], 0), {("ffa_w_down", 0): d_wd}, tag="own_wd"))
                    return own[0][0]

                def wgu_ready(brought, d_wgu):
                    ex.reduce_finish(own[0], brought)
                    own.append(ex.reduce_begin(_keys(_FFA[:1], 0), {("ffa_w_gate_up", 0): d_wgu}, tag="own_wgu"))
                    return own[1][0]

                res = fn(*args, states[0][0], (wd_ready, wgu_ready, lambda brought: ex.reduce_finish(own[1], brought)))
            else:
                res = fn(*args, states[0][0])
        ex.reduce_finish(states[0], res[-1])
        return res[:-1]

    for i in reversed(range(depth)):
        sv, gl = saved[i], per_layer[i]
        dz, dpp = _ple_bwd_elem(dh, sv["ple"]["gate"], sv["ple"]["pp"], name=f"ple{i}_bwd")
        mats["ple_w_gate", i] = _mm_tn(sv["ple"]["xn"], dz, name=f"ple{i}_dwg")
        mats["ple_w_proj", i] = _mm_tn(p[i], dpp, name=f"ple{i}_dwp")
        dh, gl["ple_norm"] = _mm_nt_rmsbwd([(dz, w["ple_gate"][i])], sv["ple"]["h_in"], w["ple_norm"][i:i + 1], dh,
                                           name=f"ple{i}_dx")
        dh, gl["ffb_norm"], mats["ffb_w_gate_up", i], mats["ffb_w_down", i] = reduce_behind(
            ("ffb", i), _ffn_bwd, dh, w["ffb_norm"][i:i + 1], w["ffb"][i], sv["ffb"], f"ffb{i}")
        dh, gl["mix_norm"], gm = reduce_behind(("mix", i), _even_bwd if i % 2 == 0 else _odd_bwd, dh, w, sv["mix"], i)
        for n, g in gm.items():
            if n in REPLICATED:
                grads[n] = g
            else:
                mats[n, 0] = g
        dh, gl["ffa_norm"], mats["ffa_w_gate_up", i], mats["ffa_w_down", i] = reduce_behind(
            ("ffa", i), _ffn_bwd, dh, w["ffa_norm"][i:i + 1], w["ffa"][i], sv["ffa"], f"ffa{i}")
    grads["final_norm"] = d_final.reshape(-1)
    for n in ("ffa_norm", "mix_norm", "ffb_norm", "ple_norm"):
        grads[n] = jnp.concatenate([per_layer[i][n] for i in range(depth)], axis=0)
    if ex is None:
        for n, _ in SHARDED:
            grads[n] = [mats[n, i] for i in range(depth) if (n, i) in mats]
    return loss_vec[0, 0], dh, grads


def _cut_mode(local_shape, axis, ncols):
    return "cols" if axis == 2 and ncols % LANES == 0 else "blk"


class _Exchange:
    def __init__(self, wts):
        self.place = jnp.stack([2 * lax.axis_index("x") + lax.axis_index("y"), lax.axis_index("c")]).astype(jnp.int32)
        self.info = {}
        for n, axis in SHARDED:
            wb = wts[n].astype(BF16)
            mode = _cut_mode(wb.shape, axis, wb.shape[2])
            for i in range(wb.shape[0]):
                self.info[n, i] = dict(shard=wb[i], mode=mode, axis=axis)
        self.halves = {}

    def _modes(self, keys):
        return [self.info[k]["mode"] for k in keys]

    def gather_rider(self, keys):
        return _gather_rider([self.info[k]["shard"] for k in keys], self._modes(keys))

    def gather_finish(self, keys, landed, *, name):
        outs = _gather_forward(landed, [self.info[k]["shard"].shape for k in keys], self._modes(keys), name=name)
        got = {}
        for k, dst in zip(keys, outs):
            if self.info[k]["mode"] == "blk":
                dst = dst.reshape(-1, dst.shape[2]) if self.info[k]["axis"] == 1 else jnp.moveaxis(dst, 0, 1).reshape(dst.shape[1], -1)
            got[k] = dst
        return got

    def gather(self, keys, *, name):
        return self.gather_finish(keys, _run_rider(self.gather_rider(keys), name=name), name=name + "_forward")

    def swap_rider(self, keys, mats):
        modes = self._modes(keys)
        arrs = []
        for k in keys:
            g2, (rr, cc) = mats[k], self.info[k]["shard"].shape
            if self.info[k]["mode"] == "blk":
                g2 = g2.reshape(N_CHIPS, rr, cc) if self.info[k]["axis"] == 1 else g2.reshape(rr, N_CHIPS, cc).transpose(1, 0, 2)
            arrs.append(g2)
        return _swap_rider(arrs, modes), (keys, modes, arrs)

    def after_swap(self, ctx, landed):
        keys, modes, arrs = ctx
        parts = []
        for (n, i), m, a, l in zip(keys, modes, arrs, landed):
            pt = _rs_pair_add(_blk_view(a, m), _blk_view(l, m), self.place, name=f"rs_pair_add_{n}{i}")
            parts.append(pt[0] if m == "cols" else pt)
        return _exchange_rider(parts, modes), keys, parts

    def reduce_begin(self, keys, mats, *, tag):
        rider, ctx = self.swap_rider(keys, mats)
        return self.after_swap(ctx, _run_rider(rider, name=f"rs_pair_swap_{tag}"))

    def reduce_finish(self, state, landed):
        _, keys, parts = state
        for (n, i), m, pt, l in zip(keys, self._modes(keys), parts, landed):
            self.halves[n, i] = _rs_chip_sum(pt, l, m, self.place, name=f"rs_chip_sum_{n}{i}")

    def reduce(self, keys, mats, *, tag):
        state = self.reduce_begin(keys, mats, tag=tag)
        self.reduce_finish(state, _run_rider(state[0], name=f"rs_chip_exchange_{tag}"))

    def join(self, wts):
        keys = list(self.info)
        joined = dict(zip(keys, _rs_pair_join([self.halves[k] for k in keys], name="rs_pair_join")))
        return {n: jnp.stack([joined[n, i] for i in range(wts[n].shape[0])]).reshape(wts[n].shape) for n, _ in SHARDED}


def _small_rows(vals):
    rows = []
    for n in REPLICATED:
        v = vals[n].reshape(-1)
        rows.append(jnp.pad(v, (0, (-v.shape[0]) % FLAT_COLS)).reshape(-1, FLAT_COLS))
    out = jnp.concatenate(rows, axis=0)
    return jnp.pad(out, ((0, (-out.shape[0]) % 8), (0, 0)))


def kernel(x, p, ffa_norm, ffa_w_gate_up, ffa_w_down, mix_norm, ffb_norm, ffb_w_gate_up, ffb_w_down, ple_norm, ple_w_gate, ple_w_proj, ev_w_in, ev_sinks, ev_cq_norm, ev_w_uq, ev_ckv_norm, ev_w_ukv, ev_w_out, od_w_in, od_b_f, od_w_out, final_norm, loss_target, m_ffa_norm, m_ffa_w_gate_up, m_ffa_w_down, m_mix_norm, m_ffb_norm, m_ffb_w_gate_up, m_ffb_w_down, m_ple_norm, m_ple_w_gate, m_ple_w_proj, m_ev_w_in, m_ev_sinks, m_ev_cq_norm, m_ev_w_uq, m_ev_ckv_norm, m_ev_w_ukv, m_ev_w_out, m_od_w_in, m_od_b_f, m_od_w_out, m_final_norm, v_ffa_norm, v_ffa_w_gate_up, v_ffa_w_down, v_mix_norm, v_ffb_norm, v_ffb_w_gate_up, v_ffb_w_down, v_ple_norm, v_ple_w_gate, v_ple_w_proj, v_ev_w_in, v_ev_sinks, v_ev_cq_norm, v_ev_w_uq, v_ev_ckv_norm, v_ev_w_ukv, v_ev_w_out, v_od_w_in, v_od_b_f, v_od_w_out, v_final_norm):
    env = dict(locals())
    wts = {n: env[n] for n in WEIGHT_ORDER}
    mom1 = {n: env["m_" + n] for n in WEIGHT_ORDER}
    mom2 = {n: env["v_" + n] for n in WEIGHT_ORDER}
    ex = _Exchange(wts)

    w = {n: wts[n] for n in REPLICATED}
    _install_weights(w, ex.gather(_GATHER_FIRST, name="weight_gather_first"))

    loss_part, grad_x, grads = _local_step(x[0], p[:, 0], loss_target[0], w, ex)
    loss = lax.psum(loss_part, ("x", "y", "c"))
    gout = ex.join(wts)
    small = _allreduce_small(_small_rows(grads), name="small_allreduce")
    r0 = 0
    for n in REPLICATED:
        size = int(np.prod(wts[n].shape))
        nr = -(-size // FLAT_COLS)
        gout[n] = small[r0:r0 + nr].reshape(-1)[:size].reshape(wts[n].shape)
        r0 += nr

    delta, new_m, new_v = {}, {}, {}
    for n in WEIGHT_ORDER:
        delta[n], new_m[n], new_v[n] = _adamw(wts[n], gout[n], mom1[n], mom2[n], name="adamw_" + n)
    return (loss, grad_x[None], *[gout[n] for n in WEIGHT_ORDER], *[delta[n] for n in WEIGHT_ORDER],
            *[new_m[n] for n in WEIGHT_ORDER], *[new_v[n] for n in WEIGHT_ORDER])
```

```python
import functools
import math

import numpy as np
import jax
import jax.numpy as jnp
from jax import lax
from jax.experimental import pallas as pl
from jax.experimental.pallas import tpu as pltpu

F32 = jnp.float32
BF16 = jnp.bfloat16
NT = (((1,), (1,)), ((), ()))
TN = (((0,), (0,)), ((), ()))
MESH = pl.DeviceIdType.MESH

RMS_EPS = 1e-6
FFN_RES_SCALE = 0.5
A_HEADS, A_KV_HEADS, A_HEAD_DIM, WINDOW = 8, 2, 64, 128
B_HEADS, B_Q_LORA, B_KV_LORA, B_NOPE, B_ROPE, B_V = 8, 256, 128, 64, 32, 64
ROPE_THETA = 10000.0
C_HEADS, C_HEAD_DIM = 16, 64
ADAM_LR, ADAM_B1, ADAM_B2, ADAM_EPS, ADAM_WD, ADAM_STEP = 0.001, 0.9, 0.999, 1e-08, 0.01, 10

N_CHIPS = 4
LANES = 128
FLAT_COLS = 1024
MASK_VALUE = -1e30
VMEM_LIMIT = 48 * 2**20

SHARDED = (
    ("ffa_w_gate_up", 2), ("ffa_w_down", 1), ("ffb_w_gate_up", 2), ("ffb_w_down", 1),
    ("ple_w_gate", 1), ("ple_w_proj", 2), ("ev_w_in", 2), ("ev_w_uq", 2), ("ev_w_ukv", 2),
    ("ev_w_out", 1), ("od_w_in", 2), ("od_w_out", 1))
REPLICATED = ("ffa_norm", "mix_norm", "ffb_norm", "ple_norm", "final_norm",
              "ev_sinks", "ev_cq_norm", "ev_ckv_norm", "od_b_f")
WEIGHT_ORDER = ("ffa_norm", "ffa_w_gate_up", "ffa_w_down", "mix_norm", "ffb_norm", "ffb_w_gate_up",
                "ffb_w_down", "ple_norm", "ple_w_gate", "ple_w_proj", "ev_w_in", "ev_sinks",
                "ev_cq_norm", "ev_w_uq", "ev_ckv_norm", "ev_w_ukv", "ev_w_out", "od_w_in", "od_b_f",
                "od_w_out", "final_norm")


def _cp(*sem):
    return pltpu.CompilerParams(dimension_semantics=sem, vmem_limit_bytes=VMEM_LIMIT)


def _sigmoid(z):
    return 1.0 / (1.0 + jnp.exp(-z))


def _rms_stats(xv):
    r = lax.rsqrt(jnp.mean(xv * xv, axis=-1, keepdims=True) + RMS_EPS)
    return r, xv * r


def _rms_bwd(dxn, xv, g):
    r, xhat = _rms_stats(xv)
    u = dxn * g
    dx = r * (u - xhat * jnp.mean(u * xhat, axis=-1, keepdims=True))
    return dx, dxn * xhat


def _col_tile(k_rows, n, budget_bytes=6 * 2**20):
    if k_rows * n * 4 <= budget_bytes or n % LANES:
        return n
    units = n // LANES
    best = LANES
    for d in range(1, units + 1):
        if units % d == 0 and k_rows * d * LANES * 4 <= budget_bytes:
            best = d * LANES
    return best


def _row_tile(rows, cols, target_elems=2**18):
    if rows * cols <= target_elems or rows % 8:
        return rows
    best = 8
    for d in range(8, rows + 1, 8):
        if rows % d == 0 and d * cols <= target_elems:
            best = d
    return best


def _fox_in_fwd(x, g, w, *, nheads, dh, q_ones, k_ones, name, tm=512):
    s, k = x.shape
    n = w.shape[1]
    wd = nheads * dh
    spare = LANES - dh

    def body(x_ref, g_ref, w_ref, q_ref, k_ref, v_ref, vt_ref, f_ref, xn_ref):
        _, xhat = _rms_stats(x_ref[...])
        xn = (xhat * g_ref[...]).astype(BF16)
        xn_ref[...] = xn
        y = jnp.dot(xn, w_ref[...], preferred_element_type=F32)
        f_ref[...] = y[:, 3 * wd:]
        lane = lax.broadcasted_iota(jnp.int32, (tm, spare), 1)

        def fill(cols):
            return functools.reduce(jnp.logical_or, [lane == c for c in cols]).astype(F32)

        q_fill, k_fill = fill(q_ones), fill(k_ones)
        for h in range(nheads):
            q_ref[h] = jnp.concatenate([y[:, h * dh:(h + 1) * dh], q_fill], axis=-1).astype(BF16)
            k_ref[h] = jnp.concatenate([y[:, wd + h * dh:wd + (h + 1) * dh], k_fill], axis=-1).astype(BF16)
            vh = y[:, 2 * wd + h * dh:2 * wd + (h + 1) * dh]
            v_ref[h] = vh.astype(BF16)
            vt_ref[h] = vh.T.astype(BF16)

    wide = pl.BlockSpec((nheads, tm, LANES), lambda i: (0, i, 0))
    return pl.pallas_call(
        body, name=name, grid=(s // tm,),
        in_specs=[pl.BlockSpec((tm, k), lambda i: (i, 0)), pl.BlockSpec((1, k), lambda i: (0, 0)),
                  pl.BlockSpec((k, n), lambda i: (0, 0))],
        out_specs=[wide, wide, pl.BlockSpec((nheads, tm, dh), lambda i: (0, i, 0)),
                   pl.BlockSpec((nheads, dh, tm), lambda i: (0, 0, i)), pl.BlockSpec((tm, LANES), lambda i: (i, 0)),
                   pl.BlockSpec((tm, k), lambda i: (i, 0))],
        out_shape=[jax.ShapeDtypeStruct((nheads, s, LANES), BF16)] * 2
        + [jax.ShapeDtypeStruct((nheads, s, dh), BF16), jax.ShapeDtypeStruct((nheads, dh, s), BF16),
           jax.ShapeDtypeStruct((s, LANES), F32), jax.ShapeDtypeStruct((s, k), BF16)],
        compiler_params=_cp("arbitrary"))(x, g, w)


def _merge_heads(dq, dk, dvt, *, dh, q_col, k_col, name, tm=512):
    nheads, s, _ = dq.shape

    def body(dq_ref, dk_ref, dvt_ref, o_ref, cols_ref):
        pieces = [dq_ref[h][:, :dh] for h in range(nheads)] + [dk_ref[h][:, :dh] for h in range(nheads)]
        pieces += [dvt_ref[h].T for h in range(nheads)]
        o_ref[...] = jnp.concatenate(pieces, axis=-1)
        lane = lax.broadcasted_iota(jnp.int32, (tm, LANES), 1)
        cols = jnp.zeros((tm, LANES), F32)
        for h in range(nheads):
            cols = jnp.where(lane == h, jnp.broadcast_to(dq_ref[h][:, q_col:q_col + 1], (tm, LANES)), cols)
            cols = jnp.where(lane == nheads + h, jnp.broadcast_to(dk_ref[h][:, k_col:k_col + 1], (tm, LANES)), cols)
        cols_ref[...] = cols

    wide = pl.BlockSpec((nheads, tm, LANES), lambda i: (0, i, 0))
    return pl.pallas_call(
        body, name=name, grid=(s // tm,),
        in_specs=[wide, wide, pl.BlockSpec((nheads, dh, tm), lambda i: (0, 0, i))],
        out_specs=[pl.BlockSpec((tm, 3 * nheads * dh), lambda i: (i, 0)), pl.BlockSpec((tm, LANES), lambda i: (i, 0))],
        out_shape=[jax.ShapeDtypeStruct((s, 3 * nheads * dh), F32), jax.ShapeDtypeStruct((s, LANES), F32)],
        compiler_params=_cp("arbitrary"))(dq, dk, dvt)


def _row_call(body, n_rows, ins, outs, *, name, tm=512):
    def spec(a, axis):
        shape = a.shape
        if axis is None:
            return pl.BlockSpec(shape, lambda i: (0,) * len(shape))
        blk = tuple(tm if d == axis else n for d, n in enumerate(shape))
        return pl.BlockSpec(blk, lambda i: tuple(i if d == axis else 0 for d in range(len(shape))))

    return pl.pallas_call(
        body, name=name, grid=(n_rows // tm,), in_specs=[spec(a, ax) for a, ax in ins],
        out_specs=[spec(a, ax) for a, ax in outs], out_shape=[a for a, _ in outs],
        compiler_params=_cp("arbitrary"))(*[a for a, _ in ins])


def _sds(shape, dtype):
    return jax.ShapeDtypeStruct(shape, dtype)


def _ev_in_fwd(x, g, w, *, name):
    s, k = x.shape
    d = A_HEAD_DIM

    def body(x_ref, g_ref, w_ref, q_ref, k_ref, v_ref, vt_ref, cq_ref, ckv_ref, kr_ref, xn_ref):
        _, xhat = _rms_stats(x_ref[...])
        xn = (xhat * g_ref[...]).astype(BF16)
        xn_ref[...] = xn
        y = jnp.dot(xn, w_ref[...], preferred_element_type=F32)
        for h in range(A_HEADS):
            q_ref[h] = y[:, h * d:(h + 1) * d].astype(BF16)
        for h in range(A_KV_HEADS):
            k_ref[h] = y[:, 512 + h * d:512 + (h + 1) * d].astype(BF16)
            vh = y[:, 640 + h * d:640 + (h + 1) * d]
            v_ref[h] = vh.astype(BF16)
            vt_ref[h] = vh.T.astype(BF16)
        cq_ref[...] = y[:, 768:1024]
        ckv_ref[...] = y[:, 1024:1152]
        kr_ref[...] = y[:, 1152:1280]

    return _row_call(
        body, s, [(x, 0), (g, None), (w, None)],
        [(_sds((A_HEADS, s, d), BF16), 1), (_sds((A_KV_HEADS, s, d), BF16), 1), (_sds((A_KV_HEADS, s, d), BF16), 1),
         (_sds((A_KV_HEADS, d, s), BF16), 2), (_sds((s, B_Q_LORA), F32), 0), (_sds((s, B_KV_LORA), F32), 0),
         (_sds((s, LANES), F32), 0), (_sds((s, k), BF16), 0)], name=name)


def _ev_q_fwd(x, g, w, cos, sin, *, name):
    s, k = x.shape
    rot = B_HEADS * B_ROPE

    def body(x_ref, g_ref, w_ref, c_ref, s_ref, q_ref, xn_ref):
        _, xhat = _rms_stats(x_ref[...])
        xn = (xhat * g_ref[...]).astype(BF16)
        xn_ref[...] = xn
        y = jnp.dot(xn, w_ref[...], preferred_element_type=F32)
        ro = y[:, 512:512 + rot] * c_ref[...] + y[:, 512 + rot:] * s_ref[...]
        zero = jnp.zeros((y.shape[0], LANES - B_NOPE - B_ROPE), F32)
        for h in range(B_HEADS):
            q_ref[h] = jnp.concatenate([y[:, h * B_NOPE:(h + 1) * B_NOPE], ro[:, h * B_ROPE:(h + 1) * B_ROPE], zero],
                                       axis=-1).astype(BF16)

    return _row_call(body, s, [(x, 0), (g, None), (w, None), (cos, 0), (sin, 0)],
                     [(_sds((B_HEADS, s, LANES), BF16), 1), (_sds((s, k), BF16), 0)], name=name)


def _ev_kv_fwd(x, g, w, kro, *, name):
    s, k = x.shape
    per = B_NOPE + B_V

    def body(x_ref, g_ref, w_ref, kr_ref, k_ref, v_ref, vt_ref, xn_ref):
        _, xhat = _rms_stats(x_ref[...])
        xn = (xhat * g_ref[...]).astype(BF16)
        xn_ref[...] = xn
        y = jnp.dot(xn, w_ref[...], preferred_element_type=F32)
        kr = kr_ref[...]
        zero = jnp.zeros((y.shape[0], LANES - B_NOPE - B_ROPE), F32)
        for h in range(B_HEADS):
            k_ref[h] = jnp.concatenate([y[:, h * per:h * per + B_NOPE], kr, zero], axis=-1).astype(BF16)
            vh = y[:, h * per + B_NOPE:(h + 1) * per]
            v_ref[h] = vh.astype(BF16)
            vt_ref[h] = vh.T.astype(BF16)

    return _row_call(body, s, [(x, 0), (g, None), (w, None), (kro, 0)],
                     [(_sds((B_HEADS, s, LANES), BF16), 1), (_sds((B_HEADS, s, B_V), BF16), 1),
                      (_sds((B_HEADS, B_V, s), BF16), 2), (_sds((s, k), BF16), 0)], name=name)


def _ev_q_merge(dq, cos, sin, *, name):
    nh, s, _ = dq.shape

    def body(dq_ref, c_ref, s_ref, o_ref):
        dro = jnp.concatenate([dq_ref[h][:, B_NOPE:B_NOPE + B_ROPE] for h in range(nh)], axis=-1)
        o_ref[...] = jnp.concatenate([dq_ref[h][:, :B_NOPE] for h in range(nh)] + [dro * c_ref[...], dro * s_ref[...]], axis=-1)

    return _row_call(body, s, [(dq, 1), (cos, 0), (sin, 0)], [(_sds((s, 2 * nh * B_NOPE), F32), 0)], name=name)[0]


def _ev_kv_merge(dk, dvt, cos, sin, *, name):
    nh, s, _ = dk.shape

    def body(dk_ref, dvt_ref, c_ref, s_ref, o_ref, kr_ref):
        pieces = []
        tot = None
        for h in range(nh):
            pieces += [dk_ref[h][:, :B_NOPE], dvt_ref[h].T]
            rot = dk_ref[h][:, B_NOPE:B_NOPE + B_ROPE]
            tot = rot if tot is None else tot + rot
        o_ref[...] = jnp.concatenate(pieces, axis=-1)
        kr_ref[...] = jnp.concatenate([tot * c_ref[...], tot * s_ref[...], jnp.zeros((tot.shape[0], LANES - 2 * B_ROPE), F32)],
                                      axis=-1)

    return _row_call(body, s, [(dk, 1), (dvt, 2), (cos, 0), (sin, 0)],
                     [(_sds((s, nh * (B_NOPE + B_V)), F32), 0), (_sds((s, LANES), F32), 0)], name=name)


def _ev_in_merge(dq, dk, dvt, dcq, dckv, dkr, *, name):
    s = dcq.shape[0]

    def body(dq_ref, dk_ref, dvt_ref, cq_ref, ckv_ref, kr_ref, o_ref):
        pieces = [dq_ref[h] for h in range(A_HEADS)] + [dk_ref[h] for h in range(A_KV_HEADS)]
        pieces += [dvt_ref[h].T for h in range(A_KV_HEADS)] + [cq_ref[...], ckv_ref[...], kr_ref[...]]
        o_ref[...] = jnp.concatenate(pieces, axis=-1)

    return _row_call(body, s, [(dq, 1), (dk, 1), (dvt, 2), (dcq, 0), (dckv, 0), (dkr, 0)],
                     [(_sds((s, 1280), F32), 0)], name=name)[0]


def _ffn_up(x, g, wgu, *, name, tm=512, rider=None):
    s, k = x.shape
    f = wgu.shape[1] // 2
    tn = _col_tile(k, f)
    nj = f // tn

    def body(x_ref, g_ref, wg_ref, wu_ref, dgate_ref, dup_ref, act_ref, xn_ref, xn_sc):
        @pl.when(pl.program_id(1) == 0)
        def _():
            _, xhat = _rms_stats(x_ref[...])
            xn = (xhat * g_ref[...]).astype(BF16)
            xn_sc[...] = xn
            xn_ref[...] = xn

        xn = xn_sc[...]
        gg = jnp.dot(xn, wg_ref[...], preferred_element_type=F32)
        uu = jnp.dot(xn, wu_ref[...], preferred_element_type=F32)
        sg = _sigmoid(gg)
        silu = gg * sg
        dgate_ref[...] = (uu * (sg * (1.0 + gg * (1.0 - sg)))).astype(BF16)
        dup_ref[...] = silu.astype(BF16)
        act_ref[...] = (silu * uu).astype(BF16)

    tile = pl.BlockSpec((tm, tn), lambda i, j: (i, j))
    return _call_with_rider(
        body, rider, name=name, grid=(s // tm, nj),
        in_specs=[pl.BlockSpec((tm, k), lambda i, j: (i, 0)), pl.BlockSpec((1, k), lambda i, j: (0, 0)),
                  pl.BlockSpec((k, tn), lambda i, j: (0, j)), pl.BlockSpec((k, tn), lambda i, j: (0, j + nj))],
        out_specs=[tile, tile, tile, pl.BlockSpec((tm, k), lambda i, j: (i, 0))],
        out_shape=[jax.ShapeDtypeStruct((s, f), BF16)] * 3 + [jax.ShapeDtypeStruct((s, k), BF16)],
        scratch_shapes=[pltpu.VMEM((tm, k), BF16)],
        compiler_params=_cp("arbitrary", "arbitrary"), args=(x, g, wgu, wgu))


def _mm_res_fwd(a, w, res, *, scale, name, tm=512, a_t=False):
    k, n = w.shape
    s = res.shape[0]

    def body(a_ref, w_ref, r_ref, o_ref):
        prod = (lax.dot_general(a_ref[...], w_ref[...], TN, preferred_element_type=F32) if a_t
                else jnp.dot(a_ref[...], w_ref[...], preferred_element_type=F32))
        o_ref[...] = r_ref[...] + scale * prod

    a_spec = pl.BlockSpec((k, tm), lambda i: (0, i)) if a_t else pl.BlockSpec((tm, k), lambda i: (i, 0))
    return pl.pallas_call(
        body, name=name, grid=(s // tm,),
        in_specs=[a_spec, pl.BlockSpec((k, n), lambda i: (0, 0)),
                  pl.BlockSpec((tm, n), lambda i: (i, 0))],
        out_specs=pl.BlockSpec((tm, n), lambda i: (i, 0)),
        out_shape=jax.ShapeDtypeStruct((s, n), F32),
        compiler_params=_cp("arbitrary"))(a, w, res)


def _ffn_down_bwd(dh, wd, dact_dgate, dact_dup, *, scale, name, tm=512, rider=None):
    s, d = dh.shape
    f = wd.shape[0]
    tn = _col_tile(d, f)

    def body(dh_ref, wd_ref, fg_ref, fu_ref, dg_ref, du_ref):
        dhb = (dh_ref[...] * scale).astype(BF16)
        da = lax.dot_general(dhb, wd_ref[...], NT, preferred_element_type=F32)
        dg_ref[...] = (da * fg_ref[...].astype(F32)).astype(BF16)
        du_ref[...] = (da * fu_ref[...].astype(F32)).astype(BF16)

    tile = pl.BlockSpec((tm, tn), lambda i, j: (i, j))
    return _call_with_rider(
        body, rider, name=name, grid=(s // tm, f // tn),
        in_specs=[pl.BlockSpec((tm, d), lambda i, j: (i, 0)), pl.BlockSpec((tn, d), lambda i, j: (j, 0)), tile, tile],
        out_specs=[tile, tile],
        out_shape=[jax.ShapeDtypeStruct((s, f), BF16)] * 2, scratch_shapes=[],
        compiler_params=_cp("arbitrary", "arbitrary"), args=(dh, wd, dact_dgate, dact_dup))


def _mm_tn(a, bs, *, name, b_scale=1.0, ts=512, rider=None, a_t=False):
    bs = list(bs) if isinstance(bs, (list, tuple)) else [bs]
    k, s = a.shape if a_t else a.shape[::-1]
    n = bs[0].shape[1]
    tn = _col_tile(k, n, 12 * 2**20)
    per = n // tn

    def body(a_ref, *refs):
        b_refs, o_ref = refs[:-1], refs[-1]
        j = pl.program_id(0)

        @pl.when(pl.program_id(1) == 0)
        def _():
            o_ref[...] = jnp.zeros_like(o_ref)

        for m, b_ref in enumerate(b_refs):
            def acc(b_ref=b_ref):
                bv = b_ref[...]
                if b_scale != 1.0:
                    bv = bv * b_scale
                av = a_ref[...].astype(BF16)
                o_ref[...] += (jnp.dot(av, bv.astype(BF16), preferred_element_type=F32) if a_t
                               else lax.dot_general(av, bv.astype(BF16), TN, preferred_element_type=F32))

            if len(b_refs) == 1:
                acc()
            else:
                pl.when(jnp.logical_and(j >= m * per, j < (m + 1) * per))(acc)

    def b_spec(m):
        def idx(j, t):
            mine = jnp.logical_and(j >= m * per, j < (m + 1) * per)
            return (jnp.where(mine, t, 0), jnp.clip(j - m * per, 0, per - 1))
        return pl.BlockSpec((ts, tn), idx)

    (out,), rode = _call_with_rider(
        body, rider, name=name, grid=(per * len(bs), s // ts),
        in_specs=[pl.BlockSpec((k, ts), lambda j, t: (0, t)) if a_t else pl.BlockSpec((ts, k), lambda j, t: (t, 0))]
        + [b_spec(m) for m in range(len(bs))],
        out_specs=[pl.BlockSpec((k, tn), lambda j, t: (0, j))],
        out_shape=[jax.ShapeDtypeStruct((k, n * len(bs)), F32)], scratch_shapes=[],
        compiler_params=_cp("arbitrary", "arbitrary"), args=(a, *bs))
    return out if rider is None else (out, rode)


def _mm_nt_t(dy, w, *, name, tm=512):
    s, n = dy.shape
    k = w.shape[0]

    def body(dy_ref, w_ref, o_ref):
        o_ref[...] = lax.dot_general(w_ref[...], dy_ref[...].astype(BF16), NT, preferred_element_type=F32).astype(BF16)

    return pl.pallas_call(
        body, name=name, grid=(s // tm,),
        in_specs=[pl.BlockSpec((tm, n), lambda i: (i, 0)), pl.BlockSpec((k, n), lambda i: (0, 0))],
        out_specs=pl.BlockSpec((k, tm), lambda i: (0, i)),
        out_shape=jax.ShapeDtypeStruct((k, s), BF16),
        compiler_params=_cp("arbitrary"))(dy, w)


def _mm_nt_rmsbwd(pairs, x, g, dres, *, name, tm=512, rider=None):
    s, k = x.shape
    npairs = len(pairs)
    pairs = [pr if len(pr) == 3 else (pr[0], pr[1], 0) for pr in pairs]

    def body(*refs):
        dy_refs = refs[0:2 * npairs:2]
        w_refs = refs[1:2 * npairs:2]
        rest = refs[2 * npairs:]
        x_ref, g_ref = rest[0], rest[1]
        if dres is None:
            dx_ref, dg_ref = rest[2], rest[3]
        else:
            dres_ref, dx_ref, dg_ref = rest[2], rest[3], rest[4]
        dxn = None
        for dy_ref, w_ref in zip(dy_refs, w_refs):
            t = lax.dot_general(dy_ref[...].astype(BF16), w_ref[...], NT, preferred_element_type=F32)
            dxn = t if dxn is None else dxn + t
        dx, dgrow = _rms_bwd(dxn, x_ref[...], g_ref[...])
        if dres is not None:
            dx = dx + dres_ref[...]
        dx_ref[...] = dx

        @pl.when(pl.program_id(0) == 0)
        def _():
            dg_ref[...] = jnp.zeros_like(dg_ref)

        dg_ref[...] += jnp.sum(dgrow, axis=0, keepdims=True)

    in_specs, args = [], []
    for dy, w, cb in pairs:
        n = dy.shape[1]
        in_specs += [pl.BlockSpec((tm, n), lambda i: (i, 0)),
                     pl.BlockSpec((k, n), lambda i, cb=cb: (0, cb), pipeline_mode=pl.Buffered(1))]
        args += [dy, w]
    row = pl.BlockSpec((tm, k), lambda i: (i, 0))
    vec = pl.BlockSpec((1, k), lambda i: (0, 0))
    in_specs += [row, vec]
    args += [x, g]
    if dres is not None:
        in_specs.append(row)
        args.append(dres)
    (dx, dgain), rode = _call_with_rider(
        body, rider, name=name, grid=(s // tm,), in_specs=in_specs, out_specs=[row, vec],
        out_shape=[jax.ShapeDtypeStruct((s, k), F32), jax.ShapeDtypeStruct((1, k), F32)], scratch_shapes=[],
        compiler_params=_cp("arbitrary"), args=args)
    return (dx, dgain) if rider is None else (dx, dgain, rode)


def _ple_fwd(h, g, wg, p, wp, *, name, tm=512):
    s, d = h.shape
    pd = p.shape[1]

    def body(h_ref, g_ref, wg_ref, p_ref, wp_ref, o_ref, xn_ref, gate_ref, pp_ref):
        hv = h_ref[...]
        _, xhat = _rms_stats(hv)
        xn = (xhat * g_ref[...]).astype(BF16)
        xn_ref[...] = xn
        gate = _sigmoid(jnp.dot(xn, wg_ref[...], preferred_element_type=F32))
        pp = jnp.dot(p_ref[...].astype(BF16), wp_ref[...], preferred_element_type=F32)
        gate_ref[...] = gate.astype(BF16)
        pp_ref[...] = pp.astype(BF16)
        o_ref[...] = hv + gate * pp

    row = pl.BlockSpec((tm, d), lambda i: (i, 0))
    return pl.pallas_call(
        body, name=name, grid=(s // tm,),
        in_specs=[row, pl.BlockSpec((1, d), lambda i: (0, 0)), pl.BlockSpec((d, d), lambda i: (0, 0)),
                  pl.BlockSpec((tm, pd), lambda i: (i, 0)), pl.BlockSpec((pd, d), lambda i: (0, 0))],
        out_specs=[row, row, row, row],
        out_shape=[jax.ShapeDtypeStruct((s, d), F32)] + [jax.ShapeDtypeStruct((s, d), BF16)] * 3,
        compiler_params=_cp("arbitrary"))(h, g, wg, p, wp)


def _ple_bwd_elem(dh, gate, pp, *, name, tm=512):
    s, d = dh.shape

    def body(dh_ref, gate_ref, pp_ref, dz_ref, dpp_ref):
        dhv = dh_ref[...]
        gt = gate_ref[...].astype(F32)
        dz_ref[...] = (dhv * pp_ref[...].astype(F32) * (gt * (1.0 - gt))).astype(BF16)
        dpp_ref[...] = (dhv * gt).astype(BF16)

    row = pl.BlockSpec((tm, d), lambda i: (i, 0))
    return pl.pallas_call(
        body, name=name, grid=(s // tm,), in_specs=[row, row, row], out_specs=[row, row],
        out_shape=[jax.ShapeDtypeStruct((s, d), BF16)] * 2,
        compiler_params=_cp("arbitrary"))(dh, gate, pp)


def _final_loss(h, g, tgt, *, name, tm=512):
    s, d = h.shape

    def body(h_ref, g_ref, t_ref, loss_ref, dh_ref, dg_ref):
        @pl.when(pl.program_id(0) == 0)
        def _():
            loss_ref[...] = jnp.zeros_like(loss_ref)
            dg_ref[...] = jnp.zeros_like(dg_ref)

        hv = h_ref[...]
        gv = g_ref[...]
        _, xhat = _rms_stats(hv)
        err = xhat * gv - t_ref[...]
        per_row = jnp.mean(err * err, axis=-1, keepdims=True)
        loss_ref[...] += 0.5 * jnp.sum(per_row, axis=0, keepdims=True)
        dx, dgrow = _rms_bwd(err * (1.0 / d), hv, gv)
        dh_ref[...] = dx
        dg_ref[...] += jnp.sum(dgrow, axis=0, keepdims=True)

    row = pl.BlockSpec((tm, d), lambda i: (i, 0))
    vec = pl.BlockSpec((1, d), lambda i: (0, 0))
    return pl.pallas_call(
        body, name=name, grid=(s // tm,), in_specs=[row, vec, row],
        out_specs=[pl.BlockSpec((1, LANES), lambda i: (0, 0)), row, vec],
        out_shape=[jax.ShapeDtypeStruct((1, LANES), F32), jax.ShapeDtypeStruct((s, d), F32),
                   jax.ShapeDtypeStruct((1, d), F32)],
        compiler_params=_cp("arbitrary"))(h, g, tgt)


def _rope_fwd(y1, y2, cos, sin, *, name, tm=512):
    s, r = y1.shape

    def body(a_ref, b_ref, c_ref, s_ref, o_ref):
        o_ref[...] = a_ref[...] * c_ref[...] + b_ref[...] * s_ref[...]

    row = pl.BlockSpec((tm, r), lambda i: (i, 0))
    return pl.pallas_call(
        body, name=name, grid=(s // tm,), in_specs=[row] * 4, out_specs=row,
        out_shape=jax.ShapeDtypeStruct((s, r), F32), compiler_params=_cp("arbitrary"))(y1, y2, cos, sin)


def _split3(v):
    h1 = v.astype(BF16)
    r1 = v - h1.astype(F32)
    h2 = r1.astype(BF16)
    h3 = (r1 - h2.astype(F32)).astype(BF16)
    return h1, h2, h3


def _tri(tb, upper):
    r = lax.broadcasted_iota(jnp.int32, (tb, tb), 0)
    c = lax.broadcasted_iota(jnp.int32, (tb, tb), 1)
    return jnp.where((r <= c) if upper else (r >= c), 1.0, 0.0).astype(BF16)


def _fox_gate_fwd(ft, bf, *, out_scale, name, tb=512):
    nh, s = ft.shape

    def body(f_ref, b_ref, o_ref, carry):
        @pl.when(pl.program_id(0) == 0)
        def _():
            carry[...] = jnp.zeros_like(carry)

        z = f_ref[...] + b_ref[...]
        lf = jnp.minimum(z, 0.0) - jnp.log(1.0 + jnp.exp(-jnp.abs(z)))
        tri = _tri(tb, True)
        cs = sum(jnp.dot(t, tri, preferred_element_type=F32) for t in _split3(lf)) + carry[...]
        for n, term in enumerate(_split3(cs * out_scale)):
            o_ref[n] = term
        carry[...] += jnp.sum(lf, axis=-1, keepdims=True)

    return pl.pallas_call(
        body, name=name, grid=(s // tb,),
        in_specs=[pl.BlockSpec((nh, tb), lambda t: (0, t)), pl.BlockSpec((nh, 1), lambda t: (0, 0))],
        out_specs=pl.BlockSpec((3, nh, tb), lambda t: (0, 0, t)),
        out_shape=jax.ShapeDtypeStruct((3, nh, s), BF16),
        scratch_shapes=[pltpu.VMEM((nh, 1), F32)], compiler_params=_cp("arbitrary"))(ft, bf)


def _fox_gate_bwd(drow, dcol, ft, bf, *, inv_scale, name, tb=512):
    nh, s = ft.shape
    nb = s // tb

    def body(dr_ref, dc_ref, f_ref, b_ref, df_ref, db_ref, carry):
        @pl.when(pl.program_id(0) == 0)
        def _():
            carry[...] = jnp.zeros_like(carry)
            db_ref[...] = jnp.zeros_like(db_ref)

        dc = (dr_ref[...] - dc_ref[...]) * inv_scale
        tri = _tri(tb, False)
        suf = sum(jnp.dot(t, tri, preferred_element_type=F32) for t in _split3(dc)) + carry[...]
        z = f_ref[...] + b_ref[...]
        dz = suf * (1.0 / (1.0 + jnp.exp(z)))
        df_ref[...] = dz
        db_ref[...] += jnp.sum(dz, axis=-1, keepdims=True)
        carry[...] += jnp.sum(dc, axis=-1, keepdims=True)

    rev = pl.BlockSpec((nh, tb), lambda t: (0, nb - 1 - t))
    one = pl.BlockSpec((nh, 1), lambda t: (0, 0))
    return pl.pallas_call(
        body, name=name, grid=(nb,), in_specs=[rev, rev, rev, one], out_specs=[rev, one],
        out_shape=[jax.ShapeDtypeStruct((nh, s), F32), jax.ShapeDtypeStruct((nh, 1), F32)],
        scratch_shapes=[pltpu.VMEM((nh, 1), F32)], compiler_params=_cp("arbitrary"))(drow, dcol, ft, bf)


def _tri_fwd(t, nq):
    i = sum((t >= (r * (r + 1)) // 2).astype(jnp.int32) for r in range(1, nq))
    return i, t - (i * (i + 1)) // 2


def _tri_bwd(t, nq):
    j = sum((t >= r * nq - (r * (r - 1)) // 2).astype(jnp.int32) for r in range(1, nq))
    return j, j + t - (j * nq - (j * (j - 1)) // 2)


def _scores_t(k, q, *, scale, diag):
    s = lax.dot_general(k, q, NT, preferred_element_type=F32) * scale
    if diag:
        r = lax.broadcasted_iota(jnp.int32, s.shape, 0)
        c = lax.broadcasted_iota(jnp.int32, s.shape, 1)
        s = jnp.where(r <= c, s, MASK_VALUE)
    return s


def _causal_fwd_t(q, k, vt, *, scale, name, tq, hb=2, rider=None):
    nh, s, dq = q.shape
    dv = vt.shape[1]
    nq = s // tq
    nsteps = (nq * (nq + 1)) // 2

    def body(q_ref, k_ref, vt_ref, o_ref, lse_ref, m_sc, l_sc, acc_sc):
        i, j = _tri_fwd(pl.program_id(1), nq)

        @pl.when(j == 0)
        def _():
            m_sc[...] = jnp.full_like(m_sc, MASK_VALUE)
            l_sc[...] = jnp.zeros_like(l_sc)
            acc_sc[...] = jnp.zeros_like(acc_sc)

        def step(diag):
            for u in range(hb):
                sc = _scores_t(k_ref[u], q_ref[u], scale=scale, diag=diag)
                m_prev = m_sc[u]
                m_new = jnp.maximum(m_prev, jnp.max(sc, axis=0, keepdims=True))
                alpha = jnp.exp(m_prev - m_new)
                pr = jnp.exp(sc - m_new)
                l_new = alpha * l_sc[u] + jnp.sum(pr, axis=0, keepdims=True)
                acc = alpha * acc_sc[u] + jnp.dot(vt_ref[u], pr.astype(BF16), preferred_element_type=F32)
                if diag:
                    o_ref[u] = (acc / l_new).astype(BF16)
                    lse_ref[u] = m_new + jnp.log(l_new)
                else:
                    m_sc[u], l_sc[u], acc_sc[u] = m_new, l_new, acc

        pl.when(j < i)(functools.partial(step, False))
        pl.when(j == i)(functools.partial(step, True))

    def qi(t):
        return _tri_fwd(t, nq)[0]

    def kj(t):
        return _tri_fwd(t, nq)[1]

    return _call_with_rider(
        body, rider, name=name, grid=(nh // hb, nsteps),
        in_specs=[pl.BlockSpec((hb, tq, dq), lambda hp, t: (hp, qi(t), 0)),
                  pl.BlockSpec((hb, tq, dq), lambda hp, t: (hp, kj(t), 0)),
                  pl.BlockSpec((hb, dv, tq), lambda hp, t: (hp, 0, kj(t)))],
        out_specs=[pl.BlockSpec((hb, dv, tq), lambda hp, t: (hp, 0, qi(t))),
                   pl.BlockSpec((hb, 1, tq), lambda hp, t: (hp, 0, qi(t)))],
        out_shape=[jax.ShapeDtypeStruct((nh, dv, s), BF16), jax.ShapeDtypeStruct((nh, 1, s), F32)],
        scratch_shapes=[pltpu.VMEM((hb, 1, tq), F32), pltpu.VMEM((hb, 1, tq), F32), pltpu.VMEM((hb, dv, tq), F32)],
        compiler_params=_cp("arbitrary", "arbitrary"), args=(q, k, vt))


def _causal_bwd_t(q, k, v, ot, dot_, lse, *, scale, name, tq, hb=2, rider=None):
    nh, s, dq = q.shape
    dv = v.shape[-1]
    nq = s // tq
    nsteps = (nq * (nq + 1)) // 2

    def body(q_ref, k_ref, v_ref, ot_ref, dot_ref, lse_ref, dq_ref, dk_ref, dvt_ref):
        t = pl.program_id(1)
        j, i = _tri_bwd(t, nq)

        @pl.when(t == 0)
        def _():
            dq_ref[...] = jnp.zeros_like(dq_ref)

        def step(diag):
            rows = pl.ds(pl.multiple_of(i * tq, tq), tq)
            for u in range(hb):
                qv, kv, dov = q_ref[u], k_ref[u], dot_ref[u]
                pr = jnp.exp(_scores_t(kv, qv, scale=scale, diag=diag) - lse_ref[u])
                dp = jnp.dot(v_ref[u], dov, preferred_element_type=F32)
                delta = jnp.sum(dov.astype(F32) * ot_ref[u].astype(F32), axis=0, keepdims=True)
                dsb = ((pr * (dp - delta)) * scale).astype(BF16)
                d_v = lax.dot_general(dov, pr.astype(BF16), NT, preferred_element_type=F32)
                d_k = jnp.dot(dsb, qv, preferred_element_type=F32)
                if diag:
                    dvt_ref[u], dk_ref[u] = d_v, d_k
                else:
                    dvt_ref[u] += d_v
                    dk_ref[u] += d_k
                dq_ref[u, rows, :] += lax.dot_general(dsb, kv, TN, preferred_element_type=F32)

        pl.when(i > j)(functools.partial(step, False))
        pl.when(i == j)(functools.partial(step, True))

    def qi(t):
        return _tri_bwd(t, nq)[1]

    def kj(t):
        return _tri_bwd(t, nq)[0]

    rows_q = pl.BlockSpec((hb, tq, dq), lambda hp, t: (hp, qi(t), 0))
    rows_k = pl.BlockSpec((hb, tq, dq), lambda hp, t: (hp, kj(t), 0))
    lanes_q = pl.BlockSpec((hb, dv, tq), lambda hp, t: (hp, 0, qi(t)))
    return _call_with_rider(
        body, rider, name=name, grid=(nh // hb, nsteps),
        in_specs=[rows_q, rows_k, pl.BlockSpec((hb, tq, dv), lambda hp, t: (hp, kj(t), 0)), lanes_q, lanes_q,
                  pl.BlockSpec((hb, 1, tq), lambda hp, t: (hp, 0, qi(t)))],
        out_specs=[pl.BlockSpec((hb, s, dq), lambda hp, t: (hp, 0, 0)), rows_k,
                   pl.BlockSpec((hb, dv, tq), lambda hp, t: (hp, 0, kj(t)))],
        out_shape=[jax.ShapeDtypeStruct((nh, s, dq), F32), jax.ShapeDtypeStruct((nh, s, dq), F32),
                   jax.ShapeDtypeStruct((nh, dv, s), F32)],
        scratch_shapes=[], compiler_params=_cp("arbitrary", "arbitrary"), args=(q, k, v, ot, dot_, lse))


def _swa_scores_t(k, q, dist, ok, *, scale, slope):
    s = lax.dot_general(k, q, NT, preferred_element_type=F32) * scale - slope * dist.astype(F32)
    return jnp.where(ok, s, MASK_VALUE)


def _swa_geometry(tb, w, has_other):
    r = lax.broadcasted_iota(jnp.int32, (tb, tb), 0)
    c = lax.broadcasted_iota(jnp.int32, (tb, tb), 1)
    d_same = c - r
    ok_same = jnp.logical_and(d_same >= 0, d_same < w)

    def other(ncols):
        rr = lax.broadcasted_iota(jnp.int32, (w, ncols), 0)
        cc = lax.broadcasted_iota(jnp.int32, (w, ncols), 1)
        dd = cc + w - rr
        return dd, jnp.logical_and(dd < w, has_other)

    return (d_same, ok_same), other


def _swa_fwd_t(q, k, vt, slopes_sinks, *, scale, window, name, tb=256):
    nh, s, d = q.shape
    nkv = k.shape[0]
    grp = nh // nkv
    w = window
    per = tb // w
    assert tb % w == 0

    def body(q_ref, kc_ref, kp_ref, vc_ref, vp_ref, ss_ref, o_ref, lse_ref):
        kvh, i = pl.program_id(0), pl.program_id(1)
        (d_c, ok_c), other = _swa_geometry(tb, w, i > 0)
        d_p, ok_p = other(tb)
        for g in range(grp):
            h = kvh * grp + g
            slope, sink = ss_ref[0, h], ss_ref[1, h]
            qg = q_ref[g]
            s_c = _swa_scores_t(kc_ref[...], qg, d_c, ok_c, scale=scale, slope=slope)
            s_p = _swa_scores_t(kp_ref[...], qg, d_p, ok_p, scale=scale, slope=slope)
            m = jnp.maximum(jnp.maximum(jnp.max(s_c, axis=0, keepdims=True), jnp.max(s_p, axis=0, keepdims=True)), sink)
            p_c, p_p = jnp.exp(s_c - m), jnp.exp(s_p - m)
            l = jnp.sum(p_c, axis=0, keepdims=True) + jnp.sum(p_p, axis=0, keepdims=True) + jnp.exp(sink - m)
            acc = (jnp.dot(vc_ref[...], p_c.astype(BF16), preferred_element_type=F32)
                   + jnp.dot(vp_ref[...], p_p.astype(BF16), preferred_element_type=F32))
            o_ref[g] = (acc / l).astype(BF16)
            lse_ref[g] = m + jnp.log(l)

    def prev(i):
        return jnp.maximum(i * per - 1, 0)

    return pl.pallas_call(
        body, name=name, grid=(nkv, s // tb),
        in_specs=[pl.BlockSpec((grp, tb, d), lambda kh, i: (kh, i, 0)),
                  pl.BlockSpec((None, tb, d), lambda kh, i: (kh, i, 0)),
                  pl.BlockSpec((None, w, d), lambda kh, i: (kh, prev(i), 0)),
                  pl.BlockSpec((None, d, tb), lambda kh, i: (kh, 0, i)),
                  pl.BlockSpec((None, d, w), lambda kh, i: (kh, 0, prev(i))),
                  pl.BlockSpec(memory_space=pltpu.SMEM)],
        out_specs=[pl.BlockSpec((grp, d, tb), lambda kh, i: (kh, 0, i)), pl.BlockSpec((grp, 1, tb), lambda kh, i: (kh, 0, i))],
        out_shape=[jax.ShapeDtypeStruct((nh, d, s), BF16), jax.ShapeDtypeStruct((nh, 1, s), F32)],
        compiler_params=_cp("arbitrary", "arbitrary"))(q, k, k, vt, vt, slopes_sinks)


def _swa_bwd_t(q, k, v, ot, dot_, lse, slopes_sinks, *, scale, window, name, tb=256, rider=None):
    nh, s, d = q.shape
    nkv = k.shape[0]
    grp = nh // nkv
    w = window
    per = tb // w
    nb = s // tb

    def body(qc_ref, qn_ref, kc_ref, kp_ref, vc_ref, vp_ref, oc_ref, on_ref, doc_ref, don_ref, lc_ref, ln_ref, ss_ref,
             dq_ref, dk_ref, dvt_ref, dsink_ref):
        kvh, i = pl.program_id(0), pl.program_id(1)

        @pl.when(i == 0)
        def _():
            dsink_ref[...] = jnp.zeros_like(dsink_ref)

        (d_c, ok_c), other = _swa_geometry(tb, w, i > 0)
        d_p, ok_p = other(tb)
        d_n, ok_n = _swa_geometry(tb, w, i < nb - 1)[1](w)
        kc, kp, vc, vp = kc_ref[...], kp_ref[...], vc_ref[...], vp_ref[...]
        k_last, v_last = kc[tb - w:, :], vc[tb - w:, :]
        dk_acc = jnp.zeros((tb, d), F32)
        dv_acc = jnp.zeros((d, tb), F32)
        dk_tail = jnp.zeros((w, d), F32)
        dv_tail = jnp.zeros((d, w), F32)
        for g in range(grp):
            h = kvh * grp + g
            slope, sink = ss_ref[0, h], ss_ref[1, h]
            qg, dog, lse_c = qc_ref[g], doc_ref[g], lc_ref[g]
            delta = jnp.sum(dog.astype(F32) * oc_ref[g].astype(F32), axis=0, keepdims=True)
            p_c = jnp.exp(_swa_scores_t(kc, qg, d_c, ok_c, scale=scale, slope=slope) - lse_c)
            p_p = jnp.exp(_swa_scores_t(kp, qg, d_p, ok_p, scale=scale, slope=slope) - lse_c)
            ds_c = ((p_c * (jnp.dot(vc, dog, preferred_element_type=F32) - delta)) * scale).astype(BF16)
            ds_p = ((p_p * (jnp.dot(vp, dog, preferred_element_type=F32) - delta)) * scale).astype(BF16)
            dq_ref[g] = (lax.dot_general(ds_c, kc, TN, preferred_element_type=F32)
                         + lax.dot_general(ds_p, kp, TN, preferred_element_type=F32))
            dk_acc += jnp.dot(ds_c, qg, preferred_element_type=F32)
            dv_acc += lax.dot_general(dog, p_c.astype(BF16), NT, preferred_element_type=F32)
            dsink_ref[g] -= jnp.broadcast_to(jnp.sum(jnp.exp(sink - lse_c) * delta, axis=1, keepdims=True), (1, LANES))
            qn, don = qn_ref[g], don_ref[g]
            delta_n = jnp.sum(don.astype(F32) * on_ref[g].astype(F32), axis=0, keepdims=True)
            p_n = jnp.exp(_swa_scores_t(k_last, qn, d_n, ok_n, scale=scale, slope=slope) - ln_ref[g])
            ds_n = ((p_n * (jnp.dot(v_last, don, preferred_element_type=F32) - delta_n)) * scale).astype(BF16)
            dk_tail += jnp.dot(ds_n, qn, preferred_element_type=F32)
            dv_tail += lax.dot_general(don, p_n.astype(BF16), NT, preferred_element_type=F32)
        dk_ref[...] = dk_acc
        dvt_ref[...] = dv_acc
        dk_ref[tb - w:, :] += dk_tail
        dvt_ref[:, tb - w:] += dv_tail

    def prev(i):
        return jnp.maximum(i * per - 1, 0)

    def nxt(i):
        return jnp.minimum((i + 1) * per, s // w - 1)

    return _call_with_rider(
        body, rider, name=name, grid=(nkv, nb), scratch_shapes=[],
        args=(q, q, k, k, v, v, ot, ot, dot_, dot_, lse, lse, slopes_sinks),
        in_specs=[pl.BlockSpec((grp, tb, d), lambda kh, i: (kh, i, 0)),
                  pl.BlockSpec((grp, w, d), lambda kh, i: (kh, nxt(i), 0)),
                  pl.BlockSpec((None, tb, d), lambda kh, i: (kh, i, 0)),
                  pl.BlockSpec((None, w, d), lambda kh, i: (kh, prev(i), 0)),
                  pl.BlockSpec((None, tb, d), lambda kh, i: (kh, i, 0)),
                  pl.BlockSpec((None, w, d), lambda kh, i: (kh, prev(i), 0)),
                  pl.BlockSpec((grp, d, tb), lambda kh, i: (kh, 0, i)),
                  pl.BlockSpec((grp, d, w), lambda kh, i: (kh, 0, nxt(i))),
                  pl.BlockSpec((grp, d, tb), lambda kh, i: (kh, 0, i)),
                  pl.BlockSpec((grp, d, w), lambda kh, i: (kh, 0, nxt(i))),
                  pl.BlockSpec((grp, 1, tb), lambda kh, i: (kh, 0, i)),
                  pl.BlockSpec((grp, 1, w), lambda kh, i: (kh, 0, nxt(i))),
                  pl.BlockSpec(memory_space=pltpu.SMEM)],
        out_specs=[pl.BlockSpec((grp, tb, d), lambda kh, i: (kh, i, 0)),
                   pl.BlockSpec((None, tb, d), lambda kh, i: (kh, i, 0)),
                   pl.BlockSpec((None, d, tb), lambda kh, i: (kh, 0, i)),
                   pl.BlockSpec((None, grp, 1, LANES), lambda kh, i: (kh, 0, 0, 0))],
        out_shape=[jax.ShapeDtypeStruct((nh, s, d), F32), jax.ShapeDtypeStruct((nkv, s, d), F32),
                   jax.ShapeDtypeStruct((nkv, d, s), F32), jax.ShapeDtypeStruct((nkv, grp, 1, LANES), F32)],
        compiler_params=_cp("arbitrary", "arbitrary"))


def _adamw(w, g, m, v, *, name):
    shape = w.shape
    cols = shape[-1]
    rows = int(np.prod(shape[:-1])) if len(shape) > 1 else 1
    tr = _row_tile(rows, cols)
    c1 = 1.0 - ADAM_B1 ** ADAM_STEP
    c2 = 1.0 - ADAM_B2 ** ADAM_STEP

    def body(w_ref, g_ref, m_ref, v_ref, d_ref, mo_ref, vo_ref):
        gv = g_ref[...]
        mn = ADAM_B1 * m_ref[...] + (1.0 - ADAM_B1) * gv
        vn = ADAM_B2 * v_ref[...] + (1.0 - ADAM_B2) * (gv * gv)
        mo_ref[...] = mn
        vo_ref[...] = vn
        d_ref[...] = -ADAM_LR * ((mn / c1) / (jnp.sqrt(vn / c2) + ADAM_EPS) + ADAM_WD * w_ref[...])

    blk = pl.BlockSpec((tr, cols), lambda i: (i, 0))
    outs = pl.pallas_call(
        body, name=name, grid=(rows // tr,), in_specs=[blk] * 4, out_specs=[blk] * 3,
        out_shape=[jax.ShapeDtypeStruct((rows, cols), F32)] * 3,
        compiler_params=_cp("arbitrary"))(*[a.reshape(rows, cols) for a in (w, g, m, v)])
    return tuple(a.reshape(shape) for a in outs)


def _hbm_spec():
    return pl.BlockSpec(memory_space=pl.ANY)


def _mesh_place():
    x, y, c = lax.axis_index("x"), lax.axis_index("y"), lax.axis_index("c")
    return x, y, c, [(1 - x, y), (x, 1 - y), (1 - x, 1 - y)]


def _half_rows(c, rows, align):
    return pl.ds(pl.multiple_of(c * (rows // 2), align), rows // 2)


def _part(ref, mode, k, n, rows=None):
    if mode == "cols":
        cols = pl.ds(pl.multiple_of(k * n, LANES), n)
        return ref.at[:, cols] if rows is None else ref.at[rows, cols]
    return ref.at[k] if rows is None else ref.at[k, rows, :]


class _Rider:
    def __init__(self, inputs, out_shape, n_sems, start, finish):
        self.inputs, self.out_shape, self.n_sems, self.start, self.finish = inputs, out_shape, n_sems, start, finish


def _call_with_rider(body, rider, *, name, grid, in_specs, out_specs, out_shape, scratch_shapes, compiler_params, args):
    if rider is None:
        outs = pl.pallas_call(body, name=name, grid=grid, in_specs=in_specs, out_specs=out_specs, out_shape=out_shape,
                              scratch_shapes=scratch_shapes, compiler_params=compiler_params)(*args)
        return outs, []
    n_in, n_out, n_sc = len(in_specs), len(out_specs), len(scratch_shapes)
    n_rin, n_rout = len(rider.inputs), len(rider.out_shape)

    def wrapped(*refs):
        pos = 0
        groups = []
        for n in (n_in, n_rin, n_out, n_rout, n_sc, 2):
            groups.append(refs[pos:pos + n])
            pos += n
        ins, rins, outs, routs, scratch, sems = groups
        ids = [pl.program_id(a) for a in range(len(grid))]
        first = functools.reduce(jnp.logical_and, [i == 0 for i in ids])
        last = functools.reduce(jnp.logical_and, [i == g - 1 for i, g in zip(ids, grid)])
        pl.when(first)(lambda: rider.start(rins, routs, *sems))
        body(*ins, *outs, *scratch)
        pl.when(last)(lambda: rider.finish(rins, routs, *sems))

    outs = pl.pallas_call(
        wrapped, name=name, grid=grid, in_specs=list(in_specs) + [_hbm_spec()] * n_rin,
        out_specs=list(out_specs) + [_hbm_spec()] * n_rout, out_shape=list(out_shape) + list(rider.out_shape),
        scratch_shapes=list(scratch_shapes) + [pltpu.SemaphoreType.DMA((rider.n_sems,))] * 2,
        compiler_params=compiler_params)(*args, *rider.inputs)
    return outs[:n_out], outs[n_out:]


def _run_rider(rider, *, name):
    n_rin = len(rider.inputs)

    def body(*refs):
        rins, routs, sems = refs[:n_rin], refs[n_rin:-2], refs[-2:]
        rider.start(rins, routs, *sems)
        rider.finish(rins, routs, *sems)

    return pl.pallas_call(
        body, name=name, in_specs=[_hbm_spec()] * n_rin, out_specs=[_hbm_spec()] * len(rider.out_shape),
        out_shape=rider.out_shape, scratch_shapes=[pltpu.SemaphoreType.DMA((rider.n_sems,))] * 2)(*rider.inputs)


def _gather_rider(shards, modes):
    n_arr = len(shards)
    out_shape = [jax.ShapeDtypeStruct((s.shape[0], N_CHIPS * s.shape[1]) if m == "cols" else (N_CHIPS,) + s.shape, s.dtype)
                 for s, m in zip(shards, modes)]
    per = 4

    def copies(srcs, dsts, send_sems, recv_sems):
        x, y, c, chips = _mesh_place()
        me = 2 * x + y
        sends, waits = [], []
        for i in range(n_arr):
            r, n = shards[i].shape
            rows = _half_rows(c, r, 16)

            def copy(slot, src, dst, to, i=i):
                return pltpu.make_async_remote_copy(src_ref=src, dst_ref=dst, send_sem=send_sems.at[i * per + slot],
                                                    recv_sem=recv_sems.at[i * per + slot], device_id=to, device_id_type=MESH)

            own = _part(dsts[i], modes[i], me, n)
            sends.append(copy(0, srcs[i], own, (x, y, 1 - c)))
            waits.append(copy(0, own, own, (x, y, 1 - c)))
            for j, (px, py) in enumerate(chips):
                sends.append(copy(1 + j, srcs[i].at[rows], _part(dsts[i], modes[i], me, n, rows), (px, py, c)))
                theirs = _part(dsts[i], modes[i], 2 * px + py, n, rows)
                waits.append(copy(1 + j, theirs, theirs, (px, py, c)))
        return sends, waits

    def start(*refs):
        for cp in copies(*refs)[0]:
            cp.start()

    def finish(*refs):
        sends, waits = copies(*refs)
        for cp in waits:
            cp.wait_recv()
        for cp in sends:
            cp.wait_send()

    return _Rider(list(shards), out_shape, per * n_arr, start, finish)


def _gather_forward(dsts, shard_shapes, modes, *, name):
    n_arr = len(dsts)

    def body(*refs):
        outs = refs[n_arr:2 * n_arr]
        send_sems, recv_sems = refs[2 * n_arr:]
        x, y, c, chips = _mesh_place()
        cps = []
        for i in range(n_arr):
            r, n = shard_shapes[i]
            for j, (px, py) in enumerate(chips):
                def view(hc, i=i, px=px, py=py, r=r, n=n):
                    return _part(outs[i], modes[i], 2 * px + py, n, _half_rows(hc, r, 16))

                def copy(ref, i=i, j=j):
                    return pltpu.make_async_remote_copy(src_ref=ref, dst_ref=ref, send_sem=send_sems.at[3 * i + j],
                                                        recv_sem=recv_sems.at[3 * i + j], device_id=(x, y, 1 - c), device_id_type=MESH)

                cps.append((copy(view(c)), copy(view(1 - c))))
        for send, _ in cps:
            send.start()
        for send, theirs in cps:
            theirs.wait_recv()
            send.wait_send()

    return pl.pallas_call(
        body, name=name, in_specs=[_hbm_spec()] * n_arr, out_specs=[_hbm_spec()] * n_arr,
        out_shape=[jax.ShapeDtypeStruct(d.shape, d.dtype) for d in dsts],
        input_output_aliases={i: i for i in range(n_arr)},
        scratch_shapes=[pltpu.SemaphoreType.DMA((3 * n_arr,)), pltpu.SemaphoreType.DMA((3 * n_arr,))])(*dsts)


def _blk_view(a, mode):
    return a[None] if mode == "cols" else a


def _swap_rider(arrs, modes):
    n_arr = len(arrs)
    out_shape = [jax.ShapeDtypeStruct((a.shape[0] // 2, a.shape[1]) if m == "cols" else (a.shape[0], a.shape[1] // 2, a.shape[2]), a.dtype)
                 for a, m in zip(arrs, modes)]

    def copies(srcs, dsts, send_sems, recv_sems):
        x, y, c, _ = _mesh_place()
        cps = []
        for i in range(n_arr):
            if modes[i] == "cols":
                src = srcs[i].at[_half_rows(1 - c, arrs[i].shape[0], 8)]
            else:
                src = srcs[i].at[:, _half_rows(1 - c, arrs[i].shape[1], 8), :]
            cps.append(pltpu.make_async_remote_copy(src_ref=src, dst_ref=dsts[i], send_sem=send_sems.at[i],
                                                    recv_sem=recv_sems.at[i], device_id=(x, y, 1 - c), device_id_type=MESH))
        return cps

    def start(*refs):
        for cp in copies(*refs):
            cp.start()

    def finish(*refs):
        for cp in copies(*refs):
            cp.wait()

    return _Rider(list(arrs), out_shape, n_arr, start, finish)


def _rs_pair_add(arr, landed, place, *, name):
    nb, r, c = arr.shape
    rh = r // 2
    tr = _row_tile(rh, c)
    nt = rh // tr

    def body(p_ref, a_ref, l_ref, o_ref):
        o_ref[...] = (a_ref[...] + l_ref[...]).astype(BF16)

    grid_spec = pltpu.PrefetchScalarGridSpec(
        num_scalar_prefetch=1, grid=(nb, nt),
        in_specs=[pl.BlockSpec((None, tr, c), lambda b, t, p_ref: (b, p_ref[1] * nt + t, 0)),
                  pl.BlockSpec((None, tr, c), lambda b, t, p_ref: (b, t, 0))],
        out_specs=pl.BlockSpec((None, tr, c), lambda b, t, p_ref: (b, t, 0)))
    return pl.pallas_call(
        body, name=name, grid_spec=grid_spec, out_shape=jax.ShapeDtypeStruct((nb, rh, c), BF16),
        compiler_params=_cp("arbitrary", "arbitrary"))(place, arr, landed)


def _exchange_rider(parts, modes):
    n_arr = len(parts)
    out_shape = []
    for a, m in zip(parts, modes):
        shp = (a.shape[0], a.shape[1] // N_CHIPS) if m == "cols" else a.shape[1:]
        out_shape.append(jax.ShapeDtypeStruct((3,) + shp, a.dtype))

    def copies(srcs, dsts, send_sems, recv_sems):
        x, y, c, chips = _mesh_place()
        cps = []
        for i in range(n_arr):
            n = out_shape[i].shape[-1]
            for j, (px, py) in enumerate(chips):
                cps.append(pltpu.make_async_remote_copy(
                    src_ref=_part(srcs[i], modes[i], 2 * px + py, n), dst_ref=dsts[i].at[j],
                    send_sem=send_sems.at[3 * i + j], recv_sem=recv_sems.at[3 * i + j],
                    device_id=(px, py, c), device_id_type=MESH))
        return cps

    def start(*refs):
        for cp in copies(*refs):
            cp.start()

    def finish(*refs):
        for cp in copies(*refs):
            cp.wait()

    return _Rider(list(parts), out_shape, 3 * n_arr, start, finish)


def _rs_chip_sum(part, landed, mode, place, *, name):
    _, rh, n = landed.shape
    tr = _row_tile(rh, n)
    nt = rh // tr

    def body(p_ref, a_ref, l_ref, o_ref):
        o_ref[...] = ((a_ref[...].astype(F32) + l_ref[0].astype(F32)) + l_ref[1].astype(F32)) + l_ref[2].astype(F32)

    if mode == "cols":
        own = pl.BlockSpec((tr, n), lambda t, p_ref: (t, p_ref[0]))
    else:
        own = pl.BlockSpec((None, tr, n), lambda t, p_ref: (p_ref[0], t, 0))
    grid_spec = pltpu.PrefetchScalarGridSpec(
        num_scalar_prefetch=1, grid=(nt,),
        in_specs=[own, pl.BlockSpec((3, tr, n), lambda t, p_ref: (0, t, 0))],
        out_specs=pl.BlockSpec((tr, n), lambda t, p_ref: (p_ref[1] * nt + t, 0)))
    return pl.pallas_call(
        body, name=name, grid_spec=grid_spec, out_shape=jax.ShapeDtypeStruct((2 * rh, n), F32),
        compiler_params=_cp("arbitrary"))(place, part, landed)


def _rs_pair_join(halves, *, name):
    n_arr = len(halves)

    def body(*refs):
        outs = refs[n_arr:2 * n_arr]
        send_sems, recv_sems = refs[2 * n_arr:]
        x, y, c, _ = _mesh_place()
        cps = []
        for i in range(n_arr):
            rows = _half_rows(c, halves[i].shape[0], 8)
            cps.append(pltpu.make_async_remote_copy(src_ref=outs[i].at[rows], dst_ref=outs[i].at[rows], send_sem=send_sems.at[i],
                                                    recv_sem=recv_sems.at[i], device_id=(x, y, 1 - c), device_id_type=MESH))
        for cp in cps:
            cp.start()
        for i, cp in enumerate(cps):
            cp.wait_send()
            theirs = outs[i].at[_half_rows(1 - c, halves[i].shape[0], 8)]
            pltpu.make_async_remote_copy(src_ref=theirs, dst_ref=theirs, send_sem=send_sems.at[i], recv_sem=recv_sems.at[i],
                                         device_id=(x, y, 1 - c), device_id_type=MESH).wait_recv()

    return pl.pallas_call(
        body, name=name, in_specs=[_hbm_spec()] * n_arr, out_specs=[_hbm_spec()] * n_arr,
        out_shape=[jax.ShapeDtypeStruct(h.shape, h.dtype) for h in halves],
        input_output_aliases={i: i for i in range(n_arr)},
        scratch_shapes=[pltpu.SemaphoreType.DMA((n_arr,)), pltpu.SemaphoreType.DMA((n_arr,))])(*halves)


def _allreduce_small(v, *, name):
    r, c = v.shape

    def body(v_ref, o_ref, gath, send_sems, recv_sems):
        x, y, cc, _ = _mesh_place()
        me = 4 * x + 2 * y + cc
        gath[me] = v_ref[...]
        cps = []
        for rel in range(1, 8):
            px = 1 - x if rel & 4 else x
            py = 1 - y if rel & 2 else y
            pc = 1 - cc if rel & 1 else cc

            def copy(slot, px=px, py=py, pc=pc, rel=rel):
                return pltpu.make_async_remote_copy(
                    src_ref=v_ref, dst_ref=gath.at[slot], send_sem=send_sems.at[rel - 1],
                    recv_sem=recv_sems.at[rel - 1], device_id=(px, py, pc), device_id_type=MESH)

            cps.append((copy(me), copy(4 * px + 2 * py + pc)))
        for send, _ in cps:
            send.start()
        for send, theirs in cps:
            theirs.wait_recv()
            send.wait_send()
        tot = gath[0]
        for d in range(1, 8):
            tot = tot + gath[d]
        o_ref[...] = tot

    vm = pl.BlockSpec(memory_space=pltpu.VMEM)
    return pl.pallas_call(
        body, name=name, in_specs=[vm], out_specs=vm, out_shape=jax.ShapeDtypeStruct((r, c), F32),
        scratch_shapes=[pltpu.VMEM((8, r, c), F32), pltpu.SemaphoreType.DMA((7,)), pltpu.SemaphoreType.DMA((7,))])(v)


def _rope_tables(s, reps):
    half = B_ROPE // 2
    inv = ROPE_THETA ** (-jnp.arange(0, B_ROPE, 2, dtype=F32) / B_ROPE)
    ang = jnp.arange(s, dtype=F32)[:, None] * inv[None, :]
    return jnp.tile(jnp.cos(ang), (1, reps)), jnp.tile(jnp.sin(ang), (1, reps))


def _alibi_slopes():
    return 2.0 ** (-8.0 * jnp.arange(1, A_HEADS + 1, dtype=F32) / A_HEADS)


def _ffn_fwd(h, norm, wts, tag, rider=None, on_rode=None):
    (dact_dgate, dact_dup, act, xn), rode = _ffn_up(h, norm, wts["wgu"], name=f"{tag}_up", rider=rider)
    if on_rode is not None:
        on_rode(rode)
    out = _mm_res_fwd(act, wts["wd"], h, scale=FFN_RES_SCALE, name=f"{tag}_down")
    return out, dict(h_in=h, dact_dgate=dact_dgate, dact_dup=dact_dup, act=act, xn=xn), rode


def _ffn_bwd(dh, norm, wts, sv, tag, rider=None, own=None):
    (dgate, dup), rode = _ffn_down_bwd(dh, wts["wd"], sv["dact_dgate"], sv["dact_dup"], scale=FFN_RES_SCALE,
                                      name=f"{tag}_down_bwd", rider=rider)
    d_wd = _mm_tn(sv["act"], dh, b_scale=FFN_RES_SCALE, name=f"{tag}_dwd")
    pairs = [(dgate, wts["wgu"], 0), (dup, wts["wgu"], 1)]
    if own is None:
        d_wgu = _mm_tn(sv["xn"], [dgate, dup], name=f"{tag}_dwgu")
        dh_in, dnorm = _mm_nt_rmsbwd(pairs, sv["h_in"], norm, dh, name=f"{tag}_dx")
    else:
        wd_ready, wgu_ready, done = own
        first = wd_ready(d_wd)
        res = _mm_tn(sv["xn"], [dgate, dup], name=f"{tag}_dwgu", rider=first)
        d_wgu, brought = (res, []) if first is None else res
        second = wgu_ready(brought, d_wgu)
        res = _mm_nt_rmsbwd(pairs, sv["h_in"], norm, dh, name=f"{tag}_dx", rider=second)
        dh_in, dnorm, brought = (*res, []) if second is None else res
        done(brought)
    return dh_in, dnorm, d_wgu, d_wd, rode


def _even_weights(w_in, w_uq, w_ukv):
    half = B_ROPE // 2
    base = w_in.shape[1]
    kr1, kr2 = w_in[:, base - B_ROPE:base - half], w_in[:, base - half:]
    w_in_cat = jnp.concatenate([w_in, -kr2, kr1, jnp.zeros((w_in.shape[0], 64), w_in.dtype)], axis=1)
    u3 = w_uq.reshape(w_uq.shape[0], B_HEADS, B_NOPE + B_ROPE)
    nope = u3[:, :, :B_NOPE].reshape(w_uq.shape[0], -1)
    rot = u3[:, :, B_NOPE:].reshape(w_uq.shape[0], -1)
    swapped = jnp.concatenate([-u3[:, :, B_NOPE + half:], u3[:, :, B_NOPE:B_NOPE + half]], axis=-1).reshape(w_uq.shape[0], -1)
    return w_in_cat, jnp.concatenate([nope, rot, swapped], axis=1), w_ukv


def _even_fwd(h, w, i, rider=None):
    s = h.shape[0]
    qa, ka, va, vat, c_q, c_kv, kr_blk, xn = _ev_in_fwd(h, w["mix_norm"][i:i + 1], w["ev_in_cat"], name="ev_in")
    cos32, sin32 = _rope_tables(s, 2)
    kro = _rope_fwd(kr_blk[:, :B_ROPE], kr_blk[:, B_ROPE:2 * B_ROPE], cos32, sin32, name="ev_k_rope")
    ss = jnp.stack([_alibi_slopes(), w["ev_sinks"].reshape(-1)])
    oa, lse_a = _swa_fwd_t(qa, ka, vat, ss, scale=A_HEAD_DIM ** -0.5, window=WINDOW, name="swa_fwd")
    cos256, sin256 = _rope_tables(s, 2 * B_HEADS)
    qb, xn_q = _ev_q_fwd(c_q, w["ev_cq_norm"], w["ev_q_cat"], cos256, sin256, name="ev_q_up")
    kb, vb, vbt, xn_kv = _ev_kv_fwd(c_kv, w["ev_ckv_norm"], w["ev_ukv"], kro, name="ev_kv_up")
    (ob, lse_b), rode = _causal_fwd_t(qb, kb, vbt, scale=(B_NOPE + B_ROPE) ** -0.5, name="mla_fwd", tq=512, hb=8, rider=rider)
    attn = jnp.concatenate([oa.reshape(-1, s), ob.reshape(-1, s)], axis=0)
    out = _mm_res_fwd(attn, w["ev_out"], h, scale=1.0, name="ev_out", a_t=True)
    sv = dict(h_in=h, xn=xn, c_q=c_q, c_kv=c_kv, xn_q=xn_q, xn_kv=xn_kv, qa=qa, ka=ka, va=va, oa=oa, lse_a=lse_a,
              ss=ss, qb=qb, kb=kb, vb=vb, ob=ob, lse_b=lse_b, attn=attn, cos32=cos32, sin32=sin32,
              cos256=cos256, sin256=sin256)
    return out, sv, rode


def _even_bwd(dh, w, sv, i, rider=None):
    s = dh.shape[0]
    half = B_ROPE // 2
    g = {}
    dattn = _mm_nt_t(dh, w["ev_out"], name="ev_out_dx")
    g["ev_w_out"] = _mm_tn(sv["attn"], dh, name="ev_out_dw", a_t=True)
    doa = dattn[:A_HEADS * A_HEAD_DIM].reshape(A_HEADS, A_HEAD_DIM, s)
    dob = dattn[A_HEADS * A_HEAD_DIM:].reshape(B_HEADS, B_V, s)
    first, then = rider if isinstance(rider, tuple) else (None, None)
    (dqa, dka, dva, dsink), brought = _swa_bwd_t(sv["qa"], sv["ka"], sv["va"], sv["oa"], doa, sv["lse_a"], sv["ss"],
                                                 scale=A_HEAD_DIM ** -0.5, window=WINDOW, name="swa_bwd", rider=first)
    if then is not None:
        rider = then(brought)
    g["ev_sinks"] = dsink[:, :, 0, 0].reshape(1, A_HEADS)
    (dqb, dkb, dvb), rode = _causal_bwd_t(sv["qb"], sv["kb"], sv["vb"], sv["ob"], dob, sv["lse_b"],
                                          scale=(B_NOPE + B_ROPE) ** -0.5, name="mla_bwd", tq=512, hb=4, rider=rider)
    dyq = _ev_q_merge(dqb, sv["cos256"], sv["sin256"], name="ev_q_merge")
    dwq = _mm_tn(sv["xn_q"], dyq, name="ev_q_up_dw")
    dcq, g["ev_cq_norm"] = _mm_nt_rmsbwd([(dyq, w["ev_q_cat"])], sv["c_q"], w["ev_cq_norm"], None, name="ev_q_up_dx")
    kq = sv["c_q"].shape[1]
    d_nope = dwq[:, :512].reshape(kq, B_HEADS, B_NOPE)
    d_rot = dwq[:, 512:768].reshape(kq, B_HEADS, B_ROPE)
    d_swp = dwq[:, 768:].reshape(kq, B_HEADS, B_ROPE)
    g["ev_w_uq"] = jnp.concatenate([d_nope, d_rot[:, :, :half] + d_swp[:, :, half:], d_rot[:, :, half:] - d_swp[:, :, :half]],
                                   axis=-1).reshape(kq, -1)
    dykv, dkr = _ev_kv_merge(dkb, dvb, sv["cos32"], sv["sin32"], name="ev_kv_merge")
    g["ev_w_ukv"] = _mm_tn(sv["xn_kv"], dykv, name="ev_kv_up_dw")
    dckv, g["ev_ckv_norm"] = _mm_nt_rmsbwd([(dykv, w["ev_ukv"])], sv["c_kv"], w["ev_ckv_norm"], None, name="ev_kv_up_dx")
    dycat = _ev_in_merge(dqa, dka, dva, dcq, dckv, dkr, name="ev_in_merge")
    dwin = _mm_tn(sv["xn"], dycat, name="ev_in_dw")
    base = 1184
    g["ev_w_in"] = jnp.concatenate([dwin[:, :base - B_ROPE],
                                    dwin[:, base - B_ROPE:base - half] + dwin[:, base + half:base + B_ROPE],
                                    dwin[:, base - half:base] - dwin[:, base:base + half]], axis=-1)
    dh_in, dnorm = _mm_nt_rmsbwd([(dycat, w["ev_in_cat"])], sv["h_in"], w["mix_norm"][i:i + 1], dh, name="ev_in_dx")
    return dh_in, dnorm, g, rode


def _odd_fwd(h, w, i, rider=None):
    s = h.shape[0]
    wd = C_HEADS * C_HEAD_DIM
    q, k, v, vt, y_f, xn = _fox_in_fwd(h, w["mix_norm"][i:i + 1], w["od_in_pad"], nheads=C_HEADS, dh=C_HEAD_DIM,
                                       q_ones=(0, 2, 3, 4), k_ones=(1,), name="od_in")
    scale = C_HEAD_DIM ** -0.5
    ft = y_f[:, :C_HEADS].T
    bf = w["od_b_f"].reshape(C_HEADS, 1)
    cb3 = _fox_gate_fwd(ft, bf, out_scale=-1.0 / scale, name="fox_gate_fwd")
    k = k + jnp.pad(cb3.transpose(1, 2, 0), ((0, 0), (0, 0), (C_HEAD_DIM + 2, LANES - C_HEAD_DIM - 5)))
    (o, lse), rode = _causal_fwd_t(q, k, vt, scale=scale, name="fox_fwd", tq=512, hb=16, rider=rider)
    attn = o.reshape(-1, s)
    out = _mm_res_fwd(attn, w["od_out"], h, scale=1.0, name="od_out", a_t=True)
    return out, dict(h_in=h, xn=xn, q=q, k=k, v=v, o=o, lse=lse, ft=ft, bf=bf, attn=attn), rode


def _odd_bwd(dh, w, sv, i, rider=None):
    s = dh.shape[0]
    g = {}
    dattn = _mm_nt_t(dh, w["od_out"], name="od_out_dx")
    g["od_w_out"] = _mm_tn(sv["attn"], dh, name="od_out_dw", a_t=True)
    do = dattn.reshape(C_HEADS, C_HEAD_DIM, s)
    scale = C_HEAD_DIM ** -0.5
    (dq, dk, dv), rode = _causal_bwd_t(sv["q"], sv["k"], sv["v"], sv["o"], do, sv["lse"], scale=scale, name="fox_bwd",
                                       tq=512, hb=4, rider=rider)
    dqkv, sums = _merge_heads(dq, dk, dv, dh=C_HEAD_DIM, q_col=C_HEAD_DIM + 1, k_col=C_HEAD_DIM, name="fox_merge")
    dft, dbf = _fox_gate_bwd(sums[:, :C_HEADS].T, sums[:, C_HEADS:2 * C_HEADS].T, sv["ft"], sv["bf"],
                             inv_scale=1.0 / scale, name="fox_gate_bwd")
    g["od_b_f"] = dbf.reshape(1, C_HEADS)
    wd = C_HEADS * C_HEAD_DIM
    df = jnp.pad(dft.T, ((0, 0), (0, LANES - C_HEADS)))
    g["od_w_in"] = jnp.concatenate([_mm_tn(sv["xn"], dqkv, name="od_in_dw"),
                                    _mm_tn(sv["xn"], df, name="od_in_dwf")[:, :C_HEADS]], axis=-1)
    dh_in, dnorm = _mm_nt_rmsbwd([(dqkv, w["od_in_pad"], 0), (df, w["od_in_pad"], 3 * wd // LANES)],
                                 sv["h_in"], w["mix_norm"][i:i + 1], dh, name="od_in_dx")
    return dh_in, dnorm, g, rode


def _kernel_weights(full, replicated):
    w = dict(replicated)
    _install_weights(w, {(n, i): a for n, per_layer in full.items() for i, a in enumerate(per_layer)})
    return w


def _install_weights(w, got):
    raw = w.setdefault("raw", {})
    raw.update(got)
    for (n, i), a in got.items():
        if n in ("ffa_w_gate_up", "ffa_w_down", "ffb_w_gate_up", "ffb_w_down"):
            w.setdefault(n[:3], {}).setdefault(i, {})["wgu" if n.endswith("gate_up") else "wd"] = a
        elif n in ("ple_w_gate", "ple_w_proj"):
            w.setdefault("ple_gate" if n.endswith("gate") else "ple_proj", {})[i] = a
    if "ev_in_cat" not in w and all((n, 0) in raw for n in ("ev_w_in", "ev_w_uq", "ev_w_ukv", "ev_w_out")):
        w["ev_in_cat"], w["ev_q_cat"], w["ev_ukv"] = _even_weights(raw["ev_w_in", 0], raw["ev_w_uq", 0], raw["ev_w_ukv", 0])
        w["ev_out"] = raw["ev_w_out", 0]
    if "od_in_pad" not in w and all((n, 0) in raw for n in ("od_w_in", "od_w_out")):
        od_in = raw["od_w_in", 0]
        w["od_in_pad"] = jnp.pad(od_in, ((0, 0), (0, (-od_in.shape[1]) % LANES)))
        w["od_out"] = raw["od_w_out", 0]


def _keys(names, layer):
    return tuple((n, layer) for n in names)


_FFA, _FFB, _PLE = ("ffa_w_gate_up", "ffa_w_down"), ("ffb_w_gate_up", "ffb_w_down"), ("ple_w_gate", "ple_w_proj")
_EV, _OD = ("ev_w_in", "ev_w_uq", "ev_w_ukv", "ev_w_out"), ("od_w_in", "od_w_out")
_GATHER_FIRST = _keys(_FFA[:1], 0)
_GATHER_RIDES = {("ffa", 0): _keys(_FFA[1:] + _EV, 0), ("mix", 0): _keys(_FFB + _PLE, 0) + _keys(_FFA[:1], 1),
                 ("ffb", 0): _keys(_FFA[1:], 1), ("ffa", 1): _keys(_OD, 0), ("mix", 1): _keys(_FFB + _PLE, 1)}
_REDUCE_RIDES = {("mix", 1): _keys(_FFB + _PLE, 1), ("mix", 0): _keys(_FFA, 1) + _keys(_OD, 0) + _keys(_FFB + _PLE, 0),
                 ("ffa", 0): _keys(_EV, 0)}
_REDUCE_OWN = ("ffa", 0)
_SWAP_AHEAD = {("ffb", 1): ("mix", 1)}


def _local_step(x, p, tgt, w, ex=None):
    depth = p.shape[0]

    def gather_behind(host, fn, *args):
        keys = None if ex is None else _GATHER_RIDES.get(host)
        if keys is None:
            return fn(*args, None)[:-1]
        done = []

        def install(rode):
            if not done:
                _install_weights(w, ex.gather_finish(keys, rode, name=f"weight_forward_{host[0]}{host[1]}"))
                done.append(True)

        res = fn(*args, ex.gather_rider(keys), install) if fn is _ffn_fwd else fn(*args, ex.gather_rider(keys))
        install(res[-1])
        return res[:-1]

    h = x
    saved = []
    for i in range(depth):
        sv = {}
        h, sv["ffa"] = gather_behind(("ffa", i), _ffn_fwd, h, w["ffa_norm"][i:i + 1], w["ffa"][i], f"ffa{i}")
        h, sv["mix"] = gather_behind(("mix", i), _even_fwd if i % 2 == 0 else _odd_fwd, h, w, i)
        h, sv["ffb"] = gather_behind(("ffb", i), _ffn_fwd, h, w["ffb_norm"][i:i + 1], w["ffb"][i], f"ffb{i}")
        h_in = h
        h, xn, gate, pp = _ple_fwd(h, w["ple_norm"][i:i + 1], w["ple_gate"][i], p[i], w["ple_proj"][i], name=f"ple{i}")
        sv["ple"] = dict(h_in=h_in, xn=xn, gate=gate, pp=pp)
        saved.append(sv)
    loss_vec, dh, d_final = _final_loss(h, w["final_norm"].reshape(1, -1), tgt, name="final_loss")

    per_layer = [dict() for _ in range(depth)]
    mats = {}
    grads = {}

    pending = {}

    def reduce_behind(host, fn, *args):
        keys = None if ex is None else _REDUCE_RIDES.get(host)
        ahead = None if ex is None else _SWAP_AHEAD.get(host)
        if keys is None and ahead is None:
            return fn(*args, None)[:-1]
        if ahead is not None:
            got, ctxs = {}, []

            def note_wd(d_wd):
                got[f"{host[0]}_w_down", host[1]] = d_wd

            def swap_now(brought, d_wgu):
                got[f"{host[0]}_w_gate_up", host[1]] = d_wgu
                swap, ctx = ex.swap_rider(_REDUCE_RIDES[ahead], {**mats, **got})
                ctxs.append(ctx)
                return swap

            def stash(brought):
                pending[ahead] = ex.after_swap(ctxs[0], brought)

            return fn(*args, None, (note_wd, swap_now, stash))[:-1]
        states = []
        if fn is _even_bwd:
            swap, ctx = ex.swap_rider(keys, mats)

            def then(brought):
                states.append(ex.after_swap(ctx, brought))
                return states[0][0]

            res = fn(*args, (swap, then))
        else:
            states.append(pending.pop(host, None) or ex.reduce_begin(keys, mats, tag=f"{host[0]}{host[1]}"))
            if fn is _ffn_bwd and host == _REDUCE_OWN:
                own = []

                def wd_ready(d_wd):
                    own.append(ex.reduce_begin(_keys(_FFA[1:], 0), {("ffa_w_down", 0): d_wd}, tag="own_wd"))
                    return own[0][0]

                def wgu_ready(brought, d_wgu):
                    ex.reduce_finish(own[0], brought)
                    own.append(ex.reduce_begin(_keys(_FFA[:1], 0), {("ffa_w_gate_up", 0): d_wgu}, tag="own_wgu"))
                    return own[1][0]

                res = fn(*args, states[0][0], (wd_ready, wgu_ready, lambda brought: ex.reduce_finish(own[1], brought)))
            else:
                res = fn(*args, states[0][0])
        ex.reduce_finish(states[0], res[-1])
        return res[:-1]

    for i in reversed(range(depth)):
        sv, gl = saved[i], per_layer[i]
        dz, dpp = _ple_bwd_elem(dh, sv["ple"]["gate"], sv["ple"]["pp"], name=f"ple{i}_bwd")
        mats["ple_w_gate", i] = _mm_tn(sv["ple"]["xn"], dz, name=f"ple{i}_dwg")
        mats["ple_w_proj", i] = _mm_tn(p[i], dpp, name=f"ple{i}_dwp")
        dh, gl["ple_norm"] = _mm_nt_rmsbwd([(dz, w["ple_gate"][i])], sv["ple"]["h_in"], w["ple_norm"][i:i + 1], dh,
                                           name=f"ple{i}_dx")
        dh, gl["ffb_norm"], mats["ffb_w_gate_up", i], mats["ffb_w_down", i] = reduce_behind(
            ("ffb", i), _ffn_bwd, dh, w["ffb_norm"][i:i + 1], w["ffb"][i], sv["ffb"], f"ffb{i}")
        dh, gl["mix_norm"], gm = reduce_behind(("mix", i), _even_bwd if i % 2 == 0 else _odd_bwd, dh, w, sv["mix"], i)
        for n, g in gm.items():
            if n in REPLICATED:
                grads[n] = g
            else:
                mats[n, 0] = g
        dh, gl["ffa_norm"], mats["ffa_w_gate_up", i], mats["ffa_w_down", i] = reduce_behind(
            ("ffa", i), _ffn_bwd, dh, w["ffa_norm"][i:i + 1], w["ffa"][i], sv["ffa"], f"ffa{i}")
    grads["final_norm"] = d_final.reshape(-1)
    for n in ("ffa_norm", "mix_norm", "ffb_norm", "ple_norm"):
        grads[n] = jnp.concatenate([per_layer[i][n] for i in range(depth)], axis=0)
    if ex is None:
        for n, _ in SHARDED:
            grads[n] = [mats[n, i] for i in range(depth) if (n, i) in mats]
    return loss_vec[0, 0], dh, grads


def _cut_mode(local_shape, axis, ncols):
    return "cols" if axis == 2 and ncols % LANES == 0 else "blk"


class _Exchange:
    def __init__(self, wts):
        self.place = jnp.stack([2 * lax.axis_index("x") + lax.axis_index("y"), lax.axis_index("c")]).astype(jnp.int32)
        self.info = {}
        for n, axis in SHARDED:
            wb = wts[n].astype(BF16)
            mode = _cut_mode(wb.shape, axis, wb.shape[2])
            for i in range(wb.shape[0]):
                self.info[n, i] = dict(shard=wb[i], mode=mode, axis=axis)
        self.halves = {}

    def _modes(self, keys):
        return [self.info[k]["mode"] for k in keys]

    def gather_rider(self, keys):
        return _gather_rider([self.info[k]["shard"] for k in keys], self._modes(keys))

    def gather_finish(self, keys, landed, *, name):
        outs = _gather_forward(landed, [self.info[k]["shard"].shape for k in keys], self._modes(keys), name=name)
        got = {}
        for k, dst in zip(keys, outs):
            if self.info[k]["mode"] == "blk":
                dst = dst.reshape(-1, dst.shape[2]) if self.info[k]["axis"] == 1 else jnp.moveaxis(dst, 0, 1).reshape(dst.shape[1], -1)
            got[k] = dst
        return got

    def gather(self, keys, *, name):
        return self.gather_finish(keys, _run_rider(self.gather_rider(keys), name=name), name=name + "_forward")

    def swap_rider(self, keys, mats):
        modes = self._modes(keys)
        arrs = []
        for k in keys:
            g2, (rr, cc) = mats[k], self.info[k]["shard"].shape
            if self.info[k]["mode"] == "blk":
                g2 = g2.reshape(N_CHIPS, rr, cc) if self.info[k]["axis"] == 1 else g2.reshape(rr, N_CHIPS, cc).transpose(1, 0, 2)
            arrs.append(g2)
        return _swap_rider(arrs, modes), (keys, modes, arrs)

    def after_swap(self, ctx, landed):
        keys, modes, arrs = ctx
        parts = []
        for (n, i), m, a, l in zip(keys, modes, arrs, landed):
            pt = _rs_pair_add(_blk_view(a, m), _blk_view(l, m), self.place, name=f"rs_pair_add_{n}{i}")
            parts.append(pt[0] if m == "cols" else pt)
        return _exchange_rider(parts, modes), keys, parts

    def reduce_begin(self, keys, mats, *, tag):
        rider, ctx = self.swap_rider(keys, mats)
        return self.after_swap(ctx, _run_rider(rider, name=f"rs_pair_swap_{tag}"))

    def reduce_finish(self, state, landed):
        _, keys, parts = state
        for (n, i), m, pt, l in zip(keys, self._modes(keys), parts, landed):
            self.halves[n, i] = _rs_chip_sum(pt, l, m, self.place, name=f"rs_chip_sum_{n}{i}")

    def reduce(self, keys, mats, *, tag):
        state = self.reduce_begin(keys, mats, tag=tag)
        self.reduce_finish(state, _run_rider(state[0], name=f"rs_chip_exchange_{tag}"))

    def join(self, wts):
        keys = list(self.info)
        joined = dict(zip(keys, _rs_pair_join([self.halves[k] for k in keys], name="rs_pair_join")))
        return {n: jnp.stack([joined[n, i] for i in range(wts[n].shape[0])]).reshape(wts[n].shape) for n, _ in SHARDED}


def _small_rows(vals):
    rows = []
    for n in REPLICATED:
        v = vals[n].reshape(-1)
        rows.append(jnp.pad(v, (0, (-v.shape[0]) % FLAT_COLS)).reshape(-1, FLAT_COLS))
    out = jnp.concatenate(rows, axis=0)
    return jnp.pad(out, ((0, (-out.shape[0]) % 8), (0, 0)))


def kernel(x, p, ffa_norm, ffa_w_gate_up, ffa_w_down, mix_norm, ffb_norm, ffb_w_gate_up, ffb_w_down, ple_norm, ple_w_gate, ple_w_proj, ev_w_in, ev_sinks, ev_cq_norm, ev_w_uq, ev_ckv_norm, ev_w_ukv, ev_w_out, od_w_in, od_b_f, od_w_out, final_norm, loss_target, m_ffa_norm, m_ffa_w_gate_up, m_ffa_w_down, m_mix_norm, m_ffb_norm, m_ffb_w_gate_up, m_ffb_w_down, m_ple_norm, m_ple_w_gate, m_ple_w_proj, m_ev_w_in, m_ev_sinks, m_ev_cq_norm, m_ev_w_uq, m_ev_ckv_norm, m_ev_w_ukv, m_ev_w_out, m_od_w_in, m_od_b_f, m_od_w_out, m_final_norm, v_ffa_norm, v_ffa_w_gate_up, v_ffa_w_down, v_mix_norm, v_ffb_norm, v_ffb_w_gate_up, v_ffb_w_down, v_ple_norm, v_ple_w_gate, v_ple_w_proj, v_ev_w_in, v_ev_sinks, v_ev_cq_norm, v_ev_w_uq, v_ev_ckv_norm, v_ev_w_ukv, v_ev_w_out, v_od_w_in, v_od_b_f, v_od_w_out, v_final_norm):
    env = dict(locals())
    wts = {n: env[n] for n in WEIGHT_ORDER}
    mom1 = {n: env["m_" + n] for n in WEIGHT_ORDER}
    mom2 = {n: env["v_" + n] for n in WEIGHT_ORDER}
    ex = _Exchange(wts)

    w = {n: wts[n] for n in REPLICATED}
    _install_weights(w, ex.gather(_GATHER_FIRST, name="weight_gather_first"))

    loss_part, grad_x, grads = _local_step(x[0], p[:, 0], loss_target[0], w, ex)
    loss = lax.psum(loss_part, ("x", "y", "c"))
    gout = ex.join(wts)
    small = _allreduce_small(_small_rows(grads), name="small_allreduce")
    r0 = 0
    for n in REPLICATED:
        size = int(np.prod(wts[n].shape))
        nr = -(-size // FLAT_COLS)
        gout[n] = small[r0:r0 + nr].reshape(-1)[:size].reshape(wts[n].shape)
        r0 += nr

    delta, new_m, new_v = {}, {}, {}
    for n in WEIGHT_ORDER:
        delta[n], new_m[n], new_v[n] = _adamw(wts[n], gout[n], mom1[n], mom2[n], name="adamw_" + n)
    return (loss, grad_x[None], *[gout[n] for n in WEIGHT_ORDER], *[delta[n] for n in WEIGHT_ORDER],
            *[new_m[n] for n in WEIGHT_ORDER], *[new_v[n] for n in WEIGHT_ORDER])
```

```python
import functools
import math

import numpy as np
import jax
import jax.numpy as jnp
from jax import lax
from jax.experimental import pallas as pl
from jax.experimental.pallas import tpu as pltpu

F32 = jnp.float32
BF16 = jnp.bfloat16
NT = (((1,), (1,)), ((), ()))
TN = (((0,), (0,)), ((), ()))
MESH = pl.DeviceIdType.MESH

RMS_EPS = 1e-6
FFN_RES_SCALE = 0.5
A_HEADS, A_KV_HEADS, A_HEAD_DIM, WINDOW = 8, 2, 64, 128
B_HEADS, B_Q_LORA, B_KV_LORA, B_NOPE, B_ROPE, B_V = 8, 256, 128, 64, 32, 64
ROPE_THETA = 10000.0
C_HEADS, C_HEAD_DIM = 16, 64
ADAM_LR, ADAM_B1, ADAM_B2, ADAM_EPS, ADAM_WD, ADAM_STEP = 0.001, 0.9, 0.999, 1e-08, 0.01, 10

N_CHIPS = 4
LANES = 128
FLAT_COLS = 1024
MASK_VALUE = -1e30
VMEM_LIMIT = 48 * 2**20

SHARDED = (
    ("ffa_w_gate_up", 2), ("ffa_w_down", 1), ("ffb_w_gate_up", 2), ("ffb_w_down", 1),
    ("ple_w_gate", 1), ("ple_w_proj", 2), ("ev_w_in", 2), ("ev_w_uq", 2), ("ev_w_ukv", 2),
    ("ev_w_out", 1), ("od_w_in", 2), ("od_w_out", 1))
REPLICATED = ("ffa_norm", "mix_norm", "ffb_norm", "ple_norm", "final_norm",
              "ev_sinks", "ev_cq_norm", "ev_ckv_norm", "od_b_f")
WEIGHT_ORDER = ("ffa_norm", "ffa_w_gate_up", "ffa_w_down", "mix_norm", "ffb_norm", "ffb_w_gate_up",
                "ffb_w_down", "ple_norm", "ple_w_gate", "ple_w_proj", "ev_w_in", "ev_sinks",
                "ev_cq_norm", "ev_w_uq", "ev_ckv_norm", "ev_w_ukv", "ev_w_out", "od_w_in", "od_b_f",
                "od_w_out", "final_norm")


def _cp(*sem):
    return pltpu.CompilerParams(dimension_semantics=sem, vmem_limit_bytes=VMEM_LIMIT)


def _sigmoid(z):
    return 1.0 / (1.0 + jnp.exp(-z))


def _rms_stats(xv):
    r = lax.rsqrt(jnp.mean(xv * xv, axis=-1, keepdims=True) + RMS_EPS)
    return r, xv * r


def _rms_bwd(dxn, xv, g):
    r, xhat = _rms_stats(xv)
    u = dxn * g
    dx = r * (u - xhat * jnp.mean(u * xhat, axis=-1, keepdims=True))
    return dx, dxn * xhat


def _col_tile(k_rows, n, budget_bytes=6 * 2**20):
    if k_rows * n * 4 <= budget_bytes or n % LANES:
        return n
    units = n // LANES
    best = LANES
    for d in range(1, units + 1):
        if units % d == 0 and k_rows * d * LANES * 4 <= budget_bytes:
            best = d * LANES
    return best


def _row_tile(rows, cols, target_elems=2**18):
    if rows * cols <= target_elems or rows % 8:
        return rows
    best = 8
    for d in range(8, rows + 1, 8):
        if rows % d == 0 and d * cols <= target_elems:
            best = d
    return best


def _fox_in_fwd(x, g, w, *, nheads, dh, q_ones, k_ones, name, tm=512):
    s, k = x.shape
    n = w.shape[1]
    wd = nheads * dh
    spare = LANES - dh

    def body(x_ref, g_ref, w_ref, q_ref, k_ref, v_ref, vt_ref, f_ref, xn_ref):
        _, xhat = _rms_stats(x_ref[...])
        xn = (xhat * g_ref[...]).astype(BF16)
        xn_ref[...] = xn
        y = jnp.dot(xn, w_ref[...], preferred_element_type=F32)
        f_ref[...] = y[:, 3 * wd:]
        lane = lax.broadcasted_iota(jnp.int32, (tm, spare), 1)

        def fill(cols):
            return functools.reduce(jnp.logical_or, [lane == c for c in cols]).astype(F32)

        q_fill, k_fill = fill(q_ones), fill(k_ones)
        for h in range(nheads):
            q_ref[h] = jnp.concatenate([y[:, h * dh:(h + 1) * dh], q_fill], axis=-1).astype(BF16)
            k_ref[h] = jnp.concatenate([y[:, wd + h * dh:wd + (h + 1) * dh], k_fill], axis=-1).astype(BF16)
            vh = y[:, 2 * wd + h * dh:2 * wd + (h + 1) * dh]
            v_ref[h] = vh.astype(BF16)
            vt_ref[h] = vh.T.astype(BF16)

    wide = pl.BlockSpec((nheads, tm, LANES), lambda i: (0, i, 0))
    return pl.pallas_call(
        body, name=name, grid=(s // tm,),
        in_specs=[pl.BlockSpec((tm, k), lambda i: (i, 0)), pl.BlockSpec((1, k), lambda i: (0, 0)),
                  pl.BlockSpec((k, n), lambda i: (0, 0))],
        out_specs=[wide, wide, pl.BlockSpec((nheads, tm, dh), lambda i: (0, i, 0)),
                   pl.BlockSpec((nheads, dh, tm), lambda i: (0, 0, i)), pl.BlockSpec((tm, LANES), lambda i: (i, 0)),
                   pl.BlockSpec((tm, k), lambda i: (i, 0))],
        out_shape=[jax.ShapeDtypeStruct((nheads, s, LANES), BF16)] * 2
        + [jax.ShapeDtypeStruct((nheads, s, dh), BF16), jax.ShapeDtypeStruct((nheads, dh, s), BF16),
           jax.ShapeDtypeStruct((s, LANES), F32), jax.ShapeDtypeStruct((s, k), BF16)],
        compiler_params=_cp("arbitrary"))(x, g, w)


def _merge_heads(dq, dk, dvt, *, dh, q_col, k_col, name, tm=512):
    nheads, s, _ = dq.shape

    def body(dq_ref, dk_ref, dvt_ref, o_ref, cols_ref):
        pieces = [dq_ref[h][:, :dh] for h in range(nheads)] + [dk_ref[h][:, :dh] for h in range(nheads)]
        pieces += [dvt_ref[h].T for h in range(nheads)]
        o_ref[...] = jnp.concatenate(pieces, axis=-1)
        lane = lax.broadcasted_iota(jnp.int32, (tm, LANES), 1)
        cols = jnp.zeros((tm, LANES), F32)
        for h in range(nheads):
            cols = jnp.where(lane == h, jnp.broadcast_to(dq_ref[h][:, q_col:q_col + 1], (tm, LANES)), cols)
            cols = jnp.where(lane == nheads + h, jnp.broadcast_to(dk_ref[h][:, k_col:k_col + 1], (tm, LANES)), cols)
        cols_ref[...] = cols

    wide = pl.BlockSpec((nheads, tm, LANES), lambda i: (0, i, 0))
    return pl.pallas_call(
        body, name=name, grid=(s // tm,),
        in_specs=[wide, wide, pl.BlockSpec((nheads, dh, tm), lambda i: (0, 0, i))],
        out_specs=[pl.BlockSpec((tm, 3 * nheads * dh), lambda i: (i, 0)), pl.BlockSpec((tm, LANES), lambda i: (i, 0))],
        out_shape=[jax.ShapeDtypeStruct((s, 3 * nheads * dh), F32), jax.ShapeDtypeStruct((s, LANES), F32)],
        compiler_params=_cp("arbitrary"))(dq, dk, dvt)


def _row_call(body, n_rows, ins, outs, *, name, tm=512):
    def spec(a, axis):
        shape = a.shape
        if axis is None:
            return pl.BlockSpec(shape, lambda i: (0,) * len(shape))
        blk = tuple(tm if d == axis else n for d, n in enumerate(shape))
        return pl.BlockSpec(blk, lambda i: tuple(i if d == axis else 0 for d in range(len(shape))))

    return pl.pallas_call(
        body, name=name, grid=(n_rows // tm,), in_specs=[spec(a, ax) for a, ax in ins],
        out_specs=[spec(a, ax) for a, ax in outs], out_shape=[a for a, _ in outs],
        compiler_params=_cp("arbitrary"))(*[a for a, _ in ins])


def _sds(shape, dtype):
    return jax.ShapeDtypeStruct(shape, dtype)


def _ev_in_fwd(x, g, w, *, name):
    s, k = x.shape
    d = A_HEAD_DIM

    def body(x_ref, g_ref, w_ref, q_ref, k_ref, v_ref, vt_ref, cq_ref, ckv_ref, kr_ref, xn_ref):
        _, xhat = _rms_stats(x_ref[...])
        xn = (xhat * g_ref[...]).astype(BF16)
        xn_ref[...] = xn
        y = jnp.dot(xn, w_ref[...], preferred_element_type=F32)
        for h in range(A_HEADS):
            q_ref[h] = y[:, h * d:(h + 1) * d].astype(BF16)
        for h in range(A_KV_HEADS):
            k_ref[h] = y[:, 512 + h * d:512 + (h + 1) * d].astype(BF16)
            vh = y[:, 640 + h * d:640 + (h + 1) * d]
            v_ref[h] = vh.astype(BF16)
            vt_ref[h] = vh.T.astype(BF16)
        cq_ref[...] = y[:, 768:1024]
        ckv_ref[...] = y[:, 1024:1152]
        kr_ref[...] = y[:, 1152:1280]

    return _row_call(
        body, s, [(x, 0), (g, None), (w, None)],
        [(_sds((A_HEADS, s, d), BF16), 1), (_sds((A_KV_HEADS, s, d), BF16), 1), (_sds((A_KV_HEADS, s, d), BF16), 1),
         (_sds((A_KV_HEADS, d, s), BF16), 2), (_sds((s, B_Q_LORA), F32), 0), (_sds((s, B_KV_LORA), F32), 0),
         (_sds((s, LANES), F32), 0), (_sds((s, k), BF16), 0)], name=name)


def _ev_q_fwd(x, g, w, cos, sin, *, name):
    s, k = x.shape
    rot = B_HEADS * B_ROPE

    def body(x_ref, g_ref, w_ref, c_ref, s_ref, q_ref, xn_ref):
        _, xhat = _rms_stats(x_ref[...])
        xn = (xhat * g_ref[...]).astype(BF16)
        xn_ref[...] = xn
        y = jnp.dot(xn, w_ref[...], preferred_element_type=F32)
        ro = y[:, 512:512 + rot] * c_ref[...] + y[:, 512 + rot:] * s_ref[...]
        zero = jnp.zeros((y.shape[0], LANES - B_NOPE - B_ROPE), F32)
        for h in range(B_HEADS):
            q_ref[h] = jnp.concatenate([y[:, h * B_NOPE:(h + 1) * B_NOPE], ro[:, h * B_ROPE:(h + 1) * B_ROPE], zero],
                                       axis=-1).astype(BF16)

    return _row_call(body, s, [(x, 0), (g, None), (w, None), (cos, 0), (sin, 0)],
                     [(_sds((B_HEADS, s, LANES), BF16), 1), (_sds((s, k), BF16), 0)], name=name)


def _ev_kv_fwd(x, g, w, kro, *, name):
    s, k = x.shape
    per = B_NOPE + B_V

    def body(x_ref, g_ref, w_ref, kr_ref, k_ref, v_ref, vt_ref, xn_ref):
        _, xhat = _rms_stats(x_ref[...])
        xn = (xhat * g_ref[...]).astype(BF16)
        xn_ref[...] = xn
        y = jnp.dot(xn, w_ref[...], preferred_element_type=F32)
        kr = kr_ref[...]
        zero = jnp.zeros((y.shape[0], LANES - B_NOPE - B_ROPE), F32)
        for h in range(B_HEADS):
            k_ref[h] = jnp.concatenate([y[:, h * per:h * per + B_NOPE], kr, zero], axis=-1).astype(BF16)
            vh = y[:, h * per + B_NOPE:(h + 1) * per]
            v_ref[h] = vh.astype(BF16)
            vt_ref[h] = vh.T.astype(BF16)

    return _row_call(body, s, [(x, 0), (g, None), (w, None), (kro, 0)],
                     [(_sds((B_HEADS, s, LANES), BF16), 1), (_sds((B_HEADS, s, B_V), BF16), 1),
                      (_sds((B_HEADS, B_V, s), BF16), 2), (_sds((s, k), BF16), 0)], name=name)


def _ev_q_merge(dq, cos, sin, *, name):
    nh, s, _ = dq.shape

    def body(dq_ref, c_ref, s_ref, o_ref):
        dro = jnp.concatenate([dq_ref[h][:, B_NOPE:B_NOPE + B_ROPE] for h in range(nh)], axis=-1)
        o_ref[...] = jnp.concatenate([dq_ref[h][:, :B_NOPE] for h in range(nh)] + [dro * c_ref[...], dro * s_ref[...]], axis=-1)

    return _row_call(body, s, [(dq, 1), (cos, 0), (sin, 0)], [(_sds((s, 2 * nh * B_NOPE), F32), 0)], name=name)[0]


def _ev_kv_merge(dk, dvt, cos, sin, *, name):
    nh, s, _ = dk.shape

    def body(dk_ref, dvt_ref, c_ref, s_ref, o_ref, kr_ref):
        pieces = []
        tot = None
        for h in range(nh):
            pieces += [dk_ref[h][:, :B_NOPE], dvt_ref[h].T]
            rot = dk_ref[h][:, B_NOPE:B_NOPE + B_ROPE]
            tot = rot if tot is None else tot + rot
        o_ref[...] = jnp.concatenate(pieces, axis=-1)
        kr_ref[...] = jnp.concatenate([tot * c_ref[...], tot * s_ref[...], jnp.zeros((tot.shape[0], LANES - 2 * B_ROPE), F32)],
                                      axis=-1)

    return _row_call(body, s, [(dk, 1), (dvt, 2), (cos, 0), (sin, 0)],
                     [(_sds((s, nh * (B_NOPE + B_V)), F32), 0), (_sds((s, LANES), F32), 0)], name=name)


def _ev_in_merge(dq, dk, dvt, dcq, dckv, dkr, *, name):
    s = dcq.shape[0]

    def body(dq_ref, dk_ref, dvt_ref, cq_ref, ckv_ref, kr_ref, o_ref):
        pieces = [dq_ref[h] for h in range(A_HEADS)] + [dk_ref[h] for h in range(A_KV_HEADS)]
        pieces += [dvt_ref[h].T for h in range(A_KV_HEADS)] + [cq_ref[...], ckv_ref[...], kr_ref[...]]
        o_ref[...] = jnp.concatenate(pieces, axis=-1)

    return _row_call(body, s, [(dq, 1), (dk, 1), (dvt, 2), (dcq, 0), (dckv, 0), (dkr, 0)],
                     [(_sds((s, 1280), F32), 0)], name=name)[0]


def _ffn_up(x, g, wgu, *, name, tm=512, rider=None):
    s, k = x.shape
    f = wgu.shape[1] // 2
    tn = _col_tile(k, f)
    nj = f // tn

    def body(x_ref, g_ref, wg_ref, wu_ref, dgate_ref, dup_ref, act_ref, xn_ref, xn_sc):
        @pl.when(pl.program_id(1) == 0)
        def _():
            _, xhat = _rms_stats(x_ref[...])
            xn = (xhat * g_ref[...]).astype(BF16)
            xn_sc[...] = xn
            xn_ref[...] = xn

        xn = xn_sc[...]
        gg = jnp.dot(xn, wg_ref[...], preferred_element_type=F32)
        uu = jnp.dot(xn, wu_ref[...], preferred_element_type=F32)
        sg = _sigmoid(gg)
        silu = gg * sg
        dgate_ref[...] = (uu * (sg * (1.0 + gg * (1.0 - sg)))).astype(BF16)
        dup_ref[...] = silu.astype(BF16)
        act_ref[...] = (silu * uu).astype(BF16)

    tile = pl.BlockSpec((tm, tn), lambda i, j: (i, j))
    return _call_with_rider(
        body, rider, name=name, grid=(s // tm, nj),
        in_specs=[pl.BlockSpec((tm, k), lambda i, j: (i, 0)), pl.BlockSpec((1, k), lambda i, j: (0, 0)),
                  pl.BlockSpec((k, tn), lambda i, j: (0, j)), pl.BlockSpec((k, tn), lambda i, j: (0, j + nj))],
        out_specs=[tile, tile, tile, pl.BlockSpec((tm, k), lambda i, j: (i, 0))],
        out_shape=[jax.ShapeDtypeStruct((s, f), BF16)] * 3 + [jax.ShapeDtypeStruct((s, k), BF16)],
        scratch_shapes=[pltpu.VMEM((tm, k), BF16)],
        compiler_params=_cp("arbitrary", "arbitrary"), args=(x, g, wgu, wgu))


def _mm_res_fwd(a, w, res, *, scale, name, tm=512, a_t=False):
    k, n = w.shape
    s = res.shape[0]

    def body(a_ref, w_ref, r_ref, o_ref):
        prod = (lax.dot_general(a_ref[...], w_ref[...], TN, preferred_element_type=F32) if a_t
                else jnp.dot(a_ref[...], w_ref[...], preferred_element_type=F32))
        o_ref[...] = r_ref[...] + scale * prod

    a_spec = pl.BlockSpec((k, tm), lambda i: (0, i)) if a_t else pl.BlockSpec((tm, k), lambda i: (i, 0))
    return pl.pallas_call(
        body, name=name, grid=(s // tm,),
        in_specs=[a_spec, pl.BlockSpec((k, n), lambda i: (0, 0)),
                  pl.BlockSpec((tm, n), lambda i: (i, 0))],
        out_specs=pl.BlockSpec((tm, n), lambda i: (i, 0)),
        out_shape=jax.ShapeDtypeStruct((s, n), F32),
        compiler_params=_cp("arbitrary"))(a, w, res)


def _ffn_down_bwd(dh, wd, dact_dgate, dact_dup, *, scale, name, tm=512, rider=None):
    s, d = dh.shape
    f = wd.shape[0]
    tn = _col_tile(d, f)

    def body(dh_ref, wd_ref, fg_ref, fu_ref, dg_ref, du_ref):
        dhb = (dh_ref[...] * scale).astype(BF16)
        da = lax.dot_general(dhb, wd_ref[...], NT, preferred_element_type=F32)
        dg_ref[...] = (da * fg_ref[...].astype(F32)).astype(BF16)
        du_ref[...] = (da * fu_ref[...].astype(F32)).astype(BF16)

    tile = pl.BlockSpec((tm, tn), lambda i, j: (i, j))
    return _call_with_rider(
        body, rider, name=name, grid=(s // tm, f // tn),
        in_specs=[pl.BlockSpec((tm, d), lambda i, j: (i, 0)), pl.BlockSpec((tn, d), lambda i, j: (j, 0)), tile, tile],
        out_specs=[tile, tile],
        out_shape=[jax.ShapeDtypeStruct((s, f), BF16)] * 2, scratch_shapes=[],
        compiler_params=_cp("arbitrary", "arbitrary"), args=(dh, wd, dact_dgate, dact_dup))


def _mm_tn(a, bs, *, name, b_scale=1.0, ts=512, rider=None, a_t=False):
    bs = list(bs) if isinstance(bs, (list, tuple)) else [bs]
    k, s = a.shape if a_t else a.shape[::-1]
    n = bs[0].shape[1]
    tn = _col_tile(k, n, 12 * 2**20)
    per = n // tn

    def body(a_ref, *refs):
        b_refs, o_ref = refs[:-1], refs[-1]
        j = pl.program_id(0)

        @pl.when(pl.program_id(1) == 0)
        def _():
            o_ref[...] = jnp.zeros_like(o_ref)

        for m, b_ref in enumerate(b_refs):
            def acc(b_ref=b_ref):
                bv = b_ref[...]
                if b_scale != 1.0:
                    bv = bv * b_scale
                av = a_ref[...].astype(BF16)
                o_ref[...] += (jnp.dot(av, bv.astype(BF16), preferred_element_type=F32) if a_t
                               else lax.dot_general(av, bv.astype(BF16), TN, preferred_element_type=F32))

            if len(b_refs) == 1:
                acc()
            else:
                pl.when(jnp.logical_and(j >= m * per, j < (m + 1) * per))(acc)

    def b_spec(m):
        def idx(j, t):
            mine = jnp.logical_and(j >= m * per, j < (m + 1) * per)
            return (jnp.where(mine, t, 0), jnp.clip(j - m * per, 0, per - 1))
        return pl.BlockSpec((ts, tn), idx)

    (out,), rode = _call_with_rider(
        body, rider, name=name, grid=(per * len(bs), s // ts),
        in_specs=[pl.BlockSpec((k, ts), lambda j, t: (0, t)) if a_t else pl.BlockSpec((ts, k), lambda j, t: (t, 0))]
        + [b_spec(m) for m in range(len(bs))],
        out_specs=[pl.BlockSpec((k, tn), lambda j, t: (0, j))],
        out_shape=[jax.ShapeDtypeStruct((k, n * len(bs)), F32)], scratch_shapes=[],
        compiler_params=_cp("arbitrary", "arbitrary"), args=(a, *bs))
    return out if rider is None else (out, rode)


def _mm_nt_t(dy, w, *, name, tm=512):
    s, n = dy.shape
    k = w.shape[0]

    def body(dy_ref, w_ref, o_ref):
        o_ref[...] = lax.dot_general(w_ref[...], dy_ref[...].astype(BF16), NT, preferred_element_type=F32).astype(BF16)

    return pl.pallas_call(
        body, name=name, grid=(s // tm,),
        in_specs=[pl.BlockSpec((tm, n), lambda i: (i, 0)), pl.BlockSpec((k, n), lambda i: (0, 0))],
        out_specs=pl.BlockSpec((k, tm), lambda i: (0, i)),
        out_shape=jax.ShapeDtypeStruct((k, s), BF16),
        compiler_params=_cp("arbitrary"))(dy, w)


def _mm_nt_rmsbwd(pairs, x, g, dres, *, name, tm=512, rider=None):
    s, k = x.shape
    npairs = len(pairs)
    pairs = [pr if len(pr) == 3 else (pr[0], pr[1], 0) for pr in pairs]

    def body(*refs):
        dy_refs = refs[0:2 * npairs:2]
        w_refs = refs[1:2 * npairs:2]
        rest = refs[2 * npairs:]
        x_ref, g_ref = rest[0], rest[1]
        if dres is None:
            dx_ref, dg_ref = rest[2], rest[3]
        else:
            dres_ref, dx_ref, dg_ref = rest[2], rest[3], rest[4]
        dxn = None
        for dy_ref, w_ref in zip(dy_refs, w_refs):
            t = lax.dot_general(dy_ref[...].astype(BF16), w_ref[...], NT, preferred_element_type=F32)
            dxn = t if dxn is None else dxn + t
        dx, dgrow = _rms_bwd(dxn, x_ref[...], g_ref[...])
        if dres is not None:
            dx = dx + dres_ref[...]
        dx_ref[...] = dx

        @pl.when(pl.program_id(0) == 0)
        def _():
            dg_ref[...] = jnp.zeros_like(dg_ref)

        dg_ref[...] += jnp.sum(dgrow, axis=0, keepdims=True)

    in_specs, args = [], []
    for dy, w, cb in pairs:
        n = dy.shape[1]
        in_specs += [pl.BlockSpec((tm, n), lambda i: (i, 0)),
                     pl.BlockSpec((k, n), lambda i, cb=cb: (0, cb), pipeline_mode=pl.Buffered(1))]
        args += [dy, w]
    row = pl.BlockSpec((tm, k), lambda i: (i, 0))
    vec = pl.BlockSpec((1, k), lambda i: (0, 0))
    in_specs += [row, vec]
    args += [x, g]
    if dres is not None:
        in_specs.append(row)
        args.append(dres)
    (dx, dgain), rode = _call_with_rider(
        body, rider, name=name, grid=(s // tm,), in_specs=in_specs, out_specs=[row, vec],
        out_shape=[jax.ShapeDtypeStruct((s, k), F32), jax.ShapeDtypeStruct((1, k), F32)], scratch_shapes=[],
        compiler_params=_cp("arbitrary"), args=args)
    return (dx, dgain) if rider is None else (dx, dgain, rode)


def _ple_fwd(h, g, wg, p, wp, *, name, tm=512):
    s, d = h.shape
    pd = p.shape[1]

    def body(h_ref, g_ref, wg_ref, p_ref, wp_ref, o_ref, xn_ref, gate_ref, pp_ref):
        hv = h_ref[...]
        _, xhat = _rms_stats(hv)
        xn = (xhat * g_ref[...]).astype(BF16)
        xn_ref[...] = xn
        gate = _sigmoid(jnp.dot(xn, wg_ref[...], preferred_element_type=F32))
        pp = jnp.dot(p_ref[...].astype(BF16), wp_ref[...], preferred_element_type=F32)
        gate_ref[...] = gate.astype(BF16)
        pp_ref[...] = pp.astype(BF16)
        o_ref[...] = hv + gate * pp

    row = pl.BlockSpec((tm, d), lambda i: (i, 0))
    return pl.pallas_call(
        body, name=name, grid=(s // tm,),
        in_specs=[row, pl.BlockSpec((1, d), lambda i: (0, 0)), pl.BlockSpec((d, d), lambda i: (0, 0)),
                  pl.BlockSpec((tm, pd), lambda i: (i, 0)), pl.BlockSpec((pd, d), lambda i: (0, 0))],
        out_specs=[row, row, row, row],
        out_shape=[jax.ShapeDtypeStruct((s, d), F32)] + [jax.ShapeDtypeStruct((s, d), BF16)] * 3,
        compiler_params=_cp("arbitrary"))(h, g, wg, p, wp)


def _ple_bwd_elem(dh, gate, pp, *, name, tm=512):
    s, d = dh.shape

    def body(dh_ref, gate_ref, pp_ref, dz_ref, dpp_ref):
        dhv = dh_ref[...]
        gt = gate_ref[...].astype(F32)
        dz_ref[...] = (dhv * pp_ref[...].astype(F32) * (gt * (1.0 - gt))).astype(BF16)
        dpp_ref[...] = (dhv * gt).astype(BF16)

    row = pl.BlockSpec((tm, d), lambda i: (i, 0))
    return pl.pallas_call(
        body, name=name, grid=(s // tm,), in_specs=[row, row, row], out_specs=[row, row],
        out_shape=[jax.ShapeDtypeStruct((s, d), BF16)] * 2,
        compiler_params=_cp("arbitrary"))(dh, gate, pp)


def _final_loss(h, g, tgt, *, name, tm=512):
    s, d = h.shape

    def body(h_ref, g_ref, t_ref, loss_ref, dh_ref, dg_ref):
        @pl.when(pl.program_id(0) == 0)
        def _():
            loss_ref[...] = jnp.zeros_like(loss_ref)
            dg_ref[...] = jnp.zeros_like(dg_ref)

        hv = h_ref[...]
        gv = g_ref[...]
        _, xhat = _rms_stats(hv)
        err = xhat * gv - t_ref[...]
        per_row = jnp.mean(err * err, axis=-1, keepdims=True)
        loss_ref[...] += 0.5 * jnp.sum(per_row, axis=0, keepdims=True)
        dx, dgrow = _rms_bwd(err * (1.0 / d), hv, gv)
        dh_ref[...] = dx
        dg_ref[...] += jnp.sum(dgrow, axis=0, keepdims=True)

    row = pl.BlockSpec((tm, d), lambda i: (i, 0))
    vec = pl.BlockSpec((1, d), lambda i: (0, 0))
    return pl.pallas_call(
        body, name=name, grid=(s // tm,), in_specs=[row, vec, row],
        out_specs=[pl.BlockSpec((1, LANES), lambda i: (0, 0)), row, vec],
        out_shape=[jax.ShapeDtypeStruct((1, LANES), F32), jax.ShapeDtypeStruct((s, d), F32),
                   jax.ShapeDtypeStruct((1, d), F32)],
        compiler_params=_cp("arbitrary"))(h, g, tgt)


def _rope_fwd(y1, y2, cos, sin, *, name, tm=512):
    s, r = y1.shape

    def body(a_ref, b_ref, c_ref, s_ref, o_ref):
        o_ref[...] = a_ref[...] * c_ref[...] + b_ref[...] * s_ref[...]

    row = pl.BlockSpec((tm, r), lambda i: (i, 0))
    return pl.pallas_call(
        body, name=name, grid=(s // tm,), in_specs=[row] * 4, out_specs=row,
        out_shape=jax.ShapeDtypeStruct((s, r), F32), compiler_params=_cp("arbitrary"))(y1, y2, cos, sin)


def _split3(v):
    h1 = v.astype(BF16)
    r1 = v - h1.astype(F32)
    h2 = r1.astype(BF16)
    h3 = (r1 - h2.astype(F32)).astype(BF16)
    return h1, h2, h3


def _tri(tb, upper):
    r = lax.broadcasted_iota(jnp.int32, (tb, tb), 0)
    c = lax.broadcasted_iota(jnp.int32, (tb, tb), 1)
    return jnp.where((r <= c) if upper else (r >= c), 1.0, 0.0).astype(BF16)


def _fox_gate_fwd(ft, bf, *, out_scale, name, tb=512):
    nh, s = ft.shape

    def body(f_ref, b_ref, o_ref, carry):
        @pl.when(pl.program_id(0) == 0)
        def _():
            carry[...] = jnp.zeros_like(carry)

        z = f_ref[...] + b_ref[...]
        lf = jnp.minimum(z, 0.0) - jnp.log(1.0 + jnp.exp(-jnp.abs(z)))
        tri = _tri(tb, True)
        cs = sum(jnp.dot(t, tri, preferred_element_type=F32) for t in _split3(lf)) + carry[...]
        for n, term in enumerate(_split3(cs * out_scale)):
            o_ref[n] = term
        carry[...] += jnp.sum(lf, axis=-1, keepdims=True)

    return pl.pallas_call(
        body, name=name, grid=(s // tb,),
        in_specs=[pl.BlockSpec((nh, tb), lambda t: (0, t)), pl.BlockSpec((nh, 1), lambda t: (0, 0))],
        out_specs=pl.BlockSpec((3, nh, tb), lambda t: (0, 0, t)),
        out_shape=jax.ShapeDtypeStruct((3, nh, s), BF16),
        scratch_shapes=[pltpu.VMEM((nh, 1), F32)], compiler_params=_cp("arbitrary"))(ft, bf)


def _fox_gate_bwd(drow, dcol, ft, bf, *, inv_scale, name, tb=512):
    nh, s = ft.shape
    nb = s // tb

    def body(dr_ref, dc_ref, f_ref, b_ref, df_ref, db_ref, carry):
        @pl.when(pl.program_id(0) == 0)
        def _():
            carry[...] = jnp.zeros_like(carry)
            db_ref[...] = jnp.zeros_like(db_ref)

        dc = (dr_ref[...] - dc_ref[...]) * inv_scale
        tri = _tri(tb, False)
        suf = sum(jnp.dot(t, tri, preferred_element_type=F32) for t in _split3(dc)) + carry[...]
        z = f_ref[...] + b_ref[...]
        dz = suf * (1.0 / (1.0 + jnp.exp(z)))
        df_ref[...] = dz
        db_ref[...] += jnp.sum(dz, axis=-1, keepdims=True)
        carry[...] += jnp.sum(dc, axis=-1, keepdims=True)

    rev = pl.BlockSpec((nh, tb), lambda t: (0, nb - 1 - t))
    one = pl.BlockSpec((nh, 1), lambda t: (0, 0))
    return pl.pallas_call(
        body, name=name, grid=(nb,), in_specs=[rev, rev, rev, one], out_specs=[rev, one],
        out_shape=[jax.ShapeDtypeStruct((nh, s), F32), jax.ShapeDtypeStruct((nh, 1), F32)],
        scratch_shapes=[pltpu.VMEM((nh, 1), F32)], compiler_params=_cp("arbitrary"))(drow, dcol, ft, bf)


def _tri_fwd(t, nq):
    i = sum((t >= (r * (r + 1)) // 2).astype(jnp.int32) for r in range(1, nq))
    return i, t - (i * (i + 1)) // 2


def _tri_bwd(t, nq):
    j = sum((t >= r * nq - (r * (r - 1)) // 2).astype(jnp.int32) for r in range(1, nq))
    return j, j + t - (j * nq - (j * (j - 1)) // 2)


def _scores_t(k, q, *, scale, diag):
    s = lax.dot_general(k, q, NT, preferred_element_type=F32) * scale
    if diag:
        r = lax.broadcasted_iota(jnp.int32, s.shape, 0)
        c = lax.broadcasted_iota(jnp.int32, s.shape, 1)
        s = jnp.where(r <= c, s, MASK_VALUE)
    return s


def _causal_fwd_t(q, k, vt, *, scale, name, tq, hb=2, rider=None):
    nh, s, dq = q.shape
    dv = vt.shape[1]
    nq = s // tq
    nsteps = (nq * (nq + 1)) // 2

    def body(q_ref, k_ref, vt_ref, o_ref, lse_ref, m_sc, l_sc, acc_sc):
        i, j = _tri_fwd(pl.program_id(1), nq)

        @pl.when(j == 0)
        def _():
            m_sc[...] = jnp.full_like(m_sc, MASK_VALUE)
            l_sc[...] = jnp.zeros_like(l_sc)
            acc_sc[...] = jnp.zeros_like(acc_sc)

        def step(diag):
            for u in range(hb):
                sc = _scores_t(k_ref[u], q_ref[u], scale=scale, diag=diag)
                m_prev = m_sc[u]
                m_new = jnp.maximum(m_prev, jnp.max(sc, axis=0, keepdims=True))
                alpha = jnp.exp(m_prev - m_new)
                pr = jnp.exp(sc - m_new)
                l_new = alpha * l_sc[u] + jnp.sum(pr, axis=0, keepdims=True)
                acc = alpha * acc_sc[u] + jnp.dot(vt_ref[u], pr.astype(BF16), preferred_element_type=F32)
                if diag:
                    o_ref[u] = (acc / l_new).astype(BF16)
                    lse_ref[u] = m_new + jnp.log(l_new)
                else:
                    m_sc[u], l_sc[u], acc_sc[u] = m_new, l_new, acc

        pl.when(j < i)(functools.partial(step, False))
        pl.when(j == i)(functools.partial(step, True))

    def qi(t):
        return _tri_fwd(t, nq)[0]

    def kj(t):
        return _tri_fwd(t, nq)[1]

    return _call_with_rider(
        body, rider, name=name, grid=(nh // hb, nsteps),
        in_specs=[pl.BlockSpec((hb, tq, dq), lambda hp, t: (hp, qi(t), 0)),
                  pl.BlockSpec((hb, tq, dq), lambda hp, t: (hp, kj(t), 0)),
                  pl.BlockSpec((hb, dv, tq), lambda hp, t: (hp, 0, kj(t)))],
        out_specs=[pl.BlockSpec((hb, dv, tq), lambda hp, t: (hp, 0, qi(t))),
                   pl.BlockSpec((hb, 1, tq), lambda hp, t: (hp, 0, qi(t)))],
        out_shape=[jax.ShapeDtypeStruct((nh, dv, s), BF16), jax.ShapeDtypeStruct((nh, 1, s), F32)],
        scratch_shapes=[pltpu.VMEM((hb, 1, tq), F32), pltpu.VMEM((hb, 1, tq), F32), pltpu.VMEM((hb, dv, tq), F32)],
        compiler_params=_cp("arbitrary", "arbitrary"), args=(q, k, vt))


def _attn_delta(ot, dot_, *, name, hb):
    nh, dv, s = ot.shape

    def body(o_ref, do_ref, d_ref):
        d_ref[...] = jnp.sum(do_ref[...].astype(F32) * o_ref[...].astype(F32), axis=1, keepdims=True)

    blk = pl.BlockSpec((hb, dv, s), lambda h: (h, 0, 0))
    return pl.pallas_call(
        body, name=name, grid=(nh // hb,), in_specs=[blk, blk],
        out_specs=pl.BlockSpec((hb, 1, s), lambda h: (h, 0, 0)),
        out_shape=jax.ShapeDtypeStruct((nh, 1, s), F32), compiler_params=_cp("arbitrary"))(ot, dot_)


def _causal_bwd_t(q, k, v, ot, dot_, lse, *, scale, name, tq, hb=2, rider=None):
    nh, s, dq = q.shape
    dv = v.shape[-1]
    nq = s // tq
    nsteps = (nq * (nq + 1)) // 2
    delta = _attn_delta(ot, dot_, name=f"{name}_delta", hb=hb)

    def body(q_ref, k_ref, v_ref, dot_ref, lse_ref, delta_ref, dq_ref, dk_ref, dvt_ref):
        t = pl.program_id(1)
        j, i = _tri_bwd(t, nq)

        @pl.when(t == 0)
        def _():
            dq_ref[...] = jnp.zeros_like(dq_ref)

        def step(diag):
            rows = pl.ds(pl.multiple_of(i * tq, tq), tq)
            for u in range(hb):
                qv, kv, dov = q_ref[u], k_ref[u], dot_ref[u]
                pr = jnp.exp(_scores_t(kv, qv, scale=scale, diag=diag) - lse_ref[u])
                dp = jnp.dot(v_ref[u], dov, preferred_element_type=F32)
                dsb = ((pr * (dp - delta_ref[u])) * scale).astype(BF16)
                d_v = lax.dot_general(dov, pr.astype(BF16), NT, preferred_element_type=F32)
                d_k = jnp.dot(dsb, qv, preferred_element_type=F32)
                if diag:
                    dvt_ref[u], dk_ref[u] = d_v, d_k
                else:
                    dvt_ref[u] += d_v
                    dk_ref[u] += d_k
                dq_ref[u, rows, :] += lax.dot_general(dsb, kv, TN, preferred_element_type=F32)

        pl.when(i > j)(functools.partial(step, False))
        pl.when(i == j)(functools.partial(step, True))

    def qi(t):
        return _tri_bwd(t, nq)[1]

    def kj(t):
        return _tri_bwd(t, nq)[0]

    rows_q = pl.BlockSpec((hb, tq, dq), lambda hp, t: (hp, qi(t), 0))
    rows_k = pl.BlockSpec((hb, tq, dq), lambda hp, t: (hp, kj(t), 0))
    lanes_q = pl.BlockSpec((hb, dv, tq), lambda hp, t: (hp, 0, qi(t)))
    stat_q = pl.BlockSpec((hb, 1, tq), lambda hp, t: (hp, 0, qi(t)))
    return _call_with_rider(
        body, rider, name=name, grid=(nh // hb, nsteps),
        in_specs=[rows_q, rows_k, pl.BlockSpec((hb, tq, dv), lambda hp, t: (hp, kj(t), 0)), lanes_q, stat_q, stat_q],
        out_specs=[pl.BlockSpec((hb, s, dq), lambda hp, t: (hp, 0, 0)), rows_k,
                   pl.BlockSpec((hb, dv, tq), lambda hp, t: (hp, 0, kj(t)))],
        out_shape=[jax.ShapeDtypeStruct((nh, s, dq), F32), jax.ShapeDtypeStruct((nh, s, dq), F32),
                   jax.ShapeDtypeStruct((nh, dv, s), F32)],
        scratch_shapes=[], compiler_params=_cp("arbitrary", "arbitrary"), args=(q, k, v, dot_, lse, delta))


def _swa_scores_t(k, q, dist, ok, *, scale, slope):
    s = lax.dot_general(k, q, NT, preferred_element_type=F32) * scale - slope * dist.astype(F32)
    return jnp.where(ok, s, MASK_VALUE)


def _swa_geometry(tb, w, has_other):
    r = lax.broadcasted_iota(jnp.int32, (tb, tb), 0)
    c = lax.broadcasted_iota(jnp.int32, (tb, tb), 1)
    d_same = c - r
    ok_same = jnp.logical_and(d_same >= 0, d_same < w)

    def other(ncols):
        rr = lax.broadcasted_iota(jnp.int32, (w, ncols), 0)
        cc = lax.broadcasted_iota(jnp.int32, (w, ncols), 1)
        dd = cc + w - rr
        return dd, jnp.logical_and(dd < w, has_other)

    return (d_same, ok_same), other


def _swa_fwd_t(q, k, vt, slopes_sinks, *, scale, window, name, tb=256):
    nh, s, d = q.shape
    nkv = k.shape[0]
    grp = nh // nkv
    w = window
    per = tb // w
    assert tb % w == 0

    def body(q_ref, kc_ref, kp_ref, vc_ref, vp_ref, ss_ref, o_ref, lse_ref):
        kvh, i = pl.program_id(0), pl.program_id(1)
        (d_c, ok_c), other = _swa_geometry(tb, w, i > 0)
        d_p, ok_p = other(tb)
        for g in range(grp):
            h = kvh * grp + g
            slope, sink = ss_ref[0, h], ss_ref[1, h]
            qg = q_ref[g]
            s_c = _swa_scores_t(kc_ref[...], qg, d_c, ok_c, scale=scale, slope=slope)
            s_p = _swa_scores_t(kp_ref[...], qg, d_p, ok_p, scale=scale, slope=slope)
            m = jnp.maximum(jnp.maximum(jnp.max(s_c, axis=0, keepdims=True), jnp.max(s_p, axis=0, keepdims=True)), sink)
            p_c, p_p = jnp.exp(s_c - m), jnp.exp(s_p - m)
            l = jnp.sum(p_c, axis=0, keepdims=True) + jnp.sum(p_p, axis=0, keepdims=True) + jnp.exp(sink - m)
            acc = (jnp.dot(vc_ref[...], p_c.astype(BF16), preferred_element_type=F32)
                   + jnp.dot(vp_ref[...], p_p.astype(BF16), preferred_element_type=F32))
            o_ref[g] = (acc / l).astype(BF16)
            lse_ref[g] = m + jnp.log(l)

    def prev(i):
        return jnp.maximum(i * per - 1, 0)

    return pl.pallas_call(
        body, name=name, grid=(nkv, s // tb),
        in_specs=[pl.BlockSpec((grp, tb, d), lambda kh, i: (kh, i, 0)),
                  pl.BlockSpec((None, tb, d), lambda kh, i: (kh, i, 0)),
                  pl.BlockSpec((None, w, d), lambda kh, i: (kh, prev(i), 0)),
                  pl.BlockSpec((None, d, tb), lambda kh, i: (kh, 0, i)),
                  pl.BlockSpec((None, d, w), lambda kh, i: (kh, 0, prev(i))),
                  pl.BlockSpec(memory_space=pltpu.SMEM)],
        out_specs=[pl.BlockSpec((grp, d, tb), lambda kh, i: (kh, 0, i)), pl.BlockSpec((grp, 1, tb), lambda kh, i: (kh, 0, i))],
        out_shape=[jax.ShapeDtypeStruct((nh, d, s), BF16), jax.ShapeDtypeStruct((nh, 1, s), F32)],
        compiler_params=_cp("arbitrary", "arbitrary"))(q, k, k, vt, vt, slopes_sinks)


def _swa_bwd_t(q, k, v, ot, dot_, lse, slopes_sinks, *, scale, window, name, tb=256, rider=None):
    nh, s, d = q.shape
    nkv = k.shape[0]
    grp = nh // nkv
    w = window
    per = tb // w
    nb = s // tb

    def body(qc_ref, qn_ref, kc_ref, kp_ref, vc_ref, vp_ref, oc_ref, on_ref, doc_ref, don_ref, lc_ref, ln_ref, ss_ref,
             dq_ref, dk_ref, dvt_ref, dsink_ref):
        kvh, i = pl.program_id(0), pl.program_id(1)

        @pl.when(i == 0)
        def _():
            dsink_ref[...] = jnp.zeros_like(dsink_ref)

        (d_c, ok_c), other = _swa_geometry(tb, w, i > 0)
        d_p, ok_p = other(tb)
        d_n, ok_n = _swa_geometry(tb, w, i < nb - 1)[1](w)
        kc, kp, vc, vp = kc_ref[...], kp_ref[...], vc_ref[...], vp_ref[...]
        k_last, v_last = kc[tb - w:, :], vc[tb - w:, :]
        dk_acc = jnp.zeros((tb, d), F32)
        dv_acc = jnp.zeros((d, tb), F32)
        dk_tail = jnp.zeros((w, d), F32)
        dv_tail = jnp.zeros((d, w), F32)
        for g in range(grp):
            h = kvh * grp + g
            slope, sink = ss_ref[0, h], ss_ref[1, h]
            qg, dog, lse_c = qc_ref[g], doc_ref[g], lc_ref[g]
            delta = jnp.sum(dog.astype(F32) * oc_ref[g].astype(F32), axis=0, keepdims=True)
            p_c = jnp.exp(_swa_scores_t(kc, qg, d_c, ok_c, scale=scale, slope=slope) - lse_c)
            p_p = jnp.exp(_swa_scores_t(kp, qg, d_p, ok_p, scale=scale, slope=slope) - lse_c)
            ds_c = ((p_c * (jnp.dot(vc, dog, preferred_element_type=F32) - delta)) * scale).astype(BF16)
            ds_p = ((p_p * (jnp.dot(vp, dog, preferred_element_type=F32) - delta)) * scale).astype(BF16)
            dq_ref[g] = (lax.dot_general(ds_c, kc, TN, preferred_element_type=F32)
                         + lax.dot_general(ds_p, kp, TN, preferred_element_type=F32))
            dk_acc += jnp.dot(ds_c, qg, preferred_element_type=F32)
            dv_acc += lax.dot_general(dog, p_c.astype(BF16), NT, preferred_element_type=F32)
            dsink_ref[g] -= jnp.broadcast_to(jnp.sum(jnp.exp(sink - lse_c) * delta, axis=1, keepdims=True), (1, LANES))
            qn, don = qn_ref[g], don_ref[g]
            delta_n = jnp.sum(don.astype(F32) * on_ref[g].astype(F32), axis=0, keepdims=True)
            p_n = jnp.exp(_swa_scores_t(k_last, qn, d_n, ok_n, scale=scale, slope=slope) - ln_ref[g])
            ds_n = ((p_n * (jnp.dot(v_last, don, preferred_element_type=F32) - delta_n)) * scale).astype(BF16)
            dk_tail += jnp.dot(ds_n, qn, preferred_element_type=F32)
            dv_tail += lax.dot_general(don, p_n.astype(BF16), NT, preferred_element_type=F32)
        dk_ref[...] = dk_acc
        dvt_ref[...] = dv_acc
        dk_ref[tb - w:, :] += dk_tail
        dvt_ref[:, tb - w:] += dv_tail

    def prev(i):
        return jnp.maximum(i * per - 1, 0)

    def nxt(i):
        return jnp.minimum((i + 1) * per, s // w - 1)

    return _call_with_rider(
        body, rider, name=name, grid=(nkv, nb), scratch_shapes=[],
        args=(q, q, k, k, v, v, ot, ot, dot_, dot_, lse, lse, slopes_sinks),
        in_specs=[pl.BlockSpec((grp, tb, d), lambda kh, i: (kh, i, 0)),
                  pl.BlockSpec((grp, w, d), lambda kh, i: (kh, nxt(i), 0)),
                  pl.BlockSpec((None, tb, d), lambda kh, i: (kh, i, 0)),
                  pl.BlockSpec((None, w, d), lambda kh, i: (kh, prev(i), 0)),
                  pl.BlockSpec((None, tb, d), lambda kh, i: (kh, i, 0)),
                  pl.BlockSpec((None, w, d), lambda kh, i: (kh, prev(i), 0)),
                  pl.BlockSpec((grp, d, tb), lambda kh, i: (kh, 0, i)),
                  pl.BlockSpec((grp, d, w), lambda kh, i: (kh, 0, nxt(i))),
                  pl.BlockSpec((grp, d, tb), lambda kh, i: (kh, 0, i)),
                  pl.BlockSpec((grp, d, w), lambda kh, i: (kh, 0, nxt(i))),
                  pl.BlockSpec((grp, 1, tb), lambda kh, i: (kh, 0, i)),
                  pl.BlockSpec((grp, 1, w), lambda kh, i: (kh, 0, nxt(i))),
                  pl.BlockSpec(memory_space=pltpu.SMEM)],
        out_specs=[pl.BlockSpec((grp, tb, d), lambda kh, i: (kh, i, 0)),
                   pl.BlockSpec((None, tb, d), lambda kh, i: (kh, i, 0)),
                   pl.BlockSpec((None, d, tb), lambda kh, i: (kh, 0, i)),
                   pl.BlockSpec((None, grp, 1, LANES), lambda kh, i: (kh, 0, 0, 0))],
        out_shape=[jax.ShapeDtypeStruct((nh, s, d), F32), jax.ShapeDtypeStruct((nkv, s, d), F32),
                   jax.ShapeDtypeStruct((nkv, d, s), F32), jax.ShapeDtypeStruct((nkv, grp, 1, LANES), F32)],
        compiler_params=_cp("arbitrary", "arbitrary"))


def _adamw(w, g, m, v, *, name):
    shape = w.shape
    cols = shape[-1]
    rows = int(np.prod(shape[:-1])) if len(shape) > 1 else 1
    tr = _row_tile(rows, cols)
    c1 = 1.0 - ADAM_B1 ** ADAM_STEP
    c2 = 1.0 - ADAM_B2 ** ADAM_STEP

    def body(w_ref, g_ref, m_ref, v_ref, d_ref, mo_ref, vo_ref):
        gv = g_ref[...]
        mn = ADAM_B1 * m_ref[...] + (1.0 - ADAM_B1) * gv
        vn = ADAM_B2 * v_ref[...] + (1.0 - ADAM_B2) * (gv * gv)
        mo_ref[...] = mn
        vo_ref[...] = vn
        d_ref[...] = -ADAM_LR * ((mn / c1) / (jnp.sqrt(vn / c2) + ADAM_EPS) + ADAM_WD * w_ref[...])

    blk = pl.BlockSpec((tr, cols), lambda i: (i, 0))
    outs = pl.pallas_call(
        body, name=name, grid=(rows // tr,), in_specs=[blk] * 4, out_specs=[blk] * 3,
        out_shape=[jax.ShapeDtypeStruct((rows, cols), F32)] * 3,
        compiler_params=_cp("arbitrary"))(*[a.reshape(rows, cols) for a in (w, g, m, v)])
    return tuple(a.reshape(shape) for a in outs)


def _hbm_spec():
    return pl.BlockSpec(memory_space=pl.ANY)


def _mesh_place():
    x, y, c = lax.axis_index("x"), lax.axis_index("y"), lax.axis_index("c")
    return x, y, c, [(1 - x, y), (x, 1 - y), (1 - x, 1 - y)]


def _half_rows(c, rows, align):
    return pl.ds(pl.multiple_of(c * (rows // 2), align), rows // 2)


def _part(ref, mode, k, n, rows=None):
    if mode == "cols":
        cols = pl.ds(pl.multiple_of(k * n, LANES), n)
        return ref.at[:, cols] if rows is None else ref.at[rows, cols]
    return ref.at[k] if rows is None else ref.at[k, rows, :]


class _Rider:
    def __init__(self, inputs, out_shape, n_sems, start, finish):
        self.inputs, self.out_shape, self.n_sems, self.start, self.finish = inputs, out_shape, n_sems, start, finish


def _call_with_rider(body, rider, *, name, grid, in_specs, out_specs, out_shape, scratch_shapes, compiler_params, args):
    if rider is None:
        outs = pl.pallas_call(body, name=name, grid=grid, in_specs=in_specs, out_specs=out_specs, out_shape=out_shape,
                              scratch_shapes=scratch_shapes, compiler_params=compiler_params)(*args)
        return outs, []
    n_in, n_out, n_sc = len(in_specs), len(out_specs), len(scratch_shapes)
    n_rin, n_rout = len(rider.inputs), len(rider.out_shape)

    def wrapped(*refs):
        pos = 0
        groups = []
        for n in (n_in, n_rin, n_out, n_rout, n_sc, 2):
            groups.append(refs[pos:pos + n])
            pos += n
        ins, rins, outs, routs, scratch, sems = groups
        ids = [pl.program_id(a) for a in range(len(grid))]
        first = functools.reduce(jnp.logical_and, [i == 0 for i in ids])
        last = functools.reduce(jnp.logical_and, [i == g - 1 for i, g in zip(ids, grid)])
        pl.when(first)(lambda: rider.start(rins, routs, *sems))
        body(*ins, *outs, *scratch)
        pl.when(last)(lambda: rider.finish(rins, routs, *sems))

    outs = pl.pallas_call(
        wrapped, name=name, grid=grid, in_specs=list(in_specs) + [_hbm_spec()] * n_rin,
        out_specs=list(out_specs) + [_hbm_spec()] * n_rout, out_shape=list(out_shape) + list(rider.out_shape),
        scratch_shapes=list(scratch_shapes) + [pltpu.SemaphoreType.DMA((rider.n_sems,))] * 2,
        compiler_params=compiler_params)(*args, *rider.inputs)
    return outs[:n_out], outs[n_out:]


def _run_rider(rider, *, name):
    n_rin = len(rider.inputs)

    def body(*refs):
        rins, routs, sems = refs[:n_rin], refs[n_rin:-2], refs[-2:]
        rider.start(rins, routs, *sems)
        rider.finish(rins, routs, *sems)

    return pl.pallas_call(
        body, name=name, in_specs=[_hbm_spec()] * n_rin, out_specs=[_hbm_spec()] * len(rider.out_shape),
        out_shape=rider.out_shape, scratch_shapes=[pltpu.SemaphoreType.DMA((rider.n_sems,))] * 2)(*rider.inputs)


def _gather_rider(shards, modes):
    n_arr = len(shards)
    out_shape = [jax.ShapeDtypeStruct((s.shape[0], N_CHIPS * s.shape[1]) if m == "cols" else (N_CHIPS,) + s.shape, s.dtype)
                 for s, m in zip(shards, modes)]
    per = 4

    def copies(srcs, dsts, send_sems, recv_sems):
        x, y, c, chips = _mesh_place()
        me = 2 * x + y
        sends, waits = [], []
        for i in range(n_arr):
            r, n = shards[i].shape
            rows = _half_rows(c, r, 16)

            def copy(slot, src, dst, to, i=i):
                return pltpu.make_async_remote_copy(src_ref=src, dst_ref=dst, send_sem=send_sems.at[i * per + slot],
                                                    recv_sem=recv_sems.at[i * per + slot], device_id=to, device_id_type=MESH)

            own = _part(dsts[i], modes[i], me, n)
            sends.append(copy(0, srcs[i], own, (x, y, 1 - c)))
            waits.append(copy(0, own, own, (x, y, 1 - c)))
            for j, (px, py) in enumerate(chips):
                sends.append(copy(1 + j, srcs[i].at[rows], _part(dsts[i], modes[i], me, n, rows), (px, py, c)))
                theirs = _part(dsts[i], modes[i], 2 * px + py, n, rows)
                waits.append(copy(1 + j, theirs, theirs, (px, py, c)))
        return sends, waits

    def start(*refs):
        for cp in copies(*refs)[0]:
            cp.start()

    def finish(*refs):
        sends, waits = copies(*refs)
        for cp in waits:
            cp.wait_recv()
        for cp in sends:
            cp.wait_send()

    return _Rider(list(shards), out_shape, per * n_arr, start, finish)


def _gather_forward(dsts, shard_shapes, modes, *, name):
    n_arr = len(dsts)

    def body(*refs):
        outs = refs[n_arr:2 * n_arr]
        send_sems, recv_sems = refs[2 * n_arr:]
        x, y, c, chips = _mesh_place()
        cps = []
        for i in range(n_arr):
            r, n = shard_shapes[i]
            for j, (px, py) in enumerate(chips):
                def view(hc, i=i, px=px, py=py, r=r, n=n):
                    return _part(outs[i], modes[i], 2 * px + py, n, _half_rows(hc, r, 16))

                def copy(ref, i=i, j=j):
                    return pltpu.make_async_remote_copy(src_ref=ref, dst_ref=ref, send_sem=send_sems.at[3 * i + j],
                                                        recv_sem=recv_sems.at[3 * i + j], device_id=(x, y, 1 - c), device_id_type=MESH)

                cps.append((copy(view(c)), copy(view(1 - c))))
        for send, _ in cps:
            send.start()
        for send, theirs in cps:
            theirs.wait_recv()
            send.wait_send()

    return pl.pallas_call(
        body, name=name, in_specs=[_hbm_spec()] * n_arr, out_specs=[_hbm_spec()] * n_arr,
        out_shape=[jax.ShapeDtypeStruct(d.shape, d.dtype) for d in dsts],
        input_output_aliases={i: i for i in range(n_arr)},
        scratch_shapes=[pltpu.SemaphoreType.DMA((3 * n_arr,)), pltpu.SemaphoreType.DMA((3 * n_arr,))])(*dsts)


def _blk_view(a, mode):
    return a[None] if mode == "cols" else a


def _swap_rider(arrs, modes):
    n_arr = len(arrs)
    out_shape = [jax.ShapeDtypeStruct((a.shape[0] // 2, a.shape[1]) if m == "cols" else (a.shape[0], a.shape[1] // 2, a.shape[2]), a.dtype)
                 for a, m in zip(arrs, modes)]

    def copies(srcs, dsts, send_sems, recv_sems):
        x, y, c, _ = _mesh_place()
        cps = []
        for i in range(n_arr):
            if modes[i] == "cols":
                src = srcs[i].at[_half_rows(1 - c, arrs[i].shape[0], 8)]
            else:
                src = srcs[i].at[:, _half_rows(1 - c, arrs[i].shape[1], 8), :]
            cps.append(pltpu.make_async_remote_copy(src_ref=src, dst_ref=dsts[i], send_sem=send_sems.at[i],
                                                    recv_sem=recv_sems.at[i], device_id=(x, y, 1 - c), device_id_type=MESH))
        return cps

    def start(*refs):
        for cp in copies(*refs):
            cp.start()

    def finish(*refs):
        for cp in copies(*refs):
            cp.wait()

    return _Rider(list(arrs), out_shape, n_arr, start, finish)


def _rs_pair_add(arr, landed, place, *, name):
    nb, r, c = arr.shape
    rh = r // 2
    tr = _row_tile(rh, c)
    nt = rh // tr

    def body(p_ref, a_ref, l_ref, o_ref):
        o_ref[...] = (a_ref[...] + l_ref[...]).astype(BF16)

    grid_spec = pltpu.PrefetchScalarGridSpec(
        num_scalar_prefetch=1, grid=(nb, nt),
        in_specs=[pl.BlockSpec((None, tr, c), lambda b, t, p_ref: (b, p_ref[1] * nt + t, 0)),
                  pl.BlockSpec((None, tr, c), lambda b, t, p_ref: (b, t, 0))],
        out_specs=pl.BlockSpec((None, tr, c), lambda b, t, p_ref: (b, t, 0)))
    return pl.pallas_call(
        body, name=name, grid_spec=grid_spec, out_shape=jax.ShapeDtypeStruct((nb, rh, c), BF16),
        compiler_params=_cp("arbitrary", "arbitrary"))(place, arr, landed)


def _exchange_rider(parts, modes):
    n_arr = len(parts)
    out_shape = []
    for a, m in zip(parts, modes):
        shp = (a.shape[0], a.shape[1] // N_CHIPS) if m == "cols" else a.shape[1:]
        out_shape.append(jax.ShapeDtypeStruct((3,) + shp, a.dtype))

    def copies(srcs, dsts, send_sems, recv_sems):
        x, y, c, chips = _mesh_place()
        cps = []
        for i in range(n_arr):
            n = out_shape[i].shape[-1]
            for j, (px, py) in enumerate(chips):
                cps.append(pltpu.make_async_remote_copy(
                    src_ref=_part(srcs[i], modes[i], 2 * px + py, n), dst_ref=dsts[i].at[j],
                    send_sem=send_sems.at[3 * i + j], recv_sem=recv_sems.at[3 * i + j],
                    device_id=(px, py, c), device_id_type=MESH))
        return cps

    def start(*refs):
        for cp in copies(*refs):
            cp.start()

    def finish(*refs):
        for cp in copies(*refs):
            cp.wait()

    return _Rider(list(parts), out_shape, 3 * n_arr, start, finish)


def _rs_chip_sum(part, landed, mode, place, *, name):
    _, rh, n = landed.shape
    tr = _row_tile(rh, n)
    nt = rh // tr

    def body(p_ref, a_ref, l_ref, o_ref):
        o_ref[...] = ((a_ref[...].astype(F32) + l_ref[0].astype(F32)) + l_ref[1].astype(F32)) + l_ref[2].astype(F32)

    if mode == "cols":
        own = pl.BlockSpec((tr, n), lambda t, p_ref: (t, p_ref[0]))
    else:
        own = pl.BlockSpec((None, tr, n), lambda t, p_ref: (p_ref[0], t, 0))
    grid_spec = pltpu.PrefetchScalarGridSpec(
        num_scalar_prefetch=1, grid=(nt,),
        in_specs=[own, pl.BlockSpec((3, tr, n), lambda t, p_ref: (0, t, 0))],
        out_specs=pl.BlockSpec((tr, n), lambda t, p_ref: (p_ref[1] * nt + t, 0)))
    return pl.pallas_call(
        body, name=name, grid_spec=grid_spec, out_shape=jax.ShapeDtypeStruct((2 * rh, n), F32),
        compiler_params=_cp("arbitrary"))(place, part, landed)


def _rs_pair_join(halves, *, name):
    n_arr = len(halves)

    def body(*refs):
        outs = refs[n_arr:2 * n_arr]
        send_sems, recv_sems = refs[2 * n_arr:]
        x, y, c, _ = _mesh_place()
        cps = []
        for i in range(n_arr):
            rows = _half_rows(c, halves[i].shape[0], 8)
            cps.append(pltpu.make_async_remote_copy(src_ref=outs[i].at[rows], dst_ref=outs[i].at[rows], send_sem=send_sems.at[i],
                                                    recv_sem=recv_sems.at[i], device_id=(x, y, 1 - c), device_id_type=MESH))
        for cp in cps:
            cp.start()
        for i, cp in enumerate(cps):
            cp.wait_send()
            theirs = outs[i].at[_half_rows(1 - c, halves[i].shape[0], 8)]
            pltpu.make_async_remote_copy(src_ref=theirs, dst_ref=theirs, send_sem=send_sems.at[i], recv_sem=recv_sems.at[i],
                                         device_id=(x, y, 1 - c), device_id_type=MESH).wait_recv()

    return pl.pallas_call(
        body, name=name, in_specs=[_hbm_spec()] * n_arr, out_specs=[_hbm_spec()] * n_arr,
        out_shape=[jax.ShapeDtypeStruct(h.shape, h.dtype) for h in halves],
        input_output_aliases={i: i for i in range(n_arr)},
        scratch_shapes=[pltpu.SemaphoreType.DMA((n_arr,)), pltpu.SemaphoreType.DMA((n_arr,))])(*halves)


def _allreduce_small(v, *, name):
    r, c = v.shape

    def body(v_ref, o_ref, gath, send_sems, recv_sems):
        x, y, cc, _ = _mesh_place()
        me = 4 * x + 2 * y + cc
        gath[me] = v_ref[...]
        cps = []
        for rel in range(1, 8):
            px = 1 - x if rel & 4 else x
            py = 1 - y if rel & 2 else y
            pc = 1 - cc if rel & 1 else cc

            def copy(slot, px=px, py=py, pc=pc, rel=rel):
                return pltpu.make_async_remote_copy(
                    src_ref=v_ref, dst_ref=gath.at[slot], send_sem=send_sems.at[rel - 1],
                    recv_sem=recv_sems.at[rel - 1], device_id=(px, py, pc), device_id_type=MESH)

            cps.append((copy(me), copy(4 * px + 2 * py + pc)))
        for send, _ in cps:
            send.start()
        for send, theirs in cps:
            theirs.wait_recv()
            send.wait_send()
        tot = gath[0]
        for d in range(1, 8):
            tot = tot + gath[d]
        o_ref[...] = tot

    vm = pl.BlockSpec(memory_space=pltpu.VMEM)
    return pl.pallas_call(
        body, name=name, in_specs=[vm], out_specs=vm, out_shape=jax.ShapeDtypeStruct((r, c), F32),
        scratch_shapes=[pltpu.VMEM((8, r, c), F32), pltpu.SemaphoreType.DMA((7,)), pltpu.SemaphoreType.DMA((7,))])(v)


def _rope_tables(s, reps):
    half = B_ROPE // 2
    inv = ROPE_THETA ** (-jnp.arange(0, B_ROPE, 2, dtype=F32) / B_ROPE)
    ang = jnp.arange(s, dtype=F32)[:, None] * inv[None, :]
    return jnp.tile(jnp.cos(ang), (1, reps)), jnp.tile(jnp.sin(ang), (1, reps))


def _alibi_slopes():
    return 2.0 ** (-8.0 * jnp.arange(1, A_HEADS + 1, dtype=F32) / A_HEADS)


def _ffn_fwd(h, norm, wts, tag, rider=None, on_rode=None):
    (dact_dgate, dact_dup, act, xn), rode = _ffn_up(h, norm, wts["wgu"], name=f"{tag}_up", rider=rider)
    if on_rode is not None:
        on_rode(rode)
    out = _mm_res_fwd(act, wts["wd"], h, scale=FFN_RES_SCALE, name=f"{tag}_down")
    return out, dict(h_in=h, dact_dgate=dact_dgate, dact_dup=dact_dup, act=act, xn=xn), rode


def _ffn_bwd(dh, norm, wts, sv, tag, rider=None, own=None):
    (dgate, dup), rode = _ffn_down_bwd(dh, wts["wd"], sv["dact_dgate"], sv["dact_dup"], scale=FFN_RES_SCALE,
                                      name=f"{tag}_down_bwd", rider=rider)
    d_wd = _mm_tn(sv["act"], dh, b_scale=FFN_RES_SCALE, name=f"{tag}_dwd")
    pairs = [(dgate, wts["wgu"], 0), (dup, wts["wgu"], 1)]
    if own is None:
        d_wgu = _mm_tn(sv["xn"], [dgate, dup], name=f"{tag}_dwgu")
        dh_in, dnorm = _mm_nt_rmsbwd(pairs, sv["h_in"], norm, dh, name=f"{tag}_dx")
    else:
        wd_ready, wgu_ready, done = own
        first = wd_ready(d_wd)
        res = _mm_tn(sv["xn"], [dgate, dup], name=f"{tag}_dwgu", rider=first)
        d_wgu, brought = (res, []) if first is None else res
        second = wgu_ready(brought, d_wgu)
        res = _mm_nt_rmsbwd(pairs, sv["h_in"], norm, dh, name=f"{tag}_dx", rider=second)
        dh_in, dnorm, brought = (*res, []) if second is None else res
        done(brought)
    return dh_in, dnorm, d_wgu, d_wd, rode


def _even_weights(w_in, w_uq, w_ukv):
    half = B_ROPE // 2
    base = w_in.shape[1]
    kr1, kr2 = w_in[:, base - B_ROPE:base - half], w_in[:, base - half:]
    w_in_cat = jnp.concatenate([w_in, -kr2, kr1, jnp.zeros((w_in.shape[0], 64), w_in.dtype)], axis=1)
    u3 = w_uq.reshape(w_uq.shape[0], B_HEADS, B_NOPE + B_ROPE)
    nope = u3[:, :, :B_NOPE].reshape(w_uq.shape[0], -1)
    rot = u3[:, :, B_NOPE:].reshape(w_uq.shape[0], -1)
    swapped = jnp.concatenate([-u3[:, :, B_NOPE + half:], u3[:, :, B_NOPE:B_NOPE + half]], axis=-1).reshape(w_uq.shape[0], -1)
    return w_in_cat, jnp.concatenate([nope, rot, swapped], axis=1), w_ukv


def _even_fwd(h, w, i, rider=None):
    s = h.shape[0]
    qa, ka, va, vat, c_q, c_kv, kr_blk, xn = _ev_in_fwd(h, w["mix_norm"][i:i + 1], w["ev_in_cat"], name="ev_in")
    cos32, sin32 = _rope_tables(s, 2)
    kro = _rope_fwd(kr_blk[:, :B_ROPE], kr_blk[:, B_ROPE:2 * B_ROPE], cos32, sin32, name="ev_k_rope")
    ss = jnp.stack([_alibi_slopes(), w["ev_sinks"].reshape(-1)])
    oa, lse_a = _swa_fwd_t(qa, ka, vat, ss, scale=A_HEAD_DIM ** -0.5, window=WINDOW, name="swa_fwd")
    cos256, sin256 = _rope_tables(s, 2 * B_HEADS)
    qb, xn_q = _ev_q_fwd(c_q, w["ev_cq_norm"], w["ev_q_cat"], cos256, sin256, name="ev_q_up")
    kb, vb, vbt, xn_kv = _ev_kv_fwd(c_kv, w["ev_ckv_norm"], w["ev_ukv"], kro, name="ev_kv_up")
    (ob, lse_b), rode = _causal_fwd_t(qb, kb, vbt, scale=(B_NOPE + B_ROPE) ** -0.5, name="mla_fwd", tq=512, hb=8, rider=rider)
    attn = jnp.concatenate([oa.reshape(-1, s), ob.reshape(-1, s)], axis=0)
    out = _mm_res_fwd(attn, w["ev_out"], h, scale=1.0, name="ev_out", a_t=True)
    sv = dict(h_in=h, xn=xn, c_q=c_q, c_kv=c_kv, xn_q=xn_q, xn_kv=xn_kv, qa=qa, ka=ka, va=va, oa=oa, lse_a=lse_a,
              ss=ss, qb=qb, kb=kb, vb=vb, ob=ob, lse_b=lse_b, attn=attn, cos32=cos32, sin32=sin32,
              cos256=cos256, sin256=sin256)
    return out, sv, rode


def _even_bwd(dh, w, sv, i, rider=None):
    s = dh.shape[0]
    half = B_ROPE // 2
    g = {}
    dattn = _mm_nt_t(dh, w["ev_out"], name="ev_out_dx")
    g["ev_w_out"] = _mm_tn(sv["attn"], dh, name="ev_out_dw", a_t=True)
    doa = dattn[:A_HEADS * A_HEAD_DIM].reshape(A_HEADS, A_HEAD_DIM, s)
    dob = dattn[A_HEADS * A_HEAD_DIM:].reshape(B_HEADS, B_V, s)
    first, then = rider if isinstance(rider, tuple) else (None, None)
    (dqa, dka, dva, dsink), brought = _swa_bwd_t(sv["qa"], sv["ka"], sv["va"], sv["oa"], doa, sv["lse_a"], sv["ss"],
                                                 scale=A_HEAD_DIM ** -0.5, window=WINDOW, name="swa_bwd", rider=first)
    if then is not None:
        rider = then(brought)
    g["ev_sinks"] = dsink[:, :, 0, 0].reshape(1, A_HEADS)
    (dqb, dkb, dvb), rode = _causal_bwd_t(sv["qb"], sv["kb"], sv["vb"], sv["ob"], dob, sv["lse_b"],
                                          scale=(B_NOPE + B_ROPE) ** -0.5, name="mla_bwd", tq=512, hb=4, rider=rider)
    dyq = _ev_q_merge(dqb, sv["cos256"], sv["sin256"], name="ev_q_merge")
    dwq = _mm_tn(sv["xn_q"], dyq, name="ev_q_up_dw")
    dcq, g["ev_cq_norm"] = _mm_nt_rmsbwd([(dyq, w["ev_q_cat"])], sv["c_q"], w["ev_cq_norm"], None, name="ev_q_up_dx")
    kq = sv["c_q"].shape[1]
    d_nope = dwq[:, :512].reshape(kq, B_HEADS, B_NOPE)
    d_rot = dwq[:, 512:768].reshape(kq, B_HEADS, B_ROPE)
    d_swp = dwq[:, 768:].reshape(kq, B_HEADS, B_ROPE)
    g["ev_w_uq"] = jnp.concatenate([d_nope, d_rot[:, :, :half] + d_swp[:, :, half:], d_rot[:, :, half:] - d_swp[:, :, :half]],
                                   axis=-1).reshape(kq, -1)
    dykv, dkr = _ev_kv_merge(dkb, dvb, sv["cos32"], sv["sin32"], name="ev_kv_merge")
    g["ev_w_ukv"] = _mm_tn(sv["xn_kv"], dykv, name="ev_kv_up_dw")
    dckv, g["ev_ckv_norm"] = _mm_nt_rmsbwd([(dykv, w["ev_ukv"])], sv["c_kv"], w["ev_ckv_norm"], None, name="ev_kv_up_dx")
    dycat = _ev_in_merge(dqa, dka, dva, dcq, dckv, dkr, name="ev_in_merge")
    dwin = _mm_tn(sv["xn"], dycat, name="ev_in_dw")
    base = 1184
    g["ev_w_in"] = jnp.concatenate([dwin[:, :base - B_ROPE],
                                    dwin[:, base - B_ROPE:base - half] + dwin[:, base + half:base + B_ROPE],
                                    dwin[:, base - half:base] - dwin[:, base:base + half]], axis=-1)
    dh_in, dnorm = _mm_nt_rmsbwd([(dycat, w["ev_in_cat"])], sv["h_in"], w["mix_norm"][i:i + 1], dh, name="ev_in_dx")
    return dh_in, dnorm, g, rode


def _odd_fwd(h, w, i, rider=None):
    s = h.shape[0]
    wd = C_HEADS * C_HEAD_DIM
    q, k, v, vt, y_f, xn = _fox_in_fwd(h, w["mix_norm"][i:i + 1], w["od_in_pad"], nheads=C_HEADS, dh=C_HEAD_DIM,
                                       q_ones=(0, 2, 3, 4), k_ones=(1,), name="od_in")
    scale = C_HEAD_DIM ** -0.5
    ft = y_f[:, :C_HEADS].T
    bf = w["od_b_f"].reshape(C_HEADS, 1)
    cb3 = _fox_gate_fwd(ft, bf, out_scale=-1.0 / scale, name="fox_gate_fwd")
    k = k + jnp.pad(cb3.transpose(1, 2, 0), ((0, 0), (0, 0), (C_HEAD_DIM + 2, LANES - C_HEAD_DIM - 5)))
    (o, lse), rode = _causal_fwd_t(q, k, vt, scale=scale, name="fox_fwd", tq=512, hb=16, rider=rider)
    attn = o.reshape(-1, s)
    out = _mm_res_fwd(attn, w["od_out"], h, scale=1.0, name="od_out", a_t=True)
    return out, dict(h_in=h, xn=xn, q=q, k=k, v=v, o=o, lse=lse, ft=ft, bf=bf, attn=attn), rode


def _odd_bwd(dh, w, sv, i, rider=None):
    s = dh.shape[0]
    g = {}
    dattn = _mm_nt_t(dh, w["od_out"], name="od_out_dx")
    g["od_w_out"] = _mm_tn(sv["attn"], dh, name="od_out_dw", a_t=True)
    do = dattn.reshape(C_HEADS, C_HEAD_DIM, s)
    scale = C_HEAD_DIM ** -0.5
    (dq, dk, dv), rode = _causal_bwd_t(sv["q"], sv["k"], sv["v"], sv["o"], do, sv["lse"], scale=scale, name="fox_bwd",
                                       tq=512, hb=4, rider=rider)
    dqkv, sums = _merge_heads(dq, dk, dv, dh=C_HEAD_DIM, q_col=C_HEAD_DIM + 1, k_col=C_HEAD_DIM, name="fox_merge")
    dft, dbf = _fox_gate_bwd(sums[:, :C_HEADS].T, sums[:, C_HEADS:2 * C_HEADS].T, sv["ft"], sv["bf"],
                             inv_scale=1.0 / scale, name="fox_gate_bwd")
    g["od_b_f"] = dbf.reshape(1, C_HEADS)
    wd = C_HEADS * C_HEAD_DIM
    df = jnp.pad(dft.T, ((0, 0), (0, LANES - C_HEADS)))
    g["od_w_in"] = jnp.concatenate([_mm_tn(sv["xn"], dqkv, name="od_in_dw"),
                                    _mm_tn(sv["xn"], df, name="od_in_dwf")[:, :C_HEADS]], axis=-1)
    dh_in, dnorm = _mm_nt_rmsbwd([(dqkv, w["od_in_pad"], 0), (df, w["od_in_pad"], 3 * wd // LANES)],
                                 sv["h_in"], w["mix_norm"][i:i + 1], dh, name="od_in_dx")
    return dh_in, dnorm, g, rode


def _kernel_weights(full, replicated):
    w = dict(replicated)
    _install_weights(w, {(n, i): a for n, per_layer in full.items() for i, a in enumerate(per_layer)})
    return w


def _install_weights(w, got):
    raw = w.setdefault("raw", {})
    raw.update(got)
    for (n, i), a in got.items():
        if n in ("ffa_w_gate_up", "ffa_w_down", "ffb_w_gate_up", "ffb_w_down"):
            w.setdefault(n[:3], {}).setdefault(i, {})["wgu" if n.endswith("gate_up") else "wd"] = a
        elif n in ("ple_w_gate", "ple_w_proj"):
            w.setdefault("ple_gate" if n.endswith("gate") else "ple_proj", {})[i] = a
    if "ev_in_cat" not in w and all((n, 0) in raw for n in ("ev_w_in", "ev_w_uq", "ev_w_ukv", "ev_w_out")):
        w["ev_in_cat"], w["ev_q_cat"], w["ev_ukv"] = _even_weights(raw["ev_w_in", 0], raw["ev_w_uq", 0], raw["ev_w_ukv", 0])
        w["ev_out"] = raw["ev_w_out", 0]
    if "od_in_pad" not in w and all((n, 0) in raw for n in ("od_w_in", "od_w_out")):
        od_in = raw["od_w_in", 0]
        w["od_in_pad"] = jnp.pad(od_in, ((0, 0), (0, (-od_in.shape[1]) % LANES)))
        w["od_out"] = raw["od_w_out", 0]


def _keys(names, layer):
    return tuple((n, layer) for n in names)


_FFA, _FFB, _PLE = ("ffa_w_gate_up", "ffa_w_down"), ("ffb_w_gate_up", "ffb_w_down"), ("ple_w_gate", "ple_w_proj")
_EV, _OD = ("ev_w_in", "ev_w_uq", "ev_w_ukv", "ev_w_out"), ("od_w_in", "od_w_out")
_GATHER_FIRST = _keys(_FFA[:1], 0)
_GATHER_RIDES = {("ffa", 0): _keys(_FFA[1:] + _EV, 0), ("mix", 0): _keys(_FFB + _PLE, 0) + _keys(_FFA[:1], 1),
                 ("ffb", 0): _keys(_FFA[1:], 1), ("ffa", 1): _keys(_OD, 0), ("mix", 1): _keys(_FFB + _PLE, 1)}
_REDUCE_RIDES = {("mix", 1): _keys(_FFB + _PLE, 1), ("mix", 0): _keys(_FFA, 1) + _keys(_OD, 0) + _keys(_FFB + _PLE, 0),
                 ("ffa", 0): _keys(_EV, 0)}
_REDUCE_OWN = ("ffa", 0)
_SWAP_AHEAD = {("ffb", 1): ("mix", 1)}


def _local_step(x, p, tgt, w, ex=None):
    depth = p.shape[0]

    def gather_behind(host, fn, *args):
        keys = None if ex is None else _GATHER_RIDES.get(host)
        if keys is None:
            return fn(*args, None)[:-1]
        done = []

        def install(rode):
            if not done:
                _install_weights(w, ex.gather_finish(keys, rode, name=f"weight_forward_{host[0]}{host[1]}"))
                done.append(True)

        res = fn(*args, ex.gather_rider(keys), install) if fn is _ffn_fwd else fn(*args, ex.gather_rider(keys))
        install(res[-1])
        return res[:-1]

    h = x
    saved = []
    for i in range(depth):
        sv = {}
        h, sv["ffa"] = gather_behind(("ffa", i), _ffn_fwd, h, w["ffa_norm"][i:i + 1], w["ffa"][i], f"ffa{i}")
        h, sv["mix"] = gather_behind(("mix", i), _even_fwd if i % 2 == 0 else _odd_fwd, h, w, i)
        h, sv["ffb"] = gather_behind(("ffb", i), _ffn_fwd, h, w["ffb_norm"][i:i + 1], w["ffb"][i], f"ffb{i}")
        h_in = h
        h, xn, gate, pp = _ple_fwd(h, w["ple_norm"][i:i + 1], w["ple_gate"][i], p[i], w["ple_proj"][i], name=f"ple{i}")
        sv["ple"] = dict(h_in=h_in, xn=xn, gate=gate, pp=pp)
        saved.append(sv)
    loss_vec, dh, d_final = _final_loss(h, w["final_norm"].reshape(1, -1), tgt, name="final_loss")

    per_layer = [dict() for _ in range(depth)]
    mats = {}
    grads = {}

    pending = {}

    def reduce_behind(host, fn, *args):
        keys = None if ex is None else _REDUCE_RIDES.get(host)
        ahead = None if ex is None else _SWAP_AHEAD.get(host)
        if keys is None and ahead is None:
            return fn(*args, None)[:-1]
        if ahead is not None:
            got, ctxs = {}, []

            def note_wd(d_wd):
                got[f"{host[0]}_w_down", host[1]] = d_wd

            def swap_now(brought, d_wgu):
                got[f"{host[0]}_w_gate_up", host[1]] = d_wgu
                swap, ctx = ex.swap_rider(_REDUCE_RIDES[ahead], {**mats, **got})
                ctxs.append(ctx)
                return swap

            def stash(brought):
                pending[ahead] = ex.after_swap(ctxs[0], brought)

            return fn(*args, None, (note_wd, swap_now, stash))[:-1]
        states = []
        if fn is _even_bwd:
            swap, ctx = ex.swap_rider(keys, mats)

            def then(brought):
                states.append(ex.after_swap(ctx, brought))
                return states[0][0]

            res = fn(*args, (swap, then))
        else:
            states.append(pending.pop(host, None) or ex.reduce_begin(keys, mats, tag=f"{host[0]}{host[1]}"))
            if fn is _ffn_bwd and host == _REDUCE_OWN:
                own = []

                def wd_ready(d_wd):
                    own.append(ex.reduce_begin(_keys(_FFA[1:], 0), {("ffa_w_down", 0): d_wd}, tag="own_wd"))
                    return own[0][0]

                def wgu_ready(brought, d_wgu):
                    ex.reduce_finish(own[0], brought)
                    own.append(ex.reduce_begin(_keys(_FFA[:1], 0), {("ffa_w_gate_up", 0): d_wgu}, tag="own_wgu"))
                    return own[1][0]

                res = fn(*args, states[0][0], (wd_ready, wgu_ready, lambda brought: ex.reduce_finish(own[1], brought)))
            else:
                res = fn(*args, states[0][0])
        ex.reduce_finish(states[0], res[-1])
        return res[:-1]

    for i in reversed(range(depth)):
        sv, gl = saved[i], per_layer[i]
        dz, dpp = _ple_bwd_elem(dh, sv["ple"]["gate"], sv["ple"]["pp"], name=f"ple{i}_bwd")
        mats["ple_w_gate", i] = _mm_tn(sv["ple"]["xn"], dz, name=f"ple{i}_dwg")
        mats["ple_w_proj", i] = _mm_tn(p[i], dpp, name=f"ple{i}_dwp")
        dh, gl["ple_norm"] = _mm_nt_rmsbwd([(dz, w["ple_gate"][i])], sv["ple"]["h_in"], w["ple_norm"][i:i + 1], dh,
                                           name=f"ple{i}_dx")
        dh, gl["ffb_norm"], mats["ffb_w_gate_up", i], mats["ffb_w_down", i] = reduce_behind(
            ("ffb", i), _ffn_bwd, dh, w["ffb_norm"][i:i + 1], w["ffb"][i], sv["ffb"], f"ffb{i}")
        dh, gl["mix_norm"], gm = reduce_behind(("mix", i), _even_bwd if i % 2 == 0 else _odd_bwd, dh, w, sv["mix"], i)
        for n, g in gm.items():
            if n in REPLICATED:
                grads[n] = g
            else:
                mats[n, 0] = g
        dh, gl["ffa_norm"], mats["ffa_w_gate_up", i], mats["ffa_w_down", i] = reduce_behind(
            ("ffa", i), _ffn_bwd, dh, w["ffa_norm"][i:i + 1], w["ffa"][i], sv["ffa"], f"ffa{i}")
    grads["final_norm"] = d_final.reshape(-1)
    for n in ("ffa_norm", "mix_norm", "ffb_norm", "ple_norm"):
        grads[n] = jnp.concatenate([per_layer[i][n] for i in range(depth)], axis=0)
    if ex is None:
        for n, _ in SHARDED:
            grads[n] = [mats[n, i] for i in range(depth) if (n, i) in mats]
    return loss_vec[0, 0], dh, grads


def _cut_mode(local_shape, axis, ncols):
    return "cols" if axis == 2 and ncols % LANES == 0 else "blk"


class _Exchange:
    def __init__(self, wts):
        self.place = jnp.stack([2 * lax.axis_index("x") + lax.axis_index("y"), lax.axis_index("c")]).astype(jnp.int32)
        self.info = {}
        for n, axis in SHARDED:
            wb = wts[n].astype(BF16)
            mode = _cut_mode(wb.shape, axis, wb.shape[2])
            for i in range(wb.shape[0]):
                self.info[n, i] = dict(shard=wb[i], mode=mode, axis=axis)
        self.halves = {}

    def _modes(self, keys):
        return [self.info[k]["mode"] for k in keys]

    def gather_rider(self, keys):
        return _gather_rider([self.info[k]["shard"] for k in keys], self._modes(keys))

    def gather_finish(self, keys, landed, *, name):
        outs = _gather_forward(landed, [self.info[k]["shard"].shape for k in keys], self._modes(keys), name=name)
        got = {}
        for k, dst in zip(keys, outs):
            if self.info[k]["mode"] == "blk":
                dst = dst.reshape(-1, dst.shape[2]) if self.info[k]["axis"] == 1 else jnp.moveaxis(dst, 0, 1).reshape(dst.shape[1], -1)
            got[k] = dst
        return got

    def gather(self, keys, *, name):
        return self.gather_finish(keys, _run_rider(self.gather_rider(keys), name=name), name=name + "_forward")

    def swap_rider(self, keys, mats):
        modes = self._modes(keys)
        arrs = []
        for k in keys:
            g2, (rr, cc) = mats[k], self.info[k]["shard"].shape
            if self.info[k]["mode"] == "blk":
                g2 = g2.reshape(N_CHIPS, rr, cc) if self.info[k]["axis"] == 1 else g2.reshape(rr, N_CHIPS, cc).transpose(1, 0, 2)
            arrs.append(g2)
        return _swap_rider(arrs, modes), (keys, modes, arrs)

    def after_swap(self, ctx, landed):
        keys, modes, arrs = ctx
        parts = []
        for (n, i), m, a, l in zip(keys, modes, arrs, landed):
            pt = _rs_pair_add(_blk_view(a, m), _blk_view(l, m), self.place, name=f"rs_pair_add_{n}{i}")
            parts.append(pt[0] if m == "cols" else pt)
        return _exchange_rider(parts, modes), keys, parts

    def reduce_begin(self, keys, mats, *, tag):
        rider, ctx = self.swap_rider(keys, mats)
        return self.after_swap(ctx, _run_rider(rider, name=f"rs_pair_swap_{tag}"))

    def reduce_finish(self, state, landed):
        _, keys, parts = state
        for (n, i), m, pt, l in zip(keys, self._modes(keys), parts, landed):
            self.halves[n, i] = _rs_chip_sum(pt, l, m, self.place, name=f"rs_chip_sum_{n}{i}")

    def reduce(self, keys, mats, *, tag):
        state = self.reduce_begin(keys, mats, tag=tag)
        self.reduce_finish(state, _run_rider(state[0], name=f"rs_chip_exchange_{tag}"))

    def join(self, wts):
        keys = list(self.info)
        joined = dict(zip(keys, _rs_pair_join([self.halves[k] for k in keys], name="rs_pair_join")))
        return {n: jnp.stack([joined[n, i] for i in range(wts[n].shape[0])]).reshape(wts[n].shape) for n, _ in SHARDED}


def _small_rows(vals):
    rows = []
    for n in REPLICATED:
        v = vals[n].reshape(-1)
        rows.append(jnp.pad(v, (0, (-v.shape[0]) % FLAT_COLS)).reshape(-1, FLAT_COLS))
    out = jnp.concatenate(rows, axis=0)
    return jnp.pad(out, ((0, (-out.shape[0]) % 8), (0, 0)))


def kernel(x, p, ffa_norm, ffa_w_gate_up, ffa_w_down, mix_norm, ffb_norm, ffb_w_gate_up, ffb_w_down, ple_norm, ple_w_gate, ple_w_proj, ev_w_in, ev_sinks, ev_cq_norm, ev_w_uq, ev_ckv_norm, ev_w_ukv, ev_w_out, od_w_in, od_b_f, od_w_out, final_norm, loss_target, m_ffa_norm, m_ffa_w_gate_up, m_ffa_w_down, m_mix_norm, m_ffb_norm, m_ffb_w_gate_up, m_ffb_w_down, m_ple_norm, m_ple_w_gate, m_ple_w_proj, m_ev_w_in, m_ev_sinks, m_ev_cq_norm, m_ev_w_uq, m_ev_ckv_norm, m_ev_w_ukv, m_ev_w_out, m_od_w_in, m_od_b_f, m_od_w_out, m_final_norm, v_ffa_norm, v_ffa_w_gate_up, v_ffa_w_down, v_mix_norm, v_ffb_norm, v_ffb_w_gate_up, v_ffb_w_down, v_ple_norm, v_ple_w_gate, v_ple_w_proj, v_ev_w_in, v_ev_sinks, v_ev_cq_norm, v_ev_w_uq, v_ev_ckv_norm, v_ev_w_ukv, v_ev_w_out, v_od_w_in, v_od_b_f, v_od_w_out, v_final_norm):
    env = dict(locals())
    wts = {n: env[n] for n in WEIGHT_ORDER}
    mom1 = {n: env["m_" + n] for n in WEIGHT_ORDER}
    mom2 = {n: env["v_" + n] for n in WEIGHT_ORDER}
    ex = _Exchange(wts)

    w = {n: wts[n] for n in REPLICATED}
    _install_weights(w, ex.gather(_GATHER_FIRST, name="weight_gather_first"))

    loss_part, grad_x, grads = _local_step(x[0], p[:, 0], loss_target[0], w, ex)
    loss = lax.psum(loss_part, ("x", "y", "c"))
    gout = ex.join(wts)
    small = _allreduce_small(_small_rows(grads), name="small_allreduce")
    r0 = 0
    for n in REPLICATED:
        size = int(np.prod(wts[n].shape))
        nr = -(-size // FLAT_COLS)
        gout[n] = small[r0:r0 + nr].reshape(-1)[:size].reshape(wts[n].shape)
        r0 += nr

    delta, new_m, new_v = {}, {}, {}
    for n in WEIGHT_ORDER:
        delta[n], new_m[n], new_v[n] = _adamw(wts[n], gout[n], mom1[n], mom2[n], name="adamw_" + n)
    return (loss, grad_x[None], *[gout[n] for n in WEIGHT_ORDER], *[delta[n] for n in WEIGHT_ORDER],
            *[new_m[n] for n in WEIGHT_ORDER], *[new_v[n] for n in WEIGHT_ORDER])
```

```python
import functools
import math

import numpy as np
import jax
import jax.numpy as jnp
from jax import lax
from jax.experimental import pallas as pl
from jax.experimental.pallas import tpu as pltpu

F32 = jnp.float32
BF16 = jnp.bfloat16
NT = (((1,), (1,)), ((), ()))
TN = (((0,), (0,)), ((), ()))
MESH = pl.DeviceIdType.MESH

RMS_EPS = 1e-6
FFN_RES_SCALE = 0.5
A_HEADS, A_KV_HEADS, A_HEAD_DIM, WINDOW = 8, 2, 64, 128
B_HEADS, B_Q_LORA, B_KV_LORA, B_NOPE, B_ROPE, B_V = 8, 256, 128, 64, 32, 64
ROPE_THETA = 10000.0
C_HEADS, C_HEAD_DIM = 16, 64
ADAM_LR, ADAM_B1, ADAM_B2, ADAM_EPS, ADAM_WD, ADAM_STEP = 0.001, 0.9, 0.999, 1e-08, 0.01, 10

N_CHIPS = 4
LANES = 128
FLAT_COLS = 1024
MASK_VALUE = -1e30
VMEM_LIMIT = 48 * 2**20

SHARDED = (
    ("ffa_w_gate_up", 2), ("ffa_w_down", 1), ("ffb_w_gate_up", 2), ("ffb_w_down", 1),
    ("ple_w_gate", 1), ("ple_w_proj", 2), ("ev_w_in", 2), ("ev_w_uq", 2), ("ev_w_ukv", 2),
    ("ev_w_out", 1), ("od_w_in", 2), ("od_w_out", 1))
REPLICATED = ("ffa_norm", "mix_norm", "ffb_norm", "ple_norm", "final_norm",
              "ev_sinks", "ev_cq_norm", "ev_ckv_norm", "od_b_f")
WEIGHT_ORDER = ("ffa_norm", "ffa_w_gate_up", "ffa_w_down", "mix_norm", "ffb_norm", "ffb_w_gate_up",
                "ffb_w_down", "ple_norm", "ple_w_gate", "ple_w_proj", "ev_w_in", "ev_sinks",
                "ev_cq_norm", "ev_w_uq", "ev_ckv_norm", "ev_w_ukv", "ev_w_out", "od_w_in", "od_b_f",
                "od_w_out", "final_norm")


def _cp(*sem):
    return pltpu.CompilerParams(dimension_semantics=sem, vmem_limit_bytes=VMEM_LIMIT)


def _sigmoid(z):
    return 1.0 / (1.0 + jnp.exp(-z))


def _rms_stats(xv):
    r = lax.rsqrt(jnp.mean(xv * xv, axis=-1, keepdims=True) + RMS_EPS)
    return r, xv * r


def _rms_bwd(dxn, xv, g):
    r, xhat = _rms_stats(xv)
    u = dxn * g
    dx = r * (u - xhat * jnp.mean(u * xhat, axis=-1, keepdims=True))
    return dx, dxn * xhat


def _col_tile(k_rows, n, budget_bytes=6 * 2**20):
    if k_rows * n * 4 <= budget_bytes or n % LANES:
        return n
    units = n // LANES
    best = LANES
    for d in range(1, units + 1):
        if units % d == 0 and k_rows * d * LANES * 4 <= budget_bytes:
            best = d * LANES
    return best


def _row_tile(rows, cols, target_elems=2**18):
    if rows * cols <= target_elems or rows % 8:
        return rows
    best = 8
    for d in range(8, rows + 1, 8):
        if rows % d == 0 and d * cols <= target_elems:
            best = d
    return best


def _fox_in_fwd(x, g, w, *, nheads, dh, q_ones, k_ones, name, tm=512):
    s, k = x.shape
    n = w.shape[1]
    wd = nheads * dh
    spare = LANES - dh

    def body(x_ref, g_ref, w_ref, q_ref, k_ref, v_ref, vt_ref, f_ref, xn_ref):
        _, xhat = _rms_stats(x_ref[...])
        xn = (xhat * g_ref[...]).astype(BF16)
        xn_ref[...] = xn
        y = jnp.dot(xn, w_ref[...], preferred_element_type=F32)
        f_ref[...] = y[:, 3 * wd:]
        lane = lax.broadcasted_iota(jnp.int32, (tm, spare), 1)

        def fill(cols):
            return functools.reduce(jnp.logical_or, [lane == c for c in cols]).astype(F32)

        q_fill, k_fill = fill(q_ones), fill(k_ones)
        for h in range(nheads):
            q_ref[h] = jnp.concatenate([y[:, h * dh:(h + 1) * dh], q_fill], axis=-1).astype(BF16)
            k_ref[h] = jnp.concatenate([y[:, wd + h * dh:wd + (h + 1) * dh], k_fill], axis=-1).astype(BF16)
            vh = y[:, 2 * wd + h * dh:2 * wd + (h + 1) * dh]
            v_ref[h] = vh.astype(BF16)
            vt_ref[h] = vh.T.astype(BF16)

    wide = pl.BlockSpec((nheads, tm, LANES), lambda i: (0, i, 0))
    return pl.pallas_call(
        body, name=name, grid=(s // tm,),
        in_specs=[pl.BlockSpec((tm, k), lambda i: (i, 0)), pl.BlockSpec((1, k), lambda i: (0, 0)),
                  pl.BlockSpec((k, n), lambda i: (0, 0))],
        out_specs=[wide, wide, pl.BlockSpec((nheads, tm, dh), lambda i: (0, i, 0)),
                   pl.BlockSpec((nheads, dh, tm), lambda i: (0, 0, i)), pl.BlockSpec((tm, LANES), lambda i: (i, 0)),
                   pl.BlockSpec((tm, k), lambda i: (i, 0))],
        out_shape=[jax.ShapeDtypeStruct((nheads, s, LANES), BF16)] * 2
        + [jax.ShapeDtypeStruct((nheads, s, dh), BF16), jax.ShapeDtypeStruct((nheads, dh, s), BF16),
           jax.ShapeDtypeStruct((s, LANES), F32), jax.ShapeDtypeStruct((s, k), BF16)],
        compiler_params=_cp("arbitrary"))(x, g, w)


def _merge_heads(dq, dk, dvt, *, dh, q_col, k_col, name, tm=512):
    nheads, s, _ = dq.shape

    def body(dq_ref, dk_ref, dvt_ref, o_ref, cols_ref):
        pieces = [dq_ref[h][:, :dh] for h in range(nheads)] + [dk_ref[h][:, :dh] for h in range(nheads)]
        pieces += [dvt_ref[h].T for h in range(nheads)]
        o_ref[...] = jnp.concatenate(pieces, axis=-1)
        lane = lax.broadcasted_iota(jnp.int32, (tm, LANES), 1)
        cols = jnp.zeros((tm, LANES), F32)
        for h in range(nheads):
            cols = jnp.where(lane == h, jnp.broadcast_to(dq_ref[h][:, q_col:q_col + 1], (tm, LANES)), cols)
            cols = jnp.where(lane == nheads + h, jnp.broadcast_to(dk_ref[h][:, k_col:k_col + 1], (tm, LANES)), cols)
        cols_ref[...] = cols

    wide = pl.BlockSpec((nheads, tm, LANES), lambda i: (0, i, 0))
    return pl.pallas_call(
        body, name=name, grid=(s // tm,),
        in_specs=[wide, wide, pl.BlockSpec((nheads, dh, tm), lambda i: (0, 0, i))],
        out_specs=[pl.BlockSpec((tm, 3 * nheads * dh), lambda i: (i, 0)), pl.BlockSpec((tm, LANES), lambda i: (i, 0))],
        out_shape=[jax.ShapeDtypeStruct((s, 3 * nheads * dh), F32), jax.ShapeDtypeStruct((s, LANES), F32)],
        compiler_params=_cp("arbitrary"))(dq, dk, dvt)


def _row_call(body, n_rows, ins, outs, *, name, tm=512):
    def spec(a, axis):
        shape = a.shape
        if axis is None:
            return pl.BlockSpec(shape, lambda i: (0,) * len(shape))
        blk = tuple(tm if d == axis else n for d, n in enumerate(shape))
        return pl.BlockSpec(blk, lambda i: tuple(i if d == axis else 0 for d in range(len(shape))))

    return pl.pallas_call(
        body, name=name, grid=(n_rows // tm,), in_specs=[spec(a, ax) for a, ax in ins],
        out_specs=[spec(a, ax) for a, ax in outs], out_shape=[a for a, _ in outs],
        compiler_params=_cp("arbitrary"))(*[a for a, _ in ins])


def _sds(shape, dtype):
    return jax.ShapeDtypeStruct(shape, dtype)


def _ev_in_fwd(x, g, w, *, name):
    s, k = x.shape
    d = A_HEAD_DIM

    def body(x_ref, g_ref, w_ref, q_ref, k_ref, v_ref, vt_ref, cq_ref, ckv_ref, kr_ref, xn_ref):
        _, xhat = _rms_stats(x_ref[...])
        xn = (xhat * g_ref[...]).astype(BF16)
        xn_ref[...] = xn
        y = jnp.dot(xn, w_ref[...], preferred_element_type=F32)
        for h in range(A_HEADS):
            q_ref[h] = y[:, h * d:(h + 1) * d].astype(BF16)
        for h in range(A_KV_HEADS):
            k_ref[h] = y[:, 512 + h * d:512 + (h + 1) * d].astype(BF16)
            vh = y[:, 640 + h * d:640 + (h + 1) * d]
            v_ref[h] = vh.astype(BF16)
            vt_ref[h] = vh.T.astype(BF16)
        cq_ref[...] = y[:, 768:1024]
        ckv_ref[...] = y[:, 1024:1152]
        kr_ref[...] = y[:, 1152:1280]

    return _row_call(
        body, s, [(x, 0), (g, None), (w, None)],
        [(_sds((A_HEADS, s, d), BF16), 1), (_sds((A_KV_HEADS, s, d), BF16), 1), (_sds((A_KV_HEADS, s, d), BF16), 1),
         (_sds((A_KV_HEADS, d, s), BF16), 2), (_sds((s, B_Q_LORA), F32), 0), (_sds((s, B_KV_LORA), F32), 0),
         (_sds((s, LANES), F32), 0), (_sds((s, k), BF16), 0)], name=name)


def _ev_q_fwd(x, g, w, cos, sin, *, name):
    s, k = x.shape
    rot = B_HEADS * B_ROPE

    def body(x_ref, g_ref, w_ref, c_ref, s_ref, q_ref, xn_ref):
        _, xhat = _rms_stats(x_ref[...])
        xn = (xhat * g_ref[...]).astype(BF16)
        xn_ref[...] = xn
        y = jnp.dot(xn, w_ref[...], preferred_element_type=F32)
        ro = y[:, 512:512 + rot] * c_ref[...] + y[:, 512 + rot:] * s_ref[...]
        zero = jnp.zeros((y.shape[0], LANES - B_NOPE - B_ROPE), F32)
        for h in range(B_HEADS):
            q_ref[h] = jnp.concatenate([y[:, h * B_NOPE:(h + 1) * B_NOPE], ro[:, h * B_ROPE:(h + 1) * B_ROPE], zero],
                                       axis=-1).astype(BF16)

    return _row_call(body, s, [(x, 0), (g, None), (w, None), (cos, 0), (sin, 0)],
                     [(_sds((B_HEADS, s, LANES), BF16), 1), (_sds((s, k), BF16), 0)], name=name)


def _ev_kv_fwd(x, g, w, kro, *, name):
    s, k = x.shape
    per = B_NOPE + B_V

    def body(x_ref, g_ref, w_ref, kr_ref, k_ref, v_ref, vt_ref, xn_ref):
        _, xhat = _rms_stats(x_ref[...])
        xn = (xhat * g_ref[...]).astype(BF16)
        xn_ref[...] = xn
        y = jnp.dot(xn, w_ref[...], preferred_element_type=F32)
        kr = kr_ref[...]
        zero = jnp.zeros((y.shape[0], LANES - B_NOPE - B_ROPE), F32)
        for h in range(B_HEADS):
            k_ref[h] = jnp.concatenate([y[:, h * per:h * per + B_NOPE], kr, zero], axis=-1).astype(BF16)
            vh = y[:, h * per + B_NOPE:(h + 1) * per]
            v_ref[h] = vh.astype(BF16)
            vt_ref[h] = vh.T.astype(BF16)

    return _row_call(body, s, [(x, 0), (g, None), (w, None), (kro, 0)],
                     [(_sds((B_HEADS, s, LANES), BF16), 1), (_sds((B_HEADS, s, B_V), BF16), 1),
                      (_sds((B_HEADS, B_V, s), BF16), 2), (_sds((s, k), BF16), 0)], name=name)


def _ev_q_merge(dq, cos, sin, *, name):
    nh, s, _ = dq.shape

    def body(dq_ref, c_ref, s_ref, o_ref):
        dro = jnp.concatenate([dq_ref[h][:, B_NOPE:B_NOPE + B_ROPE] for h in range(nh)], axis=-1)
        o_ref[...] = jnp.concatenate([dq_ref[h][:, :B_NOPE] for h in range(nh)] + [dro * c_ref[...], dro * s_ref[...]], axis=-1)

    return _row_call(body, s, [(dq, 1), (cos, 0), (sin, 0)], [(_sds((s, 2 * nh * B_NOPE), F32), 0)], name=name)[0]


def _ev_kv_merge(dk, dvt, cos, sin, *, name):
    nh, s, _ = dk.shape

    def body(dk_ref, dvt_ref, c_ref, s_ref, o_ref, kr_ref):
        pieces = []
        tot = None
        for h in range(nh):
            pieces += [dk_ref[h][:, :B_NOPE], dvt_ref[h].T]
            rot = dk_ref[h][:, B_NOPE:B_NOPE + B_ROPE]
            tot = rot if tot is None else tot + rot
        o_ref[...] = jnp.concatenate(pieces, axis=-1)
        kr_ref[...] = jnp.concatenate([tot * c_ref[...], tot * s_ref[...], jnp.zeros((tot.shape[0], LANES - 2 * B_ROPE), F32)],
                                      axis=-1)

    return _row_call(body, s, [(dk, 1), (dvt, 2), (cos, 0), (sin, 0)],
                     [(_sds((s, nh * (B_NOPE + B_V)), F32), 0), (_sds((s, LANES), F32), 0)], name=name)


def _ev_in_merge(dq, dk, dvt, dcq, dckv, dkr, *, name):
    s = dcq.shape[0]

    def body(dq_ref, dk_ref, dvt_ref, cq_ref, ckv_ref, kr_ref, o_ref):
        pieces = [dq_ref[h] for h in range(A_HEADS)] + [dk_ref[h] for h in range(A_KV_HEADS)]
        pieces += [dvt_ref[h].T for h in range(A_KV_HEADS)] + [cq_ref[...], ckv_ref[...], kr_ref[...]]
        o_ref[...] = jnp.concatenate(pieces, axis=-1)

    return _row_call(body, s, [(dq, 1), (dk, 1), (dvt, 2), (dcq, 0), (dckv, 0), (dkr, 0)],
                     [(_sds((s, 1280), F32), 0)], name=name)[0]


def _ffn_up(x, g, wgu, *, name, tm=512, rider=None):
    s, k = x.shape
    f = wgu.shape[1] // 2
    tn = _col_tile(k, f)
    nj = f // tn

    def body(x_ref, g_ref, wg_ref, wu_ref, dgate_ref, dup_ref, act_ref, xn_ref, xn_sc):
        @pl.when(pl.program_id(1) == 0)
        def _():
            _, xhat = _rms_stats(x_ref[...])
            xn = (xhat * g_ref[...]).astype(BF16)
            xn_sc[...] = xn
            xn_ref[...] = xn

        xn = xn_sc[...]
        gg = jnp.dot(xn, wg_ref[...], preferred_element_type=F32)
        uu = jnp.dot(xn, wu_ref[...], preferred_element_type=F32)
        sg = _sigmoid(gg)
        silu = gg * sg
        dgate_ref[...] = (uu * (sg * (1.0 + gg * (1.0 - sg)))).astype(BF16)
        dup_ref[...] = silu.astype(BF16)
        act_ref[...] = (silu * uu).astype(BF16)

    tile = pl.BlockSpec((tm, tn), lambda i, j: (i, j))
    return _call_with_rider(
        body, rider, name=name, grid=(s // tm, nj),
        in_specs=[pl.BlockSpec((tm, k), lambda i, j: (i, 0)), pl.BlockSpec((1, k), lambda i, j: (0, 0)),
                  pl.BlockSpec((k, tn), lambda i, j: (0, j)), pl.BlockSpec((k, tn), lambda i, j: (0, j + nj))],
        out_specs=[tile, tile, tile, pl.BlockSpec((tm, k), lambda i, j: (i, 0))],
        out_shape=[jax.ShapeDtypeStruct((s, f), BF16)] * 3 + [jax.ShapeDtypeStruct((s, k), BF16)],
        scratch_shapes=[pltpu.VMEM((tm, k), BF16)],
        compiler_params=_cp("arbitrary", "arbitrary"), args=(x, g, wgu, wgu))


def _mm_res_fwd(a, w, res, *, scale, name, tm=512, a_t=False):
    k, n = w.shape
    s = res.shape[0]

    def body(a_ref, w_ref, r_ref, o_ref):
        prod = (lax.dot_general(a_ref[...], w_ref[...], TN, preferred_element_type=F32) if a_t
                else jnp.dot(a_ref[...], w_ref[...], preferred_element_type=F32))
        o_ref[...] = r_ref[...] + scale * prod

    a_spec = pl.BlockSpec((k, tm), lambda i: (0, i)) if a_t else pl.BlockSpec((tm, k), lambda i: (i, 0))
    return pl.pallas_call(
        body, name=name, grid=(s // tm,),
        in_specs=[a_spec, pl.BlockSpec((k, n), lambda i: (0, 0)),
                  pl.BlockSpec((tm, n), lambda i: (i, 0))],
        out_specs=pl.BlockSpec((tm, n), lambda i: (i, 0)),
        out_shape=jax.ShapeDtypeStruct((s, n), F32),
        compiler_params=_cp("arbitrary"))(a, w, res)


def _ffn_down_bwd(dh, wd, dact_dgate, dact_dup, *, scale, name, tm=512, rider=None):
    s, d = dh.shape
    f = wd.shape[0]
    tn = _col_tile(d, f)

    def body(dh_ref, wd_ref, fg_ref, fu_ref, dg_ref, du_ref):
        dhb = (dh_ref[...] * scale).astype(BF16)
        da = lax.dot_general(dhb, wd_ref[...], NT, preferred_element_type=F32)
        dg_ref[...] = (da * fg_ref[...].astype(F32)).astype(BF16)
        du_ref[...] = (da * fu_ref[...].astype(F32)).astype(BF16)

    tile = pl.BlockSpec((tm, tn), lambda i, j: (i, j))
    return _call_with_rider(
        body, rider, name=name, grid=(s // tm, f // tn),
        in_specs=[pl.BlockSpec((tm, d), lambda i, j: (i, 0)), pl.BlockSpec((tn, d), lambda i, j: (j, 0)), tile, tile],
        out_specs=[tile, tile],
        out_shape=[jax.ShapeDtypeStruct((s, f), BF16)] * 2, scratch_shapes=[],
        compiler_params=_cp("arbitrary", "arbitrary"), args=(dh, wd, dact_dgate, dact_dup))


def _mm_tn(a, bs, *, name, b_scale=1.0, ts=512, out_block_bytes=12 * 2**20, rider=None, a_t=False):
    bs = list(bs) if isinstance(bs, (list, tuple)) else [bs]
    k, s = a.shape if a_t else a.shape[::-1]
    n = bs[0].shape[1]
    tn = _col_tile(k, n, out_block_bytes)
    per = n // tn

    def body(a_ref, *refs):
        b_refs, o_ref = refs[:-1], refs[-1]
        j = pl.program_id(0)

        @pl.when(pl.program_id(1) == 0)
        def _():
            o_ref[...] = jnp.zeros_like(o_ref)

        for m, b_ref in enumerate(b_refs):
            def acc(b_ref=b_ref):
                bv = b_ref[...]
                if b_scale != 1.0:
                    bv = bv * b_scale
                av = a_ref[...].astype(BF16)
                o_ref[...] += (jnp.dot(av, bv.astype(BF16), preferred_element_type=F32) if a_t
                               else lax.dot_general(av, bv.astype(BF16), TN, preferred_element_type=F32))

            if len(b_refs) == 1:
                acc()
            else:
                pl.when(jnp.logical_and(j >= m * per, j < (m + 1) * per))(acc)

    def b_spec(m):
        def idx(j, t):
            mine = jnp.logical_and(j >= m * per, j < (m + 1) * per)
            return (jnp.where(mine, t, 0), jnp.clip(j - m * per, 0, per - 1))
        return pl.BlockSpec((ts, tn), idx)

    (out,), rode = _call_with_rider(
        body, rider, name=name, grid=(per * len(bs), s // ts),
        in_specs=[pl.BlockSpec((k, ts), lambda j, t: (0, t)) if a_t else pl.BlockSpec((ts, k), lambda j, t: (t, 0))]
        + [b_spec(m) for m in range(len(bs))],
        out_specs=[pl.BlockSpec((k, tn), lambda j, t: (0, j))],
        out_shape=[jax.ShapeDtypeStruct((k, n * len(bs)), F32)], scratch_shapes=[],
        compiler_params=_cp("arbitrary", "arbitrary"), args=(a, *bs))
    return out if rider is None else (out, rode)


def _mm_nt_t(dy, w, *, name, tm=512):
    s, n = dy.shape
    k = w.shape[0]

    def body(dy_ref, w_ref, o_ref):
        o_ref[...] = lax.dot_general(w_ref[...], dy_ref[...].astype(BF16), NT, preferred_element_type=F32).astype(BF16)

    return pl.pallas_call(
        body, name=name, grid=(s // tm,),
        in_specs=[pl.BlockSpec((tm, n), lambda i: (i, 0)), pl.BlockSpec((k, n), lambda i: (0, 0))],
        out_specs=pl.BlockSpec((k, tm), lambda i: (0, i)),
        out_shape=jax.ShapeDtypeStruct((k, s), BF16),
        compiler_params=_cp("arbitrary"))(dy, w)


def _mm_nt_rmsbwd(pairs, x, g, dres, *, name, tm=512, rider=None):
    s, k = x.shape
    npairs = len(pairs)
    pairs = [pr if len(pr) == 3 else (pr[0], pr[1], 0) for pr in pairs]

    def body(*refs):
        dy_refs = refs[0:2 * npairs:2]
        w_refs = refs[1:2 * npairs:2]
        rest = refs[2 * npairs:]
        x_ref, g_ref = rest[0], rest[1]
        if dres is None:
            dx_ref, dg_ref = rest[2], rest[3]
        else:
            dres_ref, dx_ref, dg_ref = rest[2], rest[3], rest[4]
        dxn = None
        for dy_ref, w_ref in zip(dy_refs, w_refs):
            t = lax.dot_general(dy_ref[...].astype(BF16), w_ref[...], NT, preferred_element_type=F32)
            dxn = t if dxn is None else dxn + t
        dx, dgrow = _rms_bwd(dxn, x_ref[...], g_ref[...])
        if dres is not None:
            dx = dx + dres_ref[...]
        dx_ref[...] = dx

        @pl.when(pl.program_id(0) == 0)
        def _():
            dg_ref[...] = jnp.zeros_like(dg_ref)

        dg_ref[...] += jnp.sum(dgrow, axis=0, keepdims=True)

    in_specs, args = [], []
    for dy, w, cb in pairs:
        n = dy.shape[1]
        in_specs += [pl.BlockSpec((tm, n), lambda i: (i, 0)),
                     pl.BlockSpec((k, n), lambda i, cb=cb: (0, cb), pipeline_mode=pl.Buffered(1))]
        args += [dy, w]
    row = pl.BlockSpec((tm, k), lambda i: (i, 0))
    vec = pl.BlockSpec((1, k), lambda i: (0, 0))
    in_specs += [row, vec]
    args += [x, g]
    if dres is not None:
        in_specs.append(row)
        args.append(dres)
    (dx, dgain), rode = _call_with_rider(
        body, rider, name=name, grid=(s // tm,), in_specs=in_specs, out_specs=[row, vec],
        out_shape=[jax.ShapeDtypeStruct((s, k), F32), jax.ShapeDtypeStruct((1, k), F32)], scratch_shapes=[],
        compiler_params=_cp("arbitrary"), args=args)
    return (dx, dgain) if rider is None else (dx, dgain, rode)


def _ple_fwd(h, g, wg, p, wp, *, name, tm=512):
    s, d = h.shape
    pd = p.shape[1]

    def body(h_ref, g_ref, wg_ref, p_ref, wp_ref, o_ref, xn_ref, gate_ref, pp_ref):
        hv = h_ref[...]
        _, xhat = _rms_stats(hv)
        xn = (xhat * g_ref[...]).astype(BF16)
        xn_ref[...] = xn
        gate = _sigmoid(jnp.dot(xn, wg_ref[...], preferred_element_type=F32))
        pp = jnp.dot(p_ref[...].astype(BF16), wp_ref[...], preferred_element_type=F32)
        gate_ref[...] = gate.astype(BF16)
        pp_ref[...] = pp.astype(BF16)
        o_ref[...] = hv + gate * pp

    row = pl.BlockSpec((tm, d), lambda i: (i, 0))
    return pl.pallas_call(
        body, name=name, grid=(s // tm,),
        in_specs=[row, pl.BlockSpec((1, d), lambda i: (0, 0)), pl.BlockSpec((d, d), lambda i: (0, 0)),
                  pl.BlockSpec((tm, pd), lambda i: (i, 0)), pl.BlockSpec((pd, d), lambda i: (0, 0))],
        out_specs=[row, row, row, row],
        out_shape=[jax.ShapeDtypeStruct((s, d), F32)] + [jax.ShapeDtypeStruct((s, d), BF16)] * 3,
        compiler_params=_cp("arbitrary"))(h, g, wg, p, wp)


def _ple_bwd_elem(dh, gate, pp, *, name, tm=512):
    s, d = dh.shape

    def body(dh_ref, gate_ref, pp_ref, dz_ref, dpp_ref):
        dhv = dh_ref[...]
        gt = gate_ref[...].astype(F32)
        dz_ref[...] = (dhv * pp_ref[...].astype(F32) * (gt * (1.0 - gt))).astype(BF16)
        dpp_ref[...] = (dhv * gt).astype(BF16)

    row = pl.BlockSpec((tm, d), lambda i: (i, 0))
    return pl.pallas_call(
        body, name=name, grid=(s // tm,), in_specs=[row, row, row], out_specs=[row, row],
        out_shape=[jax.ShapeDtypeStruct((s, d), BF16)] * 2,
        compiler_params=_cp("arbitrary"))(dh, gate, pp)


def _final_loss(h, g, tgt, *, name, tm=512):
    s, d = h.shape

    def body(h_ref, g_ref, t_ref, loss_ref, dh_ref, dg_ref):
        @pl.when(pl.program_id(0) == 0)
        def _():
            loss_ref[...] = jnp.zeros_like(loss_ref)
            dg_ref[...] = jnp.zeros_like(dg_ref)

        hv = h_ref[...]
        gv = g_ref[...]
        _, xhat = _rms_stats(hv)
        err = xhat * gv - t_ref[...]
        per_row = jnp.mean(err * err, axis=-1, keepdims=True)
        loss_ref[...] += 0.5 * jnp.sum(per_row, axis=0, keepdims=True)
        dx, dgrow = _rms_bwd(err * (1.0 / d), hv, gv)
        dh_ref[...] = dx
        dg_ref[...] += jnp.sum(dgrow, axis=0, keepdims=True)

    row = pl.BlockSpec((tm, d), lambda i: (i, 0))
    vec = pl.BlockSpec((1, d), lambda i: (0, 0))
    return pl.pallas_call(
        body, name=name, grid=(s // tm,), in_specs=[row, vec, row],
        out_specs=[pl.BlockSpec((1, LANES), lambda i: (0, 0)), row, vec],
        out_shape=[jax.ShapeDtypeStruct((1, LANES), F32), jax.ShapeDtypeStruct((s, d), F32),
                   jax.ShapeDtypeStruct((1, d), F32)],
        compiler_params=_cp("arbitrary"))(h, g, tgt)


def _rope_fwd(y1, y2, cos, sin, *, name, tm=512):
    s, r = y1.shape

    def body(a_ref, b_ref, c_ref, s_ref, o_ref):
        o_ref[...] = a_ref[...] * c_ref[...] + b_ref[...] * s_ref[...]

    row = pl.BlockSpec((tm, r), lambda i: (i, 0))
    return pl.pallas_call(
        body, name=name, grid=(s // tm,), in_specs=[row] * 4, out_specs=row,
        out_shape=jax.ShapeDtypeStruct((s, r), F32), compiler_params=_cp("arbitrary"))(y1, y2, cos, sin)


def _split3(v):
    h1 = v.astype(BF16)
    r1 = v - h1.astype(F32)
    h2 = r1.astype(BF16)
    h3 = (r1 - h2.astype(F32)).astype(BF16)
    return h1, h2, h3


def _tri(tb, upper):
    r = lax.broadcasted_iota(jnp.int32, (tb, tb), 0)
    c = lax.broadcasted_iota(jnp.int32, (tb, tb), 1)
    return jnp.where((r <= c) if upper else (r >= c), 1.0, 0.0).astype(BF16)


def _fox_gate_fwd(ft, bf, *, out_scale, name, tb=512):
    nh, s = ft.shape

    def body(f_ref, b_ref, o_ref, carry):
        @pl.when(pl.program_id(0) == 0)
        def _():
            carry[...] = jnp.zeros_like(carry)

        z = f_ref[...] + b_ref[...]
        lf = jnp.minimum(z, 0.0) - jnp.log(1.0 + jnp.exp(-jnp.abs(z)))
        tri = _tri(tb, True)
        cs = sum(jnp.dot(t, tri, preferred_element_type=F32) for t in _split3(lf)) + carry[...]
        for n, term in enumerate(_split3(cs * out_scale)):
            o_ref[n] = term
        carry[...] += jnp.sum(lf, axis=-1, keepdims=True)

    return pl.pallas_call(
        body, name=name, grid=(s // tb,),
        in_specs=[pl.BlockSpec((nh, tb), lambda t: (0, t)), pl.BlockSpec((nh, 1), lambda t: (0, 0))],
        out_specs=pl.BlockSpec((3, nh, tb), lambda t: (0, 0, t)),
        out_shape=jax.ShapeDtypeStruct((3, nh, s), BF16),
        scratch_shapes=[pltpu.VMEM((nh, 1), F32)], compiler_params=_cp("arbitrary"))(ft, bf)


def _fox_gate_bwd(drow, dcol, ft, bf, *, inv_scale, name, tb=512):
    nh, s = ft.shape
    nb = s // tb

    def body(dr_ref, dc_ref, f_ref, b_ref, df_ref, db_ref, carry):
        @pl.when(pl.program_id(0) == 0)
        def _():
            carry[...] = jnp.zeros_like(carry)
            db_ref[...] = jnp.zeros_like(db_ref)

        dc = (dr_ref[...] - dc_ref[...]) * inv_scale
        tri = _tri(tb, False)
        suf = sum(jnp.dot(t, tri, preferred_element_type=F32) for t in _split3(dc)) + carry[...]
        z = f_ref[...] + b_ref[...]
        dz = suf * (1.0 / (1.0 + jnp.exp(z)))
        df_ref[...] = dz
        db_ref[...] += jnp.sum(dz, axis=-1, keepdims=True)
        carry[...] += jnp.sum(dc, axis=-1, keepdims=True)

    rev = pl.BlockSpec((nh, tb), lambda t: (0, nb - 1 - t))
    one = pl.BlockSpec((nh, 1), lambda t: (0, 0))
    return pl.pallas_call(
        body, name=name, grid=(nb,), in_specs=[rev, rev, rev, one], out_specs=[rev, one],
        out_shape=[jax.ShapeDtypeStruct((nh, s), F32), jax.ShapeDtypeStruct((nh, 1), F32)],
        scratch_shapes=[pltpu.VMEM((nh, 1), F32)], compiler_params=_cp("arbitrary"))(drow, dcol, ft, bf)


def _tri_fwd(t, nq):
    i = sum((t >= (r * (r + 1)) // 2).astype(jnp.int32) for r in range(1, nq))
    return i, t - (i * (i + 1)) // 2


def _tri_bwd(t, nq):
    j = sum((t >= r * nq - (r * (r - 1)) // 2).astype(jnp.int32) for r in range(1, nq))
    return j, j + t - (j * nq - (j * (j - 1)) // 2)


def _scores_t(k, q, *, scale, diag):
    s = lax.dot_general(k, q, NT, preferred_element_type=F32) * scale
    if diag:
        r = lax.broadcasted_iota(jnp.int32, s.shape, 0)
        c = lax.broadcasted_iota(jnp.int32, s.shape, 1)
        s = jnp.where(r <= c, s, MASK_VALUE)
    return s


def _causal_fwd_t(q, k, vt, *, scale, name, tq, hb=2, rider=None):
    nh, s, dq = q.shape
    dv = vt.shape[1]
    nq = s // tq
    nsteps = (nq * (nq + 1)) // 2

    def body(q_ref, k_ref, vt_ref, o_ref, lse_ref, m_sc, l_sc, acc_sc):
        i, j = _tri_fwd(pl.program_id(1), nq)

        @pl.when(j == 0)
        def _():
            m_sc[...] = jnp.full_like(m_sc, MASK_VALUE)
            l_sc[...] = jnp.zeros_like(l_sc)
            acc_sc[...] = jnp.zeros_like(acc_sc)

        def step(diag):
            for u in range(hb):
                sc = _scores_t(k_ref[u], q_ref[u], scale=scale, diag=diag)
                m_prev = m_sc[u]
                m_new = jnp.maximum(m_prev, jnp.max(sc, axis=0, keepdims=True))
                alpha = jnp.exp(m_prev - m_new)
                pr = jnp.exp(sc - m_new)
                l_new = alpha * l_sc[u] + jnp.sum(pr, axis=0, keepdims=True)
                acc = alpha * acc_sc[u] + jnp.dot(vt_ref[u], pr.astype(BF16), preferred_element_type=F32)
                if diag:
                    o_ref[u] = (acc / l_new).astype(BF16)
                    lse_ref[u] = m_new + jnp.log(l_new)
                else:
                    m_sc[u], l_sc[u], acc_sc[u] = m_new, l_new, acc

        pl.when(j < i)(functools.partial(step, False))
        pl.when(j == i)(functools.partial(step, True))

    def qi(t):
        return _tri_fwd(t, nq)[0]

    def kj(t):
        return _tri_fwd(t, nq)[1]

    return _call_with_rider(
        body, rider, name=name, grid=(nh // hb, nsteps),
        in_specs=[pl.BlockSpec((hb, tq, dq), lambda hp, t: (hp, qi(t), 0)),
                  pl.BlockSpec((hb, tq, dq), lambda hp, t: (hp, kj(t), 0)),
                  pl.BlockSpec((hb, dv, tq), lambda hp, t: (hp, 0, kj(t)))],
        out_specs=[pl.BlockSpec((hb, dv, tq), lambda hp, t: (hp, 0, qi(t))),
                   pl.BlockSpec((hb, 1, tq), lambda hp, t: (hp, 0, qi(t)))],
        out_shape=[jax.ShapeDtypeStruct((nh, dv, s), BF16), jax.ShapeDtypeStruct((nh, 1, s), F32)],
        scratch_shapes=[pltpu.VMEM((hb, 1, tq), F32), pltpu.VMEM((hb, 1, tq), F32), pltpu.VMEM((hb, dv, tq), F32)],
        compiler_params=_cp("arbitrary", "arbitrary"), args=(q, k, vt))


def _causal_bwd_t(q, k, v, ot, dot_, lse, *, scale, name, tq, hb=2, rider=None):
    nh, s, dq = q.shape
    dv = v.shape[-1]
    nq = s // tq
    nsteps = (nq * (nq + 1)) // 2

    def body(q_ref, k_ref, v_ref, ot_ref, dot_ref, lse_ref, dq_ref, dk_ref, dvt_ref):
        t = pl.program_id(1)
        j, i = _tri_bwd(t, nq)

        @pl.when(t == 0)
        def _():
            dq_ref[...] = jnp.zeros_like(dq_ref)

        def step(diag):
            rows = pl.ds(pl.multiple_of(i * tq, tq), tq)
            for u in range(hb):
                qv, kv, dov = q_ref[u], k_ref[u], dot_ref[u]
                pr = jnp.exp(_scores_t(kv, qv, scale=scale, diag=diag) - lse_ref[u])
                dp = jnp.dot(v_ref[u], dov, preferred_element_type=F32)
                delta = jnp.sum(dov.astype(F32) * ot_ref[u].astype(F32), axis=0, keepdims=True)
                dsb = ((pr * (dp - delta)) * scale).astype(BF16)
                d_v = lax.dot_general(dov, pr.astype(BF16), NT, preferred_element_type=F32)
                d_k = jnp.dot(dsb, qv, preferred_element_type=F32)
                if diag:
                    dvt_ref[u], dk_ref[u] = d_v, d_k
                else:
                    dvt_ref[u] += d_v
                    dk_ref[u] += d_k
                dq_ref[u, rows, :] += lax.dot_general(dsb, kv, TN, preferred_element_type=F32)

        pl.when(i > j)(functools.partial(step, False))
        pl.when(i == j)(functools.partial(step, True))

    def qi(t):
        return _tri_bwd(t, nq)[1]

    def kj(t):
        return _tri_bwd(t, nq)[0]

    rows_q = pl.BlockSpec((hb, tq, dq), lambda hp, t: (hp, qi(t), 0))
    rows_k = pl.BlockSpec((hb, tq, dq), lambda hp, t: (hp, kj(t), 0))
    lanes_q = pl.BlockSpec((hb, dv, tq), lambda hp, t: (hp, 0, qi(t)))
    return _call_with_rider(
        body, rider, name=name, grid=(nh // hb, nsteps),
        in_specs=[rows_q, rows_k, pl.BlockSpec((hb, tq, dv), lambda hp, t: (hp, kj(t), 0)), lanes_q, lanes_q,
                  pl.BlockSpec((hb, 1, tq), lambda hp, t: (hp, 0, qi(t)))],
        out_specs=[pl.BlockSpec((hb, s, dq), lambda hp, t: (hp, 0, 0)), rows_k,
                   pl.BlockSpec((hb, dv, tq), lambda hp, t: (hp, 0, kj(t)))],
        out_shape=[jax.ShapeDtypeStruct((nh, s, dq), F32), jax.ShapeDtypeStruct((nh, s, dq), F32),
                   jax.ShapeDtypeStruct((nh, dv, s), F32)],
        scratch_shapes=[], compiler_params=_cp("arbitrary", "arbitrary"), args=(q, k, v, ot, dot_, lse))


def _swa_scores_t(k, q, dist, ok, *, scale, slope):
    s = lax.dot_general(k, q, NT, preferred_element_type=F32) * scale - slope * dist.astype(F32)
    return jnp.where(ok, s, MASK_VALUE)


def _swa_geometry(tb, w, has_other):
    r = lax.broadcasted_iota(jnp.int32, (tb, tb), 0)
    c = lax.broadcasted_iota(jnp.int32, (tb, tb), 1)
    d_same = c - r
    ok_same = jnp.logical_and(d_same >= 0, d_same < w)

    def other(ncols):
        rr = lax.broadcasted_iota(jnp.int32, (w, ncols), 0)
        cc = lax.broadcasted_iota(jnp.int32, (w, ncols), 1)
        dd = cc + w - rr
        return dd, jnp.logical_and(dd < w, has_other)

    return (d_same, ok_same), other


def _swa_fwd_t(q, k, vt, slopes_sinks, *, scale, window, name, tb=256):
    nh, s, d = q.shape
    nkv = k.shape[0]
    grp = nh // nkv
    w = window
    per = tb // w
    assert tb % w == 0

    def body(q_ref, kc_ref, kp_ref, vc_ref, vp_ref, ss_ref, o_ref, lse_ref):
        kvh, i = pl.program_id(0), pl.program_id(1)
        (d_c, ok_c), other = _swa_geometry(tb, w, i > 0)
        d_p, ok_p = other(tb)
        for g in range(grp):
            h = kvh * grp + g
            slope, sink = ss_ref[0, h], ss_ref[1, h]
            qg = q_ref[g]
            s_c = _swa_scores_t(kc_ref[...], qg, d_c, ok_c, scale=scale, slope=slope)
            s_p = _swa_scores_t(kp_ref[...], qg, d_p, ok_p, scale=scale, slope=slope)
            m = jnp.maximum(jnp.maximum(jnp.max(s_c, axis=0, keepdims=True), jnp.max(s_p, axis=0, keepdims=True)), sink)
            p_c, p_p = jnp.exp(s_c - m), jnp.exp(s_p - m)
            l = jnp.sum(p_c, axis=0, keepdims=True) + jnp.sum(p_p, axis=0, keepdims=True) + jnp.exp(sink - m)
            acc = (jnp.dot(vc_ref[...], p_c.astype(BF16), preferred_element_type=F32)
                   + jnp.dot(vp_ref[...], p_p.astype(BF16), preferred_element_type=F32))
            o_ref[g] = (acc / l).astype(BF16)
            lse_ref[g] = m + jnp.log(l)

    def prev(i):
        return jnp.maximum(i * per - 1, 0)

    return pl.pallas_call(
        body, name=name, grid=(nkv, s // tb),
        in_specs=[pl.BlockSpec((grp, tb, d), lambda kh, i: (kh, i, 0)),
                  pl.BlockSpec((None, tb, d), lambda kh, i: (kh, i, 0)),
                  pl.BlockSpec((None, w, d), lambda kh, i: (kh, prev(i), 0)),
                  pl.BlockSpec((None, d, tb), lambda kh, i: (kh, 0, i)),
                  pl.BlockSpec((None, d, w), lambda kh, i: (kh, 0, prev(i))),
                  pl.BlockSpec(memory_space=pltpu.SMEM)],
        out_specs=[pl.BlockSpec((grp, d, tb), lambda kh, i: (kh, 0, i)), pl.BlockSpec((grp, 1, tb), lambda kh, i: (kh, 0, i))],
        out_shape=[jax.ShapeDtypeStruct((nh, d, s), BF16), jax.ShapeDtypeStruct((nh, 1, s), F32)],
        compiler_params=_cp("arbitrary", "arbitrary"))(q, k, k, vt, vt, slopes_sinks)


def _swa_bwd_t(q, k, v, ot, dot_, lse, slopes_sinks, *, scale, window, name, tb=256, rider=None):
    nh, s, d = q.shape
    nkv = k.shape[0]
    grp = nh // nkv
    w = window
    per = tb // w
    nb = s // tb

    def body(qc_ref, qn_ref, kc_ref, kp_ref, vc_ref, vp_ref, oc_ref, on_ref, doc_ref, don_ref, lc_ref, ln_ref, ss_ref,
             dq_ref, dk_ref, dvt_ref, dsink_ref):
        kvh, i = pl.program_id(0), pl.program_id(1)

        @pl.when(i == 0)
        def _():
            dsink_ref[...] = jnp.zeros_like(dsink_ref)

        (d_c, ok_c), other = _swa_geometry(tb, w, i > 0)
        d_p, ok_p = other(tb)
        d_n, ok_n = _swa_geometry(tb, w, i < nb - 1)[1](w)
        kc, kp, vc, vp = kc_ref[...], kp_ref[...], vc_ref[...], vp_ref[...]
        k_last, v_last = kc[tb - w:, :], vc[tb - w:, :]
        dk_acc = jnp.zeros((tb, d), F32)
        dv_acc = jnp.zeros((d, tb), F32)
        dk_tail = jnp.zeros((w, d), F32)
        dv_tail = jnp.zeros((d, w), F32)
        for g in range(grp):
            h = kvh * grp + g
            slope, sink = ss_ref[0, h], ss_ref[1, h]
            qg, dog, lse_c = qc_ref[g], doc_ref[g], lc_ref[g]
            delta = jnp.sum(dog.astype(F32) * oc_ref[g].astype(F32), axis=0, keepdims=True)
            p_c = jnp.exp(_swa_scores_t(kc, qg, d_c, ok_c, scale=scale, slope=slope) - lse_c)
            p_p = jnp.exp(_swa_scores_t(kp, qg, d_p, ok_p, scale=scale, slope=slope) - lse_c)
            ds_c = ((p_c * (jnp.dot(vc, dog, preferred_element_type=F32) - delta)) * scale).astype(BF16)
            ds_p = ((p_p * (jnp.dot(vp, dog, preferred_element_type=F32) - delta)) * scale).astype(BF16)
            dq_ref[g] = (lax.dot_general(ds_c, kc, TN, preferred_element_type=F32)
                         + lax.dot_general(ds_p, kp, TN, preferred_element_type=F32))
            dk_acc += jnp.dot(ds_c, qg, preferred_element_type=F32)
            dv_acc += lax.dot_general(dog, p_c.astype(BF16), NT, preferred_element_type=F32)
            dsink_ref[g] -= jnp.broadcast_to(jnp.sum(jnp.exp(sink - lse_c) * delta, axis=1, keepdims=True), (1, LANES))
            qn, don = qn_ref[g], don_ref[g]
            delta_n = jnp.sum(don.astype(F32) * on_ref[g].astype(F32), axis=0, keepdims=True)
            p_n = jnp.exp(_swa_scores_t(k_last, qn, d_n, ok_n, scale=scale, slope=slope) - ln_ref[g])
            ds_n = ((p_n * (jnp.dot(v_last, don, preferred_element_type=F32) - delta_n)) * scale).astype(BF16)
            dk_tail += jnp.dot(ds_n, qn, preferred_element_type=F32)
            dv_tail += lax.dot_general(don, p_n.astype(BF16), NT, preferred_element_type=F32)
        dk_ref[...] = dk_acc
        dvt_ref[...] = dv_acc
        dk_ref[tb - w:, :] += dk_tail
        dvt_ref[:, tb - w:] += dv_tail

    def prev(i):
        return jnp.maximum(i * per - 1, 0)

    def nxt(i):
        return jnp.minimum((i + 1) * per, s // w - 1)

    return _call_with_rider(
        body, rider, name=name, grid=(nkv, nb), scratch_shapes=[],
        args=(q, q, k, k, v, v, ot, ot, dot_, dot_, lse, lse, slopes_sinks),
        in_specs=[pl.BlockSpec((grp, tb, d), lambda kh, i: (kh, i, 0)),
                  pl.BlockSpec((grp, w, d), lambda kh, i: (kh, nxt(i), 0)),
                  pl.BlockSpec((None, tb, d), lambda kh, i: (kh, i, 0)),
                  pl.BlockSpec((None, w, d), lambda kh, i: (kh, prev(i), 0)),
                  pl.BlockSpec((None, tb, d), lambda kh, i: (kh, i, 0)),
                  pl.BlockSpec((None, w, d), lambda kh, i: (kh, prev(i), 0)),
                  pl.BlockSpec((grp, d, tb), lambda kh, i: (kh, 0, i)),
                  pl.BlockSpec((grp, d, w), lambda kh, i: (kh, 0, nxt(i))),
                  pl.BlockSpec((grp, d, tb), lambda kh, i: (kh, 0, i)),
                  pl.BlockSpec((grp, d, w), lambda kh, i: (kh, 0, nxt(i))),
                  pl.BlockSpec((grp, 1, tb), lambda kh, i: (kh, 0, i)),
                  pl.BlockSpec((grp, 1, w), lambda kh, i: (kh, 0, nxt(i))),
                  pl.BlockSpec(memory_space=pltpu.SMEM)],
        out_specs=[pl.BlockSpec((grp, tb, d), lambda kh, i: (kh, i, 0)),
                   pl.BlockSpec((None, tb, d), lambda kh, i: (kh, i, 0)),
                   pl.BlockSpec((None, d, tb), lambda kh, i: (kh, 0, i)),
                   pl.BlockSpec((None, grp, 1, LANES), lambda kh, i: (kh, 0, 0, 0))],
        out_shape=[jax.ShapeDtypeStruct((nh, s, d), F32), jax.ShapeDtypeStruct((nkv, s, d), F32),
                   jax.ShapeDtypeStruct((nkv, d, s), F32), jax.ShapeDtypeStruct((nkv, grp, 1, LANES), F32)],
        compiler_params=_cp("arbitrary", "arbitrary"))


def _adamw(w, g, m, v, *, name):
    shape = w.shape
    cols = shape[-1]
    rows = int(np.prod(shape[:-1])) if len(shape) > 1 else 1
    tr = _row_tile(rows, cols)
    c1 = 1.0 - ADAM_B1 ** ADAM_STEP
    c2 = 1.0 - ADAM_B2 ** ADAM_STEP

    def body(w_ref, g_ref, m_ref, v_ref, d_ref, mo_ref, vo_ref):
        gv = g_ref[...]
        mn = ADAM_B1 * m_ref[...] + (1.0 - ADAM_B1) * gv
        vn = ADAM_B2 * v_ref[...] + (1.0 - ADAM_B2) * (gv * gv)
        mo_ref[...] = mn
        vo_ref[...] = vn
        d_ref[...] = -ADAM_LR * ((mn / c1) / (jnp.sqrt(vn / c2) + ADAM_EPS) + ADAM_WD * w_ref[...])

    blk = pl.BlockSpec((tr, cols), lambda i: (i, 0))
    outs = pl.pallas_call(
        body, name=name, grid=(rows // tr,), in_specs=[blk] * 4, out_specs=[blk] * 3,
        out_shape=[jax.ShapeDtypeStruct((rows, cols), F32)] * 3,
        compiler_params=_cp("arbitrary"))(*[a.reshape(rows, cols) for a in (w, g, m, v)])
    return tuple(a.reshape(shape) for a in outs)


def _hbm_spec():
    return pl.BlockSpec(memory_space=pl.ANY)


def _mesh_place():
    x, y, c = lax.axis_index("x"), lax.axis_index("y"), lax.axis_index("c")
    return x, y, c, [(1 - x, y), (x, 1 - y), (1 - x, 1 - y)]


def _half_rows(c, rows, align):
    return pl.ds(pl.multiple_of(c * (rows // 2), align), rows // 2)


def _part(ref, mode, k, n, rows=None):
    if mode == "cols":
        cols = pl.ds(pl.multiple_of(k * n, LANES), n)
        return ref.at[:, cols] if rows is None else ref.at[rows, cols]
    return ref.at[k] if rows is None else ref.at[k, rows, :]


class _Rider:
    def __init__(self, inputs, out_shape, n_sems, start, finish):
        self.inputs, self.out_shape, self.n_sems, self.start, self.finish = inputs, out_shape, n_sems, start, finish


def _call_with_rider(body, rider, *, name, grid, in_specs, out_specs, out_shape, scratch_shapes, compiler_params, args):
    if rider is None:
        outs = pl.pallas_call(body, name=name, grid=grid, in_specs=in_specs, out_specs=out_specs, out_shape=out_shape,
                              scratch_shapes=scratch_shapes, compiler_params=compiler_params)(*args)
        return outs, []
    n_in, n_out, n_sc = len(in_specs), len(out_specs), len(scratch_shapes)
    n_rin, n_rout = len(rider.inputs), len(rider.out_shape)

    def wrapped(*refs):
        pos = 0
        groups = []
        for n in (n_in, n_rin, n_out, n_rout, n_sc, 2):
            groups.append(refs[pos:pos + n])
            pos += n
        ins, rins, outs, routs, scratch, sems = groups
        ids = [pl.program_id(a) for a in range(len(grid))]
        first = functools.reduce(jnp.logical_and, [i == 0 for i in ids])
        last = functools.reduce(jnp.logical_and, [i == g - 1 for i, g in zip(ids, grid)])
        pl.when(first)(lambda: rider.start(rins, routs, *sems))
        body(*ins, *outs, *scratch)
        pl.when(last)(lambda: rider.finish(rins, routs, *sems))

    outs = pl.pallas_call(
        wrapped, name=name, grid=grid, in_specs=list(in_specs) + [_hbm_spec()] * n_rin,
        out_specs=list(out_specs) + [_hbm_spec()] * n_rout, out_shape=list(out_shape) + list(rider.out_shape),
        scratch_shapes=list(scratch_shapes) + [pltpu.SemaphoreType.DMA((rider.n_sems,))] * 2,
        compiler_params=compiler_params)(*args, *rider.inputs)
    return outs[:n_out], outs[n_out:]


def _run_rider(rider, *, name):
    n_rin = len(rider.inputs)

    def body(*refs):
        rins, routs, sems = refs[:n_rin], refs[n_rin:-2], refs[-2:]
        rider.start(rins, routs, *sems)
        rider.finish(rins, routs, *sems)

    return pl.pallas_call(
        body, name=name, in_specs=[_hbm_spec()] * n_rin, out_specs=[_hbm_spec()] * len(rider.out_shape),
        out_shape=rider.out_shape, scratch_shapes=[pltpu.SemaphoreType.DMA((rider.n_sems,))] * 2)(*rider.inputs)


def _gather_rider(shards, modes):
    n_arr = len(shards)
    out_shape = [jax.ShapeDtypeStruct((s.shape[0], N_CHIPS * s.shape[1]) if m == "cols" else (N_CHIPS,) + s.shape, s.dtype)
                 for s, m in zip(shards, modes)]
    per = 4

    def copies(srcs, dsts, send_sems, recv_sems):
        x, y, c, chips = _mesh_place()
        me = 2 * x + y
        sends, waits = [], []
        for i in range(n_arr):
            r, n = shards[i].shape
            rows = _half_rows(c, r, 16)

            def copy(slot, src, dst, to, i=i):
                return pltpu.make_async_remote_copy(src_ref=src, dst_ref=dst, send_sem=send_sems.at[i * per + slot],
                                                    recv_sem=recv_sems.at[i * per + slot], device_id=to, device_id_type=MESH)

            own = _part(dsts[i], modes[i], me, n)
            sends.append(copy(0, srcs[i], own, (x, y, 1 - c)))
            waits.append(copy(0, own, own, (x, y, 1 - c)))
            for j, (px, py) in enumerate(chips):
                sends.append(copy(1 + j, srcs[i].at[rows], _part(dsts[i], modes[i], me, n, rows), (px, py, c)))
                theirs = _part(dsts[i], modes[i], 2 * px + py, n, rows)
                waits.append(copy(1 + j, theirs, theirs, (px, py, c)))
        return sends, waits

    def start(*refs):
        for cp in copies(*refs)[0]:
            cp.start()

    def finish(*refs):
        sends, waits = copies(*refs)
        for cp in waits:
            cp.wait_recv()
        for cp in sends:
            cp.wait_send()

    return _Rider(list(shards), out_shape, per * n_arr, start, finish)


def _gather_forward(dsts, shard_shapes, modes, *, name):
    n_arr = len(dsts)

    def body(*refs):
        outs = refs[n_arr:2 * n_arr]
        send_sems, recv_sems = refs[2 * n_arr:]
        x, y, c, chips = _mesh_place()
        cps = []
        for i in range(n_arr):
            r, n = shard_shapes[i]
            for j, (px, py) in enumerate(chips):
                def view(hc, i=i, px=px, py=py, r=r, n=n):
                    return _part(outs[i], modes[i], 2 * px + py, n, _half_rows(hc, r, 16))

                def copy(ref, i=i, j=j):
                    return pltpu.make_async_remote_copy(src_ref=ref, dst_ref=ref, send_sem=send_sems.at[3 * i + j],
                                                        recv_sem=recv_sems.at[3 * i + j], device_id=(x, y, 1 - c), device_id_type=MESH)

                cps.append((copy(view(c)), copy(view(1 - c))))
        for send, _ in cps:
            send.start()
        for send, theirs in cps:
            theirs.wait_recv()
            send.wait_send()

    return pl.pallas_call(
        body, name=name, in_specs=[_hbm_spec()] * n_arr, out_specs=[_hbm_spec()] * n_arr,
        out_shape=[jax.ShapeDtypeStruct(d.shape, d.dtype) for d in dsts],
        input_output_aliases={i: i for i in range(n_arr)},
        scratch_shapes=[pltpu.SemaphoreType.DMA((3 * n_arr,)), pltpu.SemaphoreType.DMA((3 * n_arr,))])(*dsts)


def _blk_view(a, mode):
    return a[None] if mode == "cols" else a


def _swap_rider(arrs, modes):
    n_arr = len(arrs)
    out_shape = [jax.ShapeDtypeStruct((a.shape[0] // 2, a.shape[1]) if m == "cols" else (a.shape[0], a.shape[1] // 2, a.shape[2]), a.dtype)
                 for a, m in zip(arrs, modes)]

    def copies(srcs, dsts, send_sems, recv_sems):
        x, y, c, _ = _mesh_place()
        cps = []
        for i in range(n_arr):
            if modes[i] == "cols":
                src = srcs[i].at[_half_rows(1 - c, arrs[i].shape[0], 8)]
            else:
                src = srcs[i].at[:, _half_rows(1 - c, arrs[i].shape[1], 8), :]
            cps.append(pltpu.make_async_remote_copy(src_ref=src, dst_ref=dsts[i], send_sem=send_sems.at[i],
                                                    recv_sem=recv_sems.at[i], device_id=(x, y, 1 - c), device_id_type=MESH))
        return cps

    def start(*refs):
        for cp in copies(*refs):
            cp.start()

    def finish(*refs):
        for cp in copies(*refs):
            cp.wait()

    return _Rider(list(arrs), out_shape, n_arr, start, finish)


def _rs_pair_add(arr, landed, place, *, name):
    nb, r, c = arr.shape
    rh = r // 2
    tr = _row_tile(rh, c)
    nt = rh // tr

    def body(p_ref, a_ref, l_ref, o_ref):
        o_ref[...] = (a_ref[...] + l_ref[...]).astype(BF16)

    grid_spec = pltpu.PrefetchScalarGridSpec(
        num_scalar_prefetch=1, grid=(nb, nt),
        in_specs=[pl.BlockSpec((None, tr, c), lambda b, t, p_ref: (b, p_ref[1] * nt + t, 0)),
                  pl.BlockSpec((None, tr, c), lambda b, t, p_ref: (b, t, 0))],
        out_specs=pl.BlockSpec((None, tr, c), lambda b, t, p_ref: (b, t, 0)))
    return pl.pallas_call(
        body, name=name, grid_spec=grid_spec, out_shape=jax.ShapeDtypeStruct((nb, rh, c), BF16),
        compiler_params=_cp("arbitrary", "arbitrary"))(place, arr, landed)


def _exchange_rider(parts, modes):
    n_arr = len(parts)
    out_shape = []
    for a, m in zip(parts, modes):
        shp = (a.shape[0], a.shape[1] // N_CHIPS) if m == "cols" else a.shape[1:]
        out_shape.append(jax.ShapeDtypeStruct((3,) + shp, a.dtype))

    def copies(srcs, dsts, send_sems, recv_sems):
        x, y, c, chips = _mesh_place()
        cps = []
        for i in range(n_arr):
            n = out_shape[i].shape[-1]
            for j, (px, py) in enumerate(chips):
                cps.append(pltpu.make_async_remote_copy(
                    src_ref=_part(srcs[i], modes[i], 2 * px + py, n), dst_ref=dsts[i].at[j],
                    send_sem=send_sems.at[3 * i + j], recv_sem=recv_sems.at[3 * i + j],
                    device_id=(px, py, c), device_id_type=MESH))
        return cps

    def start(*refs):
        for cp in copies(*refs):
            cp.start()

    def finish(*refs):
        for cp in copies(*refs):
            cp.wait()

    return _Rider(list(parts), out_shape, 3 * n_arr, start, finish)


def _rs_chip_sum(part, landed, mode, place, *, name):
    _, rh, n = landed.shape
    tr = _row_tile(rh, n)
    nt = rh // tr

    def body(p_ref, a_ref, l_ref, o_ref):
        o_ref[...] = ((a_ref[...].astype(F32) + l_ref[0].astype(F32)) + l_ref[1].astype(F32)) + l_ref[2].astype(F32)

    if mode == "cols":
        own = pl.BlockSpec((tr, n), lambda t, p_ref: (t, p_ref[0]))
    else:
        own = pl.BlockSpec((None, tr, n), lambda t, p_ref: (p_ref[0], t, 0))
    grid_spec = pltpu.PrefetchScalarGridSpec(
        num_scalar_prefetch=1, grid=(nt,),
        in_specs=[own, pl.BlockSpec((3, tr, n), lambda t, p_ref: (0, t, 0))],
        out_specs=pl.BlockSpec((tr, n), lambda t, p_ref: (p_ref[1] * nt + t, 0)))
    return pl.pallas_call(
        body, name=name, grid_spec=grid_spec, out_shape=jax.ShapeDtypeStruct((2 * rh, n), F32),
        compiler_params=_cp("arbitrary"))(place, part, landed)


def _rs_pair_join(halves, *, name):
    n_arr = len(halves)

    def body(*refs):
        outs = refs[n_arr:2 * n_arr]
        send_sems, recv_sems = refs[2 * n_arr:]
        x, y, c, _ = _mesh_place()
        cps = []
        for i in range(n_arr):
            rows = _half_rows(c, halves[i].shape[0], 8)
            cps.append(pltpu.make_async_remote_copy(src_ref=outs[i].at[rows], dst_ref=outs[i].at[rows], send_sem=send_sems.at[i],
                                                    recv_sem=recv_sems.at[i], device_id=(x, y, 1 - c), device_id_type=MESH))
        for cp in cps:
            cp.start()
        for i, cp in enumerate(cps):
            cp.wait_send()
            theirs = outs[i].at[_half_rows(1 - c, halves[i].shape[0], 8)]
            pltpu.make_async_remote_copy(src_ref=theirs, dst_ref=theirs, send_sem=send_sems.at[i], recv_sem=recv_sems.at[i],
                                         device_id=(x, y, 1 - c), device_id_type=MESH).wait_recv()

    return pl.pallas_call(
        body, name=name, in_specs=[_hbm_spec()] * n_arr, out_specs=[_hbm_spec()] * n_arr,
        out_shape=[jax.ShapeDtypeStruct(h.shape, h.dtype) for h in halves],
        input_output_aliases={i: i for i in range(n_arr)},
        scratch_shapes=[pltpu.SemaphoreType.DMA((n_arr,)), pltpu.SemaphoreType.DMA((n_arr,))])(*halves)


def _allreduce_small(v, *, name):
    r, c = v.shape

    def body(v_ref, o_ref, gath, send_sems, recv_sems):
        x, y, cc, _ = _mesh_place()
        me = 4 * x + 2 * y + cc
        gath[me] = v_ref[...]
        cps = []
        for rel in range(1, 8):
            px = 1 - x if rel & 4 else x
            py = 1 - y if rel & 2 else y
            pc = 1 - cc if rel & 1 else cc

            def copy(slot, px=px, py=py, pc=pc, rel=rel):
                return pltpu.make_async_remote_copy(
                    src_ref=v_ref, dst_ref=gath.at[slot], send_sem=send_sems.at[rel - 1],
                    recv_sem=recv_sems.at[rel - 1], device_id=(px, py, pc), device_id_type=MESH)

            cps.append((copy(me), copy(4 * px + 2 * py + pc)))
        for send, _ in cps:
            send.start()
        for send, theirs in cps:
            theirs.wait_recv()
            send.wait_send()
        tot = gath[0]
        for d in range(1, 8):
            tot = tot + gath[d]
        o_ref[...] = tot

    vm = pl.BlockSpec(memory_space=pltpu.VMEM)
    return pl.pallas_call(
        body, name=name, in_specs=[vm], out_specs=vm, out_shape=jax.ShapeDtypeStruct((r, c), F32),
        scratch_shapes=[pltpu.VMEM((8, r, c), F32), pltpu.SemaphoreType.DMA((7,)), pltpu.SemaphoreType.DMA((7,))])(v)


def _rope_tables(s, reps):
    half = B_ROPE // 2
    inv = ROPE_THETA ** (-jnp.arange(0, B_ROPE, 2, dtype=F32) / B_ROPE)
    ang = jnp.arange(s, dtype=F32)[:, None] * inv[None, :]
    return jnp.tile(jnp.cos(ang), (1, reps)), jnp.tile(jnp.sin(ang), (1, reps))


def _alibi_slopes():
    return 2.0 ** (-8.0 * jnp.arange(1, A_HEADS + 1, dtype=F32) / A_HEADS)


def _ffn_fwd(h, norm, wts, tag, rider=None, on_rode=None):
    (dact_dgate, dact_dup, act, xn), rode = _ffn_up(h, norm, wts["wgu"], name=f"{tag}_up", rider=rider)
    if on_rode is not None:
        on_rode(rode)
    out = _mm_res_fwd(act, wts["wd"], h, scale=FFN_RES_SCALE, name=f"{tag}_down")
    return out, dict(h_in=h, dact_dgate=dact_dgate, dact_dup=dact_dup, act=act, xn=xn), rode


def _ffn_bwd(dh, norm, wts, sv, tag, rider=None, own=None):
    (dgate, dup), rode = _ffn_down_bwd(dh, wts["wd"], sv["dact_dgate"], sv["dact_dup"], scale=FFN_RES_SCALE,
                                      name=f"{tag}_down_bwd", rider=rider)
    long_rows = dict(ts=1024, out_block_bytes=6 * 2**20)
    d_wd = _mm_tn(sv["act"], dh, b_scale=FFN_RES_SCALE, name=f"{tag}_dwd", **long_rows)
    pairs = [(dgate, wts["wgu"], 0), (dup, wts["wgu"], 1)]
    if own is None:
        d_wgu = _mm_tn(sv["xn"], [dgate, dup], name=f"{tag}_dwgu", **long_rows)
        dh_in, dnorm = _mm_nt_rmsbwd(pairs, sv["h_in"], norm, dh, name=f"{tag}_dx")
    else:
        wd_ready, wgu_ready, done = own
        first = wd_ready(d_wd)
        res = _mm_tn(sv["xn"], [dgate, dup], name=f"{tag}_dwgu", rider=first, **long_rows)
        d_wgu, brought = (res, []) if first is None else res
        second = wgu_ready(brought, d_wgu)
        res = _mm_nt_rmsbwd(pairs, sv["h_in"], norm, dh, name=f"{tag}_dx", rider=second)
        dh_in, dnorm, brought = (*res, []) if second is None else res
        done(brought)
    return dh_in, dnorm, d_wgu, d_wd, rode


def _even_weights(w_in, w_uq, w_ukv):
    half = B_ROPE // 2
    base = w_in.shape[1]
    kr1, kr2 = w_in[:, base - B_ROPE:base - half], w_in[:, base - half:]
    w_in_cat = jnp.concatenate([w_in, -kr2, kr1, jnp.zeros((w_in.shape[0], 64), w_in.dtype)], axis=1)
    u3 = w_uq.reshape(w_uq.shape[0], B_HEADS, B_NOPE + B_ROPE)
    nope = u3[:, :, :B_NOPE].reshape(w_uq.shape[0], -1)
    rot = u3[:, :, B_NOPE:].reshape(w_uq.shape[0], -1)
    swapped = jnp.concatenate([-u3[:, :, B_NOPE + half:], u3[:, :, B_NOPE:B_NOPE + half]], axis=-1).reshape(w_uq.shape[0], -1)
    return w_in_cat, jnp.concatenate([nope, rot, swapped], axis=1), w_ukv


def _even_fwd(h, w, i, rider=None):
    s = h.shape[0]
    qa, ka, va, vat, c_q, c_kv, kr_blk, xn = _ev_in_fwd(h, w["mix_norm"][i:i + 1], w["ev_in_cat"], name="ev_in")
    cos32, sin32 = _rope_tables(s, 2)
    kro = _rope_fwd(kr_blk[:, :B_ROPE], kr_blk[:, B_ROPE:2 * B_ROPE], cos32, sin32, name="ev_k_rope")
    ss = jnp.stack([_alibi_slopes(), w["ev_sinks"].reshape(-1)])
    oa, lse_a = _swa_fwd_t(qa, ka, vat, ss, scale=A_HEAD_DIM ** -0.5, window=WINDOW, name="swa_fwd")
    cos256, sin256 = _rope_tables(s, 2 * B_HEADS)
    qb, xn_q = _ev_q_fwd(c_q, w["ev_cq_norm"], w["ev_q_cat"], cos256, sin256, name="ev_q_up")
    kb, vb, vbt, xn_kv = _ev_kv_fwd(c_kv, w["ev_ckv_norm"], w["ev_ukv"], kro, name="ev_kv_up")
    (ob, lse_b), rode = _causal_fwd_t(qb, kb, vbt, scale=(B_NOPE + B_ROPE) ** -0.5, name="mla_fwd", tq=512, hb=8, rider=rider)
    attn = jnp.concatenate([oa.reshape(-1, s), ob.reshape(-1, s)], axis=0)
    out = _mm_res_fwd(attn, w["ev_out"], h, scale=1.0, name="ev_out", a_t=True)
    sv = dict(h_in=h, xn=xn, c_q=c_q, c_kv=c_kv, xn_q=xn_q, xn_kv=xn_kv, qa=qa, ka=ka, va=va, oa=oa, lse_a=lse_a,
              ss=ss, qb=qb, kb=kb, vb=vb, ob=ob, lse_b=lse_b, attn=attn, cos32=cos32, sin32=sin32,
              cos256=cos256, sin256=sin256)
    return out, sv, rode


def _even_bwd(dh, w, sv, i, rider=None):
    s = dh.shape[0]
    half = B_ROPE // 2
    g = {}
    dattn = _mm_nt_t(dh, w["ev_out"], name="ev_out_dx")
    g["ev_w_out"] = _mm_tn(sv["attn"], dh, name="ev_out_dw", a_t=True)
    doa = dattn[:A_HEADS * A_HEAD_DIM].reshape(A_HEADS, A_HEAD_DIM, s)
    dob = dattn[A_HEADS * A_HEAD_DIM:].reshape(B_HEADS, B_V, s)
    first, then = rider if isinstance(rider, tuple) else (None, None)
    (dqa, dka, dva, dsink), brought = _swa_bwd_t(sv["qa"], sv["ka"], sv["va"], sv["oa"], doa, sv["lse_a"], sv["ss"],
                                                 scale=A_HEAD_DIM ** -0.5, window=WINDOW, name="swa_bwd", rider=first)
    if then is not None:
        rider = then(brought)
    g["ev_sinks"] = dsink[:, :, 0, 0].reshape(1, A_HEADS)
    (dqb, dkb, dvb), rode = _causal_bwd_t(sv["qb"], sv["kb"], sv["vb"], sv["ob"], dob, sv["lse_b"],
                                          scale=(B_NOPE + B_ROPE) ** -0.5, name="mla_bwd", tq=512, hb=4, rider=rider)
    dyq = _ev_q_merge(dqb, sv["cos256"], sv["sin256"], name="ev_q_merge")
    dwq = _mm_tn(sv["xn_q"], dyq, name="ev_q_up_dw")
    dcq, g["ev_cq_norm"] = _mm_nt_rmsbwd([(dyq, w["ev_q_cat"])], sv["c_q"], w["ev_cq_norm"], None, name="ev_q_up_dx")
    kq = sv["c_q"].shape[1]
    d_nope = dwq[:, :512].reshape(kq, B_HEADS, B_NOPE)
    d_rot = dwq[:, 512:768].reshape(kq, B_HEADS, B_ROPE)
    d_swp = dwq[:, 768:].reshape(kq, B_HEADS, B_ROPE)
    g["ev_w_uq"] = jnp.concatenate([d_nope, d_rot[:, :, :half] + d_swp[:, :, half:], d_rot[:, :, half:] - d_swp[:, :, :half]],
                                   axis=-1).reshape(kq, -1)
    dykv, dkr = _ev_kv_merge(dkb, dvb, sv["cos32"], sv["sin32"], name="ev_kv_merge")
    g["ev_w_ukv"] = _mm_tn(sv["xn_kv"], dykv, name="ev_kv_up_dw")
    dckv, g["ev_ckv_norm"] = _mm_nt_rmsbwd([(dykv, w["ev_ukv"])], sv["c_kv"], w["ev_ckv_norm"], None, name="ev_kv_up_dx")
    dycat = _ev_in_merge(dqa, dka, dva, dcq, dckv, dkr, name="ev_in_merge")
    dwin = _mm_tn(sv["xn"], dycat, name="ev_in_dw")
    base = 1184
    g["ev_w_in"] = jnp.concatenate([dwin[:, :base - B_ROPE],
                                    dwin[:, base - B_ROPE:base - half] + dwin[:, base + half:base + B_ROPE],
                                    dwin[:, base - half:base] - dwin[:, base:base + half]], axis=-1)
    dh_in, dnorm = _mm_nt_rmsbwd([(dycat, w["ev_in_cat"])], sv["h_in"], w["mix_norm"][i:i + 1], dh, name="ev_in_dx")
    return dh_in, dnorm, g, rode


def _odd_fwd(h, w, i, rider=None):
    s = h.shape[0]
    wd = C_HEADS * C_HEAD_DIM
    q, k, v, vt, y_f, xn = _fox_in_fwd(h, w["mix_norm"][i:i + 1], w["od_in_pad"], nheads=C_HEADS, dh=C_HEAD_DIM,
                                       q_ones=(0, 2, 3, 4), k_ones=(1,), name="od_in")
    scale = C_HEAD_DIM ** -0.5
    ft = y_f[:, :C_HEADS].T
    bf = w["od_b_f"].reshape(C_HEADS, 1)
    cb3 = _fox_gate_fwd(ft, bf, out_scale=-1.0 / scale, name="fox_gate_fwd")
    k = k + jnp.pad(cb3.transpose(1, 2, 0), ((0, 0), (0, 0), (C_HEAD_DIM + 2, LANES - C_HEAD_DIM - 5)))
    (o, lse), rode = _causal_fwd_t(q, k, vt, scale=scale, name="fox_fwd", tq=512, hb=16, rider=rider)
    attn = o.reshape(-1, s)
    out = _mm_res_fwd(attn, w["od_out"], h, scale=1.0, name="od_out", a_t=True)
    return out, dict(h_in=h, xn=xn, q=q, k=k, v=v, o=o, lse=lse, ft=ft, bf=bf, attn=attn), rode


def _odd_bwd(dh, w, sv, i, rider=None):
    s = dh.shape[0]
    g = {}
    dattn = _mm_nt_t(dh, w["od_out"], name="od_out_dx")
    g["od_w_out"] = _mm_tn(sv["attn"], dh, name="od_out_dw", a_t=True)
    do = dattn.reshape(C_HEADS, C_HEAD_DIM, s)
    scale = C_HEAD_DIM ** -0.5
    (dq, dk, dv), rode = _causal_bwd_t(sv["q"], sv["k"], sv["v"], sv["o"], do, sv["lse"], scale=scale, name="fox_bwd",
                                       tq=512, hb=4, rider=rider)
    dqkv, sums = _merge_heads(dq, dk, dv, dh=C_HEAD_DIM, q_col=C_HEAD_DIM + 1, k_col=C_HEAD_DIM, name="fox_merge")
    dft, dbf = _fox_gate_bwd(sums[:, :C_HEADS].T, sums[:, C_HEADS:2 * C_HEADS].T, sv["ft"], sv["bf"],
                             inv_scale=1.0 / scale, name="fox_gate_bwd")
    g["od_b_f"] = dbf.reshape(1, C_HEADS)
    wd = C_HEADS * C_HEAD_DIM
    df = jnp.pad(dft.T, ((0, 0), (0, LANES - C_HEADS)))
    g["od_w_in"] = jnp.concatenate([_mm_tn(sv["xn"], dqkv, name="od_in_dw"),
                                    _mm_tn(sv["xn"], df, name="od_in_dwf")[:, :C_HEADS]], axis=-1)
    dh_in, dnorm = _mm_nt_rmsbwd([(dqkv, w["od_in_pad"], 0), (df, w["od_in_pad"], 3 * wd // LANES)],
                                 sv["h_in"], w["mix_norm"][i:i + 1], dh, name="od_in_dx")
    return dh_in, dnorm, g, rode


def _kernel_weights(full, replicated):
    w = dict(replicated)
    _install_weights(w, {(n, i): a for n, per_layer in full.items() for i, a in enumerate(per_layer)})
    return w


def _install_weights(w, got):
    raw = w.setdefault("raw", {})
    raw.update(got)
    for (n, i), a in got.items():
        if n in ("ffa_w_gate_up", "ffa_w_down", "ffb_w_gate_up", "ffb_w_down"):
            w.setdefault(n[:3], {}).setdefault(i, {})["wgu" if n.endswith("gate_up") else "wd"] = a
        elif n in ("ple_w_gate", "ple_w_proj"):
            w.setdefault("ple_gate" if n.endswith("gate") else "ple_proj", {})[i] = a
    if "ev_in_cat" not in w and all((n, 0) in raw for n in ("ev_w_in", "ev_w_uq", "ev_w_ukv", "ev_w_out")):
        w["ev_in_cat"], w["ev_q_cat"], w["ev_ukv"] = _even_weights(raw["ev_w_in", 0], raw["ev_w_uq", 0], raw["ev_w_ukv", 0])
        w["ev_out"] = raw["ev_w_out", 0]
    if "od_in_pad" not in w and all((n, 0) in raw for n in ("od_w_in", "od_w_out")):
        od_in = raw["od_w_in", 0]
        w["od_in_pad"] = jnp.pad(od_in, ((0, 0), (0, (-od_in.shape[1]) % LANES)))
        w["od_out"] = raw["od_w_out", 0]


def _keys(names, layer):
    return tuple((n, layer) for n in names)


_FFA, _FFB, _PLE = ("ffa_w_gate_up", "ffa_w_down"), ("ffb_w_gate_up", "ffb_w_down"), ("ple_w_gate", "ple_w_proj")
_EV, _OD = ("ev_w_in", "ev_w_uq", "ev_w_ukv", "ev_w_out"), ("od_w_in", "od_w_out")
_GATHER_FIRST = _keys(_FFA[:1], 0)
_GATHER_RIDES = {("ffa", 0): _keys(_FFA[1:] + _EV, 0), ("mix", 0): _keys(_FFB + _PLE, 0) + _keys(_FFA[:1], 1),
                 ("ffb", 0): _keys(_FFA[1:], 1), ("ffa", 1): _keys(_OD, 0), ("mix", 1): _keys(_FFB + _PLE, 1)}
_REDUCE_RIDES = {("mix", 1): _keys(_FFB + _PLE, 1), ("mix", 0): _keys(_FFA, 1) + _keys(_OD, 0) + _keys(_FFB + _PLE, 0),
                 ("ffa", 0): _keys(_EV, 0)}
_REDUCE_OWN = ("ffa", 0)
_SWAP_AHEAD = {("ffb", 1): ("mix", 1)}


def _local_step(x, p, tgt, w, ex=None):
    depth = p.shape[0]

    def gather_behind(host, fn, *args):
        keys = None if ex is None else _GATHER_RIDES.get(host)
        if keys is None:
            return fn(*args, None)[:-1]
        done = []

        def install(rode):
            if not done:
                _install_weights(w, ex.gather_finish(keys, rode, name=f"weight_forward_{host[0]}{host[1]}"))
                done.append(True)

        res = fn(*args, ex.gather_rider(keys), install) if fn is _ffn_fwd else fn(*args, ex.gather_rider(keys))
        install(res[-1])
        return res[:-1]

    h = x
    saved = []
    for i in range(depth):
        sv = {}
        h, sv["ffa"] = gather_behind(("ffa", i), _ffn_fwd, h, w["ffa_norm"][i:i + 1], w["ffa"][i], f"ffa{i}")
        h, sv["mix"] = gather_behind(("mix", i), _even_fwd if i % 2 == 0 else _odd_fwd, h, w, i)
        h, sv["ffb"] = gather_behind(("ffb", i), _ffn_fwd, h, w["ffb_norm"][i:i + 1], w["ffb"][i], f"ffb{i}")
        h_in = h
        h, xn, gate, pp = _ple_fwd(h, w["ple_norm"][i:i + 1], w["ple_gate"][i], p[i], w["ple_proj"][i], name=f"ple{i}")
        sv["ple"] = dict(h_in=h_in, xn=xn, gate=gate, pp=pp)
        saved.append(sv)
    loss_vec, dh, d_final = _final_loss(h, w["final_norm"].reshape(1, -1), tgt, name="final_loss")

    per_layer = [dict() for _ in range(depth)]
    mats = {}
    grads = {}

    pending = {}

    def reduce_behind(host, fn, *args):
        keys = None if ex is None else _REDUCE_RIDES.get(host)
        ahead = None if ex is None else _SWAP_AHEAD.get(host)
        if keys is None and ahead is None:
            return fn(*args, None)[:-1]
        if ahead is not None:
            got, ctxs = {}, []

            def note_wd(d_wd):
                got[f"{host[0]}_w_down", host[1]] = d_wd

            def swap_now(brought, d_wgu):
                got[f"{host[0]}_w_gate_up", host[1]] = d_wgu
                swap, ctx = ex.swap_rider(_REDUCE_RIDES[ahead], {**mats, **got})
                ctxs.append(ctx)
                return swap

            def stash(brought):
                pending[ahead] = ex.after_swap(ctxs[0], brought)

            return fn(*args, None, (note_wd, swap_now, stash))[:-1]
        states = []
        if fn is _even_bwd:
            swap, ctx = ex.swap_rider(keys, mats)

            def then(brought):
                states.append(ex.after_swap(ctx, brought))
                return states[0][0]

            res = fn(*args, (swap, then))
        else:
            states.append(pending.pop(host, None) or ex.reduce_begin(keys, mats, tag=f"{host[0]}{host[1]}"))
            if fn is _ffn_bwd and host == _REDUCE_OWN:
                own = []

                def wd_ready(d_wd):
                    own.append(ex.reduce_begin(_keys(_FFA[1:], 0), {("ffa_w_down", 0): d_wd}, tag="own_wd"))
                    return own[0][0]

                def wgu_ready(brought, d_wgu):
                    ex.reduce_finish(own[0], brought)
                    own.append(ex.reduce_begin(_keys(_FFA[:1], 0), {("ffa_w_gate_up", 0): d_wgu}, tag="own_wgu"))
                    return own[1][0]

                res = fn(*args, states[0][0], (wd_ready, wgu_ready, lambda brought: ex.reduce_finish(own[1], brought)))
            else:
                res = fn(*args, states[0][0])
        ex.reduce_finish(states[0], res[-1])
        return res[:-1]

    for i in reversed(range(depth)):
        sv, gl = saved[i], per_layer[i]
        dz, dpp = _ple_bwd_elem(dh, sv["ple"]["gate"], sv["ple"]["pp"], name=f"ple{i}_bwd")
        mats["ple_w_gate", i] = _mm_tn(sv["ple"]["xn"], dz, name=f"ple{i}_dwg")
        mats["ple_w_proj", i] = _mm_tn(p[i], dpp, name=f"ple{i}_dwp")
        dh, gl["ple_norm"] = _mm_nt_rmsbwd([(dz, w["ple_gate"][i])], sv["ple"]["h_in"], w["ple_norm"][i:i + 1], dh,
                                           name=f"ple{i}_dx")
        dh, gl["ffb_norm"], mats["ffb_w_gate_up", i], mats["ffb_w_down", i] = reduce_behind(
            ("ffb", i), _ffn_bwd, dh, w["ffb_norm"][i:i + 1], w["ffb"][i], sv["ffb"], f"ffb{i}")
        dh, gl["mix_norm"], gm = reduce_behind(("mix", i), _even_bwd if i % 2 == 0 else _odd_bwd, dh, w, sv["mix"], i)
        for n, g in gm.items():
            if n in REPLICATED:
                grads[n] = g
            else:
                mats[n, 0] = g
        dh, gl["ffa_norm"], mats["ffa_w_gate_up", i], mats["ffa_w_down", i] = reduce_behind(
            ("ffa", i), _ffn_bwd, dh, w["ffa_norm"][i:i + 1], w["ffa"][i], sv["ffa"], f"ffa{i}")
    grads["final_norm"] = d_final.reshape(-1)
    for n in ("ffa_norm", "mix_norm", "ffb_norm", "ple_norm"):
        grads[n] = jnp.concatenate([per_layer[i][n] for i in range(depth)], axis=0)
    if ex is None:
        for n, _ in SHARDED:
            grads[n] = [mats[n, i] for i in range(depth) if (n, i) in mats]
    return loss_vec[0, 0], dh, grads


def _cut_mode(local_shape, axis, ncols):
    return "cols" if axis == 2 and ncols % LANES == 0 else "blk"


class _Exchange:
    def __init__(self, wts):
        self.place = jnp.stack([2 * lax.axis_index("x") + lax.axis_index("y"), lax.axis_index("c")]).astype(jnp.int32)
        self.info = {}
        for n, axis in SHARDED:
            wb = wts[n].astype(BF16)
            mode = _cut_mode(wb.shape, axis, wb.shape[2])
            for i in range(wb.shape[0]):
                self.info[n, i] = dict(shard=wb[i], mode=mode, axis=axis)
        self.halves = {}

    def _modes(self, keys):
        return [self.info[k]["mode"] for k in keys]

    def gather_rider(self, keys):
        return _gather_rider([self.info[k]["shard"] for k in keys], self._modes(keys))

    def gather_finish(self, keys, landed, *, name):
        outs = _gather_forward(landed, [self.info[k]["shard"].shape for k in keys], self._modes(keys), name=name)
        got = {}
        for k, dst in zip(keys, outs):
            if self.info[k]["mode"] == "blk":
                dst = dst.reshape(-1, dst.shape[2]) if self.info[k]["axis"] == 1 else jnp.moveaxis(dst, 0, 1).reshape(dst.shape[1], -1)
            got[k] = dst
        return got

    def gather(self, keys, *, name):
        return self.gather_finish(keys, _run_rider(self.gather_rider(keys), name=name), name=name + "_forward")

    def swap_rider(self, keys, mats):
        modes = self._modes(keys)
        arrs = []
        for k in keys:
            g2, (rr, cc) = mats[k], self.info[k]["shard"].shape
            if self.info[k]["mode"] == "blk":
                g2 = g2.reshape(N_CHIPS, rr, cc) if self.info[k]["axis"] == 1 else g2.reshape(rr, N_CHIPS, cc).transpose(1, 0, 2)
            arrs.append(g2)
        return _swap_rider(arrs, modes), (keys, modes, arrs)

    def after_swap(self, ctx, landed):
        keys, modes, arrs = ctx
        parts = []
        for (n, i), m, a, l in zip(keys, modes, arrs, landed):
            pt = _rs_pair_add(_blk_view(a, m), _blk_view(l, m), self.place, name=f"rs_pair_add_{n}{i}")
            parts.append(pt[0] if m == "cols" else pt)
        return _exchange_rider(parts, modes), keys, parts

    def reduce_begin(self, keys, mats, *, tag):
        rider, ctx = self.swap_rider(keys, mats)
        return self.after_swap(ctx, _run_rider(rider, name=f"rs_pair_swap_{tag}"))

    def reduce_finish(self, state, landed):
        _, keys, parts = state
        for (n, i), m, pt, l in zip(keys, self._modes(keys), parts, landed):
            self.halves[n, i] = _rs_chip_sum(pt, l, m, self.place, name=f"rs_chip_sum_{n}{i}")

    def reduce(self, keys, mats, *, tag):
        state = self.reduce_begin(keys, mats, tag=tag)
        self.reduce_finish(state, _run_rider(state[0], name=f"rs_chip_exchange_{tag}"))

    def join(self, wts):
        keys = list(self.info)
        joined = dict(zip(keys, _rs_pair_join([self.halves[k] for k in keys], name="rs_pair_join")))
        return {n: jnp.stack([joined[n, i] for i in range(wts[n].shape[0])]).reshape(wts[n].shape) for n, _ in SHARDED}


def _small_rows(vals):
    rows = []
    for n in REPLICATED:
        v = vals[n].reshape(-1)
        rows.append(jnp.pad(v, (0, (-v.shape[0]) % FLAT_COLS)).reshape(-1, FLAT_COLS))
    out = jnp.concatenate(rows, axis=0)
    return jnp.pad(out, ((0, (-out.shape[0]) % 8), (0, 0)))


def kernel(x, p, ffa_norm, ffa_w_gate_up, ffa_w_down, mix_norm, ffb_norm, ffb_w_gate_up, ffb_w_down, ple_norm, ple_w_gate, ple_w_proj, ev_w_in, ev_sinks, ev_cq_norm, ev_w_uq, ev_ckv_norm, ev_w_ukv, ev_w_out, od_w_in, od_b_f, od_w_out, final_norm, loss_target, m_ffa_norm, m_ffa_w_gate_up, m_ffa_w_down, m_mix_norm, m_ffb_norm, m_ffb_w_gate_up, m_ffb_w_down, m_ple_norm, m_ple_w_gate, m_ple_w_proj, m_ev_w_in, m_ev_sinks, m_ev_cq_norm, m_ev_w_uq, m_ev_ckv_norm, m_ev_w_ukv, m_ev_w_out, m_od_w_in, m_od_b_f, m_od_w_out, m_final_norm, v_ffa_norm, v_ffa_w_gate_up, v_ffa_w_down, v_mix_norm, v_ffb_norm, v_ffb_w_gate_up, v_ffb_w_down, v_ple_norm, v_ple_w_gate, v_ple_w_proj, v_ev_w_in, v_ev_sinks, v_ev_cq_norm, v_ev_w_uq, v_ev_ckv_norm, v_ev_w_ukv, v_ev_w_out, v_od_w_in, v_od_b_f, v_od_w_out, v_final_norm):
    env = dict(locals())
    wts = {n: env[n] for n in WEIGHT_ORDER}
    mom1 = {n: env["m_" + n] for n in WEIGHT_ORDER}
    mom2 = {n: env["v_" + n] for n in WEIGHT_ORDER}
    ex = _Exchange(wts)

    w = {n: wts[n] for n in REPLICATED}
    _install_weights(w, ex.gather(_GATHER_FIRST, name="weight_gather_first"))

    loss_part, grad_x, grads = _local_step(x[0], p[:, 0], loss_target[0], w, ex)
    loss = lax.psum(loss_part, ("x", "y", "c"))
    gout = ex.join(wts)
    small = _allreduce_small(_small_rows(grads), name="small_allreduce")
    r0 = 0
    for n in REPLICATED:
        size = int(np.prod(wts[n].shape))
        nr = -(-size // FLAT_COLS)
        gout[n] = small[r0:r0 + nr].reshape(-1)[:size].reshape(wts[n].shape)
        r0 += nr

    delta, new_m, new_v = {}, {}, {}
    for n in WEIGHT_ORDER:
        delta[n], new_m[n], new_v[n] = _adamw(wts[n], gout[n], mom1[n], mom2[n], name="adamw_" + n)
    return (loss, grad_x[None], *[gout[n] for n in WEIGHT_ORDER], *[delta[n] for n in WEIGHT_ORDER],
            *[new_m[n] for n in WEIGHT_ORDER], *[new_v[n] for n in WEIGHT_ORDER])
```

```python
import functools
import math

import numpy as np
import jax
import jax.numpy as jnp
from jax import lax
from jax.experimental import pallas as pl
from jax.experimental.pallas import tpu as pltpu

F32 = jnp.float32
BF16 = jnp.bfloat16
NT = (((1,), (1,)), ((), ()))
TN = (((0,), (0,)), ((), ()))
MESH = pl.DeviceIdType.MESH

RMS_EPS = 1e-6
FFN_RES_SCALE = 0.5
A_HEADS, A_KV_HEADS, A_HEAD_DIM, WINDOW = 8, 2, 64, 128
B_HEADS, B_Q_LORA, B_KV_LORA, B_NOPE, B_ROPE, B_V = 8, 256, 128, 64, 32, 64
ROPE_THETA = 10000.0
C_HEADS, C_HEAD_DIM = 16, 64
ADAM_LR, ADAM_B1, ADAM_B2, ADAM_EPS, ADAM_WD, ADAM_STEP = 0.001, 0.9, 0.999, 1e-08, 0.01, 10

N_CHIPS = 4
LANES = 128
FLAT_COLS = 1024
MASK_VALUE = -1e30
VMEM_LIMIT = 48 * 2**20

SHARDED = (
    ("ffa_w_gate_up", 2), ("ffa_w_down", 1), ("ffb_w_gate_up", 2), ("ffb_w_down", 1),
    ("ple_w_gate", 1), ("ple_w_proj", 2), ("ev_w_in", 2), ("ev_w_uq", 2), ("ev_w_ukv", 2),
    ("ev_w_out", 1), ("od_w_in", 2), ("od_w_out", 1))
REPLICATED = ("ffa_norm", "mix_norm", "ffb_norm", "ple_norm", "final_norm",
              "ev_sinks", "ev_cq_norm", "ev_ckv_norm", "od_b_f")
WEIGHT_ORDER = ("ffa_norm", "ffa_w_gate_up", "ffa_w_down", "mix_norm", "ffb_norm", "ffb_w_gate_up",
                "ffb_w_down", "ple_norm", "ple_w_gate", "ple_w_proj", "ev_w_in", "ev_sinks",
                "ev_cq_norm", "ev_w_uq", "ev_ckv_norm", "ev_w_ukv", "ev_w_out", "od_w_in", "od_b_f",
                "od_w_out", "final_norm")


def _cp(*sem):
    return pltpu.CompilerParams(dimension_semantics=sem, vmem_limit_bytes=VMEM_LIMIT)


def _sigmoid(z):
    return 1.0 / (1.0 + jnp.exp(-z))


def _rms_stats(xv):
    r = lax.rsqrt(jnp.mean(xv * xv, axis=-1, keepdims=True) + RMS_EPS)
    return r, xv * r


def _rms_bwd(dxn, xv, g):
    r, xhat = _rms_stats(xv)
    u = dxn * g
    dx = r * (u - xhat * jnp.mean(u * xhat, axis=-1, keepdims=True))
    return dx, dxn * xhat


def _col_tile(k_rows, n, budget_bytes=6 * 2**20):
    if k_rows * n * 4 <= budget_bytes or n % LANES:
        return n
    units = n // LANES
    best = LANES
    for d in range(1, units + 1):
        if units % d == 0 and k_rows * d * LANES * 4 <= budget_bytes:
            best = d * LANES
    return best


def _row_tile(rows, cols, target_elems=2**18):
    if rows * cols <= target_elems or rows % 8:
        return rows
    best = 8
    for d in range(8, rows + 1, 8):
        if rows % d == 0 and d * cols <= target_elems:
            best = d
    return best


def _fox_in_fwd(x, g, w, *, nheads, dh, q_ones, k_ones, name, tm=512):
    s, k = x.shape
    n = w.shape[1]
    wd = nheads * dh
    spare = LANES - dh

    def body(x_ref, g_ref, w_ref, q_ref, k_ref, v_ref, vt_ref, f_ref, xn_ref):
        _, xhat = _rms_stats(x_ref[...])
        xn = (xhat * g_ref[...]).astype(BF16)
        xn_ref[...] = xn
        y = jnp.dot(xn, w_ref[...], preferred_element_type=F32)
        f_ref[...] = y[:, 3 * wd:]
        lane = lax.broadcasted_iota(jnp.int32, (tm, spare), 1)

        def fill(cols):
            return functools.reduce(jnp.logical_or, [lane == c for c in cols]).astype(F32)

        q_fill, k_fill = fill(q_ones), fill(k_ones)
        for h in range(nheads):
            q_ref[h] = jnp.concatenate([y[:, h * dh:(h + 1) * dh], q_fill], axis=-1).astype(BF16)
            k_ref[h] = jnp.concatenate([y[:, wd + h * dh:wd + (h + 1) * dh], k_fill], axis=-1).astype(BF16)
            vh = y[:, 2 * wd + h * dh:2 * wd + (h + 1) * dh]
            v_ref[h] = vh.astype(BF16)
            vt_ref[h] = vh.T.astype(BF16)

    wide = pl.BlockSpec((nheads, tm, LANES), lambda i: (0, i, 0))
    return pl.pallas_call(
        body, name=name, grid=(s // tm,),
        in_specs=[pl.BlockSpec((tm, k), lambda i: (i, 0)), pl.BlockSpec((1, k), lambda i: (0, 0)),
                  pl.BlockSpec((k, n), lambda i: (0, 0))],
        out_specs=[wide, wide, pl.BlockSpec((nheads, tm, dh), lambda i: (0, i, 0)),
                   pl.BlockSpec((nheads, dh, tm), lambda i: (0, 0, i)), pl.BlockSpec((tm, LANES), lambda i: (i, 0)),
                   pl.BlockSpec((tm, k), lambda i: (i, 0))],
        out_shape=[jax.ShapeDtypeStruct((nheads, s, LANES), BF16)] * 2
        + [jax.ShapeDtypeStruct((nheads, s, dh), BF16), jax.ShapeDtypeStruct((nheads, dh, s), BF16),
           jax.ShapeDtypeStruct((s, LANES), F32), jax.ShapeDtypeStruct((s, k), BF16)],
        compiler_params=_cp("arbitrary"))(x, g, w)


def _merge_heads(dq, dk, dv, *, dh, q_col, k_col, name, tm=512):
    nheads, s, _ = dq.shape

    def body(dq_ref, dk_ref, dv_ref, o_ref, cols_ref):
        pieces = [dq_ref[h][:, :dh] for h in range(nheads)] + [dk_ref[h][:, :dh] for h in range(nheads)]
        pieces += [dv_ref[h] for h in range(nheads)]
        o_ref[...] = jnp.concatenate(pieces, axis=-1)
        lane = lax.broadcasted_iota(jnp.int32, (tm, LANES), 1)
        cols = jnp.zeros((tm, LANES), F32)
        for h in range(nheads):
            cols = jnp.where(lane == h, jnp.broadcast_to(dq_ref[h][:, q_col:q_col + 1], (tm, LANES)), cols)
            cols = jnp.where(lane == nheads + h, jnp.broadcast_to(dk_ref[h][:, k_col:k_col + 1], (tm, LANES)), cols)
        cols_ref[...] = cols

    wide = pl.BlockSpec((nheads, tm, LANES), lambda i: (0, i, 0))
    return pl.pallas_call(
        body, name=name, grid=(s // tm,),
        in_specs=[wide, wide, pl.BlockSpec((nheads, tm, dh), lambda i: (0, i, 0))],
        out_specs=[pl.BlockSpec((tm, 3 * nheads * dh), lambda i: (i, 0)), pl.BlockSpec((tm, LANES), lambda i: (i, 0))],
        out_shape=[jax.ShapeDtypeStruct((s, 3 * nheads * dh), F32), jax.ShapeDtypeStruct((s, LANES), F32)],
        compiler_params=_cp("arbitrary"))(dq, dk, dv)


def _row_call(body, n_rows, ins, outs, *, name, tm=512):
    def spec(a, axis):
        shape = a.shape
        if axis is None:
            return pl.BlockSpec(shape, lambda i: (0,) * len(shape))
        blk = tuple(tm if d == axis else n for d, n in enumerate(shape))
        return pl.BlockSpec(blk, lambda i: tuple(i if d == axis else 0 for d in range(len(shape))))

    return pl.pallas_call(
        body, name=name, grid=(n_rows // tm,), in_specs=[spec(a, ax) for a, ax in ins],
        out_specs=[spec(a, ax) for a, ax in outs], out_shape=[a for a, _ in outs],
        compiler_params=_cp("arbitrary"))(*[a for a, _ in ins])


def _sds(shape, dtype):
    return jax.ShapeDtypeStruct(shape, dtype)


def _ev_in_fwd(x, g, w, *, name):
    s, k = x.shape
    d = A_HEAD_DIM

    def body(x_ref, g_ref, w_ref, q_ref, k_ref, v_ref, vt_ref, cq_ref, ckv_ref, kr_ref, xn_ref):
        _, xhat = _rms_stats(x_ref[...])
        xn = (xhat * g_ref[...]).astype(BF16)
        xn_ref[...] = xn
        y = jnp.dot(xn, w_ref[...], preferred_element_type=F32)
        for h in range(A_HEADS):
            q_ref[h] = y[:, h * d:(h + 1) * d].astype(BF16)
        for h in range(A_KV_HEADS):
            k_ref[h] = y[:, 512 + h * d:512 + (h + 1) * d].astype(BF16)
            vh = y[:, 640 + h * d:640 + (h + 1) * d]
            v_ref[h] = vh.astype(BF16)
            vt_ref[h] = vh.T.astype(BF16)
        cq_ref[...] = y[:, 768:1024]
        ckv_ref[...] = y[:, 1024:1152]
        kr_ref[...] = y[:, 1152:1280]

    return _row_call(
        body, s, [(x, 0), (g, None), (w, None)],
        [(_sds((A_HEADS, s, d), BF16), 1), (_sds((A_KV_HEADS, s, d), BF16), 1), (_sds((A_KV_HEADS, s, d), BF16), 1),
         (_sds((A_KV_HEADS, d, s), BF16), 2), (_sds((s, B_Q_LORA), F32), 0), (_sds((s, B_KV_LORA), F32), 0),
         (_sds((s, LANES), F32), 0), (_sds((s, k), BF16), 0)], name=name)


def _ev_q_fwd(x, g, w, cos, sin, *, name):
    s, k = x.shape
    rot = B_HEADS * B_ROPE

    def body(x_ref, g_ref, w_ref, c_ref, s_ref, q_ref, xn_ref):
        _, xhat = _rms_stats(x_ref[...])
        xn = (xhat * g_ref[...]).astype(BF16)
        xn_ref[...] = xn
        y = jnp.dot(xn, w_ref[...], preferred_element_type=F32)
        ro = y[:, 512:512 + rot] * c_ref[...] + y[:, 512 + rot:] * s_ref[...]
        zero = jnp.zeros((y.shape[0], LANES - B_NOPE - B_ROPE), F32)
        for h in range(B_HEADS):
            q_ref[h] = jnp.concatenate([y[:, h * B_NOPE:(h + 1) * B_NOPE], ro[:, h * B_ROPE:(h + 1) * B_ROPE], zero],
                                       axis=-1).astype(BF16)

    return _row_call(body, s, [(x, 0), (g, None), (w, None), (cos, 0), (sin, 0)],
                     [(_sds((B_HEADS, s, LANES), BF16), 1), (_sds((s, k), BF16), 0)], name=name)


def _ev_kv_fwd(x, g, w, kro, *, name):
    s, k = x.shape
    per = B_NOPE + B_V

    def body(x_ref, g_ref, w_ref, kr_ref, k_ref, v_ref, vt_ref, xn_ref):
        _, xhat = _rms_stats(x_ref[...])
        xn = (xhat * g_ref[...]).astype(BF16)
        xn_ref[...] = xn
        y = jnp.dot(xn, w_ref[...], preferred_element_type=F32)
        kr = kr_ref[...]
        zero = jnp.zeros((y.shape[0], LANES - B_NOPE - B_ROPE), F32)
        for h in range(B_HEADS):
            k_ref[h] = jnp.concatenate([y[:, h * per:h * per + B_NOPE], kr, zero], axis=-1).astype(BF16)
            vh = y[:, h * per + B_NOPE:(h + 1) * per]
            v_ref[h] = vh.astype(BF16)
            vt_ref[h] = vh.T.astype(BF16)

    return _row_call(body, s, [(x, 0), (g, None), (w, None), (kro, 0)],
                     [(_sds((B_HEADS, s, LANES), BF16), 1), (_sds((B_HEADS, s, B_V), BF16), 1),
                      (_sds((B_HEADS, B_V, s), BF16), 2), (_sds((s, k), BF16), 0)], name=name)


def _ev_q_merge(dq, cos, sin, *, name):
    nh, s, _ = dq.shape

    def body(dq_ref, c_ref, s_ref, o_ref):
        dro = jnp.concatenate([dq_ref[h][:, B_NOPE:B_NOPE + B_ROPE] for h in range(nh)], axis=-1)
        o_ref[...] = jnp.concatenate([dq_ref[h][:, :B_NOPE] for h in range(nh)] + [dro * c_ref[...], dro * s_ref[...]], axis=-1)

    return _row_call(body, s, [(dq, 1), (cos, 0), (sin, 0)], [(_sds((s, 2 * nh * B_NOPE), F32), 0)], name=name)[0]


def _ev_kv_merge(dk, dv, cos, sin, *, name):
    nh, s, _ = dk.shape

    def body(dk_ref, dv_ref, c_ref, s_ref, o_ref, kr_ref):
        pieces = []
        tot = None
        for h in range(nh):
            pieces += [dk_ref[h][:, :B_NOPE], dv_ref[h]]
            rot = dk_ref[h][:, B_NOPE:B_NOPE + B_ROPE]
            tot = rot if tot is None else tot + rot
        o_ref[...] = jnp.concatenate(pieces, axis=-1)
        kr_ref[...] = jnp.concatenate([tot * c_ref[...], tot * s_ref[...], jnp.zeros((tot.shape[0], LANES - 2 * B_ROPE), F32)],
                                      axis=-1)

    return _row_call(body, s, [(dk, 1), (dv, 1), (cos, 0), (sin, 0)],
                     [(_sds((s, nh * (B_NOPE + B_V)), F32), 0), (_sds((s, LANES), F32), 0)], name=name)


def _ev_in_merge(dq, dk, dvt, dcq, dckv, dkr, *, name):
    s = dcq.shape[0]

    def body(dq_ref, dk_ref, dvt_ref, cq_ref, ckv_ref, kr_ref, o_ref):
        pieces = [dq_ref[h] for h in range(A_HEADS)] + [dk_ref[h] for h in range(A_KV_HEADS)]
        pieces += [dvt_ref[h].T for h in range(A_KV_HEADS)] + [cq_ref[...], ckv_ref[...], kr_ref[...]]
        o_ref[...] = jnp.concatenate(pieces, axis=-1)

    return _row_call(body, s, [(dq, 1), (dk, 1), (dvt, 2), (dcq, 0), (dckv, 0), (dkr, 0)],
                     [(_sds((s, 1280), F32), 0)], name=name)[0]


def _ffn_up(x, g, wgu, *, name, tm=512, rider=None):
    s, k = x.shape
    f = wgu.shape[1] // 2
    tn = _col_tile(k, f)
    nj = f // tn

    def body(x_ref, g_ref, wg_ref, wu_ref, dgate_ref, dup_ref, act_ref, xn_ref, xn_sc):
        @pl.when(pl.program_id(1) == 0)
        def _():
            _, xhat = _rms_stats(x_ref[...])
            xn = (xhat * g_ref[...]).astype(BF16)
            xn_sc[...] = xn
            xn_ref[...] = xn

        xn = xn_sc[...]
        gg = jnp.dot(xn, wg_ref[...], preferred_element_type=F32)
        uu = jnp.dot(xn, wu_ref[...], preferred_element_type=F32)
        sg = _sigmoid(gg)
        silu = gg * sg
        dgate_ref[...] = (uu * (sg * (1.0 + gg * (1.0 - sg)))).astype(BF16)
        dup_ref[...] = silu.astype(BF16)
        act_ref[...] = (silu * uu).astype(BF16)

    tile = pl.BlockSpec((tm, tn), lambda i, j: (i, j))
    return _call_with_rider(
        body, rider, name=name, grid=(s // tm, nj),
        in_specs=[pl.BlockSpec((tm, k), lambda i, j: (i, 0)), pl.BlockSpec((1, k), lambda i, j: (0, 0)),
                  pl.BlockSpec((k, tn), lambda i, j: (0, j)), pl.BlockSpec((k, tn), lambda i, j: (0, j + nj))],
        out_specs=[tile, tile, tile, pl.BlockSpec((tm, k), lambda i, j: (i, 0))],
        out_shape=[jax.ShapeDtypeStruct((s, f), BF16)] * 3 + [jax.ShapeDtypeStruct((s, k), BF16)],
        scratch_shapes=[pltpu.VMEM((tm, k), BF16)],
        compiler_params=_cp("arbitrary", "arbitrary"), args=(x, g, wgu, wgu))


def _mm_res_fwd(a, w, res, *, scale, name, tm=512, a_t=False):
    k, n = w.shape
    s = res.shape[0]

    def body(a_ref, w_ref, r_ref, o_ref):
        prod = (lax.dot_general(a_ref[...], w_ref[...], TN, preferred_element_type=F32) if a_t
                else jnp.dot(a_ref[...], w_ref[...], preferred_element_type=F32))
        o_ref[...] = r_ref[...] + scale * prod

    a_spec = pl.BlockSpec((k, tm), lambda i: (0, i)) if a_t else pl.BlockSpec((tm, k), lambda i: (i, 0))
    return pl.pallas_call(
        body, name=name, grid=(s // tm,),
        in_specs=[a_spec, pl.BlockSpec((k, n), lambda i: (0, 0)),
                  pl.BlockSpec((tm, n), lambda i: (i, 0))],
        out_specs=pl.BlockSpec((tm, n), lambda i: (i, 0)),
        out_shape=jax.ShapeDtypeStruct((s, n), F32),
        compiler_params=_cp("arbitrary"))(a, w, res)


def _ffn_down_bwd(dh, wd, dact_dgate, dact_dup, *, scale, name, tm=512, rider=None):
    s, d = dh.shape
    f = wd.shape[0]
    tn = _col_tile(d, f)

    def body(dh_ref, wd_ref, fg_ref, fu_ref, dg_ref, du_ref):
        dhb = (dh_ref[...] * scale).astype(BF16)
        da = lax.dot_general(dhb, wd_ref[...], NT, preferred_element_type=F32)
        dg_ref[...] = (da * fg_ref[...].astype(F32)).astype(BF16)
        du_ref[...] = (da * fu_ref[...].astype(F32)).astype(BF16)

    tile = pl.BlockSpec((tm, tn), lambda i, j: (i, j))
    return _call_with_rider(
        body, rider, name=name, grid=(s // tm, f // tn),
        in_specs=[pl.BlockSpec((tm, d), lambda i, j: (i, 0)), pl.BlockSpec((tn, d), lambda i, j: (j, 0)), tile, tile],
        out_specs=[tile, tile],
        out_shape=[jax.ShapeDtypeStruct((s, f), BF16)] * 2, scratch_shapes=[],
        compiler_params=_cp("arbitrary", "arbitrary"), args=(dh, wd, dact_dgate, dact_dup))


def _mm_tn(a, bs, *, name, b_scale=1.0, ts=512, rider=None, a_t=False):
    bs = list(bs) if isinstance(bs, (list, tuple)) else [bs]
    k, s = a.shape if a_t else a.shape[::-1]
    n = bs[0].shape[1]
    tn = _col_tile(k, n, 12 * 2**20)
    per = n // tn

    def body(a_ref, *refs):
        b_refs, o_ref = refs[:-1], refs[-1]
        j = pl.program_id(0)

        @pl.when(pl.program_id(1) == 0)
        def _():
            o_ref[...] = jnp.zeros_like(o_ref)

        for m, b_ref in enumerate(b_refs):
            def acc(b_ref=b_ref):
                bv = b_ref[...]
                if b_scale != 1.0:
                    bv = bv * b_scale
                av = a_ref[...].astype(BF16)
                o_ref[...] += (jnp.dot(av, bv.astype(BF16), preferred_element_type=F32) if a_t
                               else lax.dot_general(av, bv.astype(BF16), TN, preferred_element_type=F32))

            if len(b_refs) == 1:
                acc()
            else:
                pl.when(jnp.logical_and(j >= m * per, j < (m + 1) * per))(acc)

    def b_spec(m):
        def idx(j, t):
            mine = jnp.logical_and(j >= m * per, j < (m + 1) * per)
            return (jnp.where(mine, t, 0), jnp.clip(j - m * per, 0, per - 1))
        return pl.BlockSpec((ts, tn), idx)

    (out,), rode = _call_with_rider(
        body, rider, name=name, grid=(per * len(bs), s // ts),
        in_specs=[pl.BlockSpec((k, ts), lambda j, t: (0, t)) if a_t else pl.BlockSpec((ts, k), lambda j, t: (t, 0))]
        + [b_spec(m) for m in range(len(bs))],
        out_specs=[pl.BlockSpec((k, tn), lambda j, t: (0, j))],
        out_shape=[jax.ShapeDtypeStruct((k, n * len(bs)), F32)], scratch_shapes=[],
        compiler_params=_cp("arbitrary", "arbitrary"), args=(a, *bs))
    return out if rider is None else (out, rode)


def _mm_nt_t(dy, w, *, name, tm=512):
    s, n = dy.shape
    k = w.shape[0]

    def body(dy_ref, w_ref, o_ref):
        o_ref[...] = lax.dot_general(w_ref[...], dy_ref[...].astype(BF16), NT, preferred_element_type=F32).astype(BF16)

    return pl.pallas_call(
        body, name=name, grid=(s // tm,),
        in_specs=[pl.BlockSpec((tm, n), lambda i: (i, 0)), pl.BlockSpec((k, n), lambda i: (0, 0))],
        out_specs=pl.BlockSpec((k, tm), lambda i: (0, i)),
        out_shape=jax.ShapeDtypeStruct((k, s), BF16),
        compiler_params=_cp("arbitrary"))(dy, w)


def _mm_nt_rmsbwd(pairs, x, g, dres, *, name, tm=512, rider=None):
    s, k = x.shape
    npairs = len(pairs)
    pairs = [pr if len(pr) == 3 else (pr[0], pr[1], 0) for pr in pairs]

    def body(*refs):
        dy_refs = refs[0:2 * npairs:2]
        w_refs = refs[1:2 * npairs:2]
        rest = refs[2 * npairs:]
        x_ref, g_ref = rest[0], rest[1]
        if dres is None:
            dx_ref, dg_ref = rest[2], rest[3]
        else:
            dres_ref, dx_ref, dg_ref = rest[2], rest[3], rest[4]
        dxn = None
        for dy_ref, w_ref in zip(dy_refs, w_refs):
            t = lax.dot_general(dy_ref[...].astype(BF16), w_ref[...], NT, preferred_element_type=F32)
            dxn = t if dxn is None else dxn + t
        dx, dgrow = _rms_bwd(dxn, x_ref[...], g_ref[...])
        if dres is not None:
            dx = dx + dres_ref[...]
        dx_ref[...] = dx

        @pl.when(pl.program_id(0) == 0)
        def _():
            dg_ref[...] = jnp.zeros_like(dg_ref)

        dg_ref[...] += jnp.sum(dgrow, axis=0, keepdims=True)

    in_specs, args = [], []
    for dy, w, cb in pairs:
        n = dy.shape[1]
        in_specs += [pl.BlockSpec((tm, n), lambda i: (i, 0)),
                     pl.BlockSpec((k, n), lambda i, cb=cb: (0, cb), pipeline_mode=pl.Buffered(1))]
        args += [dy, w]
    row = pl.BlockSpec((tm, k), lambda i: (i, 0))
    vec = pl.BlockSpec((1, k), lambda i: (0, 0))
    in_specs += [row, vec]
    args += [x, g]
    if dres is not None:
        in_specs.append(row)
        args.append(dres)
    (dx, dgain), rode = _call_with_rider(
        body, rider, name=name, grid=(s // tm,), in_specs=in_specs, out_specs=[row, vec],
        out_shape=[jax.ShapeDtypeStruct((s, k), F32), jax.ShapeDtypeStruct((1, k), F32)], scratch_shapes=[],
        compiler_params=_cp("arbitrary"), args=args)
    return (dx, dgain) if rider is None else (dx, dgain, rode)


def _ple_fwd(h, g, wg, p, wp, *, name, tm=512):
    s, d = h.shape
    pd = p.shape[1]

    def body(h_ref, g_ref, wg_ref, p_ref, wp_ref, o_ref, xn_ref, gate_ref, pp_ref):
        hv = h_ref[...]
        _, xhat = _rms_stats(hv)
        xn = (xhat * g_ref[...]).astype(BF16)
        xn_ref[...] = xn
        gate = _sigmoid(jnp.dot(xn, wg_ref[...], preferred_element_type=F32))
        pp = jnp.dot(p_ref[...].astype(BF16), wp_ref[...], preferred_element_type=F32)
        gate_ref[...] = gate.astype(BF16)
        pp_ref[...] = pp.astype(BF16)
        o_ref[...] = hv + gate * pp

    row = pl.BlockSpec((tm, d), lambda i: (i, 0))
    return pl.pallas_call(
        body, name=name, grid=(s // tm,),
        in_specs=[row, pl.BlockSpec((1, d), lambda i: (0, 0)), pl.BlockSpec((d, d), lambda i: (0, 0)),
                  pl.BlockSpec((tm, pd), lambda i: (i, 0)), pl.BlockSpec((pd, d), lambda i: (0, 0))],
        out_specs=[row, row, row, row],
        out_shape=[jax.ShapeDtypeStruct((s, d), F32)] + [jax.ShapeDtypeStruct((s, d), BF16)] * 3,
        compiler_params=_cp("arbitrary"))(h, g, wg, p, wp)


def _ple_bwd_elem(dh, gate, pp, *, name, tm=512):
    s, d = dh.shape

    def body(dh_ref, gate_ref, pp_ref, dz_ref, dpp_ref):
        dhv = dh_ref[...]
        gt = gate_ref[...].astype(F32)
        dz_ref[...] = (dhv * pp_ref[...].astype(F32) * (gt * (1.0 - gt))).astype(BF16)
        dpp_ref[...] = (dhv * gt).astype(BF16)

    row = pl.BlockSpec((tm, d), lambda i: (i, 0))
    return pl.pallas_call(
        body, name=name, grid=(s // tm,), in_specs=[row, row, row], out_specs=[row, row],
        out_shape=[jax.ShapeDtypeStruct((s, d), BF16)] * 2,
        compiler_params=_cp("arbitrary"))(dh, gate, pp)


def _final_loss(h, g, tgt, *, name, tm=512):
    s, d = h.shape

    def body(h_ref, g_ref, t_ref, loss_ref, dh_ref, dg_ref):
        @pl.when(pl.program_id(0) == 0)
        def _():
            loss_ref[...] = jnp.zeros_like(loss_ref)
            dg_ref[...] = jnp.zeros_like(dg_ref)

        hv = h_ref[...]
        gv = g_ref[...]
        _, xhat = _rms_stats(hv)
        err = xhat * gv - t_ref[...]
        per_row = jnp.mean(err * err, axis=-1, keepdims=True)
        loss_ref[...] += 0.5 * jnp.sum(per_row, axis=0, keepdims=True)
        dx, dgrow = _rms_bwd(err * (1.0 / d), hv, gv)
        dh_ref[...] = dx
        dg_ref[...] += jnp.sum(dgrow, axis=0, keepdims=True)

    row = pl.BlockSpec((tm, d), lambda i: (i, 0))
    vec = pl.BlockSpec((1, d), lambda i: (0, 0))
    return pl.pallas_call(
        body, name=name, grid=(s // tm,), in_specs=[row, vec, row],
        out_specs=[pl.BlockSpec((1, LANES), lambda i: (0, 0)), row, vec],
        out_shape=[jax.ShapeDtypeStruct((1, LANES), F32), jax.ShapeDtypeStruct((s, d), F32),
                   jax.ShapeDtypeStruct((1, d), F32)],
        compiler_params=_cp("arbitrary"))(h, g, tgt)


def _rope_fwd(y1, y2, cos, sin, *, name, tm=512):
    s, r = y1.shape

    def body(a_ref, b_ref, c_ref, s_ref, o_ref):
        o_ref[...] = a_ref[...] * c_ref[...] + b_ref[...] * s_ref[...]

    row = pl.BlockSpec((tm, r), lambda i: (i, 0))
    return pl.pallas_call(
        body, name=name, grid=(s // tm,), in_specs=[row] * 4, out_specs=row,
        out_shape=jax.ShapeDtypeStruct((s, r), F32), compiler_params=_cp("arbitrary"))(y1, y2, cos, sin)


def _split3(v):
    h1 = v.astype(BF16)
    r1 = v - h1.astype(F32)
    h2 = r1.astype(BF16)
    h3 = (r1 - h2.astype(F32)).astype(BF16)
    return h1, h2, h3


def _tri(tb, upper):
    r = lax.broadcasted_iota(jnp.int32, (tb, tb), 0)
    c = lax.broadcasted_iota(jnp.int32, (tb, tb), 1)
    return jnp.where((r <= c) if upper else (r >= c), 1.0, 0.0).astype(BF16)


def _fox_gate_fwd(ft, bf, *, out_scale, name, tb=512):
    nh, s = ft.shape

    def body(f_ref, b_ref, o_ref, carry):
        @pl.when(pl.program_id(0) == 0)
        def _():
            carry[...] = jnp.zeros_like(carry)

        z = f_ref[...] + b_ref[...]
        lf = jnp.minimum(z, 0.0) - jnp.log(1.0 + jnp.exp(-jnp.abs(z)))
        tri = _tri(tb, True)
        cs = sum(jnp.dot(t, tri, preferred_element_type=F32) for t in _split3(lf)) + carry[...]
        for n, term in enumerate(_split3(cs * out_scale)):
            o_ref[n] = term
        carry[...] += jnp.sum(lf, axis=-1, keepdims=True)

    return pl.pallas_call(
        body, name=name, grid=(s // tb,),
        in_specs=[pl.BlockSpec((nh, tb), lambda t: (0, t)), pl.BlockSpec((nh, 1), lambda t: (0, 0))],
        out_specs=pl.BlockSpec((3, nh, tb), lambda t: (0, 0, t)),
        out_shape=jax.ShapeDtypeStruct((3, nh, s), BF16),
        scratch_shapes=[pltpu.VMEM((nh, 1), F32)], compiler_params=_cp("arbitrary"))(ft, bf)


def _fox_gate_bwd(drow, dcol, ft, bf, *, inv_scale, name, tb=512):
    nh, s = ft.shape
    nb = s // tb

    def body(dr_ref, dc_ref, f_ref, b_ref, df_ref, db_ref, carry):
        @pl.when(pl.program_id(0) == 0)
        def _():
            carry[...] = jnp.zeros_like(carry)
            db_ref[...] = jnp.zeros_like(db_ref)

        dc = (dr_ref[...] - dc_ref[...]) * inv_scale
        tri = _tri(tb, False)
        suf = sum(jnp.dot(t, tri, preferred_element_type=F32) for t in _split3(dc)) + carry[...]
        z = f_ref[...] + b_ref[...]
        dz = suf * (1.0 / (1.0 + jnp.exp(z)))
        df_ref[...] = dz
        db_ref[...] += jnp.sum(dz, axis=-1, keepdims=True)
        carry[...] += jnp.sum(dc, axis=-1, keepdims=True)

    rev = pl.BlockSpec((nh, tb), lambda t: (0, nb - 1 - t))
    one = pl.BlockSpec((nh, 1), lambda t: (0, 0))
    return pl.pallas_call(
        body, name=name, grid=(nb,), in_specs=[rev, rev, rev, one], out_specs=[rev, one],
        out_shape=[jax.ShapeDtypeStruct((nh, s), F32), jax.ShapeDtypeStruct((nh, 1), F32)],
        scratch_shapes=[pltpu.VMEM((nh, 1), F32)], compiler_params=_cp("arbitrary"))(drow, dcol, ft, bf)


def _tri_fwd(t, nq):
    i = sum((t >= (r * (r + 1)) // 2).astype(jnp.int32) for r in range(1, nq))
    return i, t - (i * (i + 1)) // 2


def _tri_bwd(t, nq):
    j = sum((t >= r * nq - (r * (r - 1)) // 2).astype(jnp.int32) for r in range(1, nq))
    return j, j + t - (j * nq - (j * (j - 1)) // 2)


def _scores_t(k, q, *, scale, diag):
    s = lax.dot_general(k, q, NT, preferred_element_type=F32) * scale
    if diag:
        r = lax.broadcasted_iota(jnp.int32, s.shape, 0)
        c = lax.broadcasted_iota(jnp.int32, s.shape, 1)
        s = jnp.where(r <= c, s, MASK_VALUE)
    return s


def _causal_fwd_t(q, k, vt, *, scale, name, tq, hb=2, rider=None):
    nh, s, dq = q.shape
    dv = vt.shape[1]
    nq = s // tq
    nsteps = (nq * (nq + 1)) // 2

    def body(q_ref, k_ref, vt_ref, o_ref, lse_ref, m_sc, l_sc, acc_sc):
        i, j = _tri_fwd(pl.program_id(1), nq)

        @pl.when(j == 0)
        def _():
            m_sc[...] = jnp.full_like(m_sc, MASK_VALUE)
            l_sc[...] = jnp.zeros_like(l_sc)
            acc_sc[...] = jnp.zeros_like(acc_sc)

        def step(diag):
            for u in range(hb):
                sc = _scores_t(k_ref[u], q_ref[u], scale=scale, diag=diag)
                m_prev = m_sc[u]
                m_new = jnp.maximum(m_prev, jnp.max(sc, axis=0, keepdims=True))
                alpha = jnp.exp(m_prev - m_new)
                pr = jnp.exp(sc - m_new)
                l_new = alpha * l_sc[u] + jnp.sum(pr, axis=0, keepdims=True)
                acc = alpha * acc_sc[u] + jnp.dot(vt_ref[u], pr.astype(BF16), preferred_element_type=F32)
                if diag:
                    o_ref[u] = (acc / l_new).astype(BF16)
                    lse_ref[u] = m_new + jnp.log(l_new)
                else:
                    m_sc[u], l_sc[u], acc_sc[u] = m_new, l_new, acc

        pl.when(j < i)(functools.partial(step, False))
        pl.when(j == i)(functools.partial(step, True))

    def qi(t):
        return _tri_fwd(t, nq)[0]

    def kj(t):
        return _tri_fwd(t, nq)[1]

    return _call_with_rider(
        body, rider, name=name, grid=(nh // hb, nsteps),
        in_specs=[pl.BlockSpec((hb, tq, dq), lambda hp, t: (hp, qi(t), 0)),
                  pl.BlockSpec((hb, tq, dq), lambda hp, t: (hp, kj(t), 0)),
                  pl.BlockSpec((hb, dv, tq), lambda hp, t: (hp, 0, kj(t)))],
        out_specs=[pl.BlockSpec((hb, dv, tq), lambda hp, t: (hp, 0, qi(t))),
                   pl.BlockSpec((hb, 1, tq), lambda hp, t: (hp, 0, qi(t)))],
        out_shape=[jax.ShapeDtypeStruct((nh, dv, s), BF16), jax.ShapeDtypeStruct((nh, 1, s), F32)],
        scratch_shapes=[pltpu.VMEM((hb, 1, tq), F32), pltpu.VMEM((hb, 1, tq), F32), pltpu.VMEM((hb, dv, tq), F32)],
        compiler_params=_cp("arbitrary", "arbitrary"), args=(q, k, vt))


def _causal_bwd_t(q, k, v, ot, dot_, lse, *, scale, name, tq, hb=2, rider=None):
    nh, s, dq = q.shape
    dv = v.shape[-1]
    nq = s // tq
    nsteps = (nq * (nq + 1)) // 2

    def body(q_ref, k_ref, v_ref, ot_ref, dot_ref, lse_ref, dq_ref, dk_ref, dv_ref):
        t = pl.program_id(1)
        j, i = _tri_bwd(t, nq)

        @pl.when(t == 0)
        def _():
            dq_ref[...] = jnp.zeros_like(dq_ref)

        def step(diag):
            rows = pl.ds(pl.multiple_of(i * tq, tq), tq)
            for u in range(hb):
                qv, kv, dov = q_ref[u], k_ref[u], dot_ref[u]
                pr = jnp.exp(_scores_t(kv, qv, scale=scale, diag=diag) - lse_ref[u])
                dp = jnp.dot(v_ref[u], dov, preferred_element_type=F32)
                delta = jnp.sum(dov.astype(F32) * ot_ref[u].astype(F32), axis=0, keepdims=True)
                dsb = ((pr * (dp - delta)) * scale).astype(BF16)
                d_v = lax.dot_general(pr.astype(BF16), dov, NT, preferred_element_type=F32)
                d_k = jnp.dot(dsb, qv, preferred_element_type=F32)
                if diag:
                    dv_ref[u], dk_ref[u] = d_v, d_k
                else:
                    dv_ref[u] += d_v
                    dk_ref[u] += d_k
                dq_ref[u, rows, :] += lax.dot_general(dsb, kv, TN, preferred_element_type=F32)

        pl.when(i > j)(functools.partial(step, False))
        pl.when(i == j)(functools.partial(step, True))

    def qi(t):
        return _tri_bwd(t, nq)[1]

    def kj(t):
        return _tri_bwd(t, nq)[0]

    rows_q = pl.BlockSpec((hb, tq, dq), lambda hp, t: (hp, qi(t), 0))
    rows_k = pl.BlockSpec((hb, tq, dq), lambda hp, t: (hp, kj(t), 0))
    lanes_q = pl.BlockSpec((hb, dv, tq), lambda hp, t: (hp, 0, qi(t)))
    return _call_with_rider(
        body, rider, name=name, grid=(nh // hb, nsteps),
        in_specs=[rows_q, rows_k, pl.BlockSpec((hb, tq, dv), lambda hp, t: (hp, kj(t), 0)), lanes_q, lanes_q,
                  pl.BlockSpec((hb, 1, tq), lambda hp, t: (hp, 0, qi(t)))],
        out_specs=[pl.BlockSpec((hb, s, dq), lambda hp, t: (hp, 0, 0)), rows_k,
                   pl.BlockSpec((hb, tq, dv), lambda hp, t: (hp, kj(t), 0))],
        out_shape=[jax.ShapeDtypeStruct((nh, s, dq), F32), jax.ShapeDtypeStruct((nh, s, dq), F32),
                   jax.ShapeDtypeStruct((nh, s, dv), F32)],
        scratch_shapes=[], compiler_params=_cp("arbitrary", "arbitrary"), args=(q, k, v, ot, dot_, lse))


def _swa_scores_t(k, q, dist, ok, *, scale, slope):
    s = lax.dot_general(k, q, NT, preferred_element_type=F32) * scale - slope * dist.astype(F32)
    return jnp.where(ok, s, MASK_VALUE)


def _swa_geometry(tb, w, has_other):
    r = lax.broadcasted_iota(jnp.int32, (tb, tb), 0)
    c = lax.broadcasted_iota(jnp.int32, (tb, tb), 1)
    d_same = c - r
    ok_same = jnp.logical_and(d_same >= 0, d_same < w)

    def other(ncols):
        rr = lax.broadcasted_iota(jnp.int32, (w, ncols), 0)
        cc = lax.broadcasted_iota(jnp.int32, (w, ncols), 1)
        dd = cc + w - rr
        return dd, jnp.logical_and(dd < w, has_other)

    return (d_same, ok_same), other


def _swa_fwd_t(q, k, vt, slopes_sinks, *, scale, window, name, tb=256):
    nh, s, d = q.shape
    nkv = k.shape[0]
    grp = nh // nkv
    w = window
    per = tb // w
    assert tb % w == 0

    def body(q_ref, kc_ref, kp_ref, vc_ref, vp_ref, ss_ref, o_ref, lse_ref):
        kvh, i = pl.program_id(0), pl.program_id(1)
        (d_c, ok_c), other = _swa_geometry(tb, w, i > 0)
        d_p, ok_p = other(tb)
        for g in range(grp):
            h = kvh * grp + g
            slope, sink = ss_ref[0, h], ss_ref[1, h]
            qg = q_ref[g]
            s_c = _swa_scores_t(kc_ref[...], qg, d_c, ok_c, scale=scale, slope=slope)
            s_p = _swa_scores_t(kp_ref[...], qg, d_p, ok_p, scale=scale, slope=slope)
            m = jnp.maximum(jnp.maximum(jnp.max(s_c, axis=0, keepdims=True), jnp.max(s_p, axis=0, keepdims=True)), sink)
            p_c, p_p = jnp.exp(s_c - m), jnp.exp(s_p - m)
            l = jnp.sum(p_c, axis=0, keepdims=True) + jnp.sum(p_p, axis=0, keepdims=True) + jnp.exp(sink - m)
            acc = (jnp.dot(vc_ref[...], p_c.astype(BF16), preferred_element_type=F32)
                   + jnp.dot(vp_ref[...], p_p.astype(BF16), preferred_element_type=F32))
            o_ref[g] = (acc / l).astype(BF16)
            lse_ref[g] = m + jnp.log(l)

    def prev(i):
        return jnp.maximum(i * per - 1, 0)

    return pl.pallas_call(
        body, name=name, grid=(nkv, s // tb),
        in_specs=[pl.BlockSpec((grp, tb, d), lambda kh, i: (kh, i, 0)),
                  pl.BlockSpec((None, tb, d), lambda kh, i: (kh, i, 0)),
                  pl.BlockSpec((None, w, d), lambda kh, i: (kh, prev(i), 0)),
                  pl.BlockSpec((None, d, tb), lambda kh, i: (kh, 0, i)),
                  pl.BlockSpec((None, d, w), lambda kh, i: (kh, 0, prev(i))),
                  pl.BlockSpec(memory_space=pltpu.SMEM)],
        out_specs=[pl.BlockSpec((grp, d, tb), lambda kh, i: (kh, 0, i)), pl.BlockSpec((grp, 1, tb), lambda kh, i: (kh, 0, i))],
        out_shape=[jax.ShapeDtypeStruct((nh, d, s), BF16), jax.ShapeDtypeStruct((nh, 1, s), F32)],
        compiler_params=_cp("arbitrary", "arbitrary"))(q, k, k, vt, vt, slopes_sinks)


def _swa_bwd_t(q, k, v, ot, dot_, lse, slopes_sinks, *, scale, window, name, tb=256, rider=None):
    nh, s, d = q.shape
    nkv = k.shape[0]
    grp = nh // nkv
    w = window
    per = tb // w
    nb = s // tb

    def body(qc_ref, qn_ref, kc_ref, kp_ref, vc_ref, vp_ref, oc_ref, on_ref, doc_ref, don_ref, lc_ref, ln_ref, ss_ref,
             dq_ref, dk_ref, dvt_ref, dsink_ref):
        kvh, i = pl.program_id(0), pl.program_id(1)

        @pl.when(i == 0)
        def _():
            dsink_ref[...] = jnp.zeros_like(dsink_ref)

        (d_c, ok_c), other = _swa_geometry(tb, w, i > 0)
        d_p, ok_p = other(tb)
        d_n, ok_n = _swa_geometry(tb, w, i < nb - 1)[1](w)
        kc, kp, vc, vp = kc_ref[...], kp_ref[...], vc_ref[...], vp_ref[...]
        k_last, v_last = kc[tb - w:, :], vc[tb - w:, :]
        dk_acc = jnp.zeros((tb, d), F32)
        dv_acc = jnp.zeros((d, tb), F32)
        dk_tail = jnp.zeros((w, d), F32)
        dv_tail = jnp.zeros((d, w), F32)
        for g in range(grp):
            h = kvh * grp + g
            slope, sink = ss_ref[0, h], ss_ref[1, h]
            qg, dog, lse_c = qc_ref[g], doc_ref[g], lc_ref[g]
            delta = jnp.sum(dog.astype(F32) * oc_ref[g].astype(F32), axis=0, keepdims=True)
            p_c = jnp.exp(_swa_scores_t(kc, qg, d_c, ok_c, scale=scale, slope=slope) - lse_c)
            p_p = jnp.exp(_swa_scores_t(kp, qg, d_p, ok_p, scale=scale, slope=slope) - lse_c)
            ds_c = ((p_c * (jnp.dot(vc, dog, preferred_element_type=F32) - delta)) * scale).astype(BF16)
            ds_p = ((p_p * (jnp.dot(vp, dog, preferred_element_type=F32) - delta)) * scale).astype(BF16)
            dq_ref[g] = (lax.dot_general(ds_c, kc, TN, preferred_element_type=F32)
                         + lax.dot_general(ds_p, kp, TN, preferred_element_type=F32))
            dk_acc += jnp.dot(ds_c, qg, preferred_element_type=F32)
            dv_acc += lax.dot_general(dog, p_c.astype(BF16), NT, preferred_element_type=F32)
            dsink_ref[g] -= jnp.broadcast_to(jnp.sum(jnp.exp(sink - lse_c) * delta, axis=1, keepdims=True), (1, LANES))
            qn, don = qn_ref[g], don_ref[g]
            delta_n = jnp.sum(don.astype(F32) * on_ref[g].astype(F32), axis=0, keepdims=True)
            p_n = jnp.exp(_swa_scores_t(k_last, qn, d_n, ok_n, scale=scale, slope=slope) - ln_ref[g])
            ds_n = ((p_n * (jnp.dot(v_last, don, preferred_element_type=F32) - delta_n)) * scale).astype(BF16)
            dk_tail += jnp.dot(ds_n, qn, preferred_element_type=F32)
            dv_tail += lax.dot_general(don, p_n.astype(BF16), NT, preferred_element_type=F32)
        dk_ref[...] = dk_acc
        dvt_ref[...] = dv_acc
        dk_ref[tb - w:, :] += dk_tail
        dvt_ref[:, tb - w:] += dv_tail

    def prev(i):
        return jnp.maximum(i * per - 1, 0)

    def nxt(i):
        return jnp.minimum((i + 1) * per, s // w - 1)

    return _call_with_rider(
        body, rider, name=name, grid=(nkv, nb), scratch_shapes=[],
        args=(q, q, k, k, v, v, ot, ot, dot_, dot_, lse, lse, slopes_sinks),
        in_specs=[pl.BlockSpec((grp, tb, d), lambda kh, i: (kh, i, 0)),
                  pl.BlockSpec((grp, w, d), lambda kh, i: (kh, nxt(i), 0)),
                  pl.BlockSpec((None, tb, d), lambda kh, i: (kh, i, 0)),
                  pl.BlockSpec((None, w, d), lambda kh, i: (kh, prev(i), 0)),
                  pl.BlockSpec((None, tb, d), lambda kh, i: (kh, i, 0)),
                  pl.BlockSpec((None, w, d), lambda kh, i: (kh, prev(i), 0)),
                  pl.BlockSpec((grp, d, tb), lambda kh, i: (kh, 0, i)),
                  pl.BlockSpec((grp, d, w), lambda kh, i: (kh, 0, nxt(i))),
                  pl.BlockSpec((grp, d, tb), lambda kh, i: (kh, 0, i)),
                  pl.BlockSpec((grp, d, w), lambda kh, i: (kh, 0, nxt(i))),
                  pl.BlockSpec((grp, 1, tb), lambda kh, i: (kh, 0, i)),
                  pl.BlockSpec((grp, 1, w), lambda kh, i: (kh, 0, nxt(i))),
                  pl.BlockSpec(memory_space=pltpu.SMEM)],
        out_specs=[pl.BlockSpec((grp, tb, d), lambda kh, i: (kh, i, 0)),
                   pl.BlockSpec((None, tb, d), lambda kh, i: (kh, i, 0)),
                   pl.BlockSpec((None, d, tb), lambda kh, i: (kh, 0, i)),
                   pl.BlockSpec((None, grp, 1, LANES), lambda kh, i: (kh, 0, 0, 0))],
        out_shape=[jax.ShapeDtypeStruct((nh, s, d), F32), jax.ShapeDtypeStruct((nkv, s, d), F32),
                   jax.ShapeDtypeStruct((nkv, d, s), F32), jax.ShapeDtypeStruct((nkv, grp, 1, LANES), F32)],
        compiler_params=_cp("arbitrary", "arbitrary"))


def _adamw(w, g, m, v, *, name):
    shape = w.shape
    cols = shape[-1]
    rows = int(np.prod(shape[:-1])) if len(shape) > 1 else 1
    tr = _row_tile(rows, cols)
    c1 = 1.0 - ADAM_B1 ** ADAM_STEP
    c2 = 1.0 - ADAM_B2 ** ADAM_STEP

    def body(w_ref, g_ref, m_ref, v_ref, d_ref, mo_ref, vo_ref):
        gv = g_ref[...]
        mn = ADAM_B1 * m_ref[...] + (1.0 - ADAM_B1) * gv
        vn = ADAM_B2 * v_ref[...] + (1.0 - ADAM_B2) * (gv * gv)
        mo_ref[...] = mn
        vo_ref[...] = vn
        d_ref[...] = -ADAM_LR * ((mn / c1) / (jnp.sqrt(vn / c2) + ADAM_EPS) + ADAM_WD * w_ref[...])

    blk = pl.BlockSpec((tr, cols), lambda i: (i, 0))
    outs = pl.pallas_call(
        body, name=name, grid=(rows // tr,), in_specs=[blk] * 4, out_specs=[blk] * 3,
        out_shape=[jax.ShapeDtypeStruct((rows, cols), F32)] * 3,
        compiler_params=_cp("arbitrary"))(*[a.reshape(rows, cols) for a in (w, g, m, v)])
    return tuple(a.reshape(shape) for a in outs)


def _hbm_spec():
    return pl.BlockSpec(memory_space=pl.ANY)


def _mesh_place():
    x, y, c = lax.axis_index("x"), lax.axis_index("y"), lax.axis_index("c")
    return x, y, c, [(1 - x, y), (x, 1 - y), (1 - x, 1 - y)]


def _half_rows(c, rows, align):
    return pl.ds(pl.multiple_of(c * (rows // 2), align), rows // 2)


def _part(ref, mode, k, n, rows=None):
    if mode == "cols":
        cols = pl.ds(pl.multiple_of(k * n, LANES), n)
        return ref.at[:, cols] if rows is None else ref.at[rows, cols]
    return ref.at[k] if rows is None else ref.at[k, rows, :]


class _Rider:
    def __init__(self, inputs, out_shape, n_sems, start, finish):
        self.inputs, self.out_shape, self.n_sems, self.start, self.finish = inputs, out_shape, n_sems, start, finish


def _call_with_rider(body, rider, *, name, grid, in_specs, out_specs, out_shape, scratch_shapes, compiler_params, args):
    if rider is None:
        outs = pl.pallas_call(body, name=name, grid=grid, in_specs=in_specs, out_specs=out_specs, out_shape=out_shape,
                              scratch_shapes=scratch_shapes, compiler_params=compiler_params)(*args)
        return outs, []
    n_in, n_out, n_sc = len(in_specs), len(out_specs), len(scratch_shapes)
    n_rin, n_rout = len(rider.inputs), len(rider.out_shape)

    def wrapped(*refs):
        pos = 0
        groups = []
        for n in (n_in, n_rin, n_out, n_rout, n_sc, 2):
            groups.append(refs[pos:pos + n])
            pos += n
        ins, rins, outs, routs, scratch, sems = groups
        ids = [pl.program_id(a) for a in range(len(grid))]
        first = functools.reduce(jnp.logical_and, [i == 0 for i in ids])
        last = functools.reduce(jnp.logical_and, [i == g - 1 for i, g in zip(ids, grid)])
        pl.when(first)(lambda: rider.start(rins, routs, *sems))
        body(*ins, *outs, *scratch)
        pl.when(last)(lambda: rider.finish(rins, routs, *sems))

    outs = pl.pallas_call(
        wrapped, name=name, grid=grid, in_specs=list(in_specs) + [_hbm_spec()] * n_rin,
        out_specs=list(out_specs) + [_hbm_spec()] * n_rout, out_shape=list(out_shape) + list(rider.out_shape),
        scratch_shapes=list(scratch_shapes) + [pltpu.SemaphoreType.DMA((rider.n_sems,))] * 2,
        compiler_params=compiler_params)(*args, *rider.inputs)
    return outs[:n_out], outs[n_out:]


def _run_rider(rider, *, name):
    n_rin = len(rider.inputs)

    def body(*refs):
        rins, routs, sems = refs[:n_rin], refs[n_rin:-2], refs[-2:]
        rider.start(rins, routs, *sems)
        rider.finish(rins, routs, *sems)

    return pl.pallas_call(
        body, name=name, in_specs=[_hbm_spec()] * n_rin, out_specs=[_hbm_spec()] * len(rider.out_shape),
        out_shape=rider.out_shape, scratch_shapes=[pltpu.SemaphoreType.DMA((rider.n_sems,))] * 2)(*rider.inputs)


def _gather_rider(shards, modes):
    n_arr = len(shards)
    out_shape = [jax.ShapeDtypeStruct((s.shape[0], N_CHIPS * s.shape[1]) if m == "cols" else (N_CHIPS,) + s.shape, s.dtype)
                 for s, m in zip(shards, modes)]
    per = 4

    def copies(srcs, dsts, send_sems, recv_sems):
        x, y, c, chips = _mesh_place()
        me = 2 * x + y
        sends, waits = [], []
        for i in range(n_arr):
            r, n = shards[i].shape
            rows = _half_rows(c, r, 16)

            def copy(slot, src, dst, to, i=i):
                return pltpu.make_async_remote_copy(src_ref=src, dst_ref=dst, send_sem=send_sems.at[i * per + slot],
                                                    recv_sem=recv_sems.at[i * per + slot], device_id=to, device_id_type=MESH)

            own = _part(dsts[i], modes[i], me, n)
            sends.append(copy(0, srcs[i], own, (x, y, 1 - c)))
            waits.append(copy(0, own, own, (x, y, 1 - c)))
            for j, (px, py) in enumerate(chips):
                sends.append(copy(1 + j, srcs[i].at[rows], _part(dsts[i], modes[i], me, n, rows), (px, py, c)))
                theirs = _part(dsts[i], modes[i], 2 * px + py, n, rows)
                waits.append(copy(1 + j, theirs, theirs, (px, py, c)))
        return sends, waits

    def start(*refs):
        for cp in copies(*refs)[0]:
            cp.start()

    def finish(*refs):
        sends, waits = copies(*refs)
        for cp in waits:
            cp.wait_recv()
        for cp in sends:
            cp.wait_send()

    return _Rider(list(shards), out_shape, per * n_arr, start, finish)


def _gather_forward(dsts, shard_shapes, modes, *, name):
    n_arr = len(dsts)

    def body(*refs):
        outs = refs[n_arr:2 * n_arr]
        send_sems, recv_sems = refs[2 * n_arr:]
        x, y, c, chips = _mesh_place()
        cps = []
        for i in range(n_arr):
            r, n = shard_shapes[i]
            for j, (px, py) in enumerate(chips):
                def view(hc, i=i, px=px, py=py, r=r, n=n):
                    return _part(outs[i], modes[i], 2 * px + py, n, _half_rows(hc, r, 16))

                def copy(ref, i=i, j=j):
                    return pltpu.make_async_remote_copy(src_ref=ref, dst_ref=ref, send_sem=send_sems.at[3 * i + j],
                                                        recv_sem=recv_sems.at[3 * i + j], device_id=(x, y, 1 - c), device_id_type=MESH)

                cps.append((copy(view(c)), copy(view(1 - c))))
        for send, _ in cps:
            send.start()
        for send, theirs in cps:
            theirs.wait_recv()
            send.wait_send()

    return pl.pallas_call(
        body, name=name, in_specs=[_hbm_spec()] * n_arr, out_specs=[_hbm_spec()] * n_arr,
        out_shape=[jax.ShapeDtypeStruct(d.shape, d.dtype) for d in dsts],
        input_output_aliases={i: i for i in range(n_arr)},
        scratch_shapes=[pltpu.SemaphoreType.DMA((3 * n_arr,)), pltpu.SemaphoreType.DMA((3 * n_arr,))])(*dsts)


def _blk_view(a, mode):
    return a[None] if mode == "cols" else a


def _swap_rider(arrs, modes):
    n_arr = len(arrs)
    out_shape = [jax.ShapeDtypeStruct((a.shape[0] // 2, a.shape[1]) if m == "cols" else (a.shape[0], a.shape[1] // 2, a.shape[2]), a.dtype)
                 for a, m in zip(arrs, modes)]

    def copies(srcs, dsts, send_sems, recv_sems):
        x, y, c, _ = _mesh_place()
        cps = []
        for i in range(n_arr):
            if modes[i] == "cols":
                src = srcs[i].at[_half_rows(1 - c, arrs[i].shape[0], 8)]
            else:
                src = srcs[i].at[:, _half_rows(1 - c, arrs[i].shape[1], 8), :]
            cps.append(pltpu.make_async_remote_copy(src_ref=src, dst_ref=dsts[i], send_sem=send_sems.at[i],
                                                    recv_sem=recv_sems.at[i], device_id=(x, y, 1 - c), device_id_type=MESH))
        return cps

    def start(*refs):
        for cp in copies(*refs):
            cp.start()

    def finish(*refs):
        for cp in copies(*refs):
            cp.wait()

    return _Rider(list(arrs), out_shape, n_arr, start, finish)


def _rs_pair_add(arr, landed, place, *, name):
    nb, r, c = arr.shape
    rh = r // 2
    tr = _row_tile(rh, c)
    nt = rh // tr

    def body(p_ref, a_ref, l_ref, o_ref):
        o_ref[...] = (a_ref[...] + l_ref[...]).astype(BF16)

    grid_spec = pltpu.PrefetchScalarGridSpec(
        num_scalar_prefetch=1, grid=(nb, nt),
        in_specs=[pl.BlockSpec((None, tr, c), lambda b, t, p_ref: (b, p_ref[1] * nt + t, 0)),
                  pl.BlockSpec((None, tr, c), lambda b, t, p_ref: (b, t, 0))],
        out_specs=pl.BlockSpec((None, tr, c), lambda b, t, p_ref: (b, t, 0)))
    return pl.pallas_call(
        body, name=name, grid_spec=grid_spec, out_shape=jax.ShapeDtypeStruct((nb, rh, c), BF16),
        compiler_params=_cp("arbitrary", "arbitrary"))(place, arr, landed)


def _exchange_rider(parts, modes):
    n_arr = len(parts)
    out_shape = []
    for a, m in zip(parts, modes):
        shp = (a.shape[0], a.shape[1] // N_CHIPS) if m == "cols" else a.shape[1:]
        out_shape.append(jax.ShapeDtypeStruct((3,) + shp, a.dtype))

    def copies(srcs, dsts, send_sems, recv_sems):
        x, y, c, chips = _mesh_place()
        cps = []
        for i in range(n_arr):
            n = out_shape[i].shape[-1]
            for j, (px, py) in enumerate(chips):
                cps.append(pltpu.make_async_remote_copy(
                    src_ref=_part(srcs[i], modes[i], 2 * px + py, n), dst_ref=dsts[i].at[j],
                    send_sem=send_sems.at[3 * i + j], recv_sem=recv_sems.at[3 * i + j],
                    device_id=(px, py, c), device_id_type=MESH))
        return cps

    def start(*refs):
        for cp in copies(*refs):
            cp.start()

    def finish(*refs):
        for cp in copies(*refs):
            cp.wait()

    return _Rider(list(parts), out_shape, 3 * n_arr, start, finish)


def _rs_chip_sum(part, landed, mode, place, *, name):
    _, rh, n = landed.shape
    tr = _row_tile(rh, n)
    nt = rh // tr

    def body(p_ref, a_ref, l_ref, o_ref):
        o_ref[...] = ((a_ref[...].astype(F32) + l_ref[0].astype(F32)) + l_ref[1].astype(F32)) + l_ref[2].astype(F32)

    if mode == "cols":
        own = pl.BlockSpec((tr, n), lambda t, p_ref: (t, p_ref[0]))
    else:
        own = pl.BlockSpec((None, tr, n), lambda t, p_ref: (p_ref[0], t, 0))
    grid_spec = pltpu.PrefetchScalarGridSpec(
        num_scalar_prefetch=1, grid=(nt,),
        in_specs=[own, pl.BlockSpec((3, tr, n), lambda t, p_ref: (0, t, 0))],
        out_specs=pl.BlockSpec((tr, n), lambda t, p_ref: (p_ref[1] * nt + t, 0)))
    return pl.pallas_call(
        body, name=name, grid_spec=grid_spec, out_shape=jax.ShapeDtypeStruct((2 * rh, n), F32),
        compiler_params=_cp("arbitrary"))(place, part, landed)


def _rs_pair_join(halves, *, name):
    n_arr = len(halves)

    def body(*refs):
        outs = refs[n_arr:2 * n_arr]
        send_sems, recv_sems = refs[2 * n_arr:]
        x, y, c, _ = _mesh_place()
        cps = []
        for i in range(n_arr):
            rows = _half_rows(c, halves[i].shape[0], 8)
            cps.append(pltpu.make_async_remote_copy(src_ref=outs[i].at[rows], dst_ref=outs[i].at[rows], send_sem=send_sems.at[i],
                                                    recv_sem=recv_sems.at[i], device_id=(x, y, 1 - c), device_id_type=MESH))
        for cp in cps:
            cp.start()
        for i, cp in enumerate(cps):
            cp.wait_send()
            theirs = outs[i].at[_half_rows(1 - c, halves[i].shape[0], 8)]
            pltpu.make_async_remote_copy(src_ref=theirs, dst_ref=theirs, send_sem=send_sems.at[i], recv_sem=recv_sems.at[i],
                                         device_id=(x, y, 1 - c), device_id_type=MESH).wait_recv()

    return pl.pallas_call(
        body, name=name, in_specs=[_hbm_spec()] * n_arr, out_specs=[_hbm_spec()] * n_arr,
        out_shape=[jax.ShapeDtypeStruct(h.shape, h.dtype) for h in halves],
        input_output_aliases={i: i for i in range(n_arr)},
        scratch_shapes=[pltpu.SemaphoreType.DMA((n_arr,)), pltpu.SemaphoreType.DMA((n_arr,))])(*halves)


def _allreduce_small(v, *, name):
    r, c = v.shape

    def body(v_ref, o_ref, gath, send_sems, recv_sems):
        x, y, cc, _ = _mesh_place()
        me = 4 * x + 2 * y + cc
        gath[me] = v_ref[...]
        cps = []
        for rel in range(1, 8):
            px = 1 - x if rel & 4 else x
            py = 1 - y if rel & 2 else y
            pc = 1 - cc if rel & 1 else cc

            def copy(slot, px=px, py=py, pc=pc, rel=rel):
                return pltpu.make_async_remote_copy(
                    src_ref=v_ref, dst_ref=gath.at[slot], send_sem=send_sems.at[rel - 1],
                    recv_sem=recv_sems.at[rel - 1], device_id=(px, py, pc), device_id_type=MESH)

            cps.append((copy(me), copy(4 * px + 2 * py + pc)))
        for send, _ in cps:
            send.start()
        for send, theirs in cps:
            theirs.wait_recv()
            send.wait_send()
        tot = gath[0]
        for d in range(1, 8):
            tot = tot + gath[d]
        o_ref[...] = tot

    vm = pl.BlockSpec(memory_space=pltpu.VMEM)
    return pl.pallas_call(
        body, name=name, in_specs=[vm], out_specs=vm, out_shape=jax.ShapeDtypeStruct((r, c), F32),
        scratch_shapes=[pltpu.VMEM((8, r, c), F32), pltpu.SemaphoreType.DMA((7,)), pltpu.SemaphoreType.DMA((7,))])(v)


def _rope_tables(s, reps):
    half = B_ROPE // 2
    inv = ROPE_THETA ** (-jnp.arange(0, B_ROPE, 2, dtype=F32) / B_ROPE)
    ang = jnp.arange(s, dtype=F32)[:, None] * inv[None, :]
    return jnp.tile(jnp.cos(ang), (1, reps)), jnp.tile(jnp.sin(ang), (1, reps))


def _alibi_slopes():
    return 2.0 ** (-8.0 * jnp.arange(1, A_HEADS + 1, dtype=F32) / A_HEADS)


def _ffn_fwd(h, norm, wts, tag, rider=None, on_rode=None):
    (dact_dgate, dact_dup, act, xn), rode = _ffn_up(h, norm, wts["wgu"], name=f"{tag}_up", rider=rider)
    if on_rode is not None:
        on_rode(rode)
    out = _mm_res_fwd(act, wts["wd"], h, scale=FFN_RES_SCALE, name=f"{tag}_down")
    return out, dict(h_in=h, dact_dgate=dact_dgate, dact_dup=dact_dup, act=act, xn=xn), rode


def _ffn_bwd(dh, norm, wts, sv, tag, rider=None, own=None):
    (dgate, dup), rode = _ffn_down_bwd(dh, wts["wd"], sv["dact_dgate"], sv["dact_dup"], scale=FFN_RES_SCALE,
                                      name=f"{tag}_down_bwd", rider=rider)
    d_wd = _mm_tn(sv["act"], dh, b_scale=FFN_RES_SCALE, name=f"{tag}_dwd")
    pairs = [(dgate, wts["wgu"], 0), (dup, wts["wgu"], 1)]
    if own is None:
        d_wgu = _mm_tn(sv["xn"], [dgate, dup], name=f"{tag}_dwgu")
        dh_in, dnorm = _mm_nt_rmsbwd(pairs, sv["h_in"], norm, dh, name=f"{tag}_dx")
    else:
        wd_ready, wgu_ready, done = own
        first = wd_ready(d_wd)
        res = _mm_tn(sv["xn"], [dgate, dup], name=f"{tag}_dwgu", rider=first)
        d_wgu, brought = (res, []) if first is None else res
        second = wgu_ready(brought, d_wgu)
        res = _mm_nt_rmsbwd(pairs, sv["h_in"], norm, dh, name=f"{tag}_dx", rider=second)
        dh_in, dnorm, brought = (*res, []) if second is None else res
        done(brought)
    return dh_in, dnorm, d_wgu, d_wd, rode


def _even_weights(w_in, w_uq, w_ukv):
    half = B_ROPE // 2
    base = w_in.shape[1]
    kr1, kr2 = w_in[:, base - B_ROPE:base - half], w_in[:, base - half:]
    w_in_cat = jnp.concatenate([w_in, -kr2, kr1, jnp.zeros((w_in.shape[0], 64), w_in.dtype)], axis=1)
    u3 = w_uq.reshape(w_uq.shape[0], B_HEADS, B_NOPE + B_ROPE)
    nope = u3[:, :, :B_NOPE].reshape(w_uq.shape[0], -1)
    rot = u3[:, :, B_NOPE:].reshape(w_uq.shape[0], -1)
    swapped = jnp.concatenate([-u3[:, :, B_NOPE + half:], u3[:, :, B_NOPE:B_NOPE + half]], axis=-1).reshape(w_uq.shape[0], -1)
    return w_in_cat, jnp.concatenate([nope, rot, swapped], axis=1), w_ukv


def _even_fwd(h, w, i, rider=None):
    s = h.shape[0]
    qa, ka, va, vat, c_q, c_kv, kr_blk, xn = _ev_in_fwd(h, w["mix_norm"][i:i + 1], w["ev_in_cat"], name="ev_in")
    cos32, sin32 = _rope_tables(s, 2)
    kro = _rope_fwd(kr_blk[:, :B_ROPE], kr_blk[:, B_ROPE:2 * B_ROPE], cos32, sin32, name="ev_k_rope")
    ss = jnp.stack([_alibi_slopes(), w["ev_sinks"].reshape(-1)])
    oa, lse_a = _swa_fwd_t(qa, ka, vat, ss, scale=A_HEAD_DIM ** -0.5, window=WINDOW, name="swa_fwd")
    cos256, sin256 = _rope_tables(s, 2 * B_HEADS)
    qb, xn_q = _ev_q_fwd(c_q, w["ev_cq_norm"], w["ev_q_cat"], cos256, sin256, name="ev_q_up")
    kb, vb, vbt, xn_kv = _ev_kv_fwd(c_kv, w["ev_ckv_norm"], w["ev_ukv"], kro, name="ev_kv_up")
    (ob, lse_b), rode = _causal_fwd_t(qb, kb, vbt, scale=(B_NOPE + B_ROPE) ** -0.5, name="mla_fwd", tq=512, hb=8, rider=rider)
    attn = jnp.concatenate([oa.reshape(-1, s), ob.reshape(-1, s)], axis=0)
    out = _mm_res_fwd(attn, w["ev_out"], h, scale=1.0, name="ev_out", a_t=True)
    sv = dict(h_in=h, xn=xn, c_q=c_q, c_kv=c_kv, xn_q=xn_q, xn_kv=xn_kv, qa=qa, ka=ka, va=va, oa=oa, lse_a=lse_a,
              ss=ss, qb=qb, kb=kb, vb=vb, ob=ob, lse_b=lse_b, attn=attn, cos32=cos32, sin32=sin32,
              cos256=cos256, sin256=sin256)
    return out, sv, rode


def _even_bwd(dh, w, sv, i, rider=None):
    s = dh.shape[0]
    half = B_ROPE // 2
    g = {}
    dattn = _mm_nt_t(dh, w["ev_out"], name="ev_out_dx")
    g["ev_w_out"] = _mm_tn(sv["attn"], dh, name="ev_out_dw", a_t=True)
    doa = dattn[:A_HEADS * A_HEAD_DIM].reshape(A_HEADS, A_HEAD_DIM, s)
    dob = dattn[A_HEADS * A_HEAD_DIM:].reshape(B_HEADS, B_V, s)
    first, then = rider if isinstance(rider, tuple) else (None, None)
    (dqa, dka, dva, dsink), brought = _swa_bwd_t(sv["qa"], sv["ka"], sv["va"], sv["oa"], doa, sv["lse_a"], sv["ss"],
                                                 scale=A_HEAD_DIM ** -0.5, window=WINDOW, name="swa_bwd", rider=first)
    if then is not None:
        rider = then(brought)
    g["ev_sinks"] = dsink[:, :, 0, 0].reshape(1, A_HEADS)
    (dqb, dkb, dvb), rode = _causal_bwd_t(sv["qb"], sv["kb"], sv["vb"], sv["ob"], dob, sv["lse_b"],
                                          scale=(B_NOPE + B_ROPE) ** -0.5, name="mla_bwd", tq=512, hb=4, rider=rider)
    dyq = _ev_q_merge(dqb, sv["cos256"], sv["sin256"], name="ev_q_merge")
    dwq = _mm_tn(sv["xn_q"], dyq, name="ev_q_up_dw")
    dcq, g["ev_cq_norm"] = _mm_nt_rmsbwd([(dyq, w["ev_q_cat"])], sv["c_q"], w["ev_cq_norm"], None, name="ev_q_up_dx")
    kq = sv["c_q"].shape[1]
    d_nope = dwq[:, :512].reshape(kq, B_HEADS, B_NOPE)
    d_rot = dwq[:, 512:768].reshape(kq, B_HEADS, B_ROPE)
    d_swp = dwq[:, 768:].reshape(kq, B_HEADS, B_ROPE)
    g["ev_w_uq"] = jnp.concatenate([d_nope, d_rot[:, :, :half] + d_swp[:, :, half:], d_rot[:, :, half:] - d_swp[:, :, :half]],
                                   axis=-1).reshape(kq, -1)
    dykv, dkr = _ev_kv_merge(dkb, dvb, sv["cos32"], sv["sin32"], name="ev_kv_merge")
    g["ev_w_ukv"] = _mm_tn(sv["xn_kv"], dykv, name="ev_kv_up_dw")
    dckv, g["ev_ckv_norm"] = _mm_nt_rmsbwd([(dykv, w["ev_ukv"])], sv["c_kv"], w["ev_ckv_norm"], None, name="ev_kv_up_dx")
    dycat = _ev_in_merge(dqa, dka, dva, dcq, dckv, dkr, name="ev_in_merge")
    dwin = _mm_tn(sv["xn"], dycat, name="ev_in_dw")
    base = 1184
    g["ev_w_in"] = jnp.concatenate([dwin[:, :base - B_ROPE],
                                    dwin[:, base - B_ROPE:base - half] + dwin[:, base + half:base + B_ROPE],
                                    dwin[:, base - half:base] - dwin[:, base:base + half]], axis=-1)
    dh_in, dnorm = _mm_nt_rmsbwd([(dycat, w["ev_in_cat"])], sv["h_in"], w["mix_norm"][i:i + 1], dh, name="ev_in_dx")
    return dh_in, dnorm, g, rode


def _odd_fwd(h, w, i, rider=None):
    s = h.shape[0]
    wd = C_HEADS * C_HEAD_DIM
    q, k, v, vt, y_f, xn = _fox_in_fwd(h, w["mix_norm"][i:i + 1], w["od_in_pad"], nheads=C_HEADS, dh=C_HEAD_DIM,
                                       q_ones=(0, 2, 3, 4), k_ones=(1,), name="od_in")
    scale = C_HEAD_DIM ** -0.5
    ft = y_f[:, :C_HEADS].T
    bf = w["od_b_f"].reshape(C_HEADS, 1)
    cb3 = _fox_gate_fwd(ft, bf, out_scale=-1.0 / scale, name="fox_gate_fwd")
    k = k + jnp.pad(cb3.transpose(1, 2, 0), ((0, 0), (0, 0), (C_HEAD_DIM + 2, LANES - C_HEAD_DIM - 5)))
    (o, lse), rode = _causal_fwd_t(q, k, vt, scale=scale, name="fox_fwd", tq=512, hb=16, rider=rider)
    attn = o.reshape(-1, s)
    out = _mm_res_fwd(attn, w["od_out"], h, scale=1.0, name="od_out", a_t=True)
    return out, dict(h_in=h, xn=xn, q=q, k=k, v=v, o=o, lse=lse, ft=ft, bf=bf, attn=attn), rode


def _odd_bwd(dh, w, sv, i, rider=None):
    s = dh.shape[0]
    g = {}
    dattn = _mm_nt_t(dh, w["od_out"], name="od_out_dx")
    g["od_w_out"] = _mm_tn(sv["attn"], dh, name="od_out_dw", a_t=True)
    do = dattn.reshape(C_HEADS, C_HEAD_DIM, s)
    scale = C_HEAD_DIM ** -0.5
    (dq, dk, dv), rode = _causal_bwd_t(sv["q"], sv["k"], sv["v"], sv["o"], do, sv["lse"], scale=scale, name="fox_bwd",
                                       tq=512, hb=4, rider=rider)
    dqkv, sums = _merge_heads(dq, dk, dv, dh=C_HEAD_DIM, q_col=C_HEAD_DIM + 1, k_col=C_HEAD_DIM, name="fox_merge")
    dft, dbf = _fox_gate_bwd(sums[:, :C_HEADS].T, sums[:, C_HEADS:2 * C_HEADS].T, sv["ft"], sv["bf"],
                             inv_scale=1.0 / scale, name="fox_gate_bwd")
    g["od_b_f"] = dbf.reshape(1, C_HEADS)
    wd = C_HEADS * C_HEAD_DIM
    df = jnp.pad(dft.T, ((0, 0), (0, LANES - C_HEADS)))
    g["od_w_in"] = jnp.concatenate([_mm_tn(sv["xn"], dqkv, name="od_in_dw"),
                                    _mm_tn(sv["xn"], df, name="od_in_dwf")[:, :C_HEADS]], axis=-1)
    dh_in, dnorm = _mm_nt_rmsbwd([(dqkv, w["od_in_pad"], 0), (df, w["od_in_pad"], 3 * wd // LANES)],
                                 sv["h_in"], w["mix_norm"][i:i + 1], dh, name="od_in_dx")
    return dh_in, dnorm, g, rode


def _kernel_weights(full, replicated):
    w = dict(replicated)
    _install_weights(w, {(n, i): a for n, per_layer in full.items() for i, a in enumerate(per_layer)})
    return w


def _install_weights(w, got):
    raw = w.setdefault("raw", {})
    raw.update(got)
    for (n, i), a in got.items():
        if n in ("ffa_w_gate_up", "ffa_w_down", "ffb_w_gate_up", "ffb_w_down"):
            w.setdefault(n[:3], {}).setdefault(i, {})["wgu" if n.endswith("gate_up") else "wd"] = a
        elif n in ("ple_w_gate", "ple_w_proj"):
            w.setdefault("ple_gate" if n.endswith("gate") else "ple_proj", {})[i] = a
    if "ev_in_cat" not in w and all((n, 0) in raw for n in ("ev_w_in", "ev_w_uq", "ev_w_ukv", "ev_w_out")):
        w["ev_in_cat"], w["ev_q_cat"], w["ev_ukv"] = _even_weights(raw["ev_w_in", 0], raw["ev_w_uq", 0], raw["ev_w_ukv", 0])
        w["ev_out"] = raw["ev_w_out", 0]
    if "od_in_pad" not in w and all((n, 0) in raw for n in ("od_w_in", "od_w_out")):
        od_in = raw["od_w_in", 0]
        w["od_in_pad"] = jnp.pad(od_in, ((0, 0), (0, (-od_in.shape[1]) % LANES)))
        w["od_out"] = raw["od_w_out", 0]


def _keys(names, layer):
    return tuple((n, layer) for n in names)


_FFA, _FFB, _PLE = ("ffa_w_gate_up", "ffa_w_down"), ("ffb_w_gate_up", "ffb_w_down"), ("ple_w_gate", "ple_w_proj")
_EV, _OD = ("ev_w_in", "ev_w_uq", "ev_w_ukv", "ev_w_out"), ("od_w_in", "od_w_out")
_GATHER_FIRST = _keys(_FFA[:1], 0)
_GATHER_RIDES = {("ffa", 0): _keys(_FFA[1:] + _EV, 0), ("mix", 0): _keys(_FFB + _PLE, 0) + _keys(_FFA[:1], 1),
                 ("ffb", 0): _keys(_FFA[1:], 1), ("ffa", 1): _keys(_OD, 0), ("mix", 1): _keys(_FFB + _PLE, 1)}
_REDUCE_RIDES = {("mix", 1): _keys(_FFB + _PLE, 1), ("mix", 0): _keys(_FFA, 1) + _keys(_OD, 0) + _keys(_FFB + _PLE, 0),
                 ("ffa", 0): _keys(_EV, 0)}
_REDUCE_OWN = ("ffa", 0)
_SWAP_AHEAD = {("ffb", 1): ("mix", 1)}


def _local_step(x, p, tgt, w, ex=None):
    depth = p.shape[0]

    def gather_behind(host, fn, *args):
        keys = None if ex is None else _GATHER_RIDES.get(host)
        if keys is None:
            return fn(*args, None)[:-1]
        done = []

        def install(rode):
            if not done:
                _install_weights(w, ex.gather_finish(keys, rode, name=f"weight_forward_{host[0]}{host[1]}"))
                done.append(True)

        res = fn(*args, ex.gather_rider(keys), install) if fn is _ffn_fwd else fn(*args, ex.gather_rider(keys))
        install(res[-1])
        return res[:-1]

    h = x
    saved = []
    for i in range(depth):
        sv = {}
        h, sv["ffa"] = gather_behind(("ffa", i), _ffn_fwd, h, w["ffa_norm"][i:i + 1], w["ffa"][i], f"ffa{i}")
        h, sv["mix"] = gather_behind(("mix", i), _even_fwd if i % 2 == 0 else _odd_fwd, h, w, i)
        h, sv["ffb"] = gather_behind(("ffb", i), _ffn_fwd, h, w["ffb_norm"][i:i + 1], w["ffb"][i], f"ffb{i}")
        h_in = h
        h, xn, gate, pp = _ple_fwd(h, w["ple_norm"][i:i + 1], w["ple_gate"][i], p[i], w["ple_proj"][i], name=f"ple{i}")
        sv["ple"] = dict(h_in=h_in, xn=xn, gate=gate, pp=pp)
        saved.append(sv)
    loss_vec, dh, d_final = _final_loss(h, w["final_norm"].reshape(1, -1), tgt, name="final_loss")

    per_layer = [dict() for _ in range(depth)]
    mats = {}
    grads = {}

    pending = {}

    def reduce_behind(host, fn, *args):
        keys = None if ex is None else _REDUCE_RIDES.get(host)
        ahead = None if ex is None else _SWAP_AHEAD.get(host)
        if keys is None and ahead is None:
            return fn(*args, None)[:-1]
        if ahead is not None:
            got, ctxs = {}, []

            def note_wd(d_wd):
                got[f"{host[0]}_w_down", host[1]] = d_wd

            def swap_now(brought, d_wgu):
                got[f"{host[0]}_w_gate_up", host[1]] = d_wgu
                swap, ctx = ex.swap_rider(_REDUCE_RIDES[ahead], {**mats, **got})
                ctxs.append(ctx)
                return swap

            def stash(brought):
                pending[ahead] = ex.after_swap(ctxs[0], brought)

            return fn(*args, None, (note_wd, swap_now, stash))[:-1]
        states = []
        if fn is _even_bwd:
            swap, ctx = ex.swap_rider(keys, mats)

            def then(brought):
                states.append(ex.after_swap(ctx, brought))
                return states[0][0]

            res = fn(*args, (swap, then))
        else:
            states.append(pending.pop(host, None) or ex.reduce_begin(keys, mats, tag=f"{host[0]}{host[1]}"))
            if fn is _ffn_bwd and host == _REDUCE_OWN:
                own = []

                def wd_ready(d_wd):
                    own.append(ex.reduce_begin(_keys(_FFA[1:], 0), {("ffa_w_down", 0): d_wd}, tag="own_wd"))
                    return own[0][0]

                def wgu_ready(brought, d_wgu):
                    ex.reduce_finish(own[0], brought)
                    own.append(ex.reduce_begin(_keys(_FFA[:1], 0), {("ffa_w_gate_up", 0): d_wgu}, tag="own_wgu"))
                    return own[1][0]

                res = fn(*args, states[0][0], (wd_ready, wgu_ready, lambda brought: ex.reduce_finish(own[1], brought)))
            else:
                res = fn(*args, states[0][0])
        ex.reduce_finish(states[0], res[-1])
        return res[:-1]

    for i in reversed(range(depth)):
        sv, gl = saved[i], per_layer[i]
        dz, dpp = _ple_bwd_elem(dh, sv["ple"]["gate"], sv["ple"]["pp"], name=f"ple{i}_bwd")
        mats["ple_w_gate", i] = _mm_tn(sv["ple"]["xn"], dz, name=f"ple{i}_dwg")
        mats["ple_w_proj", i] = _mm_tn(p[i], dpp, name=f"ple{i}_dwp")
        dh, gl["ple_norm"] = _mm_nt_rmsbwd([(dz, w["ple_gate"][i])], sv["ple"]["h_in"], w["ple_norm"][i:i + 1], dh,
                                           name=f"ple{i}_dx")
        dh, gl["ffb_norm"], mats["ffb_w_gate_up", i], mats["ffb_w_down", i] = reduce_behind(
            ("ffb", i), _ffn_bwd, dh, w["ffb_norm"][i:i + 1], w["ffb"][i], sv["ffb"], f"ffb{i}")
        dh, gl["mix_norm"], gm = reduce_behind(("mix", i), _even_bwd if i % 2 == 0 else _odd_bwd, dh, w, sv["mix"], i)
        for n, g in gm.items():
            if n in REPLICATED:
                grads[n] = g
            else:
                mats[n, 0] = g
        dh, gl["ffa_norm"], mats["ffa_w_gate_up", i], mats["ffa_w_down", i] = reduce_behind(
            ("ffa", i), _ffn_bwd, dh, w["ffa_norm"][i:i + 1], w["ffa"][i], sv["ffa"], f"ffa{i}")
    grads["final_norm"] = d_final.reshape(-1)
    for n in ("ffa_norm", "mix_norm", "ffb_norm", "ple_norm"):
        grads[n] = jnp.concatenate([per_layer[i][n] for i in range(depth)], axis=0)
    if ex is None:
        for n, _ in SHARDED:
            grads[n] = [mats[n, i] for i in range(depth) if (n, i) in mats]
    return loss_vec[0, 0], dh, grads


def _cut_mode(local_shape, axis, ncols):
    return "cols" if axis == 2 and ncols % LANES == 0 else "blk"


class _Exchange:
    def __init__(self, wts):
        self.place = jnp.stack([2 * lax.axis_index("x") + lax.axis_index("y"), lax.axis_index("c")]).astype(jnp.int32)
        self.info = {}
        for n, axis in SHARDED:
            wb = wts[n].astype(BF16)
            mode = _cut_mode(wb.shape, axis, wb.shape[2])
            for i in range(wb.shape[0]):
                self.info[n, i] = dict(shard=wb[i], mode=mode, axis=axis)
        self.halves = {}

    def _modes(self, keys):
        return [self.info[k]["mode"] for k in keys]

    def gather_rider(self, keys):
        return _gather_rider([self.info[k]["shard"] for k in keys], self._modes(keys))

    def gather_finish(self, keys, landed, *, name):
        outs = _gather_forward(landed, [self.info[k]["shard"].shape for k in keys], self._modes(keys), name=name)
        got = {}
        for k, dst in zip(keys, outs):
            if self.info[k]["mode"] == "blk":
                dst = dst.reshape(-1, dst.shape[2]) if self.info[k]["axis"] == 1 else jnp.moveaxis(dst, 0, 1).reshape(dst.shape[1], -1)
            got[k] = dst
        return got

    def gather(self, keys, *, name):
        return self.gather_finish(keys, _run_rider(self.gather_rider(keys), name=name), name=name + "_forward")

    def swap_rider(self, keys, mats):
        modes = self._modes(keys)
        arrs = []
        for k in keys:
            g2, (rr, cc) = mats[k], self.info[k]["shard"].shape
            if self.info[k]["mode"] == "blk":
                g2 = g2.reshape(N_CHIPS, rr, cc) if self.info[k]["axis"] == 1 else g2.reshape(rr, N_CHIPS, cc).transpose(1, 0, 2)
            arrs.append(g2)
        return _swap_rider(arrs, modes), (keys, modes, arrs)

    def after_swap(self, ctx, landed):
        keys, modes, arrs = ctx
        parts = []
        for (n, i), m, a, l in zip(keys, modes, arrs, landed):
            pt = _rs_pair_add(_blk_view(a, m), _blk_view(l, m), self.place, name=f"rs_pair_add_{n}{i}")
            parts.append(pt[0] if m == "cols" else pt)
        return _exchange_rider(parts, modes), keys, parts

    def reduce_begin(self, keys, mats, *, tag):
        rider, ctx = self.swap_rider(keys, mats)
        return self.after_swap(ctx, _run_rider(rider, name=f"rs_pair_swap_{tag}"))

    def reduce_finish(self, state, landed):
        _, keys, parts = state
        for (n, i), m, pt, l in zip(keys, self._modes(keys), parts, landed):
            self.halves[n, i] = _rs_chip_sum(pt, l, m, self.place, name=f"rs_chip_sum_{n}{i}")

    def reduce(self, keys, mats, *, tag):
        state = self.reduce_begin(keys, mats, tag=tag)
        self.reduce_finish(state, _run_rider(state[0], name=f"rs_chip_exchange_{tag}"))

    def join(self, wts):
        keys = list(self.info)
        joined = dict(zip(keys, _rs_pair_join([self.halves[k] for k in keys], name="rs_pair_join")))
        return {n: jnp.stack([joined[n, i] for i in range(wts[n].shape[0])]).reshape(wts[n].shape) for n, _ in SHARDED}


def _small_rows(vals):
    rows = []
    for n in REPLICATED:
        v = vals[n].reshape(-1)
        rows.append(jnp.pad(v, (0, (-v.shape[0]) % FLAT_COLS)).reshape(-1, FLAT_COLS))
    out = jnp.concatenate(rows, axis=0)
    return jnp.pad(out, ((0, (-out.shape[0]) % 8), (0, 0)))


def kernel(x, p, ffa_norm, ffa_w_gate_up, ffa_w_down, mix_norm, ffb_norm, ffb_w_gate_up, ffb_w_down, ple_norm, ple_w_gate, ple_w_proj, ev_w_in, ev_sinks, ev_cq_norm, ev_w_uq, ev_ckv_norm, ev_w_ukv, ev_w_out, od_w_in, od_b_f, od_w_out, final_norm, loss_target, m_ffa_norm, m_ffa_w_gate_up, m_ffa_w_down, m_mix_norm, m_ffb_norm, m_ffb_w_gate_up, m_ffb_w_down, m_ple_norm, m_ple_w_gate, m_ple_w_proj, m_ev_w_in, m_ev_sinks, m_ev_cq_norm, m_ev_w_uq, m_ev_ckv_norm, m_ev_w_ukv, m_ev_w_out, m_od_w_in, m_od_b_f, m_od_w_out, m_final_norm, v_ffa_norm, v_ffa_w_gate_up, v_ffa_w_down, v_mix_norm, v_ffb_norm, v_ffb_w_gate_up, v_ffb_w_down, v_ple_norm, v_ple_w_gate, v_ple_w_proj, v_ev_w_in, v_ev_sinks, v_ev_cq_norm, v_ev_w_uq, v_ev_ckv_norm, v_ev_w_ukv, v_ev_w_out, v_od_w_in, v_od_b_f, v_od_w_out, v_final_norm):
    env = dict(locals())
    wts = {n: env[n] for n in WEIGHT_ORDER}
    mom1 = {n: env["m_" + n] for n in WEIGHT_ORDER}
    mom2 = {n: env["v_" + n] for n in WEIGHT_ORDER}
    ex = _Exchange(wts)

    w = {n: wts[n] for n in REPLICATED}
    _install_weights(w, ex.gather(_GATHER_FIRST, name="weight_gather_first"))

    loss_part, grad_x, grads = _local_step(x[0], p[:, 0], loss_target[0], w, ex)
    loss = lax.psum(loss_part, ("x", "y", "c"))
    gout = ex.join(wts)
    small = _allreduce_small(_small_rows(grads), name="small_allreduce")
    r0 = 0
    for n in REPLICATED:
        size = int(np.prod(wts[n].shape))
        nr = -(-size // FLAT_COLS)
        gout[n] = small[r0:r0 + nr].reshape(-1)[:size].reshape(wts[n].shape)
        r0 += nr

    delta, new_m, new_v = {}, {}, {}
    for n in WEIGHT_ORDER:
        delta[n], new_m[n], new_v[n] = _adamw(wts[n], gout[n], mom1[n], mom2[n], name="adamw_" + n)
    return (loss, grad_x[None], *[gout[n] for n in WEIGHT_ORDER], *[delta[n] for n in WEIGHT_ORDER],
            *[new_m[n] for n in WEIGHT_ORDER], *[new_v[n] for n in WEIGHT_ORDER])
```

```python
import functools
import itertools
import math

import numpy as np
import jax
import jax.numpy as jnp
from jax import lax
from jax.experimental import pallas as pl
from jax.experimental.pallas import tpu as pltpu

F32 = jnp.float32
BF16 = jnp.bfloat16
NT = (((1,), (1,)), ((), ()))
TN = (((0,), (0,)), ((), ()))
MESH = pl.DeviceIdType.MESH

RMS_EPS = 1e-6
FFN_RES_SCALE = 0.5
A_HEADS, A_KV_HEADS, A_HEAD_DIM, WINDOW = 8, 2, 64, 128
B_HEADS, B_Q_LORA, B_KV_LORA, B_NOPE, B_ROPE, B_V = 8, 256, 128, 64, 32, 64
ROPE_THETA = 10000.0
C_HEADS, C_HEAD_DIM = 16, 64
ADAM_LR, ADAM_B1, ADAM_B2, ADAM_EPS, ADAM_WD, ADAM_STEP = 0.001, 0.9, 0.999, 1e-08, 0.01, 10

N_CHIPS = 4
LANES = 128
FLAT_COLS = 1024
MASK_VALUE = -1e30
VMEM_LIMIT = 48 * 2**20

SHARDED = (
    ("ffa_w_gate_up", 2), ("ffa_w_down", 1), ("ffb_w_gate_up", 2), ("ffb_w_down", 1),
    ("ple_w_gate", 1), ("ple_w_proj", 2), ("ev_w_in", 2), ("ev_w_uq", 2), ("ev_w_ukv", 2),
    ("ev_w_out", 1), ("od_w_in", 2), ("od_w_out", 1))
REPLICATED = ("ffa_norm", "mix_norm", "ffb_norm", "ple_norm", "final_norm",
              "ev_sinks", "ev_cq_norm", "ev_ckv_norm", "od_b_f")
WEIGHT_ORDER = ("ffa_norm", "ffa_w_gate_up", "ffa_w_down", "mix_norm", "ffb_norm", "ffb_w_gate_up",
                "ffb_w_down", "ple_norm", "ple_w_gate", "ple_w_proj", "ev_w_in", "ev_sinks",
                "ev_cq_norm", "ev_w_uq", "ev_ckv_norm", "ev_w_ukv", "ev_w_out", "od_w_in", "od_b_f",
                "od_w_out", "final_norm")


def _cp(*sem):
    return pltpu.CompilerParams(dimension_semantics=sem, vmem_limit_bytes=VMEM_LIMIT)


def _sigmoid(z):
    return 1.0 / (1.0 + jnp.exp(-z))


def _rms_stats(xv):
    r = lax.rsqrt(jnp.mean(xv * xv, axis=-1, keepdims=True) + RMS_EPS)
    return r, xv * r


def _rms_bwd(dxn, xv, g):
    r, xhat = _rms_stats(xv)
    u = dxn * g
    dx = r * (u - xhat * jnp.mean(u * xhat, axis=-1, keepdims=True))
    return dx, dxn * xhat


def _col_tile(k_rows, n, budget_bytes=6 * 2**20):
    if k_rows * n * 4 <= budget_bytes or n % LANES:
        return n
    units = n // LANES
    best = LANES
    for d in range(1, units + 1):
        if units % d == 0 and k_rows * d * LANES * 4 <= budget_bytes:
            best = d * LANES
    return best


def _row_tile(rows, cols, target_elems=2**18):
    if rows * cols <= target_elems or rows % 8:
        return rows
    best = 8
    for d in range(8, rows + 1, 8):
        if rows % d == 0 and d * cols <= target_elems:
            best = d
    return best


def _fox_in_fwd(x, g, w, *, nheads, dh, q_ones, k_ones, name, tm=512):
    s, k = x.shape
    n = w.shape[1]
    wd = nheads * dh
    spare = LANES - dh

    def body(x_ref, g_ref, w_ref, q_ref, k_ref, v_ref, vt_ref, f_ref, xn_ref):
        _, xhat = _rms_stats(x_ref[...])
        xn = (xhat * g_ref[...]).astype(BF16)
        xn_ref[...] = xn
        y = jnp.dot(xn, w_ref[...], preferred_element_type=F32)
        f_ref[...] = y[:, 3 * wd:]
        lane = lax.broadcasted_iota(jnp.int32, (tm, spare), 1)

        def fill(cols):
            return functools.reduce(jnp.logical_or, [lane == c for c in cols]).astype(F32)

        q_fill, k_fill = fill(q_ones), fill(k_ones)
        for h in range(nheads):
            q_ref[h] = jnp.concatenate([y[:, h * dh:(h + 1) * dh], q_fill], axis=-1).astype(BF16)
            k_ref[h] = jnp.concatenate([y[:, wd + h * dh:wd + (h + 1) * dh], k_fill], axis=-1).astype(BF16)
            vh = y[:, 2 * wd + h * dh:2 * wd + (h + 1) * dh]
            v_ref[h] = vh.astype(BF16)
            vt_ref[h] = vh.T.astype(BF16)

    wide = pl.BlockSpec((nheads, tm, LANES), lambda i: (0, i, 0))
    return pl.pallas_call(
        body, name=name, grid=(s // tm,),
        in_specs=[pl.BlockSpec((tm, k), lambda i: (i, 0)), pl.BlockSpec((1, k), lambda i: (0, 0)),
                  pl.BlockSpec((k, n), lambda i: (0, 0))],
        out_specs=[wide, wide, pl.BlockSpec((nheads, tm, dh), lambda i: (0, i, 0)),
                   pl.BlockSpec((nheads, dh, tm), lambda i: (0, 0, i)), pl.BlockSpec((tm, LANES), lambda i: (i, 0)),
                   pl.BlockSpec((tm, k), lambda i: (i, 0))],
        out_shape=[jax.ShapeDtypeStruct((nheads, s, LANES), BF16)] * 2
        + [jax.ShapeDtypeStruct((nheads, s, dh), BF16), jax.ShapeDtypeStruct((nheads, dh, s), BF16),
           jax.ShapeDtypeStruct((s, LANES), F32), jax.ShapeDtypeStruct((s, k), BF16)],
        compiler_params=_cp("arbitrary"))(x, g, w)


def _merge_heads(dq, dk, dvt, *, dh, q_col, k_col, name, tm=512):
    nheads, s, _ = dq.shape

    def body(dq_ref, dk_ref, dvt_ref, o_ref, cols_ref):
        pieces = [dq_ref[h][:, :dh] for h in range(nheads)] + [dk_ref[h][:, :dh] for h in range(nheads)]
        pieces += [dvt_ref[h].T for h in range(nheads)]
        o_ref[...] = jnp.concatenate(pieces, axis=-1)
        lane = lax.broadcasted_iota(jnp.int32, (tm, LANES), 1)
        cols = jnp.zeros((tm, LANES), F32)
        for h in range(nheads):
            cols = jnp.where(lane == h, jnp.broadcast_to(dq_ref[h][:, q_col:q_col + 1], (tm, LANES)), cols)
            cols = jnp.where(lane == nheads + h, jnp.broadcast_to(dk_ref[h][:, k_col:k_col + 1], (tm, LANES)), cols)
        cols_ref[...] = cols

    wide = pl.BlockSpec((nheads, tm, LANES), lambda i: (0, i, 0))
    return pl.pallas_call(
        body, name=name, grid=(s // tm,),
        in_specs=[wide, wide, pl.BlockSpec((nheads, dh, tm), lambda i: (0, 0, i))],
        out_specs=[pl.BlockSpec((tm, 3 * nheads * dh), lambda i: (i, 0)), pl.BlockSpec((tm, LANES), lambda i: (i, 0))],
        out_shape=[jax.ShapeDtypeStruct((s, 3 * nheads * dh), F32), jax.ShapeDtypeStruct((s, LANES), F32)],
        compiler_params=_cp("arbitrary"))(dq, dk, dvt)


def _row_call(body, n_rows, ins, outs, *, name, tm=512):
    def spec(a, axis):
        shape = a.shape
        if axis is None:
            return pl.BlockSpec(shape, lambda i: (0,) * len(shape))
        blk = tuple(tm if d == axis else n for d, n in enumerate(shape))
        return pl.BlockSpec(blk, lambda i: tuple(i if d == axis else 0 for d in range(len(shape))))

    return pl.pallas_call(
        body, name=name, grid=(n_rows // tm,), in_specs=[spec(a, ax) for a, ax in ins],
        out_specs=[spec(a, ax) for a, ax in outs], out_shape=[a for a, _ in outs],
        compiler_params=_cp("arbitrary"))(*[a for a, _ in ins])


def _sds(shape, dtype):
    return jax.ShapeDtypeStruct(shape, dtype)


def _ev_in_fwd(x, g, w, *, name):
    s, k = x.shape
    d = A_HEAD_DIM

    def body(x_ref, g_ref, w_ref, q_ref, k_ref, v_ref, vt_ref, cq_ref, ckv_ref, kr_ref, xn_ref):
        _, xhat = _rms_stats(x_ref[...])
        xn = (xhat * g_ref[...]).astype(BF16)
        xn_ref[...] = xn
        y = jnp.dot(xn, w_ref[...], preferred_element_type=F32)
        for h in range(A_HEADS):
            q_ref[h] = y[:, h * d:(h + 1) * d].astype(BF16)
        for h in range(A_KV_HEADS):
            k_ref[h] = y[:, 512 + h * d:512 + (h + 1) * d].astype(BF16)
            vh = y[:, 640 + h * d:640 + (h + 1) * d]
            v_ref[h] = vh.astype(BF16)
            vt_ref[h] = vh.T.astype(BF16)
        cq_ref[...] = y[:, 768:1024]
        ckv_ref[...] = y[:, 1024:1152]
        kr_ref[...] = y[:, 1152:1280]

    return _row_call(
        body, s, [(x, 0), (g, None), (w, None)],
        [(_sds((A_HEADS, s, d), BF16), 1), (_sds((A_KV_HEADS, s, d), BF16), 1), (_sds((A_KV_HEADS, s, d), BF16), 1),
         (_sds((A_KV_HEADS, d, s), BF16), 2), (_sds((s, B_Q_LORA), F32), 0), (_sds((s, B_KV_LORA), F32), 0),
         (_sds((s, LANES), F32), 0), (_sds((s, k), BF16), 0)], name=name)


def _ev_q_fwd(x, g, w, cos, sin, *, name):
    s, k = x.shape
    rot = B_HEADS * B_ROPE

    def body(x_ref, g_ref, w_ref, c_ref, s_ref, q_ref, xn_ref):
        _, xhat = _rms_stats(x_ref[...])
        xn = (xhat * g_ref[...]).astype(BF16)
        xn_ref[...] = xn
        y = jnp.dot(xn, w_ref[...], preferred_element_type=F32)
        ro = y[:, 512:512 + rot] * c_ref[...] + y[:, 512 + rot:] * s_ref[...]
        zero = jnp.zeros((y.shape[0], LANES - B_NOPE - B_ROPE), F32)
        for h in range(B_HEADS):
            q_ref[h] = jnp.concatenate([y[:, h * B_NOPE:(h + 1) * B_NOPE], ro[:, h * B_ROPE:(h + 1) * B_ROPE], zero],
                                       axis=-1).astype(BF16)

    return _row_call(body, s, [(x, 0), (g, None), (w, None), (cos, 0), (sin, 0)],
                     [(_sds((B_HEADS, s, LANES), BF16), 1), (_sds((s, k), BF16), 0)], name=name)


def _ev_kv_fwd(x, g, w, kro, *, name):
    s, k = x.shape
    per = B_NOPE + B_V

    def body(x_ref, g_ref, w_ref, kr_ref, k_ref, v_ref, vt_ref, xn_ref):
        _, xhat = _rms_stats(x_ref[...])
        xn = (xhat * g_ref[...]).astype(BF16)
        xn_ref[...] = xn
        y = jnp.dot(xn, w_ref[...], preferred_element_type=F32)
        kr = kr_ref[...]
        zero = jnp.zeros((y.shape[0], LANES - B_NOPE - B_ROPE), F32)
        for h in range(B_HEADS):
            k_ref[h] = jnp.concatenate([y[:, h * per:h * per + B_NOPE], kr, zero], axis=-1).astype(BF16)
            vh = y[:, h * per + B_NOPE:(h + 1) * per]
            v_ref[h] = vh.astype(BF16)
            vt_ref[h] = vh.T.astype(BF16)

    return _row_call(body, s, [(x, 0), (g, None), (w, None), (kro, 0)],
                     [(_sds((B_HEADS, s, LANES), BF16), 1), (_sds((B_HEADS, s, B_V), BF16), 1),
                      (_sds((B_HEADS, B_V, s), BF16), 2), (_sds((s, k), BF16), 0)], name=name)


def _ev_q_merge(dq, cos, sin, *, name):
    nh, s, _ = dq.shape

    def body(dq_ref, c_ref, s_ref, o_ref):
        dro = jnp.concatenate([dq_ref[h][:, B_NOPE:B_NOPE + B_ROPE] for h in range(nh)], axis=-1)
        o_ref[...] = jnp.concatenate([dq_ref[h][:, :B_NOPE] for h in range(nh)] + [dro * c_ref[...], dro * s_ref[...]], axis=-1)

    return _row_call(body, s, [(dq, 1), (cos, 0), (sin, 0)], [(_sds((s, 2 * nh * B_NOPE), F32), 0)], name=name)[0]


def _ev_kv_merge(dk, dvt, cos, sin, *, name):
    nh, s, _ = dk.shape

    def body(dk_ref, dvt_ref, c_ref, s_ref, o_ref, kr_ref):
        pieces = []
        tot = None
        for h in range(nh):
            pieces += [dk_ref[h][:, :B_NOPE], dvt_ref[h].T]
            rot = dk_ref[h][:, B_NOPE:B_NOPE + B_ROPE]
            tot = rot if tot is None else tot + rot
        o_ref[...] = jnp.concatenate(pieces, axis=-1)
        kr_ref[...] = jnp.concatenate([tot * c_ref[...], tot * s_ref[...], jnp.zeros((tot.shape[0], LANES - 2 * B_ROPE), F32)],
                                      axis=-1)

    return _row_call(body, s, [(dk, 1), (dvt, 2), (cos, 0), (sin, 0)],
                     [(_sds((s, nh * (B_NOPE + B_V)), F32), 0), (_sds((s, LANES), F32), 0)], name=name)


def _ev_in_merge(dq, dk, dvt, dcq, dckv, dkr, *, name):
    s = dcq.shape[0]

    def body(dq_ref, dk_ref, dvt_ref, cq_ref, ckv_ref, kr_ref, o_ref):
        pieces = [dq_ref[h] for h in range(A_HEADS)] + [dk_ref[h] for h in range(A_KV_HEADS)]
        pieces += [dvt_ref[h].T for h in range(A_KV_HEADS)] + [cq_ref[...], ckv_ref[...], kr_ref[...]]
        o_ref[...] = jnp.concatenate(pieces, axis=-1)

    return _row_call(body, s, [(dq, 1), (dk, 1), (dvt, 2), (dcq, 0), (dckv, 0), (dkr, 0)],
                     [(_sds((s, 1280), F32), 0)], name=name)[0]


def _ffn_up(x, g, wgu, *, name, tm=512, rider=None):
    s, k = x.shape
    f = wgu.shape[1] // 2
    tn = _col_tile(k, f)
    nj = f // tn

    def body(x_ref, g_ref, wg_ref, wu_ref, dgate_ref, dup_ref, act_ref, xn_ref, xn_sc):
        @pl.when(pl.program_id(1) == 0)
        def _():
            _, xhat = _rms_stats(x_ref[...])
            xn = (xhat * g_ref[...]).astype(BF16)
            xn_sc[...] = xn
            xn_ref[...] = xn

        xn = xn_sc[...]
        gg = jnp.dot(xn, wg_ref[...], preferred_element_type=F32)
        uu = jnp.dot(xn, wu_ref[...], preferred_element_type=F32)
        sg = _sigmoid(gg)
        silu = gg * sg
        dgate_ref[...] = (uu * (sg * (1.0 + gg * (1.0 - sg)))).astype(BF16)
        dup_ref[...] = silu.astype(BF16)
        act_ref[...] = (silu * uu).astype(BF16)

    tile = pl.BlockSpec((tm, tn), lambda i, j: (i, j))
    return _call_with_rider(
        body, rider, name=name, grid=(s // tm, nj),
        in_specs=[pl.BlockSpec((tm, k), lambda i, j: (i, 0)), pl.BlockSpec((1, k), lambda i, j: (0, 0)),
                  pl.BlockSpec((k, tn), lambda i, j: (0, j)), pl.BlockSpec((k, tn), lambda i, j: (0, j + nj))],
        out_specs=[tile, tile, tile, pl.BlockSpec((tm, k), lambda i, j: (i, 0))],
        out_shape=[jax.ShapeDtypeStruct((s, f), BF16)] * 3 + [jax.ShapeDtypeStruct((s, k), BF16)],
        scratch_shapes=[pltpu.VMEM((tm, k), BF16)],
        compiler_params=_cp("arbitrary", "arbitrary"), args=(x, g, wgu, wgu))


def _mm_res_fwd(a, w, res, *, scale, name, tm=512, a_t=False):
    k, n = w.shape
    s = res.shape[0]

    def body(a_ref, w_ref, r_ref, o_ref):
        prod = (lax.dot_general(a_ref[...], w_ref[...], TN, preferred_element_type=F32) if a_t
                else jnp.dot(a_ref[...], w_ref[...], preferred_element_type=F32))
        o_ref[...] = r_ref[...] + scale * prod

    a_spec = pl.BlockSpec((k, tm), lambda i: (0, i)) if a_t else pl.BlockSpec((tm, k), lambda i: (i, 0))
    return pl.pallas_call(
        body, name=name, grid=(s // tm,),
        in_specs=[a_spec, pl.BlockSpec((k, n), lambda i: (0, 0)),
                  pl.BlockSpec((tm, n), lambda i: (i, 0))],
        out_specs=pl.BlockSpec((tm, n), lambda i: (i, 0)),
        out_shape=jax.ShapeDtypeStruct((s, n), F32),
        compiler_params=_cp("arbitrary"))(a, w, res)


def _ffn_down_bwd(dh, wd, dact_dgate, dact_dup, *, scale, name, tm=512, rider=None):
    s, d = dh.shape
    f = wd.shape[0]
    tn = _col_tile(d, f)

    def body(dh_ref, wd_ref, fg_ref, fu_ref, dg_ref, du_ref):
        dhb = (dh_ref[...] * scale).astype(BF16)
        da = lax.dot_general(dhb, wd_ref[...], NT, preferred_element_type=F32)
        dg_ref[...] = (da * fg_ref[...].astype(F32)).astype(BF16)
        du_ref[...] = (da * fu_ref[...].astype(F32)).astype(BF16)

    tile = pl.BlockSpec((tm, tn), lambda i, j: (i, j))
    return _call_with_rider(
        body, rider, name=name, grid=(s // tm, f // tn),
        in_specs=[pl.BlockSpec((tm, d), lambda i, j: (i, 0)), pl.BlockSpec((tn, d), lambda i, j: (j, 0)), tile, tile],
        out_specs=[tile, tile],
        out_shape=[jax.ShapeDtypeStruct((s, f), BF16)] * 2, scratch_shapes=[],
        compiler_params=_cp("arbitrary", "arbitrary"), args=(dh, wd, dact_dgate, dact_dup))


def _mm_tn(a, bs, *, name, b_scale=1.0, ts=512, rider=None, a_t=False):
    bs = list(bs) if isinstance(bs, (list, tuple)) else [bs]
    k, s = a.shape if a_t else a.shape[::-1]
    n = bs[0].shape[1]
    tn = _col_tile(k, n, 12 * 2**20)
    per = n // tn

    def body(a_ref, *refs):
        b_refs, o_ref = refs[:-1], refs[-1]
        j = pl.program_id(0)

        @pl.when(pl.program_id(1) == 0)
        def _():
            o_ref[...] = jnp.zeros_like(o_ref)

        for m, b_ref in enumerate(b_refs):
            def acc(b_ref=b_ref):
                bv = b_ref[...]
                if b_scale != 1.0:
                    bv = bv * b_scale
                av = a_ref[...].astype(BF16)
                o_ref[...] += (jnp.dot(av, bv.astype(BF16), preferred_element_type=F32) if a_t
                               else lax.dot_general(av, bv.astype(BF16), TN, preferred_element_type=F32))

            if len(b_refs) == 1:
                acc()
            else:
                pl.when(jnp.logical_and(j >= m * per, j < (m + 1) * per))(acc)

    def b_spec(m):
        def idx(j, t):
            mine = jnp.logical_and(j >= m * per, j < (m + 1) * per)
            return (jnp.where(mine, t, 0), jnp.clip(j - m * per, 0, per - 1))
        return pl.BlockSpec((ts, tn), idx)

    (out,), rode = _call_with_rider(
        body, rider, name=name, grid=(per * len(bs), s // ts),
        in_specs=[pl.BlockSpec((k, ts), lambda j, t: (0, t)) if a_t else pl.BlockSpec((ts, k), lambda j, t: (t, 0))]
        + [b_spec(m) for m in range(len(bs))],
        out_specs=[pl.BlockSpec((k, tn), lambda j, t: (0, j))],
        out_shape=[jax.ShapeDtypeStruct((k, n * len(bs)), F32)], scratch_shapes=[],
        compiler_params=_cp("arbitrary", "arbitrary"), args=(a, *bs))
    return out if rider is None else (out, rode)


def _mm_nt_t(dy, w, *, name, tm=512):
    s, n = dy.shape
    k = w.shape[0]

    def body(dy_ref, w_ref, o_ref):
        o_ref[...] = lax.dot_general(w_ref[...], dy_ref[...].astype(BF16), NT, preferred_element_type=F32).astype(BF16)

    return pl.pallas_call(
        body, name=name, grid=(s // tm,),
        in_specs=[pl.BlockSpec((tm, n), lambda i: (i, 0)), pl.BlockSpec((k, n), lambda i: (0, 0))],
        out_specs=pl.BlockSpec((k, tm), lambda i: (0, i)),
        out_shape=jax.ShapeDtypeStruct((k, s), BF16),
        compiler_params=_cp("arbitrary"))(dy, w)


def _mm_nt_rmsbwd(pairs, x, g, dres, *, name, tm=512, rider=None):
    s, k = x.shape
    npairs = len(pairs)
    pairs = [pr if len(pr) == 3 else (pr[0], pr[1], 0) for pr in pairs]

    def body(*refs):
        dy_refs = refs[0:2 * npairs:2]
        w_refs = refs[1:2 * npairs:2]
        rest = refs[2 * npairs:]
        x_ref, g_ref = rest[0], rest[1]
        if dres is None:
            dx_ref, dg_ref = rest[2], rest[3]
        else:
            dres_ref, dx_ref, dg_ref = rest[2], rest[3], rest[4]
        dxn = None
        for dy_ref, w_ref in zip(dy_refs, w_refs):
            t = lax.dot_general(dy_ref[...].astype(BF16), w_ref[...], NT, preferred_element_type=F32)
            dxn = t if dxn is None else dxn + t
        dx, dgrow = _rms_bwd(dxn, x_ref[...], g_ref[...])
        if dres is not None:
            dx = dx + dres_ref[...]
        dx_ref[...] = dx

        @pl.when(pl.program_id(0) == 0)
        def _():
            dg_ref[...] = jnp.zeros_like(dg_ref)

        dg_ref[...] += jnp.sum(dgrow, axis=0, keepdims=True)

    in_specs, args = [], []
    for dy, w, cb in pairs:
        n = dy.shape[1]
        in_specs += [pl.BlockSpec((tm, n), lambda i: (i, 0)),
                     pl.BlockSpec((k, n), lambda i, cb=cb: (0, cb), pipeline_mode=pl.Buffered(1))]
        args += [dy, w]
    row = pl.BlockSpec((tm, k), lambda i: (i, 0))
    vec = pl.BlockSpec((1, k), lambda i: (0, 0))
    in_specs += [row, vec]
    args += [x, g]
    if dres is not None:
        in_specs.append(row)
        args.append(dres)
    (dx, dgain), rode = _call_with_rider(
        body, rider, name=name, grid=(s // tm,), in_specs=in_specs, out_specs=[row, vec],
        out_shape=[jax.ShapeDtypeStruct((s, k), F32), jax.ShapeDtypeStruct((1, k), F32)], scratch_shapes=[],
        compiler_params=_cp("arbitrary"), args=args)
    return (dx, dgain) if rider is None else (dx, dgain, rode)


def _ple_fwd(h, g, wg, p, wp, *, name, tm=512):
    s, d = h.shape
    pd = p.shape[1]

    def body(h_ref, g_ref, wg_ref, p_ref, wp_ref, o_ref, xn_ref, gate_ref, pp_ref):
        hv = h_ref[...]
        _, xhat = _rms_stats(hv)
        xn = (xhat * g_ref[...]).astype(BF16)
        xn_ref[...] = xn
        gate = _sigmoid(jnp.dot(xn, wg_ref[...], preferred_element_type=F32))
        pp = jnp.dot(p_ref[...].astype(BF16), wp_ref[...], preferred_element_type=F32)
        gate_ref[...] = gate.astype(BF16)
        pp_ref[...] = pp.astype(BF16)
        o_ref[...] = hv + gate * pp

    row = pl.BlockSpec((tm, d), lambda i: (i, 0))
    return pl.pallas_call(
        body, name=name, grid=(s // tm,),
        in_specs=[row, pl.BlockSpec((1, d), lambda i: (0, 0)), pl.BlockSpec((d, d), lambda i: (0, 0)),
                  pl.BlockSpec((tm, pd), lambda i: (i, 0)), pl.BlockSpec((pd, d), lambda i: (0, 0))],
        out_specs=[row, row, row, row],
        out_shape=[jax.ShapeDtypeStruct((s, d), F32)] + [jax.ShapeDtypeStruct((s, d), BF16)] * 3,
        compiler_params=_cp("arbitrary"))(h, g, wg, p, wp)


def _ple_bwd_elem(dh, gate, pp, *, name, tm=512):
    s, d = dh.shape

    def body(dh_ref, gate_ref, pp_ref, dz_ref, dpp_ref):
        dhv = dh_ref[...]
        gt = gate_ref[...].astype(F32)
        dz_ref[...] = (dhv * pp_ref[...].astype(F32) * (gt * (1.0 - gt))).astype(BF16)
        dpp_ref[...] = (dhv * gt).astype(BF16)

    row = pl.BlockSpec((tm, d), lambda i: (i, 0))
    return pl.pallas_call(
        body, name=name, grid=(s // tm,), in_specs=[row, row, row], out_specs=[row, row],
        out_shape=[jax.ShapeDtypeStruct((s, d), BF16)] * 2,
        compiler_params=_cp("arbitrary"))(dh, gate, pp)


def _final_loss(h, g, tgt, *, name, tm=512):
    s, d = h.shape

    def body(h_ref, g_ref, t_ref, loss_ref, dh_ref, dg_ref):
        @pl.when(pl.program_id(0) == 0)
        def _():
            loss_ref[...] = jnp.zeros_like(loss_ref)
            dg_ref[...] = jnp.zeros_like(dg_ref)

        hv = h_ref[...]
        gv = g_ref[...]
        _, xhat = _rms_stats(hv)
        err = xhat * gv - t_ref[...]
        per_row = jnp.mean(err * err, axis=-1, keepdims=True)
        loss_ref[...] += 0.5 * jnp.sum(per_row, axis=0, keepdims=True)
        dx, dgrow = _rms_bwd(err * (1.0 / d), hv, gv)
        dh_ref[...] = dx
        dg_ref[...] += jnp.sum(dgrow, axis=0, keepdims=True)

    row = pl.BlockSpec((tm, d), lambda i: (i, 0))
    vec = pl.BlockSpec((1, d), lambda i: (0, 0))
    return pl.pallas_call(
        body, name=name, grid=(s // tm,), in_specs=[row, vec, row],
        out_specs=[pl.BlockSpec((1, LANES), lambda i: (0, 0)), row, vec],
        out_shape=[jax.ShapeDtypeStruct((1, LANES), F32), jax.ShapeDtypeStruct((s, d), F32),
                   jax.ShapeDtypeStruct((1, d), F32)],
        compiler_params=_cp("arbitrary"))(h, g, tgt)


def _rope_fwd(y1, y2, cos, sin, *, name, tm=512):
    s, r = y1.shape

    def body(a_ref, b_ref, c_ref, s_ref, o_ref):
        o_ref[...] = a_ref[...] * c_ref[...] + b_ref[...] * s_ref[...]

    row = pl.BlockSpec((tm, r), lambda i: (i, 0))
    return pl.pallas_call(
        body, name=name, grid=(s // tm,), in_specs=[row] * 4, out_specs=row,
        out_shape=jax.ShapeDtypeStruct((s, r), F32), compiler_params=_cp("arbitrary"))(y1, y2, cos, sin)


def _split3(v):
    h1 = v.astype(BF16)
    r1 = v - h1.astype(F32)
    h2 = r1.astype(BF16)
    h3 = (r1 - h2.astype(F32)).astype(BF16)
    return h1, h2, h3


def _tri(tb, upper):
    r = lax.broadcasted_iota(jnp.int32, (tb, tb), 0)
    c = lax.broadcasted_iota(jnp.int32, (tb, tb), 1)
    return jnp.where((r <= c) if upper else (r >= c), 1.0, 0.0).astype(BF16)


def _fox_gate_fwd(ft, bf, *, out_scale, name, tb=512):
    nh, s = ft.shape

    def body(f_ref, b_ref, o_ref, carry):
        @pl.when(pl.program_id(0) == 0)
        def _():
            carry[...] = jnp.zeros_like(carry)

        z = f_ref[...] + b_ref[...]
        lf = jnp.minimum(z, 0.0) - jnp.log(1.0 + jnp.exp(-jnp.abs(z)))
        tri = _tri(tb, True)
        cs = sum(jnp.dot(t, tri, preferred_element_type=F32) for t in _split3(lf)) + carry[...]
        for n, term in enumerate(_split3(cs * out_scale)):
            o_ref[n] = term
        carry[...] += jnp.sum(lf, axis=-1, keepdims=True)

    return pl.pallas_call(
        body, name=name, grid=(s // tb,),
        in_specs=[pl.BlockSpec((nh, tb), lambda t: (0, t)), pl.BlockSpec((nh, 1), lambda t: (0, 0))],
        out_specs=pl.BlockSpec((3, nh, tb), lambda t: (0, 0, t)),
        out_shape=jax.ShapeDtypeStruct((3, nh, s), BF16),
        scratch_shapes=[pltpu.VMEM((nh, 1), F32)], compiler_params=_cp("arbitrary"))(ft, bf)


def _fox_gate_bwd(drow, dcol, ft, bf, *, inv_scale, name, tb=512):
    nh, s = ft.shape
    nb = s // tb

    def body(dr_ref, dc_ref, f_ref, b_ref, df_ref, db_ref, carry):
        @pl.when(pl.program_id(0) == 0)
        def _():
            carry[...] = jnp.zeros_like(carry)
            db_ref[...] = jnp.zeros_like(db_ref)

        dc = (dr_ref[...] - dc_ref[...]) * inv_scale
        tri = _tri(tb, False)
        suf = sum(jnp.dot(t, tri, preferred_element_type=F32) for t in _split3(dc)) + carry[...]
        z = f_ref[...] + b_ref[...]
        dz = suf * (1.0 / (1.0 + jnp.exp(z)))
        df_ref[...] = dz
        db_ref[...] += jnp.sum(dz, axis=-1, keepdims=True)
        carry[...] += jnp.sum(dc, axis=-1, keepdims=True)

    rev = pl.BlockSpec((nh, tb), lambda t: (0, nb - 1 - t))
    one = pl.BlockSpec((nh, 1), lambda t: (0, 0))
    return pl.pallas_call(
        body, name=name, grid=(nb,), in_specs=[rev, rev, rev, one], out_specs=[rev, one],
        out_shape=[jax.ShapeDtypeStruct((nh, s), F32), jax.ShapeDtypeStruct((nh, 1), F32)],
        scratch_shapes=[pltpu.VMEM((nh, 1), F32)], compiler_params=_cp("arbitrary"))(drow, dcol, ft, bf)


def _tri_fwd(t, nq):
    i = sum((t >= (r * (r + 1)) // 2).astype(jnp.int32) for r in range(1, nq))
    return i, t - (i * (i + 1)) // 2


def _tri_bwd(t, nq):
    j = sum((t >= r * nq - (r * (r - 1)) // 2).astype(jnp.int32) for r in range(1, nq))
    return j, j + t - (j * nq - (j * (j - 1)) // 2)


def _scores_t(k, q, *, scale, diag, col0=0):
    s = lax.dot_general(k, q, NT, preferred_element_type=F32) * scale
    if diag:
        r = lax.broadcasted_iota(jnp.int32, s.shape, 0)
        c = lax.broadcasted_iota(jnp.int32, s.shape, 1)
        s = jnp.where(r <= c + col0, s, MASK_VALUE)
    return s


def _causal_fwd_t(q, k, vt, *, scale, name, tq, hb=2, rider=None):
    nh, s, dq = q.shape
    dv = vt.shape[1]
    nq = s // tq
    nsteps = (nq * (nq + 1)) // 2
    tc = tq // 2

    def body(q_ref, k_ref, vt_ref, o_ref, lse_ref, m_sc, l_sc, acc_sc):
        i, j = _tri_fwd(pl.program_id(1), nq)

        @pl.when(j == 0)
        def _():
            m_sc[...] = jnp.full_like(m_sc, MASK_VALUE)
            l_sc[...] = jnp.zeros_like(l_sc)
            acc_sc[...] = jnp.zeros_like(acc_sc)

        def step(diag):
            for u, c0 in itertools.product(range(hb), range(0, tq, tc)):
                cols = slice(c0, c0 + tc)
                nk = c0 + tc if diag else tq
                sc = _scores_t(k_ref[u, :nk, :], q_ref[u, cols, :], scale=scale, diag=diag, col0=c0)
                m_prev = m_sc[u, :, cols]
                m_new = jnp.maximum(m_prev, jnp.max(sc, axis=0, keepdims=True))
                alpha = jnp.exp(m_prev - m_new)
                pr = jnp.exp(sc - m_new)
                l_new = alpha * l_sc[u, :, cols] + jnp.sum(pr, axis=0, keepdims=True)
                acc = alpha * acc_sc[u, :, cols] + jnp.dot(vt_ref[u, :, :nk], pr.astype(BF16), preferred_element_type=F32)
                if diag:
                    o_ref[u, :, cols] = (acc / l_new).astype(BF16)
                    lse_ref[u, :, cols] = m_new + jnp.log(l_new)
                else:
                    m_sc[u, :, cols], l_sc[u, :, cols], acc_sc[u, :, cols] = m_new, l_new, acc

        pl.when(j < i)(functools.partial(step, False))
        pl.when(j == i)(functools.partial(step, True))

    def qi(t):
        return _tri_fwd(t, nq)[0]

    def kj(t):
        return _tri_fwd(t, nq)[1]

    return _call_with_rider(
        body, rider, name=name, grid=(nh // hb, nsteps),
        in_specs=[pl.BlockSpec((hb, tq, dq), lambda hp, t: (hp, qi(t), 0)),
                  pl.BlockSpec((hb, tq, dq), lambda hp, t: (hp, kj(t), 0)),
                  pl.BlockSpec((hb, dv, tq), lambda hp, t: (hp, 0, kj(t)))],
        out_specs=[pl.BlockSpec((hb, dv, tq), lambda hp, t: (hp, 0, qi(t))),
                   pl.BlockSpec((hb, 1, tq), lambda hp, t: (hp, 0, qi(t)))],
        out_shape=[jax.ShapeDtypeStruct((nh, dv, s), BF16), jax.ShapeDtypeStruct((nh, 1, s), F32)],
        scratch_shapes=[pltpu.VMEM((hb, 1, tq), F32), pltpu.VMEM((hb, 1, tq), F32), pltpu.VMEM((hb, dv, tq), F32)],
        compiler_params=_cp("arbitrary", "arbitrary"), args=(q, k, vt))


def _causal_bwd_t(q, k, v, ot, dot_, lse, *, scale, name, tq, hb=2, rider=None):
    nh, s, dq = q.shape
    dv = v.shape[-1]
    nq = s // tq
    nsteps = (nq * (nq + 1)) // 2

    def body(q_ref, k_ref, v_ref, ot_ref, dot_ref, lse_ref, dq_ref, dk_ref, dvt_ref):
        t = pl.program_id(1)
        j, i = _tri_bwd(t, nq)

        @pl.when(t == 0)
        def _():
            dq_ref[...] = jnp.zeros_like(dq_ref)

        def step(diag):
            rows = pl.ds(pl.multiple_of(i * tq, tq), tq)
            for u in range(hb):
                qv, kv, dov = q_ref[u], k_ref[u], dot_ref[u]
                pr = jnp.exp(_scores_t(kv, qv, scale=scale, diag=diag) - lse_ref[u])
                dp = jnp.dot(v_ref[u], dov, preferred_element_type=F32)
                delta = jnp.sum(dov.astype(F32) * ot_ref[u].astype(F32), axis=0, keepdims=True)
                dsb = ((pr * (dp - delta)) * scale).astype(BF16)
                d_v = lax.dot_general(dov, pr.astype(BF16), NT, preferred_element_type=F32)
                d_k = jnp.dot(dsb, qv, preferred_element_type=F32)
                if diag:
                    dvt_ref[u], dk_ref[u] = d_v, d_k
                else:
                    dvt_ref[u] += d_v
                    dk_ref[u] += d_k
                dq_ref[u, rows, :] += lax.dot_general(dsb, kv, TN, preferred_element_type=F32)

        pl.when(i > j)(functools.partial(step, False))
        pl.when(i == j)(functools.partial(step, True))

    def qi(t):
        return _tri_bwd(t, nq)[1]

    def kj(t):
        return _tri_bwd(t, nq)[0]

    rows_q = pl.BlockSpec((hb, tq, dq), lambda hp, t: (hp, qi(t), 0))
    rows_k = pl.BlockSpec((hb, tq, dq), lambda hp, t: (hp, kj(t), 0))
    lanes_q = pl.BlockSpec((hb, dv, tq), lambda hp, t: (hp, 0, qi(t)))
    return _call_with_rider(
        body, rider, name=name, grid=(nh // hb, nsteps),
        in_specs=[rows_q, rows_k, pl.BlockSpec((hb, tq, dv), lambda hp, t: (hp, kj(t), 0)), lanes_q, lanes_q,
                  pl.BlockSpec((hb, 1, tq), lambda hp, t: (hp, 0, qi(t)))],
        out_specs=[pl.BlockSpec((hb, s, dq), lambda hp, t: (hp, 0, 0)), rows_k,
                   pl.BlockSpec((hb, dv, tq), lambda hp, t: (hp, 0, kj(t)))],
        out_shape=[jax.ShapeDtypeStruct((nh, s, dq), F32), jax.ShapeDtypeStruct((nh, s, dq), F32),
                   jax.ShapeDtypeStruct((nh, dv, s), F32)],
        scratch_shapes=[], compiler_params=_cp("arbitrary", "arbitrary"), args=(q, k, v, ot, dot_, lse))


def _swa_scores_t(k, q, dist, ok, *, scale, slope):
    s = lax.dot_general(k, q, NT, preferred_element_type=F32) * scale - slope * dist.astype(F32)
    return jnp.where(ok, s, MASK_VALUE)


def _swa_geometry(tb, w, has_other):
    r = lax.broadcasted_iota(jnp.int32, (tb, tb), 0)
    c = lax.broadcasted_iota(jnp.int32, (tb, tb), 1)
    d_same = c - r
    ok_same = jnp.logical_and(d_same >= 0, d_same < w)

    def other(ncols):
        rr = lax.broadcasted_iota(jnp.int32, (w, ncols), 0)
        cc = lax.broadcasted_iota(jnp.int32, (w, ncols), 1)
        dd = cc + w - rr
        return dd, jnp.logical_and(dd < w, has_other)

    return (d_same, ok_same), other


def _swa_fwd_t(q, k, vt, slopes_sinks, *, scale, window, name, tb=256):
    nh, s, d = q.shape
    nkv = k.shape[0]
    grp = nh // nkv
    w = window
    per = tb // w
    assert tb % w == 0

    def body(q_ref, kc_ref, kp_ref, vc_ref, vp_ref, ss_ref, o_ref, lse_ref):
        kvh, i = pl.program_id(0), pl.program_id(1)
        (d_c, ok_c), other = _swa_geometry(tb, w, i > 0)
        d_p, ok_p = other(tb)
        for g in range(grp):
            h = kvh * grp + g
            slope, sink = ss_ref[0, h], ss_ref[1, h]
            qg = q_ref[g]
            s_c = _swa_scores_t(kc_ref[...], qg, d_c, ok_c, scale=scale, slope=slope)
            s_p = _swa_scores_t(kp_ref[...], qg, d_p, ok_p, scale=scale, slope=slope)
            m = jnp.maximum(jnp.maximum(jnp.max(s_c, axis=0, keepdims=True), jnp.max(s_p, axis=0, keepdims=True)), sink)
            p_c, p_p = jnp.exp(s_c - m), jnp.exp(s_p - m)
            l = jnp.sum(p_c, axis=0, keepdims=True) + jnp.sum(p_p, axis=0, keepdims=True) + jnp.exp(sink - m)
            acc = (jnp.dot(vc_ref[...], p_c.astype(BF16), preferred_element_type=F32)
                   + jnp.dot(vp_ref[...], p_p.astype(BF16), preferred_element_type=F32))
            o_ref[g] = (acc / l).astype(BF16)
            lse_ref[g] = m + jnp.log(l)

    def prev(i):
        return jnp.maximum(i * per - 1, 0)

    return pl.pallas_call(
        body, name=name, grid=(nkv, s // tb),
        in_specs=[pl.BlockSpec((grp, tb, d), lambda kh, i: (kh, i, 0)),
                  pl.BlockSpec((None, tb, d), lambda kh, i: (kh, i, 0)),
                  pl.BlockSpec((None, w, d), lambda kh, i: (kh, prev(i), 0)),
                  pl.BlockSpec((None, d, tb), lambda kh, i: (kh, 0, i)),
                  pl.BlockSpec((None, d, w), lambda kh, i: (kh, 0, prev(i))),
                  pl.BlockSpec(memory_space=pltpu.SMEM)],
        out_specs=[pl.BlockSpec((grp, d, tb), lambda kh, i: (kh, 0, i)), pl.BlockSpec((grp, 1, tb), lambda kh, i: (kh, 0, i))],
        out_shape=[jax.ShapeDtypeStruct((nh, d, s), BF16), jax.ShapeDtypeStruct((nh, 1, s), F32)],
        compiler_params=_cp("arbitrary", "arbitrary"))(q, k, k, vt, vt, slopes_sinks)


def _swa_bwd_t(q, k, v, ot, dot_, lse, slopes_sinks, *, scale, window, name, tb=256, rider=None):
    nh, s, d = q.shape
    nkv = k.shape[0]
    grp = nh // nkv
    w = window
    per = tb // w
    nb = s // tb

    def body(qc_ref, qn_ref, kc_ref, kp_ref, vc_ref, vp_ref, oc_ref, on_ref, doc_ref, don_ref, lc_ref, ln_ref, ss_ref,
             dq_ref, dk_ref, dvt_ref, dsink_ref):
        kvh, i = pl.program_id(0), pl.program_id(1)

        @pl.when(i == 0)
        def _():
            dsink_ref[...] = jnp.zeros_like(dsink_ref)

        (d_c, ok_c), other = _swa_geometry(tb, w, i > 0)
        d_p, ok_p = other(tb)
        d_n, ok_n = _swa_geometry(tb, w, i < nb - 1)[1](w)
        kc, kp, vc, vp = kc_ref[...], kp_ref[...], vc_ref[...], vp_ref[...]
        k_last, v_last = kc[tb - w:, :], vc[tb - w:, :]
        dk_acc = jnp.zeros((tb, d), F32)
        dv_acc = jnp.zeros((d, tb), F32)
        dk_tail = jnp.zeros((w, d), F32)
        dv_tail = jnp.zeros((d, w), F32)
        for g in range(grp):
            h = kvh * grp + g
            slope, sink = ss_ref[0, h], ss_ref[1, h]
            qg, dog, lse_c = qc_ref[g], doc_ref[g], lc_ref[g]
            delta = jnp.sum(dog.astype(F32) * oc_ref[g].astype(F32), axis=0, keepdims=True)
            p_c = jnp.exp(_swa_scores_t(kc, qg, d_c, ok_c, scale=scale, slope=slope) - lse_c)
            p_p = jnp.exp(_swa_scores_t(kp, qg, d_p, ok_p, scale=scale, slope=slope) - lse_c)
            ds_c = ((p_c * (jnp.dot(vc, dog, preferred_element_type=F32) - delta)) * scale).astype(BF16)
            ds_p = ((p_p * (jnp.dot(vp, dog, preferred_element_type=F32) - delta)) * scale).astype(BF16)
            dq_ref[g] = (lax.dot_general(ds_c, kc, TN, preferred_element_type=F32)
                         + lax.dot_general(ds_p, kp, TN, preferred_element_type=F32))
            dk_acc += jnp.dot(ds_c, qg, preferred_element_type=F32)
            dv_acc += lax.dot_general(dog, p_c.astype(BF16), NT, preferred_element_type=F32)
            dsink_ref[g] -= jnp.broadcast_to(jnp.sum(jnp.exp(sink - lse_c) * delta, axis=1, keepdims=True), (1, LANES))
            qn, don = qn_ref[g], don_ref[g]
            delta_n = jnp.sum(don.astype(F32) * on_ref[g].astype(F32), axis=0, keepdims=True)
            p_n = jnp.exp(_swa_scores_t(k_last, qn, d_n, ok_n, scale=scale, slope=slope) - ln_ref[g])
            ds_n = ((p_n * (jnp.dot(v_last, don, preferred_element_type=F32) - delta_n)) * scale).astype(BF16)
            dk_tail += jnp.dot(ds_n, qn, preferred_element_type=F32)
            dv_tail += lax.dot_general(don, p_n.astype(BF16), NT, preferred_element_type=F32)
        dk_ref[...] = dk_acc
        dvt_ref[...] = dv_acc
        dk_ref[tb - w:, :] += dk_tail
        dvt_ref[:, tb - w:] += dv_tail

    def prev(i):
        return jnp.maximum(i * per - 1, 0)

    def nxt(i):
        return jnp.minimum((i + 1) * per, s // w - 1)

    return _call_with_rider(
        body, rider, name=name, grid=(nkv, nb), scratch_shapes=[],
        args=(q, q, k, k, v, v, ot, ot, dot_, dot_, lse, lse, slopes_sinks),
        in_specs=[pl.BlockSpec((grp, tb, d), lambda kh, i: (kh, i, 0)),
                  pl.BlockSpec((grp, w, d), lambda kh, i: (kh, nxt(i), 0)),
                  pl.BlockSpec((None, tb, d), lambda kh, i: (kh, i, 0)),
                  pl.BlockSpec((None, w, d), lambda kh, i: (kh, prev(i), 0)),
                  pl.BlockSpec((None, tb, d), lambda kh, i: (kh, i, 0)),
                  pl.BlockSpec((None, w, d), lambda kh, i: (kh, prev(i), 0)),
                  pl.BlockSpec((grp, d, tb), lambda kh, i: (kh, 0, i)),
                  pl.BlockSpec((grp, d, w), lambda kh, i: (kh, 0, nxt(i))),
                  pl.BlockSpec((grp, d, tb), lambda kh, i: (kh, 0, i)),
                  pl.BlockSpec((grp, d, w), lambda kh, i: (kh, 0, nxt(i))),
                  pl.BlockSpec((grp, 1, tb), lambda kh, i: (kh, 0, i)),
                  pl.BlockSpec((grp, 1, w), lambda kh, i: (kh, 0, nxt(i))),
                  pl.BlockSpec(memory_space=pltpu.SMEM)],
        out_specs=[pl.BlockSpec((grp, tb, d), lambda kh, i: (kh, i, 0)),
                   pl.BlockSpec((None, tb, d), lambda kh, i: (kh, i, 0)),
                   pl.BlockSpec((None, d, tb), lambda kh, i: (kh, 0, i)),
                   pl.BlockSpec((None, grp, 1, LANES), lambda kh, i: (kh, 0, 0, 0))],
        out_shape=[jax.ShapeDtypeStruct((nh, s, d), F32), jax.ShapeDtypeStruct((nkv, s, d), F32),
                   jax.ShapeDtypeStruct((nkv, d, s), F32), jax.ShapeDtypeStruct((nkv, grp, 1, LANES), F32)],
        compiler_params=_cp("arbitrary", "arbitrary"))


def _adamw(w, g, m, v, *, name):
    shape = w.shape
    cols = shape[-1]
    rows = int(np.prod(shape[:-1])) if len(shape) > 1 else 1
    tr = _row_tile(rows, cols)
    c1 = 1.0 - ADAM_B1 ** ADAM_STEP
    c2 = 1.0 - ADAM_B2 ** ADAM_STEP

    def body(w_ref, g_ref, m_ref, v_ref, d_ref, mo_ref, vo_ref):
        gv = g_ref[...]
        mn = ADAM_B1 * m_ref[...] + (1.0 - ADAM_B1) * gv
        vn = ADAM_B2 * v_ref[...] + (1.0 - ADAM_B2) * (gv * gv)
        mo_ref[...] = mn
        vo_ref[...] = vn
        d_ref[...] = -ADAM_LR * ((mn / c1) / (jnp.sqrt(vn / c2) + ADAM_EPS) + ADAM_WD * w_ref[...])

    blk = pl.BlockSpec((tr, cols), lambda i: (i, 0))
    outs = pl.pallas_call(
        body, name=name, grid=(rows // tr,), in_specs=[blk] * 4, out_specs=[blk] * 3,
        out_shape=[jax.ShapeDtypeStruct((rows, cols), F32)] * 3,
        compiler_params=_cp("arbitrary"))(*[a.reshape(rows, cols) for a in (w, g, m, v)])
    return tuple(a.reshape(shape) for a in outs)


def _hbm_spec():
    return pl.BlockSpec(memory_space=pl.ANY)


def _mesh_place():
    x, y, c = lax.axis_index("x"), lax.axis_index("y"), lax.axis_index("c")
    return x, y, c, [(1 - x, y), (x, 1 - y), (1 - x, 1 - y)]


def _half_rows(c, rows, align):
    return pl.ds(pl.multiple_of(c * (rows // 2), align), rows // 2)


def _part(ref, mode, k, n, rows=None):
    if mode == "cols":
        cols = pl.ds(pl.multiple_of(k * n, LANES), n)
        return ref.at[:, cols] if rows is None else ref.at[rows, cols]
    return ref.at[k] if rows is None else ref.at[k, rows, :]


class _Rider:
    def __init__(self, inputs, out_shape, n_sems, start, finish):
        self.inputs, self.out_shape, self.n_sems, self.start, self.finish = inputs, out_shape, n_sems, start, finish


def _call_with_rider(body, rider, *, name, grid, in_specs, out_specs, out_shape, scratch_shapes, compiler_params, args):
    if rider is None:
        outs = pl.pallas_call(body, name=name, grid=grid, in_specs=in_specs, out_specs=out_specs, out_shape=out_shape,
                              scratch_shapes=scratch_shapes, compiler_params=compiler_params)(*args)
        return outs, []
    n_in, n_out, n_sc = len(in_specs), len(out_specs), len(scratch_shapes)
    n_rin, n_rout = len(rider.inputs), len(rider.out_shape)

    def wrapped(*refs):
        pos = 0
        groups = []
        for n in (n_in, n_rin, n_out, n_rout, n_sc, 2):
            groups.append(refs[pos:pos + n])
            pos += n
        ins, rins, outs, routs, scratch, sems = groups
        ids = [pl.program_id(a) for a in range(len(grid))]
        first = functools.reduce(jnp.logical_and, [i == 0 for i in ids])
        last = functools.reduce(jnp.logical_and, [i == g - 1 for i, g in zip(ids, grid)])
        pl.when(first)(lambda: rider.start(rins, routs, *sems))
        body(*ins, *outs, *scratch)
        pl.when(last)(lambda: rider.finish(rins, routs, *sems))

    outs = pl.pallas_call(
        wrapped, name=name, grid=grid, in_specs=list(in_specs) + [_hbm_spec()] * n_rin,
        out_specs=list(out_specs) + [_hbm_spec()] * n_rout, out_shape=list(out_shape) + list(rider.out_shape),
        scratch_shapes=list(scratch_shapes) + [pltpu.SemaphoreType.DMA((rider.n_sems,))] * 2,
        compiler_params=compiler_params)(*args, *rider.inputs)
    return outs[:n_out], outs[n_out:]


def _run_rider(rider, *, name):
    n_rin = len(rider.inputs)

    def body(*refs):
        rins, routs, sems = refs[:n_rin], refs[n_rin:-2], refs[-2:]
        rider.start(rins, routs, *sems)
        rider.finish(rins, routs, *sems)

    return pl.pallas_call(
        body, name=name, in_specs=[_hbm_spec()] * n_rin, out_specs=[_hbm_spec()] * len(rider.out_shape),
        out_shape=rider.out_shape, scratch_shapes=[pltpu.SemaphoreType.DMA((rider.n_sems,))] * 2)(*rider.inputs)


def _gather_rider(shards, modes):
    n_arr = len(shards)
    out_shape = [jax.ShapeDtypeStruct((s.shape[0], N_CHIPS * s.shape[1]) if m == "cols" else (N_CHIPS,) + s.shape, s.dtype)
                 for s, m in zip(shards, modes)]
    per = 4

    def copies(srcs, dsts, send_sems, recv_sems):
        x, y, c, chips = _mesh_place()
        me = 2 * x + y
        sends, waits = [], []
        for i in range(n_arr):
            r, n = shards[i].shape
            rows = _half_rows(c, r, 16)

            def copy(slot, src, dst, to, i=i):
                return pltpu.make_async_remote_copy(src_ref=src, dst_ref=dst, send_sem=send_sems.at[i * per + slot],
                                                    recv_sem=recv_sems.at[i * per + slot], device_id=to, device_id_type=MESH)

            own = _part(dsts[i], modes[i], me, n)
            sends.append(copy(0, srcs[i], own, (x, y, 1 - c)))
            waits.append(copy(0, own, own, (x, y, 1 - c)))
            for j, (px, py) in enumerate(chips):
                sends.append(copy(1 + j, srcs[i].at[rows], _part(dsts[i], modes[i], me, n, rows), (px, py, c)))
                theirs = _part(dsts[i], modes[i], 2 * px + py, n, rows)
                waits.append(copy(1 + j, theirs, theirs, (px, py, c)))
        return sends, waits

    def start(*refs):
        for cp in copies(*refs)[0]:
            cp.start()

    def finish(*refs):
        sends, waits = copies(*refs)
        for cp in waits:
            cp.wait_recv()
        for cp in sends:
            cp.wait_send()

    return _Rider(list(shards), out_shape, per * n_arr, start, finish)


def _gather_forward(dsts, shard_shapes, modes, *, name):
    n_arr = len(dsts)

    def body(*refs):
        outs = refs[n_arr:2 * n_arr]
        send_sems, recv_sems = refs[2 * n_arr:]
        x, y, c, chips = _mesh_place()
        cps = []
        for i in range(n_arr):
            r, n = shard_shapes[i]
            for j, (px, py) in enumerate(chips):
                def view(hc, i=i, px=px, py=py, r=r, n=n):
                    return _part(outs[i], modes[i], 2 * px + py, n, _half_rows(hc, r, 16))

                def copy(ref, i=i, j=j):
                    return pltpu.make_async_remote_copy(src_ref=ref, dst_ref=ref, send_sem=send_sems.at[3 * i + j],
                                                        recv_sem=recv_sems.at[3 * i + j], device_id=(x, y, 1 - c), device_id_type=MESH)

                cps.append((copy(view(c)), copy(view(1 - c))))
        for send, _ in cps:
            send.start()
        for send, theirs in cps:
            theirs.wait_recv()
            send.wait_send()

    return pl.pallas_call(
        body, name=name, in_specs=[_hbm_spec()] * n_arr, out_specs=[_hbm_spec()] * n_arr,
        out_shape=[jax.ShapeDtypeStruct(d.shape, d.dtype) for d in dsts],
        input_output_aliases={i: i for i in range(n_arr)},
        scratch_shapes=[pltpu.SemaphoreType.DMA((3 * n_arr,)), pltpu.SemaphoreType.DMA((3 * n_arr,))])(*dsts)


def _blk_view(a, mode):
    return a[None] if mode == "cols" else a


def _swap_rider(arrs, modes):
    n_arr = len(arrs)
    out_shape = [jax.ShapeDtypeStruct((a.shape[0] // 2, a.shape[1]) if m == "cols" else (a.shape[0], a.shape[1] // 2, a.shape[2]), a.dtype)
                 for a, m in zip(arrs, modes)]

    def copies(srcs, dsts, send_sems, recv_sems):
        x, y, c, _ = _mesh_place()
        cps = []
        for i in range(n_arr):
            if modes[i] == "cols":
                src = srcs[i].at[_half_rows(1 - c, arrs[i].shape[0], 8)]
            else:
                src = srcs[i].at[:, _half_rows(1 - c, arrs[i].shape[1], 8), :]
            cps.append(pltpu.make_async_remote_copy(src_ref=src, dst_ref=dsts[i], send_sem=send_sems.at[i],
                                                    recv_sem=recv_sems.at[i], device_id=(x, y, 1 - c), device_id_type=MESH))
        return cps

    def start(*refs):
        for cp in copies(*refs):
            cp.start()

    def finish(*refs):
        for cp in copies(*refs):
            cp.wait()

    return _Rider(list(arrs), out_shape, n_arr, start, finish)


def _rs_pair_add(arr, landed, place, *, name):
    nb, r, c = arr.shape
    rh = r // 2
    tr = _row_tile(rh, c)
    nt = rh // tr

    def body(p_ref, a_ref, l_ref, o_ref):
        o_ref[...] = (a_ref[...] + l_ref[...]).astype(BF16)

    grid_spec = pltpu.PrefetchScalarGridSpec(
        num_scalar_prefetch=1, grid=(nb, nt),
        in_specs=[pl.BlockSpec((None, tr, c), lambda b, t, p_ref: (b, p_ref[1] * nt + t, 0)),
                  pl.BlockSpec((None, tr, c), lambda b, t, p_ref: (b, t, 0))],
        out_specs=pl.BlockSpec((None, tr, c), lambda b, t, p_ref: (b, t, 0)))
    return pl.pallas_call(
        body, name=name, grid_spec=grid_spec, out_shape=jax.ShapeDtypeStruct((nb, rh, c), BF16),
        compiler_params=_cp("arbitrary", "arbitrary"))(place, arr, landed)


def _exchange_rider(parts, modes):
    n_arr = len(parts)
    out_shape = []
    for a, m in zip(parts, modes):
        shp = (a.shape[0], a.shape[1] // N_CHIPS) if m == "cols" else a.shape[1:]
        out_shape.append(jax.ShapeDtypeStruct((3,) + shp, a.dtype))

    def copies(srcs, dsts, send_sems, recv_sems):
        x, y, c, chips = _mesh_place()
        cps = []
        for i in range(n_arr):
            n = out_shape[i].shape[-1]
            for j, (px, py) in enumerate(chips):
                cps.append(pltpu.make_async_remote_copy(
                    src_ref=_part(srcs[i], modes[i], 2 * px + py, n), dst_ref=dsts[i].at[j],
                    send_sem=send_sems.at[3 * i + j], recv_sem=recv_sems.at[3 * i + j],
                    device_id=(px, py, c), device_id_type=MESH))
        return cps

    def start(*refs):
        for cp in copies(*refs):
            cp.start()

    def finish(*refs):
        for cp in copies(*refs):
            cp.wait()

    return _Rider(list(parts), out_shape, 3 * n_arr, start, finish)


def _rs_chip_sum(part, landed, mode, place, *, name):
    _, rh, n = landed.shape
    tr = _row_tile(rh, n)
    nt = rh // tr

    def body(p_ref, a_ref, l_ref, o_ref):
        o_ref[...] = ((a_ref[...].astype(F32) + l_ref[0].astype(F32)) + l_ref[1].astype(F32)) + l_ref[2].astype(F32)

    if mode == "cols":
        own = pl.BlockSpec((tr, n), lambda t, p_ref: (t, p_ref[0]))
    else:
        own = pl.BlockSpec((None, tr, n), lambda t, p_ref: (p_ref[0], t, 0))
    grid_spec = pltpu.PrefetchScalarGridSpec(
        num_scalar_prefetch=1, grid=(nt,),
        in_specs=[own, pl.BlockSpec((3, tr, n), lambda t, p_ref: (0, t, 0))],
        out_specs=pl.BlockSpec((tr, n), lambda t, p_ref: (p_ref[1] * nt + t, 0)))
    return pl.pallas_call(
        body, name=name, grid_spec=grid_spec, out_shape=jax.ShapeDtypeStruct((2 * rh, n), F32),
        compiler_params=_cp("arbitrary"))(place, part, landed)


def _rs_pair_join(halves, *, name):
    n_arr = len(halves)

    def body(*refs):
        outs = refs[n_arr:2 * n_arr]
        send_sems, recv_sems = refs[2 * n_arr:]
        x, y, c, _ = _mesh_place()
        cps = []
        for i in range(n_arr):
            rows = _half_rows(c, halves[i].shape[0], 8)
            cps.append(pltpu.make_async_remote_copy(src_ref=outs[i].at[rows], dst_ref=outs[i].at[rows], send_sem=send_sems.at[i],
                                                    recv_sem=recv_sems.at[i], device_id=(x, y, 1 - c), device_id_type=MESH))
        for cp in cps:
            cp.start()
        for i, cp in enumerate(cps):
            cp.wait_send()
            theirs = outs[i].at[_half_rows(1 - c, halves[i].shape[0], 8)]
            pltpu.make_async_remote_copy(src_ref=theirs, dst_ref=theirs, send_sem=send_sems.at[i], recv_sem=recv_sems.at[i],
                                         device_id=(x, y, 1 - c), device_id_type=MESH).wait_recv()

    return pl.pallas_call(
        body, name=name, in_specs=[_hbm_spec()] * n_arr, out_specs=[_hbm_spec()] * n_arr,
        out_shape=[jax.ShapeDtypeStruct(h.shape, h.dtype) for h in halves],
        input_output_aliases={i: i for i in range(n_arr)},
        scratch_shapes=[pltpu.SemaphoreType.DMA((n_arr,)), pltpu.SemaphoreType.DMA((n_arr,))])(*halves)


def _allreduce_small(v, *, name):
    r, c = v.shape

    def body(v_ref, o_ref, gath, send_sems, recv_sems):
        x, y, cc, _ = _mesh_place()
        me = 4 * x + 2 * y + cc
        gath[me] = v_ref[...]
        cps = []
        for rel in range(1, 8):
            px = 1 - x if rel & 4 else x
            py = 1 - y if rel & 2 else y
            pc = 1 - cc if rel & 1 else cc

            def copy(slot, px=px, py=py, pc=pc, rel=rel):
                return pltpu.make_async_remote_copy(
                    src_ref=v_ref, dst_ref=gath.at[slot], send_sem=send_sems.at[rel - 1],
                    recv_sem=recv_sems.at[rel - 1], device_id=(px, py, pc), device_id_type=MESH)

            cps.append((copy(me), copy(4 * px + 2 * py + pc)))
        for send, _ in cps:
            send.start()
        for send, theirs in cps:
            theirs.wait_recv()
            send.wait_send()
        tot = gath[0]
        for d in range(1, 8):
            tot = tot + gath[d]
        o_ref[...] = tot

    vm = pl.BlockSpec(memory_space=pltpu.VMEM)
    return pl.pallas_call(
        body, name=name, in_specs=[vm], out_specs=vm, out_shape=jax.ShapeDtypeStruct((r, c), F32),
        scratch_shapes=[pltpu.VMEM((8, r, c), F32), pltpu.SemaphoreType.DMA((7,)), pltpu.SemaphoreType.DMA((7,))])(v)


def _rope_tables(s, reps):
    half = B_ROPE // 2
    inv = ROPE_THETA ** (-jnp.arange(0, B_ROPE, 2, dtype=F32) / B_ROPE)
    ang = jnp.arange(s, dtype=F32)[:, None] * inv[None, :]
    return jnp.tile(jnp.cos(ang), (1, reps)), jnp.tile(jnp.sin(ang), (1, reps))


def _alibi_slopes():
    return 2.0 ** (-8.0 * jnp.arange(1, A_HEADS + 1, dtype=F32) / A_HEADS)


def _ffn_fwd(h, norm, wts, tag, rider=None, on_rode=None):
    (dact_dgate, dact_dup, act, xn), rode = _ffn_up(h, norm, wts["wgu"], name=f"{tag}_up", rider=rider)
    if on_rode is not None:
        on_rode(rode)
    out = _mm_res_fwd(act, wts["wd"], h, scale=FFN_RES_SCALE, name=f"{tag}_down")
    return out, dict(h_in=h, dact_dgate=dact_dgate, dact_dup=dact_dup, act=act, xn=xn), rode


def _ffn_bwd(dh, norm, wts, sv, tag, rider=None, own=None):
    (dgate, dup), rode = _ffn_down_bwd(dh, wts["wd"], sv["dact_dgate"], sv["dact_dup"], scale=FFN_RES_SCALE,
                                      name=f"{tag}_down_bwd", rider=rider)
    d_wd = _mm_tn(sv["act"], dh, b_scale=FFN_RES_SCALE, name=f"{tag}_dwd")
    pairs = [(dgate, wts["wgu"], 0), (dup, wts["wgu"], 1)]
    if own is None:
        d_wgu = _mm_tn(sv["xn"], [dgate, dup], name=f"{tag}_dwgu")
        dh_in, dnorm = _mm_nt_rmsbwd(pairs, sv["h_in"], norm, dh, name=f"{tag}_dx")
    else:
        wd_ready, wgu_ready, done = own
        first = wd_ready(d_wd)
        res = _mm_tn(sv["xn"], [dgate, dup], name=f"{tag}_dwgu", rider=first)
        d_wgu, brought = (res, []) if first is None else res
        second = wgu_ready(brought, d_wgu)
        res = _mm_nt_rmsbwd(pairs, sv["h_in"], norm, dh, name=f"{tag}_dx", rider=second)
        dh_in, dnorm, brought = (*res, []) if second is None else res
        done(brought)
    return dh_in, dnorm, d_wgu, d_wd, rode


def _even_weights(w_in, w_uq, w_ukv):
    half = B_ROPE // 2
    base = w_in.shape[1]
    kr1, kr2 = w_in[:, base - B_ROPE:base - half], w_in[:, base - half:]
    w_in_cat = jnp.concatenate([w_in, -kr2, kr1, jnp.zeros((w_in.shape[0], 64), w_in.dtype)], axis=1)
    u3 = w_uq.reshape(w_uq.shape[0], B_HEADS, B_NOPE + B_ROPE)
    nope = u3[:, :, :B_NOPE].reshape(w_uq.shape[0], -1)
    rot = u3[:, :, B_NOPE:].reshape(w_uq.shape[0], -1)
    swapped = jnp.concatenate([-u3[:, :, B_NOPE + half:], u3[:, :, B_NOPE:B_NOPE + half]], axis=-1).reshape(w_uq.shape[0], -1)
    return w_in_cat, jnp.concatenate([nope, rot, swapped], axis=1), w_ukv


def _even_fwd(h, w, i, rider=None):
    s = h.shape[0]
    qa, ka, va, vat, c_q, c_kv, kr_blk, xn = _ev_in_fwd(h, w["mix_norm"][i:i + 1], w["ev_in_cat"], name="ev_in")
    cos32, sin32 = _rope_tables(s, 2)
    kro = _rope_fwd(kr_blk[:, :B_ROPE], kr_blk[:, B_ROPE:2 * B_ROPE], cos32, sin32, name="ev_k_rope")
    ss = jnp.stack([_alibi_slopes(), w["ev_sinks"].reshape(-1)])
    oa, lse_a = _swa_fwd_t(qa, ka, vat, ss, scale=A_HEAD_DIM ** -0.5, window=WINDOW, name="swa_fwd")
    cos256, sin256 = _rope_tables(s, 2 * B_HEADS)
    qb, xn_q = _ev_q_fwd(c_q, w["ev_cq_norm"], w["ev_q_cat"], cos256, sin256, name="ev_q_up")
    kb, vb, vbt, xn_kv = _ev_kv_fwd(c_kv, w["ev_ckv_norm"], w["ev_ukv"], kro, name="ev_kv_up")
    (ob, lse_b), rode = _causal_fwd_t(qb, kb, vbt, scale=(B_NOPE + B_ROPE) ** -0.5, name="mla_fwd", tq=512, hb=8, rider=rider)
    attn = jnp.concatenate([oa.reshape(-1, s), ob.reshape(-1, s)], axis=0)
    out = _mm_res_fwd(attn, w["ev_out"], h, scale=1.0, name="ev_out", a_t=True)
    sv = dict(h_in=h, xn=xn, c_q=c_q, c_kv=c_kv, xn_q=xn_q, xn_kv=xn_kv, qa=qa, ka=ka, va=va, oa=oa, lse_a=lse_a,
              ss=ss, qb=qb, kb=kb, vb=vb, ob=ob, lse_b=lse_b, attn=attn, cos32=cos32, sin32=sin32,
              cos256=cos256, sin256=sin256)
    return out, sv, rode


def _even_bwd(dh, w, sv, i, rider=None):
    s = dh.shape[0]
    half = B_ROPE // 2
    g = {}
    dattn = _mm_nt_t(dh, w["ev_out"], name="ev_out_dx")
    g["ev_w_out"] = _mm_tn(sv["attn"], dh, name="ev_out_dw", a_t=True)
    doa = dattn[:A_HEADS * A_HEAD_DIM].reshape(A_HEADS, A_HEAD_DIM, s)
    dob = dattn[A_HEADS * A_HEAD_DIM:].reshape(B_HEADS, B_V, s)
    first, then = rider if isinstance(rider, tuple) else (None, None)
    (dqa, dka, dva, dsink), brought = _swa_bwd_t(sv["qa"], sv["ka"], sv["va"], sv["oa"], doa, sv["lse_a"], sv["ss"],
                                                 scale=A_HEAD_DIM ** -0.5, window=WINDOW, name="swa_bwd", rider=first)
    if then is not None:
        rider = then(brought)
    g["ev_sinks"] = dsink[:, :, 0, 0].reshape(1, A_HEADS)
    (dqb, dkb, dvb), rode = _causal_bwd_t(sv["qb"], sv["kb"], sv["vb"], sv["ob"], dob, sv["lse_b"],
                                          scale=(B_NOPE + B_ROPE) ** -0.5, name="mla_bwd", tq=512, hb=4, rider=rider)
    dyq = _ev_q_merge(dqb, sv["cos256"], sv["sin256"], name="ev_q_merge")
    dwq = _mm_tn(sv["xn_q"], dyq, name="ev_q_up_dw")
    dcq, g["ev_cq_norm"] = _mm_nt_rmsbwd([(dyq, w["ev_q_cat"])], sv["c_q"], w["ev_cq_norm"], None, name="ev_q_up_dx")
    kq = sv["c_q"].shape[1]
    d_nope = dwq[:, :512].reshape(kq, B_HEADS, B_NOPE)
    d_rot = dwq[:, 512:768].reshape(kq, B_HEADS, B_ROPE)
    d_swp = dwq[:, 768:].reshape(kq, B_HEADS, B_ROPE)
    g["ev_w_uq"] = jnp.concatenate([d_nope, d_rot[:, :, :half] + d_swp[:, :, half:], d_rot[:, :, half:] - d_swp[:, :, :half]],
                                   axis=-1).reshape(kq, -1)
    dykv, dkr = _ev_kv_merge(dkb, dvb, sv["cos32"], sv["sin32"], name="ev_kv_merge")
    g["ev_w_ukv"] = _mm_tn(sv["xn_kv"], dykv, name="ev_kv_up_dw")
    dckv, g["ev_ckv_norm"] = _mm_nt_rmsbwd([(dykv, w["ev_ukv"])], sv["c_kv"], w["ev_ckv_norm"], None, name="ev_kv_up_dx")
    dycat = _ev_in_merge(dqa, dka, dva, dcq, dckv, dkr, name="ev_in_merge")
    dwin = _mm_tn(sv["xn"], dycat, name="ev_in_dw")
    base = 1184
    g["ev_w_in"] = jnp.concatenate([dwin[:, :base - B_ROPE],
                                    dwin[:, base - B_ROPE:base - half] + dwin[:, base + half:base + B_ROPE],
                                    dwin[:, base - half:base] - dwin[:, base:base + half]], axis=-1)
    dh_in, dnorm = _mm_nt_rmsbwd([(dycat, w["ev_in_cat"])], sv["h_in"], w["mix_norm"][i:i + 1], dh, name="ev_in_dx")
    return dh_in, dnorm, g, rode


def _odd_fwd(h, w, i, rider=None):
    s = h.shape[0]
    wd = C_HEADS * C_HEAD_DIM
    q, k, v, vt, y_f, xn = _fox_in_fwd(h, w["mix_norm"][i:i + 1], w["od_in_pad"], nheads=C_HEADS, dh=C_HEAD_DIM,
                                       q_ones=(0, 2, 3, 4), k_ones=(1,), name="od_in")
    scale = C_HEAD_DIM ** -0.5
    ft = y_f[:, :C_HEADS].T
    bf = w["od_b_f"].reshape(C_HEADS, 1)
    cb3 = _fox_gate_fwd(ft, bf, out_scale=-1.0 / scale, name="fox_gate_fwd")
    k = k + jnp.pad(cb3.transpose(1, 2, 0), ((0, 0), (0, 0), (C_HEAD_DIM + 2, LANES - C_HEAD_DIM - 5)))
    (o, lse), rode = _causal_fwd_t(q, k, vt, scale=scale, name="fox_fwd", tq=512, hb=16, rider=rider)
    attn = o.reshape(-1, s)
    out = _mm_res_fwd(attn, w["od_out"], h, scale=1.0, name="od_out", a_t=True)
    return out, dict(h_in=h, xn=xn, q=q, k=k, v=v, o=o, lse=lse, ft=ft, bf=bf, attn=attn), rode


def _odd_bwd(dh, w, sv, i, rider=None):
    s = dh.shape[0]
    g = {}
    dattn = _mm_nt_t(dh, w["od_out"], name="od_out_dx")
    g["od_w_out"] = _mm_tn(sv["attn"], dh, name="od_out_dw", a_t=True)
    do = dattn.reshape(C_HEADS, C_HEAD_DIM, s)
    scale = C_HEAD_DIM ** -0.5
    (dq, dk, dv), rode = _causal_bwd_t(sv["q"], sv["k"], sv["v"], sv["o"], do, sv["lse"], scale=scale, name="fox_bwd",
                                       tq=512, hb=4, rider=rider)
    dqkv, sums = _merge_heads(dq, dk, dv, dh=C_HEAD_DIM, q_col=C_HEAD_DIM + 1, k_col=C_HEAD_DIM, name="fox_merge")
    dft, dbf = _fox_gate_bwd(sums[:, :C_HEADS].T, sums[:, C_HEADS:2 * C_HEADS].T, sv["ft"], sv["bf"],
                             inv_scale=1.0 / scale, name="fox_gate_bwd")
    g["od_b_f"] = dbf.reshape(1, C_HEADS)
    wd = C_HEADS * C_HEAD_DIM
    df = jnp.pad(dft.T, ((0, 0), (0, LANES - C_HEADS)))
    g["od_w_in"] = jnp.concatenate([_mm_tn(sv["xn"], dqkv, name="od_in_dw"),
                                    _mm_tn(sv["xn"], df, name="od_in_dwf")[:, :C_HEADS]], axis=-1)
    dh_in, dnorm = _mm_nt_rmsbwd([(dqkv, w["od_in_pad"], 0), (df, w["od_in_pad"], 3 * wd // LANES)],
                                 sv["h_in"], w["mix_norm"][i:i + 1], dh, name="od_in_dx")
    return dh_in, dnorm, g, rode


def _kernel_weights(full, replicated):
    w = dict(replicated)
    _install_weights(w, {(n, i): a for n, per_layer in full.items() for i, a in enumerate(per_layer)})
    return w


def _install_weights(w, got):
    raw = w.setdefault("raw", {})
    raw.update(got)
    for (n, i), a in got.items():
        if n in ("ffa_w_gate_up", "ffa_w_down", "ffb_w_gate_up", "ffb_w_down"):
            w.setdefault(n[:3], {}).setdefault(i, {})["wgu" if n.endswith("gate_up") else "wd"] = a
        elif n in ("ple_w_gate", "ple_w_proj"):
            w.setdefault("ple_gate" if n.endswith("gate") else "ple_proj", {})[i] = a
    if "ev_in_cat" not in w and all((n, 0) in raw for n in ("ev_w_in", "ev_w_uq", "ev_w_ukv", "ev_w_out")):
        w["ev_in_cat"], w["ev_q_cat"], w["ev_ukv"] = _even_weights(raw["ev_w_in", 0], raw["ev_w_uq", 0], raw["ev_w_ukv", 0])
        w["ev_out"] = raw["ev_w_out", 0]
    if "od_in_pad" not in w and all((n, 0) in raw for n in ("od_w_in", "od_w_out")):
        od_in = raw["od_w_in", 0]
        w["od_in_pad"] = jnp.pad(od_in, ((0, 0), (0, (-od_in.shape[1]) % LANES)))
        w["od_out"] = raw["od_w_out", 0]


def _keys(names, layer):
    return tuple((n, layer) for n in names)


_FFA, _FFB, _PLE = ("ffa_w_gate_up", "ffa_w_down"), ("ffb_w_gate_up", "ffb_w_down"), ("ple_w_gate", "ple_w_proj")
_EV, _OD = ("ev_w_in", "ev_w_uq", "ev_w_ukv", "ev_w_out"), ("od_w_in", "od_w_out")
_GATHER_FIRST = _keys(_FFA[:1], 0)
_GATHER_RIDES = {("ffa", 0): _keys(_FFA[1:] + _EV, 0), ("mix", 0): _keys(_FFB + _PLE, 0) + _keys(_FFA[:1], 1),
                 ("ffb", 0): _keys(_FFA[1:], 1), ("ffa", 1): _keys(_OD, 0), ("mix", 1): _keys(_FFB + _PLE, 1)}
_REDUCE_RIDES = {("mix", 1): _keys(_FFB + _PLE, 1), ("mix", 0): _keys(_FFA, 1) + _keys(_OD, 0) + _keys(_FFB + _PLE, 0),
                 ("ffa", 0): _keys(_EV, 0)}
_REDUCE_OWN = ("ffa", 0)
_SWAP_AHEAD = {("ffb", 1): ("mix", 1)}


def _local_step(x, p, tgt, w, ex=None):
    depth = p.shape[0]

    def gather_behind(host, fn, *args):
        keys = None if ex is None else _GATHER_RIDES.get(host)
        if keys is None:
            return fn(*args, None)[:-1]
        done = []

        def install(rode):
            if not done:
                _install_weights(w, ex.gather_finish(keys, rode, name=f"weight_forward_{host[0]}{host[1]}"))
                done.append(True)

        res = fn(*args, ex.gather_rider(keys), install) if fn is _ffn_fwd else fn(*args, ex.gather_rider(keys))
        install(res[-1])
        return res[:-1]

    h = x
    saved = []
    for i in range(depth):
        sv = {}
        h, sv["ffa"] = gather_behind(("ffa", i), _ffn_fwd, h, w["ffa_norm"][i:i + 1], w["ffa"][i], f"ffa{i}")
        h, sv["mix"] = gather_behind(("mix", i), _even_fwd if i % 2 == 0 else _odd_fwd, h, w, i)
        h, sv["ffb"] = gather_behind(("ffb", i), _ffn_fwd, h, w["ffb_norm"][i:i + 1], w["ffb"][i], f"ffb{i}")
        h_in = h
        h, xn, gate, pp = _ple_fwd(h, w["ple_norm"][i:i + 1], w["ple_gate"][i], p[i], w["ple_proj"][i], name=f"ple{i}")
        sv["ple"] = dict(h_in=h_in, xn=xn, gate=gate, pp=pp)
        saved.append(sv)
    loss_vec, dh, d_final = _final_loss(h, w["final_norm"].reshape(1, -1), tgt, name="final_loss")

    per_layer = [dict() for _ in range(depth)]
    mats = {}
    grads = {}

    pending = {}

    def reduce_behind(host, fn, *args):
        keys = None if ex is None else _REDUCE_RIDES.get(host)
        ahead = None if ex is None else _SWAP_AHEAD.get(host)
        if keys is None and ahead is None:
            return fn(*args, None)[:-1]
        if ahead is not None:
            got, ctxs = {}, []

            def note_wd(d_wd):
                got[f"{host[0]}_w_down", host[1]] = d_wd

            def swap_now(brought, d_wgu):
                got[f"{host[0]}_w_gate_up", host[1]] = d_wgu
                swap, ctx = ex.swap_rider(_REDUCE_RIDES[ahead], {**mats, **got})
                ctxs.append(ctx)
                return swap

            def stash(brought):
                pending[ahead] = ex.after_swap(ctxs[0], brought)

            return fn(*args, None, (note_wd, swap_now, stash))[:-1]
        states = []
        if fn is _even_bwd:
            swap, ctx = ex.swap_rider(keys, mats)

            def then(brought):
                states.append(ex.after_swap(ctx, brought))
                return states[0][0]

            res = fn(*args, (swap, then))
        else:
            states.append(pending.pop(host, None) or ex.reduce_begin(keys, mats, tag=f"{host[0]}{host[1]}"))
            if fn is _ffn_bwd and host == _REDUCE_OWN:
                own = []

                def wd_ready(d_wd):
                    own.append(ex.reduce_begin(_keys(_FFA[1:], 0), {("ffa_w_down", 0): d_wd}, tag="own_wd"))
                    return own[0][0]

                def wgu_ready(brought, d_wgu):
                    ex.reduce_finish(own[0], brought)
                    own.append(ex.reduce_begin(_keys(_FFA[:1], 0), {("ffa_w_gate_up", 0): d_wgu}, tag="own_wgu"))
                    return own[1][0]

                res = fn(*args, states[0][0], (wd_ready, wgu_ready, lambda brought: ex.reduce_finish(own[1], brought)))
            else:
                res = fn(*args, states[0][0])
        ex.reduce_finish(states[0], res[-1])
        return res[:-1]

    for i in reversed(range(depth)):
        sv, gl = saved[i], per_layer[i]
        dz, dpp = _ple_bwd_elem(dh, sv["ple"]["gate"], sv["ple"]["pp"], name=f"ple{i}_bwd")
        mats["ple_w_gate", i] = _mm_tn(sv["ple"]["xn"], dz, name=f"ple{i}_dwg")
        mats["ple_w_proj", i] = _mm_tn(p[i], dpp, name=f"ple{i}_dwp")
        dh, gl["ple_norm"] = _mm_nt_rmsbwd([(dz, w["ple_gate"][i])], sv["ple"]["h_in"], w["ple_norm"][i:i + 1], dh,
                                           name=f"ple{i}_dx")
        dh, gl["ffb_norm"], mats["ffb_w_gate_up", i], mats["ffb_w_down", i] = reduce_behind(
            ("ffb", i), _ffn_bwd, dh, w["ffb_norm"][i:i + 1], w["ffb"][i], sv["ffb"], f"ffb{i}")
        dh, gl["mix_norm"], gm = reduce_behind(("mix", i), _even_bwd if i % 2 == 0 else _odd_bwd, dh, w, sv["mix"], i)
        for n, g in gm.items():
            if n in REPLICATED:
                grads[n] = g
            else:
                mats[n, 0] = g
        dh, gl["ffa_norm"], mats["ffa_w_gate_up", i], mats["ffa_w_down", i] = reduce_behind(
            ("ffa", i), _ffn_bwd, dh, w["ffa_norm"][i:i + 1], w["ffa"][i], sv["ffa"], f"ffa{i}")
    grads["final_norm"] = d_final.reshape(-1)
    for n in ("ffa_norm", "mix_norm", "ffb_norm", "ple_norm"):
        grads[n] = jnp.concatenate([per_layer[i][n] for i in range(depth)], axis=0)
    if ex is None:
        for n, _ in SHARDED:
            grads[n] = [mats[n, i] for i in range(depth) if (n, i) in mats]
    return loss_vec[0, 0], dh, grads


def _cut_mode(local_shape, axis, ncols):
    return "cols" if axis == 2 and ncols % LANES == 0 else "blk"


class _Exchange:
    def __init__(self, wts):
        self.place = jnp.stack([2 * lax.axis_index("x") + lax.axis_index("y"), lax.axis_index("c")]).astype(jnp.int32)
        self.info = {}
        for n, axis in SHARDED:
            wb = wts[n].astype(BF16)
            mode = _cut_mode(wb.shape, axis, wb.shape[2])
            for i in range(wb.shape[0]):
                self.info[n, i] = dict(shard=wb[i], mode=mode, axis=axis)
        self.halves = {}

    def _modes(self, keys):
        return [self.info[k]["mode"] for k in keys]

    def gather_rider(self, keys):
        return _gather_rider([self.info[k]["shard"] for k in keys], self._modes(keys))

    def gather_finish(self, keys, landed, *, name):
        outs = _gather_forward(landed, [self.info[k]["shard"].shape for k in keys], self._modes(keys), name=name)
        got = {}
        for k, dst in zip(keys, outs):
            if self.info[k]["mode"] == "blk":
                dst = dst.reshape(-1, dst.shape[2]) if self.info[k]["axis"] == 1 else jnp.moveaxis(dst, 0, 1).reshape(dst.shape[1], -1)
            got[k] = dst
        return got

    def gather(self, keys, *, name):
        return self.gather_finish(keys, _run_rider(self.gather_rider(keys), name=name), name=name + "_forward")

    def swap_rider(self, keys, mats):
        modes = self._modes(keys)
        arrs = []
        for k in keys:
            g2, (rr, cc) = mats[k], self.info[k]["shard"].shape
            if self.info[k]["mode"] == "blk":
                g2 = g2.reshape(N_CHIPS, rr, cc) if self.info[k]["axis"] == 1 else g2.reshape(rr, N_CHIPS, cc).transpose(1, 0, 2)
            arrs.append(g2)
        return _swap_rider(arrs, modes), (keys, modes, arrs)

    def after_swap(self, ctx, landed):
        keys, modes, arrs = ctx
        parts = []
        for (n, i), m, a, l in zip(keys, modes, arrs, landed):
            pt = _rs_pair_add(_blk_view(a, m), _blk_view(l, m), self.place, name=f"rs_pair_add_{n}{i}")
            parts.append(pt[0] if m == "cols" else pt)
        return _exchange_rider(parts, modes), keys, parts

    def reduce_begin(self, keys, mats, *, tag):
        rider, ctx = self.swap_rider(keys, mats)
        return self.after_swap(ctx, _run_rider(rider, name=f"rs_pair_swap_{tag}"))

    def reduce_finish(self, state, landed):
        _, keys, parts = state
        for (n, i), m, pt, l in zip(keys, self._modes(keys), parts, landed):
            self.halves[n, i] = _rs_chip_sum(pt, l, m, self.place, name=f"rs_chip_sum_{n}{i}")

    def reduce(self, keys, mats, *, tag):
        state = self.reduce_begin(keys, mats, tag=tag)
        self.reduce_finish(state, _run_rider(state[0], name=f"rs_chip_exchange_{tag}"))

    def join(self, wts):
        keys = list(self.info)
        joined = dict(zip(keys, _rs_pair_join([self.halves[k] for k in keys], name="rs_pair_join")))
        return {n: jnp.stack([joined[n, i] for i in range(wts[n].shape[0])]).reshape(wts[n].shape) for n, _ in SHARDED}


def _small_rows(vals):
    rows = []
    for n in REPLICATED:
        v = vals[n].reshape(-1)
        rows.append(jnp.pad(v, (0, (-v.shape[0]) % FLAT_COLS)).reshape(-1, FLAT_COLS))
    out = jnp.concatenate(rows, axis=0)
    return jnp.pad(out, ((0, (-out.shape[0]) % 8), (0, 0)))


def kernel(x, p, ffa_norm, ffa_w_gate_up, ffa_w_down, mix_norm, ffb_norm, ffb_w_gate_up, ffb_w_down, ple_norm, ple_w_gate, ple_w_proj, ev_w_in, ev_sinks, ev_cq_norm, ev_w_uq, ev_ckv_norm, ev_w_ukv, ev_w_out, od_w_in, od_b_f, od_w_out, final_norm, loss_target, m_ffa_norm, m_ffa_w_gate_up, m_ffa_w_down, m_mix_norm, m_ffb_norm, m_ffb_w_gate_up, m_ffb_w_down, m_ple_norm, m_ple_w_gate, m_ple_w_proj, m_ev_w_in, m_ev_sinks, m_ev_cq_norm, m_ev_w_uq, m_ev_ckv_norm, m_ev_w_ukv, m_ev_w_out, m_od_w_in, m_od_b_f, m_od_w_out, m_final_norm, v_ffa_norm, v_ffa_w_gate_up, v_ffa_w_down, v_mix_norm, v_ffb_norm, v_ffb_w_gate_up, v_ffb_w_down, v_ple_norm, v_ple_w_gate, v_ple_w_proj, v_ev_w_in, v_ev_sinks, v_ev_cq_norm, v_ev_w_uq, v_ev_ckv_norm, v_ev_w_ukv, v_ev_w_out, v_od_w_in, v_od_b_f, v_od_w_out, v_final_norm):
    env = dict(locals())
    wts = {n: env[n] for n in WEIGHT_ORDER}
    mom1 = {n: env["m_" + n] for n in WEIGHT_ORDER}
    mom2 = {n: env["v_" + n] for n in WEIGHT_ORDER}
    ex = _Exchange(wts)

    w = {n: wts[n] for n in REPLICATED}
    _install_weights(w, ex.gather(_GATHER_FIRST, name="weight_gather_first"))

    loss_part, grad_x, grads = _local_step(x[0], p[:, 0], loss_target[0], w, ex)
    loss = lax.psum(loss_part, ("x", "y", "c"))
    gout = ex.join(wts)
    small = _allreduce_small(_small_rows(grads), name="small_allreduce")
    r0 = 0
    for n in REPLICATED:
        size = int(np.prod(wts[n].shape))
        nr = -(-size // FLAT_COLS)
        gout[n] = small[r0:r0 + nr].reshape(-1)[:size].reshape(wts[n].shape)
        r0 += nr

    delta, new_m, new_v = {}, {}, {}
    for n in WEIGHT_ORDER:
        delta[n], new_m[n], new_v[n] = _adamw(wts[n], gout[n], mom1[n], mom2[n], name="adamw_" + n)
    return (loss, grad_x[None], *[gout[n] for n in WEIGHT_ORDER], *[delta[n] for n in WEIGHT_ORDER],
            *[new_m[n] for n in WEIGHT_ORDER], *[new_v[n] for n in WEIGHT_ORDER])
```

```python
import functools
import math

import numpy as np
import jax
import jax.numpy as jnp
from jax import lax
from jax.experimental import pallas as pl
from jax.experimental.pallas import tpu as pltpu

F32 = jnp.float32
BF16 = jnp.bfloat16
NT = (((1,), (1,)), ((), ()))
TN = (((0,), (0,)), ((), ()))
MESH = pl.DeviceIdType.MESH

RMS_EPS = 1e-6
FFN_RES_SCALE = 0.5
A_HEADS, A_KV_HEADS, A_HEAD_DIM, WINDOW = 8, 2, 64, 128
B_HEADS, B_Q_LORA, B_KV_LORA, B_NOPE, B_ROPE, B_V = 8, 256, 128, 64, 32, 64
ROPE_THETA = 10000.0
C_HEADS, C_HEAD_DIM = 16, 64
ADAM_LR, ADAM_B1, ADAM_B2, ADAM_EPS, ADAM_WD, ADAM_STEP = 0.001, 0.9, 0.999, 1e-08, 0.01, 10

N_CHIPS = 4
LANES = 128
FLAT_COLS = 1024
MASK_VALUE = -1e30
VMEM_LIMIT = 48 * 2**20

SHARDED = (
    ("ffa_w_gate_up", 2), ("ffa_w_down", 1), ("ffb_w_gate_up", 2), ("ffb_w_down", 1),
    ("ple_w_gate", 1), ("ple_w_proj", 2), ("ev_w_in", 2), ("ev_w_uq", 2), ("ev_w_ukv", 2),
    ("ev_w_out", 1), ("od_w_in", 2), ("od_w_out", 1))
REPLICATED = ("ffa_norm", "mix_norm", "ffb_norm", "ple_norm", "final_norm",
              "ev_sinks", "ev_cq_norm", "ev_ckv_norm", "od_b_f")
WEIGHT_ORDER = ("ffa_norm", "ffa_w_gate_up", "ffa_w_down", "mix_norm", "ffb_norm", "ffb_w_gate_up",
                "ffb_w_down", "ple_norm", "ple_w_gate", "ple_w_proj", "ev_w_in", "ev_sinks",
                "ev_cq_norm", "ev_w_uq", "ev_ckv_norm", "ev_w_ukv", "ev_w_out", "od_w_in", "od_b_f",
                "od_w_out", "final_norm")


def _cp(*sem):
    return pltpu.CompilerParams(dimension_semantics=sem, vmem_limit_bytes=VMEM_LIMIT)


def _sigmoid(z):
    return 1.0 / (1.0 + jnp.exp(-z))


def _rms_stats(xv):
    r = lax.rsqrt(jnp.mean(xv * xv, axis=-1, keepdims=True) + RMS_EPS)
    return r, xv * r


def _rms_bwd(dxn, xv, g):
    r, xhat = _rms_stats(xv)
    u = dxn * g
    dx = r * (u - xhat * jnp.mean(u * xhat, axis=-1, keepdims=True))
    return dx, dxn * xhat


def _col_tile(k_rows, n, budget_bytes=6 * 2**20):
    if k_rows * n * 4 <= budget_bytes or n % LANES:
        return n
    units = n // LANES
    best = LANES
    for d in range(1, units + 1):
        if units % d == 0 and k_rows * d * LANES * 4 <= budget_bytes:
            best = d * LANES
    return best


def _row_tile(rows, cols, target_elems=2**18):
    if rows * cols <= target_elems or rows % 8:
        return rows
    best = 8
    for d in range(8, rows + 1, 8):
        if rows % d == 0 and d * cols <= target_elems:
            best = d
    return best


def _fox_in_fwd(x, g, w, *, nheads, dh, q_ones, k_ones, name, tm=512):
    s, k = x.shape
    n = w.shape[1]
    wd = nheads * dh
    spare = LANES - dh

    def body(x_ref, g_ref, w_ref, q_ref, k_ref, v_ref, vt_ref, f_ref, xn_ref):
        _, xhat = _rms_stats(x_ref[...])
        xn = (xhat * g_ref[...]).astype(BF16)
        xn_ref[...] = xn
        y = jnp.dot(xn, w_ref[...], preferred_element_type=F32)
        f_ref[...] = y[:, 3 * wd:]
        lane = lax.broadcasted_iota(jnp.int32, (tm, spare), 1)

        def fill(cols):
            return functools.reduce(jnp.logical_or, [lane == c for c in cols]).astype(F32)

        q_fill, k_fill = fill(q_ones), fill(k_ones)
        for h in range(nheads):
            q_ref[h] = jnp.concatenate([y[:, h * dh:(h + 1) * dh], q_fill], axis=-1).astype(BF16)
            k_ref[h] = jnp.concatenate([y[:, wd + h * dh:wd + (h + 1) * dh], k_fill], axis=-1).astype(BF16)
            vh = y[:, 2 * wd + h * dh:2 * wd + (h + 1) * dh]
            v_ref[h] = vh.astype(BF16)
            vt_ref[h] = vh.T.astype(BF16)

    wide = pl.BlockSpec((nheads, tm, LANES), lambda i: (0, i, 0))
    return pl.pallas_call(
        body, name=name, grid=(s // tm,),
        in_specs=[pl.BlockSpec((tm, k), lambda i: (i, 0)), pl.BlockSpec((1, k), lambda i: (0, 0)),
                  pl.BlockSpec((k, n), lambda i: (0, 0))],
        out_specs=[wide, wide, pl.BlockSpec((nheads, tm, dh), lambda i: (0, i, 0)),
                   pl.BlockSpec((nheads, dh, tm), lambda i: (0, 0, i)), pl.BlockSpec((tm, LANES), lambda i: (i, 0)),
                   pl.BlockSpec((tm, k), lambda i: (i, 0))],
        out_shape=[jax.ShapeDtypeStruct((nheads, s, LANES), BF16)] * 2
        + [jax.ShapeDtypeStruct((nheads, s, dh), BF16), jax.ShapeDtypeStruct((nheads, dh, s), BF16),
           jax.ShapeDtypeStruct((s, LANES), F32), jax.ShapeDtypeStruct((s, k), BF16)],
        compiler_params=_cp("arbitrary"))(x, g, w)


def _merge_heads(dq, dk, dvt, *, dh, q_col, k_col, name, tm=512):
    nheads, s, _ = dq.shape

    def body(dq_ref, dk_ref, dvt_ref, o_ref, cols_ref):
        pieces = [dq_ref[h][:, :dh] for h in range(nheads)] + [dk_ref[h][:, :dh] for h in range(nheads)]
        pieces += [dvt_ref[h].T for h in range(nheads)]
        o_ref[...] = jnp.concatenate(pieces, axis=-1)
        lane = lax.broadcasted_iota(jnp.int32, (tm, LANES), 1)
        cols = jnp.zeros((tm, LANES), F32)
        for h in range(nheads):
            cols = jnp.where(lane == h, jnp.broadcast_to(dq_ref[h][:, q_col:q_col + 1], (tm, LANES)), cols)
            cols = jnp.where(lane == nheads + h, jnp.broadcast_to(dk_ref[h][:, k_col:k_col + 1], (tm, LANES)), cols)
        cols_ref[...] = cols

    wide = pl.BlockSpec((nheads, tm, LANES), lambda i: (0, i, 0))
    return pl.pallas_call(
        body, name=name, grid=(s // tm,),
        in_specs=[wide, wide, pl.BlockSpec((nheads, dh, tm), lambda i: (0, 0, i))],
        out_specs=[pl.BlockSpec((tm, 3 * nheads * dh), lambda i: (i, 0)), pl.BlockSpec((tm, LANES), lambda i: (i, 0))],
        out_shape=[jax.ShapeDtypeStruct((s, 3 * nheads * dh), F32), jax.ShapeDtypeStruct((s, LANES), F32)],
        compiler_params=_cp("arbitrary"))(dq, dk, dvt)


def _row_call(body, n_rows, ins, outs, *, name, tm=512):
    def spec(a, axis):
        shape = a.shape
        if axis is None:
            return pl.BlockSpec(shape, lambda i: (0,) * len(shape))
        blk = tuple(tm if d == axis else n for d, n in enumerate(shape))
        return pl.BlockSpec(blk, lambda i: tuple(i if d == axis else 0 for d in range(len(shape))))

    return pl.pallas_call(
        body, name=name, grid=(n_rows // tm,), in_specs=[spec(a, ax) for a, ax in ins],
        out_specs=[spec(a, ax) for a, ax in outs], out_shape=[a for a, _ in outs],
        compiler_params=_cp("arbitrary"))(*[a for a, _ in ins])


def _sds(shape, dtype):
    return jax.ShapeDtypeStruct(shape, dtype)


def _ev_in_fwd(x, g, w, *, name):
    s, k = x.shape
    d = A_HEAD_DIM

    def body(x_ref, g_ref, w_ref, q_ref, k_ref, v_ref, vt_ref, cq_ref, ckv_ref, kr_ref, xn_ref):
        _, xhat = _rms_stats(x_ref[...])
        xn = (xhat * g_ref[...]).astype(BF16)
        xn_ref[...] = xn
        y = jnp.dot(xn, w_ref[...], preferred_element_type=F32)
        for h in range(A_HEADS):
            q_ref[h] = y[:, h * d:(h + 1) * d].astype(BF16)
        for h in range(A_KV_HEADS):
            k_ref[h] = y[:, 512 + h * d:512 + (h + 1) * d].astype(BF16)
            vh = y[:, 640 + h * d:640 + (h + 1) * d]
            v_ref[h] = vh.astype(BF16)
            vt_ref[h] = vh.T.astype(BF16)
        cq_ref[...] = y[:, 768:1024]
        ckv_ref[...] = y[:, 1024:1152]
        kr_ref[...] = y[:, 1152:1280]

    return _row_call(
        body, s, [(x, 0), (g, None), (w, None)],
        [(_sds((A_HEADS, s, d), BF16), 1), (_sds((A_KV_HEADS, s, d), BF16), 1), (_sds((A_KV_HEADS, s, d), BF16), 1),
         (_sds((A_KV_HEADS, d, s), BF16), 2), (_sds((s, B_Q_LORA), F32), 0), (_sds((s, B_KV_LORA), F32), 0),
         (_sds((s, LANES), F32), 0), (_sds((s, k), BF16), 0)], name=name)


def _ev_q_fwd(x, g, w, cos, sin, *, name):
    s, k = x.shape
    rot = B_HEADS * B_ROPE

    def body(x_ref, g_ref, w_ref, c_ref, s_ref, q_ref, xn_ref):
        _, xhat = _rms_stats(x_ref[...])
        xn = (xhat * g_ref[...]).astype(BF16)
        xn_ref[...] = xn
        y = jnp.dot(xn, w_ref[...], preferred_element_type=F32)
        ro = y[:, 512:512 + rot] * c_ref[...] + y[:, 512 + rot:] * s_ref[...]
        zero = jnp.zeros((y.shape[0], LANES - B_NOPE - B_ROPE), F32)
        for h in range(B_HEADS):
            q_ref[h] = jnp.concatenate([y[:, h * B_NOPE:(h + 1) * B_NOPE], ro[:, h * B_ROPE:(h + 1) * B_ROPE], zero],
                                       axis=-1).astype(BF16)

    return _row_call(body, s, [(x, 0), (g, None), (w, None), (cos, 0), (sin, 0)],
                     [(_sds((B_HEADS, s, LANES), BF16), 1), (_sds((s, k), BF16), 0)], name=name)


def _ev_kv_fwd(x, g, w, kro, *, name):
    s, k = x.shape
    per = B_NOPE + B_V

    def body(x_ref, g_ref, w_ref, kr_ref, k_ref, v_ref, vt_ref, xn_ref):
        _, xhat = _rms_stats(x_ref[...])
        xn = (xhat * g_ref[...]).astype(BF16)
        xn_ref[...] = xn
        y = jnp.dot(xn, w_ref[...], preferred_element_type=F32)
        kr = kr_ref[...]
        zero = jnp.zeros((y.shape[0], LANES - B_NOPE - B_ROPE), F32)
        for h in range(B_HEADS):
            k_ref[h] = jnp.concatenate([y[:, h * per:h * per + B_NOPE], kr, zero], axis=-1).astype(BF16)
            vh = y[:, h * per + B_NOPE:(h + 1) * per]
            v_ref[h] = vh.astype(BF16)
            vt_ref[h] = vh.T.astype(BF16)

    return _row_call(body, s, [(x, 0), (g, None), (w, None), (kro, 0)],
                     [(_sds((B_HEADS, s, LANES), BF16), 1), (_sds((B_HEADS, s, B_V), BF16), 1),
                      (_sds((B_HEADS, B_V, s), BF16), 2), (_sds((s, k), BF16), 0)], name=name)


def _ev_q_merge(dq, cos, sin, *, name):
    nh, s, _ = dq.shape

    def body(dq_ref, c_ref, s_ref, o_ref):
        dro = jnp.concatenate([dq_ref[h][:, B_NOPE:B_NOPE + B_ROPE] for h in range(nh)], axis=-1)
        o_ref[...] = jnp.concatenate([dq_ref[h][:, :B_NOPE] for h in range(nh)] + [dro * c_ref[...], dro * s_ref[...]], axis=-1)

    return _row_call(body, s, [(dq, 1), (cos, 0), (sin, 0)], [(_sds((s, 2 * nh * B_NOPE), F32), 0)], name=name)[0]


def _ev_kv_merge(dk, dvt, cos, sin, *, name):
    nh, s, _ = dk.shape

    def body(dk_ref, dvt_ref, c_ref, s_ref, o_ref, kr_ref):
        pieces = []
        tot = None
        for h in range(nh):
            pieces += [dk_ref[h][:, :B_NOPE], dvt_ref[h].T]
            rot = dk_ref[h][:, B_NOPE:B_NOPE + B_ROPE]
            tot = rot if tot is None else tot + rot
        o_ref[...] = jnp.concatenate(pieces, axis=-1)
        kr_ref[...] = jnp.concatenate([tot * c_ref[...], tot * s_ref[...], jnp.zeros((tot.shape[0], LANES - 2 * B_ROPE), F32)],
                                      axis=-1)

    return _row_call(body, s, [(dk, 1), (dvt, 2), (cos, 0), (sin, 0)],
                     [(_sds((s, nh * (B_NOPE + B_V)), F32), 0), (_sds((s, LANES), F32), 0)], name=name)


def _ev_in_merge(dq, dk, dvt, dcq, dckv, dkr, *, name):
    s = dcq.shape[0]

    def body(dq_ref, dk_ref, dvt_ref, cq_ref, ckv_ref, kr_ref, o_ref):
        pieces = [dq_ref[h] for h in range(A_HEADS)] + [dk_ref[h] for h in range(A_KV_HEADS)]
        pieces += [dvt_ref[h].T for h in range(A_KV_HEADS)] + [cq_ref[...], ckv_ref[...], kr_ref[...]]
        o_ref[...] = jnp.concatenate(pieces, axis=-1)

    return _row_call(body, s, [(dq, 1), (dk, 1), (dvt, 2), (dcq, 0), (dckv, 0), (dkr, 0)],
                     [(_sds((s, 1280), F32), 0)], name=name)[0]


def _ffn_up(x, g, wgu, *, name, tm=512, rider=None):
    s, k = x.shape
    f = wgu.shape[1] // 2
    tn = _col_tile(k, f)
    nj = f // tn

    def body(x_ref, g_ref, wg_ref, wu_ref, dgate_ref, dup_ref, act_ref, xn_ref, xn_sc):
        @pl.when(pl.program_id(1) == 0)
        def _():
            _, xhat = _rms_stats(x_ref[...])
            xn = (xhat * g_ref[...]).astype(BF16)
            xn_sc[...] = xn
            xn_ref[...] = xn

        xn = xn_sc[...]
        gg = jnp.dot(xn, wg_ref[...], preferred_element_type=F32)
        uu = jnp.dot(xn, wu_ref[...], preferred_element_type=F32)
        sg = _sigmoid(gg)
        silu = gg * sg
        dgate_ref[...] = (uu * (sg * (1.0 + gg * (1.0 - sg)))).astype(BF16)
        dup_ref[...] = silu.astype(BF16)
        act_ref[...] = (silu * uu).astype(BF16)

    tile = pl.BlockSpec((tm, tn), lambda i, j: (i, j))
    return _call_with_rider(
        body, rider, name=name, grid=(s // tm, nj),
        in_specs=[pl.BlockSpec((tm, k), lambda i, j: (i, 0)), pl.BlockSpec((1, k), lambda i, j: (0, 0)),
                  pl.BlockSpec((k, tn), lambda i, j: (0, j)), pl.BlockSpec((k, tn), lambda i, j: (0, j + nj))],
        out_specs=[tile, tile, tile, pl.BlockSpec((tm, k), lambda i, j: (i, 0))],
        out_shape=[jax.ShapeDtypeStruct((s, f), BF16)] * 3 + [jax.ShapeDtypeStruct((s, k), BF16)],
        scratch_shapes=[pltpu.VMEM((tm, k), BF16)],
        compiler_params=_cp("arbitrary", "arbitrary"), args=(x, g, wgu, wgu))


def _mm_res_fwd(a, w, res, *, scale, name, tm=512, a_t=False):
    k, n = w.shape
    s = res.shape[0]

    def body(a_ref, w_ref, r_ref, o_ref):
        prod = (lax.dot_general(a_ref[...], w_ref[...], TN, preferred_element_type=F32) if a_t
                else jnp.dot(a_ref[...], w_ref[...], preferred_element_type=F32))
        o_ref[...] = r_ref[...] + scale * prod

    a_spec = pl.BlockSpec((k, tm), lambda i: (0, i)) if a_t else pl.BlockSpec((tm, k), lambda i: (i, 0))
    return pl.pallas_call(
        body, name=name, grid=(s // tm,),
        in_specs=[a_spec, pl.BlockSpec((k, n), lambda i: (0, 0)),
                  pl.BlockSpec((tm, n), lambda i: (i, 0))],
        out_specs=pl.BlockSpec((tm, n), lambda i: (i, 0)),
        out_shape=jax.ShapeDtypeStruct((s, n), F32),
        compiler_params=_cp("arbitrary"))(a, w, res)


def _ffn_down_bwd(dh, wd, dact_dgate, dact_dup, *, scale, name, tm=512, rider=None):
    s, d = dh.shape
    f = wd.shape[0]
    tn = _col_tile(d, f)

    def body(dh_ref, wd_ref, fg_ref, fu_ref, dg_ref, du_ref):
        dhb = (dh_ref[...] * scale).astype(BF16)
        da = lax.dot_general(dhb, wd_ref[...], NT, preferred_element_type=F32)
        dg_ref[...] = (da * fg_ref[...].astype(F32)).astype(BF16)
        du_ref[...] = (da * fu_ref[...].astype(F32)).astype(BF16)

    tile = pl.BlockSpec((tm, tn), lambda i, j: (i, j))
    return _call_with_rider(
        body, rider, name=name, grid=(s // tm, f // tn),
        in_specs=[pl.BlockSpec((tm, d), lambda i, j: (i, 0)), pl.BlockSpec((tn, d), lambda i, j: (j, 0)), tile, tile],
        out_specs=[tile, tile],
        out_shape=[jax.ShapeDtypeStruct((s, f), BF16)] * 2, scratch_shapes=[],
        compiler_params=_cp("arbitrary", "arbitrary"), args=(dh, wd, dact_dgate, dact_dup))


def _mm_tn(a, bs, *, name, b_scale=1.0, ts=512, rider=None, a_t=False):
    bs = list(bs) if isinstance(bs, (list, tuple)) else [bs]
    k, s = a.shape if a_t else a.shape[::-1]
    n = bs[0].shape[1]
    tn = _col_tile(k, n, 12 * 2**20)
    per = n // tn

    def body(a_ref, *refs):
        b_refs, o_ref = refs[:-1], refs[-1]
        j = pl.program_id(0)

        @pl.when(pl.program_id(1) == 0)
        def _():
            o_ref[...] = jnp.zeros_like(o_ref)

        for m, b_ref in enumerate(b_refs):
            def acc(b_ref=b_ref):
                bv = b_ref[...]
                if b_scale != 1.0:
                    bv = bv * b_scale
                av = a_ref[...].astype(BF16)
                o_ref[...] += (jnp.dot(av, bv.astype(BF16), preferred_element_type=F32) if a_t
                               else lax.dot_general(av, bv.astype(BF16), TN, preferred_element_type=F32))

            if len(b_refs) == 1:
                acc()
            else:
                pl.when(jnp.logical_and(j >= m * per, j < (m + 1) * per))(acc)

    def b_spec(m):
        def idx(j, t):
            mine = jnp.logical_and(j >= m * per, j < (m + 1) * per)
            return (jnp.where(mine, t, 0), jnp.clip(j - m * per, 0, per - 1))
        return pl.BlockSpec((ts, tn), idx)

    (out,), rode = _call_with_rider(
        body, rider, name=name, grid=(per * len(bs), s // ts),
        in_specs=[pl.BlockSpec((k, ts), lambda j, t: (0, t)) if a_t else pl.BlockSpec((ts, k), lambda j, t: (t, 0))]
        + [b_spec(m) for m in range(len(bs))],
        out_specs=[pl.BlockSpec((k, tn), lambda j, t: (0, j))],
        out_shape=[jax.ShapeDtypeStruct((k, n * len(bs)), F32)], scratch_shapes=[],
        compiler_params=_cp("arbitrary", "arbitrary"), args=(a, *bs))
    return out if rider is None else (out, rode)


def _mm_nt_t(dy, w, *, name, tm=512):
    s, n = dy.shape
    k = w.shape[0]

    def body(dy_ref, w_ref, o_ref):
        o_ref[...] = lax.dot_general(w_ref[...], dy_ref[...].astype(BF16), NT, preferred_element_type=F32).astype(BF16)

    return pl.pallas_call(
        body, name=name, grid=(s // tm,),
        in_specs=[pl.BlockSpec((tm, n), lambda i: (i, 0)), pl.BlockSpec((k, n), lambda i: (0, 0))],
        out_specs=pl.BlockSpec((k, tm), lambda i: (0, i)),
        out_shape=jax.ShapeDtypeStruct((k, s), BF16),
        compiler_params=_cp("arbitrary"))(dy, w)


def _mm_nt_rmsbwd(pairs, x, g, dres, *, name, tm=512, rider=None):
    s, k = x.shape
    npairs = len(pairs)
    pairs = [pr if len(pr) == 3 else (pr[0], pr[1], 0) for pr in pairs]

    def body(*refs):
        dy_refs = refs[0:2 * npairs:2]
        w_refs = refs[1:2 * npairs:2]
        rest = refs[2 * npairs:]
        x_ref, g_ref = rest[0], rest[1]
        if dres is None:
            dx_ref, dg_ref = rest[2], rest[3]
        else:
            dres_ref, dx_ref, dg_ref = rest[2], rest[3], rest[4]
        dxn = None
        for dy_ref, w_ref in zip(dy_refs, w_refs):
            t = lax.dot_general(dy_ref[...].astype(BF16), w_ref[...], NT, preferred_element_type=F32)
            dxn = t if dxn is None else dxn + t
        dx, dgrow = _rms_bwd(dxn, x_ref[...], g_ref[...])
        if dres is not None:
            dx = dx + dres_ref[...]
        dx_ref[...] = dx

        @pl.when(pl.program_id(0) == 0)
        def _():
            dg_ref[...] = jnp.zeros_like(dg_ref)

        dg_ref[...] += jnp.sum(dgrow, axis=0, keepdims=True)

    in_specs, args = [], []
    for dy, w, cb in pairs:
        n = dy.shape[1]
        in_specs += [pl.BlockSpec((tm, n), lambda i: (i, 0)),
                     pl.BlockSpec((k, n), lambda i, cb=cb: (0, cb), pipeline_mode=pl.Buffered(1))]
        args += [dy, w]
    row = pl.BlockSpec((tm, k), lambda i: (i, 0))
    vec = pl.BlockSpec((1, k), lambda i: (0, 0))
    in_specs += [row, vec]
    args += [x, g]
    if dres is not None:
        in_specs.append(row)
        args.append(dres)
    (dx, dgain), rode = _call_with_rider(
        body, rider, name=name, grid=(s // tm,), in_specs=in_specs, out_specs=[row, vec],
        out_shape=[jax.ShapeDtypeStruct((s, k), F32), jax.ShapeDtypeStruct((1, k), F32)], scratch_shapes=[],
        compiler_params=_cp("arbitrary"), args=args)
    return (dx, dgain) if rider is None else (dx, dgain, rode)


def _ple_fwd(h, g, wg, p, wp, *, name, tm=512):
    s, d = h.shape
    pd = p.shape[1]

    def body(h_ref, g_ref, wg_ref, p_ref, wp_ref, o_ref, xn_ref, gate_ref, pp_ref):
        hv = h_ref[...]
        _, xhat = _rms_stats(hv)
        xn = (xhat * g_ref[...]).astype(BF16)
        xn_ref[...] = xn
        gate = _sigmoid(jnp.dot(xn, wg_ref[...], preferred_element_type=F32))
        pp = jnp.dot(p_ref[...].astype(BF16), wp_ref[...], preferred_element_type=F32)
        gate_ref[...] = gate.astype(BF16)
        pp_ref[...] = pp.astype(BF16)
        o_ref[...] = hv + gate * pp

    row = pl.BlockSpec((tm, d), lambda i: (i, 0))
    return pl.pallas_call(
        body, name=name, grid=(s // tm,),
        in_specs=[row, pl.BlockSpec((1, d), lambda i: (0, 0)), pl.BlockSpec((d, d), lambda i: (0, 0)),
                  pl.BlockSpec((tm, pd), lambda i: (i, 0)), pl.BlockSpec((pd, d), lambda i: (0, 0))],
        out_specs=[row, row, row, row],
        out_shape=[jax.ShapeDtypeStruct((s, d), F32)] + [jax.ShapeDtypeStruct((s, d), BF16)] * 3,
        compiler_params=_cp("arbitrary"))(h, g, wg, p, wp)


def _ple_bwd_elem(dh, gate, pp, *, name, tm=512):
    s, d = dh.shape

    def body(dh_ref, gate_ref, pp_ref, dz_ref, dpp_ref):
        dhv = dh_ref[...]
        gt = gate_ref[...].astype(F32)
        dz_ref[...] = (dhv * pp_ref[...].astype(F32) * (gt * (1.0 - gt))).astype(BF16)
        dpp_ref[...] = (dhv * gt).astype(BF16)

    row = pl.BlockSpec((tm, d), lambda i: (i, 0))
    return pl.pallas_call(
        body, name=name, grid=(s // tm,), in_specs=[row, row, row], out_specs=[row, row],
        out_shape=[jax.ShapeDtypeStruct((s, d), BF16)] * 2,
        compiler_params=_cp("arbitrary"))(dh, gate, pp)


def _final_loss(h, g, tgt, *, name, tm=512):
    s, d = h.shape

    def body(h_ref, g_ref, t_ref, loss_ref, dh_ref, dg_ref):
        @pl.when(pl.program_id(0) == 0)
        def _():
            loss_ref[...] = jnp.zeros_like(loss_ref)
            dg_ref[...] = jnp.zeros_like(dg_ref)

        hv = h_ref[...]
        gv = g_ref[...]
        _, xhat = _rms_stats(hv)
        err = xhat * gv - t_ref[...]
        per_row = jnp.mean(err * err, axis=-1, keepdims=True)
        loss_ref[...] += 0.5 * jnp.sum(per_row, axis=0, keepdims=True)
        dx, dgrow = _rms_bwd(err * (1.0 / d), hv, gv)
        dh_ref[...] = dx
        dg_ref[...] += jnp.sum(dgrow, axis=0, keepdims=True)

    row = pl.BlockSpec((tm, d), lambda i: (i, 0))
    vec = pl.BlockSpec((1, d), lambda i: (0, 0))
    return pl.pallas_call(
        body, name=name, grid=(s // tm,), in_specs=[row, vec, row],
        out_specs=[pl.BlockSpec((1, LANES), lambda i: (0, 0)), row, vec],
        out_shape=[jax.ShapeDtypeStruct((1, LANES), F32), jax.ShapeDtypeStruct((s, d), F32),
                   jax.ShapeDtypeStruct((1, d), F32)],
        compiler_params=_cp("arbitrary"))(h, g, tgt)


def _rope_fwd(y1, y2, cos, sin, *, name, tm=512):
    s, r = y1.shape

    def body(a_ref, b_ref, c_ref, s_ref, o_ref):
        o_ref[...] = a_ref[...] * c_ref[...] + b_ref[...] * s_ref[...]

    row = pl.BlockSpec((tm, r), lambda i: (i, 0))
    return pl.pallas_call(
        body, name=name, grid=(s // tm,), in_specs=[row] * 4, out_specs=row,
        out_shape=jax.ShapeDtypeStruct((s, r), F32), compiler_params=_cp("arbitrary"))(y1, y2, cos, sin)


def _split3(v):
    h1 = v.astype(BF16)
    r1 = v - h1.astype(F32)
    h2 = r1.astype(BF16)
    h3 = (r1 - h2.astype(F32)).astype(BF16)
    return h1, h2, h3


def _tri(tb, upper):
    r = lax.broadcasted_iota(jnp.int32, (tb, tb), 0)
    c = lax.broadcasted_iota(jnp.int32, (tb, tb), 1)
    return jnp.where((r <= c) if upper else (r >= c), 1.0, 0.0).astype(BF16)


def _fox_gate_fwd(ft, bf, *, out_scale, name, tb=512):
    nh, s = ft.shape

    def body(f_ref, b_ref, o_ref, carry):
        @pl.when(pl.program_id(0) == 0)
        def _():
            carry[...] = jnp.zeros_like(carry)

        z = f_ref[...] + b_ref[...]
        lf = jnp.minimum(z, 0.0) - jnp.log(1.0 + jnp.exp(-jnp.abs(z)))
        tri = _tri(tb, True)
        cs = sum(jnp.dot(t, tri, preferred_element_type=F32) for t in _split3(lf)) + carry[...]
        for n, term in enumerate(_split3(cs * out_scale)):
            o_ref[n] = term
        carry[...] += jnp.sum(lf, axis=-1, keepdims=True)

    return pl.pallas_call(
        body, name=name, grid=(s // tb,),
        in_specs=[pl.BlockSpec((nh, tb), lambda t: (0, t)), pl.BlockSpec((nh, 1), lambda t: (0, 0))],
        out_specs=pl.BlockSpec((3, nh, tb), lambda t: (0, 0, t)),
        out_shape=jax.ShapeDtypeStruct((3, nh, s), BF16),
        scratch_shapes=[pltpu.VMEM((nh, 1), F32)], compiler_params=_cp("arbitrary"))(ft, bf)


def _fox_gate_bwd(drow, dcol, ft, bf, *, inv_scale, name, tb=512):
    nh, s = ft.shape
    nb = s // tb

    def body(dr_ref, dc_ref, f_ref, b_ref, df_ref, db_ref, carry):
        @pl.when(pl.program_id(0) == 0)
        def _():
            carry[...] = jnp.zeros_like(carry)
            db_ref[...] = jnp.zeros_like(db_ref)

        dc = (dr_ref[...] - dc_ref[...]) * inv_scale
        tri = _tri(tb, False)
        suf = sum(jnp.dot(t, tri, preferred_element_type=F32) for t in _split3(dc)) + carry[...]
        z = f_ref[...] + b_ref[...]
        dz = suf * (1.0 / (1.0 + jnp.exp(z)))
        df_ref[...] = dz
        db_ref[...] += jnp.sum(dz, axis=-1, keepdims=True)
        carry[...] += jnp.sum(dc, axis=-1, keepdims=True)

    rev = pl.BlockSpec((nh, tb), lambda t: (0, nb - 1 - t))
    one = pl.BlockSpec((nh, 1), lambda t: (0, 0))
    return pl.pallas_call(
        body, name=name, grid=(nb,), in_specs=[rev, rev, rev, one], out_specs=[rev, one],
        out_shape=[jax.ShapeDtypeStruct((nh, s), F32), jax.ShapeDtypeStruct((nh, 1), F32)],
        scratch_shapes=[pltpu.VMEM((nh, 1), F32)], compiler_params=_cp("arbitrary"))(drow, dcol, ft, bf)


def _tri_fwd(t, nq):
    i = sum((t >= (r * (r + 1)) // 2).astype(jnp.int32) for r in range(1, nq))
    return i, t - (i * (i + 1)) // 2


def _tri_bwd(t, nq):
    j = sum((t >= r * nq - (r * (r - 1)) // 2).astype(jnp.int32) for r in range(1, nq))
    return j, j + t - (j * nq - (j * (j - 1)) // 2)


def _scores_t(k, q, *, scale, diag):
    s = lax.dot_general(k, q, NT, preferred_element_type=F32) * scale
    if diag:
        r = lax.broadcasted_iota(jnp.int32, s.shape, 0)
        c = lax.broadcasted_iota(jnp.int32, s.shape, 1)
        s = jnp.where(r <= c, s, MASK_VALUE)
    return s


def _causal_fwd_t(q, k, vt, *, scale, name, tq, hb=2, rider=None):
    nh, s, dq = q.shape
    dv = vt.shape[1]
    nq = s // tq
    nsteps = (nq * (nq + 1)) // 2

    def body(q_ref, k_ref, vt_ref, o_ref, lse_ref, m_sc, l_sc, acc_sc):
        i, j = _tri_fwd(pl.program_id(1), nq)

        @pl.when(j == 0)
        def _():
            m_sc[...] = jnp.full_like(m_sc, MASK_VALUE)
            l_sc[...] = jnp.zeros_like(l_sc)
            acc_sc[...] = jnp.zeros_like(acc_sc)

        def step(diag):
            for u in range(hb):
                sc = _scores_t(k_ref[u], q_ref[u], scale=scale, diag=diag)
                m_prev = m_sc[u]
                m_new = jnp.maximum(m_prev, jnp.max(sc, axis=0, keepdims=True))
                alpha = jnp.exp(m_prev - m_new)
                pr = jnp.exp(sc - m_new)
                l_new = alpha * l_sc[u] + jnp.sum(pr, axis=0, keepdims=True)
                acc = alpha * acc_sc[u] + jnp.dot(vt_ref[u], pr.astype(BF16), preferred_element_type=F32)
                if diag:
                    o_ref[u] = (acc / l_new).astype(BF16)
                    lse_ref[u] = m_new + jnp.log(l_new)
                else:
                    m_sc[u], l_sc[u], acc_sc[u] = m_new, l_new, acc

        pl.when(j < i)(functools.partial(step, False))
        pl.when(j == i)(functools.partial(step, True))

    def qi(t):
        return _tri_fwd(t, nq)[0]

    def kj(t):
        return _tri_fwd(t, nq)[1]

    return _call_with_rider(
        body, rider, name=name, grid=(nh // hb, nsteps),
        in_specs=[pl.BlockSpec((hb, tq, dq), lambda hp, t: (hp, qi(t), 0)),
                  pl.BlockSpec((hb, tq, dq), lambda hp, t: (hp, kj(t), 0)),
                  pl.BlockSpec((hb, dv, tq), lambda hp, t: (hp, 0, kj(t)))],
        out_specs=[pl.BlockSpec((hb, dv, tq), lambda hp, t: (hp, 0, qi(t))),
                   pl.BlockSpec((hb, 1, tq), lambda hp, t: (hp, 0, qi(t)))],
        out_shape=[jax.ShapeDtypeStruct((nh, dv, s), BF16), jax.ShapeDtypeStruct((nh, 1, s), F32)],
        scratch_shapes=[pltpu.VMEM((hb, 1, tq), F32), pltpu.VMEM((hb, 1, tq), F32), pltpu.VMEM((hb, dv, tq), F32)],
        compiler_params=_cp("arbitrary", "arbitrary"), args=(q, k, vt))


def _causal_bwd_t(q, k, v, ot, dot_, lse, *, scale, name, tq, hb=2, rider=None):
    nh, s, dq = q.shape
    dv = v.shape[-1]
    nq = s // tq
    nsteps = (nq * (nq + 1)) // 2

    def body(q_ref, k_ref, v_ref, ot_ref, dot_ref, lse_ref, dq_ref, dk_ref, dvt_ref):
        t = pl.program_id(1)
        j, i = _tri_bwd(t, nq)

        @pl.when(t == 0)
        def _():
            dq_ref[...] = jnp.zeros_like(dq_ref)

        def step(diag):
            rows = pl.ds(pl.multiple_of(i * tq, tq), tq)
            for u in range(hb):
                qv, kv, dov = q_ref[u], k_ref[u], dot_ref[u]
                pr = jnp.exp(_scores_t(kv, qv, scale=scale, diag=diag) - lse_ref[u])
                dp = jnp.dot(v_ref[u], dov, preferred_element_type=F32)
                delta = jnp.sum(dov.astype(F32) * ot_ref[u].astype(F32), axis=0, keepdims=True)
                dsb = ((pr * (dp - delta)) * scale).astype(BF16)
                d_v = lax.dot_general(dov, pr.astype(BF16), NT, preferred_element_type=F32)
                d_k = jnp.dot(dsb, qv, preferred_element_type=F32)
                if diag:
                    dvt_ref[u], dk_ref[u] = d_v, d_k
                else:
                    dvt_ref[u] += d_v
                    dk_ref[u] += d_k
                dq_ref[u, rows, :] += lax.dot_general(dsb, kv, TN, preferred_element_type=F32)

        pl.when(i > j)(functools.partial(step, False))
        pl.when(i == j)(functools.partial(step, True))

    def qi(t):
        return _tri_bwd(t, nq)[1]

    def kj(t):
        return _tri_bwd(t, nq)[0]

    rows_q = pl.BlockSpec((hb, tq, dq), lambda hp, t: (hp, qi(t), 0))
    rows_k = pl.BlockSpec((hb, tq, dq), lambda hp, t: (hp, kj(t), 0))
    lanes_q = pl.BlockSpec((hb, dv, tq), lambda hp, t: (hp, 0, qi(t)))
    return _call_with_rider(
        body, rider, name=name, grid=(nh // hb, nsteps),
        in_specs=[rows_q, rows_k, pl.BlockSpec((hb, tq, dv), lambda hp, t: (hp, kj(t), 0)), lanes_q, lanes_q,
                  pl.BlockSpec((hb, 1, tq), lambda hp, t: (hp, 0, qi(t)))],
        out_specs=[pl.BlockSpec((hb, s, dq), lambda hp, t: (hp, 0, 0)), rows_k,
                   pl.BlockSpec((hb, dv, tq), lambda hp, t: (hp, 0, kj(t)))],
        out_shape=[jax.ShapeDtypeStruct((nh, s, dq), F32), jax.ShapeDtypeStruct((nh, s, dq), F32),
                   jax.ShapeDtypeStruct((nh, dv, s), F32)],
        scratch_shapes=[], compiler_params=_cp("arbitrary", "arbitrary"), args=(q, k, v, ot, dot_, lse))


def _swa_scores_t(k, q, dist, ok, *, scale, slope):
    s = lax.dot_general(k, q, NT, preferred_element_type=F32) * scale - slope * dist.astype(F32)
    return jnp.where(ok, s, MASK_VALUE)


def _swa_geometry(tb, w, has_other):
    r = lax.broadcasted_iota(jnp.int32, (tb, tb), 0)
    c = lax.broadcasted_iota(jnp.int32, (tb, tb), 1)
    d_same = c - r
    ok_same = jnp.logical_and(d_same >= 0, d_same < w)

    def other(ncols):
        rr = lax.broadcasted_iota(jnp.int32, (w, ncols), 0)
        cc = lax.broadcasted_iota(jnp.int32, (w, ncols), 1)
        dd = cc + w - rr
        return dd, jnp.logical_and(dd < w, has_other)

    return (d_same, ok_same), other


def _swa_fwd_t(q, k, vt, slopes_sinks, *, scale, window, name, tb=256):
    nh, s, d = q.shape
    nkv = k.shape[0]
    grp = nh // nkv
    w = window
    per = tb // w
    assert tb % w == 0

    def body(q_ref, kc_ref, kp_ref, vc_ref, vp_ref, ss_ref, o_ref, lse_ref):
        kvh, i = pl.program_id(0), pl.program_id(1)
        (d_c, ok_c), other = _swa_geometry(tb, w, i > 0)
        d_p, ok_p = other(tb)
        for g in range(grp):
            h = kvh * grp + g
            slope, sink = ss_ref[0, h], ss_ref[1, h]
            qg = q_ref[g]
            s_c = _swa_scores_t(kc_ref[...], qg, d_c, ok_c, scale=scale, slope=slope)
            s_p = _swa_scores_t(kp_ref[...], qg, d_p, ok_p, scale=scale, slope=slope)
            m = jnp.maximum(jnp.maximum(jnp.max(s_c, axis=0, keepdims=True), jnp.max(s_p, axis=0, keepdims=True)), sink)
            p_c, p_p = jnp.exp(s_c - m), jnp.exp(s_p - m)
            l = jnp.sum(p_c, axis=0, keepdims=True) + jnp.sum(p_p, axis=0, keepdims=True) + jnp.exp(sink - m)
            acc = (jnp.dot(vc_ref[...], p_c.astype(BF16), preferred_element_type=F32)
                   + jnp.dot(vp_ref[...], p_p.astype(BF16), preferred_element_type=F32))
            o_ref[g] = (acc / l).astype(BF16)
            lse_ref[g] = m + jnp.log(l)

    def prev(i):
        return jnp.maximum(i * per - 1, 0)

    return pl.pallas_call(
        body, name=name, grid=(nkv, s // tb),
        in_specs=[pl.BlockSpec((grp, tb, d), lambda kh, i: (kh, i, 0)),
                  pl.BlockSpec((None, tb, d), lambda kh, i: (kh, i, 0)),
                  pl.BlockSpec((None, w, d), lambda kh, i: (kh, prev(i), 0)),
                  pl.BlockSpec((None, d, tb), lambda kh, i: (kh, 0, i)),
                  pl.BlockSpec((None, d, w), lambda kh, i: (kh, 0, prev(i))),
                  pl.BlockSpec(memory_space=pltpu.SMEM)],
        out_specs=[pl.BlockSpec((grp, d, tb), lambda kh, i: (kh, 0, i)), pl.BlockSpec((grp, 1, tb), lambda kh, i: (kh, 0, i))],
        out_shape=[jax.ShapeDtypeStruct((nh, d, s), BF16), jax.ShapeDtypeStruct((nh, 1, s), F32)],
        compiler_params=_cp("arbitrary", "arbitrary"))(q, k, k, vt, vt, slopes_sinks)


def _swa_bwd_t(q, k, v, ot, dot_, lse, slopes_sinks, *, scale, window, name, tb=256, rider=None):
    nh, s, d = q.shape
    nkv = k.shape[0]
    grp = nh // nkv
    w = window
    per = tb // w
    nb = s // tb

    def body(qc_ref, qn_ref, kc_ref, kp_ref, vc_ref, vp_ref, oc_ref, on_ref, doc_ref, don_ref, lc_ref, ln_ref, ss_ref,
             dq_ref, dk_ref, dvt_ref, dsink_ref):
        kvh, i = pl.program_id(0), pl.program_id(1)

        @pl.when(i == 0)
        def _():
            dsink_ref[...] = jnp.zeros_like(dsink_ref)

        (d_c, ok_c), other = _swa_geometry(tb, w, i > 0)
        d_p, ok_p = other(tb)
        d_n, ok_n = _swa_geometry(tb, w, i < nb - 1)[1](w)
        kc, kp, vc, vp = kc_ref[...], kp_ref[...], vc_ref[...], vp_ref[...]
        k_last, v_last = kc[tb - w:, :], vc[tb - w:, :]
        dk_acc = jnp.zeros((tb, d), F32)
        dv_acc = jnp.zeros((d, tb), F32)
        dk_tail = jnp.zeros((w, d), F32)
        dv_tail = jnp.zeros((d, w), F32)
        for g in range(grp):
            h = kvh * grp + g
            slope, sink = ss_ref[0, h], ss_ref[1, h]
            qg, dog, lse_c = qc_ref[g], doc_ref[g], lc_ref[g]
            delta = jnp.sum(dog.astype(F32) * oc_ref[g].astype(F32), axis=0, keepdims=True)
            p_c = jnp.exp(_swa_scores_t(kc, qg, d_c, ok_c, scale=scale, slope=slope) - lse_c)
            p_p = jnp.exp(_swa_scores_t(kp, qg, d_p, ok_p, scale=scale, slope=slope) - lse_c)
            ds_c = ((p_c * (jnp.dot(vc, dog, preferred_element_type=F32) - delta)) * scale).astype(BF16)
            ds_p = ((p_p * (jnp.dot(vp, dog, preferred_element_type=F32) - delta)) * scale).astype(BF16)
            dq_ref[g] = (lax.dot_general(ds_c, kc, TN, preferred_element_type=F32)
                         + lax.dot_general(ds_p, kp, TN, preferred_element_type=F32))
            dk_acc += jnp.dot(ds_c, qg, preferred_element_type=F32)
            dv_acc += lax.dot_general(dog, p_c.astype(BF16), NT, preferred_element_type=F32)
            dsink_ref[g] -= jnp.broadcast_to(jnp.sum(jnp.exp(sink - lse_c) * delta, axis=1, keepdims=True), (1, LANES))
            qn, don = qn_ref[g], don_ref[g]
            delta_n = jnp.sum(don.astype(F32) * on_ref[g].astype(F32), axis=0, keepdims=True)
            p_n = jnp.exp(_swa_scores_t(k_last, qn, d_n, ok_n, scale=scale, slope=slope) - ln_ref[g])
            ds_n = ((p_n * (jnp.dot(v_last, don, preferred_element_type=F32) - delta_n)) * scale).astype(BF16)
            dk_tail += jnp.dot(ds_n, qn, preferred_element_type=F32)
            dv_tail += lax.dot_general(don, p_n.astype(BF16), NT, preferred_element_type=F32)
        dk_ref[...] = dk_acc
        dvt_ref[...] = dv_acc
        dk_ref[tb - w:, :] += dk_tail
        dvt_ref[:, tb - w:] += dv_tail

    def prev(i):
        return jnp.maximum(i * per - 1, 0)

    def nxt(i):
        return jnp.minimum((i + 1) * per, s // w - 1)

    return _call_with_rider(
        body, rider, name=name, grid=(nkv, nb), scratch_shapes=[],
        args=(q, q, k, k, v, v, ot, ot, dot_, dot_, lse, lse, slopes_sinks),
        in_specs=[pl.BlockSpec((grp, tb, d), lambda kh, i: (kh, i, 0)),
                  pl.BlockSpec((grp, w, d), lambda kh, i: (kh, nxt(i), 0)),
                  pl.BlockSpec((None, tb, d), lambda kh, i: (kh, i, 0)),
                  pl.BlockSpec((None, w, d), lambda kh, i: (kh, prev(i), 0)),
                  pl.BlockSpec((None, tb, d), lambda kh, i: (kh, i, 0)),
                  pl.BlockSpec((None, w, d), lambda kh, i: (kh, prev(i), 0)),
                  pl.BlockSpec((grp, d, tb), lambda kh, i: (kh, 0, i)),
                  pl.BlockSpec((grp, d, w), lambda kh, i: (kh, 0, nxt(i))),
                  pl.BlockSpec((grp, d, tb), lambda kh, i: (kh, 0, i)),
                  pl.BlockSpec((grp, d, w), lambda kh, i: (kh, 0, nxt(i))),
                  pl.BlockSpec((grp, 1, tb), lambda kh, i: (kh, 0, i)),
                  pl.BlockSpec((grp, 1, w), lambda kh, i: (kh, 0, nxt(i))),
                  pl.BlockSpec(memory_space=pltpu.SMEM)],
        out_specs=[pl.BlockSpec((grp, tb, d), lambda kh, i: (kh, i, 0)),
                   pl.BlockSpec((None, tb, d), lambda kh, i: (kh, i, 0)),
                   pl.BlockSpec((None, d, tb), lambda kh, i: (kh, 0, i)),
                   pl.BlockSpec((None, grp, 1, LANES), lambda kh, i: (kh, 0, 0, 0))],
        out_shape=[jax.ShapeDtypeStruct((nh, s, d), F32), jax.ShapeDtypeStruct((nkv, s, d), F32),
                   jax.ShapeDtypeStruct((nkv, d, s), F32), jax.ShapeDtypeStruct((nkv, grp, 1, LANES), F32)],
        compiler_params=_cp("arbitrary", "arbitrary"))


def _adamw(w, g, m, v, *, name):
    shape = w.shape
    cols = shape[-1]
    rows = int(np.prod(shape[:-1])) if len(shape) > 1 else 1
    tr = _row_tile(rows, cols)
    c1 = 1.0 - ADAM_B1 ** ADAM_STEP
    c2 = 1.0 - ADAM_B2 ** ADAM_STEP

    def body(w_ref, g_ref, m_ref, v_ref, d_ref, mo_ref, vo_ref):
        gv = g_ref[...]
        mn = ADAM_B1 * m_ref[...] + (1.0 - ADAM_B1) * gv
        vn = ADAM_B2 * v_ref[...] + (1.0 - ADAM_B2) * (gv * gv)
        mo_ref[...] = mn
        vo_ref[...] = vn
        d_ref[...] = -ADAM_LR * ((mn / c1) / (jnp.sqrt(vn / c2) + ADAM_EPS) + ADAM_WD * w_ref[...])

    blk = pl.BlockSpec((tr, cols), lambda i: (i, 0))
    outs = pl.pallas_call(
        body, name=name, grid=(rows // tr,), in_specs=[blk] * 4, out_specs=[blk] * 3,
        out_shape=[jax.ShapeDtypeStruct((rows, cols), F32)] * 3,
        compiler_params=_cp("arbitrary"))(*[a.reshape(rows, cols) for a in (w, g, m, v)])
    return tuple(a.reshape(shape) for a in outs)


def _hbm_spec():
    return pl.BlockSpec(memory_space=pl.ANY)


def _mesh_place():
    x, y, c = lax.axis_index("x"), lax.axis_index("y"), lax.axis_index("c")
    return x, y, c, [(1 - x, y), (x, 1 - y), (1 - x, 1 - y)]


def _half_rows(c, rows, align):
    return pl.ds(pl.multiple_of(c * (rows // 2), align), rows // 2)


def _part(ref, mode, k, n, rows=None):
    if mode == "cols":
        cols = pl.ds(pl.multiple_of(k * n, LANES), n)
        return ref.at[:, cols] if rows is None else ref.at[rows, cols]
    return ref.at[k] if rows is None else ref.at[k, rows, :]


class _Rider:
    def __init__(self, inputs, out_shape, n_sems, start, finish):
        self.inputs, self.out_shape, self.n_sems, self.start, self.finish = inputs, out_shape, n_sems, start, finish


def _call_with_rider(body, rider, *, name, grid, in_specs, out_specs, out_shape, scratch_shapes, compiler_params, args):
    if rider is None:
        outs = pl.pallas_call(body, name=name, grid=grid, in_specs=in_specs, out_specs=out_specs, out_shape=out_shape,
                              scratch_shapes=scratch_shapes, compiler_params=compiler_params)(*args)
        return outs, []
    n_in, n_out, n_sc = len(in_specs), len(out_specs), len(scratch_shapes)
    n_rin, n_rout = len(rider.inputs), len(rider.out_shape)

    def wrapped(*refs):
        pos = 0
        groups = []
        for n in (n_in, n_rin, n_out, n_rout, n_sc, 2):
            groups.append(refs[pos:pos + n])
            pos += n
        ins, rins, outs, routs, scratch, sems = groups
        ids = [pl.program_id(a) for a in range(len(grid))]
        first = functools.reduce(jnp.logical_and, [i == 0 for i in ids])
        last = functools.reduce(jnp.logical_and, [i == g - 1 for i, g in zip(ids, grid)])
        pl.when(first)(lambda: rider.start(rins, routs, *sems))
        body(*ins, *outs, *scratch)
        pl.when(last)(lambda: rider.finish(rins, routs, *sems))

    outs = pl.pallas_call(
        wrapped, name=name, grid=grid, in_specs=list(in_specs) + [_hbm_spec()] * n_rin,
        out_specs=list(out_specs) + [_hbm_spec()] * n_rout, out_shape=list(out_shape) + list(rider.out_shape),
        scratch_shapes=list(scratch_shapes) + [pltpu.SemaphoreType.DMA((rider.n_sems,))] * 2,
        compiler_params=compiler_params)(*args, *rider.inputs)
    return outs[:n_out], outs[n_out:]


def _run_rider(rider, *, name):
    n_rin = len(rider.inputs)

    def body(*refs):
        rins, routs, sems = refs[:n_rin], refs[n_rin:-2], refs[-2:]
        rider.start(rins, routs, *sems)
        rider.finish(rins, routs, *sems)

    return pl.pallas_call(
        body, name=name, in_specs=[_hbm_spec()] * n_rin, out_specs=[_hbm_spec()] * len(rider.out_shape),
        out_shape=rider.out_shape, scratch_shapes=[pltpu.SemaphoreType.DMA((rider.n_sems,))] * 2)(*rider.inputs)


def _gather_rider(shards, modes):
    n_arr = len(shards)
    out_shape = [jax.ShapeDtypeStruct((s.shape[0], N_CHIPS * s.shape[1]) if m == "cols" else (N_CHIPS,) + s.shape, s.dtype)
                 for s, m in zip(shards, modes)]
    per = 4

    def copies(srcs, dsts, send_sems, recv_sems):
        x, y, c, chips = _mesh_place()
        me = 2 * x + y
        sends, waits = [], []
        for i in range(n_arr):
            r, n = shards[i].shape
            rows = _half_rows(c, r, 16)

            def copy(slot, src, dst, to, i=i):
                return pltpu.make_async_remote_copy(src_ref=src, dst_ref=dst, send_sem=send_sems.at[i * per + slot],
                                                    recv_sem=recv_sems.at[i * per + slot], device_id=to, device_id_type=MESH)

            own = _part(dsts[i], modes[i], me, n)
            sends.append(copy(0, srcs[i], own, (x, y, 1 - c)))
            waits.append(copy(0, own, own, (x, y, 1 - c)))
            for j, (px, py) in enumerate(chips):
                sends.append(copy(1 + j, srcs[i].at[rows], _part(dsts[i], modes[i], me, n, rows), (px, py, c)))
                theirs = _part(dsts[i], modes[i], 2 * px + py, n, rows)
                waits.append(copy(1 + j, theirs, theirs, (px, py, c)))
        return sends, waits

    def start(*refs):
        for cp in copies(*refs)[0]:
            cp.start()

    def finish(*refs):
        sends, waits = copies(*refs)
        for cp in waits:
            cp.wait_recv()
        for cp in sends:
            cp.wait_send()

    return _Rider(list(shards), out_shape, per * n_arr, start, finish)


def _gather_forward(dsts, shard_shapes, modes, *, name):
    n_arr = len(dsts)

    def body(*refs):
        outs = refs[n_arr:2 * n_arr]
        send_sems, recv_sems = refs[2 * n_arr:]
        x, y, c, chips = _mesh_place()
        cps = []
        for i in range(n_arr):
            r, n = shard_shapes[i]
            for j, (px, py) in enumerate(chips):
                def view(hc, i=i, px=px, py=py, r=r, n=n):
                    return _part(outs[i], modes[i], 2 * px + py, n, _half_rows(hc, r, 16))

                def copy(ref, i=i, j=j):
                    return pltpu.make_async_remote_copy(src_ref=ref, dst_ref=ref, send_sem=send_sems.at[3 * i + j],
                                                        recv_sem=recv_sems.at[3 * i + j], device_id=(x, y, 1 - c), device_id_type=MESH)

                cps.append((copy(view(c)), copy(view(1 - c))))
        for send, _ in cps:
            send.start()
        for send, theirs in cps:
            theirs.wait_recv()
            send.wait_send()

    return pl.pallas_call(
        body, name=name, in_specs=[_hbm_spec()] * n_arr, out_specs=[_hbm_spec()] * n_arr,
        out_shape=[jax.ShapeDtypeStruct(d.shape, d.dtype) for d in dsts],
        input_output_aliases={i: i for i in range(n_arr)},
        scratch_shapes=[pltpu.SemaphoreType.DMA((3 * n_arr,)), pltpu.SemaphoreType.DMA((3 * n_arr,))])(*dsts)


def _blk_view(a, mode):
    return a[None] if mode == "cols" else a


def _swap_rider(arrs, modes):
    n_arr = len(arrs)
    out_shape = [jax.ShapeDtypeStruct((a.shape[0] // 2, a.shape[1]) if m == "cols" else (a.shape[0], a.shape[1] // 2, a.shape[2]), a.dtype)
                 for a, m in zip(arrs, modes)]

    def copies(srcs, dsts, send_sems, recv_sems):
        x, y, c, _ = _mesh_place()
        cps = []
        for i in range(n_arr):
            if modes[i] == "cols":
                src = srcs[i].at[_half_rows(1 - c, arrs[i].shape[0], 8)]
            else:
                src = srcs[i].at[:, _half_rows(1 - c, arrs[i].shape[1], 8), :]
            cps.append(pltpu.make_async_remote_copy(src_ref=src, dst_ref=dsts[i], send_sem=send_sems.at[i],
                                                    recv_sem=recv_sems.at[i], device_id=(x, y, 1 - c), device_id_type=MESH))
        return cps

    def start(*refs):
        for cp in copies(*refs):
            cp.start()

    def finish(*refs):
        for cp in copies(*refs):
            cp.wait()

    return _Rider(list(arrs), out_shape, n_arr, start, finish)


def _rs_pair_add(arr, landed, place, *, name):
    nb, r, c = arr.shape
    rh = r // 2
    tr = _row_tile(rh, c)
    nt = rh // tr

    def body(p_ref, a_ref, l_ref, o_ref):
        o_ref[...] = (a_ref[...] + l_ref[...]).astype(BF16)

    grid_spec = pltpu.PrefetchScalarGridSpec(
        num_scalar_prefetch=1, grid=(nb, nt),
        in_specs=[pl.BlockSpec((None, tr, c), lambda b, t, p_ref: (b, p_ref[1] * nt + t, 0)),
                  pl.BlockSpec((None, tr, c), lambda b, t, p_ref: (b, t, 0))],
        out_specs=pl.BlockSpec((None, tr, c), lambda b, t, p_ref: (b, t, 0)))
    return pl.pallas_call(
        body, name=name, grid_spec=grid_spec, out_shape=jax.ShapeDtypeStruct((nb, rh, c), BF16),
        compiler_params=_cp("arbitrary", "arbitrary"))(place, arr, landed)


def _exchange_rider(parts, modes):
    n_arr = len(parts)
    out_shape = []
    for a, m in zip(parts, modes):
        shp = (a.shape[0], a.shape[1] // N_CHIPS) if m == "cols" else a.shape[1:]
        out_shape.append(jax.ShapeDtypeStruct((3,) + shp, a.dtype))

    def copies(srcs, dsts, send_sems, recv_sems):
        x, y, c, chips = _mesh_place()
        cps = []
        for i in range(n_arr):
            n = out_shape[i].shape[-1]
            for j, (px, py) in enumerate(chips):
                cps.append(pltpu.make_async_remote_copy(
                    src_ref=_part(srcs[i], modes[i], 2 * px + py, n), dst_ref=dsts[i].at[j],
                    send_sem=send_sems.at[3 * i + j], recv_sem=recv_sems.at[3 * i + j],
                    device_id=(px, py, c), device_id_type=MESH))
        return cps

    def start(*refs):
        for cp in copies(*refs):
            cp.start()

    def finish(*refs):
        for cp in copies(*refs):
            cp.wait()

    return _Rider(list(parts), out_shape, 3 * n_arr, start, finish)


def _rs_chip_sum(part, landed, mode, place, *, name):
    _, rh, n = landed.shape
    tr = _row_tile(rh, n)
    nt = rh // tr

    def body(p_ref, a_ref, l_ref, o_ref):
        o_ref[...] = ((a_ref[...].astype(F32) + l_ref[0].astype(F32)) + l_ref[1].astype(F32)) + l_ref[2].astype(F32)

    if mode == "cols":
        own = pl.BlockSpec((tr, n), lambda t, p_ref: (t, p_ref[0]))
    else:
        own = pl.BlockSpec((None, tr, n), lambda t, p_ref: (p_ref[0], t, 0))
    grid_spec = pltpu.PrefetchScalarGridSpec(
        num_scalar_prefetch=1, grid=(nt,),
        in_specs=[own, pl.BlockSpec((3, tr, n), lambda t, p_ref: (0, t, 0))],
        out_specs=pl.BlockSpec((tr, n), lambda t, p_ref: (p_ref[1] * nt + t, 0)))
    return pl.pallas_call(
        body, name=name, grid_spec=grid_spec, out_shape=jax.ShapeDtypeStruct((2 * rh, n), F32),
        compiler_params=_cp("arbitrary"))(place, part, landed)


def _rs_pair_join(halves, *, name):
    n_arr = len(halves)

    def body(*refs):
        outs = refs[n_arr:2 * n_arr]
        send_sems, recv_sems = refs[2 * n_arr:]
        x, y, c, _ = _mesh_place()
        cps = []
        for i in range(n_arr):
            rows = _half_rows(c, halves[i].shape[0], 8)
            cps.append(pltpu.make_async_remote_copy(src_ref=outs[i].at[rows], dst_ref=outs[i].at[rows], send_sem=send_sems.at[i],
                                                    recv_sem=recv_sems.at[i], device_id=(x, y, 1 - c), device_id_type=MESH))
        for cp in cps:
            cp.start()
        for i, cp in enumerate(cps):
            cp.wait_send()
            theirs = outs[i].at[_half_rows(1 - c, halves[i].shape[0], 8)]
            pltpu.make_async_remote_copy(src_ref=theirs, dst_ref=theirs, send_sem=send_sems.at[i], recv_sem=recv_sems.at[i],
                                         device_id=(x, y, 1 - c), device_id_type=MESH).wait_recv()

    return pl.pallas_call(
        body, name=name, in_specs=[_hbm_spec()] * n_arr, out_specs=[_hbm_spec()] * n_arr,
        out_shape=[jax.ShapeDtypeStruct(h.shape, h.dtype) for h in halves],
        input_output_aliases={i: i for i in range(n_arr)},
        scratch_shapes=[pltpu.SemaphoreType.DMA((n_arr,)), pltpu.SemaphoreType.DMA((n_arr,))])(*halves)


def _allreduce_small(v, *, name):
    r, c = v.shape

    def body(v_ref, o_ref, gath, send_sems, recv_sems):
        x, y, cc, _ = _mesh_place()
        me = 4 * x + 2 * y + cc
        gath[me] = v_ref[...]
        cps = []
        for rel in range(1, 8):
            px = 1 - x if rel & 4 else x
            py = 1 - y if rel & 2 else y
            pc = 1 - cc if rel & 1 else cc

            def copy(slot, px=px, py=py, pc=pc, rel=rel):
                return pltpu.make_async_remote_copy(
                    src_ref=v_ref, dst_ref=gath.at[slot], send_sem=send_sems.at[rel - 1],
                    recv_sem=recv_sems.at[rel - 1], device_id=(px, py, pc), device_id_type=MESH)

            cps.append((copy(me), copy(4 * px + 2 * py + pc)))
        for send, _ in cps:
            send.start()
        for send, theirs in cps:
            theirs.wait_recv()
            send.wait_send()
        tot = gath[0]
        for d in range(1, 8):
            tot = tot + gath[d]
        o_ref[...] = tot

    vm = pl.BlockSpec(memory_space=pltpu.VMEM)
    return pl.pallas_call(
        body, name=name, in_specs=[vm], out_specs=vm, out_shape=jax.ShapeDtypeStruct((r, c), F32),
        scratch_shapes=[pltpu.VMEM((8, r, c), F32), pltpu.SemaphoreType.DMA((7,)), pltpu.SemaphoreType.DMA((7,))])(v)


def _rope_tables(s, reps):
    half = B_ROPE // 2
    inv = ROPE_THETA ** (-jnp.arange(0, B_ROPE, 2, dtype=F32) / B_ROPE)
    ang = jnp.arange(s, dtype=F32)[:, None] * inv[None, :]
    return jnp.tile(jnp.cos(ang), (1, reps)), jnp.tile(jnp.sin(ang), (1, reps))


def _alibi_slopes():
    return 2.0 ** (-8.0 * jnp.arange(1, A_HEADS + 1, dtype=F32) / A_HEADS)


def _ffn_fwd(h, norm, wts, tag, rider=None, on_rode=None):
    (dact_dgate, dact_dup, act, xn), rode = _ffn_up(h, norm, wts["wgu"], name=f"{tag}_up", rider=rider)
    if on_rode is not None:
        on_rode(rode)
    out = _mm_res_fwd(act, wts["wd"], h, scale=FFN_RES_SCALE, name=f"{tag}_down")
    return out, dict(h_in=h, dact_dgate=dact_dgate, dact_dup=dact_dup, act=act, xn=xn), rode


def _ffn_bwd(dh, norm, wts, sv, tag, rider=None, own=None):
    (dgate, dup), rode = _ffn_down_bwd(dh, wts["wd"], sv["dact_dgate"], sv["dact_dup"], scale=FFN_RES_SCALE,
                                      name=f"{tag}_down_bwd", rider=rider)
    d_wd = _mm_tn(sv["act"], dh, b_scale=FFN_RES_SCALE, name=f"{tag}_dwd")
    pairs = [(dgate, wts["wgu"], 0), (dup, wts["wgu"], 1)]
    if own is None:
        d_wgu = _mm_tn(sv["xn"], [dgate, dup], name=f"{tag}_dwgu")
        dh_in, dnorm = _mm_nt_rmsbwd(pairs, sv["h_in"], norm, dh, name=f"{tag}_dx")
    else:
        wd_ready, wgu_ready, done = own
        first = wd_ready(d_wd)
        res = _mm_tn(sv["xn"], [dgate, dup], name=f"{tag}_dwgu", rider=first)
        d_wgu, brought = (res, []) if first is None else res
        second = wgu_ready(brought, d_wgu)
        res = _mm_nt_rmsbwd(pairs, sv["h_in"], norm, dh, name=f"{tag}_dx", rider=second)
        dh_in, dnorm, brought = (*res, []) if second is None else res
        done(brought)
    return dh_in, dnorm, d_wgu, d_wd, rode


def _even_weights(w_in, w_uq, w_ukv):
    half = B_ROPE // 2
    base = w_in.shape[1]
    kr1, kr2 = w_in[:, base - B_ROPE:base - half], w_in[:, base - half:]
    w_in_cat = jnp.concatenate([w_in, -kr2, kr1, jnp.zeros((w_in.shape[0], 64), w_in.dtype)], axis=1)
    u3 = w_uq.reshape(w_uq.shape[0], B_HEADS, B_NOPE + B_ROPE)
    nope = u3[:, :, :B_NOPE].reshape(w_uq.shape[0], -1)
    rot = u3[:, :, B_NOPE:].reshape(w_uq.shape[0], -1)
    swapped = jnp.concatenate([-u3[:, :, B_NOPE + half:], u3[:, :, B_NOPE:B_NOPE + half]], axis=-1).reshape(w_uq.shape[0], -1)
    return w_in_cat, jnp.concatenate([nope, rot, swapped], axis=1), w_ukv


def _even_fwd(h, w, i, rider=None):
    s = h.shape[0]
    qa, ka, va, vat, c_q, c_kv, kr_blk, xn = _ev_in_fwd(h, w["mix_norm"][i:i + 1], w["ev_in_cat"], name="ev_in")
    cos32, sin32 = _rope_tables(s, 2)
    kro = _rope_fwd(kr_blk[:, :B_ROPE], kr_blk[:, B_ROPE:2 * B_ROPE], cos32, sin32, name="ev_k_rope")
    ss = jnp.stack([_alibi_slopes(), w["ev_sinks"].reshape(-1)])
    oa, lse_a = _swa_fwd_t(qa, ka, vat, ss, scale=A_HEAD_DIM ** -0.5, window=WINDOW, name="swa_fwd")
    cos256, sin256 = _rope_tables(s, 2 * B_HEADS)
    qb, xn_q = _ev_q_fwd(c_q, w["ev_cq_norm"], w["ev_q_cat"], cos256, sin256, name="ev_q_up")
    kb, vb, vbt, xn_kv = _ev_kv_fwd(c_kv, w["ev_ckv_norm"], w["ev_ukv"], kro, name="ev_kv_up")
    (ob, lse_b), rode = _causal_fwd_t(qb, kb, vbt, scale=(B_NOPE + B_ROPE) ** -0.5, name="mla_fwd", tq=512, hb=8, rider=rider)
    attn = jnp.concatenate([oa.reshape(-1, s), ob.reshape(-1, s)], axis=0)
    out = _mm_res_fwd(attn, w["ev_out"], h, scale=1.0, name="ev_out", a_t=True)
    sv = dict(h_in=h, xn=xn, c_q=c_q, c_kv=c_kv, xn_q=xn_q, xn_kv=xn_kv, qa=qa, ka=ka, va=va, oa=oa, lse_a=lse_a,
              ss=ss, qb=qb, kb=kb, vb=vb, ob=ob, lse_b=lse_b, attn=attn, cos32=cos32, sin32=sin32,
              cos256=cos256, sin256=sin256)
    return out, sv, rode


def _even_bwd(dh, w, sv, i, rider=None):
    s = dh.shape[0]
    half = B_ROPE // 2
    g = {}
    dattn = _mm_nt_t(dh, w["ev_out"], name="ev_out_dx")
    g["ev_w_out"] = _mm_tn(sv["attn"], dh, name="ev_out_dw", a_t=True)
    doa = dattn[:A_HEADS * A_HEAD_DIM].reshape(A_HEADS, A_HEAD_DIM, s)
    dob = dattn[A_HEADS * A_HEAD_DIM:].reshape(B_HEADS, B_V, s)
    first, then = rider if isinstance(rider, tuple) else (None, None)
    (dqa, dka, dva, dsink), brought = _swa_bwd_t(sv["qa"], sv["ka"], sv["va"], sv["oa"], doa, sv["lse_a"], sv["ss"],
                                                 scale=A_HEAD_DIM ** -0.5, window=WINDOW, name="swa_bwd", rider=first)
    if then is not None:
        rider = then(brought)
    g["ev_sinks"] = dsink[:, :, 0, 0].reshape(1, A_HEADS)
    (dqb, dkb, dvb), rode = _causal_bwd_t(sv["qb"], sv["kb"], sv["vb"], sv["ob"], dob, sv["lse_b"],
                                          scale=(B_NOPE + B_ROPE) ** -0.5, name="mla_bwd", tq=512, hb=4, rider=rider)
    dyq = _ev_q_merge(dqb, sv["cos256"], sv["sin256"], name="ev_q_merge")
    dwq = _mm_tn(sv["xn_q"], dyq, name="ev_q_up_dw")
    dcq, g["ev_cq_norm"] = _mm_nt_rmsbwd([(dyq, w["ev_q_cat"])], sv["c_q"], w["ev_cq_norm"], None, name="ev_q_up_dx")
    kq = sv["c_q"].shape[1]
    d_nope = dwq[:, :512].reshape(kq, B_HEADS, B_NOPE)
    d_rot = dwq[:, 512:768].reshape(kq, B_HEADS, B_ROPE)
    d_swp = dwq[:, 768:].reshape(kq, B_HEADS, B_ROPE)
    g["ev_w_uq"] = jnp.concatenate([d_nope, d_rot[:, :, :half] + d_swp[:, :, half:], d_rot[:, :, half:] - d_swp[:, :, :half]],
                                   axis=-1).reshape(kq, -1)
    dykv, dkr = _ev_kv_merge(dkb, dvb, sv["cos32"], sv["sin32"], name="ev_kv_merge")
    g["ev_w_ukv"] = _mm_tn(sv["xn_kv"], dykv, name="ev_kv_up_dw")
    dckv, g["ev_ckv_norm"] = _mm_nt_rmsbwd([(dykv, w["ev_ukv"])], sv["c_kv"], w["ev_ckv_norm"], None, name="ev_kv_up_dx")
    dycat = _ev_in_merge(dqa, dka, dva, dcq, dckv, dkr, name="ev_in_merge")
    dwin = _mm_tn(sv["xn"], dycat, name="ev_in_dw")
    base = 1184
    g["ev_w_in"] = jnp.concatenate([dwin[:, :base - B_ROPE],
                                    dwin[:, base - B_ROPE:base - half] + dwin[:, base + half:base + B_ROPE],
                                    dwin[:, base - half:base] - dwin[:, base:base + half]], axis=-1)
    dh_in, dnorm = _mm_nt_rmsbwd([(dycat, w["ev_in_cat"])], sv["h_in"], w["mix_norm"][i:i + 1], dh, name="ev_in_dx")
    return dh_in, dnorm, g, rode


def _odd_fwd(h, w, i, rider=None):
    s = h.shape[0]
    wd = C_HEADS * C_HEAD_DIM
    q, k, v, vt, y_f, xn = _fox_in_fwd(h, w["mix_norm"][i:i + 1], w["od_in_pad"], nheads=C_HEADS, dh=C_HEAD_DIM,
                                       q_ones=(0, 2, 3, 4), k_ones=(1,), name="od_in")
    scale = C_HEAD_DIM ** -0.5
    ft = y_f[:, :C_HEADS].T
    bf = w["od_b_f"].reshape(C_HEADS, 1)
    cb3 = _fox_gate_fwd(ft, bf, out_scale=-1.0 / scale, name="fox_gate_fwd")
    k = k + jnp.pad(cb3.transpose(1, 2, 0), ((0, 0), (0, 0), (C_HEAD_DIM + 2, LANES - C_HEAD_DIM - 5)))
    (o, lse), rode = _causal_fwd_t(q, k, vt, scale=scale, name="fox_fwd", tq=512, hb=16, rider=rider)
    attn = o.reshape(-1, s)
    out = _mm_res_fwd(attn, w["od_out"], h, scale=1.0, name="od_out", a_t=True)
    return out, dict(h_in=h, xn=xn, q=q, k=k, v=v, o=o, lse=lse, ft=ft, bf=bf, attn=attn), rode


def _odd_bwd(dh, w, sv, i, rider=None):
    s = dh.shape[0]
    g = {}
    dattn = _mm_nt_t(dh, w["od_out"], name="od_out_dx")
    g["od_w_out"] = _mm_tn(sv["attn"], dh, name="od_out_dw", a_t=True)
    do = dattn.reshape(C_HEADS, C_HEAD_DIM, s)
    scale = C_HEAD_DIM ** -0.5
    (dq, dk, dv), rode = _causal_bwd_t(sv["q"], sv["k"], sv["v"], sv["o"], do, sv["lse"], scale=scale, name="fox_bwd",
                                       tq=512, hb=4, rider=rider)
    dqkv, sums = _merge_heads(dq, dk, dv, dh=C_HEAD_DIM, q_col=C_HEAD_DIM + 1, k_col=C_HEAD_DIM, name="fox_merge")
    dft, dbf = _fox_gate_bwd(sums[:, :C_HEADS].T, sums[:, C_HEADS:2 * C_HEADS].T, sv["ft"], sv["bf"],
                             inv_scale=1.0 / scale, name="fox_gate_bwd")
    g["od_b_f"] = dbf.reshape(1, C_HEADS)
    wd = C_HEADS * C_HEAD_DIM
    df = jnp.pad(dft.T, ((0, 0), (0, LANES - C_HEADS)))
    g["od_w_in"] = jnp.concatenate([_mm_tn(sv["xn"], dqkv, name="od_in_dw"),
                                    _mm_tn(sv["xn"], df, name="od_in_dwf")[:, :C_HEADS]], axis=-1)
    dh_in, dnorm = _mm_nt_rmsbwd([(dqkv, w["od_in_pad"], 0), (df, w["od_in_pad"], 3 * wd // LANES)],
                                 sv["h_in"], w["mix_norm"][i:i + 1], dh, name="od_in_dx")
    return dh_in, dnorm, g, rode


def _kernel_weights(full, replicated):
    w = dict(replicated)
    _install_weights(w, {(n, i): a for n, per_layer in full.items() for i, a in enumerate(per_layer)})
    return w


def _install_weights(w, got):
    raw = w.setdefault("raw", {})
    raw.update(got)
    for (n, i), a in got.items():
        if n in ("ffa_w_gate_up", "ffa_w_down", "ffb_w_gate_up", "ffb_w_down"):
            w.setdefault(n[:3], {}).setdefault(i, {})["wgu" if n.endswith("gate_up") else "wd"] = a
        elif n in ("ple_w_gate", "ple_w_proj"):
            w.setdefault("ple_gate" if n.endswith("gate") else "ple_proj", {})[i] = a
    if "ev_in_cat" not in w and all((n, 0) in raw for n in ("ev_w_in", "ev_w_uq", "ev_w_ukv", "ev_w_out")):
        w["ev_in_cat"], w["ev_q_cat"], w["ev_ukv"] = _even_weights(raw["ev_w_in", 0], raw["ev_w_uq", 0], raw["ev_w_ukv", 0])
        w["ev_out"] = raw["ev_w_out", 0]
    if "od_in_pad" not in w and all((n, 0) in raw for n in ("od_w_in", "od_w_out")):
        od_in = raw["od_w_in", 0]
        w["od_in_pad"] = jnp.pad(od_in, ((0, 0), (0, (-od_in.shape[1]) % LANES)))
        w["od_out"] = raw["od_w_out", 0]


def _keys(names, layer):
    return tuple((n, layer) for n in names)


_FFA, _FFB, _PLE = ("ffa_w_gate_up", "ffa_w_down"), ("ffb_w_gate_up", "ffb_w_down"), ("ple_w_gate", "ple_w_proj")
_EV, _OD = ("ev_w_in", "ev_w_uq", "ev_w_ukv", "ev_w_out"), ("od_w_in", "od_w_out")
_GATHER_FIRST = _keys(_FFA[:1], 0)
_GATHER_RIDES = {("ffa", 0): _keys(_FFA[1:] + _EV, 0), ("mix", 0): _keys(_FFB, 0) + _keys(_FFA[:1], 1),
                 ("ffb", 0): _keys(_PLE, 0) + _keys(_FFA[1:], 1), ("ffa", 1): _keys(_OD, 0),
                 ("mix", 1): _keys(_FFB + _PLE, 1)}
_REDUCE_RIDES = {("mix", 1): _keys(_FFB + _PLE, 1), ("mix", 0): _keys(_FFA, 1) + _keys(_OD, 0) + _keys(_FFB + _PLE, 0),
                 ("ffa", 0): _keys(_EV, 0)}
_REDUCE_OWN = ("ffa", 0)
_SWAP_AHEAD = {("ffb", 1): ("mix", 1)}


def _local_step(x, p, tgt, w, ex=None):
    depth = p.shape[0]

    def gather_behind(host, fn, *args):
        keys = None if ex is None else _GATHER_RIDES.get(host)
        if keys is None:
            return fn(*args, None)[:-1]
        done = []

        def install(rode):
            if not done:
                _install_weights(w, ex.gather_finish(keys, rode, name=f"weight_forward_{host[0]}{host[1]}"))
                done.append(True)

        res = fn(*args, ex.gather_rider(keys), install) if fn is _ffn_fwd else fn(*args, ex.gather_rider(keys))
        install(res[-1])
        return res[:-1]

    h = x
    saved = []
    for i in range(depth):
        sv = {}
        h, sv["ffa"] = gather_behind(("ffa", i), _ffn_fwd, h, w["ffa_norm"][i:i + 1], w["ffa"][i], f"ffa{i}")
        h, sv["mix"] = gather_behind(("mix", i), _even_fwd if i % 2 == 0 else _odd_fwd, h, w, i)
        h, sv["ffb"] = gather_behind(("ffb", i), _ffn_fwd, h, w["ffb_norm"][i:i + 1], w["ffb"][i], f"ffb{i}")
        h_in = h
        h, xn, gate, pp = _ple_fwd(h, w["ple_norm"][i:i + 1], w["ple_gate"][i], p[i], w["ple_proj"][i], name=f"ple{i}")
        sv["ple"] = dict(h_in=h_in, xn=xn, gate=gate, pp=pp)
        saved.append(sv)
    loss_vec, dh, d_final = _final_loss(h, w["final_norm"].reshape(1, -1), tgt, name="final_loss")

    per_layer = [dict() for _ in range(depth)]
    mats = {}
    grads = {}

    pending = {}

    def reduce_behind(host, fn, *args):
        keys = None if ex is None else _REDUCE_RIDES.get(host)
        ahead = None if ex is None else _SWAP_AHEAD.get(host)
        if keys is None and ahead is None:
            return fn(*args, None)[:-1]
        if ahead is not None:
            got, ctxs = {}, []

            def note_wd(d_wd):
                got[f"{host[0]}_w_down", host[1]] = d_wd

            def swap_now(brought, d_wgu):
                got[f"{host[0]}_w_gate_up", host[1]] = d_wgu
                swap, ctx = ex.swap_rider(_REDUCE_RIDES[ahead], {**mats, **got})
                ctxs.append(ctx)
                return swap

            def stash(brought):
                pending[ahead] = ex.after_swap(ctxs[0], brought)

            return fn(*args, None, (note_wd, swap_now, stash))[:-1]
        states = []
        if fn is _even_bwd:
            swap, ctx = ex.swap_rider(keys, mats)

            def then(brought):
                states.append(ex.after_swap(ctx, brought))
                return states[0][0]

            res = fn(*args, (swap, then))
        else:
            states.append(pending.pop(host, None) or ex.reduce_begin(keys, mats, tag=f"{host[0]}{host[1]}"))
            if fn is _ffn_bwd and host == _REDUCE_OWN:
                own = []

                def wd_ready(d_wd):
                    own.append(ex.reduce_begin(_keys(_FFA[1:], 0), {("ffa_w_down", 0): d_wd}, tag="own_wd"))
                    return own[0][0]

                def wgu_ready(brought, d_wgu):
                    ex.reduce_finish(own[0], brought)
                    own.append(ex.reduce_begin(_keys(_FFA[:1], 0), {("ffa_w_gate_up", 0): d_wgu}, tag="own_wgu"))
                    return own[1][0]

                res = fn(*args, states[0][0], (wd_ready, wgu_ready, lambda brought: ex.reduce_finish(own[1], brought)))
            else:
                res = fn(*args, states[0][0])
        ex.reduce_finish(states[0], res[-1])
        return res[:-1]

    for i in reversed(range(depth)):
        sv, gl = saved[i], per_layer[i]
        dz, dpp = _ple_bwd_elem(dh, sv["ple"]["gate"], sv["ple"]["pp"], name=f"ple{i}_bwd")
        mats["ple_w_gate", i] = _mm_tn(sv["ple"]["xn"], dz, name=f"ple{i}_dwg")
        mats["ple_w_proj", i] = _mm_tn(p[i], dpp, name=f"ple{i}_dwp")
        dh, gl["ple_norm"] = _mm_nt_rmsbwd([(dz, w["ple_gate"][i])], sv["ple"]["h_in"], w["ple_norm"][i:i + 1], dh,
                                           name=f"ple{i}_dx")
        dh, gl["ffb_norm"], mats["ffb_w_gate_up", i], mats["ffb_w_down", i] = reduce_behind(
            ("ffb", i), _ffn_bwd, dh, w["ffb_norm"][i:i + 1], w["ffb"][i], sv["ffb"], f"ffb{i}")
        dh, gl["mix_norm"], gm = reduce_behind(("mix", i), _even_bwd if i % 2 == 0 else _odd_bwd, dh, w, sv["mix"], i)
        for n, g in gm.items():
            if n in REPLICATED:
                grads[n] = g
            else:
                mats[n, 0] = g
        dh, gl["ffa_norm"], mats["ffa_w_gate_up", i], mats["ffa_w_down", i] = reduce_behind(
            ("ffa", i), _ffn_bwd, dh, w["ffa_norm"][i:i + 1], w["ffa"][i], sv["ffa"], f"ffa{i}")
    grads["final_norm"] = d_final.reshape(-1)
    for n in ("ffa_norm", "mix_norm", "ffb_norm", "ple_norm"):
        grads[n] = jnp.concatenate([per_layer[i][n] for i in range(depth)], axis=0)
    if ex is None:
        for n, _ in SHARDED:
            grads[n] = [mats[n, i] for i in range(depth) if (n, i) in mats]
    return loss_vec[0, 0], dh, grads


def _cut_mode(local_shape, axis, ncols):
    return "cols" if axis == 2 and ncols % LANES == 0 else "blk"


class _Exchange:
    def __init__(self, wts):
        self.place = jnp.stack([2 * lax.axis_index("x") + lax.axis_index("y"), lax.axis_index("c")]).astype(jnp.int32)
        self.info = {}
        for n, axis in SHARDED:
            wb = wts[n].astype(BF16)
            mode = _cut_mode(wb.shape, axis, wb.shape[2])
            for i in range(wb.shape[0]):
                self.info[n, i] = dict(shard=wb[i], mode=mode, axis=axis)
        self.halves = {}

    def _modes(self, keys):
        return [self.info[k]["mode"] for k in keys]

    def gather_rider(self, keys):
        return _gather_rider([self.info[k]["shard"] for k in keys], self._modes(keys))

    def gather_finish(self, keys, landed, *, name):
        outs = _gather_forward(landed, [self.info[k]["shard"].shape for k in keys], self._modes(keys), name=name)
        got = {}
        for k, dst in zip(keys, outs):
            if self.info[k]["mode"] == "blk":
                dst = dst.reshape(-1, dst.shape[2]) if self.info[k]["axis"] == 1 else jnp.moveaxis(dst, 0, 1).reshape(dst.shape[1], -1)
            got[k] = dst
        return got

    def gather(self, keys, *, name):
        return self.gather_finish(keys, _run_rider(self.gather_rider(keys), name=name), name=name + "_forward")

    def swap_rider(self, keys, mats):
        modes = self._modes(keys)
        arrs = []
        for k in keys:
            g2, (rr, cc) = mats[k], self.info[k]["shard"].shape
            if self.info[k]["mode"] == "blk":
                g2 = g2.reshape(N_CHIPS, rr, cc) if self.info[k]["axis"] == 1 else g2.reshape(rr, N_CHIPS, cc).transpose(1, 0, 2)
            arrs.append(g2)
        return _swap_rider(arrs, modes), (keys, modes, arrs)

    def after_swap(self, ctx, landed):
        keys, modes, arrs = ctx
        parts = []
        for (n, i), m, a, l in zip(keys, modes, arrs, landed):
            pt = _rs_pair_add(_blk_view(a, m), _blk_view(l, m), self.place, name=f"rs_pair_add_{n}{i}")
            parts.append(pt[0] if m == "cols" else pt)
        return _exchange_rider(parts, modes), keys, parts

    def reduce_begin(self, keys, mats, *, tag):
        rider, ctx = self.swap_rider(keys, mats)
        return self.after_swap(ctx, _run_rider(rider, name=f"rs_pair_swap_{tag}"))

    def reduce_finish(self, state, landed):
        _, keys, parts = state
        for (n, i), m, pt, l in zip(keys, self._modes(keys), parts, landed):
            self.halves[n, i] = _rs_chip_sum(pt, l, m, self.place, name=f"rs_chip_sum_{n}{i}")

    def reduce(self, keys, mats, *, tag):
        state = self.reduce_begin(keys, mats, tag=tag)
        self.reduce_finish(state, _run_rider(state[0], name=f"rs_chip_exchange_{tag}"))

    def join(self, wts):
        keys = list(self.info)
        joined = dict(zip(keys, _rs_pair_join([self.halves[k] for k in keys], name="rs_pair_join")))
        return {n: jnp.stack([joined[n, i] for i in range(wts[n].shape[0])]).reshape(wts[n].shape) for n, _ in SHARDED}


def _small_rows(vals):
    rows = []
    for n in REPLICATED:
        v = vals[n].reshape(-1)
        rows.append(jnp.pad(v, (0, (-v.shape[0]) % FLAT_COLS)).reshape(-1, FLAT_COLS))
    out = jnp.concatenate(rows, axis=0)
    return jnp.pad(out, ((0, (-out.shape[0]) % 8), (0, 0)))


def kernel(x, p, ffa_norm, ffa_w_gate_up, ffa_w_down, mix_norm, ffb_norm, ffb_w_gate_up, ffb_w_down, ple_norm, ple_w_gate, ple_w_proj, ev_w_in, ev_sinks, ev_cq_norm, ev_w_uq, ev_ckv_norm, ev_w_ukv, ev_w_out, od_w_in, od_b_f, od_w_out, final_norm, loss_target, m_ffa_norm, m_ffa_w_gate_up, m_ffa_w_down, m_mix_norm, m_ffb_norm, m_ffb_w_gate_up, m_ffb_w_down, m_ple_norm, m_ple_w_gate, m_ple_w_proj, m_ev_w_in, m_ev_sinks, m_ev_cq_norm, m_ev_w_uq, m_ev_ckv_norm, m_ev_w_ukv, m_ev_w_out, m_od_w_in, m_od_b_f, m_od_w_out, m_final_norm, v_ffa_norm, v_ffa_w_gate_up, v_ffa_w_down, v_mix_norm, v_ffb_norm, v_ffb_w_gate_up, v_ffb_w_down, v_ple_norm, v_ple_w_gate, v_ple_w_proj, v_ev_w_in, v_ev_sinks, v_ev_cq_norm, v_ev_w_uq, v_ev_ckv_norm, v_ev_w_ukv, v_ev_w_out, v_od_w_in, v_od_b_f, v_od_w_out, v_final_norm):
    env = dict(locals())
    wts = {n: env[n] for n in WEIGHT_ORDER}
    mom1 = {n: env["m_" + n] for n in WEIGHT_ORDER}
    mom2 = {n: env["v_" + n] for n in WEIGHT_ORDER}
    ex = _Exchange(wts)

    w = {n: wts[n] for n in REPLICATED}
    _install_weights(w, ex.gather(_GATHER_FIRST, name="weight_gather_first"))

    loss_part, grad_x, grads = _local_step(x[0], p[:, 0], loss_target[0], w, ex)
    loss = lax.psum(loss_part, ("x", "y", "c"))
    gout = ex.join(wts)
    small = _allreduce_small(_small_rows(grads), name="small_allreduce")
    r0 = 0
    for n in REPLICATED:
        size = int(np.prod(wts[n].shape))
        nr = -(-size // FLAT_COLS)
        gout[n] = small[r0:r0 + nr].reshape(-1)[:size].reshape(wts[n].shape)
        r0 += nr

    delta, new_m, new_v = {}, {}, {}
    for n in WEIGHT_ORDER:
        delta[n], new_m[n], new_v[n] = _adamw(wts[n], gout[n], mom1[n], mom2[n], name="adamw_" + n)
    return (loss, grad_x[None], *[gout[n] for n in WEIGHT_ORDER], *[delta[n] for n in WEIGHT_ORDER],
            *[new_m[n] for n in WEIGHT_ORDER], *[new_v[n] for n in WEIGHT_ORDER])
```
